```python
import math
import jax
import jax.numpy as jnp
from jax import lax
import numpy as np

D_MODEL = 1024
BATCH = 8
SEQ = 2048
DEPTH = 2

SB_HEADS = 8
SB_HEAD_DIM = 64
SB_WIDTH = SB_HEADS * SB_HEAD_DIM
SB_BLOCK = 128
HG_HEADS = 4
HG_HEAD_DIM = 128
HG_WIDTH = HG_HEADS * HG_HEAD_DIM
HG_CHUNK = 64
CONV_WIDTH = 512
CONV_K = 3
N_BRANCH = 3
IN_COLS = 4 * SB_WIDTH + 4 * HG_WIDTH + 4 * CONV_WIDTH + N_BRANCH * D_MODEL
LN_EPS = 1e-5
RMS_EPS = 1e-6

kernel_name = "hybrid_sb_hgrn2_shortconv_deepnorm"


def _standardize(x):
    xf = x.astype(jnp.float32)
    mu = jnp.mean(xf, axis=-1, keepdims=True)
    xc = xf - mu
    var = jnp.mean(xc * xc, axis=-1, keepdims=True)
    return xc * lax.rsqrt(var + LN_EPS)


def _stick_breaking(q, k, v):
    B, S, H, dh = q.shape
    scale = dh ** -0.5
    qf = q.astype(jnp.float32)
    kf = k.astype(jnp.float32)
    vf = v.astype(jnp.float32)
    outs = []
    for blk in range(S // SB_BLOCK):
        t0 = blk * SB_BLOCK
        t1 = t0 + SB_BLOCK
        z = jnp.einsum('bthd,bshd->bhts', qf[:, t0:t1], kf[:, :t1]) * scale
        mask = jnp.arange(t1)[None, :] < (t0 + jnp.arange(SB_BLOCK))[:, None]
        log_1m_beta = jnp.where(mask, -jax.nn.softplus(z), 0.0)
        log_surv = lax.cumsum(log_1m_beta, axis=3, reverse=True) - log_1m_beta
        a = jnp.where(mask, jnp.exp(jax.nn.log_sigmoid(z) + log_surv), 0.0)
        outs.append(jnp.einsum('bhts,bshd->bthd', a, vf[:, :t1]))
    return jnp.concatenate(outs, axis=1)


def _hgrn2(q, f_pre, i_in, lb):
    B, S, H, dk = q.shape
    dv = i_in.shape[-1]
    n_chunks = S // HG_CHUNK
    f = lb + (1.0 - lb) * jax.nn.sigmoid(f_pre.astype(jnp.float32))
    k = 1.0 - f
    g = jnp.log(f)

    def to_chunks(a):
        return a.reshape(B, n_chunks, HG_CHUNK, H, a.shape[-1]).transpose(1, 0, 3, 2, 4)

    qc = to_chunks(q.astype(jnp.float32))
    kc = to_chunks(k)
    vc = to_chunks(i_in.astype(jnp.float32))
    bc = jnp.cumsum(to_chunks(g), axis=3)
    causal = jnp.tril(jnp.ones((HG_CHUNK, HG_CHUNK), dtype=bool))

    def step(state, inp):
        q_c, k_c, v_c, b_c = inp
        inter = jnp.einsum('bhtd,bhde->bhte', q_c * jnp.exp(b_c), state)
        diff = b_c[:, :, :, None, :] - b_c[:, :, None, :, :]
        decay = jnp.exp(jnp.where(causal[:, :, None], diff, -jnp.inf))
        scores = jnp.einsum('bhtsd,bhsd->bhts', q_c[:, :, :, None, :] * decay, k_c)
        intra = jnp.einsum('bhts,bhse->bhte', scores, v_c)
        b_end = b_c[:, :, -1, :]
        k_dec = k_c * jnp.exp(b_end[:, :, None, :] - b_c)
        new_state = jnp.exp(b_end)[..., None] * state + jnp.einsum('bhsd,bhse->bhde', k_dec, v_c)
        return new_state, inter + intra

    state0 = jnp.zeros((B, H, dk, dv), jnp.float32)
    _, o = lax.scan(step, state0, (qc, kc, vc, bc))
    return o.transpose(1, 0, 3, 2, 4).reshape(B, S, H, dv)


def _short_conv(u, w):
    ch = u.shape[-1]
    return lax.conv_general_dilated(
        u, w[:, None, :].astype(u.dtype), window_strides=(1,), padding=[(CONV_K - 1, 0)],
        dimension_numbers=('NWC', 'WIO', 'NWC'), feature_group_count=ch)


def _fwd_setup_inputs(seed: int = 0) -> dict:
    key = jax.random.key(seed)
    ks = jax.random.split(key, 12)
    beta = (8.0 * DEPTH) ** -0.25

    def nrm(k, shape, scale):
        return jax.random.normal(k, shape, jnp.float32) * scale

    return {
        "x": nrm(ks[0], (BATCH, SEQ, D_MODEL), 1.0),
        "c": nrm(ks[1], (BATCH, D_MODEL), 1.0),
        "w_mod": nrm(ks[2], (DEPTH, D_MODEL, 3 * D_MODEL), 0.2 * D_MODEL ** -0.5),
        "b_mod": nrm(ks[3], (DEPTH, 3 * D_MODEL), 0.01),
        "w_in": nrm(ks[4], (DEPTH, D_MODEL, IN_COLS), D_MODEL ** -0.5),
        "conv_w": nrm(ks[5], (DEPTH, CONV_K, CONV_WIDTH), CONV_K ** -0.5),
        "hgrn_norm_w": 1.0 + nrm(ks[6], (DEPTH, HG_HEAD_DIM), 0.02),
        "lower_bounds": nrm(ks[7], (DEPTH, HG_WIDTH), 0.1),
        "w_branch": nrm(ks[8], (DEPTH, N_BRANCH, SB_WIDTH, D_MODEL), beta * SB_WIDTH ** -0.5),
        "w_out": nrm(ks[9], (DEPTH, D_MODEL, D_MODEL), beta * D_MODEL ** -0.5),
        "ln_g": 1.0 + nrm(ks[10], (DEPTH, D_MODEL), 0.02),
        "ln_b": nrm(ks[11], (DEPTH, D_MODEL), 0.02),
    }


def _fwd_reference(x, c, w_mod, b_mod, w_in, conv_w, hgrn_norm_w, lower_bounds, w_branch, w_out, ln_g, ln_b):
    B, S, D = x.shape
    dt = x.dtype
    alpha = (2.0 * DEPTH) ** 0.25
    p = jax.nn.softmax(lower_bounds.astype(jnp.float32), axis=0)
    lbs = jnp.cumsum(p, axis=0) - p[0:1]
    sizes = [SB_WIDTH] * 4 + [HG_WIDTH] * 4 + [CONV_WIDTH] * 4 + [D_MODEL] * N_BRANCH
    splits = np.cumsum(sizes[:-1]).tolist()

    for l in range(DEPTH):
        mod = (c @ w_mod[l] + b_mod[l])[:, None, :]
        shift, scale, gate = jnp.split(mod.astype(jnp.float32), 3, axis=-1)
        h = (_standardize(x) * (1.0 + scale) + shift).astype(dt)

        proj = h @ w_in[l]
        (q_a, k_a, v_a, z_a, q_b, f_b, i_b, z_b,
         pre_c, post_c, u_c, z_c, g_a, g_b, g_c) = jnp.split(proj, splits, axis=-1)

        o_a = _stick_breaking(q_a.reshape(B, S, SB_HEADS, SB_HEAD_DIM),
                              k_a.reshape(B, S, SB_HEADS, SB_HEAD_DIM),
                              v_a.reshape(B, S, SB_HEADS, SB_HEAD_DIM)).reshape(B, S, SB_WIDTH)
        y_a = (o_a * jax.nn.silu(z_a.astype(jnp.float32))).astype(dt)

        o_b = _hgrn2(jax.nn.silu(q_b).reshape(B, S, HG_HEADS, HG_HEAD_DIM),
                     f_b.reshape(B, S, HG_HEADS, HG_HEAD_DIM),
                     i_b.reshape(B, S, HG_HEADS, HG_HEAD_DIM),
                     lbs[l].reshape(HG_HEADS, HG_HEAD_DIM))
        o_b = o_b * lax.rsqrt(jnp.mean(o_b * o_b, axis=-1, keepdims=True) + RMS_EPS)
        o_b = (o_b * hgrn_norm_w[l].astype(jnp.float32)).reshape(B, S, HG_WIDTH)
        y_b = (o_b * jax.nn.silu(z_b.astype(jnp.float32))).astype(dt)

        y_c = post_c * _short_conv(pre_c * u_c, conv_w[l]) * jax.nn.silu(z_c)

        merged = (jax.nn.sigmoid(g_a) * (y_a @ w_branch[l, 0])
                  + jax.nn.sigmoid(g_b) * (y_b @ w_branch[l, 1])
                  + jax.nn.sigmoid(g_c) * (y_c.astype(dt) @ w_branch[l, 2]))
        y = (merged @ w_out[l]).astype(jnp.float32)

        r = alpha * x.astype(jnp.float32) + (1.0 + gate) * y
        x = (_standardize(r) * ln_g[l] + ln_b[l]).astype(dt)
    return x


import jax as _jax
import jax.numpy as _jnp

TWIN_FORMAT = 'train_step'
FWD_PARAMS = ['x', 'c', 'w_mod', 'b_mod', 'w_in', 'conv_w', 'hgrn_norm_w', 'lower_bounds', 'w_branch', 'w_out', 'ln_g', 'ln_b']
TWIN_WEIGHTS = ['w_mod', 'b_mod', 'w_in', 'conv_w', 'hgrn_norm_w', 'lower_bounds', 'w_branch', 'w_out', 'ln_g', 'ln_b']
TWIN_DIFF_INPUT = 'x'
TWIN_INPUTS = ['x', 'c', 'w_mod', 'b_mod', 'w_in', 'conv_w', 'hgrn_norm_w', 'lower_bounds', 'w_branch', 'w_out', 'ln_g', 'ln_b', 'loss_target', 'm_w_mod', 'm_b_mod', 'm_w_in', 'm_conv_w', 'm_hgrn_norm_w', 'm_lower_bounds', 'm_w_branch', 'm_w_out', 'm_ln_g', 'm_ln_b', 'v_w_mod', 'v_b_mod', 'v_w_in', 'v_conv_w', 'v_hgrn_norm_w', 'v_lower_bounds', 'v_w_branch', 'v_w_out', 'v_ln_g', 'v_ln_b']
TWIN_OUTPUTS = ['loss', 'grad_x', 'grad_w_mod', 'grad_b_mod', 'grad_w_in', 'grad_conv_w', 'grad_hgrn_norm_w', 'grad_lower_bounds', 'grad_w_branch', 'grad_w_out', 'grad_ln_g', 'grad_ln_b', 'delta_w_mod', 'delta_b_mod', 'delta_w_in', 'delta_conv_w', 'delta_hgrn_norm_w', 'delta_lower_bounds', 'delta_w_branch', 'delta_w_out', 'delta_ln_g', 'delta_ln_b', 'new_m_w_mod', 'new_m_b_mod', 'new_m_w_in', 'new_m_conv_w', 'new_m_hgrn_norm_w', 'new_m_lower_bounds', 'new_m_w_branch', 'new_m_w_out', 'new_m_ln_g', 'new_m_ln_b', 'new_v_w_mod', 'new_v_b_mod', 'new_v_w_in', 'new_v_conv_w', 'new_v_hgrn_norm_w', 'new_v_lower_bounds', 'new_v_w_branch', 'new_v_w_out', 'new_v_ln_g', 'new_v_ln_b']
TWIN_LEAF_KINDS = {'loss': 'loss', 'grad_x': 'grad_x', 'grad_w_mod': 'grad_w', 'grad_b_mod': 'grad_w', 'grad_w_in': 'grad_w', 'grad_conv_w': 'grad_w', 'grad_hgrn_norm_w': 'grad_w', 'grad_lower_bounds': 'grad_w', 'grad_w_branch': 'grad_w', 'grad_w_out': 'grad_w', 'grad_ln_g': 'grad_w', 'grad_ln_b': 'grad_w', 'delta_w_mod': 'delta_w', 'delta_b_mod': 'delta_w', 'delta_w_in': 'delta_w', 'delta_conv_w': 'delta_w', 'delta_hgrn_norm_w': 'delta_w', 'delta_lower_bounds': 'delta_w', 'delta_w_branch': 'delta_w', 'delta_w_out': 'delta_w', 'delta_ln_g': 'delta_w', 'delta_ln_b': 'delta_w', 'new_m_w_mod': 'new_m', 'new_m_b_mod': 'new_m', 'new_m_w_in': 'new_m', 'new_m_conv_w': 'new_m', 'new_m_hgrn_norm_w': 'new_m', 'new_m_lower_bounds': 'new_m', 'new_m_w_branch': 'new_m', 'new_m_w_out': 'new_m', 'new_m_ln_g': 'new_m', 'new_m_ln_b': 'new_m', 'new_v_w_mod': 'new_v', 'new_v_b_mod': 'new_v', 'new_v_w_in': 'new_v', 'new_v_conv_w': 'new_v', 'new_v_hgrn_norm_w': 'new_v', 'new_v_lower_bounds': 'new_v', 'new_v_w_branch': 'new_v', 'new_v_w_out': 'new_v', 'new_v_ln_g': 'new_v', 'new_v_ln_b': 'new_v'}


def _forward(args):
    return _fwd_reference(*[args[k] for k in FWD_PARAMS])


def _output_shape():
    out = _jax.eval_shape(lambda: _forward(_fwd_setup_inputs(0)))
    return out.shape, out.dtype

N_MICROBATCH = 1
ADAM_LR = 0.001
ADAM_B1 = 0.9
ADAM_B2 = 0.999
ADAM_EPS = 1e-08
ADAM_WD = 0.01
ADAM_STEP = 10
PER_EXAMPLE_BATCH_AXIS = {'x': 0, 'c': 0, 'loss_target': 0}
SHARED_INPUTS = []
_WEIGHT_DTYPES = {'w_mod': _jnp.float32, 'b_mod': _jnp.float32, 'w_in': _jnp.float32, 'conv_w': _jnp.float32, 'hgrn_norm_w': _jnp.float32, 'lower_bounds': _jnp.float32, 'w_branch': _jnp.float32, 'w_out': _jnp.float32, 'ln_g': _jnp.float32, 'ln_b': _jnp.float32}
MOMENT_SCALE = {'w_mod': 2.171551e-02, 'b_mod': 2.258916e-02, 'w_in': 7.620365e-03, 'conv_w': 1.241522e-02, 'hgrn_norm_w': 2.293451e-02, 'lower_bounds': 1.015487e-03, 'w_branch': 1.471166e-02, 'w_out': 2.545598e-02, 'ln_g': 1.132834e+01, 'ln_b': 3.431853e-01}


def _to_microbatches(a, axis):
    t = _jnp.moveaxis(a, axis, 0)
    t = t.reshape((N_MICROBATCH, t.shape[0] // N_MICROBATCH) + t.shape[1:])
    return _jnp.moveaxis(t, 1, axis + 1)


def setup_inputs(seed: int = 0) -> dict:
    inp = _fwd_setup_inputs(seed)
    key = _jax.random.fold_in(_jax.random.key(seed), 7919)
    shape, _ = _output_shape()
    out = dict(inp)
    out["loss_target"] = _jax.random.normal(_jax.random.fold_in(key, 0), shape, _jnp.float32)
    for i, name in enumerate(TWIN_WEIGHTS):
        w = inp[name].astype(_jnp.float32)
        if MOMENT_SCALE is None:
            s = _jnp.sqrt(_jnp.mean(_jnp.square(w)) + 1e-30)
        else:
            s = MOMENT_SCALE[name]
        km, kv = _jax.random.split(_jax.random.fold_in(key, i + 1))
        out[name] = w
        out["m_" + name] = s * _jax.random.normal(km, w.shape, _jnp.float32)
        out["v_" + name] = (s * s) * _jax.random.uniform(kv, w.shape, _jnp.float32, 0.5, 1.5)
    if N_MICROBATCH > 1:
        for name, axis in PER_EXAMPLE_BATCH_AXIS.items():
            out[name] = _to_microbatches(out[name], axis)
    return {'x': out['x'], 'c': out['c'], 'w_mod': out['w_mod'], 'b_mod': out['b_mod'], 'w_in': out['w_in'], 'conv_w': out['conv_w'], 'hgrn_norm_w': out['hgrn_norm_w'], 'lower_bounds': out['lower_bounds'], 'w_branch': out['w_branch'], 'w_out': out['w_out'], 'ln_g': out['ln_g'], 'ln_b': out['ln_b'], 'loss_target': out['loss_target'], 'm_w_mod': out['m_w_mod'], 'm_b_mod': out['m_b_mod'], 'm_w_in': out['m_w_in'], 'm_conv_w': out['m_conv_w'], 'm_hgrn_norm_w': out['m_hgrn_norm_w'], 'm_lower_bounds': out['m_lower_bounds'], 'm_w_branch': out['m_w_branch'], 'm_w_out': out['m_w_out'], 'm_ln_g': out['m_ln_g'], 'm_ln_b': out['m_ln_b'], 'v_w_mod': out['v_w_mod'], 'v_b_mod': out['v_b_mod'], 'v_w_in': out['v_w_in'], 'v_conv_w': out['v_conv_w'], 'v_hgrn_norm_w': out['v_hgrn_norm_w'], 'v_lower_bounds': out['v_lower_bounds'], 'v_w_branch': out['v_w_branch'], 'v_w_out': out['v_w_out'], 'v_ln_g': out['v_ln_g'], 'v_ln_b': out['v_ln_b']}


def _loss(weights, diff, rest, loss_target):
    with _jax.named_scope("forward"):
        args = {**rest, TWIN_DIFF_INPUT: diff, **{k: w.astype(_WEIGHT_DTYPES[k]) for k, w in weights.items()}}
        y = _forward(args)
    with _jax.named_scope("loss_head"):
        err = _jnp.square(y.astype(_jnp.float32) - loss_target)
        return 0.5 * _jnp.sum(_jnp.mean(err, axis=-1)) if err.ndim else 0.5 * err


def _adamw(w, g, m, v):
    m = ADAM_B1 * m + (1.0 - ADAM_B1) * g
    v = ADAM_B2 * v + (1.0 - ADAM_B2) * _jnp.square(g)
    m_hat = m / (1.0 - ADAM_B1 ** ADAM_STEP)
    v_hat = v / (1.0 - ADAM_B2 ** ADAM_STEP)
    delta = -ADAM_LR * (m_hat / (_jnp.sqrt(v_hat) + ADAM_EPS) + ADAM_WD * w)
    return delta, m, v


def reference(x, c, w_mod, b_mod, w_in, conv_w, hgrn_norm_w, lower_bounds, w_branch, w_out, ln_g, ln_b, loss_target, m_w_mod, m_b_mod, m_w_in, m_conv_w, m_hgrn_norm_w, m_lower_bounds, m_w_branch, m_w_out, m_ln_g, m_ln_b, v_w_mod, v_b_mod, v_w_in, v_conv_w, v_hgrn_norm_w, v_lower_bounds, v_w_branch, v_w_out, v_ln_g, v_ln_b):
    given = dict(x=x, c=c, w_mod=w_mod, b_mod=b_mod, w_in=w_in, conv_w=conv_w, hgrn_norm_w=hgrn_norm_w, lower_bounds=lower_bounds, w_branch=w_branch, w_out=w_out, ln_g=ln_g, ln_b=ln_b, loss_target=loss_target, m_w_mod=m_w_mod, m_b_mod=m_b_mod, m_w_in=m_w_in, m_conv_w=m_conv_w, m_hgrn_norm_w=m_hgrn_norm_w, m_lower_bounds=m_lower_bounds, m_w_branch=m_w_branch, m_w_out=m_w_out, m_ln_g=m_ln_g, m_ln_b=m_ln_b, v_w_mod=v_w_mod, v_b_mod=v_b_mod, v_w_in=v_w_in, v_conv_w=v_conv_w, v_hgrn_norm_w=v_hgrn_norm_w, v_lower_bounds=v_lower_bounds, v_w_branch=v_w_branch, v_w_out=v_w_out, v_ln_g=v_ln_g, v_ln_b=v_ln_b)
    weights = {n: given[n] for n in TWIN_WEIGHTS}
    shared = {n: given[n] for n in SHARED_INPUTS}
    per_example = {n: given[n] for n in ['x', 'c']}
    grad_fn = _jax.value_and_grad(_loss, argnums=(0, 1))

    def one_microbatch(ex, loss_target):
        ex = dict(ex)
        diff = ex.pop(TWIN_DIFF_INPUT)
        return grad_fn(weights, diff, {**shared, **ex}, loss_target)

    if N_MICROBATCH == 1:
        loss, (grad_w, grad_x) = one_microbatch(per_example, given["loss_target"])
    else:
        def body(carry, xs):
            loss_sum, grad_sum = carry
            l_k, (gw_k, gx_k) = one_microbatch(xs[0], xs[1])
            with _jax.named_scope("update"):
                return (loss_sum + l_k, _jax.tree.map(_jnp.add, grad_sum, gw_k)), gx_k

        init = (_jnp.zeros((), _jnp.float32), _jax.tree.map(_jnp.zeros_like, weights))
        (loss, grad_w), grad_x = _jax.lax.scan(body, init, (per_example, given["loss_target"]))
    with _jax.named_scope("update"):
        delta_w, new_m, new_v = {}, {}, {}
        for n in TWIN_WEIGHTS:
            delta_w[n], new_m[n], new_v[n] = _adamw(weights[n], grad_w[n], given["m_" + n], given["v_" + n])
    return (loss, grad_x, *[grad_w[n] for n in TWIN_WEIGHTS], *[delta_w[n] for n in TWIN_WEIGHTS],
            *[new_m[n] for n in TWIN_WEIGHTS], *[new_v[n] for n in TWIN_WEIGHTS])
```

```python
import functools
import math

import numpy as np
import jax
import jax.numpy as jnp
from jax import lax
from jax.experimental import pallas as pl
from jax.experimental.pallas import tpu as pltpu

F32 = jnp.float32
BF = jnp.bfloat16
MESH = pl.DeviceIdType.MESH

DEPTH = 2
D_MODEL = 1024
WIDTH = 512
IN_COLS = 12 * WIDTH + 3 * D_MODEL
N_CHIPS = 4
N_DEV = 8
SB_BLOCK = 128
SB_HEAD_DIM = 64
HG_CHUNK = 64
HG_DIM = 128
LN_EPS = 1e-5
RMS_EPS = 1e-6
ALPHA = (2.0 * DEPTH) ** 0.25
ADAM_LR, ADAM_B1, ADAM_B2, ADAM_EPS, ADAM_WD, ADAM_STEP = 0.001, 0.9, 0.999, 1e-08, 0.01, 10
VMEM_LIMIT = 56 << 20


def _params(**kw):
    return pltpu.CompilerParams(vmem_limit_bytes=VMEM_LIMIT, **kw)


def _dot(a, b):
    return jnp.dot(a, b, preferred_element_type=F32)


def _dot_nt(a, b):
    return lax.dot_general(a, b, (((1,), (1,)), ((), ())), preferred_element_type=F32)


def _dot_tn(a, b):
    return lax.dot_general(a, b, (((0,), (0,)), ((), ())), preferred_element_type=F32)


def _sigmoid(x):
    return 1.0 / (1.0 + jnp.exp(-x))


def _silu(x):
    return x * _sigmoid(x)


def _softplus(z):
    return jnp.maximum(z, 0.0) + jnp.log(1.0 + jnp.exp(-jnp.abs(z)))


def _split_dot(t, g, terms):
    acc = None
    rest = g
    for _ in range(terms):
        part = rest.astype(BF)
        rest = rest - part.astype(F32)
        d = _dot(t, part)
        acc = d if acc is None else acc + d
    return acc


def _split_dot_r(g, t, terms):
    acc = None
    rest = g
    for _ in range(terms):
        part = rest.astype(BF)
        rest = rest - part.astype(F32)
        d = _dot(part, t)
        acc = d if acc is None else acc + d
    return acc


def _standardize(x):
    mu = jnp.mean(x, axis=-1, keepdims=True)
    xc = x - mu
    var = jnp.mean(xc * xc, axis=-1, keepdims=True)
    rstd = lax.rsqrt(var + LN_EPS)
    return xc * rstd, rstd


def _standardize_bwd(dxs, xs, rstd):
    return rstd * (dxs - jnp.mean(dxs, axis=-1, keepdims=True) - xs * jnp.mean(dxs * xs, axis=-1, keepdims=True))


def _proj_fwd(x, mod, wg, layer, tm=512, tn=768):
    S, D = x.shape
    tm = min(tm, S)
    shard = wg.shape[-1]
    per = shard // tn

    def body(x_ref, mod_ref, w_ref, proj_ref, h_ref, hs):
        @pl.when(pl.program_id(1) == 0)
        def _():
            xs, _ = _standardize(x_ref[...])
            h = xs * (1.0 + mod_ref[:, D:2 * D]) + mod_ref[:, 0:D]
            hb = h.astype(BF)
            hs[...] = hb
            h_ref[...] = hb

        proj_ref[...] = _dot(hs[...], w_ref[...])

    return pl.pallas_call(
        body, name="proj_fwd",
        grid=(S // tm, IN_COLS // tn),
        in_specs=[pl.BlockSpec((tm, D), lambda i, j: (i, 0)),
                  pl.BlockSpec((1, 3 * D), lambda i, j: (0, 0)),
                  pl.BlockSpec((None, None, D, tn), lambda i, j: (j // per, layer, 0, j % per))],
        out_specs=[pl.BlockSpec((tm, tn), lambda i, j: (i, j)),
                   pl.BlockSpec((tm, D), lambda i, j: (i, 0))],
        out_shape=[jax.ShapeDtypeStruct((S, IN_COLS), F32), jax.ShapeDtypeStruct((S, D), BF)],
        scratch_shapes=[pltpu.VMEM((tm, D), BF)],
        compiler_params=_params(dimension_semantics=("parallel", "arbitrary")),
    )(x, mod, wg)


def _attn_consts():
    j = np.arange(SB_BLOCK)[:, None]
    s = np.arange(SB_BLOCK)[None, :]
    ones = np.ones((SB_BLOCK, SB_BLOCK), np.float32)
    after = np.concatenate([(j > s).astype(np.float32), ones], axis=1)
    before = np.concatenate([(j < s).astype(np.float32), ones], axis=1)
    return jnp.asarray(after, BF), jnp.asarray(before, BF)


def _attn_block(qm, k_ref, kj, tio):
    c0 = pl.multiple_of(kj * SB_BLOCK, SB_BLOCK)
    kb = k_ref[pl.ds(c0, SB_BLOCK), :].astype(BF)
    z = _dot_nt(qm, kb)
    sio = lax.broadcasted_iota(jnp.int32, (SB_BLOCK, SB_BLOCK), 1) + kj * SB_BLOCK
    valid = sio < tio
    sp = _softplus(z)
    lsb = jnp.where(valid, -sp, 0.0)
    return c0, z, valid, sp, lsb


def _attn_fwd(proj, after):
    S = proj.shape[0]
    nq = S // SB_BLOCK
    scale = SB_HEAD_DIM ** -0.5

    def body(q_ref, k_ref, v_ref, after_ref, o_ref):
        lane = lax.broadcasted_iota(jnp.int32, (1, 2 * SB_HEAD_DIM), 1)
        after_m = after_ref[...]

        def qloop(qi, _):
            r0 = pl.multiple_of(qi * SB_BLOCK, SB_BLOCK)
            qv = q_ref[pl.ds(r0, SB_BLOCK), :] * scale
            tio = lax.broadcasted_iota(jnp.int32, (SB_BLOCK, SB_BLOCK), 0) + qi * SB_BLOCK
            out = jnp.zeros((SB_BLOCK, 2 * SB_HEAD_DIM), F32)
            for head in range(2):
                mh = (lane >= head * SB_HEAD_DIM) & (lane < (head + 1) * SB_HEAD_DIM)
                qm = jnp.where(mh, qv, 0.0).astype(BF)

                def kloop(n, carry):
                    o, run = carry
                    c0, z, valid, sp, lsb = _attn_block(qm, k_ref, qi - n, tio)
                    cs2 = _split_dot_r(lsb, after_m, 2)
                    a = jnp.where(valid, jnp.exp(z - sp + cs2[:, :SB_BLOCK] + run), 0.0)
                    vb = jnp.where(mh, v_ref[pl.ds(c0, SB_BLOCK), :], 0.0).astype(BF)
                    return o + _dot(a.astype(BF), vb), run + cs2[:, SB_BLOCK:]

                zero = jnp.zeros((SB_BLOCK, SB_BLOCK), F32)
                o, _ = lax.fori_loop(0, qi + 1, kloop, (zero, zero))
                out = out + o
            o_ref[pl.ds(r0, SB_BLOCK), :] = out
            return 0

        lax.fori_loop(0, nq, qloop, 0)

    col = lambda base: pl.BlockSpec((S, 128), lambda p, base=base: (0, base + p))
    return pl.pallas_call(
        body, name="attn_fwd",
        grid=(WIDTH // 128,),
        in_specs=[col(0), col(4), col(8), pl.BlockSpec(after.shape, lambda p: (0, 0))],
        out_specs=pl.BlockSpec((S, 128), lambda p: (0, p)),
        out_shape=jax.ShapeDtypeStruct((S, WIDTH), F32),
        compiler_params=_params(dimension_semantics=("parallel",)),
    )(proj, proj, proj, after)


def _attn_bwd(proj, d_o, after, before):
    S = proj.shape[0]
    nq = S // SB_BLOCK
    scale = SB_HEAD_DIM ** -0.5

    def body(q_ref, k_ref, v_ref, do_ref, after_ref, before_ref, dq_ref, dk_ref, dv_ref, runs, dk_acc, dv_acc):
        lane = lax.broadcasted_iota(jnp.int32, (1, 2 * SB_HEAD_DIM), 1)
        after_m = after_ref[...]
        before_m = before_ref[...]
        dk_acc[...] = jnp.zeros_like(dk_acc)
        dv_acc[...] = jnp.zeros_like(dv_acc)
        zero = jnp.zeros((SB_BLOCK, SB_BLOCK), F32)

        def qloop(qi, _):
            r0 = pl.multiple_of(qi * SB_BLOCK, SB_BLOCK)
            qv = q_ref[pl.ds(r0, SB_BLOCK), :] * scale
            dov = do_ref[pl.ds(r0, SB_BLOCK), :]
            tio = lax.broadcasted_iota(jnp.int32, (SB_BLOCK, SB_BLOCK), 0) + qi * SB_BLOCK
            dq = jnp.zeros((SB_BLOCK, 2 * SB_HEAD_DIM), F32)
            for head in range(2):
                mh = (lane >= head * SB_HEAD_DIM) & (lane < (head + 1) * SB_HEAD_DIM)
                qm = jnp.where(mh, qv, 0.0).astype(BF)
                dob = jnp.where(mh, dov, 0.0).astype(BF)

                def sums(n, run):
                    kj = qi - n
                    _, _, _, _, lsb = _attn_block(qm, k_ref, kj, tio)
                    runs[kj] = run
                    return run + _split_dot_r(lsb, after_m[:, SB_BLOCK:], 2)

                lax.fori_loop(0, qi + 1, sums, zero)

                def kloop(kj, carry):
                    dqh, cum = carry
                    c0, z, valid, sp, lsb = _attn_block(qm, k_ref, kj, tio)
                    cs = _split_dot_r(lsb, after_m[:, :SB_BLOCK], 2)
                    a = jnp.where(valid, jnp.exp(z - sp + cs + runs[kj]), 0.0)
                    vb = v_ref[pl.ds(c0, SB_BLOCK), :].astype(BF)
                    w = _dot_nt(dob, vb) * a
                    c2 = _split_dot_r(w, before_m, 2)
                    dz = jnp.where(valid, w * jnp.exp(-sp) - (c2[:, :SB_BLOCK] + cum) * jnp.exp(z - sp), 0.0)
                    dzb = dz.astype(BF)
                    kb = jnp.where(mh, k_ref[pl.ds(c0, SB_BLOCK), :], 0.0).astype(BF)
                    dk_acc[pl.ds(c0, SB_BLOCK), :] += _dot_tn(dzb, qm)
                    dv_acc[pl.ds(c0, SB_BLOCK), :] += _dot_tn(a.astype(BF), dob)
                    return dqh + _dot(dzb, kb), cum + c2[:, SB_BLOCK:]

                dqh, _ = lax.fori_loop(0, qi + 1, kloop, (zero, zero))
                dq = dq + dqh
            dq_ref[pl.ds(r0, SB_BLOCK), :] = (dq * scale).astype(BF)
            return 0

        lax.fori_loop(0, nq, qloop, 0)
        dk_ref[...] = dk_acc[...].astype(BF)
        dv_ref[...] = dv_acc[...].astype(BF)

    col = lambda base: pl.BlockSpec((S, 128), lambda p, base=base: (0, base + p))
    whole = lambda a: pl.BlockSpec(a.shape, lambda p: (0, 0))
    out = jax.ShapeDtypeStruct((S, WIDTH), BF)
    return pl.pallas_call(
        body, name="attn_bwd",
        grid=(WIDTH // 128,),
        in_specs=[col(0), col(4), col(8), col(0), whole(after), whole(before)],
        out_specs=[col(0), col(0), col(0)],
        out_shape=[out, out, out],
        scratch_shapes=[pltpu.VMEM((nq, SB_BLOCK, SB_BLOCK), F32), pltpu.VMEM((S, 128), F32), pltpu.VMEM((S, 128), F32)],
        compiler_params=_params(dimension_semantics=("parallel",)),
    )(proj, proj, proj, d_o, after, before)


HG_LEVELS = (32, 16, 8, 4, 2, 1)


def _hgrn_consts():
    C = HG_CHUNK
    t = np.arange(C)[:, None]
    s = np.arange(C)[None, :]
    rows = [(s <= t), (s > t)]
    masks = [(t == s)]
    for m in HG_LEVELS:
        two = 2 * m
        mid = (t // two) * two + m
        right = (t % two) >= m
        rows.append(right & (s >= mid) & (s <= t))
        rows.append((~right) & (s > t) & (s <= mid - 1))
        masks.append(((t // two) == (s // two)) & right & ((s % two) < m))
    rows.append(np.ones((HG_DIM, C), bool))
    tri = np.concatenate(rows, axis=0).astype(np.float32)
    return (jnp.asarray(tri, BF), jnp.asarray(tri.T.copy(), BF), jnp.asarray(np.stack(masks).astype(np.float32), F32))


HG_SUM_ROWS = (HG_CHUNK,) * (2 + 2 * len(HG_LEVELS)) + (HG_DIM,)


@jax.custom_vjp
def _hgrn_sums(tri, tri_t, g):
    e = _split_dot(tri, g, 3)
    out, r = [], 0
    for n in HG_SUM_ROWS:
        out.append(e[r:r + n])
        r += n
    return tuple(out)


def _hgrn_sums_fwd(tri, tri_t, g):
    return _hgrn_sums(tri, tri_t, g), (tri, tri_t)


def _hgrn_sums_bwd(res, ds):
    tri, tri_t = res
    return jnp.zeros_like(tri), jnp.zeros_like(tri_t), _split_dot(tri_t, jnp.concatenate(ds, axis=0), 3)


_hgrn_sums.defvjp(_hgrn_sums_fwd, _hgrn_sums_bwd)


def _bf_dot(a, b):
    return _dot(a.astype(BF), b.astype(BF))


def _bf_dot_nt(a, b):
    return _dot_nt(a.astype(BF), b.astype(BF))


def _bf_dot_tn(a, b):
    return _dot_tn(a.astype(BF), b.astype(BF))


@jax.custom_vjp
def _mm(a, b):
    return _bf_dot(a, b)


_mm.defvjp(lambda a, b: (_bf_dot(a, b), (a, b)), lambda r, ct: (_bf_dot_nt(ct, r[1]), _bf_dot_tn(r[0], ct)))


@jax.custom_vjp
def _mm_nt(a, b):
    return _bf_dot_nt(a, b)


_mm_nt.defvjp(lambda a, b: (_bf_dot_nt(a, b), (a, b)), lambda r, ct: (_bf_dot(ct, r[1]), _bf_dot_tn(ct, r[0])))


@jax.custom_vjp
def _mm_tn(a, b):
    return _bf_dot_tn(a, b)


_mm_tn.defvjp(lambda a, b: (_bf_dot_tn(a, b), (a, b)), lambda r, ct: (_bf_dot_nt(r[1], ct), _bf_dot(r[0], ct)))


def _hgrn_chunk(tri, tri_t, masks, qraw, fpre, v, st, lb):
    q = _silu(qraw)
    f = lb + (1.0 - lb) * _sigmoid(fpre)
    k = 1.0 - f
    e = _hgrn_sums(tri, tri_t, jnp.log(f))
    prefix, suffix, whole = e[0], e[1], e[-1]
    scores = masks[0] * _mm_nt(q, k)
    for n in range(len(HG_LEVELS)):
        scores = scores + masks[n + 1] * _mm_nt(q * jnp.exp(e[2 + 2 * n]), k * jnp.exp(e[3 + 2 * n]))
    o = _mm_nt(q * jnp.exp(prefix), st) + _mm(scores, v)
    st_new = st * jnp.exp(whole) + _mm_tn(v, k * jnp.exp(suffix))
    return o, st_new


def _hgrn_specs(S, consts):
    col = lambda base: pl.BlockSpec((S, 128), lambda p, base=base: (0, base + p))
    whole = [pl.BlockSpec(a.shape, lambda p, n=a.ndim: (0,) * n) for a in consts]
    return col, whole


def _hgrn_fwd(proj, lbs, consts):
    S = proj.shape[0]
    nc = S // HG_CHUNK

    def body(q_ref, f_ref, i_ref, lb_ref, tri_ref, trit_ref, mask_ref, o_ref, st_ref):
        tri, tri_t = tri_ref[...], trit_ref[...]
        masks = [mask_ref[n] for n in range(len(HG_LEVELS) + 1)]
        lb = lb_ref[...]

        def chunk(ci, st):
            r0 = pl.multiple_of(ci * HG_CHUNK, HG_CHUNK)
            rows = pl.ds(r0, HG_CHUNK)
            st_ref[ci] = st
            o, st_new = _hgrn_chunk(tri, tri_t, masks, q_ref[rows, :], f_ref[rows, :], i_ref[rows, :], st, lb)
            o_ref[rows, :] = o
            return st_new

        lax.fori_loop(0, nc, chunk, jnp.zeros((HG_DIM, HG_DIM), F32))

    col, whole = _hgrn_specs(S, consts)
    return pl.pallas_call(
        body, name="hgrn_fwd",
        grid=(WIDTH // 128,),
        in_specs=[col(16), col(20), col(24), pl.BlockSpec((None, 1, 128), lambda p: (p, 0, 0))] + whole,
        out_specs=[col(0), pl.BlockSpec((None, nc, HG_DIM, HG_DIM), lambda p: (p, 0, 0, 0))],
        out_shape=[jax.ShapeDtypeStruct((S, WIDTH), F32), jax.ShapeDtypeStruct((WIDTH // 128, nc, HG_DIM, HG_DIM), F32)],
        compiler_params=_params(dimension_semantics=("parallel",)),
    )(proj, proj, proj, lbs, *consts)


def _hgrn_bwd(proj, lbs, states, d_o, consts):
    S = proj.shape[0]
    nc = S // HG_CHUNK

    def body(q_ref, f_ref, i_ref, lb_ref, st_ref, do_ref, tri_ref, trit_ref, mask_ref, dq_ref, df_ref, di_ref, dlb_ref):
        masks = [mask_ref[n] for n in range(len(HG_LEVELS) + 1)]
        lb = lb_ref[...]
        fn = functools.partial(_hgrn_chunk, tri_ref[...], trit_ref[...], masks)

        def chunk(n, carry):
            d_st, dlb = carry
            ci = nc - 1 - n
            r0 = pl.multiple_of(ci * HG_CHUNK, HG_CHUNK)
            rows = pl.ds(r0, HG_CHUNK)
            _, pull = jax.vjp(fn, q_ref[rows, :], f_ref[rows, :], i_ref[rows, :], st_ref[ci], lb)
            dq, df, di, d_prev, dl = pull((do_ref[rows, :], d_st))
            dq_ref[rows, :] = dq.astype(BF)
            df_ref[rows, :] = df.astype(BF)
            di_ref[rows, :] = di.astype(BF)
            return d_prev, dlb + dl

        _, dlb = lax.fori_loop(0, nc, chunk, (jnp.zeros((HG_DIM, HG_DIM), F32), jnp.zeros((1, HG_DIM), F32)))
        dlb_ref[...] = dlb

    col, whole = _hgrn_specs(S, consts)
    head = pl.BlockSpec((None, 1, 128), lambda p: (p, 0, 0))
    out = jax.ShapeDtypeStruct((S, WIDTH), BF)
    return pl.pallas_call(
        body, name="hgrn_bwd",
        grid=(WIDTH // 128,),
        in_specs=[col(16), col(20), col(24), head, pl.BlockSpec((None, nc, HG_DIM, HG_DIM), lambda p: (p, 0, 0, 0)), col(0)] + whole,
        out_specs=[col(0), col(0), col(0), head],
        out_shape=[out, out, out, jax.ShapeDtypeStruct((WIDTH // 128, 1, 128), F32)],
        compiler_params=_params(dimension_semantics=("parallel",)),
    )(proj, proj, proj, lbs, states, d_o, *consts)


def _shift_down(x, n):
    rows = lax.broadcasted_iota(jnp.int32, x.shape, 0)
    return jnp.where(rows >= n, pltpu.roll(x, n, 0), 0.0)


def _shift_up(x, n):
    S = x.shape[0]
    rows = lax.broadcasted_iota(jnp.int32, x.shape, 0)
    return jnp.where(rows < S - n, pltpu.roll(x, S - n, 0), 0.0)


def _branch_fwd(proj, o_a, o_b, norm_w, conv_w, layer):
    S = proj.shape[0]

    def body(oa_ref, za_ref, ob_ref, zb_ref, nw_ref, pre_ref, post_ref, u_ref, zc_ref, cw_ref, ya_ref, yb_ref, yc_ref):
        ya_ref[...] = (oa_ref[...] * _silu(za_ref[...])).astype(BF)
        ob = ob_ref[...]
        rn = lax.rsqrt(jnp.mean(ob * ob, axis=-1, keepdims=True) + RMS_EPS)
        yb_ref[...] = (ob * rn * nw_ref[layer:layer + 1, :] * _silu(zb_ref[...])).astype(BF)
        pu = pre_ref[...] * u_ref[...]
        conv = cw_ref[2:3, :] * pu + cw_ref[1:2, :] * _shift_down(pu, 1) + cw_ref[0:1, :] * _shift_down(pu, 2)
        yc_ref[...] = (post_ref[...] * conv * _silu(zc_ref[...])).astype(BF)

    col = lambda base: pl.BlockSpec((S, 128), lambda p, base=base: (0, base + p))
    out = jax.ShapeDtypeStruct((S, WIDTH), BF)
    return pl.pallas_call(
        body, name="branch_fwd",
        grid=(WIDTH // 128,),
        in_specs=[col(0), col(12), col(0), col(28), pl.BlockSpec(norm_w.shape, lambda p: (0, 0)),
                  col(32), col(36), col(40), col(44), pl.BlockSpec((None, None, 3, 128), lambda p: (p, layer, 0, 0))],
        out_specs=[col(0), col(0), col(0)],
        out_shape=[out, out, out],
        compiler_params=_params(dimension_semantics=("parallel",)),
    )(o_a, proj, o_b, proj, norm_w, proj, proj, proj, proj, conv_w)


def _branch_bwd(proj, o_a, o_b, norm_w, conv_w, dy_a, dy_b, dy_c, layer):
    S = proj.shape[0]

    def dsilu(z):
        s = _sigmoid(z)
        return s * z, s * (1.0 + z * (1.0 - s))

    def body(oa_ref, za_ref, ob_ref, zb_ref, nw_ref, pre_ref, post_ref, u_ref, zc_ref, cw_ref, dya_ref, dyb_ref, dyc_ref,
             doa_ref, dob_ref, dza_ref, dzb_ref, dpre_ref, dpost_ref, du_ref, dzc_ref, dnw_ref, dcw_ref):
        dya = dya_ref[...]
        sa, dsa = dsilu(za_ref[...])
        doa_ref[...] = dya * sa
        dza_ref[...] = (dya * oa_ref[...] * dsa).astype(BF)

        dyb = dyb_ref[...]
        ob = ob_ref[...]
        nw = nw_ref[layer:layer + 1, :]
        sb, dsb = dsilu(zb_ref[...])
        rn = lax.rsqrt(jnp.mean(ob * ob, axis=-1, keepdims=True) + RMS_EPS)
        on = ob * rn
        dzb_ref[...] = (dyb * on * nw * dsb).astype(BF)
        don_w = dyb * sb
        dnw_ref[...] = jnp.sum(don_w * on, axis=0, keepdims=True)
        don = don_w * nw
        dob_ref[...] = rn * (don - on * jnp.mean(don * on, axis=-1, keepdims=True))

        dyc = dyc_ref[...]
        pre, post, u = pre_ref[...], post_ref[...], u_ref[...]
        sc, dsc = dsilu(zc_ref[...])
        pu = pre * u
        pu1, pu2 = _shift_down(pu, 1), _shift_down(pu, 2)
        conv = cw_ref[2:3, :] * pu + cw_ref[1:2, :] * pu1 + cw_ref[0:1, :] * pu2
        dzc_ref[...] = (dyc * post * conv * dsc).astype(BF)
        dpost_ref[...] = (dyc * conv * sc).astype(BF)
        dconv = dyc * post * sc
        dcw_ref[0:1, :] = jnp.sum(dconv * pu2, axis=0, keepdims=True)
        dcw_ref[1:2, :] = jnp.sum(dconv * pu1, axis=0, keepdims=True)
        dcw_ref[2:3, :] = jnp.sum(dconv * pu, axis=0, keepdims=True)
        dpu = cw_ref[2:3, :] * dconv + cw_ref[1:2, :] * _shift_up(dconv, 1) + cw_ref[0:1, :] * _shift_up(dconv, 2)
        dpre_ref[...] = (dpu * u).astype(BF)
        du_ref[...] = (dpu * pre).astype(BF)

    col = lambda base: pl.BlockSpec((S, 128), lambda p, base=base: (0, base + p))
    f32 = jax.ShapeDtypeStruct((S, WIDTH), F32)
    bf = jax.ShapeDtypeStruct((S, WIDTH), BF)
    return pl.pallas_call(
        body, name="branch_bwd",
        grid=(WIDTH // 128,),
        in_specs=[col(0), col(12), col(0), col(28), pl.BlockSpec(norm_w.shape, lambda p: (0, 0)),
                  col(32), col(36), col(40), col(44), pl.BlockSpec((None, None, 3, 128), lambda p: (p, layer, 0, 0)),
                  col(0), col(0), col(0)],
        out_specs=[col(0)] * 8 + [pl.BlockSpec((None, 1, 128), lambda p: (p, 0, 0)), pl.BlockSpec((None, 3, 128), lambda p: (p, 0, 0))],
        out_shape=[f32, f32, bf, bf, bf, bf, bf, bf, jax.ShapeDtypeStruct((WIDTH // 128, 1, 128), F32),
                   jax.ShapeDtypeStruct((WIDTH // 128, 3, 128), F32)],
        compiler_params=_params(dimension_semantics=("parallel",)),
    )(o_a, proj, o_b, proj, norm_w, proj, proj, proj, proj, conv_w, dy_a, dy_b, dy_c)


def _branch_proj(y_refs, wb_ref, layer):
    out = []
    for i in range(3):
        yv = y_refs[i][...]
        out.append(jnp.concatenate([_dot(yv, wb_ref[j, layer, i]) for j in range(N_CHIPS)], axis=1))
    return out


def _merge_fwd(x, mod, proj, ys, wb, wo, ln_g, ln_b, layer, tm=256):
    S, D = x.shape
    tm = min(tm, S)

    def body(x_ref, mod_ref, ga_ref, gb_ref, gc_ref, ya_ref, yb_ref, yc_ref, wb_ref, wo_ref, g_ref, b_ref, xo_ref, mg_ref, y_ref):
        ps = _branch_proj((ya_ref, yb_ref, yc_ref), wb_ref, layer)
        merged = _sigmoid(ga_ref[...]) * ps[0] + _sigmoid(gb_ref[...]) * ps[1] + _sigmoid(gc_ref[...]) * ps[2]
        mb = merged.astype(BF)
        mg_ref[...] = mb
        y = _dot(mb, wo_ref[...].reshape(D, D))
        y_ref[...] = y
        r = ALPHA * x_ref[...] + (1.0 + mod_ref[:, 2 * D:3 * D]) * y
        xn, _ = _standardize(r)
        xo_ref[...] = xn * g_ref[layer:layer + 1, :] + b_ref[layer:layer + 1, :]

    row = lambda w, c=0: pl.BlockSpec((tm, w), lambda i, c=c: (i, c))
    whole = lambda a: pl.BlockSpec(a.shape, lambda i, n=a.ndim: (0,) * n)
    return pl.pallas_call(
        body, name="merge_fwd",
        grid=(S // tm,),
        in_specs=[row(D), whole(mod), row(D, 6), row(D, 7), row(D, 8), row(WIDTH), row(WIDTH), row(WIDTH), whole(wb),
                  pl.BlockSpec((N_CHIPS, None, D // N_CHIPS, D), lambda i: (0, layer, 0, 0)), whole(ln_g), whole(ln_b)],
        out_specs=[row(D), row(D), row(D)],
        out_shape=[jax.ShapeDtypeStruct((S, D), F32), jax.ShapeDtypeStruct((S, D), BF), jax.ShapeDtypeStruct((S, D), F32)],
        compiler_params=_params(dimension_semantics=("parallel",)),
    )(x, mod, proj, proj, proj, *ys, wb, wo, ln_g, ln_b)


def _merge_bwd(dxo, x, y, mod, proj, ys, wb, wo, ln_g, layer, tm=256):
    S, D = x.shape
    tm = min(tm, S)

    def body(dxo_ref, x_ref, y_ref, mod_ref, ga_ref, gb_ref, gc_ref, ya_ref, yb_ref, yc_ref, wb_ref, wo_ref, g_ref,
             dxr_ref, dy_ref, dp_ref, dg_ref, dya_ref, dyb_ref, dyc_ref, dlg_ref, dlb_ref, dgt_ref):
        @pl.when(pl.program_id(0) == 0)
        def _():
            dlg_ref[...] = jnp.zeros_like(dlg_ref)
            dlb_ref[...] = jnp.zeros_like(dlb_ref)
            dgt_ref[...] = jnp.zeros_like(dgt_ref)

        gate1 = 1.0 + mod_ref[:, 2 * D:3 * D]
        yv = y_ref[...]
        xn, rstd = _standardize(ALPHA * x_ref[...] + gate1 * yv)
        dxo = dxo_ref[...]
        dlg_ref[...] += jnp.sum(dxo * xn, axis=0, keepdims=True)
        dlb_ref[...] += jnp.sum(dxo, axis=0, keepdims=True)
        dr = _standardize_bwd(dxo * g_ref[layer:layer + 1, :], xn, rstd)
        dxr_ref[...] = ALPHA * dr
        dgt_ref[...] += jnp.sum(dr * yv, axis=0, keepdims=True)
        dyb = (gate1 * dr).astype(BF)
        dy_ref[...] = dyb
        dmerged = _dot_nt(dyb, wo_ref[...].reshape(D, D))
        ps = _branch_proj((ya_ref, yb_ref, yc_ref), wb_ref, layer)
        quarter = D // N_CHIPS
        for i, (gate_ref, out_ref) in enumerate(((ga_ref, dya_ref), (gb_ref, dyb_ref), (gc_ref, dyc_ref))):
            sg = _sigmoid(gate_ref[...])
            dg_ref[:, i * D:(i + 1) * D] = (dmerged * ps[i] * sg * (1.0 - sg)).astype(BF)
            dp = (dmerged * sg).astype(BF)
            dp_ref[:, i * D:(i + 1) * D] = dp
            acc = None
            for j in range(N_CHIPS):
                t = _dot_nt(dp[:, j * quarter:(j + 1) * quarter], wb_ref[j, layer, i])
                acc = t if acc is None else acc + t
            out_ref[...] = acc

    row = lambda w, c=0: pl.BlockSpec((tm, w), lambda i, c=c: (i, c))
    whole = lambda a: pl.BlockSpec(a.shape, lambda i, n=a.ndim: (0,) * n)
    vec = pl.BlockSpec((1, D), lambda i: (0, 0))
    sd = jax.ShapeDtypeStruct
    return pl.pallas_call(
        body, name="merge_bwd",
        grid=(S // tm,),
        in_specs=[row(D), row(D), row(D), whole(mod), row(D, 6), row(D, 7), row(D, 8), row(WIDTH), row(WIDTH), row(WIDTH), whole(wb),
                  pl.BlockSpec((N_CHIPS, None, D // N_CHIPS, D), lambda i: (0, layer, 0, 0)), whole(ln_g)],
        out_specs=[row(D), row(D), row(3 * D), row(3 * D), row(WIDTH), row(WIDTH), row(WIDTH), vec, vec, vec],
        out_shape=[sd((S, D), F32), sd((S, D), BF), sd((S, 3 * D), BF), sd((S, 3 * D), BF), sd((S, WIDTH), F32), sd((S, WIDTH), F32),
                   sd((S, WIDTH), F32), sd((1, D), F32), sd((1, D), F32), sd((1, D), F32)],
        compiler_params=_params(dimension_semantics=("arbitrary",)),
    )(dxo, x, y, mod, proj, proj, proj, *ys, wb, wo, ln_g)


def _loss_head(x, target, tm=512):
    S, D = x.shape
    tm = min(tm, S)

    def body(x_ref, t_ref, dx_ref, loss_ref):
        @pl.when(pl.program_id(0) == 0)
        def _():
            loss_ref[...] = jnp.zeros_like(loss_ref)

        err = x_ref[...] - t_ref[...]
        dx_ref[...] = err * (1.0 / D)
        loss_ref[...] += 0.5 * jnp.sum(jnp.mean(err * err, axis=-1, keepdims=True))

    row = pl.BlockSpec((tm, D), lambda i: (i, 0))
    return pl.pallas_call(
        body, name="loss_head",
        grid=(S // tm,),
        in_specs=[row, row],
        out_specs=[row, pl.BlockSpec((8, 128), lambda i: (0, 0))],
        out_shape=[jax.ShapeDtypeStruct((S, D), F32), jax.ShapeDtypeStruct((8, 128), F32)],
        compiler_params=_params(dimension_semantics=("arbitrary",)),
    )(x, target)


def _proj_bwd(dproj, wg, x, mod, dx_res, layer, tm=512, tk=768):
    S, D = x.shape
    tm = min(tm, S)
    shard = wg.shape[-1]
    per = shard // tk
    nk = IN_COLS // tk

    def body(dp_ref, w_ref, x_ref, mod_ref, dxr_ref, dx_ref, dsh_ref, dsc_ref, acc):
        i, k = pl.program_id(0), pl.program_id(1)

        @pl.when((i == 0) & (k == 0))
        def _():
            dsh_ref[...] = jnp.zeros_like(dsh_ref)
            dsc_ref[...] = jnp.zeros_like(dsc_ref)

        @pl.when(k == 0)
        def _():
            acc[...] = jnp.zeros_like(acc)

        acc[...] += _dot_nt(dp_ref[...], w_ref[...])

        @pl.when(k == nk - 1)
        def _():
            dh = acc[...]
            xs, rstd = _standardize(x_ref[...])
            dsh_ref[...] += jnp.sum(dh, axis=0, keepdims=True)
            dsc_ref[...] += jnp.sum(dh * xs, axis=0, keepdims=True)
            dx_ref[...] = _standardize_bwd(dh * (1.0 + mod_ref[:, D:2 * D]), xs, rstd) + dxr_ref[...]

    row = pl.BlockSpec((tm, D), lambda i, k: (i, 0))
    vec = pl.BlockSpec((1, D), lambda i, k: (0, 0))
    return pl.pallas_call(
        body, name="proj_bwd",
        grid=(S // tm, nk),
        in_specs=[pl.BlockSpec((tm, tk), lambda i, k: (i, k)),
                  pl.BlockSpec((None, None, D, tk), lambda i, k: (k // per, layer, 0, k % per)),
                  row, pl.BlockSpec((1, 3 * D), lambda i, k: (0, 0)), row],
        out_specs=[row, vec, vec],
        out_shape=[jax.ShapeDtypeStruct((S, D), F32), jax.ShapeDtypeStruct((1, D), F32), jax.ShapeDtypeStruct((1, D), F32)],
        scratch_shapes=[pltpu.VMEM((tm, D), F32)],
        compiler_params=_params(dimension_semantics=("arbitrary", "arbitrary")),
    )(dproj, wg, x, mod, dx_res)


def _grad_w(a, b, mid, prev, name, tm, tn, b_col0=0):
    S, M = a.shape
    shape = prev.shape
    n_shard = shape[-1]
    per = n_shard // tn
    nm = M // tm
    lead = len(mid)

    def body(*refs):
        a_ref, b_ref, o_ref = refs[0], refs[1], refs[-1]
        o_ref[...] = _dot_tn(a_ref[...], b_ref[...]).astype(BF)

    in_specs = [pl.BlockSpec((S, tm), lambda m, n: (0, m)),
                pl.BlockSpec((S, tn), lambda m, n: (0, b_col0 // tn + n))]
    args = [a, b]
    aliases = {}
    if not isinstance(prev, jax.ShapeDtypeStruct):
        in_specs.append(pl.BlockSpec(memory_space=pl.ANY))
        args.append(prev)
        aliases = {2: 0}
    return pl.pallas_call(
        body, name=name,
        grid=(nm, N_CHIPS * per),
        in_specs=in_specs,
        out_specs=pl.BlockSpec((None,) + (None,) * lead + (tm, tn), lambda m, n: (n // per,) + tuple(mid) + (m, n % per)),
        out_shape=jax.ShapeDtypeStruct(shape, BF),
        input_output_aliases=aliases,
        compiler_params=_params(dimension_semantics=("parallel", "parallel")),
    )(*args)


def _flip(v, bit):
    return 1 - v if bit else v


def _all_gather8(x, name):
    R, N = x.shape

    def body(x_ref, out_ref, send_sems, recv_sems):
        mx, my, mc = lax.axis_index("x"), lax.axis_index("y"), lax.axis_index("c")
        me = 4 * mx + 2 * my + mc
        out_ref[me] = x_ref[...]
        copies = []
        for k in range(1, N_DEV):
            peer = (_flip(mx, k & 4), _flip(my, k & 2), _flip(mc, k & 1))
            cp = pltpu.make_async_remote_copy(src_ref=x_ref, dst_ref=out_ref.at[me], send_sem=send_sems.at[k - 1],
                                              recv_sem=recv_sems.at[k - 1], device_id=peer, device_id_type=MESH)
            cp.start()
            copies.append(cp)
        for cp in copies:
            cp.wait()

    return pl.pallas_call(
        body, name=name,
        in_specs=[pl.BlockSpec(memory_space=pltpu.VMEM)],
        out_specs=pl.BlockSpec(memory_space=pltpu.VMEM),
        out_shape=jax.ShapeDtypeStruct((N_DEV, R, N), F32),
        scratch_shapes=[pltpu.SemaphoreType.DMA((N_DEV - 1,)), pltpu.SemaphoreType.DMA((N_DEV - 1,))],
        compiler_params=_params(),
    )(x)


def _gather_weights(shards):
    n = len(shards)

    def body(*refs):
        ins, outs = refs[:n], refs[n:2 * n]
        local_sems, send_sems, recv_sems = refs[2 * n:]
        mx, my, mc = lax.axis_index("x"), lax.axis_index("y"), lax.axis_index("c")
        mine = 2 * mx + my
        copies = []
        for a in range(n):
            cp = pltpu.make_async_copy(ins[a], outs[a].at[mine], local_sems.at[a])
            cp.start()
            copies.append(cp)
            for k in range(1, N_CHIPS):
                peer = (_flip(mx, k & 2), _flip(my, k & 1), mc)
                cp = pltpu.make_async_remote_copy(src_ref=ins[a], dst_ref=outs[a].at[mine], send_sem=send_sems.at[a, k - 1],
                                                  recv_sem=recv_sems.at[a, k - 1], device_id=peer, device_id_type=MESH)
                cp.start()
                copies.append(cp)
        for cp in copies:
            cp.wait()

    hbm = pl.BlockSpec(memory_space=pl.ANY)
    return pl.pallas_call(
        body, name="gather_weights",
        in_specs=[hbm] * n,
        out_specs=[hbm] * n,
        out_shape=[jax.ShapeDtypeStruct((N_CHIPS,) + s.shape, s.dtype) for s in shards],
        scratch_shapes=[pltpu.SemaphoreType.DMA((n,)), pltpu.SemaphoreType.DMA((n, N_CHIPS - 1)), pltpu.SemaphoreType.DMA((n, N_CHIPS - 1))],
        compiler_params=_params(),
    )(*shards)


def _scatter_grads(grads):
    n = len(grads)

    def body(*refs):
        ins, owns, gots = refs[:n], refs[n:2 * n], refs[2 * n:3 * n]
        local_sems, send_sems, recv_sems = refs[3 * n:]
        mx, my, mc = lax.axis_index("x"), lax.axis_index("y"), lax.axis_index("c")
        mine = 2 * mx + my
        copies = []
        for a in range(n):
            cp = pltpu.make_async_copy(ins[a].at[mine], owns[a], local_sems.at[a])
            cp.start()
            copies.append(cp)
            for k in range(1, N_CHIPS):
                px, py = _flip(mx, k & 2), _flip(my, k & 1)
                cp = pltpu.make_async_remote_copy(src_ref=ins[a].at[2 * px + py], dst_ref=gots[a].at[k - 1], send_sem=send_sems.at[a, k - 1],
                                                  recv_sem=recv_sems.at[a, k - 1], device_id=(px, py, mc), device_id_type=MESH)
                cp.start()
                copies.append(cp)
        for cp in copies:
            cp.wait()

    hbm = pl.BlockSpec(memory_space=pl.ANY)
    return pl.pallas_call(
        body, name="scatter_grads",
        in_specs=[hbm] * n,
        out_specs=[hbm] * (2 * n),
        out_shape=[jax.ShapeDtypeStruct(g.shape[1:], g.dtype) for g in grads]
        + [jax.ShapeDtypeStruct((N_CHIPS - 1,) + g.shape[1:], g.dtype) for g in grads],
        scratch_shapes=[pltpu.SemaphoreType.DMA((n,)), pltpu.SemaphoreType.DMA((n, N_CHIPS - 1)), pltpu.SemaphoreType.DMA((n, N_CHIPS - 1))],
        compiler_params=_params(),
    )(*grads)


def _swap_sibling(parts):
    n = len(parts)

    def body(*refs):
        ins, outs = refs[:n], refs[n:2 * n]
        send_sems, recv_sems = refs[2 * n:]
        peer = (lax.axis_index("x"), lax.axis_index("y"), 1 - lax.axis_index("c"))
        copies = []
        for a in range(n):
            cp = pltpu.make_async_remote_copy(src_ref=ins[a], dst_ref=outs[a], send_sem=send_sems.at[a], recv_sem=recv_sems.at[a],
                                              device_id=peer, device_id_type=MESH)
            cp.start()
            copies.append(cp)
        for cp in copies:
            cp.wait()

    hbm = pl.BlockSpec(memory_space=pl.ANY)
    return pl.pallas_call(
        body, name="swap_sibling",
        in_specs=[hbm] * n,
        out_specs=[hbm] * n,
        out_shape=[jax.ShapeDtypeStruct(p.shape, p.dtype) for p in parts],
        scratch_shapes=[pltpu.SemaphoreType.DMA((n,)), pltpu.SemaphoreType.DMA((n,))],
        compiler_params=_params(),
    )(*parts)


def _rows2d(a):
    return a.reshape(-1, a.shape[-1])


def _tile_rows(rows, cols, n_arrays):
    budget = (24 << 20) // (n_arrays * 2 * 4 * cols)
    if rows <= budget:
        return rows
    tm = 8
    for cand in range(8, budget + 1, 8):
        if rows % cand == 0:
            tm = cand
    return tm


def _sum4(own, got):
    o2 = _rows2d(own)
    rows, cols = o2.shape
    g3 = got.reshape(3, rows, cols)
    tm = _tile_rows(rows, cols, 3)

    def body(o_ref, g_ref, out_ref):
        out_ref[...] = ((o_ref[...].astype(F32) + g_ref[0].astype(F32)) + g_ref[1].astype(F32)) + g_ref[2].astype(F32)

    out = pl.pallas_call(
        body, name="sum_partials",
        grid=(rows // tm,),
        in_specs=[pl.BlockSpec((tm, cols), lambda i: (i, 0)), pl.BlockSpec((3, tm, cols), lambda i: (0, i, 0))],
        out_specs=pl.BlockSpec((tm, cols), lambda i: (i, 0)),
        out_shape=jax.ShapeDtypeStruct((rows, cols), F32),
        compiler_params=_params(dimension_semantics=("parallel",)),
    )(o2, g3)
    return out.reshape(own.shape)


def _adamw(w, m, v, g_parts, name):
    shape = w.shape
    w2, m2, v2 = _rows2d(w), _rows2d(m), _rows2d(v)
    gs = [_rows2d(g) for g in g_parts]
    rows, cols = w2.shape
    n = len(gs)
    tm = _tile_rows(rows, cols, 7 + n)
    c1 = 1.0 / (1.0 - ADAM_B1 ** ADAM_STEP)
    c2 = 1.0 / (1.0 - ADAM_B2 ** ADAM_STEP)

    def body(*refs):
        w_ref, m_ref, v_ref = refs[:3]
        g_refs = refs[3:3 + n]
        go_ref, d_ref, mo_ref, vo_ref = refs[3 + n:]
        g = g_refs[0][...]
        for r in g_refs[1:]:
            g = g + r[...]
        mn = ADAM_B1 * m_ref[...] + (1.0 - ADAM_B1) * g
        vn = ADAM_B2 * v_ref[...] + (1.0 - ADAM_B2) * (g * g)
        go_ref[...] = g
        mo_ref[...] = mn
        vo_ref[...] = vn
        d_ref[...] = -ADAM_LR * ((mn * c1) / (jnp.sqrt(vn * c2) + ADAM_EPS) + ADAM_WD * w_ref[...])

    spec = pl.BlockSpec((tm, cols), lambda i: (i, 0))
    outs = pl.pallas_call(
        body, name=name,
        grid=(rows // tm,),
        in_specs=[spec] * (3 + n),
        out_specs=[spec] * 4,
        out_shape=[jax.ShapeDtypeStruct((rows, cols), F32)] * 4,
        compiler_params=_params(dimension_semantics=("parallel",)),
    )(w2, m2, v2, *gs)
    return [o.reshape(shape) for o in outs]


def _lower_bounds(r0, r1):
    top = jnp.maximum(r0, r1)
    e0, e1 = jnp.exp(r0 - top), jnp.exp(r1 - top)
    p0, p1 = e0 / (e0 + e1), e1 / (e0 + e1)
    return p0 - p0, (p0 + p1) - p0


def _lbs_fwd(lb_raw):
    def body(lb_ref, out_ref):
        l0, l1 = _lower_bounds(lb_ref[0:1, :], lb_ref[1:2, :])
        out_ref[0:1, :] = l0
        out_ref[1:2, :] = l1

    return pl.pallas_call(body, name="lower_bounds", out_shape=jax.ShapeDtypeStruct(lb_raw.shape, F32), compiler_params=_params())(lb_raw)


def _mod_rows(c_all, w_mod, tn=768):
    _, D, cols = w_mod.shape

    def body(c_ref, w_ref, out_ref):
        out_ref[...] = _dot(c_ref[...].astype(BF), w_ref[...].astype(BF))

    return pl.pallas_call(
        body, name="mod_rows",
        grid=(DEPTH,),
        in_specs=[pl.BlockSpec((N_DEV, D), lambda l: (0, 0)), pl.BlockSpec((None, D, cols), lambda l: (l, 0, 0))],
        out_specs=pl.BlockSpec((N_DEV, cols), lambda l: (0, l)),
        out_shape=jax.ShapeDtypeStruct((N_DEV, DEPTH * cols), F32),
        compiler_params=_params(dimension_semantics=("parallel",)),
    )(c_all, w_mod)


def _grad_w_mod(c_all, dmod_cols):
    D = c_all.shape[1]
    cols = dmod_cols.shape[-1]

    def body(c_ref, d_ref, out_ref):
        out_ref[...] = _dot_tn(c_ref[...].astype(BF), d_ref[...].astype(BF))

    return pl.pallas_call(
        body, name="grad_w_mod",
        grid=(DEPTH,),
        in_specs=[pl.BlockSpec((N_DEV, D), lambda l: (0, 0)), pl.BlockSpec((None, N_DEV, cols), lambda l: (l, 0, 0))],
        out_specs=pl.BlockSpec((None, D, cols), lambda l: (l, 0, 0)),
        out_shape=jax.ShapeDtypeStruct((DEPTH, D, cols), F32),
        compiler_params=_params(dimension_semantics=("parallel",)),
    )(c_all, dmod_cols)


def _sum_devices(parts, lb_raw, n_lb):
    _, R, N = parts.shape

    def body(p_ref, lb_ref, out_ref, dlb_ref):
        acc = p_ref[0]
        for d in range(1, N_DEV):
            acc = acc + p_ref[d]
        out_ref[...] = acc
        dl = tuple(acc[0:1, n_lb + l * WIDTH:n_lb + (l + 1) * WIDTH] for l in range(DEPTH))
        _, pull = jax.vjp(_lower_bounds, lb_ref[0:1, :], lb_ref[1:2, :])
        d0, d1 = pull(dl)
        dlb_ref[0:1, :] = d0
        dlb_ref[1:2, :] = d1

    return pl.pallas_call(
        body, name="sum_devices",
        out_shape=[jax.ShapeDtypeStruct((R, N), F32), jax.ShapeDtypeStruct(lb_raw.shape, F32)],
        compiler_params=_params(),
    )(parts, lb_raw)


def _pad_rows(a, rows=8):
    return jnp.concatenate([a, jnp.zeros((rows - a.shape[0],) + a.shape[1:], a.dtype)], axis=0)


def kernel(x, c, w_mod, b_mod, w_in, conv_w, hgrn_norm_w, lower_bounds, w_branch, w_out, ln_g, ln_b, loss_target, m_w_mod, m_b_mod, m_w_in, m_conv_w, m_hgrn_norm_w, m_lower_bounds, m_w_branch, m_w_out, m_ln_g, m_ln_b, v_w_mod, v_b_mod, v_w_in, v_conv_w, v_hgrn_norm_w, v_lower_bounds, v_w_branch, v_w_out, v_ln_g, v_ln_b):
    D = D_MODEL
    x0 = x[0]
    target = loss_target[0]
    S = x0.shape[0]
    mx, my, mc = lax.axis_index("x"), lax.axis_index("y"), lax.axis_index("c")
    chip = 2 * mx + my
    me = 2 * chip + mc
    mod_cols = 3 * D // N_CHIPS

    wg_in, wg_br, wg_out = _gather_weights([w_in.astype(BF), w_branch.astype(BF), w_out.astype(BF)])

    n_conv = DEPTH * 3 * (WIDTH // N_CHIPS)
    first = jnp.concatenate([c, conv_w.reshape(1, n_conv)], axis=1)
    first = _all_gather8(_pad_rows(first), "gather_c")
    c_all = first[:, 0, :D]
    conv_all = first[:, 0, D:].reshape(N_DEV, DEPTH, 3, WIDTH // N_CHIPS)[0::2]
    mod_part = _all_gather8(_mod_rows(c_all, w_mod), "gather_mod")[0::2]
    mod_part = lax.dynamic_index_in_dim(mod_part, me, axis=1, keepdims=False).reshape(N_CHIPS, DEPTH, mod_cols)
    mods = [(mod_part[:, l].reshape(1, 3 * D) + b_mod[l][None, :]) for l in range(DEPTH)]
    lbs = _lbs_fwd(lower_bounds).reshape(DEPTH, WIDTH // 128, 1, 128)
    loss_blk, dx, big, small = _local_step(x0, target, mods, lbs, wg_in, wg_br, wg_out, conv_all, hgrn_norm_w, ln_g, ln_b)

    scattered = _scatter_grads(big)
    partial = [_sum4(scattered[a], scattered[len(big) + a]) for a in range(len(big))]
    sibling = _swap_sibling(partial)

    n_mod, n_nw, n_lb, n_ln, n_cw = DEPTH * 3 * D, DEPTH * 128, DEPTH * WIDTH, DEPTH * D, DEPTH * 3 * WIDTH
    row = jnp.concatenate(
        [jnp.concatenate([small[l][0], small[l][1], small[l][2]], axis=1) for l in range(DEPTH)]
        + [jnp.sum(small[l][3], axis=0) for l in range(DEPTH)]
        + [small[l][4].reshape(1, WIDTH) for l in range(DEPTH)]
        + [small[l][5] for l in range(DEPTH)] + [small[l][6] for l in range(DEPTH)]
        + [jnp.transpose(small[l][7], (1, 0, 2)).reshape(1, 3 * WIDTH) for l in range(DEPTH)]
        + [loss_blk[0:1, :]], axis=1)
    gathered = _all_gather8(_pad_rows(row), "gather_small")
    off_nw = n_mod
    off_lb = off_nw + n_nw
    off_lng = off_lb + n_lb
    off_lnb = off_lng + n_ln
    off_cw = off_lnb + n_ln
    off_loss = off_cw + n_cw
    total, d_lower = _sum_devices(gathered, lower_bounds, off_lb)
    loss = total[0, off_loss]
    d_b_mod = total[0, :n_mod].reshape(DEPTH, 3 * D)
    d_norm_w = total[0, off_nw:off_lb].reshape(DEPTH, 128)
    d_ln_g = total[0, off_lng:off_lnb].reshape(DEPTH, D)
    d_ln_b = total[0, off_lnb:off_cw].reshape(DEPTH, D)
    d_conv = total[0, off_cw:off_loss].reshape(DEPTH, 3, N_CHIPS, WIDTH // N_CHIPS)
    d_conv = lax.dynamic_index_in_dim(d_conv, chip, axis=2, keepdims=False)
    dmod_all = gathered[:, 0, :n_mod].reshape(N_DEV, DEPTH, N_CHIPS, mod_cols)
    dmod_cols = jnp.transpose(lax.dynamic_index_in_dim(dmod_all, chip, axis=2, keepdims=False), (1, 0, 2))
    d_w_mod = _grad_w_mod(c_all, dmod_cols)

    res = {}
    res["w_mod"] = _adamw(w_mod, m_w_mod, v_w_mod, [d_w_mod], "adamw_w_mod")
    res["b_mod"] = _adamw(b_mod, m_b_mod, v_b_mod, [d_b_mod], "adamw_b_mod")
    res["w_in"] = _adamw(w_in, m_w_in, v_w_in, [partial[0], sibling[0]], "adamw_w_in")
    res["conv_w"] = _adamw(conv_w, m_conv_w, v_conv_w, [d_conv], "adamw_conv_w")
    res["hgrn_norm_w"] = _adamw(hgrn_norm_w, m_hgrn_norm_w, v_hgrn_norm_w, [d_norm_w], "adamw_norm_w")
    res["lower_bounds"] = _adamw(lower_bounds, m_lower_bounds, v_lower_bounds, [d_lower], "adamw_lower_bounds")
    res["w_branch"] = _adamw(w_branch, m_w_branch, v_w_branch, [partial[1], sibling[1]], "adamw_w_branch")
    res["w_out"] = _adamw(w_out, m_w_out, v_w_out, [partial[2], sibling[2]], "adamw_w_out")
    res["ln_g"] = _adamw(ln_g, m_ln_g, v_ln_g, [d_ln_g], "adamw_ln_g")
    res["ln_b"] = _adamw(ln_b, m_ln_b, v_ln_b, [d_ln_b], "adamw_ln_b")
    names = ["w_mod", "b_mod", "w_in", "conv_w", "hgrn_norm_w", "lower_bounds", "w_branch", "w_out", "ln_g", "ln_b"]
    return (loss, dx[None], *[res[n][0] for n in names], *[res[n][1] for n in names],
            *[res[n][2] for n in names], *[res[n][3] for n in names])


def _local_step(x0, target, mods, lbs, wg_in, wg_br, wg_out, conv_all, hgrn_norm_w, ln_g, ln_b):
    D = D_MODEL
    after, before = _attn_consts()
    hg_consts = _hgrn_consts()

    saved = []
    xl = x0
    for l in range(DEPTH):
        proj, h = _proj_fwd(xl, mods[l], wg_in, l)
        o_a = _attn_fwd(proj, after)
        o_b, states = _hgrn_fwd(proj, lbs[l], hg_consts)
        ys = _branch_fwd(proj, o_a, o_b, hgrn_norm_w, conv_all, l)
        x_next, merged, y = _merge_fwd(xl, mods[l], proj, ys, wg_br, wg_out, ln_g, ln_b, l)
        saved.append((xl, proj, h, o_a, o_b, states, ys, merged, y))
        xl = x_next
    dx, loss_blk = _loss_head(xl, target)

    g_in = jax.ShapeDtypeStruct((N_CHIPS, DEPTH, D, IN_COLS // N_CHIPS), BF)
    g_br = jax.ShapeDtypeStruct((N_CHIPS, DEPTH, 3, WIDTH, D // N_CHIPS), BF)
    g_out = jax.ShapeDtypeStruct((N_CHIPS, DEPTH, D // N_CHIPS, D), BF)
    small = [None] * DEPTH
    for l in reversed(range(DEPTH)):
        xin, proj, h, o_a, o_b, states, ys, merged, y = saved[l]
        dx_res, dy, dp, dg, dy_a, dy_b, dy_c, dln_g, dln_b, dgate = _merge_bwd(dx, xin, y, mods[l], proj, ys, wg_br, wg_out, ln_g, l)
        g_out = _grad_w_out(merged, dy, l, g_out)
        for i in range(3):
            g_br = _grad_w(ys[i], dp, (l, i), g_br, "grad_w_branch", tm=WIDTH, tn=D // N_CHIPS, b_col0=i * D)
        d_oa, d_ob, dz_a, dz_b, dpre, dpost, du, dz_c, dnorm_w, dconv_w = _branch_bwd(proj, o_a, o_b, hgrn_norm_w, conv_all, dy_a, dy_b, dy_c, l)
        dq_a, dk_a, dv_a = _attn_bwd(proj, d_oa, after, before)
        dq_b, df_b, di_b, dlb = _hgrn_bwd(proj, lbs[l], states, d_ob, hg_consts)
        dproj = jnp.concatenate([dq_a, dk_a, dv_a, dz_a, dq_b, df_b, di_b, dz_b, dpre, dpost, du, dz_c, dg], axis=1)
        g_in = _grad_w(h, dproj, (l,), g_in, "grad_w_in", tm=512, tn=768)
        dx, dshift, dscale = _proj_bwd(dproj, wg_in, xin, mods[l], dx_res, l)
        small[l] = (dshift, dscale, dgate, dnorm_w, dlb, dln_g, dln_b, dconv_w)
    return loss_blk, dx, [g_in, g_br, g_out], small


def _grad_w_out(merged, dy, layer, prev):
    S, D = merged.shape
    q = D // N_CHIPS

    def body(*refs):
        a_ref, b_ref, o_ref = refs[0], refs[1], refs[-1]
        o_ref[...] = _dot_tn(a_ref[...], b_ref[...]).astype(BF)

    in_specs = [pl.BlockSpec((S, q), lambda j: (0, j)), pl.BlockSpec((S, D), lambda j: (0, 0))]
    args = [merged, dy]
    aliases = {}
    if not isinstance(prev, jax.ShapeDtypeStruct):
        in_specs.append(pl.BlockSpec(memory_space=pl.ANY))
        args.append(prev)
        aliases = {2: 0}
    return pl.pallas_call(
        body, name="grad_w_out",
        grid=(N_CHIPS,),
        in_specs=in_specs,
        out_specs=pl.BlockSpec((None, None, q, D), lambda j: (j, layer, 0, 0)),
        out_shape=jax.ShapeDtypeStruct(prev.shape, BF),
        input_output_aliases=aliases,
        compiler_params=_params(dimension_semantics=("parallel",)),
    )(*args)
```

```python
import functools
import math

import numpy as np
import jax
import jax.numpy as jnp
from jax import lax
from jax.experimental import pallas as pl
from jax.experimental.pallas import tpu as pltpu

F32 = jnp.float32
BF = jnp.bfloat16
MESH = pl.DeviceIdType.MESH

DEPTH = 2
D_MODEL = 1024
WIDTH = 512
IN_COLS = 12 * WIDTH + 3 * D_MODEL
N_CHIPS = 4
N_DEV = 8
SB_BLOCK = 128
SB_HEAD_DIM = 64
HG_CHUNK = 64
HG_DIM = 128
LN_EPS = 1e-5
RMS_EPS = 1e-6
ALPHA = (2.0 * DEPTH) ** 0.25
ADAM_LR, ADAM_B1, ADAM_B2, ADAM_EPS, ADAM_WD, ADAM_STEP = 0.001, 0.9, 0.999, 1e-08, 0.01, 10
VMEM_LIMIT = 56 << 20


def _params(**kw):
    return pltpu.CompilerParams(vmem_limit_bytes=VMEM_LIMIT, **kw)


def _dot(a, b):
    return jnp.dot(a, b, preferred_element_type=F32)


def _dot_nt(a, b):
    return lax.dot_general(a, b, (((1,), (1,)), ((), ())), preferred_element_type=F32)


def _dot_tn(a, b):
    return lax.dot_general(a, b, (((0,), (0,)), ((), ())), preferred_element_type=F32)


def _sigmoid(x):
    return 1.0 / (1.0 + jnp.exp(-x))


def _silu(x):
    return x * _sigmoid(x)


def _softplus(z):
    return jnp.maximum(z, 0.0) + jnp.log(1.0 + jnp.exp(-jnp.abs(z)))


def _split_dot(t, g, terms):
    acc = None
    rest = g
    for _ in range(terms):
        part = rest.astype(BF)
        rest = rest - part.astype(F32)
        d = _dot(t, part)
        acc = d if acc is None else acc + d
    return acc


def _split_dot_r(g, t, terms):
    acc = None
    rest = g
    for _ in range(terms):
        part = rest.astype(BF)
        rest = rest - part.astype(F32)
        d = _dot(part, t)
        acc = d if acc is None else acc + d
    return acc


def _standardize(x):
    mu = jnp.mean(x, axis=-1, keepdims=True)
    xc = x - mu
    var = jnp.mean(xc * xc, axis=-1, keepdims=True)
    rstd = lax.rsqrt(var + LN_EPS)
    return xc * rstd, rstd


def _standardize_bwd(dxs, xs, rstd):
    return rstd * (dxs - jnp.mean(dxs, axis=-1, keepdims=True) - xs * jnp.mean(dxs * xs, axis=-1, keepdims=True))


def _proj_fwd(x, mod, wg, layer, tm=512, tn=768):
    S, D = x.shape
    tm = min(tm, S)
    shard = wg.shape[-1]
    per = shard // tn

    def body(x_ref, mod_ref, w_ref, proj_ref, h_ref, hs):
        @pl.when(pl.program_id(1) == 0)
        def _():
            xs, _ = _standardize(x_ref[...])
            h = xs * (1.0 + mod_ref[:, D:2 * D]) + mod_ref[:, 0:D]
            hb = h.astype(BF)
            hs[...] = hb
            h_ref[...] = hb

        proj_ref[...] = _dot(hs[...], w_ref[...])

    return pl.pallas_call(
        body, name="proj_fwd",
        grid=(S // tm, IN_COLS // tn),
        in_specs=[pl.BlockSpec((tm, D), lambda i, j: (i, 0)),
                  pl.BlockSpec((1, 3 * D), lambda i, j: (0, 0)),
                  pl.BlockSpec((None, None, D, tn), lambda i, j: (j // per, layer, 0, j % per))],
        out_specs=[pl.BlockSpec((tm, tn), lambda i, j: (i, j)),
                   pl.BlockSpec((tm, D), lambda i, j: (i, 0))],
        out_shape=[jax.ShapeDtypeStruct((S, IN_COLS), F32), jax.ShapeDtypeStruct((S, D), BF)],
        scratch_shapes=[pltpu.VMEM((tm, D), BF)],
        compiler_params=_params(dimension_semantics=("parallel", "arbitrary")),
    )(x, mod, wg)


def _attn_consts():
    j = np.arange(SB_BLOCK)[:, None]
    s = np.arange(SB_BLOCK)[None, :]
    ones = np.ones((SB_BLOCK, SB_BLOCK), np.float32)
    after = np.concatenate([(j > s).astype(np.float32), ones], axis=1)
    before = np.concatenate([(j < s).astype(np.float32), ones], axis=1)
    return jnp.asarray(after, BF), jnp.asarray(before, BF)


SB_ROWS = 256


def _hi_lo(x):
    hi = lax.bitcast_convert_type(lax.bitcast_convert_type(x, jnp.uint32) & jnp.uint32(0xFFFF0000), F32)
    return hi.astype(BF), (x - hi).astype(BF)


def _sums_r(x, t):
    hi, lo = _hi_lo(x)
    return _dot(hi, t) + _dot(lo, t)


def _attn_rows(ref, r0, rows, head0, scale=None):
    v = ref[pl.ds(r0, rows), :]
    if scale is not None:
        v = v * scale
    return jnp.concatenate([jnp.where(head0, v, 0.0), jnp.where(head0, 0.0, v)], axis=0).astype(BF)


def _attn_scores(q2, k_ref, kj, t2, masked):
    c0 = pl.multiple_of(kj * SB_BLOCK, SB_BLOCK)
    kb = k_ref[pl.ds(c0, SB_BLOCK), :].astype(BF)
    z = _dot_nt(q2, kb)
    l = jnp.log(1.0 + jnp.exp(-jnp.abs(z)))
    lsb = -(jnp.maximum(z, 0.0) + l)
    ls = jnp.minimum(z, 0.0) - l
    if masked:
        valid = (lax.broadcasted_iota(jnp.int32, z.shape, 1) + kj * SB_BLOCK) < t2
        lsb = jnp.where(valid, lsb, 0.0)
        ls = jnp.where(valid, ls, -jnp.inf)
    return kb, lsb, ls


def _attn_fwd(proj, after):
    S = proj.shape[0]
    TQ = min(SB_ROWS, S)
    kpb = TQ // SB_BLOCK
    scale = SB_HEAD_DIM ** -0.5

    def body(q_ref, k_ref, v_ref, after_ref, o_ref, tot_ref, run, acc):
        head0 = lax.broadcasted_iota(jnp.int32, (1, 2 * SB_HEAD_DIM), 1) < SB_HEAD_DIM

        def qloop(qi, _):
            r0 = pl.multiple_of(qi * TQ, TQ)
            q2 = _attn_rows(q_ref, r0, TQ, head0, scale)
            trow = lax.broadcasted_iota(jnp.int32, (TQ, SB_BLOCK), 0) + qi * TQ
            t2 = jnp.concatenate([trow, trow], axis=0)
            run[...] = jnp.zeros_like(run)
            acc[...] = jnp.zeros_like(acc)

            def scores(kj, masked):
                _, lsb, ls = _attn_scores(q2, k_ref, kj, t2, masked)
                return kj, ls, _sums_r(lsb, after_ref[...])

            def weigh(st):
                kj, ls, cs2 = st
                c0 = pl.multiple_of(kj * SB_BLOCK, SB_BLOCK)
                a = jnp.exp(ls + cs2[:, :SB_BLOCK] + run[...])
                acc[...] += _dot(a.astype(BF), v_ref[pl.ds(c0, SB_BLOCK), :].astype(BF))
                run[...] += cs2[:, SB_BLOCK:]

            nk = (qi + 1) * kpb
            st = scores(nk - 1, True)
            for d in range(1, kpb):
                nxt = scores(nk - 1 - d, True)
                weigh(st)
                st = nxt

            def below(n, st):
                nxt = scores(nk - kpb - 1 - n, False)
                weigh(st)
                return nxt

            weigh(lax.fori_loop(0, nk - kpb, below, st))
            o_ref[pl.ds(r0, TQ), :] = jnp.where(head0, acc[0:TQ, :], acc[TQ:2 * TQ, :])
            tot_ref[0, pl.ds(r0, TQ), :] = run[0:TQ, :]
            tot_ref[1, pl.ds(r0, TQ), :] = run[TQ:2 * TQ, :]
            return 0

        lax.fori_loop(0, S // TQ, qloop, 0)

    col = lambda base: pl.BlockSpec((S, 128), lambda p, base=base: (0, base + p))
    return pl.pallas_call(
        body, name="attn_fwd",
        grid=(WIDTH // 128,),
        in_specs=[col(0), col(4), col(8), pl.BlockSpec(after.shape, lambda p: (0, 0))],
        out_specs=[pl.BlockSpec((S, 128), lambda p: (0, p)), pl.BlockSpec((None, 2, S, 128), lambda p: (p, 0, 0, 0))],
        out_shape=[jax.ShapeDtypeStruct((S, WIDTH), F32), jax.ShapeDtypeStruct((WIDTH // 128, 2, S, 128), F32)],
        scratch_shapes=[pltpu.VMEM((2 * TQ, SB_BLOCK), F32), pltpu.VMEM((2 * TQ, 128), F32)],
        compiler_params=_params(dimension_semantics=("parallel",)),
    )(proj, proj, proj, after)


def _attn_bwd(proj, d_o, tot, after, before):
    S = proj.shape[0]
    TQ = min(SB_ROWS, S)
    kpb = TQ // SB_BLOCK
    scale = SB_HEAD_DIM ** -0.5

    def body(q_ref, k_ref, v_ref, do_ref, tot_ref, after_ref, before_ref, dq_ref, dk_ref, dv_ref, pre, cum, dq_acc, dk_acc, dv_acc):
        head0 = lax.broadcasted_iota(jnp.int32, (1, 2 * SB_HEAD_DIM), 1) < SB_HEAD_DIM
        dk_acc[...] = jnp.zeros_like(dk_acc)
        dv_acc[...] = jnp.zeros_like(dv_acc)

        def qloop(qi, _):
            r0 = pl.multiple_of(qi * TQ, TQ)
            q2 = _attn_rows(q_ref, r0, TQ, head0, scale)
            do2 = _attn_rows(do_ref, r0, TQ, head0)
            trow = lax.broadcasted_iota(jnp.int32, (TQ, SB_BLOCK), 0) + qi * TQ
            t2 = jnp.concatenate([trow, trow], axis=0)
            pre[0:TQ, :] = tot_ref[0, pl.ds(r0, TQ), :]
            pre[TQ:2 * TQ, :] = tot_ref[1, pl.ds(r0, TQ), :]
            cum[...] = jnp.zeros_like(cum)
            dq_acc[...] = jnp.zeros_like(dq_acc)

            def step(kj, masked):
                c0 = pl.multiple_of(kj * SB_BLOCK, SB_BLOCK)
                kb, lsb, ls = _attn_scores(q2, k_ref, kj, t2, masked)
                cs2 = _sums_r(lsb, after_ref[...])
                later = pre[...] - cs2[:, SB_BLOCK:]
                pre[...] = later
                a = jnp.exp(ls + cs2[:, :SB_BLOCK] + later)
                w = _dot_nt(do2, v_ref[pl.ds(c0, SB_BLOCK), :].astype(BF)) * a
                c2 = _sums_r(w, before_ref[...])
                sig = jnp.exp(ls)
                dz = w * (1.0 - sig) - (c2[:, :SB_BLOCK] + cum[...]) * sig
                cum[...] += c2[:, SB_BLOCK:]
                dzb = dz.astype(BF)
                dq_acc[...] += _dot(dzb, kb)
                dk_acc[pl.ds(c0, SB_BLOCK), :] += _dot_tn(dzb, q2)
                dv_acc[pl.ds(c0, SB_BLOCK), :] += _dot_tn(a.astype(BF), do2)

            nk = (qi + 1) * kpb

            def below(kj, _):
                step(kj, False)
                return 0

            lax.fori_loop(0, nk - kpb, below, 0)
            for d in range(kpb):
                step(nk - kpb + d, True)
            dq_ref[pl.ds(r0, TQ), :] = (jnp.where(head0, dq_acc[0:TQ, :], dq_acc[TQ:2 * TQ, :]) * scale).astype(BF)
            return 0

        lax.fori_loop(0, S // TQ, qloop, 0)
        dk_ref[...] = dk_acc[...].astype(BF)
        dv_ref[...] = dv_acc[...].astype(BF)

    col = lambda base: pl.BlockSpec((S, 128), lambda p, base=base: (0, base + p))
    whole = lambda a: pl.BlockSpec(a.shape, lambda p: (0, 0))
    out = jax.ShapeDtypeStruct((S, WIDTH), BF)
    big = pltpu.VMEM((2 * TQ, SB_BLOCK), F32)
    return pl.pallas_call(
        body, name="attn_bwd",
        grid=(WIDTH // 128,),
        in_specs=[col(0), col(4), col(8), col(0), pl.BlockSpec((None, 2, S, 128), lambda p: (p, 0, 0, 0)), whole(after), whole(before)],
        out_specs=[col(0), col(0), col(0)],
        out_shape=[out, out, out],
        scratch_shapes=[big, big, big, pltpu.VMEM((S, 128), F32), pltpu.VMEM((S, 128), F32)],
        compiler_params=_params(dimension_semantics=("parallel",)),
    )(proj, proj, proj, d_o, tot, after, before)


HG_LEVELS = (32, 16, 8, 4, 2, 1)


def _hgrn_consts():
    C = HG_CHUNK
    t = np.arange(C)[:, None]
    s = np.arange(C)[None, :]
    rows = [(s <= t), (s > t)]
    masks = [(t == s)]
    for m in HG_LEVELS:
        two = 2 * m
        mid = (t // two) * two + m
        right = (t % two) >= m
        rows.append(right & (s >= mid) & (s <= t))
        rows.append((~right) & (s > t) & (s <= mid - 1))
        masks.append(((t // two) == (s // two)) & right & ((s % two) < m))
    rows.append(np.ones((HG_DIM, C), bool))
    tri = np.concatenate(rows, axis=0).astype(np.float32)
    return (jnp.asarray(tri, BF), jnp.asarray(tri.T.copy(), BF), jnp.asarray(np.stack(masks).astype(np.float32), F32))


HG_SUM_ROWS = (HG_CHUNK,) * (2 + 2 * len(HG_LEVELS)) + (HG_DIM,)


@jax.custom_vjp
def _hgrn_sums(tri, tri_t, g):
    e = _split_dot(tri, g, 3)
    out, r = [], 0
    for n in HG_SUM_ROWS:
        out.append(e[r:r + n])
        r += n
    return tuple(out)


def _hgrn_sums_fwd(tri, tri_t, g):
    return _hgrn_sums(tri, tri_t, g), (tri, tri_t)


def _hgrn_sums_bwd(res, ds):
    tri, tri_t = res
    return jnp.zeros_like(tri), jnp.zeros_like(tri_t), _split_dot(tri_t, jnp.concatenate(ds, axis=0), 3)


_hgrn_sums.defvjp(_hgrn_sums_fwd, _hgrn_sums_bwd)


def _bf_dot(a, b):
    return _dot(a.astype(BF), b.astype(BF))


def _bf_dot_nt(a, b):
    return _dot_nt(a.astype(BF), b.astype(BF))


def _bf_dot_tn(a, b):
    return _dot_tn(a.astype(BF), b.astype(BF))


@jax.custom_vjp
def _mm(a, b):
    return _bf_dot(a, b)


_mm.defvjp(lambda a, b: (_bf_dot(a, b), (a, b)), lambda r, ct: (_bf_dot_nt(ct, r[1]), _bf_dot_tn(r[0], ct)))


@jax.custom_vjp
def _mm_nt(a, b):
    return _bf_dot_nt(a, b)


_mm_nt.defvjp(lambda a, b: (_bf_dot_nt(a, b), (a, b)), lambda r, ct: (_bf_dot(ct, r[1]), _bf_dot_tn(ct, r[0])))


@jax.custom_vjp
def _mm_tn(a, b):
    return _bf_dot_tn(a, b)


_mm_tn.defvjp(lambda a, b: (_bf_dot_tn(a, b), (a, b)), lambda r, ct: (_bf_dot_nt(r[1], ct), _bf_dot(r[0], ct)))


def _hgrn_chunk(tri, tri_t, masks, qraw, fpre, v, st, lb):
    q = _silu(qraw)
    f = lb + (1.0 - lb) * _sigmoid(fpre)
    k = 1.0 - f
    e = _hgrn_sums(tri, tri_t, jnp.log(f))
    prefix, suffix, whole = e[0], e[1], e[-1]
    scores = masks[0] * _mm_nt(q, k)
    for n in range(len(HG_LEVELS)):
        scores = scores + masks[n + 1] * _mm_nt(q * jnp.exp(e[2 + 2 * n]), k * jnp.exp(e[3 + 2 * n]))
    o = _mm_nt(q * jnp.exp(prefix), st) + _mm(scores, v)
    st_new = st * jnp.exp(whole) + _mm_tn(v, k * jnp.exp(suffix))
    return o, st_new


def _hgrn_specs(S, consts):
    col = lambda base: pl.BlockSpec((S, 128), lambda p, base=base: (0, base + p))
    whole = [pl.BlockSpec(a.shape, lambda p, n=a.ndim: (0,) * n) for a in consts]
    return col, whole


def _hgrn_fwd(proj, lbs, consts):
    S = proj.shape[0]
    nc = S // HG_CHUNK

    def body(q_ref, f_ref, i_ref, lb_ref, tri_ref, trit_ref, mask_ref, o_ref, st_ref):
        tri, tri_t = tri_ref[...], trit_ref[...]
        masks = [mask_ref[n] for n in range(len(HG_LEVELS) + 1)]
        lb = lb_ref[...]

        def chunk(ci, st):
            r0 = pl.multiple_of(ci * HG_CHUNK, HG_CHUNK)
            rows = pl.ds(r0, HG_CHUNK)
            st_ref[ci] = st
            o, st_new = _hgrn_chunk(tri, tri_t, masks, q_ref[rows, :], f_ref[rows, :], i_ref[rows, :], st, lb)
            o_ref[rows, :] = o
            return st_new

        lax.fori_loop(0, nc, chunk, jnp.zeros((HG_DIM, HG_DIM), F32))

    col, whole = _hgrn_specs(S, consts)
    return pl.pallas_call(
        body, name="hgrn_fwd",
        grid=(WIDTH // 128,),
        in_specs=[col(16), col(20), col(24), pl.BlockSpec((None, 1, 128), lambda p: (p, 0, 0))] + whole,
        out_specs=[col(0), pl.BlockSpec((None, nc, HG_DIM, HG_DIM), lambda p: (p, 0, 0, 0))],
        out_shape=[jax.ShapeDtypeStruct((S, WIDTH), F32), jax.ShapeDtypeStruct((WIDTH // 128, nc, HG_DIM, HG_DIM), F32)],
        compiler_params=_params(dimension_semantics=("parallel",)),
    )(proj, proj, proj, lbs, *consts)


def _hgrn_bwd(proj, lbs, states, d_o, consts):
    S = proj.shape[0]
    nc = S // HG_CHUNK

    def body(q_ref, f_ref, i_ref, lb_ref, st_ref, do_ref, tri_ref, trit_ref, mask_ref, dq_ref, df_ref, di_ref, dlb_ref):
        masks = [mask_ref[n] for n in range(len(HG_LEVELS) + 1)]
        lb = lb_ref[...]
        fn = functools.partial(_hgrn_chunk, tri_ref[...], trit_ref[...], masks)

        def chunk(n, carry):
            d_st, dlb = carry
            ci = nc - 1 - n
            r0 = pl.multiple_of(ci * HG_CHUNK, HG_CHUNK)
            rows = pl.ds(r0, HG_CHUNK)
            _, pull = jax.vjp(fn, q_ref[rows, :], f_ref[rows, :], i_ref[rows, :], st_ref[ci], lb)
            dq, df, di, d_prev, dl = pull((do_ref[rows, :], d_st))
            dq_ref[rows, :] = dq.astype(BF)
            df_ref[rows, :] = df.astype(BF)
            di_ref[rows, :] = di.astype(BF)
            return d_prev, dlb + dl

        _, dlb = lax.fori_loop(0, nc, chunk, (jnp.zeros((HG_DIM, HG_DIM), F32), jnp.zeros((1, HG_DIM), F32)))
        dlb_ref[...] = dlb

    col, whole = _hgrn_specs(S, consts)
    head = pl.BlockSpec((None, 1, 128), lambda p: (p, 0, 0))
    out = jax.ShapeDtypeStruct((S, WIDTH), BF)
    return pl.pallas_call(
        body, name="hgrn_bwd",
        grid=(WIDTH // 128,),
        in_specs=[col(16), col(20), col(24), head, pl.BlockSpec((None, nc, HG_DIM, HG_DIM), lambda p: (p, 0, 0, 0)), col(0)] + whole,
        out_specs=[col(0), col(0), col(0), head],
        out_shape=[out, out, out, jax.ShapeDtypeStruct((WIDTH // 128, 1, 128), F32)],
        compiler_params=_params(dimension_semantics=("parallel",)),
    )(proj, proj, proj, lbs, states, d_o, *consts)


def _shift_down(x, n):
    rows = lax.broadcasted_iota(jnp.int32, x.shape, 0)
    return jnp.where(rows >= n, pltpu.roll(x, n, 0), 0.0)


def _shift_up(x, n):
    S = x.shape[0]
    rows = lax.broadcasted_iota(jnp.int32, x.shape, 0)
    return jnp.where(rows < S - n, pltpu.roll(x, S - n, 0), 0.0)


def _branch_fwd(proj, o_a, o_b, norm_w, conv_w, layer):
    S = proj.shape[0]

    def body(oa_ref, za_ref, ob_ref, zb_ref, nw_ref, pre_ref, post_ref, u_ref, zc_ref, cw_ref, ya_ref, yb_ref, yc_ref):
        ya_ref[...] = (oa_ref[...] * _silu(za_ref[...])).astype(BF)
        ob = ob_ref[...]
        rn = lax.rsqrt(jnp.mean(ob * ob, axis=-1, keepdims=True) + RMS_EPS)
        yb_ref[...] = (ob * rn * nw_ref[layer:layer + 1, :] * _silu(zb_ref[...])).astype(BF)
        pu = pre_ref[...] * u_ref[...]
        conv = cw_ref[2:3, :] * pu + cw_ref[1:2, :] * _shift_down(pu, 1) + cw_ref[0:1, :] * _shift_down(pu, 2)
        yc_ref[...] = (post_ref[...] * conv * _silu(zc_ref[...])).astype(BF)

    col = lambda base: pl.BlockSpec((S, 128), lambda p, base=base: (0, base + p))
    out = jax.ShapeDtypeStruct((S, WIDTH), BF)
    return pl.pallas_call(
        body, name="branch_fwd",
        grid=(WIDTH // 128,),
        in_specs=[col(0), col(12), col(0), col(28), pl.BlockSpec(norm_w.shape, lambda p: (0, 0)),
                  col(32), col(36), col(40), col(44), pl.BlockSpec((None, None, 3, 128), lambda p: (p, layer, 0, 0))],
        out_specs=[col(0), col(0), col(0)],
        out_shape=[out, out, out],
        compiler_params=_params(dimension_semantics=("parallel",)),
    )(o_a, proj, o_b, proj, norm_w, proj, proj, proj, proj, conv_w)


def _branch_bwd(proj, o_a, o_b, norm_w, conv_w, dy_a, dy_b, dy_c, layer):
    S = proj.shape[0]

    def dsilu(z):
        s = _sigmoid(z)
        return s * z, s * (1.0 + z * (1.0 - s))

    def body(oa_ref, za_ref, ob_ref, zb_ref, nw_ref, pre_ref, post_ref, u_ref, zc_ref, cw_ref, dya_ref, dyb_ref, dyc_ref,
             doa_ref, dob_ref, dza_ref, dzb_ref, dpre_ref, dpost_ref, du_ref, dzc_ref, dnw_ref, dcw_ref):
        dya = dya_ref[...]
        sa, dsa = dsilu(za_ref[...])
        doa_ref[...] = dya * sa
        dza_ref[...] = (dya * oa_ref[...] * dsa).astype(BF)

        dyb = dyb_ref[...]
        ob = ob_ref[...]
        nw = nw_ref[layer:layer + 1, :]
        sb, dsb = dsilu(zb_ref[...])
        rn = lax.rsqrt(jnp.mean(ob * ob, axis=-1, keepdims=True) + RMS_EPS)
        on = ob * rn
        dzb_ref[...] = (dyb * on * nw * dsb).astype(BF)
        don_w = dyb * sb
        dnw_ref[...] = jnp.sum(don_w * on, axis=0, keepdims=True)
        don = don_w * nw
        dob_ref[...] = rn * (don - on * jnp.mean(don * on, axis=-1, keepdims=True))

        dyc = dyc_ref[...]
        pre, post, u = pre_ref[...], post_ref[...], u_ref[...]
        sc, dsc = dsilu(zc_ref[...])
        pu = pre * u
        pu1, pu2 = _shift_down(pu, 1), _shift_down(pu, 2)
        conv = cw_ref[2:3, :] * pu + cw_ref[1:2, :] * pu1 + cw_ref[0:1, :] * pu2
        dzc_ref[...] = (dyc * post * conv * dsc).astype(BF)
        dpost_ref[...] = (dyc * conv * sc).astype(BF)
        dconv = dyc * post * sc
        dcw_ref[0:1, :] = jnp.sum(dconv * pu2, axis=0, keepdims=True)
        dcw_ref[1:2, :] = jnp.sum(dconv * pu1, axis=0, keepdims=True)
        dcw_ref[2:3, :] = jnp.sum(dconv * pu, axis=0, keepdims=True)
        dpu = cw_ref[2:3, :] * dconv + cw_ref[1:2, :] * _shift_up(dconv, 1) + cw_ref[0:1, :] * _shift_up(dconv, 2)
        dpre_ref[...] = (dpu * u).astype(BF)
        du_ref[...] = (dpu * pre).astype(BF)

    col = lambda base: pl.BlockSpec((S, 128), lambda p, base=base: (0, base + p))
    f32 = jax.ShapeDtypeStruct((S, WIDTH), F32)
    bf = jax.ShapeDtypeStruct((S, WIDTH), BF)
    return pl.pallas_call(
        body, name="branch_bwd",
        grid=(WIDTH // 128,),
        in_specs=[col(0), col(12), col(0), col(28), pl.BlockSpec(norm_w.shape, lambda p: (0, 0)),
                  col(32), col(36), col(40), col(44), pl.BlockSpec((None, None, 3, 128), lambda p: (p, layer, 0, 0)),
                  col(0), col(0), col(0)],
        out_specs=[col(0)] * 8 + [pl.BlockSpec((None, 1, 128), lambda p: (p, 0, 0)), pl.BlockSpec((None, 3, 128), lambda p: (p, 0, 0))],
        out_shape=[f32, f32, bf, bf, bf, bf, bf, bf, jax.ShapeDtypeStruct((WIDTH // 128, 1, 128), F32),
                   jax.ShapeDtypeStruct((WIDTH // 128, 3, 128), F32)],
        compiler_params=_params(dimension_semantics=("parallel",)),
    )(o_a, proj, o_b, proj, norm_w, proj, proj, proj, proj, conv_w, dy_a, dy_b, dy_c)


def _branch_proj(y_refs, wb_ref, layer):
    out = []
    for i in range(3):
        yv = y_refs[i][...]
        out.append(jnp.concatenate([_dot(yv, wb_ref[j, layer, i]) for j in range(N_CHIPS)], axis=1))
    return out


def _merge_fwd(x, mod, proj, ys, wb, wo, ln_g, ln_b, layer, tm=256):
    S, D = x.shape
    tm = min(tm, S)

    def body(x_ref, mod_ref, ga_ref, gb_ref, gc_ref, ya_ref, yb_ref, yc_ref, wb_ref, wo_ref, g_ref, b_ref, xo_ref, mg_ref, y_ref):
        ps = _branch_proj((ya_ref, yb_ref, yc_ref), wb_ref, layer)
        merged = _sigmoid(ga_ref[...]) * ps[0] + _sigmoid(gb_ref[...]) * ps[1] + _sigmoid(gc_ref[...]) * ps[2]
        mb = merged.astype(BF)
        mg_ref[...] = mb
        y = _dot(mb, wo_ref[...].reshape(D, D))
        y_ref[...] = y
        r = ALPHA * x_ref[...] + (1.0 + mod_ref[:, 2 * D:3 * D]) * y
        xn, _ = _standardize(r)
        xo_ref[...] = xn * g_ref[layer:layer + 1, :] + b_ref[layer:layer + 1, :]

    row = lambda w, c=0: pl.BlockSpec((tm, w), lambda i, c=c: (i, c))
    whole = lambda a: pl.BlockSpec(a.shape, lambda i, n=a.ndim: (0,) * n)
    return pl.pallas_call(
        body, name="merge_fwd",
        grid=(S // tm,),
        in_specs=[row(D), whole(mod), row(D, 6), row(D, 7), row(D, 8), row(WIDTH), row(WIDTH), row(WIDTH), whole(wb),
                  pl.BlockSpec((N_CHIPS, None, D // N_CHIPS, D), lambda i: (0, layer, 0, 0)), whole(ln_g), whole(ln_b)],
        out_specs=[row(D), row(D), row(D)],
        out_shape=[jax.ShapeDtypeStruct((S, D), F32), jax.ShapeDtypeStruct((S, D), BF), jax.ShapeDtypeStruct((S, D), F32)],
        compiler_params=_params(dimension_semantics=("parallel",)),
    )(x, mod, proj, proj, proj, *ys, wb, wo, ln_g, ln_b)


def _merge_bwd(dxo, x, y, mod, proj, ys, wb, wo, ln_g, layer, tm=256):
    S, D = x.shape
    tm = min(tm, S)

    def body(dxo_ref, x_ref, y_ref, mod_ref, ga_ref, gb_ref, gc_ref, ya_ref, yb_ref, yc_ref, wb_ref, wo_ref, g_ref,
             dxr_ref, dy_ref, dp_ref, dg_ref, dya_ref, dyb_ref, dyc_ref, dlg_ref, dlb_ref, dgt_ref):
        @pl.when(pl.program_id(0) == 0)
        def _():
            dlg_ref[...] = jnp.zeros_like(dlg_ref)
            dlb_ref[...] = jnp.zeros_like(dlb_ref)
            dgt_ref[...] = jnp.zeros_like(dgt_ref)

        gate1 = 1.0 + mod_ref[:, 2 * D:3 * D]
        yv = y_ref[...]
        xn, rstd = _standardize(ALPHA * x_ref[...] + gate1 * yv)
        dxo = dxo_ref[...]
        dlg_ref[...] += jnp.sum(dxo * xn, axis=0, keepdims=True)
        dlb_ref[...] += jnp.sum(dxo, axis=0, keepdims=True)
        dr = _standardize_bwd(dxo * g_ref[layer:layer + 1, :], xn, rstd)
        dxr_ref[...] = ALPHA * dr
        dgt_ref[...] += jnp.sum(dr * yv, axis=0, keepdims=True)
        dyb = (gate1 * dr).astype(BF)
        dy_ref[...] = dyb
        dmerged = _dot_nt(dyb, wo_ref[...].reshape(D, D))
        ps = _branch_proj((ya_ref, yb_ref, yc_ref), wb_ref, layer)
        quarter = D // N_CHIPS
        for i, (gate_ref, out_ref) in enumerate(((ga_ref, dya_ref), (gb_ref, dyb_ref), (gc_ref, dyc_ref))):
            sg = _sigmoid(gate_ref[...])
            dg_ref[:, i * D:(i + 1) * D] = (dmerged * ps[i] * sg * (1.0 - sg)).astype(BF)
            dp = (dmerged * sg).astype(BF)
            dp_ref[:, i * D:(i + 1) * D] = dp
            acc = None
            for j in range(N_CHIPS):
                t = _dot_nt(dp[:, j * quarter:(j + 1) * quarter], wb_ref[j, layer, i])
                acc = t if acc is None else acc + t
            out_ref[...] = acc

    row = lambda w, c=0: pl.BlockSpec((tm, w), lambda i, c=c: (i, c))
    whole = lambda a: pl.BlockSpec(a.shape, lambda i, n=a.ndim: (0,) * n)
    vec = pl.BlockSpec((1, D), lambda i: (0, 0))
    sd = jax.ShapeDtypeStruct
    return pl.pallas_call(
        body, name="merge_bwd",
        grid=(S // tm,),
        in_specs=[row(D), row(D), row(D), whole(mod), row(D, 6), row(D, 7), row(D, 8), row(WIDTH), row(WIDTH), row(WIDTH), whole(wb),
                  pl.BlockSpec((N_CHIPS, None, D // N_CHIPS, D), lambda i: (0, layer, 0, 0)), whole(ln_g)],
        out_specs=[row(D), row(D), row(3 * D), row(3 * D), row(WIDTH), row(WIDTH), row(WIDTH), vec, vec, vec],
        out_shape=[sd((S, D), F32), sd((S, D), BF), sd((S, 3 * D), BF), sd((S, 3 * D), BF), sd((S, WIDTH), F32), sd((S, WIDTH), F32),
                   sd((S, WIDTH), F32), sd((1, D), F32), sd((1, D), F32), sd((1, D), F32)],
        compiler_params=_params(dimension_semantics=("arbitrary",)),
    )(dxo, x, y, mod, proj, proj, proj, *ys, wb, wo, ln_g)


def _loss_head(x, target, tm=512):
    S, D = x.shape
    tm = min(tm, S)

    def body(x_ref, t_ref, dx_ref, loss_ref):
        @pl.when(pl.program_id(0) == 0)
        def _():
            loss_ref[...] = jnp.zeros_like(loss_ref)

        err = x_ref[...] - t_ref[...]
        dx_ref[...] = err * (1.0 / D)
        loss_ref[...] += 0.5 * jnp.sum(jnp.mean(err * err, axis=-1, keepdims=True))

    row = pl.BlockSpec((tm, D), lambda i: (i, 0))
    return pl.pallas_call(
        body, name="loss_head",
        grid=(S // tm,),
        in_specs=[row, row],
        out_specs=[row, pl.BlockSpec((8, 128), lambda i: (0, 0))],
        out_shape=[jax.ShapeDtypeStruct((S, D), F32), jax.ShapeDtypeStruct((8, 128), F32)],
        compiler_params=_params(dimension_semantics=("arbitrary",)),
    )(x, target)


def _proj_bwd(dproj, wg, x, mod, dx_res, layer, tm=512, tk=768):
    S, D = x.shape
    tm = min(tm, S)
    shard = wg.shape[-1]
    per = shard // tk
    nk = IN_COLS // tk

    def body(dp_ref, w_ref, x_ref, mod_ref, dxr_ref, dx_ref, dsh_ref, dsc_ref, acc):
        i, k = pl.program_id(0), pl.program_id(1)

        @pl.when((i == 0) & (k == 0))
        def _():
            dsh_ref[...] = jnp.zeros_like(dsh_ref)
            dsc_ref[...] = jnp.zeros_like(dsc_ref)

        @pl.when(k == 0)
        def _():
            acc[...] = jnp.zeros_like(acc)

        acc[...] += _dot_nt(dp_ref[...], w_ref[...])

        @pl.when(k == nk - 1)
        def _():
            dh = acc[...]
            xs, rstd = _standardize(x_ref[...])
            dsh_ref[...] += jnp.sum(dh, axis=0, keepdims=True)
            dsc_ref[...] += jnp.sum(dh * xs, axis=0, keepdims=True)
            dx_ref[...] = _standardize_bwd(dh * (1.0 + mod_ref[:, D:2 * D]), xs, rstd) + dxr_ref[...]

    row = pl.BlockSpec((tm, D), lambda i, k: (i, 0))
    vec = pl.BlockSpec((1, D), lambda i, k: (0, 0))
    return pl.pallas_call(
        body, name="proj_bwd",
        grid=(S // tm, nk),
        in_specs=[pl.BlockSpec((tm, tk), lambda i, k: (i, k)),
                  pl.BlockSpec((None, None, D, tk), lambda i, k: (k // per, layer, 0, k % per)),
                  row, pl.BlockSpec((1, 3 * D), lambda i, k: (0, 0)), row],
        out_specs=[row, vec, vec],
        out_shape=[jax.ShapeDtypeStruct((S, D), F32), jax.ShapeDtypeStruct((1, D), F32), jax.ShapeDtypeStruct((1, D), F32)],
        scratch_shapes=[pltpu.VMEM((tm, D), F32)],
        compiler_params=_params(dimension_semantics=("arbitrary", "arbitrary")),
    )(dproj, wg, x, mod, dx_res)


def _grad_w(a, b, mid, prev, name, tm, tn, b_col0=0):
    S, M = a.shape
    shape = prev.shape
    n_shard = shape[-1]
    per = n_shard // tn
    nm = M // tm
    lead = len(mid)

    def body(*refs):
        a_ref, b_ref, o_ref = refs[0], refs[1], refs[-1]
        o_ref[...] = _dot_tn(a_ref[...], b_ref[...]).astype(BF)

    in_specs = [pl.BlockSpec((S, tm), lambda m, n: (0, m)),
                pl.BlockSpec((S, tn), lambda m, n: (0, b_col0 // tn + n))]
    args = [a, b]
    aliases = {}
    if not isinstance(prev, jax.ShapeDtypeStruct):
        in_specs.append(pl.BlockSpec(memory_space=pl.ANY))
        args.append(prev)
        aliases = {2: 0}
    return pl.pallas_call(
        body, name=name,
        grid=(nm, N_CHIPS * per),
        in_specs=in_specs,
        out_specs=pl.BlockSpec((None,) + (None,) * lead + (tm, tn), lambda m, n: (n // per,) + tuple(mid) + (m, n % per)),
        out_shape=jax.ShapeDtypeStruct(shape, BF),
        input_output_aliases=aliases,
        compiler_params=_params(dimension_semantics=("parallel", "parallel")),
    )(*args)


def _flip(v, bit):
    return 1 - v if bit else v


def _all_gather8(x, name):
    R, N = x.shape

    def body(x_ref, out_ref, send_sems, recv_sems):
        mx, my, mc = lax.axis_index("x"), lax.axis_index("y"), lax.axis_index("c")
        me = 4 * mx + 2 * my + mc
        out_ref[me] = x_ref[...]
        copies = []
        for k in range(1, N_DEV):
            peer = (_flip(mx, k & 4), _flip(my, k & 2), _flip(mc, k & 1))
            cp = pltpu.make_async_remote_copy(src_ref=x_ref, dst_ref=out_ref.at[me], send_sem=send_sems.at[k - 1],
                                              recv_sem=recv_sems.at[k - 1], device_id=peer, device_id_type=MESH)
            cp.start()
            copies.append(cp)
        for cp in copies:
            cp.wait()

    return pl.pallas_call(
        body, name=name,
        in_specs=[pl.BlockSpec(memory_space=pltpu.VMEM)],
        out_specs=pl.BlockSpec(memory_space=pltpu.VMEM),
        out_shape=jax.ShapeDtypeStruct((N_DEV, R, N), F32),
        scratch_shapes=[pltpu.SemaphoreType.DMA((N_DEV - 1,)), pltpu.SemaphoreType.DMA((N_DEV - 1,))],
        compiler_params=_params(),
    )(x)


def _gather_weights(shards):
    n = len(shards)

    def body(*refs):
        ins, outs = refs[:n], refs[n:2 * n]
        local_sems, send_sems, recv_sems = refs[2 * n:]
        mx, my, mc = lax.axis_index("x"), lax.axis_index("y"), lax.axis_index("c")
        mine = 2 * mx + my
        copies = []
        for a in range(n):
            cp = pltpu.make_async_copy(ins[a], outs[a].at[mine], local_sems.at[a])
            cp.start()
            copies.append(cp)
            for k in range(1, N_CHIPS):
                peer = (_flip(mx, k & 2), _flip(my, k & 1), mc)
                cp = pltpu.make_async_remote_copy(src_ref=ins[a], dst_ref=outs[a].at[mine], send_sem=send_sems.at[a, k - 1],
                                                  recv_sem=recv_sems.at[a, k - 1], device_id=peer, device_id_type=MESH)
                cp.start()
                copies.append(cp)
        for cp in copies:
            cp.wait()

    hbm = pl.BlockSpec(memory_space=pl.ANY)
    return pl.pallas_call(
        body, name="gather_weights",
        in_specs=[hbm] * n,
        out_specs=[hbm] * n,
        out_shape=[jax.ShapeDtypeStruct((N_CHIPS,) + s.shape, s.dtype) for s in shards],
        scratch_shapes=[pltpu.SemaphoreType.DMA((n,)), pltpu.SemaphoreType.DMA((n, N_CHIPS - 1)), pltpu.SemaphoreType.DMA((n, N_CHIPS - 1))],
        compiler_params=_params(),
    )(*shards)


def _scatter_grads(grads):
    n = len(grads)

    def body(*refs):
        ins, owns, gots = refs[:n], refs[n:2 * n], refs[2 * n:3 * n]
        local_sems, send_sems, recv_sems = refs[3 * n:]
        mx, my, mc = lax.axis_index("x"), lax.axis_index("y"), lax.axis_index("c")
        mine = 2 * mx + my
        copies = []
        for a in range(n):
            cp = pltpu.make_async_copy(ins[a].at[mine], owns[a], local_sems.at[a])
            cp.start()
            copies.append(cp)
            for k in range(1, N_CHIPS):
                px, py = _flip(mx, k & 2), _flip(my, k & 1)
                cp = pltpu.make_async_remote_copy(src_ref=ins[a].at[2 * px + py], dst_ref=gots[a].at[k - 1], send_sem=send_sems.at[a, k - 1],
                                                  recv_sem=recv_sems.at[a, k - 1], device_id=(px, py, mc), device_id_type=MESH)
                cp.start()
                copies.append(cp)
        for cp in copies:
            cp.wait()

    hbm = pl.BlockSpec(memory_space=pl.ANY)
    return pl.pallas_call(
        body, name="scatter_grads",
        in_specs=[hbm] * n,
        out_specs=[hbm] * (2 * n),
        out_shape=[jax.ShapeDtypeStruct(g.shape[1:], g.dtype) for g in grads]
        + [jax.ShapeDtypeStruct((N_CHIPS - 1,) + g.shape[1:], g.dtype) for g in grads],
        scratch_shapes=[pltpu.SemaphoreType.DMA((n,)), pltpu.SemaphoreType.DMA((n, N_CHIPS - 1)), pltpu.SemaphoreType.DMA((n, N_CHIPS - 1))],
        compiler_params=_params(),
    )(*grads)


def _swap_sibling(parts):
    n = len(parts)

    def body(*refs):
        ins, outs = refs[:n], refs[n:2 * n]
        send_sems, recv_sems = refs[2 * n:]
        peer = (lax.axis_index("x"), lax.axis_index("y"), 1 - lax.axis_index("c"))
        copies = []
        for a in range(n):
            cp = pltpu.make_async_remote_copy(src_ref=ins[a], dst_ref=outs[a], send_sem=send_sems.at[a], recv_sem=recv_sems.at[a],
                                              device_id=peer, device_id_type=MESH)
            cp.start()
            copies.append(cp)
        for cp in copies:
            cp.wait()

    hbm = pl.BlockSpec(memory_space=pl.ANY)
    return pl.pallas_call(
        body, name="swap_sibling",
        in_specs=[hbm] * n,
        out_specs=[hbm] * n,
        out_shape=[jax.ShapeDtypeStruct(p.shape, p.dtype) for p in parts],
        scratch_shapes=[pltpu.SemaphoreType.DMA((n,)), pltpu.SemaphoreType.DMA((n,))],
        compiler_params=_params(),
    )(*parts)


def _rows2d(a):
    return a.reshape(-1, a.shape[-1])


def _tile_rows(rows, cols, n_arrays):
    budget = (24 << 20) // (n_arrays * 2 * 4 * cols)
    if rows <= budget:
        return rows
    tm = 8
    for cand in range(8, budget + 1, 8):
        if rows % cand == 0:
            tm = cand
    return tm


def _sum4(own, got):
    o2 = _rows2d(own)
    rows, cols = o2.shape
    g3 = got.reshape(3, rows, cols)
    tm = _tile_rows(rows, cols, 3)

    def body(o_ref, g_ref, out_ref):
        out_ref[...] = ((o_ref[...].astype(F32) + g_ref[0].astype(F32)) + g_ref[1].astype(F32)) + g_ref[2].astype(F32)

    out = pl.pallas_call(
        body, name="sum_partials",
        grid=(rows // tm,),
        in_specs=[pl.BlockSpec((tm, cols), lambda i: (i, 0)), pl.BlockSpec((3, tm, cols), lambda i: (0, i, 0))],
        out_specs=pl.BlockSpec((tm, cols), lambda i: (i, 0)),
        out_shape=jax.ShapeDtypeStruct((rows, cols), F32),
        compiler_params=_params(dimension_semantics=("parallel",)),
    )(o2, g3)
    return out.reshape(own.shape)


def _adamw(w, m, v, g_parts, name):
    shape = w.shape
    w2, m2, v2 = _rows2d(w), _rows2d(m), _rows2d(v)
    gs = [_rows2d(g) for g in g_parts]
    rows, cols = w2.shape
    n = len(gs)
    tm = _tile_rows(rows, cols, 7 + n)
    c1 = 1.0 / (1.0 - ADAM_B1 ** ADAM_STEP)
    c2 = 1.0 / (1.0 - ADAM_B2 ** ADAM_STEP)

    def body(*refs):
        w_ref, m_ref, v_ref = refs[:3]
        g_refs = refs[3:3 + n]
        go_ref, d_ref, mo_ref, vo_ref = refs[3 + n:]
        g = g_refs[0][...]
        for r in g_refs[1:]:
            g = g + r[...]
        mn = ADAM_B1 * m_ref[...] + (1.0 - ADAM_B1) * g
        vn = ADAM_B2 * v_ref[...] + (1.0 - ADAM_B2) * (g * g)
        go_ref[...] = g
        mo_ref[...] = mn
        vo_ref[...] = vn
        d_ref[...] = -ADAM_LR * ((mn * c1) / (jnp.sqrt(vn * c2) + ADAM_EPS) + ADAM_WD * w_ref[...])

    spec = pl.BlockSpec((tm, cols), lambda i: (i, 0))
    outs = pl.pallas_call(
        body, name=name,
        grid=(rows // tm,),
        in_specs=[spec] * (3 + n),
        out_specs=[spec] * 4,
        out_shape=[jax.ShapeDtypeStruct((rows, cols), F32)] * 4,
        compiler_params=_params(dimension_semantics=("parallel",)),
    )(w2, m2, v2, *gs)
    return [o.reshape(shape) for o in outs]


def _lower_bounds(r0, r1):
    top = jnp.maximum(r0, r1)
    e0, e1 = jnp.exp(r0 - top), jnp.exp(r1 - top)
    p0, p1 = e0 / (e0 + e1), e1 / (e0 + e1)
    return p0 - p0, (p0 + p1) - p0


def _lbs_fwd(lb_raw):
    def body(lb_ref, out_ref):
        l0, l1 = _lower_bounds(lb_ref[0:1, :], lb_ref[1:2, :])
        out_ref[0:1, :] = l0
        out_ref[1:2, :] = l1

    return pl.pallas_call(body, name="lower_bounds", out_shape=jax.ShapeDtypeStruct(lb_raw.shape, F32), compiler_params=_params())(lb_raw)


def _mod_rows(c_all, w_mod, tn=768):
    _, D, cols = w_mod.shape

    def body(c_ref, w_ref, out_ref):
        out_ref[...] = _dot(c_ref[...].astype(BF), w_ref[...].astype(BF))

    return pl.pallas_call(
        body, name="mod_rows",
        grid=(DEPTH,),
        in_specs=[pl.BlockSpec((N_DEV, D), lambda l: (0, 0)), pl.BlockSpec((None, D, cols), lambda l: (l, 0, 0))],
        out_specs=pl.BlockSpec((N_DEV, cols), lambda l: (0, l)),
        out_shape=jax.ShapeDtypeStruct((N_DEV, DEPTH * cols), F32),
        compiler_params=_params(dimension_semantics=("parallel",)),
    )(c_all, w_mod)


def _grad_w_mod(c_all, dmod_cols):
    D = c_all.shape[1]
    cols = dmod_cols.shape[-1]

    def body(c_ref, d_ref, out_ref):
        out_ref[...] = _dot_tn(c_ref[...].astype(BF), d_ref[...].astype(BF))

    return pl.pallas_call(
        body, name="grad_w_mod",
        grid=(DEPTH,),
        in_specs=[pl.BlockSpec((N_DEV, D), lambda l: (0, 0)), pl.BlockSpec((None, N_DEV, cols), lambda l: (l, 0, 0))],
        out_specs=pl.BlockSpec((None, D, cols), lambda l: (l, 0, 0)),
        out_shape=jax.ShapeDtypeStruct((DEPTH, D, cols), F32),
        compiler_params=_params(dimension_semantics=("parallel",)),
    )(c_all, dmod_cols)


def _sum_devices(parts, lb_raw, n_lb):
    _, R, N = parts.shape

    def body(p_ref, lb_ref, out_ref, dlb_ref):
        acc = p_ref[0]
        for d in range(1, N_DEV):
            acc = acc + p_ref[d]
        out_ref[...] = acc
        dl = tuple(acc[0:1, n_lb + l * WIDTH:n_lb + (l + 1) * WIDTH] for l in range(DEPTH))
        _, pull = jax.vjp(_lower_bounds, lb_ref[0:1, :], lb_ref[1:2, :])
        d0, d1 = pull(dl)
        dlb_ref[0:1, :] = d0
        dlb_ref[1:2, :] = d1

    return pl.pallas_call(
        body, name="sum_devices",
        out_shape=[jax.ShapeDtypeStruct((R, N), F32), jax.ShapeDtypeStruct(lb_raw.shape, F32)],
        compiler_params=_params(),
    )(parts, lb_raw)


def _pad_rows(a, rows=8):
    return jnp.concatenate([a, jnp.zeros((rows - a.shape[0],) + a.shape[1:], a.dtype)], axis=0)


def kernel(x, c, w_mod, b_mod, w_in, conv_w, hgrn_norm_w, lower_bounds, w_branch, w_out, ln_g, ln_b, loss_target, m_w_mod, m_b_mod, m_w_in, m_conv_w, m_hgrn_norm_w, m_lower_bounds, m_w_branch, m_w_out, m_ln_g, m_ln_b, v_w_mod, v_b_mod, v_w_in, v_conv_w, v_hgrn_norm_w, v_lower_bounds, v_w_branch, v_w_out, v_ln_g, v_ln_b):
    D = D_MODEL
    x0 = x[0]
    target = loss_target[0]
    S = x0.shape[0]
    mx, my, mc = lax.axis_index("x"), lax.axis_index("y"), lax.axis_index("c")
    chip = 2 * mx + my
    me = 2 * chip + mc
    mod_cols = 3 * D // N_CHIPS

    wg_in, wg_br, wg_out = _gather_weights([w_in.astype(BF), w_branch.astype(BF), w_out.astype(BF)])

    n_conv = DEPTH * 3 * (WIDTH // N_CHIPS)
    first = jnp.concatenate([c, conv_w.reshape(1, n_conv)], axis=1)
    first = _all_gather8(_pad_rows(first), "gather_c")
    c_all = first[:, 0, :D]
    conv_all = first[:, 0, D:].reshape(N_DEV, DEPTH, 3, WIDTH // N_CHIPS)[0::2]
    mod_part = _all_gather8(_mod_rows(c_all, w_mod), "gather_mod")[0::2]
    mod_part = lax.dynamic_index_in_dim(mod_part, me, axis=1, keepdims=False).reshape(N_CHIPS, DEPTH, mod_cols)
    mods = [(mod_part[:, l].reshape(1, 3 * D) + b_mod[l][None, :]) for l in range(DEPTH)]
    lbs = _lbs_fwd(lower_bounds).reshape(DEPTH, WIDTH // 128, 1, 128)
    loss_blk, dx, big, small = _local_step(x0, target, mods, lbs, wg_in, wg_br, wg_out, conv_all, hgrn_norm_w, ln_g, ln_b)

    scattered = _scatter_grads(big)
    partial = [_sum4(scattered[a], scattered[len(big) + a]) for a in range(len(big))]
    sibling = _swap_sibling(partial)

    n_mod, n_nw, n_lb, n_ln, n_cw = DEPTH * 3 * D, DEPTH * 128, DEPTH * WIDTH, DEPTH * D, DEPTH * 3 * WIDTH
    row = jnp.concatenate(
        [jnp.concatenate([small[l][0], small[l][1], small[l][2]], axis=1) for l in range(DEPTH)]
        + [jnp.sum(small[l][3], axis=0) for l in range(DEPTH)]
        + [small[l][4].reshape(1, WIDTH) for l in range(DEPTH)]
        + [small[l][5] for l in range(DEPTH)] + [small[l][6] for l in range(DEPTH)]
        + [jnp.transpose(small[l][7], (1, 0, 2)).reshape(1, 3 * WIDTH) for l in range(DEPTH)]
        + [loss_blk[0:1, :]], axis=1)
    gathered = _all_gather8(_pad_rows(row), "gather_small")
    off_nw = n_mod
    off_lb = off_nw + n_nw
    off_lng = off_lb + n_lb
    off_lnb = off_lng + n_ln
    off_cw = off_lnb + n_ln
    off_loss = off_cw + n_cw
    total, d_lower = _sum_devices(gathered, lower_bounds, off_lb)
    loss = total[0, off_loss]
    d_b_mod = total[0, :n_mod].reshape(DEPTH, 3 * D)
    d_norm_w = total[0, off_nw:off_lb].reshape(DEPTH, 128)
    d_ln_g = total[0, off_lng:off_lnb].reshape(DEPTH, D)
    d_ln_b = total[0, off_lnb:off_cw].reshape(DEPTH, D)
    d_conv = total[0, off_cw:off_loss].reshape(DEPTH, 3, N_CHIPS, WIDTH // N_CHIPS)
    d_conv = lax.dynamic_index_in_dim(d_conv, chip, axis=2, keepdims=False)
    dmod_all = gathered[:, 0, :n_mod].reshape(N_DEV, DEPTH, N_CHIPS, mod_cols)
    dmod_cols = jnp.transpose(lax.dynamic_index_in_dim(dmod_all, chip, axis=2, keepdims=False), (1, 0, 2))
    d_w_mod = _grad_w_mod(c_all, dmod_cols)

    res = {}
    res["w_mod"] = _adamw(w_mod, m_w_mod, v_w_mod, [d_w_mod], "adamw_w_mod")
    res["b_mod"] = _adamw(b_mod, m_b_mod, v_b_mod, [d_b_mod], "adamw_b_mod")
    res["w_in"] = _adamw(w_in, m_w_in, v_w_in, [partial[0], sibling[0]], "adamw_w_in")
    res["conv_w"] = _adamw(conv_w, m_conv_w, v_conv_w, [d_conv], "adamw_conv_w")
    res["hgrn_norm_w"] = _adamw(hgrn_norm_w, m_hgrn_norm_w, v_hgrn_norm_w, [d_norm_w], "adamw_norm_w")
    res["lower_bounds"] = _adamw(lower_bounds, m_lower_bounds, v_lower_bounds, [d_lower], "adamw_lower_bounds")
    res["w_branch"] = _adamw(w_branch, m_w_branch, v_w_branch, [partial[1], sibling[1]], "adamw_w_branch")
    res["w_out"] = _adamw(w_out, m_w_out, v_w_out, [partial[2], sibling[2]], "adamw_w_out")
    res["ln_g"] = _adamw(ln_g, m_ln_g, v_ln_g, [d_ln_g], "adamw_ln_g")
    res["ln_b"] = _adamw(ln_b, m_ln_b, v_ln_b, [d_ln_b], "adamw_ln_b")
    names = ["w_mod", "b_mod", "w_in", "conv_w", "hgrn_norm_w", "lower_bounds", "w_branch", "w_out", "ln_g", "ln_b"]
    return (loss, dx[None], *[res[n][0] for n in names], *[res[n][1] for n in names],
            *[res[n][2] for n in names], *[res[n][3] for n in names])


def _local_step(x0, target, mods, lbs, wg_in, wg_br, wg_out, conv_all, hgrn_norm_w, ln_g, ln_b):
    D = D_MODEL
    after, before = _attn_consts()
    hg_consts = _hgrn_consts()

    saved = []
    xl = x0
    for l in range(DEPTH):
        proj, h = _proj_fwd(xl, mods[l], wg_in, l)
        o_a, tot = _attn_fwd(proj, after)
        o_b, states = _hgrn_fwd(proj, lbs[l], hg_consts)
        ys = _branch_fwd(proj, o_a, o_b, hgrn_norm_w, conv_all, l)
        x_next, merged, y = _merge_fwd(xl, mods[l], proj, ys, wg_br, wg_out, ln_g, ln_b, l)
        saved.append((xl, proj, h, o_a, tot, o_b, states, ys, merged, y))
        xl = x_next
    dx, loss_blk = _loss_head(xl, target)

    g_in = jax.ShapeDtypeStruct((N_CHIPS, DEPTH, D, IN_COLS // N_CHIPS), BF)
    g_br = jax.ShapeDtypeStruct((N_CHIPS, DEPTH, 3, WIDTH, D // N_CHIPS), BF)
    g_out = jax.ShapeDtypeStruct((N_CHIPS, DEPTH, D // N_CHIPS, D), BF)
    small = [None] * DEPTH
    for l in reversed(range(DEPTH)):
        xin, proj, h, o_a, tot, o_b, states, ys, merged, y = saved[l]
        dx_res, dy, dp, dg, dy_a, dy_b, dy_c, dln_g, dln_b, dgate = _merge_bwd(dx, xin, y, mods[l], proj, ys, wg_br, wg_out, ln_g, l)
        g_out = _grad_w_out(merged, dy, l, g_out)
        for i in range(3):
            g_br = _grad_w(ys[i], dp, (l, i), g_br, "grad_w_branch", tm=WIDTH, tn=D // N_CHIPS, b_col0=i * D)
        d_oa, d_ob, dz_a, dz_b, dpre, dpost, du, dz_c, dnorm_w, dconv_w = _branch_bwd(proj, o_a, o_b, hgrn_norm_w, conv_all, dy_a, dy_b, dy_c, l)
        dq_a, dk_a, dv_a = _attn_bwd(proj, d_oa, tot, after, before)
        dq_b, df_b, di_b, dlb = _hgrn_bwd(proj, lbs[l], states, d_ob, hg_consts)
        dproj = jnp.concatenate([dq_a, dk_a, dv_a, dz_a, dq_b, df_b, di_b, dz_b, dpre, dpost, du, dz_c, dg], axis=1)
        g_in = _grad_w(h, dproj, (l,), g_in, "grad_w_in", tm=512, tn=768)
        dx, dshift, dscale = _proj_bwd(dproj, wg_in, xin, mods[l], dx_res, l)
        small[l] = (dshift, dscale, dgate, dnorm_w, dlb, dln_g, dln_b, dconv_w)
    return loss_blk, dx, [g_in, g_br, g_out], small


def _grad_w_out(merged, dy, layer, prev):
    S, D = merged.shape
    q = D // N_CHIPS

    def body(*refs):
        a_ref, b_ref, o_ref = refs[0], refs[1], refs[-1]
        o_ref[...] = _dot_tn(a_ref[...], b_ref[...]).astype(BF)

    in_specs = [pl.BlockSpec((S, q), lambda j: (0, j)), pl.BlockSpec((S, D), lambda j: (0, 0))]
    args = [merged, dy]
    aliases = {}
    if not isinstance(prev, jax.ShapeDtypeStruct):
        in_specs.append(pl.BlockSpec(memory_space=pl.ANY))
        args.append(prev)
        aliases = {2: 0}
    return pl.pallas_call(
        body, name="grad_w_out",
        grid=(N_CHIPS,),
        in_specs=in_specs,
        out_specs=pl.BlockSpec((None, None, q, D), lambda j: (j, layer, 0, 0)),
        out_shape=jax.ShapeDtypeStruct(prev.shape, BF),
        input_output_aliases=aliases,
        compiler_params=_params(dimension_semantics=("parallel",)),
    )(*args)
```

```python
import functools
import math

import numpy as np
import jax
import jax.numpy as jnp
from jax import lax
from jax.experimental import pallas as pl
from jax.experimental.pallas import tpu as pltpu

F32 = jnp.float32
BF = jnp.bfloat16
MESH = pl.DeviceIdType.MESH

DEPTH = 2
D_MODEL = 1024
WIDTH = 512
IN_COLS = 12 * WIDTH + 3 * D_MODEL
N_CHIPS = 4
N_DEV = 8
SB_BLOCK = 128
SB_HEAD_DIM = 64
HG_CHUNK = 64
HG_DIM = 128
LN_EPS = 1e-5
RMS_EPS = 1e-6
ALPHA = (2.0 * DEPTH) ** 0.25
ADAM_LR, ADAM_B1, ADAM_B2, ADAM_EPS, ADAM_WD, ADAM_STEP = 0.001, 0.9, 0.999, 1e-08, 0.01, 10
VMEM_LIMIT = 56 << 20


def _params(**kw):
    return pltpu.CompilerParams(vmem_limit_bytes=VMEM_LIMIT, **kw)


def _dot(a, b):
    return jnp.dot(a, b, preferred_element_type=F32)


def _dot_nt(a, b):
    return lax.dot_general(a, b, (((1,), (1,)), ((), ())), preferred_element_type=F32)


def _dot_tn(a, b):
    return lax.dot_general(a, b, (((0,), (0,)), ((), ())), preferred_element_type=F32)


def _sigmoid(x):
    return 1.0 / (1.0 + jnp.exp(-x))


def _silu(x):
    return x * _sigmoid(x)


def _softplus(z):
    return jnp.maximum(z, 0.0) + jnp.log(1.0 + jnp.exp(-jnp.abs(z)))


def _split_dot(t, g, terms):
    acc = None
    rest = g
    for _ in range(terms):
        part = rest.astype(BF)
        rest = rest - part.astype(F32)
        d = _dot(t, part)
        acc = d if acc is None else acc + d
    return acc


def _split_dot_r(g, t, terms):
    acc = None
    rest = g
    for _ in range(terms):
        part = rest.astype(BF)
        rest = rest - part.astype(F32)
        d = _dot(part, t)
        acc = d if acc is None else acc + d
    return acc


def _standardize(x):
    mu = jnp.mean(x, axis=-1, keepdims=True)
    xc = x - mu
    var = jnp.mean(xc * xc, axis=-1, keepdims=True)
    rstd = lax.rsqrt(var + LN_EPS)
    return xc * rstd, rstd


def _standardize_bwd(dxs, xs, rstd):
    return rstd * (dxs - jnp.mean(dxs, axis=-1, keepdims=True) - xs * jnp.mean(dxs * xs, axis=-1, keepdims=True))


class _Job:
    def __init__(self, ins, outs, sems, make):
        self.ins, self.outs, self.sems, self.make = list(ins), list(outs), list(sems), make


def _flip(v, bit):
    return 1 - v if bit else v


def _exchange_sems(n):
    return [pltpu.SemaphoreType.DMA((n,)), pltpu.SemaphoreType.DMA((n, N_CHIPS - 1)), pltpu.SemaphoreType.DMA((n, N_CHIPS - 1))]


def _gather_job(shards):
    n = len(shards)

    def make(ins, outs, sems):
        local_sems, send_sems, recv_sems = sems
        mx, my, mc = lax.axis_index("x"), lax.axis_index("y"), lax.axis_index("c")
        mine = 2 * mx + my
        copies = []
        for a in range(n):
            copies.append(pltpu.make_async_copy(ins[a], outs[a].at[mine], local_sems.at[a]))
            for k in range(1, N_CHIPS):
                peer = (_flip(mx, k & 2), _flip(my, k & 1), mc)
                copies.append(pltpu.make_async_remote_copy(
                    src_ref=ins[a], dst_ref=outs[a].at[mine], send_sem=send_sems.at[a, k - 1], recv_sem=recv_sems.at[a, k - 1],
                    device_id=peer, device_id_type=MESH))
        return copies

    return _Job(shards, [jax.ShapeDtypeStruct((N_CHIPS,) + s.shape, s.dtype) for s in shards], _exchange_sems(n), make)


def _scatter_job(grads):
    n = len(grads)

    def make(ins, outs, sems):
        local_sems, send_sems, recv_sems = sems
        owns, gots = outs[:n], outs[n:]
        mx, my, mc = lax.axis_index("x"), lax.axis_index("y"), lax.axis_index("c")
        copies = []
        for a in range(n):
            copies.append(pltpu.make_async_copy(ins[a].at[2 * mx + my], owns[a], local_sems.at[a]))
            for k in range(1, N_CHIPS):
                px, py = _flip(mx, k & 2), _flip(my, k & 1)
                copies.append(pltpu.make_async_remote_copy(
                    src_ref=ins[a].at[2 * px + py], dst_ref=gots[a].at[k - 1], send_sem=send_sems.at[a, k - 1],
                    recv_sem=recv_sems.at[a, k - 1], device_id=(px, py, mc), device_id_type=MESH))
        return copies

    outs = [jax.ShapeDtypeStruct(g.shape[1:], g.dtype) for g in grads]
    outs += [jax.ShapeDtypeStruct((N_CHIPS - 1,) + g.shape[1:], g.dtype) for g in grads]
    return _Job(grads, outs, _exchange_sems(n), make)


def _swap_job(parts):
    n = len(parts)

    def make(ins, outs, sems):
        send_sems, recv_sems = sems
        peer = (lax.axis_index("x"), lax.axis_index("y"), 1 - lax.axis_index("c"))
        return [pltpu.make_async_remote_copy(src_ref=ins[a], dst_ref=outs[a], send_sem=send_sems.at[a], recv_sem=recv_sems.at[a],
                                             device_id=peer, device_id_type=MESH) for a in range(n)]

    return _Job(parts, [jax.ShapeDtypeStruct(p.shape, p.dtype) for p in parts],
                [pltpu.SemaphoreType.DMA((n,)), pltpu.SemaphoreType.DMA((n,))], make)


def _run_job(job, name):
    k_in, k_out = len(job.ins), len(job.outs)

    def body(*refs):
        copies = job.make(refs[:k_in], refs[k_in:k_in + k_out], refs[k_in + k_out:])
        for cp in copies:
            cp.start()
        for cp in copies:
            cp.wait()

    hbm = pl.BlockSpec(memory_space=pl.ANY)
    return pl.pallas_call(body, name=name, in_specs=[hbm] * k_in, out_specs=[hbm] * k_out, out_shape=job.outs,
                          scratch_shapes=job.sems, compiler_params=_params())(*job.ins)


def _hosted(body, job, args, *, name, grid, in_specs, out_specs, out_shape, scratch_shapes=(), semantics):
    in_specs, out_specs, out_shape, scratch = list(in_specs), list(out_specs), list(out_shape), list(scratch_shapes)
    if job is None:
        outs = pl.pallas_call(body, name=name, grid=grid, in_specs=in_specs, out_specs=out_specs, out_shape=out_shape,
                              scratch_shapes=scratch, compiler_params=_params(dimension_semantics=semantics))(*args)
        return list(outs), None
    n_in, n_out, n_scr, k_in, k_out = len(in_specs), len(out_specs), len(scratch), len(job.ins), len(job.outs)

    def wrapped(*refs):
        ins, rest = refs[:n_in], refs[n_in:]
        job_ins, rest = rest[:k_in], rest[k_in:]
        outs, rest = rest[:n_out], rest[n_out:]
        job_outs, rest = rest[:k_out], rest[k_out:]
        scr, sems = rest[:n_scr], rest[n_scr:]
        ids = [pl.program_id(a) for a in range(len(grid))]
        first = functools.reduce(jnp.logical_and, [i == 0 for i in ids])
        last = functools.reduce(jnp.logical_and, [i == g - 1 for i, g in zip(ids, grid)])

        @pl.when(first)
        def _():
            for cp in job.make(job_ins, job_outs, sems):
                cp.start()

        body(*ins, *outs, *scr)

        @pl.when(last)
        def _():
            for cp in job.make(job_ins, job_outs, sems):
                cp.wait()

    hbm = pl.BlockSpec(memory_space=pl.ANY)
    outs = pl.pallas_call(
        wrapped, name=name, grid=grid, in_specs=in_specs + [hbm] * k_in, out_specs=out_specs + [hbm] * k_out,
        out_shape=out_shape + job.outs, scratch_shapes=scratch + job.sems,
        compiler_params=_params(dimension_semantics=("arbitrary",) * len(grid)))(*args, *job.ins)
    return list(outs[:n_out]), list(outs[n_out:])


def _proj_fwd(x, mod, wg, job=None, tm=512, tn=768):
    S, D = x.shape
    tm = min(tm, S)
    shard = wg.shape[-1]
    per = shard // tn

    def body(x_ref, mod_ref, w_ref, proj_ref, h_ref, hs):
        @pl.when(pl.program_id(1) == 0)
        def _():
            xs, _ = _standardize(x_ref[...])
            h = xs * (1.0 + mod_ref[:, D:2 * D]) + mod_ref[:, 0:D]
            hb = h.astype(BF)
            hs[...] = hb
            h_ref[...] = hb

        proj_ref[...] = _dot(hs[...], w_ref[...])

    return _hosted(
        body, job, (x, mod, wg), name="proj_fwd",
        grid=(S // tm, IN_COLS // tn),
        in_specs=[pl.BlockSpec((tm, D), lambda i, j: (i, 0)),
                  pl.BlockSpec((1, 3 * D), lambda i, j: (0, 0)),
                  pl.BlockSpec((None, D, tn), lambda i, j: (j // per, 0, j % per))],
        out_specs=[pl.BlockSpec((tm, tn), lambda i, j: (i, j)),
                   pl.BlockSpec((tm, D), lambda i, j: (i, 0))],
        out_shape=[jax.ShapeDtypeStruct((S, IN_COLS), F32), jax.ShapeDtypeStruct((S, D), BF)],
        scratch_shapes=[pltpu.VMEM((tm, D), BF)],
        semantics=("parallel", "arbitrary"))


def _attn_consts():
    j = np.arange(SB_BLOCK)[:, None]
    s = np.arange(SB_BLOCK)[None, :]
    ones = np.ones((SB_BLOCK, SB_BLOCK), np.float32)
    after = np.concatenate([(j > s).astype(np.float32), ones], axis=1)
    before = np.concatenate([(j < s).astype(np.float32), ones], axis=1)
    return jnp.asarray(after, BF), jnp.asarray(before, BF)


SB_ROWS = 256


def _hi_lo(x):
    hi = lax.bitcast_convert_type(lax.bitcast_convert_type(x, jnp.uint32) & jnp.uint32(0xFFFF0000), F32)
    return hi.astype(BF), (x - hi).astype(BF)


def _sums_r(x, t):
    hi, lo = _hi_lo(x)
    return _dot(hi, t) + _dot(lo, t)


def _attn_rows(ref, r0, rows, head0, scale=None):
    v = ref[pl.ds(r0, rows), :]
    if scale is not None:
        v = v * scale
    return jnp.concatenate([jnp.where(head0, v, 0.0), jnp.where(head0, 0.0, v)], axis=0).astype(BF)


def _attn_scores(q2, k_ref, kj, t2, masked):
    c0 = pl.multiple_of(kj * SB_BLOCK, SB_BLOCK)
    kb = k_ref[pl.ds(c0, SB_BLOCK), :].astype(BF)
    z = _dot_nt(q2, kb)
    l = jnp.log(1.0 + jnp.exp(-jnp.abs(z)))
    lsb = -(jnp.maximum(z, 0.0) + l)
    ls = jnp.minimum(z, 0.0) - l
    if masked:
        valid = (lax.broadcasted_iota(jnp.int32, z.shape, 1) + kj * SB_BLOCK) < t2
        lsb = jnp.where(valid, lsb, 0.0)
        ls = jnp.where(valid, ls, -jnp.inf)
    return kb, lsb, ls


def _attn_fwd(proj, after, job=None):
    S = proj.shape[0]
    TQ = min(SB_ROWS, S)
    kpb = TQ // SB_BLOCK
    scale = SB_HEAD_DIM ** -0.5

    def body(q_ref, k_ref, v_ref, after_ref, o_ref, tot_ref, run, acc):
        head0 = lax.broadcasted_iota(jnp.int32, (1, 2 * SB_HEAD_DIM), 1) < SB_HEAD_DIM

        def qloop(qi, _):
            r0 = pl.multiple_of(qi * TQ, TQ)
            q2 = _attn_rows(q_ref, r0, TQ, head0, scale)
            trow = lax.broadcasted_iota(jnp.int32, (TQ, SB_BLOCK), 0) + qi * TQ
            t2 = jnp.concatenate([trow, trow], axis=0)
            run[...] = jnp.zeros_like(run)
            acc[...] = jnp.zeros_like(acc)

            def scores(kj, masked):
                _, lsb, ls = _attn_scores(q2, k_ref, kj, t2, masked)
                return kj, ls, _sums_r(lsb, after_ref[...])

            def weigh(st):
                kj, ls, cs2 = st
                c0 = pl.multiple_of(kj * SB_BLOCK, SB_BLOCK)
                a = jnp.exp(ls + cs2[:, :SB_BLOCK] + run[...])
                acc[...] += _dot(a.astype(BF), v_ref[pl.ds(c0, SB_BLOCK), :].astype(BF))
                run[...] += cs2[:, SB_BLOCK:]

            nk = (qi + 1) * kpb
            st = scores(nk - 1, True)
            for d in range(1, kpb):
                nxt = scores(nk - 1 - d, True)
                weigh(st)
                st = nxt

            def below(n, st):
                nxt = scores(nk - kpb - 1 - n, False)
                weigh(st)
                return nxt

            weigh(lax.fori_loop(0, nk - kpb, below, st))
            o_ref[pl.ds(r0, TQ), :] = jnp.where(head0, acc[0:TQ, :], acc[TQ:2 * TQ, :])
            tot_ref[0, pl.ds(r0, TQ), :] = run[0:TQ, :]
            tot_ref[1, pl.ds(r0, TQ), :] = run[TQ:2 * TQ, :]
            return 0

        lax.fori_loop(0, S // TQ, qloop, 0)

    col = lambda base: pl.BlockSpec((S, 128), lambda p, base=base: (0, base + p))
    return _hosted(
        body, job, (proj, proj, proj, after), name="attn_fwd",
        grid=(WIDTH // 128,),
        in_specs=[col(0), col(4), col(8), pl.BlockSpec(after.shape, lambda p: (0, 0))],
        out_specs=[pl.BlockSpec((S, 128), lambda p: (0, p)), pl.BlockSpec((None, 2, S, 128), lambda p: (p, 0, 0, 0))],
        out_shape=[jax.ShapeDtypeStruct((S, WIDTH), F32), jax.ShapeDtypeStruct((WIDTH // 128, 2, S, 128), F32)],
        scratch_shapes=[pltpu.VMEM((2 * TQ, SB_BLOCK), F32), pltpu.VMEM((2 * TQ, 128), F32)],
        semantics=("parallel",))


def _attn_bwd(proj, d_o, tot, after, before, job=None):
    S = proj.shape[0]
    TQ = min(SB_ROWS, S)
    kpb = TQ // SB_BLOCK
    scale = SB_HEAD_DIM ** -0.5

    def body(q_ref, k_ref, v_ref, do_ref, tot_ref, after_ref, before_ref, dq_ref, dk_ref, dv_ref, pre, cum, dq_acc, dk_acc, dv_acc):
        head0 = lax.broadcasted_iota(jnp.int32, (1, 2 * SB_HEAD_DIM), 1) < SB_HEAD_DIM
        dk_acc[...] = jnp.zeros_like(dk_acc)
        dv_acc[...] = jnp.zeros_like(dv_acc)

        def qloop(qi, _):
            r0 = pl.multiple_of(qi * TQ, TQ)
            q2 = _attn_rows(q_ref, r0, TQ, head0, scale)
            do2 = _attn_rows(do_ref, r0, TQ, head0)
            trow = lax.broadcasted_iota(jnp.int32, (TQ, SB_BLOCK), 0) + qi * TQ
            t2 = jnp.concatenate([trow, trow], axis=0)
            pre[0:TQ, :] = tot_ref[0, pl.ds(r0, TQ), :]
            pre[TQ:2 * TQ, :] = tot_ref[1, pl.ds(r0, TQ), :]
            cum[...] = jnp.zeros_like(cum)
            dq_acc[...] = jnp.zeros_like(dq_acc)

            def step(kj, masked):
                c0 = pl.multiple_of(kj * SB_BLOCK, SB_BLOCK)
                kb, lsb, ls = _attn_scores(q2, k_ref, kj, t2, masked)
                cs2 = _sums_r(lsb, after_ref[...])
                later = pre[...] - cs2[:, SB_BLOCK:]
                pre[...] = later
                a = jnp.exp(ls + cs2[:, :SB_BLOCK] + later)
                w = _dot_nt(do2, v_ref[pl.ds(c0, SB_BLOCK), :].astype(BF)) * a
                c2 = _sums_r(w, before_ref[...])
                sig = jnp.exp(ls)
                dz = w * (1.0 - sig) - (c2[:, :SB_BLOCK] + cum[...]) * sig
                cum[...] += c2[:, SB_BLOCK:]
                dzb = dz.astype(BF)
                dq_acc[...] += _dot(dzb, kb)
                dk_acc[pl.ds(c0, SB_BLOCK), :] += _dot_tn(dzb, q2)
                dv_acc[pl.ds(c0, SB_BLOCK), :] += _dot_tn(a.astype(BF), do2)

            nk = (qi + 1) * kpb

            def below(kj, _):
                step(kj, False)
                return 0

            lax.fori_loop(0, nk - kpb, below, 0)
            for d in range(kpb):
                step(nk - kpb + d, True)
            dq_ref[pl.ds(r0, TQ), :] = (jnp.where(head0, dq_acc[0:TQ, :], dq_acc[TQ:2 * TQ, :]) * scale).astype(BF)
            return 0

        lax.fori_loop(0, S // TQ, qloop, 0)
        dk_ref[...] = dk_acc[...].astype(BF)
        dv_ref[...] = dv_acc[...].astype(BF)

    col = lambda base: pl.BlockSpec((S, 128), lambda p, base=base: (0, base + p))
    whole = lambda a: pl.BlockSpec(a.shape, lambda p: (0, 0))
    out = jax.ShapeDtypeStruct((S, WIDTH), BF)
    big = pltpu.VMEM((2 * TQ, SB_BLOCK), F32)
    return _hosted(
        body, job, (proj, proj, proj, d_o, tot, after, before), name="attn_bwd",
        grid=(WIDTH // 128,),
        in_specs=[col(0), col(4), col(8), col(0), pl.BlockSpec((None, 2, S, 128), lambda p: (p, 0, 0, 0)), whole(after), whole(before)],
        out_specs=[col(0), col(0), col(0)],
        out_shape=[out, out, out],
        scratch_shapes=[big, big, big, pltpu.VMEM((S, 128), F32), pltpu.VMEM((S, 128), F32)],
        semantics=("parallel",))


HG_LEVELS = (32, 16, 8, 4, 2, 1)


def _hgrn_consts():
    C = HG_CHUNK
    t = np.arange(C)[:, None]
    s = np.arange(C)[None, :]
    rows = [(s <= t), (s > t)]
    masks = [(t == s)]
    for m in HG_LEVELS:
        two = 2 * m
        mid = (t // two) * two + m
        right = (t % two) >= m
        rows.append(right & (s >= mid) & (s <= t))
        rows.append((~right) & (s > t) & (s <= mid - 1))
        masks.append(((t // two) == (s // two)) & right & ((s % two) < m))
    rows.append(np.ones((HG_DIM, C), bool))
    tri = np.concatenate(rows, axis=0).astype(np.float32)
    return (jnp.asarray(tri, BF), jnp.asarray(tri.T.copy(), BF), jnp.asarray(np.stack(masks).astype(np.float32), F32))


HG_SUM_ROWS = (HG_CHUNK,) * (2 + 2 * len(HG_LEVELS)) + (HG_DIM,)


@jax.custom_vjp
def _hgrn_sums(tri, tri_t, g):
    e = _split_dot(tri, g, 3)
    out, r = [], 0
    for n in HG_SUM_ROWS:
        out.append(e[r:r + n])
        r += n
    return tuple(out)


def _hgrn_sums_fwd(tri, tri_t, g):
    return _hgrn_sums(tri, tri_t, g), (tri, tri_t)


def _hgrn_sums_bwd(res, ds):
    tri, tri_t = res
    return jnp.zeros_like(tri), jnp.zeros_like(tri_t), _split_dot(tri_t, jnp.concatenate(ds, axis=0), 3)


_hgrn_sums.defvjp(_hgrn_sums_fwd, _hgrn_sums_bwd)


def _bf_dot(a, b):
    return _dot(a.astype(BF), b.astype(BF))


def _bf_dot_nt(a, b):
    return _dot_nt(a.astype(BF), b.astype(BF))


def _bf_dot_tn(a, b):
    return _dot_tn(a.astype(BF), b.astype(BF))


@jax.custom_vjp
def _mm(a, b):
    return _bf_dot(a, b)


_mm.defvjp(lambda a, b: (_bf_dot(a, b), (a, b)), lambda r, ct: (_bf_dot_nt(ct, r[1]), _bf_dot_tn(r[0], ct)))


@jax.custom_vjp
def _mm_nt(a, b):
    return _bf_dot_nt(a, b)


_mm_nt.defvjp(lambda a, b: (_bf_dot_nt(a, b), (a, b)), lambda r, ct: (_bf_dot(ct, r[1]), _bf_dot_tn(ct, r[0])))


@jax.custom_vjp
def _mm_tn(a, b):
    return _bf_dot_tn(a, b)


_mm_tn.defvjp(lambda a, b: (_bf_dot_tn(a, b), (a, b)), lambda r, ct: (_bf_dot_nt(r[1], ct), _bf_dot(r[0], ct)))


def _hgrn_chunk(tri, tri_t, masks, qraw, fpre, v, st, lb):
    q = _silu(qraw)
    f = lb + (1.0 - lb) * _sigmoid(fpre)
    k = 1.0 - f
    e = _hgrn_sums(tri, tri_t, jnp.log(f))
    prefix, suffix, whole = e[0], e[1], e[-1]
    scores = masks[0] * _mm_nt(q, k)
    for n in range(len(HG_LEVELS)):
        scores = scores + masks[n + 1] * _mm_nt(q * jnp.exp(e[2 + 2 * n]), k * jnp.exp(e[3 + 2 * n]))
    o = _mm_nt(q * jnp.exp(prefix), st) + _mm(scores, v)
    st_new = st * jnp.exp(whole) + _mm_tn(v, k * jnp.exp(suffix))
    return o, st_new


def _hgrn_specs(S, consts):
    col = lambda base: pl.BlockSpec((S, 128), lambda p, base=base: (0, base + p))
    whole = [pl.BlockSpec(a.shape, lambda p, n=a.ndim: (0,) * n) for a in consts]
    return col, whole


def _hgrn_fwd(proj, lbs, consts):
    S = proj.shape[0]
    nc = S // HG_CHUNK

    def body(q_ref, f_ref, i_ref, lb_ref, tri_ref, trit_ref, mask_ref, o_ref, st_ref):
        tri, tri_t = tri_ref[...], trit_ref[...]
        masks = [mask_ref[n] for n in range(len(HG_LEVELS) + 1)]
        lb = lb_ref[...]

        def chunk(ci, st):
            r0 = pl.multiple_of(ci * HG_CHUNK, HG_CHUNK)
            rows = pl.ds(r0, HG_CHUNK)
            st_ref[ci] = st
            o, st_new = _hgrn_chunk(tri, tri_t, masks, q_ref[rows, :], f_ref[rows, :], i_ref[rows, :], st, lb)
            o_ref[rows, :] = o
            return st_new

        lax.fori_loop(0, nc, chunk, jnp.zeros((HG_DIM, HG_DIM), F32))

    col, whole = _hgrn_specs(S, consts)
    return pl.pallas_call(
        body, name="hgrn_fwd",
        grid=(WIDTH // 128,),
        in_specs=[col(16), col(20), col(24), pl.BlockSpec((None, 1, 128), lambda p: (p, 0, 0))] + whole,
        out_specs=[col(0), pl.BlockSpec((None, nc, HG_DIM, HG_DIM), lambda p: (p, 0, 0, 0))],
        out_shape=[jax.ShapeDtypeStruct((S, WIDTH), F32), jax.ShapeDtypeStruct((WIDTH // 128, nc, HG_DIM, HG_DIM), F32)],
        compiler_params=_params(dimension_semantics=("parallel",)),
    )(proj, proj, proj, lbs, *consts)


def _hgrn_bwd(proj, lbs, states, d_o, consts):
    S = proj.shape[0]
    nc = S // HG_CHUNK

    def body(q_ref, f_ref, i_ref, lb_ref, st_ref, do_ref, tri_ref, trit_ref, mask_ref, dq_ref, df_ref, di_ref, dlb_ref):
        masks = [mask_ref[n] for n in range(len(HG_LEVELS) + 1)]
        lb = lb_ref[...]
        fn = functools.partial(_hgrn_chunk, tri_ref[...], trit_ref[...], masks)

        def chunk(n, carry):
            d_st, dlb = carry
            ci = nc - 1 - n
            r0 = pl.multiple_of(ci * HG_CHUNK, HG_CHUNK)
            rows = pl.ds(r0, HG_CHUNK)
            _, pull = jax.vjp(fn, q_ref[rows, :], f_ref[rows, :], i_ref[rows, :], st_ref[ci], lb)
            dq, df, di, d_prev, dl = pull((do_ref[rows, :], d_st))
            dq_ref[rows, :] = dq.astype(BF)
            df_ref[rows, :] = df.astype(BF)
            di_ref[rows, :] = di.astype(BF)
            return d_prev, dlb + dl

        _, dlb = lax.fori_loop(0, nc, chunk, (jnp.zeros((HG_DIM, HG_DIM), F32), jnp.zeros((1, HG_DIM), F32)))
        dlb_ref[...] = dlb

    col, whole = _hgrn_specs(S, consts)
    head = pl.BlockSpec((None, 1, 128), lambda p: (p, 0, 0))
    out = jax.ShapeDtypeStruct((S, WIDTH), BF)
    return pl.pallas_call(
        body, name="hgrn_bwd",
        grid=(WIDTH // 128,),
        in_specs=[col(16), col(20), col(24), head, pl.BlockSpec((None, nc, HG_DIM, HG_DIM), lambda p: (p, 0, 0, 0)), col(0)] + whole,
        out_specs=[col(0), col(0), col(0), head],
        out_shape=[out, out, out, jax.ShapeDtypeStruct((WIDTH // 128, 1, 128), F32)],
        compiler_params=_params(dimension_semantics=("parallel",)),
    )(proj, proj, proj, lbs, states, d_o, *consts)


def _shift_down(x, n):
    rows = lax.broadcasted_iota(jnp.int32, x.shape, 0)
    return jnp.where(rows >= n, pltpu.roll(x, n, 0), 0.0)


def _shift_up(x, n):
    S = x.shape[0]
    rows = lax.broadcasted_iota(jnp.int32, x.shape, 0)
    return jnp.where(rows < S - n, pltpu.roll(x, S - n, 0), 0.0)


def _branch_fwd(proj, o_a, o_b, norm_w, conv_w, layer):
    S = proj.shape[0]

    def body(oa_ref, za_ref, ob_ref, zb_ref, nw_ref, pre_ref, post_ref, u_ref, zc_ref, cw_ref, ya_ref, yb_ref, yc_ref):
        ya_ref[...] = (oa_ref[...] * _silu(za_ref[...])).astype(BF)
        ob = ob_ref[...]
        rn = lax.rsqrt(jnp.mean(ob * ob, axis=-1, keepdims=True) + RMS_EPS)
        yb_ref[...] = (ob * rn * nw_ref[layer:layer + 1, :] * _silu(zb_ref[...])).astype(BF)
        pu = pre_ref[...] * u_ref[...]
        conv = cw_ref[2:3, :] * pu + cw_ref[1:2, :] * _shift_down(pu, 1) + cw_ref[0:1, :] * _shift_down(pu, 2)
        yc_ref[...] = (post_ref[...] * conv * _silu(zc_ref[...])).astype(BF)

    col = lambda base: pl.BlockSpec((S, 128), lambda p, base=base: (0, base + p))
    out = jax.ShapeDtypeStruct((S, WIDTH), BF)
    return pl.pallas_call(
        body, name="branch_fwd",
        grid=(WIDTH // 128,),
        in_specs=[col(0), col(12), col(0), col(28), pl.BlockSpec(norm_w.shape, lambda p: (0, 0)),
                  col(32), col(36), col(40), col(44), pl.BlockSpec((None, None, 3, 128), lambda p: (p, layer, 0, 0))],
        out_specs=[col(0), col(0), col(0)],
        out_shape=[out, out, out],
        compiler_params=_params(dimension_semantics=("parallel",)),
    )(o_a, proj, o_b, proj, norm_w, proj, proj, proj, proj, conv_w)


def _branch_bwd(proj, o_a, o_b, norm_w, conv_w, dy_a, dy_b, dy_c, layer):
    S = proj.shape[0]

    def dsilu(z):
        s = _sigmoid(z)
        return s * z, s * (1.0 + z * (1.0 - s))

    def body(oa_ref, za_ref, ob_ref, zb_ref, nw_ref, pre_ref, post_ref, u_ref, zc_ref, cw_ref, dya_ref, dyb_ref, dyc_ref,
             doa_ref, dob_ref, dza_ref, dzb_ref, dpre_ref, dpost_ref, du_ref, dzc_ref, dnw_ref, dcw_ref):
        dya = dya_ref[...]
        sa, dsa = dsilu(za_ref[...])
        doa_ref[...] = dya * sa
        dza_ref[...] = (dya * oa_ref[...] * dsa).astype(BF)

        dyb = dyb_ref[...]
        ob = ob_ref[...]
        nw = nw_ref[layer:layer + 1, :]
        sb, dsb = dsilu(zb_ref[...])
        rn = lax.rsqrt(jnp.mean(ob * ob, axis=-1, keepdims=True) + RMS_EPS)
        on = ob * rn
        dzb_ref[...] = (dyb * on * nw * dsb).astype(BF)
        don_w = dyb * sb
        dnw_ref[...] = jnp.sum(don_w * on, axis=0, keepdims=True)
        don = don_w * nw
        dob_ref[...] = rn * (don - on * jnp.mean(don * on, axis=-1, keepdims=True))

        dyc = dyc_ref[...]
        pre, post, u = pre_ref[...], post_ref[...], u_ref[...]
        sc, dsc = dsilu(zc_ref[...])
        pu = pre * u
        pu1, pu2 = _shift_down(pu, 1), _shift_down(pu, 2)
        conv = cw_ref[2:3, :] * pu + cw_ref[1:2, :] * pu1 + cw_ref[0:1, :] * pu2
        dzc_ref[...] = (dyc * post * conv * dsc).astype(BF)
        dpost_ref[...] = (dyc * conv * sc).astype(BF)
        dconv = dyc * post * sc
        dcw_ref[0:1, :] = jnp.sum(dconv * pu2, axis=0, keepdims=True)
        dcw_ref[1:2, :] = jnp.sum(dconv * pu1, axis=0, keepdims=True)
        dcw_ref[2:3, :] = jnp.sum(dconv * pu, axis=0, keepdims=True)
        dpu = cw_ref[2:3, :] * dconv + cw_ref[1:2, :] * _shift_up(dconv, 1) + cw_ref[0:1, :] * _shift_up(dconv, 2)
        dpre_ref[...] = (dpu * u).astype(BF)
        du_ref[...] = (dpu * pre).astype(BF)

    col = lambda base: pl.BlockSpec((S, 128), lambda p, base=base: (0, base + p))
    f32 = jax.ShapeDtypeStruct((S, WIDTH), F32)
    bf = jax.ShapeDtypeStruct((S, WIDTH), BF)
    return pl.pallas_call(
        body, name="branch_bwd",
        grid=(WIDTH // 128,),
        in_specs=[col(0), col(12), col(0), col(28), pl.BlockSpec(norm_w.shape, lambda p: (0, 0)),
                  col(32), col(36), col(40), col(44), pl.BlockSpec((None, None, 3, 128), lambda p: (p, layer, 0, 0)),
                  col(0), col(0), col(0)],
        out_specs=[col(0)] * 8 + [pl.BlockSpec((None, 1, 128), lambda p: (p, 0, 0)), pl.BlockSpec((None, 3, 128), lambda p: (p, 0, 0))],
        out_shape=[f32, f32, bf, bf, bf, bf, bf, bf, jax.ShapeDtypeStruct((WIDTH // 128, 1, 128), F32),
                   jax.ShapeDtypeStruct((WIDTH // 128, 3, 128), F32)],
        compiler_params=_params(dimension_semantics=("parallel",)),
    )(o_a, proj, o_b, proj, norm_w, proj, proj, proj, proj, conv_w, dy_a, dy_b, dy_c)


def _branch_proj(y_refs, wb_ref):
    out = []
    for i in range(3):
        yv = y_refs[i][...]
        out.append(jnp.concatenate([_dot(yv, wb_ref[j, i]) for j in range(N_CHIPS)], axis=1))
    return out


def _merge_fwd(x, mod, proj, ys, wb, wo, ln_g, ln_b, layer, tm=256):
    S, D = x.shape
    tm = min(tm, S)

    def body(x_ref, mod_ref, ga_ref, gb_ref, gc_ref, ya_ref, yb_ref, yc_ref, wb_ref, wo_ref, g_ref, b_ref, xo_ref, mg_ref, y_ref):
        ps = _branch_proj((ya_ref, yb_ref, yc_ref), wb_ref)
        merged = _sigmoid(ga_ref[...]) * ps[0] + _sigmoid(gb_ref[...]) * ps[1] + _sigmoid(gc_ref[...]) * ps[2]
        mb = merged.astype(BF)
        mg_ref[...] = mb
        y = _dot(mb, wo_ref[...].reshape(D, D))
        y_ref[...] = y
        r = ALPHA * x_ref[...] + (1.0 + mod_ref[:, 2 * D:3 * D]) * y
        xn, _ = _standardize(r)
        xo_ref[...] = xn * g_ref[layer:layer + 1, :] + b_ref[layer:layer + 1, :]

    row = lambda w, c=0: pl.BlockSpec((tm, w), lambda i, c=c: (i, c))
    whole = lambda a: pl.BlockSpec(a.shape, lambda i, n=a.ndim: (0,) * n)
    return pl.pallas_call(
        body, name="merge_fwd",
        grid=(S // tm,),
        in_specs=[row(D), whole(mod), row(D, 6), row(D, 7), row(D, 8), row(WIDTH), row(WIDTH), row(WIDTH), whole(wb),
                  whole(wo), whole(ln_g), whole(ln_b)],
        out_specs=[row(D), row(D), row(D)],
        out_shape=[jax.ShapeDtypeStruct((S, D), F32), jax.ShapeDtypeStruct((S, D), BF), jax.ShapeDtypeStruct((S, D), F32)],
        compiler_params=_params(dimension_semantics=("parallel",)),
    )(x, mod, proj, proj, proj, *ys, wb, wo, ln_g, ln_b)


def _merge_bwd(dxo, x, y, mod, proj, ys, wb, wo, ln_g, layer, tm=256):
    S, D = x.shape
    tm = min(tm, S)

    def body(dxo_ref, x_ref, y_ref, mod_ref, ga_ref, gb_ref, gc_ref, ya_ref, yb_ref, yc_ref, wb_ref, wo_ref, g_ref,
             dxr_ref, dy_ref, dp_ref, dg_ref, dya_ref, dyb_ref, dyc_ref, dlg_ref, dlb_ref, dgt_ref):
        @pl.when(pl.program_id(0) == 0)
        def _():
            dlg_ref[...] = jnp.zeros_like(dlg_ref)
            dlb_ref[...] = jnp.zeros_like(dlb_ref)
            dgt_ref[...] = jnp.zeros_like(dgt_ref)

        gate1 = 1.0 + mod_ref[:, 2 * D:3 * D]
        yv = y_ref[...]
        xn, rstd = _standardize(ALPHA * x_ref[...] + gate1 * yv)
        dxo = dxo_ref[...]
        dlg_ref[...] += jnp.sum(dxo * xn, axis=0, keepdims=True)
        dlb_ref[...] += jnp.sum(dxo, axis=0, keepdims=True)
        dr = _standardize_bwd(dxo * g_ref[layer:layer + 1, :], xn, rstd)
        dxr_ref[...] = ALPHA * dr
        dgt_ref[...] += jnp.sum(dr * yv, axis=0, keepdims=True)
        dyb = (gate1 * dr).astype(BF)
        dy_ref[...] = dyb
        dmerged = _dot_nt(dyb, wo_ref[...].reshape(D, D))
        ps = _branch_proj((ya_ref, yb_ref, yc_ref), wb_ref)
        quarter = D // N_CHIPS
        for i, (gate_ref, out_ref) in enumerate(((ga_ref, dya_ref), (gb_ref, dyb_ref), (gc_ref, dyc_ref))):
            sg = _sigmoid(gate_ref[...])
            dg_ref[:, i * D:(i + 1) * D] = (dmerged * ps[i] * sg * (1.0 - sg)).astype(BF)
            dp = (dmerged * sg).astype(BF)
            dp_ref[:, i * D:(i + 1) * D] = dp
            acc = None
            for j in range(N_CHIPS):
                t = _dot_nt(dp[:, j * quarter:(j + 1) * quarter], wb_ref[j, i])
                acc = t if acc is None else acc + t
            out_ref[...] = acc

    row = lambda w, c=0: pl.BlockSpec((tm, w), lambda i, c=c: (i, c))
    whole = lambda a: pl.BlockSpec(a.shape, lambda i, n=a.ndim: (0,) * n)
    vec = pl.BlockSpec((1, D), lambda i: (0, 0))
    sd = jax.ShapeDtypeStruct
    return pl.pallas_call(
        body, name="merge_bwd",
        grid=(S // tm,),
        in_specs=[row(D), row(D), row(D), whole(mod), row(D, 6), row(D, 7), row(D, 8), row(WIDTH), row(WIDTH), row(WIDTH), whole(wb),
                  whole(wo), whole(ln_g)],
        out_specs=[row(D), row(D), row(3 * D), row(3 * D), row(WIDTH), row(WIDTH), row(WIDTH), vec, vec, vec],
        out_shape=[sd((S, D), F32), sd((S, D), BF), sd((S, 3 * D), BF), sd((S, 3 * D), BF), sd((S, WIDTH), F32), sd((S, WIDTH), F32),
                   sd((S, WIDTH), F32), sd((1, D), F32), sd((1, D), F32), sd((1, D), F32)],
        compiler_params=_params(dimension_semantics=("arbitrary",)),
    )(dxo, x, y, mod, proj, proj, proj, *ys, wb, wo, ln_g)


def _loss_head(x, target, tm=512):
    S, D = x.shape
    tm = min(tm, S)

    def body(x_ref, t_ref, dx_ref, loss_ref):
        @pl.when(pl.program_id(0) == 0)
        def _():
            loss_ref[...] = jnp.zeros_like(loss_ref)

        err = x_ref[...] - t_ref[...]
        dx_ref[...] = err * (1.0 / D)
        loss_ref[...] += 0.5 * jnp.sum(jnp.mean(err * err, axis=-1, keepdims=True))

    row = pl.BlockSpec((tm, D), lambda i: (i, 0))
    return pl.pallas_call(
        body, name="loss_head",
        grid=(S // tm,),
        in_specs=[row, row],
        out_specs=[row, pl.BlockSpec((8, 128), lambda i: (0, 0))],
        out_shape=[jax.ShapeDtypeStruct((S, D), F32), jax.ShapeDtypeStruct((8, 128), F32)],
        compiler_params=_params(dimension_semantics=("arbitrary",)),
    )(x, target)


def _proj_bwd(dproj, wg, x, mod, dx_res, job=None, tm=512, tk=768):
    S, D = x.shape
    tm = min(tm, S)
    shard = wg.shape[-1]
    per = shard // tk
    nk = IN_COLS // tk

    def body(dp_ref, w_ref, x_ref, mod_ref, dxr_ref, dx_ref, dsh_ref, dsc_ref, acc):
        i, k = pl.program_id(0), pl.program_id(1)

        @pl.when((i == 0) & (k == 0))
        def _():
            dsh_ref[...] = jnp.zeros_like(dsh_ref)
            dsc_ref[...] = jnp.zeros_like(dsc_ref)

        @pl.when(k == 0)
        def _():
            acc[...] = jnp.zeros_like(acc)

        acc[...] += _dot_nt(dp_ref[...], w_ref[...])

        @pl.when(k == nk - 1)
        def _():
            dh = acc[...]
            xs, rstd = _standardize(x_ref[...])
            dsh_ref[...] += jnp.sum(dh, axis=0, keepdims=True)
            dsc_ref[...] += jnp.sum(dh * xs, axis=0, keepdims=True)
            dx_ref[...] = _standardize_bwd(dh * (1.0 + mod_ref[:, D:2 * D]), xs, rstd) + dxr_ref[...]

    row = pl.BlockSpec((tm, D), lambda i, k: (i, 0))
    vec = pl.BlockSpec((1, D), lambda i, k: (0, 0))
    return _hosted(
        body, job, (dproj, wg, x, mod, dx_res), name="proj_bwd",
        grid=(S // tm, nk),
        in_specs=[pl.BlockSpec((tm, tk), lambda i, k: (i, k)),
                  pl.BlockSpec((None, D, tk), lambda i, k: (k // per, 0, k % per)),
                  row, pl.BlockSpec((1, 3 * D), lambda i, k: (0, 0)), row],
        out_specs=[row, vec, vec],
        out_shape=[jax.ShapeDtypeStruct((S, D), F32), jax.ShapeDtypeStruct((1, D), F32), jax.ShapeDtypeStruct((1, D), F32)],
        scratch_shapes=[pltpu.VMEM((tm, D), F32)],
        semantics=("arbitrary", "arbitrary"))


def _grad_w(a, b, mid, prev, name, tm, tn, b_col0=0):
    S, M = a.shape
    shape = prev.shape
    n_shard = shape[-1]
    per = n_shard // tn
    nm = M // tm
    lead = len(mid)

    def body(*refs):
        a_ref, b_ref, o_ref = refs[0], refs[1], refs[-1]
        o_ref[...] = _dot_tn(a_ref[...], b_ref[...]).astype(BF)

    in_specs = [pl.BlockSpec((S, tm), lambda m, n: (0, m)),
                pl.BlockSpec((S, tn), lambda m, n: (0, b_col0 // tn + n))]
    args = [a, b]
    aliases = {}
    if not isinstance(prev, jax.ShapeDtypeStruct):
        in_specs.append(pl.BlockSpec(memory_space=pl.ANY))
        args.append(prev)
        aliases = {2: 0}
    return pl.pallas_call(
        body, name=name,
        grid=(nm, N_CHIPS * per),
        in_specs=in_specs,
        out_specs=pl.BlockSpec((None,) + (None,) * lead + (tm, tn), lambda m, n: (n // per,) + tuple(mid) + (m, n % per)),
        out_shape=jax.ShapeDtypeStruct(shape, BF),
        input_output_aliases=aliases,
        compiler_params=_params(dimension_semantics=("parallel", "parallel")),
    )(*args)


def _all_gather8(x, name):
    R, N = x.shape

    def body(x_ref, out_ref, send_sems, recv_sems):
        mx, my, mc = lax.axis_index("x"), lax.axis_index("y"), lax.axis_index("c")
        me = 4 * mx + 2 * my + mc
        out_ref[me] = x_ref[...]
        copies = []
        for k in range(1, N_DEV):
            peer = (_flip(mx, k & 4), _flip(my, k & 2), _flip(mc, k & 1))
            cp = pltpu.make_async_remote_copy(src_ref=x_ref, dst_ref=out_ref.at[me], send_sem=send_sems.at[k - 1],
                                              recv_sem=recv_sems.at[k - 1], device_id=peer, device_id_type=MESH)
            cp.start()
            copies.append(cp)
        for cp in copies:
            cp.wait()

    return pl.pallas_call(
        body, name=name,
        in_specs=[pl.BlockSpec(memory_space=pltpu.VMEM)],
        out_specs=pl.BlockSpec(memory_space=pltpu.VMEM),
        out_shape=jax.ShapeDtypeStruct((N_DEV, R, N), F32),
        scratch_shapes=[pltpu.SemaphoreType.DMA((N_DEV - 1,)), pltpu.SemaphoreType.DMA((N_DEV - 1,))],
        compiler_params=_params(),
    )(x)


def _rows2d(a):
    return a.reshape(-1, a.shape[-1])


def _tile_rows(rows, cols, n_arrays):
    budget = (24 << 20) // (n_arrays * 2 * 4 * cols)
    if rows <= budget:
        return rows
    tm = 8
    for cand in range(8, budget + 1, 8):
        if rows % cand == 0:
            tm = cand
    return tm


def _sum4(own, got):
    o2 = _rows2d(own)
    rows, cols = o2.shape
    g3 = got.reshape(3, rows, cols)
    tm = _tile_rows(rows, cols, 3)

    def body(o_ref, g_ref, out_ref):
        out_ref[...] = ((o_ref[...].astype(F32) + g_ref[0].astype(F32)) + g_ref[1].astype(F32)) + g_ref[2].astype(F32)

    out = pl.pallas_call(
        body, name="sum_partials",
        grid=(rows // tm,),
        in_specs=[pl.BlockSpec((tm, cols), lambda i: (i, 0)), pl.BlockSpec((3, tm, cols), lambda i: (0, i, 0))],
        out_specs=pl.BlockSpec((tm, cols), lambda i: (i, 0)),
        out_shape=jax.ShapeDtypeStruct((rows, cols), F32),
        compiler_params=_params(dimension_semantics=("parallel",)),
    )(o2, g3)
    return out.reshape(own.shape)


def _adamw(w, m, v, groups, name):
    shape = w.shape
    w2, m2, v2 = _rows2d(w), _rows2d(m), _rows2d(v)
    rows, cols = w2.shape
    ng = len(groups)
    n = len(groups[0])
    slab = rows // ng
    gs = [_rows2d(g) for grp in groups for g in grp]
    tm = _tile_rows(slab, cols, 7 + n)
    tiles = slab // tm
    c1 = 1.0 / (1.0 - ADAM_B1 ** ADAM_STEP)
    c2 = 1.0 / (1.0 - ADAM_B2 ** ADAM_STEP)

    def body(*refs):
        w_ref, m_ref, v_ref = refs[:3]
        g_refs = refs[3:3 + ng * n]
        go_ref, d_ref, mo_ref, vo_ref = refs[3 + ng * n:]
        which = pl.program_id(0)
        for s in range(ng):
            @pl.when(which == s)
            def _(s=s):
                g = g_refs[s * n][...]
                for r in g_refs[s * n + 1:(s + 1) * n]:
                    g = g + r[...]
                mn = ADAM_B1 * m_ref[...] + (1.0 - ADAM_B1) * g
                vn = ADAM_B2 * v_ref[...] + (1.0 - ADAM_B2) * (g * g)
                go_ref[...] = g
                mo_ref[...] = mn
                vo_ref[...] = vn
                d_ref[...] = -ADAM_LR * ((mn * c1) / (jnp.sqrt(vn * c2) + ADAM_EPS) + ADAM_WD * w_ref[...])

    spec = pl.BlockSpec((tm, cols), lambda s, i: (s * tiles + i, 0))
    g_specs = [pl.BlockSpec((tm, cols), lambda s, i, k=k: (jnp.where(s == k, i, jnp.where(s < k, 0, tiles - 1)), 0))
               for k in range(ng) for _ in range(n)]
    outs = pl.pallas_call(
        body, name=name,
        grid=(ng, tiles),
        in_specs=[spec] * 3 + g_specs,
        out_specs=[spec] * 4,
        out_shape=[jax.ShapeDtypeStruct((rows, cols), F32)] * 4,
        compiler_params=_params(dimension_semantics=("arbitrary", "arbitrary")),
    )(w2, m2, v2, *gs)
    return [o.reshape(shape) for o in outs]


def _lower_bounds(r0, r1):
    top = jnp.maximum(r0, r1)
    e0, e1 = jnp.exp(r0 - top), jnp.exp(r1 - top)
    p0, p1 = e0 / (e0 + e1), e1 / (e0 + e1)
    return p0 - p0, (p0 + p1) - p0


def _lbs_fwd(lb_raw):
    def body(lb_ref, out_ref):
        l0, l1 = _lower_bounds(lb_ref[0:1, :], lb_ref[1:2, :])
        out_ref[0:1, :] = l0
        out_ref[1:2, :] = l1

    return pl.pallas_call(body, name="lower_bounds", out_shape=jax.ShapeDtypeStruct(lb_raw.shape, F32), compiler_params=_params())(lb_raw)


def _mod_rows(c_all, w_mod, tn=768):
    _, D, cols = w_mod.shape

    def body(c_ref, w_ref, out_ref):
        out_ref[...] = _dot(c_ref[...].astype(BF), w_ref[...].astype(BF))

    return pl.pallas_call(
        body, name="mod_rows",
        grid=(DEPTH,),
        in_specs=[pl.BlockSpec((N_DEV, D), lambda l: (0, 0)), pl.BlockSpec((None, D, cols), lambda l: (l, 0, 0))],
        out_specs=pl.BlockSpec((N_DEV, cols), lambda l: (0, l)),
        out_shape=jax.ShapeDtypeStruct((N_DEV, DEPTH * cols), F32),
        compiler_params=_params(dimension_semantics=("parallel",)),
    )(c_all, w_mod)


def _grad_w_mod(c_all, dmod_cols):
    D = c_all.shape[1]
    cols = dmod_cols.shape[-1]

    def body(c_ref, d_ref, out_ref):
        out_ref[...] = _dot_tn(c_ref[...].astype(BF), d_ref[...].astype(BF))

    return pl.pallas_call(
        body, name="grad_w_mod",
        grid=(DEPTH,),
        in_specs=[pl.BlockSpec((N_DEV, D), lambda l: (0, 0)), pl.BlockSpec((None, N_DEV, cols), lambda l: (l, 0, 0))],
        out_specs=pl.BlockSpec((None, D, cols), lambda l: (l, 0, 0)),
        out_shape=jax.ShapeDtypeStruct((DEPTH, D, cols), F32),
        compiler_params=_params(dimension_semantics=("parallel",)),
    )(c_all, dmod_cols)


def _sum_devices(parts, lb_raw, n_lb):
    _, R, N = parts.shape

    def body(p_ref, lb_ref, out_ref, dlb_ref):
        acc = p_ref[0]
        for d in range(1, N_DEV):
            acc = acc + p_ref[d]
        out_ref[...] = acc
        dl = tuple(acc[0:1, n_lb + l * WIDTH:n_lb + (l + 1) * WIDTH] for l in range(DEPTH))
        _, pull = jax.vjp(_lower_bounds, lb_ref[0:1, :], lb_ref[1:2, :])
        d0, d1 = pull(dl)
        dlb_ref[0:1, :] = d0
        dlb_ref[1:2, :] = d1

    return pl.pallas_call(
        body, name="sum_devices",
        out_shape=[jax.ShapeDtypeStruct((R, N), F32), jax.ShapeDtypeStruct(lb_raw.shape, F32)],
        compiler_params=_params(),
    )(parts, lb_raw)


def _pad_rows(a, rows=8):
    return jnp.concatenate([a, jnp.zeros((rows - a.shape[0],) + a.shape[1:], a.dtype)], axis=0)


def kernel(x, c, w_mod, b_mod, w_in, conv_w, hgrn_norm_w, lower_bounds, w_branch, w_out, ln_g, ln_b, loss_target, m_w_mod, m_b_mod, m_w_in, m_conv_w, m_hgrn_norm_w, m_lower_bounds, m_w_branch, m_w_out, m_ln_g, m_ln_b, v_w_mod, v_b_mod, v_w_in, v_conv_w, v_hgrn_norm_w, v_lower_bounds, v_w_branch, v_w_out, v_ln_g, v_ln_b):
    D = D_MODEL
    x0 = x[0]
    target = loss_target[0]
    S = x0.shape[0]
    mx, my, mc = lax.axis_index("x"), lax.axis_index("y"), lax.axis_index("c")
    chip = 2 * mx + my
    me = 2 * chip + mc
    mod_cols = 3 * D // N_CHIPS

    plan = _Plan(w_in.astype(BF), w_branch.astype(BF), w_out.astype(BF))
    plan.gathered[("in", 0)] = _run_job(_gather_job([plan.local["in"][0]]), "gather_first")[0]

    n_conv = DEPTH * 3 * (WIDTH // N_CHIPS)
    first = jnp.concatenate([c, conv_w.reshape(1, n_conv)], axis=1)
    first = _all_gather8(_pad_rows(first), "gather_c")
    c_all = first[:, 0, :D]
    conv_all = first[:, 0, D:].reshape(N_DEV, DEPTH, 3, WIDTH // N_CHIPS)[0::2]
    mod_part = _all_gather8(_mod_rows(c_all, w_mod), "gather_mod")[0::2]
    mod_part = lax.dynamic_index_in_dim(mod_part, me, axis=1, keepdims=False).reshape(N_CHIPS, DEPTH, mod_cols)
    mods = [(mod_part[:, l].reshape(1, 3 * D) + b_mod[l][None, :]) for l in range(DEPTH)]
    lbs = _lbs_fwd(lower_bounds).reshape(DEPTH, WIDTH // 128, 1, 128)
    loss_blk, dx, small = _local_step(x0, target, mods, lbs, conv_all, hgrn_norm_w, ln_g, ln_b, plan)

    kinds = ("in", "br", "out")
    partial = [_sum4(*plan.scattered[(kind, l)]) for kind in kinds for l in range(DEPTH)]
    sibling = _run_job(_swap_job(partial), "swap_sibling")
    grads = {kind: [[partial[a * DEPTH + l], sibling[a * DEPTH + l]] for l in range(DEPTH)] for a, kind in enumerate(kinds)}

    n_mod, n_nw, n_lb, n_ln, n_cw = DEPTH * 3 * D, DEPTH * 128, DEPTH * WIDTH, DEPTH * D, DEPTH * 3 * WIDTH
    row = jnp.concatenate(
        [jnp.concatenate([small[l][0], small[l][1], small[l][2]], axis=1) for l in range(DEPTH)]
        + [jnp.sum(small[l][3], axis=0) for l in range(DEPTH)]
        + [small[l][4].reshape(1, WIDTH) for l in range(DEPTH)]
        + [small[l][5] for l in range(DEPTH)] + [small[l][6] for l in range(DEPTH)]
        + [jnp.transpose(small[l][7], (1, 0, 2)).reshape(1, 3 * WIDTH) for l in range(DEPTH)]
        + [loss_blk[0:1, :]], axis=1)
    gathered = _all_gather8(_pad_rows(row), "gather_small")
    off_nw = n_mod
    off_lb = off_nw + n_nw
    off_lng = off_lb + n_lb
    off_lnb = off_lng + n_ln
    off_cw = off_lnb + n_ln
    off_loss = off_cw + n_cw
    total, d_lower = _sum_devices(gathered, lower_bounds, off_lb)
    loss = total[0, off_loss]
    d_b_mod = total[0, :n_mod].reshape(DEPTH, 3 * D)
    d_norm_w = total[0, off_nw:off_lb].reshape(DEPTH, 128)
    d_ln_g = total[0, off_lng:off_lnb].reshape(DEPTH, D)
    d_ln_b = total[0, off_lnb:off_cw].reshape(DEPTH, D)
    d_conv = total[0, off_cw:off_loss].reshape(DEPTH, 3, N_CHIPS, WIDTH // N_CHIPS)
    d_conv = lax.dynamic_index_in_dim(d_conv, chip, axis=2, keepdims=False)
    dmod_all = gathered[:, 0, :n_mod].reshape(N_DEV, DEPTH, N_CHIPS, mod_cols)
    dmod_cols = jnp.transpose(lax.dynamic_index_in_dim(dmod_all, chip, axis=2, keepdims=False), (1, 0, 2))
    d_w_mod = _grad_w_mod(c_all, dmod_cols)

    res = {}
    res["w_mod"] = _adamw(w_mod, m_w_mod, v_w_mod, [[d_w_mod]], "adamw_w_mod")
    res["b_mod"] = _adamw(b_mod, m_b_mod, v_b_mod, [[d_b_mod]], "adamw_b_mod")
    res["w_in"] = _adamw(w_in, m_w_in, v_w_in, grads["in"], "adamw_w_in")
    res["conv_w"] = _adamw(conv_w, m_conv_w, v_conv_w, [[d_conv]], "adamw_conv_w")
    res["hgrn_norm_w"] = _adamw(hgrn_norm_w, m_hgrn_norm_w, v_hgrn_norm_w, [[d_norm_w]], "adamw_norm_w")
    res["lower_bounds"] = _adamw(lower_bounds, m_lower_bounds, v_lower_bounds, [[d_lower]], "adamw_lower_bounds")
    res["w_branch"] = _adamw(w_branch, m_w_branch, v_w_branch, grads["br"], "adamw_w_branch")
    res["w_out"] = _adamw(w_out, m_w_out, v_w_out, grads["out"], "adamw_w_out")
    res["ln_g"] = _adamw(ln_g, m_ln_g, v_ln_g, [[d_ln_g]], "adamw_ln_g")
    res["ln_b"] = _adamw(ln_b, m_ln_b, v_ln_b, [[d_ln_b]], "adamw_ln_b")
    names = ["w_mod", "b_mod", "w_in", "conv_w", "hgrn_norm_w", "lower_bounds", "w_branch", "w_out", "ln_g", "ln_b"]
    return (loss, dx[None], *[res[n][0] for n in names], *[res[n][1] for n in names],
            *[res[n][2] for n in names], *[res[n][3] for n in names])


class _Plan:
    def __init__(self, w_in, w_br, w_out):
        self.local = {"in": w_in, "br": w_br, "out": w_out}
        self.gathered, self.grads, self.scattered, self.keys = {}, {}, {}, {}

    def _gather(self, stage, l, keys):
        self.keys[(stage, l)] = ("gather", keys)
        return _gather_job([self.local[kind][layer] for kind, layer in keys])

    def _scatter(self, stage, l, keys):
        self.keys[(stage, l)] = ("scatter", keys)
        return _scatter_job([self.grads[key] for key in keys])

    def job(self, stage, l):
        if stage == "proj_fwd":
            return self._gather(stage, l, [("br", l), ("out", l)])
        if stage == "attn_fwd" and l + 1 < DEPTH:
            return self._gather(stage, l, [("in", l + 1)])
        if stage == "attn_bwd":
            return self._scatter(stage, l, [("out", l), ("br", l)] + ([("in", l + 1)] if l + 1 < DEPTH else []))
        if stage == "proj_bwd" and l == 0:
            return self._scatter(stage, l, [("in", 0)])
        return None

    def done(self, stage, l, outs):
        if outs is None:
            return
        what, keys = self.keys[(stage, l)]
        for n, key in enumerate(keys):
            if what == "gather":
                self.gathered[key] = outs[n]
            else:
                self.scattered[key] = (outs[n], outs[len(keys) + n])


def _local_step(x0, target, mods, lbs, conv_all, hgrn_norm_w, ln_g, ln_b, plan):
    D = D_MODEL
    after, before = _attn_consts()
    hg_consts = _hgrn_consts()

    saved = []
    xl = x0
    for l in range(DEPTH):
        (proj, h), got = _proj_fwd(xl, mods[l], plan.gathered[("in", l)], plan.job("proj_fwd", l))
        plan.done("proj_fwd", l, got)
        (o_a, tot), got = _attn_fwd(proj, after, plan.job("attn_fwd", l))
        plan.done("attn_fwd", l, got)
        o_b, states = _hgrn_fwd(proj, lbs[l], hg_consts)
        ys = _branch_fwd(proj, o_a, o_b, hgrn_norm_w, conv_all, l)
        x_next, merged, y = _merge_fwd(xl, mods[l], proj, ys, plan.gathered[("br", l)], plan.gathered[("out", l)], ln_g, ln_b, l)
        saved.append((xl, proj, h, o_a, tot, o_b, states, ys, merged, y))
        xl = x_next
    dx, loss_blk = _loss_head(xl, target)

    small = [None] * DEPTH
    for l in reversed(range(DEPTH)):
        xin, proj, h, o_a, tot, o_b, states, ys, merged, y = saved[l]
        dx_res, dy, dp, dg, dy_a, dy_b, dy_c, dln_g, dln_b, dgate = _merge_bwd(
            dx, xin, y, mods[l], proj, ys, plan.gathered[("br", l)], plan.gathered[("out", l)], ln_g, l)
        plan.grads[("out", l)] = _grad_w_out(merged, dy)
        g_br = jax.ShapeDtypeStruct((N_CHIPS, 3, WIDTH, D // N_CHIPS), BF)
        for i in range(3):
            g_br = _grad_w(ys[i], dp, (i,), g_br, "grad_w_branch", tm=WIDTH, tn=D // N_CHIPS, b_col0=i * D)
        plan.grads[("br", l)] = g_br
        d_oa, d_ob, dz_a, dz_b, dpre, dpost, du, dz_c, dnorm_w, dconv_w = _branch_bwd(proj, o_a, o_b, hgrn_norm_w, conv_all, dy_a, dy_b, dy_c, l)
        (dq_a, dk_a, dv_a), got = _attn_bwd(proj, d_oa, tot, after, before, plan.job("attn_bwd", l))
        plan.done("attn_bwd", l, got)
        dq_b, df_b, di_b, dlb = _hgrn_bwd(proj, lbs[l], states, d_ob, hg_consts)
        dproj = jnp.concatenate([dq_a, dk_a, dv_a, dz_a, dq_b, df_b, di_b, dz_b, dpre, dpost, du, dz_c, dg], axis=1)
        plan.grads[("in", l)] = _grad_w(h, dproj, (), jax.ShapeDtypeStruct((N_CHIPS, D, IN_COLS // N_CHIPS), BF), "grad_w_in", tm=512, tn=768)
        (dx, dshift, dscale), got = _proj_bwd(dproj, plan.gathered[("in", l)], xin, mods[l], dx_res, plan.job("proj_bwd", l))
        plan.done("proj_bwd", l, got)
        small[l] = (dshift, dscale, dgate, dnorm_w, dlb, dln_g, dln_b, dconv_w)
    return loss_blk, dx, small


def _grad_w_out(merged, dy):
    S, D = merged.shape
    q = D // N_CHIPS

    def body(a_ref, b_ref, o_ref):
        o_ref[...] = _dot_tn(a_ref[...], b_ref[...]).astype(BF)

    return pl.pallas_call(
        body, name="grad_w_out",
        grid=(N_CHIPS,),
        in_specs=[pl.BlockSpec((S, q), lambda j: (0, j)), pl.BlockSpec((S, D), lambda j: (0, 0))],
        out_specs=pl.BlockSpec((None, q, D), lambda j: (j, 0, 0)),
        out_shape=jax.ShapeDtypeStruct((N_CHIPS, q, D), BF),
        compiler_params=_params(dimension_semantics=("parallel",)),
    )(merged, dy)
```

```python
import functools
import math

import numpy as np
import jax
import jax.numpy as jnp
from jax import lax
from jax.experimental import pallas as pl
from jax.experimental.pallas import tpu as pltpu

F32 = jnp.float32
BF = jnp.bfloat16
MESH = pl.DeviceIdType.MESH

DEPTH = 2
D_MODEL = 1024
WIDTH = 512
IN_COLS = 12 * WIDTH + 3 * D_MODEL
N_CHIPS = 4
N_DEV = 8
SB_BLOCK = 128
SB_HEAD_DIM = 64
HG_CHUNK = 64
HG_DIM = 128
LN_EPS = 1e-5
RMS_EPS = 1e-6
ALPHA = (2.0 * DEPTH) ** 0.25
ADAM_LR, ADAM_B1, ADAM_B2, ADAM_EPS, ADAM_WD, ADAM_STEP = 0.001, 0.9, 0.999, 1e-08, 0.01, 10
VMEM_LIMIT = 56 << 20


def _params(**kw):
    return pltpu.CompilerParams(vmem_limit_bytes=VMEM_LIMIT, **kw)


def _dot(a, b):
    return jnp.dot(a, b, preferred_element_type=F32)


def _dot_nt(a, b):
    return lax.dot_general(a, b, (((1,), (1,)), ((), ())), preferred_element_type=F32)


def _dot_tn(a, b):
    return lax.dot_general(a, b, (((0,), (0,)), ((), ())), preferred_element_type=F32)


def _sigmoid(x):
    return 1.0 / (1.0 + jnp.exp(-x))


def _silu(x):
    return x * _sigmoid(x)


def _softplus(z):
    return jnp.maximum(z, 0.0) + jnp.log(1.0 + jnp.exp(-jnp.abs(z)))


def _split_dot(t, g, terms):
    acc = None
    rest = g
    for _ in range(terms):
        part = rest.astype(BF)
        rest = rest - part.astype(F32)
        d = _dot(t, part)
        acc = d if acc is None else acc + d
    return acc


def _split_dot_r(g, t, terms):
    acc = None
    rest = g
    for _ in range(terms):
        part = rest.astype(BF)
        rest = rest - part.astype(F32)
        d = _dot(part, t)
        acc = d if acc is None else acc + d
    return acc


def _standardize(x):
    mu = jnp.mean(x, axis=-1, keepdims=True)
    xc = x - mu
    var = jnp.mean(xc * xc, axis=-1, keepdims=True)
    rstd = lax.rsqrt(var + LN_EPS)
    return xc * rstd, rstd


def _standardize_bwd(dxs, xs, rstd):
    return rstd * (dxs - jnp.mean(dxs, axis=-1, keepdims=True) - xs * jnp.mean(dxs * xs, axis=-1, keepdims=True))


class _Job:
    def __init__(self, ins, outs, sems, make):
        self.ins, self.outs, self.sems, self.make = list(ins), list(outs), list(sems), make


def _flip(v, bit):
    return 1 - v if bit else v


def _exchange_sems(n):
    return [pltpu.SemaphoreType.DMA((n,)), pltpu.SemaphoreType.DMA((n, N_CHIPS - 1)), pltpu.SemaphoreType.DMA((n, N_CHIPS - 1))]


def _gather_job(shards):
    n = len(shards)

    def make(ins, outs, sems):
        local_sems, send_sems, recv_sems = sems
        mx, my, mc = lax.axis_index("x"), lax.axis_index("y"), lax.axis_index("c")
        mine = 2 * mx + my
        copies = []
        for a in range(n):
            copies.append(pltpu.make_async_copy(ins[a], outs[a].at[mine], local_sems.at[a]))
            for k in range(1, N_CHIPS):
                peer = (_flip(mx, k & 2), _flip(my, k & 1), mc)
                copies.append(pltpu.make_async_remote_copy(
                    src_ref=ins[a], dst_ref=outs[a].at[mine], send_sem=send_sems.at[a, k - 1], recv_sem=recv_sems.at[a, k - 1],
                    device_id=peer, device_id_type=MESH))
        return copies

    return _Job(shards, [jax.ShapeDtypeStruct((N_CHIPS,) + s.shape, s.dtype) for s in shards], _exchange_sems(n), make)


def _scatter_job(grads):
    n = len(grads)

    def make(ins, outs, sems):
        local_sems, send_sems, recv_sems = sems
        owns, gots = outs[:n], outs[n:]
        mx, my, mc = lax.axis_index("x"), lax.axis_index("y"), lax.axis_index("c")
        copies = []
        for a in range(n):
            copies.append(pltpu.make_async_copy(ins[a].at[2 * mx + my], owns[a], local_sems.at[a]))
            for k in range(1, N_CHIPS):
                px, py = _flip(mx, k & 2), _flip(my, k & 1)
                copies.append(pltpu.make_async_remote_copy(
                    src_ref=ins[a].at[2 * px + py], dst_ref=gots[a].at[k - 1], send_sem=send_sems.at[a, k - 1],
                    recv_sem=recv_sems.at[a, k - 1], device_id=(px, py, mc), device_id_type=MESH))
        return copies

    outs = [jax.ShapeDtypeStruct(g.shape[1:], g.dtype) for g in grads]
    outs += [jax.ShapeDtypeStruct((N_CHIPS - 1,) + g.shape[1:], g.dtype) for g in grads]
    return _Job(grads, outs, _exchange_sems(n), make)


def _swap_job(parts):
    n = len(parts)

    def make(ins, outs, sems):
        send_sems, recv_sems = sems
        peer = (lax.axis_index("x"), lax.axis_index("y"), 1 - lax.axis_index("c"))
        return [pltpu.make_async_remote_copy(src_ref=ins[a], dst_ref=outs[a], send_sem=send_sems.at[a], recv_sem=recv_sems.at[a],
                                             device_id=peer, device_id_type=MESH) for a in range(n)]

    return _Job(parts, [jax.ShapeDtypeStruct(p.shape, p.dtype) for p in parts],
                [pltpu.SemaphoreType.DMA((n,)), pltpu.SemaphoreType.DMA((n,))], make)


def _run_job(job, name):
    k_in, k_out = len(job.ins), len(job.outs)

    def body(*refs):
        copies = job.make(refs[:k_in], refs[k_in:k_in + k_out], refs[k_in + k_out:])
        for cp in copies:
            cp.start()
        for cp in copies:
            cp.wait()

    hbm = pl.BlockSpec(memory_space=pl.ANY)
    return pl.pallas_call(body, name=name, in_specs=[hbm] * k_in, out_specs=[hbm] * k_out, out_shape=job.outs,
                          scratch_shapes=job.sems, compiler_params=_params())(*job.ins)


def _hosted(body, job, args, *, name, grid, in_specs, out_specs, out_shape, scratch_shapes=(), semantics):
    in_specs, out_specs, out_shape, scratch = list(in_specs), list(out_specs), list(out_shape), list(scratch_shapes)
    if job is None:
        outs = pl.pallas_call(body, name=name, grid=grid, in_specs=in_specs, out_specs=out_specs, out_shape=out_shape,
                              scratch_shapes=scratch, compiler_params=_params(dimension_semantics=semantics))(*args)
        return list(outs), None
    n_in, n_out, n_scr, k_in, k_out = len(in_specs), len(out_specs), len(scratch), len(job.ins), len(job.outs)

    def wrapped(*refs):
        ins, rest = refs[:n_in], refs[n_in:]
        job_ins, rest = rest[:k_in], rest[k_in:]
        outs, rest = rest[:n_out], rest[n_out:]
        job_outs, rest = rest[:k_out], rest[k_out:]
        scr, sems = rest[:n_scr], rest[n_scr:]
        ids = [pl.program_id(a) for a in range(len(grid))]
        first = functools.reduce(jnp.logical_and, [i == 0 for i in ids])
        last = functools.reduce(jnp.logical_and, [i == g - 1 for i, g in zip(ids, grid)])

        @pl.when(first)
        def _():
            for cp in job.make(job_ins, job_outs, sems):
                cp.start()

        body(*ins, *outs, *scr)

        @pl.when(last)
        def _():
            for cp in job.make(job_ins, job_outs, sems):
                cp.wait()

    hbm = pl.BlockSpec(memory_space=pl.ANY)
    outs = pl.pallas_call(
        wrapped, name=name, grid=grid, in_specs=in_specs + [hbm] * k_in, out_specs=out_specs + [hbm] * k_out,
        out_shape=out_shape + job.outs, scratch_shapes=scratch + job.sems,
        compiler_params=_params(dimension_semantics=("arbitrary",) * len(grid)))(*args, *job.ins)
    return list(outs[:n_out]), list(outs[n_out:])


def _proj_fwd(x, mod, wg, job=None, tm=512, tn=768):
    S, D = x.shape
    tm = min(tm, S)
    shard = wg.shape[-1]
    per = shard // tn

    def body(x_ref, mod_ref, w_ref, proj_ref, h_ref, hs):
        @pl.when(pl.program_id(1) == 0)
        def _():
            xs, _ = _standardize(x_ref[...])
            h = xs * (1.0 + mod_ref[:, D:2 * D]) + mod_ref[:, 0:D]
            hb = h.astype(BF)
            hs[...] = hb
            h_ref[...] = hb

        proj_ref[...] = _dot(hs[...], w_ref[...])

    return _hosted(
        body, job, (x, mod, wg), name="proj_fwd",
        grid=(S // tm, IN_COLS // tn),
        in_specs=[pl.BlockSpec((tm, D), lambda i, j: (i, 0)),
                  pl.BlockSpec((1, 3 * D), lambda i, j: (0, 0)),
                  pl.BlockSpec((None, D, tn), lambda i, j: (j // per, 0, j % per))],
        out_specs=[pl.BlockSpec((tm, tn), lambda i, j: (i, j)),
                   pl.BlockSpec((tm, D), lambda i, j: (i, 0))],
        out_shape=[jax.ShapeDtypeStruct((S, IN_COLS), F32), jax.ShapeDtypeStruct((S, D), BF)],
        scratch_shapes=[pltpu.VMEM((tm, D), BF)],
        semantics=("parallel", "arbitrary"))


SB_ROWS = 256
SB_KEYS = 256


def _attn_consts():
    j = np.arange(SB_KEYS)[:, None]
    s = np.arange(SB_KEYS)[None, :]
    from_here = np.concatenate([(j >= s), (j >= s)], axis=0).astype(np.float32)
    up_to = np.concatenate([(j <= s), (j <= s)], axis=0).astype(np.float32)
    return jnp.asarray(from_here, BF), jnp.asarray(up_to, BF)


def _hi_lo(x):
    hi = lax.bitcast_convert_type(lax.bitcast_convert_type(x, jnp.uint32) & jnp.uint32(0xFFFF0000), F32)
    return hi.astype(BF), (x - hi).astype(BF)


def _sums_r(x, t2):
    hi, lo = _hi_lo(x)
    return _dot(jnp.concatenate([hi, lo], axis=1), t2)


def _all_lanes(col, lanes):
    return jnp.broadcast_to(col, (col.shape[0], lanes))


def _attn_rows(ref, r0, rows, head0, scale=None):
    v = ref[pl.ds(r0, rows), :]
    if scale is not None:
        v = v * scale
    return jnp.concatenate([jnp.where(head0, v, 0.0), jnp.where(head0, 0.0, v)], axis=0).astype(BF)


def _attn_scores(q2n, k_ref, kj, t2, from_here_ref, masked):
    c0 = pl.multiple_of(kj * SB_KEYS, SB_KEYS)
    kb = k_ref[pl.ds(c0, SB_KEYS), :].astype(BF)
    zn = _dot_nt(q2n, kb)
    lsb = jnp.minimum(zn, 0.0) - jnp.log(1.0 + jnp.exp(-jnp.abs(zn)))
    valid = None
    if masked:
        valid = (lax.broadcasted_iota(jnp.int32, zn.shape, 1) + kj * SB_KEYS) < t2
        lsb = jnp.where(valid, lsb, 0.0)
    return c0, kb, zn, valid, lsb, _sums_r(lsb, from_here_ref[...])


def _attn_fwd(proj, from_here, job=None):
    S = proj.shape[0]
    TQ = SB_ROWS
    assert S % TQ == 0 and SB_KEYS == TQ
    scale = SB_HEAD_DIM ** -0.5

    def body(q_ref, k_ref, v_ref, from_here_ref, o_ref, tot_ref, run, acc):
        head0 = lax.broadcasted_iota(jnp.int32, (1, 2 * SB_HEAD_DIM), 1) < SB_HEAD_DIM

        def qloop(qi, _):
            r0 = pl.multiple_of(qi * TQ, TQ)
            q2n = _attn_rows(q_ref, r0, TQ, head0, -scale)
            trow = lax.broadcasted_iota(jnp.int32, (TQ, SB_KEYS), 0) + qi * TQ
            t2 = jnp.concatenate([trow, trow], axis=0)
            run[...] = jnp.zeros_like(run)
            acc[...] = jnp.zeros_like(acc)

            def step(kj, masked):
                c0, _, zn, valid, _, sums = _attn_scores(q2n, k_ref, kj, t2, from_here_ref, masked)
                r = run[...]
                e = sums - zn + jnp.concatenate([r, r], axis=1)
                if masked:
                    e = jnp.where(valid, e, -jnp.inf)
                acc[...] += _dot(jnp.exp(e).astype(BF), v_ref[pl.ds(c0, SB_KEYS), :].astype(BF))
                run[...] = r + _all_lanes(sums[:, 0:1], SB_BLOCK)

            step(qi, True)

            def below(n, _):
                step(qi - 1 - n, False)
                return 0

            lax.fori_loop(0, qi, below, 0)
            o_ref[pl.ds(r0, TQ), :] = jnp.where(head0, acc[0:TQ, :], acc[TQ:2 * TQ, :])
            tot_ref[0, pl.ds(r0, TQ), :] = run[0:TQ, :]
            tot_ref[1, pl.ds(r0, TQ), :] = run[TQ:2 * TQ, :]
            return 0

        lax.fori_loop(0, S // TQ, qloop, 0)

    col = lambda base: pl.BlockSpec((S, 128), lambda p, base=base: (0, base + p))
    return _hosted(
        body, job, (proj, proj, proj, from_here), name="attn_fwd",
        grid=(WIDTH // 128,),
        in_specs=[col(0), col(4), col(8), pl.BlockSpec(from_here.shape, lambda p: (0, 0))],
        out_specs=[pl.BlockSpec((S, 128), lambda p: (0, p)), pl.BlockSpec((None, 2, S, 128), lambda p: (p, 0, 0, 0))],
        out_shape=[jax.ShapeDtypeStruct((S, WIDTH), F32), jax.ShapeDtypeStruct((WIDTH // 128, 2, S, 128), F32)],
        scratch_shapes=[pltpu.VMEM((2 * TQ, SB_BLOCK), F32), pltpu.VMEM((2 * TQ, 128), F32)],
        semantics=("parallel",))


def _attn_bwd(proj, d_o, tot, from_here, up_to, job=None):
    S = proj.shape[0]
    TQ = SB_ROWS
    assert S % TQ == 0 and SB_KEYS == TQ
    scale = SB_HEAD_DIM ** -0.5

    def body(q_ref, k_ref, v_ref, do_ref, tot_ref, from_here_ref, up_to_ref, dq_ref, dk_ref, dv_ref, pre, cum, dq_acc, dk_acc, dv_acc):
        head0 = lax.broadcasted_iota(jnp.int32, (1, 2 * SB_HEAD_DIM), 1) < SB_HEAD_DIM
        dk_acc[...] = jnp.zeros_like(dk_acc)
        dv_acc[...] = jnp.zeros_like(dv_acc)

        def qloop(qi, _):
            r0 = pl.multiple_of(qi * TQ, TQ)
            q2n = _attn_rows(q_ref, r0, TQ, head0, -scale)
            do2 = _attn_rows(do_ref, r0, TQ, head0)
            trow = lax.broadcasted_iota(jnp.int32, (TQ, SB_KEYS), 0) + qi * TQ
            t2 = jnp.concatenate([trow, trow], axis=0)
            pre[0:TQ, :] = tot_ref[0, pl.ds(r0, TQ), :]
            pre[TQ:2 * TQ, :] = tot_ref[1, pl.ds(r0, TQ), :]
            cum[...] = jnp.zeros_like(cum)
            dq_acc[...] = jnp.zeros_like(dq_acc)

            def step(kj, masked):
                c0, kb, zn, valid, lsb, sums = _attn_scores(q2n, k_ref, kj, t2, from_here_ref, masked)
                later = pre[...] - _all_lanes(sums[:, 0:1], SB_BLOCK)
                pre[...] = later
                e = sums - zn + jnp.concatenate([later, later], axis=1)
                sig = jnp.exp(lsb - zn)
                if masked:
                    e = jnp.where(valid, e, -jnp.inf)
                    sig = jnp.where(valid, sig, 0.0)
                a = jnp.exp(e)
                w = _dot_nt(do2, v_ref[pl.ds(c0, SB_KEYS), :].astype(BF)) * a
                upto = _sums_r(w, up_to_ref[...])
                c = cum[...]
                dz = w - sig * (upto + jnp.concatenate([c, c], axis=1))
                cum[...] = c + _all_lanes(upto[:, SB_KEYS - 1:SB_KEYS], SB_BLOCK)
                dzb = dz.astype(BF)
                dq_acc[...] += _dot(dzb, kb)
                dk_acc[pl.ds(c0, SB_KEYS), :] += _dot_tn(dzb, q2n)
                dv_acc[pl.ds(c0, SB_KEYS), :] += _dot_tn(a.astype(BF), do2)

            def below(kj, _):
                step(kj, False)
                return 0

            lax.fori_loop(0, qi, below, 0)
            step(qi, True)
            dq_ref[pl.ds(r0, TQ), :] = (jnp.where(head0, dq_acc[0:TQ, :], dq_acc[TQ:2 * TQ, :]) * scale).astype(BF)
            return 0

        lax.fori_loop(0, S // TQ, qloop, 0)
        dk_ref[...] = (-dk_acc[...]).astype(BF)
        dv_ref[...] = dv_acc[...].astype(BF)

    col = lambda base: pl.BlockSpec((S, 128), lambda p, base=base: (0, base + p))
    whole = lambda a: pl.BlockSpec(a.shape, lambda p: (0, 0))
    out = jax.ShapeDtypeStruct((S, WIDTH), BF)
    big = pltpu.VMEM((2 * TQ, SB_BLOCK), F32)
    return _hosted(
        body, job, (proj, proj, proj, d_o, tot, from_here, up_to), name="attn_bwd",
        grid=(WIDTH // 128,),
        in_specs=[col(0), col(4), col(8), col(0), pl.BlockSpec((None, 2, S, 128), lambda p: (p, 0, 0, 0)), whole(from_here), whole(up_to)],
        out_specs=[col(0), col(0), col(0)],
        out_shape=[out, out, out],
        scratch_shapes=[big, big, big, pltpu.VMEM((S, 128), F32), pltpu.VMEM((S, 128), F32)],
        semantics=("parallel",))


HG_LEVELS = (32, 16, 8, 4, 2, 1)


def _hgrn_consts():
    C = HG_CHUNK
    t = np.arange(C)[:, None]
    s = np.arange(C)[None, :]
    rows = [(s <= t), (s > t)]
    masks = [(t == s)]
    for m in HG_LEVELS:
        two = 2 * m
        mid = (t // two) * two + m
        right = (t % two) >= m
        rows.append(right & (s >= mid) & (s <= t))
        rows.append((~right) & (s > t) & (s <= mid - 1))
        masks.append(((t // two) == (s // two)) & right & ((s % two) < m))
    rows.append(np.ones((HG_DIM, C), bool))
    tri = np.concatenate(rows, axis=0).astype(np.float32)
    return (jnp.asarray(tri, BF), jnp.asarray(tri.T.copy(), BF), jnp.asarray(np.stack(masks).astype(np.float32), F32))


HG_SUM_ROWS = (HG_CHUNK,) * (2 + 2 * len(HG_LEVELS)) + (HG_DIM,)


@jax.custom_vjp
def _hgrn_sums(tri, tri_t, g):
    e = _split_dot(tri, g, 3)
    out, r = [], 0
    for n in HG_SUM_ROWS:
        out.append(e[r:r + n])
        r += n
    return tuple(out)


def _hgrn_sums_fwd(tri, tri_t, g):
    return _hgrn_sums(tri, tri_t, g), (tri, tri_t)


def _hgrn_sums_bwd(res, ds):
    tri, tri_t = res
    return jnp.zeros_like(tri), jnp.zeros_like(tri_t), _split_dot(tri_t, jnp.concatenate(ds, axis=0), 3)


_hgrn_sums.defvjp(_hgrn_sums_fwd, _hgrn_sums_bwd)


def _bf_dot(a, b):
    return _dot(a.astype(BF), b.astype(BF))


def _bf_dot_nt(a, b):
    return _dot_nt(a.astype(BF), b.astype(BF))


def _bf_dot_tn(a, b):
    return _dot_tn(a.astype(BF), b.astype(BF))


@jax.custom_vjp
def _mm(a, b):
    return _bf_dot(a, b)


_mm.defvjp(lambda a, b: (_bf_dot(a, b), (a, b)), lambda r, ct: (_bf_dot_nt(ct, r[1]), _bf_dot_tn(r[0], ct)))


@jax.custom_vjp
def _mm_nt(a, b):
    return _bf_dot_nt(a, b)


_mm_nt.defvjp(lambda a, b: (_bf_dot_nt(a, b), (a, b)), lambda r, ct: (_bf_dot(ct, r[1]), _bf_dot_tn(ct, r[0])))


@jax.custom_vjp
def _mm_tn(a, b):
    return _bf_dot_tn(a, b)


_mm_tn.defvjp(lambda a, b: (_bf_dot_tn(a, b), (a, b)), lambda r, ct: (_bf_dot_nt(r[1], ct), _bf_dot(r[0], ct)))


def _hgrn_chunk(tri, tri_t, masks, qraw, fpre, v, st, lb):
    q = _silu(qraw)
    f = lb + (1.0 - lb) * _sigmoid(fpre)
    k = 1.0 - f
    e = _hgrn_sums(tri, tri_t, jnp.log(f))
    prefix, suffix, whole = e[0], e[1], e[-1]
    scores = masks[0] * _mm_nt(q, k)
    for n in range(len(HG_LEVELS)):
        scores = scores + masks[n + 1] * _mm_nt(q * jnp.exp(e[2 + 2 * n]), k * jnp.exp(e[3 + 2 * n]))
    o = _mm_nt(q * jnp.exp(prefix), st) + _mm(scores, v)
    st_new = st * jnp.exp(whole) + _mm_tn(v, k * jnp.exp(suffix))
    return o, st_new


def _hgrn_specs(S, consts):
    col = lambda base: pl.BlockSpec((S, 128), lambda p, base=base: (0, base + p))
    whole = [pl.BlockSpec(a.shape, lambda p, n=a.ndim: (0,) * n) for a in consts]
    return col, whole


def _hgrn_fwd(proj, lbs, consts):
    S = proj.shape[0]
    nc = S // HG_CHUNK

    def body(q_ref, f_ref, i_ref, lb_ref, tri_ref, trit_ref, mask_ref, o_ref, st_ref):
        tri, tri_t = tri_ref[...], trit_ref[...]
        masks = [mask_ref[n] for n in range(len(HG_LEVELS) + 1)]
        lb = lb_ref[...]

        def chunk(ci, st):
            r0 = pl.multiple_of(ci * HG_CHUNK, HG_CHUNK)
            rows = pl.ds(r0, HG_CHUNK)
            st_ref[ci] = st
            o, st_new = _hgrn_chunk(tri, tri_t, masks, q_ref[rows, :], f_ref[rows, :], i_ref[rows, :], st, lb)
            o_ref[rows, :] = o
            return st_new

        lax.fori_loop(0, nc, chunk, jnp.zeros((HG_DIM, HG_DIM), F32))

    col, whole = _hgrn_specs(S, consts)
    return pl.pallas_call(
        body, name="hgrn_fwd",
        grid=(WIDTH // 128,),
        in_specs=[col(16), col(20), col(24), pl.BlockSpec((None, 1, 128), lambda p: (p, 0, 0))] + whole,
        out_specs=[col(0), pl.BlockSpec((None, nc, HG_DIM, HG_DIM), lambda p: (p, 0, 0, 0))],
        out_shape=[jax.ShapeDtypeStruct((S, WIDTH), F32), jax.ShapeDtypeStruct((WIDTH // 128, nc, HG_DIM, HG_DIM), F32)],
        compiler_params=_params(dimension_semantics=("parallel",)),
    )(proj, proj, proj, lbs, *consts)


def _hgrn_bwd(proj, lbs, states, d_o, consts):
    S = proj.shape[0]
    nc = S // HG_CHUNK

    def body(q_ref, f_ref, i_ref, lb_ref, st_ref, do_ref, tri_ref, trit_ref, mask_ref, dq_ref, df_ref, di_ref, dlb_ref):
        masks = [mask_ref[n] for n in range(len(HG_LEVELS) + 1)]
        lb = lb_ref[...]
        fn = functools.partial(_hgrn_chunk, tri_ref[...], trit_ref[...], masks)

        def chunk(n, carry):
            d_st, dlb = carry
            ci = nc - 1 - n
            r0 = pl.multiple_of(ci * HG_CHUNK, HG_CHUNK)
            rows = pl.ds(r0, HG_CHUNK)
            _, pull = jax.vjp(fn, q_ref[rows, :], f_ref[rows, :], i_ref[rows, :], st_ref[ci], lb)
            dq, df, di, d_prev, dl = pull((do_ref[rows, :], d_st))
            dq_ref[rows, :] = dq.astype(BF)
            df_ref[rows, :] = df.astype(BF)
            di_ref[rows, :] = di.astype(BF)
            return d_prev, dlb + dl

        _, dlb = lax.fori_loop(0, nc, chunk, (jnp.zeros((HG_DIM, HG_DIM), F32), jnp.zeros((1, HG_DIM), F32)))
        dlb_ref[...] = dlb

    col, whole = _hgrn_specs(S, consts)
    head = pl.BlockSpec((None, 1, 128), lambda p: (p, 0, 0))
    out = jax.ShapeDtypeStruct((S, WIDTH), BF)
    return pl.pallas_call(
        body, name="hgrn_bwd",
        grid=(WIDTH // 128,),
        in_specs=[col(16), col(20), col(24), head, pl.BlockSpec((None, nc, HG_DIM, HG_DIM), lambda p: (p, 0, 0, 0)), col(0)] + whole,
        out_specs=[col(0), col(0), col(0), head],
        out_shape=[out, out, out, jax.ShapeDtypeStruct((WIDTH // 128, 1, 128), F32)],
        compiler_params=_params(dimension_semantics=("parallel",)),
    )(proj, proj, proj, lbs, states, d_o, *consts)


def _shift_down(x, n):
    rows = lax.broadcasted_iota(jnp.int32, x.shape, 0)
    return jnp.where(rows >= n, pltpu.roll(x, n, 0), 0.0)


def _shift_up(x, n):
    S = x.shape[0]
    rows = lax.broadcasted_iota(jnp.int32, x.shape, 0)
    return jnp.where(rows < S - n, pltpu.roll(x, S - n, 0), 0.0)


def _branch_fwd(proj, o_a, o_b, norm_w, conv_w, layer):
    S = proj.shape[0]

    def body(oa_ref, za_ref, ob_ref, zb_ref, nw_ref, pre_ref, post_ref, u_ref, zc_ref, cw_ref, ya_ref, yb_ref, yc_ref):
        ya_ref[...] = (oa_ref[...] * _silu(za_ref[...])).astype(BF)
        ob = ob_ref[...]
        rn = lax.rsqrt(jnp.mean(ob * ob, axis=-1, keepdims=True) + RMS_EPS)
        yb_ref[...] = (ob * rn * nw_ref[layer:layer + 1, :] * _silu(zb_ref[...])).astype(BF)
        pu = pre_ref[...] * u_ref[...]
        conv = cw_ref[2:3, :] * pu + cw_ref[1:2, :] * _shift_down(pu, 1) + cw_ref[0:1, :] * _shift_down(pu, 2)
        yc_ref[...] = (post_ref[...] * conv * _silu(zc_ref[...])).astype(BF)

    col = lambda base: pl.BlockSpec((S, 128), lambda p, base=base: (0, base + p))
    out = jax.ShapeDtypeStruct((S, WIDTH), BF)
    return pl.pallas_call(
        body, name="branch_fwd",
        grid=(WIDTH // 128,),
        in_specs=[col(0), col(12), col(0), col(28), pl.BlockSpec(norm_w.shape, lambda p: (0, 0)),
                  col(32), col(36), col(40), col(44), pl.BlockSpec((None, None, 3, 128), lambda p: (p, layer, 0, 0))],
        out_specs=[col(0), col(0), col(0)],
        out_shape=[out, out, out],
        compiler_params=_params(dimension_semantics=("parallel",)),
    )(o_a, proj, o_b, proj, norm_w, proj, proj, proj, proj, conv_w)


def _branch_bwd(proj, o_a, o_b, norm_w, conv_w, dy_a, dy_b, dy_c, layer):
    S = proj.shape[0]

    def dsilu(z):
        s = _sigmoid(z)
        return s * z, s * (1.0 + z * (1.0 - s))

    def body(oa_ref, za_ref, ob_ref, zb_ref, nw_ref, pre_ref, post_ref, u_ref, zc_ref, cw_ref, dya_ref, dyb_ref, dyc_ref,
             doa_ref, dob_ref, dza_ref, dzb_ref, dpre_ref, dpost_ref, du_ref, dzc_ref, dnw_ref, dcw_ref):
        dya = dya_ref[...]
        sa, dsa = dsilu(za_ref[...])
        doa_ref[...] = dya * sa
        dza_ref[...] = (dya * oa_ref[...] * dsa).astype(BF)

        dyb = dyb_ref[...]
        ob = ob_ref[...]
        nw = nw_ref[layer:layer + 1, :]
        sb, dsb = dsilu(zb_ref[...])
        rn = lax.rsqrt(jnp.mean(ob * ob, axis=-1, keepdims=True) + RMS_EPS)
        on = ob * rn
        dzb_ref[...] = (dyb * on * nw * dsb).astype(BF)
        don_w = dyb * sb
        dnw_ref[...] = jnp.sum(don_w * on, axis=0, keepdims=True)
        don = don_w * nw
        dob_ref[...] = rn * (don - on * jnp.mean(don * on, axis=-1, keepdims=True))

        dyc = dyc_ref[...]
        pre, post, u = pre_ref[...], post_ref[...], u_ref[...]
        sc, dsc = dsilu(zc_ref[...])
        pu = pre * u
        pu1, pu2 = _shift_down(pu, 1), _shift_down(pu, 2)
        conv = cw_ref[2:3, :] * pu + cw_ref[1:2, :] * pu1 + cw_ref[0:1, :] * pu2
        dzc_ref[...] = (dyc * post * conv * dsc).astype(BF)
        dpost_ref[...] = (dyc * conv * sc).astype(BF)
        dconv = dyc * post * sc
        dcw_ref[0:1, :] = jnp.sum(dconv * pu2, axis=0, keepdims=True)
        dcw_ref[1:2, :] = jnp.sum(dconv * pu1, axis=0, keepdims=True)
        dcw_ref[2:3, :] = jnp.sum(dconv * pu, axis=0, keepdims=True)
        dpu = cw_ref[2:3, :] * dconv + cw_ref[1:2, :] * _shift_up(dconv, 1) + cw_ref[0:1, :] * _shift_up(dconv, 2)
        dpre_ref[...] = (dpu * u).astype(BF)
        du_ref[...] = (dpu * pre).astype(BF)

    col = lambda base: pl.BlockSpec((S, 128), lambda p, base=base: (0, base + p))
    f32 = jax.ShapeDtypeStruct((S, WIDTH), F32)
    bf = jax.ShapeDtypeStruct((S, WIDTH), BF)
    return pl.pallas_call(
        body, name="branch_bwd",
        grid=(WIDTH // 128,),
        in_specs=[col(0), col(12), col(0), col(28), pl.BlockSpec(norm_w.shape, lambda p: (0, 0)),
                  col(32), col(36), col(40), col(44), pl.BlockSpec((None, None, 3, 128), lambda p: (p, layer, 0, 0)),
                  col(0), col(0), col(0)],
        out_specs=[col(0)] * 8 + [pl.BlockSpec((None, 1, 128), lambda p: (p, 0, 0)), pl.BlockSpec((None, 3, 128), lambda p: (p, 0, 0))],
        out_shape=[f32, f32, bf, bf, bf, bf, bf, bf, jax.ShapeDtypeStruct((WIDTH // 128, 1, 128), F32),
                   jax.ShapeDtypeStruct((WIDTH // 128, 3, 128), F32)],
        compiler_params=_params(dimension_semantics=("parallel",)),
    )(o_a, proj, o_b, proj, norm_w, proj, proj, proj, proj, conv_w, dy_a, dy_b, dy_c)


def _branch_proj(y_refs, wb_ref):
    out = []
    for i in range(3):
        yv = y_refs[i][...]
        out.append(jnp.concatenate([_dot(yv, wb_ref[j, i]) for j in range(N_CHIPS)], axis=1))
    return out


def _merge_fwd(x, mod, proj, ys, wb, wo, ln_g, ln_b, layer, tm=256):
    S, D = x.shape
    tm = min(tm, S)

    def body(x_ref, mod_ref, ga_ref, gb_ref, gc_ref, ya_ref, yb_ref, yc_ref, wb_ref, wo_ref, g_ref, b_ref, xo_ref, mg_ref, y_ref):
        ps = _branch_proj((ya_ref, yb_ref, yc_ref), wb_ref)
        merged = _sigmoid(ga_ref[...]) * ps[0] + _sigmoid(gb_ref[...]) * ps[1] + _sigmoid(gc_ref[...]) * ps[2]
        mb = merged.astype(BF)
        mg_ref[...] = mb
        y = _dot(mb, wo_ref[...].reshape(D, D))
        y_ref[...] = y
        r = ALPHA * x_ref[...] + (1.0 + mod_ref[:, 2 * D:3 * D]) * y
        xn, _ = _standardize(r)
        xo_ref[...] = xn * g_ref[layer:layer + 1, :] + b_ref[layer:layer + 1, :]

    row = lambda w, c=0: pl.BlockSpec((tm, w), lambda i, c=c: (i, c))
    whole = lambda a: pl.BlockSpec(a.shape, lambda i, n=a.ndim: (0,) * n)
    return pl.pallas_call(
        body, name="merge_fwd",
        grid=(S // tm,),
        in_specs=[row(D), whole(mod), row(D, 6), row(D, 7), row(D, 8), row(WIDTH), row(WIDTH), row(WIDTH), whole(wb),
                  whole(wo), whole(ln_g), whole(ln_b)],
        out_specs=[row(D), row(D), row(D)],
        out_shape=[jax.ShapeDtypeStruct((S, D), F32), jax.ShapeDtypeStruct((S, D), BF), jax.ShapeDtypeStruct((S, D), F32)],
        compiler_params=_params(dimension_semantics=("parallel",)),
    )(x, mod, proj, proj, proj, *ys, wb, wo, ln_g, ln_b)


def _merge_bwd(dxo, x, y, mod, proj, ys, wb, wo, ln_g, layer, tm=256):
    S, D = x.shape
    tm = min(tm, S)

    def body(dxo_ref, x_ref, y_ref, mod_ref, ga_ref, gb_ref, gc_ref, ya_ref, yb_ref, yc_ref, wb_ref, wo_ref, g_ref,
             dxr_ref, dy_ref, dp_ref, dg_ref, dya_ref, dyb_ref, dyc_ref, dlg_ref, dlb_ref, dgt_ref):
        @pl.when(pl.program_id(0) == 0)
        def _():
            dlg_ref[...] = jnp.zeros_like(dlg_ref)
            dlb_ref[...] = jnp.zeros_like(dlb_ref)
            dgt_ref[...] = jnp.zeros_like(dgt_ref)

        gate1 = 1.0 + mod_ref[:, 2 * D:3 * D]
        yv = y_ref[...]
        xn, rstd = _standardize(ALPHA * x_ref[...] + gate1 * yv)
        dxo = dxo_ref[...]
        dlg_ref[...] += jnp.sum(dxo * xn, axis=0, keepdims=True)
        dlb_ref[...] += jnp.sum(dxo, axis=0, keepdims=True)
        dr = _standardize_bwd(dxo * g_ref[layer:layer + 1, :], xn, rstd)
        dxr_ref[...] = ALPHA * dr
        dgt_ref[...] += jnp.sum(dr * yv, axis=0, keepdims=True)
        dyb = (gate1 * dr).astype(BF)
        dy_ref[...] = dyb
        dmerged = _dot_nt(dyb, wo_ref[...].reshape(D, D))
        ps = _branch_proj((ya_ref, yb_ref, yc_ref), wb_ref)
        quarter = D // N_CHIPS
        for i, (gate_ref, out_ref) in enumerate(((ga_ref, dya_ref), (gb_ref, dyb_ref), (gc_ref, dyc_ref))):
            sg = _sigmoid(gate_ref[...])
            dg_ref[:, i * D:(i + 1) * D] = (dmerged * ps[i] * sg * (1.0 - sg)).astype(BF)
            dp = (dmerged * sg).astype(BF)
            dp_ref[:, i * D:(i + 1) * D] = dp
            acc = None
            for j in range(N_CHIPS):
                t = _dot_nt(dp[:, j * quarter:(j + 1) * quarter], wb_ref[j, i])
                acc = t if acc is None else acc + t
            out_ref[...] = acc

    row = lambda w, c=0: pl.BlockSpec((tm, w), lambda i, c=c: (i, c))
    whole = lambda a: pl.BlockSpec(a.shape, lambda i, n=a.ndim: (0,) * n)
    vec = pl.BlockSpec((1, D), lambda i: (0, 0))
    sd = jax.ShapeDtypeStruct
    return pl.pallas_call(
        body, name="merge_bwd",
        grid=(S // tm,),
        in_specs=[row(D), row(D), row(D), whole(mod), row(D, 6), row(D, 7), row(D, 8), row(WIDTH), row(WIDTH), row(WIDTH), whole(wb),
                  whole(wo), whole(ln_g)],
        out_specs=[row(D), row(D), row(3 * D), row(3 * D), row(WIDTH), row(WIDTH), row(WIDTH), vec, vec, vec],
        out_shape=[sd((S, D), F32), sd((S, D), BF), sd((S, 3 * D), BF), sd((S, 3 * D), BF), sd((S, WIDTH), F32), sd((S, WIDTH), F32),
                   sd((S, WIDTH), F32), sd((1, D), F32), sd((1, D), F32), sd((1, D), F32)],
        compiler_params=_params(dimension_semantics=("arbitrary",)),
    )(dxo, x, y, mod, proj, proj, proj, *ys, wb, wo, ln_g)


def _loss_head(x, target, tm=512):
    S, D = x.shape
    tm = min(tm, S)

    def body(x_ref, t_ref, dx_ref, loss_ref):
        @pl.when(pl.program_id(0) == 0)
        def _():
            loss_ref[...] = jnp.zeros_like(loss_ref)

        err = x_ref[...] - t_ref[...]
        dx_ref[...] = err * (1.0 / D)
        loss_ref[...] += 0.5 * jnp.sum(jnp.mean(err * err, axis=-1, keepdims=True))

    row = pl.BlockSpec((tm, D), lambda i: (i, 0))
    return pl.pallas_call(
        body, name="loss_head",
        grid=(S // tm,),
        in_specs=[row, row],
        out_specs=[row, pl.BlockSpec((8, 128), lambda i: (0, 0))],
        out_shape=[jax.ShapeDtypeStruct((S, D), F32), jax.ShapeDtypeStruct((8, 128), F32)],
        compiler_params=_params(dimension_semantics=("arbitrary",)),
    )(x, target)


def _proj_bwd(dproj, wg, x, mod, dx_res, job=None, tm=512, tk=768):
    S, D = x.shape
    tm = min(tm, S)
    shard = wg.shape[-1]
    per = shard // tk
    nk = IN_COLS // tk

    def body(dp_ref, w_ref, x_ref, mod_ref, dxr_ref, dx_ref, dsh_ref, dsc_ref, acc):
        i, k = pl.program_id(0), pl.program_id(1)

        @pl.when((i == 0) & (k == 0))
        def _():
            dsh_ref[...] = jnp.zeros_like(dsh_ref)
            dsc_ref[...] = jnp.zeros_like(dsc_ref)

        @pl.when(k == 0)
        def _():
            acc[...] = jnp.zeros_like(acc)

        acc[...] += _dot_nt(dp_ref[...], w_ref[...])

        @pl.when(k == nk - 1)
        def _():
            dh = acc[...]
            xs, rstd = _standardize(x_ref[...])
            dsh_ref[...] += jnp.sum(dh, axis=0, keepdims=True)
            dsc_ref[...] += jnp.sum(dh * xs, axis=0, keepdims=True)
            dx_ref[...] = _standardize_bwd(dh * (1.0 + mod_ref[:, D:2 * D]), xs, rstd) + dxr_ref[...]

    row = pl.BlockSpec((tm, D), lambda i, k: (i, 0))
    vec = pl.BlockSpec((1, D), lambda i, k: (0, 0))
    return _hosted(
        body, job, (dproj, wg, x, mod, dx_res), name="proj_bwd",
        grid=(S // tm, nk),
        in_specs=[pl.BlockSpec((tm, tk), lambda i, k: (i, k)),
                  pl.BlockSpec((None, D, tk), lambda i, k: (k // per, 0, k % per)),
                  row, pl.BlockSpec((1, 3 * D), lambda i, k: (0, 0)), row],
        out_specs=[row, vec, vec],
        out_shape=[jax.ShapeDtypeStruct((S, D), F32), jax.ShapeDtypeStruct((1, D), F32), jax.ShapeDtypeStruct((1, D), F32)],
        scratch_shapes=[pltpu.VMEM((tm, D), F32)],
        semantics=("arbitrary", "arbitrary"))


def _grad_w(a, b, mid, prev, name, tm, tn, b_col0=0):
    S, M = a.shape
    shape = prev.shape
    n_shard = shape[-1]
    per = n_shard // tn
    nm = M // tm
    lead = len(mid)

    def body(*refs):
        a_ref, b_ref, o_ref = refs[0], refs[1], refs[-1]
        o_ref[...] = _dot_tn(a_ref[...], b_ref[...]).astype(BF)

    in_specs = [pl.BlockSpec((S, tm), lambda m, n: (0, m)),
                pl.BlockSpec((S, tn), lambda m, n: (0, b_col0 // tn + n))]
    args = [a, b]
    aliases = {}
    if not isinstance(prev, jax.ShapeDtypeStruct):
        in_specs.append(pl.BlockSpec(memory_space=pl.ANY))
        args.append(prev)
        aliases = {2: 0}
    return pl.pallas_call(
        body, name=name,
        grid=(nm, N_CHIPS * per),
        in_specs=in_specs,
        out_specs=pl.BlockSpec((None,) + (None,) * lead + (tm, tn), lambda m, n: (n // per,) + tuple(mid) + (m, n % per)),
        out_shape=jax.ShapeDtypeStruct(shape, BF),
        input_output_aliases=aliases,
        compiler_params=_params(dimension_semantics=("parallel", "parallel")),
    )(*args)


def _all_gather8(x, name):
    R, N = x.shape

    def body(x_ref, out_ref, send_sems, recv_sems):
        mx, my, mc = lax.axis_index("x"), lax.axis_index("y"), lax.axis_index("c")
        me = 4 * mx + 2 * my + mc
        out_ref[me] = x_ref[...]
        copies = []
        for k in range(1, N_DEV):
            peer = (_flip(mx, k & 4), _flip(my, k & 2), _flip(mc, k & 1))
            cp = pltpu.make_async_remote_copy(src_ref=x_ref, dst_ref=out_ref.at[me], send_sem=send_sems.at[k - 1],
                                              recv_sem=recv_sems.at[k - 1], device_id=peer, device_id_type=MESH)
            cp.start()
            copies.append(cp)
        for cp in copies:
            cp.wait()

    return pl.pallas_call(
        body, name=name,
        in_specs=[pl.BlockSpec(memory_space=pltpu.VMEM)],
        out_specs=pl.BlockSpec(memory_space=pltpu.VMEM),
        out_shape=jax.ShapeDtypeStruct((N_DEV, R, N), F32),
        scratch_shapes=[pltpu.SemaphoreType.DMA((N_DEV - 1,)), pltpu.SemaphoreType.DMA((N_DEV - 1,))],
        compiler_params=_params(),
    )(x)


def _rows2d(a):
    return a.reshape(-1, a.shape[-1])


def _tile_rows(rows, cols, n_arrays):
    budget = (24 << 20) // (n_arrays * 2 * 4 * cols)
    if rows <= budget:
        return rows
    tm = 8
    for cand in range(8, budget + 1, 8):
        if rows % cand == 0:
            tm = cand
    return tm


def _sum4(own, got):
    o2 = _rows2d(own)
    rows, cols = o2.shape
    g3 = got.reshape(3, rows, cols)
    tm = _tile_rows(rows, cols, 3)

    def body(o_ref, g_ref, out_ref):
        out_ref[...] = ((o_ref[...].astype(F32) + g_ref[0].astype(F32)) + g_ref[1].astype(F32)) + g_ref[2].astype(F32)

    out = pl.pallas_call(
        body, name="sum_partials",
        grid=(rows // tm,),
        in_specs=[pl.BlockSpec((tm, cols), lambda i: (i, 0)), pl.BlockSpec((3, tm, cols), lambda i: (0, i, 0))],
        out_specs=pl.BlockSpec((tm, cols), lambda i: (i, 0)),
        out_shape=jax.ShapeDtypeStruct((rows, cols), F32),
        compiler_params=_params(dimension_semantics=("parallel",)),
    )(o2, g3)
    return out.reshape(own.shape)


def _adamw(w, m, v, groups, name):
    shape = w.shape
    w2, m2, v2 = _rows2d(w), _rows2d(m), _rows2d(v)
    rows, cols = w2.shape
    ng = len(groups)
    n = len(groups[0])
    slab = rows // ng
    gs = [_rows2d(g) for grp in groups for g in grp]
    tm = _tile_rows(slab, cols, 7 + n)
    tiles = slab // tm
    c1 = 1.0 / (1.0 - ADAM_B1 ** ADAM_STEP)
    c2 = 1.0 / (1.0 - ADAM_B2 ** ADAM_STEP)

    def body(*refs):
        w_ref, m_ref, v_ref = refs[:3]
        g_refs = refs[3:3 + ng * n]
        go_ref, d_ref, mo_ref, vo_ref = refs[3 + ng * n:]
        which = pl.program_id(0)
        for s in range(ng):
            @pl.when(which == s)
            def _(s=s):
                g = g_refs[s * n][...]
                for r in g_refs[s * n + 1:(s + 1) * n]:
                    g = g + r[...]
                mn = ADAM_B1 * m_ref[...] + (1.0 - ADAM_B1) * g
                vn = ADAM_B2 * v_ref[...] + (1.0 - ADAM_B2) * (g * g)
                go_ref[...] = g
                mo_ref[...] = mn
                vo_ref[...] = vn
                d_ref[...] = -ADAM_LR * ((mn * c1) / (jnp.sqrt(vn * c2) + ADAM_EPS) + ADAM_WD * w_ref[...])

    spec = pl.BlockSpec((tm, cols), lambda s, i: (s * tiles + i, 0))
    g_specs = [pl.BlockSpec((tm, cols), lambda s, i, k=k: (jnp.where(s == k, i, jnp.where(s < k, 0, tiles - 1)), 0))
               for k in range(ng) for _ in range(n)]
    outs = pl.pallas_call(
        body, name=name,
        grid=(ng, tiles),
        in_specs=[spec] * 3 + g_specs,
        out_specs=[spec] * 4,
        out_shape=[jax.ShapeDtypeStruct((rows, cols), F32)] * 4,
        compiler_params=_params(dimension_semantics=("arbitrary", "arbitrary")),
    )(w2, m2, v2, *gs)
    return [o.reshape(shape) for o in outs]


def _lower_bounds(r0, r1):
    top = jnp.maximum(r0, r1)
    e0, e1 = jnp.exp(r0 - top), jnp.exp(r1 - top)
    p0, p1 = e0 / (e0 + e1), e1 / (e0 + e1)
    return p0 - p0, (p0 + p1) - p0


def _lbs_fwd(lb_raw):
    def body(lb_ref, out_ref):
        l0, l1 = _lower_bounds(lb_ref[0:1, :], lb_ref[1:2, :])
        out_ref[0:1, :] = l0
        out_ref[1:2, :] = l1

    return pl.pallas_call(body, name="lower_bounds", out_shape=jax.ShapeDtypeStruct(lb_raw.shape, F32), compiler_params=_params())(lb_raw)


def _mod_rows(c_all, w_mod, tn=768):
    _, D, cols = w_mod.shape

    def body(c_ref, w_ref, out_ref):
        out_ref[...] = _dot(c_ref[...].astype(BF), w_ref[...].astype(BF))

    return pl.pallas_call(
        body, name="mod_rows",
        grid=(DEPTH,),
        in_specs=[pl.BlockSpec((N_DEV, D), lambda l: (0, 0)), pl.BlockSpec((None, D, cols), lambda l: (l, 0, 0))],
        out_specs=pl.BlockSpec((N_DEV, cols), lambda l: (0, l)),
        out_shape=jax.ShapeDtypeStruct((N_DEV, DEPTH * cols), F32),
        compiler_params=_params(dimension_semantics=("parallel",)),
    )(c_all, w_mod)


def _grad_w_mod(c_all, dmod_cols):
    D = c_all.shape[1]
    cols = dmod_cols.shape[-1]

    def body(c_ref, d_ref, out_ref):
        out_ref[...] = _dot_tn(c_ref[...].astype(BF), d_ref[...].astype(BF))

    return pl.pallas_call(
        body, name="grad_w_mod",
        grid=(DEPTH,),
        in_specs=[pl.BlockSpec((N_DEV, D), lambda l: (0, 0)), pl.BlockSpec((None, N_DEV, cols), lambda l: (l, 0, 0))],
        out_specs=pl.BlockSpec((None, D, cols), lambda l: (l, 0, 0)),
        out_shape=jax.ShapeDtypeStruct((DEPTH, D, cols), F32),
        compiler_params=_params(dimension_semantics=("parallel",)),
    )(c_all, dmod_cols)


def _sum_devices(parts, lb_raw, n_lb):
    _, R, N = parts.shape

    def body(p_ref, lb_ref, out_ref, dlb_ref):
        acc = p_ref[0]
        for d in range(1, N_DEV):
            acc = acc + p_ref[d]
        out_ref[...] = acc
        dl = tuple(acc[0:1, n_lb + l * WIDTH:n_lb + (l + 1) * WIDTH] for l in range(DEPTH))
        _, pull = jax.vjp(_lower_bounds, lb_ref[0:1, :], lb_ref[1:2, :])
        d0, d1 = pull(dl)
        dlb_ref[0:1, :] = d0
        dlb_ref[1:2, :] = d1

    return pl.pallas_call(
        body, name="sum_devices",
        out_shape=[jax.ShapeDtypeStruct((R, N), F32), jax.ShapeDtypeStruct(lb_raw.shape, F32)],
        compiler_params=_params(),
    )(parts, lb_raw)


def _pad_rows(a, rows=8):
    return jnp.concatenate([a, jnp.zeros((rows - a.shape[0],) + a.shape[1:], a.dtype)], axis=0)


def kernel(x, c, w_mod, b_mod, w_in, conv_w, hgrn_norm_w, lower_bounds, w_branch, w_out, ln_g, ln_b, loss_target, m_w_mod, m_b_mod, m_w_in, m_conv_w, m_hgrn_norm_w, m_lower_bounds, m_w_branch, m_w_out, m_ln_g, m_ln_b, v_w_mod, v_b_mod, v_w_in, v_conv_w, v_hgrn_norm_w, v_lower_bounds, v_w_branch, v_w_out, v_ln_g, v_ln_b):
    D = D_MODEL
    x0 = x[0]
    target = loss_target[0]
    S = x0.shape[0]
    mx, my, mc = lax.axis_index("x"), lax.axis_index("y"), lax.axis_index("c")
    chip = 2 * mx + my
    me = 2 * chip + mc
    mod_cols = 3 * D // N_CHIPS

    plan = _Plan(w_in.astype(BF), w_branch.astype(BF), w_out.astype(BF))
    plan.gathered[("in", 0)] = _run_job(_gather_job([plan.local["in"][0]]), "gather_first")[0]

    n_conv = DEPTH * 3 * (WIDTH // N_CHIPS)
    first = jnp.concatenate([c, conv_w.reshape(1, n_conv)], axis=1)
    first = _all_gather8(_pad_rows(first), "gather_c")
    c_all = first[:, 0, :D]
    conv_all = first[:, 0, D:].reshape(N_DEV, DEPTH, 3, WIDTH // N_CHIPS)[0::2]
    mod_part = _all_gather8(_mod_rows(c_all, w_mod), "gather_mod")[0::2]
    mod_part = lax.dynamic_index_in_dim(mod_part, me, axis=1, keepdims=False).reshape(N_CHIPS, DEPTH, mod_cols)
    mods = [(mod_part[:, l].reshape(1, 3 * D) + b_mod[l][None, :]) for l in range(DEPTH)]
    lbs = _lbs_fwd(lower_bounds).reshape(DEPTH, WIDTH // 128, 1, 128)
    loss_blk, dx, small = _local_step(x0, target, mods, lbs, conv_all, hgrn_norm_w, ln_g, ln_b, plan)

    kinds = ("in", "br", "out")
    partial = [_sum4(*plan.scattered[(kind, l)]) for kind in kinds for l in range(DEPTH)]
    sibling = _run_job(_swap_job(partial), "swap_sibling")
    grads = {kind: [[partial[a * DEPTH + l], sibling[a * DEPTH + l]] for l in range(DEPTH)] for a, kind in enumerate(kinds)}

    n_mod, n_nw, n_lb, n_ln, n_cw = DEPTH * 3 * D, DEPTH * 128, DEPTH * WIDTH, DEPTH * D, DEPTH * 3 * WIDTH
    row = jnp.concatenate(
        [jnp.concatenate([small[l][0], small[l][1], small[l][2]], axis=1) for l in range(DEPTH)]
        + [jnp.sum(small[l][3], axis=0) for l in range(DEPTH)]
        + [small[l][4].reshape(1, WIDTH) for l in range(DEPTH)]
        + [small[l][5] for l in range(DEPTH)] + [small[l][6] for l in range(DEPTH)]
        + [jnp.transpose(small[l][7], (1, 0, 2)).reshape(1, 3 * WIDTH) for l in range(DEPTH)]
        + [loss_blk[0:1, :]], axis=1)
    gathered = _all_gather8(_pad_rows(row), "gather_small")
    off_nw = n_mod
    off_lb = off_nw + n_nw
    off_lng = off_lb + n_lb
    off_lnb = off_lng + n_ln
    off_cw = off_lnb + n_ln
    off_loss = off_cw + n_cw
    total, d_lower = _sum_devices(gathered, lower_bounds, off_lb)
    loss = total[0, off_loss]
    d_b_mod = total[0, :n_mod].reshape(DEPTH, 3 * D)
    d_norm_w = total[0, off_nw:off_lb].reshape(DEPTH, 128)
    d_ln_g = total[0, off_lng:off_lnb].reshape(DEPTH, D)
    d_ln_b = total[0, off_lnb:off_cw].reshape(DEPTH, D)
    d_conv = total[0, off_cw:off_loss].reshape(DEPTH, 3, N_CHIPS, WIDTH // N_CHIPS)
    d_conv = lax.dynamic_index_in_dim(d_conv, chip, axis=2, keepdims=False)
    dmod_all = gathered[:, 0, :n_mod].reshape(N_DEV, DEPTH, N_CHIPS, mod_cols)
    dmod_cols = jnp.transpose(lax.dynamic_index_in_dim(dmod_all, chip, axis=2, keepdims=False), (1, 0, 2))
    d_w_mod = _grad_w_mod(c_all, dmod_cols)

    res = {}
    res["w_mod"] = _adamw(w_mod, m_w_mod, v_w_mod, [[d_w_mod]], "adamw_w_mod")
    res["b_mod"] = _adamw(b_mod, m_b_mod, v_b_mod, [[d_b_mod]], "adamw_b_mod")
    res["w_in"] = _adamw(w_in, m_w_in, v_w_in, grads["in"], "adamw_w_in")
    res["conv_w"] = _adamw(conv_w, m_conv_w, v_conv_w, [[d_conv]], "adamw_conv_w")
    res["hgrn_norm_w"] = _adamw(hgrn_norm_w, m_hgrn_norm_w, v_hgrn_norm_w, [[d_norm_w]], "adamw_norm_w")
    res["lower_bounds"] = _adamw(lower_bounds, m_lower_bounds, v_lower_bounds, [[d_lower]], "adamw_lower_bounds")
    res["w_branch"] = _adamw(w_branch, m_w_branch, v_w_branch, grads["br"], "adamw_w_branch")
    res["w_out"] = _adamw(w_out, m_w_out, v_w_out, grads["out"], "adamw_w_out")
    res["ln_g"] = _adamw(ln_g, m_ln_g, v_ln_g, [[d_ln_g]], "adamw_ln_g")
    res["ln_b"] = _adamw(ln_b, m_ln_b, v_ln_b, [[d_ln_b]], "adamw_ln_b")
    names = ["w_mod", "b_mod", "w_in", "conv_w", "hgrn_norm_w", "lower_bounds", "w_branch", "w_out", "ln_g", "ln_b"]
    return (loss, dx[None], *[res[n][0] for n in names], *[res[n][1] for n in names],
            *[res[n][2] for n in names], *[res[n][3] for n in names])


class _Plan:
    def __init__(self, w_in, w_br, w_out):
        self.local = {"in": w_in, "br": w_br, "out": w_out}
        self.gathered, self.grads, self.scattered, self.keys = {}, {}, {}, {}

    def _gather(self, stage, l, keys):
        self.keys[(stage, l)] = ("gather", keys)
        return _gather_job([self.local[kind][layer] for kind, layer in keys])

    def _scatter(self, stage, l, keys):
        self.keys[(stage, l)] = ("scatter", keys)
        return _scatter_job([self.grads[key] for key in keys])

    def job(self, stage, l):
        if stage == "proj_fwd":
            return self._gather(stage, l, [("br", l), ("out", l)])
        if stage == "attn_fwd" and l + 1 < DEPTH:
            return self._gather(stage, l, [("in", l + 1)])
        if stage == "attn_bwd":
            return self._scatter(stage, l, [("out", l), ("br", l)] + ([("in", l + 1)] if l + 1 < DEPTH else []))
        if stage == "proj_bwd" and l == 0:
            return self._scatter(stage, l, [("in", 0)])
        return None

    def done(self, stage, l, outs):
        if outs is None:
            return
        what, keys = self.keys[(stage, l)]
        for n, key in enumerate(keys):
            if what == "gather":
                self.gathered[key] = outs[n]
            else:
                self.scattered[key] = (outs[n], outs[len(keys) + n])


def _local_step(x0, target, mods, lbs, conv_all, hgrn_norm_w, ln_g, ln_b, plan):
    D = D_MODEL
    after, before = _attn_consts()
    hg_consts = _hgrn_consts()

    saved = []
    xl = x0
    for l in range(DEPTH):
        (proj, h), got = _proj_fwd(xl, mods[l], plan.gathered[("in", l)], plan.job("proj_fwd", l))
        plan.done("proj_fwd", l, got)
        (o_a, tot), got = _attn_fwd(proj, after, plan.job("attn_fwd", l))
        plan.done("attn_fwd", l, got)
        o_b, states = _hgrn_fwd(proj, lbs[l], hg_consts)
        ys = _branch_fwd(proj, o_a, o_b, hgrn_norm_w, conv_all, l)
        x_next, merged, y = _merge_fwd(xl, mods[l], proj, ys, plan.gathered[("br", l)], plan.gathered[("out", l)], ln_g, ln_b, l)
        saved.append((xl, proj, h, o_a, tot, o_b, states, ys, merged, y))
        xl = x_next
    dx, loss_blk = _loss_head(xl, target)

    small = [None] * DEPTH
    for l in reversed(range(DEPTH)):
        xin, proj, h, o_a, tot, o_b, states, ys, merged, y = saved[l]
        dx_res, dy, dp, dg, dy_a, dy_b, dy_c, dln_g, dln_b, dgate = _merge_bwd(
            dx, xin, y, mods[l], proj, ys, plan.gathered[("br", l)], plan.gathered[("out", l)], ln_g, l)
        plan.grads[("out", l)] = _grad_w_out(merged, dy)
        g_br = jax.ShapeDtypeStruct((N_CHIPS, 3, WIDTH, D // N_CHIPS), BF)
        for i in range(3):
            g_br = _grad_w(ys[i], dp, (i,), g_br, "grad_w_branch", tm=WIDTH, tn=D // N_CHIPS, b_col0=i * D)
        plan.grads[("br", l)] = g_br
        d_oa, d_ob, dz_a, dz_b, dpre, dpost, du, dz_c, dnorm_w, dconv_w = _branch_bwd(proj, o_a, o_b, hgrn_norm_w, conv_all, dy_a, dy_b, dy_c, l)
        (dq_a, dk_a, dv_a), got = _attn_bwd(proj, d_oa, tot, after, before, plan.job("attn_bwd", l))
        plan.done("attn_bwd", l, got)
        dq_b, df_b, di_b, dlb = _hgrn_bwd(proj, lbs[l], states, d_ob, hg_consts)
        dproj = jnp.concatenate([dq_a, dk_a, dv_a, dz_a, dq_b, df_b, di_b, dz_b, dpre, dpost, du, dz_c, dg], axis=1)
        plan.grads[("in", l)] = _grad_w(h, dproj, (), jax.ShapeDtypeStruct((N_CHIPS, D, IN_COLS // N_CHIPS), BF), "grad_w_in", tm=512, tn=768)
        (dx, dshift, dscale), got = _proj_bwd(dproj, plan.gathered[("in", l)], xin, mods[l], dx_res, plan.job("proj_bwd", l))
        plan.done("proj_bwd", l, got)
        small[l] = (dshift, dscale, dgate, dnorm_w, dlb, dln_g, dln_b, dconv_w)
    return loss_blk, dx, small


def _grad_w_out(merged, dy):
    S, D = merged.shape
    q = D // N_CHIPS

    def body(a_ref, b_ref, o_ref):
        o_ref[...] = _dot_tn(a_ref[...], b_ref[...]).astype(BF)

    return pl.pallas_call(
        body, name="grad_w_out",
        grid=(N_CHIPS,),
        in_specs=[pl.BlockSpec((S, q), lambda j: (0, j)), pl.BlockSpec((S, D), lambda j: (0, 0))],
        out_specs=pl.BlockSpec((None, q, D), lambda j: (j, 0, 0)),
        out_shape=jax.ShapeDtypeStruct((N_CHIPS, q, D), BF),
        compiler_params=_params(dimension_semantics=("parallel",)),
    )(merged, dy)
```

```python
import functools
import math

import numpy as np
import jax
import jax.numpy as jnp
from jax import lax
from jax.experimental import pallas as pl
from jax.experimental.pallas import tpu as pltpu

F32 = jnp.float32
BF = jnp.bfloat16
MESH = pl.DeviceIdType.MESH

DEPTH = 2
D_MODEL = 1024
WIDTH = 512
IN_COLS = 12 * WIDTH + 3 * D_MODEL
N_CHIPS = 4
N_DEV = 8
SB_BLOCK = 128
SB_HEAD_DIM = 64
HG_CHUNK = 64
HG_DIM = 128
LN_EPS = 1e-5
RMS_EPS = 1e-6
ALPHA = (2.0 * DEPTH) ** 0.25
ADAM_LR, ADAM_B1, ADAM_B2, ADAM_EPS, ADAM_WD, ADAM_STEP = 0.001, 0.9, 0.999, 1e-08, 0.01, 10
VMEM_LIMIT = 56 << 20


def _params(**kw):
    return pltpu.CompilerParams(vmem_limit_bytes=VMEM_LIMIT, **kw)


def _dot(a, b):
    return jnp.dot(a, b, preferred_element_type=F32)


def _dot_nt(a, b):
    return lax.dot_general(a, b, (((1,), (1,)), ((), ())), preferred_element_type=F32)


def _dot_tn(a, b):
    return lax.dot_general(a, b, (((0,), (0,)), ((), ())), preferred_element_type=F32)


def _sigmoid(x):
    return 1.0 / (1.0 + jnp.exp(-x))


def _silu(x):
    return x * _sigmoid(x)


def _softplus(z):
    return jnp.maximum(z, 0.0) + jnp.log(1.0 + jnp.exp(-jnp.abs(z)))


def _split_dot(t, g, terms):
    acc = None
    rest = g
    for _ in range(terms):
        part = rest.astype(BF)
        rest = rest - part.astype(F32)
        d = _dot(t, part)
        acc = d if acc is None else acc + d
    return acc


def _split_dot_r(g, t, terms):
    acc = None
    rest = g
    for _ in range(terms):
        part = rest.astype(BF)
        rest = rest - part.astype(F32)
        d = _dot(part, t)
        acc = d if acc is None else acc + d
    return acc


def _standardize(x):
    mu = jnp.mean(x, axis=-1, keepdims=True)
    xc = x - mu
    var = jnp.mean(xc * xc, axis=-1, keepdims=True)
    rstd = lax.rsqrt(var + LN_EPS)
    return xc * rstd, rstd


def _standardize_bwd(dxs, xs, rstd):
    return rstd * (dxs - jnp.mean(dxs, axis=-1, keepdims=True) - xs * jnp.mean(dxs * xs, axis=-1, keepdims=True))


class _Job:
    def __init__(self, ins, outs, sems, make):
        self.ins, self.outs, self.sems, self.make = list(ins), list(outs), list(sems), make


def _join_jobs(jobs):
    jobs = [j for j in jobs if j is not None]
    if len(jobs) <= 1:
        return jobs[0] if jobs else None

    def make(ins, outs, sems):
        phases, i, o, s = [], 0, 0, 0
        for j in jobs:
            got = j.make(ins[i:i + len(j.ins)], outs[o:o + len(j.outs)], sems[s:s + len(j.sems)])
            i, o, s = i + len(j.ins), o + len(j.outs), s + len(j.sems)
            for n, phase in enumerate(got):
                if n == len(phases):
                    phases.append([])
                phases[n] += phase
        return phases

    return _Job(sum([j.ins for j in jobs], []), sum([j.outs for j in jobs], []), sum([j.sems for j in jobs], []), make)


def _flip(v, bit):
    return 1 - v if bit else v


def _split_axis(shape):
    return next(i for i, d in enumerate(shape) if d % 2 == 0)


def _half(ref, axis, which, lead=()):
    size = ref.shape[len(lead) + axis] // 2
    idx = [slice(None)] * (ref.ndim - len(lead))
    idx[axis] = pl.ds(which * size, size)
    return ref.at[tuple(lead) + tuple(idx)]


def _dma_sems(*shapes):
    return [pltpu.SemaphoreType.DMA(s) for s in shapes]


def _gather_job(shards):
    n = len(shards)
    axes = [_split_axis(s.shape) for s in shards]

    def make(ins, outs, sems):
        local_sems, send1, recv1, send2, recv2 = sems
        mx, my, mc = lax.axis_index("x"), lax.axis_index("y"), lax.axis_index("c")
        mine = 2 * mx + my
        fetch, pass_on = [], []
        for a in range(n):
            fetch.append(pltpu.make_async_copy(ins[a], outs[a].at[mine], local_sems.at[a]))
            for k in range(1, N_CHIPS):
                px, py = _flip(mx, k & 2), _flip(my, k & 1)
                fetch.append(pltpu.make_async_remote_copy(
                    src_ref=_half(ins[a], axes[a], mc), dst_ref=_half(outs[a], axes[a], mc, (mine,)),
                    send_sem=send1.at[a, k - 1], recv_sem=recv1.at[a, k - 1], device_id=(px, py, mc), device_id_type=MESH))
                theirs = _half(outs[a], axes[a], mc, (2 * px + py,))
                pass_on.append(pltpu.make_async_remote_copy(
                    src_ref=theirs, dst_ref=theirs, send_sem=send2.at[a, k - 1], recv_sem=recv2.at[a, k - 1],
                    device_id=(mx, my, 1 - mc), device_id_type=MESH))
        return [fetch, pass_on]

    pairs = (n, N_CHIPS - 1)
    return _Job(shards, [jax.ShapeDtypeStruct((N_CHIPS,) + s.shape, s.dtype) for s in shards],
                _dma_sems((n,), pairs, pairs, pairs, pairs), make)


def _to_sibling_job(grads):
    n = len(grads)
    axes = [1 + _split_axis(g.shape[1:]) for g in grads]

    def make(ins, outs, sems):
        local_sems, send_sems, recv_sems = sems
        kept, sent = outs[:n], outs[n:]
        mx, my, mc = lax.axis_index("x"), lax.axis_index("y"), lax.axis_index("c")
        copies = []
        for a in range(n):
            copies.append(pltpu.make_async_copy(_half(ins[a], axes[a], mc), kept[a], local_sems.at[a]))
            copies.append(pltpu.make_async_remote_copy(
                src_ref=_half(ins[a], axes[a], 1 - mc), dst_ref=sent[a], send_sem=send_sems.at[a], recv_sem=recv_sems.at[a],
                device_id=(mx, my, 1 - mc), device_id_type=MESH))
        return [copies]

    def halved(g):
        shape = list(g.shape)
        shape[1 + _split_axis(g.shape[1:])] //= 2
        return jax.ShapeDtypeStruct(tuple(shape), g.dtype)

    return _Job(grads, [halved(g) for g in grads] * 2, _dma_sems((n,), (n,), (n,)), make)


def _scatter_job(grads):
    n = len(grads)

    def make(ins, outs, sems):
        local_sems, send_sems, recv_sems = sems
        owns, gots = outs[:n], outs[n:]
        mx, my, mc = lax.axis_index("x"), lax.axis_index("y"), lax.axis_index("c")
        copies = []
        for a in range(n):
            copies.append(pltpu.make_async_copy(ins[a].at[2 * mx + my], owns[a], local_sems.at[a]))
            for k in range(1, N_CHIPS):
                px, py = _flip(mx, k & 2), _flip(my, k & 1)
                copies.append(pltpu.make_async_remote_copy(
                    src_ref=ins[a].at[2 * px + py], dst_ref=gots[a].at[k - 1], send_sem=send_sems.at[a, k - 1],
                    recv_sem=recv_sems.at[a, k - 1], device_id=(px, py, mc), device_id_type=MESH))
        return [copies]

    outs = [jax.ShapeDtypeStruct(g.shape[1:], g.dtype) for g in grads]
    outs += [jax.ShapeDtypeStruct((N_CHIPS - 1,) + g.shape[1:], g.dtype) for g in grads]
    pairs = (n, N_CHIPS - 1)
    return _Job(grads, outs, _dma_sems((n,), pairs, pairs), make)


def _place_job(halves, shapes):
    n = len(halves)
    axes = [_split_axis(s) for s in shapes]

    def make(ins, outs, sems):
        local_sems, send_sems, recv_sems = sems
        mx, my, mc = lax.axis_index("x"), lax.axis_index("y"), lax.axis_index("c")
        copies = []
        for a in range(n):
            here = _half(outs[a], axes[a], mc)
            copies.append(pltpu.make_async_copy(ins[a], here, local_sems.at[a]))
            copies.append(pltpu.make_async_remote_copy(src_ref=ins[a], dst_ref=here, send_sem=send_sems.at[a], recv_sem=recv_sems.at[a],
                                                       device_id=(mx, my, 1 - mc), device_id_type=MESH))
        return [copies]

    return _Job(halves, [jax.ShapeDtypeStruct(tuple(s), h.dtype) for s, h in zip(shapes, halves)], _dma_sems((n,), (n,), (n,)), make)


def _run_phases(phases, first=0):
    for n, phase in enumerate(phases):
        if n >= first:
            for cp in phase:
                cp.start()
        for cp in phase:
            cp.wait()


def _run_job(job, name):
    k_in, k_out = len(job.ins), len(job.outs)

    def body(*refs):
        _run_phases(job.make(refs[:k_in], refs[k_in:k_in + k_out], refs[k_in + k_out:]))

    hbm = pl.BlockSpec(memory_space=pl.ANY)
    return pl.pallas_call(body, name=name, in_specs=[hbm] * k_in, out_specs=[hbm] * k_out, out_shape=job.outs,
                          scratch_shapes=job.sems, compiler_params=_params())(*job.ins)


def _hosted(body, job, args, *, name, grid, in_specs, out_specs, out_shape, scratch_shapes=(), semantics):
    in_specs, out_specs, out_shape, scratch = list(in_specs), list(out_specs), list(out_shape), list(scratch_shapes)
    if job is None:
        outs = pl.pallas_call(body, name=name, grid=grid, in_specs=in_specs, out_specs=out_specs, out_shape=out_shape,
                              scratch_shapes=scratch, compiler_params=_params(dimension_semantics=semantics))(*args)
        return list(outs), None
    n_in, n_out, n_scr, k_in, k_out = len(in_specs), len(out_specs), len(scratch), len(job.ins), len(job.outs)

    def wrapped(*refs):
        ins, rest = refs[:n_in], refs[n_in:]
        job_ins, rest = rest[:k_in], rest[k_in:]
        outs, rest = rest[:n_out], rest[n_out:]
        job_outs, rest = rest[:k_out], rest[k_out:]
        scr, sems = rest[:n_scr], rest[n_scr:]
        ids = [pl.program_id(a) for a in range(len(grid))]
        first = functools.reduce(jnp.logical_and, [i == 0 for i in ids])
        last = functools.reduce(jnp.logical_and, [i == g - 1 for i, g in zip(ids, grid)])

        @pl.when(first)
        def _():
            for cp in job.make(job_ins, job_outs, sems)[0]:
                cp.start()

        body(*ins, *outs, *scr)

        @pl.when(last)
        def _():
            _run_phases(job.make(job_ins, job_outs, sems), first=1)

    hbm = pl.BlockSpec(memory_space=pl.ANY)
    outs = pl.pallas_call(
        wrapped, name=name, grid=grid, in_specs=in_specs + [hbm] * k_in, out_specs=out_specs + [hbm] * k_out,
        out_shape=out_shape + job.outs, scratch_shapes=scratch + job.sems,
        compiler_params=_params(dimension_semantics=("arbitrary",) * len(grid)))(*args, *job.ins)
    return list(outs[:n_out]), list(outs[n_out:])


def _proj_fwd(x, mod, wg, job=None, tm=512, tn=768):
    S, D = x.shape
    tm = min(tm, S)
    shard = wg.shape[-1]
    per = shard // tn

    def body(x_ref, mod_ref, w_ref, proj_ref, h_ref, hs):
        @pl.when(pl.program_id(1) == 0)
        def _():
            xs, _ = _standardize(x_ref[...])
            h = xs * (1.0 + mod_ref[:, D:2 * D]) + mod_ref[:, 0:D]
            hb = h.astype(BF)
            hs[...] = hb
            h_ref[...] = hb

        proj_ref[...] = _dot(hs[...], w_ref[...])

    return _hosted(
        body, job, (x, mod, wg), name="proj_fwd",
        grid=(S // tm, IN_COLS // tn),
        in_specs=[pl.BlockSpec((tm, D), lambda i, j: (i, 0)),
                  pl.BlockSpec((1, 3 * D), lambda i, j: (0, 0)),
                  pl.BlockSpec((None, D, tn), lambda i, j: (j // per, 0, j % per))],
        out_specs=[pl.BlockSpec((tm, tn), lambda i, j: (i, j)),
                   pl.BlockSpec((tm, D), lambda i, j: (i, 0))],
        out_shape=[jax.ShapeDtypeStruct((S, IN_COLS), F32), jax.ShapeDtypeStruct((S, D), BF)],
        scratch_shapes=[pltpu.VMEM((tm, D), BF)],
        semantics=("parallel", "arbitrary"))


SB_ROWS = 256
SB_KEYS = 256


def _attn_consts():
    j = np.arange(SB_KEYS)[:, None]
    s = np.arange(SB_KEYS)[None, :]
    from_here = np.concatenate([(j >= s), (j >= s)], axis=0).astype(np.float32)
    up_to = np.concatenate([(j <= s), (j <= s)], axis=0).astype(np.float32)
    return jnp.asarray(from_here, BF), jnp.asarray(up_to, BF)


def _hi_lo(x):
    hi = lax.bitcast_convert_type(lax.bitcast_convert_type(x, jnp.uint32) & jnp.uint32(0xFFFF0000), F32)
    return hi.astype(BF), (x - hi).astype(BF)


def _sums_r(x, t2):
    hi, lo = _hi_lo(x)
    return _dot(jnp.concatenate([hi, lo], axis=1), t2)


def _all_lanes(col, lanes):
    return jnp.broadcast_to(col, (col.shape[0], lanes))


def _attn_rows(ref, r0, rows, head0, scale=None):
    v = ref[pl.ds(r0, rows), :]
    if scale is not None:
        v = v * scale
    return jnp.concatenate([jnp.where(head0, v, 0.0), jnp.where(head0, 0.0, v)], axis=0).astype(BF)


def _attn_scores(q2n, k_ref, kj, t2, from_here_ref, masked):
    c0 = pl.multiple_of(kj * SB_KEYS, SB_KEYS)
    kb = k_ref[pl.ds(c0, SB_KEYS), :].astype(BF)
    zn = _dot_nt(q2n, kb)
    lsb = jnp.minimum(zn, 0.0) - jnp.log(1.0 + jnp.exp(-jnp.abs(zn)))
    valid = None
    if masked:
        valid = (lax.broadcasted_iota(jnp.int32, zn.shape, 1) + kj * SB_KEYS) < t2
        lsb = jnp.where(valid, lsb, 0.0)
    return c0, kb, zn, valid, lsb, _sums_r(lsb, from_here_ref[...])


def _attn_fwd(proj, from_here, job=None):
    S = proj.shape[0]
    TQ = SB_ROWS
    assert S % TQ == 0 and SB_KEYS == TQ
    scale = SB_HEAD_DIM ** -0.5

    def body(q_ref, k_ref, v_ref, from_here_ref, o_ref, tot_ref, run, acc):
        head0 = lax.broadcasted_iota(jnp.int32, (1, 2 * SB_HEAD_DIM), 1) < SB_HEAD_DIM

        def qloop(qi, _):
            r0 = pl.multiple_of(qi * TQ, TQ)
            q2n = _attn_rows(q_ref, r0, TQ, head0, -scale)
            trow = lax.broadcasted_iota(jnp.int32, (TQ, SB_KEYS), 0) + qi * TQ
            t2 = jnp.concatenate([trow, trow], axis=0)
            run[...] = jnp.zeros_like(run)
            acc[...] = jnp.zeros_like(acc)

            def step(kj, masked):
                c0, _, zn, valid, _, sums = _attn_scores(q2n, k_ref, kj, t2, from_here_ref, masked)
                r = run[...]
                e = sums - zn + jnp.concatenate([r, r], axis=1)
                if masked:
                    e = jnp.where(valid, e, -jnp.inf)
                acc[...] += _dot(jnp.exp(e).astype(BF), v_ref[pl.ds(c0, SB_KEYS), :].astype(BF))
                run[...] = r + _all_lanes(sums[:, 0:1], SB_BLOCK)

            step(qi, True)

            def below(n, _):
                step(qi - 1 - n, False)
                return 0

            lax.fori_loop(0, qi, below, 0)
            o_ref[pl.ds(r0, TQ), :] = jnp.where(head0, acc[0:TQ, :], acc[TQ:2 * TQ, :])
            tot_ref[0, pl.ds(r0, TQ), :] = run[0:TQ, :]
            tot_ref[1, pl.ds(r0, TQ), :] = run[TQ:2 * TQ, :]
            return 0

        lax.fori_loop(0, S // TQ, qloop, 0)

    col = lambda base: pl.BlockSpec((S, 128), lambda p, base=base: (0, base + p))
    return _hosted(
        body, job, (proj, proj, proj, from_here), name="attn_fwd",
        grid=(WIDTH // 128,),
        in_specs=[col(0), col(4), col(8), pl.BlockSpec(from_here.shape, lambda p: (0, 0))],
        out_specs=[pl.BlockSpec((S, 128), lambda p: (0, p)), pl.BlockSpec((None, 2, S, 128), lambda p: (p, 0, 0, 0))],
        out_shape=[jax.ShapeDtypeStruct((S, WIDTH), F32), jax.ShapeDtypeStruct((WIDTH // 128, 2, S, 128), F32)],
        scratch_shapes=[pltpu.VMEM((2 * TQ, SB_BLOCK), F32), pltpu.VMEM((2 * TQ, 128), F32)],
        semantics=("parallel",))


def _attn_bwd(proj, d_o, tot, from_here, up_to, job=None):
    S = proj.shape[0]
    TQ = SB_ROWS
    assert S % TQ == 0 and SB_KEYS == TQ
    scale = SB_HEAD_DIM ** -0.5

    def body(q_ref, k_ref, v_ref, do_ref, tot_ref, from_here_ref, up_to_ref, dq_ref, dk_ref, dv_ref, pre, cum, dq_acc, dk_acc, dv_acc):
        head0 = lax.broadcasted_iota(jnp.int32, (1, 2 * SB_HEAD_DIM), 1) < SB_HEAD_DIM
        dk_acc[...] = jnp.zeros_like(dk_acc)
        dv_acc[...] = jnp.zeros_like(dv_acc)

        def qloop(qi, _):
            r0 = pl.multiple_of(qi * TQ, TQ)
            q2n = _attn_rows(q_ref, r0, TQ, head0, -scale)
            do2 = _attn_rows(do_ref, r0, TQ, head0)
            trow = lax.broadcasted_iota(jnp.int32, (TQ, SB_KEYS), 0) + qi * TQ
            t2 = jnp.concatenate([trow, trow], axis=0)
            pre[0:TQ, :] = tot_ref[0, pl.ds(r0, TQ), :]
            pre[TQ:2 * TQ, :] = tot_ref[1, pl.ds(r0, TQ), :]
            cum[...] = jnp.zeros_like(cum)
            dq_acc[...] = jnp.zeros_like(dq_acc)

            def step(kj, masked):
                c0, kb, zn, valid, lsb, sums = _attn_scores(q2n, k_ref, kj, t2, from_here_ref, masked)
                later = pre[...] - _all_lanes(sums[:, 0:1], SB_BLOCK)
                pre[...] = later
                e = sums - zn + jnp.concatenate([later, later], axis=1)
                sig = jnp.exp(lsb - zn)
                if masked:
                    e = jnp.where(valid, e, -jnp.inf)
                    sig = jnp.where(valid, sig, 0.0)
                a = jnp.exp(e)
                w = _dot_nt(do2, v_ref[pl.ds(c0, SB_KEYS), :].astype(BF)) * a
                upto = _sums_r(w, up_to_ref[...])
                c = cum[...]
                dz = w - sig * (upto + jnp.concatenate([c, c], axis=1))
                cum[...] = c + _all_lanes(upto[:, SB_KEYS - 1:SB_KEYS], SB_BLOCK)
                dzb = dz.astype(BF)
                dq_acc[...] += _dot(dzb, kb)
                dk_acc[pl.ds(c0, SB_KEYS), :] += _dot_tn(dzb, q2n)
                dv_acc[pl.ds(c0, SB_KEYS), :] += _dot_tn(a.astype(BF), do2)

            def below(kj, _):
                step(kj, False)
                return 0

            lax.fori_loop(0, qi, below, 0)
            step(qi, True)
            dq_ref[pl.ds(r0, TQ), :] = (jnp.where(head0, dq_acc[0:TQ, :], dq_acc[TQ:2 * TQ, :]) * scale).astype(BF)
            return 0

        lax.fori_loop(0, S // TQ, qloop, 0)
        dk_ref[...] = (-dk_acc[...]).astype(BF)
        dv_ref[...] = dv_acc[...].astype(BF)

    col = lambda base: pl.BlockSpec((S, 128), lambda p, base=base: (0, base + p))
    whole = lambda a: pl.BlockSpec(a.shape, lambda p: (0, 0))
    out = jax.ShapeDtypeStruct((S, WIDTH), BF)
    big = pltpu.VMEM((2 * TQ, SB_BLOCK), F32)
    return _hosted(
        body, job, (proj, proj, proj, d_o, tot, from_here, up_to), name="attn_bwd",
        grid=(WIDTH // 128,),
        in_specs=[col(0), col(4), col(8), col(0), pl.BlockSpec((None, 2, S, 128), lambda p: (p, 0, 0, 0)), whole(from_here), whole(up_to)],
        out_specs=[col(0), col(0), col(0)],
        out_shape=[out, out, out],
        scratch_shapes=[big, big, big, pltpu.VMEM((S, 128), F32), pltpu.VMEM((S, 128), F32)],
        semantics=("parallel",))


HG_LEVELS = (32, 16, 8, 4, 2, 1)


def _hgrn_consts():
    C = HG_CHUNK
    t = np.arange(C)[:, None]
    s = np.arange(C)[None, :]
    rows = [(s <= t), (s > t)]
    masks = [(t == s)]
    for m in HG_LEVELS:
        two = 2 * m
        mid = (t // two) * two + m
        right = (t % two) >= m
        rows.append(right & (s >= mid) & (s <= t))
        rows.append((~right) & (s > t) & (s <= mid - 1))
        masks.append(((t // two) == (s // two)) & right & ((s % two) < m))
    rows.append(np.ones((HG_DIM, C), bool))
    tri = np.concatenate(rows, axis=0).astype(np.float32)
    return (jnp.asarray(tri, BF), jnp.asarray(tri.T.copy(), BF), jnp.asarray(np.stack(masks).astype(np.float32), F32))


HG_SUM_ROWS = (HG_CHUNK,) * (2 + 2 * len(HG_LEVELS)) + (HG_DIM,)


@jax.custom_vjp
def _hgrn_sums(tri, tri_t, g):
    e = _split_dot(tri, g, 3)
    out, r = [], 0
    for n in HG_SUM_ROWS:
        out.append(e[r:r + n])
        r += n
    return tuple(out)


def _hgrn_sums_fwd(tri, tri_t, g):
    return _hgrn_sums(tri, tri_t, g), (tri, tri_t)


def _hgrn_sums_bwd(res, ds):
    tri, tri_t = res
    return jnp.zeros_like(tri), jnp.zeros_like(tri_t), _split_dot(tri_t, jnp.concatenate(ds, axis=0), 3)


_hgrn_sums.defvjp(_hgrn_sums_fwd, _hgrn_sums_bwd)


def _bf_dot(a, b):
    return _dot(a.astype(BF), b.astype(BF))


def _bf_dot_nt(a, b):
    return _dot_nt(a.astype(BF), b.astype(BF))


def _bf_dot_tn(a, b):
    return _dot_tn(a.astype(BF), b.astype(BF))


@jax.custom_vjp
def _mm(a, b):
    return _bf_dot(a, b)


_mm.defvjp(lambda a, b: (_bf_dot(a, b), (a, b)), lambda r, ct: (_bf_dot_nt(ct, r[1]), _bf_dot_tn(r[0], ct)))


@jax.custom_vjp
def _mm_nt(a, b):
    return _bf_dot_nt(a, b)


_mm_nt.defvjp(lambda a, b: (_bf_dot_nt(a, b), (a, b)), lambda r, ct: (_bf_dot(ct, r[1]), _bf_dot_tn(ct, r[0])))


@jax.custom_vjp
def _mm_tn(a, b):
    return _bf_dot_tn(a, b)


_mm_tn.defvjp(lambda a, b: (_bf_dot_tn(a, b), (a, b)), lambda r, ct: (_bf_dot_nt(r[1], ct), _bf_dot(r[0], ct)))


def _hgrn_chunk(tri, tri_t, masks, qraw, fpre, v, st, lb):
    q = _silu(qraw)
    f = lb + (1.0 - lb) * _sigmoid(fpre)
    k = 1.0 - f
    e = _hgrn_sums(tri, tri_t, jnp.log(f))
    prefix, suffix, whole = e[0], e[1], e[-1]
    scores = masks[0] * _mm_nt(q, k)
    for n in range(len(HG_LEVELS)):
        scores = scores + masks[n + 1] * _mm_nt(q * jnp.exp(e[2 + 2 * n]), k * jnp.exp(e[3 + 2 * n]))
    o = _mm_nt(q * jnp.exp(prefix), st) + _mm(scores, v)
    st_new = st * jnp.exp(whole) + _mm_tn(v, k * jnp.exp(suffix))
    return o, st_new


def _hgrn_specs(S, consts):
    col = lambda base: pl.BlockSpec((S, 128), lambda p, base=base: (0, base + p))
    whole = [pl.BlockSpec(a.shape, lambda p, n=a.ndim: (0,) * n) for a in consts]
    return col, whole


def _hgrn_fwd(proj, lbs, consts):
    S = proj.shape[0]
    nc = S // HG_CHUNK

    def body(q_ref, f_ref, i_ref, lb_ref, tri_ref, trit_ref, mask_ref, o_ref, st_ref):
        tri, tri_t = tri_ref[...], trit_ref[...]
        masks = [mask_ref[n] for n in range(len(HG_LEVELS) + 1)]
        lb = lb_ref[...]

        def chunk(ci, st):
            r0 = pl.multiple_of(ci * HG_CHUNK, HG_CHUNK)
            rows = pl.ds(r0, HG_CHUNK)
            st_ref[ci] = st
            o, st_new = _hgrn_chunk(tri, tri_t, masks, q_ref[rows, :], f_ref[rows, :], i_ref[rows, :], st, lb)
            o_ref[rows, :] = o
            return st_new

        lax.fori_loop(0, nc, chunk, jnp.zeros((HG_DIM, HG_DIM), F32))

    col, whole = _hgrn_specs(S, consts)
    return pl.pallas_call(
        body, name="hgrn_fwd",
        grid=(WIDTH // 128,),
        in_specs=[col(16), col(20), col(24), pl.BlockSpec((None, 1, 128), lambda p: (p, 0, 0))] + whole,
        out_specs=[col(0), pl.BlockSpec((None, nc, HG_DIM, HG_DIM), lambda p: (p, 0, 0, 0))],
        out_shape=[jax.ShapeDtypeStruct((S, WIDTH), F32), jax.ShapeDtypeStruct((WIDTH // 128, nc, HG_DIM, HG_DIM), F32)],
        compiler_params=_params(dimension_semantics=("parallel",)),
    )(proj, proj, proj, lbs, *consts)


def _hgrn_bwd(proj, lbs, states, d_o, consts, job=None):
    S = proj.shape[0]
    nc = S // HG_CHUNK

    def body(q_ref, f_ref, i_ref, lb_ref, st_ref, do_ref, tri_ref, trit_ref, mask_ref, dq_ref, df_ref, di_ref, dlb_ref):
        masks = [mask_ref[n] for n in range(len(HG_LEVELS) + 1)]
        lb = lb_ref[...]
        fn = functools.partial(_hgrn_chunk, tri_ref[...], trit_ref[...], masks)

        def chunk(n, carry):
            d_st, dlb = carry
            ci = nc - 1 - n
            r0 = pl.multiple_of(ci * HG_CHUNK, HG_CHUNK)
            rows = pl.ds(r0, HG_CHUNK)
            _, pull = jax.vjp(fn, q_ref[rows, :], f_ref[rows, :], i_ref[rows, :], st_ref[ci], lb)
            dq, df, di, d_prev, dl = pull((do_ref[rows, :], d_st))
            dq_ref[rows, :] = dq.astype(BF)
            df_ref[rows, :] = df.astype(BF)
            di_ref[rows, :] = di.astype(BF)
            return d_prev, dlb + dl

        _, dlb = lax.fori_loop(0, nc, chunk, (jnp.zeros((HG_DIM, HG_DIM), F32), jnp.zeros((1, HG_DIM), F32)))
        dlb_ref[...] = dlb

    col, whole = _hgrn_specs(S, consts)
    head = pl.BlockSpec((None, 1, 128), lambda p: (p, 0, 0))
    out = jax.ShapeDtypeStruct((S, WIDTH), BF)
    return _hosted(
        body, job, (proj, proj, proj, lbs, states, d_o, *consts), name="hgrn_bwd",
        grid=(WIDTH // 128,),
        in_specs=[col(16), col(20), col(24), head, pl.BlockSpec((None, nc, HG_DIM, HG_DIM), lambda p: (p, 0, 0, 0)), col(0)] + whole,
        out_specs=[col(0), col(0), col(0), head],
        out_shape=[out, out, out, jax.ShapeDtypeStruct((WIDTH // 128, 1, 128), F32)],
        semantics=("parallel",))


def _shift_down(x, n):
    rows = lax.broadcasted_iota(jnp.int32, x.shape, 0)
    return jnp.where(rows >= n, pltpu.roll(x, n, 0), 0.0)


def _shift_up(x, n):
    S = x.shape[0]
    rows = lax.broadcasted_iota(jnp.int32, x.shape, 0)
    return jnp.where(rows < S - n, pltpu.roll(x, S - n, 0), 0.0)


def _branch_fwd(proj, o_a, o_b, norm_w, conv_w, layer):
    S = proj.shape[0]

    def body(oa_ref, za_ref, ob_ref, zb_ref, nw_ref, pre_ref, post_ref, u_ref, zc_ref, cw_ref, ya_ref, yb_ref, yc_ref):
        ya_ref[...] = (oa_ref[...] * _silu(za_ref[...])).astype(BF)
        ob = ob_ref[...]
        rn = lax.rsqrt(jnp.mean(ob * ob, axis=-1, keepdims=True) + RMS_EPS)
        yb_ref[...] = (ob * rn * nw_ref[layer:layer + 1, :] * _silu(zb_ref[...])).astype(BF)
        pu = pre_ref[...] * u_ref[...]
        conv = cw_ref[2:3, :] * pu + cw_ref[1:2, :] * _shift_down(pu, 1) + cw_ref[0:1, :] * _shift_down(pu, 2)
        yc_ref[...] = (post_ref[...] * conv * _silu(zc_ref[...])).astype(BF)

    col = lambda base: pl.BlockSpec((S, 128), lambda p, base=base: (0, base + p))
    out = jax.ShapeDtypeStruct((S, WIDTH), BF)
    return pl.pallas_call(
        body, name="branch_fwd",
        grid=(WIDTH // 128,),
        in_specs=[col(0), col(12), col(0), col(28), pl.BlockSpec(norm_w.shape, lambda p: (0, 0)),
                  col(32), col(36), col(40), col(44), pl.BlockSpec((None, None, 3, 128), lambda p: (p, layer, 0, 0))],
        out_specs=[col(0), col(0), col(0)],
        out_shape=[out, out, out],
        compiler_params=_params(dimension_semantics=("parallel",)),
    )(o_a, proj, o_b, proj, norm_w, proj, proj, proj, proj, conv_w)


def _branch_bwd(proj, o_a, o_b, norm_w, conv_w, dy_a, dy_b, dy_c, layer):
    S = proj.shape[0]

    def dsilu(z):
        s = _sigmoid(z)
        return s * z, s * (1.0 + z * (1.0 - s))

    def body(oa_ref, za_ref, ob_ref, zb_ref, nw_ref, pre_ref, post_ref, u_ref, zc_ref, cw_ref, dya_ref, dyb_ref, dyc_ref,
             doa_ref, dob_ref, dza_ref, dzb_ref, dpre_ref, dpost_ref, du_ref, dzc_ref, dnw_ref, dcw_ref):
        dya = dya_ref[...]
        sa, dsa = dsilu(za_ref[...])
        doa_ref[...] = dya * sa
        dza_ref[...] = (dya * oa_ref[...] * dsa).astype(BF)

        dyb = dyb_ref[...]
        ob = ob_ref[...]
        nw = nw_ref[layer:layer + 1, :]
        sb, dsb = dsilu(zb_ref[...])
        rn = lax.rsqrt(jnp.mean(ob * ob, axis=-1, keepdims=True) + RMS_EPS)
        on = ob * rn
        dzb_ref[...] = (dyb * on * nw * dsb).astype(BF)
        don_w = dyb * sb
        dnw_ref[...] = jnp.sum(don_w * on, axis=0, keepdims=True)
        don = don_w * nw
        dob_ref[...] = rn * (don - on * jnp.mean(don * on, axis=-1, keepdims=True))

        dyc = dyc_ref[...]
        pre, post, u = pre_ref[...], post_ref[...], u_ref[...]
        sc, dsc = dsilu(zc_ref[...])
        pu = pre * u
        pu1, pu2 = _shift_down(pu, 1), _shift_down(pu, 2)
        conv = cw_ref[2:3, :] * pu + cw_ref[1:2, :] * pu1 + cw_ref[0:1, :] * pu2
        dzc_ref[...] = (dyc * post * conv * dsc).astype(BF)
        dpost_ref[...] = (dyc * conv * sc).astype(BF)
        dconv = dyc * post * sc
        dcw_ref[0:1, :] = jnp.sum(dconv * pu2, axis=0, keepdims=True)
        dcw_ref[1:2, :] = jnp.sum(dconv * pu1, axis=0, keepdims=True)
        dcw_ref[2:3, :] = jnp.sum(dconv * pu, axis=0, keepdims=True)
        dpu = cw_ref[2:3, :] * dconv + cw_ref[1:2, :] * _shift_up(dconv, 1) + cw_ref[0:1, :] * _shift_up(dconv, 2)
        dpre_ref[...] = (dpu * u).astype(BF)
        du_ref[...] = (dpu * pre).astype(BF)

    col = lambda base: pl.BlockSpec((S, 128), lambda p, base=base: (0, base + p))
    f32 = jax.ShapeDtypeStruct((S, WIDTH), F32)
    bf = jax.ShapeDtypeStruct((S, WIDTH), BF)
    return pl.pallas_call(
        body, name="branch_bwd",
        grid=(WIDTH // 128,),
        in_specs=[col(0), col(12), col(0), col(28), pl.BlockSpec(norm_w.shape, lambda p: (0, 0)),
                  col(32), col(36), col(40), col(44), pl.BlockSpec((None, None, 3, 128), lambda p: (p, layer, 0, 0)),
                  col(0), col(0), col(0)],
        out_specs=[col(0)] * 8 + [pl.BlockSpec((None, 1, 128), lambda p: (p, 0, 0)), pl.BlockSpec((None, 3, 128), lambda p: (p, 0, 0))],
        out_shape=[f32, f32, bf, bf, bf, bf, bf, bf, jax.ShapeDtypeStruct((WIDTH // 128, 1, 128), F32),
                   jax.ShapeDtypeStruct((WIDTH // 128, 3, 128), F32)],
        compiler_params=_params(dimension_semantics=("parallel",)),
    )(o_a, proj, o_b, proj, norm_w, proj, proj, proj, proj, conv_w, dy_a, dy_b, dy_c)


def _branch_proj(y_refs, wb_ref):
    out = []
    for i in range(3):
        yv = y_refs[i][...]
        out.append(jnp.concatenate([_dot(yv, wb_ref[j, i]) for j in range(N_CHIPS)], axis=1))
    return out


def _merge_fwd(x, mod, proj, ys, wb, wo, ln_g, ln_b, layer, tm=256):
    S, D = x.shape
    tm = min(tm, S)

    def body(x_ref, mod_ref, ga_ref, gb_ref, gc_ref, ya_ref, yb_ref, yc_ref, wb_ref, wo_ref, g_ref, b_ref, xo_ref, mg_ref, y_ref):
        ps = _branch_proj((ya_ref, yb_ref, yc_ref), wb_ref)
        merged = _sigmoid(ga_ref[...]) * ps[0] + _sigmoid(gb_ref[...]) * ps[1] + _sigmoid(gc_ref[...]) * ps[2]
        mb = merged.astype(BF)
        mg_ref[...] = mb
        y = _dot(mb, wo_ref[...].reshape(D, D))
        y_ref[...] = y
        r = ALPHA * x_ref[...] + (1.0 + mod_ref[:, 2 * D:3 * D]) * y
        xn, _ = _standardize(r)
        xo_ref[...] = xn * g_ref[layer:layer + 1, :] + b_ref[layer:layer + 1, :]

    row = lambda w, c=0: pl.BlockSpec((tm, w), lambda i, c=c: (i, c))
    whole = lambda a: pl.BlockSpec(a.shape, lambda i, n=a.ndim: (0,) * n)
    return pl.pallas_call(
        body, name="merge_fwd",
        grid=(S // tm,),
        in_specs=[row(D), whole(mod), row(D, 6), row(D, 7), row(D, 8), row(WIDTH), row(WIDTH), row(WIDTH), whole(wb),
                  whole(wo), whole(ln_g), whole(ln_b)],
        out_specs=[row(D), row(D), row(D)],
        out_shape=[jax.ShapeDtypeStruct((S, D), F32), jax.ShapeDtypeStruct((S, D), BF), jax.ShapeDtypeStruct((S, D), F32)],
        compiler_params=_params(dimension_semantics=("parallel",)),
    )(x, mod, proj, proj, proj, *ys, wb, wo, ln_g, ln_b)


def _merge_bwd(dxo, x, y, mod, proj, ys, wb, wo, ln_g, layer, tm=256):
    S, D = x.shape
    tm = min(tm, S)

    def body(dxo_ref, x_ref, y_ref, mod_ref, ga_ref, gb_ref, gc_ref, ya_ref, yb_ref, yc_ref, wb_ref, wo_ref, g_ref,
             dxr_ref, dy_ref, dp_ref, dg_ref, dya_ref, dyb_ref, dyc_ref, dlg_ref, dlb_ref, dgt_ref):
        @pl.when(pl.program_id(0) == 0)
        def _():
            dlg_ref[...] = jnp.zeros_like(dlg_ref)
            dlb_ref[...] = jnp.zeros_like(dlb_ref)
            dgt_ref[...] = jnp.zeros_like(dgt_ref)

        gate1 = 1.0 + mod_ref[:, 2 * D:3 * D]
        yv = y_ref[...]
        xn, rstd = _standardize(ALPHA * x_ref[...] + gate1 * yv)
        dxo = dxo_ref[...]
        dlg_ref[...] += jnp.sum(dxo * xn, axis=0, keepdims=True)
        dlb_ref[...] += jnp.sum(dxo, axis=0, keepdims=True)
        dr = _standardize_bwd(dxo * g_ref[layer:layer + 1, :], xn, rstd)
        dxr_ref[...] = ALPHA * dr
        dgt_ref[...] += jnp.sum(dr * yv, axis=0, keepdims=True)
        dyb = (gate1 * dr).astype(BF)
        dy_ref[...] = dyb
        dmerged = _dot_nt(dyb, wo_ref[...].reshape(D, D))
        ps = _branch_proj((ya_ref, yb_ref, yc_ref), wb_ref)
        quarter = D // N_CHIPS
        for i, (gate_ref, out_ref) in enumerate(((ga_ref, dya_ref), (gb_ref, dyb_ref), (gc_ref, dyc_ref))):
            sg = _sigmoid(gate_ref[...])
            dg_ref[:, i * D:(i + 1) * D] = (dmerged * ps[i] * sg * (1.0 - sg)).astype(BF)
            dp = (dmerged * sg).astype(BF)
            dp_ref[:, i * D:(i + 1) * D] = dp
            acc = None
            for j in range(N_CHIPS):
                t = _dot_nt(dp[:, j * quarter:(j + 1) * quarter], wb_ref[j, i])
                acc = t if acc is None else acc + t
            out_ref[...] = acc

    row = lambda w, c=0: pl.BlockSpec((tm, w), lambda i, c=c: (i, c))
    whole = lambda a: pl.BlockSpec(a.shape, lambda i, n=a.ndim: (0,) * n)
    vec = pl.BlockSpec((1, D), lambda i: (0, 0))
    sd = jax.ShapeDtypeStruct
    return pl.pallas_call(
        body, name="merge_bwd",
        grid=(S // tm,),
        in_specs=[row(D), row(D), row(D), whole(mod), row(D, 6), row(D, 7), row(D, 8), row(WIDTH), row(WIDTH), row(WIDTH), whole(wb),
                  whole(wo), whole(ln_g)],
        out_specs=[row(D), row(D), row(3 * D), row(3 * D), row(WIDTH), row(WIDTH), row(WIDTH), vec, vec, vec],
        out_shape=[sd((S, D), F32), sd((S, D), BF), sd((S, 3 * D), BF), sd((S, 3 * D), BF), sd((S, WIDTH), F32), sd((S, WIDTH), F32),
                   sd((S, WIDTH), F32), sd((1, D), F32), sd((1, D), F32), sd((1, D), F32)],
        compiler_params=_params(dimension_semantics=("arbitrary",)),
    )(dxo, x, y, mod, proj, proj, proj, *ys, wb, wo, ln_g)


def _loss_head(x, target, tm=512):
    S, D = x.shape
    tm = min(tm, S)

    def body(x_ref, t_ref, dx_ref, loss_ref):
        @pl.when(pl.program_id(0) == 0)
        def _():
            loss_ref[...] = jnp.zeros_like(loss_ref)

        err = x_ref[...] - t_ref[...]
        dx_ref[...] = err * (1.0 / D)
        loss_ref[...] += 0.5 * jnp.sum(jnp.mean(err * err, axis=-1, keepdims=True))

    row = pl.BlockSpec((tm, D), lambda i: (i, 0))
    return pl.pallas_call(
        body, name="loss_head",
        grid=(S // tm,),
        in_specs=[row, row],
        out_specs=[row, pl.BlockSpec((8, 128), lambda i: (0, 0))],
        out_shape=[jax.ShapeDtypeStruct((S, D), F32), jax.ShapeDtypeStruct((8, 128), F32)],
        compiler_params=_params(dimension_semantics=("arbitrary",)),
    )(x, target)


def _proj_bwd(dproj, wg, x, mod, dx_res, job=None, tm=512, tk=768):
    S, D = x.shape
    tm = min(tm, S)
    shard = wg.shape[-1]
    per = shard // tk
    nk = IN_COLS // tk

    def body(dp_ref, w_ref, x_ref, mod_ref, dxr_ref, dx_ref, dsh_ref, dsc_ref, acc):
        i, k = pl.program_id(0), pl.program_id(1)

        @pl.when((i == 0) & (k == 0))
        def _():
            dsh_ref[...] = jnp.zeros_like(dsh_ref)
            dsc_ref[...] = jnp.zeros_like(dsc_ref)

        @pl.when(k == 0)
        def _():
            acc[...] = jnp.zeros_like(acc)

        acc[...] += _dot_nt(dp_ref[...], w_ref[...])

        @pl.when(k == nk - 1)
        def _():
            dh = acc[...]
            xs, rstd = _standardize(x_ref[...])
            dsh_ref[...] += jnp.sum(dh, axis=0, keepdims=True)
            dsc_ref[...] += jnp.sum(dh * xs, axis=0, keepdims=True)
            dx_ref[...] = _standardize_bwd(dh * (1.0 + mod_ref[:, D:2 * D]), xs, rstd) + dxr_ref[...]

    row = pl.BlockSpec((tm, D), lambda i, k: (i, 0))
    vec = pl.BlockSpec((1, D), lambda i, k: (0, 0))
    return _hosted(
        body, job, (dproj, wg, x, mod, dx_res), name="proj_bwd",
        grid=(S // tm, nk),
        in_specs=[pl.BlockSpec((tm, tk), lambda i, k: (i, k)),
                  pl.BlockSpec((None, D, tk), lambda i, k: (k // per, 0, k % per)),
                  row, pl.BlockSpec((1, 3 * D), lambda i, k: (0, 0)), row],
        out_specs=[row, vec, vec],
        out_shape=[jax.ShapeDtypeStruct((S, D), F32), jax.ShapeDtypeStruct((1, D), F32), jax.ShapeDtypeStruct((1, D), F32)],
        scratch_shapes=[pltpu.VMEM((tm, D), F32)],
        semantics=("arbitrary", "arbitrary"))


def _grad_w(a, b, mid, prev, name, tm, tn, b_col0=0):
    S, M = a.shape
    shape = prev.shape
    n_shard = shape[-1]
    per = n_shard // tn
    nm = M // tm
    lead = len(mid)

    def body(*refs):
        a_ref, b_ref, o_ref = refs[0], refs[1], refs[-1]
        o_ref[...] = _dot_tn(a_ref[...], b_ref[...]).astype(BF)

    in_specs = [pl.BlockSpec((S, tm), lambda m, n: (0, m)),
                pl.BlockSpec((S, tn), lambda m, n: (0, b_col0 // tn + n))]
    args = [a, b]
    aliases = {}
    if not isinstance(prev, jax.ShapeDtypeStruct):
        in_specs.append(pl.BlockSpec(memory_space=pl.ANY))
        args.append(prev)
        aliases = {2: 0}
    return pl.pallas_call(
        body, name=name,
        grid=(nm, N_CHIPS * per),
        in_specs=in_specs,
        out_specs=pl.BlockSpec((None,) + (None,) * lead + (tm, tn), lambda m, n: (n // per,) + tuple(mid) + (m, n % per)),
        out_shape=jax.ShapeDtypeStruct(shape, BF),
        input_output_aliases=aliases,
        compiler_params=_params(dimension_semantics=("parallel", "parallel")),
    )(*args)


def _all_gather8(x, name):
    R, N = x.shape

    def body(x_ref, out_ref, send_sems, recv_sems):
        mx, my, mc = lax.axis_index("x"), lax.axis_index("y"), lax.axis_index("c")
        me = 4 * mx + 2 * my + mc
        out_ref[me] = x_ref[...]
        copies = []
        for k in range(1, N_DEV):
            peer = (_flip(mx, k & 4), _flip(my, k & 2), _flip(mc, k & 1))
            cp = pltpu.make_async_remote_copy(src_ref=x_ref, dst_ref=out_ref.at[me], send_sem=send_sems.at[k - 1],
                                              recv_sem=recv_sems.at[k - 1], device_id=peer, device_id_type=MESH)
            cp.start()
            copies.append(cp)
        for cp in copies:
            cp.wait()

    return pl.pallas_call(
        body, name=name,
        in_specs=[pl.BlockSpec(memory_space=pltpu.VMEM)],
        out_specs=pl.BlockSpec(memory_space=pltpu.VMEM),
        out_shape=jax.ShapeDtypeStruct((N_DEV, R, N), F32),
        scratch_shapes=[pltpu.SemaphoreType.DMA((N_DEV - 1,)), pltpu.SemaphoreType.DMA((N_DEV - 1,))],
        compiler_params=_params(),
    )(x)


def _rows2d(a):
    return a.reshape(-1, a.shape[-1])


def _tile_rows(rows, cols, n_arrays):
    budget = (24 << 20) // (n_arrays * 2 * 4 * cols)
    if rows <= budget:
        return rows
    tm = 8
    for cand in range(8, budget + 1, 8):
        if rows % cand == 0:
            tm = cand
    return tm


def _add2(a, b):
    a2, b2 = _rows2d(a), _rows2d(b)
    rows, cols = a2.shape
    tm = _tile_rows(rows, cols, 2)

    def body(a_ref, b_ref, out_ref):
        out_ref[...] = (a_ref[...].astype(F32) + b_ref[...].astype(F32)).astype(BF)

    spec = pl.BlockSpec((tm, cols), lambda i: (i, 0))
    out = pl.pallas_call(
        body, name="sum_cores", grid=(rows // tm,), in_specs=[spec, spec], out_specs=spec,
        out_shape=jax.ShapeDtypeStruct((rows, cols), BF), compiler_params=_params(dimension_semantics=("parallel",)),
    )(a2, b2)
    return out.reshape(a.shape)


def _sum4(own, got):
    o2 = _rows2d(own)
    rows, cols = o2.shape
    g3 = got.reshape(3, rows, cols)
    tm = _tile_rows(rows, cols, 3)

    def body(o_ref, g_ref, out_ref):
        out_ref[...] = ((o_ref[...].astype(F32) + g_ref[0].astype(F32)) + g_ref[1].astype(F32)) + g_ref[2].astype(F32)

    out = pl.pallas_call(
        body, name="sum_partials",
        grid=(rows // tm,),
        in_specs=[pl.BlockSpec((tm, cols), lambda i: (i, 0)), pl.BlockSpec((3, tm, cols), lambda i: (0, i, 0))],
        out_specs=pl.BlockSpec((tm, cols), lambda i: (i, 0)),
        out_shape=jax.ShapeDtypeStruct((rows, cols), F32),
        compiler_params=_params(dimension_semantics=("parallel",)),
    )(o2, g3)
    return out.reshape(own.shape)


def _adamw(w, m, v, groups, name):
    shape = w.shape
    w2, m2, v2 = _rows2d(w), _rows2d(m), _rows2d(v)
    rows, cols = w2.shape
    ng = len(groups)
    n = len(groups[0])
    slab = rows // ng
    gs = [_rows2d(g) for grp in groups for g in grp]
    tm = _tile_rows(slab, cols, 7 + n)
    tiles = slab // tm
    c1 = 1.0 / (1.0 - ADAM_B1 ** ADAM_STEP)
    c2 = 1.0 / (1.0 - ADAM_B2 ** ADAM_STEP)

    def body(*refs):
        w_ref, m_ref, v_ref = refs[:3]
        g_refs = refs[3:3 + ng * n]
        go_ref, d_ref, mo_ref, vo_ref = refs[3 + ng * n:]
        which = pl.program_id(0)
        for s in range(ng):
            @pl.when(which == s)
            def _(s=s):
                g = g_refs[s * n][...]
                for r in g_refs[s * n + 1:(s + 1) * n]:
                    g = g + r[...]
                mn = ADAM_B1 * m_ref[...] + (1.0 - ADAM_B1) * g
                vn = ADAM_B2 * v_ref[...] + (1.0 - ADAM_B2) * (g * g)
                go_ref[...] = g
                mo_ref[...] = mn
                vo_ref[...] = vn
                d_ref[...] = -ADAM_LR * ((mn * c1) / (jnp.sqrt(vn * c2) + ADAM_EPS) + ADAM_WD * w_ref[...])

    spec = pl.BlockSpec((tm, cols), lambda s, i: (s * tiles + i, 0))
    g_specs = [pl.BlockSpec((tm, cols), lambda s, i, k=k: (jnp.where(s == k, i, jnp.where(s < k, 0, tiles - 1)), 0))
               for k in range(ng) for _ in range(n)]
    outs = pl.pallas_call(
        body, name=name,
        grid=(ng, tiles),
        in_specs=[spec] * 3 + g_specs,
        out_specs=[spec] * 4,
        out_shape=[jax.ShapeDtypeStruct((rows, cols), F32)] * 4,
        compiler_params=_params(dimension_semantics=("arbitrary", "arbitrary")),
    )(w2, m2, v2, *gs)
    return [o.reshape(shape) for o in outs]


def _lower_bounds(r0, r1):
    top = jnp.maximum(r0, r1)
    e0, e1 = jnp.exp(r0 - top), jnp.exp(r1 - top)
    p0, p1 = e0 / (e0 + e1), e1 / (e0 + e1)
    return p0 - p0, (p0 + p1) - p0


def _lbs_fwd(lb_raw):
    def body(lb_ref, out_ref):
        l0, l1 = _lower_bounds(lb_ref[0:1, :], lb_ref[1:2, :])
        out_ref[0:1, :] = l0
        out_ref[1:2, :] = l1

    return pl.pallas_call(body, name="lower_bounds", out_shape=jax.ShapeDtypeStruct(lb_raw.shape, F32), compiler_params=_params())(lb_raw)


def _mod_rows(c_all, w_mod, tn=768):
    _, D, cols = w_mod.shape

    def body(c_ref, w_ref, out_ref):
        out_ref[...] = _dot(c_ref[...].astype(BF), w_ref[...].astype(BF))

    return pl.pallas_call(
        body, name="mod_rows",
        grid=(DEPTH,),
        in_specs=[pl.BlockSpec((N_DEV, D), lambda l: (0, 0)), pl.BlockSpec((None, D, cols), lambda l: (l, 0, 0))],
        out_specs=pl.BlockSpec((N_DEV, cols), lambda l: (0, l)),
        out_shape=jax.ShapeDtypeStruct((N_DEV, DEPTH * cols), F32),
        compiler_params=_params(dimension_semantics=("parallel",)),
    )(c_all, w_mod)


def _grad_w_mod(c_all, dmod_cols):
    D = c_all.shape[1]
    cols = dmod_cols.shape[-1]

    def body(c_ref, d_ref, out_ref):
        out_ref[...] = _dot_tn(c_ref[...].astype(BF), d_ref[...].astype(BF))

    return pl.pallas_call(
        body, name="grad_w_mod",
        grid=(DEPTH,),
        in_specs=[pl.BlockSpec((N_DEV, D), lambda l: (0, 0)), pl.BlockSpec((None, N_DEV, cols), lambda l: (l, 0, 0))],
        out_specs=pl.BlockSpec((None, D, cols), lambda l: (l, 0, 0)),
        out_shape=jax.ShapeDtypeStruct((DEPTH, D, cols), F32),
        compiler_params=_params(dimension_semantics=("parallel",)),
    )(c_all, dmod_cols)


def _sum_devices(parts, lb_raw, n_lb):
    _, R, N = parts.shape

    def body(p_ref, lb_ref, out_ref, dlb_ref):
        acc = p_ref[0]
        for d in range(1, N_DEV):
            acc = acc + p_ref[d]
        out_ref[...] = acc
        dl = tuple(acc[0:1, n_lb + l * WIDTH:n_lb + (l + 1) * WIDTH] for l in range(DEPTH))
        _, pull = jax.vjp(_lower_bounds, lb_ref[0:1, :], lb_ref[1:2, :])
        d0, d1 = pull(dl)
        dlb_ref[0:1, :] = d0
        dlb_ref[1:2, :] = d1

    return pl.pallas_call(
        body, name="sum_devices",
        out_shape=[jax.ShapeDtypeStruct((R, N), F32), jax.ShapeDtypeStruct(lb_raw.shape, F32)],
        compiler_params=_params(),
    )(parts, lb_raw)


def _pad_rows(a, rows=8):
    return jnp.concatenate([a, jnp.zeros((rows - a.shape[0],) + a.shape[1:], a.dtype)], axis=0)


def kernel(x, c, w_mod, b_mod, w_in, conv_w, hgrn_norm_w, lower_bounds, w_branch, w_out, ln_g, ln_b, loss_target, m_w_mod, m_b_mod, m_w_in, m_conv_w, m_hgrn_norm_w, m_lower_bounds, m_w_branch, m_w_out, m_ln_g, m_ln_b, v_w_mod, v_b_mod, v_w_in, v_conv_w, v_hgrn_norm_w, v_lower_bounds, v_w_branch, v_w_out, v_ln_g, v_ln_b):
    D = D_MODEL
    x0 = x[0]
    target = loss_target[0]
    S = x0.shape[0]
    mx, my, mc = lax.axis_index("x"), lax.axis_index("y"), lax.axis_index("c")
    chip = 2 * mx + my
    me = 2 * chip + mc
    mod_cols = 3 * D // N_CHIPS

    plan = _Plan(w_in.astype(BF), w_branch.astype(BF), w_out.astype(BF))
    plan.gathered[("in", 0)] = _run_job(_gather_job([plan.local["in"][0]]), "gather_first")[0]

    n_conv = DEPTH * 3 * (WIDTH // N_CHIPS)
    first = jnp.concatenate([c, conv_w.reshape(1, n_conv)], axis=1)
    first = _all_gather8(_pad_rows(first), "gather_c")
    c_all = first[:, 0, :D]
    conv_all = first[:, 0, D:].reshape(N_DEV, DEPTH, 3, WIDTH // N_CHIPS)[0::2]
    mod_part = _all_gather8(_mod_rows(c_all, w_mod), "gather_mod")[0::2]
    mod_part = lax.dynamic_index_in_dim(mod_part, me, axis=1, keepdims=False).reshape(N_CHIPS, DEPTH, mod_cols)
    mods = [(mod_part[:, l].reshape(1, 3 * D) + b_mod[l][None, :]) for l in range(DEPTH)]
    lbs = _lbs_fwd(lower_bounds).reshape(DEPTH, WIDTH // 128, 1, 128)
    loss_blk, dx, small = _local_step(x0, target, mods, lbs, conv_all, hgrn_norm_w, ln_g, ln_b, plan)

    whole = plan.finish()
    grads = {kind: [[whole[(kind, l)]] for l in range(DEPTH)] for kind in ("in", "br", "out")}

    n_mod, n_nw, n_lb, n_ln, n_cw = DEPTH * 3 * D, DEPTH * 128, DEPTH * WIDTH, DEPTH * D, DEPTH * 3 * WIDTH
    row = jnp.concatenate(
        [jnp.concatenate([small[l][0], small[l][1], small[l][2]], axis=1) for l in range(DEPTH)]
        + [jnp.sum(small[l][3], axis=0) for l in range(DEPTH)]
        + [small[l][4].reshape(1, WIDTH) for l in range(DEPTH)]
        + [small[l][5] for l in range(DEPTH)] + [small[l][6] for l in range(DEPTH)]
        + [jnp.transpose(small[l][7], (1, 0, 2)).reshape(1, 3 * WIDTH) for l in range(DEPTH)]
        + [loss_blk[0:1, :]], axis=1)
    gathered = _all_gather8(_pad_rows(row), "gather_small")
    off_nw = n_mod
    off_lb = off_nw + n_nw
    off_lng = off_lb + n_lb
    off_lnb = off_lng + n_ln
    off_cw = off_lnb + n_ln
    off_loss = off_cw + n_cw
    total, d_lower = _sum_devices(gathered, lower_bounds, off_lb)
    loss = total[0, off_loss]
    d_b_mod = total[0, :n_mod].reshape(DEPTH, 3 * D)
    d_norm_w = total[0, off_nw:off_lb].reshape(DEPTH, 128)
    d_ln_g = total[0, off_lng:off_lnb].reshape(DEPTH, D)
    d_ln_b = total[0, off_lnb:off_cw].reshape(DEPTH, D)
    d_conv = total[0, off_cw:off_loss].reshape(DEPTH, 3, N_CHIPS, WIDTH // N_CHIPS)
    d_conv = lax.dynamic_index_in_dim(d_conv, chip, axis=2, keepdims=False)
    dmod_all = gathered[:, 0, :n_mod].reshape(N_DEV, DEPTH, N_CHIPS, mod_cols)
    dmod_cols = jnp.transpose(lax.dynamic_index_in_dim(dmod_all, chip, axis=2, keepdims=False), (1, 0, 2))
    d_w_mod = _grad_w_mod(c_all, dmod_cols)

    res = {}
    res["w_mod"] = _adamw(w_mod, m_w_mod, v_w_mod, [[d_w_mod]], "adamw_w_mod")
    res["b_mod"] = _adamw(b_mod, m_b_mod, v_b_mod, [[d_b_mod]], "adamw_b_mod")
    res["w_in"] = _adamw(w_in, m_w_in, v_w_in, grads["in"], "adamw_w_in")
    res["conv_w"] = _adamw(conv_w, m_conv_w, v_conv_w, [[d_conv]], "adamw_conv_w")
    res["hgrn_norm_w"] = _adamw(hgrn_norm_w, m_hgrn_norm_w, v_hgrn_norm_w, [[d_norm_w]], "adamw_norm_w")
    res["lower_bounds"] = _adamw(lower_bounds, m_lower_bounds, v_lower_bounds, [[d_lower]], "adamw_lower_bounds")
    res["w_branch"] = _adamw(w_branch, m_w_branch, v_w_branch, grads["br"], "adamw_w_branch")
    res["w_out"] = _adamw(w_out, m_w_out, v_w_out, grads["out"], "adamw_w_out")
    res["ln_g"] = _adamw(ln_g, m_ln_g, v_ln_g, [[d_ln_g]], "adamw_ln_g")
    res["ln_b"] = _adamw(ln_b, m_ln_b, v_ln_b, [[d_ln_b]], "adamw_ln_b")
    names = ["w_mod", "b_mod", "w_in", "conv_w", "hgrn_norm_w", "lower_bounds", "w_branch", "w_out", "ln_g", "ln_b"]
    return (loss, dx[None], *[res[n][0] for n in names], *[res[n][1] for n in names],
            *[res[n][2] for n in names], *[res[n][3] for n in names])


class _Plan:
    def __init__(self, w_in, w_br, w_out):
        self.local = {"in": w_in, "br": w_br, "out": w_out}
        self.gathered, self.grads, self.chip_sums, self.scattered, self.pending = {}, {}, {}, {}, {}

    def _gather(self, keys):
        return ("gather", keys), _gather_job([self.local[kind][layer] for kind, layer in keys])

    def _to_sibling(self, keys):
        return ("to_sibling", keys), _to_sibling_job([self.grads[key] for key in keys])

    def _scatter(self, keys):
        return ("scatter", keys), _scatter_job([self.chip_sums[key] for key in keys])

    def job(self, stage, l):
        parts = []
        if stage == "proj_fwd":
            parts = [self._gather([("br", l), ("out", l)])]
        elif stage == "attn_fwd" and l + 1 < DEPTH:
            parts = [self._gather([("in", l + 1)])]
        elif stage == "attn_bwd":
            parts = [self._to_sibling([("out", l), ("br", l)])] + ([self._scatter([("in", l + 1)])] if l + 1 < DEPTH else [])
        elif stage == "hgrn_bwd":
            parts = [self._scatter([("out", l), ("br", l)])]
        elif stage == "proj_bwd":
            parts = [self._to_sibling([("in", l)])]
        self.pending[(stage, l)] = [(tag, len(job.outs)) for tag, job in parts]
        return _join_jobs([job for _, job in parts])

    def done(self, stage, l, outs):
        if outs is None:
            return
        at = 0
        for (what, keys), n_outs in self.pending[(stage, l)]:
            mine, at = outs[at:at + n_outs], at + n_outs
            for n, key in enumerate(keys):
                if what == "gather":
                    self.gathered[key] = mine[n]
                elif what == "to_sibling":
                    self.chip_sums[key] = _add2(mine[n], mine[len(keys) + n])
                else:
                    self.scattered[key] = (mine[n], mine[len(keys) + n])

    def finish(self):
        keys = [(kind, l) for kind in ("in", "br", "out") for l in range(DEPTH)]
        _, last = self._scatter([("in", 0)])
        own, got = _run_job(last, "scatter_last")
        self.scattered[("in", 0)] = (own, got)
        halves = [_sum4(*self.scattered[key]) for key in keys]
        whole = _run_job(_place_job(halves, [self.grads[key].shape[1:] for key in keys]), "place_halves")
        return dict(zip(keys, whole))


def _local_step(x0, target, mods, lbs, conv_all, hgrn_norm_w, ln_g, ln_b, plan):
    D = D_MODEL
    after, before = _attn_consts()
    hg_consts = _hgrn_consts()

    saved = []
    xl = x0
    for l in range(DEPTH):
        (proj, h), got = _proj_fwd(xl, mods[l], plan.gathered[("in", l)], plan.job("proj_fwd", l))
        plan.done("proj_fwd", l, got)
        (o_a, tot), got = _attn_fwd(proj, after, plan.job("attn_fwd", l))
        plan.done("attn_fwd", l, got)
        o_b, states = _hgrn_fwd(proj, lbs[l], hg_consts)
        ys = _branch_fwd(proj, o_a, o_b, hgrn_norm_w, conv_all, l)
        x_next, merged, y = _merge_fwd(xl, mods[l], proj, ys, plan.gathered[("br", l)], plan.gathered[("out", l)], ln_g, ln_b, l)
        saved.append((xl, proj, h, o_a, tot, o_b, states, ys, merged, y))
        xl = x_next
    dx, loss_blk = _loss_head(xl, target)

    small = [None] * DEPTH
    for l in reversed(range(DEPTH)):
        xin, proj, h, o_a, tot, o_b, states, ys, merged, y = saved[l]
        dx_res, dy, dp, dg, dy_a, dy_b, dy_c, dln_g, dln_b, dgate = _merge_bwd(
            dx, xin, y, mods[l], proj, ys, plan.gathered[("br", l)], plan.gathered[("out", l)], ln_g, l)
        plan.grads[("out", l)] = _grad_w_out(merged, dy)
        g_br = jax.ShapeDtypeStruct((N_CHIPS, 3, WIDTH, D // N_CHIPS), BF)
        for i in range(3):
            g_br = _grad_w(ys[i], dp, (i,), g_br, "grad_w_branch", tm=WIDTH, tn=D // N_CHIPS, b_col0=i * D)
        plan.grads[("br", l)] = g_br
        d_oa, d_ob, dz_a, dz_b, dpre, dpost, du, dz_c, dnorm_w, dconv_w = _branch_bwd(proj, o_a, o_b, hgrn_norm_w, conv_all, dy_a, dy_b, dy_c, l)
        (dq_a, dk_a, dv_a), got = _attn_bwd(proj, d_oa, tot, after, before, plan.job("attn_bwd", l))
        plan.done("attn_bwd", l, got)
        (dq_b, df_b, di_b, dlb), got = _hgrn_bwd(proj, lbs[l], states, d_ob, hg_consts, plan.job("hgrn_bwd", l))
        plan.done("hgrn_bwd", l, got)
        dproj = jnp.concatenate([dq_a, dk_a, dv_a, dz_a, dq_b, df_b, di_b, dz_b, dpre, dpost, du, dz_c, dg], axis=1)
        plan.grads[("in", l)] = _grad_w(h, dproj, (), jax.ShapeDtypeStruct((N_CHIPS, D, IN_COLS // N_CHIPS), BF), "grad_w_in", tm=512, tn=768)
        (dx, dshift, dscale), got = _proj_bwd(dproj, plan.gathered[("in", l)], xin, mods[l], dx_res, plan.job("proj_bwd", l))
        plan.done("proj_bwd", l, got)
        small[l] = (dshift, dscale, dgate, dnorm_w, dlb, dln_g, dln_b, dconv_w)
    return loss_blk, dx, small


def _grad_w_out(merged, dy):
    S, D = merged.shape
    q = D // N_CHIPS

    def body(a_ref, b_ref, o_ref):
        o_ref[...] = _dot_tn(a_ref[...], b_ref[...]).astype(BF)

    return pl.pallas_call(
        body, name="grad_w_out",
        grid=(N_CHIPS,),
        in_specs=[pl.BlockSpec((S, q), lambda j: (0, j)), pl.BlockSpec((S, D), lambda j: (0, 0))],
        out_specs=pl.BlockSpec((None, q, D), lambda j: (j, 0, 0)),
        out_shape=jax.ShapeDtypeStruct((N_CHIPS, q, D), BF),
        compiler_params=_params(dimension_semantics=("parallel",)),
    )(merged, dy)
```

```python
import functools
import math

import numpy as np
import jax
import jax.numpy as jnp
from jax import lax
from jax.experimental import pallas as pl
from jax.experimental.pallas import tpu as pltpu

F32 = jnp.float32
BF = jnp.bfloat16
MESH = pl.DeviceIdType.MESH

DEPTH = 2
D_MODEL = 1024
WIDTH = 512
IN_COLS = 12 * WIDTH + 3 * D_MODEL
N_CHIPS = 4
N_DEV = 8
SB_BLOCK = 128
SB_HEAD_DIM = 64
HG_CHUNK = 64
HG_DIM = 128
LN_EPS = 1e-5
RMS_EPS = 1e-6
ALPHA = (2.0 * DEPTH) ** 0.25
ADAM_LR, ADAM_B1, ADAM_B2, ADAM_EPS, ADAM_WD, ADAM_STEP = 0.001, 0.9, 0.999, 1e-08, 0.01, 10
VMEM_LIMIT = 56 << 20


def _params(**kw):
    return pltpu.CompilerParams(vmem_limit_bytes=VMEM_LIMIT, **kw)


def _dot(a, b):
    return jnp.dot(a, b, preferred_element_type=F32)


def _dot_nt(a, b):
    return lax.dot_general(a, b, (((1,), (1,)), ((), ())), preferred_element_type=F32)


def _dot_tn(a, b):
    return lax.dot_general(a, b, (((0,), (0,)), ((), ())), preferred_element_type=F32)


def _sigmoid(x):
    return 1.0 / (1.0 + jnp.exp(-x))


def _silu(x):
    return x * _sigmoid(x)


def _softplus(z):
    return jnp.maximum(z, 0.0) + jnp.log(1.0 + jnp.exp(-jnp.abs(z)))


def _split_dot(t, g, terms):
    acc = None
    rest = g
    for _ in range(terms):
        part = rest.astype(BF)
        rest = rest - part.astype(F32)
        d = _dot(t, part)
        acc = d if acc is None else acc + d
    return acc


def _split_dot_r(g, t, terms):
    acc = None
    rest = g
    for _ in range(terms):
        part = rest.astype(BF)
        rest = rest - part.astype(F32)
        d = _dot(part, t)
        acc = d if acc is None else acc + d
    return acc


def _standardize(x):
    mu = jnp.mean(x, axis=-1, keepdims=True)
    xc = x - mu
    var = jnp.mean(xc * xc, axis=-1, keepdims=True)
    rstd = lax.rsqrt(var + LN_EPS)
    return xc * rstd, rstd


def _standardize_bwd(dxs, xs, rstd):
    return rstd * (dxs - jnp.mean(dxs, axis=-1, keepdims=True) - xs * jnp.mean(dxs * xs, axis=-1, keepdims=True))


class _Job:
    def __init__(self, ins, outs, sems, make, alias=None):
        self.ins, self.outs, self.sems, self.make = list(ins), list(outs), list(sems), make
        self.alias = dict(alias or {})


def _join_jobs(jobs):
    jobs = [j for j in jobs if j is not None]
    if len(jobs) <= 1:
        return jobs[0] if jobs else None

    def make(ins, outs, sems):
        phases, i, o, s = [], 0, 0, 0
        for j in jobs:
            got = j.make(ins[i:i + len(j.ins)], outs[o:o + len(j.outs)], sems[s:s + len(j.sems)])
            i, o, s = i + len(j.ins), o + len(j.outs), s + len(j.sems)
            for n, phase in enumerate(got):
                if n == len(phases):
                    phases.append([])
                phases[n] += phase
        return phases

    alias, i, o = {}, 0, 0
    for j in jobs:
        alias.update({i + a: o + b for a, b in j.alias.items()})
        i, o = i + len(j.ins), o + len(j.outs)
    return _Job(sum([j.ins for j in jobs], []), sum([j.outs for j in jobs], []), sum([j.sems for j in jobs], []), make, alias)


def _flip(v, bit):
    return 1 - v if bit else v


def _halves(a, front=0):
    shape = a.shape
    lead = math.prod(shape[front:-2])
    return a.reshape(shape[:front] + (lead, 2, shape[-2] // 2, shape[-1]))


def _dma_sems(*shapes):
    return [pltpu.SemaphoreType.DMA(s) for s in shapes]


def _same(arrays):
    return [jax.ShapeDtypeStruct(a.shape, a.dtype) for a in arrays]


def _gather_job(slabs):
    n = len(slabs)

    def make(ins, outs, sems):
        send1, recv1, send2, recv2 = sems
        mx, my, mc = lax.axis_index("x"), lax.axis_index("y"), lax.axis_index("c")
        fetch, pass_on = [], []
        for a in range(n):
            ours = outs[a].at[2 * mx + my, :, mc]
            for k in range(1, N_CHIPS):
                px, py = _flip(mx, k & 2), _flip(my, k & 1)
                fetch.append(pltpu.make_async_remote_copy(
                    src_ref=ours, dst_ref=ours, send_sem=send1.at[a, k - 1], recv_sem=recv1.at[a, k - 1],
                    device_id=(px, py, mc), device_id_type=MESH))
                theirs = outs[a].at[2 * px + py, :, mc]
                pass_on.append(pltpu.make_async_remote_copy(
                    src_ref=theirs, dst_ref=theirs, send_sem=send2.at[a, k - 1], recv_sem=recv2.at[a, k - 1],
                    device_id=(mx, my, 1 - mc), device_id_type=MESH))
        return [fetch, pass_on]

    pairs = (n, N_CHIPS - 1)
    return _Job(slabs, _same(slabs), _dma_sems(pairs, pairs, pairs, pairs), make, {a: a for a in range(n)})


def _to_sibling_job(grads):
    n = len(grads)

    def make(ins, outs, sems):
        send_sems, recv_sems = sems
        mx, my, mc = lax.axis_index("x"), lax.axis_index("y"), lax.axis_index("c")
        return [[pltpu.make_async_remote_copy(
            src_ref=ins[a].at[:, :, 1 - mc], dst_ref=outs[a], send_sem=send_sems.at[a], recv_sem=recv_sems.at[a],
            device_id=(mx, my, 1 - mc), device_id_type=MESH) for a in range(n)]]

    outs = [jax.ShapeDtypeStruct(g.shape[:2] + g.shape[3:], g.dtype) for g in grads]
    return _Job(grads, outs, _dma_sems((n,), (n,)), make)


def _scatter_job(sums):
    n = len(sums)

    def make(ins, outs, sems):
        send_sems, recv_sems = sems
        mx, my, mc = lax.axis_index("x"), lax.axis_index("y"), lax.axis_index("c")
        copies = []
        for a in range(n):
            for k in range(1, N_CHIPS):
                px, py = _flip(mx, k & 2), _flip(my, k & 1)
                copies.append(pltpu.make_async_remote_copy(
                    src_ref=ins[a].at[2 * px + py], dst_ref=outs[a].at[k - 1], send_sem=send_sems.at[a, k - 1],
                    recv_sem=recv_sems.at[a, k - 1], device_id=(px, py, mc), device_id_type=MESH))
        return [copies]

    pairs = (n, N_CHIPS - 1)
    return _Job(sums, [jax.ShapeDtypeStruct((N_CHIPS - 1,) + s.shape[1:], s.dtype) for s in sums], _dma_sems(pairs, pairs), make)


def _place_job(wholes):
    n = len(wholes)

    def make(ins, outs, sems):
        send_sems, recv_sems = sems
        mx, my, mc = lax.axis_index("x"), lax.axis_index("y"), lax.axis_index("c")
        copies = []
        for a in range(n):
            here = outs[a].at[:, mc]
            copies.append(pltpu.make_async_remote_copy(src_ref=here, dst_ref=here, send_sem=send_sems.at[a], recv_sem=recv_sems.at[a],
                                                       device_id=(mx, my, 1 - mc), device_id_type=MESH))
        return [copies]

    return _Job(wholes, _same(wholes), _dma_sems((n,), (n,)), make, {a: a for a in range(n)})


def _run_phases(phases, first=0):
    for n, phase in enumerate(phases):
        if n >= first:
            for cp in phase:
                cp.start()
        for cp in phase:
            cp.wait()


def _run_job(job, name):
    k_in, k_out = len(job.ins), len(job.outs)

    def body(*refs):
        _run_phases(job.make(refs[:k_in], refs[k_in:k_in + k_out], refs[k_in + k_out:]))

    hbm = pl.BlockSpec(memory_space=pl.ANY)
    return pl.pallas_call(body, name=name, in_specs=[hbm] * k_in, out_specs=[hbm] * k_out, out_shape=job.outs,
                          scratch_shapes=job.sems, input_output_aliases=job.alias, compiler_params=_params())(*job.ins)


def _hosted(body, job, args, *, name, grid, in_specs, out_specs, out_shape, scratch_shapes=(), semantics):
    in_specs, out_specs, out_shape, scratch = list(in_specs), list(out_specs), list(out_shape), list(scratch_shapes)
    if job is None:
        outs = pl.pallas_call(body, name=name, grid=grid, in_specs=in_specs, out_specs=out_specs, out_shape=out_shape,
                              scratch_shapes=scratch, compiler_params=_params(dimension_semantics=semantics))(*args)
        return list(outs), None
    n_in, n_out, n_scr, k_in, k_out = len(in_specs), len(out_specs), len(scratch), len(job.ins), len(job.outs)

    def wrapped(*refs):
        ins, rest = refs[:n_in], refs[n_in:]
        job_ins, rest = rest[:k_in], rest[k_in:]
        outs, rest = rest[:n_out], rest[n_out:]
        job_outs, rest = rest[:k_out], rest[k_out:]
        scr, sems = rest[:n_scr], rest[n_scr:]
        ids = [pl.program_id(a) for a in range(len(grid))]
        first = functools.reduce(jnp.logical_and, [i == 0 for i in ids])
        last = functools.reduce(jnp.logical_and, [i == g - 1 for i, g in zip(ids, grid)])

        @pl.when(first)
        def _():
            for cp in job.make(job_ins, job_outs, sems)[0]:
                cp.start()

        body(*ins, *outs, *scr)

        @pl.when(last)
        def _():
            _run_phases(job.make(job_ins, job_outs, sems), first=1)

    hbm = pl.BlockSpec(memory_space=pl.ANY)
    outs = pl.pallas_call(
        wrapped, name=name, grid=grid, in_specs=in_specs + [hbm] * k_in, out_specs=out_specs + [hbm] * k_out,
        out_shape=out_shape + job.outs, scratch_shapes=scratch + job.sems,
        input_output_aliases={n_in + i: n_out + o for i, o in job.alias.items()},
        compiler_params=_params(dimension_semantics=("arbitrary",) * len(grid)))(*args, *job.ins)
    return list(outs[:n_out]), list(outs[n_out:])


def _proj_fwd(x, mod, wg, job=None, tm=512, tn=768):
    S, D = x.shape
    tm = min(tm, S)
    shard = wg.shape[-1]
    per = shard // tn

    def body(x_ref, mod_ref, w_ref, proj_ref, h_ref, hs):
        @pl.when(pl.program_id(1) == 0)
        def _():
            xs, _ = _standardize(x_ref[...])
            h = xs * (1.0 + mod_ref[:, D:2 * D]) + mod_ref[:, 0:D]
            hb = h.astype(BF)
            hs[...] = hb
            h_ref[...] = hb

        proj_ref[...] = _dot(hs[...], w_ref[...])

    return _hosted(
        body, job, (x, mod, wg), name="proj_fwd",
        grid=(S // tm, IN_COLS // tn),
        in_specs=[pl.BlockSpec((tm, D), lambda i, j: (i, 0)),
                  pl.BlockSpec((1, 3 * D), lambda i, j: (0, 0)),
                  pl.BlockSpec((None, D, tn), lambda i, j: (j // per, 0, j % per))],
        out_specs=[pl.BlockSpec((tm, tn), lambda i, j: (i, j)),
                   pl.BlockSpec((tm, D), lambda i, j: (i, 0))],
        out_shape=[jax.ShapeDtypeStruct((S, IN_COLS), F32), jax.ShapeDtypeStruct((S, D), BF)],
        scratch_shapes=[pltpu.VMEM((tm, D), BF)],
        semantics=("parallel", "arbitrary"))


SB_ROWS = 256
SB_KEYS = 256


def _attn_consts():
    j = np.arange(SB_KEYS)[:, None]
    s = np.arange(SB_KEYS)[None, :]
    from_here = np.concatenate([(j >= s), (j >= s)], axis=0).astype(np.float32)
    up_to = np.concatenate([(j <= s), (j <= s)], axis=0).astype(np.float32)
    return jnp.asarray(from_here, BF), jnp.asarray(up_to, BF)


def _hi_lo(x):
    hi = lax.bitcast_convert_type(lax.bitcast_convert_type(x, jnp.uint32) & jnp.uint32(0xFFFF0000), F32)
    return hi.astype(BF), (x - hi).astype(BF)


def _sums_r(x, t2):
    hi, lo = _hi_lo(x)
    return _dot(jnp.concatenate([hi, lo], axis=1), t2)


def _all_lanes(col, lanes):
    return jnp.broadcast_to(col, (col.shape[0], lanes))


def _attn_rows(ref, r0, rows, head0, scale=None):
    v = ref[pl.ds(r0, rows), :]
    if scale is not None:
        v = v * scale
    return jnp.concatenate([jnp.where(head0, v, 0.0), jnp.where(head0, 0.0, v)], axis=0).astype(BF)


def _attn_scores(q2n, k_ref, kj, t2, from_here_ref, masked):
    c0 = pl.multiple_of(kj * SB_KEYS, SB_KEYS)
    kb = k_ref[pl.ds(c0, SB_KEYS), :].astype(BF)
    zn = _dot_nt(q2n, kb)
    lsb = jnp.minimum(zn, 0.0) - jnp.log(1.0 + jnp.exp(-jnp.abs(zn)))
    valid = None
    if masked:
        valid = (lax.broadcasted_iota(jnp.int32, zn.shape, 1) + kj * SB_KEYS) < t2
        lsb = jnp.where(valid, lsb, 0.0)
    return c0, kb, zn, valid, lsb, _sums_r(lsb, from_here_ref[...])


def _attn_fwd(proj, from_here, job=None):
    S = proj.shape[0]
    TQ = SB_ROWS
    assert S % TQ == 0 and SB_KEYS == TQ
    scale = SB_HEAD_DIM ** -0.5

    def body(q_ref, k_ref, v_ref, from_here_ref, o_ref, tot_ref, run, acc):
        head0 = lax.broadcasted_iota(jnp.int32, (1, 2 * SB_HEAD_DIM), 1) < SB_HEAD_DIM

        def qloop(qi, _):
            r0 = pl.multiple_of(qi * TQ, TQ)
            q2n = _attn_rows(q_ref, r0, TQ, head0, -scale)
            trow = lax.broadcasted_iota(jnp.int32, (TQ, SB_KEYS), 0) + qi * TQ
            t2 = jnp.concatenate([trow, trow], axis=0)
            run[...] = jnp.zeros_like(run)
            acc[...] = jnp.zeros_like(acc)

            def step(kj, masked):
                c0, _, zn, valid, _, sums = _attn_scores(q2n, k_ref, kj, t2, from_here_ref, masked)
                r = run[...]
                e = sums - zn + jnp.concatenate([r, r], axis=1)
                if masked:
                    e = jnp.where(valid, e, -jnp.inf)
                acc[...] += _dot(jnp.exp(e).astype(BF), v_ref[pl.ds(c0, SB_KEYS), :].astype(BF))
                run[...] = r + _all_lanes(sums[:, 0:1], SB_BLOCK)

            step(qi, True)

            def below(n, _):
                step(qi - 1 - n, False)
                return 0

            lax.fori_loop(0, qi, below, 0)
            o_ref[pl.ds(r0, TQ), :] = jnp.where(head0, acc[0:TQ, :], acc[TQ:2 * TQ, :])
            tot_ref[0, pl.ds(r0, TQ), :] = run[0:TQ, :]
            tot_ref[1, pl.ds(r0, TQ), :] = run[TQ:2 * TQ, :]
            return 0

        lax.fori_loop(0, S // TQ, qloop, 0)

    col = lambda base: pl.BlockSpec((S, 128), lambda p, base=base: (0, base + p))
    return _hosted(
        body, job, (proj, proj, proj, from_here), name="attn_fwd",
        grid=(WIDTH // 128,),
        in_specs=[col(0), col(4), col(8), pl.BlockSpec(from_here.shape, lambda p: (0, 0))],
        out_specs=[pl.BlockSpec((S, 128), lambda p: (0, p)), pl.BlockSpec((None, 2, S, 128), lambda p: (p, 0, 0, 0))],
        out_shape=[jax.ShapeDtypeStruct((S, WIDTH), F32), jax.ShapeDtypeStruct((WIDTH // 128, 2, S, 128), F32)],
        scratch_shapes=[pltpu.VMEM((2 * TQ, SB_BLOCK), F32), pltpu.VMEM((2 * TQ, 128), F32)],
        semantics=("parallel",))


def _attn_bwd(proj, d_o, tot, from_here, up_to, job=None):
    S = proj.shape[0]
    TQ = SB_ROWS
    assert S % TQ == 0 and SB_KEYS == TQ
    scale = SB_HEAD_DIM ** -0.5

    def body(q_ref, k_ref, v_ref, do_ref, tot_ref, from_here_ref, up_to_ref, dq_ref, dk_ref, dv_ref, pre, cum, dq_acc, dk_acc, dv_acc):
        head0 = lax.broadcasted_iota(jnp.int32, (1, 2 * SB_HEAD_DIM), 1) < SB_HEAD_DIM
        dk_acc[...] = jnp.zeros_like(dk_acc)
        dv_acc[...] = jnp.zeros_like(dv_acc)

        def qloop(qi, _):
            r0 = pl.multiple_of(qi * TQ, TQ)
            q2n = _attn_rows(q_ref, r0, TQ, head0, -scale)
            do2 = _attn_rows(do_ref, r0, TQ, head0)
            trow = lax.broadcasted_iota(jnp.int32, (TQ, SB_KEYS), 0) + qi * TQ
            t2 = jnp.concatenate([trow, trow], axis=0)
            pre[0:TQ, :] = tot_ref[0, pl.ds(r0, TQ), :]
            pre[TQ:2 * TQ, :] = tot_ref[1, pl.ds(r0, TQ), :]
            cum[...] = jnp.zeros_like(cum)
            dq_acc[...] = jnp.zeros_like(dq_acc)

            def step(kj, masked):
                c0, kb, zn, valid, lsb, sums = _attn_scores(q2n, k_ref, kj, t2, from_here_ref, masked)
                later = pre[...] - _all_lanes(sums[:, 0:1], SB_BLOCK)
                pre[...] = later
                e = sums - zn + jnp.concatenate([later, later], axis=1)
                sig = jnp.exp(lsb - zn)
                if masked:
                    e = jnp.where(valid, e, -jnp.inf)
                    sig = jnp.where(valid, sig, 0.0)
                a = jnp.exp(e)
                w = _dot_nt(do2, v_ref[pl.ds(c0, SB_KEYS), :].astype(BF)) * a
                upto = _sums_r(w, up_to_ref[...])
                c = cum[...]
                dz = w - sig * (upto + jnp.concatenate([c, c], axis=1))
                cum[...] = c + _all_lanes(upto[:, SB_KEYS - 1:SB_KEYS], SB_BLOCK)
                dzb = dz.astype(BF)
                dq_acc[...] += _dot(dzb, kb)
                dk_acc[pl.ds(c0, SB_KEYS), :] += _dot_tn(dzb, q2n)
                dv_acc[pl.ds(c0, SB_KEYS), :] += _dot_tn(a.astype(BF), do2)

            def below(kj, _):
                step(kj, False)
                return 0

            lax.fori_loop(0, qi, below, 0)
            step(qi, True)
            dq_ref[pl.ds(r0, TQ), :] = (jnp.where(head0, dq_acc[0:TQ, :], dq_acc[TQ:2 * TQ, :]) * scale).astype(BF)
            return 0

        lax.fori_loop(0, S // TQ, qloop, 0)
        dk_ref[...] = (-dk_acc[...]).astype(BF)
        dv_ref[...] = dv_acc[...].astype(BF)

    col = lambda base: pl.BlockSpec((S, 128), lambda p, base=base: (0, base + p))
    whole = lambda a: pl.BlockSpec(a.shape, lambda p: (0, 0))
    out = jax.ShapeDtypeStruct((S, WIDTH), BF)
    big = pltpu.VMEM((2 * TQ, SB_BLOCK), F32)
    return _hosted(
        body, job, (proj, proj, proj, d_o, tot, from_here, up_to), name="attn_bwd",
        grid=(WIDTH // 128,),
        in_specs=[col(0), col(4), col(8), col(0), pl.BlockSpec((None, 2, S, 128), lambda p: (p, 0, 0, 0)), whole(from_here), whole(up_to)],
        out_specs=[col(0), col(0), col(0)],
        out_shape=[out, out, out],
        scratch_shapes=[big, big, big, pltpu.VMEM((S, 128), F32), pltpu.VMEM((S, 128), F32)],
        semantics=("parallel",))


HG_LEVELS = (32, 16, 8, 4, 2, 1)


def _hgrn_consts():
    C = HG_CHUNK
    t = np.arange(C)[:, None]
    s = np.arange(C)[None, :]
    rows = [(s <= t), (s > t)]
    masks = [(t == s)]
    for m in HG_LEVELS:
        two = 2 * m
        mid = (t // two) * two + m
        right = (t % two) >= m
        rows.append(right & (s >= mid) & (s <= t))
        rows.append((~right) & (s > t) & (s <= mid - 1))
        masks.append(((t // two) == (s // two)) & right & ((s % two) < m))
    rows.append(np.ones((HG_DIM, C), bool))
    tri = np.concatenate(rows, axis=0).astype(np.float32)
    return (jnp.asarray(tri, BF), jnp.asarray(tri.T.copy(), BF), jnp.asarray(np.stack(masks).astype(np.float32), F32))


HG_SUM_ROWS = (HG_CHUNK,) * (2 + 2 * len(HG_LEVELS)) + (HG_DIM,)


@jax.custom_vjp
def _hgrn_sums(tri, tri_t, g):
    e = _split_dot(tri, g, 3)
    out, r = [], 0
    for n in HG_SUM_ROWS:
        out.append(e[r:r + n])
        r += n
    return tuple(out)


def _hgrn_sums_fwd(tri, tri_t, g):
    return _hgrn_sums(tri, tri_t, g), (tri, tri_t)


def _hgrn_sums_bwd(res, ds):
    tri, tri_t = res
    return jnp.zeros_like(tri), jnp.zeros_like(tri_t), _split_dot(tri_t, jnp.concatenate(ds, axis=0), 3)


_hgrn_sums.defvjp(_hgrn_sums_fwd, _hgrn_sums_bwd)


def _bf_dot(a, b):
    return _dot(a.astype(BF), b.astype(BF))


def _bf_dot_nt(a, b):
    return _dot_nt(a.astype(BF), b.astype(BF))


def _bf_dot_tn(a, b):
    return _dot_tn(a.astype(BF), b.astype(BF))


@jax.custom_vjp
def _mm(a, b):
    return _bf_dot(a, b)


_mm.defvjp(lambda a, b: (_bf_dot(a, b), (a, b)), lambda r, ct: (_bf_dot_nt(ct, r[1]), _bf_dot_tn(r[0], ct)))


@jax.custom_vjp
def _mm_nt(a, b):
    return _bf_dot_nt(a, b)


_mm_nt.defvjp(lambda a, b: (_bf_dot_nt(a, b), (a, b)), lambda r, ct: (_bf_dot(ct, r[1]), _bf_dot_tn(ct, r[0])))


@jax.custom_vjp
def _mm_tn(a, b):
    return _bf_dot_tn(a, b)


_mm_tn.defvjp(lambda a, b: (_bf_dot_tn(a, b), (a, b)), lambda r, ct: (_bf_dot_nt(r[1], ct), _bf_dot(r[0], ct)))


def _hgrn_chunk(tri, tri_t, masks, qraw, fpre, v, st, lb):
    q = _silu(qraw)
    f = lb + (1.0 - lb) * _sigmoid(fpre)
    k = 1.0 - f
    e = _hgrn_sums(tri, tri_t, jnp.log(f))
    prefix, suffix, whole = e[0], e[1], e[-1]
    scores = masks[0] * _mm_nt(q, k)
    for n in range(len(HG_LEVELS)):
        scores = scores + masks[n + 1] * _mm_nt(q * jnp.exp(e[2 + 2 * n]), k * jnp.exp(e[3 + 2 * n]))
    o = _mm_nt(q * jnp.exp(prefix), st) + _mm(scores, v)
    st_new = st * jnp.exp(whole) + _mm_tn(v, k * jnp.exp(suffix))
    return o, st_new


def _hgrn_specs(S, consts):
    col = lambda base: pl.BlockSpec((S, 128), lambda p, base=base: (0, base + p))
    whole = [pl.BlockSpec(a.shape, lambda p, n=a.ndim: (0,) * n) for a in consts]
    return col, whole


def _hgrn_fwd(proj, lbs, consts):
    S = proj.shape[0]
    nc = S // HG_CHUNK

    def body(q_ref, f_ref, i_ref, lb_ref, tri_ref, trit_ref, mask_ref, o_ref, st_ref):
        tri, tri_t = tri_ref[...], trit_ref[...]
        masks = [mask_ref[n] for n in range(len(HG_LEVELS) + 1)]
        lb = lb_ref[...]

        def chunk(ci, st):
            r0 = pl.multiple_of(ci * HG_CHUNK, HG_CHUNK)
            rows = pl.ds(r0, HG_CHUNK)
            st_ref[ci] = st
            o, st_new = _hgrn_chunk(tri, tri_t, masks, q_ref[rows, :], f_ref[rows, :], i_ref[rows, :], st, lb)
            o_ref[rows, :] = o
            return st_new

        lax.fori_loop(0, nc, chunk, jnp.zeros((HG_DIM, HG_DIM), F32))

    col, whole = _hgrn_specs(S, consts)
    return pl.pallas_call(
        body, name="hgrn_fwd",
        grid=(WIDTH // 128,),
        in_specs=[col(16), col(20), col(24), pl.BlockSpec((None, 1, 128), lambda p: (p, 0, 0))] + whole,
        out_specs=[col(0), pl.BlockSpec((None, nc, HG_DIM, HG_DIM), lambda p: (p, 0, 0, 0))],
        out_shape=[jax.ShapeDtypeStruct((S, WIDTH), F32), jax.ShapeDtypeStruct((WIDTH // 128, nc, HG_DIM, HG_DIM), F32)],
        compiler_params=_params(dimension_semantics=("parallel",)),
    )(proj, proj, proj, lbs, *consts)


def _hgrn_bwd(proj, lbs, states, d_o, consts, job=None):
    S = proj.shape[0]
    nc = S // HG_CHUNK

    def body(q_ref, f_ref, i_ref, lb_ref, st_ref, do_ref, tri_ref, trit_ref, mask_ref, dq_ref, df_ref, di_ref, dlb_ref):
        masks = [mask_ref[n] for n in range(len(HG_LEVELS) + 1)]
        lb = lb_ref[...]
        fn = functools.partial(_hgrn_chunk, tri_ref[...], trit_ref[...], masks)

        def chunk(n, carry):
            d_st, dlb = carry
            ci = nc - 1 - n
            r0 = pl.multiple_of(ci * HG_CHUNK, HG_CHUNK)
            rows = pl.ds(r0, HG_CHUNK)
            _, pull = jax.vjp(fn, q_ref[rows, :], f_ref[rows, :], i_ref[rows, :], st_ref[ci], lb)
            dq, df, di, d_prev, dl = pull((do_ref[rows, :], d_st))
            dq_ref[rows, :] = dq.astype(BF)
            df_ref[rows, :] = df.astype(BF)
            di_ref[rows, :] = di.astype(BF)
            return d_prev, dlb + dl

        _, dlb = lax.fori_loop(0, nc, chunk, (jnp.zeros((HG_DIM, HG_DIM), F32), jnp.zeros((1, HG_DIM), F32)))
        dlb_ref[...] = dlb

    col, whole = _hgrn_specs(S, consts)
    head = pl.BlockSpec((None, 1, 128), lambda p: (p, 0, 0))
    out = jax.ShapeDtypeStruct((S, WIDTH), BF)
    return _hosted(
        body, job, (proj, proj, proj, lbs, states, d_o, *consts), name="hgrn_bwd",
        grid=(WIDTH // 128,),
        in_specs=[col(16), col(20), col(24), head, pl.BlockSpec((None, nc, HG_DIM, HG_DIM), lambda p: (p, 0, 0, 0)), col(0)] + whole,
        out_specs=[col(0), col(0), col(0), head],
        out_shape=[out, out, out, jax.ShapeDtypeStruct((WIDTH // 128, 1, 128), F32)],
        semantics=("parallel",))


def _shift_down(x, n):
    rows = lax.broadcasted_iota(jnp.int32, x.shape, 0)
    return jnp.where(rows >= n, pltpu.roll(x, n, 0), 0.0)


def _shift_up(x, n):
    S = x.shape[0]
    rows = lax.broadcasted_iota(jnp.int32, x.shape, 0)
    return jnp.where(rows < S - n, pltpu.roll(x, S - n, 0), 0.0)


def _branch_fwd(proj, o_a, o_b, norm_w, conv_w, layer):
    S = proj.shape[0]

    def body(oa_ref, za_ref, ob_ref, zb_ref, nw_ref, pre_ref, post_ref, u_ref, zc_ref, cw_ref, ya_ref, yb_ref, yc_ref):
        ya_ref[...] = (oa_ref[...] * _silu(za_ref[...])).astype(BF)
        ob = ob_ref[...]
        rn = lax.rsqrt(jnp.mean(ob * ob, axis=-1, keepdims=True) + RMS_EPS)
        yb_ref[...] = (ob * rn * nw_ref[layer:layer + 1, :] * _silu(zb_ref[...])).astype(BF)
        pu = pre_ref[...] * u_ref[...]
        conv = cw_ref[2:3, :] * pu + cw_ref[1:2, :] * _shift_down(pu, 1) + cw_ref[0:1, :] * _shift_down(pu, 2)
        yc_ref[...] = (post_ref[...] * conv * _silu(zc_ref[...])).astype(BF)

    col = lambda base: pl.BlockSpec((S, 128), lambda p, base=base: (0, base + p))
    out = jax.ShapeDtypeStruct((S, WIDTH), BF)
    return pl.pallas_call(
        body, name="branch_fwd",
        grid=(WIDTH // 128,),
        in_specs=[col(0), col(12), col(0), col(28), pl.BlockSpec(norm_w.shape, lambda p: (0, 0)),
                  col(32), col(36), col(40), col(44), pl.BlockSpec((None, None, 3, 128), lambda p: (p, layer, 0, 0))],
        out_specs=[col(0), col(0), col(0)],
        out_shape=[out, out, out],
        compiler_params=_params(dimension_semantics=("parallel",)),
    )(o_a, proj, o_b, proj, norm_w, proj, proj, proj, proj, conv_w)


def _branch_bwd(proj, o_a, o_b, norm_w, conv_w, dy_a, dy_b, dy_c, layer):
    S = proj.shape[0]

    def dsilu(z):
        s = _sigmoid(z)
        return s * z, s * (1.0 + z * (1.0 - s))

    def body(oa_ref, za_ref, ob_ref, zb_ref, nw_ref, pre_ref, post_ref, u_ref, zc_ref, cw_ref, dya_ref, dyb_ref, dyc_ref,
             doa_ref, dob_ref, dza_ref, dzb_ref, dpre_ref, dpost_ref, du_ref, dzc_ref, dnw_ref, dcw_ref):
        dya = dya_ref[...]
        sa, dsa = dsilu(za_ref[...])
        doa_ref[...] = dya * sa
        dza_ref[...] = (dya * oa_ref[...] * dsa).astype(BF)

        dyb = dyb_ref[...]
        ob = ob_ref[...]
        nw = nw_ref[layer:layer + 1, :]
        sb, dsb = dsilu(zb_ref[...])
        rn = lax.rsqrt(jnp.mean(ob * ob, axis=-1, keepdims=True) + RMS_EPS)
        on = ob * rn
        dzb_ref[...] = (dyb * on * nw * dsb).astype(BF)
        don_w = dyb * sb
        dnw_ref[...] = jnp.sum(don_w * on, axis=0, keepdims=True)
        don = don_w * nw
        dob_ref[...] = rn * (don - on * jnp.mean(don * on, axis=-1, keepdims=True))

        dyc = dyc_ref[...]
        pre, post, u = pre_ref[...], post_ref[...], u_ref[...]
        sc, dsc = dsilu(zc_ref[...])
        pu = pre * u
        pu1, pu2 = _shift_down(pu, 1), _shift_down(pu, 2)
        conv = cw_ref[2:3, :] * pu + cw_ref[1:2, :] * pu1 + cw_ref[0:1, :] * pu2
        dzc_ref[...] = (dyc * post * conv * dsc).astype(BF)
        dpost_ref[...] = (dyc * conv * sc).astype(BF)
        dconv = dyc * post * sc
        dcw_ref[0:1, :] = jnp.sum(dconv * pu2, axis=0, keepdims=True)
        dcw_ref[1:2, :] = jnp.sum(dconv * pu1, axis=0, keepdims=True)
        dcw_ref[2:3, :] = jnp.sum(dconv * pu, axis=0, keepdims=True)
        dpu = cw_ref[2:3, :] * dconv + cw_ref[1:2, :] * _shift_up(dconv, 1) + cw_ref[0:1, :] * _shift_up(dconv, 2)
        dpre_ref[...] = (dpu * u).astype(BF)
        du_ref[...] = (dpu * pre).astype(BF)

    col = lambda base: pl.BlockSpec((S, 128), lambda p, base=base: (0, base + p))
    f32 = jax.ShapeDtypeStruct((S, WIDTH), F32)
    bf = jax.ShapeDtypeStruct((S, WIDTH), BF)
    return pl.pallas_call(
        body, name="branch_bwd",
        grid=(WIDTH // 128,),
        in_specs=[col(0), col(12), col(0), col(28), pl.BlockSpec(norm_w.shape, lambda p: (0, 0)),
                  col(32), col(36), col(40), col(44), pl.BlockSpec((None, None, 3, 128), lambda p: (p, layer, 0, 0)),
                  col(0), col(0), col(0)],
        out_specs=[col(0)] * 8 + [pl.BlockSpec((None, 1, 128), lambda p: (p, 0, 0)), pl.BlockSpec((None, 3, 128), lambda p: (p, 0, 0))],
        out_shape=[f32, f32, bf, bf, bf, bf, bf, bf, jax.ShapeDtypeStruct((WIDTH // 128, 1, 128), F32),
                   jax.ShapeDtypeStruct((WIDTH // 128, 3, 128), F32)],
        compiler_params=_params(dimension_semantics=("parallel",)),
    )(o_a, proj, o_b, proj, norm_w, proj, proj, proj, proj, conv_w, dy_a, dy_b, dy_c)


def _branch_proj(y_refs, wb_ref):
    out = []
    for i in range(3):
        yv = y_refs[i][...]
        out.append(jnp.concatenate([_dot(yv, wb_ref[j, i]) for j in range(N_CHIPS)], axis=1))
    return out


def _merge_fwd(x, mod, proj, ys, wb, wo, ln_g, ln_b, layer, tm=256):
    S, D = x.shape
    tm = min(tm, S)

    def body(x_ref, mod_ref, ga_ref, gb_ref, gc_ref, ya_ref, yb_ref, yc_ref, wb_ref, wo_ref, g_ref, b_ref, xo_ref, mg_ref, y_ref):
        ps = _branch_proj((ya_ref, yb_ref, yc_ref), wb_ref)
        merged = _sigmoid(ga_ref[...]) * ps[0] + _sigmoid(gb_ref[...]) * ps[1] + _sigmoid(gc_ref[...]) * ps[2]
        mb = merged.astype(BF)
        mg_ref[...] = mb
        y = _dot(mb, wo_ref[...].reshape(D, D))
        y_ref[...] = y
        r = ALPHA * x_ref[...] + (1.0 + mod_ref[:, 2 * D:3 * D]) * y
        xn, _ = _standardize(r)
        xo_ref[...] = xn * g_ref[layer:layer + 1, :] + b_ref[layer:layer + 1, :]

    row = lambda w, c=0: pl.BlockSpec((tm, w), lambda i, c=c: (i, c))
    whole = lambda a: pl.BlockSpec(a.shape, lambda i, n=a.ndim: (0,) * n)
    return pl.pallas_call(
        body, name="merge_fwd",
        grid=(S // tm,),
        in_specs=[row(D), whole(mod), row(D, 6), row(D, 7), row(D, 8), row(WIDTH), row(WIDTH), row(WIDTH), whole(wb),
                  whole(wo), whole(ln_g), whole(ln_b)],
        out_specs=[row(D), row(D), row(D)],
        out_shape=[jax.ShapeDtypeStruct((S, D), F32), jax.ShapeDtypeStruct((S, D), BF), jax.ShapeDtypeStruct((S, D), F32)],
        compiler_params=_params(dimension_semantics=("parallel",)),
    )(x, mod, proj, proj, proj, *ys, wb, wo, ln_g, ln_b)


def _merge_bwd(dxo, x, y, mod, proj, ys, wb, wo, ln_g, layer, tm=256):
    S, D = x.shape
    tm = min(tm, S)

    def body(dxo_ref, x_ref, y_ref, mod_ref, ga_ref, gb_ref, gc_ref, ya_ref, yb_ref, yc_ref, wb_ref, wo_ref, g_ref,
             dxr_ref, dy_ref, dp_ref, dg_ref, dya_ref, dyb_ref, dyc_ref, dlg_ref, dlb_ref, dgt_ref):
        @pl.when(pl.program_id(0) == 0)
        def _():
            dlg_ref[...] = jnp.zeros_like(dlg_ref)
            dlb_ref[...] = jnp.zeros_like(dlb_ref)
            dgt_ref[...] = jnp.zeros_like(dgt_ref)

        gate1 = 1.0 + mod_ref[:, 2 * D:3 * D]
        yv = y_ref[...]
        xn, rstd = _standardize(ALPHA * x_ref[...] + gate1 * yv)
        dxo = dxo_ref[...]
        dlg_ref[...] += jnp.sum(dxo * xn, axis=0, keepdims=True)
        dlb_ref[...] += jnp.sum(dxo, axis=0, keepdims=True)
        dr = _standardize_bwd(dxo * g_ref[layer:layer + 1, :], xn, rstd)
        dxr_ref[...] = ALPHA * dr
        dgt_ref[...] += jnp.sum(dr * yv, axis=0, keepdims=True)
        dyb = (gate1 * dr).astype(BF)
        dy_ref[...] = dyb
        dmerged = _dot_nt(dyb, wo_ref[...].reshape(D, D))
        ps = _branch_proj((ya_ref, yb_ref, yc_ref), wb_ref)
        quarter = D // N_CHIPS
        for i, (gate_ref, out_ref) in enumerate(((ga_ref, dya_ref), (gb_ref, dyb_ref), (gc_ref, dyc_ref))):
            sg = _sigmoid(gate_ref[...])
            dg_ref[:, i * D:(i + 1) * D] = (dmerged * ps[i] * sg * (1.0 - sg)).astype(BF)
            dp = (dmerged * sg).astype(BF)
            dp_ref[:, i * D:(i + 1) * D] = dp
            acc = None
            for j in range(N_CHIPS):
                t = _dot_nt(dp[:, j * quarter:(j + 1) * quarter], wb_ref[j, i])
                acc = t if acc is None else acc + t
            out_ref[...] = acc

    row = lambda w, c=0: pl.BlockSpec((tm, w), lambda i, c=c: (i, c))
    whole = lambda a: pl.BlockSpec(a.shape, lambda i, n=a.ndim: (0,) * n)
    vec = pl.BlockSpec((1, D), lambda i: (0, 0))
    sd = jax.ShapeDtypeStruct
    return pl.pallas_call(
        body, name="merge_bwd",
        grid=(S // tm,),
        in_specs=[row(D), row(D), row(D), whole(mod), row(D, 6), row(D, 7), row(D, 8), row(WIDTH), row(WIDTH), row(WIDTH), whole(wb),
                  whole(wo), whole(ln_g)],
        out_specs=[row(D), row(D), row(3 * D), row(3 * D), row(WIDTH), row(WIDTH), row(WIDTH), vec, vec, vec],
        out_shape=[sd((S, D), F32), sd((S, D), BF), sd((S, 3 * D), BF), sd((S, 3 * D), BF), sd((S, WIDTH), F32), sd((S, WIDTH), F32),
                   sd((S, WIDTH), F32), sd((1, D), F32), sd((1, D), F32), sd((1, D), F32)],
        compiler_params=_params(dimension_semantics=("arbitrary",)),
    )(dxo, x, y, mod, proj, proj, proj, *ys, wb, wo, ln_g)


def _loss_head(x, target, tm=512):
    S, D = x.shape
    tm = min(tm, S)

    def body(x_ref, t_ref, dx_ref, loss_ref):
        @pl.when(pl.program_id(0) == 0)
        def _():
            loss_ref[...] = jnp.zeros_like(loss_ref)

        err = x_ref[...] - t_ref[...]
        dx_ref[...] = err * (1.0 / D)
        loss_ref[...] += 0.5 * jnp.sum(jnp.mean(err * err, axis=-1, keepdims=True))

    row = pl.BlockSpec((tm, D), lambda i: (i, 0))
    return pl.pallas_call(
        body, name="loss_head",
        grid=(S // tm,),
        in_specs=[row, row],
        out_specs=[row, pl.BlockSpec((8, 128), lambda i: (0, 0))],
        out_shape=[jax.ShapeDtypeStruct((S, D), F32), jax.ShapeDtypeStruct((8, 128), F32)],
        compiler_params=_params(dimension_semantics=("arbitrary",)),
    )(x, target)


def _proj_bwd(dproj, wg, x, mod, dx_res, job=None, tm=512, tk=768):
    S, D = x.shape
    tm = min(tm, S)
    shard = wg.shape[-1]
    per = shard // tk
    nk = IN_COLS // tk

    def body(dp_ref, w_ref, x_ref, mod_ref, dxr_ref, dx_ref, dsh_ref, dsc_ref, acc):
        i, k = pl.program_id(0), pl.program_id(1)

        @pl.when((i == 0) & (k == 0))
        def _():
            dsh_ref[...] = jnp.zeros_like(dsh_ref)
            dsc_ref[...] = jnp.zeros_like(dsc_ref)

        @pl.when(k == 0)
        def _():
            acc[...] = jnp.zeros_like(acc)

        acc[...] += _dot_nt(dp_ref[...], w_ref[...])

        @pl.when(k == nk - 1)
        def _():
            dh = acc[...]
            xs, rstd = _standardize(x_ref[...])
            dsh_ref[...] += jnp.sum(dh, axis=0, keepdims=True)
            dsc_ref[...] += jnp.sum(dh * xs, axis=0, keepdims=True)
            dx_ref[...] = _standardize_bwd(dh * (1.0 + mod_ref[:, D:2 * D]), xs, rstd) + dxr_ref[...]

    row = pl.BlockSpec((tm, D), lambda i, k: (i, 0))
    vec = pl.BlockSpec((1, D), lambda i, k: (0, 0))
    return _hosted(
        body, job, (dproj, wg, x, mod, dx_res), name="proj_bwd",
        grid=(S // tm, nk),
        in_specs=[pl.BlockSpec((tm, tk), lambda i, k: (i, k)),
                  pl.BlockSpec((None, D, tk), lambda i, k: (k // per, 0, k % per)),
                  row, pl.BlockSpec((1, 3 * D), lambda i, k: (0, 0)), row],
        out_specs=[row, vec, vec],
        out_shape=[jax.ShapeDtypeStruct((S, D), F32), jax.ShapeDtypeStruct((1, D), F32), jax.ShapeDtypeStruct((1, D), F32)],
        scratch_shapes=[pltpu.VMEM((tm, D), F32)],
        semantics=("arbitrary", "arbitrary"))


def _grad_w(a, b, mid, prev, name, tm, tn, b_col0=0):
    S, M = a.shape
    shape = prev.shape
    n_shard = shape[-1]
    per = n_shard // tn
    nm = M // tm
    lead = len(mid)

    def body(*refs):
        a_ref, b_ref, o_ref = refs[0], refs[1], refs[-1]
        o_ref[...] = _dot_tn(a_ref[...], b_ref[...]).astype(BF)

    in_specs = [pl.BlockSpec((S, tm), lambda m, n: (0, m)),
                pl.BlockSpec((S, tn), lambda m, n: (0, b_col0 // tn + n))]
    args = [a, b]
    aliases = {}
    if not isinstance(prev, jax.ShapeDtypeStruct):
        in_specs.append(pl.BlockSpec(memory_space=pl.ANY))
        args.append(prev)
        aliases = {2: 0}
    return pl.pallas_call(
        body, name=name,
        grid=(nm, N_CHIPS * per),
        in_specs=in_specs,
        out_specs=pl.BlockSpec((None,) + (None,) * lead + (tm, tn), lambda m, n: (n // per,) + tuple(mid) + (m, n % per)),
        out_shape=jax.ShapeDtypeStruct(shape, BF),
        input_output_aliases=aliases,
        compiler_params=_params(dimension_semantics=("parallel", "parallel")),
    )(*args)


def _all_gather8(x, name):
    R, N = x.shape

    def body(x_ref, out_ref, send_sems, recv_sems):
        mx, my, mc = lax.axis_index("x"), lax.axis_index("y"), lax.axis_index("c")
        me = 4 * mx + 2 * my + mc
        out_ref[me] = x_ref[...]
        copies = []
        for k in range(1, N_DEV):
            peer = (_flip(mx, k & 4), _flip(my, k & 2), _flip(mc, k & 1))
            cp = pltpu.make_async_remote_copy(src_ref=x_ref, dst_ref=out_ref.at[me], send_sem=send_sems.at[k - 1],
                                              recv_sem=recv_sems.at[k - 1], device_id=peer, device_id_type=MESH)
            cp.start()
            copies.append(cp)
        for cp in copies:
            cp.wait()

    return pl.pallas_call(
        body, name=name,
        in_specs=[pl.BlockSpec(memory_space=pltpu.VMEM)],
        out_specs=pl.BlockSpec(memory_space=pltpu.VMEM),
        out_shape=jax.ShapeDtypeStruct((N_DEV, R, N), F32),
        scratch_shapes=[pltpu.SemaphoreType.DMA((N_DEV - 1,)), pltpu.SemaphoreType.DMA((N_DEV - 1,))],
        compiler_params=_params(),
    )(x)


def _rows2d(a):
    return a.reshape(-1, a.shape[-1])


def _tile_rows(rows, cols, n_arrays):
    budget = (24 << 20) // (n_arrays * 2 * 4 * cols)
    if rows <= budget:
        return rows
    tm = 8
    for cand in range(8, budget + 1, 8):
        if rows % cand == 0:
            tm = cand
    return tm


SUM_ROWS = 256


def _sum_cores(g, sent, where):
    chips, lead, _, r, cols = g.shape
    tr = min(r, SUM_ROWS)

    def body(where_ref, g_ref, s_ref, out_ref):
        out_ref[...] = (g_ref[...].astype(F32) + s_ref[...].astype(F32)).astype(BF)

    spec = pl.BlockSpec((None, tr, cols), lambda i, j, where_ref: (i, j, 0))
    out = pl.pallas_call(
        body, name="sum_cores",
        grid_spec=pltpu.PrefetchScalarGridSpec(
            num_scalar_prefetch=1, grid=(chips * lead, r // tr),
            in_specs=[pl.BlockSpec((None, None, tr, cols), lambda i, j, where_ref: (i, where_ref[1], j, 0)), spec],
            out_specs=spec),
        out_shape=jax.ShapeDtypeStruct((chips * lead, r, cols), BF),
        compiler_params=_params(dimension_semantics=("parallel", "parallel")),
    )(where, g.reshape(chips * lead, 2, r, cols), sent.reshape(chips * lead, r, cols))
    return out.reshape(chips, lead, r, cols)


def _sum_chips(sums, got, where):
    _, lead, r, cols = sums.shape
    tr = min(r, SUM_ROWS)

    def body(where_ref, s_ref, g_ref, out_ref):
        out_ref[...] = ((s_ref[...].astype(F32) + g_ref[0].astype(F32)) + g_ref[1].astype(F32)) + g_ref[2].astype(F32)

    return pl.pallas_call(
        body, name="sum_chips",
        grid_spec=pltpu.PrefetchScalarGridSpec(
            num_scalar_prefetch=1, grid=(lead, r // tr),
            in_specs=[pl.BlockSpec((None, None, tr, cols), lambda i, j, where_ref: (where_ref[0], i, j, 0)),
                      pl.BlockSpec((N_CHIPS - 1, None, tr, cols), lambda i, j, where_ref: (0, i, j, 0))],
            out_specs=pl.BlockSpec((None, None, tr, cols), lambda i, j, where_ref: (i, where_ref[1], j, 0))),
        out_shape=jax.ShapeDtypeStruct((lead, 2, r, cols), F32),
        compiler_params=_params(dimension_semantics=("parallel", "parallel")),
    )(where, sums, got)


def _adamw(w, m, v, groups, name):
    shape = w.shape
    w2, m2, v2 = _rows2d(w), _rows2d(m), _rows2d(v)
    rows, cols = w2.shape
    ng = len(groups)
    n = len(groups[0])
    slab = rows // ng
    gs = [_rows2d(g) for grp in groups for g in grp]
    tm = _tile_rows(slab, cols, 7 + n)
    tiles = slab // tm
    c1 = 1.0 / (1.0 - ADAM_B1 ** ADAM_STEP)
    c2 = 1.0 / (1.0 - ADAM_B2 ** ADAM_STEP)

    def body(*refs):
        w_ref, m_ref, v_ref = refs[:3]
        g_refs = refs[3:3 + ng * n]
        go_ref, d_ref, mo_ref, vo_ref = refs[3 + ng * n:]
        which = pl.program_id(0)
        for s in range(ng):
            @pl.when(which == s)
            def _(s=s):
                g = g_refs[s * n][...]
                for r in g_refs[s * n + 1:(s + 1) * n]:
                    g = g + r[...]
                mn = ADAM_B1 * m_ref[...] + (1.0 - ADAM_B1) * g
                vn = ADAM_B2 * v_ref[...] + (1.0 - ADAM_B2) * (g * g)
                go_ref[...] = g
                mo_ref[...] = mn
                vo_ref[...] = vn
                d_ref[...] = -ADAM_LR * ((mn * c1) / (jnp.sqrt(vn * c2) + ADAM_EPS) + ADAM_WD * w_ref[...])

    spec = pl.BlockSpec((tm, cols), lambda s, i: (s * tiles + i, 0))
    g_specs = [pl.BlockSpec((tm, cols), lambda s, i, k=k: (jnp.where(s == k, i, jnp.where(s < k, 0, tiles - 1)), 0))
               for k in range(ng) for _ in range(n)]
    outs = pl.pallas_call(
        body, name=name,
        grid=(ng, tiles),
        in_specs=[spec] * 3 + g_specs,
        out_specs=[spec] * 4,
        out_shape=[jax.ShapeDtypeStruct((rows, cols), F32)] * 4,
        compiler_params=_params(dimension_semantics=("arbitrary", "arbitrary")),
    )(w2, m2, v2, *gs)
    return [o.reshape(shape) for o in outs]


def _lower_bounds(r0, r1):
    top = jnp.maximum(r0, r1)
    e0, e1 = jnp.exp(r0 - top), jnp.exp(r1 - top)
    p0, p1 = e0 / (e0 + e1), e1 / (e0 + e1)
    return p0 - p0, (p0 + p1) - p0


def _lbs_fwd(lb_raw):
    def body(lb_ref, out_ref):
        l0, l1 = _lower_bounds(lb_ref[0:1, :], lb_ref[1:2, :])
        out_ref[0:1, :] = l0
        out_ref[1:2, :] = l1

    return pl.pallas_call(body, name="lower_bounds", out_shape=jax.ShapeDtypeStruct(lb_raw.shape, F32), compiler_params=_params())(lb_raw)


def _mod_rows(c_all, w_mod, tn=768):
    _, D, cols = w_mod.shape

    def body(c_ref, w_ref, out_ref):
        out_ref[...] = _dot(c_ref[...].astype(BF), w_ref[...].astype(BF))

    return pl.pallas_call(
        body, name="mod_rows",
        grid=(DEPTH,),
        in_specs=[pl.BlockSpec((N_DEV, D), lambda l: (0, 0)), pl.BlockSpec((None, D, cols), lambda l: (l, 0, 0))],
        out_specs=pl.BlockSpec((N_DEV, cols), lambda l: (0, l)),
        out_shape=jax.ShapeDtypeStruct((N_DEV, DEPTH * cols), F32),
        compiler_params=_params(dimension_semantics=("parallel",)),
    )(c_all, w_mod)


def _grad_w_mod(c_all, dmod_cols):
    D = c_all.shape[1]
    cols = dmod_cols.shape[-1]

    def body(c_ref, d_ref, out_ref):
        out_ref[...] = _dot_tn(c_ref[...].astype(BF), d_ref[...].astype(BF))

    return pl.pallas_call(
        body, name="grad_w_mod",
        grid=(DEPTH,),
        in_specs=[pl.BlockSpec((N_DEV, D), lambda l: (0, 0)), pl.BlockSpec((None, N_DEV, cols), lambda l: (l, 0, 0))],
        out_specs=pl.BlockSpec((None, D, cols), lambda l: (l, 0, 0)),
        out_shape=jax.ShapeDtypeStruct((DEPTH, D, cols), F32),
        compiler_params=_params(dimension_semantics=("parallel",)),
    )(c_all, dmod_cols)


def _sum_devices(parts, lb_raw, n_lb):
    _, R, N = parts.shape

    def body(p_ref, lb_ref, out_ref, dlb_ref):
        acc = p_ref[0]
        for d in range(1, N_DEV):
            acc = acc + p_ref[d]
        out_ref[...] = acc
        dl = tuple(acc[0:1, n_lb + l * WIDTH:n_lb + (l + 1) * WIDTH] for l in range(DEPTH))
        _, pull = jax.vjp(_lower_bounds, lb_ref[0:1, :], lb_ref[1:2, :])
        d0, d1 = pull(dl)
        dlb_ref[0:1, :] = d0
        dlb_ref[1:2, :] = d1

    return pl.pallas_call(
        body, name="sum_devices",
        out_shape=[jax.ShapeDtypeStruct((R, N), F32), jax.ShapeDtypeStruct(lb_raw.shape, F32)],
        compiler_params=_params(),
    )(parts, lb_raw)


def _pad_rows(a, rows=8):
    return jnp.concatenate([a, jnp.zeros((rows - a.shape[0],) + a.shape[1:], a.dtype)], axis=0)


def kernel(x, c, w_mod, b_mod, w_in, conv_w, hgrn_norm_w, lower_bounds, w_branch, w_out, ln_g, ln_b, loss_target, m_w_mod, m_b_mod, m_w_in, m_conv_w, m_hgrn_norm_w, m_lower_bounds, m_w_branch, m_w_out, m_ln_g, m_ln_b, v_w_mod, v_b_mod, v_w_in, v_conv_w, v_hgrn_norm_w, v_lower_bounds, v_w_branch, v_w_out, v_ln_g, v_ln_b):
    D = D_MODEL
    x0 = x[0]
    target = loss_target[0]
    S = x0.shape[0]
    mx, my, mc = lax.axis_index("x"), lax.axis_index("y"), lax.axis_index("c")
    chip = 2 * mx + my
    me = 2 * chip + mc
    mod_cols = 3 * D // N_CHIPS

    plan = _Plan(w_in.astype(BF), w_branch.astype(BF), w_out.astype(BF), chip, mc)
    plan.first()

    n_conv = DEPTH * 3 * (WIDTH // N_CHIPS)
    first = jnp.concatenate([c, conv_w.reshape(1, n_conv)], axis=1)
    first = _all_gather8(_pad_rows(first), "gather_c")
    c_all = first[:, 0, :D]
    conv_all = first[:, 0, D:].reshape(N_DEV, DEPTH, 3, WIDTH // N_CHIPS)[0::2]
    mod_part = _all_gather8(_mod_rows(c_all, w_mod), "gather_mod")[0::2]
    mod_part = lax.dynamic_index_in_dim(mod_part, me, axis=1, keepdims=False).reshape(N_CHIPS, DEPTH, mod_cols)
    mods = [(mod_part[:, l].reshape(1, 3 * D) + b_mod[l][None, :]) for l in range(DEPTH)]
    lbs = _lbs_fwd(lower_bounds).reshape(DEPTH, WIDTH // 128, 1, 128)
    loss_blk, dx, small = _local_step(x0, target, mods, lbs, conv_all, hgrn_norm_w, ln_g, ln_b, plan)

    whole = plan.finish()
    grads = {kind: [[whole[(kind, l)]] for l in range(DEPTH)] for kind in ("in", "br", "out")}

    n_mod, n_nw, n_lb, n_ln, n_cw = DEPTH * 3 * D, DEPTH * 128, DEPTH * WIDTH, DEPTH * D, DEPTH * 3 * WIDTH
    row = jnp.concatenate(
        [jnp.concatenate([small[l][0], small[l][1], small[l][2]], axis=1) for l in range(DEPTH)]
        + [jnp.sum(small[l][3], axis=0) for l in range(DEPTH)]
        + [small[l][4].reshape(1, WIDTH) for l in range(DEPTH)]
        + [small[l][5] for l in range(DEPTH)] + [small[l][6] for l in range(DEPTH)]
        + [jnp.transpose(small[l][7], (1, 0, 2)).reshape(1, 3 * WIDTH) for l in range(DEPTH)]
        + [loss_blk[0:1, :]], axis=1)
    gathered = _all_gather8(_pad_rows(row), "gather_small")
    off_nw = n_mod
    off_lb = off_nw + n_nw
    off_lng = off_lb + n_lb
    off_lnb = off_lng + n_ln
    off_cw = off_lnb + n_ln
    off_loss = off_cw + n_cw
    total, d_lower = _sum_devices(gathered, lower_bounds, off_lb)
    loss = total[0, off_loss]
    d_b_mod = total[0, :n_mod].reshape(DEPTH, 3 * D)
    d_norm_w = total[0, off_nw:off_lb].reshape(DEPTH, 128)
    d_ln_g = total[0, off_lng:off_lnb].reshape(DEPTH, D)
    d_ln_b = total[0, off_lnb:off_cw].reshape(DEPTH, D)
    d_conv = total[0, off_cw:off_loss].reshape(DEPTH, 3, N_CHIPS, WIDTH // N_CHIPS)
    d_conv = lax.dynamic_index_in_dim(d_conv, chip, axis=2, keepdims=False)
    dmod_all = gathered[:, 0, :n_mod].reshape(N_DEV, DEPTH, N_CHIPS, mod_cols)
    dmod_cols = jnp.transpose(lax.dynamic_index_in_dim(dmod_all, chip, axis=2, keepdims=False), (1, 0, 2))
    d_w_mod = _grad_w_mod(c_all, dmod_cols)

    res = {}
    res["w_mod"] = _adamw(w_mod, m_w_mod, v_w_mod, [[d_w_mod]], "adamw_w_mod")
    res["b_mod"] = _adamw(b_mod, m_b_mod, v_b_mod, [[d_b_mod]], "adamw_b_mod")
    res["w_in"] = _adamw(w_in, m_w_in, v_w_in, grads["in"], "adamw_w_in")
    res["conv_w"] = _adamw(conv_w, m_conv_w, v_conv_w, [[d_conv]], "adamw_conv_w")
    res["hgrn_norm_w"] = _adamw(hgrn_norm_w, m_hgrn_norm_w, v_hgrn_norm_w, [[d_norm_w]], "adamw_norm_w")
    res["lower_bounds"] = _adamw(lower_bounds, m_lower_bounds, v_lower_bounds, [[d_lower]], "adamw_lower_bounds")
    res["w_branch"] = _adamw(w_branch, m_w_branch, v_w_branch, grads["br"], "adamw_w_branch")
    res["w_out"] = _adamw(w_out, m_w_out, v_w_out, grads["out"], "adamw_w_out")
    res["ln_g"] = _adamw(ln_g, m_ln_g, v_ln_g, [[d_ln_g]], "adamw_ln_g")
    res["ln_b"] = _adamw(ln_b, m_ln_b, v_ln_b, [[d_ln_b]], "adamw_ln_b")
    names = ["w_mod", "b_mod", "w_in", "conv_w", "hgrn_norm_w", "lower_bounds", "w_branch", "w_out", "ln_g", "ln_b"]
    return (loss, dx[None], *[res[n][0] for n in names], *[res[n][1] for n in names],
            *[res[n][2] for n in names], *[res[n][3] for n in names])


class _Plan:
    def __init__(self, w_in, w_br, w_out, chip, core):
        self.local = {"in": w_in, "br": w_br, "out": w_out}
        self.chip, self.where = chip, jnp.stack([chip, core]).astype(jnp.int32)
        self.gathered, self.grads, self.chip_sums, self.scattered, self.pending = {}, {}, {}, {}, {}

    def _slab(self, key):
        mine = _halves(self.local[key[0]][key[1]])
        return lax.dynamic_update_slice(lax.empty((N_CHIPS,) + mine.shape, mine.dtype), mine[None], (self.chip, 0, 0, 0, 0))

    def _gather(self, keys):
        return ("gather", keys), _gather_job([self._slab(key) for key in keys])

    def _to_sibling(self, keys):
        return ("to_sibling", keys), _to_sibling_job([_halves(self.grads[key], 1) for key in keys])

    def _scatter(self, keys):
        return ("scatter", keys), _scatter_job([self.chip_sums[key] for key in keys])

    def job(self, stage, l):
        parts = []
        if stage == "proj_fwd":
            parts = [self._gather([("br", l), ("out", l)])]
        elif stage == "attn_fwd" and l + 1 < DEPTH:
            parts = [self._gather([("in", l + 1)])]
        elif stage == "attn_bwd":
            parts = [self._to_sibling([("out", l), ("br", l)])] + ([self._scatter([("in", l + 1)])] if l + 1 < DEPTH else [])
        elif stage == "hgrn_bwd":
            parts = [self._scatter([("out", l), ("br", l)])]
        elif stage == "proj_bwd":
            parts = [self._to_sibling([("in", l)])]
        self.pending[(stage, l)] = [(tag, len(job.outs)) for tag, job in parts]
        return _join_jobs([job for _, job in parts])

    def done(self, stage, l, outs):
        if outs is None:
            return
        at = 0
        for (what, keys), n_outs in self.pending[(stage, l)]:
            mine, at = outs[at:at + n_outs], at + n_outs
            for n, key in enumerate(keys):
                if what == "gather":
                    self.gathered[key] = mine[n].reshape((N_CHIPS,) + self.local[key[0]].shape[1:])
                elif what == "to_sibling":
                    self.chip_sums[key] = _sum_cores(_halves(self.grads[key], 1), mine[n], self.where)
                else:
                    self.scattered[key] = mine[n]

    def first(self):
        tag, job = self._gather([("in", 0)])
        self.pending[("first", 0)] = [(tag, len(job.outs))]
        self.done("first", 0, _run_job(job, "gather_first"))

    def finish(self):
        keys = [(kind, l) for kind in ("in", "br", "out") for l in range(DEPTH)]
        _, last = self._scatter([("in", 0)])
        self.scattered[("in", 0)] = _run_job(last, "scatter_last")[0]
        halves = [_sum_chips(self.chip_sums[key], self.scattered[key], self.where) for key in keys]
        whole = _run_job(_place_job(halves), "place_halves")
        return {key: w.reshape(self.grads[key].shape[1:]) for key, w in zip(keys, whole)}


def _local_step(x0, target, mods, lbs, conv_all, hgrn_norm_w, ln_g, ln_b, plan):
    D = D_MODEL
    after, before = _attn_consts()
    hg_consts = _hgrn_consts()

    saved = []
    xl = x0
    for l in range(DEPTH):
        (proj, h), got = _proj_fwd(xl, mods[l], plan.gathered[("in", l)], plan.job("proj_fwd", l))
        plan.done("proj_fwd", l, got)
        (o_a, tot), got = _attn_fwd(proj, after, plan.job("attn_fwd", l))
        plan.done("attn_fwd", l, got)
        o_b, states = _hgrn_fwd(proj, lbs[l], hg_consts)
        ys = _branch_fwd(proj, o_a, o_b, hgrn_norm_w, conv_all, l)
        x_next, merged, y = _merge_fwd(xl, mods[l], proj, ys, plan.gathered[("br", l)], plan.gathered[("out", l)], ln_g, ln_b, l)
        saved.append((xl, proj, h, o_a, tot, o_b, states, ys, merged, y))
        xl = x_next
    dx, loss_blk = _loss_head(xl, target)

    small = [None] * DEPTH
    for l in reversed(range(DEPTH)):
        xin, proj, h, o_a, tot, o_b, states, ys, merged, y = saved[l]
        dx_res, dy, dp, dg, dy_a, dy_b, dy_c, dln_g, dln_b, dgate = _merge_bwd(
            dx, xin, y, mods[l], proj, ys, plan.gathered[("br", l)], plan.gathered[("out", l)], ln_g, l)
        plan.grads[("out", l)] = _grad_w_out(merged, dy)
        g_br = jax.ShapeDtypeStruct((N_CHIPS, 3, WIDTH, D // N_CHIPS), BF)
        for i in range(3):
            g_br = _grad_w(ys[i], dp, (i,), g_br, "grad_w_branch", tm=WIDTH, tn=D // N_CHIPS, b_col0=i * D)
        plan.grads[("br", l)] = g_br
        d_oa, d_ob, dz_a, dz_b, dpre, dpost, du, dz_c, dnorm_w, dconv_w = _branch_bwd(proj, o_a, o_b, hgrn_norm_w, conv_all, dy_a, dy_b, dy_c, l)
        (dq_a, dk_a, dv_a), got = _attn_bwd(proj, d_oa, tot, after, before, plan.job("attn_bwd", l))
        plan.done("attn_bwd", l, got)
        (dq_b, df_b, di_b, dlb), got = _hgrn_bwd(proj, lbs[l], states, d_ob, hg_consts, plan.job("hgrn_bwd", l))
        plan.done("hgrn_bwd", l, got)
        dproj = jnp.concatenate([dq_a, dk_a, dv_a, dz_a, dq_b, df_b, di_b, dz_b, dpre, dpost, du, dz_c, dg], axis=1)
        plan.grads[("in", l)] = _grad_w(h, dproj, (), jax.ShapeDtypeStruct((N_CHIPS, D, IN_COLS // N_CHIPS), BF), "grad_w_in", tm=512, tn=768)
        (dx, dshift, dscale), got = _proj_bwd(dproj, plan.gathered[("in", l)], xin, mods[l], dx_res, plan.job("proj_bwd", l))
        plan.done("proj_bwd", l, got)
        small[l] = (dshift, dscale, dgate, dnorm_w, dlb, dln_g, dln_b, dconv_w)
    return loss_blk, dx, small


def _grad_w_out(merged, dy):
    S, D = merged.shape
    q = D // N_CHIPS

    def body(a_ref, b_ref, o_ref):
        o_ref[...] = _dot_tn(a_ref[...], b_ref[...]).astype(BF)

    return pl.pallas_call(
        body, name="grad_w_out",
        grid=(N_CHIPS,),
        in_specs=[pl.BlockSpec((S, q), lambda j: (0, j)), pl.BlockSpec((S, D), lambda j: (0, 0))],
        out_specs=pl.BlockSpec((None, q, D), lambda j: (j, 0, 0)),
        out_shape=jax.ShapeDtypeStruct((N_CHIPS, q, D), BF),
        compiler_params=_params(dimension_semantics=("parallel",)),
    )(merged, dy)
```

```python
import functools
import math

import numpy as np
import jax
import jax.numpy as jnp
from jax import lax
from jax.experimental import pallas as pl
from jax.experimental.pallas import tpu as pltpu

F32 = jnp.float32
BF = jnp.bfloat16
MESH = pl.DeviceIdType.MESH

DEPTH = 2
D_MODEL = 1024
WIDTH = 512
IN_COLS = 12 * WIDTH + 3 * D_MODEL
N_CHIPS = 4
N_DEV = 8
SB_BLOCK = 128
SB_HEAD_DIM = 64
HG_CHUNK = 64
HG_DIM = 128
LN_EPS = 1e-5
RMS_EPS = 1e-6
ALPHA = (2.0 * DEPTH) ** 0.25
ADAM_LR, ADAM_B1, ADAM_B2, ADAM_EPS, ADAM_WD, ADAM_STEP = 0.001, 0.9, 0.999, 1e-08, 0.01, 10
VMEM_LIMIT = 56 << 20


def _params(**kw):
    return pltpu.CompilerParams(vmem_limit_bytes=VMEM_LIMIT, **kw)


def _dot(a, b):
    return jnp.dot(a, b, preferred_element_type=F32)


def _dot_nt(a, b):
    return lax.dot_general(a, b, (((1,), (1,)), ((), ())), preferred_element_type=F32)


def _dot_tn(a, b):
    return lax.dot_general(a, b, (((0,), (0,)), ((), ())), preferred_element_type=F32)


def _sigmoid(x):
    return 1.0 / (1.0 + jnp.exp(-x))


def _silu(x):
    return x * _sigmoid(x)


def _softplus(z):
    return jnp.maximum(z, 0.0) + jnp.log(1.0 + jnp.exp(-jnp.abs(z)))


def _split_dot(t, g, terms):
    acc = None
    rest = g
    for _ in range(terms):
        part = rest.astype(BF)
        rest = rest - part.astype(F32)
        d = _dot(t, part)
        acc = d if acc is None else acc + d
    return acc


def _split_dot_r(g, t, terms):
    acc = None
    rest = g
    for _ in range(terms):
        part = rest.astype(BF)
        rest = rest - part.astype(F32)
        d = _dot(part, t)
        acc = d if acc is None else acc + d
    return acc


def _standardize(x):
    mu = jnp.mean(x, axis=-1, keepdims=True)
    xc = x - mu
    var = jnp.mean(xc * xc, axis=-1, keepdims=True)
    rstd = lax.rsqrt(var + LN_EPS)
    return xc * rstd, rstd


def _standardize_bwd(dxs, xs, rstd):
    return rstd * (dxs - jnp.mean(dxs, axis=-1, keepdims=True) - xs * jnp.mean(dxs * xs, axis=-1, keepdims=True))


class _Job:
    def __init__(self, ins, outs, sems, make, alias=None):
        self.ins, self.outs, self.sems, self.make = list(ins), list(outs), list(sems), make
        self.alias = dict(alias or {})


def _join_jobs(jobs):
    jobs = [j for j in jobs if j is not None]
    if len(jobs) <= 1:
        return jobs[0] if jobs else None

    def make(ins, outs, sems):
        phases, i, o, s = [], 0, 0, 0
        for j in jobs:
            got = j.make(ins[i:i + len(j.ins)], outs[o:o + len(j.outs)], sems[s:s + len(j.sems)])
            i, o, s = i + len(j.ins), o + len(j.outs), s + len(j.sems)
            for n, phase in enumerate(got):
                if n == len(phases):
                    phases.append([])
                phases[n] += phase
        return phases

    alias, i, o = {}, 0, 0
    for j in jobs:
        alias.update({i + a: o + b for a, b in j.alias.items()})
        i, o = i + len(j.ins), o + len(j.outs)
    return _Job(sum([j.ins for j in jobs], []), sum([j.outs for j in jobs], []), sum([j.sems for j in jobs], []), make, alias)


def _flip(v, bit):
    return 1 - v if bit else v


def _halves(a, front=0):
    shape = a.shape
    lead = math.prod(shape[front:-2])
    return a.reshape(shape[:front] + (lead, 2, shape[-2] // 2, shape[-1]))


def _dma_sems(*shapes):
    return [pltpu.SemaphoreType.DMA(s) for s in shapes]


def _same(arrays):
    return [jax.ShapeDtypeStruct(a.shape, a.dtype) for a in arrays]


def _gather_job(slabs):
    n = len(slabs)

    def make(ins, outs, sems):
        send1, recv1, send2, recv2 = sems
        mx, my, mc = lax.axis_index("x"), lax.axis_index("y"), lax.axis_index("c")
        fetch, pass_on = [], []
        for a in range(n):
            ours = outs[a].at[2 * mx + my, :, mc]
            for k in range(1, N_CHIPS):
                px, py = _flip(mx, k & 2), _flip(my, k & 1)
                fetch.append(pltpu.make_async_remote_copy(
                    src_ref=ours, dst_ref=ours, send_sem=send1.at[a, k - 1], recv_sem=recv1.at[a, k - 1],
                    device_id=(px, py, mc), device_id_type=MESH))
                theirs = outs[a].at[2 * px + py, :, mc]
                pass_on.append(pltpu.make_async_remote_copy(
                    src_ref=theirs, dst_ref=theirs, send_sem=send2.at[a, k - 1], recv_sem=recv2.at[a, k - 1],
                    device_id=(mx, my, 1 - mc), device_id_type=MESH))
        return [fetch, pass_on]

    pairs = (n, N_CHIPS - 1)
    return _Job(slabs, _same(slabs), _dma_sems(pairs, pairs, pairs, pairs), make, {a: a for a in range(n)})


def _to_sibling_job(grads):
    n = len(grads)

    def make(ins, outs, sems):
        send_sems, recv_sems = sems
        mx, my, mc = lax.axis_index("x"), lax.axis_index("y"), lax.axis_index("c")
        return [[pltpu.make_async_remote_copy(
            src_ref=ins[a].at[:, :, 1 - mc], dst_ref=outs[a], send_sem=send_sems.at[a], recv_sem=recv_sems.at[a],
            device_id=(mx, my, 1 - mc), device_id_type=MESH) for a in range(n)]]

    outs = [jax.ShapeDtypeStruct(g.shape[:2] + g.shape[3:], g.dtype) for g in grads]
    return _Job(grads, outs, _dma_sems((n,), (n,)), make)


def _scatter_job(sums):
    n = len(sums)

    def make(ins, outs, sems):
        send_sems, recv_sems = sems
        mx, my, mc = lax.axis_index("x"), lax.axis_index("y"), lax.axis_index("c")
        copies = []
        for a in range(n):
            for k in range(1, N_CHIPS):
                px, py = _flip(mx, k & 2), _flip(my, k & 1)
                copies.append(pltpu.make_async_remote_copy(
                    src_ref=ins[a].at[2 * px + py], dst_ref=outs[a].at[k - 1], send_sem=send_sems.at[a, k - 1],
                    recv_sem=recv_sems.at[a, k - 1], device_id=(px, py, mc), device_id_type=MESH))
        return [copies]

    pairs = (n, N_CHIPS - 1)
    return _Job(sums, [jax.ShapeDtypeStruct((N_CHIPS - 1,) + s.shape[1:], s.dtype) for s in sums], _dma_sems(pairs, pairs), make)


def _place_job(wholes):
    n = len(wholes)

    def make(ins, outs, sems):
        send_sems, recv_sems = sems
        mx, my, mc = lax.axis_index("x"), lax.axis_index("y"), lax.axis_index("c")
        copies = []
        for a in range(n):
            here = outs[a].at[:, mc]
            copies.append(pltpu.make_async_remote_copy(src_ref=here, dst_ref=here, send_sem=send_sems.at[a], recv_sem=recv_sems.at[a],
                                                       device_id=(mx, my, 1 - mc), device_id_type=MESH))
        return [copies]

    return _Job(wholes, _same(wholes), _dma_sems((n,), (n,)), make, {a: a for a in range(n)})


def _run_phases(phases, first=0):
    for n, phase in enumerate(phases):
        if n >= first:
            for cp in phase:
                cp.start()
        for cp in phase:
            cp.wait()


def _run_job(job, name):
    k_in, k_out = len(job.ins), len(job.outs)

    def body(*refs):
        _run_phases(job.make(refs[:k_in], refs[k_in:k_in + k_out], refs[k_in + k_out:]))

    hbm = pl.BlockSpec(memory_space=pl.ANY)
    return pl.pallas_call(body, name=name, in_specs=[hbm] * k_in, out_specs=[hbm] * k_out, out_shape=job.outs,
                          scratch_shapes=job.sems, input_output_aliases=job.alias, compiler_params=_params())(*job.ins)


def _hosted(body, job, args, *, name, grid, in_specs, out_specs, out_shape, scratch_shapes=(), semantics):
    in_specs, out_specs, out_shape, scratch = list(in_specs), list(out_specs), list(out_shape), list(scratch_shapes)
    if job is None:
        outs = pl.pallas_call(body, name=name, grid=grid, in_specs=in_specs, out_specs=out_specs, out_shape=out_shape,
                              scratch_shapes=scratch, compiler_params=_params(dimension_semantics=semantics))(*args)
        return list(outs), None
    n_in, n_out, n_scr, k_in, k_out = len(in_specs), len(out_specs), len(scratch), len(job.ins), len(job.outs)

    def wrapped(*refs):
        ins, rest = refs[:n_in], refs[n_in:]
        job_ins, rest = rest[:k_in], rest[k_in:]
        outs, rest = rest[:n_out], rest[n_out:]
        job_outs, rest = rest[:k_out], rest[k_out:]
        scr, sems = rest[:n_scr], rest[n_scr:]
        ids = [pl.program_id(a) for a in range(len(grid))]
        first = functools.reduce(jnp.logical_and, [i == 0 for i in ids])
        last = functools.reduce(jnp.logical_and, [i == g - 1 for i, g in zip(ids, grid)])

        @pl.when(first)
        def _():
            for cp in job.make(job_ins, job_outs, sems)[0]:
                cp.start()

        body(*ins, *outs, *scr)

        @pl.when(last)
        def _():
            _run_phases(job.make(job_ins, job_outs, sems), first=1)

    hbm = pl.BlockSpec(memory_space=pl.ANY)
    outs = pl.pallas_call(
        wrapped, name=name, grid=grid, in_specs=in_specs + [hbm] * k_in, out_specs=out_specs + [hbm] * k_out,
        out_shape=out_shape + job.outs, scratch_shapes=scratch + job.sems,
        input_output_aliases={n_in + i: n_out + o for i, o in job.alias.items()},
        compiler_params=_params(dimension_semantics=("arbitrary",) * len(grid)))(*args, *job.ins)
    return list(outs[:n_out]), list(outs[n_out:])


def _proj_fwd(x, mod, wg, job=None, tm=512, tn=768):
    S, D = x.shape
    tm = min(tm, S)
    shard = wg.shape[-1]
    per = shard // tn

    def body(x_ref, mod_ref, w_ref, proj_ref, h_ref, hs):
        @pl.when(pl.program_id(1) == 0)
        def _():
            xs, _ = _standardize(x_ref[...])
            h = xs * (1.0 + mod_ref[:, D:2 * D]) + mod_ref[:, 0:D]
            hb = h.astype(BF)
            hs[...] = hb
            h_ref[...] = hb

        proj_ref[...] = _dot(hs[...], w_ref[...])

    return _hosted(
        body, job, (x, mod, wg), name="proj_fwd",
        grid=(S // tm, IN_COLS // tn),
        in_specs=[pl.BlockSpec((tm, D), lambda i, j: (i, 0)),
                  pl.BlockSpec((1, 3 * D), lambda i, j: (0, 0)),
                  pl.BlockSpec((None, D, tn), lambda i, j: (j // per, 0, j % per))],
        out_specs=[pl.BlockSpec((tm, tn), lambda i, j: (i, j)),
                   pl.BlockSpec((tm, D), lambda i, j: (i, 0))],
        out_shape=[jax.ShapeDtypeStruct((S, IN_COLS), F32), jax.ShapeDtypeStruct((S, D), BF)],
        scratch_shapes=[pltpu.VMEM((tm, D), BF)],
        semantics=("parallel", "arbitrary"))


SB_ROWS = 256
SB_KEYS = 256


def _attn_consts():
    j = np.arange(SB_KEYS)[:, None]
    s = np.arange(SB_KEYS)[None, :]
    from_here = np.concatenate([(j >= s), (j >= s)], axis=0).astype(np.float32)
    up_to = np.concatenate([(j <= s), (j <= s)], axis=0).astype(np.float32)
    return jnp.asarray(from_here, BF), jnp.asarray(up_to, BF)


def _hi_lo(x):
    hi = lax.bitcast_convert_type(lax.bitcast_convert_type(x, jnp.uint32) & jnp.uint32(0xFFFF0000), F32)
    return hi.astype(BF), (x - hi).astype(BF)


def _sums_r(x, t2):
    hi, lo = _hi_lo(x)
    return _dot(jnp.concatenate([hi, lo], axis=1), t2)


def _all_lanes(col, lanes):
    return jnp.broadcast_to(col, (col.shape[0], lanes))


def _attn_rows(ref, r0, rows, head0, scale=None):
    v = ref[pl.ds(r0, rows), :]
    if scale is not None:
        v = v * scale
    return jnp.concatenate([jnp.where(head0, v, 0.0), jnp.where(head0, 0.0, v)], axis=0).astype(BF)


def _attn_scores(q2n, k_ref, kj, t2, from_here_ref, masked):
    c0 = pl.multiple_of(kj * SB_KEYS, SB_KEYS)
    kb = k_ref[pl.ds(c0, SB_KEYS), :].astype(BF)
    zn = _dot_nt(q2n, kb)
    lsb = jnp.minimum(zn, 0.0) - jnp.log(1.0 + jnp.exp(-jnp.abs(zn)))
    valid = None
    if masked:
        valid = (lax.broadcasted_iota(jnp.int32, zn.shape, 1) + kj * SB_KEYS) < t2
        lsb = jnp.where(valid, lsb, 0.0)
    return c0, kb, zn, valid, lsb, _sums_r(lsb, from_here_ref[...])


def _attn_fwd(proj, from_here, job=None):
    S = proj.shape[0]
    TQ = SB_ROWS
    assert S % TQ == 0 and SB_KEYS == TQ
    scale = SB_HEAD_DIM ** -0.5

    def body(q_ref, k_ref, v_ref, from_here_ref, o_ref, tot_ref, run, acc):
        head0 = lax.broadcasted_iota(jnp.int32, (1, 2 * SB_HEAD_DIM), 1) < SB_HEAD_DIM

        def qloop(qi, _):
            r0 = pl.multiple_of(qi * TQ, TQ)
            q2n = _attn_rows(q_ref, r0, TQ, head0, -scale)
            trow = lax.broadcasted_iota(jnp.int32, (TQ, SB_KEYS), 0) + qi * TQ
            t2 = jnp.concatenate([trow, trow], axis=0)
            run[...] = jnp.zeros_like(run)
            acc[...] = jnp.zeros_like(acc)

            def step(kj, masked):
                c0, _, zn, valid, _, sums = _attn_scores(q2n, k_ref, kj, t2, from_here_ref, masked)
                r = run[...]
                e = sums - zn + jnp.concatenate([r, r], axis=1)
                if masked:
                    e = jnp.where(valid, e, -jnp.inf)
                acc[...] += _dot(jnp.exp(e).astype(BF), v_ref[pl.ds(c0, SB_KEYS), :].astype(BF))
                run[...] = r + _all_lanes(sums[:, 0:1], SB_BLOCK)

            step(qi, True)

            def below(n, _):
                step(qi - 1 - n, False)
                return 0

            lax.fori_loop(0, qi, below, 0)
            o_ref[pl.ds(r0, TQ), :] = jnp.where(head0, acc[0:TQ, :], acc[TQ:2 * TQ, :])
            tot_ref[0, pl.ds(r0, TQ), :] = run[0:TQ, :]
            tot_ref[1, pl.ds(r0, TQ), :] = run[TQ:2 * TQ, :]
            return 0

        lax.fori_loop(0, S // TQ, qloop, 0)

    col = lambda base: pl.BlockSpec((S, 128), lambda p, base=base: (0, base + p))
    return _hosted(
        body, job, (proj, proj, proj, from_here), name="attn_fwd",
        grid=(WIDTH // 128,),
        in_specs=[col(0), col(4), col(8), pl.BlockSpec(from_here.shape, lambda p: (0, 0))],
        out_specs=[pl.BlockSpec((S, 128), lambda p: (0, p)), pl.BlockSpec((None, 2, S, 128), lambda p: (p, 0, 0, 0))],
        out_shape=[jax.ShapeDtypeStruct((S, WIDTH), F32), jax.ShapeDtypeStruct((WIDTH // 128, 2, S, 128), F32)],
        scratch_shapes=[pltpu.VMEM((2 * TQ, SB_BLOCK), F32), pltpu.VMEM((2 * TQ, 128), F32)],
        semantics=("parallel",))


def _attn_bwd(proj, d_o, tot, from_here, up_to, job=None):
    S = proj.shape[0]
    TQ = SB_ROWS
    assert S % TQ == 0 and SB_KEYS == TQ
    scale = SB_HEAD_DIM ** -0.5

    def body(q_ref, k_ref, v_ref, do_ref, tot_ref, from_here_ref, up_to_ref, dq_ref, dk_ref, dv_ref, pre, cum, dq_acc, dk_acc, dv_acc):
        head0 = lax.broadcasted_iota(jnp.int32, (1, 2 * SB_HEAD_DIM), 1) < SB_HEAD_DIM
        dk_acc[...] = jnp.zeros_like(dk_acc)
        dv_acc[...] = jnp.zeros_like(dv_acc)

        def qloop(qi, _):
            r0 = pl.multiple_of(qi * TQ, TQ)
            q2n = _attn_rows(q_ref, r0, TQ, head0, -scale)
            do2 = _attn_rows(do_ref, r0, TQ, head0)
            trow = lax.broadcasted_iota(jnp.int32, (TQ, SB_KEYS), 0) + qi * TQ
            t2 = jnp.concatenate([trow, trow], axis=0)
            pre[0:TQ, :] = tot_ref[0, pl.ds(r0, TQ), :]
            pre[TQ:2 * TQ, :] = tot_ref[1, pl.ds(r0, TQ), :]
            cum[...] = jnp.zeros_like(cum)
            dq_acc[...] = jnp.zeros_like(dq_acc)

            def step(kj, masked):
                c0, kb, zn, valid, lsb, sums = _attn_scores(q2n, k_ref, kj, t2, from_here_ref, masked)
                later = pre[...] - _all_lanes(sums[:, 0:1], SB_BLOCK)
                pre[...] = later
                e = sums - zn + jnp.concatenate([later, later], axis=1)
                sig = jnp.exp(lsb - zn)
                if masked:
                    e = jnp.where(valid, e, -jnp.inf)
                    sig = jnp.where(valid, sig, 0.0)
                a = jnp.exp(e)
                w = _dot_nt(do2, v_ref[pl.ds(c0, SB_KEYS), :].astype(BF)) * a
                upto = _sums_r(w, up_to_ref[...])
                c = cum[...]
                dz = w - sig * (upto + jnp.concatenate([c, c], axis=1))
                cum[...] = c + _all_lanes(upto[:, SB_KEYS - 1:SB_KEYS], SB_BLOCK)
                dzb = dz.astype(BF)
                dq_acc[...] += _dot(dzb, kb)
                dk_acc[pl.ds(c0, SB_KEYS), :] += _dot_tn(dzb, q2n)
                dv_acc[pl.ds(c0, SB_KEYS), :] += _dot_tn(a.astype(BF), do2)

            def below(kj, _):
                step(kj, False)
                return 0

            lax.fori_loop(0, qi, below, 0)
            step(qi, True)
            dq_ref[pl.ds(r0, TQ), :] = (jnp.where(head0, dq_acc[0:TQ, :], dq_acc[TQ:2 * TQ, :]) * scale).astype(BF)
            return 0

        lax.fori_loop(0, S // TQ, qloop, 0)
        dk_ref[...] = (-dk_acc[...]).astype(BF)
        dv_ref[...] = dv_acc[...].astype(BF)

    col = lambda base: pl.BlockSpec((S, 128), lambda p, base=base: (0, base + p))
    whole = lambda a: pl.BlockSpec(a.shape, lambda p: (0, 0))
    out = jax.ShapeDtypeStruct((S, WIDTH), BF)
    big = pltpu.VMEM((2 * TQ, SB_BLOCK), F32)
    return _hosted(
        body, job, (proj, proj, proj, d_o, tot, from_here, up_to), name="attn_bwd",
        grid=(WIDTH // 128,),
        in_specs=[col(0), col(4), col(8), col(0), pl.BlockSpec((None, 2, S, 128), lambda p: (p, 0, 0, 0)), whole(from_here), whole(up_to)],
        out_specs=[col(0), col(0), col(0)],
        out_shape=[out, out, out],
        scratch_shapes=[big, big, big, pltpu.VMEM((S, 128), F32), pltpu.VMEM((S, 128), F32)],
        semantics=("parallel",))


HG_LEVELS = (32, 16, 8, 4, 2, 1)


def _hgrn_consts():
    C = HG_CHUNK
    t = np.arange(C)[:, None]
    s = np.arange(C)[None, :]
    rows = [(s <= t), (s > t)]
    masks = [(t == s)]
    for m in HG_LEVELS:
        two = 2 * m
        mid = (t // two) * two + m
        right = (t % two) >= m
        rows.append((right & (s >= mid) & (s <= t)) | ((~right) & (s > t) & (s <= mid - 1)))
        masks.append(((t // two) == (s // two)) & right & ((s % two) < m))
    tri = np.concatenate(rows, axis=0).astype(np.float32)
    return (jnp.asarray(tri, BF), jnp.asarray(tri.T.copy(), BF), jnp.asarray(np.stack(masks).astype(np.float32), F32))


HG_SUM_BLOCKS = 2 + len(HG_LEVELS)
HG_SUM_TERMS = 2


@jax.custom_vjp
def _hgrn_sums(tri, tri_t, g):
    C = HG_CHUNK
    e = _split_dot(tri, g, HG_SUM_TERMS)
    blocks = tuple(e[n * C:(n + 1) * C] for n in range(HG_SUM_BLOCKS))
    return blocks + (jnp.broadcast_to(e[C - 1:C], (HG_DIM, e.shape[1])),)


def _hgrn_sums_fwd(tri, tri_t, g):
    return _hgrn_sums(tri, tri_t, g), (tri, tri_t)


def _hgrn_sums_bwd(res, ds):
    tri, tri_t = res
    C = HG_CHUNK
    last = lax.broadcasted_iota(jnp.int32, (C, 1), 0) == C - 1
    prefix = ds[0] + jnp.where(last, jnp.sum(ds[-1], axis=0, keepdims=True), 0.0)
    d = jnp.concatenate((prefix,) + tuple(ds[1:-1]), axis=0)
    return jnp.zeros_like(tri), jnp.zeros_like(tri_t), _split_dot(tri_t, d, HG_SUM_TERMS)


_hgrn_sums.defvjp(_hgrn_sums_fwd, _hgrn_sums_bwd)


def _bf_dot(a, b):
    return _dot(a.astype(BF), b.astype(BF))


def _bf_dot_nt(a, b):
    return _dot_nt(a.astype(BF), b.astype(BF))


def _bf_dot_tn(a, b):
    return _dot_tn(a.astype(BF), b.astype(BF))


@jax.custom_vjp
def _mm(a, b):
    return _bf_dot(a, b)


_mm.defvjp(lambda a, b: (_bf_dot(a, b), (a, b)), lambda r, ct: (_bf_dot_nt(ct, r[1]), _bf_dot_tn(r[0], ct)))


@jax.custom_vjp
def _mm_nt(a, b):
    return _bf_dot_nt(a, b)


_mm_nt.defvjp(lambda a, b: (_bf_dot_nt(a, b), (a, b)), lambda r, ct: (_bf_dot(ct, r[1]), _bf_dot_tn(ct, r[0])))


@jax.custom_vjp
def _mm_tn(a, b):
    return _bf_dot_tn(a, b)


_mm_tn.defvjp(lambda a, b: (_bf_dot_tn(a, b), (a, b)), lambda r, ct: (_bf_dot_nt(r[1], ct), _bf_dot(r[0], ct)))


def _hgrn_chunk(tri, tri_t, masks, qraw, fpre, v, st, lb):
    q = _silu(qraw)
    f = lb + (1.0 - lb) * _sigmoid(fpre)
    k = 1.0 - f
    e = _hgrn_sums(tri, tri_t, jnp.log(f))
    prefix, suffix, whole = e[0], e[1], e[-1]
    scores = masks[0] * _mm_nt(q, k)
    for n in range(len(HG_LEVELS)):
        decay = jnp.exp(e[2 + n])
        scores = scores + masks[n + 1] * _mm_nt(q * decay, k * decay)
    o = _mm_nt(q * jnp.exp(prefix), st) + _mm(scores, v)
    st_new = st * jnp.exp(whole) + _mm_tn(v, k * jnp.exp(suffix))
    return o, st_new


HG_HEADS_PER_STEP = 2
HG_LANES = HG_HEADS_PER_STEP * HG_DIM


def _hgrn_specs(S, consts):
    col = lambda base: pl.BlockSpec((S, HG_LANES), lambda p, base=base: (0, base // HG_HEADS_PER_STEP + p))
    whole = [pl.BlockSpec(a.shape, lambda p, n=a.ndim: (0,) * n) for a in consts]
    return col, whole


def _hgrn_fwd(proj, lbs, consts):
    S = proj.shape[0]
    nc = S // HG_CHUNK
    heads = range(HG_HEADS_PER_STEP)

    def body(q_ref, f_ref, i_ref, lb_ref, tri_ref, trit_ref, mask_ref, o_ref, st_ref):
        tri, tri_t = tri_ref[...], trit_ref[...]
        masks = [mask_ref[n] for n in range(len(HG_LEVELS) + 1)]

        def chunk(ci, sts):
            r0 = pl.multiple_of(ci * HG_CHUNK, HG_CHUNK)
            rows = pl.ds(r0, HG_CHUNK)
            new = []
            for hd in heads:
                lanes = pl.ds(hd * HG_DIM, HG_DIM)
                st_ref[hd, ci] = sts[hd]
                o, st_new = _hgrn_chunk(tri, tri_t, masks, q_ref[rows, lanes], f_ref[rows, lanes], i_ref[rows, lanes], sts[hd], lb_ref[hd])
                o_ref[rows, lanes] = o
                new.append(st_new)
            return tuple(new)

        lax.fori_loop(0, nc, chunk, tuple(jnp.zeros((HG_DIM, HG_DIM), F32) for _ in heads))

    col, whole = _hgrn_specs(S, consts)
    return pl.pallas_call(
        body, name="hgrn_fwd",
        grid=(WIDTH // HG_LANES,),
        in_specs=[col(16), col(20), col(24), pl.BlockSpec((HG_HEADS_PER_STEP, 1, 128), lambda p: (p, 0, 0))] + whole,
        out_specs=[col(0), pl.BlockSpec((HG_HEADS_PER_STEP, nc, HG_DIM, HG_DIM), lambda p: (p, 0, 0, 0))],
        out_shape=[jax.ShapeDtypeStruct((S, WIDTH), F32), jax.ShapeDtypeStruct((WIDTH // 128, nc, HG_DIM, HG_DIM), F32)],
        compiler_params=_params(dimension_semantics=("parallel",)),
    )(proj, proj, proj, lbs, *consts)


def _hgrn_bwd(proj, lbs, states, d_o, consts, job=None):
    S = proj.shape[0]
    nc = S // HG_CHUNK

    def body(q_ref, f_ref, i_ref, lb_ref, st_ref, do_ref, tri_ref, trit_ref, mask_ref, dq_ref, df_ref, di_ref, dlb_ref):
        masks = [mask_ref[n] for n in range(len(HG_LEVELS) + 1)]
        fn = functools.partial(_hgrn_chunk, tri_ref[...], trit_ref[...], masks)
        heads = range(HG_HEADS_PER_STEP)

        def chunk(n, carry):
            ci = nc - 1 - n
            r0 = pl.multiple_of(ci * HG_CHUNK, HG_CHUNK)
            rows = pl.ds(r0, HG_CHUNK)
            new = []
            for hd in heads:
                d_st, dlb = carry[hd]
                lanes = pl.ds(hd * HG_DIM, HG_DIM)
                _, pull = jax.vjp(fn, q_ref[rows, lanes], f_ref[rows, lanes], i_ref[rows, lanes], st_ref[hd, ci], lb_ref[hd])
                dq, df, di, d_prev, dl = pull((do_ref[rows, lanes], d_st))
                dq_ref[rows, lanes] = dq.astype(BF)
                df_ref[rows, lanes] = df.astype(BF)
                di_ref[rows, lanes] = di.astype(BF)
                new.append((d_prev, dlb + dl))
            return tuple(new)

        zero = (jnp.zeros((HG_DIM, HG_DIM), F32), jnp.zeros((1, HG_DIM), F32))
        done = lax.fori_loop(0, nc, chunk, tuple(zero for _ in heads))
        for hd in heads:
            dlb_ref[hd] = done[hd][1]

    col, whole = _hgrn_specs(S, consts)
    head = pl.BlockSpec((HG_HEADS_PER_STEP, 1, 128), lambda p: (p, 0, 0))
    out = jax.ShapeDtypeStruct((S, WIDTH), BF)
    return _hosted(
        body, job, (proj, proj, proj, lbs, states, d_o, *consts), name="hgrn_bwd",
        grid=(WIDTH // HG_LANES,),
        in_specs=[col(16), col(20), col(24), head, pl.BlockSpec((HG_HEADS_PER_STEP, nc, HG_DIM, HG_DIM), lambda p: (p, 0, 0, 0)), col(0)] + whole,
        out_specs=[col(0), col(0), col(0), head],
        out_shape=[out, out, out, jax.ShapeDtypeStruct((WIDTH // 128, 1, 128), F32)],
        semantics=("parallel",))


def _shift_down(x, n):
    rows = lax.broadcasted_iota(jnp.int32, x.shape, 0)
    return jnp.where(rows >= n, pltpu.roll(x, n, 0), 0.0)


def _shift_up(x, n):
    S = x.shape[0]
    rows = lax.broadcasted_iota(jnp.int32, x.shape, 0)
    return jnp.where(rows < S - n, pltpu.roll(x, S - n, 0), 0.0)


def _branch_fwd(proj, o_a, o_b, norm_w, conv_w, layer):
    S = proj.shape[0]

    def body(oa_ref, za_ref, ob_ref, zb_ref, nw_ref, pre_ref, post_ref, u_ref, zc_ref, cw_ref, ya_ref, yb_ref, yc_ref):
        ya_ref[...] = (oa_ref[...] * _silu(za_ref[...])).astype(BF)
        ob = ob_ref[...]
        rn = lax.rsqrt(jnp.mean(ob * ob, axis=-1, keepdims=True) + RMS_EPS)
        yb_ref[...] = (ob * rn * nw_ref[layer:layer + 1, :] * _silu(zb_ref[...])).astype(BF)
        pu = pre_ref[...] * u_ref[...]
        conv = cw_ref[2:3, :] * pu + cw_ref[1:2, :] * _shift_down(pu, 1) + cw_ref[0:1, :] * _shift_down(pu, 2)
        yc_ref[...] = (post_ref[...] * conv * _silu(zc_ref[...])).astype(BF)

    col = lambda base: pl.BlockSpec((S, 128), lambda p, base=base: (0, base + p))
    out = jax.ShapeDtypeStruct((S, WIDTH), BF)
    return pl.pallas_call(
        body, name="branch_fwd",
        grid=(WIDTH // 128,),
        in_specs=[col(0), col(12), col(0), col(28), pl.BlockSpec(norm_w.shape, lambda p: (0, 0)),
                  col(32), col(36), col(40), col(44), pl.BlockSpec((None, None, 3, 128), lambda p: (p, layer, 0, 0))],
        out_specs=[col(0), col(0), col(0)],
        out_shape=[out, out, out],
        compiler_params=_params(dimension_semantics=("parallel",)),
    )(o_a, proj, o_b, proj, norm_w, proj, proj, proj, proj, conv_w)


def _branch_bwd(proj, o_a, o_b, norm_w, conv_w, dy_a, dy_b, dy_c, layer):
    S = proj.shape[0]

    def dsilu(z):
        s = _sigmoid(z)
        return s * z, s * (1.0 + z * (1.0 - s))

    def body(oa_ref, za_ref, ob_ref, zb_ref, nw_ref, pre_ref, post_ref, u_ref, zc_ref, cw_ref, dya_ref, dyb_ref, dyc_ref,
             doa_ref, dob_ref, dza_ref, dzb_ref, dpre_ref, dpost_ref, du_ref, dzc_ref, dnw_ref, dcw_ref):
        dya = dya_ref[...]
        sa, dsa = dsilu(za_ref[...])
        doa_ref[...] = dya * sa
        dza_ref[...] = (dya * oa_ref[...] * dsa).astype(BF)

        dyb = dyb_ref[...]
        ob = ob_ref[...]
        nw = nw_ref[layer:layer + 1, :]
        sb, dsb = dsilu(zb_ref[...])
        rn = lax.rsqrt(jnp.mean(ob * ob, axis=-1, keepdims=True) + RMS_EPS)
        on = ob * rn
        dzb_ref[...] = (dyb * on * nw * dsb).astype(BF)
        don_w = dyb * sb
        dnw_ref[...] = jnp.sum(don_w * on, axis=0, keepdims=True)
        don = don_w * nw
        dob_ref[...] = rn * (don - on * jnp.mean(don * on, axis=-1, keepdims=True))

        dyc = dyc_ref[...]
        pre, post, u = pre_ref[...], post_ref[...], u_ref[...]
        sc, dsc = dsilu(zc_ref[...])
        pu = pre * u
        pu1, pu2 = _shift_down(pu, 1), _shift_down(pu, 2)
        conv = cw_ref[2:3, :] * pu + cw_ref[1:2, :] * pu1 + cw_ref[0:1, :] * pu2
        dzc_ref[...] = (dyc * post * conv * dsc).astype(BF)
        dpost_ref[...] = (dyc * conv * sc).astype(BF)
        dconv = dyc * post * sc
        dcw_ref[0:1, :] = jnp.sum(dconv * pu2, axis=0, keepdims=True)
        dcw_ref[1:2, :] = jnp.sum(dconv * pu1, axis=0, keepdims=True)
        dcw_ref[2:3, :] = jnp.sum(dconv * pu, axis=0, keepdims=True)
        dpu = cw_ref[2:3, :] * dconv + cw_ref[1:2, :] * _shift_up(dconv, 1) + cw_ref[0:1, :] * _shift_up(dconv, 2)
        dpre_ref[...] = (dpu * u).astype(BF)
        du_ref[...] = (dpu * pre).astype(BF)

    col = lambda base: pl.BlockSpec((S, 128), lambda p, base=base: (0, base + p))
    f32 = jax.ShapeDtypeStruct((S, WIDTH), F32)
    bf = jax.ShapeDtypeStruct((S, WIDTH), BF)
    return pl.pallas_call(
        body, name="branch_bwd",
        grid=(WIDTH // 128,),
        in_specs=[col(0), col(12), col(0), col(28), pl.BlockSpec(norm_w.shape, lambda p: (0, 0)),
                  col(32), col(36), col(40), col(44), pl.BlockSpec((None, None, 3, 128), lambda p: (p, layer, 0, 0)),
                  col(0), col(0), col(0)],
        out_specs=[col(0)] * 8 + [pl.BlockSpec((None, 1, 128), lambda p: (p, 0, 0)), pl.BlockSpec((None, 3, 128), lambda p: (p, 0, 0))],
        out_shape=[f32, f32, bf, bf, bf, bf, bf, bf, jax.ShapeDtypeStruct((WIDTH // 128, 1, 128), F32),
                   jax.ShapeDtypeStruct((WIDTH // 128, 3, 128), F32)],
        compiler_params=_params(dimension_semantics=("parallel",)),
    )(o_a, proj, o_b, proj, norm_w, proj, proj, proj, proj, conv_w, dy_a, dy_b, dy_c)


def _branch_proj(y_refs, wb_ref):
    out = []
    for i in range(3):
        yv = y_refs[i][...]
        out.append(jnp.concatenate([_dot(yv, wb_ref[j, i]) for j in range(N_CHIPS)], axis=1))
    return out


def _merge_fwd(x, mod, proj, ys, wb, wo, ln_g, ln_b, layer, tm=256):
    S, D = x.shape
    tm = min(tm, S)

    def body(x_ref, mod_ref, ga_ref, gb_ref, gc_ref, ya_ref, yb_ref, yc_ref, wb_ref, wo_ref, g_ref, b_ref, xo_ref, mg_ref, y_ref):
        ps = _branch_proj((ya_ref, yb_ref, yc_ref), wb_ref)
        merged = _sigmoid(ga_ref[...]) * ps[0] + _sigmoid(gb_ref[...]) * ps[1] + _sigmoid(gc_ref[...]) * ps[2]
        mb = merged.astype(BF)
        mg_ref[...] = mb
        y = _dot(mb, wo_ref[...].reshape(D, D))
        y_ref[...] = y
        r = ALPHA * x_ref[...] + (1.0 + mod_ref[:, 2 * D:3 * D]) * y
        xn, _ = _standardize(r)
        xo_ref[...] = xn * g_ref[layer:layer + 1, :] + b_ref[layer:layer + 1, :]

    row = lambda w, c=0: pl.BlockSpec((tm, w), lambda i, c=c: (i, c))
    whole = lambda a: pl.BlockSpec(a.shape, lambda i, n=a.ndim: (0,) * n)
    return pl.pallas_call(
        body, name="merge_fwd",
        grid=(S // tm,),
        in_specs=[row(D), whole(mod), row(D, 6), row(D, 7), row(D, 8), row(WIDTH), row(WIDTH), row(WIDTH), whole(wb),
                  whole(wo), whole(ln_g), whole(ln_b)],
        out_specs=[row(D), row(D), row(D)],
        out_shape=[jax.ShapeDtypeStruct((S, D), F32), jax.ShapeDtypeStruct((S, D), BF), jax.ShapeDtypeStruct((S, D), F32)],
        compiler_params=_params(dimension_semantics=("parallel",)),
    )(x, mod, proj, proj, proj, *ys, wb, wo, ln_g, ln_b)


def _merge_bwd(dxo, x, y, mod, proj, ys, wb, wo, ln_g, layer, tm=256):
    S, D = x.shape
    tm = min(tm, S)

    def body(dxo_ref, x_ref, y_ref, mod_ref, ga_ref, gb_ref, gc_ref, ya_ref, yb_ref, yc_ref, wb_ref, wo_ref, g_ref,
             dxr_ref, dy_ref, dp_ref, dg_ref, dya_ref, dyb_ref, dyc_ref, dlg_ref, dlb_ref, dgt_ref):
        @pl.when(pl.program_id(0) == 0)
        def _():
            dlg_ref[...] = jnp.zeros_like(dlg_ref)
            dlb_ref[...] = jnp.zeros_like(dlb_ref)
            dgt_ref[...] = jnp.zeros_like(dgt_ref)

        gate1 = 1.0 + mod_ref[:, 2 * D:3 * D]
        yv = y_ref[...]
        xn, rstd = _standardize(ALPHA * x_ref[...] + gate1 * yv)
        dxo = dxo_ref[...]
        dlg_ref[...] += jnp.sum(dxo * xn, axis=0, keepdims=True)
        dlb_ref[...] += jnp.sum(dxo, axis=0, keepdims=True)
        dr = _standardize_bwd(dxo * g_ref[layer:layer + 1, :], xn, rstd)
        dxr_ref[...] = ALPHA * dr
        dgt_ref[...] += jnp.sum(dr * yv, axis=0, keepdims=True)
        dyb = (gate1 * dr).astype(BF)
        dy_ref[...] = dyb
        dmerged = _dot_nt(dyb, wo_ref[...].reshape(D, D))
        ps = _branch_proj((ya_ref, yb_ref, yc_ref), wb_ref)
        quarter = D // N_CHIPS
        for i, (gate_ref, out_ref) in enumerate(((ga_ref, dya_ref), (gb_ref, dyb_ref), (gc_ref, dyc_ref))):
            sg = _sigmoid(gate_ref[...])
            dg_ref[:, i * D:(i + 1) * D] = (dmerged * ps[i] * sg * (1.0 - sg)).astype(BF)
            dp = (dmerged * sg).astype(BF)
            dp_ref[:, i * D:(i + 1) * D] = dp
            acc = None
            for j in range(N_CHIPS):
                t = _dot_nt(dp[:, j * quarter:(j + 1) * quarter], wb_ref[j, i])
                acc = t if acc is None else acc + t
            out_ref[...] = acc

    row = lambda w, c=0: pl.BlockSpec((tm, w), lambda i, c=c: (i, c))
    whole = lambda a: pl.BlockSpec(a.shape, lambda i, n=a.ndim: (0,) * n)
    vec = pl.BlockSpec((1, D), lambda i: (0, 0))
    sd = jax.ShapeDtypeStruct
    return pl.pallas_call(
        body, name="merge_bwd",
        grid=(S // tm,),
        in_specs=[row(D), row(D), row(D), whole(mod), row(D, 6), row(D, 7), row(D, 8), row(WIDTH), row(WIDTH), row(WIDTH), whole(wb),
                  whole(wo), whole(ln_g)],
        out_specs=[row(D), row(D), row(3 * D), row(3 * D), row(WIDTH), row(WIDTH), row(WIDTH), vec, vec, vec],
        out_shape=[sd((S, D), F32), sd((S, D), BF), sd((S, 3 * D), BF), sd((S, 3 * D), BF), sd((S, WIDTH), F32), sd((S, WIDTH), F32),
                   sd((S, WIDTH), F32), sd((1, D), F32), sd((1, D), F32), sd((1, D), F32)],
        compiler_params=_params(dimension_semantics=("arbitrary",)),
    )(dxo, x, y, mod, proj, proj, proj, *ys, wb, wo, ln_g)


def _loss_head(x, target, tm=512):
    S, D = x.shape
    tm = min(tm, S)

    def body(x_ref, t_ref, dx_ref, loss_ref):
        @pl.when(pl.program_id(0) == 0)
        def _():
            loss_ref[...] = jnp.zeros_like(loss_ref)

        err = x_ref[...] - t_ref[...]
        dx_ref[...] = err * (1.0 / D)
        loss_ref[...] += 0.5 * jnp.sum(jnp.mean(err * err, axis=-1, keepdims=True))

    row = pl.BlockSpec((tm, D), lambda i: (i, 0))
    return pl.pallas_call(
        body, name="loss_head",
        grid=(S // tm,),
        in_specs=[row, row],
        out_specs=[row, pl.BlockSpec((8, 128), lambda i: (0, 0))],
        out_shape=[jax.ShapeDtypeStruct((S, D), F32), jax.ShapeDtypeStruct((8, 128), F32)],
        compiler_params=_params(dimension_semantics=("arbitrary",)),
    )(x, target)


def _proj_bwd(dproj, wg, x, mod, dx_res, job=None, tm=512, tk=768):
    S, D = x.shape
    tm = min(tm, S)
    shard = wg.shape[-1]
    per = shard // tk
    nk = IN_COLS // tk

    def body(dp_ref, w_ref, x_ref, mod_ref, dxr_ref, dx_ref, dsh_ref, dsc_ref, acc):
        i, k = pl.program_id(0), pl.program_id(1)

        @pl.when((i == 0) & (k == 0))
        def _():
            dsh_ref[...] = jnp.zeros_like(dsh_ref)
            dsc_ref[...] = jnp.zeros_like(dsc_ref)

        @pl.when(k == 0)
        def _():
            acc[...] = jnp.zeros_like(acc)

        acc[...] += _dot_nt(dp_ref[...], w_ref[...])

        @pl.when(k == nk - 1)
        def _():
            dh = acc[...]
            xs, rstd = _standardize(x_ref[...])
            dsh_ref[...] += jnp.sum(dh, axis=0, keepdims=True)
            dsc_ref[...] += jnp.sum(dh * xs, axis=0, keepdims=True)
            dx_ref[...] = _standardize_bwd(dh * (1.0 + mod_ref[:, D:2 * D]), xs, rstd) + dxr_ref[...]

    row = pl.BlockSpec((tm, D), lambda i, k: (i, 0))
    vec = pl.BlockSpec((1, D), lambda i, k: (0, 0))
    return _hosted(
        body, job, (dproj, wg, x, mod, dx_res), name="proj_bwd",
        grid=(S // tm, nk),
        in_specs=[pl.BlockSpec((tm, tk), lambda i, k: (i, k)),
                  pl.BlockSpec((None, D, tk), lambda i, k: (k // per, 0, k % per)),
                  row, pl.BlockSpec((1, 3 * D), lambda i, k: (0, 0)), row],
        out_specs=[row, vec, vec],
        out_shape=[jax.ShapeDtypeStruct((S, D), F32), jax.ShapeDtypeStruct((1, D), F32), jax.ShapeDtypeStruct((1, D), F32)],
        scratch_shapes=[pltpu.VMEM((tm, D), F32)],
        semantics=("arbitrary", "arbitrary"))


def _grad_w(a, b, mid, prev, name, tm, tn, b_col0=0):
    S, M = a.shape
    shape = prev.shape
    n_shard = shape[-1]
    per = n_shard // tn
    nm = M // tm
    lead = len(mid)

    def body(*refs):
        a_ref, b_ref, o_ref = refs[0], refs[1], refs[-1]
        o_ref[...] = _dot_tn(a_ref[...], b_ref[...]).astype(BF)

    in_specs = [pl.BlockSpec((S, tm), lambda m, n: (0, m)),
                pl.BlockSpec((S, tn), lambda m, n: (0, b_col0 // tn + n))]
    args = [a, b]
    aliases = {}
    if not isinstance(prev, jax.ShapeDtypeStruct):
        in_specs.append(pl.BlockSpec(memory_space=pl.ANY))
        args.append(prev)
        aliases = {2: 0}
    return pl.pallas_call(
        body, name=name,
        grid=(nm, N_CHIPS * per),
        in_specs=in_specs,
        out_specs=pl.BlockSpec((None,) + (None,) * lead + (tm, tn), lambda m, n: (n // per,) + tuple(mid) + (m, n % per)),
        out_shape=jax.ShapeDtypeStruct(shape, BF),
        input_output_aliases=aliases,
        compiler_params=_params(dimension_semantics=("parallel", "parallel")),
    )(*args)


def _all_gather8(x, name):
    R, N = x.shape

    def body(x_ref, out_ref, send_sems, recv_sems):
        mx, my, mc = lax.axis_index("x"), lax.axis_index("y"), lax.axis_index("c")
        me = 4 * mx + 2 * my + mc
        out_ref[me] = x_ref[...]
        copies = []
        for k in range(1, N_DEV):
            peer = (_flip(mx, k & 4), _flip(my, k & 2), _flip(mc, k & 1))
            cp = pltpu.make_async_remote_copy(src_ref=x_ref, dst_ref=out_ref.at[me], send_sem=send_sems.at[k - 1],
                                              recv_sem=recv_sems.at[k - 1], device_id=peer, device_id_type=MESH)
            cp.start()
            copies.append(cp)
        for cp in copies:
            cp.wait()

    return pl.pallas_call(
        body, name=name,
        in_specs=[pl.BlockSpec(memory_space=pltpu.VMEM)],
        out_specs=pl.BlockSpec(memory_space=pltpu.VMEM),
        out_shape=jax.ShapeDtypeStruct((N_DEV, R, N), F32),
        scratch_shapes=[pltpu.SemaphoreType.DMA((N_DEV - 1,)), pltpu.SemaphoreType.DMA((N_DEV - 1,))],
        compiler_params=_params(),
    )(x)


def _rows2d(a):
    return a.reshape(-1, a.shape[-1])


def _tile_rows(rows, cols, n_arrays):
    budget = (24 << 20) // (n_arrays * 2 * 4 * cols)
    if rows <= budget:
        return rows
    tm = 8
    for cand in range(8, budget + 1, 8):
        if rows % cand == 0:
            tm = cand
    return tm


SUM_ROWS = 256


def _sum_cores(g, sent, where):
    chips, lead, _, r, cols = g.shape
    tr = min(r, SUM_ROWS)

    def body(where_ref, g_ref, s_ref, out_ref):
        out_ref[...] = (g_ref[...].astype(F32) + s_ref[...].astype(F32)).astype(BF)

    spec = pl.BlockSpec((None, tr, cols), lambda i, j, where_ref: (i, j, 0))
    out = pl.pallas_call(
        body, name="sum_cores",
        grid_spec=pltpu.PrefetchScalarGridSpec(
            num_scalar_prefetch=1, grid=(chips * lead, r // tr),
            in_specs=[pl.BlockSpec((None, None, tr, cols), lambda i, j, where_ref: (i, where_ref[1], j, 0)), spec],
            out_specs=spec),
        out_shape=jax.ShapeDtypeStruct((chips * lead, r, cols), BF),
        compiler_params=_params(dimension_semantics=("parallel", "parallel")),
    )(where, g.reshape(chips * lead, 2, r, cols), sent.reshape(chips * lead, r, cols))
    return out.reshape(chips, lead, r, cols)


def _sum_chips(sums, got, where):
    _, lead, r, cols = sums.shape
    tr = min(r, SUM_ROWS)

    def body(where_ref, s_ref, g_ref, out_ref):
        out_ref[...] = ((s_ref[...].astype(F32) + g_ref[0].astype(F32)) + g_ref[1].astype(F32)) + g_ref[2].astype(F32)

    return pl.pallas_call(
        body, name="sum_chips",
        grid_spec=pltpu.PrefetchScalarGridSpec(
            num_scalar_prefetch=1, grid=(lead, r // tr),
            in_specs=[pl.BlockSpec((None, None, tr, cols), lambda i, j, where_ref: (where_ref[0], i, j, 0)),
                      pl.BlockSpec((N_CHIPS - 1, None, tr, cols), lambda i, j, where_ref: (0, i, j, 0))],
            out_specs=pl.BlockSpec((None, None, tr, cols), lambda i, j, where_ref: (i, where_ref[1], j, 0))),
        out_shape=jax.ShapeDtypeStruct((lead, 2, r, cols), F32),
        compiler_params=_params(dimension_semantics=("parallel", "parallel")),
    )(where, sums, got)


def _adamw(w, m, v, groups, name):
    shape = w.shape
    w2, m2, v2 = _rows2d(w), _rows2d(m), _rows2d(v)
    rows, cols = w2.shape
    ng = len(groups)
    n = len(groups[0])
    slab = rows // ng
    gs = [_rows2d(g) for grp in groups for g in grp]
    tm = _tile_rows(slab, cols, 7 + n)
    tiles = slab // tm
    c1 = 1.0 / (1.0 - ADAM_B1 ** ADAM_STEP)
    c2 = 1.0 / (1.0 - ADAM_B2 ** ADAM_STEP)

    def body(*refs):
        w_ref, m_ref, v_ref = refs[:3]
        g_refs = refs[3:3 + ng * n]
        go_ref, d_ref, mo_ref, vo_ref = refs[3 + ng * n:]
        which = pl.program_id(0)
        for s in range(ng):
            @pl.when(which == s)
            def _(s=s):
                g = g_refs[s * n][...]
                for r in g_refs[s * n + 1:(s + 1) * n]:
                    g = g + r[...]
                mn = ADAM_B1 * m_ref[...] + (1.0 - ADAM_B1) * g
                vn = ADAM_B2 * v_ref[...] + (1.0 - ADAM_B2) * (g * g)
                go_ref[...] = g
                mo_ref[...] = mn
                vo_ref[...] = vn
                d_ref[...] = -ADAM_LR * ((mn * c1) / (jnp.sqrt(vn * c2) + ADAM_EPS) + ADAM_WD * w_ref[...])

    spec = pl.BlockSpec((tm, cols), lambda s, i: (s * tiles + i, 0))
    g_specs = [pl.BlockSpec((tm, cols), lambda s, i, k=k: (jnp.where(s == k, i, jnp.where(s < k, 0, tiles - 1)), 0))
               for k in range(ng) for _ in range(n)]
    outs = pl.pallas_call(
        body, name=name,
        grid=(ng, tiles),
        in_specs=[spec] * 3 + g_specs,
        out_specs=[spec] * 4,
        out_shape=[jax.ShapeDtypeStruct((rows, cols), F32)] * 4,
        compiler_params=_params(dimension_semantics=("arbitrary", "arbitrary")),
    )(w2, m2, v2, *gs)
    return [o.reshape(shape) for o in outs]


def _lower_bounds(r0, r1):
    top = jnp.maximum(r0, r1)
    e0, e1 = jnp.exp(r0 - top), jnp.exp(r1 - top)
    p0, p1 = e0 / (e0 + e1), e1 / (e0 + e1)
    return p0 - p0, (p0 + p1) - p0


def _lbs_fwd(lb_raw):
    def body(lb_ref, out_ref):
        l0, l1 = _lower_bounds(lb_ref[0:1, :], lb_ref[1:2, :])
        out_ref[0:1, :] = l0
        out_ref[1:2, :] = l1

    return pl.pallas_call(body, name="lower_bounds", out_shape=jax.ShapeDtypeStruct(lb_raw.shape, F32), compiler_params=_params())(lb_raw)


def _mod_rows(c_all, w_mod, tn=768):
    _, D, cols = w_mod.shape

    def body(c_ref, w_ref, out_ref):
        out_ref[...] = _dot(c_ref[...].astype(BF), w_ref[...].astype(BF))

    return pl.pallas_call(
        body, name="mod_rows",
        grid=(DEPTH,),
        in_specs=[pl.BlockSpec((N_DEV, D), lambda l: (0, 0)), pl.BlockSpec((None, D, cols), lambda l: (l, 0, 0))],
        out_specs=pl.BlockSpec((N_DEV, cols), lambda l: (0, l)),
        out_shape=jax.ShapeDtypeStruct((N_DEV, DEPTH * cols), F32),
        compiler_params=_params(dimension_semantics=("parallel",)),
    )(c_all, w_mod)


def _grad_w_mod(c_all, dmod_cols):
    D = c_all.shape[1]
    cols = dmod_cols.shape[-1]

    def body(c_ref, d_ref, out_ref):
        out_ref[...] = _dot_tn(c_ref[...].astype(BF), d_ref[...].astype(BF))

    return pl.pallas_call(
        body, name="grad_w_mod",
        grid=(DEPTH,),
        in_specs=[pl.BlockSpec((N_DEV, D), lambda l: (0, 0)), pl.BlockSpec((None, N_DEV, cols), lambda l: (l, 0, 0))],
        out_specs=pl.BlockSpec((None, D, cols), lambda l: (l, 0, 0)),
        out_shape=jax.ShapeDtypeStruct((DEPTH, D, cols), F32),
        compiler_params=_params(dimension_semantics=("parallel",)),
    )(c_all, dmod_cols)


def _sum_devices(parts, lb_raw, n_lb):
    _, R, N = parts.shape

    def body(p_ref, lb_ref, out_ref, dlb_ref):
        acc = p_ref[0]
        for d in range(1, N_DEV):
            acc = acc + p_ref[d]
        out_ref[...] = acc
        dl = tuple(acc[0:1, n_lb + l * WIDTH:n_lb + (l + 1) * WIDTH] for l in range(DEPTH))
        _, pull = jax.vjp(_lower_bounds, lb_ref[0:1, :], lb_ref[1:2, :])
        d0, d1 = pull(dl)
        dlb_ref[0:1, :] = d0
        dlb_ref[1:2, :] = d1

    return pl.pallas_call(
        body, name="sum_devices",
        out_shape=[jax.ShapeDtypeStruct((R, N), F32), jax.ShapeDtypeStruct(lb_raw.shape, F32)],
        compiler_params=_params(),
    )(parts, lb_raw)


def _pad_rows(a, rows=8):
    return jnp.concatenate([a, jnp.zeros((rows - a.shape[0],) + a.shape[1:], a.dtype)], axis=0)


def kernel(x, c, w_mod, b_mod, w_in, conv_w, hgrn_norm_w, lower_bounds, w_branch, w_out, ln_g, ln_b, loss_target, m_w_mod, m_b_mod, m_w_in, m_conv_w, m_hgrn_norm_w, m_lower_bounds, m_w_branch, m_w_out, m_ln_g, m_ln_b, v_w_mod, v_b_mod, v_w_in, v_conv_w, v_hgrn_norm_w, v_lower_bounds, v_w_branch, v_w_out, v_ln_g, v_ln_b):
    D = D_MODEL
    x0 = x[0]
    target = loss_target[0]
    S = x0.shape[0]
    mx, my, mc = lax.axis_index("x"), lax.axis_index("y"), lax.axis_index("c")
    chip = 2 * mx + my
    me = 2 * chip + mc
    mod_cols = 3 * D // N_CHIPS

    plan = _Plan(w_in.astype(BF), w_branch.astype(BF), w_out.astype(BF), chip, mc)
    plan.first()

    n_conv = DEPTH * 3 * (WIDTH // N_CHIPS)
    first = jnp.concatenate([c, conv_w.reshape(1, n_conv)], axis=1)
    first = _all_gather8(_pad_rows(first), "gather_c")
    c_all = first[:, 0, :D]
    conv_all = first[:, 0, D:].reshape(N_DEV, DEPTH, 3, WIDTH // N_CHIPS)[0::2]
    mod_part = _all_gather8(_mod_rows(c_all, w_mod), "gather_mod")[0::2]
    mod_part = lax.dynamic_index_in_dim(mod_part, me, axis=1, keepdims=False).reshape(N_CHIPS, DEPTH, mod_cols)
    mods = [(mod_part[:, l].reshape(1, 3 * D) + b_mod[l][None, :]) for l in range(DEPTH)]
    lbs = _lbs_fwd(lower_bounds).reshape(DEPTH, WIDTH // 128, 1, 128)
    loss_blk, dx, small = _local_step(x0, target, mods, lbs, conv_all, hgrn_norm_w, ln_g, ln_b, plan)

    whole = plan.finish()
    grads = {kind: [[whole[(kind, l)]] for l in range(DEPTH)] for kind in ("in", "br", "out")}

    n_mod, n_nw, n_lb, n_ln, n_cw = DEPTH * 3 * D, DEPTH * 128, DEPTH * WIDTH, DEPTH * D, DEPTH * 3 * WIDTH
    row = jnp.concatenate(
        [jnp.concatenate([small[l][0], small[l][1], small[l][2]], axis=1) for l in range(DEPTH)]
        + [jnp.sum(small[l][3], axis=0) for l in range(DEPTH)]
        + [small[l][4].reshape(1, WIDTH) for l in range(DEPTH)]
        + [small[l][5] for l in range(DEPTH)] + [small[l][6] for l in range(DEPTH)]
        + [jnp.transpose(small[l][7], (1, 0, 2)).reshape(1, 3 * WIDTH) for l in range(DEPTH)]
        + [loss_blk[0:1, :]], axis=1)
    gathered = _all_gather8(_pad_rows(row), "gather_small")
    off_nw = n_mod
    off_lb = off_nw + n_nw
    off_lng = off_lb + n_lb
    off_lnb = off_lng + n_ln
    off_cw = off_lnb + n_ln
    off_loss = off_cw + n_cw
    total, d_lower = _sum_devices(gathered, lower_bounds, off_lb)
    loss = total[0, off_loss]
    d_b_mod = total[0, :n_mod].reshape(DEPTH, 3 * D)
    d_norm_w = total[0, off_nw:off_lb].reshape(DEPTH, 128)
    d_ln_g = total[0, off_lng:off_lnb].reshape(DEPTH, D)
    d_ln_b = total[0, off_lnb:off_cw].reshape(DEPTH, D)
    d_conv = total[0, off_cw:off_loss].reshape(DEPTH, 3, N_CHIPS, WIDTH // N_CHIPS)
    d_conv = lax.dynamic_index_in_dim(d_conv, chip, axis=2, keepdims=False)
    dmod_all = gathered[:, 0, :n_mod].reshape(N_DEV, DEPTH, N_CHIPS, mod_cols)
    dmod_cols = jnp.transpose(lax.dynamic_index_in_dim(dmod_all, chip, axis=2, keepdims=False), (1, 0, 2))
    d_w_mod = _grad_w_mod(c_all, dmod_cols)

    res = {}
    res["w_mod"] = _adamw(w_mod, m_w_mod, v_w_mod, [[d_w_mod]], "adamw_w_mod")
    res["b_mod"] = _adamw(b_mod, m_b_mod, v_b_mod, [[d_b_mod]], "adamw_b_mod")
    res["w_in"] = _adamw(w_in, m_w_in, v_w_in, grads["in"], "adamw_w_in")
    res["conv_w"] = _adamw(conv_w, m_conv_w, v_conv_w, [[d_conv]], "adamw_conv_w")
    res["hgrn_norm_w"] = _adamw(hgrn_norm_w, m_hgrn_norm_w, v_hgrn_norm_w, [[d_norm_w]], "adamw_norm_w")
    res["lower_bounds"] = _adamw(lower_bounds, m_lower_bounds, v_lower_bounds, [[d_lower]], "adamw_lower_bounds")
    res["w_branch"] = _adamw(w_branch, m_w_branch, v_w_branch, grads["br"], "adamw_w_branch")
    res["w_out"] = _adamw(w_out, m_w_out, v_w_out, grads["out"], "adamw_w_out")
    res["ln_g"] = _adamw(ln_g, m_ln_g, v_ln_g, [[d_ln_g]], "adamw_ln_g")
    res["ln_b"] = _adamw(ln_b, m_ln_b, v_ln_b, [[d_ln_b]], "adamw_ln_b")
    names = ["w_mod", "b_mod", "w_in", "conv_w", "hgrn_norm_w", "lower_bounds", "w_branch", "w_out", "ln_g", "ln_b"]
    return (loss, dx[None], *[res[n][0] for n in names], *[res[n][1] for n in names],
            *[res[n][2] for n in names], *[res[n][3] for n in names])


class _Plan:
    def __init__(self, w_in, w_br, w_out, chip, core):
        self.local = {"in": w_in, "br": w_br, "out": w_out}
        self.chip, self.where = chip, jnp.stack([chip, core]).astype(jnp.int32)
        self.gathered, self.grads, self.chip_sums, self.scattered, self.pending = {}, {}, {}, {}, {}

    def _slab(self, key):
        mine = _halves(self.local[key[0]][key[1]])
        return lax.dynamic_update_slice(lax.empty((N_CHIPS,) + mine.shape, mine.dtype), mine[None], (self.chip, 0, 0, 0, 0))

    def _gather(self, keys):
        return ("gather", keys), _gather_job([self._slab(key) for key in keys])

    def _to_sibling(self, keys):
        return ("to_sibling", keys), _to_sibling_job([_halves(self.grads[key], 1) for key in keys])

    def _scatter(self, keys):
        return ("scatter", keys), _scatter_job([self.chip_sums[key] for key in keys])

    def job(self, stage, l):
        parts = []
        if stage == "proj_fwd":
            parts = [self._gather([("br", l), ("out", l)])]
        elif stage == "attn_fwd" and l + 1 < DEPTH:
            parts = [self._gather([("in", l + 1)])]
        elif stage == "attn_bwd":
            parts = [self._to_sibling([("out", l), ("br", l)])] + ([self._scatter([("in", l + 1)])] if l + 1 < DEPTH else [])
        elif stage == "hgrn_bwd":
            parts = [self._scatter([("out", l), ("br", l)])]
        elif stage == "proj_bwd":
            parts = [self._to_sibling([("in", l)])]
        self.pending[(stage, l)] = [(tag, len(job.outs)) for tag, job in parts]
        return _join_jobs([job for _, job in parts])

    def done(self, stage, l, outs):
        if outs is None:
            return
        at = 0
        for (what, keys), n_outs in self.pending[(stage, l)]:
            mine, at = outs[at:at + n_outs], at + n_outs
            for n, key in enumerate(keys):
                if what == "gather":
                    self.gathered[key] = mine[n].reshape((N_CHIPS,) + self.local[key[0]].shape[1:])
                elif what == "to_sibling":
                    self.chip_sums[key] = _sum_cores(_halves(self.grads[key], 1), mine[n], self.where)
                else:
                    self.scattered[key] = mine[n]

    def first(self):
        tag, job = self._gather([("in", 0)])
        self.pending[("first", 0)] = [(tag, len(job.outs))]
        self.done("first", 0, _run_job(job, "gather_first"))

    def finish(self):
        keys = [(kind, l) for kind in ("in", "br", "out") for l in range(DEPTH)]
        _, last = self._scatter([("in", 0)])
        self.scattered[("in", 0)] = _run_job(last, "scatter_last")[0]
        halves = [_sum_chips(self.chip_sums[key], self.scattered[key], self.where) for key in keys]
        whole = _run_job(_place_job(halves), "place_halves")
        return {key: w.reshape(self.grads[key].shape[1:]) for key, w in zip(keys, whole)}


def _local_step(x0, target, mods, lbs, conv_all, hgrn_norm_w, ln_g, ln_b, plan):
    D = D_MODEL
    after, before = _attn_consts()
    hg_consts = _hgrn_consts()

    saved = []
    xl = x0
    for l in range(DEPTH):
        (proj, h), got = _proj_fwd(xl, mods[l], plan.gathered[("in", l)], plan.job("proj_fwd", l))
        plan.done("proj_fwd", l, got)
        (o_a, tot), got = _attn_fwd(proj, after, plan.job("attn_fwd", l))
        plan.done("attn_fwd", l, got)
        o_b, states = _hgrn_fwd(proj, lbs[l], hg_consts)
        ys = _branch_fwd(proj, o_a, o_b, hgrn_norm_w, conv_all, l)
        x_next, merged, y = _merge_fwd(xl, mods[l], proj, ys, plan.gathered[("br", l)], plan.gathered[("out", l)], ln_g, ln_b, l)
        saved.append((xl, proj, h, o_a, tot, o_b, states, ys, merged, y))
        xl = x_next
    dx, loss_blk = _loss_head(xl, target)

    small = [None] * DEPTH
    for l in reversed(range(DEPTH)):
        xin, proj, h, o_a, tot, o_b, states, ys, merged, y = saved[l]
        dx_res, dy, dp, dg, dy_a, dy_b, dy_c, dln_g, dln_b, dgate = _merge_bwd(
            dx, xin, y, mods[l], proj, ys, plan.gathered[("br", l)], plan.gathered[("out", l)], ln_g, l)
        plan.grads[("out", l)] = _grad_w_out(merged, dy)
        g_br = jax.ShapeDtypeStruct((N_CHIPS, 3, WIDTH, D // N_CHIPS), BF)
        for i in range(3):
            g_br = _grad_w(ys[i], dp, (i,), g_br, "grad_w_branch", tm=WIDTH, tn=D // N_CHIPS, b_col0=i * D)
        plan.grads[("br", l)] = g_br
        d_oa, d_ob, dz_a, dz_b, dpre, dpost, du, dz_c, dnorm_w, dconv_w = _branch_bwd(proj, o_a, o_b, hgrn_norm_w, conv_all, dy_a, dy_b, dy_c, l)
        (dq_a, dk_a, dv_a), got = _attn_bwd(proj, d_oa, tot, after, before, plan.job("attn_bwd", l))
        plan.done("attn_bwd", l, got)
        (dq_b, df_b, di_b, dlb), got = _hgrn_bwd(proj, lbs[l], states, d_ob, hg_consts, plan.job("hgrn_bwd", l))
        plan.done("hgrn_bwd", l, got)
        dproj = jnp.concatenate([dq_a, dk_a, dv_a, dz_a, dq_b, df_b, di_b, dz_b, dpre, dpost, du, dz_c, dg], axis=1)
        plan.grads[("in", l)] = _grad_w(h, dproj, (), jax.ShapeDtypeStruct((N_CHIPS, D, IN_COLS // N_CHIPS), BF), "grad_w_in", tm=512, tn=768)
        (dx, dshift, dscale), got = _proj_bwd(dproj, plan.gathered[("in", l)], xin, mods[l], dx_res, plan.job("proj_bwd", l))
        plan.done("proj_bwd", l, got)
        small[l] = (dshift, dscale, dgate, dnorm_w, dlb, dln_g, dln_b, dconv_w)
    return loss_blk, dx, small


def _grad_w_out(merged, dy):
    S, D = merged.shape
    q = D // N_CHIPS

    def body(a_ref, b_ref, o_ref):
        o_ref[...] = _dot_tn(a_ref[...], b_ref[...]).astype(BF)

    return pl.pallas_call(
        body, name="grad_w_out",
        grid=(N_CHIPS,),
        in_specs=[pl.BlockSpec((S, q), lambda j: (0, j)), pl.BlockSpec((S, D), lambda j: (0, 0))],
        out_specs=pl.BlockSpec((None, q, D), lambda j: (j, 0, 0)),
        out_shape=jax.ShapeDtypeStruct((N_CHIPS, q, D), BF),
        compiler_params=_params(dimension_semantics=("parallel",)),
    )(merged, dy)
```

```python
import functools
import math

import numpy as np
import jax
import jax.numpy as jnp
from jax import lax
from jax.experimental import pallas as pl
from jax.experimental.pallas import tpu as pltpu

F32 = jnp.float32
BF = jnp.bfloat16
MESH = pl.DeviceIdType.MESH

DEPTH = 2
D_MODEL = 1024
WIDTH = 512
IN_COLS = 12 * WIDTH + 3 * D_MODEL
N_CHIPS = 4
N_DEV = 8
SB_BLOCK = 128
SB_HEAD_DIM = 64
HG_CHUNK = 64
HG_DIM = 128
LN_EPS = 1e-5
RMS_EPS = 1e-6
ALPHA = (2.0 * DEPTH) ** 0.25
ADAM_LR, ADAM_B1, ADAM_B2, ADAM_EPS, ADAM_WD, ADAM_STEP = 0.001, 0.9, 0.999, 1e-08, 0.01, 10
VMEM_LIMIT = 56 << 20


def _params(**kw):
    return pltpu.CompilerParams(vmem_limit_bytes=VMEM_LIMIT, **kw)


def _dot(a, b):
    return jnp.dot(a, b, preferred_element_type=F32)


def _dot_nt(a, b):
    return lax.dot_general(a, b, (((1,), (1,)), ((), ())), preferred_element_type=F32)


def _dot_tn(a, b):
    return lax.dot_general(a, b, (((0,), (0,)), ((), ())), preferred_element_type=F32)


def _sigmoid(x):
    return 1.0 / (1.0 + jnp.exp(-x))


def _silu(x):
    return x * _sigmoid(x)


def _softplus(z):
    return jnp.maximum(z, 0.0) + jnp.log(1.0 + jnp.exp(-jnp.abs(z)))


def _split_dot(t, g, terms):
    acc = None
    rest = g
    for _ in range(terms):
        part = rest.astype(BF)
        rest = rest - part.astype(F32)
        d = _dot(t, part)
        acc = d if acc is None else acc + d
    return acc


def _split_dot_r(g, t, terms):
    acc = None
    rest = g
    for _ in range(terms):
        part = rest.astype(BF)
        rest = rest - part.astype(F32)
        d = _dot(part, t)
        acc = d if acc is None else acc + d
    return acc


def _standardize(x):
    mu = jnp.mean(x, axis=-1, keepdims=True)
    xc = x - mu
    var = jnp.mean(xc * xc, axis=-1, keepdims=True)
    rstd = lax.rsqrt(var + LN_EPS)
    return xc * rstd, rstd


def _standardize_bwd(dxs, xs, rstd):
    return rstd * (dxs - jnp.mean(dxs, axis=-1, keepdims=True) - xs * jnp.mean(dxs * xs, axis=-1, keepdims=True))


class _Job:
    def __init__(self, ins, outs, sems, make, alias=None):
        self.ins, self.outs, self.sems, self.make = list(ins), list(outs), list(sems), make
        self.alias = dict(alias or {})


def _join_jobs(jobs):
    jobs = [j for j in jobs if j is not None]
    if len(jobs) <= 1:
        return jobs[0] if jobs else None

    def make(ins, outs, sems):
        phases, i, o, s = [], 0, 0, 0
        for j in jobs:
            got = j.make(ins[i:i + len(j.ins)], outs[o:o + len(j.outs)], sems[s:s + len(j.sems)])
            i, o, s = i + len(j.ins), o + len(j.outs), s + len(j.sems)
            for n, phase in enumerate(got):
                if n == len(phases):
                    phases.append([])
                phases[n] += phase
        return phases

    alias, i, o = {}, 0, 0
    for j in jobs:
        alias.update({i + a: o + b for a, b in j.alias.items()})
        i, o = i + len(j.ins), o + len(j.outs)
    return _Job(sum([j.ins for j in jobs], []), sum([j.outs for j in jobs], []), sum([j.sems for j in jobs], []), make, alias)


def _flip(v, bit):
    return 1 - v if bit else v


def _halves(a, front=0):
    shape = a.shape
    lead = math.prod(shape[front:-2])
    return a.reshape(shape[:front] + (lead, 2, shape[-2] // 2, shape[-1]))


def _dma_sems(*shapes):
    return [pltpu.SemaphoreType.DMA(s) for s in shapes]


def _same(arrays):
    return [jax.ShapeDtypeStruct(a.shape, a.dtype) for a in arrays]


def _gather_job(slabs):
    n = len(slabs)

    def make(ins, outs, sems):
        send1, recv1, send2, recv2 = sems
        mx, my, mc = lax.axis_index("x"), lax.axis_index("y"), lax.axis_index("c")
        fetch, pass_on = [], []
        for a in range(n):
            ours = outs[a].at[2 * mx + my, :, mc]
            for k in range(1, N_CHIPS):
                px, py = _flip(mx, k & 2), _flip(my, k & 1)
                fetch.append(pltpu.make_async_remote_copy(
                    src_ref=ours, dst_ref=ours, send_sem=send1.at[a, k - 1], recv_sem=recv1.at[a, k - 1],
                    device_id=(px, py, mc), device_id_type=MESH))
                theirs = outs[a].at[2 * px + py, :, mc]
                pass_on.append(pltpu.make_async_remote_copy(
                    src_ref=theirs, dst_ref=theirs, send_sem=send2.at[a, k - 1], recv_sem=recv2.at[a, k - 1],
                    device_id=(mx, my, 1 - mc), device_id_type=MESH))
        return [fetch, pass_on]

    pairs = (n, N_CHIPS - 1)
    return _Job(slabs, _same(slabs), _dma_sems(pairs, pairs, pairs, pairs), make, {a: a for a in range(n)})


def _to_sibling_job(grads):
    n = len(grads)

    def make(ins, outs, sems):
        send_sems, recv_sems = sems
        mx, my, mc = lax.axis_index("x"), lax.axis_index("y"), lax.axis_index("c")
        return [[pltpu.make_async_remote_copy(
            src_ref=ins[a].at[:, :, 1 - mc], dst_ref=outs[a], send_sem=send_sems.at[a], recv_sem=recv_sems.at[a],
            device_id=(mx, my, 1 - mc), device_id_type=MESH) for a in range(n)]]

    outs = [jax.ShapeDtypeStruct(g.shape[:2] + g.shape[3:], g.dtype) for g in grads]
    return _Job(grads, outs, _dma_sems((n,), (n,)), make)


def _scatter_job(sums):
    n = len(sums)

    def make(ins, outs, sems):
        send_sems, recv_sems = sems
        mx, my, mc = lax.axis_index("x"), lax.axis_index("y"), lax.axis_index("c")
        copies = []
        for a in range(n):
            for k in range(1, N_CHIPS):
                px, py = _flip(mx, k & 2), _flip(my, k & 1)
                copies.append(pltpu.make_async_remote_copy(
                    src_ref=ins[a].at[2 * px + py], dst_ref=outs[a].at[k - 1], send_sem=send_sems.at[a, k - 1],
                    recv_sem=recv_sems.at[a, k - 1], device_id=(px, py, mc), device_id_type=MESH))
        return [copies]

    pairs = (n, N_CHIPS - 1)
    return _Job(sums, [jax.ShapeDtypeStruct((N_CHIPS - 1,) + s.shape[1:], s.dtype) for s in sums], _dma_sems(pairs, pairs), make)


def _place_job(wholes):
    n = len(wholes)

    def make(ins, outs, sems):
        send_sems, recv_sems = sems
        mx, my, mc = lax.axis_index("x"), lax.axis_index("y"), lax.axis_index("c")
        copies = []
        for a in range(n):
            here = outs[a].at[:, mc]
            copies.append(pltpu.make_async_remote_copy(src_ref=here, dst_ref=here, send_sem=send_sems.at[a], recv_sem=recv_sems.at[a],
                                                       device_id=(mx, my, 1 - mc), device_id_type=MESH))
        return [copies]

    return _Job(wholes, _same(wholes), _dma_sems((n,), (n,)), make, {a: a for a in range(n)})


def _gather8_job(x):
    def make(ins, outs, sems):
        local_sem, send_sems, recv_sems = sems
        mx, my, mc = lax.axis_index("x"), lax.axis_index("y"), lax.axis_index("c")
        here = outs[0].at[4 * mx + 2 * my + mc]
        copies = [pltpu.make_async_copy(ins[0], here, local_sem.at[0])]
        for k in range(1, N_DEV):
            peer = (_flip(mx, k & 4), _flip(my, k & 2), _flip(mc, k & 1))
            copies.append(pltpu.make_async_remote_copy(src_ref=ins[0], dst_ref=here, send_sem=send_sems.at[k - 1],
                                                       recv_sem=recv_sems.at[k - 1], device_id=peer, device_id_type=MESH))
        return [copies]

    return _Job([x], [jax.ShapeDtypeStruct((N_DEV,) + x.shape, x.dtype)], _dma_sems((1,), (N_DEV - 1,), (N_DEV - 1,)), make)


def _run_phases(phases, first=0):
    for n, phase in enumerate(phases):
        if n >= first:
            for cp in phase:
                cp.start()
        for cp in phase:
            cp.wait()


def _run_job(job, name):
    k_in, k_out = len(job.ins), len(job.outs)

    def body(*refs):
        _run_phases(job.make(refs[:k_in], refs[k_in:k_in + k_out], refs[k_in + k_out:]))

    hbm = pl.BlockSpec(memory_space=pl.ANY)
    return pl.pallas_call(body, name=name, in_specs=[hbm] * k_in, out_specs=[hbm] * k_out, out_shape=job.outs,
                          scratch_shapes=job.sems, input_output_aliases=job.alias, compiler_params=_params())(*job.ins)


def _hosted(body, job, args, *, name, grid, in_specs, out_specs, out_shape, scratch_shapes=(), semantics):
    in_specs, out_specs, out_shape, scratch = list(in_specs), list(out_specs), list(out_shape), list(scratch_shapes)
    if job is None:
        outs = pl.pallas_call(body, name=name, grid=grid, in_specs=in_specs, out_specs=out_specs, out_shape=out_shape,
                              scratch_shapes=scratch, compiler_params=_params(dimension_semantics=semantics))(*args)
        return list(outs), None
    n_in, n_out, n_scr, k_in, k_out = len(in_specs), len(out_specs), len(scratch), len(job.ins), len(job.outs)

    def wrapped(*refs):
        ins, rest = refs[:n_in], refs[n_in:]
        job_ins, rest = rest[:k_in], rest[k_in:]
        outs, rest = rest[:n_out], rest[n_out:]
        job_outs, rest = rest[:k_out], rest[k_out:]
        scr, sems = rest[:n_scr], rest[n_scr:]
        ids = [pl.program_id(a) for a in range(len(grid))]
        first = functools.reduce(jnp.logical_and, [i == 0 for i in ids])
        last = functools.reduce(jnp.logical_and, [i == g - 1 for i, g in zip(ids, grid)])

        @pl.when(first)
        def _():
            for cp in job.make(job_ins, job_outs, sems)[0]:
                cp.start()

        body(*ins, *outs, *scr)

        @pl.when(last)
        def _():
            _run_phases(job.make(job_ins, job_outs, sems), first=1)

    hbm = pl.BlockSpec(memory_space=pl.ANY)
    outs = pl.pallas_call(
        wrapped, name=name, grid=grid, in_specs=in_specs + [hbm] * k_in, out_specs=out_specs + [hbm] * k_out,
        out_shape=out_shape + job.outs, scratch_shapes=scratch + job.sems,
        input_output_aliases={n_in + i: n_out + o for i, o in job.alias.items()},
        compiler_params=_params(dimension_semantics=("arbitrary",) * len(grid)))(*args, *job.ins)
    return list(outs[:n_out]), list(outs[n_out:])


def _proj_fwd(x, mod, wg, job=None, tm=512, tn=768):
    S, D = x.shape
    tm = min(tm, S)
    shard = wg.shape[-1]
    per = shard // tn

    def body(x_ref, mod_ref, w_ref, proj_ref, h_ref, hs):
        @pl.when(pl.program_id(1) == 0)
        def _():
            xs, _ = _standardize(x_ref[...])
            h = xs * (1.0 + mod_ref[:, D:2 * D]) + mod_ref[:, 0:D]
            hb = h.astype(BF)
            hs[...] = hb
            h_ref[...] = hb

        proj_ref[...] = _dot(hs[...], w_ref[...])

    return _hosted(
        body, job, (x, mod, wg), name="proj_fwd",
        grid=(S // tm, IN_COLS // tn),
        in_specs=[pl.BlockSpec((tm, D), lambda i, j: (i, 0)),
                  pl.BlockSpec((1, 3 * D), lambda i, j: (0, 0)),
                  pl.BlockSpec((None, D, tn), lambda i, j: (j // per, 0, j % per))],
        out_specs=[pl.BlockSpec((tm, tn), lambda i, j: (i, j)),
                   pl.BlockSpec((tm, D), lambda i, j: (i, 0))],
        out_shape=[jax.ShapeDtypeStruct((S, IN_COLS), F32), jax.ShapeDtypeStruct((S, D), BF)],
        scratch_shapes=[pltpu.VMEM((tm, D), BF)],
        semantics=("parallel", "arbitrary"))


SB_ROWS = 256
SB_KEYS = 256


def _attn_consts():
    j = np.arange(SB_KEYS)[:, None]
    s = np.arange(SB_KEYS)[None, :]
    from_here = np.concatenate([(j >= s), (j >= s)], axis=0).astype(np.float32)
    up_to = np.concatenate([(j <= s), (j <= s)], axis=0).astype(np.float32)
    return jnp.asarray(from_here, BF), jnp.asarray(up_to, BF)


def _hi_lo(x):
    hi = lax.bitcast_convert_type(lax.bitcast_convert_type(x, jnp.uint32) & jnp.uint32(0xFFFF0000), F32)
    return hi.astype(BF), (x - hi).astype(BF)


def _sums_r(x, t2):
    hi, lo = _hi_lo(x)
    return _dot(jnp.concatenate([hi, lo], axis=1), t2)


def _all_lanes(col, lanes):
    return jnp.broadcast_to(col, (col.shape[0], lanes))


def _attn_rows(ref, r0, rows, head0, scale=None):
    v = ref[pl.ds(r0, rows), :]
    if scale is not None:
        v = v * scale
    return jnp.concatenate([jnp.where(head0, v, 0.0), jnp.where(head0, 0.0, v)], axis=0).astype(BF)


def _attn_scores(q2n, k_ref, kj, t2, from_here_ref, masked):
    c0 = pl.multiple_of(kj * SB_KEYS, SB_KEYS)
    kb = k_ref[pl.ds(c0, SB_KEYS), :].astype(BF)
    zn = _dot_nt(q2n, kb)
    lsb = jnp.minimum(zn, 0.0) - jnp.log(1.0 + jnp.exp(-jnp.abs(zn)))
    valid = None
    if masked:
        valid = (lax.broadcasted_iota(jnp.int32, zn.shape, 1) + kj * SB_KEYS) < t2
        lsb = jnp.where(valid, lsb, 0.0)
    return c0, kb, zn, valid, lsb, _sums_r(lsb, from_here_ref[...])


def _attn_fwd(proj, from_here, job=None):
    S = proj.shape[0]
    TQ = SB_ROWS
    assert S % TQ == 0 and SB_KEYS == TQ
    scale = SB_HEAD_DIM ** -0.5

    def body(q_ref, k_ref, v_ref, from_here_ref, o_ref, tot_ref, run, acc):
        head0 = lax.broadcasted_iota(jnp.int32, (1, 2 * SB_HEAD_DIM), 1) < SB_HEAD_DIM

        def qloop(qi, _):
            r0 = pl.multiple_of(qi * TQ, TQ)
            q2n = _attn_rows(q_ref, r0, TQ, head0, -scale)
            trow = lax.broadcasted_iota(jnp.int32, (TQ, SB_KEYS), 0) + qi * TQ
            t2 = jnp.concatenate([trow, trow], axis=0)
            run[...] = jnp.zeros_like(run)
            acc[...] = jnp.zeros_like(acc)

            def step(kj, masked):
                c0, _, zn, valid, _, sums = _attn_scores(q2n, k_ref, kj, t2, from_here_ref, masked)
                r = run[...]
                e = sums - zn + jnp.concatenate([r, r], axis=1)
                if masked:
                    e = jnp.where(valid, e, -jnp.inf)
                acc[...] += _dot(jnp.exp(e).astype(BF), v_ref[pl.ds(c0, SB_KEYS), :].astype(BF))
                run[...] = r + _all_lanes(sums[:, 0:1], SB_BLOCK)

            step(qi, True)

            def below(n, _):
                step(qi - 1 - n, False)
                return 0

            lax.fori_loop(0, qi, below, 0)
            o_ref[pl.ds(r0, TQ), :] = jnp.where(head0, acc[0:TQ, :], acc[TQ:2 * TQ, :])
            tot_ref[0, pl.ds(r0, TQ), :] = run[0:TQ, :]
            tot_ref[1, pl.ds(r0, TQ), :] = run[TQ:2 * TQ, :]
            return 0

        lax.fori_loop(0, S // TQ, qloop, 0)

    col = lambda base: pl.BlockSpec((S, 128), lambda p, base=base: (0, base + p))
    return _hosted(
        body, job, (proj, proj, proj, from_here), name="attn_fwd",
        grid=(WIDTH // 128,),
        in_specs=[col(0), col(4), col(8), pl.BlockSpec(from_here.shape, lambda p: (0, 0))],
        out_specs=[pl.BlockSpec((S, 128), lambda p: (0, p)), pl.BlockSpec((None, 2, S, 128), lambda p: (p, 0, 0, 0))],
        out_shape=[jax.ShapeDtypeStruct((S, WIDTH), F32), jax.ShapeDtypeStruct((WIDTH // 128, 2, S, 128), F32)],
        scratch_shapes=[pltpu.VMEM((2 * TQ, SB_BLOCK), F32), pltpu.VMEM((2 * TQ, 128), F32)],
        semantics=("parallel",))


def _attn_bwd(proj, d_o, tot, from_here, up_to, job=None):
    S = proj.shape[0]
    TQ = SB_ROWS
    assert S % TQ == 0 and SB_KEYS == TQ
    scale = SB_HEAD_DIM ** -0.5

    def body(q_ref, k_ref, v_ref, do_ref, tot_ref, from_here_ref, up_to_ref, dq_ref, dk_ref, dv_ref, pre, cum, dq_acc, dk_acc, dv_acc):
        head0 = lax.broadcasted_iota(jnp.int32, (1, 2 * SB_HEAD_DIM), 1) < SB_HEAD_DIM
        dk_acc[...] = jnp.zeros_like(dk_acc)
        dv_acc[...] = jnp.zeros_like(dv_acc)

        def qloop(qi, _):
            r0 = pl.multiple_of(qi * TQ, TQ)
            q2n = _attn_rows(q_ref, r0, TQ, head0, -scale)
            do2 = _attn_rows(do_ref, r0, TQ, head0)
            trow = lax.broadcasted_iota(jnp.int32, (TQ, SB_KEYS), 0) + qi * TQ
            t2 = jnp.concatenate([trow, trow], axis=0)
            pre[0:TQ, :] = tot_ref[0, pl.ds(r0, TQ), :]
            pre[TQ:2 * TQ, :] = tot_ref[1, pl.ds(r0, TQ), :]
            cum[...] = jnp.zeros_like(cum)
            dq_acc[...] = jnp.zeros_like(dq_acc)

            def step(kj, masked):
                c0, kb, zn, valid, lsb, sums = _attn_scores(q2n, k_ref, kj, t2, from_here_ref, masked)
                later = pre[...] - _all_lanes(sums[:, 0:1], SB_BLOCK)
                pre[...] = later
                e = sums - zn + jnp.concatenate([later, later], axis=1)
                sig = jnp.exp(lsb - zn)
                if masked:
                    e = jnp.where(valid, e, -jnp.inf)
                    sig = jnp.where(valid, sig, 0.0)
                a = jnp.exp(e)
                w = _dot_nt(do2, v_ref[pl.ds(c0, SB_KEYS), :].astype(BF)) * a
                upto = _sums_r(w, up_to_ref[...])
                c = cum[...]
                dz = w - sig * (upto + jnp.concatenate([c, c], axis=1))
                cum[...] = c + _all_lanes(upto[:, SB_KEYS - 1:SB_KEYS], SB_BLOCK)
                dzb = dz.astype(BF)
                dq_acc[...] += _dot(dzb, kb)
                dk_acc[pl.ds(c0, SB_KEYS), :] += _dot_tn(dzb, q2n)
                dv_acc[pl.ds(c0, SB_KEYS), :] += _dot_tn(a.astype(BF), do2)

            def below(kj, _):
                step(kj, False)
                return 0

            lax.fori_loop(0, qi, below, 0)
            step(qi, True)
            dq_ref[pl.ds(r0, TQ), :] = (jnp.where(head0, dq_acc[0:TQ, :], dq_acc[TQ:2 * TQ, :]) * scale).astype(BF)
            return 0

        lax.fori_loop(0, S // TQ, qloop, 0)
        dk_ref[...] = (-dk_acc[...]).astype(BF)
        dv_ref[...] = dv_acc[...].astype(BF)

    col = lambda base: pl.BlockSpec((S, 128), lambda p, base=base: (0, base + p))
    whole = lambda a: pl.BlockSpec(a.shape, lambda p: (0, 0))
    out = jax.ShapeDtypeStruct((S, WIDTH), BF)
    big = pltpu.VMEM((2 * TQ, SB_BLOCK), F32)
    return _hosted(
        body, job, (proj, proj, proj, d_o, tot, from_here, up_to), name="attn_bwd",
        grid=(WIDTH // 128,),
        in_specs=[col(0), col(4), col(8), col(0), pl.BlockSpec((None, 2, S, 128), lambda p: (p, 0, 0, 0)), whole(from_here), whole(up_to)],
        out_specs=[col(0), col(0), col(0)],
        out_shape=[out, out, out],
        scratch_shapes=[big, big, big, pltpu.VMEM((S, 128), F32), pltpu.VMEM((S, 128), F32)],
        semantics=("parallel",))


HG_LEVELS = (32, 16, 8, 4, 2, 1)


def _hgrn_consts():
    C = HG_CHUNK
    t = np.arange(C)[:, None]
    s = np.arange(C)[None, :]
    rows = [(s <= t), (s > t)]
    masks = [(t == s)]
    for m in HG_LEVELS:
        two = 2 * m
        mid = (t // two) * two + m
        right = (t % two) >= m
        rows.append((right & (s >= mid) & (s <= t)) | ((~right) & (s > t) & (s <= mid - 1)))
        masks.append(((t // two) == (s // two)) & right & ((s % two) < m))
    tri = np.concatenate(rows, axis=0).astype(np.float32)
    return (jnp.asarray(tri, BF), jnp.asarray(tri.T.copy(), BF), jnp.asarray(np.stack(masks).astype(np.float32), F32))


HG_SUM_BLOCKS = 2 + len(HG_LEVELS)
HG_SUM_TERMS = 2


@jax.custom_vjp
def _hgrn_sums(tri, tri_t, g):
    C = HG_CHUNK
    e = _split_dot(tri, g, HG_SUM_TERMS)
    blocks = tuple(e[n * C:(n + 1) * C] for n in range(HG_SUM_BLOCKS))
    return blocks + (jnp.broadcast_to(e[C - 1:C], (HG_DIM, e.shape[1])),)


def _hgrn_sums_fwd(tri, tri_t, g):
    return _hgrn_sums(tri, tri_t, g), (tri, tri_t)


def _hgrn_sums_bwd(res, ds):
    tri, tri_t = res
    C = HG_CHUNK
    last = lax.broadcasted_iota(jnp.int32, (C, 1), 0) == C - 1
    prefix = ds[0] + jnp.where(last, jnp.sum(ds[-1], axis=0, keepdims=True), 0.0)
    d = jnp.concatenate((prefix,) + tuple(ds[1:-1]), axis=0)
    return jnp.zeros_like(tri), jnp.zeros_like(tri_t), _split_dot(tri_t, d, HG_SUM_TERMS)


_hgrn_sums.defvjp(_hgrn_sums_fwd, _hgrn_sums_bwd)


def _bf_dot(a, b):
    return _dot(a.astype(BF), b.astype(BF))


def _bf_dot_nt(a, b):
    return _dot_nt(a.astype(BF), b.astype(BF))


def _bf_dot_tn(a, b):
    return _dot_tn(a.astype(BF), b.astype(BF))


@jax.custom_vjp
def _mm(a, b):
    return _bf_dot(a, b)


_mm.defvjp(lambda a, b: (_bf_dot(a, b), (a, b)), lambda r, ct: (_bf_dot_nt(ct, r[1]), _bf_dot_tn(r[0], ct)))


@jax.custom_vjp
def _mm_nt(a, b):
    return _bf_dot_nt(a, b)


_mm_nt.defvjp(lambda a, b: (_bf_dot_nt(a, b), (a, b)), lambda r, ct: (_bf_dot(ct, r[1]), _bf_dot_tn(ct, r[0])))


@jax.custom_vjp
def _mm_tn(a, b):
    return _bf_dot_tn(a, b)


_mm_tn.defvjp(lambda a, b: (_bf_dot_tn(a, b), (a, b)), lambda r, ct: (_bf_dot_nt(r[1], ct), _bf_dot(r[0], ct)))


def _hgrn_chunk(tri, tri_t, masks, qraw, fpre, v, st, lb):
    q = _silu(qraw)
    f = lb + (1.0 - lb) * _sigmoid(fpre)
    k = 1.0 - f
    e = _hgrn_sums(tri, tri_t, jnp.log(f))
    prefix, suffix, whole = e[0], e[1], e[-1]
    scores = masks[0] * _mm_nt(q, k)
    for n in range(len(HG_LEVELS)):
        decay = jnp.exp(e[2 + n])
        scores = scores + masks[n + 1] * _mm_nt(q * decay, k * decay)
    o = _mm_nt(q * jnp.exp(prefix), st) + _mm(scores, v)
    st_new = st * jnp.exp(whole) + _mm_tn(v, k * jnp.exp(suffix))
    return o, st_new


HG_HEADS_PER_STEP = 2
HG_LANES = HG_HEADS_PER_STEP * HG_DIM


def _hgrn_specs(S, consts):
    col = lambda base: pl.BlockSpec((S, HG_LANES), lambda p, base=base: (0, base // HG_HEADS_PER_STEP + p))
    whole = [pl.BlockSpec(a.shape, lambda p, n=a.ndim: (0,) * n) for a in consts]
    return col, whole


def _hgrn_fwd(proj, lbs, consts):
    S = proj.shape[0]
    nc = S // HG_CHUNK
    heads = range(HG_HEADS_PER_STEP)

    def body(q_ref, f_ref, i_ref, lb_ref, tri_ref, trit_ref, mask_ref, o_ref, st_ref):
        tri, tri_t = tri_ref[...], trit_ref[...]
        masks = [mask_ref[n] for n in range(len(HG_LEVELS) + 1)]

        def chunk(ci, sts):
            r0 = pl.multiple_of(ci * HG_CHUNK, HG_CHUNK)
            rows = pl.ds(r0, HG_CHUNK)
            new = []
            for hd in heads:
                lanes = pl.ds(hd * HG_DIM, HG_DIM)
                st_ref[hd, ci] = sts[hd]
                o, st_new = _hgrn_chunk(tri, tri_t, masks, q_ref[rows, lanes], f_ref[rows, lanes], i_ref[rows, lanes], sts[hd], lb_ref[hd])
                o_ref[rows, lanes] = o
                new.append(st_new)
            return tuple(new)

        lax.fori_loop(0, nc, chunk, tuple(jnp.zeros((HG_DIM, HG_DIM), F32) for _ in heads))

    col, whole = _hgrn_specs(S, consts)
    return pl.pallas_call(
        body, name="hgrn_fwd",
        grid=(WIDTH // HG_LANES,),
        in_specs=[col(16), col(20), col(24), pl.BlockSpec((HG_HEADS_PER_STEP, 1, 128), lambda p: (p, 0, 0))] + whole,
        out_specs=[col(0), pl.BlockSpec((HG_HEADS_PER_STEP, nc, HG_DIM, HG_DIM), lambda p: (p, 0, 0, 0))],
        out_shape=[jax.ShapeDtypeStruct((S, WIDTH), F32), jax.ShapeDtypeStruct((WIDTH // 128, nc, HG_DIM, HG_DIM), F32)],
        compiler_params=_params(dimension_semantics=("parallel",)),
    )(proj, proj, proj, lbs, *consts)


def _hgrn_bwd(proj, lbs, states, d_o, consts, job=None):
    S = proj.shape[0]
    nc = S // HG_CHUNK

    def body(q_ref, f_ref, i_ref, lb_ref, st_ref, do_ref, tri_ref, trit_ref, mask_ref, dq_ref, df_ref, di_ref, dlb_ref):
        masks = [mask_ref[n] for n in range(len(HG_LEVELS) + 1)]
        fn = functools.partial(_hgrn_chunk, tri_ref[...], trit_ref[...], masks)
        heads = range(HG_HEADS_PER_STEP)

        def chunk(n, carry):
            ci = nc - 1 - n
            r0 = pl.multiple_of(ci * HG_CHUNK, HG_CHUNK)
            rows = pl.ds(r0, HG_CHUNK)
            new = []
            for hd in heads:
                d_st, dlb = carry[hd]
                lanes = pl.ds(hd * HG_DIM, HG_DIM)
                _, pull = jax.vjp(fn, q_ref[rows, lanes], f_ref[rows, lanes], i_ref[rows, lanes], st_ref[hd, ci], lb_ref[hd])
                dq, df, di, d_prev, dl = pull((do_ref[rows, lanes], d_st))
                dq_ref[rows, lanes] = dq.astype(BF)
                df_ref[rows, lanes] = df.astype(BF)
                di_ref[rows, lanes] = di.astype(BF)
                new.append((d_prev, dlb + dl))
            return tuple(new)

        zero = (jnp.zeros((HG_DIM, HG_DIM), F32), jnp.zeros((1, HG_DIM), F32))
        done = lax.fori_loop(0, nc, chunk, tuple(zero for _ in heads))
        for hd in heads:
            dlb_ref[hd] = done[hd][1]

    col, whole = _hgrn_specs(S, consts)
    head = pl.BlockSpec((HG_HEADS_PER_STEP, 1, 128), lambda p: (p, 0, 0))
    out = jax.ShapeDtypeStruct((S, WIDTH), BF)
    return _hosted(
        body, job, (proj, proj, proj, lbs, states, d_o, *consts), name="hgrn_bwd",
        grid=(WIDTH // HG_LANES,),
        in_specs=[col(16), col(20), col(24), head, pl.BlockSpec((HG_HEADS_PER_STEP, nc, HG_DIM, HG_DIM), lambda p: (p, 0, 0, 0)), col(0)] + whole,
        out_specs=[col(0), col(0), col(0), head],
        out_shape=[out, out, out, jax.ShapeDtypeStruct((WIDTH // 128, 1, 128), F32)],
        semantics=("parallel",))


def _shift_down(x, n):
    rows = lax.broadcasted_iota(jnp.int32, x.shape, 0)
    return jnp.where(rows >= n, pltpu.roll(x, n, 0), 0.0)


def _shift_up(x, n):
    S = x.shape[0]
    rows = lax.broadcasted_iota(jnp.int32, x.shape, 0)
    return jnp.where(rows < S - n, pltpu.roll(x, S - n, 0), 0.0)


def _branch_fwd(proj, o_a, o_b, norm_w, conv_w, layer):
    S = proj.shape[0]

    def body(oa_ref, za_ref, ob_ref, zb_ref, nw_ref, pre_ref, post_ref, u_ref, zc_ref, cw_ref, ya_ref, yb_ref, yc_ref):
        ya_ref[...] = (oa_ref[...] * _silu(za_ref[...])).astype(BF)
        ob = ob_ref[...]
        rn = lax.rsqrt(jnp.mean(ob * ob, axis=-1, keepdims=True) + RMS_EPS)
        yb_ref[...] = (ob * rn * nw_ref[layer:layer + 1, :] * _silu(zb_ref[...])).astype(BF)
        pu = pre_ref[...] * u_ref[...]
        conv = cw_ref[2:3, :] * pu + cw_ref[1:2, :] * _shift_down(pu, 1) + cw_ref[0:1, :] * _shift_down(pu, 2)
        yc_ref[...] = (post_ref[...] * conv * _silu(zc_ref[...])).astype(BF)

    col = lambda base: pl.BlockSpec((S, 128), lambda p, base=base: (0, base + p))
    out = jax.ShapeDtypeStruct((S, WIDTH), BF)
    return pl.pallas_call(
        body, name="branch_fwd",
        grid=(WIDTH // 128,),
        in_specs=[col(0), col(12), col(0), col(28), pl.BlockSpec(norm_w.shape, lambda p: (0, 0)),
                  col(32), col(36), col(40), col(44), pl.BlockSpec((None, None, 3, 128), lambda p: (p, layer, 0, 0))],
        out_specs=[col(0), col(0), col(0)],
        out_shape=[out, out, out],
        compiler_params=_params(dimension_semantics=("parallel",)),
    )(o_a, proj, o_b, proj, norm_w, proj, proj, proj, proj, conv_w)


def _branch_bwd(proj, o_a, o_b, norm_w, conv_w, dy_a, dy_b, dy_c, layer):
    S = proj.shape[0]

    def dsilu(z):
        s = _sigmoid(z)
        return s * z, s * (1.0 + z * (1.0 - s))

    def body(oa_ref, za_ref, ob_ref, zb_ref, nw_ref, pre_ref, post_ref, u_ref, zc_ref, cw_ref, dya_ref, dyb_ref, dyc_ref,
             doa_ref, dob_ref, dza_ref, dzb_ref, dpre_ref, dpost_ref, du_ref, dzc_ref, dnw_ref, dcw_ref):
        dya = dya_ref[...]
        sa, dsa = dsilu(za_ref[...])
        doa_ref[...] = dya * sa
        dza_ref[...] = (dya * oa_ref[...] * dsa).astype(BF)

        dyb = dyb_ref[...]
        ob = ob_ref[...]
        nw = nw_ref[layer:layer + 1, :]
        sb, dsb = dsilu(zb_ref[...])
        rn = lax.rsqrt(jnp.mean(ob * ob, axis=-1, keepdims=True) + RMS_EPS)
        on = ob * rn
        dzb_ref[...] = (dyb * on * nw * dsb).astype(BF)
        don_w = dyb * sb
        dnw_ref[...] = jnp.sum(don_w * on, axis=0, keepdims=True)
        don = don_w * nw
        dob_ref[...] = rn * (don - on * jnp.mean(don * on, axis=-1, keepdims=True))

        dyc = dyc_ref[...]
        pre, post, u = pre_ref[...], post_ref[...], u_ref[...]
        sc, dsc = dsilu(zc_ref[...])
        pu = pre * u
        pu1, pu2 = _shift_down(pu, 1), _shift_down(pu, 2)
        conv = cw_ref[2:3, :] * pu + cw_ref[1:2, :] * pu1 + cw_ref[0:1, :] * pu2
        dzc_ref[...] = (dyc * post * conv * dsc).astype(BF)
        dpost_ref[...] = (dyc * conv * sc).astype(BF)
        dconv = dyc * post * sc
        dcw_ref[0:1, :] = jnp.sum(dconv * pu2, axis=0, keepdims=True)
        dcw_ref[1:2, :] = jnp.sum(dconv * pu1, axis=0, keepdims=True)
        dcw_ref[2:3, :] = jnp.sum(dconv * pu, axis=0, keepdims=True)
        dpu = cw_ref[2:3, :] * dconv + cw_ref[1:2, :] * _shift_up(dconv, 1) + cw_ref[0:1, :] * _shift_up(dconv, 2)
        dpre_ref[...] = (dpu * u).astype(BF)
        du_ref[...] = (dpu * pre).astype(BF)

    col = lambda base: pl.BlockSpec((S, 128), lambda p, base=base: (0, base + p))
    f32 = jax.ShapeDtypeStruct((S, WIDTH), F32)
    bf = jax.ShapeDtypeStruct((S, WIDTH), BF)
    return pl.pallas_call(
        body, name="branch_bwd",
        grid=(WIDTH // 128,),
        in_specs=[col(0), col(12), col(0), col(28), pl.BlockSpec(norm_w.shape, lambda p: (0, 0)),
                  col(32), col(36), col(40), col(44), pl.BlockSpec((None, None, 3, 128), lambda p: (p, layer, 0, 0)),
                  col(0), col(0), col(0)],
        out_specs=[col(0)] * 8 + [pl.BlockSpec((None, 1, 128), lambda p: (p, 0, 0)), pl.BlockSpec((None, 3, 128), lambda p: (p, 0, 0))],
        out_shape=[f32, f32, bf, bf, bf, bf, bf, bf, jax.ShapeDtypeStruct((WIDTH // 128, 1, 128), F32),
                   jax.ShapeDtypeStruct((WIDTH // 128, 3, 128), F32)],
        compiler_params=_params(dimension_semantics=("parallel",)),
    )(o_a, proj, o_b, proj, norm_w, proj, proj, proj, proj, conv_w, dy_a, dy_b, dy_c)


def _branch_proj(y_refs, wb_ref):
    out = []
    for i in range(3):
        yv = y_refs[i][...]
        out.append(jnp.concatenate([_dot(yv, wb_ref[j, i]) for j in range(N_CHIPS)], axis=1))
    return out


def _merge_fwd(x, mod, proj, ys, wb, wo, ln_g, ln_b, layer, tm=256):
    S, D = x.shape
    tm = min(tm, S)

    def body(x_ref, mod_ref, ga_ref, gb_ref, gc_ref, ya_ref, yb_ref, yc_ref, wb_ref, wo_ref, g_ref, b_ref, xo_ref, mg_ref, y_ref):
        ps = _branch_proj((ya_ref, yb_ref, yc_ref), wb_ref)
        merged = _sigmoid(ga_ref[...]) * ps[0] + _sigmoid(gb_ref[...]) * ps[1] + _sigmoid(gc_ref[...]) * ps[2]
        mb = merged.astype(BF)
        mg_ref[...] = mb
        y = _dot(mb, wo_ref[...].reshape(D, D))
        y_ref[...] = y
        r = ALPHA * x_ref[...] + (1.0 + mod_ref[:, 2 * D:3 * D]) * y
        xn, _ = _standardize(r)
        xo_ref[...] = xn * g_ref[layer:layer + 1, :] + b_ref[layer:layer + 1, :]

    row = lambda w, c=0: pl.BlockSpec((tm, w), lambda i, c=c: (i, c))
    whole = lambda a: pl.BlockSpec(a.shape, lambda i, n=a.ndim: (0,) * n)
    return pl.pallas_call(
        body, name="merge_fwd",
        grid=(S // tm,),
        in_specs=[row(D), whole(mod), row(D, 6), row(D, 7), row(D, 8), row(WIDTH), row(WIDTH), row(WIDTH), whole(wb),
                  whole(wo), whole(ln_g), whole(ln_b)],
        out_specs=[row(D), row(D), row(D)],
        out_shape=[jax.ShapeDtypeStruct((S, D), F32), jax.ShapeDtypeStruct((S, D), BF), jax.ShapeDtypeStruct((S, D), F32)],
        compiler_params=_params(dimension_semantics=("parallel",)),
    )(x, mod, proj, proj, proj, *ys, wb, wo, ln_g, ln_b)


def _merge_bwd(dxo, x, y, mod, proj, ys, wb, wo, ln_g, layer, tm=256):
    S, D = x.shape
    tm = min(tm, S)

    def body(dxo_ref, x_ref, y_ref, mod_ref, ga_ref, gb_ref, gc_ref, ya_ref, yb_ref, yc_ref, wb_ref, wo_ref, g_ref,
             dxr_ref, dy_ref, dp_ref, dg_ref, dya_ref, dyb_ref, dyc_ref, dlg_ref, dlb_ref, dgt_ref):
        @pl.when(pl.program_id(0) == 0)
        def _():
            dlg_ref[...] = jnp.zeros_like(dlg_ref)
            dlb_ref[...] = jnp.zeros_like(dlb_ref)
            dgt_ref[...] = jnp.zeros_like(dgt_ref)

        gate1 = 1.0 + mod_ref[:, 2 * D:3 * D]
        yv = y_ref[...]
        xn, rstd = _standardize(ALPHA * x_ref[...] + gate1 * yv)
        dxo = dxo_ref[...]
        dlg_ref[...] += jnp.sum(dxo * xn, axis=0, keepdims=True)
        dlb_ref[...] += jnp.sum(dxo, axis=0, keepdims=True)
        dr = _standardize_bwd(dxo * g_ref[layer:layer + 1, :], xn, rstd)
        dxr_ref[...] = ALPHA * dr
        dgt_ref[...] += jnp.sum(dr * yv, axis=0, keepdims=True)
        dyb = (gate1 * dr).astype(BF)
        dy_ref[...] = dyb
        dmerged = _dot_nt(dyb, wo_ref[...].reshape(D, D))
        ps = _branch_proj((ya_ref, yb_ref, yc_ref), wb_ref)
        quarter = D // N_CHIPS
        for i, (gate_ref, out_ref) in enumerate(((ga_ref, dya_ref), (gb_ref, dyb_ref), (gc_ref, dyc_ref))):
            sg = _sigmoid(gate_ref[...])
            dg_ref[:, i * D:(i + 1) * D] = (dmerged * ps[i] * sg * (1.0 - sg)).astype(BF)
            dp = (dmerged * sg).astype(BF)
            dp_ref[:, i * D:(i + 1) * D] = dp
            acc = None
            for j in range(N_CHIPS):
                t = _dot_nt(dp[:, j * quarter:(j + 1) * quarter], wb_ref[j, i])
                acc = t if acc is None else acc + t
            out_ref[...] = acc

    row = lambda w, c=0: pl.BlockSpec((tm, w), lambda i, c=c: (i, c))
    whole = lambda a: pl.BlockSpec(a.shape, lambda i, n=a.ndim: (0,) * n)
    vec = pl.BlockSpec((1, D), lambda i: (0, 0))
    sd = jax.ShapeDtypeStruct
    return pl.pallas_call(
        body, name="merge_bwd",
        grid=(S // tm,),
        in_specs=[row(D), row(D), row(D), whole(mod), row(D, 6), row(D, 7), row(D, 8), row(WIDTH), row(WIDTH), row(WIDTH), whole(wb),
                  whole(wo), whole(ln_g)],
        out_specs=[row(D), row(D), row(3 * D), row(3 * D), row(WIDTH), row(WIDTH), row(WIDTH), vec, vec, vec],
        out_shape=[sd((S, D), F32), sd((S, D), BF), sd((S, 3 * D), BF), sd((S, 3 * D), BF), sd((S, WIDTH), F32), sd((S, WIDTH), F32),
                   sd((S, WIDTH), F32), sd((1, D), F32), sd((1, D), F32), sd((1, D), F32)],
        compiler_params=_params(dimension_semantics=("arbitrary",)),
    )(dxo, x, y, mod, proj, proj, proj, *ys, wb, wo, ln_g)


def _loss_head(x, target, tm=512):
    S, D = x.shape
    tm = min(tm, S)

    def body(x_ref, t_ref, dx_ref, loss_ref):
        @pl.when(pl.program_id(0) == 0)
        def _():
            loss_ref[...] = jnp.zeros_like(loss_ref)

        err = x_ref[...] - t_ref[...]
        dx_ref[...] = err * (1.0 / D)
        loss_ref[...] += 0.5 * jnp.sum(jnp.mean(err * err, axis=-1, keepdims=True))

    row = pl.BlockSpec((tm, D), lambda i: (i, 0))
    return pl.pallas_call(
        body, name="loss_head",
        grid=(S // tm,),
        in_specs=[row, row],
        out_specs=[row, pl.BlockSpec((8, 128), lambda i: (0, 0))],
        out_shape=[jax.ShapeDtypeStruct((S, D), F32), jax.ShapeDtypeStruct((8, 128), F32)],
        compiler_params=_params(dimension_semantics=("arbitrary",)),
    )(x, target)


def _proj_bwd(dproj, wg, x, mod, dx_res, job=None, tm=512, tk=768):
    S, D = x.shape
    tm = min(tm, S)
    shard = wg.shape[-1]
    per = shard // tk
    nk = IN_COLS // tk

    def body(dp_ref, w_ref, x_ref, mod_ref, dxr_ref, dx_ref, dsh_ref, dsc_ref, acc):
        i, k = pl.program_id(0), pl.program_id(1)

        @pl.when((i == 0) & (k == 0))
        def _():
            dsh_ref[...] = jnp.zeros_like(dsh_ref)
            dsc_ref[...] = jnp.zeros_like(dsc_ref)

        @pl.when(k == 0)
        def _():
            acc[...] = jnp.zeros_like(acc)

        acc[...] += _dot_nt(dp_ref[...], w_ref[...])

        @pl.when(k == nk - 1)
        def _():
            dh = acc[...]
            xs, rstd = _standardize(x_ref[...])
            dsh_ref[...] += jnp.sum(dh, axis=0, keepdims=True)
            dsc_ref[...] += jnp.sum(dh * xs, axis=0, keepdims=True)
            dx_ref[...] = _standardize_bwd(dh * (1.0 + mod_ref[:, D:2 * D]), xs, rstd) + dxr_ref[...]

    row = pl.BlockSpec((tm, D), lambda i, k: (i, 0))
    vec = pl.BlockSpec((1, D), lambda i, k: (0, 0))
    return _hosted(
        body, job, (dproj, wg, x, mod, dx_res), name="proj_bwd",
        grid=(S // tm, nk),
        in_specs=[pl.BlockSpec((tm, tk), lambda i, k: (i, k)),
                  pl.BlockSpec((None, D, tk), lambda i, k: (k // per, 0, k % per)),
                  row, pl.BlockSpec((1, 3 * D), lambda i, k: (0, 0)), row],
        out_specs=[row, vec, vec],
        out_shape=[jax.ShapeDtypeStruct((S, D), F32), jax.ShapeDtypeStruct((1, D), F32), jax.ShapeDtypeStruct((1, D), F32)],
        scratch_shapes=[pltpu.VMEM((tm, D), F32)],
        semantics=("arbitrary", "arbitrary"))


def _grad_w(a, b, mid, prev, name, tm, tn, b_col0=0):
    S, M = a.shape
    shape = prev.shape
    n_shard = shape[-1]
    per = n_shard // tn
    nm = M // tm
    lead = len(mid)

    def body(*refs):
        a_ref, b_ref, o_ref = refs[0], refs[1], refs[-1]
        o_ref[...] = _dot_tn(a_ref[...], b_ref[...]).astype(BF)

    in_specs = [pl.BlockSpec((S, tm), lambda m, n: (0, m)),
                pl.BlockSpec((S, tn), lambda m, n: (0, b_col0 // tn + n))]
    args = [a, b]
    aliases = {}
    if not isinstance(prev, jax.ShapeDtypeStruct):
        in_specs.append(pl.BlockSpec(memory_space=pl.ANY))
        args.append(prev)
        aliases = {2: 0}
    return pl.pallas_call(
        body, name=name,
        grid=(nm, N_CHIPS * per),
        in_specs=in_specs,
        out_specs=pl.BlockSpec((None,) + (None,) * lead + (tm, tn), lambda m, n: (n // per,) + tuple(mid) + (m, n % per)),
        out_shape=jax.ShapeDtypeStruct(shape, BF),
        input_output_aliases=aliases,
        compiler_params=_params(dimension_semantics=("parallel", "parallel")),
    )(*args)


def _all_gather8(x, name):
    R, N = x.shape

    def body(x_ref, out_ref, send_sems, recv_sems):
        mx, my, mc = lax.axis_index("x"), lax.axis_index("y"), lax.axis_index("c")
        me = 4 * mx + 2 * my + mc
        out_ref[me] = x_ref[...]
        copies = []
        for k in range(1, N_DEV):
            peer = (_flip(mx, k & 4), _flip(my, k & 2), _flip(mc, k & 1))
            cp = pltpu.make_async_remote_copy(src_ref=x_ref, dst_ref=out_ref.at[me], send_sem=send_sems.at[k - 1],
                                              recv_sem=recv_sems.at[k - 1], device_id=peer, device_id_type=MESH)
            cp.start()
            copies.append(cp)
        for cp in copies:
            cp.wait()

    return pl.pallas_call(
        body, name=name,
        in_specs=[pl.BlockSpec(memory_space=pltpu.VMEM)],
        out_specs=pl.BlockSpec(memory_space=pltpu.VMEM),
        out_shape=jax.ShapeDtypeStruct((N_DEV, R, N), F32),
        scratch_shapes=[pltpu.SemaphoreType.DMA((N_DEV - 1,)), pltpu.SemaphoreType.DMA((N_DEV - 1,))],
        compiler_params=_params(),
    )(x)


def _rows2d(a):
    return a.reshape(-1, a.shape[-1])


def _tile_rows(rows, cols, n_arrays):
    budget = (24 << 20) // (n_arrays * 2 * 4 * cols)
    if rows <= budget:
        return rows
    tm = 8
    for cand in range(8, budget + 1, 8):
        if rows % cand == 0:
            tm = cand
    return tm


SUM_ROWS = 256


def _sum_cores(g, sent, where):
    chips, lead, _, r, cols = g.shape
    tr = min(r, SUM_ROWS)

    def body(where_ref, g_ref, s_ref, out_ref):
        out_ref[...] = (g_ref[...].astype(F32) + s_ref[...].astype(F32)).astype(BF)

    spec = pl.BlockSpec((None, tr, cols), lambda i, j, where_ref: (i, j, 0))
    out = pl.pallas_call(
        body, name="sum_cores",
        grid_spec=pltpu.PrefetchScalarGridSpec(
            num_scalar_prefetch=1, grid=(chips * lead, r // tr),
            in_specs=[pl.BlockSpec((None, None, tr, cols), lambda i, j, where_ref: (i, where_ref[1], j, 0)), spec],
            out_specs=spec),
        out_shape=jax.ShapeDtypeStruct((chips * lead, r, cols), BF),
        compiler_params=_params(dimension_semantics=("parallel", "parallel")),
    )(where, g.reshape(chips * lead, 2, r, cols), sent.reshape(chips * lead, r, cols))
    return out.reshape(chips, lead, r, cols)


def _sum_chips(sums, got, where):
    _, lead, r, cols = sums.shape
    tr = min(r, SUM_ROWS)

    def body(where_ref, s_ref, g_ref, out_ref):
        out_ref[...] = ((s_ref[...].astype(F32) + g_ref[0].astype(F32)) + g_ref[1].astype(F32)) + g_ref[2].astype(F32)

    return pl.pallas_call(
        body, name="sum_chips",
        grid_spec=pltpu.PrefetchScalarGridSpec(
            num_scalar_prefetch=1, grid=(lead, r // tr),
            in_specs=[pl.BlockSpec((None, None, tr, cols), lambda i, j, where_ref: (where_ref[0], i, j, 0)),
                      pl.BlockSpec((N_CHIPS - 1, None, tr, cols), lambda i, j, where_ref: (0, i, j, 0))],
            out_specs=pl.BlockSpec((None, None, tr, cols), lambda i, j, where_ref: (i, where_ref[1], j, 0))),
        out_shape=jax.ShapeDtypeStruct((lead, 2, r, cols), F32),
        compiler_params=_params(dimension_semantics=("parallel", "parallel")),
    )(where, sums, got)


def _adamw(w, m, v, groups, name):
    shape = w.shape
    w2, m2, v2 = _rows2d(w), _rows2d(m), _rows2d(v)
    rows, cols = w2.shape
    ng = len(groups)
    n = len(groups[0])
    slab = rows // ng
    gs = [_rows2d(g) for grp in groups for g in grp]
    tm = _tile_rows(slab, cols, 7 + n)
    tiles = slab // tm
    c1 = 1.0 / (1.0 - ADAM_B1 ** ADAM_STEP)
    c2 = 1.0 / (1.0 - ADAM_B2 ** ADAM_STEP)

    def body(*refs):
        w_ref, m_ref, v_ref = refs[:3]
        g_refs = refs[3:3 + ng * n]
        go_ref, d_ref, mo_ref, vo_ref = refs[3 + ng * n:]
        which = pl.program_id(0)
        for s in range(ng):
            @pl.when(which == s)
            def _(s=s):
                g = g_refs[s * n][...]
                for r in g_refs[s * n + 1:(s + 1) * n]:
                    g = g + r[...]
                mn = ADAM_B1 * m_ref[...] + (1.0 - ADAM_B1) * g
                vn = ADAM_B2 * v_ref[...] + (1.0 - ADAM_B2) * (g * g)
                go_ref[...] = g
                mo_ref[...] = mn
                vo_ref[...] = vn
                d_ref[...] = -ADAM_LR * ((mn * c1) / (jnp.sqrt(vn * c2) + ADAM_EPS) + ADAM_WD * w_ref[...])

    spec = pl.BlockSpec((tm, cols), lambda s, i: (s * tiles + i, 0))
    g_specs = [pl.BlockSpec((tm, cols), lambda s, i, k=k: (jnp.where(s == k, i, jnp.where(s < k, 0, tiles - 1)), 0))
               for k in range(ng) for _ in range(n)]
    outs = pl.pallas_call(
        body, name=name,
        grid=(ng, tiles),
        in_specs=[spec] * 3 + g_specs,
        out_specs=[spec] * 4,
        out_shape=[jax.ShapeDtypeStruct((rows, cols), F32)] * 4,
        compiler_params=_params(dimension_semantics=("arbitrary", "arbitrary")),
    )(w2, m2, v2, *gs)
    return [o.reshape(shape) for o in outs]


def _lower_bounds(r0, r1):
    top = jnp.maximum(r0, r1)
    e0, e1 = jnp.exp(r0 - top), jnp.exp(r1 - top)
    p0, p1 = e0 / (e0 + e1), e1 / (e0 + e1)
    return p0 - p0, (p0 + p1) - p0


def _lbs_fwd(lb_raw):
    def body(lb_ref, out_ref):
        l0, l1 = _lower_bounds(lb_ref[0:1, :], lb_ref[1:2, :])
        out_ref[0:1, :] = l0
        out_ref[1:2, :] = l1

    return pl.pallas_call(body, name="lower_bounds", out_shape=jax.ShapeDtypeStruct(lb_raw.shape, F32), compiler_params=_params())(lb_raw)


def _mod_rows(c_all, w_mod, tn=768):
    _, D, cols = w_mod.shape

    def body(c_ref, w_ref, out_ref):
        out_ref[...] = _dot(c_ref[...].astype(BF), w_ref[...].astype(BF))

    return pl.pallas_call(
        body, name="mod_rows",
        grid=(DEPTH,),
        in_specs=[pl.BlockSpec((N_DEV, D), lambda l: (0, 0)), pl.BlockSpec((None, D, cols), lambda l: (l, 0, 0))],
        out_specs=pl.BlockSpec((N_DEV, cols), lambda l: (0, l)),
        out_shape=jax.ShapeDtypeStruct((N_DEV, DEPTH * cols), F32),
        compiler_params=_params(dimension_semantics=("parallel",)),
    )(c_all, w_mod)


def _grad_w_mod(c_all, dmod_cols):
    D = c_all.shape[1]
    cols = dmod_cols.shape[-1]

    def body(c_ref, d_ref, out_ref):
        out_ref[...] = _dot_tn(c_ref[...].astype(BF), d_ref[...].astype(BF))

    return pl.pallas_call(
        body, name="grad_w_mod",
        grid=(DEPTH,),
        in_specs=[pl.BlockSpec((N_DEV, D), lambda l: (0, 0)), pl.BlockSpec((None, N_DEV, cols), lambda l: (l, 0, 0))],
        out_specs=pl.BlockSpec((None, D, cols), lambda l: (l, 0, 0)),
        out_shape=jax.ShapeDtypeStruct((DEPTH, D, cols), F32),
        compiler_params=_params(dimension_semantics=("parallel",)),
    )(c_all, dmod_cols)


def _sum_devices(parts):
    _, R, N = parts.shape

    def body(p_ref, out_ref):
        acc = p_ref[0]
        for d in range(1, N_DEV):
            acc = acc + p_ref[d]
        out_ref[...] = acc

    return pl.pallas_call(body, name="sum_devices", out_shape=jax.ShapeDtypeStruct((R, N), F32), compiler_params=_params())(parts)


def _lbs_bwd(lb_raw, dl):
    def body(lb_ref, dl_ref, out_ref):
        _, pull = jax.vjp(_lower_bounds, lb_ref[0:1, :], lb_ref[1:2, :])
        d0, d1 = pull((dl_ref[0:1, :], dl_ref[1:2, :]))
        out_ref[0:1, :] = d0
        out_ref[1:2, :] = d1

    return pl.pallas_call(body, name="lower_bounds_bwd", out_shape=jax.ShapeDtypeStruct(lb_raw.shape, F32), compiler_params=_params())(lb_raw, dl)


def kernel(x, c, w_mod, b_mod, w_in, conv_w, hgrn_norm_w, lower_bounds, w_branch, w_out, ln_g, ln_b, loss_target, m_w_mod, m_b_mod, m_w_in, m_conv_w, m_hgrn_norm_w, m_lower_bounds, m_w_branch, m_w_out, m_ln_g, m_ln_b, v_w_mod, v_b_mod, v_w_in, v_conv_w, v_hgrn_norm_w, v_lower_bounds, v_w_branch, v_w_out, v_ln_g, v_ln_b):
    D = D_MODEL
    x0 = x[0]
    target = loss_target[0]
    S = x0.shape[0]
    mx, my, mc = lax.axis_index("x"), lax.axis_index("y"), lax.axis_index("c")
    chip = 2 * mx + my
    me = 2 * chip + mc
    mod_cols = 3 * D // N_CHIPS

    plan = _Plan(w_in.astype(BF), w_branch.astype(BF), w_out.astype(BF), chip, mc)
    plan.first()

    n_conv = DEPTH * 3 * (WIDTH // N_CHIPS)
    first = jnp.concatenate([c, conv_w.reshape(1, n_conv), jnp.zeros((1, 2 * D - D - n_conv), F32)], axis=1)
    first = _all_gather8(first.reshape(8, 2 * D // 8), "gather_c").reshape(N_DEV, 2 * D)
    c_all = first[:, :D]
    conv_all = first[:, D:D + n_conv].reshape(N_DEV, DEPTH, 3, WIDTH // N_CHIPS)[0::2]
    mod_part = _all_gather8(_mod_rows(c_all, w_mod), "gather_mod")[0::2]
    mod_part = lax.dynamic_index_in_dim(mod_part, me, axis=1, keepdims=False).reshape(N_CHIPS, DEPTH, mod_cols)
    mods = [(mod_part[:, l].reshape(1, 3 * D) + b_mod[l][None, :]) for l in range(DEPTH)]
    lbs = _lbs_fwd(lower_bounds).reshape(DEPTH, WIDTH // 128, 1, 128)
    loss_blk, dx, small = _local_step(x0, target, mods, lbs, conv_all, hgrn_norm_w, ln_g, ln_b, plan)

    n_mod, n_nw, n_lb, n_ln, n_cw = DEPTH * 3 * D, DEPTH * 128, DEPTH * WIDTH, DEPTH * D, DEPTH * 3 * WIDTH
    row = jnp.concatenate(
        [jnp.concatenate([small[l][0], small[l][1], small[l][2]], axis=1) for l in range(DEPTH)]
        + [jnp.sum(small[l][3], axis=0) for l in range(DEPTH)]
        + [small[l][4].reshape(1, WIDTH) for l in range(DEPTH)]
        + [small[l][5] for l in range(DEPTH)] + [small[l][6] for l in range(DEPTH)]
        + [jnp.transpose(small[l][7], (1, 0, 2)).reshape(1, 3 * WIDTH) for l in range(DEPTH)]
        + [loss_blk[0:1, :]], axis=1)
    n_row = row.shape[1]
    fold = -(-n_row // (8 * 128)) * 128
    rows = jnp.concatenate([row, jnp.zeros((1, 8 * fold - n_row), F32)], axis=1).reshape(8, fold)

    whole, gathered = plan.finish(rows)
    grads = {kind: [[whole[(kind, l)]] for l in range(DEPTH)] for kind in ("in", "br", "out")}

    off_nw = n_mod
    off_lb = off_nw + n_nw
    off_lng = off_lb + n_lb
    off_lnb = off_lng + n_ln
    off_cw = off_lnb + n_ln
    off_loss = off_cw + n_cw
    total = _sum_devices(gathered).reshape(1, 8 * fold)
    gathered = gathered.reshape(N_DEV, 1, 8 * fold)
    d_lower = _lbs_bwd(lower_bounds, total[0, off_lb:off_lng].reshape(DEPTH, WIDTH))
    loss = total[0, off_loss]
    d_b_mod = total[0, :n_mod].reshape(DEPTH, 3 * D)
    d_norm_w = total[0, off_nw:off_lb].reshape(DEPTH, 128)
    d_ln_g = total[0, off_lng:off_lnb].reshape(DEPTH, D)
    d_ln_b = total[0, off_lnb:off_cw].reshape(DEPTH, D)
    d_conv = total[0, off_cw:off_loss].reshape(DEPTH, 3, N_CHIPS, WIDTH // N_CHIPS)
    d_conv = lax.dynamic_index_in_dim(d_conv, chip, axis=2, keepdims=False)
    dmod_all = gathered[:, 0, :n_mod].reshape(N_DEV, DEPTH, N_CHIPS, mod_cols)
    dmod_cols = jnp.transpose(lax.dynamic_index_in_dim(dmod_all, chip, axis=2, keepdims=False), (1, 0, 2))
    d_w_mod = _grad_w_mod(c_all, dmod_cols)

    res = {}
    res["w_mod"] = _adamw(w_mod, m_w_mod, v_w_mod, [[d_w_mod]], "adamw_w_mod")
    res["b_mod"] = _adamw(b_mod, m_b_mod, v_b_mod, [[d_b_mod]], "adamw_b_mod")
    res["w_in"] = _adamw(w_in, m_w_in, v_w_in, grads["in"], "adamw_w_in")
    res["conv_w"] = _adamw(conv_w, m_conv_w, v_conv_w, [[d_conv]], "adamw_conv_w")
    res["hgrn_norm_w"] = _adamw(hgrn_norm_w, m_hgrn_norm_w, v_hgrn_norm_w, [[d_norm_w]], "adamw_norm_w")
    res["lower_bounds"] = _adamw(lower_bounds, m_lower_bounds, v_lower_bounds, [[d_lower]], "adamw_lower_bounds")
    res["w_branch"] = _adamw(w_branch, m_w_branch, v_w_branch, grads["br"], "adamw_w_branch")
    res["w_out"] = _adamw(w_out, m_w_out, v_w_out, grads["out"], "adamw_w_out")
    res["ln_g"] = _adamw(ln_g, m_ln_g, v_ln_g, [[d_ln_g]], "adamw_ln_g")
    res["ln_b"] = _adamw(ln_b, m_ln_b, v_ln_b, [[d_ln_b]], "adamw_ln_b")
    names = ["w_mod", "b_mod", "w_in", "conv_w", "hgrn_norm_w", "lower_bounds", "w_branch", "w_out", "ln_g", "ln_b"]
    return (loss, dx[None], *[res[n][0] for n in names], *[res[n][1] for n in names],
            *[res[n][2] for n in names], *[res[n][3] for n in names])


class _Plan:
    def __init__(self, w_in, w_br, w_out, chip, core):
        self.local = {"in": w_in, "br": w_br, "out": w_out}
        self.chip, self.where = chip, jnp.stack([chip, core]).astype(jnp.int32)
        self.gathered, self.grads, self.chip_sums, self.scattered, self.pending = {}, {}, {}, {}, {}

    def _slab(self, key):
        mine = _halves(self.local[key[0]][key[1]])
        return lax.dynamic_update_slice(lax.empty((N_CHIPS,) + mine.shape, mine.dtype), mine[None], (self.chip, 0, 0, 0, 0))

    def _gather(self, keys):
        return ("gather", keys), _gather_job([self._slab(key) for key in keys])

    def _to_sibling(self, keys):
        return ("to_sibling", keys), _to_sibling_job([_halves(self.grads[key], 1) for key in keys])

    def _scatter(self, keys):
        return ("scatter", keys), _scatter_job([self.chip_sums[key] for key in keys])

    def job(self, stage, l):
        parts = []
        if stage == "proj_fwd":
            parts = [self._gather([("br", l), ("out", l)])]
        elif stage == "attn_fwd" and l + 1 < DEPTH:
            parts = [self._gather([("in", l + 1)])]
        elif stage == "attn_bwd":
            parts = [self._to_sibling([("out", l), ("br", l)])] + ([self._scatter([("in", l + 1)])] if l + 1 < DEPTH else [])
        elif stage == "hgrn_bwd":
            parts = [self._scatter([("out", l), ("br", l)])]
        elif stage == "proj_bwd":
            parts = [self._to_sibling([("in", l)])] if l else [self._scatter([("in", 0)])]
        self.pending[(stage, l)] = [(tag, len(job.outs)) for tag, job in parts]
        return _join_jobs([job for _, job in parts])

    def done(self, stage, l, outs):
        if outs is None:
            return
        at = 0
        for (what, keys), n_outs in self.pending[(stage, l)]:
            mine, at = outs[at:at + n_outs], at + n_outs
            for n, key in enumerate(keys):
                if what == "gather":
                    self.gathered[key] = mine[n].reshape((N_CHIPS,) + self.local[key[0]].shape[1:])
                elif what == "to_sibling":
                    self.chip_sums[key] = _sum_cores(_halves(self.grads[key], 1), mine[n], self.where)
                else:
                    self.scattered[key] = mine[n]

    def first(self):
        tag, job = self._gather([("in", 0)])
        self.pending[("first", 0)] = [(tag, len(job.outs))]
        self.done("first", 0, _run_job(job, "gather_first"))

    def took(self, key, grad):
        self.grads[key] = grad
        if key == ("in", 0):
            tag, job = self._to_sibling([key])
            self.pending[("took", 0)] = [(tag, len(job.outs))]
            self.done("took", 0, _run_job(job, "to_sibling_last"))

    def finish(self, rows):
        keys = [(kind, l) for kind in ("in", "br", "out") for l in range(DEPTH)]
        halves = [_sum_chips(self.chip_sums[key], self.scattered[key], self.where) for key in keys]
        outs = _run_job(_join_jobs([_place_job(halves), _gather8_job(rows)]), "place_halves")
        return {key: w.reshape(self.grads[key].shape[1:]) for key, w in zip(keys, outs[:-1])}, outs[-1]


def _local_step(x0, target, mods, lbs, conv_all, hgrn_norm_w, ln_g, ln_b, plan):
    D = D_MODEL
    after, before = _attn_consts()
    hg_consts = _hgrn_consts()

    saved = []
    xl = x0
    for l in range(DEPTH):
        (proj, h), got = _proj_fwd(xl, mods[l], plan.gathered[("in", l)], plan.job("proj_fwd", l))
        plan.done("proj_fwd", l, got)
        (o_a, tot), got = _attn_fwd(proj, after, plan.job("attn_fwd", l))
        plan.done("attn_fwd", l, got)
        o_b, states = _hgrn_fwd(proj, lbs[l], hg_consts)
        ys = _branch_fwd(proj, o_a, o_b, hgrn_norm_w, conv_all, l)
        x_next, merged, y = _merge_fwd(xl, mods[l], proj, ys, plan.gathered[("br", l)], plan.gathered[("out", l)], ln_g, ln_b, l)
        saved.append((xl, proj, h, o_a, tot, o_b, states, ys, merged, y))
        xl = x_next
    dx, loss_blk = _loss_head(xl, target)

    small = [None] * DEPTH
    for l in reversed(range(DEPTH)):
        xin, proj, h, o_a, tot, o_b, states, ys, merged, y = saved[l]
        dx_res, dy, dp, dg, dy_a, dy_b, dy_c, dln_g, dln_b, dgate = _merge_bwd(
            dx, xin, y, mods[l], proj, ys, plan.gathered[("br", l)], plan.gathered[("out", l)], ln_g, l)
        plan.took(("out", l), _grad_w_out(merged, dy))
        g_br = jax.ShapeDtypeStruct((N_CHIPS, 3, WIDTH, D // N_CHIPS), BF)
        for i in range(3):
            g_br = _grad_w(ys[i], dp, (i,), g_br, "grad_w_branch", tm=WIDTH, tn=D // N_CHIPS, b_col0=i * D)
        plan.took(("br", l), g_br)
        d_oa, d_ob, dz_a, dz_b, dpre, dpost, du, dz_c, dnorm_w, dconv_w = _branch_bwd(proj, o_a, o_b, hgrn_norm_w, conv_all, dy_a, dy_b, dy_c, l)
        (dq_a, dk_a, dv_a), got = _attn_bwd(proj, d_oa, tot, after, before, plan.job("attn_bwd", l))
        plan.done("attn_bwd", l, got)
        (dq_b, df_b, di_b, dlb), got = _hgrn_bwd(proj, lbs[l], states, d_ob, hg_consts, plan.job("hgrn_bwd", l))
        plan.done("hgrn_bwd", l, got)
        dproj = jnp.concatenate([dq_a, dk_a, dv_a, dz_a, dq_b, df_b, di_b, dz_b, dpre, dpost, du, dz_c, dg], axis=1)
        plan.took(("in", l), _grad_w(h, dproj, (), jax.ShapeDtypeStruct((N_CHIPS, D, IN_COLS // N_CHIPS), BF), "grad_w_in", tm=512, tn=768))
        (dx, dshift, dscale), got = _proj_bwd(dproj, plan.gathered[("in", l)], xin, mods[l], dx_res, plan.job("proj_bwd", l))
        plan.done("proj_bwd", l, got)
        small[l] = (dshift, dscale, dgate, dnorm_w, dlb, dln_g, dln_b, dconv_w)
    return loss_blk, dx, small


def _grad_w_out(merged, dy):
    S, D = merged.shape
    q = D // N_CHIPS

    def body(a_ref, b_ref, o_ref):
        o_ref[...] = _dot_tn(a_ref[...], b_ref[...]).astype(BF)

    return pl.pallas_call(
        body, name="grad_w_out",
        grid=(N_CHIPS,),
        in_specs=[pl.BlockSpec((S, q), lambda j: (0, j)), pl.BlockSpec((S, D), lambda j: (0, 0))],
        out_specs=pl.BlockSpec((None, q, D), lambda j: (j, 0, 0)),
        out_shape=jax.ShapeDtypeStruct((N_CHIPS, q, D), BF),
        compiler_params=_params(dimension_semantics=("parallel",)),
    )(merged, dy)
```

```python
import functools
import math

import numpy as np
import jax
import jax.numpy as jnp
from jax import lax
from jax.experimental import pallas as pl
from jax.experimental.pallas import tpu as pltpu

F32 = jnp.float32
BF = jnp.bfloat16
MESH = pl.DeviceIdType.MESH

DEPTH = 2
D_MODEL = 1024
WIDTH = 512
IN_COLS = 12 * WIDTH + 3 * D_MODEL
N_CHIPS = 4
N_DEV = 8
SB_BLOCK = 128
SB_HEAD_DIM = 64
HG_CHUNK = 64
HG_DIM = 128
LN_EPS = 1e-5
RMS_EPS = 1e-6
ALPHA = (2.0 * DEPTH) ** 0.25
ADAM_LR, ADAM_B1, ADAM_B2, ADAM_EPS, ADAM_WD, ADAM_STEP = 0.001, 0.9, 0.999, 1e-08, 0.01, 10
VMEM_LIMIT = 56 << 20


def _params(**kw):
    return pltpu.CompilerParams(vmem_limit_bytes=VMEM_LIMIT, **kw)


def _dot(a, b):
    return jnp.dot(a, b, preferred_element_type=F32)


def _dot_nt(a, b):
    return lax.dot_general(a, b, (((1,), (1,)), ((), ())), preferred_element_type=F32)


def _dot_tn(a, b):
    return lax.dot_general(a, b, (((0,), (0,)), ((), ())), preferred_element_type=F32)


def _sigmoid(x):
    return 1.0 / (1.0 + jnp.exp(-x))


def _silu(x):
    return x * _sigmoid(x)


def _softplus(z):
    return jnp.maximum(z, 0.0) + jnp.log(1.0 + jnp.exp(-jnp.abs(z)))


def _split_dot(t, g, terms):
    acc = None
    rest = g
    for _ in range(terms):
        part = rest.astype(BF)
        rest = rest - part.astype(F32)
        d = _dot(t, part)
        acc = d if acc is None else acc + d
    return acc


def _split_dot_r(g, t, terms):
    acc = None
    rest = g
    for _ in range(terms):
        part = rest.astype(BF)
        rest = rest - part.astype(F32)
        d = _dot(part, t)
        acc = d if acc is None else acc + d
    return acc


def _standardize(x):
    mu = jnp.mean(x, axis=-1, keepdims=True)
    xc = x - mu
    var = jnp.mean(xc * xc, axis=-1, keepdims=True)
    rstd = lax.rsqrt(var + LN_EPS)
    return xc * rstd, rstd


def _standardize_bwd(dxs, xs, rstd):
    return rstd * (dxs - jnp.mean(dxs, axis=-1, keepdims=True) - xs * jnp.mean(dxs * xs, axis=-1, keepdims=True))


class _Job:
    def __init__(self, ins, outs, sems, make, alias=None):
        self.ins, self.outs, self.sems, self.make = list(ins), list(outs), list(sems), make
        self.alias = dict(alias or {})


def _join_jobs(jobs):
    jobs = [j for j in jobs if j is not None]
    if len(jobs) <= 1:
        return jobs[0] if jobs else None

    def make(ins, outs, sems):
        phases, i, o, s = [], 0, 0, 0
        for j in jobs:
            got = j.make(ins[i:i + len(j.ins)], outs[o:o + len(j.outs)], sems[s:s + len(j.sems)])
            i, o, s = i + len(j.ins), o + len(j.outs), s + len(j.sems)
            for n, phase in enumerate(got):
                if n == len(phases):
                    phases.append([])
                phases[n] += phase
        return phases

    alias, i, o = {}, 0, 0
    for j in jobs:
        alias.update({i + a: o + b for a, b in j.alias.items()})
        i, o = i + len(j.ins), o + len(j.outs)
    return _Job(sum([j.ins for j in jobs], []), sum([j.outs for j in jobs], []), sum([j.sems for j in jobs], []), make, alias)


def _flip(v, bit):
    return 1 - v if bit else v


def _halves(a, front=0):
    shape = a.shape
    lead = math.prod(shape[front:-2])
    return a.reshape(shape[:front] + (lead, 2, shape[-2] // 2, shape[-1]))


def _dma_sems(*shapes):
    return [pltpu.SemaphoreType.DMA(s) for s in shapes]


def _same(arrays):
    return [jax.ShapeDtypeStruct(a.shape, a.dtype) for a in arrays]


def _gather_job(slabs):
    n = len(slabs)

    def make(ins, outs, sems):
        send1, recv1, send2, recv2 = sems
        mx, my, mc = lax.axis_index("x"), lax.axis_index("y"), lax.axis_index("c")
        fetch, pass_on = [], []
        for a in range(n):
            ours = outs[a].at[2 * mx + my, :, mc]
            for k in range(1, N_CHIPS):
                px, py = _flip(mx, k & 2), _flip(my, k & 1)
                fetch.append(pltpu.make_async_remote_copy(
                    src_ref=ours, dst_ref=ours, send_sem=send1.at[a, k - 1], recv_sem=recv1.at[a, k - 1],
                    device_id=(px, py, mc), device_id_type=MESH))
                theirs = outs[a].at[2 * px + py, :, mc]
                pass_on.append(pltpu.make_async_remote_copy(
                    src_ref=theirs, dst_ref=theirs, send_sem=send2.at[a, k - 1], recv_sem=recv2.at[a, k - 1],
                    device_id=(mx, my, 1 - mc), device_id_type=MESH))
        return [fetch, pass_on]

    pairs = (n, N_CHIPS - 1)
    return _Job(slabs, _same(slabs), _dma_sems(pairs, pairs, pairs, pairs), make, {a: a for a in range(n)})


def _to_sibling_job(grads):
    n = len(grads)

    def make(ins, outs, sems):
        send_sems, recv_sems = sems
        mx, my, mc = lax.axis_index("x"), lax.axis_index("y"), lax.axis_index("c")
        return [[pltpu.make_async_remote_copy(
            src_ref=ins[a].at[:, :, 1 - mc], dst_ref=outs[a], send_sem=send_sems.at[a], recv_sem=recv_sems.at[a],
            device_id=(mx, my, 1 - mc), device_id_type=MESH) for a in range(n)]]

    outs = [jax.ShapeDtypeStruct(g.shape[:2] + g.shape[3:], g.dtype) for g in grads]
    return _Job(grads, outs, _dma_sems((n,), (n,)), make)


def _scatter_job(sums):
    n = len(sums)

    def make(ins, outs, sems):
        send_sems, recv_sems = sems
        mx, my, mc = lax.axis_index("x"), lax.axis_index("y"), lax.axis_index("c")
        copies = []
        for a in range(n):
            for k in range(1, N_CHIPS):
                px, py = _flip(mx, k & 2), _flip(my, k & 1)
                copies.append(pltpu.make_async_remote_copy(
                    src_ref=ins[a].at[2 * px + py], dst_ref=outs[a].at[k - 1], send_sem=send_sems.at[a, k - 1],
                    recv_sem=recv_sems.at[a, k - 1], device_id=(px, py, mc), device_id_type=MESH))
        return [copies]

    pairs = (n, N_CHIPS - 1)
    return _Job(sums, [jax.ShapeDtypeStruct((N_CHIPS - 1,) + s.shape[1:], s.dtype) for s in sums], _dma_sems(pairs, pairs), make)


def _place_job(wholes):
    n = len(wholes)

    def make(ins, outs, sems):
        send_sems, recv_sems = sems
        mx, my, mc = lax.axis_index("x"), lax.axis_index("y"), lax.axis_index("c")
        copies = []
        for a in range(n):
            here = outs[a].at[:, mc]
            copies.append(pltpu.make_async_remote_copy(src_ref=here, dst_ref=here, send_sem=send_sems.at[a], recv_sem=recv_sems.at[a],
                                                       device_id=(mx, my, 1 - mc), device_id_type=MESH))
        return [copies]

    return _Job(wholes, _same(wholes), _dma_sems((n,), (n,)), make, {a: a for a in range(n)})


def _gather8_job(x):
    def make(ins, outs, sems):
        local_sem, send_sems, recv_sems = sems
        mx, my, mc = lax.axis_index("x"), lax.axis_index("y"), lax.axis_index("c")
        here = outs[0].at[4 * mx + 2 * my + mc]
        copies = [pltpu.make_async_copy(ins[0], here, local_sem.at[0])]
        for k in range(1, N_DEV):
            peer = (_flip(mx, k & 4), _flip(my, k & 2), _flip(mc, k & 1))
            copies.append(pltpu.make_async_remote_copy(src_ref=ins[0], dst_ref=here, send_sem=send_sems.at[k - 1],
                                                       recv_sem=recv_sems.at[k - 1], device_id=peer, device_id_type=MESH))
        return [copies]

    return _Job([x], [jax.ShapeDtypeStruct((N_DEV,) + x.shape, x.dtype)], _dma_sems((1,), (N_DEV - 1,), (N_DEV - 1,)), make)


def _run_phases(phases, first=0):
    for n, phase in enumerate(phases):
        if n >= first:
            for cp in phase:
                cp.start()
        for cp in phase:
            cp.wait()


def _run_job(job, name):
    k_in, k_out = len(job.ins), len(job.outs)

    def body(*refs):
        _run_phases(job.make(refs[:k_in], refs[k_in:k_in + k_out], refs[k_in + k_out:]))

    hbm = pl.BlockSpec(memory_space=pl.ANY)
    return pl.pallas_call(body, name=name, in_specs=[hbm] * k_in, out_specs=[hbm] * k_out, out_shape=job.outs,
                          scratch_shapes=job.sems, input_output_aliases=job.alias, compiler_params=_params())(*job.ins)


def _hosted(body, job, args, *, name, grid, in_specs, out_specs, out_shape, scratch_shapes=(), semantics):
    in_specs, out_specs, out_shape, scratch = list(in_specs), list(out_specs), list(out_shape), list(scratch_shapes)
    if job is None:
        outs = pl.pallas_call(body, name=name, grid=grid, in_specs=in_specs, out_specs=out_specs, out_shape=out_shape,
                              scratch_shapes=scratch, compiler_params=_params(dimension_semantics=semantics))(*args)
        return list(outs), None
    n_in, n_out, n_scr, k_in, k_out = len(in_specs), len(out_specs), len(scratch), len(job.ins), len(job.outs)

    def wrapped(*refs):
        ins, rest = refs[:n_in], refs[n_in:]
        job_ins, rest = rest[:k_in], rest[k_in:]
        outs, rest = rest[:n_out], rest[n_out:]
        job_outs, rest = rest[:k_out], rest[k_out:]
        scr, sems = rest[:n_scr], rest[n_scr:]
        ids = [pl.program_id(a) for a in range(len(grid))]
        first = functools.reduce(jnp.logical_and, [i == 0 for i in ids])
        last = functools.reduce(jnp.logical_and, [i == g - 1 for i, g in zip(ids, grid)])

        @pl.when(first)
        def _():
            for cp in job.make(job_ins, job_outs, sems)[0]:
                cp.start()

        body(*ins, *outs, *scr)

        @pl.when(last)
        def _():
            _run_phases(job.make(job_ins, job_outs, sems), first=1)

    hbm = pl.BlockSpec(memory_space=pl.ANY)
    outs = pl.pallas_call(
        wrapped, name=name, grid=grid, in_specs=in_specs + [hbm] * k_in, out_specs=out_specs + [hbm] * k_out,
        out_shape=out_shape + job.outs, scratch_shapes=scratch + job.sems,
        input_output_aliases={n_in + i: n_out + o for i, o in job.alias.items()},
        compiler_params=_params(dimension_semantics=("arbitrary",) * len(grid)))(*args, *job.ins)
    return list(outs[:n_out]), list(outs[n_out:])


def _proj_fwd(x, mod, wg, job=None, tm=512, tn=2304):
    S, D = x.shape
    tm = min(tm, S)
    shard = wg.shape[-1]
    per = shard // tn

    def body(x_ref, mod_ref, w_ref, proj_ref, h_ref, hs):
        @pl.when(pl.program_id(1) == 0)
        def _():
            xs, _ = _standardize(x_ref[...])
            h = xs * (1.0 + mod_ref[:, D:2 * D]) + mod_ref[:, 0:D]
            hb = h.astype(BF)
            hs[...] = hb
            h_ref[...] = hb

        proj_ref[...] = _dot(hs[...], w_ref[...])

    return _hosted(
        body, job, (x, mod, wg), name="proj_fwd",
        grid=(S // tm, IN_COLS // tn),
        in_specs=[pl.BlockSpec((tm, D), lambda i, j: (i, 0)),
                  pl.BlockSpec((1, 3 * D), lambda i, j: (0, 0)),
                  pl.BlockSpec((None, D, tn), lambda i, j: (j // per, 0, j % per))],
        out_specs=[pl.BlockSpec((tm, tn), lambda i, j: (i, j)),
                   pl.BlockSpec((tm, D), lambda i, j: (i, 0))],
        out_shape=[jax.ShapeDtypeStruct((S, IN_COLS), F32), jax.ShapeDtypeStruct((S, D), BF)],
        scratch_shapes=[pltpu.VMEM((tm, D), BF)],
        semantics=("parallel", "arbitrary"))


SB_ROWS = 256
SB_KEYS = 256


def _attn_consts():
    j = np.arange(SB_KEYS)[:, None]
    s = np.arange(SB_KEYS)[None, :]
    from_here = np.concatenate([(j >= s), (j >= s)], axis=0).astype(np.float32)
    up_to = np.concatenate([(j <= s), (j <= s)], axis=0).astype(np.float32)
    return jnp.asarray(from_here, BF), jnp.asarray(up_to, BF)


def _hi_lo(x):
    hi = lax.bitcast_convert_type(lax.bitcast_convert_type(x, jnp.uint32) & jnp.uint32(0xFFFF0000), F32)
    return hi.astype(BF), (x - hi).astype(BF)


def _sums_r(x, t2):
    hi, lo = _hi_lo(x)
    return _dot(jnp.concatenate([hi, lo], axis=1), t2)


def _all_lanes(col, lanes):
    return jnp.broadcast_to(col, (col.shape[0], lanes))


def _attn_rows(ref, r0, rows, lanes, head0, scale=None):
    v = ref[pl.ds(r0, rows), lanes]
    if scale is not None:
        v = v * scale
    return jnp.concatenate([jnp.where(head0, v, 0.0), jnp.where(head0, 0.0, v)], axis=0).astype(BF)


SB_PAIRS = 2
SB_LANES = SB_PAIRS * SB_BLOCK


def _attn_specs(S):
    return lambda base: pl.BlockSpec((S, SB_LANES), lambda p, base=base: (0, base // SB_PAIRS + p))


def _attn_scores(q2n, k_ref, lanes, kj, t2, from_here_ref, masked):
    c0 = pl.multiple_of(kj * SB_KEYS, SB_KEYS)
    kb = k_ref[pl.ds(c0, SB_KEYS), lanes].astype(BF)
    zn = _dot_nt(q2n, kb)
    lsb = jnp.minimum(zn, 0.0) - jnp.log(1.0 + jnp.exp(-jnp.abs(zn)))
    valid = None
    if masked:
        valid = (lax.broadcasted_iota(jnp.int32, zn.shape, 1) + kj * SB_KEYS) < t2
        lsb = jnp.where(valid, lsb, 0.0)
    return c0, kb, zn, valid, lsb, _sums_r(lsb, from_here_ref[...])


def _attn_fwd(proj, from_here, job=None):
    S = proj.shape[0]
    TQ = SB_ROWS
    assert S % TQ == 0 and SB_KEYS == TQ
    scale = SB_HEAD_DIM ** -0.5
    pairs = range(SB_PAIRS)
    lanes = [pl.ds(p * SB_BLOCK, SB_BLOCK) for p in pairs]

    def body(q_ref, k_ref, v_ref, from_here_ref, o_ref, tot_ref, run, acc):
        head0 = lax.broadcasted_iota(jnp.int32, (1, 2 * SB_HEAD_DIM), 1) < SB_HEAD_DIM

        def qloop(qi, _):
            r0 = pl.multiple_of(qi * TQ, TQ)
            q2n = [_attn_rows(q_ref, r0, TQ, lanes[p], head0, -scale) for p in pairs]
            trow = lax.broadcasted_iota(jnp.int32, (TQ, SB_KEYS), 0) + qi * TQ
            t2 = jnp.concatenate([trow, trow], axis=0)
            run[...] = jnp.zeros_like(run)
            acc[...] = jnp.zeros_like(acc)

            def step(kj, masked):
                got = [_attn_scores(q2n[p], k_ref, lanes[p], kj, t2, from_here_ref, masked) for p in pairs]
                for p in pairs:
                    c0, _, zn, valid, _, sums = got[p]
                    r = run[p]
                    e = sums - zn + jnp.concatenate([r, r], axis=1)
                    if masked:
                        e = jnp.where(valid, e, -jnp.inf)
                    acc[p] += _dot(jnp.exp(e).astype(BF), v_ref[pl.ds(c0, SB_KEYS), lanes[p]].astype(BF))
                    run[p] = r + _all_lanes(sums[:, 0:1], SB_BLOCK)

            step(qi, True)

            def below(n, _):
                step(qi - 1 - n, False)
                return 0

            lax.fori_loop(0, qi, below, 0)
            for p in pairs:
                o_ref[pl.ds(r0, TQ), lanes[p]] = jnp.where(head0, acc[p, 0:TQ, :], acc[p, TQ:2 * TQ, :])
                tot_ref[p, 0, pl.ds(r0, TQ), :] = run[p, 0:TQ, :]
                tot_ref[p, 1, pl.ds(r0, TQ), :] = run[p, TQ:2 * TQ, :]
            return 0

        lax.fori_loop(0, S // TQ, qloop, 0)

    col = _attn_specs(S)
    state = pltpu.VMEM((SB_PAIRS, 2 * TQ, SB_BLOCK), F32)
    return _hosted(
        body, job, (proj, proj, proj, from_here), name="attn_fwd",
        grid=(WIDTH // SB_LANES,),
        in_specs=[col(0), col(4), col(8), pl.BlockSpec(from_here.shape, lambda p: (0, 0))],
        out_specs=[col(0), pl.BlockSpec((SB_PAIRS, 2, S, 128), lambda p: (p, 0, 0, 0))],
        out_shape=[jax.ShapeDtypeStruct((S, WIDTH), F32), jax.ShapeDtypeStruct((WIDTH // 128, 2, S, 128), F32)],
        scratch_shapes=[state, state],
        semantics=("parallel",))


def _attn_bwd(proj, d_o, tot, from_here, up_to, job=None):
    S = proj.shape[0]
    TQ = SB_ROWS
    assert S % TQ == 0 and SB_KEYS == TQ
    scale = SB_HEAD_DIM ** -0.5
    pairs = range(SB_PAIRS)
    lanes = [pl.ds(p * SB_BLOCK, SB_BLOCK) for p in pairs]

    def body(q_ref, k_ref, v_ref, do_ref, tot_ref, from_here_ref, up_to_ref, dq_ref, dk_ref, dv_ref, pre, cum, dq_acc, dk_acc, dv_acc):
        head0 = lax.broadcasted_iota(jnp.int32, (1, 2 * SB_HEAD_DIM), 1) < SB_HEAD_DIM
        dk_acc[...] = jnp.zeros_like(dk_acc)
        dv_acc[...] = jnp.zeros_like(dv_acc)

        def qloop(qi, _):
            r0 = pl.multiple_of(qi * TQ, TQ)
            q2n = [_attn_rows(q_ref, r0, TQ, lanes[p], head0, -scale) for p in pairs]
            do2 = [_attn_rows(do_ref, r0, TQ, lanes[p], head0) for p in pairs]
            trow = lax.broadcasted_iota(jnp.int32, (TQ, SB_KEYS), 0) + qi * TQ
            t2 = jnp.concatenate([trow, trow], axis=0)
            for p in pairs:
                pre[p, 0:TQ, :] = tot_ref[p, 0, pl.ds(r0, TQ), :]
                pre[p, TQ:2 * TQ, :] = tot_ref[p, 1, pl.ds(r0, TQ), :]
            cum[...] = jnp.zeros_like(cum)
            dq_acc[...] = jnp.zeros_like(dq_acc)

            def step(kj, masked):
                got = [_attn_scores(q2n[p], k_ref, lanes[p], kj, t2, from_here_ref, masked) for p in pairs]
                for p in pairs:
                    c0, kb, zn, valid, lsb, sums = got[p]
                    later = pre[p] - _all_lanes(sums[:, 0:1], SB_BLOCK)
                    pre[p] = later
                    e = sums - zn + jnp.concatenate([later, later], axis=1)
                    sig = jnp.exp(lsb - zn)
                    if masked:
                        e = jnp.where(valid, e, -jnp.inf)
                        sig = jnp.where(valid, sig, 0.0)
                    a = jnp.exp(e)
                    w = _dot_nt(do2[p], v_ref[pl.ds(c0, SB_KEYS), lanes[p]].astype(BF)) * a
                    upto = _sums_r(w, up_to_ref[...])
                    c = cum[p]
                    dz = w - sig * (upto + jnp.concatenate([c, c], axis=1))
                    cum[p] = c + _all_lanes(upto[:, SB_KEYS - 1:SB_KEYS], SB_BLOCK)
                    dzb = dz.astype(BF)
                    dq_acc[p] += _dot(dzb, kb)
                    dk_acc[pl.ds(c0, SB_KEYS), lanes[p]] += _dot_tn(dzb, q2n[p])
                    dv_acc[pl.ds(c0, SB_KEYS), lanes[p]] += _dot_tn(a.astype(BF), do2[p])

            def below(kj, _):
                step(kj, False)
                return 0

            lax.fori_loop(0, qi, below, 0)
            step(qi, True)
            for p in pairs:
                dq_ref[pl.ds(r0, TQ), lanes[p]] = (jnp.where(head0, dq_acc[p, 0:TQ, :], dq_acc[p, TQ:2 * TQ, :]) * scale).astype(BF)
            return 0

        lax.fori_loop(0, S // TQ, qloop, 0)
        dk_ref[...] = (-dk_acc[...]).astype(BF)
        dv_ref[...] = dv_acc[...].astype(BF)

    col = _attn_specs(S)
    whole = lambda a: pl.BlockSpec(a.shape, lambda p: (0, 0))
    out = jax.ShapeDtypeStruct((S, WIDTH), BF)
    state = pltpu.VMEM((SB_PAIRS, 2 * TQ, SB_BLOCK), F32)
    grads = pltpu.VMEM((S, SB_LANES), F32)
    return _hosted(
        body, job, (proj, proj, proj, d_o, tot, from_here, up_to), name="attn_bwd",
        grid=(WIDTH // SB_LANES,),
        in_specs=[col(0), col(4), col(8), col(0), pl.BlockSpec((SB_PAIRS, 2, S, 128), lambda p: (p, 0, 0, 0)), whole(from_here), whole(up_to)],
        out_specs=[col(0), col(0), col(0)],
        out_shape=[out, out, out],
        scratch_shapes=[state, state, state, grads, grads],
        semantics=("parallel",))


HG_LEVELS = (32, 16, 8, 4, 2, 1)


def _hgrn_consts():
    C = HG_CHUNK
    t = np.arange(C)[:, None]
    s = np.arange(C)[None, :]
    rows = [(s <= t), (s > t)]
    masks = [(t == s)]
    for m in HG_LEVELS:
        two = 2 * m
        mid = (t // two) * two + m
        right = (t % two) >= m
        rows.append((right & (s >= mid) & (s <= t)) | ((~right) & (s > t) & (s <= mid - 1)))
        masks.append(((t // two) == (s // two)) & right & ((s % two) < m))
    tri = np.concatenate(rows, axis=0).astype(np.float32)
    return (jnp.asarray(tri, BF), jnp.asarray(tri.T.copy(), BF), jnp.asarray(np.stack(masks).astype(np.float32), F32))


HG_SUM_BLOCKS = 2 + len(HG_LEVELS)
HG_SUM_TERMS = 2


@jax.custom_vjp
def _hgrn_sums(tri, tri_t, g):
    C = HG_CHUNK
    e = _split_dot(tri, g, HG_SUM_TERMS)
    blocks = tuple(e[n * C:(n + 1) * C] for n in range(HG_SUM_BLOCKS))
    return blocks + (jnp.broadcast_to(e[C - 1:C], (HG_DIM, e.shape[1])),)


def _hgrn_sums_fwd(tri, tri_t, g):
    return _hgrn_sums(tri, tri_t, g), (tri, tri_t)


def _hgrn_sums_bwd(res, ds):
    tri, tri_t = res
    C = HG_CHUNK
    last = lax.broadcasted_iota(jnp.int32, (C, 1), 0) == C - 1
    prefix = ds[0] + jnp.where(last, jnp.sum(ds[-1], axis=0, keepdims=True), 0.0)
    d = jnp.concatenate((prefix,) + tuple(ds[1:-1]), axis=0)
    return jnp.zeros_like(tri), jnp.zeros_like(tri_t), _split_dot(tri_t, d, HG_SUM_TERMS)


_hgrn_sums.defvjp(_hgrn_sums_fwd, _hgrn_sums_bwd)


def _bf_dot(a, b):
    return _dot(a.astype(BF), b.astype(BF))


def _bf_dot_nt(a, b):
    return _dot_nt(a.astype(BF), b.astype(BF))


def _bf_dot_tn(a, b):
    return _dot_tn(a.astype(BF), b.astype(BF))


@jax.custom_vjp
def _mm(a, b):
    return _bf_dot(a, b)


_mm.defvjp(lambda a, b: (_bf_dot(a, b), (a, b)), lambda r, ct: (_bf_dot_nt(ct, r[1]), _bf_dot_tn(r[0], ct)))


@jax.custom_vjp
def _mm_nt(a, b):
    return _bf_dot_nt(a, b)


_mm_nt.defvjp(lambda a, b: (_bf_dot_nt(a, b), (a, b)), lambda r, ct: (_bf_dot(ct, r[1]), _bf_dot_tn(ct, r[0])))


@jax.custom_vjp
def _mm_tn(a, b):
    return _bf_dot_tn(a, b)


_mm_tn.defvjp(lambda a, b: (_bf_dot_tn(a, b), (a, b)), lambda r, ct: (_bf_dot_nt(r[1], ct), _bf_dot(r[0], ct)))


def _hgrn_chunk(tri, tri_t, masks, qraw, fpre, v, st, lb):
    q = _silu(qraw)
    f = lb + (1.0 - lb) * _sigmoid(fpre)
    k = 1.0 - f
    e = _hgrn_sums(tri, tri_t, jnp.log(f))
    prefix, suffix, whole = e[0], e[1], e[-1]
    scores = masks[0] * _mm_nt(q, k)
    for n in range(len(HG_LEVELS)):
        decay = jnp.exp(e[2 + n])
        scores = scores + masks[n + 1] * _mm_nt(q * decay, k * decay)
    o = _mm_nt(q * jnp.exp(prefix), st) + _mm(scores, v)
    st_new = st * jnp.exp(whole) + _mm_tn(v, k * jnp.exp(suffix))
    return o, st_new


HG_HEADS_PER_STEP = 2
HG_LANES = HG_HEADS_PER_STEP * HG_DIM


def _hgrn_specs(S, consts):
    col = lambda base: pl.BlockSpec((S, HG_LANES), lambda p, base=base: (0, base // HG_HEADS_PER_STEP + p))
    whole = [pl.BlockSpec(a.shape, lambda p, n=a.ndim: (0,) * n) for a in consts]
    return col, whole


def _hgrn_fwd(proj, lbs, consts):
    S = proj.shape[0]
    nc = S // HG_CHUNK
    heads = range(HG_HEADS_PER_STEP)

    def body(q_ref, f_ref, i_ref, lb_ref, tri_ref, trit_ref, mask_ref, o_ref, st_ref):
        tri, tri_t = tri_ref[...], trit_ref[...]
        masks = [mask_ref[n] for n in range(len(HG_LEVELS) + 1)]

        def chunk(ci, sts):
            r0 = pl.multiple_of(ci * HG_CHUNK, HG_CHUNK)
            rows = pl.ds(r0, HG_CHUNK)
            new = []
            for hd in heads:
                lanes = pl.ds(hd * HG_DIM, HG_DIM)
                st_ref[hd, ci] = sts[hd]
                o, st_new = _hgrn_chunk(tri, tri_t, masks, q_ref[rows, lanes], f_ref[rows, lanes], i_ref[rows, lanes], sts[hd], lb_ref[hd])
                o_ref[rows, lanes] = o
                new.append(st_new)
            return tuple(new)

        lax.fori_loop(0, nc, chunk, tuple(jnp.zeros((HG_DIM, HG_DIM), F32) for _ in heads))

    col, whole = _hgrn_specs(S, consts)
    return pl.pallas_call(
        body, name="hgrn_fwd",
        grid=(WIDTH // HG_LANES,),
        in_specs=[col(16), col(20), col(24), pl.BlockSpec((HG_HEADS_PER_STEP, 1, 128), lambda p: (p, 0, 0))] + whole,
        out_specs=[col(0), pl.BlockSpec((HG_HEADS_PER_STEP, nc, HG_DIM, HG_DIM), lambda p: (p, 0, 0, 0))],
        out_shape=[jax.ShapeDtypeStruct((S, WIDTH), F32), jax.ShapeDtypeStruct((WIDTH // 128, nc, HG_DIM, HG_DIM), F32)],
        compiler_params=_params(dimension_semantics=("parallel",)),
    )(proj, proj, proj, lbs, *consts)


def _hgrn_bwd(proj, lbs, states, d_o, consts, job=None):
    S = proj.shape[0]
    nc = S // HG_CHUNK

    def body(q_ref, f_ref, i_ref, lb_ref, st_ref, do_ref, tri_ref, trit_ref, mask_ref, dq_ref, df_ref, di_ref, dlb_ref):
        masks = [mask_ref[n] for n in range(len(HG_LEVELS) + 1)]
        fn = functools.partial(_hgrn_chunk, tri_ref[...], trit_ref[...], masks)
        heads = range(HG_HEADS_PER_STEP)

        def chunk(n, carry):
            ci = nc - 1 - n
            r0 = pl.multiple_of(ci * HG_CHUNK, HG_CHUNK)
            rows = pl.ds(r0, HG_CHUNK)
            new = []
            for hd in heads:
                d_st, dlb = carry[hd]
                lanes = pl.ds(hd * HG_DIM, HG_DIM)
                _, pull = jax.vjp(fn, q_ref[rows, lanes], f_ref[rows, lanes], i_ref[rows, lanes], st_ref[hd, ci], lb_ref[hd])
                dq, df, di, d_prev, dl = pull((do_ref[rows, lanes], d_st))
                dq_ref[rows, lanes] = dq.astype(BF)
                df_ref[rows, lanes] = df.astype(BF)
                di_ref[rows, lanes] = di.astype(BF)
                new.append((d_prev, dlb + dl))
            return tuple(new)

        zero = (jnp.zeros((HG_DIM, HG_DIM), F32), jnp.zeros((1, HG_DIM), F32))
        done = lax.fori_loop(0, nc, chunk, tuple(zero for _ in heads))
        for hd in heads:
            dlb_ref[hd] = done[hd][1]

    col, whole = _hgrn_specs(S, consts)
    head = pl.BlockSpec((HG_HEADS_PER_STEP, 1, 128), lambda p: (p, 0, 0))
    out = jax.ShapeDtypeStruct((S, WIDTH), BF)
    return _hosted(
        body, job, (proj, proj, proj, lbs, states, d_o, *consts), name="hgrn_bwd",
        grid=(WIDTH // HG_LANES,),
        in_specs=[col(16), col(20), col(24), head, pl.BlockSpec((HG_HEADS_PER_STEP, nc, HG_DIM, HG_DIM), lambda p: (p, 0, 0, 0)), col(0)] + whole,
        out_specs=[col(0), col(0), col(0), head],
        out_shape=[out, out, out, jax.ShapeDtypeStruct((WIDTH // 128, 1, 128), F32)],
        semantics=("parallel",))


def _shift_down(x, n):
    rows = lax.broadcasted_iota(jnp.int32, x.shape, 0)
    return jnp.where(rows >= n, pltpu.roll(x, n, 0), 0.0)


def _shift_up(x, n):
    S = x.shape[0]
    rows = lax.broadcasted_iota(jnp.int32, x.shape, 0)
    return jnp.where(rows < S - n, pltpu.roll(x, S - n, 0), 0.0)


def _branch_fwd(proj, o_a, o_b, norm_w, conv_w, layer):
    S = proj.shape[0]

    def body(oa_ref, za_ref, ob_ref, zb_ref, nw_ref, pre_ref, post_ref, u_ref, zc_ref, cw_ref, ya_ref, yb_ref, yc_ref):
        ya_ref[...] = (oa_ref[...] * _silu(za_ref[...])).astype(BF)
        ob = ob_ref[...]
        rn = lax.rsqrt(jnp.mean(ob * ob, axis=-1, keepdims=True) + RMS_EPS)
        yb_ref[...] = (ob * rn * nw_ref[layer:layer + 1, :] * _silu(zb_ref[...])).astype(BF)
        pu = pre_ref[...] * u_ref[...]
        conv = cw_ref[2:3, :] * pu + cw_ref[1:2, :] * _shift_down(pu, 1) + cw_ref[0:1, :] * _shift_down(pu, 2)
        yc_ref[...] = (post_ref[...] * conv * _silu(zc_ref[...])).astype(BF)

    col = lambda base: pl.BlockSpec((S, 128), lambda p, base=base: (0, base + p))
    out = jax.ShapeDtypeStruct((S, WIDTH), BF)
    return pl.pallas_call(
        body, name="branch_fwd",
        grid=(WIDTH // 128,),
        in_specs=[col(0), col(12), col(0), col(28), pl.BlockSpec(norm_w.shape, lambda p: (0, 0)),
                  col(32), col(36), col(40), col(44), pl.BlockSpec((None, None, 3, 128), lambda p: (p, layer, 0, 0))],
        out_specs=[col(0), col(0), col(0)],
        out_shape=[out, out, out],
        compiler_params=_params(dimension_semantics=("parallel",)),
    )(o_a, proj, o_b, proj, norm_w, proj, proj, proj, proj, conv_w)


def _branch_bwd(proj, o_a, o_b, norm_w, conv_w, dy_a, dy_b, dy_c, layer):
    S = proj.shape[0]

    def dsilu(z):
        s = _sigmoid(z)
        return s * z, s * (1.0 + z * (1.0 - s))

    def body(oa_ref, za_ref, ob_ref, zb_ref, nw_ref, pre_ref, post_ref, u_ref, zc_ref, cw_ref, dya_ref, dyb_ref, dyc_ref,
             doa_ref, dob_ref, dza_ref, dzb_ref, dpre_ref, dpost_ref, du_ref, dzc_ref, dnw_ref, dcw_ref):
        dya = dya_ref[...]
        sa, dsa = dsilu(za_ref[...])
        doa_ref[...] = dya * sa
        dza_ref[...] = (dya * oa_ref[...] * dsa).astype(BF)

        dyb = dyb_ref[...]
        ob = ob_ref[...]
        nw = nw_ref[layer:layer + 1, :]
        sb, dsb = dsilu(zb_ref[...])
        rn = lax.rsqrt(jnp.mean(ob * ob, axis=-1, keepdims=True) + RMS_EPS)
        on = ob * rn
        dzb_ref[...] = (dyb * on * nw * dsb).astype(BF)
        don_w = dyb * sb
        dnw_ref[...] = jnp.sum(don_w * on, axis=0, keepdims=True)
        don = don_w * nw
        dob_ref[...] = rn * (don - on * jnp.mean(don * on, axis=-1, keepdims=True))

        dyc = dyc_ref[...]
        pre, post, u = pre_ref[...], post_ref[...], u_ref[...]
        sc, dsc = dsilu(zc_ref[...])
        pu = pre * u
        pu1, pu2 = _shift_down(pu, 1), _shift_down(pu, 2)
        conv = cw_ref[2:3, :] * pu + cw_ref[1:2, :] * pu1 + cw_ref[0:1, :] * pu2
        dzc_ref[...] = (dyc * post * conv * dsc).astype(BF)
        dpost_ref[...] = (dyc * conv * sc).astype(BF)
        dconv = dyc * post * sc
        dcw_ref[0:1, :] = jnp.sum(dconv * pu2, axis=0, keepdims=True)
        dcw_ref[1:2, :] = jnp.sum(dconv * pu1, axis=0, keepdims=True)
        dcw_ref[2:3, :] = jnp.sum(dconv * pu, axis=0, keepdims=True)
        dpu = cw_ref[2:3, :] * dconv + cw_ref[1:2, :] * _shift_up(dconv, 1) + cw_ref[0:1, :] * _shift_up(dconv, 2)
        dpre_ref[...] = (dpu * u).astype(BF)
        du_ref[...] = (dpu * pre).astype(BF)

    col = lambda base: pl.BlockSpec((S, 128), lambda p, base=base: (0, base + p))
    f32 = jax.ShapeDtypeStruct((S, WIDTH), F32)
    bf = jax.ShapeDtypeStruct((S, WIDTH), BF)
    return pl.pallas_call(
        body, name="branch_bwd",
        grid=(WIDTH // 128,),
        in_specs=[col(0), col(12), col(0), col(28), pl.BlockSpec(norm_w.shape, lambda p: (0, 0)),
                  col(32), col(36), col(40), col(44), pl.BlockSpec((None, None, 3, 128), lambda p: (p, layer, 0, 0)),
                  col(0), col(0), col(0)],
        out_specs=[col(0)] * 8 + [pl.BlockSpec((None, 1, 128), lambda p: (p, 0, 0)), pl.BlockSpec((None, 3, 128), lambda p: (p, 0, 0))],
        out_shape=[f32, f32, bf, bf, bf, bf, bf, bf, jax.ShapeDtypeStruct((WIDTH // 128, 1, 128), F32),
                   jax.ShapeDtypeStruct((WIDTH // 128, 3, 128), F32)],
        compiler_params=_params(dimension_semantics=("parallel",)),
    )(o_a, proj, o_b, proj, norm_w, proj, proj, proj, proj, conv_w, dy_a, dy_b, dy_c)


def _branch_proj(y_refs, wb_ref):
    out = []
    for i in range(3):
        yv = y_refs[i][...]
        out.append(jnp.concatenate([_dot(yv, wb_ref[j, i]) for j in range(N_CHIPS)], axis=1))
    return out


def _merge_fwd(x, mod, proj, ys, wb, wo, ln_g, ln_b, layer, tm=256):
    S, D = x.shape
    tm = min(tm, S)

    def body(x_ref, mod_ref, ga_ref, gb_ref, gc_ref, ya_ref, yb_ref, yc_ref, wb_ref, wo_ref, g_ref, b_ref, xo_ref, mg_ref, y_ref):
        ps = _branch_proj((ya_ref, yb_ref, yc_ref), wb_ref)
        merged = _sigmoid(ga_ref[...]) * ps[0] + _sigmoid(gb_ref[...]) * ps[1] + _sigmoid(gc_ref[...]) * ps[2]
        mb = merged.astype(BF)
        mg_ref[...] = mb
        y = _dot(mb, wo_ref[...].reshape(D, D))
        y_ref[...] = y
        r = ALPHA * x_ref[...] + (1.0 + mod_ref[:, 2 * D:3 * D]) * y
        xn, _ = _standardize(r)
        xo_ref[...] = xn * g_ref[layer:layer + 1, :] + b_ref[layer:layer + 1, :]

    row = lambda w, c=0: pl.BlockSpec((tm, w), lambda i, c=c: (i, c))
    whole = lambda a: pl.BlockSpec(a.shape, lambda i, n=a.ndim: (0,) * n)
    return pl.pallas_call(
        body, name="merge_fwd",
        grid=(S // tm,),
        in_specs=[row(D), whole(mod), row(D, 6), row(D, 7), row(D, 8), row(WIDTH), row(WIDTH), row(WIDTH), whole(wb),
                  whole(wo), whole(ln_g), whole(ln_b)],
        out_specs=[row(D), row(D), row(D)],
        out_shape=[jax.ShapeDtypeStruct((S, D), F32), jax.ShapeDtypeStruct((S, D), BF), jax.ShapeDtypeStruct((S, D), F32)],
        compiler_params=_params(dimension_semantics=("parallel",)),
    )(x, mod, proj, proj, proj, *ys, wb, wo, ln_g, ln_b)


def _merge_bwd(dxo, x, y, mod, proj, ys, wb, wo, ln_g, layer, tm=256):
    S, D = x.shape
    tm = min(tm, S)

    def body(dxo_ref, x_ref, y_ref, mod_ref, ga_ref, gb_ref, gc_ref, ya_ref, yb_ref, yc_ref, wb_ref, wo_ref, g_ref,
             dxr_ref, dy_ref, dp_ref, dg_ref, dya_ref, dyb_ref, dyc_ref, dlg_ref, dlb_ref, dgt_ref):
        @pl.when(pl.program_id(0) == 0)
        def _():
            dlg_ref[...] = jnp.zeros_like(dlg_ref)
            dlb_ref[...] = jnp.zeros_like(dlb_ref)
            dgt_ref[...] = jnp.zeros_like(dgt_ref)

        gate1 = 1.0 + mod_ref[:, 2 * D:3 * D]
        yv = y_ref[...]
        xn, rstd = _standardize(ALPHA * x_ref[...] + gate1 * yv)
        dxo = dxo_ref[...]
        dlg_ref[...] += jnp.sum(dxo * xn, axis=0, keepdims=True)
        dlb_ref[...] += jnp.sum(dxo, axis=0, keepdims=True)
        dr = _standardize_bwd(dxo * g_ref[layer:layer + 1, :], xn, rstd)
        dxr_ref[...] = ALPHA * dr
        dgt_ref[...] += jnp.sum(dr * yv, axis=0, keepdims=True)
        dyb = (gate1 * dr).astype(BF)
        dy_ref[...] = dyb
        dmerged = _dot_nt(dyb, wo_ref[...].reshape(D, D))
        ps = _branch_proj((ya_ref, yb_ref, yc_ref), wb_ref)
        quarter = D // N_CHIPS
        for i, (gate_ref, out_ref) in enumerate(((ga_ref, dya_ref), (gb_ref, dyb_ref), (gc_ref, dyc_ref))):
            sg = _sigmoid(gate_ref[...])
            dg_ref[:, i * D:(i + 1) * D] = (dmerged * ps[i] * sg * (1.0 - sg)).astype(BF)
            dp = (dmerged * sg).astype(BF)
            dp_ref[:, i * D:(i + 1) * D] = dp
            acc = None
            for j in range(N_CHIPS):
                t = _dot_nt(dp[:, j * quarter:(j + 1) * quarter], wb_ref[j, i])
                acc = t if acc is None else acc + t
            out_ref[...] = acc

    row = lambda w, c=0: pl.BlockSpec((tm, w), lambda i, c=c: (i, c))
    whole = lambda a: pl.BlockSpec(a.shape, lambda i, n=a.ndim: (0,) * n)
    vec = pl.BlockSpec((1, D), lambda i: (0, 0))
    sd = jax.ShapeDtypeStruct
    return pl.pallas_call(
        body, name="merge_bwd",
        grid=(S // tm,),
        in_specs=[row(D), row(D), row(D), whole(mod), row(D, 6), row(D, 7), row(D, 8), row(WIDTH), row(WIDTH), row(WIDTH), whole(wb),
                  whole(wo), whole(ln_g)],
        out_specs=[row(D), row(D), row(3 * D), row(3 * D), row(WIDTH), row(WIDTH), row(WIDTH), vec, vec, vec],
        out_shape=[sd((S, D), F32), sd((S, D), BF), sd((S, 3 * D), BF), sd((S, 3 * D), BF), sd((S, WIDTH), F32), sd((S, WIDTH), F32),
                   sd((S, WIDTH), F32), sd((1, D), F32), sd((1, D), F32), sd((1, D), F32)],
        compiler_params=_params(dimension_semantics=("arbitrary",)),
    )(dxo, x, y, mod, proj, proj, proj, *ys, wb, wo, ln_g)


def _loss_head(x, target, tm=512):
    S, D = x.shape
    tm = min(tm, S)

    def body(x_ref, t_ref, dx_ref, loss_ref):
        @pl.when(pl.program_id(0) == 0)
        def _():
            loss_ref[...] = jnp.zeros_like(loss_ref)

        err = x_ref[...] - t_ref[...]
        dx_ref[...] = err * (1.0 / D)
        loss_ref[...] += 0.5 * jnp.sum(jnp.mean(err * err, axis=-1, keepdims=True))

    row = pl.BlockSpec((tm, D), lambda i: (i, 0))
    return pl.pallas_call(
        body, name="loss_head",
        grid=(S // tm,),
        in_specs=[row, row],
        out_specs=[row, pl.BlockSpec((8, 128), lambda i: (0, 0))],
        out_shape=[jax.ShapeDtypeStruct((S, D), F32), jax.ShapeDtypeStruct((8, 128), F32)],
        compiler_params=_params(dimension_semantics=("arbitrary",)),
    )(x, target)


def _proj_bwd(dproj, wg, x, mod, dx_res, job=None, tm=512, tk=768):
    S, D = x.shape
    tm = min(tm, S)
    shard = wg.shape[-1]
    per = shard // tk
    nk = IN_COLS // tk

    def body(dp_ref, w_ref, x_ref, mod_ref, dxr_ref, dx_ref, dsh_ref, dsc_ref, acc):
        i, k = pl.program_id(0), pl.program_id(1)

        @pl.when((i == 0) & (k == 0))
        def _():
            dsh_ref[...] = jnp.zeros_like(dsh_ref)
            dsc_ref[...] = jnp.zeros_like(dsc_ref)

        @pl.when(k == 0)
        def _():
            acc[...] = jnp.zeros_like(acc)

        acc[...] += _dot_nt(dp_ref[...], w_ref[...])

        @pl.when(k == nk - 1)
        def _():
            dh = acc[...]
            xs, rstd = _standardize(x_ref[...])
            dsh_ref[...] += jnp.sum(dh, axis=0, keepdims=True)
            dsc_ref[...] += jnp.sum(dh * xs, axis=0, keepdims=True)
            dx_ref[...] = _standardize_bwd(dh * (1.0 + mod_ref[:, D:2 * D]), xs, rstd) + dxr_ref[...]

    row = pl.BlockSpec((tm, D), lambda i, k: (i, 0))
    vec = pl.BlockSpec((1, D), lambda i, k: (0, 0))
    return _hosted(
        body, job, (dproj, wg, x, mod, dx_res), name="proj_bwd",
        grid=(S // tm, nk),
        in_specs=[pl.BlockSpec((tm, tk), lambda i, k: (i, k)),
                  pl.BlockSpec((None, D, tk), lambda i, k: (k // per, 0, k % per)),
                  row, pl.BlockSpec((1, 3 * D), lambda i, k: (0, 0)), row],
        out_specs=[row, vec, vec],
        out_shape=[jax.ShapeDtypeStruct((S, D), F32), jax.ShapeDtypeStruct((1, D), F32), jax.ShapeDtypeStruct((1, D), F32)],
        scratch_shapes=[pltpu.VMEM((tm, D), F32)],
        semantics=("arbitrary", "arbitrary"))


def _grad_w(a, b, mid, prev, name, tm, tn, b_col0=0):
    S, M = a.shape
    shape = prev.shape
    n_shard = shape[-1]
    per = n_shard // tn
    nm = M // tm
    lead = len(mid)

    def body(*refs):
        a_ref, b_ref, o_ref = refs[0], refs[1], refs[-1]
        o_ref[...] = _dot_tn(a_ref[...], b_ref[...]).astype(BF)

    in_specs = [pl.BlockSpec((S, tm), lambda m, n: (0, m)),
                pl.BlockSpec((S, tn), lambda m, n: (0, b_col0 // tn + n))]
    args = [a, b]
    aliases = {}
    if not isinstance(prev, jax.ShapeDtypeStruct):
        in_specs.append(pl.BlockSpec(memory_space=pl.ANY))
        args.append(prev)
        aliases = {2: 0}
    return pl.pallas_call(
        body, name=name,
        grid=(nm, N_CHIPS * per),
        in_specs=in_specs,
        out_specs=pl.BlockSpec((None,) + (None,) * lead + (tm, tn), lambda m, n: (n // per,) + tuple(mid) + (m, n % per)),
        out_shape=jax.ShapeDtypeStruct(shape, BF),
        input_output_aliases=aliases,
        compiler_params=_params(dimension_semantics=("parallel", "parallel")),
    )(*args)


def _all_gather8(x, name):
    R, N = x.shape

    def body(x_ref, out_ref, send_sems, recv_sems):
        mx, my, mc = lax.axis_index("x"), lax.axis_index("y"), lax.axis_index("c")
        me = 4 * mx + 2 * my + mc
        out_ref[me] = x_ref[...]
        copies = []
        for k in range(1, N_DEV):
            peer = (_flip(mx, k & 4), _flip(my, k & 2), _flip(mc, k & 1))
            cp = pltpu.make_async_remote_copy(src_ref=x_ref, dst_ref=out_ref.at[me], send_sem=send_sems.at[k - 1],
                                              recv_sem=recv_sems.at[k - 1], device_id=peer, device_id_type=MESH)
            cp.start()
            copies.append(cp)
        for cp in copies:
            cp.wait()

    return pl.pallas_call(
        body, name=name,
        in_specs=[pl.BlockSpec(memory_space=pltpu.VMEM)],
        out_specs=pl.BlockSpec(memory_space=pltpu.VMEM),
        out_shape=jax.ShapeDtypeStruct((N_DEV, R, N), F32),
        scratch_shapes=[pltpu.SemaphoreType.DMA((N_DEV - 1,)), pltpu.SemaphoreType.DMA((N_DEV - 1,))],
        compiler_params=_params(),
    )(x)


def _rows2d(a):
    return a.reshape(-1, a.shape[-1])


def _tile_rows(rows, cols, n_arrays):
    budget = (24 << 20) // (n_arrays * 2 * 4 * cols)
    if rows <= budget:
        return rows
    tm = 8
    for cand in range(8, budget + 1, 8):
        if rows % cand == 0:
            tm = cand
    return tm


SUM_ROWS = 256


def _sum_cores(g, sent, where):
    chips, lead, _, r, cols = g.shape
    tr = min(r, SUM_ROWS)

    def body(where_ref, g_ref, s_ref, out_ref):
        out_ref[...] = (g_ref[...].astype(F32) + s_ref[...].astype(F32)).astype(BF)

    spec = pl.BlockSpec((None, tr, cols), lambda i, j, where_ref: (i, j, 0))
    out = pl.pallas_call(
        body, name="sum_cores",
        grid_spec=pltpu.PrefetchScalarGridSpec(
            num_scalar_prefetch=1, grid=(chips * lead, r // tr),
            in_specs=[pl.BlockSpec((None, None, tr, cols), lambda i, j, where_ref: (i, where_ref[1], j, 0)), spec],
            out_specs=spec),
        out_shape=jax.ShapeDtypeStruct((chips * lead, r, cols), BF),
        compiler_params=_params(dimension_semantics=("parallel", "parallel")),
    )(where, g.reshape(chips * lead, 2, r, cols), sent.reshape(chips * lead, r, cols))
    return out.reshape(chips, lead, r, cols)


def _sum_chips(sums, got, where):
    _, lead, r, cols = sums.shape
    tr = min(r, SUM_ROWS)

    def body(where_ref, s_ref, g_ref, out_ref):
        out_ref[...] = ((s_ref[...].astype(F32) + g_ref[0].astype(F32)) + g_ref[1].astype(F32)) + g_ref[2].astype(F32)

    return pl.pallas_call(
        body, name="sum_chips",
        grid_spec=pltpu.PrefetchScalarGridSpec(
            num_scalar_prefetch=1, grid=(lead, r // tr),
            in_specs=[pl.BlockSpec((None, None, tr, cols), lambda i, j, where_ref: (where_ref[0], i, j, 0)),
                      pl.BlockSpec((N_CHIPS - 1, None, tr, cols), lambda i, j, where_ref: (0, i, j, 0))],
            out_specs=pl.BlockSpec((None, None, tr, cols), lambda i, j, where_ref: (i, where_ref[1], j, 0))),
        out_shape=jax.ShapeDtypeStruct((lead, 2, r, cols), F32),
        compiler_params=_params(dimension_semantics=("parallel", "parallel")),
    )(where, sums, got)


def _adamw(w, m, v, groups, name):
    shape = w.shape
    w2, m2, v2 = _rows2d(w), _rows2d(m), _rows2d(v)
    rows, cols = w2.shape
    ng = len(groups)
    n = len(groups[0])
    slab = rows // ng
    gs = [_rows2d(g) for grp in groups for g in grp]
    tm = _tile_rows(slab, cols, 7 + n)
    tiles = slab // tm
    c1 = 1.0 / (1.0 - ADAM_B1 ** ADAM_STEP)
    c2 = 1.0 / (1.0 - ADAM_B2 ** ADAM_STEP)

    def body(*refs):
        w_ref, m_ref, v_ref = refs[:3]
        g_refs = refs[3:3 + ng * n]
        go_ref, d_ref, mo_ref, vo_ref = refs[3 + ng * n:]
        which = pl.program_id(0)
        for s in range(ng):
            @pl.when(which == s)
            def _(s=s):
                g = g_refs[s * n][...]
                for r in g_refs[s * n + 1:(s + 1) * n]:
                    g = g + r[...]
                mn = ADAM_B1 * m_ref[...] + (1.0 - ADAM_B1) * g
                vn = ADAM_B2 * v_ref[...] + (1.0 - ADAM_B2) * (g * g)
                go_ref[...] = g
                mo_ref[...] = mn
                vo_ref[...] = vn
                d_ref[...] = -ADAM_LR * ((mn * c1) / (jnp.sqrt(vn * c2) + ADAM_EPS) + ADAM_WD * w_ref[...])

    spec = pl.BlockSpec((tm, cols), lambda s, i: (s * tiles + i, 0))
    g_specs = [pl.BlockSpec((tm, cols), lambda s, i, k=k: (jnp.where(s == k, i, jnp.where(s < k, 0, tiles - 1)), 0))
               for k in range(ng) for _ in range(n)]
    outs = pl.pallas_call(
        body, name=name,
        grid=(ng, tiles),
        in_specs=[spec] * 3 + g_specs,
        out_specs=[spec] * 4,
        out_shape=[jax.ShapeDtypeStruct((rows, cols), F32)] * 4,
        compiler_params=_params(dimension_semantics=("arbitrary", "arbitrary")),
    )(w2, m2, v2, *gs)
    return [o.reshape(shape) for o in outs]


def _lower_bounds(r0, r1):
    top = jnp.maximum(r0, r1)
    e0, e1 = jnp.exp(r0 - top), jnp.exp(r1 - top)
    p0, p1 = e0 / (e0 + e1), e1 / (e0 + e1)
    return p0 - p0, (p0 + p1) - p0


def _lbs_fwd(lb_raw):
    def body(lb_ref, out_ref):
        l0, l1 = _lower_bounds(lb_ref[0:1, :], lb_ref[1:2, :])
        out_ref[0:1, :] = l0
        out_ref[1:2, :] = l1

    return pl.pallas_call(body, name="lower_bounds", out_shape=jax.ShapeDtypeStruct(lb_raw.shape, F32), compiler_params=_params())(lb_raw)


def _mod_rows(c_all, w_mod, tn=768):
    _, D, cols = w_mod.shape

    def body(c_ref, w_ref, out_ref):
        out_ref[...] = _dot(c_ref[...].astype(BF), w_ref[...].astype(BF))

    return pl.pallas_call(
        body, name="mod_rows",
        grid=(DEPTH,),
        in_specs=[pl.BlockSpec((N_DEV, D), lambda l: (0, 0)), pl.BlockSpec((None, D, cols), lambda l: (l, 0, 0))],
        out_specs=pl.BlockSpec((N_DEV, cols), lambda l: (0, l)),
        out_shape=jax.ShapeDtypeStruct((N_DEV, DEPTH * cols), F32),
        compiler_params=_params(dimension_semantics=("parallel",)),
    )(c_all, w_mod)


def _grad_w_mod(c_all, dmod_cols):
    D = c_all.shape[1]
    cols = dmod_cols.shape[-1]

    def body(c_ref, d_ref, out_ref):
        out_ref[...] = _dot_tn(c_ref[...].astype(BF), d_ref[...].astype(BF))

    return pl.pallas_call(
        body, name="grad_w_mod",
        grid=(DEPTH,),
        in_specs=[pl.BlockSpec((N_DEV, D), lambda l: (0, 0)), pl.BlockSpec((None, N_DEV, cols), lambda l: (l, 0, 0))],
        out_specs=pl.BlockSpec((None, D, cols), lambda l: (l, 0, 0)),
        out_shape=jax.ShapeDtypeStruct((DEPTH, D, cols), F32),
        compiler_params=_params(dimension_semantics=("parallel",)),
    )(c_all, dmod_cols)


def _sum_devices(parts):
    _, R, N = parts.shape

    def body(p_ref, out_ref):
        acc = p_ref[0]
        for d in range(1, N_DEV):
            acc = acc + p_ref[d]
        out_ref[...] = acc

    return pl.pallas_call(body, name="sum_devices", out_shape=jax.ShapeDtypeStruct((R, N), F32), compiler_params=_params())(parts)


def _lbs_bwd(lb_raw, dl):
    def body(lb_ref, dl_ref, out_ref):
        _, pull = jax.vjp(_lower_bounds, lb_ref[0:1, :], lb_ref[1:2, :])
        d0, d1 = pull((dl_ref[0:1, :], dl_ref[1:2, :]))
        out_ref[0:1, :] = d0
        out_ref[1:2, :] = d1

    return pl.pallas_call(body, name="lower_bounds_bwd", out_shape=jax.ShapeDtypeStruct(lb_raw.shape, F32), compiler_params=_params())(lb_raw, dl)


def kernel(x, c, w_mod, b_mod, w_in, conv_w, hgrn_norm_w, lower_bounds, w_branch, w_out, ln_g, ln_b, loss_target, m_w_mod, m_b_mod, m_w_in, m_conv_w, m_hgrn_norm_w, m_lower_bounds, m_w_branch, m_w_out, m_ln_g, m_ln_b, v_w_mod, v_b_mod, v_w_in, v_conv_w, v_hgrn_norm_w, v_lower_bounds, v_w_branch, v_w_out, v_ln_g, v_ln_b):
    D = D_MODEL
    x0 = x[0]
    target = loss_target[0]
    S = x0.shape[0]
    mx, my, mc = lax.axis_index("x"), lax.axis_index("y"), lax.axis_index("c")
    chip = 2 * mx + my
    me = 2 * chip + mc
    mod_cols = 3 * D // N_CHIPS

    plan = _Plan(w_in.astype(BF), w_branch.astype(BF), w_out.astype(BF), chip, mc)
    plan.first()

    n_conv = DEPTH * 3 * (WIDTH // N_CHIPS)
    first = jnp.concatenate([c, conv_w.reshape(1, n_conv), jnp.zeros((1, 2 * D - D - n_conv), F32)], axis=1)
    first = _all_gather8(first.reshape(8, 2 * D // 8), "gather_c").reshape(N_DEV, 2 * D)
    c_all = first[:, :D]
    conv_all = first[:, D:D + n_conv].reshape(N_DEV, DEPTH, 3, WIDTH // N_CHIPS)[0::2]
    mod_part = _all_gather8(_mod_rows(c_all, w_mod), "gather_mod")[0::2]
    mod_part = lax.dynamic_index_in_dim(mod_part, me, axis=1, keepdims=False).reshape(N_CHIPS, DEPTH, mod_cols)
    mods = [(mod_part[:, l].reshape(1, 3 * D) + b_mod[l][None, :]) for l in range(DEPTH)]
    lbs = _lbs_fwd(lower_bounds).reshape(DEPTH, WIDTH // 128, 1, 128)
    loss_blk, dx, small = _local_step(x0, target, mods, lbs, conv_all, hgrn_norm_w, ln_g, ln_b, plan)

    n_mod, n_nw, n_lb, n_ln, n_cw = DEPTH * 3 * D, DEPTH * 128, DEPTH * WIDTH, DEPTH * D, DEPTH * 3 * WIDTH
    row = jnp.concatenate(
        [jnp.concatenate([small[l][0], small[l][1], small[l][2]], axis=1) for l in range(DEPTH)]
        + [jnp.sum(small[l][3], axis=0) for l in range(DEPTH)]
        + [small[l][4].reshape(1, WIDTH) for l in range(DEPTH)]
        + [small[l][5] for l in range(DEPTH)] + [small[l][6] for l in range(DEPTH)]
        + [jnp.transpose(small[l][7], (1, 0, 2)).reshape(1, 3 * WIDTH) for l in range(DEPTH)]
        + [loss_blk[0:1, :]], axis=1)
    n_row = row.shape[1]
    fold = -(-n_row // (8 * 128)) * 128
    rows = jnp.concatenate([row, jnp.zeros((1, 8 * fold - n_row), F32)], axis=1).reshape(8, fold)

    whole, gathered = plan.finish(rows)
    grads = {kind: [[whole[(kind, l)]] for l in range(DEPTH)] for kind in ("in", "br", "out")}

    off_nw = n_mod
    off_lb = off_nw + n_nw
    off_lng = off_lb + n_lb
    off_lnb = off_lng + n_ln
    off_cw = off_lnb + n_ln
    off_loss = off_cw + n_cw
    total = _sum_devices(gathered).reshape(1, 8 * fold)
    gathered = gathered.reshape(N_DEV, 1, 8 * fold)
    d_lower = _lbs_bwd(lower_bounds, total[0, off_lb:off_lng].reshape(DEPTH, WIDTH))
    loss = total[0, off_loss]
    d_b_mod = total[0, :n_mod].reshape(DEPTH, 3 * D)
    d_norm_w = total[0, off_nw:off_lb].reshape(DEPTH, 128)
    d_ln_g = total[0, off_lng:off_lnb].reshape(DEPTH, D)
    d_ln_b = total[0, off_lnb:off_cw].reshape(DEPTH, D)
    d_conv = total[0, off_cw:off_loss].reshape(DEPTH, 3, N_CHIPS, WIDTH // N_CHIPS)
    d_conv = lax.dynamic_index_in_dim(d_conv, chip, axis=2, keepdims=False)
    dmod_all = gathered[:, 0, :n_mod].reshape(N_DEV, DEPTH, N_CHIPS, mod_cols)
    dmod_cols = jnp.transpose(lax.dynamic_index_in_dim(dmod_all, chip, axis=2, keepdims=False), (1, 0, 2))
    d_w_mod = _grad_w_mod(c_all, dmod_cols)

    res = {}
    res["w_mod"] = _adamw(w_mod, m_w_mod, v_w_mod, [[d_w_mod]], "adamw_w_mod")
    res["b_mod"] = _adamw(b_mod, m_b_mod, v_b_mod, [[d_b_mod]], "adamw_b_mod")
    res["w_in"] = _adamw(w_in, m_w_in, v_w_in, grads["in"], "adamw_w_in")
    res["conv_w"] = _adamw(conv_w, m_conv_w, v_conv_w, [[d_conv]], "adamw_conv_w")
    res["hgrn_norm_w"] = _adamw(hgrn_norm_w, m_hgrn_norm_w, v_hgrn_norm_w, [[d_norm_w]], "adamw_norm_w")
    res["lower_bounds"] = _adamw(lower_bounds, m_lower_bounds, v_lower_bounds, [[d_lower]], "adamw_lower_bounds")
    res["w_branch"] = _adamw(w_branch, m_w_branch, v_w_branch, grads["br"], "adamw_w_branch")
    res["w_out"] = _adamw(w_out, m_w_out, v_w_out, grads["out"], "adamw_w_out")
    res["ln_g"] = _adamw(ln_g, m_ln_g, v_ln_g, [[d_ln_g]], "adamw_ln_g")
    res["ln_b"] = _adamw(ln_b, m_ln_b, v_ln_b, [[d_ln_b]], "adamw_ln_b")
    names = ["w_mod", "b_mod", "w_in", "conv_w", "hgrn_norm_w", "lower_bounds", "w_branch", "w_out", "ln_g", "ln_b"]
    return (loss, dx[None], *[res[n][0] for n in names], *[res[n][1] for n in names],
            *[res[n][2] for n in names], *[res[n][3] for n in names])


class _Plan:
    def __init__(self, w_in, w_br, w_out, chip, core):
        self.local = {"in": w_in, "br": w_br, "out": w_out}
        self.chip, self.where = chip, jnp.stack([chip, core]).astype(jnp.int32)
        self.gathered, self.grads, self.chip_sums, self.scattered, self.pending = {}, {}, {}, {}, {}

    def _slab(self, key):
        mine = _halves(self.local[key[0]][key[1]])
        return lax.dynamic_update_slice(lax.empty((N_CHIPS,) + mine.shape, mine.dtype), mine[None], (self.chip, 0, 0, 0, 0))

    def _gather(self, keys):
        return ("gather", keys), _gather_job([self._slab(key) for key in keys])

    def _to_sibling(self, keys):
        return ("to_sibling", keys), _to_sibling_job([_halves(self.grads[key], 1) for key in keys])

    def _scatter(self, keys):
        return ("scatter", keys), _scatter_job([self.chip_sums[key] for key in keys])

    def job(self, stage, l):
        parts = []
        if stage == "proj_fwd":
            parts = [self._gather([("br", l), ("out", l)])]
        elif stage == "attn_fwd" and l + 1 < DEPTH:
            parts = [self._gather([("in", l + 1)])]
        elif stage == "attn_bwd":
            parts = [self._to_sibling([("out", l), ("br", l)])] + ([self._scatter([("in", l + 1)])] if l + 1 < DEPTH else [])
        elif stage == "hgrn_bwd":
            parts = [self._scatter([("out", l), ("br", l)])]
        elif stage == "proj_bwd":
            parts = [self._to_sibling([("in", l)])] if l else [self._scatter([("in", 0)])]
        self.pending[(stage, l)] = [(tag, len(job.outs)) for tag, job in parts]
        return _join_jobs([job for _, job in parts])

    def done(self, stage, l, outs):
        if outs is None:
            return
        at = 0
        for (what, keys), n_outs in self.pending[(stage, l)]:
            mine, at = outs[at:at + n_outs], at + n_outs
            for n, key in enumerate(keys):
                if what == "gather":
                    self.gathered[key] = mine[n].reshape((N_CHIPS,) + self.local[key[0]].shape[1:])
                elif what == "to_sibling":
                    self.chip_sums[key] = _sum_cores(_halves(self.grads[key], 1), mine[n], self.where)
                else:
                    self.scattered[key] = mine[n]

    def first(self):
        tag, job = self._gather([("in", 0)])
        self.pending[("first", 0)] = [(tag, len(job.outs))]
        self.done("first", 0, _run_job(job, "gather_first"))

    def took(self, key, grad):
        self.grads[key] = grad
        if key == ("in", 0):
            tag, job = self._to_sibling([key])
            self.pending[("took", 0)] = [(tag, len(job.outs))]
            self.done("took", 0, _run_job(job, "to_sibling_last"))

    def finish(self, rows):
        keys = [(kind, l) for kind in ("in", "br", "out") for l in range(DEPTH)]
        halves = [_sum_chips(self.chip_sums[key], self.scattered[key], self.where) for key in keys]
        outs = _run_job(_join_jobs([_place_job(halves), _gather8_job(rows)]), "place_halves")
        return {key: w.reshape(self.grads[key].shape[1:]) for key, w in zip(keys, outs[:-1])}, outs[-1]


def _local_step(x0, target, mods, lbs, conv_all, hgrn_norm_w, ln_g, ln_b, plan):
    D = D_MODEL
    after, before = _attn_consts()
    hg_consts = _hgrn_consts()

    saved = []
    xl = x0
    for l in range(DEPTH):
        (proj, h), got = _proj_fwd(xl, mods[l], plan.gathered[("in", l)], plan.job("proj_fwd", l))
        plan.done("proj_fwd", l, got)
        (o_a, tot), got = _attn_fwd(proj, after, plan.job("attn_fwd", l))
        plan.done("attn_fwd", l, got)
        o_b, states = _hgrn_fwd(proj, lbs[l], hg_consts)
        ys = _branch_fwd(proj, o_a, o_b, hgrn_norm_w, conv_all, l)
        x_next, merged, y = _merge_fwd(xl, mods[l], proj, ys, plan.gathered[("br", l)], plan.gathered[("out", l)], ln_g, ln_b, l)
        saved.append((xl, proj, h, o_a, tot, o_b, states, ys, merged, y))
        xl = x_next
    dx, loss_blk = _loss_head(xl, target)

    small = [None] * DEPTH
    for l in reversed(range(DEPTH)):
        xin, proj, h, o_a, tot, o_b, states, ys, merged, y = saved[l]
        dx_res, dy, dp, dg, dy_a, dy_b, dy_c, dln_g, dln_b, dgate = _merge_bwd(
            dx, xin, y, mods[l], proj, ys, plan.gathered[("br", l)], plan.gathered[("out", l)], ln_g, l)
        plan.took(("out", l), _grad_w_out(merged, dy))
        g_br = jax.ShapeDtypeStruct((N_CHIPS, 3, WIDTH, D // N_CHIPS), BF)
        for i in range(3):
            g_br = _grad_w(ys[i], dp, (i,), g_br, "grad_w_branch", tm=WIDTH, tn=D // N_CHIPS, b_col0=i * D)
        plan.took(("br", l), g_br)
        d_oa, d_ob, dz_a, dz_b, dpre, dpost, du, dz_c, dnorm_w, dconv_w = _branch_bwd(proj, o_a, o_b, hgrn_norm_w, conv_all, dy_a, dy_b, dy_c, l)
        (dq_a, dk_a, dv_a), got = _attn_bwd(proj, d_oa, tot, after, before, plan.job("attn_bwd", l))
        plan.done("attn_bwd", l, got)
        (dq_b, df_b, di_b, dlb), got = _hgrn_bwd(proj, lbs[l], states, d_ob, hg_consts, plan.job("hgrn_bwd", l))
        plan.done("hgrn_bwd", l, got)
        dproj = jnp.concatenate([dq_a, dk_a, dv_a, dz_a, dq_b, df_b, di_b, dz_b, dpre, dpost, du, dz_c, dg], axis=1)
        plan.took(("in", l), _grad_w(h, dproj, (), jax.ShapeDtypeStruct((N_CHIPS, D, IN_COLS // N_CHIPS), BF), "grad_w_in", tm=512, tn=2304))
        (dx, dshift, dscale), got = _proj_bwd(dproj, plan.gathered[("in", l)], xin, mods[l], dx_res, plan.job("proj_bwd", l))
        plan.done("proj_bwd", l, got)
        small[l] = (dshift, dscale, dgate, dnorm_w, dlb, dln_g, dln_b, dconv_w)
    return loss_blk, dx, small


def _grad_w_out(merged, dy):
    S, D = merged.shape
    q = D // N_CHIPS

    def body(a_ref, b_ref, o_ref):
        o_ref[...] = _dot_tn(a_ref[...], b_ref[...]).astype(BF)

    return pl.pallas_call(
        body, name="grad_w_out",
        grid=(N_CHIPS,),
        in_specs=[pl.BlockSpec((S, q), lambda j: (0, j)), pl.BlockSpec((S, D), lambda j: (0, 0))],
        out_specs=pl.BlockSpec((None, q, D), lambda j: (j, 0, 0)),
        out_shape=jax.ShapeDtypeStruct((N_CHIPS, q, D), BF),
        compiler_params=_params(dimension_semantics=("parallel",)),
    )(merged, dy)
```

```python
import functools
import math

import numpy as np
import jax
import jax.numpy as jnp
from jax import lax
from jax.experimental import pallas as pl
from jax.experimental.pallas import tpu as pltpu

F32 = jnp.float32
BF = jnp.bfloat16
MESH = pl.DeviceIdType.MESH

DEPTH = 2
D_MODEL = 1024
WIDTH = 512
IN_COLS = 12 * WIDTH + 3 * D_MODEL
N_CHIPS = 4
N_DEV = 8
SB_BLOCK = 128
SB_HEAD_DIM = 64
HG_CHUNK = 64
HG_DIM = 128
LN_EPS = 1e-5
RMS_EPS = 1e-6
ALPHA = (2.0 * DEPTH) ** 0.25
ADAM_LR, ADAM_B1, ADAM_B2, ADAM_EPS, ADAM_WD, ADAM_STEP = 0.001, 0.9, 0.999, 1e-08, 0.01, 10
VMEM_LIMIT = 56 << 20


def _params(**kw):
    return pltpu.CompilerParams(vmem_limit_bytes=VMEM_LIMIT, **kw)


def _dot(a, b):
    return jnp.dot(a, b, preferred_element_type=F32)


def _dot_nt(a, b):
    return lax.dot_general(a, b, (((1,), (1,)), ((), ())), preferred_element_type=F32)


def _dot_tn(a, b):
    return lax.dot_general(a, b, (((0,), (0,)), ((), ())), preferred_element_type=F32)


def _sigmoid(x):
    return 1.0 / (1.0 + jnp.exp(-x))


def _silu(x):
    return x * _sigmoid(x)


def _softplus(z):
    return jnp.maximum(z, 0.0) + jnp.log(1.0 + jnp.exp(-jnp.abs(z)))


def _split_dot(t, g, terms):
    acc = None
    rest = g
    for _ in range(terms):
        part = rest.astype(BF)
        rest = rest - part.astype(F32)
        d = _dot(t, part)
        acc = d if acc is None else acc + d
    return acc


def _split_dot_r(g, t, terms):
    acc = None
    rest = g
    for _ in range(terms):
        part = rest.astype(BF)
        rest = rest - part.astype(F32)
        d = _dot(part, t)
        acc = d if acc is None else acc + d
    return acc


def _standardize(x):
    mu = jnp.mean(x, axis=-1, keepdims=True)
    xc = x - mu
    var = jnp.mean(xc * xc, axis=-1, keepdims=True)
    rstd = lax.rsqrt(var + LN_EPS)
    return xc * rstd, rstd


def _standardize_bwd(dxs, xs, rstd):
    return rstd * (dxs - jnp.mean(dxs, axis=-1, keepdims=True) - xs * jnp.mean(dxs * xs, axis=-1, keepdims=True))


class _Job:
    def __init__(self, ins, outs, sems, make, alias=None):
        self.ins, self.outs, self.sems, self.make = list(ins), list(outs), list(sems), make
        self.alias = dict(alias or {})


def _join_jobs(jobs):
    jobs = [j for j in jobs if j is not None]
    if len(jobs) <= 1:
        return jobs[0] if jobs else None

    def make(ins, outs, sems):
        phases, i, o, s = [], 0, 0, 0
        for j in jobs:
            got = j.make(ins[i:i + len(j.ins)], outs[o:o + len(j.outs)], sems[s:s + len(j.sems)])
            i, o, s = i + len(j.ins), o + len(j.outs), s + len(j.sems)
            for n, phase in enumerate(got):
                if n == len(phases):
                    phases.append([])
                phases[n] += phase
        return phases

    alias, i, o = {}, 0, 0
    for j in jobs:
        alias.update({i + a: o + b for a, b in j.alias.items()})
        i, o = i + len(j.ins), o + len(j.outs)
    return _Job(sum([j.ins for j in jobs], []), sum([j.outs for j in jobs], []), sum([j.sems for j in jobs], []), make, alias)


def _flip(v, bit):
    return 1 - v if bit else v


def _halves(a, front=0):
    shape = a.shape
    lead = math.prod(shape[front:-2])
    return a.reshape(shape[:front] + (lead, 2, shape[-2] // 2, shape[-1]))


def _dma_sems(*shapes):
    return [pltpu.SemaphoreType.DMA(s) for s in shapes]


def _same(arrays):
    return [jax.ShapeDtypeStruct(a.shape, a.dtype) for a in arrays]


def _gather_job(slabs, window=None):
    n = len(slabs)
    cols = slice(None) if window is None else pl.ds(*window)

    def make(ins, outs, sems):
        send1, recv1, send2, recv2 = sems
        mx, my, mc = lax.axis_index("x"), lax.axis_index("y"), lax.axis_index("c")
        fetch, pass_on = [], []
        for a in range(n):
            ours = outs[a].at[2 * mx + my, :, mc, :, cols]
            for k in range(1, N_CHIPS):
                px, py = _flip(mx, k & 2), _flip(my, k & 1)
                fetch.append(pltpu.make_async_remote_copy(
                    src_ref=ours, dst_ref=ours, send_sem=send1.at[a, k - 1], recv_sem=recv1.at[a, k - 1],
                    device_id=(px, py, mc), device_id_type=MESH))
                theirs = outs[a].at[2 * px + py, :, mc, :, cols]
                pass_on.append(pltpu.make_async_remote_copy(
                    src_ref=theirs, dst_ref=theirs, send_sem=send2.at[a, k - 1], recv_sem=recv2.at[a, k - 1],
                    device_id=(mx, my, 1 - mc), device_id_type=MESH))
        return [fetch, pass_on]

    pairs = (n, N_CHIPS - 1)
    return _Job(slabs, _same(slabs), _dma_sems(pairs, pairs, pairs, pairs), make, {a: a for a in range(n)})


def _to_sibling_job(grads):
    n = len(grads)

    def make(ins, outs, sems):
        send_sems, recv_sems = sems
        mx, my, mc = lax.axis_index("x"), lax.axis_index("y"), lax.axis_index("c")
        return [[pltpu.make_async_remote_copy(
            src_ref=ins[a].at[:, :, 1 - mc], dst_ref=outs[a], send_sem=send_sems.at[a], recv_sem=recv_sems.at[a],
            device_id=(mx, my, 1 - mc), device_id_type=MESH) for a in range(n)]]

    outs = [jax.ShapeDtypeStruct(g.shape[:2] + g.shape[3:], g.dtype) for g in grads]
    return _Job(grads, outs, _dma_sems((n,), (n,)), make)


def _scatter_job(sums):
    n = len(sums)

    def make(ins, outs, sems):
        send_sems, recv_sems = sems
        mx, my, mc = lax.axis_index("x"), lax.axis_index("y"), lax.axis_index("c")
        copies = []
        for a in range(n):
            for k in range(1, N_CHIPS):
                px, py = _flip(mx, k & 2), _flip(my, k & 1)
                copies.append(pltpu.make_async_remote_copy(
                    src_ref=ins[a].at[2 * px + py], dst_ref=outs[a].at[k - 1], send_sem=send_sems.at[a, k - 1],
                    recv_sem=recv_sems.at[a, k - 1], device_id=(px, py, mc), device_id_type=MESH))
        return [copies]

    pairs = (n, N_CHIPS - 1)
    return _Job(sums, [jax.ShapeDtypeStruct((N_CHIPS - 1,) + s.shape[1:], s.dtype) for s in sums], _dma_sems(pairs, pairs), make)


def _place_job(wholes):
    n = len(wholes)

    def make(ins, outs, sems):
        send_sems, recv_sems = sems
        mx, my, mc = lax.axis_index("x"), lax.axis_index("y"), lax.axis_index("c")
        copies = []
        for a in range(n):
            here = outs[a].at[:, mc]
            copies.append(pltpu.make_async_remote_copy(src_ref=here, dst_ref=here, send_sem=send_sems.at[a], recv_sem=recv_sems.at[a],
                                                       device_id=(mx, my, 1 - mc), device_id_type=MESH))
        return [copies]

    return _Job(wholes, _same(wholes), _dma_sems((n,), (n,)), make, {a: a for a in range(n)})


def _gather8_job(x):
    def make(ins, outs, sems):
        local_sem, send_sems, recv_sems = sems
        mx, my, mc = lax.axis_index("x"), lax.axis_index("y"), lax.axis_index("c")
        here = outs[0].at[4 * mx + 2 * my + mc]
        copies = [pltpu.make_async_copy(ins[0], here, local_sem.at[0])]
        for k in range(1, N_DEV):
            peer = (_flip(mx, k & 4), _flip(my, k & 2), _flip(mc, k & 1))
            copies.append(pltpu.make_async_remote_copy(src_ref=ins[0], dst_ref=here, send_sem=send_sems.at[k - 1],
                                                       recv_sem=recv_sems.at[k - 1], device_id=peer, device_id_type=MESH))
        return [copies]

    return _Job([x], [jax.ShapeDtypeStruct((N_DEV,) + x.shape, x.dtype)], _dma_sems((1,), (N_DEV - 1,), (N_DEV - 1,)), make)


def _run_phases(phases, first=0):
    for n, phase in enumerate(phases):
        if n >= first:
            for cp in phase:
                cp.start()
        for cp in phase:
            cp.wait()


def _run_job(job, name):
    k_in, k_out = len(job.ins), len(job.outs)

    def body(*refs):
        _run_phases(job.make(refs[:k_in], refs[k_in:k_in + k_out], refs[k_in + k_out:]))

    hbm = pl.BlockSpec(memory_space=pl.ANY)
    return pl.pallas_call(body, name=name, in_specs=[hbm] * k_in, out_specs=[hbm] * k_out, out_shape=job.outs,
                          scratch_shapes=job.sems, input_output_aliases=job.alias, compiler_params=_params())(*job.ins)


def _hosted(body, job, args, *, name, grid, in_specs, out_specs, out_shape, scratch_shapes=(), semantics, aliases=None):
    in_specs, out_specs, out_shape, scratch = list(in_specs), list(out_specs), list(out_shape), list(scratch_shapes)
    aliases = dict(aliases or {})
    if job is None:
        outs = pl.pallas_call(body, name=name, grid=grid, in_specs=in_specs, out_specs=out_specs, out_shape=out_shape,
                              scratch_shapes=scratch, input_output_aliases=aliases,
                              compiler_params=_params(dimension_semantics=semantics))(*args)
        return list(outs), None
    n_in, n_out, n_scr, k_in, k_out = len(in_specs), len(out_specs), len(scratch), len(job.ins), len(job.outs)

    def wrapped(*refs):
        ins, rest = refs[:n_in], refs[n_in:]
        job_ins, rest = rest[:k_in], rest[k_in:]
        outs, rest = rest[:n_out], rest[n_out:]
        job_outs, rest = rest[:k_out], rest[k_out:]
        scr, sems = rest[:n_scr], rest[n_scr:]
        ids = [pl.program_id(a) for a in range(len(grid))]
        first = functools.reduce(jnp.logical_and, [i == 0 for i in ids])
        last = functools.reduce(jnp.logical_and, [i == g - 1 for i, g in zip(ids, grid)])

        @pl.when(first)
        def _():
            for cp in job.make(job_ins, job_outs, sems)[0]:
                cp.start()

        body(*ins, *outs, *scr)

        @pl.when(last)
        def _():
            _run_phases(job.make(job_ins, job_outs, sems), first=1)

    hbm = pl.BlockSpec(memory_space=pl.ANY)
    outs = pl.pallas_call(
        wrapped, name=name, grid=grid, in_specs=in_specs + [hbm] * k_in, out_specs=out_specs + [hbm] * k_out,
        out_shape=out_shape + job.outs, scratch_shapes=scratch + job.sems,
        input_output_aliases={**aliases, **{n_in + i: n_out + o for i, o in job.alias.items()}},
        compiler_params=_params(dimension_semantics=("arbitrary",) * len(grid)))(*args, *job.ins)
    return list(outs[:n_out]), list(outs[n_out:])


def _proj_fwd(x, mod, wg, job=None, chunk=(0, 1), tm=512):
    S, D = x.shape
    tm = min(tm, S)
    tn = wg.shape[-1]
    c, n = chunk

    def body(x_ref, mod_ref, w_ref, proj_ref, h_ref, hs):
        @pl.when(pl.program_id(1) == 0)
        def _():
            xs, _ = _standardize(x_ref[...])
            h = xs * (1.0 + mod_ref[:, D:2 * D]) + mod_ref[:, 0:D]
            hb = h.astype(BF)
            hs[...] = hb
            h_ref[...] = hb

        proj_ref[...] = _dot(hs[...], w_ref[...])

    return _hosted(
        body, job, (x, mod, wg), name="proj_fwd",
        grid=(S // tm, N_CHIPS),
        in_specs=[pl.BlockSpec((tm, D), lambda i, j: (i, 0)),
                  pl.BlockSpec((1, 3 * D), lambda i, j: (0, 0)),
                  pl.BlockSpec((None, D, tn), lambda i, j: (j, 0, 0))],
        out_specs=[pl.BlockSpec((tm, tn), lambda i, j: (i, j * n + c)),
                   pl.BlockSpec((tm, D), lambda i, j: (i, 0))],
        out_shape=[jax.ShapeDtypeStruct((S, IN_COLS), F32), jax.ShapeDtypeStruct((S, D), BF)],
        scratch_shapes=[pltpu.VMEM((tm, D), BF)],
        semantics=("parallel", "arbitrary"))


def _proj_cols(h, wg, chunk, proj, job=None, tm=512):
    S, D = h.shape
    tm = min(tm, S)
    tn = wg.shape[-1]
    c, n = chunk

    def body(h_ref, w_ref, prev_ref, proj_ref):
        proj_ref[...] = _dot(h_ref[...], w_ref[...])

    in_specs = [pl.BlockSpec((tm, D), lambda i, j: (i, 0)), pl.BlockSpec((None, D, tn), lambda i, j: (j, 0, 0)),
                pl.BlockSpec(memory_space=pl.ANY)]
    return _hosted(body, job, (h, wg, proj), name="proj_cols", grid=(S // tm, N_CHIPS), in_specs=in_specs,
                   out_specs=[pl.BlockSpec((tm, tn), lambda i, j: (i, j * n + c))],
                   out_shape=[jax.ShapeDtypeStruct(proj.shape, proj.dtype)],
                   semantics=("parallel", "arbitrary"), aliases={2: 0})


SB_ROWS = 256
SB_KEYS = 256


def _attn_consts():
    j = np.arange(SB_KEYS)[:, None]
    s = np.arange(SB_KEYS)[None, :]
    from_here = np.concatenate([(j >= s), (j >= s)], axis=0).astype(np.float32)
    up_to = np.concatenate([(j <= s), (j <= s)], axis=0).astype(np.float32)
    return jnp.asarray(from_here, BF), jnp.asarray(up_to, BF)


def _hi_lo(x):
    hi = lax.bitcast_convert_type(lax.bitcast_convert_type(x, jnp.uint32) & jnp.uint32(0xFFFF0000), F32)
    return hi.astype(BF), (x - hi).astype(BF)


def _sums_r(x, t2):
    hi, lo = _hi_lo(x)
    return _dot(jnp.concatenate([hi, lo], axis=1), t2)


def _all_lanes(col, lanes):
    return jnp.broadcast_to(col, (col.shape[0], lanes))


def _attn_rows(ref, r0, rows, lanes, head0, scale=None):
    v = ref[pl.ds(r0, rows), lanes]
    if scale is not None:
        v = v * scale
    return jnp.concatenate([jnp.where(head0, v, 0.0), jnp.where(head0, 0.0, v)], axis=0).astype(BF)


SB_PAIRS = 2
SB_LANES = SB_PAIRS * SB_BLOCK


def _attn_specs(S):
    return lambda base: pl.BlockSpec((S, SB_LANES), lambda p, base=base: (0, base // SB_PAIRS + p))


def _attn_scores(q2n, k_ref, lanes, kj, t2, from_here_ref, masked):
    c0 = pl.multiple_of(kj * SB_KEYS, SB_KEYS)
    kb = k_ref[pl.ds(c0, SB_KEYS), lanes].astype(BF)
    zn = _dot_nt(q2n, kb)
    lsb = jnp.minimum(zn, 0.0) - jnp.log(1.0 + jnp.exp(-jnp.abs(zn)))
    valid = None
    if masked:
        valid = (lax.broadcasted_iota(jnp.int32, zn.shape, 1) + kj * SB_KEYS) < t2
        lsb = jnp.where(valid, lsb, 0.0)
    return c0, kb, zn, valid, lsb, _sums_r(lsb, from_here_ref[...])


def _attn_fwd(proj, from_here, job=None):
    S = proj.shape[0]
    TQ = SB_ROWS
    assert S % TQ == 0 and SB_KEYS == TQ
    scale = SB_HEAD_DIM ** -0.5
    pairs = range(SB_PAIRS)
    lanes = [pl.ds(p * SB_BLOCK, SB_BLOCK) for p in pairs]

    def body(q_ref, k_ref, v_ref, from_here_ref, o_ref, tot_ref, run, acc):
        head0 = lax.broadcasted_iota(jnp.int32, (1, 2 * SB_HEAD_DIM), 1) < SB_HEAD_DIM

        def qloop(qi, _):
            r0 = pl.multiple_of(qi * TQ, TQ)
            q2n = [_attn_rows(q_ref, r0, TQ, lanes[p], head0, -scale) for p in pairs]
            trow = lax.broadcasted_iota(jnp.int32, (TQ, SB_KEYS), 0) + qi * TQ
            t2 = jnp.concatenate([trow, trow], axis=0)
            run[...] = jnp.zeros_like(run)
            acc[...] = jnp.zeros_like(acc)

            def step(kj, masked):
                got = [_attn_scores(q2n[p], k_ref, lanes[p], kj, t2, from_here_ref, masked) for p in pairs]
                for p in pairs:
                    c0, _, zn, valid, _, sums = got[p]
                    r = run[p]
                    e = sums - zn + jnp.concatenate([r, r], axis=1)
                    if masked:
                        e = jnp.where(valid, e, -jnp.inf)
                    acc[p] += _dot(jnp.exp(e).astype(BF), v_ref[pl.ds(c0, SB_KEYS), lanes[p]].astype(BF))
                    run[p] = r + _all_lanes(sums[:, 0:1], SB_BLOCK)

            step(qi, True)

            def below(n, _):
                step(qi - 1 - n, False)
                return 0

            lax.fori_loop(0, qi, below, 0)
            for p in pairs:
                o_ref[pl.ds(r0, TQ), lanes[p]] = jnp.where(head0, acc[p, 0:TQ, :], acc[p, TQ:2 * TQ, :])
                tot_ref[p, 0, pl.ds(r0, TQ), :] = run[p, 0:TQ, :]
                tot_ref[p, 1, pl.ds(r0, TQ), :] = run[p, TQ:2 * TQ, :]
            return 0

        lax.fori_loop(0, S // TQ, qloop, 0)

    col = _attn_specs(S)
    state = pltpu.VMEM((SB_PAIRS, 2 * TQ, SB_BLOCK), F32)
    return _hosted(
        body, job, (proj, proj, proj, from_here), name="attn_fwd",
        grid=(WIDTH // SB_LANES,),
        in_specs=[col(0), col(4), col(8), pl.BlockSpec(from_here.shape, lambda p: (0, 0))],
        out_specs=[col(0), pl.BlockSpec((SB_PAIRS, 2, S, 128), lambda p: (p, 0, 0, 0))],
        out_shape=[jax.ShapeDtypeStruct((S, WIDTH), F32), jax.ShapeDtypeStruct((WIDTH // 128, 2, S, 128), F32)],
        scratch_shapes=[state, state],
        semantics=("parallel",))


def _attn_bwd(proj, d_o, tot, from_here, up_to, job=None):
    S = proj.shape[0]
    TQ = SB_ROWS
    assert S % TQ == 0 and SB_KEYS == TQ
    scale = SB_HEAD_DIM ** -0.5
    pairs = range(SB_PAIRS)
    lanes = [pl.ds(p * SB_BLOCK, SB_BLOCK) for p in pairs]

    def body(q_ref, k_ref, v_ref, do_ref, tot_ref, from_here_ref, up_to_ref, dq_ref, dk_ref, dv_ref, pre, cum, dq_acc, dk_acc, dv_acc):
        head0 = lax.broadcasted_iota(jnp.int32, (1, 2 * SB_HEAD_DIM), 1) < SB_HEAD_DIM
        dk_acc[...] = jnp.zeros_like(dk_acc)
        dv_acc[...] = jnp.zeros_like(dv_acc)

        def qloop(qi, _):
            r0 = pl.multiple_of(qi * TQ, TQ)
            q2n = [_attn_rows(q_ref, r0, TQ, lanes[p], head0, -scale) for p in pairs]
            do2 = [_attn_rows(do_ref, r0, TQ, lanes[p], head0) for p in pairs]
            trow = lax.broadcasted_iota(jnp.int32, (TQ, SB_KEYS), 0) + qi * TQ
            t2 = jnp.concatenate([trow, trow], axis=0)
            for p in pairs:
                pre[p, 0:TQ, :] = tot_ref[p, 0, pl.ds(r0, TQ), :]
                pre[p, TQ:2 * TQ, :] = tot_ref[p, 1, pl.ds(r0, TQ), :]
            cum[...] = jnp.zeros_like(cum)
            dq_acc[...] = jnp.zeros_like(dq_acc)

            def step(kj, masked):
                got = [_attn_scores(q2n[p], k_ref, lanes[p], kj, t2, from_here_ref, masked) for p in pairs]
                for p in pairs:
                    c0, kb, zn, valid, lsb, sums = got[p]
                    later = pre[p] - _all_lanes(sums[:, 0:1], SB_BLOCK)
                    pre[p] = later
                    e = sums - zn + jnp.concatenate([later, later], axis=1)
                    sig = jnp.exp(lsb - zn)
                    if masked:
                        e = jnp.where(valid, e, -jnp.inf)
                        sig = jnp.where(valid, sig, 0.0)
                    a = jnp.exp(e)
                    w = _dot_nt(do2[p], v_ref[pl.ds(c0, SB_KEYS), lanes[p]].astype(BF)) * a
                    upto = _sums_r(w, up_to_ref[...])
                    c = cum[p]
                    dz = w - sig * (upto + jnp.concatenate([c, c], axis=1))
                    cum[p] = c + _all_lanes(upto[:, SB_KEYS - 1:SB_KEYS], SB_BLOCK)
                    dzb = dz.astype(BF)
                    dq_acc[p] += _dot(dzb, kb)
                    dk_acc[pl.ds(c0, SB_KEYS), lanes[p]] += _dot_tn(dzb, q2n[p])
                    dv_acc[pl.ds(c0, SB_KEYS), lanes[p]] += _dot_tn(a.astype(BF), do2[p])

            def below(kj, _):
                step(kj, False)
                return 0

            lax.fori_loop(0, qi, below, 0)
            step(qi, True)
            for p in pairs:
                dq_ref[pl.ds(r0, TQ), lanes[p]] = (jnp.where(head0, dq_acc[p, 0:TQ, :], dq_acc[p, TQ:2 * TQ, :]) * scale).astype(BF)
            return 0

        lax.fori_loop(0, S // TQ, qloop, 0)
        dk_ref[...] = (-dk_acc[...]).astype(BF)
        dv_ref[...] = dv_acc[...].astype(BF)

    col = _attn_specs(S)
    whole = lambda a: pl.BlockSpec(a.shape, lambda p: (0, 0))
    out = jax.ShapeDtypeStruct((S, WIDTH), BF)
    state = pltpu.VMEM((SB_PAIRS, 2 * TQ, SB_BLOCK), F32)
    grads = pltpu.VMEM((S, SB_LANES), F32)
    return _hosted(
        body, job, (proj, proj, proj, d_o, tot, from_here, up_to), name="attn_bwd",
        grid=(WIDTH // SB_LANES,),
        in_specs=[col(0), col(4), col(8), col(0), pl.BlockSpec((SB_PAIRS, 2, S, 128), lambda p: (p, 0, 0, 0)), whole(from_here), whole(up_to)],
        out_specs=[col(0), col(0), col(0)],
        out_shape=[out, out, out],
        scratch_shapes=[state, state, state, grads, grads],
        semantics=("parallel",))


HG_LEVELS = (32, 16, 8, 4, 2, 1)


def _hgrn_consts():
    C = HG_CHUNK
    t = np.arange(C)[:, None]
    s = np.arange(C)[None, :]
    rows = [(s <= t), (s > t)]
    masks = [(t == s)]
    for m in HG_LEVELS:
        two = 2 * m
        mid = (t // two) * two + m
        right = (t % two) >= m
        rows.append((right & (s >= mid) & (s <= t)) | ((~right) & (s > t) & (s <= mid - 1)))
        masks.append(((t // two) == (s // two)) & right & ((s % two) < m))
    tri = np.concatenate(rows, axis=0).astype(np.float32)
    return (jnp.asarray(tri, BF), jnp.asarray(tri.T.copy(), BF), jnp.asarray(np.stack(masks).astype(np.float32), F32))


HG_SUM_BLOCKS = 2 + len(HG_LEVELS)
HG_SUM_TERMS = 2


@jax.custom_vjp
def _hgrn_sums(tri, tri_t, g):
    C = HG_CHUNK
    e = _split_dot(tri, g, HG_SUM_TERMS)
    blocks = tuple(e[n * C:(n + 1) * C] for n in range(HG_SUM_BLOCKS))
    return blocks + (jnp.broadcast_to(e[C - 1:C], (HG_DIM, e.shape[1])),)


def _hgrn_sums_fwd(tri, tri_t, g):
    return _hgrn_sums(tri, tri_t, g), (tri, tri_t)


def _hgrn_sums_bwd(res, ds):
    tri, tri_t = res
    C = HG_CHUNK
    last = lax.broadcasted_iota(jnp.int32, (C, 1), 0) == C - 1
    prefix = ds[0] + jnp.where(last, jnp.sum(ds[-1], axis=0, keepdims=True), 0.0)
    d = jnp.concatenate((prefix,) + tuple(ds[1:-1]), axis=0)
    return jnp.zeros_like(tri), jnp.zeros_like(tri_t), _split_dot(tri_t, d, HG_SUM_TERMS)


_hgrn_sums.defvjp(_hgrn_sums_fwd, _hgrn_sums_bwd)


def _bf_dot(a, b):
    return _dot(a.astype(BF), b.astype(BF))


def _bf_dot_nt(a, b):
    return _dot_nt(a.astype(BF), b.astype(BF))


def _bf_dot_tn(a, b):
    return _dot_tn(a.astype(BF), b.astype(BF))


@jax.custom_vjp
def _mm(a, b):
    return _bf_dot(a, b)


_mm.defvjp(lambda a, b: (_bf_dot(a, b), (a, b)), lambda r, ct: (_bf_dot_nt(ct, r[1]), _bf_dot_tn(r[0], ct)))


@jax.custom_vjp
def _mm_nt(a, b):
    return _bf_dot_nt(a, b)


_mm_nt.defvjp(lambda a, b: (_bf_dot_nt(a, b), (a, b)), lambda r, ct: (_bf_dot(ct, r[1]), _bf_dot_tn(ct, r[0])))


@jax.custom_vjp
def _mm_tn(a, b):
    return _bf_dot_tn(a, b)


_mm_tn.defvjp(lambda a, b: (_bf_dot_tn(a, b), (a, b)), lambda r, ct: (_bf_dot_nt(r[1], ct), _bf_dot(r[0], ct)))


def _hgrn_chunk(tri, tri_t, masks, qraw, fpre, v, st, lb):
    q = _silu(qraw)
    f = lb + (1.0 - lb) * _sigmoid(fpre)
    k = 1.0 - f
    e = _hgrn_sums(tri, tri_t, jnp.log(f))
    prefix, suffix, whole = e[0], e[1], e[-1]
    scores = masks[0] * _mm_nt(q, k)
    for n in range(len(HG_LEVELS)):
        decay = jnp.exp(e[2 + n])
        scores = scores + masks[n + 1] * _mm_nt(q * decay, k * decay)
    o = _mm_nt(q * jnp.exp(prefix), st) + _mm(scores, v)
    st_new = st * jnp.exp(whole) + _mm_tn(v, k * jnp.exp(suffix))
    return o, st_new


HG_HEADS_PER_STEP = 2
HG_LANES = HG_HEADS_PER_STEP * HG_DIM


def _hgrn_specs(S, consts):
    col = lambda base: pl.BlockSpec((S, HG_LANES), lambda p, base=base: (0, base // HG_HEADS_PER_STEP + p))
    whole = [pl.BlockSpec(a.shape, lambda p, n=a.ndim: (0,) * n) for a in consts]
    return col, whole


def _hgrn_fwd(proj, lbs, consts, job=None):
    S = proj.shape[0]
    nc = S // HG_CHUNK
    heads = range(HG_HEADS_PER_STEP)

    def body(q_ref, f_ref, i_ref, lb_ref, tri_ref, trit_ref, mask_ref, o_ref, st_ref):
        tri, tri_t = tri_ref[...], trit_ref[...]
        masks = [mask_ref[n] for n in range(len(HG_LEVELS) + 1)]

        def chunk(ci, sts):
            r0 = pl.multiple_of(ci * HG_CHUNK, HG_CHUNK)
            rows = pl.ds(r0, HG_CHUNK)
            new = []
            for hd in heads:
                lanes = pl.ds(hd * HG_DIM, HG_DIM)
                st_ref[hd, ci] = sts[hd]
                o, st_new = _hgrn_chunk(tri, tri_t, masks, q_ref[rows, lanes], f_ref[rows, lanes], i_ref[rows, lanes], sts[hd], lb_ref[hd])
                o_ref[rows, lanes] = o
                new.append(st_new)
            return tuple(new)

        lax.fori_loop(0, nc, chunk, tuple(jnp.zeros((HG_DIM, HG_DIM), F32) for _ in heads))

    col, whole = _hgrn_specs(S, consts)
    return _hosted(
        body, job, (proj, proj, proj, lbs, *consts), name="hgrn_fwd",
        grid=(WIDTH // HG_LANES,),
        in_specs=[col(16), col(20), col(24), pl.BlockSpec((HG_HEADS_PER_STEP, 1, 128), lambda p: (p, 0, 0))] + whole,
        out_specs=[col(0), pl.BlockSpec((HG_HEADS_PER_STEP, nc, HG_DIM, HG_DIM), lambda p: (p, 0, 0, 0))],
        out_shape=[jax.ShapeDtypeStruct((S, WIDTH), F32), jax.ShapeDtypeStruct((WIDTH // 128, nc, HG_DIM, HG_DIM), F32)],
        semantics=("parallel",))


def _hgrn_bwd(proj, lbs, states, d_o, consts, job=None):
    S = proj.shape[0]
    nc = S // HG_CHUNK

    def body(q_ref, f_ref, i_ref, lb_ref, st_ref, do_ref, tri_ref, trit_ref, mask_ref, dq_ref, df_ref, di_ref, dlb_ref):
        masks = [mask_ref[n] for n in range(len(HG_LEVELS) + 1)]
        fn = functools.partial(_hgrn_chunk, tri_ref[...], trit_ref[...], masks)
        heads = range(HG_HEADS_PER_STEP)

        def chunk(n, carry):
            ci = nc - 1 - n
            r0 = pl.multiple_of(ci * HG_CHUNK, HG_CHUNK)
            rows = pl.ds(r0, HG_CHUNK)
            new = []
            for hd in heads:
                d_st, dlb = carry[hd]
                lanes = pl.ds(hd * HG_DIM, HG_DIM)
                _, pull = jax.vjp(fn, q_ref[rows, lanes], f_ref[rows, lanes], i_ref[rows, lanes], st_ref[hd, ci], lb_ref[hd])
                dq, df, di, d_prev, dl = pull((do_ref[rows, lanes], d_st))
                dq_ref[rows, lanes] = dq.astype(BF)
                df_ref[rows, lanes] = df.astype(BF)
                di_ref[rows, lanes] = di.astype(BF)
                new.append((d_prev, dlb + dl))
            return tuple(new)

        zero = (jnp.zeros((HG_DIM, HG_DIM), F32), jnp.zeros((1, HG_DIM), F32))
        done = lax.fori_loop(0, nc, chunk, tuple(zero for _ in heads))
        for hd in heads:
            dlb_ref[hd] = done[hd][1]

    col, whole = _hgrn_specs(S, consts)
    head = pl.BlockSpec((HG_HEADS_PER_STEP, 1, 128), lambda p: (p, 0, 0))
    out = jax.ShapeDtypeStruct((S, WIDTH), BF)
    return _hosted(
        body, job, (proj, proj, proj, lbs, states, d_o, *consts), name="hgrn_bwd",
        grid=(WIDTH // HG_LANES,),
        in_specs=[col(16), col(20), col(24), head, pl.BlockSpec((HG_HEADS_PER_STEP, nc, HG_DIM, HG_DIM), lambda p: (p, 0, 0, 0)), col(0)] + whole,
        out_specs=[col(0), col(0), col(0), head],
        out_shape=[out, out, out, jax.ShapeDtypeStruct((WIDTH // 128, 1, 128), F32)],
        semantics=("parallel",))


def _shift_down(x, n):
    rows = lax.broadcasted_iota(jnp.int32, x.shape, 0)
    return jnp.where(rows >= n, pltpu.roll(x, n, 0), 0.0)


def _shift_up(x, n):
    S = x.shape[0]
    rows = lax.broadcasted_iota(jnp.int32, x.shape, 0)
    return jnp.where(rows < S - n, pltpu.roll(x, S - n, 0), 0.0)


def _branch_fwd(proj, o_a, o_b, norm_w, conv_w, layer):
    S = proj.shape[0]

    def body(oa_ref, za_ref, ob_ref, zb_ref, nw_ref, pre_ref, post_ref, u_ref, zc_ref, cw_ref, ya_ref, yb_ref, yc_ref):
        ya_ref[...] = (oa_ref[...] * _silu(za_ref[...])).astype(BF)
        ob = ob_ref[...]
        rn = lax.rsqrt(jnp.mean(ob * ob, axis=-1, keepdims=True) + RMS_EPS)
        yb_ref[...] = (ob * rn * nw_ref[layer:layer + 1, :] * _silu(zb_ref[...])).astype(BF)
        pu = pre_ref[...] * u_ref[...]
        conv = cw_ref[2:3, :] * pu + cw_ref[1:2, :] * _shift_down(pu, 1) + cw_ref[0:1, :] * _shift_down(pu, 2)
        yc_ref[...] = (post_ref[...] * conv * _silu(zc_ref[...])).astype(BF)

    col = lambda base: pl.BlockSpec((S, 128), lambda p, base=base: (0, base + p))
    out = jax.ShapeDtypeStruct((S, WIDTH), BF)
    return pl.pallas_call(
        body, name="branch_fwd",
        grid=(WIDTH // 128,),
        in_specs=[col(0), col(12), col(0), col(28), pl.BlockSpec(norm_w.shape, lambda p: (0, 0)),
                  col(32), col(36), col(40), col(44), pl.BlockSpec((None, None, 3, 128), lambda p: (p, layer, 0, 0))],
        out_specs=[col(0), col(0), col(0)],
        out_shape=[out, out, out],
        compiler_params=_params(dimension_semantics=("parallel",)),
    )(o_a, proj, o_b, proj, norm_w, proj, proj, proj, proj, conv_w)


def _branch_bwd(proj, o_a, o_b, norm_w, conv_w, dy_a, dy_b, dy_c, layer):
    S = proj.shape[0]

    def dsilu(z):
        s = _sigmoid(z)
        return s * z, s * (1.0 + z * (1.0 - s))

    def body(oa_ref, za_ref, ob_ref, zb_ref, nw_ref, pre_ref, post_ref, u_ref, zc_ref, cw_ref, dya_ref, dyb_ref, dyc_ref,
             doa_ref, dob_ref, dza_ref, dzb_ref, dpre_ref, dpost_ref, du_ref, dzc_ref, dnw_ref, dcw_ref):
        dya = dya_ref[...]
        sa, dsa = dsilu(za_ref[...])
        doa_ref[...] = dya * sa
        dza_ref[...] = (dya * oa_ref[...] * dsa).astype(BF)

        dyb = dyb_ref[...]
        ob = ob_ref[...]
        nw = nw_ref[layer:layer + 1, :]
        sb, dsb = dsilu(zb_ref[...])
        rn = lax.rsqrt(jnp.mean(ob * ob, axis=-1, keepdims=True) + RMS_EPS)
        on = ob * rn
        dzb_ref[...] = (dyb * on * nw * dsb).astype(BF)
        don_w = dyb * sb
        dnw_ref[...] = jnp.sum(don_w * on, axis=0, keepdims=True)
        don = don_w * nw
        dob_ref[...] = rn * (don - on * jnp.mean(don * on, axis=-1, keepdims=True))

        dyc = dyc_ref[...]
        pre, post, u = pre_ref[...], post_ref[...], u_ref[...]
        sc, dsc = dsilu(zc_ref[...])
        pu = pre * u
        pu1, pu2 = _shift_down(pu, 1), _shift_down(pu, 2)
        conv = cw_ref[2:3, :] * pu + cw_ref[1:2, :] * pu1 + cw_ref[0:1, :] * pu2
        dzc_ref[...] = (dyc * post * conv * dsc).astype(BF)
        dpost_ref[...] = (dyc * conv * sc).astype(BF)
        dconv = dyc * post * sc
        dcw_ref[0:1, :] = jnp.sum(dconv * pu2, axis=0, keepdims=True)
        dcw_ref[1:2, :] = jnp.sum(dconv * pu1, axis=0, keepdims=True)
        dcw_ref[2:3, :] = jnp.sum(dconv * pu, axis=0, keepdims=True)
        dpu = cw_ref[2:3, :] * dconv + cw_ref[1:2, :] * _shift_up(dconv, 1) + cw_ref[0:1, :] * _shift_up(dconv, 2)
        dpre_ref[...] = (dpu * u).astype(BF)
        du_ref[...] = (dpu * pre).astype(BF)

    col = lambda base: pl.BlockSpec((S, 128), lambda p, base=base: (0, base + p))
    f32 = jax.ShapeDtypeStruct((S, WIDTH), F32)
    bf = jax.ShapeDtypeStruct((S, WIDTH), BF)
    return pl.pallas_call(
        body, name="branch_bwd",
        grid=(WIDTH // 128,),
        in_specs=[col(0), col(12), col(0), col(28), pl.BlockSpec(norm_w.shape, lambda p: (0, 0)),
                  col(32), col(36), col(40), col(44), pl.BlockSpec((None, None, 3, 128), lambda p: (p, layer, 0, 0)),
                  col(0), col(0), col(0)],
        out_specs=[col(0)] * 8 + [pl.BlockSpec((None, 1, 128), lambda p: (p, 0, 0)), pl.BlockSpec((None, 3, 128), lambda p: (p, 0, 0))],
        out_shape=[f32, f32, bf, bf, bf, bf, bf, bf, jax.ShapeDtypeStruct((WIDTH // 128, 1, 128), F32),
                   jax.ShapeDtypeStruct((WIDTH // 128, 3, 128), F32)],
        compiler_params=_params(dimension_semantics=("parallel",)),
    )(o_a, proj, o_b, proj, norm_w, proj, proj, proj, proj, conv_w, dy_a, dy_b, dy_c)


def _branch_proj(y_refs, wb_ref):
    out = []
    for i in range(3):
        yv = y_refs[i][...]
        out.append(jnp.concatenate([_dot(yv, wb_ref[j, i]) for j in range(N_CHIPS)], axis=1))
    return out


def _merge_fwd(x, mod, proj, ys, wb, wo, ln_g, ln_b, layer, tm=256):
    S, D = x.shape
    tm = min(tm, S)

    def body(x_ref, mod_ref, ga_ref, gb_ref, gc_ref, ya_ref, yb_ref, yc_ref, wb_ref, wo_ref, g_ref, b_ref, xo_ref, mg_ref, y_ref):
        ps = _branch_proj((ya_ref, yb_ref, yc_ref), wb_ref)
        merged = _sigmoid(ga_ref[...]) * ps[0] + _sigmoid(gb_ref[...]) * ps[1] + _sigmoid(gc_ref[...]) * ps[2]
        mb = merged.astype(BF)
        mg_ref[...] = mb
        y = _dot(mb, wo_ref[...].reshape(D, D))
        y_ref[...] = y
        r = ALPHA * x_ref[...] + (1.0 + mod_ref[:, 2 * D:3 * D]) * y
        xn, _ = _standardize(r)
        xo_ref[...] = xn * g_ref[layer:layer + 1, :] + b_ref[layer:layer + 1, :]

    row = lambda w, c=0: pl.BlockSpec((tm, w), lambda i, c=c: (i, c))
    whole = lambda a: pl.BlockSpec(a.shape, lambda i, n=a.ndim: (0,) * n)
    return pl.pallas_call(
        body, name="merge_fwd",
        grid=(S // tm,),
        in_specs=[row(D), whole(mod), row(D, 6), row(D, 7), row(D, 8), row(WIDTH), row(WIDTH), row(WIDTH), whole(wb),
                  whole(wo), whole(ln_g), whole(ln_b)],
        out_specs=[row(D), row(D), row(D)],
        out_shape=[jax.ShapeDtypeStruct((S, D), F32), jax.ShapeDtypeStruct((S, D), BF), jax.ShapeDtypeStruct((S, D), F32)],
        compiler_params=_params(dimension_semantics=("parallel",)),
    )(x, mod, proj, proj, proj, *ys, wb, wo, ln_g, ln_b)


def _merge_bwd(dxo, x, y, mod, proj, ys, wb, wo, ln_g, layer, tm=256):
    S, D = x.shape
    tm = min(tm, S)

    def body(dxo_ref, x_ref, y_ref, mod_ref, ga_ref, gb_ref, gc_ref, ya_ref, yb_ref, yc_ref, wb_ref, wo_ref, g_ref,
             dxr_ref, dy_ref, dp_ref, dg_ref, dya_ref, dyb_ref, dyc_ref, dlg_ref, dlb_ref, dgt_ref):
        @pl.when(pl.program_id(0) == 0)
        def _():
            dlg_ref[...] = jnp.zeros_like(dlg_ref)
            dlb_ref[...] = jnp.zeros_like(dlb_ref)
            dgt_ref[...] = jnp.zeros_like(dgt_ref)

        gate1 = 1.0 + mod_ref[:, 2 * D:3 * D]
        yv = y_ref[...]
        xn, rstd = _standardize(ALPHA * x_ref[...] + gate1 * yv)
        dxo = dxo_ref[...]
        dlg_ref[...] += jnp.sum(dxo * xn, axis=0, keepdims=True)
        dlb_ref[...] += jnp.sum(dxo, axis=0, keepdims=True)
        dr = _standardize_bwd(dxo * g_ref[layer:layer + 1, :], xn, rstd)
        dxr_ref[...] = ALPHA * dr
        dgt_ref[...] += jnp.sum(dr * yv, axis=0, keepdims=True)
        dyb = (gate1 * dr).astype(BF)
        dy_ref[...] = dyb
        dmerged = _dot_nt(dyb, wo_ref[...].reshape(D, D))
        ps = _branch_proj((ya_ref, yb_ref, yc_ref), wb_ref)
        quarter = D // N_CHIPS
        for i, (gate_ref, out_ref) in enumerate(((ga_ref, dya_ref), (gb_ref, dyb_ref), (gc_ref, dyc_ref))):
            sg = _sigmoid(gate_ref[...])
            dg_ref[:, i * D:(i + 1) * D] = (dmerged * ps[i] * sg * (1.0 - sg)).astype(BF)
            dp = (dmerged * sg).astype(BF)
            dp_ref[:, i * D:(i + 1) * D] = dp
            acc = None
            for j in range(N_CHIPS):
                t = _dot_nt(dp[:, j * quarter:(j + 1) * quarter], wb_ref[j, i])
                acc = t if acc is None else acc + t
            out_ref[...] = acc

    row = lambda w, c=0: pl.BlockSpec((tm, w), lambda i, c=c: (i, c))
    whole = lambda a: pl.BlockSpec(a.shape, lambda i, n=a.ndim: (0,) * n)
    vec = pl.BlockSpec((1, D), lambda i: (0, 0))
    sd = jax.ShapeDtypeStruct
    return pl.pallas_call(
        body, name="merge_bwd",
        grid=(S // tm,),
        in_specs=[row(D), row(D), row(D), whole(mod), row(D, 6), row(D, 7), row(D, 8), row(WIDTH), row(WIDTH), row(WIDTH), whole(wb),
                  whole(wo), whole(ln_g)],
        out_specs=[row(D), row(D), row(3 * D), row(3 * D), row(WIDTH), row(WIDTH), row(WIDTH), vec, vec, vec],
        out_shape=[sd((S, D), F32), sd((S, D), BF), sd((S, 3 * D), BF), sd((S, 3 * D), BF), sd((S, WIDTH), F32), sd((S, WIDTH), F32),
                   sd((S, WIDTH), F32), sd((1, D), F32), sd((1, D), F32), sd((1, D), F32)],
        compiler_params=_params(dimension_semantics=("arbitrary",)),
    )(dxo, x, y, mod, proj, proj, proj, *ys, wb, wo, ln_g)


def _loss_head(x, target, tm=512):
    S, D = x.shape
    tm = min(tm, S)

    def body(x_ref, t_ref, dx_ref, loss_ref):
        @pl.when(pl.program_id(0) == 0)
        def _():
            loss_ref[...] = jnp.zeros_like(loss_ref)

        err = x_ref[...] - t_ref[...]
        dx_ref[...] = err * (1.0 / D)
        loss_ref[...] += 0.5 * jnp.sum(jnp.mean(err * err, axis=-1, keepdims=True))

    row = pl.BlockSpec((tm, D), lambda i: (i, 0))
    return pl.pallas_call(
        body, name="loss_head",
        grid=(S // tm,),
        in_specs=[row, row],
        out_specs=[row, pl.BlockSpec((8, 128), lambda i: (0, 0))],
        out_shape=[jax.ShapeDtypeStruct((S, D), F32), jax.ShapeDtypeStruct((8, 128), F32)],
        compiler_params=_params(dimension_semantics=("arbitrary",)),
    )(x, target)


def _proj_bwd(dproj, wgs, x, mod, dx_res, job=None, tm=512, tk=768):
    S, D = x.shape
    tm = min(tm, S)
    n = len(wgs)
    if n == 1:
        n = wgs[0].shape[-1] // tk
        w_args = [wgs[0]] * n
        w_specs = [pl.BlockSpec((None, D, tk), lambda i, k, c=c: (k // n, 0, c)) for c in range(n)]
    else:
        assert wgs[0].shape[-1] == tk
        w_args = list(wgs)
        w_specs = [pl.BlockSpec((None, D, tk), lambda i, k: (k // n, 0, 0)) for _ in range(n)]
    nk = IN_COLS // tk

    def body(dp_ref, *rest):
        w_refs, (x_ref, mod_ref, dxr_ref, dx_ref, dsh_ref, dsc_ref, acc) = rest[:n], rest[n:]
        i, k = pl.program_id(0), pl.program_id(1)

        @pl.when((i == 0) & (k == 0))
        def _():
            dsh_ref[...] = jnp.zeros_like(dsh_ref)
            dsc_ref[...] = jnp.zeros_like(dsc_ref)

        @pl.when(k == 0)
        def _():
            acc[...] = jnp.zeros_like(acc)

        for c in range(n):
            @pl.when(k % n == c)
            def _(c=c):
                acc[...] += _dot_nt(dp_ref[...], w_refs[c][...])

        @pl.when(k == nk - 1)
        def _():
            dh = acc[...]
            xs, rstd = _standardize(x_ref[...])
            dsh_ref[...] += jnp.sum(dh, axis=0, keepdims=True)
            dsc_ref[...] += jnp.sum(dh * xs, axis=0, keepdims=True)
            dx_ref[...] = _standardize_bwd(dh * (1.0 + mod_ref[:, D:2 * D]), xs, rstd) + dxr_ref[...]

    row = pl.BlockSpec((tm, D), lambda i, k: (i, 0))
    vec = pl.BlockSpec((1, D), lambda i, k: (0, 0))
    return _hosted(
        body, job, (dproj, *w_args, x, mod, dx_res), name="proj_bwd",
        grid=(S // tm, nk),
        in_specs=[pl.BlockSpec((tm, tk), lambda i, k: (i, k))] + w_specs + [row, pl.BlockSpec((1, 3 * D), lambda i, k: (0, 0)), row],
        out_specs=[row, vec, vec],
        out_shape=[jax.ShapeDtypeStruct((S, D), F32), jax.ShapeDtypeStruct((1, D), F32), jax.ShapeDtypeStruct((1, D), F32)],
        scratch_shapes=[pltpu.VMEM((tm, D), F32)],
        semantics=("arbitrary", "arbitrary"))


def _grad_w(a, b, mid, prev, name, tm, tn, b_col0=0):
    S, M = a.shape
    shape = prev.shape
    n_shard = shape[-1]
    per = n_shard // tn
    nm = M // tm
    lead = len(mid)

    def body(*refs):
        a_ref, b_ref, o_ref = refs[0], refs[1], refs[-1]
        o_ref[...] = _dot_tn(a_ref[...], b_ref[...]).astype(BF)

    in_specs = [pl.BlockSpec((S, tm), lambda m, n: (0, m)),
                pl.BlockSpec((S, tn), lambda m, n: (0, b_col0 // tn + n))]
    args = [a, b]
    aliases = {}
    if not isinstance(prev, jax.ShapeDtypeStruct):
        in_specs.append(pl.BlockSpec(memory_space=pl.ANY))
        args.append(prev)
        aliases = {2: 0}
    return pl.pallas_call(
        body, name=name,
        grid=(nm, N_CHIPS * per),
        in_specs=in_specs,
        out_specs=pl.BlockSpec((None,) + (None,) * lead + (tm, tn), lambda m, n: (n // per,) + tuple(mid) + (m, n % per)),
        out_shape=jax.ShapeDtypeStruct(shape, BF),
        input_output_aliases=aliases,
        compiler_params=_params(dimension_semantics=("parallel", "parallel")),
    )(*args)


def _all_gather8(x, name):
    R, N = x.shape

    def body(x_ref, out_ref, send_sems, recv_sems):
        mx, my, mc = lax.axis_index("x"), lax.axis_index("y"), lax.axis_index("c")
        me = 4 * mx + 2 * my + mc
        out_ref[me] = x_ref[...]
        copies = []
        for k in range(1, N_DEV):
            peer = (_flip(mx, k & 4), _flip(my, k & 2), _flip(mc, k & 1))
            cp = pltpu.make_async_remote_copy(src_ref=x_ref, dst_ref=out_ref.at[me], send_sem=send_sems.at[k - 1],
                                              recv_sem=recv_sems.at[k - 1], device_id=peer, device_id_type=MESH)
            cp.start()
            copies.append(cp)
        for cp in copies:
            cp.wait()

    return pl.pallas_call(
        body, name=name,
        in_specs=[pl.BlockSpec(memory_space=pltpu.VMEM)],
        out_specs=pl.BlockSpec(memory_space=pltpu.VMEM),
        out_shape=jax.ShapeDtypeStruct((N_DEV, R, N), F32),
        scratch_shapes=[pltpu.SemaphoreType.DMA((N_DEV - 1,)), pltpu.SemaphoreType.DMA((N_DEV - 1,))],
        compiler_params=_params(),
    )(x)


def _rows2d(a):
    return a.reshape(-1, a.shape[-1])


def _tile_rows(rows, cols, n_arrays):
    budget = (24 << 20) // (n_arrays * 2 * 4 * cols)
    if rows <= budget:
        return rows
    tm = 8
    for cand in range(8, budget + 1, 8):
        if rows % cand == 0:
            tm = cand
    return tm


SUM_ROWS = 256


def _sum_cores(g, sent, where):
    chips, lead, _, r, cols = g.shape
    tr = min(r, SUM_ROWS)

    def body(where_ref, g_ref, s_ref, out_ref):
        out_ref[...] = (g_ref[...].astype(F32) + s_ref[...].astype(F32)).astype(BF)

    spec = pl.BlockSpec((None, tr, cols), lambda i, j, where_ref: (i, j, 0))
    out = pl.pallas_call(
        body, name="sum_cores",
        grid_spec=pltpu.PrefetchScalarGridSpec(
            num_scalar_prefetch=1, grid=(chips * lead, r // tr),
            in_specs=[pl.BlockSpec((None, None, tr, cols), lambda i, j, where_ref: (i, where_ref[1], j, 0)), spec],
            out_specs=spec),
        out_shape=jax.ShapeDtypeStruct((chips * lead, r, cols), BF),
        compiler_params=_params(dimension_semantics=("parallel", "parallel")),
    )(where, g.reshape(chips * lead, 2, r, cols), sent.reshape(chips * lead, r, cols))
    return out.reshape(chips, lead, r, cols)


def _sum_chips(sums, got, where):
    _, lead, r, cols = sums.shape
    tr = min(r, SUM_ROWS)

    def body(where_ref, s_ref, g_ref, out_ref):
        out_ref[...] = ((s_ref[...].astype(F32) + g_ref[0].astype(F32)) + g_ref[1].astype(F32)) + g_ref[2].astype(F32)

    return pl.pallas_call(
        body, name="sum_chips",
        grid_spec=pltpu.PrefetchScalarGridSpec(
            num_scalar_prefetch=1, grid=(lead, r // tr),
            in_specs=[pl.BlockSpec((None, None, tr, cols), lambda i, j, where_ref: (where_ref[0], i, j, 0)),
                      pl.BlockSpec((N_CHIPS - 1, None, tr, cols), lambda i, j, where_ref: (0, i, j, 0))],
            out_specs=pl.BlockSpec((None, None, tr, cols), lambda i, j, where_ref: (i, where_ref[1], j, 0))),
        out_shape=jax.ShapeDtypeStruct((lead, 2, r, cols), F32),
        compiler_params=_params(dimension_semantics=("parallel", "parallel")),
    )(where, sums, got)


def _adamw(w, m, v, groups, name):
    shape = w.shape
    w2, m2, v2 = _rows2d(w), _rows2d(m), _rows2d(v)
    rows, cols = w2.shape
    ng = len(groups)
    n = len(groups[0])
    slab = rows // ng
    gs = [_rows2d(g) for grp in groups for g in grp]
    tm = _tile_rows(slab, cols, 7 + n)
    tiles = slab // tm
    c1 = 1.0 / (1.0 - ADAM_B1 ** ADAM_STEP)
    c2 = 1.0 / (1.0 - ADAM_B2 ** ADAM_STEP)

    def body(*refs):
        w_ref, m_ref, v_ref = refs[:3]
        g_refs = refs[3:3 + ng * n]
        go_ref, d_ref, mo_ref, vo_ref = refs[3 + ng * n:]
        which = pl.program_id(0)
        for s in range(ng):
            @pl.when(which == s)
            def _(s=s):
                g = g_refs[s * n][...]
                for r in g_refs[s * n + 1:(s + 1) * n]:
                    g = g + r[...]
                mn = ADAM_B1 * m_ref[...] + (1.0 - ADAM_B1) * g
                vn = ADAM_B2 * v_ref[...] + (1.0 - ADAM_B2) * (g * g)
                go_ref[...] = g
                mo_ref[...] = mn
                vo_ref[...] = vn
                d_ref[...] = -ADAM_LR * ((mn * c1) / (jnp.sqrt(vn * c2) + ADAM_EPS) + ADAM_WD * w_ref[...])

    spec = pl.BlockSpec((tm, cols), lambda s, i: (s * tiles + i, 0))
    g_specs = [pl.BlockSpec((tm, cols), lambda s, i, k=k: (jnp.where(s == k, i, jnp.where(s < k, 0, tiles - 1)), 0))
               for k in range(ng) for _ in range(n)]
    outs = pl.pallas_call(
        body, name=name,
        grid=(ng, tiles),
        in_specs=[spec] * 3 + g_specs,
        out_specs=[spec] * 4,
        out_shape=[jax.ShapeDtypeStruct((rows, cols), F32)] * 4,
        compiler_params=_params(dimension_semantics=("arbitrary", "arbitrary")),
    )(w2, m2, v2, *gs)
    return [o.reshape(shape) for o in outs]


def _lower_bounds(r0, r1):
    top = jnp.maximum(r0, r1)
    e0, e1 = jnp.exp(r0 - top), jnp.exp(r1 - top)
    p0, p1 = e0 / (e0 + e1), e1 / (e0 + e1)
    return p0 - p0, (p0 + p1) - p0


def _lbs_fwd(lb_raw):
    def body(lb_ref, out_ref):
        l0, l1 = _lower_bounds(lb_ref[0:1, :], lb_ref[1:2, :])
        out_ref[0:1, :] = l0
        out_ref[1:2, :] = l1

    return pl.pallas_call(body, name="lower_bounds", out_shape=jax.ShapeDtypeStruct(lb_raw.shape, F32), compiler_params=_params())(lb_raw)


def _mod_rows(c_all, w_mod, tn=768):
    _, D, cols = w_mod.shape

    def body(c_ref, w_ref, out_ref):
        out_ref[...] = _dot(c_ref[...].astype(BF), w_ref[...].astype(BF))

    return pl.pallas_call(
        body, name="mod_rows",
        grid=(DEPTH,),
        in_specs=[pl.BlockSpec((N_DEV, D), lambda l: (0, 0)), pl.BlockSpec((None, D, cols), lambda l: (l, 0, 0))],
        out_specs=pl.BlockSpec((N_DEV, cols), lambda l: (0, l)),
        out_shape=jax.ShapeDtypeStruct((N_DEV, DEPTH * cols), F32),
        compiler_params=_params(dimension_semantics=("parallel",)),
    )(c_all, w_mod)


def _grad_w_mod(c_all, dmod_cols):
    D = c_all.shape[1]
    cols = dmod_cols.shape[-1]

    def body(c_ref, d_ref, out_ref):
        out_ref[...] = _dot_tn(c_ref[...].astype(BF), d_ref[...].astype(BF))

    return pl.pallas_call(
        body, name="grad_w_mod",
        grid=(DEPTH,),
        in_specs=[pl.BlockSpec((N_DEV, D), lambda l: (0, 0)), pl.BlockSpec((None, N_DEV, cols), lambda l: (l, 0, 0))],
        out_specs=pl.BlockSpec((None, D, cols), lambda l: (l, 0, 0)),
        out_shape=jax.ShapeDtypeStruct((DEPTH, D, cols), F32),
        compiler_params=_params(dimension_semantics=("parallel",)),
    )(c_all, dmod_cols)


def _sum_devices(parts):
    _, R, N = parts.shape

    def body(p_ref, out_ref):
        acc = p_ref[0]
        for d in range(1, N_DEV):
            acc = acc + p_ref[d]
        out_ref[...] = acc

    return pl.pallas_call(body, name="sum_devices", out_shape=jax.ShapeDtypeStruct((R, N), F32), compiler_params=_params())(parts)


def _lbs_bwd(lb_raw, dl):
    def body(lb_ref, dl_ref, out_ref):
        _, pull = jax.vjp(_lower_bounds, lb_ref[0:1, :], lb_ref[1:2, :])
        d0, d1 = pull((dl_ref[0:1, :], dl_ref[1:2, :]))
        out_ref[0:1, :] = d0
        out_ref[1:2, :] = d1

    return pl.pallas_call(body, name="lower_bounds_bwd", out_shape=jax.ShapeDtypeStruct(lb_raw.shape, F32), compiler_params=_params())(lb_raw, dl)


def kernel(x, c, w_mod, b_mod, w_in, conv_w, hgrn_norm_w, lower_bounds, w_branch, w_out, ln_g, ln_b, loss_target, m_w_mod, m_b_mod, m_w_in, m_conv_w, m_hgrn_norm_w, m_lower_bounds, m_w_branch, m_w_out, m_ln_g, m_ln_b, v_w_mod, v_b_mod, v_w_in, v_conv_w, v_hgrn_norm_w, v_lower_bounds, v_w_branch, v_w_out, v_ln_g, v_ln_b):
    D = D_MODEL
    x0 = x[0]
    target = loss_target[0]
    S = x0.shape[0]
    mx, my, mc = lax.axis_index("x"), lax.axis_index("y"), lax.axis_index("c")
    chip = 2 * mx + my
    me = 2 * chip + mc
    mod_cols = 3 * D // N_CHIPS

    plan = _Plan(w_in.astype(BF), w_branch.astype(BF), w_out.astype(BF), chip, mc)
    plan.first()

    n_conv = DEPTH * 3 * (WIDTH // N_CHIPS)
    first = jnp.concatenate([c, conv_w.reshape(1, n_conv), jnp.zeros((1, 2 * D - D - n_conv), F32)], axis=1)
    first = _all_gather8(first.reshape(8, 2 * D // 8), "gather_c").reshape(N_DEV, 2 * D)
    c_all = first[:, :D]
    conv_all = first[:, D:D + n_conv].reshape(N_DEV, DEPTH, 3, WIDTH // N_CHIPS)[0::2]
    mod_part = _all_gather8(_mod_rows(c_all, w_mod), "gather_mod")[0::2]
    mod_part = lax.dynamic_index_in_dim(mod_part, me, axis=1, keepdims=False).reshape(N_CHIPS, DEPTH, mod_cols)
    mods = [(mod_part[:, l].reshape(1, 3 * D) + b_mod[l][None, :]) for l in range(DEPTH)]
    lbs = _lbs_fwd(lower_bounds).reshape(DEPTH, WIDTH // 128, 1, 128)
    loss_blk, dx, small = _local_step(x0, target, mods, lbs, conv_all, hgrn_norm_w, ln_g, ln_b, plan)

    n_mod, n_nw, n_lb, n_ln, n_cw = DEPTH * 3 * D, DEPTH * 128, DEPTH * WIDTH, DEPTH * D, DEPTH * 3 * WIDTH
    row = jnp.concatenate(
        [jnp.concatenate([small[l][0], small[l][1], small[l][2]], axis=1) for l in range(DEPTH)]
        + [jnp.sum(small[l][3], axis=0) for l in range(DEPTH)]
        + [small[l][4].reshape(1, WIDTH) for l in range(DEPTH)]
        + [small[l][5] for l in range(DEPTH)] + [small[l][6] for l in range(DEPTH)]
        + [jnp.transpose(small[l][7], (1, 0, 2)).reshape(1, 3 * WIDTH) for l in range(DEPTH)]
        + [loss_blk[0:1, :]], axis=1)
    n_row = row.shape[1]
    fold = -(-n_row // (8 * 128)) * 128
    rows = jnp.concatenate([row, jnp.zeros((1, 8 * fold - n_row), F32)], axis=1).reshape(8, fold)

    whole, gathered = plan.finish(rows)
    grads = {kind: [[whole[(kind, l)]] for l in range(DEPTH)] for kind in ("in", "br", "out")}

    off_nw = n_mod
    off_lb = off_nw + n_nw
    off_lng = off_lb + n_lb
    off_lnb = off_lng + n_ln
    off_cw = off_lnb + n_ln
    off_loss = off_cw + n_cw
    total = _sum_devices(gathered).reshape(1, 8 * fold)
    gathered = gathered.reshape(N_DEV, 1, 8 * fold)
    d_lower = _lbs_bwd(lower_bounds, total[0, off_lb:off_lng].reshape(DEPTH, WIDTH))
    loss = total[0, off_loss]
    d_b_mod = total[0, :n_mod].reshape(DEPTH, 3 * D)
    d_norm_w = total[0, off_nw:off_lb].reshape(DEPTH, 128)
    d_ln_g = total[0, off_lng:off_lnb].reshape(DEPTH, D)
    d_ln_b = total[0, off_lnb:off_cw].reshape(DEPTH, D)
    d_conv = total[0, off_cw:off_loss].reshape(DEPTH, 3, N_CHIPS, WIDTH // N_CHIPS)
    d_conv = lax.dynamic_index_in_dim(d_conv, chip, axis=2, keepdims=False)
    dmod_all = gathered[:, 0, :n_mod].reshape(N_DEV, DEPTH, N_CHIPS, mod_cols)
    dmod_cols = jnp.transpose(lax.dynamic_index_in_dim(dmod_all, chip, axis=2, keepdims=False), (1, 0, 2))
    d_w_mod = _grad_w_mod(c_all, dmod_cols)

    res = {}
    res["w_mod"] = _adamw(w_mod, m_w_mod, v_w_mod, [[d_w_mod]], "adamw_w_mod")
    res["b_mod"] = _adamw(b_mod, m_b_mod, v_b_mod, [[d_b_mod]], "adamw_b_mod")
    res["w_in"] = _adamw(w_in, m_w_in, v_w_in, grads["in"], "adamw_w_in")
    res["conv_w"] = _adamw(conv_w, m_conv_w, v_conv_w, [[d_conv]], "adamw_conv_w")
    res["hgrn_norm_w"] = _adamw(hgrn_norm_w, m_hgrn_norm_w, v_hgrn_norm_w, [[d_norm_w]], "adamw_norm_w")
    res["lower_bounds"] = _adamw(lower_bounds, m_lower_bounds, v_lower_bounds, [[d_lower]], "adamw_lower_bounds")
    res["w_branch"] = _adamw(w_branch, m_w_branch, v_w_branch, grads["br"], "adamw_w_branch")
    res["w_out"] = _adamw(w_out, m_w_out, v_w_out, grads["out"], "adamw_w_out")
    res["ln_g"] = _adamw(ln_g, m_ln_g, v_ln_g, [[d_ln_g]], "adamw_ln_g")
    res["ln_b"] = _adamw(ln_b, m_ln_b, v_ln_b, [[d_ln_b]], "adamw_ln_b")
    names = ["w_mod", "b_mod", "w_in", "conv_w", "hgrn_norm_w", "lower_bounds", "w_branch", "w_out", "ln_g", "ln_b"]
    return (loss, dx[None], *[res[n][0] for n in names], *[res[n][1] for n in names],
            *[res[n][2] for n in names], *[res[n][3] for n in names])


class _Plan:
    FIRST_CHUNKS = 3
    WINDOWS = ((0, 2 * IN_COLS // N_CHIPS // 3), (2 * IN_COLS // N_CHIPS // 3, IN_COLS // N_CHIPS // 3))

    def __init__(self, w_in, w_br, w_out, chip, core):
        self.local = {"in": w_in, "br": w_br, "out": w_out}
        self.chip, self.where = chip, jnp.stack([chip, core]).astype(jnp.int32)
        self.gathered, self.partial, self.grads, self.chip_sums, self.scattered, self.pending = {}, {}, {}, {}, {}, {}

    def chunks(self, l):
        return self.FIRST_CHUNKS if l == 0 else 1

    def w_in(self, l):
        return [self.gathered[("in", l, c)] for c in range(self.chunks(l))]

    def _shard(self, key):
        mine = self.local[key[0]][key[1]]
        if key[0] == "in":
            cols = mine.shape[-1] // self.chunks(key[1])
            mine = mine[:, key[2] * cols:(key[2] + 1) * cols]
        return mine

    def _slab(self, key):
        mine = _halves(self._shard(key))
        return lax.dynamic_update_slice(lax.empty((N_CHIPS,) + mine.shape, mine.dtype), mine[None], (self.chip, 0, 0, 0, 0))

    def _gather(self, keys):
        return ("gather", keys), _gather_job([self._slab(key) for key in keys])

    def _gather_window(self, key, n):
        slab = self._slab(key) if n == 0 else self.partial[key]
        return ("gather" if n == len(self.WINDOWS) - 1 else "gather_part", [key]), _gather_job([slab], self.WINDOWS[n])

    def _to_sibling(self, keys):
        return ("to_sibling", keys), _to_sibling_job([_halves(self.grads[key], 1) for key in keys])

    def _scatter(self, keys):
        return ("scatter", keys), _scatter_job([self.chip_sums[key] for key in keys])

    def job(self, stage, l, c=0):
        parts = []
        if stage == "proj_fwd":
            parts = [self._gather([("in", l, c + 1)] if c + 1 < self.chunks(l) else [("br", l), ("out", l)])]
        elif stage == "attn_fwd" and l + 1 < DEPTH:
            parts = [self._gather_window(("in", l + 1, 0), 0)]
        elif stage == "hgrn_fwd" and l + 1 < DEPTH:
            parts = [self._gather_window(("in", l + 1, 0), 1)]
        elif stage == "attn_bwd":
            parts = [self._to_sibling([("out", l), ("br", l)])] + ([self._scatter([("in", l + 1)])] if l + 1 < DEPTH else [])
        elif stage == "hgrn_bwd":
            parts = [self._scatter([("out", l), ("br", l)])]
        elif stage == "proj_bwd":
            parts = [self._to_sibling([("in", l)])] if l else [self._scatter([("in", 0)])]
        self.pending[(stage, l, c)] = [(tag, len(job.outs)) for tag, job in parts]
        return _join_jobs([job for _, job in parts])

    def done(self, stage, l, outs, c=0):
        if outs is None:
            return
        at = 0
        for (what, keys), n_outs in self.pending[(stage, l, c)]:
            mine, at = outs[at:at + n_outs], at + n_outs
            for n, key in enumerate(keys):
                if what == "gather":
                    self.gathered[key] = mine[n].reshape((N_CHIPS,) + self._shard(key).shape)
                elif what == "gather_part":
                    self.partial[key] = mine[n]
                elif what == "to_sibling":
                    self.chip_sums[key] = _sum_cores(_halves(self.grads[key], 1), mine[n], self.where)
                else:
                    self.scattered[key] = mine[n]

    def first(self):
        tag, job = self._gather([("in", 0, 0)])
        self.pending[("first", 0, 0)] = [(tag, len(job.outs))]
        self.done("first", 0, _run_job(job, "gather_first"))

    def took(self, key, grad):
        self.grads[key] = grad
        if key == ("in", 0):
            tag, job = self._to_sibling([key])
            self.pending[("took", 0, 0)] = [(tag, len(job.outs))]
            self.done("took", 0, _run_job(job, "to_sibling_last"))

    def finish(self, rows):
        keys = [(kind, l) for kind in ("in", "br", "out") for l in range(DEPTH)]
        halves = [_sum_chips(self.chip_sums[key], self.scattered[key], self.where) for key in keys]
        outs = _run_job(_join_jobs([_place_job(halves), _gather8_job(rows)]), "place_halves")
        return {key: w.reshape(self.grads[key].shape[1:]) for key, w in zip(keys, outs[:-1])}, outs[-1]


def _local_step(x0, target, mods, lbs, conv_all, hgrn_norm_w, ln_g, ln_b, plan):
    D = D_MODEL
    after, before = _attn_consts()
    hg_consts = _hgrn_consts()

    saved = []
    xl = x0
    for l in range(DEPTH):
        n = plan.chunks(l)
        (proj, h), got = _proj_fwd(xl, mods[l], plan.gathered[("in", l, 0)], plan.job("proj_fwd", l, 0), (0, n))
        plan.done("proj_fwd", l, got, 0)
        for c in range(1, n):
            (proj,), got = _proj_cols(h, plan.gathered[("in", l, c)], (c, n), proj, plan.job("proj_fwd", l, c))
            plan.done("proj_fwd", l, got, c)
        (o_a, tot), got = _attn_fwd(proj, after, plan.job("attn_fwd", l))
        plan.done("attn_fwd", l, got)
        (o_b, states), got = _hgrn_fwd(proj, lbs[l], hg_consts, plan.job("hgrn_fwd", l))
        plan.done("hgrn_fwd", l, got)
        ys = _branch_fwd(proj, o_a, o_b, hgrn_norm_w, conv_all, l)
        x_next, merged, y = _merge_fwd(xl, mods[l], proj, ys, plan.gathered[("br", l)], plan.gathered[("out", l)], ln_g, ln_b, l)
        saved.append((xl, proj, h, o_a, tot, o_b, states, ys, merged, y))
        xl = x_next
    dx, loss_blk = _loss_head(xl, target)

    small = [None] * DEPTH
    for l in reversed(range(DEPTH)):
        xin, proj, h, o_a, tot, o_b, states, ys, merged, y = saved[l]
        dx_res, dy, dp, dg, dy_a, dy_b, dy_c, dln_g, dln_b, dgate = _merge_bwd(
            dx, xin, y, mods[l], proj, ys, plan.gathered[("br", l)], plan.gathered[("out", l)], ln_g, l)
        plan.took(("out", l), _grad_w_out(merged, dy))
        g_br = jax.ShapeDtypeStruct((N_CHIPS, 3, WIDTH, D // N_CHIPS), BF)
        for i in range(3):
            g_br = _grad_w(ys[i], dp, (i,), g_br, "grad_w_branch", tm=WIDTH, tn=D // N_CHIPS, b_col0=i * D)
        plan.took(("br", l), g_br)
        d_oa, d_ob, dz_a, dz_b, dpre, dpost, du, dz_c, dnorm_w, dconv_w = _branch_bwd(proj, o_a, o_b, hgrn_norm_w, conv_all, dy_a, dy_b, dy_c, l)
        (dq_a, dk_a, dv_a), got = _attn_bwd(proj, d_oa, tot, after, before, plan.job("attn_bwd", l))
        plan.done("attn_bwd", l, got)
        (dq_b, df_b, di_b, dlb), got = _hgrn_bwd(proj, lbs[l], states, d_ob, hg_consts, plan.job("hgrn_bwd", l))
        plan.done("hgrn_bwd", l, got)
        dproj = jnp.concatenate([dq_a, dk_a, dv_a, dz_a, dq_b, df_b, di_b, dz_b, dpre, dpost, du, dz_c, dg], axis=1)
        plan.took(("in", l), _grad_w(h, dproj, (), jax.ShapeDtypeStruct((N_CHIPS, D, IN_COLS // N_CHIPS), BF), "grad_w_in", tm=512, tn=2304))
        (dx, dshift, dscale), got = _proj_bwd(dproj, plan.w_in(l), xin, mods[l], dx_res, plan.job("proj_bwd", l))
        plan.done("proj_bwd", l, got)
        small[l] = (dshift, dscale, dgate, dnorm_w, dlb, dln_g, dln_b, dconv_w)
    return loss_blk, dx, small


def _grad_w_out(merged, dy):
    S, D = merged.shape
    q = D // N_CHIPS

    def body(a_ref, b_ref, o_ref):
        o_ref[...] = _dot_tn(a_ref[...], b_ref[...]).astype(BF)

    return pl.pallas_call(
        body, name="grad_w_out",
        grid=(N_CHIPS,),
        in_specs=[pl.BlockSpec((S, q), lambda j: (0, j)), pl.BlockSpec((S, D), lambda j: (0, 0))],
        out_specs=pl.BlockSpec((None, q, D), lambda j: (j, 0, 0)),
        out_shape=jax.ShapeDtypeStruct((N_CHIPS, q, D), BF),
        compiler_params=_params(dimension_semantics=("parallel",)),
    )(merged, dy)
```

```python
import functools
import math

import numpy as np
import jax
import jax.numpy as jnp
from jax import lax
from jax.experimental import pallas as pl
from jax.experimental.pallas import tpu as pltpu

F32 = jnp.float32
BF = jnp.bfloat16
MESH = pl.DeviceIdType.MESH

DEPTH = 2
D_MODEL = 1024
WIDTH = 512
IN_COLS = 12 * WIDTH + 3 * D_MODEL
N_CHIPS = 4
N_DEV = 8
SB_BLOCK = 128
SB_HEAD_DIM = 64
HG_CHUNK = 128
HG_DIM = 128
LN_EPS = 1e-5
RMS_EPS = 1e-6
ALPHA = (2.0 * DEPTH) ** 0.25
ADAM_LR, ADAM_B1, ADAM_B2, ADAM_EPS, ADAM_WD, ADAM_STEP = 0.001, 0.9, 0.999, 1e-08, 0.01, 10
VMEM_LIMIT = 56 << 20


def _params(**kw):
    return pltpu.CompilerParams(vmem_limit_bytes=VMEM_LIMIT, **kw)


def _dot(a, b):
    return jnp.dot(a, b, preferred_element_type=F32)


def _dot_nt(a, b):
    return lax.dot_general(a, b, (((1,), (1,)), ((), ())), preferred_element_type=F32)


def _dot_tn(a, b):
    return lax.dot_general(a, b, (((0,), (0,)), ((), ())), preferred_element_type=F32)


def _sigmoid(x):
    return 1.0 / (1.0 + jnp.exp(-x))


def _silu(x):
    return x * _sigmoid(x)


def _softplus(z):
    return jnp.maximum(z, 0.0) + jnp.log(1.0 + jnp.exp(-jnp.abs(z)))


def _split_dot(t, g, terms):
    acc = None
    rest = g
    for _ in range(terms):
        part = rest.astype(BF)
        rest = rest - part.astype(F32)
        d = _dot(t, part)
        acc = d if acc is None else acc + d
    return acc


def _split_dot_r(g, t, terms):
    acc = None
    rest = g
    for _ in range(terms):
        part = rest.astype(BF)
        rest = rest - part.astype(F32)
        d = _dot(part, t)
        acc = d if acc is None else acc + d
    return acc


def _standardize(x):
    mu = jnp.mean(x, axis=-1, keepdims=True)
    xc = x - mu
    var = jnp.mean(xc * xc, axis=-1, keepdims=True)
    rstd = lax.rsqrt(var + LN_EPS)
    return xc * rstd, rstd


def _standardize_bwd(dxs, xs, rstd):
    return rstd * (dxs - jnp.mean(dxs, axis=-1, keepdims=True) - xs * jnp.mean(dxs * xs, axis=-1, keepdims=True))


class _Job:
    def __init__(self, ins, outs, sems, make, alias=None):
        self.ins, self.outs, self.sems, self.make = list(ins), list(outs), list(sems), make
        self.alias = dict(alias or {})


def _join_jobs(jobs):
    jobs = [j for j in jobs if j is not None]
    if len(jobs) <= 1:
        return jobs[0] if jobs else None

    def make(ins, outs, sems):
        phases, i, o, s = [], 0, 0, 0
        for j in jobs:
            got = j.make(ins[i:i + len(j.ins)], outs[o:o + len(j.outs)], sems[s:s + len(j.sems)])
            i, o, s = i + len(j.ins), o + len(j.outs), s + len(j.sems)
            for n, phase in enumerate(got):
                if n == len(phases):
                    phases.append([])
                phases[n] += phase
        return phases

    alias, i, o = {}, 0, 0
    for j in jobs:
        alias.update({i + a: o + b for a, b in j.alias.items()})
        i, o = i + len(j.ins), o + len(j.outs)
    return _Job(sum([j.ins for j in jobs], []), sum([j.outs for j in jobs], []), sum([j.sems for j in jobs], []), make, alias)


def _flip(v, bit):
    return 1 - v if bit else v


def _halves(a, front=0):
    shape = a.shape
    lead = math.prod(shape[front:-2])
    return a.reshape(shape[:front] + (lead, 2, shape[-2] // 2, shape[-1]))


def _dma_sems(*shapes):
    return [pltpu.SemaphoreType.DMA(s) for s in shapes]


def _same(arrays):
    return [jax.ShapeDtypeStruct(a.shape, a.dtype) for a in arrays]


def _gather_job(slabs, window=None):
    n = len(slabs)
    cols = slice(None) if window is None else pl.ds(*window)

    def make(ins, outs, sems):
        send1, recv1, send2, recv2 = sems
        mx, my, mc = lax.axis_index("x"), lax.axis_index("y"), lax.axis_index("c")
        fetch, pass_on = [], []
        for a in range(n):
            ours = outs[a].at[2 * mx + my, :, mc, :, cols]
            for k in range(1, N_CHIPS):
                px, py = _flip(mx, k & 2), _flip(my, k & 1)
                fetch.append(pltpu.make_async_remote_copy(
                    src_ref=ours, dst_ref=ours, send_sem=send1.at[a, k - 1], recv_sem=recv1.at[a, k - 1],
                    device_id=(px, py, mc), device_id_type=MESH))
                theirs = outs[a].at[2 * px + py, :, mc, :, cols]
                pass_on.append(pltpu.make_async_remote_copy(
                    src_ref=theirs, dst_ref=theirs, send_sem=send2.at[a, k - 1], recv_sem=recv2.at[a, k - 1],
                    device_id=(mx, my, 1 - mc), device_id_type=MESH))
        return [fetch, pass_on]

    pairs = (n, N_CHIPS - 1)
    return _Job(slabs, _same(slabs), _dma_sems(pairs, pairs, pairs, pairs), make, {a: a for a in range(n)})


def _to_sibling_job(grads):
    n = len(grads)

    def make(ins, outs, sems):
        send_sems, recv_sems = sems
        mx, my, mc = lax.axis_index("x"), lax.axis_index("y"), lax.axis_index("c")
        return [[pltpu.make_async_remote_copy(
            src_ref=ins[a].at[:, :, 1 - mc], dst_ref=outs[a], send_sem=send_sems.at[a], recv_sem=recv_sems.at[a],
            device_id=(mx, my, 1 - mc), device_id_type=MESH) for a in range(n)]]

    outs = [jax.ShapeDtypeStruct(g.shape[:2] + g.shape[3:], g.dtype) for g in grads]
    return _Job(grads, outs, _dma_sems((n,), (n,)), make)


def _scatter_job(sums):
    n = len(sums)

    def make(ins, outs, sems):
        send_sems, recv_sems = sems
        mx, my, mc = lax.axis_index("x"), lax.axis_index("y"), lax.axis_index("c")
        copies = []
        for a in range(n):
            for k in range(1, N_CHIPS):
                px, py = _flip(mx, k & 2), _flip(my, k & 1)
                copies.append(pltpu.make_async_remote_copy(
                    src_ref=ins[a].at[2 * px + py], dst_ref=outs[a].at[k - 1], send_sem=send_sems.at[a, k - 1],
                    recv_sem=recv_sems.at[a, k - 1], device_id=(px, py, mc), device_id_type=MESH))
        return [copies]

    pairs = (n, N_CHIPS - 1)
    return _Job(sums, [jax.ShapeDtypeStruct((N_CHIPS - 1,) + s.shape[1:], s.dtype) for s in sums], _dma_sems(pairs, pairs), make)


def _place_job(wholes):
    n = len(wholes)

    def make(ins, outs, sems):
        send_sems, recv_sems = sems
        mx, my, mc = lax.axis_index("x"), lax.axis_index("y"), lax.axis_index("c")
        copies = []
        for a in range(n):
            here = outs[a].at[:, mc]
            copies.append(pltpu.make_async_remote_copy(src_ref=here, dst_ref=here, send_sem=send_sems.at[a], recv_sem=recv_sems.at[a],
                                                       device_id=(mx, my, 1 - mc), device_id_type=MESH))
        return [copies]

    return _Job(wholes, _same(wholes), _dma_sems((n,), (n,)), make, {a: a for a in range(n)})


def _gather8_job(x):
    def make(ins, outs, sems):
        local_sem, send_sems, recv_sems = sems
        mx, my, mc = lax.axis_index("x"), lax.axis_index("y"), lax.axis_index("c")
        here = outs[0].at[4 * mx + 2 * my + mc]
        copies = [pltpu.make_async_copy(ins[0], here, local_sem.at[0])]
        for k in range(1, N_DEV):
            peer = (_flip(mx, k & 4), _flip(my, k & 2), _flip(mc, k & 1))
            copies.append(pltpu.make_async_remote_copy(src_ref=ins[0], dst_ref=here, send_sem=send_sems.at[k - 1],
                                                       recv_sem=recv_sems.at[k - 1], device_id=peer, device_id_type=MESH))
        return [copies]

    return _Job([x], [jax.ShapeDtypeStruct((N_DEV,) + x.shape, x.dtype)], _dma_sems((1,), (N_DEV - 1,), (N_DEV - 1,)), make)


def _run_phases(phases, first=0):
    for n, phase in enumerate(phases):
        if n >= first:
            for cp in phase:
                cp.start()
        for cp in phase:
            cp.wait()


def _run_job(job, name):
    k_in, k_out = len(job.ins), len(job.outs)

    def body(*refs):
        _run_phases(job.make(refs[:k_in], refs[k_in:k_in + k_out], refs[k_in + k_out:]))

    hbm = pl.BlockSpec(memory_space=pl.ANY)
    return pl.pallas_call(body, name=name, in_specs=[hbm] * k_in, out_specs=[hbm] * k_out, out_shape=job.outs,
                          scratch_shapes=job.sems, input_output_aliases=job.alias, compiler_params=_params())(*job.ins)


def _hosted(body, job, args, *, name, grid, in_specs, out_specs, out_shape, scratch_shapes=(), semantics, aliases=None):
    in_specs, out_specs, out_shape, scratch = list(in_specs), list(out_specs), list(out_shape), list(scratch_shapes)
    aliases = dict(aliases or {})
    if job is None:
        outs = pl.pallas_call(body, name=name, grid=grid, in_specs=in_specs, out_specs=out_specs, out_shape=out_shape,
                              scratch_shapes=scratch, input_output_aliases=aliases,
                              compiler_params=_params(dimension_semantics=semantics))(*args)
        return list(outs), None
    n_in, n_out, n_scr, k_in, k_out = len(in_specs), len(out_specs), len(scratch), len(job.ins), len(job.outs)

    def wrapped(*refs):
        ins, rest = refs[:n_in], refs[n_in:]
        job_ins, rest = rest[:k_in], rest[k_in:]
        outs, rest = rest[:n_out], rest[n_out:]
        job_outs, rest = rest[:k_out], rest[k_out:]
        scr, sems = rest[:n_scr], rest[n_scr:]
        ids = [pl.program_id(a) for a in range(len(grid))]
        first = functools.reduce(jnp.logical_and, [i == 0 for i in ids])
        last = functools.reduce(jnp.logical_and, [i == g - 1 for i, g in zip(ids, grid)])

        @pl.when(first)
        def _():
            for cp in job.make(job_ins, job_outs, sems)[0]:
                cp.start()

        body(*ins, *outs, *scr)

        @pl.when(last)
        def _():
            _run_phases(job.make(job_ins, job_outs, sems), first=1)

    hbm = pl.BlockSpec(memory_space=pl.ANY)
    outs = pl.pallas_call(
        wrapped, name=name, grid=grid, in_specs=in_specs + [hbm] * k_in, out_specs=out_specs + [hbm] * k_out,
        out_shape=out_shape + job.outs, scratch_shapes=scratch + job.sems,
        input_output_aliases={**aliases, **{n_in + i: n_out + o for i, o in job.alias.items()}},
        compiler_params=_params(dimension_semantics=("arbitrary",) * len(grid)))(*args, *job.ins)
    return list(outs[:n_out]), list(outs[n_out:])


def _proj_fwd(x, mod, wg, job=None, chunk=(0, 1), tm=512):
    S, D = x.shape
    tm = min(tm, S)
    tn = wg.shape[-1]
    c, n = chunk

    def body(x_ref, mod_ref, w_ref, proj_ref, h_ref, hs):
        @pl.when(pl.program_id(1) == 0)
        def _():
            xs, _ = _standardize(x_ref[...])
            h = xs * (1.0 + mod_ref[:, D:2 * D]) + mod_ref[:, 0:D]
            hb = h.astype(BF)
            hs[...] = hb
            h_ref[...] = hb

        proj_ref[...] = _dot(hs[...], w_ref[...])

    return _hosted(
        body, job, (x, mod, wg), name="proj_fwd",
        grid=(S // tm, N_CHIPS),
        in_specs=[pl.BlockSpec((tm, D), lambda i, j: (i, 0)),
                  pl.BlockSpec((1, 3 * D), lambda i, j: (0, 0)),
                  pl.BlockSpec((None, D, tn), lambda i, j: (j, 0, 0))],
        out_specs=[pl.BlockSpec((tm, tn), lambda i, j: (i, j * n + c)),
                   pl.BlockSpec((tm, D), lambda i, j: (i, 0))],
        out_shape=[jax.ShapeDtypeStruct((S, IN_COLS), F32), jax.ShapeDtypeStruct((S, D), BF)],
        scratch_shapes=[pltpu.VMEM((tm, D), BF)],
        semantics=("parallel", "arbitrary"))


def _proj_cols(h, wg, chunk, proj, job=None, tm=512):
    S, D = h.shape
    tm = min(tm, S)
    tn = wg.shape[-1]
    c, n = chunk

    def body(h_ref, w_ref, prev_ref, proj_ref):
        proj_ref[...] = _dot(h_ref[...], w_ref[...])

    in_specs = [pl.BlockSpec((tm, D), lambda i, j: (i, 0)), pl.BlockSpec((None, D, tn), lambda i, j: (j, 0, 0)),
                pl.BlockSpec(memory_space=pl.ANY)]
    return _hosted(body, job, (h, wg, proj), name="proj_cols", grid=(S // tm, N_CHIPS), in_specs=in_specs,
                   out_specs=[pl.BlockSpec((tm, tn), lambda i, j: (i, j * n + c))],
                   out_shape=[jax.ShapeDtypeStruct(proj.shape, proj.dtype)],
                   semantics=("parallel", "arbitrary"), aliases={2: 0})


SB_ROWS = 256
SB_KEYS = 256


def _attn_consts():
    j = np.arange(SB_KEYS)[:, None]
    s = np.arange(SB_KEYS)[None, :]
    from_here = np.concatenate([(j >= s), (j >= s)], axis=0).astype(np.float32)
    up_to = np.concatenate([(j <= s), (j <= s)], axis=0).astype(np.float32)
    return jnp.asarray(from_here, BF), jnp.asarray(up_to, BF)


def _hi_lo(x):
    hi = lax.bitcast_convert_type(lax.bitcast_convert_type(x, jnp.uint32) & jnp.uint32(0xFFFF0000), F32)
    return hi.astype(BF), (x - hi).astype(BF)


def _sums_r(x, t2):
    hi, lo = _hi_lo(x)
    return _dot(jnp.concatenate([hi, lo], axis=1), t2)


def _all_lanes(col, lanes):
    return jnp.broadcast_to(col, (col.shape[0], lanes))


def _attn_rows(ref, r0, rows, lanes, head0, scale=None):
    v = ref[pl.ds(r0, rows), lanes]
    if scale is not None:
        v = v * scale
    return jnp.concatenate([jnp.where(head0, v, 0.0), jnp.where(head0, 0.0, v)], axis=0).astype(BF)


SB_PAIRS = 2
SB_LANES = SB_PAIRS * SB_BLOCK


def _attn_specs(S):
    return lambda base: pl.BlockSpec((S, SB_LANES), lambda p, base=base: (0, base // SB_PAIRS + p))


def _attn_scores(q2n, k_ref, lanes, kj, t2, from_here_ref, masked):
    c0 = pl.multiple_of(kj * SB_KEYS, SB_KEYS)
    kb = k_ref[pl.ds(c0, SB_KEYS), lanes].astype(BF)
    zn = _dot_nt(q2n, kb)
    lsb = jnp.minimum(zn, 0.0) - jnp.log(1.0 + jnp.exp(-jnp.abs(zn)))
    valid = None
    if masked:
        valid = (lax.broadcasted_iota(jnp.int32, zn.shape, 1) + kj * SB_KEYS) < t2
        lsb = jnp.where(valid, lsb, 0.0)
    return c0, kb, zn, valid, lsb, _sums_r(lsb, from_here_ref[...])


def _attn_fwd(proj, from_here, job=None):
    S = proj.shape[0]
    TQ = SB_ROWS
    assert S % TQ == 0 and SB_KEYS == TQ
    scale = SB_HEAD_DIM ** -0.5
    pairs = range(SB_PAIRS)
    lanes = [pl.ds(p * SB_BLOCK, SB_BLOCK) for p in pairs]

    def body(q_ref, k_ref, v_ref, from_here_ref, o_ref, tot_ref, run, acc):
        head0 = lax.broadcasted_iota(jnp.int32, (1, 2 * SB_HEAD_DIM), 1) < SB_HEAD_DIM

        def qloop(qi, _):
            r0 = pl.multiple_of(qi * TQ, TQ)
            q2n = [_attn_rows(q_ref, r0, TQ, lanes[p], head0, -scale) for p in pairs]
            trow = lax.broadcasted_iota(jnp.int32, (TQ, SB_KEYS), 0) + qi * TQ
            t2 = jnp.concatenate([trow, trow], axis=0)
            run[...] = jnp.zeros_like(run)
            acc[...] = jnp.zeros_like(acc)

            def step(kj, masked):
                got = [_attn_scores(q2n[p], k_ref, lanes[p], kj, t2, from_here_ref, masked) for p in pairs]
                for p in pairs:
                    c0, _, zn, valid, _, sums = got[p]
                    r = run[p]
                    e = sums - zn + jnp.concatenate([r, r], axis=1)
                    if masked:
                        e = jnp.where(valid, e, -jnp.inf)
                    acc[p] += _dot(jnp.exp(e).astype(BF), v_ref[pl.ds(c0, SB_KEYS), lanes[p]].astype(BF))
                    run[p] = r + _all_lanes(sums[:, 0:1], SB_BLOCK)

            step(qi, True)

            def below(n, _):
                step(qi - 1 - n, False)
                return 0

            lax.fori_loop(0, qi, below, 0)
            for p in pairs:
                o_ref[pl.ds(r0, TQ), lanes[p]] = jnp.where(head0, acc[p, 0:TQ, :], acc[p, TQ:2 * TQ, :])
                tot_ref[p, 0, pl.ds(r0, TQ), :] = run[p, 0:TQ, :]
                tot_ref[p, 1, pl.ds(r0, TQ), :] = run[p, TQ:2 * TQ, :]
            return 0

        lax.fori_loop(0, S // TQ, qloop, 0)

    col = _attn_specs(S)
    state = pltpu.VMEM((SB_PAIRS, 2 * TQ, SB_BLOCK), F32)
    return _hosted(
        body, job, (proj, proj, proj, from_here), name="attn_fwd",
        grid=(WIDTH // SB_LANES,),
        in_specs=[col(0), col(4), col(8), pl.BlockSpec(from_here.shape, lambda p: (0, 0))],
        out_specs=[col(0), pl.BlockSpec((SB_PAIRS, 2, S, 128), lambda p: (p, 0, 0, 0))],
        out_shape=[jax.ShapeDtypeStruct((S, WIDTH), F32), jax.ShapeDtypeStruct((WIDTH // 128, 2, S, 128), F32)],
        scratch_shapes=[state, state],
        semantics=("parallel",))


def _attn_bwd(proj, d_o, tot, from_here, up_to, job=None):
    S = proj.shape[0]
    TQ = SB_ROWS
    assert S % TQ == 0 and SB_KEYS == TQ
    scale = SB_HEAD_DIM ** -0.5
    pairs = range(SB_PAIRS)
    lanes = [pl.ds(p * SB_BLOCK, SB_BLOCK) for p in pairs]

    def body(q_ref, k_ref, v_ref, do_ref, tot_ref, from_here_ref, up_to_ref, dq_ref, dk_ref, dv_ref, pre, cum, dq_acc, dk_acc, dv_acc):
        head0 = lax.broadcasted_iota(jnp.int32, (1, 2 * SB_HEAD_DIM), 1) < SB_HEAD_DIM
        dk_acc[...] = jnp.zeros_like(dk_acc)
        dv_acc[...] = jnp.zeros_like(dv_acc)

        def qloop(qi, _):
            r0 = pl.multiple_of(qi * TQ, TQ)
            q2n = [_attn_rows(q_ref, r0, TQ, lanes[p], head0, -scale) for p in pairs]
            do2 = [_attn_rows(do_ref, r0, TQ, lanes[p], head0) for p in pairs]
            trow = lax.broadcasted_iota(jnp.int32, (TQ, SB_KEYS), 0) + qi * TQ
            t2 = jnp.concatenate([trow, trow], axis=0)
            for p in pairs:
                pre[p, 0:TQ, :] = tot_ref[p, 0, pl.ds(r0, TQ), :]
                pre[p, TQ:2 * TQ, :] = tot_ref[p, 1, pl.ds(r0, TQ), :]
            cum[...] = jnp.zeros_like(cum)
            dq_acc[...] = jnp.zeros_like(dq_acc)

            def step(kj, masked):
                got = [_attn_scores(q2n[p], k_ref, lanes[p], kj, t2, from_here_ref, masked) for p in pairs]
                for p in pairs:
                    c0, kb, zn, valid, lsb, sums = got[p]
                    later = pre[p] - _all_lanes(sums[:, 0:1], SB_BLOCK)
                    pre[p] = later
                    e = sums - zn + jnp.concatenate([later, later], axis=1)
                    sig = jnp.exp(lsb - zn)
                    if masked:
                        e = jnp.where(valid, e, -jnp.inf)
                        sig = jnp.where(valid, sig, 0.0)
                    a = jnp.exp(e)
                    w = _dot_nt(do2[p], v_ref[pl.ds(c0, SB_KEYS), lanes[p]].astype(BF)) * a
                    upto = _sums_r(w, up_to_ref[...])
                    c = cum[p]
                    dz = w - sig * (upto + jnp.concatenate([c, c], axis=1))
                    cum[p] = c + _all_lanes(upto[:, SB_KEYS - 1:SB_KEYS], SB_BLOCK)
                    dzb = dz.astype(BF)
                    dq_acc[p] += _dot(dzb, kb)
                    dk_acc[pl.ds(c0, SB_KEYS), lanes[p]] += _dot_tn(dzb, q2n[p])
                    dv_acc[pl.ds(c0, SB_KEYS), lanes[p]] += _dot_tn(a.astype(BF), do2[p])

            def below(kj, _):
                step(kj, False)
                return 0

            lax.fori_loop(0, qi, below, 0)
            step(qi, True)
            for p in pairs:
                dq_ref[pl.ds(r0, TQ), lanes[p]] = (jnp.where(head0, dq_acc[p, 0:TQ, :], dq_acc[p, TQ:2 * TQ, :]) * scale).astype(BF)
            return 0

        lax.fori_loop(0, S // TQ, qloop, 0)
        dk_ref[...] = (-dk_acc[...]).astype(BF)
        dv_ref[...] = dv_acc[...].astype(BF)

    col = _attn_specs(S)
    whole = lambda a: pl.BlockSpec(a.shape, lambda p: (0, 0))
    out = jax.ShapeDtypeStruct((S, WIDTH), BF)
    state = pltpu.VMEM((SB_PAIRS, 2 * TQ, SB_BLOCK), F32)
    grads = pltpu.VMEM((S, SB_LANES), F32)
    return _hosted(
        body, job, (proj, proj, proj, d_o, tot, from_here, up_to), name="attn_bwd",
        grid=(WIDTH // SB_LANES,),
        in_specs=[col(0), col(4), col(8), col(0), pl.BlockSpec((SB_PAIRS, 2, S, 128), lambda p: (p, 0, 0, 0)), whole(from_here), whole(up_to)],
        out_specs=[col(0), col(0), col(0)],
        out_shape=[out, out, out],
        scratch_shapes=[state, state, state, grads, grads],
        semantics=("parallel",))


HG_LEVELS = tuple(HG_CHUNK >> n for n in range(1, HG_CHUNK.bit_length()))


def _hgrn_consts():
    C = HG_CHUNK
    t = np.arange(C)[:, None]
    s = np.arange(C)[None, :]
    rows = [(s <= t), (s > t)]
    masks = [(t == s)]
    for m in HG_LEVELS:
        two = 2 * m
        mid = (t // two) * two + m
        right = (t % two) >= m
        rows.append((right & (s >= mid) & (s <= t)) | ((~right) & (s > t) & (s <= mid - 1)))
        masks.append(((t // two) == (s // two)) & right & ((s % two) < m))
    tri = np.concatenate(rows, axis=0).astype(np.float32)
    return (jnp.asarray(tri, BF), jnp.asarray(tri.T.copy(), BF), jnp.asarray(np.stack(masks).astype(np.float32), F32))


HG_SUM_BLOCKS = 2 + len(HG_LEVELS)
HG_SUM_TERMS = 2


@jax.custom_vjp
def _hgrn_sums(tri, tri_t, g):
    C = HG_CHUNK
    e = _split_dot(tri, g, HG_SUM_TERMS)
    blocks = tuple(e[n * C:(n + 1) * C] for n in range(HG_SUM_BLOCKS))
    return blocks + (jnp.broadcast_to(e[C - 1:C], (HG_DIM, e.shape[1])),)


def _hgrn_sums_fwd(tri, tri_t, g):
    return _hgrn_sums(tri, tri_t, g), (tri, tri_t)


def _hgrn_sums_bwd(res, ds):
    tri, tri_t = res
    C = HG_CHUNK
    last = lax.broadcasted_iota(jnp.int32, (C, 1), 0) == C - 1
    prefix = ds[0] + jnp.where(last, jnp.sum(ds[-1], axis=0, keepdims=True), 0.0)
    d = jnp.concatenate((prefix,) + tuple(ds[1:-1]), axis=0)
    return jnp.zeros_like(tri), jnp.zeros_like(tri_t), _split_dot(tri_t, d, HG_SUM_TERMS)


_hgrn_sums.defvjp(_hgrn_sums_fwd, _hgrn_sums_bwd)


def _bf_dot(a, b):
    return _dot(a.astype(BF), b.astype(BF))


def _bf_dot_nt(a, b):
    return _dot_nt(a.astype(BF), b.astype(BF))


def _bf_dot_tn(a, b):
    return _dot_tn(a.astype(BF), b.astype(BF))


@jax.custom_vjp
def _mm(a, b):
    return _bf_dot(a, b)


_mm.defvjp(lambda a, b: (_bf_dot(a, b), (a, b)), lambda r, ct: (_bf_dot_nt(ct, r[1]), _bf_dot_tn(r[0], ct)))


@jax.custom_vjp
def _mm_nt(a, b):
    return _bf_dot_nt(a, b)


_mm_nt.defvjp(lambda a, b: (_bf_dot_nt(a, b), (a, b)), lambda r, ct: (_bf_dot(ct, r[1]), _bf_dot_tn(ct, r[0])))


@jax.custom_vjp
def _mm_tn(a, b):
    return _bf_dot_tn(a, b)


_mm_tn.defvjp(lambda a, b: (_bf_dot_tn(a, b), (a, b)), lambda r, ct: (_bf_dot_nt(r[1], ct), _bf_dot(r[0], ct)))


def _hgrn_chunk(tri, tri_t, masks, qraw, fpre, v, st, lb):
    q = _silu(qraw)
    f = lb + (1.0 - lb) * _sigmoid(fpre)
    k = 1.0 - f
    e = _hgrn_sums(tri, tri_t, jnp.log(f))
    prefix, suffix, whole = e[0], e[1], e[-1]
    scores = masks[0] * _mm_nt(q, k)
    for n in range(len(HG_LEVELS)):
        decay = jnp.exp(e[2 + n])
        scores = scores + masks[n + 1] * _mm_nt(q * decay, k * decay)
    o = _mm_nt(q * jnp.exp(prefix), st) + _mm(scores, v)
    st_new = st * jnp.exp(whole) + _mm_tn(v, k * jnp.exp(suffix))
    return o, st_new


HG_HEADS_PER_STEP = 2
HG_LANES = HG_HEADS_PER_STEP * HG_DIM


def _hgrn_specs(S, consts):
    col = lambda base: pl.BlockSpec((S, HG_LANES), lambda p, base=base: (0, base // HG_HEADS_PER_STEP + p))
    whole = [pl.BlockSpec(a.shape, lambda p, n=a.ndim: (0,) * n) for a in consts]
    return col, whole


def _hgrn_fwd(proj, lbs, consts, job=None):
    S = proj.shape[0]
    nc = S // HG_CHUNK
    heads = range(HG_HEADS_PER_STEP)

    def body(q_ref, f_ref, i_ref, lb_ref, tri_ref, trit_ref, mask_ref, o_ref, st_ref):
        tri, tri_t = tri_ref[...], trit_ref[...]
        masks = [mask_ref[n] for n in range(len(HG_LEVELS) + 1)]

        def chunk(ci, sts):
            r0 = pl.multiple_of(ci * HG_CHUNK, HG_CHUNK)
            rows = pl.ds(r0, HG_CHUNK)
            new = []
            for hd in heads:
                lanes = pl.ds(hd * HG_DIM, HG_DIM)
                st_ref[hd, ci] = sts[hd]
                o, st_new = _hgrn_chunk(tri, tri_t, masks, q_ref[rows, lanes], f_ref[rows, lanes], i_ref[rows, lanes], sts[hd], lb_ref[hd])
                o_ref[rows, lanes] = o
                new.append(st_new)
            return tuple(new)

        lax.fori_loop(0, nc, chunk, tuple(jnp.zeros((HG_DIM, HG_DIM), F32) for _ in heads))

    col, whole = _hgrn_specs(S, consts)
    return _hosted(
        body, job, (proj, proj, proj, lbs, *consts), name="hgrn_fwd",
        grid=(WIDTH // HG_LANES,),
        in_specs=[col(16), col(20), col(24), pl.BlockSpec((HG_HEADS_PER_STEP, 1, 128), lambda p: (p, 0, 0))] + whole,
        out_specs=[col(0), pl.BlockSpec((HG_HEADS_PER_STEP, nc, HG_DIM, HG_DIM), lambda p: (p, 0, 0, 0))],
        out_shape=[jax.ShapeDtypeStruct((S, WIDTH), F32), jax.ShapeDtypeStruct((WIDTH // 128, nc, HG_DIM, HG_DIM), F32)],
        semantics=("parallel",))


def _hgrn_bwd(proj, lbs, states, d_o, consts, job=None):
    S = proj.shape[0]
    nc = S // HG_CHUNK

    def body(q_ref, f_ref, i_ref, lb_ref, st_ref, do_ref, tri_ref, trit_ref, mask_ref, dq_ref, df_ref, di_ref, dlb_ref):
        masks = [mask_ref[n] for n in range(len(HG_LEVELS) + 1)]
        fn = functools.partial(_hgrn_chunk, tri_ref[...], trit_ref[...], masks)
        heads = range(HG_HEADS_PER_STEP)

        def chunk(n, carry):
            ci = nc - 1 - n
            r0 = pl.multiple_of(ci * HG_CHUNK, HG_CHUNK)
            rows = pl.ds(r0, HG_CHUNK)
            new = []
            for hd in heads:
                d_st, dlb = carry[hd]
                lanes = pl.ds(hd * HG_DIM, HG_DIM)
                _, pull = jax.vjp(fn, q_ref[rows, lanes], f_ref[rows, lanes], i_ref[rows, lanes], st_ref[hd, ci], lb_ref[hd])
                dq, df, di, d_prev, dl = pull((do_ref[rows, lanes], d_st))
                dq_ref[rows, lanes] = dq.astype(BF)
                df_ref[rows, lanes] = df.astype(BF)
                di_ref[rows, lanes] = di.astype(BF)
                new.append((d_prev, dlb + dl))
            return tuple(new)

        zero = (jnp.zeros((HG_DIM, HG_DIM), F32), jnp.zeros((1, HG_DIM), F32))
        done = lax.fori_loop(0, nc, chunk, tuple(zero for _ in heads))
        for hd in heads:
            dlb_ref[hd] = done[hd][1]

    col, whole = _hgrn_specs(S, consts)
    head = pl.BlockSpec((HG_HEADS_PER_STEP, 1, 128), lambda p: (p, 0, 0))
    out = jax.ShapeDtypeStruct((S, WIDTH), BF)
    return _hosted(
        body, job, (proj, proj, proj, lbs, states, d_o, *consts), name="hgrn_bwd",
        grid=(WIDTH // HG_LANES,),
        in_specs=[col(16), col(20), col(24), head, pl.BlockSpec((HG_HEADS_PER_STEP, nc, HG_DIM, HG_DIM), lambda p: (p, 0, 0, 0)), col(0)] + whole,
        out_specs=[col(0), col(0), col(0), head],
        out_shape=[out, out, out, jax.ShapeDtypeStruct((WIDTH // 128, 1, 128), F32)],
        semantics=("parallel",))


def _shift_down(x, n):
    rows = lax.broadcasted_iota(jnp.int32, x.shape, 0)
    return jnp.where(rows >= n, pltpu.roll(x, n, 0), 0.0)


def _shift_up(x, n):
    S = x.shape[0]
    rows = lax.broadcasted_iota(jnp.int32, x.shape, 0)
    return jnp.where(rows < S - n, pltpu.roll(x, S - n, 0), 0.0)


def _branch_fwd(proj, o_a, o_b, norm_w, conv_w, layer):
    S = proj.shape[0]

    def body(oa_ref, za_ref, ob_ref, zb_ref, nw_ref, pre_ref, post_ref, u_ref, zc_ref, cw_ref, ya_ref, yb_ref, yc_ref):
        ya_ref[...] = (oa_ref[...] * _silu(za_ref[...])).astype(BF)
        ob = ob_ref[...]
        rn = lax.rsqrt(jnp.mean(ob * ob, axis=-1, keepdims=True) + RMS_EPS)
        yb_ref[...] = (ob * rn * nw_ref[layer:layer + 1, :] * _silu(zb_ref[...])).astype(BF)
        pu = pre_ref[...] * u_ref[...]
        conv = cw_ref[2:3, :] * pu + cw_ref[1:2, :] * _shift_down(pu, 1) + cw_ref[0:1, :] * _shift_down(pu, 2)
        yc_ref[...] = (post_ref[...] * conv * _silu(zc_ref[...])).astype(BF)

    col = lambda base: pl.BlockSpec((S, 128), lambda p, base=base: (0, base + p))
    out = jax.ShapeDtypeStruct((S, WIDTH), BF)
    return pl.pallas_call(
        body, name="branch_fwd",
        grid=(WIDTH // 128,),
        in_specs=[col(0), col(12), col(0), col(28), pl.BlockSpec(norm_w.shape, lambda p: (0, 0)),
                  col(32), col(36), col(40), col(44), pl.BlockSpec((None, None, 3, 128), lambda p: (p, layer, 0, 0))],
        out_specs=[col(0), col(0), col(0)],
        out_shape=[out, out, out],
        compiler_params=_params(dimension_semantics=("parallel",)),
    )(o_a, proj, o_b, proj, norm_w, proj, proj, proj, proj, conv_w)


def _branch_bwd(proj, o_a, o_b, norm_w, conv_w, dy_a, dy_b, dy_c, layer):
    S = proj.shape[0]

    def dsilu(z):
        s = _sigmoid(z)
        return s * z, s * (1.0 + z * (1.0 - s))

    def body(oa_ref, za_ref, ob_ref, zb_ref, nw_ref, pre_ref, post_ref, u_ref, zc_ref, cw_ref, dya_ref, dyb_ref, dyc_ref,
             doa_ref, dob_ref, dza_ref, dzb_ref, dpre_ref, dpost_ref, du_ref, dzc_ref, dnw_ref, dcw_ref):
        dya = dya_ref[...]
        sa, dsa = dsilu(za_ref[...])
        doa_ref[...] = dya * sa
        dza_ref[...] = (dya * oa_ref[...] * dsa).astype(BF)

        dyb = dyb_ref[...]
        ob = ob_ref[...]
        nw = nw_ref[layer:layer + 1, :]
        sb, dsb = dsilu(zb_ref[...])
        rn = lax.rsqrt(jnp.mean(ob * ob, axis=-1, keepdims=True) + RMS_EPS)
        on = ob * rn
        dzb_ref[...] = (dyb * on * nw * dsb).astype(BF)
        don_w = dyb * sb
        dnw_ref[...] = jnp.sum(don_w * on, axis=0, keepdims=True)
        don = don_w * nw
        dob_ref[...] = rn * (don - on * jnp.mean(don * on, axis=-1, keepdims=True))

        dyc = dyc_ref[...]
        pre, post, u = pre_ref[...], post_ref[...], u_ref[...]
        sc, dsc = dsilu(zc_ref[...])
        pu = pre * u
        pu1, pu2 = _shift_down(pu, 1), _shift_down(pu, 2)
        conv = cw_ref[2:3, :] * pu + cw_ref[1:2, :] * pu1 + cw_ref[0:1, :] * pu2
        dzc_ref[...] = (dyc * post * conv * dsc).astype(BF)
        dpost_ref[...] = (dyc * conv * sc).astype(BF)
        dconv = dyc * post * sc
        dcw_ref[0:1, :] = jnp.sum(dconv * pu2, axis=0, keepdims=True)
        dcw_ref[1:2, :] = jnp.sum(dconv * pu1, axis=0, keepdims=True)
        dcw_ref[2:3, :] = jnp.sum(dconv * pu, axis=0, keepdims=True)
        dpu = cw_ref[2:3, :] * dconv + cw_ref[1:2, :] * _shift_up(dconv, 1) + cw_ref[0:1, :] * _shift_up(dconv, 2)
        dpre_ref[...] = (dpu * u).astype(BF)
        du_ref[...] = (dpu * pre).astype(BF)

    col = lambda base: pl.BlockSpec((S, 128), lambda p, base=base: (0, base + p))
    f32 = jax.ShapeDtypeStruct((S, WIDTH), F32)
    bf = jax.ShapeDtypeStruct((S, WIDTH), BF)
    return pl.pallas_call(
        body, name="branch_bwd",
        grid=(WIDTH // 128,),
        in_specs=[col(0), col(12), col(0), col(28), pl.BlockSpec(norm_w.shape, lambda p: (0, 0)),
                  col(32), col(36), col(40), col(44), pl.BlockSpec((None, None, 3, 128), lambda p: (p, layer, 0, 0)),
                  col(0), col(0), col(0)],
        out_specs=[col(0)] * 8 + [pl.BlockSpec((None, 1, 128), lambda p: (p, 0, 0)), pl.BlockSpec((None, 3, 128), lambda p: (p, 0, 0))],
        out_shape=[f32, f32, bf, bf, bf, bf, bf, bf, jax.ShapeDtypeStruct((WIDTH // 128, 1, 128), F32),
                   jax.ShapeDtypeStruct((WIDTH // 128, 3, 128), F32)],
        compiler_params=_params(dimension_semantics=("parallel",)),
    )(o_a, proj, o_b, proj, norm_w, proj, proj, proj, proj, conv_w, dy_a, dy_b, dy_c)


def _branch_proj(y_refs, wb_ref):
    out = []
    for i in range(3):
        yv = y_refs[i][...]
        out.append(jnp.concatenate([_dot(yv, wb_ref[j, i]) for j in range(N_CHIPS)], axis=1))
    return out


def _merge_fwd(x, mod, proj, ys, wb, wo, ln_g, ln_b, layer, tm=256):
    S, D = x.shape
    tm = min(tm, S)

    def body(x_ref, mod_ref, ga_ref, gb_ref, gc_ref, ya_ref, yb_ref, yc_ref, wb_ref, wo_ref, g_ref, b_ref, xo_ref, mg_ref, y_ref):
        ps = _branch_proj((ya_ref, yb_ref, yc_ref), wb_ref)
        merged = _sigmoid(ga_ref[...]) * ps[0] + _sigmoid(gb_ref[...]) * ps[1] + _sigmoid(gc_ref[...]) * ps[2]
        mb = merged.astype(BF)
        mg_ref[...] = mb
        y = _dot(mb, wo_ref[...].reshape(D, D))
        y_ref[...] = y
        r = ALPHA * x_ref[...] + (1.0 + mod_ref[:, 2 * D:3 * D]) * y
        xn, _ = _standardize(r)
        xo_ref[...] = xn * g_ref[layer:layer + 1, :] + b_ref[layer:layer + 1, :]

    row = lambda w, c=0: pl.BlockSpec((tm, w), lambda i, c=c: (i, c))
    whole = lambda a: pl.BlockSpec(a.shape, lambda i, n=a.ndim: (0,) * n)
    return pl.pallas_call(
        body, name="merge_fwd",
        grid=(S // tm,),
        in_specs=[row(D), whole(mod), row(D, 6), row(D, 7), row(D, 8), row(WIDTH), row(WIDTH), row(WIDTH), whole(wb),
                  whole(wo), whole(ln_g), whole(ln_b)],
        out_specs=[row(D), row(D), row(D)],
        out_shape=[jax.ShapeDtypeStruct((S, D), F32), jax.ShapeDtypeStruct((S, D), BF), jax.ShapeDtypeStruct((S, D), F32)],
        compiler_params=_params(dimension_semantics=("parallel",)),
    )(x, mod, proj, proj, proj, *ys, wb, wo, ln_g, ln_b)


def _merge_bwd(dxo, x, y, mod, proj, ys, wb, wo, ln_g, layer, tm=256):
    S, D = x.shape
    tm = min(tm, S)

    def body(dxo_ref, x_ref, y_ref, mod_ref, ga_ref, gb_ref, gc_ref, ya_ref, yb_ref, yc_ref, wb_ref, wo_ref, g_ref,
             dxr_ref, dy_ref, dp_ref, dg_ref, dya_ref, dyb_ref, dyc_ref, dlg_ref, dlb_ref, dgt_ref):
        @pl.when(pl.program_id(0) == 0)
        def _():
            dlg_ref[...] = jnp.zeros_like(dlg_ref)
            dlb_ref[...] = jnp.zeros_like(dlb_ref)
            dgt_ref[...] = jnp.zeros_like(dgt_ref)

        gate1 = 1.0 + mod_ref[:, 2 * D:3 * D]
        yv = y_ref[...]
        xn, rstd = _standardize(ALPHA * x_ref[...] + gate1 * yv)
        dxo = dxo_ref[...]
        dlg_ref[...] += jnp.sum(dxo * xn, axis=0, keepdims=True)
        dlb_ref[...] += jnp.sum(dxo, axis=0, keepdims=True)
        dr = _standardize_bwd(dxo * g_ref[layer:layer + 1, :], xn, rstd)
        dxr_ref[...] = ALPHA * dr
        dgt_ref[...] += jnp.sum(dr * yv, axis=0, keepdims=True)
        dyb = (gate1 * dr).astype(BF)
        dy_ref[...] = dyb
        dmerged = _dot_nt(dyb, wo_ref[...].reshape(D, D))
        ps = _branch_proj((ya_ref, yb_ref, yc_ref), wb_ref)
        quarter = D // N_CHIPS
        for i, (gate_ref, out_ref) in enumerate(((ga_ref, dya_ref), (gb_ref, dyb_ref), (gc_ref, dyc_ref))):
            sg = _sigmoid(gate_ref[...])
            dg_ref[:, i * D:(i + 1) * D] = (dmerged * ps[i] * sg * (1.0 - sg)).astype(BF)
            dp = (dmerged * sg).astype(BF)
            dp_ref[:, i * D:(i + 1) * D] = dp
            acc = None
            for j in range(N_CHIPS):
                t = _dot_nt(dp[:, j * quarter:(j + 1) * quarter], wb_ref[j, i])
                acc = t if acc is None else acc + t
            out_ref[...] = acc

    row = lambda w, c=0: pl.BlockSpec((tm, w), lambda i, c=c: (i, c))
    whole = lambda a: pl.BlockSpec(a.shape, lambda i, n=a.ndim: (0,) * n)
    vec = pl.BlockSpec((1, D), lambda i: (0, 0))
    sd = jax.ShapeDtypeStruct
    return pl.pallas_call(
        body, name="merge_bwd",
        grid=(S // tm,),
        in_specs=[row(D), row(D), row(D), whole(mod), row(D, 6), row(D, 7), row(D, 8), row(WIDTH), row(WIDTH), row(WIDTH), whole(wb),
                  whole(wo), whole(ln_g)],
        out_specs=[row(D), row(D), row(3 * D), row(3 * D), row(WIDTH), row(WIDTH), row(WIDTH), vec, vec, vec],
        out_shape=[sd((S, D), F32), sd((S, D), BF), sd((S, 3 * D), BF), sd((S, 3 * D), BF), sd((S, WIDTH), F32), sd((S, WIDTH), F32),
                   sd((S, WIDTH), F32), sd((1, D), F32), sd((1, D), F32), sd((1, D), F32)],
        compiler_params=_params(dimension_semantics=("arbitrary",)),
    )(dxo, x, y, mod, proj, proj, proj, *ys, wb, wo, ln_g)


def _loss_head(x, target, tm=512):
    S, D = x.shape
    tm = min(tm, S)

    def body(x_ref, t_ref, dx_ref, loss_ref):
        @pl.when(pl.program_id(0) == 0)
        def _():
            loss_ref[...] = jnp.zeros_like(loss_ref)

        err = x_ref[...] - t_ref[...]
        dx_ref[...] = err * (1.0 / D)
        loss_ref[...] += 0.5 * jnp.sum(jnp.mean(err * err, axis=-1, keepdims=True))

    row = pl.BlockSpec((tm, D), lambda i: (i, 0))
    return pl.pallas_call(
        body, name="loss_head",
        grid=(S // tm,),
        in_specs=[row, row],
        out_specs=[row, pl.BlockSpec((8, 128), lambda i: (0, 0))],
        out_shape=[jax.ShapeDtypeStruct((S, D), F32), jax.ShapeDtypeStruct((8, 128), F32)],
        compiler_params=_params(dimension_semantics=("arbitrary",)),
    )(x, target)


def _proj_bwd(dproj, wgs, x, mod, dx_res, job=None, tm=512, tk=768):
    S, D = x.shape
    tm = min(tm, S)
    n = len(wgs)
    w_args = list(wgs)
    if n == 1:
        per = wgs[0].shape[-1] // tk
        w_specs = [pl.BlockSpec((None, D, tk), lambda i, k: (k // per, 0, k % per))]
    else:
        assert wgs[0].shape[-1] == tk
        w_specs = [pl.BlockSpec((None, D, tk), lambda i, k, c=c: (jnp.minimum((k + n - 1 - c) // n, N_CHIPS - 1), 0, 0))
                   for c in range(n)]
    nk = IN_COLS // tk

    def body(dp_ref, *rest):
        w_refs, (x_ref, mod_ref, dxr_ref, dx_ref, dsh_ref, dsc_ref, acc) = rest[:n], rest[n:]
        i, k = pl.program_id(0), pl.program_id(1)

        @pl.when((i == 0) & (k == 0))
        def _():
            dsh_ref[...] = jnp.zeros_like(dsh_ref)
            dsc_ref[...] = jnp.zeros_like(dsc_ref)

        @pl.when(k == 0)
        def _():
            acc[...] = jnp.zeros_like(acc)

        for c in range(n):
            @pl.when(k % n == c)
            def _(c=c):
                acc[...] += _dot_nt(dp_ref[...], w_refs[c][...])

        @pl.when(k == nk - 1)
        def _():
            dh = acc[...]
            xs, rstd = _standardize(x_ref[...])
            dsh_ref[...] += jnp.sum(dh, axis=0, keepdims=True)
            dsc_ref[...] += jnp.sum(dh * xs, axis=0, keepdims=True)
            dx_ref[...] = _standardize_bwd(dh * (1.0 + mod_ref[:, D:2 * D]), xs, rstd) + dxr_ref[...]

    row = pl.BlockSpec((tm, D), lambda i, k: (i, 0))
    vec = pl.BlockSpec((1, D), lambda i, k: (0, 0))
    return _hosted(
        body, job, (dproj, *w_args, x, mod, dx_res), name="proj_bwd",
        grid=(S // tm, nk),
        in_specs=[pl.BlockSpec((tm, tk), lambda i, k: (i, k))] + w_specs + [row, pl.BlockSpec((1, 3 * D), lambda i, k: (0, 0)), row],
        out_specs=[row, vec, vec],
        out_shape=[jax.ShapeDtypeStruct((S, D), F32), jax.ShapeDtypeStruct((1, D), F32), jax.ShapeDtypeStruct((1, D), F32)],
        scratch_shapes=[pltpu.VMEM((tm, D), F32)],
        semantics=("arbitrary", "arbitrary"))


def _grad_w(a, b, mid, prev, name, tm, tn, b_col0=0):
    S, M = a.shape
    shape = prev.shape
    n_shard = shape[-1]
    per = n_shard // tn
    nm = M // tm
    lead = len(mid)

    def body(*refs):
        a_ref, b_ref, o_ref = refs[0], refs[1], refs[-1]
        o_ref[...] = _dot_tn(a_ref[...], b_ref[...]).astype(BF)

    in_specs = [pl.BlockSpec((S, tm), lambda m, n: (0, m)),
                pl.BlockSpec((S, tn), lambda m, n: (0, b_col0 // tn + n))]
    args = [a, b]
    aliases = {}
    if not isinstance(prev, jax.ShapeDtypeStruct):
        in_specs.append(pl.BlockSpec(memory_space=pl.ANY))
        args.append(prev)
        aliases = {2: 0}
    return pl.pallas_call(
        body, name=name,
        grid=(nm, N_CHIPS * per),
        in_specs=in_specs,
        out_specs=pl.BlockSpec((None,) + (None,) * lead + (tm, tn), lambda m, n: (n // per,) + tuple(mid) + (m, n % per)),
        out_shape=jax.ShapeDtypeStruct(shape, BF),
        input_output_aliases=aliases,
        compiler_params=_params(dimension_semantics=("parallel", "parallel")),
    )(*args)


def _all_gather8(x, name):
    R, N = x.shape

    def body(x_ref, out_ref, send_sems, recv_sems):
        mx, my, mc = lax.axis_index("x"), lax.axis_index("y"), lax.axis_index("c")
        me = 4 * mx + 2 * my + mc
        out_ref[me] = x_ref[...]
        copies = []
        for k in range(1, N_DEV):
            peer = (_flip(mx, k & 4), _flip(my, k & 2), _flip(mc, k & 1))
            cp = pltpu.make_async_remote_copy(src_ref=x_ref, dst_ref=out_ref.at[me], send_sem=send_sems.at[k - 1],
                                              recv_sem=recv_sems.at[k - 1], device_id=peer, device_id_type=MESH)
            cp.start()
            copies.append(cp)
        for cp in copies:
            cp.wait()

    return pl.pallas_call(
        body, name=name,
        in_specs=[pl.BlockSpec(memory_space=pltpu.VMEM)],
        out_specs=pl.BlockSpec(memory_space=pltpu.VMEM),
        out_shape=jax.ShapeDtypeStruct((N_DEV, R, N), F32),
        scratch_shapes=[pltpu.SemaphoreType.DMA((N_DEV - 1,)), pltpu.SemaphoreType.DMA((N_DEV - 1,))],
        compiler_params=_params(),
    )(x)


def _rows2d(a):
    return a.reshape(-1, a.shape[-1])


def _tile_rows(rows, cols, n_arrays):
    budget = (24 << 20) // (n_arrays * 2 * 4 * cols)
    if rows <= budget:
        return rows
    tm = 8
    for cand in range(8, budget + 1, 8):
        if rows % cand == 0:
            tm = cand
    return tm


SUM_ROWS = 256


def _sum_cores(g, sent, where):
    chips, lead, _, r, cols = g.shape
    tr = min(r, SUM_ROWS)

    def body(where_ref, g_ref, s_ref, out_ref):
        out_ref[...] = (g_ref[...].astype(F32) + s_ref[...].astype(F32)).astype(BF)

    spec = pl.BlockSpec((None, tr, cols), lambda i, j, where_ref: (i, j, 0))
    out = pl.pallas_call(
        body, name="sum_cores",
        grid_spec=pltpu.PrefetchScalarGridSpec(
            num_scalar_prefetch=1, grid=(chips * lead, r // tr),
            in_specs=[pl.BlockSpec((None, None, tr, cols), lambda i, j, where_ref: (i, where_ref[1], j, 0)), spec],
            out_specs=spec),
        out_shape=jax.ShapeDtypeStruct((chips * lead, r, cols), BF),
        compiler_params=_params(dimension_semantics=("parallel", "parallel")),
    )(where, g.reshape(chips * lead, 2, r, cols), sent.reshape(chips * lead, r, cols))
    return out.reshape(chips, lead, r, cols)


def _sum_chips(sums, got, where):
    _, lead, r, cols = sums.shape
    tr = min(r, SUM_ROWS)

    def body(where_ref, s_ref, g_ref, out_ref):
        out_ref[...] = ((s_ref[...].astype(F32) + g_ref[0].astype(F32)) + g_ref[1].astype(F32)) + g_ref[2].astype(F32)

    return pl.pallas_call(
        body, name="sum_chips",
        grid_spec=pltpu.PrefetchScalarGridSpec(
            num_scalar_prefetch=1, grid=(lead, r // tr),
            in_specs=[pl.BlockSpec((None, None, tr, cols), lambda i, j, where_ref: (where_ref[0], i, j, 0)),
                      pl.BlockSpec((N_CHIPS - 1, None, tr, cols), lambda i, j, where_ref: (0, i, j, 0))],
            out_specs=pl.BlockSpec((None, None, tr, cols), lambda i, j, where_ref: (i, where_ref[1], j, 0))),
        out_shape=jax.ShapeDtypeStruct((lead, 2, r, cols), F32),
        compiler_params=_params(dimension_semantics=("parallel", "parallel")),
    )(where, sums, got)


def _adamw(w, m, v, groups, name):
    shape = w.shape
    w2, m2, v2 = _rows2d(w), _rows2d(m), _rows2d(v)
    rows, cols = w2.shape
    ng = len(groups)
    n = len(groups[0])
    slab = rows // ng
    gs = [_rows2d(g) for grp in groups for g in grp]
    tm = _tile_rows(slab, cols, 7 + n)
    tiles = slab // tm
    c1 = 1.0 / (1.0 - ADAM_B1 ** ADAM_STEP)
    c2 = 1.0 / (1.0 - ADAM_B2 ** ADAM_STEP)

    def body(*refs):
        w_ref, m_ref, v_ref = refs[:3]
        g_refs = refs[3:3 + ng * n]
        go_ref, d_ref, mo_ref, vo_ref = refs[3 + ng * n:]
        which = pl.program_id(0)
        for s in range(ng):
            @pl.when(which == s)
            def _(s=s):
                g = g_refs[s * n][...]
                for r in g_refs[s * n + 1:(s + 1) * n]:
                    g = g + r[...]
                mn = ADAM_B1 * m_ref[...] + (1.0 - ADAM_B1) * g
                vn = ADAM_B2 * v_ref[...] + (1.0 - ADAM_B2) * (g * g)
                go_ref[...] = g
                mo_ref[...] = mn
                vo_ref[...] = vn
                d_ref[...] = -ADAM_LR * ((mn * c1) / (jnp.sqrt(vn * c2) + ADAM_EPS) + ADAM_WD * w_ref[...])

    spec = pl.BlockSpec((tm, cols), lambda s, i: (s * tiles + i, 0))
    g_specs = [pl.BlockSpec((tm, cols), lambda s, i, k=k: (jnp.where(s == k, i, jnp.where(s < k, 0, tiles - 1)), 0))
               for k in range(ng) for _ in range(n)]
    outs = pl.pallas_call(
        body, name=name,
        grid=(ng, tiles),
        in_specs=[spec] * 3 + g_specs,
        out_specs=[spec] * 4,
        out_shape=[jax.ShapeDtypeStruct((rows, cols), F32)] * 4,
        compiler_params=_params(dimension_semantics=("arbitrary", "arbitrary")),
    )(w2, m2, v2, *gs)
    return [o.reshape(shape) for o in outs]


def _lower_bounds(r0, r1):
    top = jnp.maximum(r0, r1)
    e0, e1 = jnp.exp(r0 - top), jnp.exp(r1 - top)
    p0, p1 = e0 / (e0 + e1), e1 / (e0 + e1)
    return p0 - p0, (p0 + p1) - p0


def _lbs_fwd(lb_raw):
    def body(lb_ref, out_ref):
        l0, l1 = _lower_bounds(lb_ref[0:1, :], lb_ref[1:2, :])
        out_ref[0:1, :] = l0
        out_ref[1:2, :] = l1

    return pl.pallas_call(body, name="lower_bounds", out_shape=jax.ShapeDtypeStruct(lb_raw.shape, F32), compiler_params=_params())(lb_raw)


def _mod_rows(c_all, w_mod, tn=768):
    _, D, cols = w_mod.shape

    def body(c_ref, w_ref, out_ref):
        out_ref[...] = _dot(c_ref[...].astype(BF), w_ref[...].astype(BF))

    return pl.pallas_call(
        body, name="mod_rows",
        grid=(DEPTH,),
        in_specs=[pl.BlockSpec((N_DEV, D), lambda l: (0, 0)), pl.BlockSpec((None, D, cols), lambda l: (l, 0, 0))],
        out_specs=pl.BlockSpec((N_DEV, cols), lambda l: (0, l)),
        out_shape=jax.ShapeDtypeStruct((N_DEV, DEPTH * cols), F32),
        compiler_params=_params(dimension_semantics=("parallel",)),
    )(c_all, w_mod)


def _grad_w_mod(c_all, dmod_cols):
    D = c_all.shape[1]
    cols = dmod_cols.shape[-1]

    def body(c_ref, d_ref, out_ref):
        out_ref[...] = _dot_tn(c_ref[...].astype(BF), d_ref[...].astype(BF))

    return pl.pallas_call(
        body, name="grad_w_mod",
        grid=(DEPTH,),
        in_specs=[pl.BlockSpec((N_DEV, D), lambda l: (0, 0)), pl.BlockSpec((None, N_DEV, cols), lambda l: (l, 0, 0))],
        out_specs=pl.BlockSpec((None, D, cols), lambda l: (l, 0, 0)),
        out_shape=jax.ShapeDtypeStruct((DEPTH, D, cols), F32),
        compiler_params=_params(dimension_semantics=("parallel",)),
    )(c_all, dmod_cols)


def _sum_devices(parts):
    _, R, N = parts.shape

    def body(p_ref, out_ref):
        acc = p_ref[0]
        for d in range(1, N_DEV):
            acc = acc + p_ref[d]
        out_ref[...] = acc

    return pl.pallas_call(body, name="sum_devices", out_shape=jax.ShapeDtypeStruct((R, N), F32), compiler_params=_params())(parts)


def _lbs_bwd(lb_raw, dl):
    def body(lb_ref, dl_ref, out_ref):
        _, pull = jax.vjp(_lower_bounds, lb_ref[0:1, :], lb_ref[1:2, :])
        d0, d1 = pull((dl_ref[0:1, :], dl_ref[1:2, :]))
        out_ref[0:1, :] = d0
        out_ref[1:2, :] = d1

    return pl.pallas_call(body, name="lower_bounds_bwd", out_shape=jax.ShapeDtypeStruct(lb_raw.shape, F32), compiler_params=_params())(lb_raw, dl)


def kernel(x, c, w_mod, b_mod, w_in, conv_w, hgrn_norm_w, lower_bounds, w_branch, w_out, ln_g, ln_b, loss_target, m_w_mod, m_b_mod, m_w_in, m_conv_w, m_hgrn_norm_w, m_lower_bounds, m_w_branch, m_w_out, m_ln_g, m_ln_b, v_w_mod, v_b_mod, v_w_in, v_conv_w, v_hgrn_norm_w, v_lower_bounds, v_w_branch, v_w_out, v_ln_g, v_ln_b):
    D = D_MODEL
    x0 = x[0]
    target = loss_target[0]
    S = x0.shape[0]
    mx, my, mc = lax.axis_index("x"), lax.axis_index("y"), lax.axis_index("c")
    chip = 2 * mx + my
    me = 2 * chip + mc
    mod_cols = 3 * D // N_CHIPS

    plan = _Plan(w_in.astype(BF), w_branch.astype(BF), w_out.astype(BF), chip, mc)
    plan.first()

    n_conv = DEPTH * 3 * (WIDTH // N_CHIPS)
    first = jnp.concatenate([c, conv_w.reshape(1, n_conv), jnp.zeros((1, 2 * D - D - n_conv), F32)], axis=1)
    first = _all_gather8(first.reshape(8, 2 * D // 8), "gather_c").reshape(N_DEV, 2 * D)
    c_all = first[:, :D]
    conv_all = first[:, D:D + n_conv].reshape(N_DEV, DEPTH, 3, WIDTH // N_CHIPS)[0::2]
    mod_part = _all_gather8(_mod_rows(c_all, w_mod), "gather_mod")[0::2]
    mod_part = lax.dynamic_index_in_dim(mod_part, me, axis=1, keepdims=False).reshape(N_CHIPS, DEPTH, mod_cols)
    mods = [(mod_part[:, l].reshape(1, 3 * D) + b_mod[l][None, :]) for l in range(DEPTH)]
    lbs = _lbs_fwd(lower_bounds).reshape(DEPTH, WIDTH // 128, 1, 128)
    loss_blk, dx, small = _local_step(x0, target, mods, lbs, conv_all, hgrn_norm_w, ln_g, ln_b, plan)

    n_mod, n_nw, n_lb, n_ln, n_cw = DEPTH * 3 * D, DEPTH * 128, DEPTH * WIDTH, DEPTH * D, DEPTH * 3 * WIDTH
    row = jnp.concatenate(
        [jnp.concatenate([small[l][0], small[l][1], small[l][2]], axis=1) for l in range(DEPTH)]
        + [jnp.sum(small[l][3], axis=0) for l in range(DEPTH)]
        + [small[l][4].reshape(1, WIDTH) for l in range(DEPTH)]
        + [small[l][5] for l in range(DEPTH)] + [small[l][6] for l in range(DEPTH)]
        + [jnp.transpose(small[l][7], (1, 0, 2)).reshape(1, 3 * WIDTH) for l in range(DEPTH)]
        + [loss_blk[0:1, :]], axis=1)
    n_row = row.shape[1]
    fold = -(-n_row // (8 * 128)) * 128
    rows = jnp.concatenate([row, jnp.zeros((1, 8 * fold - n_row), F32)], axis=1).reshape(8, fold)

    whole, gathered = plan.finish(rows)
    grads = {kind: [[whole[(kind, l)]] for l in range(DEPTH)] for kind in ("in", "br", "out")}

    off_nw = n_mod
    off_lb = off_nw + n_nw
    off_lng = off_lb + n_lb
    off_lnb = off_lng + n_ln
    off_cw = off_lnb + n_ln
    off_loss = off_cw + n_cw
    total = _sum_devices(gathered).reshape(1, 8 * fold)
    gathered = gathered.reshape(N_DEV, 1, 8 * fold)
    d_lower = _lbs_bwd(lower_bounds, total[0, off_lb:off_lng].reshape(DEPTH, WIDTH))
    loss = total[0, off_loss]
    d_b_mod = total[0, :n_mod].reshape(DEPTH, 3 * D)
    d_norm_w = total[0, off_nw:off_lb].reshape(DEPTH, 128)
    d_ln_g = total[0, off_lng:off_lnb].reshape(DEPTH, D)
    d_ln_b = total[0, off_lnb:off_cw].reshape(DEPTH, D)
    d_conv = total[0, off_cw:off_loss].reshape(DEPTH, 3, N_CHIPS, WIDTH // N_CHIPS)
    d_conv = lax.dynamic_index_in_dim(d_conv, chip, axis=2, keepdims=False)
    dmod_all = gathered[:, 0, :n_mod].reshape(N_DEV, DEPTH, N_CHIPS, mod_cols)
    dmod_cols = jnp.transpose(lax.dynamic_index_in_dim(dmod_all, chip, axis=2, keepdims=False), (1, 0, 2))
    d_w_mod = _grad_w_mod(c_all, dmod_cols)

    res = {}
    res["w_mod"] = _adamw(w_mod, m_w_mod, v_w_mod, [[d_w_mod]], "adamw_w_mod")
    res["b_mod"] = _adamw(b_mod, m_b_mod, v_b_mod, [[d_b_mod]], "adamw_b_mod")
    res["w_in"] = _adamw(w_in, m_w_in, v_w_in, grads["in"], "adamw_w_in")
    res["conv_w"] = _adamw(conv_w, m_conv_w, v_conv_w, [[d_conv]], "adamw_conv_w")
    res["hgrn_norm_w"] = _adamw(hgrn_norm_w, m_hgrn_norm_w, v_hgrn_norm_w, [[d_norm_w]], "adamw_norm_w")
    res["lower_bounds"] = _adamw(lower_bounds, m_lower_bounds, v_lower_bounds, [[d_lower]], "adamw_lower_bounds")
    res["w_branch"] = _adamw(w_branch, m_w_branch, v_w_branch, grads["br"], "adamw_w_branch")
    res["w_out"] = _adamw(w_out, m_w_out, v_w_out, grads["out"], "adamw_w_out")
    res["ln_g"] = _adamw(ln_g, m_ln_g, v_ln_g, [[d_ln_g]], "adamw_ln_g")
    res["ln_b"] = _adamw(ln_b, m_ln_b, v_ln_b, [[d_ln_b]], "adamw_ln_b")
    names = ["w_mod", "b_mod", "w_in", "conv_w", "hgrn_norm_w", "lower_bounds", "w_branch", "w_out", "ln_g", "ln_b"]
    return (loss, dx[None], *[res[n][0] for n in names], *[res[n][1] for n in names],
            *[res[n][2] for n in names], *[res[n][3] for n in names])


class _Plan:
    FIRST_CHUNKS = 3
    WINDOWS = ((0, 2 * IN_COLS // N_CHIPS // 3), (2 * IN_COLS // N_CHIPS // 3, IN_COLS // N_CHIPS // 3))

    def __init__(self, w_in, w_br, w_out, chip, core):
        self.local = {"in": w_in, "br": w_br, "out": w_out}
        self.chip, self.where = chip, jnp.stack([chip, core]).astype(jnp.int32)
        self.gathered, self.partial, self.grads, self.chip_sums, self.scattered, self.pending = {}, {}, {}, {}, {}, {}

    def chunks(self, l):
        return self.FIRST_CHUNKS if l == 0 else 1

    def w_in(self, l):
        return [self.gathered[("in", l, c)] for c in range(self.chunks(l))]

    def _shard(self, key):
        mine = self.local[key[0]][key[1]]
        if key[0] == "in":
            cols = mine.shape[-1] // self.chunks(key[1])
            mine = mine[:, key[2] * cols:(key[2] + 1) * cols]
        return mine

    def _slab(self, key):
        mine = _halves(self._shard(key))
        return lax.dynamic_update_slice(lax.empty((N_CHIPS,) + mine.shape, mine.dtype), mine[None], (self.chip, 0, 0, 0, 0))

    def _gather(self, keys):
        return ("gather", keys), _gather_job([self._slab(key) for key in keys])

    def _gather_window(self, key, n):
        slab = self._slab(key) if n == 0 else self.partial[key]
        return ("gather" if n == len(self.WINDOWS) - 1 else "gather_part", [key]), _gather_job([slab], self.WINDOWS[n])

    def _to_sibling(self, keys):
        return ("to_sibling", keys), _to_sibling_job([_halves(self.grads[key], 1) for key in keys])

    def _scatter(self, keys):
        return ("scatter", keys), _scatter_job([self.chip_sums[key] for key in keys])

    def job(self, stage, l, c=0):
        parts = []
        if stage == "proj_fwd":
            parts = [self._gather([("in", l, c + 1)] if c + 1 < self.chunks(l) else [("br", l), ("out", l)])]
        elif stage == "attn_fwd" and l + 1 < DEPTH:
            parts = [self._gather_window(("in", l + 1, 0), 0)]
        elif stage == "hgrn_fwd" and l + 1 < DEPTH:
            parts = [self._gather_window(("in", l + 1, 0), 1)]
        elif stage == "attn_bwd":
            parts = [self._to_sibling([("out", l), ("br", l)])] + ([self._scatter([("in", l + 1)])] if l + 1 < DEPTH else [])
        elif stage == "hgrn_bwd":
            parts = [self._scatter([("out", l), ("br", l)])]
        elif stage == "proj_bwd":
            parts = [self._to_sibling([("in", l)])] if l else [self._scatter([("in", 0)])]
        self.pending[(stage, l, c)] = [(tag, len(job.outs)) for tag, job in parts]
        return _join_jobs([job for _, job in parts])

    def done(self, stage, l, outs, c=0):
        if outs is None:
            return
        at = 0
        for (what, keys), n_outs in self.pending[(stage, l, c)]:
            mine, at = outs[at:at + n_outs], at + n_outs
            for n, key in enumerate(keys):
                if what == "gather":
                    self.gathered[key] = mine[n].reshape((N_CHIPS,) + self._shard(key).shape)
                elif what == "gather_part":
                    self.partial[key] = mine[n]
                elif what == "to_sibling":
                    self.chip_sums[key] = _sum_cores(_halves(self.grads[key], 1), mine[n], self.where)
                else:
                    self.scattered[key] = mine[n]

    def first(self):
        tag, job = self._gather([("in", 0, 0)])
        self.pending[("first", 0, 0)] = [(tag, len(job.outs))]
        self.done("first", 0, _run_job(job, "gather_first"))

    def took(self, key, grad):
        self.grads[key] = grad
        if key == ("in", 0):
            tag, job = self._to_sibling([key])
            self.pending[("took", 0, 0)] = [(tag, len(job.outs))]
            self.done("took", 0, _run_job(job, "to_sibling_last"))

    def finish(self, rows):
        keys = [(kind, l) for kind in ("in", "br", "out") for l in range(DEPTH)]
        halves = [_sum_chips(self.chip_sums[key], self.scattered[key], self.where) for key in keys]
        outs = _run_job(_join_jobs([_place_job(halves), _gather8_job(rows)]), "place_halves")
        return {key: w.reshape(self.grads[key].shape[1:]) for key, w in zip(keys, outs[:-1])}, outs[-1]


def _local_step(x0, target, mods, lbs, conv_all, hgrn_norm_w, ln_g, ln_b, plan):
    D = D_MODEL
    after, before = _attn_consts()
    hg_consts = _hgrn_consts()

    saved = []
    xl = x0
    for l in range(DEPTH):
        n = plan.chunks(l)
        (proj, h), got = _proj_fwd(xl, mods[l], plan.gathered[("in", l, 0)], plan.job("proj_fwd", l, 0), (0, n))
        plan.done("proj_fwd", l, got, 0)
        for c in range(1, n):
            (proj,), got = _proj_cols(h, plan.gathered[("in", l, c)], (c, n), proj, plan.job("proj_fwd", l, c))
            plan.done("proj_fwd", l, got, c)
        (o_a, tot), got = _attn_fwd(proj, after, plan.job("attn_fwd", l))
        plan.done("attn_fwd", l, got)
        (o_b, states), got = _hgrn_fwd(proj, lbs[l], hg_consts, plan.job("hgrn_fwd", l))
        plan.done("hgrn_fwd", l, got)
        ys = _branch_fwd(proj, o_a, o_b, hgrn_norm_w, conv_all, l)
        x_next, merged, y = _merge_fwd(xl, mods[l], proj, ys, plan.gathered[("br", l)], plan.gathered[("out", l)], ln_g, ln_b, l)
        saved.append((xl, proj, h, o_a, tot, o_b, states, ys, merged, y))
        xl = x_next
    dx, loss_blk = _loss_head(xl, target)

    small = [None] * DEPTH
    for l in reversed(range(DEPTH)):
        xin, proj, h, o_a, tot, o_b, states, ys, merged, y = saved[l]
        dx_res, dy, dp, dg, dy_a, dy_b, dy_c, dln_g, dln_b, dgate = _merge_bwd(
            dx, xin, y, mods[l], proj, ys, plan.gathered[("br", l)], plan.gathered[("out", l)], ln_g, l)
        plan.took(("out", l), _grad_w_out(merged, dy))
        g_br = jax.ShapeDtypeStruct((N_CHIPS, 3, WIDTH, D // N_CHIPS), BF)
        for i in range(3):
            g_br = _grad_w(ys[i], dp, (i,), g_br, "grad_w_branch", tm=WIDTH, tn=D // N_CHIPS, b_col0=i * D)
        plan.took(("br", l), g_br)
        d_oa, d_ob, dz_a, dz_b, dpre, dpost, du, dz_c, dnorm_w, dconv_w = _branch_bwd(proj, o_a, o_b, hgrn_norm_w, conv_all, dy_a, dy_b, dy_c, l)
        (dq_a, dk_a, dv_a), got = _attn_bwd(proj, d_oa, tot, after, before, plan.job("attn_bwd", l))
        plan.done("attn_bwd", l, got)
        (dq_b, df_b, di_b, dlb), got = _hgrn_bwd(proj, lbs[l], states, d_ob, hg_consts, plan.job("hgrn_bwd", l))
        plan.done("hgrn_bwd", l, got)
        dproj = jnp.concatenate([dq_a, dk_a, dv_a, dz_a, dq_b, df_b, di_b, dz_b, dpre, dpost, du, dz_c, dg], axis=1)
        plan.took(("in", l), _grad_w(h, dproj, (), jax.ShapeDtypeStruct((N_CHIPS, D, IN_COLS // N_CHIPS), BF), "grad_w_in", tm=512, tn=2304))
        (dx, dshift, dscale), got = _proj_bwd(dproj, plan.w_in(l), xin, mods[l], dx_res, plan.job("proj_bwd", l))
        plan.done("proj_bwd", l, got)
        small[l] = (dshift, dscale, dgate, dnorm_w, dlb, dln_g, dln_b, dconv_w)
    return loss_blk, dx, small


def _grad_w_out(merged, dy):
    S, D = merged.shape
    q = D // N_CHIPS

    def body(a_ref, b_ref, o_ref):
        o_ref[...] = _dot_tn(a_ref[...], b_ref[...]).astype(BF)

    return pl.pallas_call(
        body, name="grad_w_out",
        grid=(N_CHIPS,),
        in_specs=[pl.BlockSpec((S, q), lambda j: (0, j)), pl.BlockSpec((S, D), lambda j: (0, 0))],
        out_specs=pl.BlockSpec((None, q, D), lambda j: (j, 0, 0)),
        out_shape=jax.ShapeDtypeStruct((N_CHIPS, q, D), BF),
        compiler_params=_params(dimension_semantics=("parallel",)),
    )(merged, dy)
```

```python
import functools
import math

import numpy as np
import jax
import jax.numpy as jnp
from jax import lax
from jax.experimental import pallas as pl
from jax.experimental.pallas import tpu as pltpu

F32 = jnp.float32
BF = jnp.bfloat16
MESH = pl.DeviceIdType.MESH

DEPTH = 2
D_MODEL = 1024
WIDTH = 512
IN_COLS = 12 * WIDTH + 3 * D_MODEL
N_CHIPS = 4
N_DEV = 8
SB_BLOCK = 128
SB_HEAD_DIM = 64
HG_CHUNK = 128
HG_DIM = 128
LN_EPS = 1e-5
RMS_EPS = 1e-6
ALPHA = (2.0 * DEPTH) ** 0.25
ADAM_LR, ADAM_B1, ADAM_B2, ADAM_EPS, ADAM_WD, ADAM_STEP = 0.001, 0.9, 0.999, 1e-08, 0.01, 10
VMEM_LIMIT = 56 << 20


def _params(**kw):
    return pltpu.CompilerParams(vmem_limit_bytes=VMEM_LIMIT, **kw)


def _dot(a, b):
    return jnp.dot(a, b, preferred_element_type=F32)


def _dot_nt(a, b):
    return lax.dot_general(a, b, (((1,), (1,)), ((), ())), preferred_element_type=F32)


def _dot_tn(a, b):
    return lax.dot_general(a, b, (((0,), (0,)), ((), ())), preferred_element_type=F32)


def _sigmoid(x):
    return 1.0 / (1.0 + jnp.exp(-x))


def _silu(x):
    return x * _sigmoid(x)


def _softplus(z):
    return jnp.maximum(z, 0.0) + jnp.log(1.0 + jnp.exp(-jnp.abs(z)))


def _split_dot(t, g, terms):
    acc = None
    rest = g
    for _ in range(terms):
        part = rest.astype(BF)
        rest = rest - part.astype(F32)
        d = _dot(t, part)
        acc = d if acc is None else acc + d
    return acc


def _split_dot_r(g, t, terms):
    acc = None
    rest = g
    for _ in range(terms):
        part = rest.astype(BF)
        rest = rest - part.astype(F32)
        d = _dot(part, t)
        acc = d if acc is None else acc + d
    return acc


def _standardize(x):
    mu = jnp.mean(x, axis=-1, keepdims=True)
    xc = x - mu
    var = jnp.mean(xc * xc, axis=-1, keepdims=True)
    rstd = lax.rsqrt(var + LN_EPS)
    return xc * rstd, rstd


def _standardize_bwd(dxs, xs, rstd):
    return rstd * (dxs - jnp.mean(dxs, axis=-1, keepdims=True) - xs * jnp.mean(dxs * xs, axis=-1, keepdims=True))


class _Job:
    def __init__(self, ins, outs, sems, make, alias=None):
        self.ins, self.outs, self.sems, self.make = list(ins), list(outs), list(sems), make
        self.alias = dict(alias or {})


def _join_jobs(jobs):
    jobs = [j for j in jobs if j is not None]
    if len(jobs) <= 1:
        return jobs[0] if jobs else None

    def make(ins, outs, sems):
        phases, i, o, s = [], 0, 0, 0
        for j in jobs:
            got = j.make(ins[i:i + len(j.ins)], outs[o:o + len(j.outs)], sems[s:s + len(j.sems)])
            i, o, s = i + len(j.ins), o + len(j.outs), s + len(j.sems)
            for n, phase in enumerate(got):
                if n == len(phases):
                    phases.append([])
                phases[n] += phase
        return phases

    alias, i, o = {}, 0, 0
    for j in jobs:
        alias.update({i + a: o + b for a, b in j.alias.items()})
        i, o = i + len(j.ins), o + len(j.outs)
    return _Job(sum([j.ins for j in jobs], []), sum([j.outs for j in jobs], []), sum([j.sems for j in jobs], []), make, alias)


def _flip(v, bit):
    return 1 - v if bit else v


def _halves(a, front=0):
    shape = a.shape
    lead = math.prod(shape[front:-2])
    return a.reshape(shape[:front] + (lead, 2, shape[-2] // 2, shape[-1]))


def _dma_sems(*shapes):
    return [pltpu.SemaphoreType.DMA(s) for s in shapes]


def _same(arrays):
    return [jax.ShapeDtypeStruct(a.shape, a.dtype) for a in arrays]


def _gather_job(slabs, window=None):
    n = len(slabs)
    cols = slice(None) if window is None else pl.ds(*window)

    def make(ins, outs, sems):
        send1, recv1, send2, recv2 = sems
        mx, my, mc = lax.axis_index("x"), lax.axis_index("y"), lax.axis_index("c")
        fetch, pass_on = [], []
        for a in range(n):
            ours = outs[a].at[2 * mx + my, :, mc, :, cols]
            for k in range(1, N_CHIPS):
                px, py = _flip(mx, k & 2), _flip(my, k & 1)
                fetch.append(pltpu.make_async_remote_copy(
                    src_ref=ours, dst_ref=ours, send_sem=send1.at[a, k - 1], recv_sem=recv1.at[a, k - 1],
                    device_id=(px, py, mc), device_id_type=MESH))
                theirs = outs[a].at[2 * px + py, :, mc, :, cols]
                pass_on.append(pltpu.make_async_remote_copy(
                    src_ref=theirs, dst_ref=theirs, send_sem=send2.at[a, k - 1], recv_sem=recv2.at[a, k - 1],
                    device_id=(mx, my, 1 - mc), device_id_type=MESH))
        return [fetch, pass_on]

    pairs = (n, N_CHIPS - 1)
    return _Job(slabs, _same(slabs), _dma_sems(pairs, pairs, pairs, pairs), make, {a: a for a in range(n)})


def _to_sibling_job(grads):
    n = len(grads)

    def make(ins, outs, sems):
        send_sems, recv_sems = sems
        mx, my, mc = lax.axis_index("x"), lax.axis_index("y"), lax.axis_index("c")
        return [[pltpu.make_async_remote_copy(
            src_ref=ins[a].at[:, :, 1 - mc], dst_ref=outs[a], send_sem=send_sems.at[a], recv_sem=recv_sems.at[a],
            device_id=(mx, my, 1 - mc), device_id_type=MESH) for a in range(n)]]

    outs = [jax.ShapeDtypeStruct(g.shape[:2] + g.shape[3:], g.dtype) for g in grads]
    return _Job(grads, outs, _dma_sems((n,), (n,)), make)


def _scatter_job(sums):
    n = len(sums)

    def make(ins, outs, sems):
        send_sems, recv_sems = sems
        mx, my, mc = lax.axis_index("x"), lax.axis_index("y"), lax.axis_index("c")
        copies = []
        for a in range(n):
            for k in range(1, N_CHIPS):
                px, py = _flip(mx, k & 2), _flip(my, k & 1)
                copies.append(pltpu.make_async_remote_copy(
                    src_ref=ins[a].at[2 * px + py], dst_ref=outs[a].at[k - 1], send_sem=send_sems.at[a, k - 1],
                    recv_sem=recv_sems.at[a, k - 1], device_id=(px, py, mc), device_id_type=MESH))
        return [copies]

    pairs = (n, N_CHIPS - 1)
    return _Job(sums, [jax.ShapeDtypeStruct((N_CHIPS - 1,) + s.shape[1:], s.dtype) for s in sums], _dma_sems(pairs, pairs), make)


def _place_job(wholes):
    n = len(wholes)

    def make(ins, outs, sems):
        send_sems, recv_sems = sems
        mx, my, mc = lax.axis_index("x"), lax.axis_index("y"), lax.axis_index("c")
        copies = []
        for a in range(n):
            here = outs[a].at[:, mc]
            copies.append(pltpu.make_async_remote_copy(src_ref=here, dst_ref=here, send_sem=send_sems.at[a], recv_sem=recv_sems.at[a],
                                                       device_id=(mx, my, 1 - mc), device_id_type=MESH))
        return [copies]

    return _Job(wholes, _same(wholes), _dma_sems((n,), (n,)), make, {a: a for a in range(n)})


def _gather8_job(x):
    def make(ins, outs, sems):
        local_sem, send_sems, recv_sems = sems
        mx, my, mc = lax.axis_index("x"), lax.axis_index("y"), lax.axis_index("c")
        here = outs[0].at[4 * mx + 2 * my + mc]
        copies = [pltpu.make_async_copy(ins[0], here, local_sem.at[0])]
        for k in range(1, N_DEV):
            peer = (_flip(mx, k & 4), _flip(my, k & 2), _flip(mc, k & 1))
            copies.append(pltpu.make_async_remote_copy(src_ref=ins[0], dst_ref=here, send_sem=send_sems.at[k - 1],
                                                       recv_sem=recv_sems.at[k - 1], device_id=peer, device_id_type=MESH))
        return [copies]

    return _Job([x], [jax.ShapeDtypeStruct((N_DEV,) + x.shape, x.dtype)], _dma_sems((1,), (N_DEV - 1,), (N_DEV - 1,)), make)


def _run_phases(phases, first=0):
    for n, phase in enumerate(phases):
        if n >= first:
            for cp in phase:
                cp.start()
        for cp in phase:
            cp.wait()


def _run_job(job, name):
    k_in, k_out = len(job.ins), len(job.outs)

    def body(*refs):
        _run_phases(job.make(refs[:k_in], refs[k_in:k_in + k_out], refs[k_in + k_out:]))

    hbm = pl.BlockSpec(memory_space=pl.ANY)
    return pl.pallas_call(body, name=name, in_specs=[hbm] * k_in, out_specs=[hbm] * k_out, out_shape=job.outs,
                          scratch_shapes=job.sems, input_output_aliases=job.alias, compiler_params=_params())(*job.ins)


def _hosted(body, job, args, *, name, grid, in_specs, out_specs, out_shape, scratch_shapes=(), semantics, aliases=None):
    in_specs, out_specs, out_shape, scratch = list(in_specs), list(out_specs), list(out_shape), list(scratch_shapes)
    aliases = dict(aliases or {})
    if job is None:
        outs = pl.pallas_call(body, name=name, grid=grid, in_specs=in_specs, out_specs=out_specs, out_shape=out_shape,
                              scratch_shapes=scratch, input_output_aliases=aliases,
                              compiler_params=_params(dimension_semantics=semantics))(*args)
        return list(outs), None
    n_in, n_out, n_scr, k_in, k_out = len(in_specs), len(out_specs), len(scratch), len(job.ins), len(job.outs)

    def wrapped(*refs):
        ins, rest = refs[:n_in], refs[n_in:]
        job_ins, rest = rest[:k_in], rest[k_in:]
        outs, rest = rest[:n_out], rest[n_out:]
        job_outs, rest = rest[:k_out], rest[k_out:]
        scr, sems = rest[:n_scr], rest[n_scr:]
        ids = [pl.program_id(a) for a in range(len(grid))]
        first = functools.reduce(jnp.logical_and, [i == 0 for i in ids])
        last = functools.reduce(jnp.logical_and, [i == g - 1 for i, g in zip(ids, grid)])

        @pl.when(first)
        def _():
            for cp in job.make(job_ins, job_outs, sems)[0]:
                cp.start()

        body(*ins, *outs, *scr)

        @pl.when(last)
        def _():
            _run_phases(job.make(job_ins, job_outs, sems), first=1)

    hbm = pl.BlockSpec(memory_space=pl.ANY)
    outs = pl.pallas_call(
        wrapped, name=name, grid=grid, in_specs=in_specs + [hbm] * k_in, out_specs=out_specs + [hbm] * k_out,
        out_shape=out_shape + job.outs, scratch_shapes=scratch + job.sems,
        input_output_aliases={**aliases, **{n_in + i: n_out + o for i, o in job.alias.items()}},
        compiler_params=_params(dimension_semantics=("arbitrary",) * len(grid)))(*args, *job.ins)
    return list(outs[:n_out]), list(outs[n_out:])


def _proj_fwd(x, mod, wg, job=None, chunk=(0, 1), tm=512):
    S, D = x.shape
    tm = min(tm, S)
    tn = wg.shape[-1]
    c, n = chunk

    def body(x_ref, mod_ref, w_ref, proj_ref, h_ref, hs):
        @pl.when(pl.program_id(1) == 0)
        def _():
            xs, _ = _standardize(x_ref[...])
            h = xs * (1.0 + mod_ref[:, D:2 * D]) + mod_ref[:, 0:D]
            hb = h.astype(BF)
            hs[...] = hb
            h_ref[...] = hb

        proj_ref[...] = _dot(hs[...], w_ref[...])

    return _hosted(
        body, job, (x, mod, wg), name="proj_fwd",
        grid=(S // tm, N_CHIPS),
        in_specs=[pl.BlockSpec((tm, D), lambda i, j: (i, 0)),
                  pl.BlockSpec((1, 3 * D), lambda i, j: (0, 0)),
                  pl.BlockSpec((None, D, tn), lambda i, j: (j, 0, 0))],
        out_specs=[pl.BlockSpec((tm, tn), lambda i, j: (i, j * n + c)),
                   pl.BlockSpec((tm, D), lambda i, j: (i, 0))],
        out_shape=[jax.ShapeDtypeStruct((S, IN_COLS), F32), jax.ShapeDtypeStruct((S, D), BF)],
        scratch_shapes=[pltpu.VMEM((tm, D), BF)],
        semantics=("parallel", "arbitrary"))


def _proj_cols(h, wg, chunk, proj, job=None, tm=512):
    S, D = h.shape
    tm = min(tm, S)
    tn = wg.shape[-1]
    c, n = chunk

    def body(h_ref, w_ref, prev_ref, proj_ref):
        proj_ref[...] = _dot(h_ref[...], w_ref[...])

    in_specs = [pl.BlockSpec((tm, D), lambda i, j: (i, 0)), pl.BlockSpec((None, D, tn), lambda i, j: (j, 0, 0)),
                pl.BlockSpec(memory_space=pl.ANY)]
    return _hosted(body, job, (h, wg, proj), name="proj_cols", grid=(S // tm, N_CHIPS), in_specs=in_specs,
                   out_specs=[pl.BlockSpec((tm, tn), lambda i, j: (i, j * n + c))],
                   out_shape=[jax.ShapeDtypeStruct(proj.shape, proj.dtype)],
                   semantics=("parallel", "arbitrary"), aliases={2: 0})


SB_ROWS = 256
SB_KEYS = 256


def _attn_consts():
    j = np.arange(SB_KEYS)[:, None]
    s = np.arange(SB_KEYS)[None, :]
    from_here = np.concatenate([(j >= s), (j >= s)], axis=0).astype(np.float32)
    up_to = np.concatenate([(j <= s), (j <= s)], axis=0).astype(np.float32)
    return jnp.asarray(from_here, BF), jnp.asarray(up_to, BF)


def _hi_lo(x):
    hi = lax.bitcast_convert_type(lax.bitcast_convert_type(x, jnp.uint32) & jnp.uint32(0xFFFF0000), F32)
    return hi.astype(BF), (x - hi).astype(BF)


def _sums_r(x, t2):
    hi, lo = _hi_lo(x)
    return _dot(jnp.concatenate([hi, lo], axis=1), t2)


def _all_lanes(col, lanes):
    return jnp.broadcast_to(col, (col.shape[0], lanes))


def _attn_rows(ref, r0, rows, lanes, head0, scale=None):
    v = ref[pl.ds(r0, rows), lanes]
    if scale is not None:
        v = v * scale
    return jnp.concatenate([jnp.where(head0, v, 0.0), jnp.where(head0, 0.0, v)], axis=0).astype(BF)


SB_PAIRS_FWD = 4
SB_PAIRS_BWD = 2


def _attn_specs(S, n_pairs):
    return lambda base: pl.BlockSpec((S, n_pairs * SB_BLOCK), lambda p, base=base: (0, base // n_pairs + p))


def _attn_scores(q2n, k_ref, lanes, kj, t2, from_here_ref, masked):
    c0 = pl.multiple_of(kj * SB_KEYS, SB_KEYS)
    kb = k_ref[pl.ds(c0, SB_KEYS), lanes].astype(BF)
    zn = _dot_nt(q2n, kb)
    lsb = jnp.minimum(zn, 0.0) - jnp.log(1.0 + jnp.exp(-jnp.abs(zn)))
    valid = None
    if masked:
        valid = (lax.broadcasted_iota(jnp.int32, zn.shape, 1) + kj * SB_KEYS) < t2
        lsb = jnp.where(valid, lsb, 0.0)
    return c0, kb, zn, valid, lsb, _sums_r(lsb, from_here_ref[...])


def _attn_fwd(proj, from_here, job=None):
    S = proj.shape[0]
    TQ = SB_ROWS
    assert S % TQ == 0 and SB_KEYS == TQ
    scale = SB_HEAD_DIM ** -0.5
    n_pairs = SB_PAIRS_FWD
    pairs = range(n_pairs)
    lanes = [pl.ds(p * SB_BLOCK, SB_BLOCK) for p in pairs]

    def body(q_ref, k_ref, v_ref, from_here_ref, o_ref, tot_ref, run, acc):
        head0 = lax.broadcasted_iota(jnp.int32, (1, 2 * SB_HEAD_DIM), 1) < SB_HEAD_DIM

        def qloop(qi, _):
            r0 = pl.multiple_of(qi * TQ, TQ)
            q2n = [_attn_rows(q_ref, r0, TQ, lanes[p], head0, -scale) for p in pairs]
            trow = lax.broadcasted_iota(jnp.int32, (TQ, SB_KEYS), 0) + qi * TQ
            t2 = jnp.concatenate([trow, trow], axis=0)
            run[...] = jnp.zeros_like(run)
            acc[...] = jnp.zeros_like(acc)

            def step(kj, masked):
                got = [_attn_scores(q2n[p], k_ref, lanes[p], kj, t2, from_here_ref, masked) for p in pairs]
                for p in pairs:
                    c0, _, zn, valid, _, sums = got[p]
                    r = run[p]
                    e = sums - zn + jnp.concatenate([r, r], axis=1)
                    if masked:
                        e = jnp.where(valid, e, -jnp.inf)
                    acc[p] += _dot(jnp.exp(e).astype(BF), v_ref[pl.ds(c0, SB_KEYS), lanes[p]].astype(BF))
                    run[p] = r + _all_lanes(sums[:, 0:1], SB_BLOCK)

            step(qi, True)

            def below(n, _):
                step(qi - 1 - n, False)
                return 0

            lax.fori_loop(0, qi, below, 0)
            for p in pairs:
                o_ref[pl.ds(r0, TQ), lanes[p]] = jnp.where(head0, acc[p, 0:TQ, :], acc[p, TQ:2 * TQ, :])
                tot_ref[p, 0, pl.ds(r0, TQ), :] = run[p, 0:TQ, :]
                tot_ref[p, 1, pl.ds(r0, TQ), :] = run[p, TQ:2 * TQ, :]
            return 0

        lax.fori_loop(0, S // TQ, qloop, 0)

    col = _attn_specs(S, n_pairs)
    state = pltpu.VMEM((n_pairs, 2 * TQ, SB_BLOCK), F32)
    return _hosted(
        body, job, (proj, proj, proj, from_here), name="attn_fwd",
        grid=(WIDTH // (n_pairs * SB_BLOCK),),
        in_specs=[col(0), col(4), col(8), pl.BlockSpec(from_here.shape, lambda p: (0, 0))],
        out_specs=[col(0), pl.BlockSpec((n_pairs, 2, S, 128), lambda p: (p, 0, 0, 0))],
        out_shape=[jax.ShapeDtypeStruct((S, WIDTH), F32), jax.ShapeDtypeStruct((WIDTH // 128, 2, S, 128), F32)],
        scratch_shapes=[state, state],
        semantics=("parallel",))


def _attn_bwd(proj, d_o, tot, from_here, up_to, job=None):
    S = proj.shape[0]
    TQ = SB_ROWS
    assert S % TQ == 0 and SB_KEYS == TQ
    scale = SB_HEAD_DIM ** -0.5
    n_pairs = SB_PAIRS_BWD
    pairs = range(n_pairs)
    lanes = [pl.ds(p * SB_BLOCK, SB_BLOCK) for p in pairs]

    def body(q_ref, k_ref, v_ref, do_ref, tot_ref, from_here_ref, up_to_ref, dq_ref, dk_ref, dv_ref, pre, cum, dq_acc, dk_acc, dv_acc):
        head0 = lax.broadcasted_iota(jnp.int32, (1, 2 * SB_HEAD_DIM), 1) < SB_HEAD_DIM
        dk_acc[...] = jnp.zeros_like(dk_acc)
        dv_acc[...] = jnp.zeros_like(dv_acc)

        def qloop(qi, _):
            r0 = pl.multiple_of(qi * TQ, TQ)
            q2n = [_attn_rows(q_ref, r0, TQ, lanes[p], head0, -scale) for p in pairs]
            do2 = [_attn_rows(do_ref, r0, TQ, lanes[p], head0) for p in pairs]
            trow = lax.broadcasted_iota(jnp.int32, (TQ, SB_KEYS), 0) + qi * TQ
            t2 = jnp.concatenate([trow, trow], axis=0)
            for p in pairs:
                pre[p, 0:TQ, :] = tot_ref[p, 0, pl.ds(r0, TQ), :]
                pre[p, TQ:2 * TQ, :] = tot_ref[p, 1, pl.ds(r0, TQ), :]
            cum[...] = jnp.zeros_like(cum)
            dq_acc[...] = jnp.zeros_like(dq_acc)

            def step(kj, masked):
                got = [_attn_scores(q2n[p], k_ref, lanes[p], kj, t2, from_here_ref, masked) for p in pairs]
                for p in pairs:
                    c0, kb, zn, valid, lsb, sums = got[p]
                    later = pre[p] - _all_lanes(sums[:, 0:1], SB_BLOCK)
                    pre[p] = later
                    e = sums - zn + jnp.concatenate([later, later], axis=1)
                    sig = jnp.exp(lsb - zn)
                    if masked:
                        e = jnp.where(valid, e, -jnp.inf)
                        sig = jnp.where(valid, sig, 0.0)
                    a = jnp.exp(e)
                    w = _dot_nt(do2[p], v_ref[pl.ds(c0, SB_KEYS), lanes[p]].astype(BF)) * a
                    upto = _sums_r(w, up_to_ref[...])
                    c = cum[p]
                    dz = w - sig * (upto + jnp.concatenate([c, c], axis=1))
                    cum[p] = c + _all_lanes(upto[:, SB_KEYS - 1:SB_KEYS], SB_BLOCK)
                    dzb = dz.astype(BF)
                    dq_acc[p] += _dot(dzb, kb)
                    dk_acc[pl.ds(c0, SB_KEYS), lanes[p]] += _dot_tn(dzb, q2n[p])
                    dv_acc[pl.ds(c0, SB_KEYS), lanes[p]] += _dot_tn(a.astype(BF), do2[p])

            def below(kj, _):
                step(kj, False)
                return 0

            lax.fori_loop(0, qi, below, 0)
            step(qi, True)
            for p in pairs:
                dq_ref[pl.ds(r0, TQ), lanes[p]] = (jnp.where(head0, dq_acc[p, 0:TQ, :], dq_acc[p, TQ:2 * TQ, :]) * scale).astype(BF)
            return 0

        lax.fori_loop(0, S // TQ, qloop, 0)
        dk_ref[...] = (-dk_acc[...]).astype(BF)
        dv_ref[...] = dv_acc[...].astype(BF)

    col = _attn_specs(S, n_pairs)
    whole = lambda a: pl.BlockSpec(a.shape, lambda p: (0, 0))
    out = jax.ShapeDtypeStruct((S, WIDTH), BF)
    state = pltpu.VMEM((n_pairs, 2 * TQ, SB_BLOCK), F32)
    grads = pltpu.VMEM((S, n_pairs * SB_BLOCK), F32)
    return _hosted(
        body, job, (proj, proj, proj, d_o, tot, from_here, up_to), name="attn_bwd",
        grid=(WIDTH // (n_pairs * SB_BLOCK),),
        in_specs=[col(0), col(4), col(8), col(0), pl.BlockSpec((n_pairs, 2, S, 128), lambda p: (p, 0, 0, 0)), whole(from_here), whole(up_to)],
        out_specs=[col(0), col(0), col(0)],
        out_shape=[out, out, out],
        scratch_shapes=[state, state, state, grads, grads],
        semantics=("parallel",))


HG_LEVELS = tuple(HG_CHUNK >> n for n in range(1, HG_CHUNK.bit_length()))


def _hgrn_consts():
    C = HG_CHUNK
    t = np.arange(C)[:, None]
    s = np.arange(C)[None, :]
    rows = [(s <= t), (s > t)]
    masks = [(t == s)]
    for m in HG_LEVELS:
        two = 2 * m
        mid = (t // two) * two + m
        right = (t % two) >= m
        rows.append((right & (s >= mid) & (s <= t)) | ((~right) & (s > t) & (s <= mid - 1)))
        masks.append(((t // two) == (s // two)) & right & ((s % two) < m))
    tri = np.concatenate(rows, axis=0).astype(np.float32)
    return (jnp.asarray(tri, BF), jnp.asarray(tri.T.copy(), BF), jnp.asarray(np.stack(masks).astype(np.float32), F32))


HG_SUM_BLOCKS = 2 + len(HG_LEVELS)
HG_SUM_TERMS = 2


@jax.custom_vjp
def _hgrn_sums(tri, tri_t, g):
    C = HG_CHUNK
    e = _split_dot(tri, g, HG_SUM_TERMS)
    blocks = tuple(e[n * C:(n + 1) * C] for n in range(HG_SUM_BLOCKS))
    return blocks + (jnp.broadcast_to(e[C - 1:C], (HG_DIM, e.shape[1])),)


def _hgrn_sums_fwd(tri, tri_t, g):
    return _hgrn_sums(tri, tri_t, g), (tri, tri_t)


def _hgrn_sums_bwd(res, ds):
    tri, tri_t = res
    C = HG_CHUNK
    last = lax.broadcasted_iota(jnp.int32, (C, 1), 0) == C - 1
    prefix = ds[0] + jnp.where(last, jnp.sum(ds[-1], axis=0, keepdims=True), 0.0)
    d = jnp.concatenate((prefix,) + tuple(ds[1:-1]), axis=0)
    return jnp.zeros_like(tri), jnp.zeros_like(tri_t), _split_dot(tri_t, d, HG_SUM_TERMS)


_hgrn_sums.defvjp(_hgrn_sums_fwd, _hgrn_sums_bwd)


def _bf_dot(a, b):
    return _dot(a.astype(BF), b.astype(BF))


def _bf_dot_nt(a, b):
    return _dot_nt(a.astype(BF), b.astype(BF))


def _bf_dot_tn(a, b):
    return _dot_tn(a.astype(BF), b.astype(BF))


@jax.custom_vjp
def _mm(a, b):
    return _bf_dot(a, b)


_mm.defvjp(lambda a, b: (_bf_dot(a, b), (a, b)), lambda r, ct: (_bf_dot_nt(ct, r[1]), _bf_dot_tn(r[0], ct)))


@jax.custom_vjp
def _mm_nt(a, b):
    return _bf_dot_nt(a, b)


_mm_nt.defvjp(lambda a, b: (_bf_dot_nt(a, b), (a, b)), lambda r, ct: (_bf_dot(ct, r[1]), _bf_dot_tn(ct, r[0])))


@jax.custom_vjp
def _mm_tn(a, b):
    return _bf_dot_tn(a, b)


_mm_tn.defvjp(lambda a, b: (_bf_dot_tn(a, b), (a, b)), lambda r, ct: (_bf_dot_nt(r[1], ct), _bf_dot(r[0], ct)))


def _hgrn_chunk(tri, tri_t, masks, qraw, fpre, v, st, lb):
    q = _silu(qraw)
    f = lb + (1.0 - lb) * _sigmoid(fpre)
    k = 1.0 - f
    e = _hgrn_sums(tri, tri_t, jnp.log(f))
    prefix, suffix, whole = e[0], e[1], e[-1]
    scores = masks[0] * _mm_nt(q, k)
    for n in range(len(HG_LEVELS)):
        decay = jnp.exp(e[2 + n])
        scores = scores + masks[n + 1] * _mm_nt(q * decay, k * decay)
    o = _mm_nt(q * jnp.exp(prefix), st) + _mm(scores, v)
    st_new = st * jnp.exp(whole) + _mm_tn(v, k * jnp.exp(suffix))
    return o, st_new


HG_HEADS_PER_STEP = 4
HG_LANES = HG_HEADS_PER_STEP * HG_DIM


def _hgrn_specs(S, consts):
    col = lambda base: pl.BlockSpec((S, HG_LANES), lambda p, base=base: (0, base // HG_HEADS_PER_STEP + p))
    whole = [pl.BlockSpec(a.shape, lambda p, n=a.ndim: (0,) * n) for a in consts]
    return col, whole


def _hgrn_fwd(proj, lbs, consts, job=None):
    S = proj.shape[0]
    nc = S // HG_CHUNK
    heads = range(HG_HEADS_PER_STEP)

    def body(q_ref, f_ref, i_ref, lb_ref, tri_ref, trit_ref, mask_ref, o_ref, st_ref):
        tri, tri_t = tri_ref[...], trit_ref[...]
        masks = [mask_ref[n] for n in range(len(HG_LEVELS) + 1)]

        def chunk(ci, sts):
            r0 = pl.multiple_of(ci * HG_CHUNK, HG_CHUNK)
            rows = pl.ds(r0, HG_CHUNK)
            new = []
            for hd in heads:
                lanes = pl.ds(hd * HG_DIM, HG_DIM)
                st_ref[hd, ci] = sts[hd]
                o, st_new = _hgrn_chunk(tri, tri_t, masks, q_ref[rows, lanes], f_ref[rows, lanes], i_ref[rows, lanes], sts[hd], lb_ref[hd])
                o_ref[rows, lanes] = o
                new.append(st_new)
            return tuple(new)

        lax.fori_loop(0, nc, chunk, tuple(jnp.zeros((HG_DIM, HG_DIM), F32) for _ in heads))

    col, whole = _hgrn_specs(S, consts)
    return _hosted(
        body, job, (proj, proj, proj, lbs, *consts), name="hgrn_fwd",
        grid=(WIDTH // HG_LANES,),
        in_specs=[col(16), col(20), col(24), pl.BlockSpec((HG_HEADS_PER_STEP, 1, 128), lambda p: (p, 0, 0))] + whole,
        out_specs=[col(0), pl.BlockSpec((HG_HEADS_PER_STEP, nc, HG_DIM, HG_DIM), lambda p: (p, 0, 0, 0))],
        out_shape=[jax.ShapeDtypeStruct((S, WIDTH), F32), jax.ShapeDtypeStruct((WIDTH // 128, nc, HG_DIM, HG_DIM), F32)],
        semantics=("parallel",))


def _hgrn_bwd(proj, lbs, states, d_o, consts, job=None):
    S = proj.shape[0]
    nc = S // HG_CHUNK

    def body(q_ref, f_ref, i_ref, lb_ref, st_ref, do_ref, tri_ref, trit_ref, mask_ref, dq_ref, df_ref, di_ref, dlb_ref):
        masks = [mask_ref[n] for n in range(len(HG_LEVELS) + 1)]
        fn = functools.partial(_hgrn_chunk, tri_ref[...], trit_ref[...], masks)
        heads = range(HG_HEADS_PER_STEP)

        def chunk(n, carry):
            ci = nc - 1 - n
            r0 = pl.multiple_of(ci * HG_CHUNK, HG_CHUNK)
            rows = pl.ds(r0, HG_CHUNK)
            new = []
            for hd in heads:
                d_st, dlb = carry[hd]
                lanes = pl.ds(hd * HG_DIM, HG_DIM)
                _, pull = jax.vjp(fn, q_ref[rows, lanes], f_ref[rows, lanes], i_ref[rows, lanes], st_ref[hd, ci], lb_ref[hd])
                dq, df, di, d_prev, dl = pull((do_ref[rows, lanes], d_st))
                dq_ref[rows, lanes] = dq.astype(BF)
                df_ref[rows, lanes] = df.astype(BF)
                di_ref[rows, lanes] = di.astype(BF)
                new.append((d_prev, dlb + dl))
            return tuple(new)

        zero = (jnp.zeros((HG_DIM, HG_DIM), F32), jnp.zeros((1, HG_DIM), F32))
        done = lax.fori_loop(0, nc, chunk, tuple(zero for _ in heads))
        for hd in heads:
            dlb_ref[hd] = done[hd][1]

    col, whole = _hgrn_specs(S, consts)
    head = pl.BlockSpec((HG_HEADS_PER_STEP, 1, 128), lambda p: (p, 0, 0))
    out = jax.ShapeDtypeStruct((S, WIDTH), BF)
    return _hosted(
        body, job, (proj, proj, proj, lbs, states, d_o, *consts), name="hgrn_bwd",
        grid=(WIDTH // HG_LANES,),
        in_specs=[col(16), col(20), col(24), head, pl.BlockSpec((HG_HEADS_PER_STEP, nc, HG_DIM, HG_DIM), lambda p: (p, 0, 0, 0)), col(0)] + whole,
        out_specs=[col(0), col(0), col(0), head],
        out_shape=[out, out, out, jax.ShapeDtypeStruct((WIDTH // 128, 1, 128), F32)],
        semantics=("parallel",))


def _shift_down(x, n):
    rows = lax.broadcasted_iota(jnp.int32, x.shape, 0)
    return jnp.where(rows >= n, pltpu.roll(x, n, 0), 0.0)


def _shift_up(x, n):
    S = x.shape[0]
    rows = lax.broadcasted_iota(jnp.int32, x.shape, 0)
    return jnp.where(rows < S - n, pltpu.roll(x, S - n, 0), 0.0)


def _branch_fwd(proj, o_a, o_b, norm_w, conv_w, layer):
    S = proj.shape[0]

    def body(oa_ref, za_ref, ob_ref, zb_ref, nw_ref, pre_ref, post_ref, u_ref, zc_ref, cw_ref, ya_ref, yb_ref, yc_ref):
        ya_ref[...] = (oa_ref[...] * _silu(za_ref[...])).astype(BF)
        ob = ob_ref[...]
        rn = lax.rsqrt(jnp.mean(ob * ob, axis=-1, keepdims=True) + RMS_EPS)
        yb_ref[...] = (ob * rn * nw_ref[layer:layer + 1, :] * _silu(zb_ref[...])).astype(BF)
        pu = pre_ref[...] * u_ref[...]
        conv = cw_ref[2:3, :] * pu + cw_ref[1:2, :] * _shift_down(pu, 1) + cw_ref[0:1, :] * _shift_down(pu, 2)
        yc_ref[...] = (post_ref[...] * conv * _silu(zc_ref[...])).astype(BF)

    col = lambda base: pl.BlockSpec((S, 128), lambda p, base=base: (0, base + p))
    out = jax.ShapeDtypeStruct((S, WIDTH), BF)
    return pl.pallas_call(
        body, name="branch_fwd",
        grid=(WIDTH // 128,),
        in_specs=[col(0), col(12), col(0), col(28), pl.BlockSpec(norm_w.shape, lambda p: (0, 0)),
                  col(32), col(36), col(40), col(44), pl.BlockSpec((None, None, 3, 128), lambda p: (p, layer, 0, 0))],
        out_specs=[col(0), col(0), col(0)],
        out_shape=[out, out, out],
        compiler_params=_params(dimension_semantics=("parallel",)),
    )(o_a, proj, o_b, proj, norm_w, proj, proj, proj, proj, conv_w)


def _branch_bwd(proj, o_a, o_b, norm_w, conv_w, dy_a, dy_b, dy_c, layer):
    S = proj.shape[0]

    def dsilu(z):
        s = _sigmoid(z)
        return s * z, s * (1.0 + z * (1.0 - s))

    def body(oa_ref, za_ref, ob_ref, zb_ref, nw_ref, pre_ref, post_ref, u_ref, zc_ref, cw_ref, dya_ref, dyb_ref, dyc_ref,
             doa_ref, dob_ref, dza_ref, dzb_ref, dpre_ref, dpost_ref, du_ref, dzc_ref, dnw_ref, dcw_ref):
        dya = dya_ref[...]
        sa, dsa = dsilu(za_ref[...])
        doa_ref[...] = dya * sa
        dza_ref[...] = (dya * oa_ref[...] * dsa).astype(BF)

        dyb = dyb_ref[...]
        ob = ob_ref[...]
        nw = nw_ref[layer:layer + 1, :]
        sb, dsb = dsilu(zb_ref[...])
        rn = lax.rsqrt(jnp.mean(ob * ob, axis=-1, keepdims=True) + RMS_EPS)
        on = ob * rn
        dzb_ref[...] = (dyb * on * nw * dsb).astype(BF)
        don_w = dyb * sb
        dnw_ref[...] = jnp.sum(don_w * on, axis=0, keepdims=True)
        don = don_w * nw
        dob_ref[...] = rn * (don - on * jnp.mean(don * on, axis=-1, keepdims=True))

        dyc = dyc_ref[...]
        pre, post, u = pre_ref[...], post_ref[...], u_ref[...]
        sc, dsc = dsilu(zc_ref[...])
        pu = pre * u
        pu1, pu2 = _shift_down(pu, 1), _shift_down(pu, 2)
        conv = cw_ref[2:3, :] * pu + cw_ref[1:2, :] * pu1 + cw_ref[0:1, :] * pu2
        dzc_ref[...] = (dyc * post * conv * dsc).astype(BF)
        dpost_ref[...] = (dyc * conv * sc).astype(BF)
        dconv = dyc * post * sc
        dcw_ref[0:1, :] = jnp.sum(dconv * pu2, axis=0, keepdims=True)
        dcw_ref[1:2, :] = jnp.sum(dconv * pu1, axis=0, keepdims=True)
        dcw_ref[2:3, :] = jnp.sum(dconv * pu, axis=0, keepdims=True)
        dpu = cw_ref[2:3, :] * dconv + cw_ref[1:2, :] * _shift_up(dconv, 1) + cw_ref[0:1, :] * _shift_up(dconv, 2)
        dpre_ref[...] = (dpu * u).astype(BF)
        du_ref[...] = (dpu * pre).astype(BF)

    col = lambda base: pl.BlockSpec((S, 128), lambda p, base=base: (0, base + p))
    f32 = jax.ShapeDtypeStruct((S, WIDTH), F32)
    bf = jax.ShapeDtypeStruct((S, WIDTH), BF)
    return pl.pallas_call(
        body, name="branch_bwd",
        grid=(WIDTH // 128,),
        in_specs=[col(0), col(12), col(0), col(28), pl.BlockSpec(norm_w.shape, lambda p: (0, 0)),
                  col(32), col(36), col(40), col(44), pl.BlockSpec((None, None, 3, 128), lambda p: (p, layer, 0, 0)),
                  col(0), col(0), col(0)],
        out_specs=[col(0)] * 8 + [pl.BlockSpec((None, 1, 128), lambda p: (p, 0, 0)), pl.BlockSpec((None, 3, 128), lambda p: (p, 0, 0))],
        out_shape=[f32, f32, bf, bf, bf, bf, bf, bf, jax.ShapeDtypeStruct((WIDTH // 128, 1, 128), F32),
                   jax.ShapeDtypeStruct((WIDTH // 128, 3, 128), F32)],
        compiler_params=_params(dimension_semantics=("parallel",)),
    )(o_a, proj, o_b, proj, norm_w, proj, proj, proj, proj, conv_w, dy_a, dy_b, dy_c)


def _branch_proj(y_refs, wb_ref):
    out = []
    for i in range(3):
        yv = y_refs[i][...]
        out.append(jnp.concatenate([_dot(yv, wb_ref[j, i]) for j in range(N_CHIPS)], axis=1))
    return out


def _merge_fwd(x, mod, proj, ys, wb, wo, ln_g, ln_b, layer, tm=256):
    S, D = x.shape
    tm = min(tm, S)

    def body(x_ref, mod_ref, ga_ref, gb_ref, gc_ref, ya_ref, yb_ref, yc_ref, wb_ref, wo_ref, g_ref, b_ref, xo_ref, mg_ref, y_ref):
        ps = _branch_proj((ya_ref, yb_ref, yc_ref), wb_ref)
        merged = _sigmoid(ga_ref[...]) * ps[0] + _sigmoid(gb_ref[...]) * ps[1] + _sigmoid(gc_ref[...]) * ps[2]
        mb = merged.astype(BF)
        mg_ref[...] = mb
        y = _dot(mb, wo_ref[...].reshape(D, D))
        y_ref[...] = y
        r = ALPHA * x_ref[...] + (1.0 + mod_ref[:, 2 * D:3 * D]) * y
        xn, _ = _standardize(r)
        xo_ref[...] = xn * g_ref[layer:layer + 1, :] + b_ref[layer:layer + 1, :]

    row = lambda w, c=0: pl.BlockSpec((tm, w), lambda i, c=c: (i, c))
    whole = lambda a: pl.BlockSpec(a.shape, lambda i, n=a.ndim: (0,) * n)
    return pl.pallas_call(
        body, name="merge_fwd",
        grid=(S // tm,),
        in_specs=[row(D), whole(mod), row(D, 6), row(D, 7), row(D, 8), row(WIDTH), row(WIDTH), row(WIDTH), whole(wb),
                  whole(wo), whole(ln_g), whole(ln_b)],
        out_specs=[row(D), row(D), row(D)],
        out_shape=[jax.ShapeDtypeStruct((S, D), F32), jax.ShapeDtypeStruct((S, D), BF), jax.ShapeDtypeStruct((S, D), F32)],
        compiler_params=_params(dimension_semantics=("parallel",)),
    )(x, mod, proj, proj, proj, *ys, wb, wo, ln_g, ln_b)


def _merge_bwd(dxo, x, y, mod, proj, ys, wb, wo, ln_g, layer, tm=256):
    S, D = x.shape
    tm = min(tm, S)

    def body(dxo_ref, x_ref, y_ref, mod_ref, ga_ref, gb_ref, gc_ref, ya_ref, yb_ref, yc_ref, wb_ref, wo_ref, g_ref,
             dxr_ref, dy_ref, dp_ref, dg_ref, dya_ref, dyb_ref, dyc_ref, dlg_ref, dlb_ref, dgt_ref):
        @pl.when(pl.program_id(0) == 0)
        def _():
            dlg_ref[...] = jnp.zeros_like(dlg_ref)
            dlb_ref[...] = jnp.zeros_like(dlb_ref)
            dgt_ref[...] = jnp.zeros_like(dgt_ref)

        gate1 = 1.0 + mod_ref[:, 2 * D:3 * D]
        yv = y_ref[...]
        xn, rstd = _standardize(ALPHA * x_ref[...] + gate1 * yv)
        dxo = dxo_ref[...]
        dlg_ref[...] += jnp.sum(dxo * xn, axis=0, keepdims=True)
        dlb_ref[...] += jnp.sum(dxo, axis=0, keepdims=True)
        dr = _standardize_bwd(dxo * g_ref[layer:layer + 1, :], xn, rstd)
        dxr_ref[...] = ALPHA * dr
        dgt_ref[...] += jnp.sum(dr * yv, axis=0, keepdims=True)
        dyb = (gate1 * dr).astype(BF)
        dy_ref[...] = dyb
        dmerged = _dot_nt(dyb, wo_ref[...].reshape(D, D))
        ps = _branch_proj((ya_ref, yb_ref, yc_ref), wb_ref)
        quarter = D // N_CHIPS
        for i, (gate_ref, out_ref) in enumerate(((ga_ref, dya_ref), (gb_ref, dyb_ref), (gc_ref, dyc_ref))):
            sg = _sigmoid(gate_ref[...])
            dg_ref[:, i * D:(i + 1) * D] = (dmerged * ps[i] * sg * (1.0 - sg)).astype(BF)
            dp = (dmerged * sg).astype(BF)
            dp_ref[:, i * D:(i + 1) * D] = dp
            acc = None
            for j in range(N_CHIPS):
                t = _dot_nt(dp[:, j * quarter:(j + 1) * quarter], wb_ref[j, i])
                acc = t if acc is None else acc + t
            out_ref[...] = acc

    row = lambda w, c=0: pl.BlockSpec((tm, w), lambda i, c=c: (i, c))
    whole = lambda a: pl.BlockSpec(a.shape, lambda i, n=a.ndim: (0,) * n)
    vec = pl.BlockSpec((1, D), lambda i: (0, 0))
    sd = jax.ShapeDtypeStruct
    return pl.pallas_call(
        body, name="merge_bwd",
        grid=(S // tm,),
        in_specs=[row(D), row(D), row(D), whole(mod), row(D, 6), row(D, 7), row(D, 8), row(WIDTH), row(WIDTH), row(WIDTH), whole(wb),
                  whole(wo), whole(ln_g)],
        out_specs=[row(D), row(D), row(3 * D), row(3 * D), row(WIDTH), row(WIDTH), row(WIDTH), vec, vec, vec],
        out_shape=[sd((S, D), F32), sd((S, D), BF), sd((S, 3 * D), BF), sd((S, 3 * D), BF), sd((S, WIDTH), F32), sd((S, WIDTH), F32),
                   sd((S, WIDTH), F32), sd((1, D), F32), sd((1, D), F32), sd((1, D), F32)],
        compiler_params=_params(dimension_semantics=("arbitrary",)),
    )(dxo, x, y, mod, proj, proj, proj, *ys, wb, wo, ln_g)


def _loss_head(x, target, tm=512):
    S, D = x.shape
    tm = min(tm, S)

    def body(x_ref, t_ref, dx_ref, loss_ref):
        @pl.when(pl.program_id(0) == 0)
        def _():
            loss_ref[...] = jnp.zeros_like(loss_ref)

        err = x_ref[...] - t_ref[...]
        dx_ref[...] = err * (1.0 / D)
        loss_ref[...] += 0.5 * jnp.sum(jnp.mean(err * err, axis=-1, keepdims=True))

    row = pl.BlockSpec((tm, D), lambda i: (i, 0))
    return pl.pallas_call(
        body, name="loss_head",
        grid=(S // tm,),
        in_specs=[row, row],
        out_specs=[row, pl.BlockSpec((8, 128), lambda i: (0, 0))],
        out_shape=[jax.ShapeDtypeStruct((S, D), F32), jax.ShapeDtypeStruct((8, 128), F32)],
        compiler_params=_params(dimension_semantics=("arbitrary",)),
    )(x, target)


def _proj_bwd(dproj, wgs, x, mod, dx_res, job=None, tm=512, tk=768):
    S, D = x.shape
    tm = min(tm, S)
    n = len(wgs)
    w_args = list(wgs)
    if n == 1:
        per = wgs[0].shape[-1] // tk
        w_specs = [pl.BlockSpec((None, D, tk), lambda i, k: (k // per, 0, k % per))]
    else:
        assert wgs[0].shape[-1] == tk
        w_specs = [pl.BlockSpec((None, D, tk), lambda i, k, c=c: (jnp.minimum((k + n - 1 - c) // n, N_CHIPS - 1), 0, 0))
                   for c in range(n)]
    nk = IN_COLS // tk

    def body(dp_ref, *rest):
        w_refs, (x_ref, mod_ref, dxr_ref, dx_ref, dsh_ref, dsc_ref, acc) = rest[:n], rest[n:]
        i, k = pl.program_id(0), pl.program_id(1)

        @pl.when((i == 0) & (k == 0))
        def _():
            dsh_ref[...] = jnp.zeros_like(dsh_ref)
            dsc_ref[...] = jnp.zeros_like(dsc_ref)

        @pl.when(k == 0)
        def _():
            acc[...] = jnp.zeros_like(acc)

        for c in range(n):
            @pl.when(k % n == c)
            def _(c=c):
                acc[...] += _dot_nt(dp_ref[...], w_refs[c][...])

        @pl.when(k == nk - 1)
        def _():
            dh = acc[...]
            xs, rstd = _standardize(x_ref[...])
            dsh_ref[...] += jnp.sum(dh, axis=0, keepdims=True)
            dsc_ref[...] += jnp.sum(dh * xs, axis=0, keepdims=True)
            dx_ref[...] = _standardize_bwd(dh * (1.0 + mod_ref[:, D:2 * D]), xs, rstd) + dxr_ref[...]

    row = pl.BlockSpec((tm, D), lambda i, k: (i, 0))
    vec = pl.BlockSpec((1, D), lambda i, k: (0, 0))
    return _hosted(
        body, job, (dproj, *w_args, x, mod, dx_res), name="proj_bwd",
        grid=(S // tm, nk),
        in_specs=[pl.BlockSpec((tm, tk), lambda i, k: (i, k))] + w_specs + [row, pl.BlockSpec((1, 3 * D), lambda i, k: (0, 0)), row],
        out_specs=[row, vec, vec],
        out_shape=[jax.ShapeDtypeStruct((S, D), F32), jax.ShapeDtypeStruct((1, D), F32), jax.ShapeDtypeStruct((1, D), F32)],
        scratch_shapes=[pltpu.VMEM((tm, D), F32)],
        semantics=("arbitrary", "arbitrary"))


def _grad_w(a, b, mid, prev, name, tm, tn, b_col0=0):
    S, M = a.shape
    shape = prev.shape
    n_shard = shape[-1]
    per = n_shard // tn
    nm = M // tm
    lead = len(mid)

    def body(*refs):
        a_ref, b_ref, o_ref = refs[0], refs[1], refs[-1]
        o_ref[...] = _dot_tn(a_ref[...], b_ref[...]).astype(BF)

    in_specs = [pl.BlockSpec((S, tm), lambda m, n: (0, m)),
                pl.BlockSpec((S, tn), lambda m, n: (0, b_col0 // tn + n))]
    args = [a, b]
    aliases = {}
    if not isinstance(prev, jax.ShapeDtypeStruct):
        in_specs.append(pl.BlockSpec(memory_space=pl.ANY))
        args.append(prev)
        aliases = {2: 0}
    return pl.pallas_call(
        body, name=name,
        grid=(nm, N_CHIPS * per),
        in_specs=in_specs,
        out_specs=pl.BlockSpec((None,) + (None,) * lead + (tm, tn), lambda m, n: (n // per,) + tuple(mid) + (m, n % per)),
        out_shape=jax.ShapeDtypeStruct(shape, BF),
        input_output_aliases=aliases,
        compiler_params=_params(dimension_semantics=("parallel", "parallel")),
    )(*args)


def _all_gather8(x, name):
    R, N = x.shape

    def body(x_ref, out_ref, send_sems, recv_sems):
        mx, my, mc = lax.axis_index("x"), lax.axis_index("y"), lax.axis_index("c")
        me = 4 * mx + 2 * my + mc
        out_ref[me] = x_ref[...]
        copies = []
        for k in range(1, N_DEV):
            peer = (_flip(mx, k & 4), _flip(my, k & 2), _flip(mc, k & 1))
            cp = pltpu.make_async_remote_copy(src_ref=x_ref, dst_ref=out_ref.at[me], send_sem=send_sems.at[k - 1],
                                              recv_sem=recv_sems.at[k - 1], device_id=peer, device_id_type=MESH)
            cp.start()
            copies.append(cp)
        for cp in copies:
            cp.wait()

    return pl.pallas_call(
        body, name=name,
        in_specs=[pl.BlockSpec(memory_space=pltpu.VMEM)],
        out_specs=pl.BlockSpec(memory_space=pltpu.VMEM),
        out_shape=jax.ShapeDtypeStruct((N_DEV, R, N), F32),
        scratch_shapes=[pltpu.SemaphoreType.DMA((N_DEV - 1,)), pltpu.SemaphoreType.DMA((N_DEV - 1,))],
        compiler_params=_params(),
    )(x)


def _rows2d(a):
    return a.reshape(-1, a.shape[-1])


def _tile_rows(rows, cols, n_arrays):
    budget = (24 << 20) // (n_arrays * 2 * 4 * cols)
    if rows <= budget:
        return rows
    tm = 8
    for cand in range(8, budget + 1, 8):
        if rows % cand == 0:
            tm = cand
    return tm


SUM_ROWS = 256


def _sum_cores(g, sent, where):
    chips, lead, _, r, cols = g.shape
    tr = min(r, SUM_ROWS)

    def body(where_ref, g_ref, s_ref, out_ref):
        out_ref[...] = (g_ref[...].astype(F32) + s_ref[...].astype(F32)).astype(BF)

    spec = pl.BlockSpec((None, tr, cols), lambda i, j, where_ref: (i, j, 0))
    out = pl.pallas_call(
        body, name="sum_cores",
        grid_spec=pltpu.PrefetchScalarGridSpec(
            num_scalar_prefetch=1, grid=(chips * lead, r // tr),
            in_specs=[pl.BlockSpec((None, None, tr, cols), lambda i, j, where_ref: (i, where_ref[1], j, 0)), spec],
            out_specs=spec),
        out_shape=jax.ShapeDtypeStruct((chips * lead, r, cols), BF),
        compiler_params=_params(dimension_semantics=("parallel", "parallel")),
    )(where, g.reshape(chips * lead, 2, r, cols), sent.reshape(chips * lead, r, cols))
    return out.reshape(chips, lead, r, cols)


def _sum_chips(sums, got, where):
    _, lead, r, cols = sums.shape
    tr = min(r, SUM_ROWS)

    def body(where_ref, s_ref, g_ref, out_ref):
        out_ref[...] = ((s_ref[...].astype(F32) + g_ref[0].astype(F32)) + g_ref[1].astype(F32)) + g_ref[2].astype(F32)

    return pl.pallas_call(
        body, name="sum_chips",
        grid_spec=pltpu.PrefetchScalarGridSpec(
            num_scalar_prefetch=1, grid=(lead, r // tr),
            in_specs=[pl.BlockSpec((None, None, tr, cols), lambda i, j, where_ref: (where_ref[0], i, j, 0)),
                      pl.BlockSpec((N_CHIPS - 1, None, tr, cols), lambda i, j, where_ref: (0, i, j, 0))],
            out_specs=pl.BlockSpec((None, None, tr, cols), lambda i, j, where_ref: (i, where_ref[1], j, 0))),
        out_shape=jax.ShapeDtypeStruct((lead, 2, r, cols), F32),
        compiler_params=_params(dimension_semantics=("parallel", "parallel")),
    )(where, sums, got)


def _adamw(w, m, v, groups, name):
    shape = w.shape
    w2, m2, v2 = _rows2d(w), _rows2d(m), _rows2d(v)
    rows, cols = w2.shape
    ng = len(groups)
    n = len(groups[0])
    slab = rows // ng
    gs = [_rows2d(g) for grp in groups for g in grp]
    tm = _tile_rows(slab, cols, 7 + n)
    tiles = slab // tm
    c1 = 1.0 / (1.0 - ADAM_B1 ** ADAM_STEP)
    c2 = 1.0 / (1.0 - ADAM_B2 ** ADAM_STEP)

    def body(*refs):
        w_ref, m_ref, v_ref = refs[:3]
        g_refs = refs[3:3 + ng * n]
        go_ref, d_ref, mo_ref, vo_ref = refs[3 + ng * n:]
        which = pl.program_id(0)
        for s in range(ng):
            @pl.when(which == s)
            def _(s=s):
                g = g_refs[s * n][...]
                for r in g_refs[s * n + 1:(s + 1) * n]:
                    g = g + r[...]
                mn = ADAM_B1 * m_ref[...] + (1.0 - ADAM_B1) * g
                vn = ADAM_B2 * v_ref[...] + (1.0 - ADAM_B2) * (g * g)
                go_ref[...] = g
                mo_ref[...] = mn
                vo_ref[...] = vn
                d_ref[...] = -ADAM_LR * ((mn * c1) / (jnp.sqrt(vn * c2) + ADAM_EPS) + ADAM_WD * w_ref[...])

    spec = pl.BlockSpec((tm, cols), lambda s, i: (s * tiles + i, 0))
    g_specs = [pl.BlockSpec((tm, cols), lambda s, i, k=k: (jnp.where(s == k, i, jnp.where(s < k, 0, tiles - 1)), 0))
               for k in range(ng) for _ in range(n)]
    outs = pl.pallas_call(
        body, name=name,
        grid=(ng, tiles),
        in_specs=[spec] * 3 + g_specs,
        out_specs=[spec] * 4,
        out_shape=[jax.ShapeDtypeStruct((rows, cols), F32)] * 4,
        compiler_params=_params(dimension_semantics=("arbitrary", "arbitrary")),
    )(w2, m2, v2, *gs)
    return [o.reshape(shape) for o in outs]


def _lower_bounds(r0, r1):
    top = jnp.maximum(r0, r1)
    e0, e1 = jnp.exp(r0 - top), jnp.exp(r1 - top)
    p0, p1 = e0 / (e0 + e1), e1 / (e0 + e1)
    return p0 - p0, (p0 + p1) - p0


def _lbs_fwd(lb_raw):
    def body(lb_ref, out_ref):
        l0, l1 = _lower_bounds(lb_ref[0:1, :], lb_ref[1:2, :])
        out_ref[0:1, :] = l0
        out_ref[1:2, :] = l1

    return pl.pallas_call(body, name="lower_bounds", out_shape=jax.ShapeDtypeStruct(lb_raw.shape, F32), compiler_params=_params())(lb_raw)


def _mod_rows(c_all, w_mod, tn=768):
    _, D, cols = w_mod.shape

    def body(c_ref, w_ref, out_ref):
        out_ref[...] = _dot(c_ref[...].astype(BF), w_ref[...].astype(BF))

    return pl.pallas_call(
        body, name="mod_rows",
        grid=(DEPTH,),
        in_specs=[pl.BlockSpec((N_DEV, D), lambda l: (0, 0)), pl.BlockSpec((None, D, cols), lambda l: (l, 0, 0))],
        out_specs=pl.BlockSpec((N_DEV, cols), lambda l: (0, l)),
        out_shape=jax.ShapeDtypeStruct((N_DEV, DEPTH * cols), F32),
        compiler_params=_params(dimension_semantics=("parallel",)),
    )(c_all, w_mod)


def _grad_w_mod(c_all, dmod_cols):
    D = c_all.shape[1]
    cols = dmod_cols.shape[-1]

    def body(c_ref, d_ref, out_ref):
        out_ref[...] = _dot_tn(c_ref[...].astype(BF), d_ref[...].astype(BF))

    return pl.pallas_call(
        body, name="grad_w_mod",
        grid=(DEPTH,),
        in_specs=[pl.BlockSpec((N_DEV, D), lambda l: (0, 0)), pl.BlockSpec((None, N_DEV, cols), lambda l: (l, 0, 0))],
        out_specs=pl.BlockSpec((None, D, cols), lambda l: (l, 0, 0)),
        out_shape=jax.ShapeDtypeStruct((DEPTH, D, cols), F32),
        compiler_params=_params(dimension_semantics=("parallel",)),
    )(c_all, dmod_cols)


def _sum_devices(parts):
    _, R, N = parts.shape

    def body(p_ref, out_ref):
        acc = p_ref[0]
        for d in range(1, N_DEV):
            acc = acc + p_ref[d]
        out_ref[...] = acc

    return pl.pallas_call(body, name="sum_devices", out_shape=jax.ShapeDtypeStruct((R, N), F32), compiler_params=_params())(parts)


def _lbs_bwd(lb_raw, dl):
    def body(lb_ref, dl_ref, out_ref):
        _, pull = jax.vjp(_lower_bounds, lb_ref[0:1, :], lb_ref[1:2, :])
        d0, d1 = pull((dl_ref[0:1, :], dl_ref[1:2, :]))
        out_ref[0:1, :] = d0
        out_ref[1:2, :] = d1

    return pl.pallas_call(body, name="lower_bounds_bwd", out_shape=jax.ShapeDtypeStruct(lb_raw.shape, F32), compiler_params=_params())(lb_raw, dl)


def kernel(x, c, w_mod, b_mod, w_in, conv_w, hgrn_norm_w, lower_bounds, w_branch, w_out, ln_g, ln_b, loss_target, m_w_mod, m_b_mod, m_w_in, m_conv_w, m_hgrn_norm_w, m_lower_bounds, m_w_branch, m_w_out, m_ln_g, m_ln_b, v_w_mod, v_b_mod, v_w_in, v_conv_w, v_hgrn_norm_w, v_lower_bounds, v_w_branch, v_w_out, v_ln_g, v_ln_b):
    D = D_MODEL
    x0 = x[0]
    target = loss_target[0]
    S = x0.shape[0]
    mx, my, mc = lax.axis_index("x"), lax.axis_index("y"), lax.axis_index("c")
    chip = 2 * mx + my
    me = 2 * chip + mc
    mod_cols = 3 * D // N_CHIPS

    plan = _Plan(w_in.astype(BF), w_branch.astype(BF), w_out.astype(BF), chip, mc)
    plan.first()

    n_conv = DEPTH * 3 * (WIDTH // N_CHIPS)
    first = jnp.concatenate([c, conv_w.reshape(1, n_conv), jnp.zeros((1, 2 * D - D - n_conv), F32)], axis=1)
    first = _all_gather8(first.reshape(8, 2 * D // 8), "gather_c").reshape(N_DEV, 2 * D)
    c_all = first[:, :D]
    conv_all = first[:, D:D + n_conv].reshape(N_DEV, DEPTH, 3, WIDTH // N_CHIPS)[0::2]
    mod_part = _all_gather8(_mod_rows(c_all, w_mod), "gather_mod")[0::2]
    mod_part = lax.dynamic_index_in_dim(mod_part, me, axis=1, keepdims=False).reshape(N_CHIPS, DEPTH, mod_cols)
    mods = [(mod_part[:, l].reshape(1, 3 * D) + b_mod[l][None, :]) for l in range(DEPTH)]
    lbs = _lbs_fwd(lower_bounds).reshape(DEPTH, WIDTH // 128, 1, 128)
    loss_blk, dx, small = _local_step(x0, target, mods, lbs, conv_all, hgrn_norm_w, ln_g, ln_b, plan)

    n_mod, n_nw, n_lb, n_ln, n_cw = DEPTH * 3 * D, DEPTH * 128, DEPTH * WIDTH, DEPTH * D, DEPTH * 3 * WIDTH
    row = jnp.concatenate(
        [jnp.concatenate([small[l][0], small[l][1], small[l][2]], axis=1) for l in range(DEPTH)]
        + [jnp.sum(small[l][3], axis=0) for l in range(DEPTH)]
        + [small[l][4].reshape(1, WIDTH) for l in range(DEPTH)]
        + [small[l][5] for l in range(DEPTH)] + [small[l][6] for l in range(DEPTH)]
        + [jnp.transpose(small[l][7], (1, 0, 2)).reshape(1, 3 * WIDTH) for l in range(DEPTH)]
        + [loss_blk[0:1, :]], axis=1)
    n_row = row.shape[1]
    fold = -(-n_row // (8 * 128)) * 128
    rows = jnp.concatenate([row, jnp.zeros((1, 8 * fold - n_row), F32)], axis=1).reshape(8, fold)

    whole, gathered = plan.finish(rows)
    grads = {kind: [[whole[(kind, l)]] for l in range(DEPTH)] for kind in ("in", "br", "out")}

    off_nw = n_mod
    off_lb = off_nw + n_nw
    off_lng = off_lb + n_lb
    off_lnb = off_lng + n_ln
    off_cw = off_lnb + n_ln
    off_loss = off_cw + n_cw
    total = _sum_devices(gathered).reshape(1, 8 * fold)
    gathered = gathered.reshape(N_DEV, 1, 8 * fold)
    d_lower = _lbs_bwd(lower_bounds, total[0, off_lb:off_lng].reshape(DEPTH, WIDTH))
    loss = total[0, off_loss]
    d_b_mod = total[0, :n_mod].reshape(DEPTH, 3 * D)
    d_norm_w = total[0, off_nw:off_lb].reshape(DEPTH, 128)
    d_ln_g = total[0, off_lng:off_lnb].reshape(DEPTH, D)
    d_ln_b = total[0, off_lnb:off_cw].reshape(DEPTH, D)
    d_conv = total[0, off_cw:off_loss].reshape(DEPTH, 3, N_CHIPS, WIDTH // N_CHIPS)
    d_conv = lax.dynamic_index_in_dim(d_conv, chip, axis=2, keepdims=False)
    dmod_all = gathered[:, 0, :n_mod].reshape(N_DEV, DEPTH, N_CHIPS, mod_cols)
    dmod_cols = jnp.transpose(lax.dynamic_index_in_dim(dmod_all, chip, axis=2, keepdims=False), (1, 0, 2))
    d_w_mod = _grad_w_mod(c_all, dmod_cols)

    res = {}
    res["w_mod"] = _adamw(w_mod, m_w_mod, v_w_mod, [[d_w_mod]], "adamw_w_mod")
    res["b_mod"] = _adamw(b_mod, m_b_mod, v_b_mod, [[d_b_mod]], "adamw_b_mod")
    res["w_in"] = _adamw(w_in, m_w_in, v_w_in, grads["in"], "adamw_w_in")
    res["conv_w"] = _adamw(conv_w, m_conv_w, v_conv_w, [[d_conv]], "adamw_conv_w")
    res["hgrn_norm_w"] = _adamw(hgrn_norm_w, m_hgrn_norm_w, v_hgrn_norm_w, [[d_norm_w]], "adamw_norm_w")
    res["lower_bounds"] = _adamw(lower_bounds, m_lower_bounds, v_lower_bounds, [[d_lower]], "adamw_lower_bounds")
    res["w_branch"] = _adamw(w_branch, m_w_branch, v_w_branch, grads["br"], "adamw_w_branch")
    res["w_out"] = _adamw(w_out, m_w_out, v_w_out, grads["out"], "adamw_w_out")
    res["ln_g"] = _adamw(ln_g, m_ln_g, v_ln_g, [[d_ln_g]], "adamw_ln_g")
    res["ln_b"] = _adamw(ln_b, m_ln_b, v_ln_b, [[d_ln_b]], "adamw_ln_b")
    names = ["w_mod", "b_mod", "w_in", "conv_w", "hgrn_norm_w", "lower_bounds", "w_branch", "w_out", "ln_g", "ln_b"]
    return (loss, dx[None], *[res[n][0] for n in names], *[res[n][1] for n in names],
            *[res[n][2] for n in names], *[res[n][3] for n in names])


class _Plan:
    FIRST_CHUNKS = 3
    WINDOWS = ((0, 2 * IN_COLS // N_CHIPS // 3), (2 * IN_COLS // N_CHIPS // 3, IN_COLS // N_CHIPS // 3))

    def __init__(self, w_in, w_br, w_out, chip, core):
        self.local = {"in": w_in, "br": w_br, "out": w_out}
        self.chip, self.where = chip, jnp.stack([chip, core]).astype(jnp.int32)
        self.gathered, self.partial, self.grads, self.chip_sums, self.scattered, self.pending = {}, {}, {}, {}, {}, {}

    def chunks(self, l):
        return self.FIRST_CHUNKS if l == 0 else 1

    def w_in(self, l):
        return [self.gathered[("in", l, c)] for c in range(self.chunks(l))]

    def _shard(self, key):
        mine = self.local[key[0]][key[1]]
        if key[0] == "in":
            cols = mine.shape[-1] // self.chunks(key[1])
            mine = mine[:, key[2] * cols:(key[2] + 1) * cols]
        return mine

    def _slab(self, key):
        mine = _halves(self._shard(key))
        return lax.dynamic_update_slice(lax.empty((N_CHIPS,) + mine.shape, mine.dtype), mine[None], (self.chip, 0, 0, 0, 0))

    def _gather(self, keys):
        return ("gather", keys), _gather_job([self._slab(key) for key in keys])

    def _gather_window(self, key, n):
        slab = self._slab(key) if n == 0 else self.partial[key]
        return ("gather" if n == len(self.WINDOWS) - 1 else "gather_part", [key]), _gather_job([slab], self.WINDOWS[n])

    def _to_sibling(self, keys):
        return ("to_sibling", keys), _to_sibling_job([_halves(self.grads[key], 1) for key in keys])

    def _scatter(self, keys):
        return ("scatter", keys), _scatter_job([self.chip_sums[key] for key in keys])

    def job(self, stage, l, c=0):
        parts = []
        if stage == "proj_fwd":
            parts = [self._gather([("in", l, c + 1)] if c + 1 < self.chunks(l) else [("br", l), ("out", l)])]
        elif stage == "attn_fwd" and l + 1 < DEPTH:
            parts = [self._gather_window(("in", l + 1, 0), 0)]
        elif stage == "hgrn_fwd" and l + 1 < DEPTH:
            parts = [self._gather_window(("in", l + 1, 0), 1)]
        elif stage == "attn_bwd":
            parts = [self._to_sibling([("out", l), ("br", l)])] + ([self._scatter([("in", l + 1)])] if l + 1 < DEPTH else [])
        elif stage == "hgrn_bwd":
            parts = [self._scatter([("out", l), ("br", l)])]
        elif stage == "proj_bwd":
            parts = [self._to_sibling([("in", l)])] if l else [self._scatter([("in", 0)])]
        self.pending[(stage, l, c)] = [(tag, len(job.outs)) for tag, job in parts]
        return _join_jobs([job for _, job in parts])

    def done(self, stage, l, outs, c=0):
        if outs is None:
            return
        at = 0
        for (what, keys), n_outs in self.pending[(stage, l, c)]:
            mine, at = outs[at:at + n_outs], at + n_outs
            for n, key in enumerate(keys):
                if what == "gather":
                    self.gathered[key] = mine[n].reshape((N_CHIPS,) + self._shard(key).shape)
                elif what == "gather_part":
                    self.partial[key] = mine[n]
                elif what == "to_sibling":
                    self.chip_sums[key] = _sum_cores(_halves(self.grads[key], 1), mine[n], self.where)
                else:
                    self.scattered[key] = mine[n]

    def first(self):
        tag, job = self._gather([("in", 0, 0)])
        self.pending[("first", 0, 0)] = [(tag, len(job.outs))]
        self.done("first", 0, _run_job(job, "gather_first"))

    def took(self, key, grad):
        self.grads[key] = grad
        if key == ("in", 0):
            tag, job = self._to_sibling([key])
            self.pending[("took", 0, 0)] = [(tag, len(job.outs))]
            self.done("took", 0, _run_job(job, "to_sibling_last"))

    def finish(self, rows):
        keys = [(kind, l) for kind in ("in", "br", "out") for l in range(DEPTH)]
        halves = [_sum_chips(self.chip_sums[key], self.scattered[key], self.where) for key in keys]
        outs = _run_job(_join_jobs([_place_job(halves), _gather8_job(rows)]), "place_halves")
        return {key: w.reshape(self.grads[key].shape[1:]) for key, w in zip(keys, outs[:-1])}, outs[-1]


def _local_step(x0, target, mods, lbs, conv_all, hgrn_norm_w, ln_g, ln_b, plan):
    D = D_MODEL
    after, before = _attn_consts()
    hg_consts = _hgrn_consts()

    saved = []
    xl = x0
    for l in range(DEPTH):
        n = plan.chunks(l)
        (proj, h), got = _proj_fwd(xl, mods[l], plan.gathered[("in", l, 0)], plan.job("proj_fwd", l, 0), (0, n))
        plan.done("proj_fwd", l, got, 0)
        for c in range(1, n):
            (proj,), got = _proj_cols(h, plan.gathered[("in", l, c)], (c, n), proj, plan.job("proj_fwd", l, c))
            plan.done("proj_fwd", l, got, c)
        (o_a, tot), got = _attn_fwd(proj, after, plan.job("attn_fwd", l))
        plan.done("attn_fwd", l, got)
        (o_b, states), got = _hgrn_fwd(proj, lbs[l], hg_consts, plan.job("hgrn_fwd", l))
        plan.done("hgrn_fwd", l, got)
        ys = _branch_fwd(proj, o_a, o_b, hgrn_norm_w, conv_all, l)
        x_next, merged, y = _merge_fwd(xl, mods[l], proj, ys, plan.gathered[("br", l)], plan.gathered[("out", l)], ln_g, ln_b, l)
        saved.append((xl, proj, h, o_a, tot, o_b, states, ys, merged, y))
        xl = x_next
    dx, loss_blk = _loss_head(xl, target)

    small = [None] * DEPTH
    for l in reversed(range(DEPTH)):
        xin, proj, h, o_a, tot, o_b, states, ys, merged, y = saved[l]
        dx_res, dy, dp, dg, dy_a, dy_b, dy_c, dln_g, dln_b, dgate = _merge_bwd(
            dx, xin, y, mods[l], proj, ys, plan.gathered[("br", l)], plan.gathered[("out", l)], ln_g, l)
        plan.took(("out", l), _grad_w_out(merged, dy))
        g_br = jax.ShapeDtypeStruct((N_CHIPS, 3, WIDTH, D // N_CHIPS), BF)
        for i in range(3):
            g_br = _grad_w(ys[i], dp, (i,), g_br, "grad_w_branch", tm=WIDTH, tn=D // N_CHIPS, b_col0=i * D)
        plan.took(("br", l), g_br)
        d_oa, d_ob, dz_a, dz_b, dpre, dpost, du, dz_c, dnorm_w, dconv_w = _branch_bwd(proj, o_a, o_b, hgrn_norm_w, conv_all, dy_a, dy_b, dy_c, l)
        (dq_a, dk_a, dv_a), got = _attn_bwd(proj, d_oa, tot, after, before, plan.job("attn_bwd", l))
        plan.done("attn_bwd", l, got)
        (dq_b, df_b, di_b, dlb), got = _hgrn_bwd(proj, lbs[l], states, d_ob, hg_consts, plan.job("hgrn_bwd", l))
        plan.done("hgrn_bwd", l, got)
        dproj = jnp.concatenate([dq_a, dk_a, dv_a, dz_a, dq_b, df_b, di_b, dz_b, dpre, dpost, du, dz_c, dg], axis=1)
        plan.took(("in", l), _grad_w(h, dproj, (), jax.ShapeDtypeStruct((N_CHIPS, D, IN_COLS // N_CHIPS), BF), "grad_w_in", tm=512, tn=2304))
        (dx, dshift, dscale), got = _proj_bwd(dproj, plan.w_in(l), xin, mods[l], dx_res, plan.job("proj_bwd", l))
        plan.done("proj_bwd", l, got)
        small[l] = (dshift, dscale, dgate, dnorm_w, dlb, dln_g, dln_b, dconv_w)
    return loss_blk, dx, small


def _grad_w_out(merged, dy):
    S, D = merged.shape
    q = D // N_CHIPS

    def body(a_ref, b_ref, o_ref):
        o_ref[...] = _dot_tn(a_ref[...], b_ref[...]).astype(BF)

    return pl.pallas_call(
        body, name="grad_w_out",
        grid=(N_CHIPS,),
        in_specs=[pl.BlockSpec((S, q), lambda j: (0, j)), pl.BlockSpec((S, D), lambda j: (0, 0))],
        out_specs=pl.BlockSpec((None, q, D), lambda j: (j, 0, 0)),
        out_shape=jax.ShapeDtypeStruct((N_CHIPS, q, D), BF),
        compiler_params=_params(dimension_semantics=("parallel",)),
    )(merged, dy)
```

```python
import functools
import math

import numpy as np
import jax
import jax.numpy as jnp
from jax import lax
from jax.experimental import pallas as pl
from jax.experimental.pallas import tpu as pltpu

F32 = jnp.float32
BF = jnp.bfloat16
MESH = pl.DeviceIdType.MESH

DEPTH = 2
D_MODEL = 1024
WIDTH = 512
IN_COLS = 12 * WIDTH + 3 * D_MODEL
N_CHIPS = 4
N_DEV = 8
SB_BLOCK = 128
SB_HEAD_DIM = 64
HG_CHUNK = 128
HG_DIM = 128
LN_EPS = 1e-5
RMS_EPS = 1e-6
ALPHA = (2.0 * DEPTH) ** 0.25
ADAM_LR, ADAM_B1, ADAM_B2, ADAM_EPS, ADAM_WD, ADAM_STEP = 0.001, 0.9, 0.999, 1e-08, 0.01, 10
VMEM_LIMIT = 56 << 20


def _params(**kw):
    return pltpu.CompilerParams(vmem_limit_bytes=VMEM_LIMIT, **kw)


def _dot(a, b):
    return jnp.dot(a, b, preferred_element_type=F32)


def _dot_nt(a, b):
    return lax.dot_general(a, b, (((1,), (1,)), ((), ())), preferred_element_type=F32)


def _dot_tn(a, b):
    return lax.dot_general(a, b, (((0,), (0,)), ((), ())), preferred_element_type=F32)


def _sigmoid(x):
    return 1.0 / (1.0 + jnp.exp(-x))


def _silu(x):
    return x * _sigmoid(x)


def _split_dot(t, g, terms):
    acc = None
    rest = g
    for _ in range(terms):
        part = rest.astype(BF)
        rest = rest - part.astype(F32)
        d = _dot(t, part)
        acc = d if acc is None else acc + d
    return acc


def _standardize(x):
    mu = jnp.mean(x, axis=-1, keepdims=True)
    xc = x - mu
    var = jnp.mean(xc * xc, axis=-1, keepdims=True)
    rstd = lax.rsqrt(var + LN_EPS)
    return xc * rstd, rstd


def _standardize_bwd(dxs, xs, rstd):
    return rstd * (dxs - jnp.mean(dxs, axis=-1, keepdims=True) - xs * jnp.mean(dxs * xs, axis=-1, keepdims=True))


class _Job:
    def __init__(self, ins, outs, sems, make, alias=None):
        self.ins, self.outs, self.sems, self.make = list(ins), list(outs), list(sems), make
        self.alias = dict(alias or {})


def _join_jobs(jobs):
    jobs = [j for j in jobs if j is not None]
    if len(jobs) <= 1:
        return jobs[0] if jobs else None

    def make(ins, outs, sems):
        phases, i, o, s = [], 0, 0, 0
        for j in jobs:
            got = j.make(ins[i:i + len(j.ins)], outs[o:o + len(j.outs)], sems[s:s + len(j.sems)])
            i, o, s = i + len(j.ins), o + len(j.outs), s + len(j.sems)
            for n, phase in enumerate(got):
                if n == len(phases):
                    phases.append([])
                phases[n] += phase
        return phases

    alias, i, o = {}, 0, 0
    for j in jobs:
        alias.update({i + a: o + b for a, b in j.alias.items()})
        i, o = i + len(j.ins), o + len(j.outs)
    return _Job(sum([j.ins for j in jobs], []), sum([j.outs for j in jobs], []), sum([j.sems for j in jobs], []), make, alias)


def _flip(v, bit):
    return 1 - v if bit else v


def _halves(a, front=0):
    shape = a.shape
    lead = math.prod(shape[front:-2])
    return a.reshape(shape[:front] + (lead, 2, shape[-2] // 2, shape[-1]))


def _dma_sems(*shapes):
    return [pltpu.SemaphoreType.DMA(s) for s in shapes]


def _same(arrays):
    return [jax.ShapeDtypeStruct(a.shape, a.dtype) for a in arrays]


def _gather_job(slabs, window=None):
    n = len(slabs)
    cols = slice(None) if window is None else pl.ds(*window)

    def make(ins, outs, sems):
        send1, recv1, send2, recv2 = sems
        mx, my, mc = lax.axis_index("x"), lax.axis_index("y"), lax.axis_index("c")
        fetch, pass_on = [], []
        for a in range(n):
            ours = outs[a].at[2 * mx + my, :, mc, :, cols]
            for k in range(1, N_CHIPS):
                px, py = _flip(mx, k & 2), _flip(my, k & 1)
                fetch.append(pltpu.make_async_remote_copy(
                    src_ref=ours, dst_ref=ours, send_sem=send1.at[a, k - 1], recv_sem=recv1.at[a, k - 1],
                    device_id=(px, py, mc), device_id_type=MESH))
                theirs = outs[a].at[2 * px + py, :, mc, :, cols]
                pass_on.append(pltpu.make_async_remote_copy(
                    src_ref=theirs, dst_ref=theirs, send_sem=send2.at[a, k - 1], recv_sem=recv2.at[a, k - 1],
                    device_id=(mx, my, 1 - mc), device_id_type=MESH))
        return [fetch, pass_on]

    pairs = (n, N_CHIPS - 1)
    return _Job(slabs, _same(slabs), _dma_sems(pairs, pairs, pairs, pairs), make, {a: a for a in range(n)})


def _to_sibling_job(grads):
    n = len(grads)

    def make(ins, outs, sems):
        send_sems, recv_sems = sems
        mx, my, mc = lax.axis_index("x"), lax.axis_index("y"), lax.axis_index("c")
        return [[pltpu.make_async_remote_copy(
            src_ref=ins[a].at[:, :, 1 - mc], dst_ref=outs[a], send_sem=send_sems.at[a], recv_sem=recv_sems.at[a],
            device_id=(mx, my, 1 - mc), device_id_type=MESH) for a in range(n)]]

    outs = [jax.ShapeDtypeStruct(g.shape[:2] + g.shape[3:], g.dtype) for g in grads]
    return _Job(grads, outs, _dma_sems((n,), (n,)), make)


def _scatter_job(sums):
    n = len(sums)

    def make(ins, outs, sems):
        send_sems, recv_sems = sems
        mx, my, mc = lax.axis_index("x"), lax.axis_index("y"), lax.axis_index("c")
        copies = []
        for a in range(n):
            for k in range(1, N_CHIPS):
                px, py = _flip(mx, k & 2), _flip(my, k & 1)
                copies.append(pltpu.make_async_remote_copy(
                    src_ref=ins[a].at[2 * px + py], dst_ref=outs[a].at[k - 1], send_sem=send_sems.at[a, k - 1],
                    recv_sem=recv_sems.at[a, k - 1], device_id=(px, py, mc), device_id_type=MESH))
        return [copies]

    pairs = (n, N_CHIPS - 1)
    return _Job(sums, [jax.ShapeDtypeStruct((N_CHIPS - 1,) + s.shape[1:], s.dtype) for s in sums], _dma_sems(pairs, pairs), make)


def _place_job(wholes):
    n = len(wholes)

    def make(ins, outs, sems):
        send_sems, recv_sems = sems
        mx, my, mc = lax.axis_index("x"), lax.axis_index("y"), lax.axis_index("c")
        copies = []
        for a in range(n):
            here = outs[a].at[:, mc]
            copies.append(pltpu.make_async_remote_copy(src_ref=here, dst_ref=here, send_sem=send_sems.at[a], recv_sem=recv_sems.at[a],
                                                       device_id=(mx, my, 1 - mc), device_id_type=MESH))
        return [copies]

    return _Job(wholes, _same(wholes), _dma_sems((n,), (n,)), make, {a: a for a in range(n)})


def _gather8_job(x):
    def make(ins, outs, sems):
        local_sem, send_sems, recv_sems = sems
        mx, my, mc = lax.axis_index("x"), lax.axis_index("y"), lax.axis_index("c")
        here = outs[0].at[4 * mx + 2 * my + mc]
        copies = [pltpu.make_async_copy(ins[0], here, local_sem.at[0])]
        for k in range(1, N_DEV):
            peer = (_flip(mx, k & 4), _flip(my, k & 2), _flip(mc, k & 1))
            copies.append(pltpu.make_async_remote_copy(src_ref=ins[0], dst_ref=here, send_sem=send_sems.at[k - 1],
                                                       recv_sem=recv_sems.at[k - 1], device_id=peer, device_id_type=MESH))
        return [copies]

    return _Job([x], [jax.ShapeDtypeStruct((N_DEV,) + x.shape, x.dtype)], _dma_sems((1,), (N_DEV - 1,), (N_DEV - 1,)), make)


def _run_phases(phases, first=0):
    for n, phase in enumerate(phases):
        if n >= first:
            for cp in phase:
                cp.start()
        for cp in phase:
            cp.wait()


def _run_job(job, name):
    k_in, k_out = len(job.ins), len(job.outs)

    def body(*refs):
        _run_phases(job.make(refs[:k_in], refs[k_in:k_in + k_out], refs[k_in + k_out:]))

    hbm = pl.BlockSpec(memory_space=pl.ANY)
    return pl.pallas_call(body, name=name, in_specs=[hbm] * k_in, out_specs=[hbm] * k_out, out_shape=job.outs,
                          scratch_shapes=job.sems, input_output_aliases=job.alias, compiler_params=_params())(*job.ins)


def _hosted(body, job, args, *, name, grid, in_specs, out_specs, out_shape, scratch_shapes=(), semantics, aliases=None):
    in_specs, out_specs, out_shape, scratch = list(in_specs), list(out_specs), list(out_shape), list(scratch_shapes)
    aliases = dict(aliases or {})
    if job is None:
        outs = pl.pallas_call(body, name=name, grid=grid, in_specs=in_specs, out_specs=out_specs, out_shape=out_shape,
                              scratch_shapes=scratch, input_output_aliases=aliases,
                              compiler_params=_params(dimension_semantics=semantics))(*args)
        return list(outs), None
    n_in, n_out, n_scr, k_in, k_out = len(in_specs), len(out_specs), len(scratch), len(job.ins), len(job.outs)

    def wrapped(*refs):
        ins, rest = refs[:n_in], refs[n_in:]
        job_ins, rest = rest[:k_in], rest[k_in:]
        outs, rest = rest[:n_out], rest[n_out:]
        job_outs, rest = rest[:k_out], rest[k_out:]
        scr, sems = rest[:n_scr], rest[n_scr:]
        ids = [pl.program_id(a) for a in range(len(grid))]
        first = functools.reduce(jnp.logical_and, [i == 0 for i in ids])
        last = functools.reduce(jnp.logical_and, [i == g - 1 for i, g in zip(ids, grid)])

        @pl.when(first)
        def _():
            for cp in job.make(job_ins, job_outs, sems)[0]:
                cp.start()

        body(*ins, *outs, *scr)

        @pl.when(last)
        def _():
            _run_phases(job.make(job_ins, job_outs, sems), first=1)

    hbm = pl.BlockSpec(memory_space=pl.ANY)
    outs = pl.pallas_call(
        wrapped, name=name, grid=grid, in_specs=in_specs + [hbm] * k_in, out_specs=out_specs + [hbm] * k_out,
        out_shape=out_shape + job.outs, scratch_shapes=scratch + job.sems,
        input_output_aliases={**aliases, **{n_in + i: n_out + o for i, o in job.alias.items()}},
        compiler_params=_params(dimension_semantics=("arbitrary",) * len(grid)))(*args, *job.ins)
    return list(outs[:n_out]), list(outs[n_out:])


def _proj_fwd(x, mod, wg, job=None, chunk=(0, 1), tm=512):
    S, D = x.shape
    tm = min(tm, S)
    tn = wg.shape[-1]
    c, n = chunk

    def body(x_ref, mod_ref, w_ref, proj_ref, h_ref, hs):
        @pl.when(pl.program_id(1) == 0)
        def _():
            xs, _ = _standardize(x_ref[...])
            h = xs * (1.0 + mod_ref[:, D:2 * D]) + mod_ref[:, 0:D]
            hb = h.astype(BF)
            hs[...] = hb
            h_ref[...] = hb

        proj_ref[...] = _dot(hs[...], w_ref[...])

    return _hosted(
        body, job, (x, mod, wg), name="proj_fwd",
        grid=(S // tm, N_CHIPS),
        in_specs=[pl.BlockSpec((tm, D), lambda i, j: (i, 0)),
                  pl.BlockSpec((1, 3 * D), lambda i, j: (0, 0)),
                  pl.BlockSpec((None, D, tn), lambda i, j: (j, 0, 0))],
        out_specs=[pl.BlockSpec((tm, tn), lambda i, j: (i, j * n + c)),
                   pl.BlockSpec((tm, D), lambda i, j: (i, 0))],
        out_shape=[jax.ShapeDtypeStruct((S, IN_COLS), F32), jax.ShapeDtypeStruct((S, D), BF)],
        scratch_shapes=[pltpu.VMEM((tm, D), BF)],
        semantics=("parallel", "arbitrary"))


def _proj_cols(h, wg, chunk, proj, job=None, tm=512):
    S, D = h.shape
    tm = min(tm, S)
    tn = wg.shape[-1]
    c, n = chunk

    def body(h_ref, w_ref, prev_ref, proj_ref):
        proj_ref[...] = _dot(h_ref[...], w_ref[...])

    in_specs = [pl.BlockSpec((tm, D), lambda i, j: (i, 0)), pl.BlockSpec((None, D, tn), lambda i, j: (j, 0, 0)),
                pl.BlockSpec(memory_space=pl.ANY)]
    return _hosted(body, job, (h, wg, proj), name="proj_cols", grid=(S // tm, N_CHIPS), in_specs=in_specs,
                   out_specs=[pl.BlockSpec((tm, tn), lambda i, j: (i, j * n + c))],
                   out_shape=[jax.ShapeDtypeStruct(proj.shape, proj.dtype)],
                   semantics=("parallel", "arbitrary"), aliases={2: 0})


SB_ROWS = 256
SB_KEYS = 256


def _attn_consts():
    j = np.arange(SB_KEYS)[:, None]
    s = np.arange(SB_KEYS)[None, :]
    from_here = np.concatenate([(j >= s), (j >= s)], axis=0).astype(np.float32)
    up_to = np.concatenate([(j <= s), (j <= s)], axis=0).astype(np.float32)
    return jnp.asarray(from_here, BF), jnp.asarray(up_to, BF)


def _hi_lo(x):
    hi = lax.bitcast_convert_type(lax.bitcast_convert_type(x, jnp.uint32) & jnp.uint32(0xFFFF0000), F32)
    return hi.astype(BF), (x - hi).astype(BF)


def _sums_r(x, t2):
    hi, lo = _hi_lo(x)
    return _dot(jnp.concatenate([hi, lo], axis=1), t2)


def _all_lanes(col, lanes):
    return jnp.broadcast_to(col, (col.shape[0], lanes))


def _attn_rows(ref, r0, rows, lanes, head0, scale=None):
    v = ref[pl.ds(r0, rows), lanes]
    if scale is not None:
        v = v * scale
    return jnp.concatenate([jnp.where(head0, v, 0.0), jnp.where(head0, 0.0, v)], axis=0).astype(BF)


SB_PAIRS_FWD = 4
SB_PAIRS_BWD = 2


def _attn_specs(S, n_pairs):
    return lambda base: pl.BlockSpec((S, n_pairs * SB_BLOCK), lambda p, base=base: (0, base // n_pairs + p))


def _attn_scores(q2n, k_ref, lanes, kj, t2, from_here_ref, masked):
    c0 = pl.multiple_of(kj * SB_KEYS, SB_KEYS)
    kb = k_ref[pl.ds(c0, SB_KEYS), lanes].astype(BF)
    zn = _dot_nt(q2n, kb)
    lsb = jnp.minimum(zn, 0.0) - jnp.log(1.0 + jnp.exp(-jnp.abs(zn)))
    valid = None
    if masked:
        valid = (lax.broadcasted_iota(jnp.int32, zn.shape, 1) + kj * SB_KEYS) < t2
        lsb = jnp.where(valid, lsb, 0.0)
    return c0, kb, zn, valid, lsb, _sums_r(lsb, from_here_ref[...])


def _attn_fwd(proj, from_here, job=None):
    S = proj.shape[0]
    TQ = SB_ROWS
    assert S % TQ == 0 and SB_KEYS == TQ
    scale = SB_HEAD_DIM ** -0.5
    n_pairs = SB_PAIRS_FWD
    pairs = range(n_pairs)
    lanes = [pl.ds(p * SB_BLOCK, SB_BLOCK) for p in pairs]

    def body(q_ref, k_ref, v_ref, from_here_ref, o_ref, tot_ref, run, acc):
        head0 = lax.broadcasted_iota(jnp.int32, (1, 2 * SB_HEAD_DIM), 1) < SB_HEAD_DIM

        def qloop(qi, _):
            r0 = pl.multiple_of(qi * TQ, TQ)
            q2n = [_attn_rows(q_ref, r0, TQ, lanes[p], head0, -scale) for p in pairs]
            trow = lax.broadcasted_iota(jnp.int32, (TQ, SB_KEYS), 0) + qi * TQ
            t2 = jnp.concatenate([trow, trow], axis=0)
            run[...] = jnp.zeros_like(run)
            acc[...] = jnp.zeros_like(acc)

            def step(kj, masked):
                got = [_attn_scores(q2n[p], k_ref, lanes[p], kj, t2, from_here_ref, masked) for p in pairs]
                for p in pairs:
                    c0, _, zn, valid, _, sums = got[p]
                    r = run[p]
                    e = sums - zn + jnp.concatenate([r, r], axis=1)
                    if masked:
                        e = jnp.where(valid, e, -jnp.inf)
                    acc[p] += _dot(jnp.exp(e).astype(BF), v_ref[pl.ds(c0, SB_KEYS), lanes[p]].astype(BF))
                    run[p] = r + _all_lanes(sums[:, 0:1], SB_BLOCK)

            step(qi, True)

            def below(n, _):
                step(qi - 1 - n, False)
                return 0

            lax.fori_loop(0, qi, below, 0)
            for p in pairs:
                o_ref[pl.ds(r0, TQ), lanes[p]] = jnp.where(head0, acc[p, 0:TQ, :], acc[p, TQ:2 * TQ, :])
                tot_ref[p, 0, pl.ds(r0, TQ), :] = run[p, 0:TQ, :]
                tot_ref[p, 1, pl.ds(r0, TQ), :] = run[p, TQ:2 * TQ, :]
            return 0

        lax.fori_loop(0, S // TQ, qloop, 0)

    col = _attn_specs(S, n_pairs)
    state = pltpu.VMEM((n_pairs, 2 * TQ, SB_BLOCK), F32)
    return _hosted(
        body, job, (proj, proj, proj, from_here), name="attn_fwd",
        grid=(WIDTH // (n_pairs * SB_BLOCK),),
        in_specs=[col(0), col(4), col(8), pl.BlockSpec(from_here.shape, lambda p: (0, 0))],
        out_specs=[col(0), pl.BlockSpec((n_pairs, 2, S, 128), lambda p: (p, 0, 0, 0))],
        out_shape=[jax.ShapeDtypeStruct((S, WIDTH), F32), jax.ShapeDtypeStruct((WIDTH // 128, 2, S, 128), F32)],
        scratch_shapes=[state, state],
        semantics=("parallel",))


def _attn_bwd(proj, d_o, tot, from_here, up_to, dproj, job=None):
    S = proj.shape[0]
    TQ = SB_ROWS
    assert S % TQ == 0 and SB_KEYS == TQ
    scale = SB_HEAD_DIM ** -0.5
    n_pairs = SB_PAIRS_BWD
    pairs = range(n_pairs)
    lanes = [pl.ds(p * SB_BLOCK, SB_BLOCK) for p in pairs]

    def body(q_ref, k_ref, v_ref, do_ref, tot_ref, from_here_ref, up_to_ref, _, dproj_ref, pre, cum, dq_acc, dk_acc, dv_acc, stage, sems):
        dq_ref, dk_ref, dv_ref = stage.at[0], stage.at[1], stage.at[2]
        head0 = lax.broadcasted_iota(jnp.int32, (1, 2 * SB_HEAD_DIM), 1) < SB_HEAD_DIM
        dk_acc[...] = jnp.zeros_like(dk_acc)
        dv_acc[...] = jnp.zeros_like(dv_acc)

        def qloop(qi, _):
            r0 = pl.multiple_of(qi * TQ, TQ)
            q2n = [_attn_rows(q_ref, r0, TQ, lanes[p], head0, -scale) for p in pairs]
            do2 = [_attn_rows(do_ref, r0, TQ, lanes[p], head0) for p in pairs]
            trow = lax.broadcasted_iota(jnp.int32, (TQ, SB_KEYS), 0) + qi * TQ
            t2 = jnp.concatenate([trow, trow], axis=0)
            for p in pairs:
                pre[p, 0:TQ, :] = tot_ref[p, 0, pl.ds(r0, TQ), :]
                pre[p, TQ:2 * TQ, :] = tot_ref[p, 1, pl.ds(r0, TQ), :]
            cum[...] = jnp.zeros_like(cum)
            dq_acc[...] = jnp.zeros_like(dq_acc)

            def step(kj, masked):
                got = [_attn_scores(q2n[p], k_ref, lanes[p], kj, t2, from_here_ref, masked) for p in pairs]
                for p in pairs:
                    c0, kb, zn, valid, lsb, sums = got[p]
                    later = pre[p] - _all_lanes(sums[:, 0:1], SB_BLOCK)
                    pre[p] = later
                    e = sums - zn + jnp.concatenate([later, later], axis=1)
                    sig = jnp.exp(lsb - zn)
                    if masked:
                        e = jnp.where(valid, e, -jnp.inf)
                        sig = jnp.where(valid, sig, 0.0)
                    a = jnp.exp(e)
                    w = _dot_nt(do2[p], v_ref[pl.ds(c0, SB_KEYS), lanes[p]].astype(BF)) * a
                    upto = _sums_r(w, up_to_ref[...])
                    c = cum[p]
                    dz = w - sig * (upto + jnp.concatenate([c, c], axis=1))
                    cum[p] = c + _all_lanes(upto[:, SB_KEYS - 1:SB_KEYS], SB_BLOCK)
                    dzb = dz.astype(BF)
                    dq_acc[p] += _dot(dzb, kb)
                    dk_acc[pl.ds(c0, SB_KEYS), lanes[p]] += _dot_tn(dzb, q2n[p])
                    dv_acc[pl.ds(c0, SB_KEYS), lanes[p]] += _dot_tn(a.astype(BF), do2[p])

            def below(kj, _):
                step(kj, False)
                return 0

            lax.fori_loop(0, qi, below, 0)
            step(qi, True)
            for p in pairs:
                dq_ref[pl.ds(r0, TQ), lanes[p]] = (jnp.where(head0, dq_acc[p, 0:TQ, :], dq_acc[p, TQ:2 * TQ, :]) * scale).astype(BF)
            return 0

        lax.fori_loop(0, S // TQ, qloop, 0)
        dk_ref[...] = (-dk_acc[...]).astype(BF)
        dv_ref[...] = dv_acc[...].astype(BF)
        lane0 = pl.program_id(0) * (n_pairs * SB_BLOCK)
        _put_columns([dq_ref, dk_ref, dv_ref], dproj_ref, [WIDTH * n + lane0 for n in range(3)], sems)

    col = _attn_specs(S, n_pairs)
    whole = lambda a: pl.BlockSpec(a.shape, lambda p: (0, 0))
    hbm = pl.BlockSpec(memory_space=pl.ANY)
    state = pltpu.VMEM((n_pairs, 2 * TQ, SB_BLOCK), F32)
    grads = pltpu.VMEM((S, n_pairs * SB_BLOCK), F32)
    return _hosted(
        body, job, (proj, proj, proj, d_o, tot, from_here, up_to, dproj), name="attn_bwd",
        grid=(WIDTH // (n_pairs * SB_BLOCK),),
        in_specs=[col(0), col(4), col(8), col(0), pl.BlockSpec((n_pairs, 2, S, 128), lambda p: (p, 0, 0, 0)), whole(from_here),
                  whole(up_to), hbm],
        out_specs=[hbm],
        out_shape=[jax.ShapeDtypeStruct(dproj.shape, dproj.dtype)],
        scratch_shapes=[state, state, state, grads, grads, pltpu.VMEM((3, S, n_pairs * SB_BLOCK), BF), pltpu.SemaphoreType.DMA((3,))],
        semantics=("arbitrary",), aliases={7: 0})


HG_LEVELS = tuple(HG_CHUNK >> n for n in range(1, HG_CHUNK.bit_length()))


def _hgrn_consts():
    C = HG_CHUNK
    t = np.arange(C)[:, None]
    s = np.arange(C)[None, :]
    rows = [(s <= t), (s > t)]
    masks = [(t == s)]
    for m in HG_LEVELS:
        two = 2 * m
        mid = (t // two) * two + m
        right = (t % two) >= m
        rows.append((right & (s >= mid) & (s <= t)) | ((~right) & (s > t) & (s <= mid - 1)))
        masks.append(((t // two) == (s // two)) & right & ((s % two) < m))
    tri = np.concatenate(rows, axis=0).astype(np.float32)
    twice = lambda a: jnp.asarray(np.concatenate([a, a], axis=1), BF)
    return (twice(tri), twice(tri.T), jnp.asarray(np.stack(masks).astype(np.float32), F32))


HG_SUM_BLOCKS = 2 + len(HG_LEVELS)


def _split_rows(g):
    hi = g.astype(BF)
    return jnp.concatenate([hi, (g - hi.astype(F32)).astype(BF)], axis=0)


def _hgrn_sum_blocks(e):
    C = HG_CHUNK
    blocks = tuple(e[n * C:(n + 1) * C] for n in range(HG_SUM_BLOCKS))
    return blocks + (jnp.broadcast_to(e[C - 1:C], (HG_DIM, e.shape[1])),)


@jax.custom_vjp
def _hgrn_sums(tri, tri_t, g):
    return _hgrn_sum_blocks(_split_dot(tri[:, :HG_CHUNK], g, 2))


def _hgrn_sums_fwd(tri, tri_t, g):
    return _hgrn_sum_blocks(_dot(tri, _split_rows(g))), (tri, tri_t)


def _hgrn_sums_bwd(res, ds):
    tri, tri_t = res
    C = HG_CHUNK
    last = lax.broadcasted_iota(jnp.int32, (C, 1), 0) == C - 1
    prefix = ds[0] + jnp.where(last, jnp.sum(ds[-1], axis=0, keepdims=True), 0.0)
    d = jnp.concatenate((prefix,) + tuple(ds[1:-1]), axis=0)
    return jnp.zeros_like(tri), jnp.zeros_like(tri_t), _dot(tri_t, _split_rows(d))


_hgrn_sums.defvjp(_hgrn_sums_fwd, _hgrn_sums_bwd)


def _bf_dot(a, b):
    return _dot(a.astype(BF), b.astype(BF))


def _bf_dot_nt(a, b):
    return _dot_nt(a.astype(BF), b.astype(BF))


def _bf_dot_tn(a, b):
    return _dot_tn(a.astype(BF), b.astype(BF))


@jax.custom_vjp
def _mm(a, b):
    return _bf_dot(a, b)


_mm.defvjp(lambda a, b: (_bf_dot(a, b), (a, b)), lambda r, ct: (_bf_dot_nt(ct, r[1]), _bf_dot_tn(r[0], ct)))


@jax.custom_vjp
def _mm_nt(a, b):
    return _bf_dot_nt(a, b)


_mm_nt.defvjp(lambda a, b: (_bf_dot_nt(a, b), (a, b)), lambda r, ct: (_bf_dot(ct, r[1]), _bf_dot_tn(ct, r[0])))


@jax.custom_vjp
def _mm_tn(a, b):
    return _bf_dot_tn(a, b)


_mm_tn.defvjp(lambda a, b: (_bf_dot_tn(a, b), (a, b)), lambda r, ct: (_bf_dot_nt(r[1], ct), _bf_dot(r[0], ct)))


def _hgrn_chunk(tri, tri_t, masks, qraw, fpre, v, st, lb):
    q = _silu(qraw)
    f = lb + (1.0 - lb) * _sigmoid(fpre)
    k = 1.0 - f
    e = _hgrn_sums(tri, tri_t, jnp.log(f))
    prefix, suffix, whole = e[0], e[1], e[-1]
    scores = masks[0] * _mm_nt(q, k)
    for n in range(len(HG_LEVELS)):
        decay = jnp.exp(e[2 + n])
        scores = scores + masks[n + 1] * _mm_nt(q * decay, k * decay)
    o = _mm_nt(q * jnp.exp(prefix), st) + _mm(scores, v)
    st_new = st * jnp.exp(whole) + _mm_tn(v, k * jnp.exp(suffix))
    return o, st_new


HG_HEADS_PER_STEP = 4
HG_LANES = HG_HEADS_PER_STEP * HG_DIM


def _hgrn_specs(S, consts):
    col = lambda base: pl.BlockSpec((S, HG_LANES), lambda p, base=base: (0, base // HG_HEADS_PER_STEP + p))
    whole = [pl.BlockSpec(a.shape, lambda p, n=a.ndim: (0,) * n) for a in consts]
    return col, whole


def _hgrn_fwd(proj, lbs, consts, job=None):
    S = proj.shape[0]
    nc = S // HG_CHUNK
    heads = range(HG_HEADS_PER_STEP)

    def body(q_ref, f_ref, i_ref, lb_ref, tri_ref, trit_ref, mask_ref, o_ref, st_ref):
        tri, tri_t = tri_ref[...], trit_ref[...]
        masks = [mask_ref[n] for n in range(len(HG_LEVELS) + 1)]

        def chunk(ci, sts):
            r0 = pl.multiple_of(ci * HG_CHUNK, HG_CHUNK)
            rows = pl.ds(r0, HG_CHUNK)
            new = []
            for hd in heads:
                lanes = pl.ds(hd * HG_DIM, HG_DIM)
                st_ref[hd, ci] = sts[hd]
                o, st_new = _hgrn_chunk(tri, tri_t, masks, q_ref[rows, lanes], f_ref[rows, lanes], i_ref[rows, lanes], sts[hd], lb_ref[hd])
                o_ref[rows, lanes] = o
                new.append(st_new)
            return tuple(new)

        lax.fori_loop(0, nc, chunk, tuple(jnp.zeros((HG_DIM, HG_DIM), F32) for _ in heads))

    col, whole = _hgrn_specs(S, consts)
    return _hosted(
        body, job, (proj, proj, proj, lbs, *consts), name="hgrn_fwd",
        grid=(WIDTH // HG_LANES,),
        in_specs=[col(16), col(20), col(24), pl.BlockSpec((HG_HEADS_PER_STEP, 1, 128), lambda p: (p, 0, 0))] + whole,
        out_specs=[col(0), pl.BlockSpec((HG_HEADS_PER_STEP, nc, HG_DIM, HG_DIM), lambda p: (p, 0, 0, 0))],
        out_shape=[jax.ShapeDtypeStruct((S, WIDTH), F32), jax.ShapeDtypeStruct((WIDTH // 128, nc, HG_DIM, HG_DIM), F32)],
        semantics=("parallel",))


def _hgrn_bwd(proj, lbs, states, d_o, consts, dproj, job=None):
    S = proj.shape[0]
    nc = S // HG_CHUNK

    def body(q_ref, f_ref, i_ref, lb_ref, st_ref, do_ref, tri_ref, trit_ref, mask_ref, _, dlb_ref, dproj_ref, stage, sems):
        dq_ref, df_ref, di_ref = stage.at[0], stage.at[1], stage.at[2]
        masks = [mask_ref[n] for n in range(len(HG_LEVELS) + 1)]
        fn = functools.partial(_hgrn_chunk, tri_ref[...], trit_ref[...], masks)
        heads = range(HG_HEADS_PER_STEP)

        def chunk(n, carry):
            ci = nc - 1 - n
            r0 = pl.multiple_of(ci * HG_CHUNK, HG_CHUNK)
            rows = pl.ds(r0, HG_CHUNK)
            new = []
            for hd in heads:
                d_st, dlb = carry[hd]
                lanes = pl.ds(hd * HG_DIM, HG_DIM)
                _, pull = jax.vjp(fn, q_ref[rows, lanes], f_ref[rows, lanes], i_ref[rows, lanes], st_ref[hd, ci], lb_ref[hd])
                dq, df, di, d_prev, dl = pull((do_ref[rows, lanes], d_st))
                dq_ref[rows, lanes] = dq.astype(BF)
                df_ref[rows, lanes] = df.astype(BF)
                di_ref[rows, lanes] = di.astype(BF)
                new.append((d_prev, dlb + dl))
            return tuple(new)

        zero = (jnp.zeros((HG_DIM, HG_DIM), F32), jnp.zeros((1, HG_DIM), F32))
        done = lax.fori_loop(0, nc, chunk, tuple(zero for _ in heads))
        for hd in heads:
            dlb_ref[hd] = done[hd][1]
        lane0 = pl.program_id(0) * HG_LANES
        _put_columns([dq_ref, df_ref, di_ref], dproj_ref, [WIDTH * n + lane0 for n in (4, 5, 6)], sems)

    col, whole = _hgrn_specs(S, consts)
    head = pl.BlockSpec((HG_HEADS_PER_STEP, 1, 128), lambda p: (p, 0, 0))
    hbm = pl.BlockSpec(memory_space=pl.ANY)
    n_in = 6 + len(consts)
    return _hosted(
        body, job, (proj, proj, proj, lbs, states, d_o, *consts, dproj), name="hgrn_bwd",
        grid=(WIDTH // HG_LANES,),
        in_specs=[col(16), col(20), col(24), head, pl.BlockSpec((HG_HEADS_PER_STEP, nc, HG_DIM, HG_DIM), lambda p: (p, 0, 0, 0)), col(0)]
        + whole + [hbm],
        out_specs=[head, hbm],
        out_shape=[jax.ShapeDtypeStruct((WIDTH // 128, 1, 128), F32), jax.ShapeDtypeStruct(dproj.shape, dproj.dtype)],
        scratch_shapes=[pltpu.VMEM((3, S, HG_LANES), BF), pltpu.SemaphoreType.DMA((3,))],
        semantics=("arbitrary",), aliases={n_in: 1})


def _shift_down(x, n):
    rows = lax.broadcasted_iota(jnp.int32, x.shape, 0)
    return jnp.where(rows >= n, pltpu.roll(x, n, 0), 0.0)


def _shift_up(x, n):
    S = x.shape[0]
    rows = lax.broadcasted_iota(jnp.int32, x.shape, 0)
    return jnp.where(rows < S - n, pltpu.roll(x, S - n, 0), 0.0)


def _branch_fwd(proj, o_a, o_b, norm_w, conv_w, layer):
    S = proj.shape[0]

    def body(oa_ref, za_ref, ob_ref, zb_ref, nw_ref, pre_ref, post_ref, u_ref, zc_ref, cw_ref, ya_ref, yb_ref, yc_ref):
        ya_ref[...] = (oa_ref[...] * _silu(za_ref[...])).astype(BF)
        ob = ob_ref[...]
        rn = lax.rsqrt(jnp.mean(ob * ob, axis=-1, keepdims=True) + RMS_EPS)
        yb_ref[...] = (ob * rn * nw_ref[layer:layer + 1, :] * _silu(zb_ref[...])).astype(BF)
        pu = pre_ref[...] * u_ref[...]
        conv = cw_ref[2:3, :] * pu + cw_ref[1:2, :] * _shift_down(pu, 1) + cw_ref[0:1, :] * _shift_down(pu, 2)
        yc_ref[...] = (post_ref[...] * conv * _silu(zc_ref[...])).astype(BF)

    col = lambda base: pl.BlockSpec((S, 128), lambda p, base=base: (0, base + p))
    out = jax.ShapeDtypeStruct((S, WIDTH), BF)
    return pl.pallas_call(
        body, name="branch_fwd",
        grid=(WIDTH // 128,),
        in_specs=[col(0), col(12), col(0), col(28), pl.BlockSpec(norm_w.shape, lambda p: (0, 0)),
                  col(32), col(36), col(40), col(44), pl.BlockSpec((None, None, 3, 128), lambda p: (p, layer, 0, 0))],
        out_specs=[col(0), col(0), col(0)],
        out_shape=[out, out, out],
        compiler_params=_params(dimension_semantics=("parallel",)),
    )(o_a, proj, o_b, proj, norm_w, proj, proj, proj, proj, conv_w)


def _put_columns(tiles, dproj_ref, firsts, sems):
    copies = [pltpu.make_async_copy(t, dproj_ref.at[:, pl.ds(pl.multiple_of(c, 128), t.shape[1])], sems.at[n])
              for n, (t, c) in enumerate(zip(tiles, firsts))]
    for cp in copies:
        cp.start()
    for cp in copies:
        cp.wait()


def _branch_bwd(proj, o_a, o_b, norm_w, conv_w, dy_a, dy_b, dy_c, dproj, layer):
    S = proj.shape[0]
    firsts = [WIDTH * n for n in (3, 7, 8, 9, 10, 11)]

    def dsilu(z):
        s = _sigmoid(z)
        return s * z, s * (1.0 + z * (1.0 - s))

    def body(oa_ref, za_ref, ob_ref, zb_ref, nw_ref, pre_ref, post_ref, u_ref, zc_ref, cw_ref, dya_ref, dyb_ref, dyc_ref, _,
             doa_ref, dob_ref, dnw_ref, dcw_ref, dproj_ref, stage, sems):
        dza_ref, dzb_ref, dpre_ref, dpost_ref, du_ref, dzc_ref = [stage.at[n] for n in range(6)]
        dya = dya_ref[...]
        sa, dsa = dsilu(za_ref[...])
        doa_ref[...] = dya * sa
        dza_ref[...] = (dya * oa_ref[...] * dsa).astype(BF)

        dyb = dyb_ref[...]
        ob = ob_ref[...]
        nw = nw_ref[layer:layer + 1, :]
        sb, dsb = dsilu(zb_ref[...])
        rn = lax.rsqrt(jnp.mean(ob * ob, axis=-1, keepdims=True) + RMS_EPS)
        on = ob * rn
        dzb_ref[...] = (dyb * on * nw * dsb).astype(BF)
        don_w = dyb * sb
        dnw_ref[...] = jnp.sum(don_w * on, axis=0, keepdims=True)
        don = don_w * nw
        dob_ref[...] = rn * (don - on * jnp.mean(don * on, axis=-1, keepdims=True))

        dyc = dyc_ref[...]
        pre, post, u = pre_ref[...], post_ref[...], u_ref[...]
        sc, dsc = dsilu(zc_ref[...])
        pu = pre * u
        pu1, pu2 = _shift_down(pu, 1), _shift_down(pu, 2)
        conv = cw_ref[2:3, :] * pu + cw_ref[1:2, :] * pu1 + cw_ref[0:1, :] * pu2
        dzc_ref[...] = (dyc * post * conv * dsc).astype(BF)
        dpost_ref[...] = (dyc * conv * sc).astype(BF)
        dconv = dyc * post * sc
        dcw_ref[0:1, :] = jnp.sum(dconv * pu2, axis=0, keepdims=True)
        dcw_ref[1:2, :] = jnp.sum(dconv * pu1, axis=0, keepdims=True)
        dcw_ref[2:3, :] = jnp.sum(dconv * pu, axis=0, keepdims=True)
        dpu = cw_ref[2:3, :] * dconv + cw_ref[1:2, :] * _shift_up(dconv, 1) + cw_ref[0:1, :] * _shift_up(dconv, 2)
        dpre_ref[...] = (dpu * u).astype(BF)
        du_ref[...] = (dpu * pre).astype(BF)
        lane0 = pl.program_id(0) * 128
        _put_columns([stage.at[n] for n in range(6)], dproj_ref, [c + lane0 for c in firsts], sems)

    col = lambda base: pl.BlockSpec((S, 128), lambda p, base=base: (0, base + p))
    f32 = jax.ShapeDtypeStruct((S, WIDTH), F32)
    hbm = pl.BlockSpec(memory_space=pl.ANY)
    return pl.pallas_call(
        body, name="branch_bwd",
        grid=(WIDTH // 128,),
        in_specs=[col(0), col(12), col(0), col(28), pl.BlockSpec(norm_w.shape, lambda p: (0, 0)),
                  col(32), col(36), col(40), col(44), pl.BlockSpec((None, None, 3, 128), lambda p: (p, layer, 0, 0)),
                  col(0), col(0), col(0), hbm],
        out_specs=[col(0), col(0), pl.BlockSpec((None, 1, 128), lambda p: (p, 0, 0)), pl.BlockSpec((None, 3, 128), lambda p: (p, 0, 0)), hbm],
        out_shape=[f32, f32, jax.ShapeDtypeStruct((WIDTH // 128, 1, 128), F32), jax.ShapeDtypeStruct((WIDTH // 128, 3, 128), F32),
                   jax.ShapeDtypeStruct(dproj.shape, dproj.dtype)],
        scratch_shapes=[pltpu.VMEM((6, S, 128), BF), pltpu.SemaphoreType.DMA((6,))],
        input_output_aliases={13: 4},
        compiler_params=_params(dimension_semantics=("arbitrary",)),
    )(o_a, proj, o_b, proj, norm_w, proj, proj, proj, proj, conv_w, dy_a, dy_b, dy_c, dproj)


def _branch_proj(y_refs, wb_ref):
    out = []
    for i in range(3):
        yv = y_refs[i][...]
        out.append(jnp.concatenate([_dot(yv, wb_ref[j, i]) for j in range(N_CHIPS)], axis=1))
    return out


def _merge_fwd(x, mod, proj, ys, wb, wo, ln_g, ln_b, layer, tm=256):
    S, D = x.shape
    tm = min(tm, S)

    def body(x_ref, mod_ref, ga_ref, gb_ref, gc_ref, ya_ref, yb_ref, yc_ref, wb_ref, wo_ref, g_ref, b_ref, xo_ref, mg_ref, y_ref):
        ps = _branch_proj((ya_ref, yb_ref, yc_ref), wb_ref)
        merged = _sigmoid(ga_ref[...]) * ps[0] + _sigmoid(gb_ref[...]) * ps[1] + _sigmoid(gc_ref[...]) * ps[2]
        mb = merged.astype(BF)
        mg_ref[...] = mb
        y = _dot(mb, wo_ref[...].reshape(D, D))
        y_ref[...] = y
        r = ALPHA * x_ref[...] + (1.0 + mod_ref[:, 2 * D:3 * D]) * y
        xn, _ = _standardize(r)
        xo_ref[...] = xn * g_ref[layer:layer + 1, :] + b_ref[layer:layer + 1, :]

    row = lambda w, c=0: pl.BlockSpec((tm, w), lambda i, c=c: (i, c))
    whole = lambda a: pl.BlockSpec(a.shape, lambda i, n=a.ndim: (0,) * n)
    return pl.pallas_call(
        body, name="merge_fwd",
        grid=(S // tm,),
        in_specs=[row(D), whole(mod), row(D, 6), row(D, 7), row(D, 8), row(WIDTH), row(WIDTH), row(WIDTH), whole(wb),
                  whole(wo), whole(ln_g), whole(ln_b)],
        out_specs=[row(D), row(D), row(D)],
        out_shape=[jax.ShapeDtypeStruct((S, D), F32), jax.ShapeDtypeStruct((S, D), BF), jax.ShapeDtypeStruct((S, D), F32)],
        compiler_params=_params(dimension_semantics=("parallel",)),
    )(x, mod, proj, proj, proj, *ys, wb, wo, ln_g, ln_b)


def _merge_bwd(dxo, x, y, mod, proj, ys, wb, wo, ln_g, layer, tm=256):
    S, D = x.shape
    tm = min(tm, S)

    def body(dxo_ref, x_ref, y_ref, mod_ref, ga_ref, gb_ref, gc_ref, ya_ref, yb_ref, yc_ref, wb_ref, wo_ref, g_ref,
             dxr_ref, dy_ref, dp_ref, dg_ref, dya_ref, dyb_ref, dyc_ref, dlg_ref, dlb_ref, dgt_ref):
        @pl.when(pl.program_id(0) == 0)
        def _():
            dlg_ref[...] = jnp.zeros_like(dlg_ref)
            dlb_ref[...] = jnp.zeros_like(dlb_ref)
            dgt_ref[...] = jnp.zeros_like(dgt_ref)

        gate1 = 1.0 + mod_ref[:, 2 * D:3 * D]
        yv = y_ref[...]
        xn, rstd = _standardize(ALPHA * x_ref[...] + gate1 * yv)
        dxo = dxo_ref[...]
        dlg_ref[...] += jnp.sum(dxo * xn, axis=0, keepdims=True)
        dlb_ref[...] += jnp.sum(dxo, axis=0, keepdims=True)
        dr = _standardize_bwd(dxo * g_ref[layer:layer + 1, :], xn, rstd)
        dxr_ref[...] = ALPHA * dr
        dgt_ref[...] += jnp.sum(dr * yv, axis=0, keepdims=True)
        dyb = (gate1 * dr).astype(BF)
        dy_ref[...] = dyb
        dmerged = _dot_nt(dyb, wo_ref[...].reshape(D, D))
        ps = _branch_proj((ya_ref, yb_ref, yc_ref), wb_ref)
        quarter = D // N_CHIPS
        for i, (gate_ref, out_ref) in enumerate(((ga_ref, dya_ref), (gb_ref, dyb_ref), (gc_ref, dyc_ref))):
            sg = _sigmoid(gate_ref[...])
            dg_ref[:, i * D:(i + 1) * D] = (dmerged * ps[i] * sg * (1.0 - sg)).astype(BF)
            dp = (dmerged * sg).astype(BF)
            dp_ref[:, i * D:(i + 1) * D] = dp
            acc = None
            for j in range(N_CHIPS):
                t = _dot_nt(dp[:, j * quarter:(j + 1) * quarter], wb_ref[j, i])
                acc = t if acc is None else acc + t
            out_ref[...] = acc

    row = lambda w, c=0: pl.BlockSpec((tm, w), lambda i, c=c: (i, c))
    whole = lambda a: pl.BlockSpec(a.shape, lambda i, n=a.ndim: (0,) * n)
    vec = pl.BlockSpec((1, D), lambda i: (0, 0))
    sd = jax.ShapeDtypeStruct
    return pl.pallas_call(
        body, name="merge_bwd",
        grid=(S // tm,),
        in_specs=[row(D), row(D), row(D), whole(mod), row(D, 6), row(D, 7), row(D, 8), row(WIDTH), row(WIDTH), row(WIDTH), whole(wb),
                  whole(wo), whole(ln_g)],
        out_specs=[row(D), row(D), row(3 * D), row(3 * D, IN_COLS // (3 * D) - 1), row(WIDTH), row(WIDTH), row(WIDTH), vec, vec, vec],
        out_shape=[sd((S, D), F32), sd((S, D), BF), sd((S, 3 * D), BF), sd((S, IN_COLS), BF), sd((S, WIDTH), F32), sd((S, WIDTH), F32),
                   sd((S, WIDTH), F32), sd((1, D), F32), sd((1, D), F32), sd((1, D), F32)],
        compiler_params=_params(dimension_semantics=("arbitrary",)),
    )(dxo, x, y, mod, proj, proj, proj, *ys, wb, wo, ln_g)


def _loss_head(x, target, tm=512):
    S, D = x.shape
    tm = min(tm, S)

    def body(x_ref, t_ref, dx_ref, loss_ref):
        @pl.when(pl.program_id(0) == 0)
        def _():
            loss_ref[...] = jnp.zeros_like(loss_ref)

        err = x_ref[...] - t_ref[...]
        dx_ref[...] = err * (1.0 / D)
        loss_ref[...] += 0.5 * jnp.sum(jnp.mean(err * err, axis=-1, keepdims=True))

    row = pl.BlockSpec((tm, D), lambda i: (i, 0))
    return pl.pallas_call(
        body, name="loss_head",
        grid=(S // tm,),
        in_specs=[row, row],
        out_specs=[row, pl.BlockSpec((8, 128), lambda i: (0, 0))],
        out_shape=[jax.ShapeDtypeStruct((S, D), F32), jax.ShapeDtypeStruct((8, 128), F32)],
        compiler_params=_params(dimension_semantics=("arbitrary",)),
    )(x, target)


def _proj_bwd(dproj, wgs, x, mod, dx_res, job=None, tm=512, tk=768):
    S, D = x.shape
    tm = min(tm, S)
    n = len(wgs)
    w_args = list(wgs)
    if n == 1:
        per = wgs[0].shape[-1] // tk
        w_specs = [pl.BlockSpec((None, D, tk), lambda i, k: (k // per, 0, k % per))]
    else:
        assert wgs[0].shape[-1] == tk
        w_specs = [pl.BlockSpec((None, D, tk), lambda i, k, c=c: (jnp.minimum((k + n - 1 - c) // n, N_CHIPS - 1), 0, 0))
                   for c in range(n)]
    nk = IN_COLS // tk

    def body(dp_ref, *rest):
        w_refs, (x_ref, mod_ref, dxr_ref, dx_ref, dsh_ref, dsc_ref, acc) = rest[:n], rest[n:]
        i, k = pl.program_id(0), pl.program_id(1)

        @pl.when((i == 0) & (k == 0))
        def _():
            dsh_ref[...] = jnp.zeros_like(dsh_ref)
            dsc_ref[...] = jnp.zeros_like(dsc_ref)

        @pl.when(k == 0)
        def _():
            acc[...] = jnp.zeros_like(acc)

        for c in range(n):
            @pl.when(k % n == c)
            def _(c=c):
                acc[...] += _dot_nt(dp_ref[...], w_refs[c][...])

        @pl.when(k == nk - 1)
        def _():
            dh = acc[...]
            xs, rstd = _standardize(x_ref[...])
            dsh_ref[...] += jnp.sum(dh, axis=0, keepdims=True)
            dsc_ref[...] += jnp.sum(dh * xs, axis=0, keepdims=True)
            dx_ref[...] = _standardize_bwd(dh * (1.0 + mod_ref[:, D:2 * D]), xs, rstd) + dxr_ref[...]

    row = pl.BlockSpec((tm, D), lambda i, k: (i, 0))
    vec = pl.BlockSpec((1, D), lambda i, k: (0, 0))
    return _hosted(
        body, job, (dproj, *w_args, x, mod, dx_res), name="proj_bwd",
        grid=(S // tm, nk),
        in_specs=[pl.BlockSpec((tm, tk), lambda i, k: (i, k))] + w_specs + [row, pl.BlockSpec((1, 3 * D), lambda i, k: (0, 0)), row],
        out_specs=[row, vec, vec],
        out_shape=[jax.ShapeDtypeStruct((S, D), F32), jax.ShapeDtypeStruct((1, D), F32), jax.ShapeDtypeStruct((1, D), F32)],
        scratch_shapes=[pltpu.VMEM((tm, D), F32)],
        semantics=("arbitrary", "arbitrary"))


def _grad_w(a, b, mid, prev, name, tm, tn, b_col0=0):
    S, M = a.shape
    shape = prev.shape
    n_shard = shape[-1]
    per = n_shard // tn
    nm = M // tm
    lead = len(mid)

    def body(*refs):
        a_ref, b_ref, o_ref = refs[0], refs[1], refs[-1]
        o_ref[...] = _dot_tn(a_ref[...], b_ref[...]).astype(BF)

    in_specs = [pl.BlockSpec((S, tm), lambda m, n: (0, m)),
                pl.BlockSpec((S, tn), lambda m, n: (0, b_col0 // tn + n))]
    args = [a, b]
    aliases = {}
    if not isinstance(prev, jax.ShapeDtypeStruct):
        in_specs.append(pl.BlockSpec(memory_space=pl.ANY))
        args.append(prev)
        aliases = {2: 0}
    return pl.pallas_call(
        body, name=name,
        grid=(nm, N_CHIPS * per),
        in_specs=in_specs,
        out_specs=pl.BlockSpec((None,) + (None,) * lead + (tm, tn), lambda m, n: (n // per,) + tuple(mid) + (m, n % per)),
        out_shape=jax.ShapeDtypeStruct(shape, BF),
        input_output_aliases=aliases,
        compiler_params=_params(dimension_semantics=("parallel", "parallel")),
    )(*args)


def _all_gather8(x, name):
    R, N = x.shape

    def body(x_ref, out_ref, send_sems, recv_sems):
        mx, my, mc = lax.axis_index("x"), lax.axis_index("y"), lax.axis_index("c")
        me = 4 * mx + 2 * my + mc
        out_ref[me] = x_ref[...]
        copies = []
        for k in range(1, N_DEV):
            peer = (_flip(mx, k & 4), _flip(my, k & 2), _flip(mc, k & 1))
            cp = pltpu.make_async_remote_copy(src_ref=x_ref, dst_ref=out_ref.at[me], send_sem=send_sems.at[k - 1],
                                              recv_sem=recv_sems.at[k - 1], device_id=peer, device_id_type=MESH)
            cp.start()
            copies.append(cp)
        for cp in copies:
            cp.wait()

    return pl.pallas_call(
        body, name=name,
        in_specs=[pl.BlockSpec(memory_space=pltpu.VMEM)],
        out_specs=pl.BlockSpec(memory_space=pltpu.VMEM),
        out_shape=jax.ShapeDtypeStruct((N_DEV, R, N), F32),
        scratch_shapes=[pltpu.SemaphoreType.DMA((N_DEV - 1,)), pltpu.SemaphoreType.DMA((N_DEV - 1,))],
        compiler_params=_params(),
    )(x)


def _rows2d(a):
    return a.reshape(-1, a.shape[-1])


def _tile_rows(rows, cols, n_arrays):
    budget = (24 << 20) // (n_arrays * 2 * 4 * cols)
    if rows <= budget:
        return rows
    tm = 8
    for cand in range(8, budget + 1, 8):
        if rows % cand == 0:
            tm = cand
    return tm


SUM_ROWS = 256


def _sum_cores(g, sent, where):
    chips, lead, _, r, cols = g.shape
    tr = min(r, SUM_ROWS)

    def body(where_ref, g_ref, s_ref, out_ref):
        out_ref[...] = (g_ref[...].astype(F32) + s_ref[...].astype(F32)).astype(BF)

    spec = pl.BlockSpec((None, tr, cols), lambda i, j, where_ref: (i, j, 0))
    out = pl.pallas_call(
        body, name="sum_cores",
        grid_spec=pltpu.PrefetchScalarGridSpec(
            num_scalar_prefetch=1, grid=(chips * lead, r // tr),
            in_specs=[pl.BlockSpec((None, None, tr, cols), lambda i, j, where_ref: (i, where_ref[1], j, 0)), spec],
            out_specs=spec),
        out_shape=jax.ShapeDtypeStruct((chips * lead, r, cols), BF),
        compiler_params=_params(dimension_semantics=("parallel", "parallel")),
    )(where, g.reshape(chips * lead, 2, r, cols), sent.reshape(chips * lead, r, cols))
    return out.reshape(chips, lead, r, cols)


def _sum_chips(sums, got, where):
    _, lead, r, cols = sums.shape
    tr = min(r, SUM_ROWS)

    def body(where_ref, s_ref, g_ref, out_ref):
        out_ref[...] = ((s_ref[...].astype(F32) + g_ref[0].astype(F32)) + g_ref[1].astype(F32)) + g_ref[2].astype(F32)

    return pl.pallas_call(
        body, name="sum_chips",
        grid_spec=pltpu.PrefetchScalarGridSpec(
            num_scalar_prefetch=1, grid=(lead, r // tr),
            in_specs=[pl.BlockSpec((None, None, tr, cols), lambda i, j, where_ref: (where_ref[0], i, j, 0)),
                      pl.BlockSpec((N_CHIPS - 1, None, tr, cols), lambda i, j, where_ref: (0, i, j, 0))],
            out_specs=pl.BlockSpec((None, None, tr, cols), lambda i, j, where_ref: (i, where_ref[1], j, 0))),
        out_shape=jax.ShapeDtypeStruct((lead, 2, r, cols), F32),
        compiler_params=_params(dimension_semantics=("parallel", "parallel")),
    )(where, sums, got)


def _adamw(w, m, v, groups, name):
    shape = w.shape
    w2, m2, v2 = _rows2d(w), _rows2d(m), _rows2d(v)
    rows, cols = w2.shape
    ng = len(groups)
    n = len(groups[0])
    slab = rows // ng
    gs = [_rows2d(g) for grp in groups for g in grp]
    tm = _tile_rows(slab, cols, 7 + n)
    tiles = slab // tm
    c1 = 1.0 / (1.0 - ADAM_B1 ** ADAM_STEP)
    c2 = 1.0 / (1.0 - ADAM_B2 ** ADAM_STEP)

    def body(*refs):
        w_ref, m_ref, v_ref = refs[:3]
        g_refs = refs[3:3 + ng * n]
        go_ref, d_ref, mo_ref, vo_ref = refs[3 + ng * n:]
        which = pl.program_id(0)
        for s in range(ng):
            @pl.when(which == s)
            def _(s=s):
                g = g_refs[s * n][...]
                for r in g_refs[s * n + 1:(s + 1) * n]:
                    g = g + r[...]
                mn = ADAM_B1 * m_ref[...] + (1.0 - ADAM_B1) * g
                vn = ADAM_B2 * v_ref[...] + (1.0 - ADAM_B2) * (g * g)
                go_ref[...] = g
                mo_ref[...] = mn
                vo_ref[...] = vn
                d_ref[...] = -ADAM_LR * ((mn * c1) / (jnp.sqrt(vn * c2) + ADAM_EPS) + ADAM_WD * w_ref[...])

    spec = pl.BlockSpec((tm, cols), lambda s, i: (s * tiles + i, 0))
    g_specs = [pl.BlockSpec((tm, cols), lambda s, i, k=k: (jnp.where(s == k, i, jnp.where(s < k, 0, tiles - 1)), 0))
               for k in range(ng) for _ in range(n)]
    outs = pl.pallas_call(
        body, name=name,
        grid=(ng, tiles),
        in_specs=[spec] * 3 + g_specs,
        out_specs=[spec] * 4,
        out_shape=[jax.ShapeDtypeStruct((rows, cols), F32)] * 4,
        compiler_params=_params(dimension_semantics=("arbitrary", "arbitrary")),
    )(w2, m2, v2, *gs)
    return [o.reshape(shape) for o in outs]


def _lower_bounds(r0, r1):
    top = jnp.maximum(r0, r1)
    e0, e1 = jnp.exp(r0 - top), jnp.exp(r1 - top)
    p0, p1 = e0 / (e0 + e1), e1 / (e0 + e1)
    return p0 - p0, (p0 + p1) - p0


def _lbs_fwd(lb_raw):
    def body(lb_ref, out_ref):
        l0, l1 = _lower_bounds(lb_ref[0:1, :], lb_ref[1:2, :])
        out_ref[0:1, :] = l0
        out_ref[1:2, :] = l1

    return pl.pallas_call(body, name="lower_bounds", out_shape=jax.ShapeDtypeStruct(lb_raw.shape, F32), compiler_params=_params())(lb_raw)


def _mod_rows(c_all, w_mod, tn=768):
    _, D, cols = w_mod.shape

    def body(c_ref, w_ref, out_ref):
        out_ref[...] = _dot(c_ref[...].astype(BF), w_ref[...].astype(BF))

    return pl.pallas_call(
        body, name="mod_rows",
        grid=(DEPTH,),
        in_specs=[pl.BlockSpec((N_DEV, D), lambda l: (0, 0)), pl.BlockSpec((None, D, cols), lambda l: (l, 0, 0))],
        out_specs=pl.BlockSpec((N_DEV, cols), lambda l: (0, l)),
        out_shape=jax.ShapeDtypeStruct((N_DEV, DEPTH * cols), F32),
        compiler_params=_params(dimension_semantics=("parallel",)),
    )(c_all, w_mod)


def _grad_w_mod(c_all, dmod_cols):
    D = c_all.shape[1]
    cols = dmod_cols.shape[-1]

    def body(c_ref, d_ref, out_ref):
        out_ref[...] = _dot_tn(c_ref[...].astype(BF), d_ref[...].astype(BF))

    return pl.pallas_call(
        body, name="grad_w_mod",
        grid=(DEPTH,),
        in_specs=[pl.BlockSpec((N_DEV, D), lambda l: (0, 0)), pl.BlockSpec((None, N_DEV, cols), lambda l: (l, 0, 0))],
        out_specs=pl.BlockSpec((None, D, cols), lambda l: (l, 0, 0)),
        out_shape=jax.ShapeDtypeStruct((DEPTH, D, cols), F32),
        compiler_params=_params(dimension_semantics=("parallel",)),
    )(c_all, dmod_cols)


def _sum_devices(parts):
    _, R, N = parts.shape

    def body(p_ref, out_ref):
        acc = p_ref[0]
        for d in range(1, N_DEV):
            acc = acc + p_ref[d]
        out_ref[...] = acc

    return pl.pallas_call(body, name="sum_devices", out_shape=jax.ShapeDtypeStruct((R, N), F32), compiler_params=_params())(parts)


def _lbs_bwd(lb_raw, dl):
    def body(lb_ref, dl_ref, out_ref):
        _, pull = jax.vjp(_lower_bounds, lb_ref[0:1, :], lb_ref[1:2, :])
        d0, d1 = pull((dl_ref[0:1, :], dl_ref[1:2, :]))
        out_ref[0:1, :] = d0
        out_ref[1:2, :] = d1

    return pl.pallas_call(body, name="lower_bounds_bwd", out_shape=jax.ShapeDtypeStruct(lb_raw.shape, F32), compiler_params=_params())(lb_raw, dl)


def kernel(x, c, w_mod, b_mod, w_in, conv_w, hgrn_norm_w, lower_bounds, w_branch, w_out, ln_g, ln_b, loss_target, m_w_mod, m_b_mod, m_w_in, m_conv_w, m_hgrn_norm_w, m_lower_bounds, m_w_branch, m_w_out, m_ln_g, m_ln_b, v_w_mod, v_b_mod, v_w_in, v_conv_w, v_hgrn_norm_w, v_lower_bounds, v_w_branch, v_w_out, v_ln_g, v_ln_b):
    D = D_MODEL
    x0 = x[0]
    target = loss_target[0]
    S = x0.shape[0]
    mx, my, mc = lax.axis_index("x"), lax.axis_index("y"), lax.axis_index("c")
    chip = 2 * mx + my
    me = 2 * chip + mc
    mod_cols = 3 * D // N_CHIPS

    plan = _Plan(w_in.astype(BF), w_branch.astype(BF), w_out.astype(BF), chip, mc)
    plan.first()

    n_conv = DEPTH * 3 * (WIDTH // N_CHIPS)
    first = jnp.concatenate([c, conv_w.reshape(1, n_conv), jnp.zeros((1, 2 * D - D - n_conv), F32)], axis=1)
    first = _all_gather8(first.reshape(8, 2 * D // 8), "gather_c").reshape(N_DEV, 2 * D)
    c_all = first[:, :D]
    conv_all = first[:, D:D + n_conv].reshape(N_DEV, DEPTH, 3, WIDTH // N_CHIPS)[0::2]
    mod_part = _all_gather8(_mod_rows(c_all, w_mod), "gather_mod")[0::2]
    mod_part = lax.dynamic_index_in_dim(mod_part, me, axis=1, keepdims=False).reshape(N_CHIPS, DEPTH, mod_cols)
    mods = [(mod_part[:, l].reshape(1, 3 * D) + b_mod[l][None, :]) for l in range(DEPTH)]
    lbs = _lbs_fwd(lower_bounds).reshape(DEPTH, WIDTH // 128, 1, 128)
    loss_blk, dx, small = _local_step(x0, target, mods, lbs, conv_all, hgrn_norm_w, ln_g, ln_b, plan)

    n_mod, n_nw, n_lb, n_ln, n_cw = DEPTH * 3 * D, DEPTH * 128, DEPTH * WIDTH, DEPTH * D, DEPTH * 3 * WIDTH
    row = jnp.concatenate(
        [jnp.concatenate([small[l][0], small[l][1], small[l][2]], axis=1) for l in range(DEPTH)]
        + [jnp.sum(small[l][3], axis=0) for l in range(DEPTH)]
        + [small[l][4].reshape(1, WIDTH) for l in range(DEPTH)]
        + [small[l][5] for l in range(DEPTH)] + [small[l][6] for l in range(DEPTH)]
        + [jnp.transpose(small[l][7], (1, 0, 2)).reshape(1, 3 * WIDTH) for l in range(DEPTH)]
        + [loss_blk[0:1, :]], axis=1)
    n_row = row.shape[1]
    fold = -(-n_row // (8 * 128)) * 128
    rows = jnp.concatenate([row, jnp.zeros((1, 8 * fold - n_row), F32)], axis=1).reshape(8, fold)

    whole, gathered = plan.finish(rows)
    grads = {kind: [[whole[(kind, l)]] for l in range(DEPTH)] for kind in ("in", "br", "out")}

    off_nw = n_mod
    off_lb = off_nw + n_nw
    off_lng = off_lb + n_lb
    off_lnb = off_lng + n_ln
    off_cw = off_lnb + n_ln
    off_loss = off_cw + n_cw
    total = _sum_devices(gathered).reshape(1, 8 * fold)
    gathered = gathered.reshape(N_DEV, 1, 8 * fold)
    d_lower = _lbs_bwd(lower_bounds, total[0, off_lb:off_lng].reshape(DEPTH, WIDTH))
    loss = total[0, off_loss]
    d_b_mod = total[0, :n_mod].reshape(DEPTH, 3 * D)
    d_norm_w = total[0, off_nw:off_lb].reshape(DEPTH, 128)
    d_ln_g = total[0, off_lng:off_lnb].reshape(DEPTH, D)
    d_ln_b = total[0, off_lnb:off_cw].reshape(DEPTH, D)
    d_conv = total[0, off_cw:off_loss].reshape(DEPTH, 3, N_CHIPS, WIDTH // N_CHIPS)
    d_conv = lax.dynamic_index_in_dim(d_conv, chip, axis=2, keepdims=False)
    dmod_all = gathered[:, 0, :n_mod].reshape(N_DEV, DEPTH, N_CHIPS, mod_cols)
    dmod_cols = jnp.transpose(lax.dynamic_index_in_dim(dmod_all, chip, axis=2, keepdims=False), (1, 0, 2))
    d_w_mod = _grad_w_mod(c_all, dmod_cols)

    res = {}
    res["w_mod"] = _adamw(w_mod, m_w_mod, v_w_mod, [[d_w_mod]], "adamw_w_mod")
    res["b_mod"] = _adamw(b_mod, m_b_mod, v_b_mod, [[d_b_mod]], "adamw_b_mod")
    res["w_in"] = _adamw(w_in, m_w_in, v_w_in, grads["in"], "adamw_w_in")
    res["conv_w"] = _adamw(conv_w, m_conv_w, v_conv_w, [[d_conv]], "adamw_conv_w")
    res["hgrn_norm_w"] = _adamw(hgrn_norm_w, m_hgrn_norm_w, v_hgrn_norm_w, [[d_norm_w]], "adamw_norm_w")
    res["lower_bounds"] = _adamw(lower_bounds, m_lower_bounds, v_lower_bounds, [[d_lower]], "adamw_lower_bounds")
    res["w_branch"] = _adamw(w_branch, m_w_branch, v_w_branch, grads["br"], "adamw_w_branch")
    res["w_out"] = _adamw(w_out, m_w_out, v_w_out, grads["out"], "adamw_w_out")
    res["ln_g"] = _adamw(ln_g, m_ln_g, v_ln_g, [[d_ln_g]], "adamw_ln_g")
    res["ln_b"] = _adamw(ln_b, m_ln_b, v_ln_b, [[d_ln_b]], "adamw_ln_b")
    names = ["w_mod", "b_mod", "w_in", "conv_w", "hgrn_norm_w", "lower_bounds", "w_branch", "w_out", "ln_g", "ln_b"]
    return (loss, dx[None], *[res[n][0] for n in names], *[res[n][1] for n in names],
            *[res[n][2] for n in names], *[res[n][3] for n in names])


class _Plan:
    FIRST_CHUNKS = 3
    WINDOWS = ((0, 2 * IN_COLS // N_CHIPS // 3), (2 * IN_COLS // N_CHIPS // 3, IN_COLS // N_CHIPS // 3))

    def __init__(self, w_in, w_br, w_out, chip, core):
        self.local = {"in": w_in, "br": w_br, "out": w_out}
        self.chip, self.where = chip, jnp.stack([chip, core]).astype(jnp.int32)
        self.gathered, self.partial, self.grads, self.chip_sums, self.scattered, self.pending = {}, {}, {}, {}, {}, {}

    def chunks(self, l):
        return self.FIRST_CHUNKS if l == 0 else 1

    def w_in(self, l):
        return [self.gathered[("in", l, c)] for c in range(self.chunks(l))]

    def _shard(self, key):
        mine = self.local[key[0]][key[1]]
        if key[0] == "in":
            cols = mine.shape[-1] // self.chunks(key[1])
            mine = mine[:, key[2] * cols:(key[2] + 1) * cols]
        return mine

    def _slab(self, key):
        mine = _halves(self._shard(key))
        return lax.dynamic_update_slice(lax.empty((N_CHIPS,) + mine.shape, mine.dtype), mine[None], (self.chip, 0, 0, 0, 0))

    def _gather(self, keys):
        return ("gather", keys), _gather_job([self._slab(key) for key in keys])

    def _gather_window(self, key, n):
        slab = self._slab(key) if n == 0 else self.partial[key]
        return ("gather" if n == len(self.WINDOWS) - 1 else "gather_part", [key]), _gather_job([slab], self.WINDOWS[n])

    def _to_sibling(self, keys):
        return ("to_sibling", keys), _to_sibling_job([_halves(self.grads[key], 1) for key in keys])

    def _scatter(self, keys):
        return ("scatter", keys), _scatter_job([self.chip_sums[key] for key in keys])

    def job(self, stage, l, c=0):
        parts = []
        if stage == "proj_fwd":
            parts = [self._gather([("in", l, c + 1)] if c + 1 < self.chunks(l) else [("br", l), ("out", l)])]
        elif stage == "attn_fwd" and l + 1 < DEPTH:
            parts = [self._gather_window(("in", l + 1, 0), 0)]
        elif stage == "hgrn_fwd" and l + 1 < DEPTH:
            parts = [self._gather_window(("in", l + 1, 0), 1)]
        elif stage == "attn_bwd":
            parts = [self._to_sibling([("out", l), ("br", l)])] + ([self._scatter([("in", l + 1)])] if l + 1 < DEPTH else [])
        elif stage == "hgrn_bwd":
            parts = [self._scatter([("out", l), ("br", l)])]
        elif stage == "proj_bwd":
            parts = [self._to_sibling([("in", l)])] if l else [self._scatter([("in", 0)])]
        self.pending[(stage, l, c)] = [(tag, len(job.outs)) for tag, job in parts]
        return _join_jobs([job for _, job in parts])

    def done(self, stage, l, outs, c=0):
        if outs is None:
            return
        at = 0
        for (what, keys), n_outs in self.pending[(stage, l, c)]:
            mine, at = outs[at:at + n_outs], at + n_outs
            for n, key in enumerate(keys):
                if what == "gather":
                    self.gathered[key] = mine[n].reshape((N_CHIPS,) + self._shard(key).shape)
                elif what == "gather_part":
                    self.partial[key] = mine[n]
                elif what == "to_sibling":
                    self.chip_sums[key] = _sum_cores(_halves(self.grads[key], 1), mine[n], self.where)
                else:
                    self.scattered[key] = mine[n]

    def first(self):
        tag, job = self._gather([("in", 0, 0)])
        self.pending[("first", 0, 0)] = [(tag, len(job.outs))]
        self.done("first", 0, _run_job(job, "gather_first"))

    def took(self, key, grad):
        self.grads[key] = grad
        if key == ("in", 0):
            tag, job = self._to_sibling([key])
            self.pending[("took", 0, 0)] = [(tag, len(job.outs))]
            self.done("took", 0, _run_job(job, "to_sibling_last"))

    def finish(self, rows):
        keys = [(kind, l) for kind in ("in", "br", "out") for l in range(DEPTH)]
        halves = [_sum_chips(self.chip_sums[key], self.scattered[key], self.where) for key in keys]
        outs = _run_job(_join_jobs([_place_job(halves), _gather8_job(rows)]), "place_halves")
        return {key: w.reshape(self.grads[key].shape[1:]) for key, w in zip(keys, outs[:-1])}, outs[-1]


def _local_step(x0, target, mods, lbs, conv_all, hgrn_norm_w, ln_g, ln_b, plan):
    D = D_MODEL
    after, before = _attn_consts()
    hg_consts = _hgrn_consts()

    saved = []
    xl = x0
    for l in range(DEPTH):
        n = plan.chunks(l)
        (proj, h), got = _proj_fwd(xl, mods[l], plan.gathered[("in", l, 0)], plan.job("proj_fwd", l, 0), (0, n))
        plan.done("proj_fwd", l, got, 0)
        for c in range(1, n):
            (proj,), got = _proj_cols(h, plan.gathered[("in", l, c)], (c, n), proj, plan.job("proj_fwd", l, c))
            plan.done("proj_fwd", l, got, c)
        (o_a, tot), got = _attn_fwd(proj, after, plan.job("attn_fwd", l))
        plan.done("attn_fwd", l, got)
        (o_b, states), got = _hgrn_fwd(proj, lbs[l], hg_consts, plan.job("hgrn_fwd", l))
        plan.done("hgrn_fwd", l, got)
        ys = _branch_fwd(proj, o_a, o_b, hgrn_norm_w, conv_all, l)
        x_next, merged, y = _merge_fwd(xl, mods[l], proj, ys, plan.gathered[("br", l)], plan.gathered[("out", l)], ln_g, ln_b, l)
        saved.append((xl, proj, h, o_a, tot, o_b, states, ys, merged, y))
        xl = x_next
    dx, loss_blk = _loss_head(xl, target)

    small = [None] * DEPTH
    for l in reversed(range(DEPTH)):
        xin, proj, h, o_a, tot, o_b, states, ys, merged, y = saved[l]
        dx_res, dy, dp, dproj, dy_a, dy_b, dy_c, dln_g, dln_b, dgate = _merge_bwd(
            dx, xin, y, mods[l], proj, ys, plan.gathered[("br", l)], plan.gathered[("out", l)], ln_g, l)
        plan.took(("out", l), _grad_w_out(merged, dy))
        g_br = jax.ShapeDtypeStruct((N_CHIPS, 3, WIDTH, D // N_CHIPS), BF)
        for i in range(3):
            g_br = _grad_w(ys[i], dp, (i,), g_br, "grad_w_branch", tm=WIDTH, tn=D // N_CHIPS, b_col0=i * D)
        plan.took(("br", l), g_br)
        d_oa, d_ob, dnorm_w, dconv_w, dproj = _branch_bwd(proj, o_a, o_b, hgrn_norm_w, conv_all, dy_a, dy_b, dy_c, dproj, l)
        (dproj,), got = _attn_bwd(proj, d_oa, tot, after, before, dproj, plan.job("attn_bwd", l))
        plan.done("attn_bwd", l, got)
        (dlb, dproj), got = _hgrn_bwd(proj, lbs[l], states, d_ob, hg_consts, dproj, plan.job("hgrn_bwd", l))
        plan.done("hgrn_bwd", l, got)
        plan.took(("in", l), _grad_w(h, dproj, (), jax.ShapeDtypeStruct((N_CHIPS, D, IN_COLS // N_CHIPS), BF), "grad_w_in", tm=512, tn=2304))
        (dx, dshift, dscale), got = _proj_bwd(dproj, plan.w_in(l), xin, mods[l], dx_res, plan.job("proj_bwd", l))
        plan.done("proj_bwd", l, got)
        small[l] = (dshift, dscale, dgate, dnorm_w, dlb, dln_g, dln_b, dconv_w)
    return loss_blk, dx, small


def _grad_w_out(merged, dy):
    S, D = merged.shape
    q = D // N_CHIPS

    def body(a_ref, b_ref, o_ref):
        o_ref[...] = _dot_tn(a_ref[...], b_ref[...]).astype(BF)

    return pl.pallas_call(
        body, name="grad_w_out",
        grid=(N_CHIPS,),
        in_specs=[pl.BlockSpec((S, q), lambda j: (0, j)), pl.BlockSpec((S, D), lambda j: (0, 0))],
        out_specs=pl.BlockSpec((None, q, D), lambda j: (j, 0, 0)),
        out_shape=jax.ShapeDtypeStruct((N_CHIPS, q, D), BF),
        compiler_params=_params(dimension_semantics=("parallel",)),
    )(merged, dy)
```

```python
import functools
import math

import numpy as np
import jax
import jax.numpy as jnp
from jax import lax
from jax.experimental import pallas as pl
from jax.experimental.pallas import tpu as pltpu

F32 = jnp.float32
BF = jnp.bfloat16
MESH = pl.DeviceIdType.MESH

DEPTH = 2
D_MODEL = 1024
WIDTH = 512
IN_COLS = 12 * WIDTH + 3 * D_MODEL
N_CHIPS = 4
N_DEV = 8
SB_BLOCK = 128
SB_HEAD_DIM = 64
HG_CHUNK = 128
HG_DIM = 128
LN_EPS = 1e-5
RMS_EPS = 1e-6
ALPHA = (2.0 * DEPTH) ** 0.25
ADAM_LR, ADAM_B1, ADAM_B2, ADAM_EPS, ADAM_WD, ADAM_STEP = 0.001, 0.9, 0.999, 1e-08, 0.01, 10
VMEM_LIMIT = 56 << 20


def _params(**kw):
    return pltpu.CompilerParams(vmem_limit_bytes=VMEM_LIMIT, **kw)


def _dot(a, b):
    return jnp.dot(a, b, preferred_element_type=F32)


def _dot_nt(a, b):
    return lax.dot_general(a, b, (((1,), (1,)), ((), ())), preferred_element_type=F32)


def _dot_tn(a, b):
    return lax.dot_general(a, b, (((0,), (0,)), ((), ())), preferred_element_type=F32)


def _sigmoid(x):
    return 1.0 / (1.0 + jnp.exp(-x))


def _silu(x):
    return x * _sigmoid(x)


def _split_dot(t, g, terms):
    acc = None
    rest = g
    for _ in range(terms):
        part = rest.astype(BF)
        rest = rest - part.astype(F32)
        d = _dot(t, part)
        acc = d if acc is None else acc + d
    return acc


def _standardize(x):
    mu = jnp.mean(x, axis=-1, keepdims=True)
    xc = x - mu
    var = jnp.mean(xc * xc, axis=-1, keepdims=True)
    rstd = lax.rsqrt(var + LN_EPS)
    return xc * rstd, rstd


def _standardize_bwd(dxs, xs, rstd):
    return rstd * (dxs - jnp.mean(dxs, axis=-1, keepdims=True) - xs * jnp.mean(dxs * xs, axis=-1, keepdims=True))


class _Job:
    def __init__(self, ins, outs, sems, make, alias=None):
        self.ins, self.outs, self.sems, self.make = list(ins), list(outs), list(sems), make
        self.alias = dict(alias or {})


def _join_jobs(jobs):
    jobs = [j for j in jobs if j is not None]
    if len(jobs) <= 1:
        return jobs[0] if jobs else None

    def make(ins, outs, sems):
        phases, i, o, s = [], 0, 0, 0
        for j in jobs:
            got = j.make(ins[i:i + len(j.ins)], outs[o:o + len(j.outs)], sems[s:s + len(j.sems)])
            i, o, s = i + len(j.ins), o + len(j.outs), s + len(j.sems)
            for n, phase in enumerate(got):
                if n == len(phases):
                    phases.append([])
                phases[n] += phase
        return phases

    alias, i, o = {}, 0, 0
    for j in jobs:
        alias.update({i + a: o + b for a, b in j.alias.items()})
        i, o = i + len(j.ins), o + len(j.outs)
    return _Job(sum([j.ins for j in jobs], []), sum([j.outs for j in jobs], []), sum([j.sems for j in jobs], []), make, alias)


def _flip(v, bit):
    return 1 - v if bit else v


def _halves(a, front=0):
    shape = a.shape
    lead = math.prod(shape[front:-2])
    return a.reshape(shape[:front] + (lead, 2, shape[-2] // 2, shape[-1]))


def _dma_sems(*shapes):
    return [pltpu.SemaphoreType.DMA(s) for s in shapes]


def _same(arrays):
    return [jax.ShapeDtypeStruct(a.shape, a.dtype) for a in arrays]


def _gather_job(slabs, window=None):
    n = len(slabs)
    cols = slice(None) if window is None else pl.ds(*window)

    def make(ins, outs, sems):
        send1, recv1, send2, recv2 = sems
        mx, my, mc = lax.axis_index("x"), lax.axis_index("y"), lax.axis_index("c")
        fetch, pass_on = [], []
        for a in range(n):
            ours = outs[a].at[2 * mx + my, :, mc, :, cols]
            for k in range(1, N_CHIPS):
                px, py = _flip(mx, k & 2), _flip(my, k & 1)
                fetch.append(pltpu.make_async_remote_copy(
                    src_ref=ours, dst_ref=ours, send_sem=send1.at[a, k - 1], recv_sem=recv1.at[a, k - 1],
                    device_id=(px, py, mc), device_id_type=MESH))
                theirs = outs[a].at[2 * px + py, :, mc, :, cols]
                pass_on.append(pltpu.make_async_remote_copy(
                    src_ref=theirs, dst_ref=theirs, send_sem=send2.at[a, k - 1], recv_sem=recv2.at[a, k - 1],
                    device_id=(mx, my, 1 - mc), device_id_type=MESH))
        return [fetch, pass_on]

    pairs = (n, N_CHIPS - 1)
    return _Job(slabs, _same(slabs), _dma_sems(pairs, pairs, pairs, pairs), make, {a: a for a in range(n)})


def _to_sibling_job(grads):
    n = len(grads)

    def make(ins, outs, sems):
        send_sems, recv_sems = sems
        mx, my, mc = lax.axis_index("x"), lax.axis_index("y"), lax.axis_index("c")
        return [[pltpu.make_async_remote_copy(
            src_ref=ins[a].at[:, :, 1 - mc], dst_ref=outs[a], send_sem=send_sems.at[a], recv_sem=recv_sems.at[a],
            device_id=(mx, my, 1 - mc), device_id_type=MESH) for a in range(n)]]

    outs = [jax.ShapeDtypeStruct(g.shape[:2] + g.shape[3:], g.dtype) for g in grads]
    return _Job(grads, outs, _dma_sems((n,), (n,)), make)


def _scatter_job(sums):
    n = len(sums)

    def make(ins, outs, sems):
        send_sems, recv_sems = sems
        mx, my, mc = lax.axis_index("x"), lax.axis_index("y"), lax.axis_index("c")
        copies = []
        for a in range(n):
            for k in range(1, N_CHIPS):
                px, py = _flip(mx, k & 2), _flip(my, k & 1)
                copies.append(pltpu.make_async_remote_copy(
                    src_ref=ins[a].at[2 * px + py], dst_ref=outs[a].at[k - 1], send_sem=send_sems.at[a, k - 1],
                    recv_sem=recv_sems.at[a, k - 1], device_id=(px, py, mc), device_id_type=MESH))
        return [copies]

    pairs = (n, N_CHIPS - 1)
    return _Job(sums, [jax.ShapeDtypeStruct((N_CHIPS - 1,) + s.shape[1:], s.dtype) for s in sums], _dma_sems(pairs, pairs), make)


def _place_job(wholes):
    n = len(wholes)

    def make(ins, outs, sems):
        send_sems, recv_sems = sems
        mx, my, mc = lax.axis_index("x"), lax.axis_index("y"), lax.axis_index("c")
        copies = []
        for a in range(n):
            here = outs[a].at[:, mc]
            copies.append(pltpu.make_async_remote_copy(src_ref=here, dst_ref=here, send_sem=send_sems.at[a], recv_sem=recv_sems.at[a],
                                                       device_id=(mx, my, 1 - mc), device_id_type=MESH))
        return [copies]

    return _Job(wholes, _same(wholes), _dma_sems((n,), (n,)), make, {a: a for a in range(n)})


def _gather8_job(x):
    def make(ins, outs, sems):
        local_sem, send_sems, recv_sems = sems
        mx, my, mc = lax.axis_index("x"), lax.axis_index("y"), lax.axis_index("c")
        here = outs[0].at[4 * mx + 2 * my + mc]
        copies = [pltpu.make_async_copy(ins[0], here, local_sem.at[0])]
        for k in range(1, N_DEV):
            peer = (_flip(mx, k & 4), _flip(my, k & 2), _flip(mc, k & 1))
            copies.append(pltpu.make_async_remote_copy(src_ref=ins[0], dst_ref=here, send_sem=send_sems.at[k - 1],
                                                       recv_sem=recv_sems.at[k - 1], device_id=peer, device_id_type=MESH))
        return [copies]

    return _Job([x], [jax.ShapeDtypeStruct((N_DEV,) + x.shape, x.dtype)], _dma_sems((1,), (N_DEV - 1,), (N_DEV - 1,)), make)


def _run_phases(phases, first=0):
    for n, phase in enumerate(phases):
        if n >= first:
            for cp in phase:
                cp.start()
        for cp in phase:
            cp.wait()


def _run_job(job, name):
    k_in, k_out = len(job.ins), len(job.outs)

    def body(*refs):
        _run_phases(job.make(refs[:k_in], refs[k_in:k_in + k_out], refs[k_in + k_out:]))

    hbm = pl.BlockSpec(memory_space=pl.ANY)
    return pl.pallas_call(body, name=name, in_specs=[hbm] * k_in, out_specs=[hbm] * k_out, out_shape=job.outs,
                          scratch_shapes=job.sems, input_output_aliases=job.alias, compiler_params=_params())(*job.ins)


def _hosted(body, job, args, *, name, grid, in_specs, out_specs, out_shape, scratch_shapes=(), semantics, aliases=None):
    in_specs, out_specs, out_shape, scratch = list(in_specs), list(out_specs), list(out_shape), list(scratch_shapes)
    aliases = dict(aliases or {})
    if job is None:
        outs = pl.pallas_call(body, name=name, grid=grid, in_specs=in_specs, out_specs=out_specs, out_shape=out_shape,
                              scratch_shapes=scratch, input_output_aliases=aliases,
                              compiler_params=_params(dimension_semantics=semantics))(*args)
        return list(outs), None
    n_in, n_out, n_scr, k_in, k_out = len(in_specs), len(out_specs), len(scratch), len(job.ins), len(job.outs)

    def wrapped(*refs):
        ins, rest = refs[:n_in], refs[n_in:]
        job_ins, rest = rest[:k_in], rest[k_in:]
        outs, rest = rest[:n_out], rest[n_out:]
        job_outs, rest = rest[:k_out], rest[k_out:]
        scr, sems = rest[:n_scr], rest[n_scr:]
        ids = [pl.program_id(a) for a in range(len(grid))]
        first = functools.reduce(jnp.logical_and, [i == 0 for i in ids])
        last = functools.reduce(jnp.logical_and, [i == g - 1 for i, g in zip(ids, grid)])

        @pl.when(first)
        def _():
            for cp in job.make(job_ins, job_outs, sems)[0]:
                cp.start()

        body(*ins, *outs, *scr)

        @pl.when(last)
        def _():
            _run_phases(job.make(job_ins, job_outs, sems), first=1)

    hbm = pl.BlockSpec(memory_space=pl.ANY)
    outs = pl.pallas_call(
        wrapped, name=name, grid=grid, in_specs=in_specs + [hbm] * k_in, out_specs=out_specs + [hbm] * k_out,
        out_shape=out_shape + job.outs, scratch_shapes=scratch + job.sems,
        input_output_aliases={**aliases, **{n_in + i: n_out + o for i, o in job.alias.items()}},
        compiler_params=_params(dimension_semantics=("arbitrary",) * len(grid)))(*args, *job.ins)
    return list(outs[:n_out]), list(outs[n_out:])


def _proj_fwd(x, mod, wg, job=None, chunk=(0, 1), tm=512):
    S, D = x.shape
    tm = min(tm, S)
    tn = wg.shape[-1]
    c, n = chunk

    def body(x_ref, mod_ref, w_ref, proj_ref, h_ref, hs):
        @pl.when(pl.program_id(1) == 0)
        def _():
            xs, _ = _standardize(x_ref[...])
            h = xs * (1.0 + mod_ref[:, D:2 * D]) + mod_ref[:, 0:D]
            hb = h.astype(BF)
            hs[...] = hb
            h_ref[...] = hb

        proj_ref[...] = _dot(hs[...], w_ref[...])

    return _hosted(
        body, job, (x, mod, wg), name="proj_fwd",
        grid=(S // tm, N_CHIPS),
        in_specs=[pl.BlockSpec((tm, D), lambda i, j: (i, 0)),
                  pl.BlockSpec((1, 3 * D), lambda i, j: (0, 0)),
                  pl.BlockSpec((None, D, tn), lambda i, j: (j, 0, 0))],
        out_specs=[pl.BlockSpec((tm, tn), lambda i, j: (i, j * n + c)),
                   pl.BlockSpec((tm, D), lambda i, j: (i, 0))],
        out_shape=[jax.ShapeDtypeStruct((S, IN_COLS), F32), jax.ShapeDtypeStruct((S, D), BF)],
        scratch_shapes=[pltpu.VMEM((tm, D), BF)],
        semantics=("parallel", "arbitrary"))


def _proj_cols(h, wg, chunk, proj, job=None, tm=512):
    S, D = h.shape
    tm = min(tm, S)
    tn = wg.shape[-1]
    c, n = chunk

    def body(h_ref, w_ref, prev_ref, proj_ref):
        proj_ref[...] = _dot(h_ref[...], w_ref[...])

    in_specs = [pl.BlockSpec((tm, D), lambda i, j: (i, 0)), pl.BlockSpec((None, D, tn), lambda i, j: (j, 0, 0)),
                pl.BlockSpec(memory_space=pl.ANY)]
    return _hosted(body, job, (h, wg, proj), name="proj_cols", grid=(S // tm, N_CHIPS), in_specs=in_specs,
                   out_specs=[pl.BlockSpec((tm, tn), lambda i, j: (i, j * n + c))],
                   out_shape=[jax.ShapeDtypeStruct(proj.shape, proj.dtype)],
                   semantics=("parallel", "arbitrary"), aliases={2: 0})


SB_ROWS = 256
SB_KEYS = 256


def _attn_consts():
    j = np.arange(SB_KEYS)[:, None]
    s = np.arange(SB_KEYS)[None, :]
    from_here = np.concatenate([(j >= s), (j >= s)], axis=0).astype(np.float32)
    up_to = np.concatenate([(j <= s), (j <= s)], axis=0).astype(np.float32)
    return jnp.asarray(from_here, BF), jnp.asarray(up_to, BF)


def _hi_lo(x):
    hi = lax.bitcast_convert_type(lax.bitcast_convert_type(x, jnp.uint32) & jnp.uint32(0xFFFF0000), F32)
    return hi.astype(BF), (x - hi).astype(BF)


def _sums_r(x, t2):
    hi, lo = _hi_lo(x)
    return _dot(jnp.concatenate([hi, lo], axis=1), t2)


def _all_lanes(col, lanes):
    return jnp.broadcast_to(col, (col.shape[0], lanes))


def _attn_rows(ref, r0, rows, lanes, head0, scale=None):
    v = ref[pl.ds(r0, rows), lanes]
    if scale is not None:
        v = v * scale
    return jnp.concatenate([jnp.where(head0, v, 0.0), jnp.where(head0, 0.0, v)], axis=0).astype(BF)


SB_PAIRS_FWD = 4
SB_PAIRS_BWD = 2


def _attn_specs(S, n_pairs):
    return lambda base: pl.BlockSpec((S, n_pairs * SB_BLOCK), lambda p, base=base: (0, base // n_pairs + p))


def _attn_scores(q2n, k_ref, lanes, kj, t2, from_here_ref, masked):
    c0 = pl.multiple_of(kj * SB_KEYS, SB_KEYS)
    kb = k_ref[pl.ds(c0, SB_KEYS), lanes].astype(BF)
    zn = _dot_nt(q2n, kb)
    lsb = jnp.minimum(zn, 0.0) - jnp.log(1.0 + jnp.exp(-jnp.abs(zn)))
    valid = None
    if masked:
        valid = (lax.broadcasted_iota(jnp.int32, zn.shape, 1) + kj * SB_KEYS) < t2
        lsb = jnp.where(valid, lsb, 0.0)
    return c0, kb, zn, valid, lsb, _sums_r(lsb, from_here_ref[...])


def _attn_fwd(proj, from_here, job=None):
    S = proj.shape[0]
    TQ = SB_ROWS
    assert S % TQ == 0 and SB_KEYS == TQ
    scale = SB_HEAD_DIM ** -0.5
    n_pairs = SB_PAIRS_FWD
    pairs = range(n_pairs)
    lanes = [pl.ds(p * SB_BLOCK, SB_BLOCK) for p in pairs]

    def body(q_ref, k_ref, v_ref, from_here_ref, o_ref, tot_ref, run, acc):
        head0 = lax.broadcasted_iota(jnp.int32, (1, 2 * SB_HEAD_DIM), 1) < SB_HEAD_DIM

        def qloop(qi, _):
            r0 = pl.multiple_of(qi * TQ, TQ)
            q2n = [_attn_rows(q_ref, r0, TQ, lanes[p], head0, -scale) for p in pairs]
            trow = lax.broadcasted_iota(jnp.int32, (TQ, SB_KEYS), 0) + qi * TQ
            t2 = jnp.concatenate([trow, trow], axis=0)
            run[...] = jnp.zeros_like(run)
            acc[...] = jnp.zeros_like(acc)

            def step(kj, masked):
                got = [_attn_scores(q2n[p], k_ref, lanes[p], kj, t2, from_here_ref, masked) for p in pairs]
                for p in pairs:
                    c0, _, zn, valid, _, sums = got[p]
                    r = run[p]
                    e = sums - zn + jnp.concatenate([r, r], axis=1)
                    if masked:
                        e = jnp.where(valid, e, -jnp.inf)
                    acc[p] += _dot(jnp.exp(e).astype(BF), v_ref[pl.ds(c0, SB_KEYS), lanes[p]].astype(BF))
                    run[p] = r + _all_lanes(sums[:, 0:1], SB_BLOCK)

            step(qi, True)

            def below(n, _):
                step(qi - 1 - n, False)
                return 0

            lax.fori_loop(0, qi, below, 0)
            for p in pairs:
                o_ref[pl.ds(r0, TQ), lanes[p]] = jnp.where(head0, acc[p, 0:TQ, :], acc[p, TQ:2 * TQ, :])
                tot_ref[p, 0, pl.ds(r0, TQ), :] = run[p, 0:TQ, :]
                tot_ref[p, 1, pl.ds(r0, TQ), :] = run[p, TQ:2 * TQ, :]
            return 0

        lax.fori_loop(0, S // TQ, qloop, 0)

    col = _attn_specs(S, n_pairs)
    state = pltpu.VMEM((n_pairs, 2 * TQ, SB_BLOCK), F32)
    return _hosted(
        body, job, (proj, proj, proj, from_here), name="attn_fwd",
        grid=(WIDTH // (n_pairs * SB_BLOCK),),
        in_specs=[col(0), col(4), col(8), pl.BlockSpec(from_here.shape, lambda p: (0, 0))],
        out_specs=[col(0), pl.BlockSpec((n_pairs, 2, S, 128), lambda p: (p, 0, 0, 0))],
        out_shape=[jax.ShapeDtypeStruct((S, WIDTH), F32), jax.ShapeDtypeStruct((WIDTH // 128, 2, S, 128), F32)],
        scratch_shapes=[state, state],
        semantics=("parallel",))


def _attn_bwd(proj, d_o, tot, from_here, up_to, dproj, job=None):
    S = proj.shape[0]
    TQ = SB_ROWS
    assert S % TQ == 0 and SB_KEYS == TQ
    scale = SB_HEAD_DIM ** -0.5
    n_pairs = SB_PAIRS_BWD
    pairs = range(n_pairs)
    lanes = [pl.ds(p * SB_BLOCK, SB_BLOCK) for p in pairs]

    def body(q_ref, k_ref, v_ref, do_ref, tot_ref, from_here_ref, up_to_ref, _, dproj_ref, pre, cum, dq_acc, dk_acc, dv_acc, stage, sems):
        dq_ref, dk_ref, dv_ref = stage.at[0], stage.at[1], stage.at[2]
        head0 = lax.broadcasted_iota(jnp.int32, (1, 2 * SB_HEAD_DIM), 1) < SB_HEAD_DIM
        dk_acc[...] = jnp.zeros_like(dk_acc)
        dv_acc[...] = jnp.zeros_like(dv_acc)

        def qloop(qi, _):
            r0 = pl.multiple_of(qi * TQ, TQ)
            q2n = [_attn_rows(q_ref, r0, TQ, lanes[p], head0, -scale) for p in pairs]
            do2 = [_attn_rows(do_ref, r0, TQ, lanes[p], head0) for p in pairs]
            trow = lax.broadcasted_iota(jnp.int32, (TQ, SB_KEYS), 0) + qi * TQ
            t2 = jnp.concatenate([trow, trow], axis=0)
            for p in pairs:
                pre[p, 0:TQ, :] = tot_ref[p, 0, pl.ds(r0, TQ), :]
                pre[p, TQ:2 * TQ, :] = tot_ref[p, 1, pl.ds(r0, TQ), :]
            cum[...] = jnp.zeros_like(cum)
            dq_acc[...] = jnp.zeros_like(dq_acc)

            def step(kj, masked):
                got = [_attn_scores(q2n[p], k_ref, lanes[p], kj, t2, from_here_ref, masked) for p in pairs]
                for p in pairs:
                    c0, kb, zn, valid, lsb, sums = got[p]
                    later = pre[p] - _all_lanes(sums[:, 0:1], SB_BLOCK)
                    pre[p] = later
                    e = sums - zn + jnp.concatenate([later, later], axis=1)
                    sig = jnp.exp(lsb - zn)
                    if masked:
                        e = jnp.where(valid, e, -jnp.inf)
                        sig = jnp.where(valid, sig, 0.0)
                    a = jnp.exp(e)
                    w = _dot_nt(do2[p], v_ref[pl.ds(c0, SB_KEYS), lanes[p]].astype(BF)) * a
                    upto = _sums_r(w, up_to_ref[...])
                    c = cum[p]
                    dz = w - sig * (upto + jnp.concatenate([c, c], axis=1))
                    cum[p] = c + _all_lanes(upto[:, SB_KEYS - 1:SB_KEYS], SB_BLOCK)
                    dzb = dz.astype(BF)
                    dq_acc[p] += _dot(dzb, kb)
                    dk_acc[pl.ds(c0, SB_KEYS), lanes[p]] += _dot_tn(dzb, q2n[p])
                    dv_acc[pl.ds(c0, SB_KEYS), lanes[p]] += _dot_tn(a.astype(BF), do2[p])

            def below(kj, _):
                step(kj, False)
                return 0

            lax.fori_loop(0, qi, below, 0)
            step(qi, True)
            for p in pairs:
                dq_ref[pl.ds(r0, TQ), lanes[p]] = (jnp.where(head0, dq_acc[p, 0:TQ, :], dq_acc[p, TQ:2 * TQ, :]) * scale).astype(BF)
            return 0

        lax.fori_loop(0, S // TQ, qloop, 0)
        dk_ref[...] = (-dk_acc[...]).astype(BF)
        dv_ref[...] = dv_acc[...].astype(BF)
        lane0 = pl.program_id(0) * (n_pairs * SB_BLOCK)
        _put_columns([dq_ref, dk_ref, dv_ref], dproj_ref, [WIDTH * n + lane0 for n in range(3)], sems)

    col = _attn_specs(S, n_pairs)
    whole = lambda a: pl.BlockSpec(a.shape, lambda p: (0, 0))
    hbm = pl.BlockSpec(memory_space=pl.ANY)
    state = pltpu.VMEM((n_pairs, 2 * TQ, SB_BLOCK), F32)
    grads = pltpu.VMEM((S, n_pairs * SB_BLOCK), F32)
    return _hosted(
        body, job, (proj, proj, proj, d_o, tot, from_here, up_to, dproj), name="attn_bwd",
        grid=(WIDTH // (n_pairs * SB_BLOCK),),
        in_specs=[col(0), col(4), col(8), col(0), pl.BlockSpec((n_pairs, 2, S, 128), lambda p: (p, 0, 0, 0)), whole(from_here),
                  whole(up_to), hbm],
        out_specs=[hbm],
        out_shape=[jax.ShapeDtypeStruct(dproj.shape, dproj.dtype)],
        scratch_shapes=[state, state, state, grads, grads, pltpu.VMEM((3, S, n_pairs * SB_BLOCK), BF), pltpu.SemaphoreType.DMA((3,))],
        semantics=("arbitrary",), aliases={7: 0})


HG_LEVELS = tuple(HG_CHUNK >> n for n in range(1, HG_CHUNK.bit_length()))


def _hgrn_consts():
    C = HG_CHUNK
    t = np.arange(C)[:, None]
    s = np.arange(C)[None, :]
    rows = [(s <= t), (s > t)]
    masks = [(t == s)]
    for m in HG_LEVELS:
        two = 2 * m
        mid = (t // two) * two + m
        right = (t % two) >= m
        rows.append((right & (s >= mid) & (s <= t)) | ((~right) & (s > t) & (s <= mid - 1)))
        masks.append(((t // two) == (s // two)) & right & ((s % two) < m))
    tri = np.concatenate(rows, axis=0).astype(np.float32)
    twice = lambda a: jnp.asarray(np.concatenate([a, a], axis=1), BF)
    return (twice(tri), twice(tri.T), jnp.asarray(np.stack(masks).astype(np.float32), F32))


HG_SUM_BLOCKS = 2 + len(HG_LEVELS)


def _split_rows(g):
    hi = g.astype(BF)
    return jnp.concatenate([hi, (g - hi.astype(F32)).astype(BF)], axis=0)


def _hgrn_sum_blocks(e):
    C = HG_CHUNK
    blocks = tuple(e[n * C:(n + 1) * C] for n in range(HG_SUM_BLOCKS))
    return blocks + (jnp.broadcast_to(e[C - 1:C], (HG_DIM, e.shape[1])),)


@jax.custom_vjp
def _hgrn_sums(tri, tri_t, g):
    return _hgrn_sum_blocks(_split_dot(tri[:, :HG_CHUNK], g, 2))


def _hgrn_sums_fwd(tri, tri_t, g):
    return _hgrn_sum_blocks(_dot(tri, _split_rows(g))), (tri, tri_t)


def _hgrn_sums_bwd(res, ds):
    tri, tri_t = res
    C = HG_CHUNK
    last = lax.broadcasted_iota(jnp.int32, (C, 1), 0) == C - 1
    prefix = ds[0] + jnp.where(last, jnp.sum(ds[-1], axis=0, keepdims=True), 0.0)
    d = jnp.concatenate((prefix,) + tuple(ds[1:-1]), axis=0)
    return jnp.zeros_like(tri), jnp.zeros_like(tri_t), _dot(tri_t, _split_rows(d))


_hgrn_sums.defvjp(_hgrn_sums_fwd, _hgrn_sums_bwd)


def _bf_dot(a, b):
    return _dot(a.astype(BF), b.astype(BF))


def _bf_dot_nt(a, b):
    return _dot_nt(a.astype(BF), b.astype(BF))


def _bf_dot_tn(a, b):
    return _dot_tn(a.astype(BF), b.astype(BF))


@jax.custom_vjp
def _mm(a, b):
    return _bf_dot(a, b)


_mm.defvjp(lambda a, b: (_bf_dot(a, b), (a, b)), lambda r, ct: (_bf_dot_nt(ct, r[1]), _bf_dot_tn(r[0], ct)))


@jax.custom_vjp
def _mm_nt(a, b):
    return _bf_dot_nt(a, b)


_mm_nt.defvjp(lambda a, b: (_bf_dot_nt(a, b), (a, b)), lambda r, ct: (_bf_dot(ct, r[1]), _bf_dot_tn(ct, r[0])))


@jax.custom_vjp
def _mm_tn(a, b):
    return _bf_dot_tn(a, b)


_mm_tn.defvjp(lambda a, b: (_bf_dot_tn(a, b), (a, b)), lambda r, ct: (_bf_dot_nt(r[1], ct), _bf_dot(r[0], ct)))


def _hgrn_chunk(tri, tri_t, masks, qraw, fpre, v, st, lb):
    q = _silu(qraw)
    f = lb + (1.0 - lb) * _sigmoid(fpre)
    k = 1.0 - f
    e = _hgrn_sums(tri, tri_t, jnp.log(f))
    prefix, suffix, whole = e[0], e[1], e[-1]
    scores = masks[0] * _mm_nt(q, k)
    for n in range(len(HG_LEVELS)):
        decay = jnp.exp(e[2 + n])
        scores = scores + masks[n + 1] * _mm_nt(q * decay, k * decay)
    o = _mm_nt(q * jnp.exp(prefix), st) + _mm(scores, v)
    st_new = st * jnp.exp(whole) + _mm_tn(v, k * jnp.exp(suffix))
    return o, st_new


HG_HEADS_PER_STEP = 4
HG_LANES = HG_HEADS_PER_STEP * HG_DIM


def _hgrn_specs(S, consts):
    col = lambda base: pl.BlockSpec((S, HG_LANES), lambda p, base=base: (0, base // HG_HEADS_PER_STEP + p))
    whole = [pl.BlockSpec(a.shape, lambda p, n=a.ndim: (0,) * n) for a in consts]
    return col, whole


def _hgrn_fwd(proj, lbs, consts, job=None):
    S = proj.shape[0]
    nc = S // HG_CHUNK
    heads = range(HG_HEADS_PER_STEP)

    def body(q_ref, f_ref, i_ref, lb_ref, tri_ref, trit_ref, mask_ref, o_ref, st_ref):
        tri, tri_t = tri_ref[...], trit_ref[...]
        masks = [mask_ref[n] for n in range(len(HG_LEVELS) + 1)]

        def chunk(ci, sts):
            r0 = pl.multiple_of(ci * HG_CHUNK, HG_CHUNK)
            rows = pl.ds(r0, HG_CHUNK)
            new = []
            for hd in heads:
                lanes = pl.ds(hd * HG_DIM, HG_DIM)
                st_ref[hd, ci] = sts[hd]
                o, st_new = _hgrn_chunk(tri, tri_t, masks, q_ref[rows, lanes], f_ref[rows, lanes], i_ref[rows, lanes], sts[hd], lb_ref[hd])
                o_ref[rows, lanes] = o
                new.append(st_new)
            return tuple(new)

        lax.fori_loop(0, nc, chunk, tuple(jnp.zeros((HG_DIM, HG_DIM), F32) for _ in heads))

    col, whole = _hgrn_specs(S, consts)
    return _hosted(
        body, job, (proj, proj, proj, lbs, *consts), name="hgrn_fwd",
        grid=(WIDTH // HG_LANES,),
        in_specs=[col(16), col(20), col(24), pl.BlockSpec((HG_HEADS_PER_STEP, 1, 128), lambda p: (p, 0, 0))] + whole,
        out_specs=[col(0), pl.BlockSpec((HG_HEADS_PER_STEP, nc, HG_DIM, HG_DIM), lambda p: (p, 0, 0, 0))],
        out_shape=[jax.ShapeDtypeStruct((S, WIDTH), F32), jax.ShapeDtypeStruct((WIDTH // 128, nc, HG_DIM, HG_DIM), F32)],
        semantics=("parallel",))


def _hgrn_bwd(proj, lbs, states, d_o, consts, dproj, job=None):
    S = proj.shape[0]
    nc = S // HG_CHUNK

    def body(q_ref, f_ref, i_ref, lb_ref, st_ref, do_ref, tri_ref, trit_ref, mask_ref, _, dlb_ref, dproj_ref, stage, sems):
        dq_ref, df_ref, di_ref = stage.at[0], stage.at[1], stage.at[2]
        masks = [mask_ref[n] for n in range(len(HG_LEVELS) + 1)]
        fn = functools.partial(_hgrn_chunk, tri_ref[...], trit_ref[...], masks)
        heads = range(HG_HEADS_PER_STEP)

        def chunk(n, carry):
            ci = nc - 1 - n
            r0 = pl.multiple_of(ci * HG_CHUNK, HG_CHUNK)
            rows = pl.ds(r0, HG_CHUNK)
            new = []
            for hd in heads:
                d_st, dlb = carry[hd]
                lanes = pl.ds(hd * HG_DIM, HG_DIM)
                _, pull = jax.vjp(fn, q_ref[rows, lanes], f_ref[rows, lanes], i_ref[rows, lanes], st_ref[hd, ci], lb_ref[hd])
                dq, df, di, d_prev, dl = pull((do_ref[rows, lanes], d_st))
                dq_ref[rows, lanes] = dq.astype(BF)
                df_ref[rows, lanes] = df.astype(BF)
                di_ref[rows, lanes] = di.astype(BF)
                new.append((d_prev, dlb + dl))
            return tuple(new)

        zero = (jnp.zeros((HG_DIM, HG_DIM), F32), jnp.zeros((1, HG_DIM), F32))
        done = lax.fori_loop(0, nc, chunk, tuple(zero for _ in heads))
        for hd in heads:
            dlb_ref[hd] = done[hd][1]
        lane0 = pl.program_id(0) * HG_LANES
        _put_columns([dq_ref, df_ref, di_ref], dproj_ref, [WIDTH * n + lane0 for n in (4, 5, 6)], sems)

    col, whole = _hgrn_specs(S, consts)
    head = pl.BlockSpec((HG_HEADS_PER_STEP, 1, 128), lambda p: (p, 0, 0))
    hbm = pl.BlockSpec(memory_space=pl.ANY)
    n_in = 6 + len(consts)
    return _hosted(
        body, job, (proj, proj, proj, lbs, states, d_o, *consts, dproj), name="hgrn_bwd",
        grid=(WIDTH // HG_LANES,),
        in_specs=[col(16), col(20), col(24), head, pl.BlockSpec((HG_HEADS_PER_STEP, nc, HG_DIM, HG_DIM), lambda p: (p, 0, 0, 0)), col(0)]
        + whole + [hbm],
        out_specs=[head, hbm],
        out_shape=[jax.ShapeDtypeStruct((WIDTH // 128, 1, 128), F32), jax.ShapeDtypeStruct(dproj.shape, dproj.dtype)],
        scratch_shapes=[pltpu.VMEM((3, S, HG_LANES), BF), pltpu.SemaphoreType.DMA((3,))],
        semantics=("arbitrary",), aliases={n_in: 1})


def _shift_down(x, n):
    rows = lax.broadcasted_iota(jnp.int32, x.shape, 0)
    return jnp.where(rows >= n, pltpu.roll(x, n, 0), 0.0)


def _shift_up(x, n):
    S = x.shape[0]
    rows = lax.broadcasted_iota(jnp.int32, x.shape, 0)
    return jnp.where(rows < S - n, pltpu.roll(x, S - n, 0), 0.0)


def _branch_fwd(proj, o_a, o_b, norm_w, conv_w, layer):
    S = proj.shape[0]

    def body(oa_ref, za_ref, ob_ref, zb_ref, nw_ref, pre_ref, post_ref, u_ref, zc_ref, cw_ref, ya_ref, yb_ref, yc_ref):
        ya_ref[...] = (oa_ref[...] * _silu(za_ref[...])).astype(BF)
        ob = ob_ref[...]
        rn = lax.rsqrt(jnp.mean(ob * ob, axis=-1, keepdims=True) + RMS_EPS)
        yb_ref[...] = (ob * rn * nw_ref[layer:layer + 1, :] * _silu(zb_ref[...])).astype(BF)
        pu = pre_ref[...] * u_ref[...]
        conv = cw_ref[2:3, :] * pu + cw_ref[1:2, :] * _shift_down(pu, 1) + cw_ref[0:1, :] * _shift_down(pu, 2)
        yc_ref[...] = (post_ref[...] * conv * _silu(zc_ref[...])).astype(BF)

    col = lambda base: pl.BlockSpec((S, 128), lambda p, base=base: (0, base + p))
    out = jax.ShapeDtypeStruct((S, WIDTH), BF)
    return pl.pallas_call(
        body, name="branch_fwd",
        grid=(WIDTH // 128,),
        in_specs=[col(0), col(12), col(0), col(28), pl.BlockSpec(norm_w.shape, lambda p: (0, 0)),
                  col(32), col(36), col(40), col(44), pl.BlockSpec((None, None, 3, 128), lambda p: (p, layer, 0, 0))],
        out_specs=[col(0), col(0), col(0)],
        out_shape=[out, out, out],
        compiler_params=_params(dimension_semantics=("parallel",)),
    )(o_a, proj, o_b, proj, norm_w, proj, proj, proj, proj, conv_w)


def _put_columns(tiles, dproj_ref, firsts, sems):
    copies = [pltpu.make_async_copy(t, dproj_ref.at[:, pl.ds(pl.multiple_of(c, 128), t.shape[1])], sems.at[n])
              for n, (t, c) in enumerate(zip(tiles, firsts))]
    for cp in copies:
        cp.start()
    for cp in copies:
        cp.wait()


def _branch_bwd(proj, o_a, o_b, norm_w, conv_w, dy_a, dy_b, dy_c, dproj, layer):
    S = proj.shape[0]
    firsts = [WIDTH * n for n in (3, 7, 8, 9, 10, 11)]

    def dsilu(z):
        s = _sigmoid(z)
        return s * z, s * (1.0 + z * (1.0 - s))

    def body(oa_ref, za_ref, ob_ref, zb_ref, nw_ref, pre_ref, post_ref, u_ref, zc_ref, cw_ref, dya_ref, dyb_ref, dyc_ref, _,
             doa_ref, dob_ref, dnw_ref, dcw_ref, dproj_ref, stage, sems):
        dza_ref, dzb_ref, dpre_ref, dpost_ref, du_ref, dzc_ref = [stage.at[n] for n in range(6)]
        dya = dya_ref[...]
        sa, dsa = dsilu(za_ref[...])
        doa_ref[...] = dya * sa
        dza_ref[...] = (dya * oa_ref[...] * dsa).astype(BF)

        dyb = dyb_ref[...]
        ob = ob_ref[...]
        nw = nw_ref[layer:layer + 1, :]
        sb, dsb = dsilu(zb_ref[...])
        rn = lax.rsqrt(jnp.mean(ob * ob, axis=-1, keepdims=True) + RMS_EPS)
        on = ob * rn
        dzb_ref[...] = (dyb * on * nw * dsb).astype(BF)
        don_w = dyb * sb
        dnw_ref[...] = jnp.sum(don_w * on, axis=0, keepdims=True)
        don = don_w * nw
        dob_ref[...] = rn * (don - on * jnp.mean(don * on, axis=-1, keepdims=True))

        dyc = dyc_ref[...]
        pre, post, u = pre_ref[...], post_ref[...], u_ref[...]
        sc, dsc = dsilu(zc_ref[...])
        pu = pre * u
        pu1, pu2 = _shift_down(pu, 1), _shift_down(pu, 2)
        conv = cw_ref[2:3, :] * pu + cw_ref[1:2, :] * pu1 + cw_ref[0:1, :] * pu2
        dzc_ref[...] = (dyc * post * conv * dsc).astype(BF)
        dpost_ref[...] = (dyc * conv * sc).astype(BF)
        dconv = dyc * post * sc
        dcw_ref[0:1, :] = jnp.sum(dconv * pu2, axis=0, keepdims=True)
        dcw_ref[1:2, :] = jnp.sum(dconv * pu1, axis=0, keepdims=True)
        dcw_ref[2:3, :] = jnp.sum(dconv * pu, axis=0, keepdims=True)
        dpu = cw_ref[2:3, :] * dconv + cw_ref[1:2, :] * _shift_up(dconv, 1) + cw_ref[0:1, :] * _shift_up(dconv, 2)
        dpre_ref[...] = (dpu * u).astype(BF)
        du_ref[...] = (dpu * pre).astype(BF)
        lane0 = pl.program_id(0) * 128
        _put_columns([stage.at[n] for n in range(6)], dproj_ref, [c + lane0 for c in firsts], sems)

    col = lambda base: pl.BlockSpec((S, 128), lambda p, base=base: (0, base + p))
    f32 = jax.ShapeDtypeStruct((S, WIDTH), F32)
    hbm = pl.BlockSpec(memory_space=pl.ANY)
    return pl.pallas_call(
        body, name="branch_bwd",
        grid=(WIDTH // 128,),
        in_specs=[col(0), col(12), col(0), col(28), pl.BlockSpec(norm_w.shape, lambda p: (0, 0)),
                  col(32), col(36), col(40), col(44), pl.BlockSpec((None, None, 3, 128), lambda p: (p, layer, 0, 0)),
                  col(0), col(0), col(0), hbm],
        out_specs=[col(0), col(0), pl.BlockSpec((None, 1, 128), lambda p: (p, 0, 0)), pl.BlockSpec((None, 3, 128), lambda p: (p, 0, 0)), hbm],
        out_shape=[f32, f32, jax.ShapeDtypeStruct((WIDTH // 128, 1, 128), F32), jax.ShapeDtypeStruct((WIDTH // 128, 3, 128), F32),
                   jax.ShapeDtypeStruct(dproj.shape, dproj.dtype)],
        scratch_shapes=[pltpu.VMEM((6, S, 128), BF), pltpu.SemaphoreType.DMA((6,))],
        input_output_aliases={13: 4},
        compiler_params=_params(dimension_semantics=("arbitrary",)),
    )(o_a, proj, o_b, proj, norm_w, proj, proj, proj, proj, conv_w, dy_a, dy_b, dy_c, dproj)


def _branch_proj(y_refs, wb_ref):
    out = []
    for i in range(3):
        yv = y_refs[i][...]
        out.append(jnp.concatenate([_dot(yv, wb_ref[j, i]) for j in range(N_CHIPS)], axis=1))
    return out


def _merge_fwd(x, mod, proj, ys, wb, wo, ln_g, ln_b, layer, job=None, tm=256):
    S, D = x.shape
    tm = min(tm, S)

    def body(x_ref, mod_ref, ga_ref, gb_ref, gc_ref, ya_ref, yb_ref, yc_ref, wb_ref, wo_ref, g_ref, b_ref, xo_ref, mg_ref, y_ref):
        ps = _branch_proj((ya_ref, yb_ref, yc_ref), wb_ref)
        merged = _sigmoid(ga_ref[...]) * ps[0] + _sigmoid(gb_ref[...]) * ps[1] + _sigmoid(gc_ref[...]) * ps[2]
        mb = merged.astype(BF)
        mg_ref[...] = mb
        y = _dot(mb, wo_ref[...].reshape(D, D))
        y_ref[...] = y
        r = ALPHA * x_ref[...] + (1.0 + mod_ref[:, 2 * D:3 * D]) * y
        xn, _ = _standardize(r)
        xo_ref[...] = xn * g_ref[layer:layer + 1, :] + b_ref[layer:layer + 1, :]

    row = lambda w, c=0: pl.BlockSpec((tm, w), lambda i, c=c: (i, c))
    whole = lambda a: pl.BlockSpec(a.shape, lambda i, n=a.ndim: (0,) * n)
    return _hosted(
        body, job, (x, mod, proj, proj, proj, *ys, wb, wo, ln_g, ln_b), name="merge_fwd",
        grid=(S // tm,),
        in_specs=[row(D), whole(mod), row(D, 6), row(D, 7), row(D, 8), row(WIDTH), row(WIDTH), row(WIDTH), whole(wb),
                  whole(wo), whole(ln_g), whole(ln_b)],
        out_specs=[row(D), row(D), row(D)],
        out_shape=[jax.ShapeDtypeStruct((S, D), F32), jax.ShapeDtypeStruct((S, D), BF), jax.ShapeDtypeStruct((S, D), F32)],
        semantics=("parallel",))


def _merge_bwd(dxo, x, y, mod, proj, ys, wb, wo, ln_g, layer, tm=256):
    S, D = x.shape
    tm = min(tm, S)

    def body(dxo_ref, x_ref, y_ref, mod_ref, ga_ref, gb_ref, gc_ref, ya_ref, yb_ref, yc_ref, wb_ref, wo_ref, g_ref,
             dxr_ref, dy_ref, dp_ref, dg_ref, dya_ref, dyb_ref, dyc_ref, dlg_ref, dlb_ref, dgt_ref):
        @pl.when(pl.program_id(0) == 0)
        def _():
            dlg_ref[...] = jnp.zeros_like(dlg_ref)
            dlb_ref[...] = jnp.zeros_like(dlb_ref)
            dgt_ref[...] = jnp.zeros_like(dgt_ref)

        gate1 = 1.0 + mod_ref[:, 2 * D:3 * D]
        yv = y_ref[...]
        xn, rstd = _standardize(ALPHA * x_ref[...] + gate1 * yv)
        dxo = dxo_ref[...]
        dlg_ref[...] += jnp.sum(dxo * xn, axis=0, keepdims=True)
        dlb_ref[...] += jnp.sum(dxo, axis=0, keepdims=True)
        dr = _standardize_bwd(dxo * g_ref[layer:layer + 1, :], xn, rstd)
        dxr_ref[...] = ALPHA * dr
        dgt_ref[...] += jnp.sum(dr * yv, axis=0, keepdims=True)
        dyb = (gate1 * dr).astype(BF)
        dy_ref[...] = dyb
        dmerged = _dot_nt(dyb, wo_ref[...].reshape(D, D))
        ps = _branch_proj((ya_ref, yb_ref, yc_ref), wb_ref)
        quarter = D // N_CHIPS
        for i, (gate_ref, out_ref) in enumerate(((ga_ref, dya_ref), (gb_ref, dyb_ref), (gc_ref, dyc_ref))):
            sg = _sigmoid(gate_ref[...])
            dg_ref[:, i * D:(i + 1) * D] = (dmerged * ps[i] * sg * (1.0 - sg)).astype(BF)
            dp = (dmerged * sg).astype(BF)
            dp_ref[:, i * D:(i + 1) * D] = dp
            acc = None
            for j in range(N_CHIPS):
                t = _dot_nt(dp[:, j * quarter:(j + 1) * quarter], wb_ref[j, i])
                acc = t if acc is None else acc + t
            out_ref[...] = acc

    row = lambda w, c=0: pl.BlockSpec((tm, w), lambda i, c=c: (i, c))
    whole = lambda a: pl.BlockSpec(a.shape, lambda i, n=a.ndim: (0,) * n)
    vec = pl.BlockSpec((1, D), lambda i: (0, 0))
    sd = jax.ShapeDtypeStruct
    return pl.pallas_call(
        body, name="merge_bwd",
        grid=(S // tm,),
        in_specs=[row(D), row(D), row(D), whole(mod), row(D, 6), row(D, 7), row(D, 8), row(WIDTH), row(WIDTH), row(WIDTH), whole(wb),
                  whole(wo), whole(ln_g)],
        out_specs=[row(D), row(D), row(3 * D), row(3 * D, IN_COLS // (3 * D) - 1), row(WIDTH), row(WIDTH), row(WIDTH), vec, vec, vec],
        out_shape=[sd((S, D), F32), sd((S, D), BF), sd((S, 3 * D), BF), sd((S, IN_COLS), BF), sd((S, WIDTH), F32), sd((S, WIDTH), F32),
                   sd((S, WIDTH), F32), sd((1, D), F32), sd((1, D), F32), sd((1, D), F32)],
        compiler_params=_params(dimension_semantics=("arbitrary",)),
    )(dxo, x, y, mod, proj, proj, proj, *ys, wb, wo, ln_g)


def _loss_head(x, target, tm=512):
    S, D = x.shape
    tm = min(tm, S)

    def body(x_ref, t_ref, dx_ref, loss_ref):
        @pl.when(pl.program_id(0) == 0)
        def _():
            loss_ref[...] = jnp.zeros_like(loss_ref)

        err = x_ref[...] - t_ref[...]
        dx_ref[...] = err * (1.0 / D)
        loss_ref[...] += 0.5 * jnp.sum(jnp.mean(err * err, axis=-1, keepdims=True))

    row = pl.BlockSpec((tm, D), lambda i: (i, 0))
    return pl.pallas_call(
        body, name="loss_head",
        grid=(S // tm,),
        in_specs=[row, row],
        out_specs=[row, pl.BlockSpec((8, 128), lambda i: (0, 0))],
        out_shape=[jax.ShapeDtypeStruct((S, D), F32), jax.ShapeDtypeStruct((8, 128), F32)],
        compiler_params=_params(dimension_semantics=("arbitrary",)),
    )(x, target)


def _proj_bwd(dproj, wgs, x, mod, dx_res, job=None, tm=512, tk=768):
    S, D = x.shape
    tm = min(tm, S)
    n = len(wgs)
    w_args = list(wgs)
    if n == 1:
        per = wgs[0].shape[-1] // tk
        w_specs = [pl.BlockSpec((None, D, tk), lambda i, k: (k // per, 0, k % per))]
    else:
        assert wgs[0].shape[-1] == tk
        w_specs = [pl.BlockSpec((None, D, tk), lambda i, k, c=c: (jnp.minimum((k + n - 1 - c) // n, N_CHIPS - 1), 0, 0))
                   for c in range(n)]
    nk = IN_COLS // tk

    def body(dp_ref, *rest):
        w_refs, (x_ref, mod_ref, dxr_ref, dx_ref, dsh_ref, dsc_ref, acc) = rest[:n], rest[n:]
        i, k = pl.program_id(0), pl.program_id(1)

        @pl.when((i == 0) & (k == 0))
        def _():
            dsh_ref[...] = jnp.zeros_like(dsh_ref)
            dsc_ref[...] = jnp.zeros_like(dsc_ref)

        @pl.when(k == 0)
        def _():
            acc[...] = jnp.zeros_like(acc)

        for c in range(n):
            @pl.when(k % n == c)
            def _(c=c):
                acc[...] += _dot_nt(dp_ref[...], w_refs[c][...])

        @pl.when(k == nk - 1)
        def _():
            dh = acc[...]
            xs, rstd = _standardize(x_ref[...])
            dsh_ref[...] += jnp.sum(dh, axis=0, keepdims=True)
            dsc_ref[...] += jnp.sum(dh * xs, axis=0, keepdims=True)
            dx_ref[...] = _standardize_bwd(dh * (1.0 + mod_ref[:, D:2 * D]), xs, rstd) + dxr_ref[...]

    row = pl.BlockSpec((tm, D), lambda i, k: (i, 0))
    vec = pl.BlockSpec((1, D), lambda i, k: (0, 0))
    return _hosted(
        body, job, (dproj, *w_args, x, mod, dx_res), name="proj_bwd",
        grid=(S // tm, nk),
        in_specs=[pl.BlockSpec((tm, tk), lambda i, k: (i, k))] + w_specs + [row, pl.BlockSpec((1, 3 * D), lambda i, k: (0, 0)), row],
        out_specs=[row, vec, vec],
        out_shape=[jax.ShapeDtypeStruct((S, D), F32), jax.ShapeDtypeStruct((1, D), F32), jax.ShapeDtypeStruct((1, D), F32)],
        scratch_shapes=[pltpu.VMEM((tm, D), F32)],
        semantics=("arbitrary", "arbitrary"))


def _grad_w(a, b, mid, prev, name, tm, tn, b_col0=0):
    S, M = a.shape
    shape = prev.shape
    n_shard = shape[-1]
    per = n_shard // tn
    nm = M // tm
    lead = len(mid)

    def body(*refs):
        a_ref, b_ref, o_ref = refs[0], refs[1], refs[-1]
        o_ref[...] = _dot_tn(a_ref[...], b_ref[...]).astype(BF)

    in_specs = [pl.BlockSpec((S, tm), lambda m, n: (0, m)),
                pl.BlockSpec((S, tn), lambda m, n: (0, b_col0 // tn + n))]
    args = [a, b]
    aliases = {}
    if not isinstance(prev, jax.ShapeDtypeStruct):
        in_specs.append(pl.BlockSpec(memory_space=pl.ANY))
        args.append(prev)
        aliases = {2: 0}
    return pl.pallas_call(
        body, name=name,
        grid=(nm, N_CHIPS * per),
        in_specs=in_specs,
        out_specs=pl.BlockSpec((None,) + (None,) * lead + (tm, tn), lambda m, n: (n // per,) + tuple(mid) + (m, n % per)),
        out_shape=jax.ShapeDtypeStruct(shape, BF),
        input_output_aliases=aliases,
        compiler_params=_params(dimension_semantics=("parallel", "parallel")),
    )(*args)


def _all_gather8(x, name):
    R, N = x.shape

    def body(x_ref, out_ref, send_sems, recv_sems):
        mx, my, mc = lax.axis_index("x"), lax.axis_index("y"), lax.axis_index("c")
        me = 4 * mx + 2 * my + mc
        out_ref[me] = x_ref[...]
        copies = []
        for k in range(1, N_DEV):
            peer = (_flip(mx, k & 4), _flip(my, k & 2), _flip(mc, k & 1))
            cp = pltpu.make_async_remote_copy(src_ref=x_ref, dst_ref=out_ref.at[me], send_sem=send_sems.at[k - 1],
                                              recv_sem=recv_sems.at[k - 1], device_id=peer, device_id_type=MESH)
            cp.start()
            copies.append(cp)
        for cp in copies:
            cp.wait()

    return pl.pallas_call(
        body, name=name,
        in_specs=[pl.BlockSpec(memory_space=pltpu.VMEM)],
        out_specs=pl.BlockSpec(memory_space=pltpu.VMEM),
        out_shape=jax.ShapeDtypeStruct((N_DEV, R, N), F32),
        scratch_shapes=[pltpu.SemaphoreType.DMA((N_DEV - 1,)), pltpu.SemaphoreType.DMA((N_DEV - 1,))],
        compiler_params=_params(),
    )(x)


def _rows2d(a):
    return a.reshape(-1, a.shape[-1])


def _tile_rows(rows, cols, n_arrays):
    budget = (24 << 20) // (n_arrays * 2 * 4 * cols)
    if rows <= budget:
        return rows
    tm = 8
    for cand in range(8, budget + 1, 8):
        if rows % cand == 0:
            tm = cand
    return tm


SUM_ROWS = 256


def _sum_cores(g, sent, where):
    chips, lead, _, r, cols = g.shape
    tr = min(r, SUM_ROWS)

    def body(where_ref, g_ref, s_ref, out_ref):
        out_ref[...] = (g_ref[...].astype(F32) + s_ref[...].astype(F32)).astype(BF)

    spec = pl.BlockSpec((None, tr, cols), lambda i, j, where_ref: (i, j, 0))
    out = pl.pallas_call(
        body, name="sum_cores",
        grid_spec=pltpu.PrefetchScalarGridSpec(
            num_scalar_prefetch=1, grid=(chips * lead, r // tr),
            in_specs=[pl.BlockSpec((None, None, tr, cols), lambda i, j, where_ref: (i, where_ref[1], j, 0)), spec],
            out_specs=spec),
        out_shape=jax.ShapeDtypeStruct((chips * lead, r, cols), BF),
        compiler_params=_params(dimension_semantics=("parallel", "parallel")),
    )(where, g.reshape(chips * lead, 2, r, cols), sent.reshape(chips * lead, r, cols))
    return out.reshape(chips, lead, r, cols)


def _sum_chips(sums, got, where):
    _, lead, r, cols = sums.shape
    tr = min(r, SUM_ROWS)

    def body(where_ref, s_ref, g_ref, out_ref):
        out_ref[...] = ((s_ref[...].astype(F32) + g_ref[0].astype(F32)) + g_ref[1].astype(F32)) + g_ref[2].astype(F32)

    return pl.pallas_call(
        body, name="sum_chips",
        grid_spec=pltpu.PrefetchScalarGridSpec(
            num_scalar_prefetch=1, grid=(lead, r // tr),
            in_specs=[pl.BlockSpec((None, None, tr, cols), lambda i, j, where_ref: (where_ref[0], i, j, 0)),
                      pl.BlockSpec((N_CHIPS - 1, None, tr, cols), lambda i, j, where_ref: (0, i, j, 0))],
            out_specs=pl.BlockSpec((None, None, tr, cols), lambda i, j, where_ref: (i, where_ref[1], j, 0))),
        out_shape=jax.ShapeDtypeStruct((lead, 2, r, cols), F32),
        compiler_params=_params(dimension_semantics=("parallel", "parallel")),
    )(where, sums, got)


def _adamw(w, m, v, groups, name):
    shape = w.shape
    w2, m2, v2 = _rows2d(w), _rows2d(m), _rows2d(v)
    rows, cols = w2.shape
    ng = len(groups)
    n = len(groups[0])
    slab = rows // ng
    gs = [_rows2d(g) for grp in groups for g in grp]
    tm = _tile_rows(slab, cols, 7 + n)
    tiles = slab // tm
    c1 = 1.0 / (1.0 - ADAM_B1 ** ADAM_STEP)
    c2 = 1.0 / (1.0 - ADAM_B2 ** ADAM_STEP)

    def body(*refs):
        w_ref, m_ref, v_ref = refs[:3]
        g_refs = refs[3:3 + ng * n]
        go_ref, d_ref, mo_ref, vo_ref = refs[3 + ng * n:]
        which = pl.program_id(0)
        for s in range(ng):
            @pl.when(which == s)
            def _(s=s):
                g = g_refs[s * n][...]
                for r in g_refs[s * n + 1:(s + 1) * n]:
                    g = g + r[...]
                mn = ADAM_B1 * m_ref[...] + (1.0 - ADAM_B1) * g
                vn = ADAM_B2 * v_ref[...] + (1.0 - ADAM_B2) * (g * g)
                go_ref[...] = g
                mo_ref[...] = mn
                vo_ref[...] = vn
                d_ref[...] = -ADAM_LR * ((mn * c1) / (jnp.sqrt(vn * c2) + ADAM_EPS) + ADAM_WD * w_ref[...])

    spec = pl.BlockSpec((tm, cols), lambda s, i: (s * tiles + i, 0))
    g_specs = [pl.BlockSpec((tm, cols), lambda s, i, k=k: (jnp.where(s == k, i, jnp.where(s < k, 0, tiles - 1)), 0))
               for k in range(ng) for _ in range(n)]
    outs = pl.pallas_call(
        body, name=name,
        grid=(ng, tiles),
        in_specs=[spec] * 3 + g_specs,
        out_specs=[spec] * 4,
        out_shape=[jax.ShapeDtypeStruct((rows, cols), F32)] * 4,
        compiler_params=_params(dimension_semantics=("arbitrary", "arbitrary")),
    )(w2, m2, v2, *gs)
    return [o.reshape(shape) for o in outs]


def _lower_bounds(r0, r1):
    top = jnp.maximum(r0, r1)
    e0, e1 = jnp.exp(r0 - top), jnp.exp(r1 - top)
    p0, p1 = e0 / (e0 + e1), e1 / (e0 + e1)
    return p0 - p0, (p0 + p1) - p0


def _lbs_fwd(lb_raw):
    def body(lb_ref, out_ref):
        l0, l1 = _lower_bounds(lb_ref[0:1, :], lb_ref[1:2, :])
        out_ref[0:1, :] = l0
        out_ref[1:2, :] = l1

    return pl.pallas_call(body, name="lower_bounds", out_shape=jax.ShapeDtypeStruct(lb_raw.shape, F32), compiler_params=_params())(lb_raw)


def _mod_rows(c_all, w_mod, tn=768):
    _, D, cols = w_mod.shape

    def body(c_ref, w_ref, out_ref):
        out_ref[...] = _dot(c_ref[...].astype(BF), w_ref[...].astype(BF))

    return pl.pallas_call(
        body, name="mod_rows",
        grid=(DEPTH,),
        in_specs=[pl.BlockSpec((N_DEV, D), lambda l: (0, 0)), pl.BlockSpec((None, D, cols), lambda l: (l, 0, 0))],
        out_specs=pl.BlockSpec((N_DEV, cols), lambda l: (0, l)),
        out_shape=jax.ShapeDtypeStruct((N_DEV, DEPTH * cols), F32),
        compiler_params=_params(dimension_semantics=("parallel",)),
    )(c_all, w_mod)


def _grad_w_mod(c_all, dmod_cols):
    D = c_all.shape[1]
    cols = dmod_cols.shape[-1]

    def body(c_ref, d_ref, out_ref):
        out_ref[...] = _dot_tn(c_ref[...].astype(BF), d_ref[...].astype(BF))

    return pl.pallas_call(
        body, name="grad_w_mod",
        grid=(DEPTH,),
        in_specs=[pl.BlockSpec((N_DEV, D), lambda l: (0, 0)), pl.BlockSpec((None, N_DEV, cols), lambda l: (l, 0, 0))],
        out_specs=pl.BlockSpec((None, D, cols), lambda l: (l, 0, 0)),
        out_shape=jax.ShapeDtypeStruct((DEPTH, D, cols), F32),
        compiler_params=_params(dimension_semantics=("parallel",)),
    )(c_all, dmod_cols)


def _sum_devices(parts):
    _, R, N = parts.shape

    def body(p_ref, out_ref):
        acc = p_ref[0]
        for d in range(1, N_DEV):
            acc = acc + p_ref[d]
        out_ref[...] = acc

    return pl.pallas_call(body, name="sum_devices", out_shape=jax.ShapeDtypeStruct((R, N), F32), compiler_params=_params())(parts)


def _lbs_bwd(lb_raw, dl):
    def body(lb_ref, dl_ref, out_ref):
        _, pull = jax.vjp(_lower_bounds, lb_ref[0:1, :], lb_ref[1:2, :])
        d0, d1 = pull((dl_ref[0:1, :], dl_ref[1:2, :]))
        out_ref[0:1, :] = d0
        out_ref[1:2, :] = d1

    return pl.pallas_call(body, name="lower_bounds_bwd", out_shape=jax.ShapeDtypeStruct(lb_raw.shape, F32), compiler_params=_params())(lb_raw, dl)


def kernel(x, c, w_mod, b_mod, w_in, conv_w, hgrn_norm_w, lower_bounds, w_branch, w_out, ln_g, ln_b, loss_target, m_w_mod, m_b_mod, m_w_in, m_conv_w, m_hgrn_norm_w, m_lower_bounds, m_w_branch, m_w_out, m_ln_g, m_ln_b, v_w_mod, v_b_mod, v_w_in, v_conv_w, v_hgrn_norm_w, v_lower_bounds, v_w_branch, v_w_out, v_ln_g, v_ln_b):
    D = D_MODEL
    x0 = x[0]
    target = loss_target[0]
    S = x0.shape[0]
    mx, my, mc = lax.axis_index("x"), lax.axis_index("y"), lax.axis_index("c")
    chip = 2 * mx + my
    me = 2 * chip + mc
    mod_cols = 3 * D // N_CHIPS

    plan = _Plan(w_in.astype(BF), w_branch.astype(BF), w_out.astype(BF), chip, mc)
    n_conv = DEPTH * 3 * (WIDTH // N_CHIPS)
    first = jnp.concatenate([c, conv_w.reshape(1, n_conv), jnp.zeros((1, 2 * D - D - n_conv), F32)], axis=1)
    first = plan.first(first.reshape(8, 2 * D // 8)).reshape(N_DEV, 2 * D)

    c_all = first[:, :D]
    conv_all = first[:, D:D + n_conv].reshape(N_DEV, DEPTH, 3, WIDTH // N_CHIPS)[0::2]
    mod_part = _all_gather8(_mod_rows(c_all, w_mod), "gather_mod")[0::2]
    mod_part = lax.dynamic_index_in_dim(mod_part, me, axis=1, keepdims=False).reshape(N_CHIPS, DEPTH, mod_cols)
    mods = [(mod_part[:, l].reshape(1, 3 * D) + b_mod[l][None, :]) for l in range(DEPTH)]
    lbs = _lbs_fwd(lower_bounds).reshape(DEPTH, WIDTH // 128, 1, 128)
    loss_blk, dx, small = _local_step(x0, target, mods, lbs, conv_all, hgrn_norm_w, ln_g, ln_b, plan)

    n_mod, n_nw, n_lb, n_ln, n_cw = DEPTH * 3 * D, DEPTH * 128, DEPTH * WIDTH, DEPTH * D, DEPTH * 3 * WIDTH
    row = jnp.concatenate(
        [jnp.concatenate([small[l][0], small[l][1], small[l][2]], axis=1) for l in range(DEPTH)]
        + [jnp.sum(small[l][3], axis=0) for l in range(DEPTH)]
        + [small[l][4].reshape(1, WIDTH) for l in range(DEPTH)]
        + [small[l][5] for l in range(DEPTH)] + [small[l][6] for l in range(DEPTH)]
        + [jnp.transpose(small[l][7], (1, 0, 2)).reshape(1, 3 * WIDTH) for l in range(DEPTH)]
        + [loss_blk[0:1, :]], axis=1)
    n_row = row.shape[1]
    fold = -(-n_row // (8 * 128)) * 128
    rows = jnp.concatenate([row, jnp.zeros((1, 8 * fold - n_row), F32)], axis=1).reshape(8, fold)

    whole, gathered = plan.finish(rows)
    grads = {kind: [[whole[(kind, l)]] for l in range(DEPTH)] for kind in ("in", "br", "out")}

    off_nw = n_mod
    off_lb = off_nw + n_nw
    off_lng = off_lb + n_lb
    off_lnb = off_lng + n_ln
    off_cw = off_lnb + n_ln
    off_loss = off_cw + n_cw
    total = _sum_devices(gathered).reshape(1, 8 * fold)
    gathered = gathered.reshape(N_DEV, 1, 8 * fold)
    d_lower = _lbs_bwd(lower_bounds, total[0, off_lb:off_lng].reshape(DEPTH, WIDTH))
    loss = total[0, off_loss]
    d_b_mod = total[0, :n_mod].reshape(DEPTH, 3 * D)
    d_norm_w = total[0, off_nw:off_lb].reshape(DEPTH, 128)
    d_ln_g = total[0, off_lng:off_lnb].reshape(DEPTH, D)
    d_ln_b = total[0, off_lnb:off_cw].reshape(DEPTH, D)
    d_conv = total[0, off_cw:off_loss].reshape(DEPTH, 3, N_CHIPS, WIDTH // N_CHIPS)
    d_conv = lax.dynamic_index_in_dim(d_conv, chip, axis=2, keepdims=False)
    dmod_all = gathered[:, 0, :n_mod].reshape(N_DEV, DEPTH, N_CHIPS, mod_cols)
    dmod_cols = jnp.transpose(lax.dynamic_index_in_dim(dmod_all, chip, axis=2, keepdims=False), (1, 0, 2))
    d_w_mod = _grad_w_mod(c_all, dmod_cols)

    res = {}
    res["w_mod"] = _adamw(w_mod, m_w_mod, v_w_mod, [[d_w_mod]], "adamw_w_mod")
    res["b_mod"] = _adamw(b_mod, m_b_mod, v_b_mod, [[d_b_mod]], "adamw_b_mod")
    res["w_in"] = _adamw(w_in, m_w_in, v_w_in, grads["in"], "adamw_w_in")
    res["conv_w"] = _adamw(conv_w, m_conv_w, v_conv_w, [[d_conv]], "adamw_conv_w")
    res["hgrn_norm_w"] = _adamw(hgrn_norm_w, m_hgrn_norm_w, v_hgrn_norm_w, [[d_norm_w]], "adamw_norm_w")
    res["lower_bounds"] = _adamw(lower_bounds, m_lower_bounds, v_lower_bounds, [[d_lower]], "adamw_lower_bounds")
    res["w_branch"] = _adamw(w_branch, m_w_branch, v_w_branch, grads["br"], "adamw_w_branch")
    res["w_out"] = _adamw(w_out, m_w_out, v_w_out, grads["out"], "adamw_w_out")
    res["ln_g"] = _adamw(ln_g, m_ln_g, v_ln_g, [[d_ln_g]], "adamw_ln_g")
    res["ln_b"] = _adamw(ln_b, m_ln_b, v_ln_b, [[d_ln_b]], "adamw_ln_b")
    names = ["w_mod", "b_mod", "w_in", "conv_w", "hgrn_norm_w", "lower_bounds", "w_branch", "w_out", "ln_g", "ln_b"]
    return (loss, dx[None], *[res[n][0] for n in names], *[res[n][1] for n in names],
            *[res[n][2] for n in names], *[res[n][3] for n in names])


class _Plan:
    FIRST_CHUNKS = 3
    WINDOWS = ((0, 1152), (1152, 640), (1792, 512))

    def __init__(self, w_in, w_br, w_out, chip, core):
        self.local = {"in": w_in, "br": w_br, "out": w_out}
        self.chip, self.where = chip, jnp.stack([chip, core]).astype(jnp.int32)
        self.gathered, self.partial, self.grads, self.chip_sums, self.scattered, self.pending = {}, {}, {}, {}, {}, {}

    def chunks(self, l):
        return self.FIRST_CHUNKS if l == 0 else 1

    def w_in(self, l):
        return [self.gathered[("in", l, c)] for c in range(self.chunks(l))]

    def _shard(self, key):
        mine = self.local[key[0]][key[1]]
        if key[0] == "in":
            cols = mine.shape[-1] // self.chunks(key[1])
            mine = mine[:, key[2] * cols:(key[2] + 1) * cols]
        return mine

    def _slab(self, key):
        mine = _halves(self._shard(key))
        return lax.dynamic_update_slice(lax.empty((N_CHIPS,) + mine.shape, mine.dtype), mine[None], (self.chip, 0, 0, 0, 0))

    def _gather(self, keys):
        return ("gather", keys), _gather_job([self._slab(key) for key in keys])

    def _gather_window(self, key, n):
        slab = self._slab(key) if n == 0 else self.partial[key]
        return ("gather" if n == len(self.WINDOWS) - 1 else "gather_part", [key]), _gather_job([slab], self.WINDOWS[n])

    def _to_sibling(self, keys):
        return ("to_sibling", keys), _to_sibling_job([_halves(self.grads[key], 1) for key in keys])

    def _scatter(self, keys):
        return ("scatter", keys), _scatter_job([self.chip_sums[key] for key in keys])

    def job(self, stage, l, c=0):
        parts = []
        if stage == "proj_fwd":
            parts = [self._gather([("in", l, c + 1)] if c + 1 < self.chunks(l) else [("br", l), ("out", l)])]
        elif stage == "attn_fwd" and l + 1 < DEPTH:
            parts = [self._gather_window(("in", l + 1, 0), 0)]
        elif stage == "hgrn_fwd" and l + 1 < DEPTH:
            parts = [self._gather_window(("in", l + 1, 0), 1)]
        elif stage == "merge_fwd" and l + 1 < DEPTH:
            parts = [self._gather_window(("in", l + 1, 0), 2)]
        elif stage == "attn_bwd":
            parts = [self._to_sibling([("out", l), ("br", l)])] + ([self._scatter([("in", l + 1)])] if l + 1 < DEPTH else [])
        elif stage == "hgrn_bwd":
            parts = [self._scatter([("out", l), ("br", l)])]
        elif stage == "proj_bwd":
            parts = [self._to_sibling([("in", l)])] if l else [self._scatter([("in", 0)])]
        self.pending[(stage, l, c)] = [(tag, len(job.outs)) for tag, job in parts]
        return _join_jobs([job for _, job in parts])

    def done(self, stage, l, outs, c=0):
        if outs is None:
            return
        at = 0
        for (what, keys), n_outs in self.pending[(stage, l, c)]:
            mine, at = outs[at:at + n_outs], at + n_outs
            for n, key in enumerate(keys):
                if what == "gather":
                    self.gathered[key] = mine[n].reshape((N_CHIPS,) + self._shard(key).shape)
                elif what == "gather_part":
                    self.partial[key] = mine[n]
                elif what == "to_sibling":
                    self.chip_sums[key] = _sum_cores(_halves(self.grads[key], 1), mine[n], self.where)
                else:
                    self.scattered[key] = mine[n]

    def first(self, rows):
        tag, job = self._gather([("in", 0, 0)])
        self.pending[("first", 0, 0)] = [(tag, len(job.outs))]
        outs = _run_job(_join_jobs([job, _gather8_job(rows)]), "gather_first")
        self.done("first", 0, outs[:-1])
        return outs[-1]

    def took(self, key, grad):
        self.grads[key] = grad
        if key == ("in", 0):
            tag, job = self._to_sibling([key])
            self.pending[("took", 0, 0)] = [(tag, len(job.outs))]
            self.done("took", 0, _run_job(job, "to_sibling_last"))

    def finish(self, rows):
        keys = [(kind, l) for kind in ("in", "br", "out") for l in range(DEPTH)]
        halves = [_sum_chips(self.chip_sums[key], self.scattered[key], self.where) for key in keys]
        outs = _run_job(_join_jobs([_place_job(halves), _gather8_job(rows)]), "place_halves")
        return {key: w.reshape(self.grads[key].shape[1:]) for key, w in zip(keys, outs[:-1])}, outs[-1]


def _local_step(x0, target, mods, lbs, conv_all, hgrn_norm_w, ln_g, ln_b, plan):
    D = D_MODEL
    after, before = _attn_consts()
    hg_consts = _hgrn_consts()

    saved = []
    xl = x0
    for l in range(DEPTH):
        n = plan.chunks(l)
        (proj, h), got = _proj_fwd(xl, mods[l], plan.gathered[("in", l, 0)], plan.job("proj_fwd", l, 0), (0, n))
        plan.done("proj_fwd", l, got, 0)
        for c in range(1, n):
            (proj,), got = _proj_cols(h, plan.gathered[("in", l, c)], (c, n), proj, plan.job("proj_fwd", l, c))
            plan.done("proj_fwd", l, got, c)
        (o_a, tot), got = _attn_fwd(proj, after, plan.job("attn_fwd", l))
        plan.done("attn_fwd", l, got)
        (o_b, states), got = _hgrn_fwd(proj, lbs[l], hg_consts, plan.job("hgrn_fwd", l))
        plan.done("hgrn_fwd", l, got)
        ys = _branch_fwd(proj, o_a, o_b, hgrn_norm_w, conv_all, l)
        (x_next, merged, y), got = _merge_fwd(xl, mods[l], proj, ys, plan.gathered[("br", l)], plan.gathered[("out", l)], ln_g, ln_b, l,
                                              plan.job("merge_fwd", l))
        plan.done("merge_fwd", l, got)
        saved.append((xl, proj, h, o_a, tot, o_b, states, ys, merged, y))
        xl = x_next
    dx, loss_blk = _loss_head(xl, target)

    small = [None] * DEPTH
    for l in reversed(range(DEPTH)):
        xin, proj, h, o_a, tot, o_b, states, ys, merged, y = saved[l]
        dx_res, dy, dp, dproj, dy_a, dy_b, dy_c, dln_g, dln_b, dgate = _merge_bwd(
            dx, xin, y, mods[l], proj, ys, plan.gathered[("br", l)], plan.gathered[("out", l)], ln_g, l)
        plan.took(("out", l), _grad_w_out(merged, dy))
        g_br = jax.ShapeDtypeStruct((N_CHIPS, 3, WIDTH, D // N_CHIPS), BF)
        for i in range(3):
            g_br = _grad_w(ys[i], dp, (i,), g_br, "grad_w_branch", tm=WIDTH, tn=D // N_CHIPS, b_col0=i * D)
        plan.took(("br", l), g_br)
        d_oa, d_ob, dnorm_w, dconv_w, dproj = _branch_bwd(proj, o_a, o_b, hgrn_norm_w, conv_all, dy_a, dy_b, dy_c, dproj, l)
        (dproj,), got = _attn_bwd(proj, d_oa, tot, after, before, dproj, plan.job("attn_bwd", l))
        plan.done("attn_bwd", l, got)
        (dlb, dproj), got = _hgrn_bwd(proj, lbs[l], states, d_ob, hg_consts, dproj, plan.job("hgrn_bwd", l))
        plan.done("hgrn_bwd", l, got)
        plan.took(("in", l), _grad_w(h, dproj, (), jax.ShapeDtypeStruct((N_CHIPS, D, IN_COLS // N_CHIPS), BF), "grad_w_in", tm=512, tn=2304))
        (dx, dshift, dscale), got = _proj_bwd(dproj, plan.w_in(l), xin, mods[l], dx_res, plan.job("proj_bwd", l))
        plan.done("proj_bwd", l, got)
        small[l] = (dshift, dscale, dgate, dnorm_w, dlb, dln_g, dln_b, dconv_w)
    return loss_blk, dx, small


def _grad_w_out(merged, dy):
    S, D = merged.shape
    q = D // N_CHIPS

    def body(a_ref, b_ref, o_ref):
        o_ref[...] = _dot_tn(a_ref[...], b_ref[...]).astype(BF)

    return pl.pallas_call(
        body, name="grad_w_out",
        grid=(N_CHIPS,),
        in_specs=[pl.BlockSpec((S, q), lambda j: (0, j)), pl.BlockSpec((S, D), lambda j: (0, 0))],
        out_specs=pl.BlockSpec((None, q, D), lambda j: (j, 0, 0)),
        out_shape=jax.ShapeDtypeStruct((N_CHIPS, q, D), BF),
        compiler_params=_params(dimension_semantics=("parallel",)),
    )(merged, dy)
```

```python
import functools
import math

import numpy as np
import jax
import jax.numpy as jnp
from jax import lax
from jax.experimental import pallas as pl
from jax.experimental.pallas import tpu as pltpu

F32 = jnp.float32
BF = jnp.bfloat16
MESH = pl.DeviceIdType.MESH

DEPTH = 2
D_MODEL = 1024
WIDTH = 512
IN_COLS = 12 * WIDTH + 3 * D_MODEL
N_CHIPS = 4
N_DEV = 8
SB_BLOCK = 128
SB_HEAD_DIM = 64
HG_CHUNK = 128
HG_DIM = 128
LN_EPS = 1e-5
RMS_EPS = 1e-6
ALPHA = (2.0 * DEPTH) ** 0.25
ADAM_LR, ADAM_B1, ADAM_B2, ADAM_EPS, ADAM_WD, ADAM_STEP = 0.001, 0.9, 0.999, 1e-08, 0.01, 10
VMEM_LIMIT = 56 << 20


def _params(**kw):
    return pltpu.CompilerParams(vmem_limit_bytes=VMEM_LIMIT, **kw)


def _dot(a, b):
    return jnp.dot(a, b, preferred_element_type=F32)


def _dot_nt(a, b):
    return lax.dot_general(a, b, (((1,), (1,)), ((), ())), preferred_element_type=F32)


def _dot_tn(a, b):
    return lax.dot_general(a, b, (((0,), (0,)), ((), ())), preferred_element_type=F32)


def _sigmoid(x):
    return 1.0 / (1.0 + jnp.exp(-x))


def _silu(x):
    return x * _sigmoid(x)


def _split_dot(t, g, terms):
    acc = None
    rest = g
    for _ in range(terms):
        part = rest.astype(BF)
        rest = rest - part.astype(F32)
        d = _dot(t, part)
        acc = d if acc is None else acc + d
    return acc


def _standardize(x):
    mu = jnp.mean(x, axis=-1, keepdims=True)
    xc = x - mu
    var = jnp.mean(xc * xc, axis=-1, keepdims=True)
    rstd = lax.rsqrt(var + LN_EPS)
    return xc * rstd, rstd


def _standardize_bwd(dxs, xs, rstd):
    return rstd * (dxs - jnp.mean(dxs, axis=-1, keepdims=True) - xs * jnp.mean(dxs * xs, axis=-1, keepdims=True))


class _Job:
    def __init__(self, ins, outs, sems, make, alias=None):
        self.ins, self.outs, self.sems, self.make = list(ins), list(outs), list(sems), make
        self.alias = dict(alias or {})


def _join_jobs(jobs):
    jobs = [j for j in jobs if j is not None]
    if len(jobs) <= 1:
        return jobs[0] if jobs else None

    def make(ins, outs, sems):
        phases, i, o, s = [], 0, 0, 0
        for j in jobs:
            got = j.make(ins[i:i + len(j.ins)], outs[o:o + len(j.outs)], sems[s:s + len(j.sems)])
            i, o, s = i + len(j.ins), o + len(j.outs), s + len(j.sems)
            for n, phase in enumerate(got):
                if n == len(phases):
                    phases.append([])
                phases[n] += phase
        return phases

    alias, i, o = {}, 0, 0
    for j in jobs:
        alias.update({i + a: o + b for a, b in j.alias.items()})
        i, o = i + len(j.ins), o + len(j.outs)
    return _Job(sum([j.ins for j in jobs], []), sum([j.outs for j in jobs], []), sum([j.sems for j in jobs], []), make, alias)


def _flip(v, bit):
    return 1 - v if bit else v


def _halves(a, front=0):
    shape = a.shape
    lead = math.prod(shape[front:-2])
    return a.reshape(shape[:front] + (lead, 2, shape[-2] // 2, shape[-1]))


def _dma_sems(*shapes):
    return [pltpu.SemaphoreType.DMA(s) for s in shapes]


def _same(arrays):
    return [jax.ShapeDtypeStruct(a.shape, a.dtype) for a in arrays]


def _gather_job(slabs, window=None):
    n = len(slabs)
    cols = slice(None) if window is None else pl.ds(*window)

    def make(ins, outs, sems):
        send1, recv1, send2, recv2 = sems
        mx, my, mc = lax.axis_index("x"), lax.axis_index("y"), lax.axis_index("c")
        fetch, pass_on = [], []
        for a in range(n):
            ours = outs[a].at[2 * mx + my, :, mc, :, cols]
            for k in range(1, N_CHIPS):
                px, py = _flip(mx, k & 2), _flip(my, k & 1)
                fetch.append(pltpu.make_async_remote_copy(
                    src_ref=ours, dst_ref=ours, send_sem=send1.at[a, k - 1], recv_sem=recv1.at[a, k - 1],
                    device_id=(px, py, mc), device_id_type=MESH))
                theirs = outs[a].at[2 * px + py, :, mc, :, cols]
                pass_on.append(pltpu.make_async_remote_copy(
                    src_ref=theirs, dst_ref=theirs, send_sem=send2.at[a, k - 1], recv_sem=recv2.at[a, k - 1],
                    device_id=(mx, my, 1 - mc), device_id_type=MESH))
        return [fetch, pass_on]

    pairs = (n, N_CHIPS - 1)
    return _Job(slabs, _same(slabs), _dma_sems(pairs, pairs, pairs, pairs), make, {a: a for a in range(n)})


def _to_sibling_job(grads):
    n = len(grads)

    def make(ins, outs, sems):
        send_sems, recv_sems = sems
        mx, my, mc = lax.axis_index("x"), lax.axis_index("y"), lax.axis_index("c")
        return [[pltpu.make_async_remote_copy(
            src_ref=ins[a].at[:, :, 1 - mc], dst_ref=outs[a], send_sem=send_sems.at[a], recv_sem=recv_sems.at[a],
            device_id=(mx, my, 1 - mc), device_id_type=MESH) for a in range(n)]]

    outs = [jax.ShapeDtypeStruct(g.shape[:2] + g.shape[3:], g.dtype) for g in grads]
    return _Job(grads, outs, _dma_sems((n,), (n,)), make)


def _scatter_job(sums):
    n = len(sums)

    def make(ins, outs, sems):
        send_sems, recv_sems = sems
        mx, my, mc = lax.axis_index("x"), lax.axis_index("y"), lax.axis_index("c")
        copies = []
        for a in range(n):
            for k in range(1, N_CHIPS):
                px, py = _flip(mx, k & 2), _flip(my, k & 1)
                copies.append(pltpu.make_async_remote_copy(
                    src_ref=ins[a].at[2 * px + py], dst_ref=outs[a].at[k - 1], send_sem=send_sems.at[a, k - 1],
                    recv_sem=recv_sems.at[a, k - 1], device_id=(px, py, mc), device_id_type=MESH))
        return [copies]

    pairs = (n, N_CHIPS - 1)
    return _Job(sums, [jax.ShapeDtypeStruct((N_CHIPS - 1,) + s.shape[1:], s.dtype) for s in sums], _dma_sems(pairs, pairs), make)


def _place_job(wholes):
    n = len(wholes)

    def make(ins, outs, sems):
        send_sems, recv_sems = sems
        mx, my, mc = lax.axis_index("x"), lax.axis_index("y"), lax.axis_index("c")
        copies = []
        for a in range(n):
            here = outs[a].at[:, mc]
            copies.append(pltpu.make_async_remote_copy(src_ref=here, dst_ref=here, send_sem=send_sems.at[a], recv_sem=recv_sems.at[a],
                                                       device_id=(mx, my, 1 - mc), device_id_type=MESH))
        return [copies]

    return _Job(wholes, _same(wholes), _dma_sems((n,), (n,)), make, {a: a for a in range(n)})


def _gather8_job(x):
    def make(ins, outs, sems):
        local_sem, send_sems, recv_sems = sems
        mx, my, mc = lax.axis_index("x"), lax.axis_index("y"), lax.axis_index("c")
        here = outs[0].at[4 * mx + 2 * my + mc]
        copies = [pltpu.make_async_copy(ins[0], here, local_sem.at[0])]
        for k in range(1, N_DEV):
            peer = (_flip(mx, k & 4), _flip(my, k & 2), _flip(mc, k & 1))
            copies.append(pltpu.make_async_remote_copy(src_ref=ins[0], dst_ref=here, send_sem=send_sems.at[k - 1],
                                                       recv_sem=recv_sems.at[k - 1], device_id=peer, device_id_type=MESH))
        return [copies]

    return _Job([x], [jax.ShapeDtypeStruct((N_DEV,) + x.shape, x.dtype)], _dma_sems((1,), (N_DEV - 1,), (N_DEV - 1,)), make)


def _run_phases(phases, first=0):
    for n, phase in enumerate(phases):
        if n >= first:
            for cp in phase:
                cp.start()
        for cp in phase:
            cp.wait()


def _run_job(job, name):
    k_in, k_out = len(job.ins), len(job.outs)

    def body(*refs):
        _run_phases(job.make(refs[:k_in], refs[k_in:k_in + k_out], refs[k_in + k_out:]))

    hbm = pl.BlockSpec(memory_space=pl.ANY)
    return pl.pallas_call(body, name=name, in_specs=[hbm] * k_in, out_specs=[hbm] * k_out, out_shape=job.outs,
                          scratch_shapes=job.sems, input_output_aliases=job.alias, compiler_params=_params())(*job.ins)


def _hosted(body, job, args, *, name, grid, in_specs, out_specs, out_shape, scratch_shapes=(), semantics, aliases=None):
    in_specs, out_specs, out_shape, scratch = list(in_specs), list(out_specs), list(out_shape), list(scratch_shapes)
    aliases = dict(aliases or {})
    if job is None:
        outs = pl.pallas_call(body, name=name, grid=grid, in_specs=in_specs, out_specs=out_specs, out_shape=out_shape,
                              scratch_shapes=scratch, input_output_aliases=aliases,
                              compiler_params=_params(dimension_semantics=semantics))(*args)
        return list(outs), None
    n_in, n_out, n_scr, k_in, k_out = len(in_specs), len(out_specs), len(scratch), len(job.ins), len(job.outs)

    def wrapped(*refs):
        ins, rest = refs[:n_in], refs[n_in:]
        job_ins, rest = rest[:k_in], rest[k_in:]
        outs, rest = rest[:n_out], rest[n_out:]
        job_outs, rest = rest[:k_out], rest[k_out:]
        scr, sems = rest[:n_scr], rest[n_scr:]
        ids = [pl.program_id(a) for a in range(len(grid))]
        first = functools.reduce(jnp.logical_and, [i == 0 for i in ids])
        last = functools.reduce(jnp.logical_and, [i == g - 1 for i, g in zip(ids, grid)])

        @pl.when(first)
        def _():
            for cp in job.make(job_ins, job_outs, sems)[0]:
                cp.start()

        body(*ins, *outs, *scr)

        @pl.when(last)
        def _():
            _run_phases(job.make(job_ins, job_outs, sems), first=1)

    hbm = pl.BlockSpec(memory_space=pl.ANY)
    outs = pl.pallas_call(
        wrapped, name=name, grid=grid, in_specs=in_specs + [hbm] * k_in, out_specs=out_specs + [hbm] * k_out,
        out_shape=out_shape + job.outs, scratch_shapes=scratch + job.sems,
        input_output_aliases={**aliases, **{n_in + i: n_out + o for i, o in job.alias.items()}},
        compiler_params=_params(dimension_semantics=("arbitrary",) * len(grid)))(*args, *job.ins)
    return list(outs[:n_out]), list(outs[n_out:])


def _proj_fwd(x, mod, wg, job=None, chunk=(0, 1), tm=512):
    S, D = x.shape
    tm = min(tm, S)
    tn = wg.shape[-1]
    c, n = chunk

    def body(x_ref, mod_ref, w_ref, proj_ref, h_ref, hs):
        @pl.when(pl.program_id(1) == 0)
        def _():
            xs, _ = _standardize(x_ref[...])
            h = xs * (1.0 + mod_ref[:, D:2 * D]) + mod_ref[:, 0:D]
            hb = h.astype(BF)
            hs[...] = hb
            h_ref[...] = hb

        proj_ref[...] = _dot(hs[...], w_ref[...])

    return _hosted(
        body, job, (x, mod, wg), name="proj_fwd",
        grid=(S // tm, N_CHIPS),
        in_specs=[pl.BlockSpec((tm, D), lambda i, j: (i, 0)),
                  pl.BlockSpec((1, 3 * D), lambda i, j: (0, 0)),
                  pl.BlockSpec((None, D, tn), lambda i, j: (j, 0, 0))],
        out_specs=[pl.BlockSpec((tm, tn), lambda i, j: (i, j * n + c)),
                   pl.BlockSpec((tm, D), lambda i, j: (i, 0))],
        out_shape=[jax.ShapeDtypeStruct((S, IN_COLS), F32), jax.ShapeDtypeStruct((S, D), BF)],
        scratch_shapes=[pltpu.VMEM((tm, D), BF)],
        semantics=("parallel", "arbitrary"))


def _proj_cols(h, wg, chunk, proj, job=None, tm=512):
    S, D = h.shape
    tm = min(tm, S)
    tn = wg.shape[-1]
    c, n = chunk

    def body(h_ref, w_ref, prev_ref, proj_ref):
        proj_ref[...] = _dot(h_ref[...], w_ref[...])

    in_specs = [pl.BlockSpec((tm, D), lambda i, j: (i, 0)), pl.BlockSpec((None, D, tn), lambda i, j: (j, 0, 0)),
                pl.BlockSpec(memory_space=pl.ANY)]
    return _hosted(body, job, (h, wg, proj), name="proj_cols", grid=(S // tm, N_CHIPS), in_specs=in_specs,
                   out_specs=[pl.BlockSpec((tm, tn), lambda i, j: (i, j * n + c))],
                   out_shape=[jax.ShapeDtypeStruct(proj.shape, proj.dtype)],
                   semantics=("parallel", "arbitrary"), aliases={2: 0})


SB_ROWS = 256
SB_KEYS = 256


def _attn_consts():
    j = np.arange(SB_KEYS)[:, None]
    s = np.arange(SB_KEYS)[None, :]
    from_here = np.concatenate([(j >= s), (j >= s)], axis=0).astype(np.float32)
    return jnp.asarray(from_here, BF), jnp.asarray((j <= s).astype(np.float32), BF)


def _hi_lo(x):
    hi = lax.bitcast_convert_type(lax.bitcast_convert_type(x, jnp.uint32) & jnp.uint32(0xFFFF0000), F32)
    return hi.astype(BF), (x - hi).astype(BF)


def _sums_r(x, t2):
    hi, lo = _hi_lo(x)
    return _dot(jnp.concatenate([hi, lo], axis=1), t2)


def _all_lanes(col, lanes):
    return jnp.broadcast_to(col, (col.shape[0], lanes))


def _attn_rows(ref, r0, rows, lanes, head0, scale=None):
    v = ref[pl.ds(r0, rows), lanes]
    if scale is not None:
        v = v * scale
    return jnp.concatenate([jnp.where(head0, v, 0.0), jnp.where(head0, 0.0, v)], axis=0).astype(BF)


SB_PAIRS_FWD = 4
SB_PAIRS_BWD = 2


def _attn_specs(S, n_pairs):
    return lambda base: pl.BlockSpec((S, n_pairs * SB_BLOCK), lambda p, base=base: (0, base // n_pairs + p))


def _attn_scores(q2n, k_ref, lanes, kj, t2, from_here_ref, masked):
    c0 = pl.multiple_of(kj * SB_KEYS, SB_KEYS)
    kb = k_ref[pl.ds(c0, SB_KEYS), lanes].astype(BF)
    zn = _dot_nt(q2n, kb)
    lsb = jnp.minimum(zn, 0.0) - jnp.log(1.0 + jnp.exp(-jnp.abs(zn)))
    valid = None
    if masked:
        valid = (lax.broadcasted_iota(jnp.int32, zn.shape, 1) + kj * SB_KEYS) < t2
        lsb = jnp.where(valid, lsb, 0.0)
    return c0, kb, zn, valid, lsb, _sums_r(lsb, from_here_ref[...])


def _attn_fwd(proj, from_here, job=None):
    S = proj.shape[0]
    TQ = SB_ROWS
    assert S % TQ == 0 and SB_KEYS == TQ
    scale = SB_HEAD_DIM ** -0.5
    n_pairs = SB_PAIRS_FWD
    pairs = range(n_pairs)
    lanes = [pl.ds(p * SB_BLOCK, SB_BLOCK) for p in pairs]

    def body(q_ref, k_ref, v_ref, from_here_ref, o_ref, tot_ref, run, acc):
        head0 = lax.broadcasted_iota(jnp.int32, (1, 2 * SB_HEAD_DIM), 1) < SB_HEAD_DIM

        def qloop(qi, _):
            r0 = pl.multiple_of(qi * TQ, TQ)
            q2n = [_attn_rows(q_ref, r0, TQ, lanes[p], head0, -scale) for p in pairs]
            trow = lax.broadcasted_iota(jnp.int32, (TQ, SB_KEYS), 0) + qi * TQ
            t2 = jnp.concatenate([trow, trow], axis=0)
            run[...] = jnp.zeros_like(run)
            acc[...] = jnp.zeros_like(acc)

            def step(kj, masked):
                got = [_attn_scores(q2n[p], k_ref, lanes[p], kj, t2, from_here_ref, masked) for p in pairs]
                for p in pairs:
                    c0, _, zn, valid, _, sums = got[p]
                    r = run[p]
                    e = sums - zn + jnp.concatenate([r, r], axis=1)
                    if masked:
                        e = jnp.where(valid, e, -jnp.inf)
                    acc[p] += _dot(jnp.exp(e).astype(BF), v_ref[pl.ds(c0, SB_KEYS), lanes[p]].astype(BF))
                    run[p] = r + _all_lanes(sums[:, 0:1], SB_BLOCK)

            step(qi, True)

            def below(n, _):
                step(qi - 1 - n, False)
                return 0

            lax.fori_loop(0, qi, below, 0)
            for p in pairs:
                o_ref[pl.ds(r0, TQ), lanes[p]] = jnp.where(head0, acc[p, 0:TQ, :], acc[p, TQ:2 * TQ, :])
                tot_ref[p, 0, pl.ds(r0, TQ), :] = run[p, 0:TQ, :]
                tot_ref[p, 1, pl.ds(r0, TQ), :] = run[p, TQ:2 * TQ, :]
            return 0

        lax.fori_loop(0, S // TQ, qloop, 0)

    col = _attn_specs(S, n_pairs)
    state = pltpu.VMEM((n_pairs, 2 * TQ, SB_BLOCK), F32)
    return _hosted(
        body, job, (proj, proj, proj, from_here), name="attn_fwd",
        grid=(WIDTH // (n_pairs * SB_BLOCK),),
        in_specs=[col(0), col(4), col(8), pl.BlockSpec(from_here.shape, lambda p: (0, 0))],
        out_specs=[col(0), pl.BlockSpec((n_pairs, 2, S, 128), lambda p: (p, 0, 0, 0))],
        out_shape=[jax.ShapeDtypeStruct((S, WIDTH), F32), jax.ShapeDtypeStruct((WIDTH // 128, 2, S, 128), F32)],
        scratch_shapes=[state, state],
        semantics=("parallel",))


def _attn_bwd(proj, d_o, tot, from_here, up_to, dproj, job=None):
    S = proj.shape[0]
    TQ = SB_ROWS
    assert S % TQ == 0 and SB_KEYS == TQ
    scale = SB_HEAD_DIM ** -0.5
    n_pairs = SB_PAIRS_BWD
    pairs = range(n_pairs)
    lanes = [pl.ds(p * SB_BLOCK, SB_BLOCK) for p in pairs]

    def body(q_ref, k_ref, v_ref, do_ref, tot_ref, from_here_ref, up_to_ref, _, dproj_ref, pre, cum, dq_acc, dk_acc, dv_acc, stage, sems):
        dq_ref, dk_ref, dv_ref = stage.at[0], stage.at[1], stage.at[2]
        head0 = lax.broadcasted_iota(jnp.int32, (1, 2 * SB_HEAD_DIM), 1) < SB_HEAD_DIM
        dk_acc[...] = jnp.zeros_like(dk_acc)
        dv_acc[...] = jnp.zeros_like(dv_acc)

        def qloop(qi, _):
            r0 = pl.multiple_of(qi * TQ, TQ)
            q2n = [_attn_rows(q_ref, r0, TQ, lanes[p], head0, -scale) for p in pairs]
            do2 = [_attn_rows(do_ref, r0, TQ, lanes[p], head0) for p in pairs]
            trow = lax.broadcasted_iota(jnp.int32, (TQ, SB_KEYS), 0) + qi * TQ
            t2 = jnp.concatenate([trow, trow], axis=0)
            for p in pairs:
                pre[p, 0:TQ, :] = tot_ref[p, 0, pl.ds(r0, TQ), :]
                pre[p, TQ:2 * TQ, :] = tot_ref[p, 1, pl.ds(r0, TQ), :]
            cum[...] = jnp.zeros_like(cum)
            dq_acc[...] = jnp.zeros_like(dq_acc)

            def step(kj, masked):
                got = [_attn_scores(q2n[p], k_ref, lanes[p], kj, t2, from_here_ref, masked) for p in pairs]
                for p in pairs:
                    c0, kb, zn, valid, lsb, sums = got[p]
                    later = pre[p] - _all_lanes(sums[:, 0:1], SB_BLOCK)
                    pre[p] = later
                    e = sums - zn + jnp.concatenate([later, later], axis=1)
                    sig = jnp.exp(lsb - zn)
                    if masked:
                        e = jnp.where(valid, e, -jnp.inf)
                        sig = jnp.where(valid, sig, 0.0)
                    a = jnp.exp(e)
                    w = _dot_nt(do2[p], v_ref[pl.ds(c0, SB_KEYS), lanes[p]].astype(BF)) * a
                    upto = _dot(w.astype(BF), up_to_ref[...])
                    c = cum[p]
                    dz = w - sig * (upto + jnp.concatenate([c, c], axis=1))
                    cum[p] = c + _all_lanes(upto[:, SB_KEYS - 1:SB_KEYS], SB_BLOCK)
                    dzb = dz.astype(BF)
                    dq_acc[p] += _dot(dzb, kb)
                    dk_acc[pl.ds(c0, SB_KEYS), lanes[p]] += _dot_tn(dzb, q2n[p])
                    dv_acc[pl.ds(c0, SB_KEYS), lanes[p]] += _dot_tn(a.astype(BF), do2[p])

            def below(kj, _):
                step(kj, False)
                return 0

            lax.fori_loop(0, qi, below, 0)
            step(qi, True)
            for p in pairs:
                dq_ref[pl.ds(r0, TQ), lanes[p]] = (jnp.where(head0, dq_acc[p, 0:TQ, :], dq_acc[p, TQ:2 * TQ, :]) * scale).astype(BF)
            return 0

        lax.fori_loop(0, S // TQ, qloop, 0)
        dk_ref[...] = (-dk_acc[...]).astype(BF)
        dv_ref[...] = dv_acc[...].astype(BF)
        lane0 = pl.program_id(0) * (n_pairs * SB_BLOCK)
        _put_columns([dq_ref, dk_ref, dv_ref], dproj_ref, [WIDTH * n + lane0 for n in range(3)], sems)

    col = _attn_specs(S, n_pairs)
    whole = lambda a: pl.BlockSpec(a.shape, lambda p: (0, 0))
    hbm = pl.BlockSpec(memory_space=pl.ANY)
    state = pltpu.VMEM((n_pairs, 2 * TQ, SB_BLOCK), F32)
    grads = pltpu.VMEM((S, n_pairs * SB_BLOCK), F32)
    return _hosted(
        body, job, (proj, proj, proj, d_o, tot, from_here, up_to, dproj), name="attn_bwd",
        grid=(WIDTH // (n_pairs * SB_BLOCK),),
        in_specs=[col(0), col(4), col(8), col(0), pl.BlockSpec((n_pairs, 2, S, 128), lambda p: (p, 0, 0, 0)), whole(from_here),
                  whole(up_to), hbm],
        out_specs=[hbm],
        out_shape=[jax.ShapeDtypeStruct(dproj.shape, dproj.dtype)],
        scratch_shapes=[state, state, state, grads, grads, pltpu.VMEM((3, S, n_pairs * SB_BLOCK), BF), pltpu.SemaphoreType.DMA((3,))],
        semantics=("arbitrary",), aliases={7: 0})


HG_LEVELS = tuple(HG_CHUNK >> n for n in range(1, HG_CHUNK.bit_length()))


def _hgrn_consts():
    C = HG_CHUNK
    t = np.arange(C)[:, None]
    s = np.arange(C)[None, :]
    rows = [(s <= t), (s > t)]
    masks = [(t == s)]
    for m in HG_LEVELS:
        two = 2 * m
        mid = (t // two) * two + m
        right = (t % two) >= m
        rows.append((right & (s >= mid) & (s <= t)) | ((~right) & (s > t) & (s <= mid - 1)))
        masks.append(((t // two) == (s // two)) & right & ((s % two) < m))
    tri = np.concatenate(rows, axis=0).astype(np.float32)
    twice = lambda a: jnp.asarray(np.concatenate([a, a], axis=1), BF)
    return (twice(tri), twice(tri.T), jnp.asarray(np.stack(masks).astype(np.float32), F32))


HG_SUM_BLOCKS = 2 + len(HG_LEVELS)


def _split_rows(g):
    hi = g.astype(BF)
    return jnp.concatenate([hi, (g - hi.astype(F32)).astype(BF)], axis=0)


def _hgrn_sum_blocks(e):
    C = HG_CHUNK
    blocks = tuple(e[n * C:(n + 1) * C] for n in range(HG_SUM_BLOCKS))
    return blocks + (jnp.broadcast_to(e[C - 1:C], (HG_DIM, e.shape[1])),)


@jax.custom_vjp
def _hgrn_sums(tri, tri_t, g):
    return _hgrn_sum_blocks(_split_dot(tri[:, :HG_CHUNK], g, 2))


def _hgrn_sums_fwd(tri, tri_t, g):
    return _hgrn_sum_blocks(_dot(tri, _split_rows(g))), (tri, tri_t)


def _hgrn_sums_bwd(res, ds):
    tri, tri_t = res
    C = HG_CHUNK
    last = lax.broadcasted_iota(jnp.int32, (C, 1), 0) == C - 1
    prefix = ds[0] + jnp.where(last, jnp.sum(ds[-1], axis=0, keepdims=True), 0.0)
    d = jnp.concatenate((prefix,) + tuple(ds[1:-1]), axis=0)
    return jnp.zeros_like(tri), jnp.zeros_like(tri_t), _dot(tri_t, _split_rows(d))


_hgrn_sums.defvjp(_hgrn_sums_fwd, _hgrn_sums_bwd)


def _bf_dot(a, b):
    return _dot(a.astype(BF), b.astype(BF))


def _bf_dot_nt(a, b):
    return _dot_nt(a.astype(BF), b.astype(BF))


def _bf_dot_tn(a, b):
    return _dot_tn(a.astype(BF), b.astype(BF))


@jax.custom_vjp
def _mm(a, b):
    return _bf_dot(a, b)


_mm.defvjp(lambda a, b: (_bf_dot(a, b), (a, b)), lambda r, ct: (_bf_dot_nt(ct, r[1]), _bf_dot_tn(r[0], ct)))


@jax.custom_vjp
def _mm_nt(a, b):
    return _bf_dot_nt(a, b)


_mm_nt.defvjp(lambda a, b: (_bf_dot_nt(a, b), (a, b)), lambda r, ct: (_bf_dot(ct, r[1]), _bf_dot_tn(ct, r[0])))


@jax.custom_vjp
def _mm_tn(a, b):
    return _bf_dot_tn(a, b)


_mm_tn.defvjp(lambda a, b: (_bf_dot_tn(a, b), (a, b)), lambda r, ct: (_bf_dot_nt(r[1], ct), _bf_dot(r[0], ct)))


def _hgrn_chunk(tri, tri_t, masks, qraw, fpre, v, st, lb):
    q = _silu(qraw)
    f = lb + (1.0 - lb) * _sigmoid(fpre)
    k = 1.0 - f
    e = _hgrn_sums(tri, tri_t, jnp.log(f))
    prefix, suffix, whole = e[0], e[1], e[-1]
    scores = masks[0] * _mm_nt(q, k)
    for n in range(len(HG_LEVELS)):
        decay = jnp.exp(e[2 + n])
        scores = scores + masks[n + 1] * _mm_nt(q * decay, k * decay)
    o = _mm_nt(q * jnp.exp(prefix), st) + _mm(scores, v)
    st_new = st * jnp.exp(whole) + _mm_tn(v, k * jnp.exp(suffix))
    return o, st_new


HG_HEADS_PER_STEP = 4
HG_LANES = HG_HEADS_PER_STEP * HG_DIM


def _hgrn_specs(S, consts):
    col = lambda base: pl.BlockSpec((S, HG_LANES), lambda p, base=base: (0, base // HG_HEADS_PER_STEP + p))
    whole = [pl.BlockSpec(a.shape, lambda p, n=a.ndim: (0,) * n) for a in consts]
    return col, whole


def _hgrn_fwd(proj, lbs, consts, job=None):
    S = proj.shape[0]
    nc = S // HG_CHUNK
    heads = range(HG_HEADS_PER_STEP)

    def body(q_ref, f_ref, i_ref, lb_ref, tri_ref, trit_ref, mask_ref, o_ref, st_ref):
        tri, tri_t = tri_ref[...], trit_ref[...]
        masks = [mask_ref[n] for n in range(len(HG_LEVELS) + 1)]

        def chunk(ci, sts):
            r0 = pl.multiple_of(ci * HG_CHUNK, HG_CHUNK)
            rows = pl.ds(r0, HG_CHUNK)
            new = []
            for hd in heads:
                lanes = pl.ds(hd * HG_DIM, HG_DIM)
                st_ref[hd, ci] = sts[hd]
                o, st_new = _hgrn_chunk(tri, tri_t, masks, q_ref[rows, lanes], f_ref[rows, lanes], i_ref[rows, lanes], sts[hd], lb_ref[hd])
                o_ref[rows, lanes] = o
                new.append(st_new)
            return tuple(new)

        lax.fori_loop(0, nc, chunk, tuple(jnp.zeros((HG_DIM, HG_DIM), F32) for _ in heads))

    col, whole = _hgrn_specs(S, consts)
    return _hosted(
        body, job, (proj, proj, proj, lbs, *consts), name="hgrn_fwd",
        grid=(WIDTH // HG_LANES,),
        in_specs=[col(16), col(20), col(24), pl.BlockSpec((HG_HEADS_PER_STEP, 1, 128), lambda p: (p, 0, 0))] + whole,
        out_specs=[col(0), pl.BlockSpec((HG_HEADS_PER_STEP, nc, HG_DIM, HG_DIM), lambda p: (p, 0, 0, 0))],
        out_shape=[jax.ShapeDtypeStruct((S, WIDTH), F32), jax.ShapeDtypeStruct((WIDTH // 128, nc, HG_DIM, HG_DIM), F32)],
        semantics=("parallel",))


def _hgrn_bwd(proj, lbs, states, d_o, consts, dproj, job=None):
    S = proj.shape[0]
    nc = S // HG_CHUNK

    def body(q_ref, f_ref, i_ref, lb_ref, st_ref, do_ref, tri_ref, trit_ref, mask_ref, _, dlb_ref, dproj_ref, stage, sems):
        dq_ref, df_ref, di_ref = stage.at[0], stage.at[1], stage.at[2]
        masks = [mask_ref[n] for n in range(len(HG_LEVELS) + 1)]
        fn = functools.partial(_hgrn_chunk, tri_ref[...], trit_ref[...], masks)
        heads = range(HG_HEADS_PER_STEP)

        def chunk(n, carry):
            ci = nc - 1 - n
            r0 = pl.multiple_of(ci * HG_CHUNK, HG_CHUNK)
            rows = pl.ds(r0, HG_CHUNK)
            new = []
            for hd in heads:
                d_st, dlb = carry[hd]
                lanes = pl.ds(hd * HG_DIM, HG_DIM)
                _, pull = jax.vjp(fn, q_ref[rows, lanes], f_ref[rows, lanes], i_ref[rows, lanes], st_ref[hd, ci], lb_ref[hd])
                dq, df, di, d_prev, dl = pull((do_ref[rows, lanes], d_st))
                dq_ref[rows, lanes] = dq.astype(BF)
                df_ref[rows, lanes] = df.astype(BF)
                di_ref[rows, lanes] = di.astype(BF)
                new.append((d_prev, dlb + dl))
            return tuple(new)

        zero = (jnp.zeros((HG_DIM, HG_DIM), F32), jnp.zeros((1, HG_DIM), F32))
        done = lax.fori_loop(0, nc, chunk, tuple(zero for _ in heads))
        for hd in heads:
            dlb_ref[hd] = done[hd][1]
        lane0 = pl.program_id(0) * HG_LANES
        _put_columns([dq_ref, df_ref, di_ref], dproj_ref, [WIDTH * n + lane0 for n in (4, 5, 6)], sems)

    col, whole = _hgrn_specs(S, consts)
    head = pl.BlockSpec((HG_HEADS_PER_STEP, 1, 128), lambda p: (p, 0, 0))
    hbm = pl.BlockSpec(memory_space=pl.ANY)
    n_in = 6 + len(consts)
    return _hosted(
        body, job, (proj, proj, proj, lbs, states, d_o, *consts, dproj), name="hgrn_bwd",
        grid=(WIDTH // HG_LANES,),
        in_specs=[col(16), col(20), col(24), head, pl.BlockSpec((HG_HEADS_PER_STEP, nc, HG_DIM, HG_DIM), lambda p: (p, 0, 0, 0)), col(0)]
        + whole + [hbm],
        out_specs=[head, hbm],
        out_shape=[jax.ShapeDtypeStruct((WIDTH // 128, 1, 128), F32), jax.ShapeDtypeStruct(dproj.shape, dproj.dtype)],
        scratch_shapes=[pltpu.VMEM((3, S, HG_LANES), BF), pltpu.SemaphoreType.DMA((3,))],
        semantics=("arbitrary",), aliases={n_in: 1})


def _shift_down(x, n):
    rows = lax.broadcasted_iota(jnp.int32, x.shape, 0)
    return jnp.where(rows >= n, pltpu.roll(x, n, 0), 0.0)


def _shift_up(x, n):
    S = x.shape[0]
    rows = lax.broadcasted_iota(jnp.int32, x.shape, 0)
    return jnp.where(rows < S - n, pltpu.roll(x, S - n, 0), 0.0)


def _branch_fwd(proj, o_a, o_b, norm_w, conv_w, layer):
    S = proj.shape[0]

    def body(oa_ref, za_ref, ob_ref, zb_ref, nw_ref, pre_ref, post_ref, u_ref, zc_ref, cw_ref, ya_ref, yb_ref, yc_ref):
        ya_ref[...] = (oa_ref[...] * _silu(za_ref[...])).astype(BF)
        ob = ob_ref[...]
        rn = lax.rsqrt(jnp.mean(ob * ob, axis=-1, keepdims=True) + RMS_EPS)
        yb_ref[...] = (ob * rn * nw_ref[layer:layer + 1, :] * _silu(zb_ref[...])).astype(BF)
        pu = pre_ref[...] * u_ref[...]
        conv = cw_ref[2:3, :] * pu + cw_ref[1:2, :] * _shift_down(pu, 1) + cw_ref[0:1, :] * _shift_down(pu, 2)
        yc_ref[...] = (post_ref[...] * conv * _silu(zc_ref[...])).astype(BF)

    col = lambda base: pl.BlockSpec((S, 128), lambda p, base=base: (0, base + p))
    out = jax.ShapeDtypeStruct((S, WIDTH), BF)
    return pl.pallas_call(
        body, name="branch_fwd",
        grid=(WIDTH // 128,),
        in_specs=[col(0), col(12), col(0), col(28), pl.BlockSpec(norm_w.shape, lambda p: (0, 0)),
                  col(32), col(36), col(40), col(44), pl.BlockSpec((None, None, 3, 128), lambda p: (p, layer, 0, 0))],
        out_specs=[col(0), col(0), col(0)],
        out_shape=[out, out, out],
        compiler_params=_params(dimension_semantics=("parallel",)),
    )(o_a, proj, o_b, proj, norm_w, proj, proj, proj, proj, conv_w)


def _put_columns(tiles, dproj_ref, firsts, sems):
    copies = [pltpu.make_async_copy(t, dproj_ref.at[:, pl.ds(pl.multiple_of(c, 128), t.shape[1])], sems.at[n])
              for n, (t, c) in enumerate(zip(tiles, firsts))]
    for cp in copies:
        cp.start()
    for cp in copies:
        cp.wait()


def _branch_bwd(proj, o_a, o_b, norm_w, conv_w, dy_a, dy_b, dy_c, dproj, layer):
    S = proj.shape[0]
    firsts = [WIDTH * n for n in (3, 7, 8, 9, 10, 11)]

    def dsilu(z):
        s = _sigmoid(z)
        return s * z, s * (1.0 + z * (1.0 - s))

    def body(oa_ref, za_ref, ob_ref, zb_ref, nw_ref, pre_ref, post_ref, u_ref, zc_ref, cw_ref, dya_ref, dyb_ref, dyc_ref, _,
             doa_ref, dob_ref, dnw_ref, dcw_ref, dproj_ref, stage, sems):
        dza_ref, dzb_ref, dpre_ref, dpost_ref, du_ref, dzc_ref = [stage.at[n] for n in range(6)]
        dya = dya_ref[...]
        sa, dsa = dsilu(za_ref[...])
        doa_ref[...] = dya * sa
        dza_ref[...] = (dya * oa_ref[...] * dsa).astype(BF)

        dyb = dyb_ref[...]
        ob = ob_ref[...]
        nw = nw_ref[layer:layer + 1, :]
        sb, dsb = dsilu(zb_ref[...])
        rn = lax.rsqrt(jnp.mean(ob * ob, axis=-1, keepdims=True) + RMS_EPS)
        on = ob * rn
        dzb_ref[...] = (dyb * on * nw * dsb).astype(BF)
        don_w = dyb * sb
        dnw_ref[...] = jnp.sum(don_w * on, axis=0, keepdims=True)
        don = don_w * nw
        dob_ref[...] = rn * (don - on * jnp.mean(don * on, axis=-1, keepdims=True))

        dyc = dyc_ref[...]
        pre, post, u = pre_ref[...], post_ref[...], u_ref[...]
        sc, dsc = dsilu(zc_ref[...])
        pu = pre * u
        pu1, pu2 = _shift_down(pu, 1), _shift_down(pu, 2)
        conv = cw_ref[2:3, :] * pu + cw_ref[1:2, :] * pu1 + cw_ref[0:1, :] * pu2
        dzc_ref[...] = (dyc * post * conv * dsc).astype(BF)
        dpost_ref[...] = (dyc * conv * sc).astype(BF)
        dconv = dyc * post * sc
        dcw_ref[0:1, :] = jnp.sum(dconv * pu2, axis=0, keepdims=True)
        dcw_ref[1:2, :] = jnp.sum(dconv * pu1, axis=0, keepdims=True)
        dcw_ref[2:3, :] = jnp.sum(dconv * pu, axis=0, keepdims=True)
        dpu = cw_ref[2:3, :] * dconv + cw_ref[1:2, :] * _shift_up(dconv, 1) + cw_ref[0:1, :] * _shift_up(dconv, 2)
        dpre_ref[...] = (dpu * u).astype(BF)
        du_ref[...] = (dpu * pre).astype(BF)
        lane0 = pl.program_id(0) * 128
        _put_columns([stage.at[n] for n in range(6)], dproj_ref, [c + lane0 for c in firsts], sems)

    col = lambda base: pl.BlockSpec((S, 128), lambda p, base=base: (0, base + p))
    f32 = jax.ShapeDtypeStruct((S, WIDTH), F32)
    hbm = pl.BlockSpec(memory_space=pl.ANY)
    return pl.pallas_call(
        body, name="branch_bwd",
        grid=(WIDTH // 128,),
        in_specs=[col(0), col(12), col(0), col(28), pl.BlockSpec(norm_w.shape, lambda p: (0, 0)),
                  col(32), col(36), col(40), col(44), pl.BlockSpec((None, None, 3, 128), lambda p: (p, layer, 0, 0)),
                  col(0), col(0), col(0), hbm],
        out_specs=[col(0), col(0), pl.BlockSpec((None, 1, 128), lambda p: (p, 0, 0)), pl.BlockSpec((None, 3, 128), lambda p: (p, 0, 0)), hbm],
        out_shape=[f32, f32, jax.ShapeDtypeStruct((WIDTH // 128, 1, 128), F32), jax.ShapeDtypeStruct((WIDTH // 128, 3, 128), F32),
                   jax.ShapeDtypeStruct(dproj.shape, dproj.dtype)],
        scratch_shapes=[pltpu.VMEM((6, S, 128), BF), pltpu.SemaphoreType.DMA((6,))],
        input_output_aliases={13: 4},
        compiler_params=_params(dimension_semantics=("arbitrary",)),
    )(o_a, proj, o_b, proj, norm_w, proj, proj, proj, proj, conv_w, dy_a, dy_b, dy_c, dproj)


def _branch_proj(y_refs, wb_ref):
    out = []
    for i in range(3):
        yv = y_refs[i][...]
        out.append(jnp.concatenate([_dot(yv, wb_ref[j, i]) for j in range(N_CHIPS)], axis=1))
    return out


def _merge_fwd(x, mod, proj, ys, wb, wo, ln_g, ln_b, layer, job=None, tm=256):
    S, D = x.shape
    tm = min(tm, S)

    def body(x_ref, mod_ref, ga_ref, gb_ref, gc_ref, ya_ref, yb_ref, yc_ref, wb_ref, wo_ref, g_ref, b_ref, xo_ref, mg_ref, y_ref):
        ps = _branch_proj((ya_ref, yb_ref, yc_ref), wb_ref)
        merged = _sigmoid(ga_ref[...]) * ps[0] + _sigmoid(gb_ref[...]) * ps[1] + _sigmoid(gc_ref[...]) * ps[2]
        mb = merged.astype(BF)
        mg_ref[...] = mb
        y = _dot(mb, wo_ref[...].reshape(D, D))
        y_ref[...] = y
        r = ALPHA * x_ref[...] + (1.0 + mod_ref[:, 2 * D:3 * D]) * y
        xn, _ = _standardize(r)
        xo_ref[...] = xn * g_ref[layer:layer + 1, :] + b_ref[layer:layer + 1, :]

    row = lambda w, c=0: pl.BlockSpec((tm, w), lambda i, c=c: (i, c))
    whole = lambda a: pl.BlockSpec(a.shape, lambda i, n=a.ndim: (0,) * n)
    return _hosted(
        body, job, (x, mod, proj, proj, proj, *ys, wb, wo, ln_g, ln_b), name="merge_fwd",
        grid=(S // tm,),
        in_specs=[row(D), whole(mod), row(D, 6), row(D, 7), row(D, 8), row(WIDTH), row(WIDTH), row(WIDTH), whole(wb),
                  whole(wo), whole(ln_g), whole(ln_b)],
        out_specs=[row(D), row(D), row(D)],
        out_shape=[jax.ShapeDtypeStruct((S, D), F32), jax.ShapeDtypeStruct((S, D), BF), jax.ShapeDtypeStruct((S, D), F32)],
        semantics=("parallel",))


def _merge_bwd(dxo, x, y, mod, proj, ys, wb, wo, ln_g, layer, tm=256):
    S, D = x.shape
    tm = min(tm, S)

    def body(dxo_ref, x_ref, y_ref, mod_ref, ga_ref, gb_ref, gc_ref, ya_ref, yb_ref, yc_ref, wb_ref, wo_ref, g_ref,
             dxr_ref, dy_ref, dp_ref, dg_ref, dya_ref, dyb_ref, dyc_ref, dlg_ref, dlb_ref, dgt_ref):
        @pl.when(pl.program_id(0) == 0)
        def _():
            dlg_ref[...] = jnp.zeros_like(dlg_ref)
            dlb_ref[...] = jnp.zeros_like(dlb_ref)
            dgt_ref[...] = jnp.zeros_like(dgt_ref)

        gate1 = 1.0 + mod_ref[:, 2 * D:3 * D]
        yv = y_ref[...]
        xn, rstd = _standardize(ALPHA * x_ref[...] + gate1 * yv)
        dxo = dxo_ref[...]
        dlg_ref[...] += jnp.sum(dxo * xn, axis=0, keepdims=True)
        dlb_ref[...] += jnp.sum(dxo, axis=0, keepdims=True)
        dr = _standardize_bwd(dxo * g_ref[layer:layer + 1, :], xn, rstd)
        dxr_ref[...] = ALPHA * dr
        dgt_ref[...] += jnp.sum(dr * yv, axis=0, keepdims=True)
        dyb = (gate1 * dr).astype(BF)
        dy_ref[...] = dyb
        dmerged = _dot_nt(dyb, wo_ref[...].reshape(D, D))
        ps = _branch_proj((ya_ref, yb_ref, yc_ref), wb_ref)
        quarter = D // N_CHIPS
        for i, (gate_ref, out_ref) in enumerate(((ga_ref, dya_ref), (gb_ref, dyb_ref), (gc_ref, dyc_ref))):
            sg = _sigmoid(gate_ref[...])
            dg_ref[:, i * D:(i + 1) * D] = (dmerged * ps[i] * sg * (1.0 - sg)).astype(BF)
            dp = (dmerged * sg).astype(BF)
            dp_ref[:, i * D:(i + 1) * D] = dp
            acc = None
            for j in range(N_CHIPS):
                t = _dot_nt(dp[:, j * quarter:(j + 1) * quarter], wb_ref[j, i])
                acc = t if acc is None else acc + t
            out_ref[...] = acc

    row = lambda w, c=0: pl.BlockSpec((tm, w), lambda i, c=c: (i, c))
    whole = lambda a: pl.BlockSpec(a.shape, lambda i, n=a.ndim: (0,) * n)
    vec = pl.BlockSpec((1, D), lambda i: (0, 0))
    sd = jax.ShapeDtypeStruct
    return pl.pallas_call(
        body, name="merge_bwd",
        grid=(S // tm,),
        in_specs=[row(D), row(D), row(D), whole(mod), row(D, 6), row(D, 7), row(D, 8), row(WIDTH), row(WIDTH), row(WIDTH), whole(wb),
                  whole(wo), whole(ln_g)],
        out_specs=[row(D), row(D), row(3 * D), row(3 * D, IN_COLS // (3 * D) - 1), row(WIDTH), row(WIDTH), row(WIDTH), vec, vec, vec],
        out_shape=[sd((S, D), F32), sd((S, D), BF), sd((S, 3 * D), BF), sd((S, IN_COLS), BF), sd((S, WIDTH), F32), sd((S, WIDTH), F32),
                   sd((S, WIDTH), F32), sd((1, D), F32), sd((1, D), F32), sd((1, D), F32)],
        compiler_params=_params(dimension_semantics=("arbitrary",)),
    )(dxo, x, y, mod, proj, proj, proj, *ys, wb, wo, ln_g)


def _loss_head(x, target, tm=512):
    S, D = x.shape
    tm = min(tm, S)

    def body(x_ref, t_ref, dx_ref, loss_ref):
        @pl.when(pl.program_id(0) == 0)
        def _():
            loss_ref[...] = jnp.zeros_like(loss_ref)

        err = x_ref[...] - t_ref[...]
        dx_ref[...] = err * (1.0 / D)
        loss_ref[...] += 0.5 * jnp.sum(jnp.mean(err * err, axis=-1, keepdims=True))

    row = pl.BlockSpec((tm, D), lambda i: (i, 0))
    return pl.pallas_call(
        body, name="loss_head",
        grid=(S // tm,),
        in_specs=[row, row],
        out_specs=[row, pl.BlockSpec((8, 128), lambda i: (0, 0))],
        out_shape=[jax.ShapeDtypeStruct((S, D), F32), jax.ShapeDtypeStruct((8, 128), F32)],
        compiler_params=_params(dimension_semantics=("arbitrary",)),
    )(x, target)


def _proj_bwd(dproj, wgs, x, mod, dx_res, job=None, tm=512, tk=768):
    S, D = x.shape
    tm = min(tm, S)
    n = len(wgs)
    w_args = list(wgs)
    if n == 1:
        per = wgs[0].shape[-1] // tk
        w_specs = [pl.BlockSpec((None, D, tk), lambda i, k: (k // per, 0, k % per))]
    else:
        assert wgs[0].shape[-1] == tk
        w_specs = [pl.BlockSpec((None, D, tk), lambda i, k, c=c: (jnp.minimum((k + n - 1 - c) // n, N_CHIPS - 1), 0, 0))
                   for c in range(n)]
    nk = IN_COLS // tk

    def body(dp_ref, *rest):
        w_refs, (x_ref, mod_ref, dxr_ref, dx_ref, dsh_ref, dsc_ref, acc) = rest[:n], rest[n:]
        i, k = pl.program_id(0), pl.program_id(1)

        @pl.when((i == 0) & (k == 0))
        def _():
            dsh_ref[...] = jnp.zeros_like(dsh_ref)
            dsc_ref[...] = jnp.zeros_like(dsc_ref)

        @pl.when(k == 0)
        def _():
            acc[...] = jnp.zeros_like(acc)

        for c in range(n):
            @pl.when(k % n == c)
            def _(c=c):
                acc[...] += _dot_nt(dp_ref[...], w_refs[c][...])

        @pl.when(k == nk - 1)
        def _():
            dh = acc[...]
            xs, rstd = _standardize(x_ref[...])
            dsh_ref[...] += jnp.sum(dh, axis=0, keepdims=True)
            dsc_ref[...] += jnp.sum(dh * xs, axis=0, keepdims=True)
            dx_ref[...] = _standardize_bwd(dh * (1.0 + mod_ref[:, D:2 * D]), xs, rstd) + dxr_ref[...]

    row = pl.BlockSpec((tm, D), lambda i, k: (i, 0))
    vec = pl.BlockSpec((1, D), lambda i, k: (0, 0))
    return _hosted(
        body, job, (dproj, *w_args, x, mod, dx_res), name="proj_bwd",
        grid=(S // tm, nk),
        in_specs=[pl.BlockSpec((tm, tk), lambda i, k: (i, k))] + w_specs + [row, pl.BlockSpec((1, 3 * D), lambda i, k: (0, 0)), row],
        out_specs=[row, vec, vec],
        out_shape=[jax.ShapeDtypeStruct((S, D), F32), jax.ShapeDtypeStruct((1, D), F32), jax.ShapeDtypeStruct((1, D), F32)],
        scratch_shapes=[pltpu.VMEM((tm, D), F32)],
        semantics=("arbitrary", "arbitrary"))


def _grad_w(a, b, mid, prev, name, tm, tn, b_col0=0):
    S, M = a.shape
    shape = prev.shape
    n_shard = shape[-1]
    per = n_shard // tn
    nm = M // tm
    lead = len(mid)

    def body(*refs):
        a_ref, b_ref, o_ref = refs[0], refs[1], refs[-1]
        o_ref[...] = _dot_tn(a_ref[...], b_ref[...]).astype(BF)

    in_specs = [pl.BlockSpec((S, tm), lambda m, n: (0, m)),
                pl.BlockSpec((S, tn), lambda m, n: (0, b_col0 // tn + n))]
    args = [a, b]
    aliases = {}
    if not isinstance(prev, jax.ShapeDtypeStruct):
        in_specs.append(pl.BlockSpec(memory_space=pl.ANY))
        args.append(prev)
        aliases = {2: 0}
    return pl.pallas_call(
        body, name=name,
        grid=(nm, N_CHIPS * per),
        in_specs=in_specs,
        out_specs=pl.BlockSpec((None,) + (None,) * lead + (tm, tn), lambda m, n: (n // per,) + tuple(mid) + (m, n % per)),
        out_shape=jax.ShapeDtypeStruct(shape, BF),
        input_output_aliases=aliases,
        compiler_params=_params(dimension_semantics=("parallel", "parallel")),
    )(*args)


def _all_gather8(x, name):
    R, N = x.shape

    def body(x_ref, out_ref, send_sems, recv_sems):
        mx, my, mc = lax.axis_index("x"), lax.axis_index("y"), lax.axis_index("c")
        me = 4 * mx + 2 * my + mc
        out_ref[me] = x_ref[...]
        copies = []
        for k in range(1, N_DEV):
            peer = (_flip(mx, k & 4), _flip(my, k & 2), _flip(mc, k & 1))
            cp = pltpu.make_async_remote_copy(src_ref=x_ref, dst_ref=out_ref.at[me], send_sem=send_sems.at[k - 1],
                                              recv_sem=recv_sems.at[k - 1], device_id=peer, device_id_type=MESH)
            cp.start()
            copies.append(cp)
        for cp in copies:
            cp.wait()

    return pl.pallas_call(
        body, name=name,
        in_specs=[pl.BlockSpec(memory_space=pltpu.VMEM)],
        out_specs=pl.BlockSpec(memory_space=pltpu.VMEM),
        out_shape=jax.ShapeDtypeStruct((N_DEV, R, N), F32),
        scratch_shapes=[pltpu.SemaphoreType.DMA((N_DEV - 1,)), pltpu.SemaphoreType.DMA((N_DEV - 1,))],
        compiler_params=_params(),
    )(x)


def _rows2d(a):
    return a.reshape(-1, a.shape[-1])


def _tile_rows(rows, cols, n_arrays):
    budget = (24 << 20) // (n_arrays * 2 * 4 * cols)
    if rows <= budget:
        return rows
    tm = 8
    for cand in range(8, budget + 1, 8):
        if rows % cand == 0:
            tm = cand
    return tm


SUM_ROWS = 256


def _sum_cores(g, sent, where):
    chips, lead, _, r, cols = g.shape
    tr = min(r, SUM_ROWS)

    def body(where_ref, g_ref, s_ref, out_ref):
        out_ref[...] = (g_ref[...].astype(F32) + s_ref[...].astype(F32)).astype(BF)

    spec = pl.BlockSpec((None, tr, cols), lambda i, j, where_ref: (i, j, 0))
    out = pl.pallas_call(
        body, name="sum_cores",
        grid_spec=pltpu.PrefetchScalarGridSpec(
            num_scalar_prefetch=1, grid=(chips * lead, r // tr),
            in_specs=[pl.BlockSpec((None, None, tr, cols), lambda i, j, where_ref: (i, where_ref[1], j, 0)), spec],
            out_specs=spec),
        out_shape=jax.ShapeDtypeStruct((chips * lead, r, cols), BF),
        compiler_params=_params(dimension_semantics=("parallel", "parallel")),
    )(where, g.reshape(chips * lead, 2, r, cols), sent.reshape(chips * lead, r, cols))
    return out.reshape(chips, lead, r, cols)


def _sum_chips(sums, got, where):
    _, lead, r, cols = sums.shape
    tr = min(r, SUM_ROWS)

    def body(where_ref, s_ref, g_ref, out_ref):
        out_ref[...] = ((s_ref[...].astype(F32) + g_ref[0].astype(F32)) + g_ref[1].astype(F32)) + g_ref[2].astype(F32)

    return pl.pallas_call(
        body, name="sum_chips",
        grid_spec=pltpu.PrefetchScalarGridSpec(
            num_scalar_prefetch=1, grid=(lead, r // tr),
            in_specs=[pl.BlockSpec((None, None, tr, cols), lambda i, j, where_ref: (where_ref[0], i, j, 0)),
                      pl.BlockSpec((N_CHIPS - 1, None, tr, cols), lambda i, j, where_ref: (0, i, j, 0))],
            out_specs=pl.BlockSpec((None, None, tr, cols), lambda i, j, where_ref: (i, where_ref[1], j, 0))),
        out_shape=jax.ShapeDtypeStruct((lead, 2, r, cols), F32),
        compiler_params=_params(dimension_semantics=("parallel", "parallel")),
    )(where, sums, got)


def _adamw(w, m, v, groups, name):
    shape = w.shape
    w2, m2, v2 = _rows2d(w), _rows2d(m), _rows2d(v)
    rows, cols = w2.shape
    ng = len(groups)
    n = len(groups[0])
    slab = rows // ng
    gs = [_rows2d(g) for grp in groups for g in grp]
    tm = _tile_rows(slab, cols, 7 + n)
    tiles = slab // tm
    c1 = 1.0 / (1.0 - ADAM_B1 ** ADAM_STEP)
    c2 = 1.0 / (1.0 - ADAM_B2 ** ADAM_STEP)

    def body(*refs):
        w_ref, m_ref, v_ref = refs[:3]
        g_refs = refs[3:3 + ng * n]
        go_ref, d_ref, mo_ref, vo_ref = refs[3 + ng * n:]
        which = pl.program_id(0)
        for s in range(ng):
            @pl.when(which == s)
            def _(s=s):
                g = g_refs[s * n][...]
                for r in g_refs[s * n + 1:(s + 1) * n]:
                    g = g + r[...]
                mn = ADAM_B1 * m_ref[...] + (1.0 - ADAM_B1) * g
                vn = ADAM_B2 * v_ref[...] + (1.0 - ADAM_B2) * (g * g)
                go_ref[...] = g
                mo_ref[...] = mn
                vo_ref[...] = vn
                d_ref[...] = -ADAM_LR * ((mn * c1) / (jnp.sqrt(vn * c2) + ADAM_EPS) + ADAM_WD * w_ref[...])

    spec = pl.BlockSpec((tm, cols), lambda s, i: (s * tiles + i, 0))
    g_specs = [pl.BlockSpec((tm, cols), lambda s, i, k=k: (jnp.where(s == k, i, jnp.where(s < k, 0, tiles - 1)), 0))
               for k in range(ng) for _ in range(n)]
    outs = pl.pallas_call(
        body, name=name,
        grid=(ng, tiles),
        in_specs=[spec] * 3 + g_specs,
        out_specs=[spec] * 4,
        out_shape=[jax.ShapeDtypeStruct((rows, cols), F32)] * 4,
        compiler_params=_params(dimension_semantics=("arbitrary", "arbitrary")),
    )(w2, m2, v2, *gs)
    return [o.reshape(shape) for o in outs]


def _lower_bounds(r0, r1):
    top = jnp.maximum(r0, r1)
    e0, e1 = jnp.exp(r0 - top), jnp.exp(r1 - top)
    p0, p1 = e0 / (e0 + e1), e1 / (e0 + e1)
    return p0 - p0, (p0 + p1) - p0


def _lbs_fwd(lb_raw):
    def body(lb_ref, out_ref):
        l0, l1 = _lower_bounds(lb_ref[0:1, :], lb_ref[1:2, :])
        out_ref[0:1, :] = l0
        out_ref[1:2, :] = l1

    return pl.pallas_call(body, name="lower_bounds", out_shape=jax.ShapeDtypeStruct(lb_raw.shape, F32), compiler_params=_params())(lb_raw)


def _mod_rows(c_all, w_mod, tn=768):
    _, D, cols = w_mod.shape

    def body(c_ref, w_ref, out_ref):
        out_ref[...] = _dot(c_ref[...].astype(BF), w_ref[...].astype(BF))

    return pl.pallas_call(
        body, name="mod_rows",
        grid=(DEPTH,),
        in_specs=[pl.BlockSpec((N_DEV, D), lambda l: (0, 0)), pl.BlockSpec((None, D, cols), lambda l: (l, 0, 0))],
        out_specs=pl.BlockSpec((N_DEV, cols), lambda l: (0, l)),
        out_shape=jax.ShapeDtypeStruct((N_DEV, DEPTH * cols), F32),
        compiler_params=_params(dimension_semantics=("parallel",)),
    )(c_all, w_mod)


def _grad_w_mod(c_all, dmod_cols):
    D = c_all.shape[1]
    cols = dmod_cols.shape[-1]

    def body(c_ref, d_ref, out_ref):
        out_ref[...] = _dot_tn(c_ref[...].astype(BF), d_ref[...].astype(BF))

    return pl.pallas_call(
        body, name="grad_w_mod",
        grid=(DEPTH,),
        in_specs=[pl.BlockSpec((N_DEV, D), lambda l: (0, 0)), pl.BlockSpec((None, N_DEV, cols), lambda l: (l, 0, 0))],
        out_specs=pl.BlockSpec((None, D, cols), lambda l: (l, 0, 0)),
        out_shape=jax.ShapeDtypeStruct((DEPTH, D, cols), F32),
        compiler_params=_params(dimension_semantics=("parallel",)),
    )(c_all, dmod_cols)


def _sum_devices(parts):
    _, R, N = parts.shape

    def body(p_ref, out_ref):
        acc = p_ref[0]
        for d in range(1, N_DEV):
            acc = acc + p_ref[d]
        out_ref[...] = acc

    return pl.pallas_call(body, name="sum_devices", out_shape=jax.ShapeDtypeStruct((R, N), F32), compiler_params=_params())(parts)


def _lbs_bwd(lb_raw, dl):
    def body(lb_ref, dl_ref, out_ref):
        _, pull = jax.vjp(_lower_bounds, lb_ref[0:1, :], lb_ref[1:2, :])
        d0, d1 = pull((dl_ref[0:1, :], dl_ref[1:2, :]))
        out_ref[0:1, :] = d0
        out_ref[1:2, :] = d1

    return pl.pallas_call(body, name="lower_bounds_bwd", out_shape=jax.ShapeDtypeStruct(lb_raw.shape, F32), compiler_params=_params())(lb_raw, dl)


def kernel(x, c, w_mod, b_mod, w_in, conv_w, hgrn_norm_w, lower_bounds, w_branch, w_out, ln_g, ln_b, loss_target, m_w_mod, m_b_mod, m_w_in, m_conv_w, m_hgrn_norm_w, m_lower_bounds, m_w_branch, m_w_out, m_ln_g, m_ln_b, v_w_mod, v_b_mod, v_w_in, v_conv_w, v_hgrn_norm_w, v_lower_bounds, v_w_branch, v_w_out, v_ln_g, v_ln_b):
    D = D_MODEL
    x0 = x[0]
    target = loss_target[0]
    S = x0.shape[0]
    mx, my, mc = lax.axis_index("x"), lax.axis_index("y"), lax.axis_index("c")
    chip = 2 * mx + my
    me = 2 * chip + mc
    mod_cols = 3 * D // N_CHIPS

    plan = _Plan(w_in.astype(BF), w_branch.astype(BF), w_out.astype(BF), chip, mc)
    n_conv = DEPTH * 3 * (WIDTH // N_CHIPS)
    first = jnp.concatenate([c, conv_w.reshape(1, n_conv), jnp.zeros((1, 2 * D - D - n_conv), F32)], axis=1)
    first = plan.first(first.reshape(8, 2 * D // 8)).reshape(N_DEV, 2 * D)

    c_all = first[:, :D]
    conv_all = first[:, D:D + n_conv].reshape(N_DEV, DEPTH, 3, WIDTH // N_CHIPS)[0::2]
    mod_part = _all_gather8(_mod_rows(c_all, w_mod), "gather_mod")[0::2]
    mod_part = lax.dynamic_index_in_dim(mod_part, me, axis=1, keepdims=False).reshape(N_CHIPS, DEPTH, mod_cols)
    mods = [(mod_part[:, l].reshape(1, 3 * D) + b_mod[l][None, :]) for l in range(DEPTH)]
    lbs = _lbs_fwd(lower_bounds).reshape(DEPTH, WIDTH // 128, 1, 128)
    loss_blk, dx, small = _local_step(x0, target, mods, lbs, conv_all, hgrn_norm_w, ln_g, ln_b, plan)

    n_mod, n_nw, n_lb, n_ln, n_cw = DEPTH * 3 * D, DEPTH * 128, DEPTH * WIDTH, DEPTH * D, DEPTH * 3 * WIDTH
    row = jnp.concatenate(
        [jnp.concatenate([small[l][0], small[l][1], small[l][2]], axis=1) for l in range(DEPTH)]
        + [jnp.sum(small[l][3], axis=0) for l in range(DEPTH)]
        + [small[l][4].reshape(1, WIDTH) for l in range(DEPTH)]
        + [small[l][5] for l in range(DEPTH)] + [small[l][6] for l in range(DEPTH)]
        + [jnp.transpose(small[l][7], (1, 0, 2)).reshape(1, 3 * WIDTH) for l in range(DEPTH)]
        + [loss_blk[0:1, :]], axis=1)
    n_row = row.shape[1]
    fold = -(-n_row // (8 * 128)) * 128
    rows = jnp.concatenate([row, jnp.zeros((1, 8 * fold - n_row), F32)], axis=1).reshape(8, fold)

    whole, gathered = plan.finish(rows)
    grads = {kind: [[whole[(kind, l)]] for l in range(DEPTH)] for kind in ("in", "br", "out")}

    off_nw = n_mod
    off_lb = off_nw + n_nw
    off_lng = off_lb + n_lb
    off_lnb = off_lng + n_ln
    off_cw = off_lnb + n_ln
    off_loss = off_cw + n_cw
    total = _sum_devices(gathered).reshape(1, 8 * fold)
    gathered = gathered.reshape(N_DEV, 1, 8 * fold)
    d_lower = _lbs_bwd(lower_bounds, total[0, off_lb:off_lng].reshape(DEPTH, WIDTH))
    loss = total[0, off_loss]
    d_b_mod = total[0, :n_mod].reshape(DEPTH, 3 * D)
    d_norm_w = total[0, off_nw:off_lb].reshape(DEPTH, 128)
    d_ln_g = total[0, off_lng:off_lnb].reshape(DEPTH, D)
    d_ln_b = total[0, off_lnb:off_cw].reshape(DEPTH, D)
    d_conv = total[0, off_cw:off_loss].reshape(DEPTH, 3, N_CHIPS, WIDTH // N_CHIPS)
    d_conv = lax.dynamic_index_in_dim(d_conv, chip, axis=2, keepdims=False)
    dmod_all = gathered[:, 0, :n_mod].reshape(N_DEV, DEPTH, N_CHIPS, mod_cols)
    dmod_cols = jnp.transpose(lax.dynamic_index_in_dim(dmod_all, chip, axis=2, keepdims=False), (1, 0, 2))
    d_w_mod = _grad_w_mod(c_all, dmod_cols)

    res = {}
    res["w_mod"] = _adamw(w_mod, m_w_mod, v_w_mod, [[d_w_mod]], "adamw_w_mod")
    res["b_mod"] = _adamw(b_mod, m_b_mod, v_b_mod, [[d_b_mod]], "adamw_b_mod")
    res["w_in"] = _adamw(w_in, m_w_in, v_w_in, grads["in"], "adamw_w_in")
    res["conv_w"] = _adamw(conv_w, m_conv_w, v_conv_w, [[d_conv]], "adamw_conv_w")
    res["hgrn_norm_w"] = _adamw(hgrn_norm_w, m_hgrn_norm_w, v_hgrn_norm_w, [[d_norm_w]], "adamw_norm_w")
    res["lower_bounds"] = _adamw(lower_bounds, m_lower_bounds, v_lower_bounds, [[d_lower]], "adamw_lower_bounds")
    res["w_branch"] = _adamw(w_branch, m_w_branch, v_w_branch, grads["br"], "adamw_w_branch")
    res["w_out"] = _adamw(w_out, m_w_out, v_w_out, grads["out"], "adamw_w_out")
    res["ln_g"] = _adamw(ln_g, m_ln_g, v_ln_g, [[d_ln_g]], "adamw_ln_g")
    res["ln_b"] = _adamw(ln_b, m_ln_b, v_ln_b, [[d_ln_b]], "adamw_ln_b")
    names = ["w_mod", "b_mod", "w_in", "conv_w", "hgrn_norm_w", "lower_bounds", "w_branch", "w_out", "ln_g", "ln_b"]
    return (loss, dx[None], *[res[n][0] for n in names], *[res[n][1] for n in names],
            *[res[n][2] for n in names], *[res[n][3] for n in names])


class _Plan:
    FIRST_CHUNKS = 3
    WINDOWS = ((0, 1152), (1152, 640), (1792, 512))

    def __init__(self, w_in, w_br, w_out, chip, core):
        self.local = {"in": w_in, "br": w_br, "out": w_out}
        self.chip, self.where = chip, jnp.stack([chip, core]).astype(jnp.int32)
        self.gathered, self.partial, self.grads, self.chip_sums, self.scattered, self.pending = {}, {}, {}, {}, {}, {}

    def chunks(self, l):
        return self.FIRST_CHUNKS if l == 0 else 1

    def w_in(self, l):
        return [self.gathered[("in", l, c)] for c in range(self.chunks(l))]

    def _shard(self, key):
        mine = self.local[key[0]][key[1]]
        if key[0] == "in":
            cols = mine.shape[-1] // self.chunks(key[1])
            mine = mine[:, key[2] * cols:(key[2] + 1) * cols]
        return mine

    def _slab(self, key):
        mine = _halves(self._shard(key))
        return lax.dynamic_update_slice(lax.empty((N_CHIPS,) + mine.shape, mine.dtype), mine[None], (self.chip, 0, 0, 0, 0))

    def _gather(self, keys):
        return ("gather", keys), _gather_job([self._slab(key) for key in keys])

    def _gather_window(self, key, n):
        slab = self._slab(key) if n == 0 else self.partial[key]
        return ("gather" if n == len(self.WINDOWS) - 1 else "gather_part", [key]), _gather_job([slab], self.WINDOWS[n])

    def _to_sibling(self, keys):
        return ("to_sibling", keys), _to_sibling_job([_halves(self.grads[key], 1) for key in keys])

    def _scatter(self, keys):
        return ("scatter", keys), _scatter_job([self.chip_sums[key] for key in keys])

    def job(self, stage, l, c=0):
        parts = []
        if stage == "proj_fwd":
            parts = [self._gather([("in", l, c + 1)] if c + 1 < self.chunks(l) else [("br", l), ("out", l)])]
        elif stage == "attn_fwd" and l + 1 < DEPTH:
            parts = [self._gather_window(("in", l + 1, 0), 0)]
        elif stage == "hgrn_fwd" and l + 1 < DEPTH:
            parts = [self._gather_window(("in", l + 1, 0), 1)]
        elif stage == "merge_fwd" and l + 1 < DEPTH:
            parts = [self._gather_window(("in", l + 1, 0), 2)]
        elif stage == "attn_bwd":
            parts = [self._to_sibling([("out", l), ("br", l)])] + ([self._scatter([("in", l + 1)])] if l + 1 < DEPTH else [])
        elif stage == "hgrn_bwd":
            parts = [self._scatter([("out", l), ("br", l)])]
        elif stage == "proj_bwd":
            parts = [self._to_sibling([("in", l)])] if l else [self._scatter([("in", 0)])]
        self.pending[(stage, l, c)] = [(tag, len(job.outs)) for tag, job in parts]
        return _join_jobs([job for _, job in parts])

    def done(self, stage, l, outs, c=0):
        if outs is None:
            return
        at = 0
        for (what, keys), n_outs in self.pending[(stage, l, c)]:
            mine, at = outs[at:at + n_outs], at + n_outs
            for n, key in enumerate(keys):
                if what == "gather":
                    self.gathered[key] = mine[n].reshape((N_CHIPS,) + self._shard(key).shape)
                elif what == "gather_part":
                    self.partial[key] = mine[n]
                elif what == "to_sibling":
                    self.chip_sums[key] = _sum_cores(_halves(self.grads[key], 1), mine[n], self.where)
                else:
                    self.scattered[key] = mine[n]

    def first(self, rows):
        tag, job = self._gather([("in", 0, 0)])
        self.pending[("first", 0, 0)] = [(tag, len(job.outs))]
        outs = _run_job(_join_jobs([job, _gather8_job(rows)]), "gather_first")
        self.done("first", 0, outs[:-1])
        return outs[-1]

    def took(self, key, grad):
        self.grads[key] = grad
        if key == ("in", 0):
            tag, job = self._to_sibling([key])
            self.pending[("took", 0, 0)] = [(tag, len(job.outs))]
            self.done("took", 0, _run_job(job, "to_sibling_last"))

    def finish(self, rows):
        keys = [(kind, l) for kind in ("in", "br", "out") for l in range(DEPTH)]
        halves = [_sum_chips(self.chip_sums[key], self.scattered[key], self.where) for key in keys]
        outs = _run_job(_join_jobs([_place_job(halves), _gather8_job(rows)]), "place_halves")
        return {key: w.reshape(self.grads[key].shape[1:]) for key, w in zip(keys, outs[:-1])}, outs[-1]


def _local_step(x0, target, mods, lbs, conv_all, hgrn_norm_w, ln_g, ln_b, plan):
    D = D_MODEL
    after, before = _attn_consts()
    hg_consts = _hgrn_consts()

    saved = []
    xl = x0
    for l in range(DEPTH):
        n = plan.chunks(l)
        (proj, h), got = _proj_fwd(xl, mods[l], plan.gathered[("in", l, 0)], plan.job("proj_fwd", l, 0), (0, n))
        plan.done("proj_fwd", l, got, 0)
        for c in range(1, n):
            (proj,), got = _proj_cols(h, plan.gathered[("in", l, c)], (c, n), proj, plan.job("proj_fwd", l, c))
            plan.done("proj_fwd", l, got, c)
        (o_a, tot), got = _attn_fwd(proj, after, plan.job("attn_fwd", l))
        plan.done("attn_fwd", l, got)
        (o_b, states), got = _hgrn_fwd(proj, lbs[l], hg_consts, plan.job("hgrn_fwd", l))
        plan.done("hgrn_fwd", l, got)
        ys = _branch_fwd(proj, o_a, o_b, hgrn_norm_w, conv_all, l)
        (x_next, merged, y), got = _merge_fwd(xl, mods[l], proj, ys, plan.gathered[("br", l)], plan.gathered[("out", l)], ln_g, ln_b, l,
                                              plan.job("merge_fwd", l))
        plan.done("merge_fwd", l, got)
        saved.append((xl, proj, h, o_a, tot, o_b, states, ys, merged, y))
        xl = x_next
    dx, loss_blk = _loss_head(xl, target)

    small = [None] * DEPTH
    for l in reversed(range(DEPTH)):
        xin, proj, h, o_a, tot, o_b, states, ys, merged, y = saved[l]
        dx_res, dy, dp, dproj, dy_a, dy_b, dy_c, dln_g, dln_b, dgate = _merge_bwd(
            dx, xin, y, mods[l], proj, ys, plan.gathered[("br", l)], plan.gathered[("out", l)], ln_g, l)
        plan.took(("out", l), _grad_w_out(merged, dy))
        g_br = jax.ShapeDtypeStruct((N_CHIPS, 3, WIDTH, D // N_CHIPS), BF)
        for i in range(3):
            g_br = _grad_w(ys[i], dp, (i,), g_br, "grad_w_branch", tm=WIDTH, tn=D // N_CHIPS, b_col0=i * D)
        plan.took(("br", l), g_br)
        d_oa, d_ob, dnorm_w, dconv_w, dproj = _branch_bwd(proj, o_a, o_b, hgrn_norm_w, conv_all, dy_a, dy_b, dy_c, dproj, l)
        (dproj,), got = _attn_bwd(proj, d_oa, tot, after, before, dproj, plan.job("attn_bwd", l))
        plan.done("attn_bwd", l, got)
        (dlb, dproj), got = _hgrn_bwd(proj, lbs[l], states, d_ob, hg_consts, dproj, plan.job("hgrn_bwd", l))
        plan.done("hgrn_bwd", l, got)
        plan.took(("in", l), _grad_w(h, dproj, (), jax.ShapeDtypeStruct((N_CHIPS, D, IN_COLS // N_CHIPS), BF), "grad_w_in", tm=512, tn=2304))
        (dx, dshift, dscale), got = _proj_bwd(dproj, plan.w_in(l), xin, mods[l], dx_res, plan.job("proj_bwd", l))
        plan.done("proj_bwd", l, got)
        small[l] = (dshift, dscale, dgate, dnorm_w, dlb, dln_g, dln_b, dconv_w)
    return loss_blk, dx, small


def _grad_w_out(merged, dy):
    S, D = merged.shape
    q = D // N_CHIPS

    def body(a_ref, b_ref, o_ref):
        o_ref[...] = _dot_tn(a_ref[...], b_ref[...]).astype(BF)

    return pl.pallas_call(
        body, name="grad_w_out",
        grid=(N_CHIPS,),
        in_specs=[pl.BlockSpec((S, q), lambda j: (0, j)), pl.BlockSpec((S, D), lambda j: (0, 0))],
        out_specs=pl.BlockSpec((None, q, D), lambda j: (j, 0, 0)),
        out_shape=jax.ShapeDtypeStruct((N_CHIPS, q, D), BF),
        compiler_params=_params(dimension_semantics=("parallel",)),
    )(merged, dy)
```

```python
import functools
import math

import numpy as np
import jax
import jax.numpy as jnp
from jax import lax
from jax.experimental import pallas as pl
from jax.experimental.pallas import tpu as pltpu

F32 = jnp.float32
BF = jnp.bfloat16
MESH = pl.DeviceIdType.MESH

DEPTH = 2
D_MODEL = 1024
WIDTH = 512
IN_COLS = 12 * WIDTH + 3 * D_MODEL
N_CHIPS = 4
N_DEV = 8
SB_BLOCK = 128
SB_HEAD_DIM = 64
HG_CHUNK = 128
HG_DIM = 128
LN_EPS = 1e-5
RMS_EPS = 1e-6
ALPHA = (2.0 * DEPTH) ** 0.25
ADAM_LR, ADAM_B1, ADAM_B2, ADAM_EPS, ADAM_WD, ADAM_STEP = 0.001, 0.9, 0.999, 1e-08, 0.01, 10
VMEM_LIMIT = 56 << 20


def _params(**kw):
    return pltpu.CompilerParams(vmem_limit_bytes=VMEM_LIMIT, **kw)


def _dot(a, b):
    return jnp.dot(a, b, preferred_element_type=F32)


def _dot_nt(a, b):
    return lax.dot_general(a, b, (((1,), (1,)), ((), ())), preferred_element_type=F32)


def _dot_tn(a, b):
    return lax.dot_general(a, b, (((0,), (0,)), ((), ())), preferred_element_type=F32)


def _sigmoid(x):
    return 1.0 / (1.0 + jnp.exp(-x))


def _silu(x):
    return x * _sigmoid(x)


def _split_dot(t, g, terms):
    acc = None
    rest = g
    for _ in range(terms):
        part = rest.astype(BF)
        rest = rest - part.astype(F32)
        d = _dot(t, part)
        acc = d if acc is None else acc + d
    return acc


def _standardize(x):
    mu = jnp.mean(x, axis=-1, keepdims=True)
    xc = x - mu
    var = jnp.mean(xc * xc, axis=-1, keepdims=True)
    rstd = lax.rsqrt(var + LN_EPS)
    return xc * rstd, rstd


def _standardize_bwd(dxs, xs, rstd):
    return rstd * (dxs - jnp.mean(dxs, axis=-1, keepdims=True) - xs * jnp.mean(dxs * xs, axis=-1, keepdims=True))


class _Job:
    def __init__(self, ins, outs, sems, make, alias=None):
        self.ins, self.outs, self.sems, self.make = list(ins), list(outs), list(sems), make
        self.alias = dict(alias or {})


def _join_jobs(jobs):
    jobs = [j for j in jobs if j is not None]
    if len(jobs) <= 1:
        return jobs[0] if jobs else None

    def make(ins, outs, sems):
        phases, i, o, s = [], 0, 0, 0
        for j in jobs:
            got = j.make(ins[i:i + len(j.ins)], outs[o:o + len(j.outs)], sems[s:s + len(j.sems)])
            i, o, s = i + len(j.ins), o + len(j.outs), s + len(j.sems)
            for n, phase in enumerate(got):
                if n == len(phases):
                    phases.append([])
                phases[n] += phase
        return phases

    alias, i, o = {}, 0, 0
    for j in jobs:
        alias.update({i + a: o + b for a, b in j.alias.items()})
        i, o = i + len(j.ins), o + len(j.outs)
    return _Job(sum([j.ins for j in jobs], []), sum([j.outs for j in jobs], []), sum([j.sems for j in jobs], []), make, alias)


def _flip(v, bit):
    return 1 - v if bit else v


def _halves(a, front=0):
    shape = a.shape
    lead = math.prod(shape[front:-2])
    return a.reshape(shape[:front] + (lead, 2, shape[-2] // 2, shape[-1]))


def _dma_sems(*shapes):
    return [pltpu.SemaphoreType.DMA(s) for s in shapes]


def _same(arrays):
    return [jax.ShapeDtypeStruct(a.shape, a.dtype) for a in arrays]


def _gather_job(slabs, window=None):
    n = len(slabs)
    cols = slice(None) if window is None else pl.ds(*window)

    def make(ins, outs, sems):
        send1, recv1, send2, recv2 = sems
        mx, my, mc = lax.axis_index("x"), lax.axis_index("y"), lax.axis_index("c")
        fetch, pass_on = [], []
        for a in range(n):
            ours = outs[a].at[2 * mx + my, :, mc, :, cols]
            for k in range(1, N_CHIPS):
                px, py = _flip(mx, k & 2), _flip(my, k & 1)
                fetch.append(pltpu.make_async_remote_copy(
                    src_ref=ours, dst_ref=ours, send_sem=send1.at[a, k - 1], recv_sem=recv1.at[a, k - 1],
                    device_id=(px, py, mc), device_id_type=MESH))
                theirs = outs[a].at[2 * px + py, :, mc, :, cols]
                pass_on.append(pltpu.make_async_remote_copy(
                    src_ref=theirs, dst_ref=theirs, send_sem=send2.at[a, k - 1], recv_sem=recv2.at[a, k - 1],
                    device_id=(mx, my, 1 - mc), device_id_type=MESH))
        return [fetch, pass_on]

    pairs = (n, N_CHIPS - 1)
    return _Job(slabs, _same(slabs), _dma_sems(pairs, pairs, pairs, pairs), make, {a: a for a in range(n)})


def _to_sibling_job(grads):
    n = len(grads)

    def make(ins, outs, sems):
        send_sems, recv_sems = sems
        mx, my, mc = lax.axis_index("x"), lax.axis_index("y"), lax.axis_index("c")
        return [[pltpu.make_async_remote_copy(
            src_ref=ins[a].at[:, :, 1 - mc], dst_ref=outs[a], send_sem=send_sems.at[a], recv_sem=recv_sems.at[a],
            device_id=(mx, my, 1 - mc), device_id_type=MESH) for a in range(n)]]

    outs = [jax.ShapeDtypeStruct(g.shape[:2] + g.shape[3:], g.dtype) for g in grads]
    return _Job(grads, outs, _dma_sems((n,), (n,)), make)


def _scatter_job(sums):
    n = len(sums)

    def make(ins, outs, sems):
        send_sems, recv_sems = sems
        mx, my, mc = lax.axis_index("x"), lax.axis_index("y"), lax.axis_index("c")
        copies = []
        for a in range(n):
            for k in range(1, N_CHIPS):
                px, py = _flip(mx, k & 2), _flip(my, k & 1)
                copies.append(pltpu.make_async_remote_copy(
                    src_ref=ins[a].at[2 * px + py], dst_ref=outs[a].at[k - 1], send_sem=send_sems.at[a, k - 1],
                    recv_sem=recv_sems.at[a, k - 1], device_id=(px, py, mc), device_id_type=MESH))
        return [copies]

    pairs = (n, N_CHIPS - 1)
    return _Job(sums, [jax.ShapeDtypeStruct((N_CHIPS - 1,) + s.shape[1:], s.dtype) for s in sums], _dma_sems(pairs, pairs), make)


def _place_job(wholes):
    n = len(wholes)

    def make(ins, outs, sems):
        send_sems, recv_sems = sems
        mx, my, mc = lax.axis_index("x"), lax.axis_index("y"), lax.axis_index("c")
        copies = []
        for a in range(n):
            here = outs[a].at[:, mc]
            copies.append(pltpu.make_async_remote_copy(src_ref=here, dst_ref=here, send_sem=send_sems.at[a], recv_sem=recv_sems.at[a],
                                                       device_id=(mx, my, 1 - mc), device_id_type=MESH))
        return [copies]

    return _Job(wholes, _same(wholes), _dma_sems((n,), (n,)), make, {a: a for a in range(n)})


def _gather8_job(x):
    def make(ins, outs, sems):
        local_sem, send_sems, recv_sems = sems
        mx, my, mc = lax.axis_index("x"), lax.axis_index("y"), lax.axis_index("c")
        here = outs[0].at[4 * mx + 2 * my + mc]
        copies = [pltpu.make_async_copy(ins[0], here, local_sem.at[0])]
        for k in range(1, N_DEV):
            peer = (_flip(mx, k & 4), _flip(my, k & 2), _flip(mc, k & 1))
            copies.append(pltpu.make_async_remote_copy(src_ref=ins[0], dst_ref=here, send_sem=send_sems.at[k - 1],
                                                       recv_sem=recv_sems.at[k - 1], device_id=peer, device_id_type=MESH))
        return [copies]

    return _Job([x], [jax.ShapeDtypeStruct((N_DEV,) + x.shape, x.dtype)], _dma_sems((1,), (N_DEV - 1,), (N_DEV - 1,)), make)


def _run_phases(phases, first=0):
    for n, phase in enumerate(phases):
        if n >= first:
            for cp in phase:
                cp.start()
        for cp in phase:
            cp.wait()


def _run_job(job, name):
    k_in, k_out = len(job.ins), len(job.outs)

    def body(*refs):
        _run_phases(job.make(refs[:k_in], refs[k_in:k_in + k_out], refs[k_in + k_out:]))

    hbm = pl.BlockSpec(memory_space=pl.ANY)
    return pl.pallas_call(body, name=name, in_specs=[hbm] * k_in, out_specs=[hbm] * k_out, out_shape=job.outs,
                          scratch_shapes=job.sems, input_output_aliases=job.alias, compiler_params=_params())(*job.ins)


def _hosted(body, job, args, *, name, grid, in_specs, out_specs, out_shape, scratch_shapes=(), semantics, aliases=None):
    in_specs, out_specs, out_shape, scratch = list(in_specs), list(out_specs), list(out_shape), list(scratch_shapes)
    aliases = dict(aliases or {})
    if job is None:
        outs = pl.pallas_call(body, name=name, grid=grid, in_specs=in_specs, out_specs=out_specs, out_shape=out_shape,
                              scratch_shapes=scratch, input_output_aliases=aliases,
                              compiler_params=_params(dimension_semantics=semantics))(*args)
        return list(outs), None
    n_in, n_out, n_scr, k_in, k_out = len(in_specs), len(out_specs), len(scratch), len(job.ins), len(job.outs)

    def wrapped(*refs):
        ins, rest = refs[:n_in], refs[n_in:]
        job_ins, rest = rest[:k_in], rest[k_in:]
        outs, rest = rest[:n_out], rest[n_out:]
        job_outs, rest = rest[:k_out], rest[k_out:]
        scr, sems = rest[:n_scr], rest[n_scr:]
        ids = [pl.program_id(a) for a in range(len(grid))]
        first = functools.reduce(jnp.logical_and, [i == 0 for i in ids])
        last = functools.reduce(jnp.logical_and, [i == g - 1 for i, g in zip(ids, grid)])

        @pl.when(first)
        def _():
            for cp in job.make(job_ins, job_outs, sems)[0]:
                cp.start()

        body(*ins, *outs, *scr)

        @pl.when(last)
        def _():
            _run_phases(job.make(job_ins, job_outs, sems), first=1)

    hbm = pl.BlockSpec(memory_space=pl.ANY)
    outs = pl.pallas_call(
        wrapped, name=name, grid=grid, in_specs=in_specs + [hbm] * k_in, out_specs=out_specs + [hbm] * k_out,
        out_shape=out_shape + job.outs, scratch_shapes=scratch + job.sems,
        input_output_aliases={**aliases, **{n_in + i: n_out + o for i, o in job.alias.items()}},
        compiler_params=_params(dimension_semantics=("arbitrary",) * len(grid)))(*args, *job.ins)
    return list(outs[:n_out]), list(outs[n_out:])


def _proj_fwd(x, mod, wg, job=None, chunk=(0, 1), tm=512):
    S, D = x.shape
    tm = min(tm, S)
    tn = wg.shape[-1]
    c, n = chunk

    rows = S // tm

    def body(x_ref, mod_ref, w_ref, proj_ref, h_ref, hs):
        mine = pl.ds(pl.multiple_of(pl.program_id(1) * tm, tm), tm)

        @pl.when(pl.program_id(0) == 0)
        def _():
            xs, _ = _standardize(x_ref[...])
            h = xs * (1.0 + mod_ref[:, D:2 * D]) + mod_ref[:, 0:D]
            hb = h.astype(BF)
            hs[mine, :] = hb
            h_ref[...] = hb

        proj_ref[...] = _dot(hs[mine, :], w_ref[...])

    once = lambda j, i: jnp.where(j == 0, i, rows - 1)
    return _hosted(
        body, job, (x, mod, wg), name="proj_fwd",
        grid=(N_CHIPS, rows),
        in_specs=[pl.BlockSpec((tm, D), lambda j, i: (once(j, i), 0)),
                  pl.BlockSpec((1, 3 * D), lambda j, i: (0, 0)),
                  pl.BlockSpec((None, D, tn), lambda j, i: (j, 0, 0))],
        out_specs=[pl.BlockSpec((tm, tn), lambda j, i: (i, j * n + c)),
                   pl.BlockSpec((tm, D), lambda j, i: (once(j, i), 0))],
        out_shape=[jax.ShapeDtypeStruct((S, IN_COLS), F32), jax.ShapeDtypeStruct((S, D), BF)],
        scratch_shapes=[pltpu.VMEM((S, D), BF)],
        semantics=("arbitrary", "arbitrary"))


def _proj_cols(h, wg, chunk, proj, job=None, tm=512):
    S, D = h.shape
    tm = min(tm, S)
    tn = wg.shape[-1]
    c, n = chunk

    def body(h_ref, w_ref, prev_ref, proj_ref):
        mine = pl.ds(pl.multiple_of(pl.program_id(1) * tm, tm), tm)
        proj_ref[...] = _dot(h_ref[mine, :], w_ref[...])

    in_specs = [pl.BlockSpec((S, D), lambda j, i: (0, 0)), pl.BlockSpec((None, D, tn), lambda j, i: (j, 0, 0)),
                pl.BlockSpec(memory_space=pl.ANY)]
    return _hosted(body, job, (h, wg, proj), name="proj_cols", grid=(N_CHIPS, S // tm), in_specs=in_specs,
                   out_specs=[pl.BlockSpec((tm, tn), lambda j, i: (i, j * n + c))],
                   out_shape=[jax.ShapeDtypeStruct(proj.shape, proj.dtype)],
                   semantics=("arbitrary", "arbitrary"), aliases={2: 0})


SB_ROWS = 256
SB_KEYS = 256


def _attn_consts():
    j = np.arange(SB_KEYS)[:, None]
    s = np.arange(SB_KEYS)[None, :]
    from_here = np.concatenate([(j >= s), (j >= s)], axis=0).astype(np.float32)
    return jnp.asarray(from_here, BF), jnp.asarray((j <= s).astype(np.float32), BF)


def _hi_lo(x):
    hi = lax.bitcast_convert_type(lax.bitcast_convert_type(x, jnp.uint32) & jnp.uint32(0xFFFF0000), F32)
    return hi.astype(BF), (x - hi).astype(BF)


def _sums_r(x, t2):
    hi, lo = _hi_lo(x)
    return _dot(jnp.concatenate([hi, lo], axis=1), t2)


def _all_lanes(col, lanes):
    return jnp.broadcast_to(col, (col.shape[0], lanes))


def _attn_rows(ref, r0, rows, lanes, head0, scale=None):
    v = ref[pl.ds(r0, rows), lanes]
    if scale is not None:
        v = v * scale
    return jnp.concatenate([jnp.where(head0, v, 0.0), jnp.where(head0, 0.0, v)], axis=0).astype(BF)


SB_PAIRS_FWD = 4
SB_PAIRS_BWD = 2


def _attn_specs(S, n_pairs):
    return lambda base: pl.BlockSpec((S, n_pairs * SB_BLOCK), lambda p, base=base: (0, base // n_pairs + p))


def _attn_scores(q2n, k_ref, lanes, kj, t2, from_here_ref, masked):
    c0 = pl.multiple_of(kj * SB_KEYS, SB_KEYS)
    kb = k_ref[pl.ds(c0, SB_KEYS), lanes].astype(BF)
    zn = _dot_nt(q2n, kb)
    lsb = jnp.minimum(zn, 0.0) - jnp.log(1.0 + jnp.exp(-jnp.abs(zn)))
    valid = None
    if masked:
        valid = (lax.broadcasted_iota(jnp.int32, zn.shape, 1) + kj * SB_KEYS) < t2
        lsb = jnp.where(valid, lsb, 0.0)
    return c0, kb, zn, valid, lsb, _sums_r(lsb, from_here_ref[...])


def _attn_fwd(proj, from_here, job=None):
    S = proj.shape[0]
    TQ = SB_ROWS
    assert S % TQ == 0 and SB_KEYS == TQ
    scale = SB_HEAD_DIM ** -0.5
    n_pairs = SB_PAIRS_FWD
    pairs = range(n_pairs)
    lanes = [pl.ds(p * SB_BLOCK, SB_BLOCK) for p in pairs]

    def body(q_ref, k_ref, v_ref, from_here_ref, o_ref, tot_ref, run, acc):
        head0 = lax.broadcasted_iota(jnp.int32, (1, 2 * SB_HEAD_DIM), 1) < SB_HEAD_DIM

        def qloop(qi, _):
            r0 = pl.multiple_of(qi * TQ, TQ)
            q2n = [_attn_rows(q_ref, r0, TQ, lanes[p], head0, -scale) for p in pairs]
            trow = lax.broadcasted_iota(jnp.int32, (TQ, SB_KEYS), 0) + qi * TQ
            t2 = jnp.concatenate([trow, trow], axis=0)
            run[...] = jnp.zeros_like(run)
            acc[...] = jnp.zeros_like(acc)

            def step(kj, masked):
                got = [_attn_scores(q2n[p], k_ref, lanes[p], kj, t2, from_here_ref, masked) for p in pairs]
                for p in pairs:
                    c0, _, zn, valid, _, sums = got[p]
                    r = run[p]
                    e = sums - zn + jnp.concatenate([r, r], axis=1)
                    if masked:
                        e = jnp.where(valid, e, -jnp.inf)
                    acc[p] += _dot(jnp.exp(e).astype(BF), v_ref[pl.ds(c0, SB_KEYS), lanes[p]].astype(BF))
                    run[p] = r + _all_lanes(sums[:, 0:1], SB_BLOCK)

            step(qi, True)

            def below(n, _):
                step(qi - 1 - n, False)
                return 0

            lax.fori_loop(0, qi, below, 0)
            for p in pairs:
                o_ref[pl.ds(r0, TQ), lanes[p]] = jnp.where(head0, acc[p, 0:TQ, :], acc[p, TQ:2 * TQ, :])
                tot_ref[p, 0, pl.ds(r0, TQ), :] = run[p, 0:TQ, :]
                tot_ref[p, 1, pl.ds(r0, TQ), :] = run[p, TQ:2 * TQ, :]
            return 0

        lax.fori_loop(0, S // TQ, qloop, 0)

    col = _attn_specs(S, n_pairs)
    state = pltpu.VMEM((n_pairs, 2 * TQ, SB_BLOCK), F32)
    return _hosted(
        body, job, (proj, proj, proj, from_here), name="attn_fwd",
        grid=(WIDTH // (n_pairs * SB_BLOCK),),
        in_specs=[col(0), col(4), col(8), pl.BlockSpec(from_here.shape, lambda p: (0, 0))],
        out_specs=[col(0), pl.BlockSpec((n_pairs, 2, S, 128), lambda p: (p, 0, 0, 0))],
        out_shape=[jax.ShapeDtypeStruct((S, WIDTH), F32), jax.ShapeDtypeStruct((WIDTH // 128, 2, S, 128), F32)],
        scratch_shapes=[state, state],
        semantics=("parallel",))


def _attn_bwd(proj, d_o, tot, from_here, up_to, dproj, job=None):
    S = proj.shape[0]
    TQ = SB_ROWS
    assert S % TQ == 0 and SB_KEYS == TQ
    scale = SB_HEAD_DIM ** -0.5
    n_pairs = SB_PAIRS_BWD
    pairs = range(n_pairs)
    lanes = [pl.ds(p * SB_BLOCK, SB_BLOCK) for p in pairs]

    def body(q_ref, k_ref, v_ref, do_ref, tot_ref, from_here_ref, up_to_ref, _, dproj_ref, pre, cum, dq_acc, dk_acc, dv_acc, stage, sems):
        dq_ref, dk_ref, dv_ref = stage.at[0], stage.at[1], stage.at[2]
        head0 = lax.broadcasted_iota(jnp.int32, (1, 2 * SB_HEAD_DIM), 1) < SB_HEAD_DIM
        dk_acc[...] = jnp.zeros_like(dk_acc)
        dv_acc[...] = jnp.zeros_like(dv_acc)

        def qloop(qi, _):
            r0 = pl.multiple_of(qi * TQ, TQ)
            q2n = [_attn_rows(q_ref, r0, TQ, lanes[p], head0, -scale) for p in pairs]
            do2 = [_attn_rows(do_ref, r0, TQ, lanes[p], head0) for p in pairs]
            trow = lax.broadcasted_iota(jnp.int32, (TQ, SB_KEYS), 0) + qi * TQ
            t2 = jnp.concatenate([trow, trow], axis=0)
            for p in pairs:
                pre[p, 0:TQ, :] = tot_ref[p, 0, pl.ds(r0, TQ), :]
                pre[p, TQ:2 * TQ, :] = tot_ref[p, 1, pl.ds(r0, TQ), :]
            cum[...] = jnp.zeros_like(cum)
            dq_acc[...] = jnp.zeros_like(dq_acc)

            def step(kj, masked):
                got = [_attn_scores(q2n[p], k_ref, lanes[p], kj, t2, from_here_ref, masked) for p in pairs]
                for p in pairs:
                    c0, kb, zn, valid, lsb, sums = got[p]
                    later = pre[p] - _all_lanes(sums[:, 0:1], SB_BLOCK)
                    pre[p] = later
                    e = sums - zn + jnp.concatenate([later, later], axis=1)
                    sig = jnp.exp(lsb - zn)
                    if masked:
                        e = jnp.where(valid, e, -jnp.inf)
                        sig = jnp.where(valid, sig, 0.0)
                    a = jnp.exp(e)
                    w = _dot_nt(do2[p], v_ref[pl.ds(c0, SB_KEYS), lanes[p]].astype(BF)) * a
                    upto = _dot(w.astype(BF), up_to_ref[...])
                    c = cum[p]
                    dz = w - sig * (upto + jnp.concatenate([c, c], axis=1))
                    cum[p] = c + _all_lanes(upto[:, SB_KEYS - 1:SB_KEYS], SB_BLOCK)
                    dzb = dz.astype(BF)
                    dq_acc[p] += _dot(dzb, kb)
                    dk_acc[pl.ds(c0, SB_KEYS), lanes[p]] += _dot_tn(dzb, q2n[p])
                    dv_acc[pl.ds(c0, SB_KEYS), lanes[p]] += _dot_tn(a.astype(BF), do2[p])

            def below(kj, _):
                step(kj, False)
                return 0

            lax.fori_loop(0, qi, below, 0)
            step(qi, True)
            for p in pairs:
                dq_ref[pl.ds(r0, TQ), lanes[p]] = (jnp.where(head0, dq_acc[p, 0:TQ, :], dq_acc[p, TQ:2 * TQ, :]) * scale).astype(BF)
            return 0

        lax.fori_loop(0, S // TQ, qloop, 0)
        dk_ref[...] = (-dk_acc[...]).astype(BF)
        dv_ref[...] = dv_acc[...].astype(BF)
        lane0 = pl.program_id(0) * (n_pairs * SB_BLOCK)
        _put_columns([dq_ref, dk_ref, dv_ref], dproj_ref, [WIDTH * n + lane0 for n in range(3)], sems)

    col = _attn_specs(S, n_pairs)
    whole = lambda a: pl.BlockSpec(a.shape, lambda p: (0, 0))
    hbm = pl.BlockSpec(memory_space=pl.ANY)
    state = pltpu.VMEM((n_pairs, 2 * TQ, SB_BLOCK), F32)
    grads = pltpu.VMEM((S, n_pairs * SB_BLOCK), F32)
    return _hosted(
        body, job, (proj, proj, proj, d_o, tot, from_here, up_to, dproj), name="attn_bwd",
        grid=(WIDTH // (n_pairs * SB_BLOCK),),
        in_specs=[col(0), col(4), col(8), col(0), pl.BlockSpec((n_pairs, 2, S, 128), lambda p: (p, 0, 0, 0)), whole(from_here),
                  whole(up_to), hbm],
        out_specs=[hbm],
        out_shape=[jax.ShapeDtypeStruct(dproj.shape, dproj.dtype)],
        scratch_shapes=[state, state, state, grads, grads, pltpu.VMEM((3, S, n_pairs * SB_BLOCK), BF), pltpu.SemaphoreType.DMA((3,))],
        semantics=("arbitrary",), aliases={7: 0})


HG_LEVELS = tuple(HG_CHUNK >> n for n in range(1, HG_CHUNK.bit_length()))


def _hgrn_consts():
    C = HG_CHUNK
    t = np.arange(C)[:, None]
    s = np.arange(C)[None, :]
    rows = [(s <= t), (s > t)]
    masks = [(t == s)]
    for m in HG_LEVELS:
        two = 2 * m
        mid = (t // two) * two + m
        right = (t % two) >= m
        rows.append((right & (s >= mid) & (s <= t)) | ((~right) & (s > t) & (s <= mid - 1)))
        masks.append(((t // two) == (s // two)) & right & ((s % two) < m))
    tri = np.concatenate(rows, axis=0).astype(np.float32)
    twice = lambda a: jnp.asarray(np.concatenate([a, a], axis=1), BF)
    return (twice(tri), twice(tri.T), jnp.asarray(np.stack(masks).astype(np.float32), F32))


HG_SUM_BLOCKS = 2 + len(HG_LEVELS)


def _split_rows(g):
    hi = g.astype(BF)
    return jnp.concatenate([hi, (g - hi.astype(F32)).astype(BF)], axis=0)


def _hgrn_sum_blocks(e):
    C = HG_CHUNK
    blocks = tuple(e[n * C:(n + 1) * C] for n in range(HG_SUM_BLOCKS))
    return blocks + (jnp.broadcast_to(e[C - 1:C], (HG_DIM, e.shape[1])),)


@jax.custom_vjp
def _hgrn_sums(tri, tri_t, g):
    return _hgrn_sum_blocks(_split_dot(tri[:, :HG_CHUNK], g, 2))


def _hgrn_sums_fwd(tri, tri_t, g):
    return _hgrn_sum_blocks(_dot(tri, _split_rows(g))), (tri, tri_t)


def _hgrn_sums_bwd(res, ds):
    tri, tri_t = res
    C = HG_CHUNK
    last = lax.broadcasted_iota(jnp.int32, (C, 1), 0) == C - 1
    prefix = ds[0] + jnp.where(last, jnp.sum(ds[-1], axis=0, keepdims=True), 0.0)
    d = jnp.concatenate((prefix,) + tuple(ds[1:-1]), axis=0)
    return jnp.zeros_like(tri), jnp.zeros_like(tri_t), _dot(tri_t[:, :d.shape[0]], d.astype(BF))


_hgrn_sums.defvjp(_hgrn_sums_fwd, _hgrn_sums_bwd)


def _bf_dot(a, b):
    return _dot(a.astype(BF), b.astype(BF))


def _bf_dot_nt(a, b):
    return _dot_nt(a.astype(BF), b.astype(BF))


def _bf_dot_tn(a, b):
    return _dot_tn(a.astype(BF), b.astype(BF))


@jax.custom_vjp
def _mm(a, b):
    return _bf_dot(a, b)


_mm.defvjp(lambda a, b: (_bf_dot(a, b), (a, b)), lambda r, ct: (_bf_dot_nt(ct, r[1]), _bf_dot_tn(r[0], ct)))


@jax.custom_vjp
def _mm_nt(a, b):
    return _bf_dot_nt(a, b)


_mm_nt.defvjp(lambda a, b: (_bf_dot_nt(a, b), (a, b)), lambda r, ct: (_bf_dot(ct, r[1]), _bf_dot_tn(ct, r[0])))


@jax.custom_vjp
def _mm_tn(a, b):
    return _bf_dot_tn(a, b)


_mm_tn.defvjp(lambda a, b: (_bf_dot_tn(a, b), (a, b)), lambda r, ct: (_bf_dot_nt(r[1], ct), _bf_dot(r[0], ct)))


def _hgrn_chunk(tri, tri_t, masks, qraw, fpre, v, st, lb):
    q = _silu(qraw)
    f = lb + (1.0 - lb) * _sigmoid(fpre)
    k = 1.0 - f
    e = _hgrn_sums(tri, tri_t, jnp.log(f))
    prefix, suffix, whole = e[0], e[1], e[-1]
    scores = masks[0] * _mm_nt(q, k)
    for n in range(len(HG_LEVELS)):
        decay = jnp.exp(e[2 + n])
        scores = scores + masks[n + 1] * _mm_nt(q * decay, k * decay)
    o = _mm_nt(q * jnp.exp(prefix), st) + _mm(scores, v)
    st_new = st * jnp.exp(whole) + _mm_tn(v, k * jnp.exp(suffix))
    return o, st_new


HG_HEADS_PER_STEP = 4
HG_LANES = HG_HEADS_PER_STEP * HG_DIM


def _hgrn_specs(S, consts):
    col = lambda base: pl.BlockSpec((S, HG_LANES), lambda p, base=base: (0, base // HG_HEADS_PER_STEP + p))
    whole = [pl.BlockSpec(a.shape, lambda p, n=a.ndim: (0,) * n) for a in consts]
    return col, whole


def _hgrn_fwd(proj, lbs, consts, job=None):
    S = proj.shape[0]
    nc = S // HG_CHUNK
    heads = range(HG_HEADS_PER_STEP)

    def body(q_ref, f_ref, i_ref, lb_ref, tri_ref, trit_ref, mask_ref, o_ref, st_ref):
        tri, tri_t = tri_ref[...], trit_ref[...]
        masks = [mask_ref[n] for n in range(len(HG_LEVELS) + 1)]

        def chunk(ci, sts):
            r0 = pl.multiple_of(ci * HG_CHUNK, HG_CHUNK)
            rows = pl.ds(r0, HG_CHUNK)
            new = []
            for hd in heads:
                lanes = pl.ds(hd * HG_DIM, HG_DIM)
                st_ref[hd, ci] = sts[hd]
                o, st_new = _hgrn_chunk(tri, tri_t, masks, q_ref[rows, lanes], f_ref[rows, lanes], i_ref[rows, lanes], sts[hd], lb_ref[hd])
                o_ref[rows, lanes] = o
                new.append(st_new)
            return tuple(new)

        lax.fori_loop(0, nc, chunk, tuple(jnp.zeros((HG_DIM, HG_DIM), F32) for _ in heads))

    col, whole = _hgrn_specs(S, consts)
    return _hosted(
        body, job, (proj, proj, proj, lbs, *consts), name="hgrn_fwd",
        grid=(WIDTH // HG_LANES,),
        in_specs=[col(16), col(20), col(24), pl.BlockSpec((HG_HEADS_PER_STEP, 1, 128), lambda p: (p, 0, 0))] + whole,
        out_specs=[col(0), pl.BlockSpec((HG_HEADS_PER_STEP, nc, HG_DIM, HG_DIM), lambda p: (p, 0, 0, 0))],
        out_shape=[jax.ShapeDtypeStruct((S, WIDTH), F32), jax.ShapeDtypeStruct((WIDTH // 128, nc, HG_DIM, HG_DIM), F32)],
        semantics=("parallel",))


def _hgrn_bwd(proj, lbs, states, d_o, consts, dproj, job=None):
    S = proj.shape[0]
    nc = S // HG_CHUNK

    def body(q_ref, f_ref, i_ref, lb_ref, st_ref, do_ref, tri_ref, trit_ref, mask_ref, _, dlb_ref, dproj_ref, stage, sems):
        dq_ref, df_ref, di_ref = stage.at[0], stage.at[1], stage.at[2]
        masks = [mask_ref[n] for n in range(len(HG_LEVELS) + 1)]
        fn = functools.partial(_hgrn_chunk, tri_ref[...], trit_ref[...], masks)
        heads = range(HG_HEADS_PER_STEP)

        def chunk(n, carry):
            ci = nc - 1 - n
            r0 = pl.multiple_of(ci * HG_CHUNK, HG_CHUNK)
            rows = pl.ds(r0, HG_CHUNK)
            new = []
            for hd in heads:
                d_st, dlb = carry[hd]
                lanes = pl.ds(hd * HG_DIM, HG_DIM)
                _, pull = jax.vjp(fn, q_ref[rows, lanes], f_ref[rows, lanes], i_ref[rows, lanes], st_ref[hd, ci], lb_ref[hd])
                dq, df, di, d_prev, dl = pull((do_ref[rows, lanes], d_st))
                dq_ref[rows, lanes] = dq.astype(BF)
                df_ref[rows, lanes] = df.astype(BF)
                di_ref[rows, lanes] = di.astype(BF)
                new.append((d_prev, dlb + dl))
            return tuple(new)

        zero = (jnp.zeros((HG_DIM, HG_DIM), F32), jnp.zeros((1, HG_DIM), F32))
        done = lax.fori_loop(0, nc, chunk, tuple(zero for _ in heads))
        for hd in heads:
            dlb_ref[hd] = done[hd][1]
        lane0 = pl.program_id(0) * HG_LANES
        _put_columns([dq_ref, df_ref, di_ref], dproj_ref, [WIDTH * n + lane0 for n in (4, 5, 6)], sems)

    col, whole = _hgrn_specs(S, consts)
    head = pl.BlockSpec((HG_HEADS_PER_STEP, 1, 128), lambda p: (p, 0, 0))
    hbm = pl.BlockSpec(memory_space=pl.ANY)
    n_in = 6 + len(consts)
    return _hosted(
        body, job, (proj, proj, proj, lbs, states, d_o, *consts, dproj), name="hgrn_bwd",
        grid=(WIDTH // HG_LANES,),
        in_specs=[col(16), col(20), col(24), head, pl.BlockSpec((HG_HEADS_PER_STEP, nc, HG_DIM, HG_DIM), lambda p: (p, 0, 0, 0)), col(0)]
        + whole + [hbm],
        out_specs=[head, hbm],
        out_shape=[jax.ShapeDtypeStruct((WIDTH // 128, 1, 128), F32), jax.ShapeDtypeStruct(dproj.shape, dproj.dtype)],
        scratch_shapes=[pltpu.VMEM((3, S, HG_LANES), BF), pltpu.SemaphoreType.DMA((3,))],
        semantics=("arbitrary",), aliases={n_in: 1})


def _shift_down(x, n):
    rows = lax.broadcasted_iota(jnp.int32, x.shape, 0)
    return jnp.where(rows >= n, pltpu.roll(x, n, 0), 0.0)


def _shift_up(x, n):
    S = x.shape[0]
    rows = lax.broadcasted_iota(jnp.int32, x.shape, 0)
    return jnp.where(rows < S - n, pltpu.roll(x, S - n, 0), 0.0)


def _branch_fwd(proj, o_a, o_b, norm_w, conv_w, layer):
    S = proj.shape[0]

    def body(oa_ref, za_ref, ob_ref, zb_ref, nw_ref, pre_ref, post_ref, u_ref, zc_ref, cw_ref, ya_ref, yb_ref, yc_ref):
        ya_ref[...] = (oa_ref[...] * _silu(za_ref[...])).astype(BF)
        ob = ob_ref[...]
        rn = lax.rsqrt(jnp.mean(ob * ob, axis=-1, keepdims=True) + RMS_EPS)
        yb_ref[...] = (ob * rn * nw_ref[layer:layer + 1, :] * _silu(zb_ref[...])).astype(BF)
        pu = pre_ref[...] * u_ref[...]
        conv = cw_ref[2:3, :] * pu + cw_ref[1:2, :] * _shift_down(pu, 1) + cw_ref[0:1, :] * _shift_down(pu, 2)
        yc_ref[...] = (post_ref[...] * conv * _silu(zc_ref[...])).astype(BF)

    col = lambda base: pl.BlockSpec((S, 128), lambda p, base=base: (0, base + p))
    out = jax.ShapeDtypeStruct((S, WIDTH), BF)
    return pl.pallas_call(
        body, name="branch_fwd",
        grid=(WIDTH // 128,),
        in_specs=[col(0), col(12), col(0), col(28), pl.BlockSpec(norm_w.shape, lambda p: (0, 0)),
                  col(32), col(36), col(40), col(44), pl.BlockSpec((None, None, 3, 128), lambda p: (p, layer, 0, 0))],
        out_specs=[col(0), col(0), col(0)],
        out_shape=[out, out, out],
        compiler_params=_params(dimension_semantics=("parallel",)),
    )(o_a, proj, o_b, proj, norm_w, proj, proj, proj, proj, conv_w)


def _put_columns(tiles, dproj_ref, firsts, sems):
    copies = [pltpu.make_async_copy(t, dproj_ref.at[:, pl.ds(pl.multiple_of(c, 128), t.shape[1])], sems.at[n])
              for n, (t, c) in enumerate(zip(tiles, firsts))]
    for cp in copies:
        cp.start()
    for cp in copies:
        cp.wait()


def _branch_bwd(proj, o_a, o_b, norm_w, conv_w, dy_a, dy_b, dy_c, dproj, layer):
    S = proj.shape[0]
    firsts = [WIDTH * n for n in (3, 7, 8, 9, 10, 11)]

    def dsilu(z):
        s = _sigmoid(z)
        return s * z, s * (1.0 + z * (1.0 - s))

    def body(oa_ref, za_ref, ob_ref, zb_ref, nw_ref, pre_ref, post_ref, u_ref, zc_ref, cw_ref, dya_ref, dyb_ref, dyc_ref, _,
             doa_ref, dob_ref, dnw_ref, dcw_ref, dproj_ref, stage, sems):
        dza_ref, dzb_ref, dpre_ref, dpost_ref, du_ref, dzc_ref = [stage.at[n] for n in range(6)]
        dya = dya_ref[...]
        sa, dsa = dsilu(za_ref[...])
        doa_ref[...] = dya * sa
        dza_ref[...] = (dya * oa_ref[...] * dsa).astype(BF)

        dyb = dyb_ref[...]
        ob = ob_ref[...]
        nw = nw_ref[layer:layer + 1, :]
        sb, dsb = dsilu(zb_ref[...])
        rn = lax.rsqrt(jnp.mean(ob * ob, axis=-1, keepdims=True) + RMS_EPS)
        on = ob * rn
        dzb_ref[...] = (dyb * on * nw * dsb).astype(BF)
        don_w = dyb * sb
        dnw_ref[...] = jnp.sum(don_w * on, axis=0, keepdims=True)
        don = don_w * nw
        dob_ref[...] = rn * (don - on * jnp.mean(don * on, axis=-1, keepdims=True))

        dyc = dyc_ref[...]
        pre, post, u = pre_ref[...], post_ref[...], u_ref[...]
        sc, dsc = dsilu(zc_ref[...])
        pu = pre * u
        pu1, pu2 = _shift_down(pu, 1), _shift_down(pu, 2)
        conv = cw_ref[2:3, :] * pu + cw_ref[1:2, :] * pu1 + cw_ref[0:1, :] * pu2
        dzc_ref[...] = (dyc * post * conv * dsc).astype(BF)
        dpost_ref[...] = (dyc * conv * sc).astype(BF)
        dconv = dyc * post * sc
        dcw_ref[0:1, :] = jnp.sum(dconv * pu2, axis=0, keepdims=True)
        dcw_ref[1:2, :] = jnp.sum(dconv * pu1, axis=0, keepdims=True)
        dcw_ref[2:3, :] = jnp.sum(dconv * pu, axis=0, keepdims=True)
        dpu = cw_ref[2:3, :] * dconv + cw_ref[1:2, :] * _shift_up(dconv, 1) + cw_ref[0:1, :] * _shift_up(dconv, 2)
        dpre_ref[...] = (dpu * u).astype(BF)
        du_ref[...] = (dpu * pre).astype(BF)
        lane0 = pl.program_id(0) * 128
        _put_columns([stage.at[n] for n in range(6)], dproj_ref, [c + lane0 for c in firsts], sems)

    col = lambda base: pl.BlockSpec((S, 128), lambda p, base=base: (0, base + p))
    f32 = jax.ShapeDtypeStruct((S, WIDTH), F32)
    hbm = pl.BlockSpec(memory_space=pl.ANY)
    return pl.pallas_call(
        body, name="branch_bwd",
        grid=(WIDTH // 128,),
        in_specs=[col(0), col(12), col(0), col(28), pl.BlockSpec(norm_w.shape, lambda p: (0, 0)),
                  col(32), col(36), col(40), col(44), pl.BlockSpec((None, None, 3, 128), lambda p: (p, layer, 0, 0)),
                  col(0), col(0), col(0), hbm],
        out_specs=[col(0), col(0), pl.BlockSpec((None, 1, 128), lambda p: (p, 0, 0)), pl.BlockSpec((None, 3, 128), lambda p: (p, 0, 0)), hbm],
        out_shape=[f32, f32, jax.ShapeDtypeStruct((WIDTH // 128, 1, 128), F32), jax.ShapeDtypeStruct((WIDTH // 128, 3, 128), F32),
                   jax.ShapeDtypeStruct(dproj.shape, dproj.dtype)],
        scratch_shapes=[pltpu.VMEM((6, S, 128), BF), pltpu.SemaphoreType.DMA((6,))],
        input_output_aliases={13: 4},
        compiler_params=_params(dimension_semantics=("arbitrary",)),
    )(o_a, proj, o_b, proj, norm_w, proj, proj, proj, proj, conv_w, dy_a, dy_b, dy_c, dproj)


def _branch_proj(y_refs, wb_ref):
    out = []
    for i in range(3):
        yv = y_refs[i][...]
        out.append(jnp.concatenate([_dot(yv, wb_ref[j, i]) for j in range(N_CHIPS)], axis=1))
    return out


def _merge_fwd(x, mod, proj, ys, wb, wo, ln_g, ln_b, layer, job=None, tm=256):
    S, D = x.shape
    tm = min(tm, S)

    def body(x_ref, mod_ref, ga_ref, gb_ref, gc_ref, ya_ref, yb_ref, yc_ref, wb_ref, wo_ref, g_ref, b_ref, xo_ref, mg_ref, y_ref):
        ps = _branch_proj((ya_ref, yb_ref, yc_ref), wb_ref)
        merged = _sigmoid(ga_ref[...]) * ps[0] + _sigmoid(gb_ref[...]) * ps[1] + _sigmoid(gc_ref[...]) * ps[2]
        mb = merged.astype(BF)
        mg_ref[...] = mb
        y = _dot(mb, wo_ref[...].reshape(D, D))
        y_ref[...] = y
        r = ALPHA * x_ref[...] + (1.0 + mod_ref[:, 2 * D:3 * D]) * y
        xn, _ = _standardize(r)
        xo_ref[...] = xn * g_ref[layer:layer + 1, :] + b_ref[layer:layer + 1, :]

    row = lambda w, c=0: pl.BlockSpec((tm, w), lambda i, c=c: (i, c))
    whole = lambda a: pl.BlockSpec(a.shape, lambda i, n=a.ndim: (0,) * n)
    return _hosted(
        body, job, (x, mod, proj, proj, proj, *ys, wb, wo, ln_g, ln_b), name="merge_fwd",
        grid=(S // tm,),
        in_specs=[row(D), whole(mod), row(D, 6), row(D, 7), row(D, 8), row(WIDTH), row(WIDTH), row(WIDTH), whole(wb),
                  whole(wo), whole(ln_g), whole(ln_b)],
        out_specs=[row(D), row(D), row(D)],
        out_shape=[jax.ShapeDtypeStruct((S, D), F32), jax.ShapeDtypeStruct((S, D), BF), jax.ShapeDtypeStruct((S, D), F32)],
        semantics=("parallel",))


def _merge_bwd(dxo, x, y, mod, proj, ys, wb, wo, ln_g, layer, tm=256):
    S, D = x.shape
    tm = min(tm, S)

    def body(dxo_ref, x_ref, y_ref, mod_ref, ga_ref, gb_ref, gc_ref, ya_ref, yb_ref, yc_ref, wb_ref, wo_ref, g_ref,
             dxr_ref, dy_ref, dp_ref, dg_ref, dya_ref, dyb_ref, dyc_ref, dlg_ref, dlb_ref, dgt_ref):
        @pl.when(pl.program_id(0) == 0)
        def _():
            dlg_ref[...] = jnp.zeros_like(dlg_ref)
            dlb_ref[...] = jnp.zeros_like(dlb_ref)
            dgt_ref[...] = jnp.zeros_like(dgt_ref)

        gate1 = 1.0 + mod_ref[:, 2 * D:3 * D]
        yv = y_ref[...]
        xn, rstd = _standardize(ALPHA * x_ref[...] + gate1 * yv)
        dxo = dxo_ref[...]
        dlg_ref[...] += jnp.sum(dxo * xn, axis=0, keepdims=True)
        dlb_ref[...] += jnp.sum(dxo, axis=0, keepdims=True)
        dr = _standardize_bwd(dxo * g_ref[layer:layer + 1, :], xn, rstd)
        dxr_ref[...] = ALPHA * dr
        dgt_ref[...] += jnp.sum(dr * yv, axis=0, keepdims=True)
        dyb = (gate1 * dr).astype(BF)
        dy_ref[...] = dyb
        dmerged = _dot_nt(dyb, wo_ref[...].reshape(D, D))
        ps = _branch_proj((ya_ref, yb_ref, yc_ref), wb_ref)
        quarter = D // N_CHIPS
        for i, (gate_ref, out_ref) in enumerate(((ga_ref, dya_ref), (gb_ref, dyb_ref), (gc_ref, dyc_ref))):
            sg = _sigmoid(gate_ref[...])
            dg_ref[:, i * D:(i + 1) * D] = (dmerged * ps[i] * sg * (1.0 - sg)).astype(BF)
            dp = (dmerged * sg).astype(BF)
            dp_ref[:, i * D:(i + 1) * D] = dp
            acc = None
            for j in range(N_CHIPS):
                t = _dot_nt(dp[:, j * quarter:(j + 1) * quarter], wb_ref[j, i])
                acc = t if acc is None else acc + t
            out_ref[...] = acc

    row = lambda w, c=0: pl.BlockSpec((tm, w), lambda i, c=c: (i, c))
    whole = lambda a: pl.BlockSpec(a.shape, lambda i, n=a.ndim: (0,) * n)
    vec = pl.BlockSpec((1, D), lambda i: (0, 0))
    sd = jax.ShapeDtypeStruct
    return pl.pallas_call(
        body, name="merge_bwd",
        grid=(S // tm,),
        in_specs=[row(D), row(D), row(D), whole(mod), row(D, 6), row(D, 7), row(D, 8), row(WIDTH), row(WIDTH), row(WIDTH), whole(wb),
                  whole(wo), whole(ln_g)],
        out_specs=[row(D), row(D), row(3 * D), row(3 * D, IN_COLS // (3 * D) - 1), row(WIDTH), row(WIDTH), row(WIDTH), vec, vec, vec],
        out_shape=[sd((S, D), F32), sd((S, D), BF), sd((S, 3 * D), BF), sd((S, IN_COLS), BF), sd((S, WIDTH), F32), sd((S, WIDTH), F32),
                   sd((S, WIDTH), F32), sd((1, D), F32), sd((1, D), F32), sd((1, D), F32)],
        compiler_params=_params(dimension_semantics=("arbitrary",)),
    )(dxo, x, y, mod, proj, proj, proj, *ys, wb, wo, ln_g)


def _loss_head(x, target, tm=512):
    S, D = x.shape
    tm = min(tm, S)

    def body(x_ref, t_ref, dx_ref, loss_ref):
        @pl.when(pl.program_id(0) == 0)
        def _():
            loss_ref[...] = jnp.zeros_like(loss_ref)

        err = x_ref[...] - t_ref[...]
        dx_ref[...] = err * (1.0 / D)
        loss_ref[...] += 0.5 * jnp.sum(jnp.mean(err * err, axis=-1, keepdims=True))

    row = pl.BlockSpec((tm, D), lambda i: (i, 0))
    return pl.pallas_call(
        body, name="loss_head",
        grid=(S // tm,),
        in_specs=[row, row],
        out_specs=[row, pl.BlockSpec((8, 128), lambda i: (0, 0))],
        out_shape=[jax.ShapeDtypeStruct((S, D), F32), jax.ShapeDtypeStruct((8, 128), F32)],
        compiler_params=_params(dimension_semantics=("arbitrary",)),
    )(x, target)


def _proj_bwd(dproj, wgs, x, mod, dx_res, job=None, tm=512, tk=768):
    S, D = x.shape
    tm = min(tm, S)
    n = len(wgs)
    w_args = list(wgs)
    if n == 1:
        per = wgs[0].shape[-1] // tk
        w_specs = [pl.BlockSpec((None, D, tk), lambda k, i: (k // per, 0, k % per))]
    else:
        assert wgs[0].shape[-1] == tk
        w_specs = [pl.BlockSpec((None, D, tk), lambda k, i, c=c: (jnp.minimum((k + n - 1 - c) // n, N_CHIPS - 1), 0, 0))
                   for c in range(n)]
    nk = IN_COLS // tk

    def body(dp_ref, *rest):
        w_refs, (x_ref, mod_ref, dxr_ref, dx_ref, dsh_ref, dsc_ref, acc) = rest[:n], rest[n:]
        k, i = pl.program_id(0), pl.program_id(1)
        mine = pl.ds(pl.multiple_of(i * tm, tm), tm)

        @pl.when((i == 0) & (k == 0))
        def _():
            dsh_ref[...] = jnp.zeros_like(dsh_ref)
            dsc_ref[...] = jnp.zeros_like(dsc_ref)

        @pl.when(k == 0)
        def _():
            acc[mine, :] = jnp.zeros((tm, D), F32)

        for c in range(n):
            @pl.when(k % n == c)
            def _(c=c):
                acc[mine, :] += _dot_nt(dp_ref[...], w_refs[c][...])

        @pl.when(k == nk - 1)
        def _():
            dh = acc[mine, :]
            xs, rstd = _standardize(x_ref[...])
            dsh_ref[...] += jnp.sum(dh, axis=0, keepdims=True)
            dsc_ref[...] += jnp.sum(dh * xs, axis=0, keepdims=True)
            dx_ref[...] = _standardize_bwd(dh * (1.0 + mod_ref[:, D:2 * D]), xs, rstd) + dxr_ref[...]

    row = pl.BlockSpec((tm, D), lambda k, i: (jnp.where(k == nk - 1, i, 0), 0))
    vec = pl.BlockSpec((1, D), lambda k, i: (0, 0))
    return _hosted(
        body, job, (dproj, *w_args, x, mod, dx_res), name="proj_bwd",
        grid=(nk, S // tm),
        in_specs=[pl.BlockSpec((tm, tk), lambda k, i: (i, k))] + w_specs + [row, pl.BlockSpec((1, 3 * D), lambda k, i: (0, 0)), row],
        out_specs=[row, vec, vec],
        out_shape=[jax.ShapeDtypeStruct((S, D), F32), jax.ShapeDtypeStruct((1, D), F32), jax.ShapeDtypeStruct((1, D), F32)],
        scratch_shapes=[pltpu.VMEM((S, D), F32)],
        semantics=("arbitrary", "arbitrary"))


def _grad_w(a, b, mid, prev, name, tm, tn, b_col0=0):
    S, M = a.shape
    shape = prev.shape
    n_shard = shape[-1]
    per = n_shard // tn
    nm = M // tm
    lead = len(mid)

    def body(*refs):
        a_ref, b_ref, o_ref = refs[0], refs[1], refs[-1]
        o_ref[...] = _dot_tn(a_ref[...], b_ref[...]).astype(BF)

    in_specs = [pl.BlockSpec((S, tm), lambda m, n: (0, m)),
                pl.BlockSpec((S, tn), lambda m, n: (0, b_col0 // tn + n))]
    args = [a, b]
    aliases = {}
    if not isinstance(prev, jax.ShapeDtypeStruct):
        in_specs.append(pl.BlockSpec(memory_space=pl.ANY))
        args.append(prev)
        aliases = {2: 0}
    return pl.pallas_call(
        body, name=name,
        grid=(nm, N_CHIPS * per),
        in_specs=in_specs,
        out_specs=pl.BlockSpec((None,) + (None,) * lead + (tm, tn), lambda m, n: (n // per,) + tuple(mid) + (m, n % per)),
        out_shape=jax.ShapeDtypeStruct(shape, BF),
        input_output_aliases=aliases,
        compiler_params=_params(dimension_semantics=("parallel", "parallel")),
    )(*args)


def _all_gather8(x, name):
    R, N = x.shape

    def body(x_ref, out_ref, send_sems, recv_sems):
        mx, my, mc = lax.axis_index("x"), lax.axis_index("y"), lax.axis_index("c")
        me = 4 * mx + 2 * my + mc
        out_ref[me] = x_ref[...]
        copies = []
        for k in range(1, N_DEV):
            peer = (_flip(mx, k & 4), _flip(my, k & 2), _flip(mc, k & 1))
            cp = pltpu.make_async_remote_copy(src_ref=x_ref, dst_ref=out_ref.at[me], send_sem=send_sems.at[k - 1],
                                              recv_sem=recv_sems.at[k - 1], device_id=peer, device_id_type=MESH)
            cp.start()
            copies.append(cp)
        for cp in copies:
            cp.wait()

    return pl.pallas_call(
        body, name=name,
        in_specs=[pl.BlockSpec(memory_space=pltpu.VMEM)],
        out_specs=pl.BlockSpec(memory_space=pltpu.VMEM),
        out_shape=jax.ShapeDtypeStruct((N_DEV, R, N), F32),
        scratch_shapes=[pltpu.SemaphoreType.DMA((N_DEV - 1,)), pltpu.SemaphoreType.DMA((N_DEV - 1,))],
        compiler_params=_params(),
    )(x)


def _rows2d(a):
    return a.reshape(-1, a.shape[-1])


def _tile_rows(rows, cols, n_arrays):
    budget = (24 << 20) // (n_arrays * 2 * 4 * cols)
    if rows <= budget:
        return rows
    tm = 8
    for cand in range(8, budget + 1, 8):
        if rows % cand == 0:
            tm = cand
    return tm


SUM_ROWS = 256


def _sum_cores(g, sent, where):
    chips, lead, _, r, cols = g.shape
    tr = min(r, SUM_ROWS)

    def body(where_ref, g_ref, s_ref, out_ref):
        out_ref[...] = (g_ref[...].astype(F32) + s_ref[...].astype(F32)).astype(BF)

    spec = pl.BlockSpec((None, tr, cols), lambda i, j, where_ref: (i, j, 0))
    out = pl.pallas_call(
        body, name="sum_cores",
        grid_spec=pltpu.PrefetchScalarGridSpec(
            num_scalar_prefetch=1, grid=(chips * lead, r // tr),
            in_specs=[pl.BlockSpec((None, None, tr, cols), lambda i, j, where_ref: (i, where_ref[1], j, 0)), spec],
            out_specs=spec),
        out_shape=jax.ShapeDtypeStruct((chips * lead, r, cols), BF),
        compiler_params=_params(dimension_semantics=("parallel", "parallel")),
    )(where, g.reshape(chips * lead, 2, r, cols), sent.reshape(chips * lead, r, cols))
    return out.reshape(chips, lead, r, cols)


def _sum_chips(sums, got, where):
    _, lead, r, cols = sums.shape
    tr = min(r, SUM_ROWS)

    def body(where_ref, s_ref, g_ref, out_ref):
        out_ref[...] = ((s_ref[...].astype(F32) + g_ref[0].astype(F32)) + g_ref[1].astype(F32)) + g_ref[2].astype(F32)

    return pl.pallas_call(
        body, name="sum_chips",
        grid_spec=pltpu.PrefetchScalarGridSpec(
            num_scalar_prefetch=1, grid=(lead, r // tr),
            in_specs=[pl.BlockSpec((None, None, tr, cols), lambda i, j, where_ref: (where_ref[0], i, j, 0)),
                      pl.BlockSpec((N_CHIPS - 1, None, tr, cols), lambda i, j, where_ref: (0, i, j, 0))],
            out_specs=pl.BlockSpec((None, None, tr, cols), lambda i, j, where_ref: (i, where_ref[1], j, 0))),
        out_shape=jax.ShapeDtypeStruct((lead, 2, r, cols), F32),
        compiler_params=_params(dimension_semantics=("parallel", "parallel")),
    )(where, sums, got)


def _adamw(w, m, v, groups, name):
    shape = w.shape
    w2, m2, v2 = _rows2d(w), _rows2d(m), _rows2d(v)
    rows, cols = w2.shape
    ng = len(groups)
    n = len(groups[0])
    slab = rows // ng
    gs = [_rows2d(g) for grp in groups for g in grp]
    tm = _tile_rows(slab, cols, 7 + n)
    tiles = slab // tm
    c1 = 1.0 / (1.0 - ADAM_B1 ** ADAM_STEP)
    c2 = 1.0 / (1.0 - ADAM_B2 ** ADAM_STEP)

    def body(*refs):
        w_ref, m_ref, v_ref = refs[:3]
        g_refs = refs[3:3 + ng * n]
        go_ref, d_ref, mo_ref, vo_ref = refs[3 + ng * n:]
        which = pl.program_id(0)
        for s in range(ng):
            @pl.when(which == s)
            def _(s=s):
                g = g_refs[s * n][...]
                for r in g_refs[s * n + 1:(s + 1) * n]:
                    g = g + r[...]
                mn = ADAM_B1 * m_ref[...] + (1.0 - ADAM_B1) * g
                vn = ADAM_B2 * v_ref[...] + (1.0 - ADAM_B2) * (g * g)
                go_ref[...] = g
                mo_ref[...] = mn
                vo_ref[...] = vn
                d_ref[...] = -ADAM_LR * ((mn * c1) / (jnp.sqrt(vn * c2) + ADAM_EPS) + ADAM_WD * w_ref[...])

    spec = pl.BlockSpec((tm, cols), lambda s, i: (s * tiles + i, 0))
    g_specs = [pl.BlockSpec((tm, cols), lambda s, i, k=k: (jnp.where(s == k, i, jnp.where(s < k, 0, tiles - 1)), 0))
               for k in range(ng) for _ in range(n)]
    outs = pl.pallas_call(
        body, name=name,
        grid=(ng, tiles),
        in_specs=[spec] * 3 + g_specs,
        out_specs=[spec] * 4,
        out_shape=[jax.ShapeDtypeStruct((rows, cols), F32)] * 4,
        compiler_params=_params(dimension_semantics=("arbitrary", "arbitrary")),
    )(w2, m2, v2, *gs)
    return [o.reshape(shape) for o in outs]


def _lower_bounds(r0, r1):
    top = jnp.maximum(r0, r1)
    e0, e1 = jnp.exp(r0 - top), jnp.exp(r1 - top)
    p0, p1 = e0 / (e0 + e1), e1 / (e0 + e1)
    return p0 - p0, (p0 + p1) - p0


def _lbs_fwd(lb_raw):
    def body(lb_ref, out_ref):
        l0, l1 = _lower_bounds(lb_ref[0:1, :], lb_ref[1:2, :])
        out_ref[0:1, :] = l0
        out_ref[1:2, :] = l1

    return pl.pallas_call(body, name="lower_bounds", out_shape=jax.ShapeDtypeStruct(lb_raw.shape, F32), compiler_params=_params())(lb_raw)


def _mod_rows(c_all, w_mod, tn=768):
    _, D, cols = w_mod.shape

    def body(c_ref, w_ref, out_ref):
        out_ref[...] = _dot(c_ref[...].astype(BF), w_ref[...].astype(BF))

    return pl.pallas_call(
        body, name="mod_rows",
        grid=(DEPTH,),
        in_specs=[pl.BlockSpec((N_DEV, D), lambda l: (0, 0)), pl.BlockSpec((None, D, cols), lambda l: (l, 0, 0))],
        out_specs=pl.BlockSpec((N_DEV, cols), lambda l: (0, l)),
        out_shape=jax.ShapeDtypeStruct((N_DEV, DEPTH * cols), F32),
        compiler_params=_params(dimension_semantics=("parallel",)),
    )(c_all, w_mod)


def _grad_w_mod(c_all, dmod_cols):
    D = c_all.shape[1]
    cols = dmod_cols.shape[-1]

    def body(c_ref, d_ref, out_ref):
        out_ref[...] = _dot_tn(c_ref[...].astype(BF), d_ref[...].astype(BF))

    return pl.pallas_call(
        body, name="grad_w_mod",
        grid=(DEPTH,),
        in_specs=[pl.BlockSpec((N_DEV, D), lambda l: (0, 0)), pl.BlockSpec((None, N_DEV, cols), lambda l: (l, 0, 0))],
        out_specs=pl.BlockSpec((None, D, cols), lambda l: (l, 0, 0)),
        out_shape=jax.ShapeDtypeStruct((DEPTH, D, cols), F32),
        compiler_params=_params(dimension_semantics=("parallel",)),
    )(c_all, dmod_cols)


def _sum_devices(parts):
    _, R, N = parts.shape

    def body(p_ref, out_ref):
        acc = p_ref[0]
        for d in range(1, N_DEV):
            acc = acc + p_ref[d]
        out_ref[...] = acc

    return pl.pallas_call(body, name="sum_devices", out_shape=jax.ShapeDtypeStruct((R, N), F32), compiler_params=_params())(parts)


def _lbs_bwd(lb_raw, dl):
    def body(lb_ref, dl_ref, out_ref):
        _, pull = jax.vjp(_lower_bounds, lb_ref[0:1, :], lb_ref[1:2, :])
        d0, d1 = pull((dl_ref[0:1, :], dl_ref[1:2, :]))
        out_ref[0:1, :] = d0
        out_ref[1:2, :] = d1

    return pl.pallas_call(body, name="lower_bounds_bwd", out_shape=jax.ShapeDtypeStruct(lb_raw.shape, F32), compiler_params=_params())(lb_raw, dl)


def kernel(x, c, w_mod, b_mod, w_in, conv_w, hgrn_norm_w, lower_bounds, w_branch, w_out, ln_g, ln_b, loss_target, m_w_mod, m_b_mod, m_w_in, m_conv_w, m_hgrn_norm_w, m_lower_bounds, m_w_branch, m_w_out, m_ln_g, m_ln_b, v_w_mod, v_b_mod, v_w_in, v_conv_w, v_hgrn_norm_w, v_lower_bounds, v_w_branch, v_w_out, v_ln_g, v_ln_b):
    D = D_MODEL
    x0 = x[0]
    target = loss_target[0]
    S = x0.shape[0]
    mx, my, mc = lax.axis_index("x"), lax.axis_index("y"), lax.axis_index("c")
    chip = 2 * mx + my
    me = 2 * chip + mc
    mod_cols = 3 * D // N_CHIPS

    plan = _Plan(w_in.astype(BF), w_branch.astype(BF), w_out.astype(BF), chip, mc)
    n_conv = DEPTH * 3 * (WIDTH // N_CHIPS)
    first = jnp.concatenate([c, conv_w.reshape(1, n_conv), jnp.zeros((1, 2 * D - D - n_conv), F32)], axis=1)
    first = plan.first(first.reshape(8, 2 * D // 8)).reshape(N_DEV, 2 * D)

    c_all = first[:, :D]
    conv_all = first[:, D:D + n_conv].reshape(N_DEV, DEPTH, 3, WIDTH // N_CHIPS)[0::2]
    mod_part = _all_gather8(_mod_rows(c_all, w_mod), "gather_mod")[0::2]
    mod_part = lax.dynamic_index_in_dim(mod_part, me, axis=1, keepdims=False).reshape(N_CHIPS, DEPTH, mod_cols)
    mods = [(mod_part[:, l].reshape(1, 3 * D) + b_mod[l][None, :]) for l in range(DEPTH)]
    lbs = _lbs_fwd(lower_bounds).reshape(DEPTH, WIDTH // 128, 1, 128)
    loss_blk, dx, small = _local_step(x0, target, mods, lbs, conv_all, hgrn_norm_w, ln_g, ln_b, plan)

    n_mod, n_nw, n_lb, n_ln, n_cw = DEPTH * 3 * D, DEPTH * 128, DEPTH * WIDTH, DEPTH * D, DEPTH * 3 * WIDTH
    row = jnp.concatenate(
        [jnp.concatenate([small[l][0], small[l][1], small[l][2]], axis=1) for l in range(DEPTH)]
        + [jnp.sum(small[l][3], axis=0) for l in range(DEPTH)]
        + [small[l][4].reshape(1, WIDTH) for l in range(DEPTH)]
        + [small[l][5] for l in range(DEPTH)] + [small[l][6] for l in range(DEPTH)]
        + [jnp.transpose(small[l][7], (1, 0, 2)).reshape(1, 3 * WIDTH) for l in range(DEPTH)]
        + [loss_blk[0:1, :]], axis=1)
    n_row = row.shape[1]
    fold = -(-n_row // (8 * 128)) * 128
    rows = jnp.concatenate([row, jnp.zeros((1, 8 * fold - n_row), F32)], axis=1).reshape(8, fold)

    whole, gathered = plan.finish(rows)
    grads = {kind: [[whole[(kind, l)]] for l in range(DEPTH)] for kind in ("in", "br", "out")}

    off_nw = n_mod
    off_lb = off_nw + n_nw
    off_lng = off_lb + n_lb
    off_lnb = off_lng + n_ln
    off_cw = off_lnb + n_ln
    off_loss = off_cw + n_cw
    total = _sum_devices(gathered).reshape(1, 8 * fold)
    gathered = gathered.reshape(N_DEV, 1, 8 * fold)
    d_lower = _lbs_bwd(lower_bounds, total[0, off_lb:off_lng].reshape(DEPTH, WIDTH))
    loss = total[0, off_loss]
    d_b_mod = total[0, :n_mod].reshape(DEPTH, 3 * D)
    d_norm_w = total[0, off_nw:off_lb].reshape(DEPTH, 128)
    d_ln_g = total[0, off_lng:off_lnb].reshape(DEPTH, D)
    d_ln_b = total[0, off_lnb:off_cw].reshape(DEPTH, D)
    d_conv = total[0, off_cw:off_loss].reshape(DEPTH, 3, N_CHIPS, WIDTH // N_CHIPS)
    d_conv = lax.dynamic_index_in_dim(d_conv, chip, axis=2, keepdims=False)
    dmod_all = gathered[:, 0, :n_mod].reshape(N_DEV, DEPTH, N_CHIPS, mod_cols)
    dmod_cols = jnp.transpose(lax.dynamic_index_in_dim(dmod_all, chip, axis=2, keepdims=False), (1, 0, 2))
    d_w_mod = _grad_w_mod(c_all, dmod_cols)

    res = {}
    res["w_mod"] = _adamw(w_mod, m_w_mod, v_w_mod, [[d_w_mod]], "adamw_w_mod")
    res["b_mod"] = _adamw(b_mod, m_b_mod, v_b_mod, [[d_b_mod]], "adamw_b_mod")
    res["w_in"] = _adamw(w_in, m_w_in, v_w_in, grads["in"], "adamw_w_in")
    res["conv_w"] = _adamw(conv_w, m_conv_w, v_conv_w, [[d_conv]], "adamw_conv_w")
    res["hgrn_norm_w"] = _adamw(hgrn_norm_w, m_hgrn_norm_w, v_hgrn_norm_w, [[d_norm_w]], "adamw_norm_w")
    res["lower_bounds"] = _adamw(lower_bounds, m_lower_bounds, v_lower_bounds, [[d_lower]], "adamw_lower_bounds")
    res["w_branch"] = _adamw(w_branch, m_w_branch, v_w_branch, grads["br"], "adamw_w_branch")
    res["w_out"] = _adamw(w_out, m_w_out, v_w_out, grads["out"], "adamw_w_out")
    res["ln_g"] = _adamw(ln_g, m_ln_g, v_ln_g, [[d_ln_g]], "adamw_ln_g")
    res["ln_b"] = _adamw(ln_b, m_ln_b, v_ln_b, [[d_ln_b]], "adamw_ln_b")
    names = ["w_mod", "b_mod", "w_in", "conv_w", "hgrn_norm_w", "lower_bounds", "w_branch", "w_out", "ln_g", "ln_b"]
    return (loss, dx[None], *[res[n][0] for n in names], *[res[n][1] for n in names],
            *[res[n][2] for n in names], *[res[n][3] for n in names])


class _Plan:
    FIRST_CHUNKS = 3
    WINDOWS = ((0, 1152), (1152, 640), (1792, 512))

    def __init__(self, w_in, w_br, w_out, chip, core):
        self.local = {"in": w_in, "br": w_br, "out": w_out}
        self.chip, self.where = chip, jnp.stack([chip, core]).astype(jnp.int32)
        self.gathered, self.partial, self.grads, self.chip_sums, self.scattered, self.pending = {}, {}, {}, {}, {}, {}

    def chunks(self, l):
        return self.FIRST_CHUNKS if l == 0 else 1

    def w_in(self, l):
        return [self.gathered[("in", l, c)] for c in range(self.chunks(l))]

    def _shard(self, key):
        mine = self.local[key[0]][key[1]]
        if key[0] == "in":
            cols = mine.shape[-1] // self.chunks(key[1])
            mine = mine[:, key[2] * cols:(key[2] + 1) * cols]
        return mine

    def _slab(self, key):
        mine = _halves(self._shard(key))
        return lax.dynamic_update_slice(lax.empty((N_CHIPS,) + mine.shape, mine.dtype), mine[None], (self.chip, 0, 0, 0, 0))

    def _gather(self, keys):
        return ("gather", keys), _gather_job([self._slab(key) for key in keys])

    def _gather_window(self, key, n):
        slab = self._slab(key) if n == 0 else self.partial[key]
        return ("gather" if n == len(self.WINDOWS) - 1 else "gather_part", [key]), _gather_job([slab], self.WINDOWS[n])

    def _to_sibling(self, keys):
        return ("to_sibling", keys), _to_sibling_job([_halves(self.grads[key], 1) for key in keys])

    def _scatter(self, keys):
        return ("scatter", keys), _scatter_job([self.chip_sums[key] for key in keys])

    def job(self, stage, l, c=0):
        parts = []
        if stage == "proj_fwd":
            parts = [self._gather([("in", l, c + 1)] if c + 1 < self.chunks(l) else [("br", l), ("out", l)])]
        elif stage == "attn_fwd" and l + 1 < DEPTH:
            parts = [self._gather_window(("in", l + 1, 0), 0)]
        elif stage == "hgrn_fwd" and l + 1 < DEPTH:
            parts = [self._gather_window(("in", l + 1, 0), 1)]
        elif stage == "merge_fwd" and l + 1 < DEPTH:
            parts = [self._gather_window(("in", l + 1, 0), 2)]
        elif stage == "attn_bwd":
            parts = [self._to_sibling([("out", l), ("br", l)])] + ([self._scatter([("in", l + 1)])] if l + 1 < DEPTH else [])
        elif stage == "hgrn_bwd":
            parts = [self._scatter([("out", l), ("br", l)])]
        elif stage == "proj_bwd":
            parts = [self._to_sibling([("in", l)])] if l else [self._scatter([("in", 0)])]
        self.pending[(stage, l, c)] = [(tag, len(job.outs)) for tag, job in parts]
        return _join_jobs([job for _, job in parts])

    def done(self, stage, l, outs, c=0):
        if outs is None:
            return
        at = 0
        for (what, keys), n_outs in self.pending[(stage, l, c)]:
            mine, at = outs[at:at + n_outs], at + n_outs
            for n, key in enumerate(keys):
                if what == "gather":
                    self.gathered[key] = mine[n].reshape((N_CHIPS,) + self._shard(key).shape)
                elif what == "gather_part":
                    self.partial[key] = mine[n]
                elif what == "to_sibling":
                    self.chip_sums[key] = _sum_cores(_halves(self.grads[key], 1), mine[n], self.where)
                else:
                    self.scattered[key] = mine[n]

    def first(self, rows):
        tag, job = self._gather([("in", 0, 0)])
        self.pending[("first", 0, 0)] = [(tag, len(job.outs))]
        outs = _run_job(_join_jobs([job, _gather8_job(rows)]), "gather_first")
        self.done("first", 0, outs[:-1])
        return outs[-1]

    def took(self, key, grad):
        self.grads[key] = grad
        if key == ("in", 0):
            tag, job = self._to_sibling([key])
            self.pending[("took", 0, 0)] = [(tag, len(job.outs))]
            self.done("took", 0, _run_job(job, "to_sibling_last"))

    def finish(self, rows):
        keys = [(kind, l) for kind in ("in", "br", "out") for l in range(DEPTH)]
        halves = [_sum_chips(self.chip_sums[key], self.scattered[key], self.where) for key in keys]
        outs = _run_job(_join_jobs([_place_job(halves), _gather8_job(rows)]), "place_halves")
        return {key: w.reshape(self.grads[key].shape[1:]) for key, w in zip(keys, outs[:-1])}, outs[-1]


def _local_step(x0, target, mods, lbs, conv_all, hgrn_norm_w, ln_g, ln_b, plan):
    D = D_MODEL
    after, before = _attn_consts()
    hg_consts = _hgrn_consts()

    saved = []
    xl = x0
    for l in range(DEPTH):
        n = plan.chunks(l)
        (proj, h), got = _proj_fwd(xl, mods[l], plan.gathered[("in", l, 0)], plan.job("proj_fwd", l, 0), (0, n))
        plan.done("proj_fwd", l, got, 0)
        for c in range(1, n):
            (proj,), got = _proj_cols(h, plan.gathered[("in", l, c)], (c, n), proj, plan.job("proj_fwd", l, c))
            plan.done("proj_fwd", l, got, c)
        (o_a, tot), got = _attn_fwd(proj, after, plan.job("attn_fwd", l))
        plan.done("attn_fwd", l, got)
        (o_b, states), got = _hgrn_fwd(proj, lbs[l], hg_consts, plan.job("hgrn_fwd", l))
        plan.done("hgrn_fwd", l, got)
        ys = _branch_fwd(proj, o_a, o_b, hgrn_norm_w, conv_all, l)
        (x_next, merged, y), got = _merge_fwd(xl, mods[l], proj, ys, plan.gathered[("br", l)], plan.gathered[("out", l)], ln_g, ln_b, l,
                                              plan.job("merge_fwd", l))
        plan.done("merge_fwd", l, got)
        saved.append((xl, proj, h, o_a, tot, o_b, states, ys, merged, y))
        xl = x_next
    dx, loss_blk = _loss_head(xl, target)

    small = [None] * DEPTH
    for l in reversed(range(DEPTH)):
        xin, proj, h, o_a, tot, o_b, states, ys, merged, y = saved[l]
        dx_res, dy, dp, dproj, dy_a, dy_b, dy_c, dln_g, dln_b, dgate = _merge_bwd(
            dx, xin, y, mods[l], proj, ys, plan.gathered[("br", l)], plan.gathered[("out", l)], ln_g, l)
        plan.took(("out", l), _grad_w_out(merged, dy))
        g_br = jax.ShapeDtypeStruct((N_CHIPS, 3, WIDTH, D // N_CHIPS), BF)
        for i in range(3):
            g_br = _grad_w(ys[i], dp, (i,), g_br, "grad_w_branch", tm=WIDTH, tn=D // N_CHIPS, b_col0=i * D)
        plan.took(("br", l), g_br)
        d_oa, d_ob, dnorm_w, dconv_w, dproj = _branch_bwd(proj, o_a, o_b, hgrn_norm_w, conv_all, dy_a, dy_b, dy_c, dproj, l)
        (dproj,), got = _attn_bwd(proj, d_oa, tot, after, before, dproj, plan.job("attn_bwd", l))
        plan.done("attn_bwd", l, got)
        (dlb, dproj), got = _hgrn_bwd(proj, lbs[l], states, d_ob, hg_consts, dproj, plan.job("hgrn_bwd", l))
        plan.done("hgrn_bwd", l, got)
        plan.took(("in", l), _grad_w(h, dproj, (), jax.ShapeDtypeStruct((N_CHIPS, D, IN_COLS // N_CHIPS), BF), "grad_w_in", tm=1024, tn=2304))
        (dx, dshift, dscale), got = _proj_bwd(dproj, plan.w_in(l), xin, mods[l], dx_res, plan.job("proj_bwd", l))
        plan.done("proj_bwd", l, got)
        small[l] = (dshift, dscale, dgate, dnorm_w, dlb, dln_g, dln_b, dconv_w)
    return loss_blk, dx, small


def _grad_w_out(merged, dy):
    S, D = merged.shape
    q = D // N_CHIPS

    def body(a_ref, b_ref, o_ref):
        o_ref[...] = _dot_tn(a_ref[...], b_ref[...]).astype(BF)

    return pl.pallas_call(
        body, name="grad_w_out",
        grid=(N_CHIPS,),
        in_specs=[pl.BlockSpec((S, q), lambda j: (0, j)), pl.BlockSpec((S, D), lambda j: (0, 0))],
        out_specs=pl.BlockSpec((None, q, D), lambda j: (j, 0, 0)),
        out_shape=jax.ShapeDtypeStruct((N_CHIPS, q, D), BF),
        compiler_params=_params(dimension_semantics=("parallel",)),
    )(merged, dy)
```

```python
import functools
import math

import numpy as np
import jax
import jax.numpy as jnp
from jax import lax
from jax.experimental import pallas as pl
from jax.experimental.pallas import tpu as pltpu

F32 = jnp.float32
BF = jnp.bfloat16
MESH = pl.DeviceIdType.MESH

DEPTH = 2
D_MODEL = 1024
WIDTH = 512
IN_COLS = 12 * WIDTH + 3 * D_MODEL
N_CHIPS = 4
N_DEV = 8
SB_BLOCK = 128
SB_HEAD_DIM = 64
HG_CHUNK = 128
HG_DIM = 128
LN_EPS = 1e-5
RMS_EPS = 1e-6
ALPHA = (2.0 * DEPTH) ** 0.25
ADAM_LR, ADAM_B1, ADAM_B2, ADAM_EPS, ADAM_WD, ADAM_STEP = 0.001, 0.9, 0.999, 1e-08, 0.01, 10
VMEM_LIMIT = 56 << 20


def _params(**kw):
    return pltpu.CompilerParams(vmem_limit_bytes=VMEM_LIMIT, **kw)


def _dot(a, b):
    return jnp.dot(a, b, preferred_element_type=F32)


def _dot_nt(a, b):
    return lax.dot_general(a, b, (((1,), (1,)), ((), ())), preferred_element_type=F32)


def _dot_tn(a, b):
    return lax.dot_general(a, b, (((0,), (0,)), ((), ())), preferred_element_type=F32)


def _sigmoid(x):
    return 1.0 / (1.0 + jnp.exp(-x))


def _silu(x):
    return x * _sigmoid(x)


def _split_dot(t, g, terms):
    acc = None
    rest = g
    for _ in range(terms):
        part = rest.astype(BF)
        rest = rest - part.astype(F32)
        d = _dot(t, part)
        acc = d if acc is None else acc + d
    return acc


def _standardize(x):
    mu = jnp.mean(x, axis=-1, keepdims=True)
    xc = x - mu
    var = jnp.mean(xc * xc, axis=-1, keepdims=True)
    rstd = lax.rsqrt(var + LN_EPS)
    return xc * rstd, rstd


def _standardize_bwd(dxs, xs, rstd):
    return rstd * (dxs - jnp.mean(dxs, axis=-1, keepdims=True) - xs * jnp.mean(dxs * xs, axis=-1, keepdims=True))


class _Job:
    def __init__(self, ins, outs, sems, make, alias=None):
        self.ins, self.outs, self.sems, self.make = list(ins), list(outs), list(sems), make
        self.alias = dict(alias or {})


def _join_jobs(jobs):
    jobs = [j for j in jobs if j is not None]
    if len(jobs) <= 1:
        return jobs[0] if jobs else None

    def make(ins, outs, sems):
        phases, i, o, s = [], 0, 0, 0
        for j in jobs:
            got = j.make(ins[i:i + len(j.ins)], outs[o:o + len(j.outs)], sems[s:s + len(j.sems)])
            i, o, s = i + len(j.ins), o + len(j.outs), s + len(j.sems)
            for n, phase in enumerate(got):
                if n == len(phases):
                    phases.append([])
                phases[n] += phase
        return phases

    alias, i, o = {}, 0, 0
    for j in jobs:
        alias.update({i + a: o + b for a, b in j.alias.items()})
        i, o = i + len(j.ins), o + len(j.outs)
    return _Job(sum([j.ins for j in jobs], []), sum([j.outs for j in jobs], []), sum([j.sems for j in jobs], []), make, alias)


def _flip(v, bit):
    return 1 - v if bit else v


def _halves(a, front=0):
    shape = a.shape
    lead = math.prod(shape[front:-2])
    return a.reshape(shape[:front] + (lead, 2, shape[-2] // 2, shape[-1]))


def _dma_sems(*shapes):
    return [pltpu.SemaphoreType.DMA(s) for s in shapes]


def _same(arrays):
    return [jax.ShapeDtypeStruct(a.shape, a.dtype) for a in arrays]


def _gather_job(slabs, window=None):
    n = len(slabs)
    cols = slice(None) if window is None else pl.ds(*window)

    def make(ins, outs, sems):
        send1, recv1, send2, recv2 = sems
        mx, my, mc = lax.axis_index("x"), lax.axis_index("y"), lax.axis_index("c")
        fetch, pass_on = [], []
        for a in range(n):
            ours = outs[a].at[2 * mx + my, :, mc, :, cols]
            for k in range(1, N_CHIPS):
                px, py = _flip(mx, k & 2), _flip(my, k & 1)
                fetch.append(pltpu.make_async_remote_copy(
                    src_ref=ours, dst_ref=ours, send_sem=send1.at[a, k - 1], recv_sem=recv1.at[a, k - 1],
                    device_id=(px, py, mc), device_id_type=MESH))
                theirs = outs[a].at[2 * px + py, :, mc, :, cols]
                pass_on.append(pltpu.make_async_remote_copy(
                    src_ref=theirs, dst_ref=theirs, send_sem=send2.at[a, k - 1], recv_sem=recv2.at[a, k - 1],
                    device_id=(mx, my, 1 - mc), device_id_type=MESH))
        return [fetch, pass_on]

    pairs = (n, N_CHIPS - 1)
    return _Job(slabs, _same(slabs), _dma_sems(pairs, pairs, pairs, pairs), make, {a: a for a in range(n)})


def _to_sibling_job(grads):
    n = len(grads)

    def make(ins, outs, sems):
        send_sems, recv_sems = sems
        mx, my, mc = lax.axis_index("x"), lax.axis_index("y"), lax.axis_index("c")
        return [[pltpu.make_async_remote_copy(
            src_ref=ins[a].at[:, :, 1 - mc], dst_ref=outs[a], send_sem=send_sems.at[a], recv_sem=recv_sems.at[a],
            device_id=(mx, my, 1 - mc), device_id_type=MESH) for a in range(n)]]

    outs = [jax.ShapeDtypeStruct(g.shape[:2] + g.shape[3:], g.dtype) for g in grads]
    return _Job(grads, outs, _dma_sems((n,), (n,)), make)


def _scatter_job(sums):
    n = len(sums)

    def make(ins, outs, sems):
        send_sems, recv_sems = sems
        mx, my, mc = lax.axis_index("x"), lax.axis_index("y"), lax.axis_index("c")
        copies = []
        for a in range(n):
            for k in range(1, N_CHIPS):
                px, py = _flip(mx, k & 2), _flip(my, k & 1)
                copies.append(pltpu.make_async_remote_copy(
                    src_ref=ins[a].at[2 * px + py], dst_ref=outs[a].at[k - 1], send_sem=send_sems.at[a, k - 1],
                    recv_sem=recv_sems.at[a, k - 1], device_id=(px, py, mc), device_id_type=MESH))
        return [copies]

    pairs = (n, N_CHIPS - 1)
    return _Job(sums, [jax.ShapeDtypeStruct((N_CHIPS - 1,) + s.shape[1:], s.dtype) for s in sums], _dma_sems(pairs, pairs), make)


def _place_job(wholes):
    n = len(wholes)

    def make(ins, outs, sems):
        send_sems, recv_sems = sems
        mx, my, mc = lax.axis_index("x"), lax.axis_index("y"), lax.axis_index("c")
        copies = []
        for a in range(n):
            here = outs[a].at[:, mc]
            copies.append(pltpu.make_async_remote_copy(src_ref=here, dst_ref=here, send_sem=send_sems.at[a], recv_sem=recv_sems.at[a],
                                                       device_id=(mx, my, 1 - mc), device_id_type=MESH))
        return [copies]

    return _Job(wholes, _same(wholes), _dma_sems((n,), (n,)), make, {a: a for a in range(n)})


def _gather8_job(x):
    def make(ins, outs, sems):
        local_sem, send_sems, recv_sems = sems
        mx, my, mc = lax.axis_index("x"), lax.axis_index("y"), lax.axis_index("c")
        here = outs[0].at[4 * mx + 2 * my + mc]
        copies = [pltpu.make_async_copy(ins[0], here, local_sem.at[0])]
        for k in range(1, N_DEV):
            peer = (_flip(mx, k & 4), _flip(my, k & 2), _flip(mc, k & 1))
            copies.append(pltpu.make_async_remote_copy(src_ref=ins[0], dst_ref=here, send_sem=send_sems.at[k - 1],
                                                       recv_sem=recv_sems.at[k - 1], device_id=peer, device_id_type=MESH))
        return [copies]

    return _Job([x], [jax.ShapeDtypeStruct((N_DEV,) + x.shape, x.dtype)], _dma_sems((1,), (N_DEV - 1,), (N_DEV - 1,)), make)


def _run_phases(phases, first=0):
    for n, phase in enumerate(phases):
        if n >= first:
            for cp in phase:
                cp.start()
        for cp in phase:
            cp.wait()


def _run_job(job, name):
    k_in, k_out = len(job.ins), len(job.outs)

    def body(*refs):
        _run_phases(job.make(refs[:k_in], refs[k_in:k_in + k_out], refs[k_in + k_out:]))

    hbm = pl.BlockSpec(memory_space=pl.ANY)
    return pl.pallas_call(body, name=name, in_specs=[hbm] * k_in, out_specs=[hbm] * k_out, out_shape=job.outs,
                          scratch_shapes=job.sems, input_output_aliases=job.alias, compiler_params=_params())(*job.ins)


def _hosted(body, job, args, *, name, grid, in_specs, out_specs, out_shape, scratch_shapes=(), semantics, aliases=None):
    in_specs, out_specs, out_shape, scratch = list(in_specs), list(out_specs), list(out_shape), list(scratch_shapes)
    aliases = dict(aliases or {})
    if job is None:
        outs = pl.pallas_call(body, name=name, grid=grid, in_specs=in_specs, out_specs=out_specs, out_shape=out_shape,
                              scratch_shapes=scratch, input_output_aliases=aliases,
                              compiler_params=_params(dimension_semantics=semantics))(*args)
        return list(outs), None
    n_in, n_out, n_scr, k_in, k_out = len(in_specs), len(out_specs), len(scratch), len(job.ins), len(job.outs)

    def wrapped(*refs):
        ins, rest = refs[:n_in], refs[n_in:]
        job_ins, rest = rest[:k_in], rest[k_in:]
        outs, rest = rest[:n_out], rest[n_out:]
        job_outs, rest = rest[:k_out], rest[k_out:]
        scr, sems = rest[:n_scr], rest[n_scr:]
        ids = [pl.program_id(a) for a in range(len(grid))]
        first = functools.reduce(jnp.logical_and, [i == 0 for i in ids])
        last = functools.reduce(jnp.logical_and, [i == g - 1 for i, g in zip(ids, grid)])

        @pl.when(first)
        def _():
            for cp in job.make(job_ins, job_outs, sems)[0]:
                cp.start()

        body(*ins, *outs, *scr)

        @pl.when(last)
        def _():
            _run_phases(job.make(job_ins, job_outs, sems), first=1)

    hbm = pl.BlockSpec(memory_space=pl.ANY)
    outs = pl.pallas_call(
        wrapped, name=name, grid=grid, in_specs=in_specs + [hbm] * k_in, out_specs=out_specs + [hbm] * k_out,
        out_shape=out_shape + job.outs, scratch_shapes=scratch + job.sems,
        input_output_aliases={**aliases, **{n_in + i: n_out + o for i, o in job.alias.items()}},
        compiler_params=_params(dimension_semantics=("arbitrary",) * len(grid)))(*args, *job.ins)
    return list(outs[:n_out]), list(outs[n_out:])


def _proj_fwd(x, mod, wg, job=None, chunk=(0, 1), tm=512):
    S, D = x.shape
    tm = min(tm, S)
    tn = wg.shape[-1]
    c, n = chunk

    rows = S // tm

    def body(x_ref, mod_ref, w_ref, proj_ref, h_ref, hs):
        mine = pl.ds(pl.multiple_of(pl.program_id(1) * tm, tm), tm)

        @pl.when(pl.program_id(0) == 0)
        def _():
            xs, _ = _standardize(x_ref[...])
            h = xs * (1.0 + mod_ref[:, D:2 * D]) + mod_ref[:, 0:D]
            hb = h.astype(BF)
            hs[mine, :] = hb
            h_ref[...] = hb

        proj_ref[...] = _dot(hs[mine, :], w_ref[...])

    once = lambda j, i: jnp.where(j == 0, i, rows - 1)
    return _hosted(
        body, job, (x, mod, wg), name="proj_fwd",
        grid=(N_CHIPS, rows),
        in_specs=[pl.BlockSpec((tm, D), lambda j, i: (once(j, i), 0)),
                  pl.BlockSpec((1, 3 * D), lambda j, i: (0, 0)),
                  pl.BlockSpec((None, D, tn), lambda j, i: (j, 0, 0))],
        out_specs=[pl.BlockSpec((tm, tn), lambda j, i: (i, j * n + c)),
                   pl.BlockSpec((tm, D), lambda j, i: (once(j, i), 0))],
        out_shape=[jax.ShapeDtypeStruct((S, IN_COLS), F32), jax.ShapeDtypeStruct((S, D), BF)],
        scratch_shapes=[pltpu.VMEM((S, D), BF)],
        semantics=("arbitrary", "arbitrary"))


def _proj_cols(h, wg, chunk, proj, job=None, tm=512):
    S, D = h.shape
    tm = min(tm, S)
    tn = wg.shape[-1]
    c, n = chunk

    def body(h_ref, w_ref, prev_ref, proj_ref):
        mine = pl.ds(pl.multiple_of(pl.program_id(1) * tm, tm), tm)
        proj_ref[...] = _dot(h_ref[mine, :], w_ref[...])

    in_specs = [pl.BlockSpec((S, D), lambda j, i: (0, 0)), pl.BlockSpec((None, D, tn), lambda j, i: (j, 0, 0)),
                pl.BlockSpec(memory_space=pl.ANY)]
    return _hosted(body, job, (h, wg, proj), name="proj_cols", grid=(N_CHIPS, S // tm), in_specs=in_specs,
                   out_specs=[pl.BlockSpec((tm, tn), lambda j, i: (i, j * n + c))],
                   out_shape=[jax.ShapeDtypeStruct(proj.shape, proj.dtype)],
                   semantics=("arbitrary", "arbitrary"), aliases={2: 0})


SB_ROWS = 256
SB_KEYS = 256


def _attn_consts():
    j = np.arange(SB_KEYS)[:, None]
    s = np.arange(SB_KEYS)[None, :]
    from_here = np.concatenate([(j >= s), (j >= s)], axis=0).astype(np.float32)
    return jnp.asarray(from_here, BF), jnp.asarray((j <= s).astype(np.float32), BF)


def _hi_lo(x):
    hi = lax.bitcast_convert_type(lax.bitcast_convert_type(x, jnp.uint32) & jnp.uint32(0xFFFF0000), F32)
    return hi.astype(BF), (x - hi).astype(BF)


def _sums_r(x, t2):
    hi, lo = _hi_lo(x)
    return _dot(jnp.concatenate([hi, lo], axis=1), t2)


def _all_lanes(col, lanes):
    return jnp.broadcast_to(col, (col.shape[0], lanes))


def _attn_rows(ref, r0, rows, lanes, head0, scale=None):
    v = ref[pl.ds(r0, rows), lanes]
    if scale is not None:
        v = v * scale
    return jnp.concatenate([jnp.where(head0, v, 0.0), jnp.where(head0, 0.0, v)], axis=0).astype(BF)


SB_PAIRS_FWD = 4
SB_PAIRS_BWD = 2


def _attn_specs(S, n_pairs):
    return lambda base: pl.BlockSpec((S, n_pairs * SB_BLOCK), lambda p, base=base: (0, base // n_pairs + p))


def _attn_scores(q2n, k_ref, lanes, kj, t2, from_here_ref, masked):
    c0 = pl.multiple_of(kj * SB_KEYS, SB_KEYS)
    kb = k_ref[pl.ds(c0, SB_KEYS), lanes].astype(BF)
    zn = _dot_nt(q2n, kb)
    lsb = jnp.minimum(zn, 0.0) - jnp.log(1.0 + jnp.exp(-jnp.abs(zn)))
    valid = None
    if masked:
        valid = (lax.broadcasted_iota(jnp.int32, zn.shape, 1) + kj * SB_KEYS) < t2
        lsb = jnp.where(valid, lsb, 0.0)
    return c0, kb, zn, valid, lsb, _sums_r(lsb, from_here_ref[...])


def _attn_fwd(proj, from_here, job=None):
    S = proj.shape[0]
    TQ = SB_ROWS
    assert S % TQ == 0 and SB_KEYS == TQ
    scale = SB_HEAD_DIM ** -0.5
    n_pairs = SB_PAIRS_FWD
    pairs = range(n_pairs)
    lanes = [pl.ds(p * SB_BLOCK, SB_BLOCK) for p in pairs]

    def body(q_ref, k_ref, v_ref, from_here_ref, o_ref, tot_ref, run, acc):
        head0 = lax.broadcasted_iota(jnp.int32, (1, 2 * SB_HEAD_DIM), 1) < SB_HEAD_DIM

        def qloop(qi, _):
            r0 = pl.multiple_of(qi * TQ, TQ)
            q2n = [_attn_rows(q_ref, r0, TQ, lanes[p], head0, -scale) for p in pairs]
            trow = lax.broadcasted_iota(jnp.int32, (TQ, SB_KEYS), 0) + qi * TQ
            t2 = jnp.concatenate([trow, trow], axis=0)
            run[...] = jnp.zeros_like(run)
            acc[...] = jnp.zeros_like(acc)

            def step(kj, masked):
                got = [_attn_scores(q2n[p], k_ref, lanes[p], kj, t2, from_here_ref, masked) for p in pairs]
                for p in pairs:
                    c0, _, zn, valid, _, sums = got[p]
                    r = run[p]
                    e = sums - zn + jnp.concatenate([r, r], axis=1)
                    if masked:
                        e = jnp.where(valid, e, -jnp.inf)
                    acc[p] += _dot(jnp.exp(e).astype(BF), v_ref[pl.ds(c0, SB_KEYS), lanes[p]].astype(BF))
                    run[p] = r + _all_lanes(sums[:, 0:1], SB_BLOCK)

            step(qi, True)

            def below(n, _):
                step(qi - 1 - n, False)
                return 0

            lax.fori_loop(0, qi, below, 0)
            for p in pairs:
                o_ref[pl.ds(r0, TQ), lanes[p]] = jnp.where(head0, acc[p, 0:TQ, :], acc[p, TQ:2 * TQ, :])
                tot_ref[p, 0, pl.ds(r0, TQ), :] = run[p, 0:TQ, :]
                tot_ref[p, 1, pl.ds(r0, TQ), :] = run[p, TQ:2 * TQ, :]
            return 0

        lax.fori_loop(0, S // TQ, qloop, 0)

    col = _attn_specs(S, n_pairs)
    state = pltpu.VMEM((n_pairs, 2 * TQ, SB_BLOCK), F32)
    return _hosted(
        body, job, (proj, proj, proj, from_here), name="attn_fwd",
        grid=(WIDTH // (n_pairs * SB_BLOCK),),
        in_specs=[col(0), col(4), col(8), pl.BlockSpec(from_here.shape, lambda p: (0, 0))],
        out_specs=[col(0), pl.BlockSpec((n_pairs, 2, S, 128), lambda p: (p, 0, 0, 0))],
        out_shape=[jax.ShapeDtypeStruct((S, WIDTH), F32), jax.ShapeDtypeStruct((WIDTH // 128, 2, S, 128), F32)],
        scratch_shapes=[state, state],
        semantics=("parallel",))


def _attn_bwd(proj, d_o, tot, from_here, up_to, dproj, job=None):
    S = proj.shape[0]
    TQ = SB_ROWS
    assert S % TQ == 0 and SB_KEYS == TQ
    scale = SB_HEAD_DIM ** -0.5
    n_pairs = SB_PAIRS_BWD
    pairs = range(n_pairs)
    lanes = [pl.ds(p * SB_BLOCK, SB_BLOCK) for p in pairs]

    def body(q_ref, k_ref, v_ref, do_ref, tot_ref, from_here_ref, up_to_ref, _, dproj_ref, pre, cum, dq_acc, dk_acc, dv_acc, stage, sems):
        dq_ref, dk_ref, dv_ref = stage.at[0], stage.at[1], stage.at[2]
        head0 = lax.broadcasted_iota(jnp.int32, (1, 2 * SB_HEAD_DIM), 1) < SB_HEAD_DIM
        dk_acc[...] = jnp.zeros_like(dk_acc)
        dv_acc[...] = jnp.zeros_like(dv_acc)

        def qloop(qi, _):
            r0 = pl.multiple_of(qi * TQ, TQ)
            q2n = [_attn_rows(q_ref, r0, TQ, lanes[p], head0, -scale) for p in pairs]
            do2 = [_attn_rows(do_ref, r0, TQ, lanes[p], head0) for p in pairs]
            trow = lax.broadcasted_iota(jnp.int32, (TQ, SB_KEYS), 0) + qi * TQ
            t2 = jnp.concatenate([trow, trow], axis=0)
            for p in pairs:
                pre[p, 0:TQ, :] = tot_ref[p, 0, pl.ds(r0, TQ), :]
                pre[p, TQ:2 * TQ, :] = tot_ref[p, 1, pl.ds(r0, TQ), :]
            cum[...] = jnp.zeros_like(cum)
            dq_acc[...] = jnp.zeros_like(dq_acc)

            def step(kj, masked):
                got = [_attn_scores(q2n[p], k_ref, lanes[p], kj, t2, from_here_ref, masked) for p in pairs]
                for p in pairs:
                    c0, kb, zn, valid, lsb, sums = got[p]
                    later = pre[p] - _all_lanes(sums[:, 0:1], SB_BLOCK)
                    pre[p] = later
                    e = sums - zn + jnp.concatenate([later, later], axis=1)
                    sig = jnp.exp(lsb - zn)
                    if masked:
                        e = jnp.where(valid, e, -jnp.inf)
                        sig = jnp.where(valid, sig, 0.0)
                    a = jnp.exp(e)
                    w = _dot_nt(do2[p], v_ref[pl.ds(c0, SB_KEYS), lanes[p]].astype(BF)) * a
                    upto = _dot(w.astype(BF), up_to_ref[...])
                    c = cum[p]
                    dz = w - sig * (upto + jnp.concatenate([c, c], axis=1))
                    cum[p] = c + _all_lanes(upto[:, SB_KEYS - 1:SB_KEYS], SB_BLOCK)
                    dzb = dz.astype(BF)
                    dq_acc[p] += _dot(dzb, kb)
                    dk_acc[pl.ds(c0, SB_KEYS), lanes[p]] += _dot_tn(dzb, q2n[p])
                    dv_acc[pl.ds(c0, SB_KEYS), lanes[p]] += _dot_tn(a.astype(BF), do2[p])

            def below(kj, _):
                step(kj, False)
                return 0

            lax.fori_loop(0, qi, below, 0)
            step(qi, True)
            for p in pairs:
                dq_ref[pl.ds(r0, TQ), lanes[p]] = (jnp.where(head0, dq_acc[p, 0:TQ, :], dq_acc[p, TQ:2 * TQ, :]) * scale).astype(BF)
            return 0

        lax.fori_loop(0, S // TQ, qloop, 0)
        dk_ref[...] = (-dk_acc[...]).astype(BF)
        dv_ref[...] = dv_acc[...].astype(BF)
        lane0 = pl.program_id(0) * (n_pairs * SB_BLOCK)
        _put_columns([dq_ref, dk_ref, dv_ref], dproj_ref, [WIDTH * n + lane0 for n in range(3)], sems)

    col = _attn_specs(S, n_pairs)
    whole = lambda a: pl.BlockSpec(a.shape, lambda p: (0, 0))
    hbm = pl.BlockSpec(memory_space=pl.ANY)
    state = pltpu.VMEM((n_pairs, 2 * TQ, SB_BLOCK), F32)
    grads = pltpu.VMEM((S, n_pairs * SB_BLOCK), F32)
    return _hosted(
        body, job, (proj, proj, proj, d_o, tot, from_here, up_to, dproj), name="attn_bwd",
        grid=(WIDTH // (n_pairs * SB_BLOCK),),
        in_specs=[col(0), col(4), col(8), col(0), pl.BlockSpec((n_pairs, 2, S, 128), lambda p: (p, 0, 0, 0)), whole(from_here),
                  whole(up_to), hbm],
        out_specs=[hbm],
        out_shape=[jax.ShapeDtypeStruct(dproj.shape, dproj.dtype)],
        scratch_shapes=[state, state, state, grads, grads, pltpu.VMEM((3, S, n_pairs * SB_BLOCK), BF), pltpu.SemaphoreType.DMA((3,))],
        semantics=("arbitrary",), aliases={7: 0})


HG_LEVELS = tuple(HG_CHUNK >> n for n in range(1, HG_CHUNK.bit_length()))


def _hgrn_consts():
    C = HG_CHUNK
    t = np.arange(C)[:, None]
    s = np.arange(C)[None, :]
    rows = [(s <= t), (s > t)]
    masks = [(t == s)]
    for m in HG_LEVELS:
        two = 2 * m
        mid = (t // two) * two + m
        right = (t % two) >= m
        rows.append((right & (s >= mid) & (s <= t)) | ((~right) & (s > t) & (s <= mid - 1)))
        masks.append(((t // two) == (s // two)) & right & ((s % two) < m))
    tri = np.concatenate(rows, axis=0).astype(np.float32)
    twice = lambda a: jnp.asarray(np.concatenate([a, a], axis=1), BF)
    return (twice(tri), twice(tri.T), jnp.asarray(np.stack(masks).astype(np.float32), F32))


HG_SUM_BLOCKS = 2 + len(HG_LEVELS)


def _split_rows(g):
    hi = g.astype(BF)
    return jnp.concatenate([hi, (g - hi.astype(F32)).astype(BF)], axis=0)


def _hgrn_sum_blocks(e):
    C = HG_CHUNK
    blocks = tuple(e[n * C:(n + 1) * C] for n in range(HG_SUM_BLOCKS))
    return blocks + (jnp.broadcast_to(e[C - 1:C], (HG_DIM, e.shape[1])),)


@jax.custom_vjp
def _hgrn_sums(tri, tri_t, g):
    return _hgrn_sum_blocks(_split_dot(tri[:, :HG_CHUNK], g, 2))


def _hgrn_sums_fwd(tri, tri_t, g):
    return _hgrn_sum_blocks(_dot(tri, _split_rows(g))), (tri, tri_t)


def _hgrn_sums_bwd(res, ds):
    tri, tri_t = res
    C = HG_CHUNK
    last = lax.broadcasted_iota(jnp.int32, (C, 1), 0) == C - 1
    prefix = ds[0] + jnp.where(last, jnp.sum(ds[-1], axis=0, keepdims=True), 0.0)
    d = jnp.concatenate((prefix,) + tuple(ds[1:-1]), axis=0)
    return jnp.zeros_like(tri), jnp.zeros_like(tri_t), _dot(tri_t[:, :d.shape[0]], d.astype(BF))


_hgrn_sums.defvjp(_hgrn_sums_fwd, _hgrn_sums_bwd)


def _bf_dot(a, b):
    return _dot(a.astype(BF), b.astype(BF))


def _bf_dot_nt(a, b):
    return _dot_nt(a.astype(BF), b.astype(BF))


def _bf_dot_tn(a, b):
    return _dot_tn(a.astype(BF), b.astype(BF))


@jax.custom_vjp
def _mm(a, b):
    return _bf_dot(a, b)


_mm.defvjp(lambda a, b: (_bf_dot(a, b), (a, b)), lambda r, ct: (_bf_dot_nt(ct, r[1]), _bf_dot_tn(r[0], ct)))


@jax.custom_vjp
def _mm_nt(a, b):
    return _bf_dot_nt(a, b)


_mm_nt.defvjp(lambda a, b: (_bf_dot_nt(a, b), (a, b)), lambda r, ct: (_bf_dot(ct, r[1]), _bf_dot_tn(ct, r[0])))


@jax.custom_vjp
def _mm_tn(a, b):
    return _bf_dot_tn(a, b)


_mm_tn.defvjp(lambda a, b: (_bf_dot_tn(a, b), (a, b)), lambda r, ct: (_bf_dot_nt(r[1], ct), _bf_dot(r[0], ct)))


def _hgrn_chunk(tri, tri_t, masks, qraw, fpre, v, st, lb):
    q = _silu(qraw)
    f = lb + (1.0 - lb) * _sigmoid(fpre)
    k = 1.0 - f
    e = _hgrn_sums(tri, tri_t, jnp.log(f))
    prefix, suffix, whole = e[0], e[1], e[-1]
    scores = masks[0] * _mm_nt(q, k)
    for n in range(len(HG_LEVELS)):
        decay = jnp.exp(e[2 + n])
        scores = scores + masks[n + 1] * _mm_nt(q * decay, k * decay)
    o = _mm_nt(q * jnp.exp(prefix), st) + _mm(scores, v)
    st_new = st * jnp.exp(whole) + _mm_tn(v, k * jnp.exp(suffix))
    return o, st_new


HG_HEADS_PER_STEP = 4
HG_LANES = HG_HEADS_PER_STEP * HG_DIM


def _hgrn_specs(S, consts):
    col = lambda base: pl.BlockSpec((S, HG_LANES), lambda p, base=base: (0, base // HG_HEADS_PER_STEP + p))
    whole = [pl.BlockSpec(a.shape, lambda p, n=a.ndim: (0,) * n) for a in consts]
    return col, whole


def _hgrn_fwd(proj, lbs, consts, job=None):
    S = proj.shape[0]
    nc = S // HG_CHUNK
    heads = range(HG_HEADS_PER_STEP)

    def body(q_ref, f_ref, i_ref, lb_ref, tri_ref, trit_ref, mask_ref, o_ref, st_ref):
        tri, tri_t = tri_ref[...], trit_ref[...]
        masks = [mask_ref[n] for n in range(len(HG_LEVELS) + 1)]

        def chunk(ci, sts):
            r0 = pl.multiple_of(ci * HG_CHUNK, HG_CHUNK)
            rows = pl.ds(r0, HG_CHUNK)
            new = []
            for hd in heads:
                lanes = pl.ds(hd * HG_DIM, HG_DIM)
                st_ref[hd, ci] = sts[hd]
                o, st_new = _hgrn_chunk(tri, tri_t, masks, q_ref[rows, lanes], f_ref[rows, lanes], i_ref[rows, lanes], sts[hd], lb_ref[hd])
                o_ref[rows, lanes] = o
                new.append(st_new)
            return tuple(new)

        lax.fori_loop(0, nc, chunk, tuple(jnp.zeros((HG_DIM, HG_DIM), F32) for _ in heads))

    col, whole = _hgrn_specs(S, consts)
    return _hosted(
        body, job, (proj, proj, proj, lbs, *consts), name="hgrn_fwd",
        grid=(WIDTH // HG_LANES,),
        in_specs=[col(16), col(20), col(24), pl.BlockSpec((HG_HEADS_PER_STEP, 1, 128), lambda p: (p, 0, 0))] + whole,
        out_specs=[col(0), pl.BlockSpec((HG_HEADS_PER_STEP, nc, HG_DIM, HG_DIM), lambda p: (p, 0, 0, 0))],
        out_shape=[jax.ShapeDtypeStruct((S, WIDTH), F32), jax.ShapeDtypeStruct((WIDTH // 128, nc, HG_DIM, HG_DIM), F32)],
        semantics=("parallel",))


def _hgrn_bwd(proj, lbs, states, d_o, consts, dproj, job=None):
    S = proj.shape[0]
    nc = S // HG_CHUNK

    def body(q_ref, f_ref, i_ref, lb_ref, st_ref, do_ref, tri_ref, trit_ref, mask_ref, _, dlb_ref, dproj_ref, stage, sems):
        dq_ref, df_ref, di_ref = stage.at[0], stage.at[1], stage.at[2]
        masks = [mask_ref[n] for n in range(len(HG_LEVELS) + 1)]
        fn = functools.partial(_hgrn_chunk, tri_ref[...], trit_ref[...], masks)
        heads = range(HG_HEADS_PER_STEP)

        def chunk(n, carry):
            ci = nc - 1 - n
            r0 = pl.multiple_of(ci * HG_CHUNK, HG_CHUNK)
            rows = pl.ds(r0, HG_CHUNK)
            new = []
            for hd in heads:
                d_st, dlb = carry[hd]
                lanes = pl.ds(hd * HG_DIM, HG_DIM)
                _, pull = jax.vjp(fn, q_ref[rows, lanes], f_ref[rows, lanes], i_ref[rows, lanes], st_ref[hd, ci], lb_ref[hd])
                dq, df, di, d_prev, dl = pull((do_ref[rows, lanes], d_st))
                dq_ref[rows, lanes] = dq.astype(BF)
                df_ref[rows, lanes] = df.astype(BF)
                di_ref[rows, lanes] = di.astype(BF)
                new.append((d_prev, dlb + dl))
            return tuple(new)

        zero = (jnp.zeros((HG_DIM, HG_DIM), F32), jnp.zeros((1, HG_DIM), F32))
        done = lax.fori_loop(0, nc, chunk, tuple(zero for _ in heads))
        for hd in heads:
            dlb_ref[hd] = done[hd][1]
        lane0 = pl.program_id(0) * HG_LANES
        _put_columns([dq_ref, df_ref, di_ref], dproj_ref, [WIDTH * n + lane0 for n in (4, 5, 6)], sems)

    col, whole = _hgrn_specs(S, consts)
    head = pl.BlockSpec((HG_HEADS_PER_STEP, 1, 128), lambda p: (p, 0, 0))
    hbm = pl.BlockSpec(memory_space=pl.ANY)
    n_in = 6 + len(consts)
    return _hosted(
        body, job, (proj, proj, proj, lbs, states, d_o, *consts, dproj), name="hgrn_bwd",
        grid=(WIDTH // HG_LANES,),
        in_specs=[col(16), col(20), col(24), head, pl.BlockSpec((HG_HEADS_PER_STEP, nc, HG_DIM, HG_DIM), lambda p: (p, 0, 0, 0)), col(0)]
        + whole + [hbm],
        out_specs=[head, hbm],
        out_shape=[jax.ShapeDtypeStruct((WIDTH // 128, 1, 128), F32), jax.ShapeDtypeStruct(dproj.shape, dproj.dtype)],
        scratch_shapes=[pltpu.VMEM((3, S, HG_LANES), BF), pltpu.SemaphoreType.DMA((3,))],
        semantics=("arbitrary",), aliases={n_in: 1})


def _shift_down(x, n):
    rows = lax.broadcasted_iota(jnp.int32, x.shape, 0)
    return jnp.where(rows >= n, pltpu.roll(x, n, 0), 0.0)


def _shift_up(x, n):
    S = x.shape[0]
    rows = lax.broadcasted_iota(jnp.int32, x.shape, 0)
    return jnp.where(rows < S - n, pltpu.roll(x, S - n, 0), 0.0)


def _branch_fwd(proj, o_a, o_b, norm_w, conv_w, layer):
    S = proj.shape[0]

    def body(oa_ref, za_ref, ob_ref, zb_ref, nw_ref, pre_ref, post_ref, u_ref, zc_ref, cw_ref, ya_ref, yb_ref, yc_ref):
        ya_ref[...] = (oa_ref[...] * _silu(za_ref[...])).astype(BF)
        ob = ob_ref[...]
        rn = lax.rsqrt(jnp.mean(ob * ob, axis=-1, keepdims=True) + RMS_EPS)
        yb_ref[...] = (ob * rn * nw_ref[layer:layer + 1, :] * _silu(zb_ref[...])).astype(BF)
        pu = pre_ref[...] * u_ref[...]
        conv = cw_ref[2:3, :] * pu + cw_ref[1:2, :] * _shift_down(pu, 1) + cw_ref[0:1, :] * _shift_down(pu, 2)
        yc_ref[...] = (post_ref[...] * conv * _silu(zc_ref[...])).astype(BF)

    col = lambda base: pl.BlockSpec((S, 128), lambda p, base=base: (0, base + p))
    out = jax.ShapeDtypeStruct((S, WIDTH), BF)
    return pl.pallas_call(
        body, name="branch_fwd",
        grid=(WIDTH // 128,),
        in_specs=[col(0), col(12), col(0), col(28), pl.BlockSpec(norm_w.shape, lambda p: (0, 0)),
                  col(32), col(36), col(40), col(44), pl.BlockSpec((None, None, 3, 128), lambda p: (p, layer, 0, 0))],
        out_specs=[col(0), col(0), col(0)],
        out_shape=[out, out, out],
        compiler_params=_params(dimension_semantics=("parallel",)),
    )(o_a, proj, o_b, proj, norm_w, proj, proj, proj, proj, conv_w)


def _put_columns(tiles, dproj_ref, firsts, sems):
    copies = [pltpu.make_async_copy(t, dproj_ref.at[:, pl.ds(pl.multiple_of(c, 128), t.shape[1])], sems.at[n])
              for n, (t, c) in enumerate(zip(tiles, firsts))]
    for cp in copies:
        cp.start()
    for cp in copies:
        cp.wait()


def _branch_bwd(proj, o_a, o_b, norm_w, conv_w, dy_a, dy_b, dy_c, dproj, layer):
    S = proj.shape[0]
    firsts = [WIDTH * n for n in (3, 7, 8, 9, 10, 11)]

    def dsilu(z):
        s = _sigmoid(z)
        return s * z, s * (1.0 + z * (1.0 - s))

    def body(oa_ref, za_ref, ob_ref, zb_ref, nw_ref, pre_ref, post_ref, u_ref, zc_ref, cw_ref, dya_ref, dyb_ref, dyc_ref, _,
             doa_ref, dob_ref, dnw_ref, dcw_ref, dproj_ref, stage, sems):
        dza_ref, dzb_ref, dpre_ref, dpost_ref, du_ref, dzc_ref = [stage.at[n] for n in range(6)]
        dya = dya_ref[...]
        sa, dsa = dsilu(za_ref[...])
        doa_ref[...] = dya * sa
        dza_ref[...] = (dya * oa_ref[...] * dsa).astype(BF)

        dyb = dyb_ref[...]
        ob = ob_ref[...]
        nw = nw_ref[layer:layer + 1, :]
        sb, dsb = dsilu(zb_ref[...])
        rn = lax.rsqrt(jnp.mean(ob * ob, axis=-1, keepdims=True) + RMS_EPS)
        on = ob * rn
        dzb_ref[...] = (dyb * on * nw * dsb).astype(BF)
        don_w = dyb * sb
        dnw_ref[...] = jnp.sum(don_w * on, axis=0, keepdims=True)
        don = don_w * nw
        dob_ref[...] = rn * (don - on * jnp.mean(don * on, axis=-1, keepdims=True))

        dyc = dyc_ref[...]
        pre, post, u = pre_ref[...], post_ref[...], u_ref[...]
        sc, dsc = dsilu(zc_ref[...])
        pu = pre * u
        pu1, pu2 = _shift_down(pu, 1), _shift_down(pu, 2)
        conv = cw_ref[2:3, :] * pu + cw_ref[1:2, :] * pu1 + cw_ref[0:1, :] * pu2
        dzc_ref[...] = (dyc * post * conv * dsc).astype(BF)
        dpost_ref[...] = (dyc * conv * sc).astype(BF)
        dconv = dyc * post * sc
        dcw_ref[0:1, :] = jnp.sum(dconv * pu2, axis=0, keepdims=True)
        dcw_ref[1:2, :] = jnp.sum(dconv * pu1, axis=0, keepdims=True)
        dcw_ref[2:3, :] = jnp.sum(dconv * pu, axis=0, keepdims=True)
        dpu = cw_ref[2:3, :] * dconv + cw_ref[1:2, :] * _shift_up(dconv, 1) + cw_ref[0:1, :] * _shift_up(dconv, 2)
        dpre_ref[...] = (dpu * u).astype(BF)
        du_ref[...] = (dpu * pre).astype(BF)
        lane0 = pl.program_id(0) * 128
        _put_columns([stage.at[n] for n in range(6)], dproj_ref, [c + lane0 for c in firsts], sems)

    col = lambda base: pl.BlockSpec((S, 128), lambda p, base=base: (0, base + p))
    f32 = jax.ShapeDtypeStruct((S, WIDTH), F32)
    hbm = pl.BlockSpec(memory_space=pl.ANY)
    return pl.pallas_call(
        body, name="branch_bwd",
        grid=(WIDTH // 128,),
        in_specs=[col(0), col(12), col(0), col(28), pl.BlockSpec(norm_w.shape, lambda p: (0, 0)),
                  col(32), col(36), col(40), col(44), pl.BlockSpec((None, None, 3, 128), lambda p: (p, layer, 0, 0)),
                  col(0), col(0), col(0), hbm],
        out_specs=[col(0), col(0), pl.BlockSpec((None, 1, 128), lambda p: (p, 0, 0)), pl.BlockSpec((None, 3, 128), lambda p: (p, 0, 0)), hbm],
        out_shape=[f32, f32, jax.ShapeDtypeStruct((WIDTH // 128, 1, 128), F32), jax.ShapeDtypeStruct((WIDTH // 128, 3, 128), F32),
                   jax.ShapeDtypeStruct(dproj.shape, dproj.dtype)],
        scratch_shapes=[pltpu.VMEM((6, S, 128), BF), pltpu.SemaphoreType.DMA((6,))],
        input_output_aliases={13: 4},
        compiler_params=_params(dimension_semantics=("arbitrary",)),
    )(o_a, proj, o_b, proj, norm_w, proj, proj, proj, proj, conv_w, dy_a, dy_b, dy_c, dproj)


def _branch_proj(y_refs, wb_ref):
    return [_dot(y_refs[i][...], wb_ref[i]) for i in range(3)]


def _merge_fwd(x, mod, proj, ys, wb, wo, ln_g, ln_b, layer, job=None, tm=512):
    S, D = x.shape
    tm = min(tm, S)

    def body(x_ref, mod_ref, ga_ref, gb_ref, gc_ref, ya_ref, yb_ref, yc_ref, wb_ref, wo_ref, g_ref, b_ref, xo_ref, mg_ref, y_ref):
        ps = _branch_proj((ya_ref, yb_ref, yc_ref), wb_ref)
        merged = _sigmoid(ga_ref[...]) * ps[0] + _sigmoid(gb_ref[...]) * ps[1] + _sigmoid(gc_ref[...]) * ps[2]
        mb = merged.astype(BF)
        mg_ref[...] = mb
        y = _dot(mb, wo_ref[...].reshape(D, D))
        y_ref[...] = y
        r = ALPHA * x_ref[...] + (1.0 + mod_ref[:, 2 * D:3 * D]) * y
        xn, _ = _standardize(r)
        xo_ref[...] = xn * g_ref[layer:layer + 1, :] + b_ref[layer:layer + 1, :]

    row = lambda w, c=0: pl.BlockSpec((tm, w), lambda i, c=c: (i, c))
    whole = lambda a: pl.BlockSpec(a.shape, lambda i, n=a.ndim: (0,) * n)
    return _hosted(
        body, job, (x, mod, proj, proj, proj, *ys, wb, wo, ln_g, ln_b), name="merge_fwd",
        grid=(S // tm,),
        in_specs=[row(D), whole(mod), row(D, 6), row(D, 7), row(D, 8), row(WIDTH), row(WIDTH), row(WIDTH), whole(wb),
                  whole(wo), whole(ln_g), whole(ln_b)],
        out_specs=[row(D), row(D), row(D)],
        out_shape=[jax.ShapeDtypeStruct((S, D), F32), jax.ShapeDtypeStruct((S, D), BF), jax.ShapeDtypeStruct((S, D), F32)],
        semantics=("parallel",))


def _merge_bwd(dxo, x, y, merged, mod, proj, ys, wb, wo, ln_g, layer, tm=256):
    S, D = x.shape
    tm = min(tm, S)
    steps = S // tm
    quarter = D // N_CHIPS

    def body(dxo_ref, x_ref, y_ref, mg_ref, mod_ref, ga_ref, gb_ref, gc_ref, ya_ref, yb_ref, yc_ref, wb_ref, wo_ref, g_ref,
             dxr_ref, dg_ref, dya_ref, dyb_ref, dyc_ref, dlg_ref, dlb_ref, dgt_ref, gbr_ref, gout_ref, acc_br, acc_out):
        @pl.when(pl.program_id(0) == 0)
        def _():
            dlg_ref[...] = jnp.zeros_like(dlg_ref)
            dlb_ref[...] = jnp.zeros_like(dlb_ref)
            dgt_ref[...] = jnp.zeros_like(dgt_ref)
            acc_br[...] = jnp.zeros_like(acc_br)
            acc_out[...] = jnp.zeros_like(acc_out)

        gate1 = 1.0 + mod_ref[:, 2 * D:3 * D]
        yv = y_ref[...]
        xn, rstd = _standardize(ALPHA * x_ref[...] + gate1 * yv)
        dxo = dxo_ref[...]
        dlg_ref[...] += jnp.sum(dxo * xn, axis=0, keepdims=True)
        dlb_ref[...] += jnp.sum(dxo, axis=0, keepdims=True)
        dr = _standardize_bwd(dxo * g_ref[layer:layer + 1, :], xn, rstd)
        dxr_ref[...] = ALPHA * dr
        dgt_ref[...] += jnp.sum(dr * yv, axis=0, keepdims=True)
        dyb = (gate1 * dr).astype(BF)
        acc_out[...] += _dot_tn(mg_ref[...], dyb)
        dmerged = _dot_nt(dyb, wo_ref[...].reshape(D, D))
        y_refs = (ya_ref, yb_ref, yc_ref)
        ps = _branch_proj(y_refs, wb_ref)
        for i, (gate_ref, out_ref) in enumerate(((ga_ref, dya_ref), (gb_ref, dyb_ref), (gc_ref, dyc_ref))):
            sg = _sigmoid(gate_ref[...])
            dg_ref[:, i * D:(i + 1) * D] = (dmerged * ps[i] * sg * (1.0 - sg)).astype(BF)
            dp = (dmerged * sg).astype(BF)
            acc_br[i] += _dot_tn(y_refs[i][...], dp)
            out_ref[...] = _dot_nt(dp, wb_ref[i])

        @pl.when(pl.program_id(0) == steps - 1)
        def _():
            for j in range(N_CHIPS):
                gout_ref[j] = acc_out[j * quarter:(j + 1) * quarter, :].astype(BF)
                for i in range(3):
                    gbr_ref[j, i] = acc_br[i, :, j * quarter:(j + 1) * quarter].astype(BF)

    row = lambda w, c=0: pl.BlockSpec((tm, w), lambda i, c=c: (i, c))
    whole = lambda a: pl.BlockSpec(a.shape, lambda i, n=len(a.shape): (0,) * n)
    vec = pl.BlockSpec((1, D), lambda i: (0, 0))
    sd = jax.ShapeDtypeStruct
    g_br, g_out = sd((N_CHIPS, 3, WIDTH, quarter), BF), sd((N_CHIPS, quarter, D), BF)
    return pl.pallas_call(
        body, name="merge_bwd",
        grid=(steps,),
        in_specs=[row(D), row(D), row(D), row(D), whole(mod), row(D, 6), row(D, 7), row(D, 8), row(WIDTH), row(WIDTH), row(WIDTH),
                  whole(wb), whole(wo), whole(ln_g)],
        out_specs=[row(D), row(3 * D, IN_COLS // (3 * D) - 1), row(WIDTH), row(WIDTH), row(WIDTH), vec, vec, vec, whole(g_br), whole(g_out)],
        out_shape=[sd((S, D), F32), sd((S, IN_COLS), BF), sd((S, WIDTH), F32), sd((S, WIDTH), F32), sd((S, WIDTH), F32),
                   sd((1, D), F32), sd((1, D), F32), sd((1, D), F32), g_br, g_out],
        scratch_shapes=[pltpu.VMEM((3, WIDTH, D), F32), pltpu.VMEM((D, D), F32)],
        compiler_params=_params(dimension_semantics=("arbitrary",)),
    )(dxo, x, y, merged, mod, proj, proj, proj, *ys, wb, wo, ln_g)


def _loss_head(x, target, tm=512):
    S, D = x.shape
    tm = min(tm, S)

    def body(x_ref, t_ref, dx_ref, loss_ref):
        @pl.when(pl.program_id(0) == 0)
        def _():
            loss_ref[...] = jnp.zeros_like(loss_ref)

        err = x_ref[...] - t_ref[...]
        dx_ref[...] = err * (1.0 / D)
        loss_ref[...] += 0.5 * jnp.sum(jnp.mean(err * err, axis=-1, keepdims=True))

    row = pl.BlockSpec((tm, D), lambda i: (i, 0))
    return pl.pallas_call(
        body, name="loss_head",
        grid=(S // tm,),
        in_specs=[row, row],
        out_specs=[row, pl.BlockSpec((8, 128), lambda i: (0, 0))],
        out_shape=[jax.ShapeDtypeStruct((S, D), F32), jax.ShapeDtypeStruct((8, 128), F32)],
        compiler_params=_params(dimension_semantics=("arbitrary",)),
    )(x, target)


def _proj_bwd(dproj, wgs, x, mod, dx_res, job=None, tm=512, tk=768):
    S, D = x.shape
    tm = min(tm, S)
    n = len(wgs)
    w_args = list(wgs)
    if n == 1:
        per = wgs[0].shape[-1] // tk
        w_specs = [pl.BlockSpec((None, D, tk), lambda k, i: (k // per, 0, k % per))]
    else:
        assert wgs[0].shape[-1] == tk
        w_specs = [pl.BlockSpec((None, D, tk), lambda k, i, c=c: (jnp.minimum((k + n - 1 - c) // n, N_CHIPS - 1), 0, 0))
                   for c in range(n)]
    nk = IN_COLS // tk

    def body(dp_ref, *rest):
        w_refs, (x_ref, mod_ref, dxr_ref, dx_ref, dsh_ref, dsc_ref, acc) = rest[:n], rest[n:]
        k, i = pl.program_id(0), pl.program_id(1)
        mine = pl.ds(pl.multiple_of(i * tm, tm), tm)

        @pl.when((i == 0) & (k == 0))
        def _():
            dsh_ref[...] = jnp.zeros_like(dsh_ref)
            dsc_ref[...] = jnp.zeros_like(dsc_ref)

        @pl.when(k == 0)
        def _():
            acc[mine, :] = jnp.zeros((tm, D), F32)

        for c in range(n):
            @pl.when(k % n == c)
            def _(c=c):
                acc[mine, :] += _dot_nt(dp_ref[...], w_refs[c][...])

        @pl.when(k == nk - 1)
        def _():
            dh = acc[mine, :]
            xs, rstd = _standardize(x_ref[...])
            dsh_ref[...] += jnp.sum(dh, axis=0, keepdims=True)
            dsc_ref[...] += jnp.sum(dh * xs, axis=0, keepdims=True)
            dx_ref[...] = _standardize_bwd(dh * (1.0 + mod_ref[:, D:2 * D]), xs, rstd) + dxr_ref[...]

    row = pl.BlockSpec((tm, D), lambda k, i: (jnp.where(k == nk - 1, i, 0), 0))
    vec = pl.BlockSpec((1, D), lambda k, i: (0, 0))
    return _hosted(
        body, job, (dproj, *w_args, x, mod, dx_res), name="proj_bwd",
        grid=(nk, S // tm),
        in_specs=[pl.BlockSpec((tm, tk), lambda k, i: (i, k))] + w_specs + [row, pl.BlockSpec((1, 3 * D), lambda k, i: (0, 0)), row],
        out_specs=[row, vec, vec],
        out_shape=[jax.ShapeDtypeStruct((S, D), F32), jax.ShapeDtypeStruct((1, D), F32), jax.ShapeDtypeStruct((1, D), F32)],
        scratch_shapes=[pltpu.VMEM((S, D), F32)],
        semantics=("arbitrary", "arbitrary"))


def _grad_w_in(h, dproj):
    S, D = h.shape
    shard = IN_COLS // N_CHIPS

    def body(h_ref, d_ref, o_ref):
        o_ref[...] = _dot_tn(h_ref[...], d_ref[...]).astype(BF)

    return pl.pallas_call(
        body, name="grad_w_in",
        grid=(N_CHIPS,),
        in_specs=[pl.BlockSpec((S, D), lambda n: (0, 0)), pl.BlockSpec((S, shard), lambda n: (0, n))],
        out_specs=pl.BlockSpec((None, D, shard), lambda n: (n, 0, 0)),
        out_shape=jax.ShapeDtypeStruct((N_CHIPS, D, shard), BF),
        compiler_params=_params(dimension_semantics=("parallel",)),
    )(h, dproj)


def _all_gather8(x, name):
    R, N = x.shape

    def body(x_ref, out_ref, send_sems, recv_sems):
        mx, my, mc = lax.axis_index("x"), lax.axis_index("y"), lax.axis_index("c")
        me = 4 * mx + 2 * my + mc
        out_ref[me] = x_ref[...]
        copies = []
        for k in range(1, N_DEV):
            peer = (_flip(mx, k & 4), _flip(my, k & 2), _flip(mc, k & 1))
            cp = pltpu.make_async_remote_copy(src_ref=x_ref, dst_ref=out_ref.at[me], send_sem=send_sems.at[k - 1],
                                              recv_sem=recv_sems.at[k - 1], device_id=peer, device_id_type=MESH)
            cp.start()
            copies.append(cp)
        for cp in copies:
            cp.wait()

    return pl.pallas_call(
        body, name=name,
        in_specs=[pl.BlockSpec(memory_space=pltpu.VMEM)],
        out_specs=pl.BlockSpec(memory_space=pltpu.VMEM),
        out_shape=jax.ShapeDtypeStruct((N_DEV, R, N), F32),
        scratch_shapes=[pltpu.SemaphoreType.DMA((N_DEV - 1,)), pltpu.SemaphoreType.DMA((N_DEV - 1,))],
        compiler_params=_params(),
    )(x)


def _rows2d(a):
    return a.reshape(-1, a.shape[-1])


def _tile_rows(rows, cols, n_arrays):
    budget = (24 << 20) // (n_arrays * 2 * 4 * cols)
    if rows <= budget:
        return rows
    tm = 8
    for cand in range(8, budget + 1, 8):
        if rows % cand == 0:
            tm = cand
    return tm


SUM_ROWS = 256


def _sum_cores(g, sent, where):
    chips, lead, _, r, cols = g.shape
    tr = min(r, SUM_ROWS)

    def body(where_ref, g_ref, s_ref, out_ref):
        out_ref[...] = (g_ref[...].astype(F32) + s_ref[...].astype(F32)).astype(BF)

    spec = pl.BlockSpec((None, tr, cols), lambda i, j, where_ref: (i, j, 0))
    out = pl.pallas_call(
        body, name="sum_cores",
        grid_spec=pltpu.PrefetchScalarGridSpec(
            num_scalar_prefetch=1, grid=(chips * lead, r // tr),
            in_specs=[pl.BlockSpec((None, None, tr, cols), lambda i, j, where_ref: (i, where_ref[1], j, 0)), spec],
            out_specs=spec),
        out_shape=jax.ShapeDtypeStruct((chips * lead, r, cols), BF),
        compiler_params=_params(dimension_semantics=("parallel", "parallel")),
    )(where, g.reshape(chips * lead, 2, r, cols), sent.reshape(chips * lead, r, cols))
    return out.reshape(chips, lead, r, cols)


def _sum_chips(sums, got, where):
    _, lead, r, cols = sums.shape
    tr = min(r, SUM_ROWS)

    def body(where_ref, s_ref, g_ref, out_ref):
        out_ref[...] = ((s_ref[...].astype(F32) + g_ref[0].astype(F32)) + g_ref[1].astype(F32)) + g_ref[2].astype(F32)

    return pl.pallas_call(
        body, name="sum_chips",
        grid_spec=pltpu.PrefetchScalarGridSpec(
            num_scalar_prefetch=1, grid=(lead, r // tr),
            in_specs=[pl.BlockSpec((None, None, tr, cols), lambda i, j, where_ref: (where_ref[0], i, j, 0)),
                      pl.BlockSpec((N_CHIPS - 1, None, tr, cols), lambda i, j, where_ref: (0, i, j, 0))],
            out_specs=pl.BlockSpec((None, None, tr, cols), lambda i, j, where_ref: (i, where_ref[1], j, 0))),
        out_shape=jax.ShapeDtypeStruct((lead, 2, r, cols), F32),
        compiler_params=_params(dimension_semantics=("parallel", "parallel")),
    )(where, sums, got)


def _adamw(w, m, v, groups, name):
    shape = w.shape
    w2, m2, v2 = _rows2d(w), _rows2d(m), _rows2d(v)
    rows, cols = w2.shape
    ng = len(groups)
    n = len(groups[0])
    slab = rows // ng
    gs = [_rows2d(g) for grp in groups for g in grp]
    tm = _tile_rows(slab, cols, 7 + n)
    tiles = slab // tm
    c1 = 1.0 / (1.0 - ADAM_B1 ** ADAM_STEP)
    c2 = 1.0 / (1.0 - ADAM_B2 ** ADAM_STEP)

    def body(*refs):
        w_ref, m_ref, v_ref = refs[:3]
        g_refs = refs[3:3 + ng * n]
        go_ref, d_ref, mo_ref, vo_ref = refs[3 + ng * n:]
        which = pl.program_id(0)
        for s in range(ng):
            @pl.when(which == s)
            def _(s=s):
                g = g_refs[s * n][...]
                for r in g_refs[s * n + 1:(s + 1) * n]:
                    g = g + r[...]
                mn = ADAM_B1 * m_ref[...] + (1.0 - ADAM_B1) * g
                vn = ADAM_B2 * v_ref[...] + (1.0 - ADAM_B2) * (g * g)
                go_ref[...] = g
                mo_ref[...] = mn
                vo_ref[...] = vn
                d_ref[...] = -ADAM_LR * ((mn * c1) / (jnp.sqrt(vn * c2) + ADAM_EPS) + ADAM_WD * w_ref[...])

    spec = pl.BlockSpec((tm, cols), lambda s, i: (s * tiles + i, 0))
    g_specs = [pl.BlockSpec((tm, cols), lambda s, i, k=k: (jnp.where(s == k, i, jnp.where(s < k, 0, tiles - 1)), 0))
               for k in range(ng) for _ in range(n)]
    outs = pl.pallas_call(
        body, name=name,
        grid=(ng, tiles),
        in_specs=[spec] * 3 + g_specs,
        out_specs=[spec] * 4,
        out_shape=[jax.ShapeDtypeStruct((rows, cols), F32)] * 4,
        compiler_params=_params(dimension_semantics=("arbitrary", "arbitrary")),
    )(w2, m2, v2, *gs)
    return [o.reshape(shape) for o in outs]


def _lower_bounds(r0, r1):
    top = jnp.maximum(r0, r1)
    e0, e1 = jnp.exp(r0 - top), jnp.exp(r1 - top)
    p0, p1 = e0 / (e0 + e1), e1 / (e0 + e1)
    return p0 - p0, (p0 + p1) - p0


def _lbs_fwd(lb_raw):
    def body(lb_ref, out_ref):
        l0, l1 = _lower_bounds(lb_ref[0:1, :], lb_ref[1:2, :])
        out_ref[0:1, :] = l0
        out_ref[1:2, :] = l1

    return pl.pallas_call(body, name="lower_bounds", out_shape=jax.ShapeDtypeStruct(lb_raw.shape, F32), compiler_params=_params())(lb_raw)


def _mod_rows(c_all, w_mod, tn=768):
    _, D, cols = w_mod.shape

    def body(c_ref, w_ref, out_ref):
        out_ref[...] = _dot(c_ref[...].astype(BF), w_ref[...].astype(BF))

    return pl.pallas_call(
        body, name="mod_rows",
        grid=(DEPTH,),
        in_specs=[pl.BlockSpec((N_DEV, D), lambda l: (0, 0)), pl.BlockSpec((None, D, cols), lambda l: (l, 0, 0))],
        out_specs=pl.BlockSpec((N_DEV, cols), lambda l: (0, l)),
        out_shape=jax.ShapeDtypeStruct((N_DEV, DEPTH * cols), F32),
        compiler_params=_params(dimension_semantics=("parallel",)),
    )(c_all, w_mod)


def _grad_w_mod(c_all, dmod_cols):
    D = c_all.shape[1]
    cols = dmod_cols.shape[-1]

    def body(c_ref, d_ref, out_ref):
        out_ref[...] = _dot_tn(c_ref[...].astype(BF), d_ref[...].astype(BF))

    return pl.pallas_call(
        body, name="grad_w_mod",
        grid=(DEPTH,),
        in_specs=[pl.BlockSpec((N_DEV, D), lambda l: (0, 0)), pl.BlockSpec((None, N_DEV, cols), lambda l: (l, 0, 0))],
        out_specs=pl.BlockSpec((None, D, cols), lambda l: (l, 0, 0)),
        out_shape=jax.ShapeDtypeStruct((DEPTH, D, cols), F32),
        compiler_params=_params(dimension_semantics=("parallel",)),
    )(c_all, dmod_cols)


def _sum_devices(parts):
    _, R, N = parts.shape

    def body(p_ref, out_ref):
        acc = p_ref[0]
        for d in range(1, N_DEV):
            acc = acc + p_ref[d]
        out_ref[...] = acc

    return pl.pallas_call(body, name="sum_devices", out_shape=jax.ShapeDtypeStruct((R, N), F32), compiler_params=_params())(parts)


def _lbs_bwd(lb_raw, dl):
    def body(lb_ref, dl_ref, out_ref):
        _, pull = jax.vjp(_lower_bounds, lb_ref[0:1, :], lb_ref[1:2, :])
        d0, d1 = pull((dl_ref[0:1, :], dl_ref[1:2, :]))
        out_ref[0:1, :] = d0
        out_ref[1:2, :] = d1

    return pl.pallas_call(body, name="lower_bounds_bwd", out_shape=jax.ShapeDtypeStruct(lb_raw.shape, F32), compiler_params=_params())(lb_raw, dl)


def kernel(x, c, w_mod, b_mod, w_in, conv_w, hgrn_norm_w, lower_bounds, w_branch, w_out, ln_g, ln_b, loss_target, m_w_mod, m_b_mod, m_w_in, m_conv_w, m_hgrn_norm_w, m_lower_bounds, m_w_branch, m_w_out, m_ln_g, m_ln_b, v_w_mod, v_b_mod, v_w_in, v_conv_w, v_hgrn_norm_w, v_lower_bounds, v_w_branch, v_w_out, v_ln_g, v_ln_b):
    D = D_MODEL
    x0 = x[0]
    target = loss_target[0]
    S = x0.shape[0]
    mx, my, mc = lax.axis_index("x"), lax.axis_index("y"), lax.axis_index("c")
    chip = 2 * mx + my
    me = 2 * chip + mc
    mod_cols = 3 * D // N_CHIPS

    plan = _Plan(w_in.astype(BF), w_branch.astype(BF), w_out.astype(BF), chip, mc)
    n_conv = DEPTH * 3 * (WIDTH // N_CHIPS)
    first = jnp.concatenate([c, conv_w.reshape(1, n_conv), jnp.zeros((1, 2 * D - D - n_conv), F32)], axis=1)
    first = plan.first(first.reshape(8, 2 * D // 8)).reshape(N_DEV, 2 * D)

    c_all = first[:, :D]
    conv_all = first[:, D:D + n_conv].reshape(N_DEV, DEPTH, 3, WIDTH // N_CHIPS)[0::2]
    mod_part = _all_gather8(_mod_rows(c_all, w_mod), "gather_mod")[0::2]
    mod_part = lax.dynamic_index_in_dim(mod_part, me, axis=1, keepdims=False).reshape(N_CHIPS, DEPTH, mod_cols)
    mods = [(mod_part[:, l].reshape(1, 3 * D) + b_mod[l][None, :]) for l in range(DEPTH)]
    lbs = _lbs_fwd(lower_bounds).reshape(DEPTH, WIDTH // 128, 1, 128)
    loss_blk, dx, small = _local_step(x0, target, mods, lbs, conv_all, hgrn_norm_w, ln_g, ln_b, plan)

    n_mod, n_nw, n_lb, n_ln, n_cw = DEPTH * 3 * D, DEPTH * 128, DEPTH * WIDTH, DEPTH * D, DEPTH * 3 * WIDTH
    row = jnp.concatenate(
        [jnp.concatenate([small[l][0], small[l][1], small[l][2]], axis=1) for l in range(DEPTH)]
        + [jnp.sum(small[l][3], axis=0) for l in range(DEPTH)]
        + [small[l][4].reshape(1, WIDTH) for l in range(DEPTH)]
        + [small[l][5] for l in range(DEPTH)] + [small[l][6] for l in range(DEPTH)]
        + [jnp.transpose(small[l][7], (1, 0, 2)).reshape(1, 3 * WIDTH) for l in range(DEPTH)]
        + [loss_blk[0:1, :]], axis=1)
    n_row = row.shape[1]
    fold = -(-n_row // (8 * 128)) * 128
    rows = jnp.concatenate([row, jnp.zeros((1, 8 * fold - n_row), F32)], axis=1).reshape(8, fold)

    whole, gathered = plan.finish(rows)
    grads = {kind: [[whole[(kind, l)]] for l in range(DEPTH)] for kind in ("in", "br", "out")}

    off_nw = n_mod
    off_lb = off_nw + n_nw
    off_lng = off_lb + n_lb
    off_lnb = off_lng + n_ln
    off_cw = off_lnb + n_ln
    off_loss = off_cw + n_cw
    total = _sum_devices(gathered).reshape(1, 8 * fold)
    gathered = gathered.reshape(N_DEV, 1, 8 * fold)
    d_lower = _lbs_bwd(lower_bounds, total[0, off_lb:off_lng].reshape(DEPTH, WIDTH))
    loss = total[0, off_loss]
    d_b_mod = total[0, :n_mod].reshape(DEPTH, 3 * D)
    d_norm_w = total[0, off_nw:off_lb].reshape(DEPTH, 128)
    d_ln_g = total[0, off_lng:off_lnb].reshape(DEPTH, D)
    d_ln_b = total[0, off_lnb:off_cw].reshape(DEPTH, D)
    d_conv = total[0, off_cw:off_loss].reshape(DEPTH, 3, N_CHIPS, WIDTH // N_CHIPS)
    d_conv = lax.dynamic_index_in_dim(d_conv, chip, axis=2, keepdims=False)
    dmod_all = gathered[:, 0, :n_mod].reshape(N_DEV, DEPTH, N_CHIPS, mod_cols)
    dmod_cols = jnp.transpose(lax.dynamic_index_in_dim(dmod_all, chip, axis=2, keepdims=False), (1, 0, 2))
    d_w_mod = _grad_w_mod(c_all, dmod_cols)

    res = {}
    res["w_mod"] = _adamw(w_mod, m_w_mod, v_w_mod, [[d_w_mod]], "adamw_w_mod")
    res["b_mod"] = _adamw(b_mod, m_b_mod, v_b_mod, [[d_b_mod]], "adamw_b_mod")
    res["w_in"] = _adamw(w_in, m_w_in, v_w_in, grads["in"], "adamw_w_in")
    res["conv_w"] = _adamw(conv_w, m_conv_w, v_conv_w, [[d_conv]], "adamw_conv_w")
    res["hgrn_norm_w"] = _adamw(hgrn_norm_w, m_hgrn_norm_w, v_hgrn_norm_w, [[d_norm_w]], "adamw_norm_w")
    res["lower_bounds"] = _adamw(lower_bounds, m_lower_bounds, v_lower_bounds, [[d_lower]], "adamw_lower_bounds")
    res["w_branch"] = _adamw(w_branch, m_w_branch, v_w_branch, grads["br"], "adamw_w_branch")
    res["w_out"] = _adamw(w_out, m_w_out, v_w_out, grads["out"], "adamw_w_out")
    res["ln_g"] = _adamw(ln_g, m_ln_g, v_ln_g, [[d_ln_g]], "adamw_ln_g")
    res["ln_b"] = _adamw(ln_b, m_ln_b, v_ln_b, [[d_ln_b]], "adamw_ln_b")
    names = ["w_mod", "b_mod", "w_in", "conv_w", "hgrn_norm_w", "lower_bounds", "w_branch", "w_out", "ln_g", "ln_b"]
    return (loss, dx[None], *[res[n][0] for n in names], *[res[n][1] for n in names],
            *[res[n][2] for n in names], *[res[n][3] for n in names])


class _Plan:
    FIRST_CHUNKS = 3
    WINDOWS = ((0, 1152), (1152, 640), (1792, 512))

    def __init__(self, w_in, w_br, w_out, chip, core):
        self.local = {"in": w_in, "br": w_br, "out": w_out}
        self.chip, self.where = chip, jnp.stack([chip, core]).astype(jnp.int32)
        self.gathered, self.partial, self.grads, self.chip_sums, self.scattered, self.pending = {}, {}, {}, {}, {}, {}

    def chunks(self, l):
        return self.FIRST_CHUNKS if l == 0 else 1

    def w_in(self, l):
        return [self.gathered[("in", l, c)] for c in range(self.chunks(l))]

    def _shard(self, key):
        mine = self.local[key[0]][key[1]]
        if key[0] == "in":
            cols = mine.shape[-1] // self.chunks(key[1])
            mine = mine[:, key[2] * cols:(key[2] + 1) * cols]
        return mine

    def _slab(self, key):
        mine = _halves(self._shard(key))
        return lax.dynamic_update_slice(lax.empty((N_CHIPS,) + mine.shape, mine.dtype), mine[None], (self.chip, 0, 0, 0, 0))

    def _gather(self, keys):
        return ("gather", keys), _gather_job([self._slab(key) for key in keys])

    def _gather_window(self, key, n):
        slab = self._slab(key) if n == 0 else self.partial[key]
        return ("gather" if n == len(self.WINDOWS) - 1 else "gather_part", [key]), _gather_job([slab], self.WINDOWS[n])

    def _to_sibling(self, keys):
        return ("to_sibling", keys), _to_sibling_job([_halves(self.grads[key], 1) for key in keys])

    def _scatter(self, keys):
        return ("scatter", keys), _scatter_job([self.chip_sums[key] for key in keys])

    def job(self, stage, l, c=0):
        parts = []
        if stage == "proj_fwd":
            parts = [self._gather([("in", l, c + 1)] if c + 1 < self.chunks(l) else [("br", l), ("out", l)])]
        elif stage == "attn_fwd" and l + 1 < DEPTH:
            parts = [self._gather_window(("in", l + 1, 0), 0)]
        elif stage == "hgrn_fwd" and l + 1 < DEPTH:
            parts = [self._gather_window(("in", l + 1, 0), 1)]
        elif stage == "merge_fwd" and l + 1 < DEPTH:
            parts = [self._gather_window(("in", l + 1, 0), 2)]
        elif stage == "attn_bwd":
            parts = [self._to_sibling([("out", l), ("br", l)])] + ([self._scatter([("in", l + 1)])] if l + 1 < DEPTH else [])
        elif stage == "hgrn_bwd":
            parts = [self._scatter([("out", l), ("br", l)])]
        elif stage == "proj_bwd":
            parts = [self._to_sibling([("in", l)])] if l else [self._scatter([("in", 0)])]
        self.pending[(stage, l, c)] = [(tag, len(job.outs)) for tag, job in parts]
        return _join_jobs([job for _, job in parts])

    def done(self, stage, l, outs, c=0):
        if outs is None:
            return
        at = 0
        for (what, keys), n_outs in self.pending[(stage, l, c)]:
            mine, at = outs[at:at + n_outs], at + n_outs
            for n, key in enumerate(keys):
                if what == "gather":
                    self.gathered[key] = mine[n].reshape((N_CHIPS,) + self._shard(key).shape)
                elif what == "gather_part":
                    self.partial[key] = mine[n]
                elif what == "to_sibling":
                    self.chip_sums[key] = _sum_cores(_halves(self.grads[key], 1), mine[n], self.where)
                else:
                    self.scattered[key] = mine[n]

    def first(self, rows):
        tag, job = self._gather([("in", 0, 0)])
        self.pending[("first", 0, 0)] = [(tag, len(job.outs))]
        outs = _run_job(_join_jobs([job, _gather8_job(rows)]), "gather_first")
        self.done("first", 0, outs[:-1])
        return outs[-1]

    def took(self, key, grad):
        self.grads[key] = grad
        if key == ("in", 0):
            tag, job = self._to_sibling([key])
            self.pending[("took", 0, 0)] = [(tag, len(job.outs))]
            self.done("took", 0, _run_job(job, "to_sibling_last"))

    def finish(self, rows):
        keys = [(kind, l) for kind in ("in", "br", "out") for l in range(DEPTH)]
        halves = [_sum_chips(self.chip_sums[key], self.scattered[key], self.where) for key in keys]
        outs = _run_job(_join_jobs([_place_job(halves), _gather8_job(rows)]), "place_halves")
        return {key: w.reshape(self.grads[key].shape[1:]) for key, w in zip(keys, outs[:-1])}, outs[-1]


def _local_step(x0, target, mods, lbs, conv_all, hgrn_norm_w, ln_g, ln_b, plan):
    D = D_MODEL
    after, before = _attn_consts()
    hg_consts = _hgrn_consts()

    saved = []
    xl = x0
    for l in range(DEPTH):
        n = plan.chunks(l)
        (proj, h), got = _proj_fwd(xl, mods[l], plan.gathered[("in", l, 0)], plan.job("proj_fwd", l, 0), (0, n))
        plan.done("proj_fwd", l, got, 0)
        for c in range(1, n):
            (proj,), got = _proj_cols(h, plan.gathered[("in", l, c)], (c, n), proj, plan.job("proj_fwd", l, c))
            plan.done("proj_fwd", l, got, c)
        (o_a, tot), got = _attn_fwd(proj, after, plan.job("attn_fwd", l))
        plan.done("attn_fwd", l, got)
        (o_b, states), got = _hgrn_fwd(proj, lbs[l], hg_consts, plan.job("hgrn_fwd", l))
        plan.done("hgrn_fwd", l, got)
        ys = _branch_fwd(proj, o_a, o_b, hgrn_norm_w, conv_all, l)
        wb = jnp.concatenate(list(plan.gathered[("br", l)]), axis=-1)
        (x_next, merged, y), got = _merge_fwd(xl, mods[l], proj, ys, wb, plan.gathered[("out", l)], ln_g, ln_b, l, plan.job("merge_fwd", l))
        plan.done("merge_fwd", l, got)
        saved.append((xl, proj, h, o_a, tot, o_b, states, ys, merged, y, wb))
        xl = x_next
    dx, loss_blk = _loss_head(xl, target)

    small = [None] * DEPTH
    for l in reversed(range(DEPTH)):
        xin, proj, h, o_a, tot, o_b, states, ys, merged, y, wb = saved[l]
        dx_res, dproj, dy_a, dy_b, dy_c, dln_g, dln_b, dgate, g_br, g_out = _merge_bwd(
            dx, xin, y, merged, mods[l], proj, ys, wb, plan.gathered[("out", l)], ln_g, l)
        plan.took(("out", l), g_out)
        plan.took(("br", l), g_br)
        d_oa, d_ob, dnorm_w, dconv_w, dproj = _branch_bwd(proj, o_a, o_b, hgrn_norm_w, conv_all, dy_a, dy_b, dy_c, dproj, l)
        (dproj,), got = _attn_bwd(proj, d_oa, tot, after, before, dproj, plan.job("attn_bwd", l))
        plan.done("attn_bwd", l, got)
        (dlb, dproj), got = _hgrn_bwd(proj, lbs[l], states, d_ob, hg_consts, dproj, plan.job("hgrn_bwd", l))
        plan.done("hgrn_bwd", l, got)
        plan.took(("in", l), _grad_w_in(h, dproj))
        (dx, dshift, dscale), got = _proj_bwd(dproj, plan.w_in(l), xin, mods[l], dx_res, plan.job("proj_bwd", l))
        plan.done("proj_bwd", l, got)
        small[l] = (dshift, dscale, dgate, dnorm_w, dlb, dln_g, dln_b, dconv_w)
    return loss_blk, dx, small
```

```python
import functools
import math

import numpy as np
import jax
import jax.numpy as jnp
from jax import lax
from jax.experimental import pallas as pl
from jax.experimental.pallas import tpu as pltpu

F32 = jnp.float32
BF = jnp.bfloat16
MESH = pl.DeviceIdType.MESH

DEPTH = 2
D_MODEL = 1024
WIDTH = 512
IN_COLS = 12 * WIDTH + 3 * D_MODEL
N_CHIPS = 4
N_DEV = 8
SB_BLOCK = 128
SB_HEAD_DIM = 64
HG_CHUNK = 128
HG_DIM = 128
LN_EPS = 1e-5
RMS_EPS = 1e-6
ALPHA = (2.0 * DEPTH) ** 0.25
ADAM_LR, ADAM_B1, ADAM_B2, ADAM_EPS, ADAM_WD, ADAM_STEP = 0.001, 0.9, 0.999, 1e-08, 0.01, 10
VMEM_LIMIT = 56 << 20


def _params(**kw):
    return pltpu.CompilerParams(vmem_limit_bytes=VMEM_LIMIT, **kw)


def _dot(a, b):
    return jnp.dot(a, b, preferred_element_type=F32)


def _dot_nt(a, b):
    return lax.dot_general(a, b, (((1,), (1,)), ((), ())), preferred_element_type=F32)


def _dot_tn(a, b):
    return lax.dot_general(a, b, (((0,), (0,)), ((), ())), preferred_element_type=F32)


def _sigmoid(x):
    return 1.0 / (1.0 + jnp.exp(-x))


def _silu(x):
    return x * _sigmoid(x)


def _split_dot(t, g, terms):
    acc = None
    rest = g
    for _ in range(terms):
        part = rest.astype(BF)
        rest = rest - part.astype(F32)
        d = _dot(t, part)
        acc = d if acc is None else acc + d
    return acc


def _standardize(x):
    mu = jnp.mean(x, axis=-1, keepdims=True)
    xc = x - mu
    var = jnp.mean(xc * xc, axis=-1, keepdims=True)
    rstd = lax.rsqrt(var + LN_EPS)
    return xc * rstd, rstd


def _standardize_bwd(dxs, xs, rstd):
    return rstd * (dxs - jnp.mean(dxs, axis=-1, keepdims=True) - xs * jnp.mean(dxs * xs, axis=-1, keepdims=True))


class _Job:
    def __init__(self, ins, outs, sems, make, alias=None):
        self.ins, self.outs, self.sems, self.make = list(ins), list(outs), list(sems), make
        self.alias = dict(alias or {})


def _join_jobs(jobs):
    jobs = [j for j in jobs if j is not None]
    if len(jobs) <= 1:
        return jobs[0] if jobs else None

    def make(ins, outs, sems):
        phases, i, o, s = [], 0, 0, 0
        for j in jobs:
            got = j.make(ins[i:i + len(j.ins)], outs[o:o + len(j.outs)], sems[s:s + len(j.sems)])
            i, o, s = i + len(j.ins), o + len(j.outs), s + len(j.sems)
            for n, phase in enumerate(got):
                if n == len(phases):
                    phases.append([])
                phases[n] += phase
        return phases

    alias, i, o = {}, 0, 0
    for j in jobs:
        alias.update({i + a: o + b for a, b in j.alias.items()})
        i, o = i + len(j.ins), o + len(j.outs)
    return _Job(sum([j.ins for j in jobs], []), sum([j.outs for j in jobs], []), sum([j.sems for j in jobs], []), make, alias)


def _flip(v, bit):
    return 1 - v if bit else v


def _halves(a, front=0):
    shape = a.shape
    lead = math.prod(shape[front:-2])
    return a.reshape(shape[:front] + (lead, 2, shape[-2] // 2, shape[-1]))


def _dma_sems(*shapes):
    return [pltpu.SemaphoreType.DMA(s) for s in shapes]


def _same(arrays):
    return [jax.ShapeDtypeStruct(a.shape, a.dtype) for a in arrays]


def _gather_job(slabs, window=None):
    n = len(slabs)
    cols = slice(None) if window is None else pl.ds(*window)

    def make(ins, outs, sems):
        send1, recv1, send2, recv2 = sems
        mx, my, mc = lax.axis_index("x"), lax.axis_index("y"), lax.axis_index("c")
        fetch, pass_on = [], []
        for a in range(n):
            ours = outs[a].at[2 * mx + my, :, mc, :, cols]
            for k in range(1, N_CHIPS):
                px, py = _flip(mx, k & 2), _flip(my, k & 1)
                fetch.append(pltpu.make_async_remote_copy(
                    src_ref=ours, dst_ref=ours, send_sem=send1.at[a, k - 1], recv_sem=recv1.at[a, k - 1],
                    device_id=(px, py, mc), device_id_type=MESH))
                theirs = outs[a].at[2 * px + py, :, mc, :, cols]
                pass_on.append(pltpu.make_async_remote_copy(
                    src_ref=theirs, dst_ref=theirs, send_sem=send2.at[a, k - 1], recv_sem=recv2.at[a, k - 1],
                    device_id=(mx, my, 1 - mc), device_id_type=MESH))
        return [fetch, pass_on]

    pairs = (n, N_CHIPS - 1)
    return _Job(slabs, _same(slabs), _dma_sems(pairs, pairs, pairs, pairs), make, {a: a for a in range(n)})


def _to_sibling_job(grads):
    n = len(grads)

    def make(ins, outs, sems):
        send_sems, recv_sems = sems
        mx, my, mc = lax.axis_index("x"), lax.axis_index("y"), lax.axis_index("c")
        return [[pltpu.make_async_remote_copy(
            src_ref=ins[a].at[:, :, 1 - mc], dst_ref=outs[a], send_sem=send_sems.at[a], recv_sem=recv_sems.at[a],
            device_id=(mx, my, 1 - mc), device_id_type=MESH) for a in range(n)]]

    outs = [jax.ShapeDtypeStruct(g.shape[:2] + g.shape[3:], g.dtype) for g in grads]
    return _Job(grads, outs, _dma_sems((n,), (n,)), make)


def _scatter_job(sums):
    n = len(sums)

    def make(ins, outs, sems):
        send_sems, recv_sems = sems
        mx, my, mc = lax.axis_index("x"), lax.axis_index("y"), lax.axis_index("c")
        copies = []
        for a in range(n):
            for k in range(1, N_CHIPS):
                px, py = _flip(mx, k & 2), _flip(my, k & 1)
                copies.append(pltpu.make_async_remote_copy(
                    src_ref=ins[a].at[2 * px + py], dst_ref=outs[a].at[k - 1], send_sem=send_sems.at[a, k - 1],
                    recv_sem=recv_sems.at[a, k - 1], device_id=(px, py, mc), device_id_type=MESH))
        return [copies]

    pairs = (n, N_CHIPS - 1)
    return _Job(sums, [jax.ShapeDtypeStruct((N_CHIPS - 1,) + s.shape[1:], s.dtype) for s in sums], _dma_sems(pairs, pairs), make)


def _place_job(wholes):
    n = len(wholes)

    def make(ins, outs, sems):
        send_sems, recv_sems = sems
        mx, my, mc = lax.axis_index("x"), lax.axis_index("y"), lax.axis_index("c")
        copies = []
        for a in range(n):
            here = outs[a].at[:, mc]
            copies.append(pltpu.make_async_remote_copy(src_ref=here, dst_ref=here, send_sem=send_sems.at[a], recv_sem=recv_sems.at[a],
                                                       device_id=(mx, my, 1 - mc), device_id_type=MESH))
        return [copies]

    return _Job(wholes, _same(wholes), _dma_sems((n,), (n,)), make, {a: a for a in range(n)})


def _gather8_job(x):
    def make(ins, outs, sems):
        local_sem, send_sems, recv_sems = sems
        mx, my, mc = lax.axis_index("x"), lax.axis_index("y"), lax.axis_index("c")
        here = outs[0].at[4 * mx + 2 * my + mc]
        copies = [pltpu.make_async_copy(ins[0], here, local_sem.at[0])]
        for k in range(1, N_DEV):
            peer = (_flip(mx, k & 4), _flip(my, k & 2), _flip(mc, k & 1))
            copies.append(pltpu.make_async_remote_copy(src_ref=ins[0], dst_ref=here, send_sem=send_sems.at[k - 1],
                                                       recv_sem=recv_sems.at[k - 1], device_id=peer, device_id_type=MESH))
        return [copies]

    return _Job([x], [jax.ShapeDtypeStruct((N_DEV,) + x.shape, x.dtype)], _dma_sems((1,), (N_DEV - 1,), (N_DEV - 1,)), make)


def _run_phases(phases, first=0):
    for n, phase in enumerate(phases):
        if n >= first:
            for cp in phase:
                cp.start()
        for cp in phase:
            cp.wait()


def _run_job(job, name):
    k_in, k_out = len(job.ins), len(job.outs)

    def body(*refs):
        _run_phases(job.make(refs[:k_in], refs[k_in:k_in + k_out], refs[k_in + k_out:]))

    hbm = pl.BlockSpec(memory_space=pl.ANY)
    return pl.pallas_call(body, name=name, in_specs=[hbm] * k_in, out_specs=[hbm] * k_out, out_shape=job.outs,
                          scratch_shapes=job.sems, input_output_aliases=job.alias, compiler_params=_params())(*job.ins)


def _hosted(body, job, args, *, name, grid, in_specs, out_specs, out_shape, scratch_shapes=(), semantics, aliases=None):
    in_specs, out_specs, out_shape, scratch = list(in_specs), list(out_specs), list(out_shape), list(scratch_shapes)
    aliases = dict(aliases or {})
    if job is None:
        outs = pl.pallas_call(body, name=name, grid=grid, in_specs=in_specs, out_specs=out_specs, out_shape=out_shape,
                              scratch_shapes=scratch, input_output_aliases=aliases,
                              compiler_params=_params(dimension_semantics=semantics))(*args)
        return list(outs), None
    n_in, n_out, n_scr, k_in, k_out = len(in_specs), len(out_specs), len(scratch), len(job.ins), len(job.outs)

    def wrapped(*refs):
        ins, rest = refs[:n_in], refs[n_in:]
        job_ins, rest = rest[:k_in], rest[k_in:]
        outs, rest = rest[:n_out], rest[n_out:]
        job_outs, rest = rest[:k_out], rest[k_out:]
        scr, sems = rest[:n_scr], rest[n_scr:]
        ids = [pl.program_id(a) for a in range(len(grid))]
        first = functools.reduce(jnp.logical_and, [i == 0 for i in ids])
        last = functools.reduce(jnp.logical_and, [i == g - 1 for i, g in zip(ids, grid)])

        @pl.when(first)
        def _():
            for cp in job.make(job_ins, job_outs, sems)[0]:
                cp.start()

        body(*ins, *outs, *scr)

        @pl.when(last)
        def _():
            _run_phases(job.make(job_ins, job_outs, sems), first=1)

    hbm = pl.BlockSpec(memory_space=pl.ANY)
    outs = pl.pallas_call(
        wrapped, name=name, grid=grid, in_specs=in_specs + [hbm] * k_in, out_specs=out_specs + [hbm] * k_out,
        out_shape=out_shape + job.outs, scratch_shapes=scratch + job.sems,
        input_output_aliases={**aliases, **{n_in + i: n_out + o for i, o in job.alias.items()}},
        compiler_params=_params(dimension_semantics=("arbitrary",) * len(grid)))(*args, *job.ins)
    return list(outs[:n_out]), list(outs[n_out:])


def _proj_fwd(x, mod, wg, job=None, chunk=(0, 1), tm=512):
    S, D = x.shape
    tm = min(tm, S)
    tn = wg.shape[-1]
    c, n = chunk

    rows = S // tm

    def body(x_ref, mod_ref, w_ref, proj_ref, h_ref, hs):
        mine = pl.ds(pl.multiple_of(pl.program_id(1) * tm, tm), tm)

        @pl.when(pl.program_id(0) == 0)
        def _():
            xs, _ = _standardize(x_ref[...])
            h = xs * (1.0 + mod_ref[:, D:2 * D]) + mod_ref[:, 0:D]
            hb = h.astype(BF)
            hs[mine, :] = hb
            h_ref[...] = hb

        proj_ref[...] = _dot(hs[mine, :], w_ref[...])

    once = lambda j, i: jnp.where(j == 0, i, rows - 1)
    return _hosted(
        body, job, (x, mod, wg), name="proj_fwd",
        grid=(N_CHIPS, rows),
        in_specs=[pl.BlockSpec((tm, D), lambda j, i: (once(j, i), 0)),
                  pl.BlockSpec((1, 3 * D), lambda j, i: (0, 0)),
                  pl.BlockSpec((None, D, tn), lambda j, i: (j, 0, 0))],
        out_specs=[pl.BlockSpec((tm, tn), lambda j, i: (i, j * n + c)),
                   pl.BlockSpec((tm, D), lambda j, i: (once(j, i), 0))],
        out_shape=[jax.ShapeDtypeStruct((S, IN_COLS), F32), jax.ShapeDtypeStruct((S, D), BF)],
        scratch_shapes=[pltpu.VMEM((S, D), BF)],
        semantics=("arbitrary", "arbitrary"))


def _proj_cols(h, wg, chunk, proj, job=None, tm=512):
    S, D = h.shape
    tm = min(tm, S)
    tn = wg.shape[-1]
    c, n = chunk

    def body(h_ref, w_ref, prev_ref, proj_ref):
        mine = pl.ds(pl.multiple_of(pl.program_id(1) * tm, tm), tm)
        proj_ref[...] = _dot(h_ref[mine, :], w_ref[...])

    in_specs = [pl.BlockSpec((S, D), lambda j, i: (0, 0)), pl.BlockSpec((None, D, tn), lambda j, i: (j, 0, 0)),
                pl.BlockSpec(memory_space=pl.ANY)]
    return _hosted(body, job, (h, wg, proj), name="proj_cols", grid=(N_CHIPS, S // tm), in_specs=in_specs,
                   out_specs=[pl.BlockSpec((tm, tn), lambda j, i: (i, j * n + c))],
                   out_shape=[jax.ShapeDtypeStruct(proj.shape, proj.dtype)],
                   semantics=("arbitrary", "arbitrary"), aliases={2: 0})


SB_ROWS = 256
SB_KEYS = 256


def _attn_consts():
    j = np.arange(SB_KEYS)[:, None]
    s = np.arange(SB_KEYS)[None, :]
    from_here = np.concatenate([(j >= s), (j >= s)], axis=0).astype(np.float32)
    return jnp.asarray(from_here, BF), jnp.asarray((j <= s).astype(np.float32), BF)


def _hi_lo(x):
    hi = lax.bitcast_convert_type(lax.bitcast_convert_type(x, jnp.uint32) & jnp.uint32(0xFFFF0000), F32)
    return hi.astype(BF), (x - hi).astype(BF)


def _sums_r(x, t2):
    hi, lo = _hi_lo(x)
    return _dot(jnp.concatenate([hi, lo], axis=1), t2)


def _all_lanes(col, lanes):
    return jnp.broadcast_to(col, (col.shape[0], lanes))


def _attn_rows(ref, r0, rows, lanes, head0, scale=None):
    v = ref[pl.ds(r0, rows), lanes]
    if scale is not None:
        v = v * scale
    return jnp.concatenate([jnp.where(head0, v, 0.0), jnp.where(head0, 0.0, v)], axis=0).astype(BF)


SB_PAIRS_FWD = 4
SB_PAIRS_BWD = 2


def _attn_specs(S, n_pairs):
    return lambda base: pl.BlockSpec((S, n_pairs * SB_BLOCK), lambda p, base=base: (0, base // n_pairs + p))


def _attn_scores(q2n, k_ref, lanes, kj, t2, from_here_ref, masked):
    c0 = pl.multiple_of(kj * SB_KEYS, SB_KEYS)
    kb = k_ref[pl.ds(c0, SB_KEYS), lanes].astype(BF)
    zn = _dot_nt(q2n, kb)
    lsb = jnp.minimum(zn, 0.0) - jnp.log(1.0 + jnp.exp(-jnp.abs(zn)))
    valid = None
    if masked:
        valid = (lax.broadcasted_iota(jnp.int32, zn.shape, 1) + kj * SB_KEYS) < t2
        lsb = jnp.where(valid, lsb, 0.0)
    return c0, kb, zn, valid, lsb, _sums_r(lsb, from_here_ref[...])


def _attn_fwd(proj, from_here, job=None):
    S = proj.shape[0]
    TQ = SB_ROWS
    assert S % TQ == 0 and SB_KEYS == TQ
    scale = SB_HEAD_DIM ** -0.5
    n_pairs = SB_PAIRS_FWD
    pairs = range(n_pairs)
    lanes = [pl.ds(p * SB_BLOCK, SB_BLOCK) for p in pairs]

    def body(q_ref, k_ref, v_ref, from_here_ref, o_ref, tot_ref, run, acc):
        head0 = lax.broadcasted_iota(jnp.int32, (1, 2 * SB_HEAD_DIM), 1) < SB_HEAD_DIM

        def qloop(qi, _):
            r0 = pl.multiple_of(qi * TQ, TQ)
            q2n = [_attn_rows(q_ref, r0, TQ, lanes[p], head0, -scale) for p in pairs]
            trow = lax.broadcasted_iota(jnp.int32, (TQ, SB_KEYS), 0) + qi * TQ
            t2 = jnp.concatenate([trow, trow], axis=0)
            run[...] = jnp.zeros_like(run)
            acc[...] = jnp.zeros_like(acc)

            def step(kj, masked):
                got = [_attn_scores(q2n[p], k_ref, lanes[p], kj, t2, from_here_ref, masked) for p in pairs]
                for p in pairs:
                    c0, _, zn, valid, _, sums = got[p]
                    r = run[p]
                    e = sums - zn + jnp.concatenate([r, r], axis=1)
                    if masked:
                        e = jnp.where(valid, e, -jnp.inf)
                    acc[p] += _dot(jnp.exp(e).astype(BF), v_ref[pl.ds(c0, SB_KEYS), lanes[p]].astype(BF))
                    run[p] = r + _all_lanes(sums[:, 0:1], SB_BLOCK)

            step(qi, True)

            def below(n, _):
                step(qi - 1 - n, False)
                return 0

            lax.fori_loop(0, qi, below, 0)
            for p in pairs:
                o_ref[pl.ds(r0, TQ), lanes[p]] = jnp.where(head0, acc[p, 0:TQ, :], acc[p, TQ:2 * TQ, :])
                tot_ref[p, 0, pl.ds(r0, TQ), :] = run[p, 0:TQ, :]
                tot_ref[p, 1, pl.ds(r0, TQ), :] = run[p, TQ:2 * TQ, :]
            return 0

        lax.fori_loop(0, S // TQ, qloop, 0)

    col = _attn_specs(S, n_pairs)
    state = pltpu.VMEM((n_pairs, 2 * TQ, SB_BLOCK), F32)
    return _hosted(
        body, job, (proj, proj, proj, from_here), name="attn_fwd",
        grid=(WIDTH // (n_pairs * SB_BLOCK),),
        in_specs=[col(0), col(4), col(8), pl.BlockSpec(from_here.shape, lambda p: (0, 0))],
        out_specs=[col(0), pl.BlockSpec((n_pairs, 2, S, 128), lambda p: (p, 0, 0, 0))],
        out_shape=[jax.ShapeDtypeStruct((S, WIDTH), F32), jax.ShapeDtypeStruct((WIDTH // 128, 2, S, 128), F32)],
        scratch_shapes=[state, state],
        semantics=("parallel",))


def _attn_bwd(proj, d_o, tot, from_here, up_to, dproj, job=None):
    S = proj.shape[0]
    TQ = SB_ROWS
    assert S % TQ == 0 and SB_KEYS == TQ
    scale = SB_HEAD_DIM ** -0.5
    n_pairs = SB_PAIRS_BWD
    pairs = range(n_pairs)
    lanes = [pl.ds(p * SB_BLOCK, SB_BLOCK) for p in pairs]

    def body(q_ref, k_ref, v_ref, do_ref, tot_ref, from_here_ref, up_to_ref, _, dproj_ref, pre, cum, dq_acc, dk_acc, dv_acc, stage, sems):
        dq_ref, dk_ref, dv_ref = stage.at[0], stage.at[1], stage.at[2]
        head0 = lax.broadcasted_iota(jnp.int32, (1, 2 * SB_HEAD_DIM), 1) < SB_HEAD_DIM
        dk_acc[...] = jnp.zeros_like(dk_acc)
        dv_acc[...] = jnp.zeros_like(dv_acc)

        def qloop(qi, _):
            r0 = pl.multiple_of(qi * TQ, TQ)
            q2n = [_attn_rows(q_ref, r0, TQ, lanes[p], head0, -scale) for p in pairs]
            do2 = [_attn_rows(do_ref, r0, TQ, lanes[p], head0) for p in pairs]
            trow = lax.broadcasted_iota(jnp.int32, (TQ, SB_KEYS), 0) + qi * TQ
            t2 = jnp.concatenate([trow, trow], axis=0)
            for p in pairs:
                pre[p, 0:TQ, :] = tot_ref[p, 0, pl.ds(r0, TQ), :]
                pre[p, TQ:2 * TQ, :] = tot_ref[p, 1, pl.ds(r0, TQ), :]
            cum[...] = jnp.zeros_like(cum)
            dq_acc[...] = jnp.zeros_like(dq_acc)

            def step(kj, masked):
                got = [_attn_scores(q2n[p], k_ref, lanes[p], kj, t2, from_here_ref, masked) for p in pairs]
                das = [_dot_nt(do2[p], v_ref[pl.ds(got[p][0], SB_KEYS), lanes[p]].astype(BF)) for p in pairs]
                for p in pairs:
                    c0, kb, zn, valid, lsb, sums = got[p]
                    later = pre[p] - _all_lanes(sums[:, 0:1], SB_BLOCK)
                    pre[p] = later
                    e = sums - zn + jnp.concatenate([later, later], axis=1)
                    sig = jnp.exp(lsb - zn)
                    if masked:
                        e = jnp.where(valid, e, -jnp.inf)
                        sig = jnp.where(valid, sig, 0.0)
                    a = jnp.exp(e)
                    w = das[p] * a
                    upto = _dot(w.astype(BF), up_to_ref[...])
                    c = cum[p]
                    dz = w - sig * (upto + jnp.concatenate([c, c], axis=1))
                    cum[p] = c + _all_lanes(upto[:, SB_KEYS - 1:SB_KEYS], SB_BLOCK)
                    dzb = dz.astype(BF)
                    dq_acc[p] += _dot(dzb, kb)
                    dk_acc[pl.ds(c0, SB_KEYS), lanes[p]] += _dot_tn(dzb, q2n[p])
                    dv_acc[pl.ds(c0, SB_KEYS), lanes[p]] += _dot_tn(a.astype(BF), do2[p])

            def below(kj, _):
                step(kj, False)
                return 0

            lax.fori_loop(0, qi, below, 0)
            step(qi, True)
            for p in pairs:
                dq_ref[pl.ds(r0, TQ), lanes[p]] = (jnp.where(head0, dq_acc[p, 0:TQ, :], dq_acc[p, TQ:2 * TQ, :]) * scale).astype(BF)
            return 0

        lax.fori_loop(0, S // TQ, qloop, 0)
        dk_ref[...] = (-dk_acc[...]).astype(BF)
        dv_ref[...] = dv_acc[...].astype(BF)
        lane0 = pl.program_id(0) * (n_pairs * SB_BLOCK)
        _put_columns([dq_ref, dk_ref, dv_ref], dproj_ref, [WIDTH * n + lane0 for n in range(3)], sems)

    col = _attn_specs(S, n_pairs)
    whole = lambda a: pl.BlockSpec(a.shape, lambda p: (0, 0))
    hbm = pl.BlockSpec(memory_space=pl.ANY)
    state = pltpu.VMEM((n_pairs, 2 * TQ, SB_BLOCK), F32)
    grads = pltpu.VMEM((S, n_pairs * SB_BLOCK), F32)
    return _hosted(
        body, job, (proj, proj, proj, d_o, tot, from_here, up_to, dproj), name="attn_bwd",
        grid=(WIDTH // (n_pairs * SB_BLOCK),),
        in_specs=[col(0), col(4), col(8), col(0), pl.BlockSpec((n_pairs, 2, S, 128), lambda p: (p, 0, 0, 0)), whole(from_here),
                  whole(up_to), hbm],
        out_specs=[hbm],
        out_shape=[jax.ShapeDtypeStruct(dproj.shape, dproj.dtype)],
        scratch_shapes=[state, state, state, grads, grads, pltpu.VMEM((3, S, n_pairs * SB_BLOCK), BF), pltpu.SemaphoreType.DMA((3,))],
        semantics=("arbitrary",), aliases={7: 0})


HG_LEVELS = tuple(HG_CHUNK >> n for n in range(1, HG_CHUNK.bit_length()))


def _hgrn_consts():
    C = HG_CHUNK
    t = np.arange(C)[:, None]
    s = np.arange(C)[None, :]
    rows = [(s <= t), (s > t)]
    masks = [(t == s)]
    for m in HG_LEVELS:
        two = 2 * m
        mid = (t // two) * two + m
        right = (t % two) >= m
        rows.append((right & (s >= mid) & (s <= t)) | ((~right) & (s > t) & (s <= mid - 1)))
        masks.append(((t // two) == (s // two)) & right & ((s % two) < m))
    tri = np.concatenate(rows, axis=0).astype(np.float32)
    twice = lambda a: jnp.asarray(np.concatenate([a, a], axis=1), BF)
    return (twice(tri), twice(tri.T), jnp.asarray(np.stack(masks).astype(np.float32), F32))


HG_SUM_BLOCKS = 2 + len(HG_LEVELS)


def _split_rows(g):
    hi = g.astype(BF)
    return jnp.concatenate([hi, (g - hi.astype(F32)).astype(BF)], axis=0)


def _hgrn_sum_blocks(e):
    C = HG_CHUNK
    blocks = tuple(e[n * C:(n + 1) * C] for n in range(HG_SUM_BLOCKS))
    return blocks + (jnp.broadcast_to(e[C - 1:C], (HG_DIM, e.shape[1])),)


@jax.custom_vjp
def _hgrn_sums(tri, tri_t, g):
    return _hgrn_sum_blocks(_split_dot(tri[:, :HG_CHUNK], g, 2))


def _hgrn_sums_fwd(tri, tri_t, g):
    return _hgrn_sum_blocks(_dot(tri, _split_rows(g))), (tri, tri_t)


def _hgrn_sums_bwd(res, ds):
    tri, tri_t = res
    C = HG_CHUNK
    last = lax.broadcasted_iota(jnp.int32, (C, 1), 0) == C - 1
    prefix = ds[0] + jnp.where(last, jnp.sum(ds[-1], axis=0, keepdims=True), 0.0)
    d = jnp.concatenate((prefix,) + tuple(ds[1:-1]), axis=0)
    return jnp.zeros_like(tri), jnp.zeros_like(tri_t), _dot(tri_t[:, :d.shape[0]], d.astype(BF))


_hgrn_sums.defvjp(_hgrn_sums_fwd, _hgrn_sums_bwd)


def _bf_dot(a, b):
    return _dot(a.astype(BF), b.astype(BF))


def _bf_dot_nt(a, b):
    return _dot_nt(a.astype(BF), b.astype(BF))


def _bf_dot_tn(a, b):
    return _dot_tn(a.astype(BF), b.astype(BF))


@jax.custom_vjp
def _mm(a, b):
    return _bf_dot(a, b)


_mm.defvjp(lambda a, b: (_bf_dot(a, b), (a, b)), lambda r, ct: (_bf_dot_nt(ct, r[1]), _bf_dot_tn(r[0], ct)))


@jax.custom_vjp
def _mm_nt(a, b):
    return _bf_dot_nt(a, b)


_mm_nt.defvjp(lambda a, b: (_bf_dot_nt(a, b), (a, b)), lambda r, ct: (_bf_dot(ct, r[1]), _bf_dot_tn(ct, r[0])))


@jax.custom_vjp
def _mm_tn(a, b):
    return _bf_dot_tn(a, b)


_mm_tn.defvjp(lambda a, b: (_bf_dot_tn(a, b), (a, b)), lambda r, ct: (_bf_dot_nt(r[1], ct), _bf_dot(r[0], ct)))


def _hgrn_gates(tri, tri_t, qraw, fpre, lb):
    q = _silu(qraw)
    f = lb + (1.0 - lb) * _sigmoid(fpre)
    return q, 1.0 - f, _hgrn_sums(tri, tri_t, jnp.log(f))


def _hgrn_mix(masks, q, k, e, v, st):
    prefix, suffix, whole = e[0], e[1], e[-1]
    scores = masks[0] * _mm_nt(q, k)
    for n in range(len(HG_LEVELS)):
        decay = jnp.exp(e[2 + n])
        scores = scores + masks[n + 1] * _mm_nt(q * decay, k * decay)
    o = _mm_nt(q * jnp.exp(prefix), st) + _mm(scores, v)
    st_new = st * jnp.exp(whole) + _mm_tn(v, k * jnp.exp(suffix))
    return o, st_new


def _hgrn_chunk(tri, tri_t, masks, qraw, fpre, v, st, lb):
    q, k, e = _hgrn_gates(tri, tri_t, qraw, fpre, lb)
    return _hgrn_mix(masks, q, k, e, v, st)


HG_HEADS_PER_STEP = 4
HG_LANES = HG_HEADS_PER_STEP * HG_DIM


def _hgrn_specs(S, consts):
    col = lambda base: pl.BlockSpec((S, HG_LANES), lambda p, base=base: (0, base // HG_HEADS_PER_STEP + p))
    whole = [pl.BlockSpec(a.shape, lambda p, n=a.ndim: (0,) * n) for a in consts]
    return col, whole


def _hgrn_fwd(proj, lbs, consts, job=None):
    S = proj.shape[0]
    nc = S // HG_CHUNK
    heads = range(HG_HEADS_PER_STEP)

    def body(q_ref, f_ref, i_ref, lb_ref, tri_ref, trit_ref, mask_ref, o_ref, st_ref):
        tri, tri_t = tri_ref[...], trit_ref[...]
        masks = [mask_ref[n] for n in range(len(HG_LEVELS) + 1)]

        def chunk(ci, sts):
            r0 = pl.multiple_of(ci * HG_CHUNK, HG_CHUNK)
            rows = pl.ds(r0, HG_CHUNK)
            new = []
            lane = [pl.ds(hd * HG_DIM, HG_DIM) for hd in heads]
            gates = [_hgrn_gates(tri, tri_t, q_ref[rows, lane[hd]], f_ref[rows, lane[hd]], lb_ref[hd]) for hd in heads]
            for hd in heads:
                st_ref[hd, ci] = sts[hd]
                o, st_new = _hgrn_mix(masks, *gates[hd], i_ref[rows, lane[hd]], sts[hd])
                o_ref[rows, lane[hd]] = o
                new.append(st_new)
            return tuple(new)

        lax.fori_loop(0, nc, chunk, tuple(jnp.zeros((HG_DIM, HG_DIM), F32) for _ in heads))

    col, whole = _hgrn_specs(S, consts)
    return _hosted(
        body, job, (proj, proj, proj, lbs, *consts), name="hgrn_fwd",
        grid=(WIDTH // HG_LANES,),
        in_specs=[col(16), col(20), col(24), pl.BlockSpec((HG_HEADS_PER_STEP, 1, 128), lambda p: (p, 0, 0))] + whole,
        out_specs=[col(0), pl.BlockSpec((HG_HEADS_PER_STEP, nc, HG_DIM, HG_DIM), lambda p: (p, 0, 0, 0))],
        out_shape=[jax.ShapeDtypeStruct((S, WIDTH), F32), jax.ShapeDtypeStruct((WIDTH // 128, nc, HG_DIM, HG_DIM), F32)],
        semantics=("parallel",))


def _hgrn_bwd(proj, lbs, states, d_o, consts, dproj, job=None):
    S = proj.shape[0]
    nc = S // HG_CHUNK

    def body(q_ref, f_ref, i_ref, lb_ref, st_ref, do_ref, tri_ref, trit_ref, mask_ref, _, dlb_ref, dproj_ref, stage, sems):
        dq_ref, df_ref, di_ref = stage.at[0], stage.at[1], stage.at[2]
        masks = [mask_ref[n] for n in range(len(HG_LEVELS) + 1)]
        fn = functools.partial(_hgrn_chunk, tri_ref[...], trit_ref[...], masks)
        heads = range(HG_HEADS_PER_STEP)

        def chunk(n, carry):
            ci = nc - 1 - n
            r0 = pl.multiple_of(ci * HG_CHUNK, HG_CHUNK)
            rows = pl.ds(r0, HG_CHUNK)
            new = []
            lane = [pl.ds(hd * HG_DIM, HG_DIM) for hd in heads]
            pulls = [jax.vjp(fn, q_ref[rows, lane[hd]], f_ref[rows, lane[hd]], i_ref[rows, lane[hd]], st_ref[hd, ci], lb_ref[hd])[1]
                     for hd in heads]
            for hd in heads:
                d_st, dlb = carry[hd]
                lanes = lane[hd]
                dq, df, di, d_prev, dl = pulls[hd]((do_ref[rows, lanes], d_st))
                dq_ref[rows, lanes] = dq.astype(BF)
                df_ref[rows, lanes] = df.astype(BF)
                di_ref[rows, lanes] = di.astype(BF)
                new.append((d_prev, dlb + dl))
            return tuple(new)

        zero = (jnp.zeros((HG_DIM, HG_DIM), F32), jnp.zeros((1, HG_DIM), F32))
        done = lax.fori_loop(0, nc, chunk, tuple(zero for _ in heads))
        for hd in heads:
            dlb_ref[hd] = done[hd][1]
        lane0 = pl.program_id(0) * HG_LANES
        _put_columns([dq_ref, df_ref, di_ref], dproj_ref, [WIDTH * n + lane0 for n in (4, 5, 6)], sems)

    col, whole = _hgrn_specs(S, consts)
    head = pl.BlockSpec((HG_HEADS_PER_STEP, 1, 128), lambda p: (p, 0, 0))
    hbm = pl.BlockSpec(memory_space=pl.ANY)
    n_in = 6 + len(consts)
    return _hosted(
        body, job, (proj, proj, proj, lbs, states, d_o, *consts, dproj), name="hgrn_bwd",
        grid=(WIDTH // HG_LANES,),
        in_specs=[col(16), col(20), col(24), head, pl.BlockSpec((HG_HEADS_PER_STEP, nc, HG_DIM, HG_DIM), lambda p: (p, 0, 0, 0)), col(0)]
        + whole + [hbm],
        out_specs=[head, hbm],
        out_shape=[jax.ShapeDtypeStruct((WIDTH // 128, 1, 128), F32), jax.ShapeDtypeStruct(dproj.shape, dproj.dtype)],
        scratch_shapes=[pltpu.VMEM((3, S, HG_LANES), BF), pltpu.SemaphoreType.DMA((3,))],
        semantics=("arbitrary",), aliases={n_in: 1})


def _shift_down(x, n):
    rows = lax.broadcasted_iota(jnp.int32, x.shape, 0)
    return jnp.where(rows >= n, pltpu.roll(x, n, 0), 0.0)


def _shift_up(x, n):
    S = x.shape[0]
    rows = lax.broadcasted_iota(jnp.int32, x.shape, 0)
    return jnp.where(rows < S - n, pltpu.roll(x, S - n, 0), 0.0)


def _branch_fwd(proj, o_a, o_b, norm_w, conv_w, layer):
    S = proj.shape[0]

    def body(oa_ref, za_ref, ob_ref, zb_ref, nw_ref, pre_ref, post_ref, u_ref, zc_ref, cw_ref, ya_ref, yb_ref, yc_ref):
        ya_ref[...] = (oa_ref[...] * _silu(za_ref[...])).astype(BF)
        ob = ob_ref[...]
        rn = lax.rsqrt(jnp.mean(ob * ob, axis=-1, keepdims=True) + RMS_EPS)
        yb_ref[...] = (ob * rn * nw_ref[layer:layer + 1, :] * _silu(zb_ref[...])).astype(BF)
        pu = pre_ref[...] * u_ref[...]
        conv = cw_ref[2:3, :] * pu + cw_ref[1:2, :] * _shift_down(pu, 1) + cw_ref[0:1, :] * _shift_down(pu, 2)
        yc_ref[...] = (post_ref[...] * conv * _silu(zc_ref[...])).astype(BF)

    col = lambda base: pl.BlockSpec((S, 128), lambda p, base=base: (0, base + p))
    out = jax.ShapeDtypeStruct((S, WIDTH), BF)
    return pl.pallas_call(
        body, name="branch_fwd",
        grid=(WIDTH // 128,),
        in_specs=[col(0), col(12), col(0), col(28), pl.BlockSpec(norm_w.shape, lambda p: (0, 0)),
                  col(32), col(36), col(40), col(44), pl.BlockSpec((None, None, 3, 128), lambda p: (p, layer, 0, 0))],
        out_specs=[col(0), col(0), col(0)],
        out_shape=[out, out, out],
        compiler_params=_params(dimension_semantics=("parallel",)),
    )(o_a, proj, o_b, proj, norm_w, proj, proj, proj, proj, conv_w)


def _put_columns(tiles, dproj_ref, firsts, sems):
    copies = [pltpu.make_async_copy(t, dproj_ref.at[:, pl.ds(pl.multiple_of(c, 128), t.shape[1])], sems.at[n])
              for n, (t, c) in enumerate(zip(tiles, firsts))]
    for cp in copies:
        cp.start()
    for cp in copies:
        cp.wait()


def _branch_bwd(proj, o_a, o_b, norm_w, conv_w, dy_a, dy_b, dy_c, dproj, layer):
    S = proj.shape[0]
    firsts = [WIDTH * n for n in (3, 7, 8, 9, 10, 11)]

    def dsilu(z):
        s = _sigmoid(z)
        return s * z, s * (1.0 + z * (1.0 - s))

    def body(oa_ref, za_ref, ob_ref, zb_ref, nw_ref, pre_ref, post_ref, u_ref, zc_ref, cw_ref, dya_ref, dyb_ref, dyc_ref, _,
             doa_ref, dob_ref, dnw_ref, dcw_ref, dproj_ref, stage, sems):
        dza_ref, dzb_ref, dpre_ref, dpost_ref, du_ref, dzc_ref = [stage.at[n] for n in range(6)]
        dya = dya_ref[...]
        sa, dsa = dsilu(za_ref[...])
        doa_ref[...] = dya * sa
        dza_ref[...] = (dya * oa_ref[...] * dsa).astype(BF)

        dyb = dyb_ref[...]
        ob = ob_ref[...]
        nw = nw_ref[layer:layer + 1, :]
        sb, dsb = dsilu(zb_ref[...])
        rn = lax.rsqrt(jnp.mean(ob * ob, axis=-1, keepdims=True) + RMS_EPS)
        on = ob * rn
        dzb_ref[...] = (dyb * on * nw * dsb).astype(BF)
        don_w = dyb * sb
        dnw_ref[...] = jnp.sum(don_w * on, axis=0, keepdims=True)
        don = don_w * nw
        dob_ref[...] = rn * (don - on * jnp.mean(don * on, axis=-1, keepdims=True))

        dyc = dyc_ref[...]
        pre, post, u = pre_ref[...], post_ref[...], u_ref[...]
        sc, dsc = dsilu(zc_ref[...])
        pu = pre * u
        pu1, pu2 = _shift_down(pu, 1), _shift_down(pu, 2)
        conv = cw_ref[2:3, :] * pu + cw_ref[1:2, :] * pu1 + cw_ref[0:1, :] * pu2
        dzc_ref[...] = (dyc * post * conv * dsc).astype(BF)
        dpost_ref[...] = (dyc * conv * sc).astype(BF)
        dconv = dyc * post * sc
        dcw_ref[0:1, :] = jnp.sum(dconv * pu2, axis=0, keepdims=True)
        dcw_ref[1:2, :] = jnp.sum(dconv * pu1, axis=0, keepdims=True)
        dcw_ref[2:3, :] = jnp.sum(dconv * pu, axis=0, keepdims=True)
        dpu = cw_ref[2:3, :] * dconv + cw_ref[1:2, :] * _shift_up(dconv, 1) + cw_ref[0:1, :] * _shift_up(dconv, 2)
        dpre_ref[...] = (dpu * u).astype(BF)
        du_ref[...] = (dpu * pre).astype(BF)
        lane0 = pl.program_id(0) * 128
        _put_columns([stage.at[n] for n in range(6)], dproj_ref, [c + lane0 for c in firsts], sems)

    col = lambda base: pl.BlockSpec((S, 128), lambda p, base=base: (0, base + p))
    f32 = jax.ShapeDtypeStruct((S, WIDTH), F32)
    hbm = pl.BlockSpec(memory_space=pl.ANY)
    return pl.pallas_call(
        body, name="branch_bwd",
        grid=(WIDTH // 128,),
        in_specs=[col(0), col(12), col(0), col(28), pl.BlockSpec(norm_w.shape, lambda p: (0, 0)),
                  col(32), col(36), col(40), col(44), pl.BlockSpec((None, None, 3, 128), lambda p: (p, layer, 0, 0)),
                  col(0), col(0), col(0), hbm],
        out_specs=[col(0), col(0), pl.BlockSpec((None, 1, 128), lambda p: (p, 0, 0)), pl.BlockSpec((None, 3, 128), lambda p: (p, 0, 0)), hbm],
        out_shape=[f32, f32, jax.ShapeDtypeStruct((WIDTH // 128, 1, 128), F32), jax.ShapeDtypeStruct((WIDTH // 128, 3, 128), F32),
                   jax.ShapeDtypeStruct(dproj.shape, dproj.dtype)],
        scratch_shapes=[pltpu.VMEM((6, S, 128), BF), pltpu.SemaphoreType.DMA((6,))],
        input_output_aliases={13: 4},
        compiler_params=_params(dimension_semantics=("arbitrary",)),
    )(o_a, proj, o_b, proj, norm_w, proj, proj, proj, proj, conv_w, dy_a, dy_b, dy_c, dproj)


def _branch_proj(y_refs, wb_ref):
    return [_dot(y_refs[i][...], wb_ref[i]) for i in range(3)]


def _merge_fwd(x, mod, proj, ys, wb, wo, ln_g, ln_b, layer, job=None, tm=512):
    S, D = x.shape
    tm = min(tm, S)

    def body(x_ref, mod_ref, ga_ref, gb_ref, gc_ref, ya_ref, yb_ref, yc_ref, wb_ref, wo_ref, g_ref, b_ref, xo_ref, mg_ref, y_ref):
        ps = _branch_proj((ya_ref, yb_ref, yc_ref), wb_ref)
        merged = _sigmoid(ga_ref[...]) * ps[0] + _sigmoid(gb_ref[...]) * ps[1] + _sigmoid(gc_ref[...]) * ps[2]
        mb = merged.astype(BF)
        mg_ref[...] = mb
        y = _dot(mb, wo_ref[...].reshape(D, D))
        y_ref[...] = y
        r = ALPHA * x_ref[...] + (1.0 + mod_ref[:, 2 * D:3 * D]) * y
        xn, _ = _standardize(r)
        xo_ref[...] = xn * g_ref[layer:layer + 1, :] + b_ref[layer:layer + 1, :]

    row = lambda w, c=0: pl.BlockSpec((tm, w), lambda i, c=c: (i, c))
    whole = lambda a: pl.BlockSpec(a.shape, lambda i, n=a.ndim: (0,) * n)
    return _hosted(
        body, job, (x, mod, proj, proj, proj, *ys, wb, wo, ln_g, ln_b), name="merge_fwd",
        grid=(S // tm,),
        in_specs=[row(D), whole(mod), row(D, 6), row(D, 7), row(D, 8), row(WIDTH), row(WIDTH), row(WIDTH), whole(wb),
                  whole(wo), whole(ln_g), whole(ln_b)],
        out_specs=[row(D), row(D), row(D)],
        out_shape=[jax.ShapeDtypeStruct((S, D), F32), jax.ShapeDtypeStruct((S, D), BF), jax.ShapeDtypeStruct((S, D), F32)],
        semantics=("parallel",))


def _merge_bwd(dxo, x, y, merged, mod, proj, ys, wb, wo, ln_g, layer, tm=256):
    S, D = x.shape
    tm = min(tm, S)
    steps = S // tm
    quarter = D // N_CHIPS

    def body(dxo_ref, x_ref, y_ref, mg_ref, mod_ref, ga_ref, gb_ref, gc_ref, ya_ref, yb_ref, yc_ref, wb_ref, wo_ref, g_ref,
             dxr_ref, dg_ref, dya_ref, dyb_ref, dyc_ref, dlg_ref, dlb_ref, dgt_ref, gbr_ref, gout_ref, acc_br, acc_out):
        @pl.when(pl.program_id(0) == 0)
        def _():
            dlg_ref[...] = jnp.zeros_like(dlg_ref)
            dlb_ref[...] = jnp.zeros_like(dlb_ref)
            dgt_ref[...] = jnp.zeros_like(dgt_ref)
            acc_br[...] = jnp.zeros_like(acc_br)
            acc_out[...] = jnp.zeros_like(acc_out)

        gate1 = 1.0 + mod_ref[:, 2 * D:3 * D]
        yv = y_ref[...]
        xn, rstd = _standardize(ALPHA * x_ref[...] + gate1 * yv)
        dxo = dxo_ref[...]
        dlg_ref[...] += jnp.sum(dxo * xn, axis=0, keepdims=True)
        dlb_ref[...] += jnp.sum(dxo, axis=0, keepdims=True)
        dr = _standardize_bwd(dxo * g_ref[layer:layer + 1, :], xn, rstd)
        dxr_ref[...] = ALPHA * dr
        dgt_ref[...] += jnp.sum(dr * yv, axis=0, keepdims=True)
        dyb = (gate1 * dr).astype(BF)
        acc_out[...] += _dot_tn(mg_ref[...], dyb)
        dmerged = _dot_nt(dyb, wo_ref[...].reshape(D, D))
        y_refs = (ya_ref, yb_ref, yc_ref)
        ps = _branch_proj(y_refs, wb_ref)
        for i, (gate_ref, out_ref) in enumerate(((ga_ref, dya_ref), (gb_ref, dyb_ref), (gc_ref, dyc_ref))):
            sg = _sigmoid(gate_ref[...])
            dg_ref[:, i * D:(i + 1) * D] = (dmerged * ps[i] * sg * (1.0 - sg)).astype(BF)
            dp = (dmerged * sg).astype(BF)
            acc_br[i] += _dot_tn(y_refs[i][...], dp)
            out_ref[...] = _dot_nt(dp, wb_ref[i])

        @pl.when(pl.program_id(0) == steps - 1)
        def _():
            for j in range(N_CHIPS):
                gout_ref[j] = acc_out[j * quarter:(j + 1) * quarter, :].astype(BF)
                for i in range(3):
                    gbr_ref[j, i] = acc_br[i, :, j * quarter:(j + 1) * quarter].astype(BF)

    row = lambda w, c=0: pl.BlockSpec((tm, w), lambda i, c=c: (i, c))
    whole = lambda a: pl.BlockSpec(a.shape, lambda i, n=len(a.shape): (0,) * n)
    vec = pl.BlockSpec((1, D), lambda i: (0, 0))
    sd = jax.ShapeDtypeStruct
    g_br, g_out = sd((N_CHIPS, 3, WIDTH, quarter), BF), sd((N_CHIPS, quarter, D), BF)
    return pl.pallas_call(
        body, name="merge_bwd",
        grid=(steps,),
        in_specs=[row(D), row(D), row(D), row(D), whole(mod), row(D, 6), row(D, 7), row(D, 8), row(WIDTH), row(WIDTH), row(WIDTH),
                  whole(wb), whole(wo), whole(ln_g)],
        out_specs=[row(D), row(3 * D, IN_COLS // (3 * D) - 1), row(WIDTH), row(WIDTH), row(WIDTH), vec, vec, vec, whole(g_br), whole(g_out)],
        out_shape=[sd((S, D), F32), sd((S, IN_COLS), BF), sd((S, WIDTH), F32), sd((S, WIDTH), F32), sd((S, WIDTH), F32),
                   sd((1, D), F32), sd((1, D), F32), sd((1, D), F32), g_br, g_out],
        scratch_shapes=[pltpu.VMEM((3, WIDTH, D), F32), pltpu.VMEM((D, D), F32)],
        compiler_params=_params(dimension_semantics=("arbitrary",)),
    )(dxo, x, y, merged, mod, proj, proj, proj, *ys, wb, wo, ln_g)


def _loss_head(x, target, tm=512):
    S, D = x.shape
    tm = min(tm, S)

    def body(x_ref, t_ref, dx_ref, loss_ref):
        @pl.when(pl.program_id(0) == 0)
        def _():
            loss_ref[...] = jnp.zeros_like(loss_ref)

        err = x_ref[...] - t_ref[...]
        dx_ref[...] = err * (1.0 / D)
        loss_ref[...] += 0.5 * jnp.sum(jnp.mean(err * err, axis=-1, keepdims=True))

    row = pl.BlockSpec((tm, D), lambda i: (i, 0))
    return pl.pallas_call(
        body, name="loss_head",
        grid=(S // tm,),
        in_specs=[row, row],
        out_specs=[row, pl.BlockSpec((8, 128), lambda i: (0, 0))],
        out_shape=[jax.ShapeDtypeStruct((S, D), F32), jax.ShapeDtypeStruct((8, 128), F32)],
        compiler_params=_params(dimension_semantics=("arbitrary",)),
    )(x, target)


def _proj_bwd(dproj, wgs, x, mod, dx_res, job=None, tm=512, tk=768):
    S, D = x.shape
    tm = min(tm, S)
    n = len(wgs)
    w_args = list(wgs)
    if n == 1:
        per = wgs[0].shape[-1] // tk
        w_specs = [pl.BlockSpec((None, D, tk), lambda k, i: (k // per, 0, k % per))]
    else:
        assert wgs[0].shape[-1] == tk
        w_specs = [pl.BlockSpec((None, D, tk), lambda k, i, c=c: (jnp.minimum((k + n - 1 - c) // n, N_CHIPS - 1), 0, 0))
                   for c in range(n)]
    nk = IN_COLS // tk

    def body(dp_ref, *rest):
        w_refs, (x_ref, mod_ref, dxr_ref, dx_ref, dsh_ref, dsc_ref, acc) = rest[:n], rest[n:]
        k, i = pl.program_id(0), pl.program_id(1)
        mine = pl.ds(pl.multiple_of(i * tm, tm), tm)

        @pl.when((i == 0) & (k == 0))
        def _():
            dsh_ref[...] = jnp.zeros_like(dsh_ref)
            dsc_ref[...] = jnp.zeros_like(dsc_ref)

        @pl.when(k == 0)
        def _():
            acc[mine, :] = jnp.zeros((tm, D), F32)

        for c in range(n):
            @pl.when(k % n == c)
            def _(c=c):
                acc[mine, :] += _dot_nt(dp_ref[...], w_refs[c][...])

        @pl.when(k == nk - 1)
        def _():
            dh = acc[mine, :]
            xs, rstd = _standardize(x_ref[...])
            dsh_ref[...] += jnp.sum(dh, axis=0, keepdims=True)
            dsc_ref[...] += jnp.sum(dh * xs, axis=0, keepdims=True)
            dx_ref[...] = _standardize_bwd(dh * (1.0 + mod_ref[:, D:2 * D]), xs, rstd) + dxr_ref[...]

    row = pl.BlockSpec((tm, D), lambda k, i: (jnp.where(k == nk - 1, i, 0), 0))
    vec = pl.BlockSpec((1, D), lambda k, i: (0, 0))
    return _hosted(
        body, job, (dproj, *w_args, x, mod, dx_res), name="proj_bwd",
        grid=(nk, S // tm),
        in_specs=[pl.BlockSpec((tm, tk), lambda k, i: (i, k))] + w_specs + [row, pl.BlockSpec((1, 3 * D), lambda k, i: (0, 0)), row],
        out_specs=[row, vec, vec],
        out_shape=[jax.ShapeDtypeStruct((S, D), F32), jax.ShapeDtypeStruct((1, D), F32), jax.ShapeDtypeStruct((1, D), F32)],
        scratch_shapes=[pltpu.VMEM((S, D), F32)],
        semantics=("arbitrary", "arbitrary"))


def _grad_w_in(h, dproj):
    S, D = h.shape
    shard = IN_COLS // N_CHIPS

    def body(h_ref, d_ref, o_ref):
        o_ref[...] = _dot_tn(h_ref[...], d_ref[...]).astype(BF)

    return pl.pallas_call(
        body, name="grad_w_in",
        grid=(N_CHIPS,),
        in_specs=[pl.BlockSpec((S, D), lambda n: (0, 0)), pl.BlockSpec((S, shard), lambda n: (0, n))],
        out_specs=pl.BlockSpec((None, D, shard), lambda n: (n, 0, 0)),
        out_shape=jax.ShapeDtypeStruct((N_CHIPS, D, shard), BF),
        compiler_params=_params(dimension_semantics=("parallel",)),
    )(h, dproj)


def _all_gather8(x, name):
    R, N = x.shape

    def body(x_ref, out_ref, send_sems, recv_sems):
        mx, my, mc = lax.axis_index("x"), lax.axis_index("y"), lax.axis_index("c")
        me = 4 * mx + 2 * my + mc
        out_ref[me] = x_ref[...]
        copies = []
        for k in range(1, N_DEV):
            peer = (_flip(mx, k & 4), _flip(my, k & 2), _flip(mc, k & 1))
            cp = pltpu.make_async_remote_copy(src_ref=x_ref, dst_ref=out_ref.at[me], send_sem=send_sems.at[k - 1],
                                              recv_sem=recv_sems.at[k - 1], device_id=peer, device_id_type=MESH)
            cp.start()
            copies.append(cp)
        for cp in copies:
            cp.wait()

    return pl.pallas_call(
        body, name=name,
        in_specs=[pl.BlockSpec(memory_space=pltpu.VMEM)],
        out_specs=pl.BlockSpec(memory_space=pltpu.VMEM),
        out_shape=jax.ShapeDtypeStruct((N_DEV, R, N), F32),
        scratch_shapes=[pltpu.SemaphoreType.DMA((N_DEV - 1,)), pltpu.SemaphoreType.DMA((N_DEV - 1,))],
        compiler_params=_params(),
    )(x)


def _rows2d(a):
    return a.reshape(-1, a.shape[-1])


def _tile_rows(rows, cols, n_arrays):
    budget = (24 << 20) // (n_arrays * 2 * 4 * cols)
    if rows <= budget:
        return rows
    tm = 8
    for cand in range(8, budget + 1, 8):
        if rows % cand == 0:
            tm = cand
    return tm


SUM_ROWS = 256


def _sum_cores(g, sent, where):
    chips, lead, _, r, cols = g.shape
    tr = min(r, SUM_ROWS)

    def body(where_ref, g_ref, s_ref, out_ref):
        out_ref[...] = (g_ref[...].astype(F32) + s_ref[...].astype(F32)).astype(BF)

    spec = pl.BlockSpec((None, tr, cols), lambda i, j, where_ref: (i, j, 0))
    out = pl.pallas_call(
        body, name="sum_cores",
        grid_spec=pltpu.PrefetchScalarGridSpec(
            num_scalar_prefetch=1, grid=(chips * lead, r // tr),
            in_specs=[pl.BlockSpec((None, None, tr, cols), lambda i, j, where_ref: (i, where_ref[1], j, 0)), spec],
            out_specs=spec),
        out_shape=jax.ShapeDtypeStruct((chips * lead, r, cols), BF),
        compiler_params=_params(dimension_semantics=("parallel", "parallel")),
    )(where, g.reshape(chips * lead, 2, r, cols), sent.reshape(chips * lead, r, cols))
    return out.reshape(chips, lead, r, cols)


def _sum_chips(sums, got, where):
    _, lead, r, cols = sums.shape
    tr = min(r, SUM_ROWS)

    def body(where_ref, s_ref, g_ref, out_ref):
        out_ref[...] = ((s_ref[...].astype(F32) + g_ref[0].astype(F32)) + g_ref[1].astype(F32)) + g_ref[2].astype(F32)

    return pl.pallas_call(
        body, name="sum_chips",
        grid_spec=pltpu.PrefetchScalarGridSpec(
            num_scalar_prefetch=1, grid=(lead, r // tr),
            in_specs=[pl.BlockSpec((None, None, tr, cols), lambda i, j, where_ref: (where_ref[0], i, j, 0)),
                      pl.BlockSpec((N_CHIPS - 1, None, tr, cols), lambda i, j, where_ref: (0, i, j, 0))],
            out_specs=pl.BlockSpec((None, None, tr, cols), lambda i, j, where_ref: (i, where_ref[1], j, 0))),
        out_shape=jax.ShapeDtypeStruct((lead, 2, r, cols), F32),
        compiler_params=_params(dimension_semantics=("parallel", "parallel")),
    )(where, sums, got)


def _adamw(w, m, v, groups, name):
    shape = w.shape
    w2, m2, v2 = _rows2d(w), _rows2d(m), _rows2d(v)
    rows, cols = w2.shape
    ng = len(groups)
    n = len(groups[0])
    slab = rows // ng
    gs = [_rows2d(g) for grp in groups for g in grp]
    tm = _tile_rows(slab, cols, 7 + n)
    tiles = slab // tm
    c1 = 1.0 / (1.0 - ADAM_B1 ** ADAM_STEP)
    c2 = 1.0 / (1.0 - ADAM_B2 ** ADAM_STEP)

    def body(*refs):
        w_ref, m_ref, v_ref = refs[:3]
        g_refs = refs[3:3 + ng * n]
        go_ref, d_ref, mo_ref, vo_ref = refs[3 + ng * n:]
        which = pl.program_id(0)
        for s in range(ng):
            @pl.when(which == s)
            def _(s=s):
                g = g_refs[s * n][...]
                for r in g_refs[s * n + 1:(s + 1) * n]:
                    g = g + r[...]
                mn = ADAM_B1 * m_ref[...] + (1.0 - ADAM_B1) * g
                vn = ADAM_B2 * v_ref[...] + (1.0 - ADAM_B2) * (g * g)
                go_ref[...] = g
                mo_ref[...] = mn
                vo_ref[...] = vn
                d_ref[...] = -ADAM_LR * ((mn * c1) / (jnp.sqrt(vn * c2) + ADAM_EPS) + ADAM_WD * w_ref[...])

    spec = pl.BlockSpec((tm, cols), lambda s, i: (s * tiles + i, 0))
    g_specs = [pl.BlockSpec((tm, cols), lambda s, i, k=k: (jnp.where(s == k, i, jnp.where(s < k, 0, tiles - 1)), 0))
               for k in range(ng) for _ in range(n)]
    outs = pl.pallas_call(
        body, name=name,
        grid=(ng, tiles),
        in_specs=[spec] * 3 + g_specs,
        out_specs=[spec] * 4,
        out_shape=[jax.ShapeDtypeStruct((rows, cols), F32)] * 4,
        compiler_params=_params(dimension_semantics=("arbitrary", "arbitrary")),
    )(w2, m2, v2, *gs)
    return [o.reshape(shape) for o in outs]


def _lower_bounds(r0, r1):
    top = jnp.maximum(r0, r1)
    e0, e1 = jnp.exp(r0 - top), jnp.exp(r1 - top)
    p0, p1 = e0 / (e0 + e1), e1 / (e0 + e1)
    return p0 - p0, (p0 + p1) - p0


def _lbs_fwd(lb_raw):
    def body(lb_ref, out_ref):
        l0, l1 = _lower_bounds(lb_ref[0:1, :], lb_ref[1:2, :])
        out_ref[0:1, :] = l0
        out_ref[1:2, :] = l1

    return pl.pallas_call(body, name="lower_bounds", out_shape=jax.ShapeDtypeStruct(lb_raw.shape, F32), compiler_params=_params())(lb_raw)


def _mod_rows(c_all, w_mod, tn=768):
    _, D, cols = w_mod.shape

    def body(c_ref, w_ref, out_ref):
        out_ref[...] = _dot(c_ref[...].astype(BF), w_ref[...].astype(BF))

    return pl.pallas_call(
        body, name="mod_rows",
        grid=(DEPTH,),
        in_specs=[pl.BlockSpec((N_DEV, D), lambda l: (0, 0)), pl.BlockSpec((None, D, cols), lambda l: (l, 0, 0))],
        out_specs=pl.BlockSpec((N_DEV, cols), lambda l: (0, l)),
        out_shape=jax.ShapeDtypeStruct((N_DEV, DEPTH * cols), F32),
        compiler_params=_params(dimension_semantics=("parallel",)),
    )(c_all, w_mod)


def _grad_w_mod(c_all, dmod_cols):
    D = c_all.shape[1]
    cols = dmod_cols.shape[-1]

    def body(c_ref, d_ref, out_ref):
        out_ref[...] = _dot_tn(c_ref[...].astype(BF), d_ref[...].astype(BF))

    return pl.pallas_call(
        body, name="grad_w_mod",
        grid=(DEPTH,),
        in_specs=[pl.BlockSpec((N_DEV, D), lambda l: (0, 0)), pl.BlockSpec((None, N_DEV, cols), lambda l: (l, 0, 0))],
        out_specs=pl.BlockSpec((None, D, cols), lambda l: (l, 0, 0)),
        out_shape=jax.ShapeDtypeStruct((DEPTH, D, cols), F32),
        compiler_params=_params(dimension_semantics=("parallel",)),
    )(c_all, dmod_cols)


def _sum_devices(parts):
    _, R, N = parts.shape

    def body(p_ref, out_ref):
        acc = p_ref[0]
        for d in range(1, N_DEV):
            acc = acc + p_ref[d]
        out_ref[...] = acc

    return pl.pallas_call(body, name="sum_devices", out_shape=jax.ShapeDtypeStruct((R, N), F32), compiler_params=_params())(parts)


def _lbs_bwd(lb_raw, dl):
    def body(lb_ref, dl_ref, out_ref):
        _, pull = jax.vjp(_lower_bounds, lb_ref[0:1, :], lb_ref[1:2, :])
        d0, d1 = pull((dl_ref[0:1, :], dl_ref[1:2, :]))
        out_ref[0:1, :] = d0
        out_ref[1:2, :] = d1

    return pl.pallas_call(body, name="lower_bounds_bwd", out_shape=jax.ShapeDtypeStruct(lb_raw.shape, F32), compiler_params=_params())(lb_raw, dl)


def kernel(x, c, w_mod, b_mod, w_in, conv_w, hgrn_norm_w, lower_bounds, w_branch, w_out, ln_g, ln_b, loss_target, m_w_mod, m_b_mod, m_w_in, m_conv_w, m_hgrn_norm_w, m_lower_bounds, m_w_branch, m_w_out, m_ln_g, m_ln_b, v_w_mod, v_b_mod, v_w_in, v_conv_w, v_hgrn_norm_w, v_lower_bounds, v_w_branch, v_w_out, v_ln_g, v_ln_b):
    D = D_MODEL
    x0 = x[0]
    target = loss_target[0]
    S = x0.shape[0]
    mx, my, mc = lax.axis_index("x"), lax.axis_index("y"), lax.axis_index("c")
    chip = 2 * mx + my
    me = 2 * chip + mc
    mod_cols = 3 * D // N_CHIPS

    plan = _Plan(w_in.astype(BF), w_branch.astype(BF), w_out.astype(BF), chip, mc)
    n_conv = DEPTH * 3 * (WIDTH // N_CHIPS)
    first = jnp.concatenate([c, conv_w.reshape(1, n_conv), jnp.zeros((1, 2 * D - D - n_conv), F32)], axis=1)
    first = plan.first(first.reshape(8, 2 * D // 8)).reshape(N_DEV, 2 * D)

    c_all = first[:, :D]
    conv_all = first[:, D:D + n_conv].reshape(N_DEV, DEPTH, 3, WIDTH // N_CHIPS)[0::2]
    mod_part = _all_gather8(_mod_rows(c_all, w_mod), "gather_mod")[0::2]
    mod_part = lax.dynamic_index_in_dim(mod_part, me, axis=1, keepdims=False).reshape(N_CHIPS, DEPTH, mod_cols)
    mods = [(mod_part[:, l].reshape(1, 3 * D) + b_mod[l][None, :]) for l in range(DEPTH)]
    lbs = _lbs_fwd(lower_bounds).reshape(DEPTH, WIDTH // 128, 1, 128)
    loss_blk, dx, small = _local_step(x0, target, mods, lbs, conv_all, hgrn_norm_w, ln_g, ln_b, plan)

    n_mod, n_nw, n_lb, n_ln, n_cw = DEPTH * 3 * D, DEPTH * 128, DEPTH * WIDTH, DEPTH * D, DEPTH * 3 * WIDTH
    row = jnp.concatenate(
        [jnp.concatenate([small[l][0], small[l][1], small[l][2]], axis=1) for l in range(DEPTH)]
        + [jnp.sum(small[l][3], axis=0) for l in range(DEPTH)]
        + [small[l][4].reshape(1, WIDTH) for l in range(DEPTH)]
        + [small[l][5] for l in range(DEPTH)] + [small[l][6] for l in range(DEPTH)]
        + [jnp.transpose(small[l][7], (1, 0, 2)).reshape(1, 3 * WIDTH) for l in range(DEPTH)]
        + [loss_blk[0:1, :]], axis=1)
    n_row = row.shape[1]
    fold = -(-n_row // (8 * 128)) * 128
    rows = jnp.concatenate([row, jnp.zeros((1, 8 * fold - n_row), F32)], axis=1).reshape(8, fold)

    whole, gathered = plan.finish(rows)
    grads = {kind: [[whole[(kind, l)]] for l in range(DEPTH)] for kind in ("in", "br", "out")}

    off_nw = n_mod
    off_lb = off_nw + n_nw
    off_lng = off_lb + n_lb
    off_lnb = off_lng + n_ln
    off_cw = off_lnb + n_ln
    off_loss = off_cw + n_cw
    total = _sum_devices(gathered).reshape(1, 8 * fold)
    gathered = gathered.reshape(N_DEV, 1, 8 * fold)
    d_lower = _lbs_bwd(lower_bounds, total[0, off_lb:off_lng].reshape(DEPTH, WIDTH))
    loss = total[0, off_loss]
    d_b_mod = total[0, :n_mod].reshape(DEPTH, 3 * D)
    d_norm_w = total[0, off_nw:off_lb].reshape(DEPTH, 128)
    d_ln_g = total[0, off_lng:off_lnb].reshape(DEPTH, D)
    d_ln_b = total[0, off_lnb:off_cw].reshape(DEPTH, D)
    d_conv = total[0, off_cw:off_loss].reshape(DEPTH, 3, N_CHIPS, WIDTH // N_CHIPS)
    d_conv = lax.dynamic_index_in_dim(d_conv, chip, axis=2, keepdims=False)
    dmod_all = gathered[:, 0, :n_mod].reshape(N_DEV, DEPTH, N_CHIPS, mod_cols)
    dmod_cols = jnp.transpose(lax.dynamic_index_in_dim(dmod_all, chip, axis=2, keepdims=False), (1, 0, 2))
    d_w_mod = _grad_w_mod(c_all, dmod_cols)

    res = {}
    res["w_mod"] = _adamw(w_mod, m_w_mod, v_w_mod, [[d_w_mod]], "adamw_w_mod")
    res["b_mod"] = _adamw(b_mod, m_b_mod, v_b_mod, [[d_b_mod]], "adamw_b_mod")
    res["w_in"] = _adamw(w_in, m_w_in, v_w_in, grads["in"], "adamw_w_in")
    res["conv_w"] = _adamw(conv_w, m_conv_w, v_conv_w, [[d_conv]], "adamw_conv_w")
    res["hgrn_norm_w"] = _adamw(hgrn_norm_w, m_hgrn_norm_w, v_hgrn_norm_w, [[d_norm_w]], "adamw_norm_w")
    res["lower_bounds"] = _adamw(lower_bounds, m_lower_bounds, v_lower_bounds, [[d_lower]], "adamw_lower_bounds")
    res["w_branch"] = _adamw(w_branch, m_w_branch, v_w_branch, grads["br"], "adamw_w_branch")
    res["w_out"] = _adamw(w_out, m_w_out, v_w_out, grads["out"], "adamw_w_out")
    res["ln_g"] = _adamw(ln_g, m_ln_g, v_ln_g, [[d_ln_g]], "adamw_ln_g")
    res["ln_b"] = _adamw(ln_b, m_ln_b, v_ln_b, [[d_ln_b]], "adamw_ln_b")
    names = ["w_mod", "b_mod", "w_in", "conv_w", "hgrn_norm_w", "lower_bounds", "w_branch", "w_out", "ln_g", "ln_b"]
    return (loss, dx[None], *[res[n][0] for n in names], *[res[n][1] for n in names],
            *[res[n][2] for n in names], *[res[n][3] for n in names])


class _Plan:
    FIRST_CHUNKS = 3
    WINDOWS = ((0, 1152), (1152, 640), (1792, 512))

    def __init__(self, w_in, w_br, w_out, chip, core):
        self.local = {"in": w_in, "br": w_br, "out": w_out}
        self.chip, self.where = chip, jnp.stack([chip, core]).astype(jnp.int32)
        self.gathered, self.partial, self.grads, self.chip_sums, self.scattered, self.pending = {}, {}, {}, {}, {}, {}

    def chunks(self, l):
        return self.FIRST_CHUNKS if l == 0 else 1

    def w_in(self, l):
        return [self.gathered[("in", l, c)] for c in range(self.chunks(l))]

    def _shard(self, key):
        mine = self.local[key[0]][key[1]]
        if key[0] == "in":
            cols = mine.shape[-1] // self.chunks(key[1])
            mine = mine[:, key[2] * cols:(key[2] + 1) * cols]
        return mine

    def _slab(self, key):
        mine = _halves(self._shard(key))
        return lax.dynamic_update_slice(lax.empty((N_CHIPS,) + mine.shape, mine.dtype), mine[None], (self.chip, 0, 0, 0, 0))

    def _gather(self, keys):
        return ("gather", keys), _gather_job([self._slab(key) for key in keys])

    def _gather_window(self, key, n):
        slab = self._slab(key) if n == 0 else self.partial[key]
        return ("gather" if n == len(self.WINDOWS) - 1 else "gather_part", [key]), _gather_job([slab], self.WINDOWS[n])

    def _to_sibling(self, keys):
        return ("to_sibling", keys), _to_sibling_job([_halves(self.grads[key], 1) for key in keys])

    def _scatter(self, keys):
        return ("scatter", keys), _scatter_job([self.chip_sums[key] for key in keys])

    def job(self, stage, l, c=0):
        parts = []
        if stage == "proj_fwd":
            parts = [self._gather([("in", l, c + 1)] if c + 1 < self.chunks(l) else [("br", l), ("out", l)])]
        elif stage == "attn_fwd" and l + 1 < DEPTH:
            parts = [self._gather_window(("in", l + 1, 0), 0)]
        elif stage == "hgrn_fwd" and l + 1 < DEPTH:
            parts = [self._gather_window(("in", l + 1, 0), 1)]
        elif stage == "merge_fwd" and l + 1 < DEPTH:
            parts = [self._gather_window(("in", l + 1, 0), 2)]
        elif stage == "attn_bwd":
            parts = [self._to_sibling([("out", l), ("br", l)])] + ([self._scatter([("in", l + 1)])] if l + 1 < DEPTH else [])
        elif stage == "hgrn_bwd":
            parts = [self._scatter([("out", l), ("br", l)])]
        elif stage == "proj_bwd":
            parts = [self._to_sibling([("in", l)])] if l else [self._scatter([("in", 0)])]
        self.pending[(stage, l, c)] = [(tag, len(job.outs)) for tag, job in parts]
        return _join_jobs([job for _, job in parts])

    def done(self, stage, l, outs, c=0):
        if outs is None:
            return
        at = 0
        for (what, keys), n_outs in self.pending[(stage, l, c)]:
            mine, at = outs[at:at + n_outs], at + n_outs
            for n, key in enumerate(keys):
                if what == "gather":
                    self.gathered[key] = mine[n].reshape((N_CHIPS,) + self._shard(key).shape)
                elif what == "gather_part":
                    self.partial[key] = mine[n]
                elif what == "to_sibling":
                    self.chip_sums[key] = _sum_cores(_halves(self.grads[key], 1), mine[n], self.where)
                else:
                    self.scattered[key] = mine[n]

    def first(self, rows):
        tag, job = self._gather([("in", 0, 0)])
        self.pending[("first", 0, 0)] = [(tag, len(job.outs))]
        outs = _run_job(_join_jobs([job, _gather8_job(rows)]), "gather_first")
        self.done("first", 0, outs[:-1])
        return outs[-1]

    def took(self, key, grad):
        self.grads[key] = grad
        if key == ("in", 0):
            tag, job = self._to_sibling([key])
            self.pending[("took", 0, 0)] = [(tag, len(job.outs))]
            self.done("took", 0, _run_job(job, "to_sibling_last"))

    def finish(self, rows):
        keys = [(kind, l) for kind in ("in", "br", "out") for l in range(DEPTH)]
        halves = [_sum_chips(self.chip_sums[key], self.scattered[key], self.where) for key in keys]
        outs = _run_job(_join_jobs([_place_job(halves), _gather8_job(rows)]), "place_halves")
        return {key: w.reshape(self.grads[key].shape[1:]) for key, w in zip(keys, outs[:-1])}, outs[-1]


def _local_step(x0, target, mods, lbs, conv_all, hgrn_norm_w, ln_g, ln_b, plan):
    D = D_MODEL
    after, before = _attn_consts()
    hg_consts = _hgrn_consts()

    saved = []
    xl = x0
    for l in range(DEPTH):
        n = plan.chunks(l)
        (proj, h), got = _proj_fwd(xl, mods[l], plan.gathered[("in", l, 0)], plan.job("proj_fwd", l, 0), (0, n))
        plan.done("proj_fwd", l, got, 0)
        for c in range(1, n):
            (proj,), got = _proj_cols(h, plan.gathered[("in", l, c)], (c, n), proj, plan.job("proj_fwd", l, c))
            plan.done("proj_fwd", l, got, c)
        (o_a, tot), got = _attn_fwd(proj, after, plan.job("attn_fwd", l))
        plan.done("attn_fwd", l, got)
        (o_b, states), got = _hgrn_fwd(proj, lbs[l], hg_consts, plan.job("hgrn_fwd", l))
        plan.done("hgrn_fwd", l, got)
        ys = _branch_fwd(proj, o_a, o_b, hgrn_norm_w, conv_all, l)
        wb = jnp.concatenate(list(plan.gathered[("br", l)]), axis=-1)
        (x_next, merged, y), got = _merge_fwd(xl, mods[l], proj, ys, wb, plan.gathered[("out", l)], ln_g, ln_b, l, plan.job("merge_fwd", l))
        plan.done("merge_fwd", l, got)
        saved.append((xl, proj, h, o_a, tot, o_b, states, ys, merged, y, wb))
        xl = x_next
    dx, loss_blk = _loss_head(xl, target)

    small = [None] * DEPTH
    for l in reversed(range(DEPTH)):
        xin, proj, h, o_a, tot, o_b, states, ys, merged, y, wb = saved[l]
        dx_res, dproj, dy_a, dy_b, dy_c, dln_g, dln_b, dgate, g_br, g_out = _merge_bwd(
            dx, xin, y, merged, mods[l], proj, ys, wb, plan.gathered[("out", l)], ln_g, l)
        plan.took(("out", l), g_out)
        plan.took(("br", l), g_br)
        d_oa, d_ob, dnorm_w, dconv_w, dproj = _branch_bwd(proj, o_a, o_b, hgrn_norm_w, conv_all, dy_a, dy_b, dy_c, dproj, l)
        (dproj,), got = _attn_bwd(proj, d_oa, tot, after, before, dproj, plan.job("attn_bwd", l))
        plan.done("attn_bwd", l, got)
        (dlb, dproj), got = _hgrn_bwd(proj, lbs[l], states, d_ob, hg_consts, dproj, plan.job("hgrn_bwd", l))
        plan.done("hgrn_bwd", l, got)
        plan.took(("in", l), _grad_w_in(h, dproj))
        (dx, dshift, dscale), got = _proj_bwd(dproj, plan.w_in(l), xin, mods[l], dx_res, plan.job("proj_bwd", l))
        plan.done("proj_bwd", l, got)
        small[l] = (dshift, dscale, dgate, dnorm_w, dlb, dln_g, dln_b, dconv_w)
    return loss_blk, dx, small
```

```python
import functools
import math

import numpy as np
import jax
import jax.numpy as jnp
from jax import lax
from jax.experimental import pallas as pl
from jax.experimental.pallas import tpu as pltpu

F32 = jnp.float32
BF = jnp.bfloat16
MESH = pl.DeviceIdType.MESH

DEPTH = 2
D_MODEL = 1024
WIDTH = 512
IN_COLS = 12 * WIDTH + 3 * D_MODEL
N_CHIPS = 4
N_DEV = 8
SB_BLOCK = 128
SB_HEAD_DIM = 64
HG_CHUNK = 128
HG_DIM = 128
LN_EPS = 1e-5
RMS_EPS = 1e-6
ALPHA = (2.0 * DEPTH) ** 0.25
ADAM_LR, ADAM_B1, ADAM_B2, ADAM_EPS, ADAM_WD, ADAM_STEP = 0.001, 0.9, 0.999, 1e-08, 0.01, 10
VMEM_LIMIT = 56 << 20


def _params(**kw):
    return pltpu.CompilerParams(vmem_limit_bytes=VMEM_LIMIT, **kw)


def _dot(a, b):
    return jnp.dot(a, b, preferred_element_type=F32)


def _dot_nt(a, b):
    return lax.dot_general(a, b, (((1,), (1,)), ((), ())), preferred_element_type=F32)


def _dot_tn(a, b):
    return lax.dot_general(a, b, (((0,), (0,)), ((), ())), preferred_element_type=F32)


def _sigmoid(x):
    return 1.0 / (1.0 + jnp.exp(-x))


def _silu(x):
    return x * _sigmoid(x)


def _standardize(x):
    mu = jnp.mean(x, axis=-1, keepdims=True)
    xc = x - mu
    var = jnp.mean(xc * xc, axis=-1, keepdims=True)
    rstd = lax.rsqrt(var + LN_EPS)
    return xc * rstd, rstd


def _standardize_bwd(dxs, xs, rstd):
    return rstd * (dxs - jnp.mean(dxs, axis=-1, keepdims=True) - xs * jnp.mean(dxs * xs, axis=-1, keepdims=True))


class _Job:
    def __init__(self, ins, outs, sems, make, alias=None):
        self.ins, self.outs, self.sems, self.make = list(ins), list(outs), list(sems), make
        self.alias = dict(alias or {})


def _join_jobs(jobs):
    jobs = [j for j in jobs if j is not None]
    if len(jobs) <= 1:
        return jobs[0] if jobs else None

    def make(ins, outs, sems):
        phases, i, o, s = [], 0, 0, 0
        for j in jobs:
            got = j.make(ins[i:i + len(j.ins)], outs[o:o + len(j.outs)], sems[s:s + len(j.sems)])
            i, o, s = i + len(j.ins), o + len(j.outs), s + len(j.sems)
            for n, phase in enumerate(got):
                if n == len(phases):
                    phases.append([])
                phases[n] += phase
        return phases

    alias, i, o = {}, 0, 0
    for j in jobs:
        alias.update({i + a: o + b for a, b in j.alias.items()})
        i, o = i + len(j.ins), o + len(j.outs)
    return _Job(sum([j.ins for j in jobs], []), sum([j.outs for j in jobs], []), sum([j.sems for j in jobs], []), make, alias)


def _flip(v, bit):
    return 1 - v if bit else v


def _halves(a, front=0):
    shape = a.shape
    lead = math.prod(shape[front:-2])
    return a.reshape(shape[:front] + (lead, 2, shape[-2] // 2, shape[-1]))


def _dma_sems(*shapes):
    return [pltpu.SemaphoreType.DMA(s) for s in shapes]


def _same(arrays):
    return [jax.ShapeDtypeStruct(a.shape, a.dtype) for a in arrays]


def _gather_job(slabs, window=None):
    n = len(slabs)
    cols = slice(None) if window is None else pl.ds(*window)

    def make(ins, outs, sems):
        send1, recv1, send2, recv2 = sems
        mx, my, mc = lax.axis_index("x"), lax.axis_index("y"), lax.axis_index("c")
        fetch, pass_on = [], []
        for a in range(n):
            ours = outs[a].at[2 * mx + my, :, mc, :, cols]
            for k in range(1, N_CHIPS):
                px, py = _flip(mx, k & 2), _flip(my, k & 1)
                fetch.append(pltpu.make_async_remote_copy(
                    src_ref=ours, dst_ref=ours, send_sem=send1.at[a, k - 1], recv_sem=recv1.at[a, k - 1],
                    device_id=(px, py, mc), device_id_type=MESH))
                theirs = outs[a].at[2 * px + py, :, mc, :, cols]
                pass_on.append(pltpu.make_async_remote_copy(
                    src_ref=theirs, dst_ref=theirs, send_sem=send2.at[a, k - 1], recv_sem=recv2.at[a, k - 1],
                    device_id=(mx, my, 1 - mc), device_id_type=MESH))
        return [fetch, pass_on]

    pairs = (n, N_CHIPS - 1)
    return _Job(slabs, _same(slabs), _dma_sems(pairs, pairs, pairs, pairs), make, {a: a for a in range(n)})


def _to_sibling_job(grads):
    n = len(grads)

    def make(ins, outs, sems):
        send_sems, recv_sems = sems
        mx, my, mc = lax.axis_index("x"), lax.axis_index("y"), lax.axis_index("c")
        return [[pltpu.make_async_remote_copy(
            src_ref=ins[a].at[:, :, 1 - mc], dst_ref=outs[a], send_sem=send_sems.at[a], recv_sem=recv_sems.at[a],
            device_id=(mx, my, 1 - mc), device_id_type=MESH) for a in range(n)]]

    outs = [jax.ShapeDtypeStruct(g.shape[:2] + g.shape[3:], g.dtype) for g in grads]
    return _Job(grads, outs, _dma_sems((n,), (n,)), make)


def _scatter_job(sums):
    n = len(sums)

    def make(ins, outs, sems):
        send_sems, recv_sems = sems
        mx, my, mc = lax.axis_index("x"), lax.axis_index("y"), lax.axis_index("c")
        copies = []
        for a in range(n):
            for k in range(1, N_CHIPS):
                px, py = _flip(mx, k & 2), _flip(my, k & 1)
                copies.append(pltpu.make_async_remote_copy(
                    src_ref=ins[a].at[2 * px + py], dst_ref=outs[a].at[k - 1], send_sem=send_sems.at[a, k - 1],
                    recv_sem=recv_sems.at[a, k - 1], device_id=(px, py, mc), device_id_type=MESH))
        return [copies]

    pairs = (n, N_CHIPS - 1)
    return _Job(sums, [jax.ShapeDtypeStruct((N_CHIPS - 1,) + s.shape[1:], s.dtype) for s in sums], _dma_sems(pairs, pairs), make)


def _place_job(wholes):
    n = len(wholes)

    def make(ins, outs, sems):
        send_sems, recv_sems = sems
        mx, my, mc = lax.axis_index("x"), lax.axis_index("y"), lax.axis_index("c")
        copies = []
        for a in range(n):
            here = outs[a].at[:, mc]
            copies.append(pltpu.make_async_remote_copy(src_ref=here, dst_ref=here, send_sem=send_sems.at[a], recv_sem=recv_sems.at[a],
                                                       device_id=(mx, my, 1 - mc), device_id_type=MESH))
        return [copies]

    return _Job(wholes, _same(wholes), _dma_sems((n,), (n,)), make, {a: a for a in range(n)})


def _gather8_job(x):
    def make(ins, outs, sems):
        local_sem, send_sems, recv_sems = sems
        mx, my, mc = lax.axis_index("x"), lax.axis_index("y"), lax.axis_index("c")
        here = outs[0].at[4 * mx + 2 * my + mc]
        copies = [pltpu.make_async_copy(ins[0], here, local_sem.at[0])]
        for k in range(1, N_DEV):
            peer = (_flip(mx, k & 4), _flip(my, k & 2), _flip(mc, k & 1))
            copies.append(pltpu.make_async_remote_copy(src_ref=ins[0], dst_ref=here, send_sem=send_sems.at[k - 1],
                                                       recv_sem=recv_sems.at[k - 1], device_id=peer, device_id_type=MESH))
        return [copies]

    return _Job([x], [jax.ShapeDtypeStruct((N_DEV,) + x.shape, x.dtype)], _dma_sems((1,), (N_DEV - 1,), (N_DEV - 1,)), make)


def _run_phases(phases, first=0):
    for n, phase in enumerate(phases):
        if n >= first:
            for cp in phase:
                cp.start()
        for cp in phase:
            cp.wait()


def _run_job(job, name):
    k_in, k_out = len(job.ins), len(job.outs)

    def body(*refs):
        _run_phases(job.make(refs[:k_in], refs[k_in:k_in + k_out], refs[k_in + k_out:]))

    hbm = pl.BlockSpec(memory_space=pl.ANY)
    return pl.pallas_call(body, name=name, in_specs=[hbm] * k_in, out_specs=[hbm] * k_out, out_shape=job.outs,
                          scratch_shapes=job.sems, input_output_aliases=job.alias, compiler_params=_params())(*job.ins)


def _hosted(body, job, args, *, name, grid, in_specs, out_specs, out_shape, scratch_shapes=(), semantics, aliases=None):
    in_specs, out_specs, out_shape, scratch = list(in_specs), list(out_specs), list(out_shape), list(scratch_shapes)
    aliases = dict(aliases or {})
    if job is None:
        outs = pl.pallas_call(body, name=name, grid=grid, in_specs=in_specs, out_specs=out_specs, out_shape=out_shape,
                              scratch_shapes=scratch, input_output_aliases=aliases,
                              compiler_params=_params(dimension_semantics=semantics))(*args)
        return list(outs), None
    n_in, n_out, n_scr, k_in, k_out = len(in_specs), len(out_specs), len(scratch), len(job.ins), len(job.outs)

    def wrapped(*refs):
        ins, rest = refs[:n_in], refs[n_in:]
        job_ins, rest = rest[:k_in], rest[k_in:]
        outs, rest = rest[:n_out], rest[n_out:]
        job_outs, rest = rest[:k_out], rest[k_out:]
        scr, sems = rest[:n_scr], rest[n_scr:]
        ids = [pl.program_id(a) for a in range(len(grid))]
        first = functools.reduce(jnp.logical_and, [i == 0 for i in ids])
        last = functools.reduce(jnp.logical_and, [i == g - 1 for i, g in zip(ids, grid)])

        @pl.when(first)
        def _():
            for cp in job.make(job_ins, job_outs, sems)[0]:
                cp.start()

        body(*ins, *outs, *scr)

        @pl.when(last)
        def _():
            _run_phases(job.make(job_ins, job_outs, sems), first=1)

    hbm = pl.BlockSpec(memory_space=pl.ANY)
    outs = pl.pallas_call(
        wrapped, name=name, grid=grid, in_specs=in_specs + [hbm] * k_in, out_specs=out_specs + [hbm] * k_out,
        out_shape=out_shape + job.outs, scratch_shapes=scratch + job.sems,
        input_output_aliases={**aliases, **{n_in + i: n_out + o for i, o in job.alias.items()}},
        compiler_params=_params(dimension_semantics=("arbitrary",) * len(grid)))(*args, *job.ins)
    return list(outs[:n_out]), list(outs[n_out:])


def _proj_fwd(x, mod, wg, job=None, chunk=(0, 1), tm=512):
    S, D = x.shape
    tm = min(tm, S)
    tn = wg.shape[-1]
    c, n = chunk

    rows = S // tm

    def body(x_ref, mod_ref, w_ref, proj_ref, h_ref, hs):
        mine = pl.ds(pl.multiple_of(pl.program_id(1) * tm, tm), tm)

        @pl.when(pl.program_id(0) == 0)
        def _():
            xs, _ = _standardize(x_ref[...])
            h = xs * (1.0 + mod_ref[:, D:2 * D]) + mod_ref[:, 0:D]
            hb = h.astype(BF)
            hs[mine, :] = hb
            h_ref[...] = hb

        proj_ref[...] = _dot(hs[mine, :], w_ref[...])

    once = lambda j, i: jnp.where(j == 0, i, rows - 1)
    return _hosted(
        body, job, (x, mod, wg), name="proj_fwd",
        grid=(N_CHIPS, rows),
        in_specs=[pl.BlockSpec((tm, D), lambda j, i: (once(j, i), 0)),
                  pl.BlockSpec((1, 3 * D), lambda j, i: (0, 0)),
                  pl.BlockSpec((None, D, tn), lambda j, i: (j, 0, 0))],
        out_specs=[pl.BlockSpec((tm, tn), lambda j, i: (i, j * n + c)),
                   pl.BlockSpec((tm, D), lambda j, i: (once(j, i), 0))],
        out_shape=[jax.ShapeDtypeStruct((S, IN_COLS), F32), jax.ShapeDtypeStruct((S, D), BF)],
        scratch_shapes=[pltpu.VMEM((S, D), BF)],
        semantics=("arbitrary", "arbitrary"))


def _proj_cols(h, wg, chunk, proj, job=None, tm=512):
    S, D = h.shape
    tm = min(tm, S)
    tn = wg.shape[-1]
    c, n = chunk

    def body(h_ref, w_ref, prev_ref, proj_ref):
        mine = pl.ds(pl.multiple_of(pl.program_id(1) * tm, tm), tm)
        proj_ref[...] = _dot(h_ref[mine, :], w_ref[...])

    in_specs = [pl.BlockSpec((S, D), lambda j, i: (0, 0)), pl.BlockSpec((None, D, tn), lambda j, i: (j, 0, 0)),
                pl.BlockSpec(memory_space=pl.ANY)]
    return _hosted(body, job, (h, wg, proj), name="proj_cols", grid=(N_CHIPS, S // tm), in_specs=in_specs,
                   out_specs=[pl.BlockSpec((tm, tn), lambda j, i: (i, j * n + c))],
                   out_shape=[jax.ShapeDtypeStruct(proj.shape, proj.dtype)],
                   semantics=("arbitrary", "arbitrary"), aliases={2: 0})


SB_ROWS = 256
SB_KEYS = 256


def _attn_consts():
    j = np.arange(SB_KEYS)[:, None]
    s = np.arange(SB_KEYS)[None, :]
    from_here = np.concatenate([(j >= s), (j >= s)], axis=0).astype(np.float32)
    return jnp.asarray(from_here, BF), jnp.asarray((j <= s).astype(np.float32), BF)


def _hi_lo(x):
    hi = lax.bitcast_convert_type(lax.bitcast_convert_type(x, jnp.uint32) & jnp.uint32(0xFFFF0000), F32)
    return hi.astype(BF), (x - hi).astype(BF)


def _sums_r(x, t2):
    hi, lo = _hi_lo(x)
    return _dot(jnp.concatenate([hi, lo], axis=1), t2)


def _all_lanes(col, lanes):
    return jnp.broadcast_to(col, (col.shape[0], lanes))


def _attn_rows(ref, r0, rows, lanes, head0, scale=None):
    v = ref[pl.ds(r0, rows), lanes]
    if scale is not None:
        v = v * scale
    return jnp.concatenate([jnp.where(head0, v, 0.0), jnp.where(head0, 0.0, v)], axis=0).astype(BF)


SB_PAIRS_FWD = 4
SB_PAIRS_BWD = 2


def _attn_specs(S, n_pairs):
    return lambda base: pl.BlockSpec((S, n_pairs * SB_BLOCK), lambda p, base=base: (0, base // n_pairs + p))


def _attn_scores(q2n, k_ref, lanes, kj, t2, from_here_ref, masked):
    c0 = pl.multiple_of(kj * SB_KEYS, SB_KEYS)
    kb = k_ref[pl.ds(c0, SB_KEYS), lanes].astype(BF)
    zn = _dot_nt(q2n, kb)
    lsb = jnp.minimum(zn, 0.0) - jnp.log(1.0 + jnp.exp(-jnp.abs(zn)))
    valid = None
    if masked:
        valid = (lax.broadcasted_iota(jnp.int32, zn.shape, 1) + kj * SB_KEYS) < t2
        lsb = jnp.where(valid, lsb, 0.0)
    return c0, kb, zn, valid, lsb, _sums_r(lsb, from_here_ref[...])


def _attn_fwd(proj, from_here, job=None):
    S = proj.shape[0]
    TQ = SB_ROWS
    assert S % TQ == 0 and SB_KEYS == TQ
    scale = SB_HEAD_DIM ** -0.5
    n_pairs = SB_PAIRS_FWD
    pairs = range(n_pairs)
    lanes = [pl.ds(p * SB_BLOCK, SB_BLOCK) for p in pairs]

    def body(q_ref, k_ref, v_ref, from_here_ref, o_ref, tot_ref, run, acc):
        head0 = lax.broadcasted_iota(jnp.int32, (1, 2 * SB_HEAD_DIM), 1) < SB_HEAD_DIM

        def qloop(qi, _):
            r0 = pl.multiple_of(qi * TQ, TQ)
            q2n = [_attn_rows(q_ref, r0, TQ, lanes[p], head0, -scale) for p in pairs]
            trow = lax.broadcasted_iota(jnp.int32, (TQ, SB_KEYS), 0) + qi * TQ
            t2 = jnp.concatenate([trow, trow], axis=0)
            run[...] = jnp.zeros_like(run)
            acc[...] = jnp.zeros_like(acc)

            def step(kj, masked):
                got = [_attn_scores(q2n[p], k_ref, lanes[p], kj, t2, from_here_ref, masked) for p in pairs]
                for p in pairs:
                    c0, _, zn, valid, _, sums = got[p]
                    r = run[p]
                    e = sums - zn + jnp.concatenate([r, r], axis=1)
                    if masked:
                        e = jnp.where(valid, e, -jnp.inf)
                    acc[p] += _dot(jnp.exp(e).astype(BF), v_ref[pl.ds(c0, SB_KEYS), lanes[p]].astype(BF))
                    run[p] = r + _all_lanes(sums[:, 0:1], SB_BLOCK)

            step(qi, True)

            def below(n, _):
                step(qi - 1 - n, False)
                return 0

            lax.fori_loop(0, qi, below, 0)
            for p in pairs:
                o_ref[pl.ds(r0, TQ), lanes[p]] = jnp.where(head0, acc[p, 0:TQ, :], acc[p, TQ:2 * TQ, :])
                tot_ref[p, 0, pl.ds(r0, TQ), :] = run[p, 0:TQ, :]
                tot_ref[p, 1, pl.ds(r0, TQ), :] = run[p, TQ:2 * TQ, :]
            return 0

        lax.fori_loop(0, S // TQ, qloop, 0)

    col = _attn_specs(S, n_pairs)
    state = pltpu.VMEM((n_pairs, 2 * TQ, SB_BLOCK), F32)
    return _hosted(
        body, job, (proj, proj, proj, from_here), name="attn_fwd",
        grid=(WIDTH // (n_pairs * SB_BLOCK),),
        in_specs=[col(0), col(4), col(8), pl.BlockSpec(from_here.shape, lambda p: (0, 0))],
        out_specs=[col(0), pl.BlockSpec((n_pairs, 2, S, 128), lambda p: (p, 0, 0, 0))],
        out_shape=[jax.ShapeDtypeStruct((S, WIDTH), F32), jax.ShapeDtypeStruct((WIDTH // 128, 2, S, 128), F32)],
        scratch_shapes=[state, state],
        semantics=("parallel",))


def _attn_bwd(proj, d_o, tot, from_here, up_to, dproj, job=None):
    S = proj.shape[0]
    TQ = SB_ROWS
    assert S % TQ == 0 and SB_KEYS == TQ
    scale = SB_HEAD_DIM ** -0.5
    n_pairs = SB_PAIRS_BWD
    pairs = range(n_pairs)
    lanes = [pl.ds(p * SB_BLOCK, SB_BLOCK) for p in pairs]

    def body(q_ref, k_ref, v_ref, do_ref, tot_ref, from_here_ref, up_to_ref, _, dproj_ref, pre, cum, dq_acc, dk_acc, dv_acc, stage, sems):
        dq_ref, dk_ref, dv_ref = stage.at[0], stage.at[1], stage.at[2]
        head0 = lax.broadcasted_iota(jnp.int32, (1, 2 * SB_HEAD_DIM), 1) < SB_HEAD_DIM
        dk_acc[...] = jnp.zeros_like(dk_acc)
        dv_acc[...] = jnp.zeros_like(dv_acc)

        def qloop(qi, _):
            r0 = pl.multiple_of(qi * TQ, TQ)
            q2n = [_attn_rows(q_ref, r0, TQ, lanes[p], head0, -scale) for p in pairs]
            do2 = [_attn_rows(do_ref, r0, TQ, lanes[p], head0) for p in pairs]
            trow = lax.broadcasted_iota(jnp.int32, (TQ, SB_KEYS), 0) + qi * TQ
            t2 = jnp.concatenate([trow, trow], axis=0)
            for p in pairs:
                pre[p, 0:TQ, :] = tot_ref[p, 0, pl.ds(r0, TQ), :]
                pre[p, TQ:2 * TQ, :] = tot_ref[p, 1, pl.ds(r0, TQ), :]
            cum[...] = jnp.zeros_like(cum)
            dq_acc[...] = jnp.zeros_like(dq_acc)

            def step(kj, masked):
                got = [_attn_scores(q2n[p], k_ref, lanes[p], kj, t2, from_here_ref, masked) for p in pairs]
                das = [_dot_nt(do2[p], v_ref[pl.ds(got[p][0], SB_KEYS), lanes[p]].astype(BF)) for p in pairs]
                for p in pairs:
                    c0, kb, zn, valid, lsb, sums = got[p]
                    later = pre[p] - _all_lanes(sums[:, 0:1], SB_BLOCK)
                    pre[p] = later
                    e = sums - zn + jnp.concatenate([later, later], axis=1)
                    sig = jnp.exp(lsb - zn)
                    if masked:
                        e = jnp.where(valid, e, -jnp.inf)
                        sig = jnp.where(valid, sig, 0.0)
                    a = jnp.exp(e)
                    w = das[p] * a
                    upto = _dot(w.astype(BF), up_to_ref[...])
                    c = cum[p]
                    dz = w - sig * (upto + jnp.concatenate([c, c], axis=1))
                    cum[p] = c + _all_lanes(upto[:, SB_KEYS - 1:SB_KEYS], SB_BLOCK)
                    dzb = dz.astype(BF)
                    dq_acc[p] += _dot(dzb, kb)
                    dk_acc[pl.ds(c0, SB_KEYS), lanes[p]] += _dot_tn(dzb, q2n[p])
                    dv_acc[pl.ds(c0, SB_KEYS), lanes[p]] += _dot_tn(a.astype(BF), do2[p])

            def below(kj, _):
                step(kj, False)
                return 0

            lax.fori_loop(0, qi, below, 0)
            step(qi, True)
            for p in pairs:
                dq_ref[pl.ds(r0, TQ), lanes[p]] = (jnp.where(head0, dq_acc[p, 0:TQ, :], dq_acc[p, TQ:2 * TQ, :]) * scale).astype(BF)
            return 0

        lax.fori_loop(0, S // TQ, qloop, 0)
        dk_ref[...] = (-dk_acc[...]).astype(BF)
        dv_ref[...] = dv_acc[...].astype(BF)
        lane0 = pl.program_id(0) * (n_pairs * SB_BLOCK)
        _put_columns([dq_ref, dk_ref, dv_ref], dproj_ref, [WIDTH * n + lane0 for n in range(3)], sems)

    col = _attn_specs(S, n_pairs)
    whole = lambda a: pl.BlockSpec(a.shape, lambda p: (0, 0))
    hbm = pl.BlockSpec(memory_space=pl.ANY)
    state = pltpu.VMEM((n_pairs, 2 * TQ, SB_BLOCK), F32)
    grads = pltpu.VMEM((S, n_pairs * SB_BLOCK), F32)
    return _hosted(
        body, job, (proj, proj, proj, d_o, tot, from_here, up_to, dproj), name="attn_bwd",
        grid=(WIDTH // (n_pairs * SB_BLOCK),),
        in_specs=[col(0), col(4), col(8), col(0), pl.BlockSpec((n_pairs, 2, S, 128), lambda p: (p, 0, 0, 0)), whole(from_here),
                  whole(up_to), hbm],
        out_specs=[hbm],
        out_shape=[jax.ShapeDtypeStruct(dproj.shape, dproj.dtype)],
        scratch_shapes=[state, state, state, grads, grads, pltpu.VMEM((3, S, n_pairs * SB_BLOCK), BF), pltpu.SemaphoreType.DMA((3,))],
        semantics=("arbitrary",), aliases={7: 0})


HG_LEVELS = tuple(HG_CHUNK >> n for n in range(1, HG_CHUNK.bit_length()))


def _hgrn_consts():
    C = HG_CHUNK
    t = np.arange(C)[:, None]
    s = np.arange(C)[None, :]
    rows = [(s <= t), (s > t)]
    masks = [(t == s)]
    for m in HG_LEVELS:
        two = 2 * m
        mid = (t // two) * two + m
        right = (t % two) >= m
        rows.append((right & (s >= mid) & (s <= t)) | ((~right) & (s > t) & (s <= mid - 1)))
        masks.append(((t // two) == (s // two)) & right & ((s % two) < m))
    tri = np.concatenate(rows, axis=0).astype(np.float32)
    twice = lambda a: jnp.asarray(np.concatenate([a, a], axis=1), BF)
    return (twice(tri), twice(tri.T), jnp.asarray(np.stack(masks).astype(np.float32), F32))


HG_SUM_BLOCKS = 2 + len(HG_LEVELS)


def _split_rows(g):
    hi = g.astype(BF)
    return jnp.concatenate([hi, (g - hi.astype(F32)).astype(BF)], axis=0)


def _hgrn_sum_blocks(e):
    C = HG_CHUNK
    blocks = tuple(e[n * C:(n + 1) * C] for n in range(HG_SUM_BLOCKS))
    return blocks + (jnp.broadcast_to(e[C - 1:C], (HG_DIM, e.shape[1])),)


@jax.custom_vjp
def _hgrn_sums(tri, tri_t, g):
    return _hgrn_sum_blocks(_dot(tri, _split_rows(g)))


def _hgrn_sums_fwd(tri, tri_t, g):
    return _hgrn_sums(tri, tri_t, g), (tri, tri_t)


def _hgrn_sums_bwd(res, ds):
    tri, tri_t = res
    C = HG_CHUNK
    last = lax.broadcasted_iota(jnp.int32, (C, 1), 0) == C - 1
    prefix = ds[0] + jnp.where(last, jnp.sum(ds[-1], axis=0, keepdims=True), 0.0)
    d = jnp.concatenate((prefix,) + tuple(ds[1:-1]), axis=0)
    return jnp.zeros_like(tri), jnp.zeros_like(tri_t), _dot(tri_t[:, :d.shape[0]], d.astype(BF))


_hgrn_sums.defvjp(_hgrn_sums_fwd, _hgrn_sums_bwd)


def _bf_dot(a, b):
    return _dot(a.astype(BF), b.astype(BF))


def _bf_dot_nt(a, b):
    return _dot_nt(a.astype(BF), b.astype(BF))


def _bf_dot_tn(a, b):
    return _dot_tn(a.astype(BF), b.astype(BF))


@jax.custom_vjp
def _mm(a, b):
    return _bf_dot(a, b)


_mm.defvjp(lambda a, b: (_bf_dot(a, b), (a, b)), lambda r, ct: (_bf_dot_nt(ct, r[1]), _bf_dot_tn(r[0], ct)))


@jax.custom_vjp
def _mm_nt(a, b):
    return _bf_dot_nt(a, b)


_mm_nt.defvjp(lambda a, b: (_bf_dot_nt(a, b), (a, b)), lambda r, ct: (_bf_dot(ct, r[1]), _bf_dot_tn(ct, r[0])))


@jax.custom_vjp
def _mm_tn(a, b):
    return _bf_dot_tn(a, b)


_mm_tn.defvjp(lambda a, b: (_bf_dot_tn(a, b), (a, b)), lambda r, ct: (_bf_dot_nt(r[1], ct), _bf_dot(r[0], ct)))


def _hgrn_gates(tri, tri_t, qraw, fpre, lb):
    q = _silu(qraw)
    f = lb + (1.0 - lb) * _sigmoid(fpre)
    return q, 1.0 - f, _hgrn_sums(tri, tri_t, jnp.log(f))


def _hgrn_mix(masks, q, k, e, v, st):
    prefix, suffix, whole = e[0], e[1], e[-1]
    scores = masks[0] * _mm_nt(q, k)
    for n in range(len(HG_LEVELS)):
        decay = jnp.exp(e[2 + n])
        scores = scores + masks[n + 1] * _mm_nt(q * decay, k * decay)
    o = _mm_nt(q * jnp.exp(prefix), st) + _mm(scores, v)
    st_new = st * jnp.exp(whole) + _mm_tn(v, k * jnp.exp(suffix))
    return o, st_new


def _hgrn_chunk(tri, tri_t, masks, qraw, fpre, v, st, lb):
    q, k, e = _hgrn_gates(tri, tri_t, qraw, fpre, lb)
    return _hgrn_mix(masks, q, k, e, v, st)


HG_HEADS_PER_STEP = 4
HG_LANES = HG_HEADS_PER_STEP * HG_DIM


def _hgrn_specs(S, consts):
    col = lambda base: pl.BlockSpec((S, HG_LANES), lambda p, base=base: (0, base // HG_HEADS_PER_STEP + p))
    whole = [pl.BlockSpec(a.shape, lambda p, n=a.ndim: (0,) * n) for a in consts]
    return col, whole


def _hgrn_fwd(proj, lbs, consts, job=None):
    S = proj.shape[0]
    nc = S // HG_CHUNK
    heads = range(HG_HEADS_PER_STEP)

    def body(q_ref, f_ref, i_ref, lb_ref, tri_ref, trit_ref, mask_ref, o_ref, st_ref):
        tri, tri_t = tri_ref[...], trit_ref[...]
        masks = [mask_ref[n] for n in range(len(HG_LEVELS) + 1)]

        def chunk(ci, sts):
            r0 = pl.multiple_of(ci * HG_CHUNK, HG_CHUNK)
            rows = pl.ds(r0, HG_CHUNK)
            new = []
            lane = [pl.ds(hd * HG_DIM, HG_DIM) for hd in heads]
            gates = [_hgrn_gates(tri, tri_t, q_ref[rows, lane[hd]], f_ref[rows, lane[hd]], lb_ref[hd]) for hd in heads]
            for hd in heads:
                st_ref[hd, ci] = sts[hd]
                o, st_new = _hgrn_mix(masks, *gates[hd], i_ref[rows, lane[hd]], sts[hd])
                o_ref[rows, lane[hd]] = o
                new.append(st_new)
            return tuple(new)

        lax.fori_loop(0, nc, chunk, tuple(jnp.zeros((HG_DIM, HG_DIM), F32) for _ in heads))

    col, whole = _hgrn_specs(S, consts)
    return _hosted(
        body, job, (proj, proj, proj, lbs, *consts), name="hgrn_fwd",
        grid=(WIDTH // HG_LANES,),
        in_specs=[col(16), col(20), col(24), pl.BlockSpec((HG_HEADS_PER_STEP, 1, 128), lambda p: (p, 0, 0))] + whole,
        out_specs=[col(0), pl.BlockSpec((HG_HEADS_PER_STEP, nc, HG_DIM, HG_DIM), lambda p: (p, 0, 0, 0))],
        out_shape=[jax.ShapeDtypeStruct((S, WIDTH), F32), jax.ShapeDtypeStruct((WIDTH // 128, nc, HG_DIM, HG_DIM), F32)],
        semantics=("parallel",))


def _hgrn_bwd(proj, lbs, states, d_o, consts, dproj, job=None):
    S = proj.shape[0]
    nc = S // HG_CHUNK

    def body(q_ref, f_ref, i_ref, lb_ref, st_ref, do_ref, tri_ref, trit_ref, mask_ref, _, dlb_ref, dproj_ref, stage, sems):
        dq_ref, df_ref, di_ref = stage.at[0], stage.at[1], stage.at[2]
        masks = [mask_ref[n] for n in range(len(HG_LEVELS) + 1)]
        fn = functools.partial(_hgrn_chunk, tri_ref[...], trit_ref[...], masks)
        heads = range(HG_HEADS_PER_STEP)

        def chunk(n, carry):
            ci = nc - 1 - n
            r0 = pl.multiple_of(ci * HG_CHUNK, HG_CHUNK)
            rows = pl.ds(r0, HG_CHUNK)
            new = []
            lane = [pl.ds(hd * HG_DIM, HG_DIM) for hd in heads]
            pulls = [jax.vjp(fn, q_ref[rows, lane[hd]], f_ref[rows, lane[hd]], i_ref[rows, lane[hd]], st_ref[hd, ci], lb_ref[hd])[1]
                     for hd in heads]
            for hd in heads:
                d_st, dlb = carry[hd]
                lanes = lane[hd]
                dq, df, di, d_prev, dl = pulls[hd]((do_ref[rows, lanes], d_st))
                dq_ref[rows, lanes] = dq.astype(BF)
                df_ref[rows, lanes] = df.astype(BF)
                di_ref[rows, lanes] = di.astype(BF)
                new.append((d_prev, dlb + dl))
            return tuple(new)

        zero = (jnp.zeros((HG_DIM, HG_DIM), F32), jnp.zeros((1, HG_DIM), F32))
        done = lax.fori_loop(0, nc, chunk, tuple(zero for _ in heads))
        for hd in heads:
            dlb_ref[hd] = done[hd][1]
        lane0 = pl.program_id(0) * HG_LANES
        _put_columns([dq_ref, df_ref, di_ref], dproj_ref, [WIDTH * n + lane0 for n in (4, 5, 6)], sems)

    col, whole = _hgrn_specs(S, consts)
    head = pl.BlockSpec((HG_HEADS_PER_STEP, 1, 128), lambda p: (p, 0, 0))
    hbm = pl.BlockSpec(memory_space=pl.ANY)
    n_in = 6 + len(consts)
    return _hosted(
        body, job, (proj, proj, proj, lbs, states, d_o, *consts, dproj), name="hgrn_bwd",
        grid=(WIDTH // HG_LANES,),
        in_specs=[col(16), col(20), col(24), head, pl.BlockSpec((HG_HEADS_PER_STEP, nc, HG_DIM, HG_DIM), lambda p: (p, 0, 0, 0)), col(0)]
        + whole + [hbm],
        out_specs=[head, hbm],
        out_shape=[jax.ShapeDtypeStruct((WIDTH // 128, 1, 128), F32), jax.ShapeDtypeStruct(dproj.shape, dproj.dtype)],
        scratch_shapes=[pltpu.VMEM((3, S, HG_LANES), BF), pltpu.SemaphoreType.DMA((3,))],
        semantics=("arbitrary",), aliases={n_in: 1})


def _shift_down(x, n):
    rows = lax.broadcasted_iota(jnp.int32, x.shape, 0)
    return jnp.where(rows >= n, pltpu.roll(x, n, 0), 0.0)


def _shift_up(x, n):
    S = x.shape[0]
    rows = lax.broadcasted_iota(jnp.int32, x.shape, 0)
    return jnp.where(rows < S - n, pltpu.roll(x, S - n, 0), 0.0)


def _branch_fwd(proj, o_a, o_b, norm_w, conv_w, layer):
    S = proj.shape[0]

    def body(oa_ref, za_ref, ob_ref, zb_ref, nw_ref, pre_ref, post_ref, u_ref, zc_ref, cw_ref, ya_ref, yb_ref, yc_ref):
        ya_ref[...] = (oa_ref[...] * _silu(za_ref[...])).astype(BF)
        ob = ob_ref[...]
        rn = lax.rsqrt(jnp.mean(ob * ob, axis=-1, keepdims=True) + RMS_EPS)
        yb_ref[...] = (ob * rn * nw_ref[layer:layer + 1, :] * _silu(zb_ref[...])).astype(BF)
        pu = pre_ref[...] * u_ref[...]
        conv = cw_ref[2:3, :] * pu + cw_ref[1:2, :] * _shift_down(pu, 1) + cw_ref[0:1, :] * _shift_down(pu, 2)
        yc_ref[...] = (post_ref[...] * conv * _silu(zc_ref[...])).astype(BF)

    col = lambda base: pl.BlockSpec((S, 128), lambda p, base=base: (0, base + p))
    out = jax.ShapeDtypeStruct((S, WIDTH), BF)
    return pl.pallas_call(
        body, name="branch_fwd",
        grid=(WIDTH // 128,),
        in_specs=[col(0), col(12), col(0), col(28), pl.BlockSpec(norm_w.shape, lambda p: (0, 0)),
                  col(32), col(36), col(40), col(44), pl.BlockSpec((None, None, 3, 128), lambda p: (p, layer, 0, 0))],
        out_specs=[col(0), col(0), col(0)],
        out_shape=[out, out, out],
        compiler_params=_params(dimension_semantics=("parallel",)),
    )(o_a, proj, o_b, proj, norm_w, proj, proj, proj, proj, conv_w)


def _put_columns(tiles, dproj_ref, firsts, sems):
    copies = [pltpu.make_async_copy(t, dproj_ref.at[:, pl.ds(pl.multiple_of(c, 128), t.shape[1])], sems.at[n])
              for n, (t, c) in enumerate(zip(tiles, firsts))]
    for cp in copies:
        cp.start()
    for cp in copies:
        cp.wait()


def _branch_bwd(proj, o_a, o_b, norm_w, conv_w, dy_a, dy_b, dy_c, dproj, layer):
    S = proj.shape[0]
    firsts = [WIDTH * n for n in (3, 7, 8, 9, 10, 11)]

    def dsilu(z):
        s = _sigmoid(z)
        return s * z, s * (1.0 + z * (1.0 - s))

    def body(oa_ref, za_ref, ob_ref, zb_ref, nw_ref, pre_ref, post_ref, u_ref, zc_ref, cw_ref, dya_ref, dyb_ref, dyc_ref, _,
             doa_ref, dob_ref, dnw_ref, dcw_ref, dproj_ref, stage, sems):
        dza_ref, dzb_ref, dpre_ref, dpost_ref, du_ref, dzc_ref = [stage.at[n] for n in range(6)]
        dya = dya_ref[...]
        sa, dsa = dsilu(za_ref[...])
        doa_ref[...] = dya * sa
        dza_ref[...] = (dya * oa_ref[...] * dsa).astype(BF)

        dyb = dyb_ref[...]
        ob = ob_ref[...]
        nw = nw_ref[layer:layer + 1, :]
        sb, dsb = dsilu(zb_ref[...])
        rn = lax.rsqrt(jnp.mean(ob * ob, axis=-1, keepdims=True) + RMS_EPS)
        on = ob * rn
        dzb_ref[...] = (dyb * on * nw * dsb).astype(BF)
        don_w = dyb * sb
        dnw_ref[...] = jnp.sum(don_w * on, axis=0, keepdims=True)
        don = don_w * nw
        dob_ref[...] = rn * (don - on * jnp.mean(don * on, axis=-1, keepdims=True))

        dyc = dyc_ref[...]
        pre, post, u = pre_ref[...], post_ref[...], u_ref[...]
        sc, dsc = dsilu(zc_ref[...])
        pu = pre * u
        pu1, pu2 = _shift_down(pu, 1), _shift_down(pu, 2)
        conv = cw_ref[2:3, :] * pu + cw_ref[1:2, :] * pu1 + cw_ref[0:1, :] * pu2
        dzc_ref[...] = (dyc * post * conv * dsc).astype(BF)
        dpost_ref[...] = (dyc * conv * sc).astype(BF)
        dconv = dyc * post * sc
        dcw_ref[0:1, :] = jnp.sum(dconv * pu2, axis=0, keepdims=True)
        dcw_ref[1:2, :] = jnp.sum(dconv * pu1, axis=0, keepdims=True)
        dcw_ref[2:3, :] = jnp.sum(dconv * pu, axis=0, keepdims=True)
        dpu = cw_ref[2:3, :] * dconv + cw_ref[1:2, :] * _shift_up(dconv, 1) + cw_ref[0:1, :] * _shift_up(dconv, 2)
        dpre_ref[...] = (dpu * u).astype(BF)
        du_ref[...] = (dpu * pre).astype(BF)
        lane0 = pl.program_id(0) * 128
        _put_columns([stage.at[n] for n in range(6)], dproj_ref, [c + lane0 for c in firsts], sems)

    col = lambda base: pl.BlockSpec((S, 128), lambda p, base=base: (0, base + p))
    f32 = jax.ShapeDtypeStruct((S, WIDTH), F32)
    hbm = pl.BlockSpec(memory_space=pl.ANY)
    return pl.pallas_call(
        body, name="branch_bwd",
        grid=(WIDTH // 128,),
        in_specs=[col(0), col(12), col(0), col(28), pl.BlockSpec(norm_w.shape, lambda p: (0, 0)),
                  col(32), col(36), col(40), col(44), pl.BlockSpec((None, None, 3, 128), lambda p: (p, layer, 0, 0)),
                  col(0), col(0), col(0), hbm],
        out_specs=[col(0), col(0), pl.BlockSpec((None, 1, 128), lambda p: (p, 0, 0)), pl.BlockSpec((None, 3, 128), lambda p: (p, 0, 0)), hbm],
        out_shape=[f32, f32, jax.ShapeDtypeStruct((WIDTH // 128, 1, 128), F32), jax.ShapeDtypeStruct((WIDTH // 128, 3, 128), F32),
                   jax.ShapeDtypeStruct(dproj.shape, dproj.dtype)],
        scratch_shapes=[pltpu.VMEM((6, S, 128), BF), pltpu.SemaphoreType.DMA((6,))],
        input_output_aliases={13: 4},
        compiler_params=_params(dimension_semantics=("arbitrary",)),
    )(o_a, proj, o_b, proj, norm_w, proj, proj, proj, proj, conv_w, dy_a, dy_b, dy_c, dproj)


def _branch_proj(y_refs, wb_ref):
    return [_dot(y_refs[i][...], wb_ref[i]) for i in range(3)]


def _merge_fwd(x, mod, proj, ys, wb, wo, ln_g, ln_b, layer, job=None, tm=512):
    S, D = x.shape
    tm = min(tm, S)

    def body(x_ref, mod_ref, ga_ref, gb_ref, gc_ref, ya_ref, yb_ref, yc_ref, wb_ref, wo_ref, g_ref, b_ref, xo_ref, mg_ref, y_ref):
        ps = _branch_proj((ya_ref, yb_ref, yc_ref), wb_ref)
        merged = _sigmoid(ga_ref[...]) * ps[0] + _sigmoid(gb_ref[...]) * ps[1] + _sigmoid(gc_ref[...]) * ps[2]
        mb = merged.astype(BF)
        mg_ref[...] = mb
        y = _dot(mb, wo_ref[...].reshape(D, D))
        y_ref[...] = y
        r = ALPHA * x_ref[...] + (1.0 + mod_ref[:, 2 * D:3 * D]) * y
        xn, _ = _standardize(r)
        xo_ref[...] = xn * g_ref[layer:layer + 1, :] + b_ref[layer:layer + 1, :]

    row = lambda w, c=0: pl.BlockSpec((tm, w), lambda i, c=c: (i, c))
    whole = lambda a: pl.BlockSpec(a.shape, lambda i, n=a.ndim: (0,) * n)
    return _hosted(
        body, job, (x, mod, proj, proj, proj, *ys, wb, wo, ln_g, ln_b), name="merge_fwd",
        grid=(S // tm,),
        in_specs=[row(D), whole(mod), row(D, 6), row(D, 7), row(D, 8), row(WIDTH), row(WIDTH), row(WIDTH), whole(wb),
                  whole(wo), whole(ln_g), whole(ln_b)],
        out_specs=[row(D), row(D), row(D)],
        out_shape=[jax.ShapeDtypeStruct((S, D), F32), jax.ShapeDtypeStruct((S, D), BF), jax.ShapeDtypeStruct((S, D), F32)],
        semantics=("parallel",))


def _merge_bwd(dxo, x, y, merged, mod, proj, ys, wb, wo, ln_g, layer, tm=256):
    S, D = x.shape
    tm = min(tm, S)
    steps = S // tm
    quarter = D // N_CHIPS

    def body(dxo_ref, x_ref, y_ref, mg_ref, mod_ref, ga_ref, gb_ref, gc_ref, ya_ref, yb_ref, yc_ref, wb_ref, wo_ref, g_ref,
             dxr_ref, dg_ref, dya_ref, dyb_ref, dyc_ref, dlg_ref, dlb_ref, dgt_ref, gbr_ref, gout_ref, acc_br, acc_out):
        @pl.when(pl.program_id(0) == 0)
        def _():
            dlg_ref[...] = jnp.zeros_like(dlg_ref)
            dlb_ref[...] = jnp.zeros_like(dlb_ref)
            dgt_ref[...] = jnp.zeros_like(dgt_ref)
            acc_br[...] = jnp.zeros_like(acc_br)
            acc_out[...] = jnp.zeros_like(acc_out)

        gate1 = 1.0 + mod_ref[:, 2 * D:3 * D]
        yv = y_ref[...]
        xn, rstd = _standardize(ALPHA * x_ref[...] + gate1 * yv)
        dxo = dxo_ref[...]
        dlg_ref[...] += jnp.sum(dxo * xn, axis=0, keepdims=True)
        dlb_ref[...] += jnp.sum(dxo, axis=0, keepdims=True)
        dr = _standardize_bwd(dxo * g_ref[layer:layer + 1, :], xn, rstd)
        dxr_ref[...] = ALPHA * dr
        dgt_ref[...] += jnp.sum(dr * yv, axis=0, keepdims=True)
        dyb = (gate1 * dr).astype(BF)
        acc_out[...] += _dot_tn(mg_ref[...], dyb)
        dmerged = _dot_nt(dyb, wo_ref[...].reshape(D, D))
        y_refs = (ya_ref, yb_ref, yc_ref)
        ps = _branch_proj(y_refs, wb_ref)
        for i, (gate_ref, out_ref) in enumerate(((ga_ref, dya_ref), (gb_ref, dyb_ref), (gc_ref, dyc_ref))):
            sg = _sigmoid(gate_ref[...])
            dg_ref[:, i * D:(i + 1) * D] = (dmerged * ps[i] * sg * (1.0 - sg)).astype(BF)
            dp = (dmerged * sg).astype(BF)
            acc_br[i] += _dot_tn(y_refs[i][...], dp)
            out_ref[...] = _dot_nt(dp, wb_ref[i])

        @pl.when(pl.program_id(0) == steps - 1)
        def _():
            for j in range(N_CHIPS):
                gout_ref[j] = acc_out[j * quarter:(j + 1) * quarter, :].astype(BF)
                for i in range(3):
                    gbr_ref[j, i] = acc_br[i, :, j * quarter:(j + 1) * quarter].astype(BF)

    row = lambda w, c=0: pl.BlockSpec((tm, w), lambda i, c=c: (i, c))
    whole = lambda a: pl.BlockSpec(a.shape, lambda i, n=len(a.shape): (0,) * n)
    vec = pl.BlockSpec((1, D), lambda i: (0, 0))
    sd = jax.ShapeDtypeStruct
    g_br, g_out = sd((N_CHIPS, 3, WIDTH, quarter), BF), sd((N_CHIPS, quarter, D), BF)
    return pl.pallas_call(
        body, name="merge_bwd",
        grid=(steps,),
        in_specs=[row(D), row(D), row(D), row(D), whole(mod), row(D, 6), row(D, 7), row(D, 8), row(WIDTH), row(WIDTH), row(WIDTH),
                  whole(wb), whole(wo), whole(ln_g)],
        out_specs=[row(D), row(3 * D, IN_COLS // (3 * D) - 1), row(WIDTH), row(WIDTH), row(WIDTH), vec, vec, vec, whole(g_br), whole(g_out)],
        out_shape=[sd((S, D), F32), sd((S, IN_COLS), BF), sd((S, WIDTH), F32), sd((S, WIDTH), F32), sd((S, WIDTH), F32),
                   sd((1, D), F32), sd((1, D), F32), sd((1, D), F32), g_br, g_out],
        scratch_shapes=[pltpu.VMEM((3, WIDTH, D), F32), pltpu.VMEM((D, D), F32)],
        compiler_params=_params(dimension_semantics=("arbitrary",)),
    )(dxo, x, y, merged, mod, proj, proj, proj, *ys, wb, wo, ln_g)


def _loss_head(x, target, tm=512):
    S, D = x.shape
    tm = min(tm, S)

    def body(x_ref, t_ref, dx_ref, loss_ref):
        @pl.when(pl.program_id(0) == 0)
        def _():
            loss_ref[...] = jnp.zeros_like(loss_ref)

        err = x_ref[...] - t_ref[...]
        dx_ref[...] = err * (1.0 / D)
        loss_ref[...] += 0.5 * jnp.sum(jnp.mean(err * err, axis=-1, keepdims=True))

    row = pl.BlockSpec((tm, D), lambda i: (i, 0))
    return pl.pallas_call(
        body, name="loss_head",
        grid=(S // tm,),
        in_specs=[row, row],
        out_specs=[row, pl.BlockSpec((8, 128), lambda i: (0, 0))],
        out_shape=[jax.ShapeDtypeStruct((S, D), F32), jax.ShapeDtypeStruct((8, 128), F32)],
        compiler_params=_params(dimension_semantics=("arbitrary",)),
    )(x, target)


def _proj_bwd(dproj, wgs, x, mod, dx_res, job=None, tm=512, tk=768):
    S, D = x.shape
    tm = min(tm, S)
    n = len(wgs)
    w_args = list(wgs)
    if n == 1:
        per = wgs[0].shape[-1] // tk
        w_specs = [pl.BlockSpec((None, D, tk), lambda k, i: (k // per, 0, k % per))]
    else:
        assert wgs[0].shape[-1] == tk
        w_specs = [pl.BlockSpec((None, D, tk), lambda k, i, c=c: (jnp.minimum((k + n - 1 - c) // n, N_CHIPS - 1), 0, 0))
                   for c in range(n)]
    nk = IN_COLS // tk

    def body(dp_ref, *rest):
        w_refs, (x_ref, mod_ref, dxr_ref, dx_ref, dsh_ref, dsc_ref, acc) = rest[:n], rest[n:]
        k, i = pl.program_id(0), pl.program_id(1)
        mine = pl.ds(pl.multiple_of(i * tm, tm), tm)

        @pl.when((i == 0) & (k == 0))
        def _():
            dsh_ref[...] = jnp.zeros_like(dsh_ref)
            dsc_ref[...] = jnp.zeros_like(dsc_ref)

        @pl.when(k == 0)
        def _():
            acc[mine, :] = jnp.zeros((tm, D), F32)

        for c in range(n):
            @pl.when(k % n == c)
            def _(c=c):
                acc[mine, :] += _dot_nt(dp_ref[...], w_refs[c][...])

        @pl.when(k == nk - 1)
        def _():
            dh = acc[mine, :]
            xs, rstd = _standardize(x_ref[...])
            dsh_ref[...] += jnp.sum(dh, axis=0, keepdims=True)
            dsc_ref[...] += jnp.sum(dh * xs, axis=0, keepdims=True)
            dx_ref[...] = _standardize_bwd(dh * (1.0 + mod_ref[:, D:2 * D]), xs, rstd) + dxr_ref[...]

    row = pl.BlockSpec((tm, D), lambda k, i: (jnp.where(k == nk - 1, i, 0), 0))
    vec = pl.BlockSpec((1, D), lambda k, i: (0, 0))
    return _hosted(
        body, job, (dproj, *w_args, x, mod, dx_res), name="proj_bwd",
        grid=(nk, S // tm),
        in_specs=[pl.BlockSpec((tm, tk), lambda k, i: (i, k))] + w_specs + [row, pl.BlockSpec((1, 3 * D), lambda k, i: (0, 0)), row],
        out_specs=[row, vec, vec],
        out_shape=[jax.ShapeDtypeStruct((S, D), F32), jax.ShapeDtypeStruct((1, D), F32), jax.ShapeDtypeStruct((1, D), F32)],
        scratch_shapes=[pltpu.VMEM((S, D), F32)],
        semantics=("arbitrary", "arbitrary"))


def _grad_w_in(h, dproj):
    S, D = h.shape
    shard = IN_COLS // N_CHIPS

    def body(h_ref, d_ref, o_ref):
        o_ref[...] = _dot_tn(h_ref[...], d_ref[...]).astype(BF)

    return pl.pallas_call(
        body, name="grad_w_in",
        grid=(N_CHIPS,),
        in_specs=[pl.BlockSpec((S, D), lambda n: (0, 0)), pl.BlockSpec((S, shard), lambda n: (0, n))],
        out_specs=pl.BlockSpec((None, D, shard), lambda n: (n, 0, 0)),
        out_shape=jax.ShapeDtypeStruct((N_CHIPS, D, shard), BF),
        compiler_params=_params(dimension_semantics=("parallel",)),
    )(h, dproj)


def _all_gather8(x, name):
    R, N = x.shape

    def body(x_ref, out_ref, send_sems, recv_sems):
        mx, my, mc = lax.axis_index("x"), lax.axis_index("y"), lax.axis_index("c")
        me = 4 * mx + 2 * my + mc
        out_ref[me] = x_ref[...]
        copies = []
        for k in range(1, N_DEV):
            peer = (_flip(mx, k & 4), _flip(my, k & 2), _flip(mc, k & 1))
            cp = pltpu.make_async_remote_copy(src_ref=x_ref, dst_ref=out_ref.at[me], send_sem=send_sems.at[k - 1],
                                              recv_sem=recv_sems.at[k - 1], device_id=peer, device_id_type=MESH)
            cp.start()
            copies.append(cp)
        for cp in copies:
            cp.wait()

    return pl.pallas_call(
        body, name=name,
        in_specs=[pl.BlockSpec(memory_space=pltpu.VMEM)],
        out_specs=pl.BlockSpec(memory_space=pltpu.VMEM),
        out_shape=jax.ShapeDtypeStruct((N_DEV, R, N), F32),
        scratch_shapes=[pltpu.SemaphoreType.DMA((N_DEV - 1,)), pltpu.SemaphoreType.DMA((N_DEV - 1,))],
        compiler_params=_params(),
    )(x)


def _rows2d(a):
    return a.reshape(-1, a.shape[-1])


def _tile_rows(rows, cols, n_arrays):
    budget = (24 << 20) // (n_arrays * 2 * 4 * cols)
    if rows <= budget:
        return rows
    tm = 8
    for cand in range(8, budget + 1, 8):
        if rows % cand == 0:
            tm = cand
    return tm


SUM_ROWS = 256


def _sum_cores(g, sent, where):
    chips, lead, _, r, cols = g.shape
    tr = min(r, SUM_ROWS)

    def body(where_ref, g_ref, s_ref, out_ref):
        out_ref[...] = (g_ref[...].astype(F32) + s_ref[...].astype(F32)).astype(BF)

    spec = pl.BlockSpec((None, tr, cols), lambda i, j, where_ref: (i, j, 0))
    out = pl.pallas_call(
        body, name="sum_cores",
        grid_spec=pltpu.PrefetchScalarGridSpec(
            num_scalar_prefetch=1, grid=(chips * lead, r // tr),
            in_specs=[pl.BlockSpec((None, None, tr, cols), lambda i, j, where_ref: (i, where_ref[1], j, 0)), spec],
            out_specs=spec),
        out_shape=jax.ShapeDtypeStruct((chips * lead, r, cols), BF),
        compiler_params=_params(dimension_semantics=("parallel", "parallel")),
    )(where, g.reshape(chips * lead, 2, r, cols), sent.reshape(chips * lead, r, cols))
    return out.reshape(chips, lead, r, cols)


def _sum_chips(sums, got, where):
    _, lead, r, cols = sums.shape
    tr = min(r, SUM_ROWS)

    def body(where_ref, s_ref, g_ref, out_ref):
        out_ref[...] = ((s_ref[...].astype(F32) + g_ref[0].astype(F32)) + g_ref[1].astype(F32)) + g_ref[2].astype(F32)

    return pl.pallas_call(
        body, name="sum_chips",
        grid_spec=pltpu.PrefetchScalarGridSpec(
            num_scalar_prefetch=1, grid=(lead, r // tr),
            in_specs=[pl.BlockSpec((None, None, tr, cols), lambda i, j, where_ref: (where_ref[0], i, j, 0)),
                      pl.BlockSpec((N_CHIPS - 1, None, tr, cols), lambda i, j, where_ref: (0, i, j, 0))],
            out_specs=pl.BlockSpec((None, None, tr, cols), lambda i, j, where_ref: (i, where_ref[1], j, 0))),
        out_shape=jax.ShapeDtypeStruct((lead, 2, r, cols), F32),
        compiler_params=_params(dimension_semantics=("parallel", "parallel")),
    )(where, sums, got)


def _adamw(w, m, v, groups, name):
    shape = w.shape
    w2, m2, v2 = _rows2d(w), _rows2d(m), _rows2d(v)
    rows, cols = w2.shape
    ng = len(groups)
    n = len(groups[0])
    slab = rows // ng
    gs = [_rows2d(g) for grp in groups for g in grp]
    tm = _tile_rows(slab, cols, 7 + n)
    tiles = slab // tm
    c1 = 1.0 / (1.0 - ADAM_B1 ** ADAM_STEP)
    c2 = 1.0 / (1.0 - ADAM_B2 ** ADAM_STEP)

    def body(*refs):
        w_ref, m_ref, v_ref = refs[:3]
        g_refs = refs[3:3 + ng * n]
        go_ref, d_ref, mo_ref, vo_ref = refs[3 + ng * n:]
        which = pl.program_id(0)
        for s in range(ng):
            @pl.when(which == s)
            def _(s=s):
                g = g_refs[s * n][...]
                for r in g_refs[s * n + 1:(s + 1) * n]:
                    g = g + r[...]
                mn = ADAM_B1 * m_ref[...] + (1.0 - ADAM_B1) * g
                vn = ADAM_B2 * v_ref[...] + (1.0 - ADAM_B2) * (g * g)
                go_ref[...] = g
                mo_ref[...] = mn
                vo_ref[...] = vn
                d_ref[...] = -ADAM_LR * ((mn * c1) / (jnp.sqrt(vn * c2) + ADAM_EPS) + ADAM_WD * w_ref[...])

    spec = pl.BlockSpec((tm, cols), lambda s, i: (s * tiles + i, 0))
    g_specs = [pl.BlockSpec((tm, cols), lambda s, i, k=k: (jnp.where(s == k, i, jnp.where(s < k, 0, tiles - 1)), 0))
               for k in range(ng) for _ in range(n)]
    outs = pl.pallas_call(
        body, name=name,
        grid=(ng, tiles),
        in_specs=[spec] * 3 + g_specs,
        out_specs=[spec] * 4,
        out_shape=[jax.ShapeDtypeStruct((rows, cols), F32)] * 4,
        compiler_params=_params(dimension_semantics=("arbitrary", "arbitrary")),
    )(w2, m2, v2, *gs)
    return [o.reshape(shape) for o in outs]


def _lower_bounds(r0, r1):
    top = jnp.maximum(r0, r1)
    e0, e1 = jnp.exp(r0 - top), jnp.exp(r1 - top)
    p0, p1 = e0 / (e0 + e1), e1 / (e0 + e1)
    return p0 - p0, (p0 + p1) - p0


def _lbs_fwd(lb_raw):
    def body(lb_ref, out_ref):
        l0, l1 = _lower_bounds(lb_ref[0:1, :], lb_ref[1:2, :])
        out_ref[0:1, :] = l0
        out_ref[1:2, :] = l1

    return pl.pallas_call(body, name="lower_bounds", out_shape=jax.ShapeDtypeStruct(lb_raw.shape, F32), compiler_params=_params())(lb_raw)


def _mod_rows(c_all, w_mod, tn=768):
    _, D, cols = w_mod.shape

    def body(c_ref, w_ref, out_ref):
        out_ref[...] = _dot(c_ref[...].astype(BF), w_ref[...].astype(BF))

    return pl.pallas_call(
        body, name="mod_rows",
        grid=(DEPTH,),
        in_specs=[pl.BlockSpec((N_DEV, D), lambda l: (0, 0)), pl.BlockSpec((None, D, cols), lambda l: (l, 0, 0))],
        out_specs=pl.BlockSpec((N_DEV, cols), lambda l: (0, l)),
        out_shape=jax.ShapeDtypeStruct((N_DEV, DEPTH * cols), F32),
        compiler_params=_params(dimension_semantics=("parallel",)),
    )(c_all, w_mod)


def _grad_w_mod(c_all, dmod_cols):
    D = c_all.shape[1]
    cols = dmod_cols.shape[-1]

    def body(c_ref, d_ref, out_ref):
        out_ref[...] = _dot_tn(c_ref[...].astype(BF), d_ref[...].astype(BF))

    return pl.pallas_call(
        body, name="grad_w_mod",
        grid=(DEPTH,),
        in_specs=[pl.BlockSpec((N_DEV, D), lambda l: (0, 0)), pl.BlockSpec((None, N_DEV, cols), lambda l: (l, 0, 0))],
        out_specs=pl.BlockSpec((None, D, cols), lambda l: (l, 0, 0)),
        out_shape=jax.ShapeDtypeStruct((DEPTH, D, cols), F32),
        compiler_params=_params(dimension_semantics=("parallel",)),
    )(c_all, dmod_cols)


def _sum_devices(parts):
    _, R, N = parts.shape

    def body(p_ref, out_ref):
        acc = p_ref[0]
        for d in range(1, N_DEV):
            acc = acc + p_ref[d]
        out_ref[...] = acc

    return pl.pallas_call(body, name="sum_devices", out_shape=jax.ShapeDtypeStruct((R, N), F32), compiler_params=_params())(parts)


def _lbs_bwd(lb_raw, dl):
    def body(lb_ref, dl_ref, out_ref):
        _, pull = jax.vjp(_lower_bounds, lb_ref[0:1, :], lb_ref[1:2, :])
        d0, d1 = pull((dl_ref[0:1, :], dl_ref[1:2, :]))
        out_ref[0:1, :] = d0
        out_ref[1:2, :] = d1

    return pl.pallas_call(body, name="lower_bounds_bwd", out_shape=jax.ShapeDtypeStruct(lb_raw.shape, F32), compiler_params=_params())(lb_raw, dl)


def kernel(x, c, w_mod, b_mod, w_in, conv_w, hgrn_norm_w, lower_bounds, w_branch, w_out, ln_g, ln_b, loss_target, m_w_mod, m_b_mod, m_w_in, m_conv_w, m_hgrn_norm_w, m_lower_bounds, m_w_branch, m_w_out, m_ln_g, m_ln_b, v_w_mod, v_b_mod, v_w_in, v_conv_w, v_hgrn_norm_w, v_lower_bounds, v_w_branch, v_w_out, v_ln_g, v_ln_b):
    D = D_MODEL
    x0 = x[0]
    target = loss_target[0]
    S = x0.shape[0]
    mx, my, mc = lax.axis_index("x"), lax.axis_index("y"), lax.axis_index("c")
    chip = 2 * mx + my
    me = 2 * chip + mc
    mod_cols = 3 * D // N_CHIPS

    plan = _Plan(w_in.astype(BF), w_branch.astype(BF), w_out.astype(BF), chip, mc)
    n_conv = DEPTH * 3 * (WIDTH // N_CHIPS)
    first = jnp.concatenate([c, conv_w.reshape(1, n_conv), jnp.zeros((1, 2 * D - D - n_conv), F32)], axis=1)
    first = plan.first(first.reshape(8, 2 * D // 8)).reshape(N_DEV, 2 * D)

    c_all = first[:, :D]
    conv_all = first[:, D:D + n_conv].reshape(N_DEV, DEPTH, 3, WIDTH // N_CHIPS)[0::2]
    mod_part = _all_gather8(_mod_rows(c_all, w_mod), "gather_mod")[0::2]
    mod_part = lax.dynamic_index_in_dim(mod_part, me, axis=1, keepdims=False).reshape(N_CHIPS, DEPTH, mod_cols)
    mods = [(mod_part[:, l].reshape(1, 3 * D) + b_mod[l][None, :]) for l in range(DEPTH)]
    lbs = _lbs_fwd(lower_bounds).reshape(DEPTH, WIDTH // 128, 1, 128)
    loss_blk, dx, small = _local_step(x0, target, mods, lbs, conv_all, hgrn_norm_w, ln_g, ln_b, plan)

    n_mod, n_nw, n_lb, n_ln, n_cw = DEPTH * 3 * D, DEPTH * 128, DEPTH * WIDTH, DEPTH * D, DEPTH * 3 * WIDTH
    row = jnp.concatenate(
        [jnp.concatenate([small[l][0], small[l][1], small[l][2]], axis=1) for l in range(DEPTH)]
        + [jnp.sum(small[l][3], axis=0) for l in range(DEPTH)]
        + [small[l][4].reshape(1, WIDTH) for l in range(DEPTH)]
        + [small[l][5] for l in range(DEPTH)] + [small[l][6] for l in range(DEPTH)]
        + [jnp.transpose(small[l][7], (1, 0, 2)).reshape(1, 3 * WIDTH) for l in range(DEPTH)]
        + [loss_blk[0:1, :]], axis=1)
    n_row = row.shape[1]
    fold = -(-n_row // (8 * 128)) * 128
    rows = jnp.concatenate([row, jnp.zeros((1, 8 * fold - n_row), F32)], axis=1).reshape(8, fold)

    whole, gathered = plan.finish(rows)
    grads = {kind: [[whole[(kind, l)]] for l in range(DEPTH)] for kind in ("in", "br", "out")}

    off_nw = n_mod
    off_lb = off_nw + n_nw
    off_lng = off_lb + n_lb
    off_lnb = off_lng + n_ln
    off_cw = off_lnb + n_ln
    off_loss = off_cw + n_cw
    total = _sum_devices(gathered).reshape(1, 8 * fold)
    gathered = gathered.reshape(N_DEV, 1, 8 * fold)
    d_lower = _lbs_bwd(lower_bounds, total[0, off_lb:off_lng].reshape(DEPTH, WIDTH))
    loss = total[0, off_loss]
    d_b_mod = total[0, :n_mod].reshape(DEPTH, 3 * D)
    d_norm_w = total[0, off_nw:off_lb].reshape(DEPTH, 128)
    d_ln_g = total[0, off_lng:off_lnb].reshape(DEPTH, D)
    d_ln_b = total[0, off_lnb:off_cw].reshape(DEPTH, D)
    d_conv = total[0, off_cw:off_loss].reshape(DEPTH, 3, N_CHIPS, WIDTH // N_CHIPS)
    d_conv = lax.dynamic_index_in_dim(d_conv, chip, axis=2, keepdims=False)
    dmod_all = gathered[:, 0, :n_mod].reshape(N_DEV, DEPTH, N_CHIPS, mod_cols)
    dmod_cols = jnp.transpose(lax.dynamic_index_in_dim(dmod_all, chip, axis=2, keepdims=False), (1, 0, 2))
    d_w_mod = _grad_w_mod(c_all, dmod_cols)

    res = {}
    res["w_mod"] = _adamw(w_mod, m_w_mod, v_w_mod, [[d_w_mod]], "adamw_w_mod")
    res["b_mod"] = _adamw(b_mod, m_b_mod, v_b_mod, [[d_b_mod]], "adamw_b_mod")
    res["w_in"] = _adamw(w_in, m_w_in, v_w_in, grads["in"], "adamw_w_in")
    res["conv_w"] = _adamw(conv_w, m_conv_w, v_conv_w, [[d_conv]], "adamw_conv_w")
    res["hgrn_norm_w"] = _adamw(hgrn_norm_w, m_hgrn_norm_w, v_hgrn_norm_w, [[d_norm_w]], "adamw_norm_w")
    res["lower_bounds"] = _adamw(lower_bounds, m_lower_bounds, v_lower_bounds, [[d_lower]], "adamw_lower_bounds")
    res["w_branch"] = _adamw(w_branch, m_w_branch, v_w_branch, grads["br"], "adamw_w_branch")
    res["w_out"] = _adamw(w_out, m_w_out, v_w_out, grads["out"], "adamw_w_out")
    res["ln_g"] = _adamw(ln_g, m_ln_g, v_ln_g, [[d_ln_g]], "adamw_ln_g")
    res["ln_b"] = _adamw(ln_b, m_ln_b, v_ln_b, [[d_ln_b]], "adamw_ln_b")
    names = ["w_mod", "b_mod", "w_in", "conv_w", "hgrn_norm_w", "lower_bounds", "w_branch", "w_out", "ln_g", "ln_b"]
    return (loss, dx[None], *[res[n][0] for n in names], *[res[n][1] for n in names],
            *[res[n][2] for n in names], *[res[n][3] for n in names])


class _Plan:
    FIRST_CHUNKS = 3
    WINDOWS = ((0, 768), (768, 896), (1664, 640))

    def __init__(self, w_in, w_br, w_out, chip, core):
        self.local = {"in": w_in, "br": w_br, "out": w_out}
        self.chip, self.where = chip, jnp.stack([chip, core]).astype(jnp.int32)
        self.gathered, self.partial, self.grads, self.chip_sums, self.scattered, self.pending = {}, {}, {}, {}, {}, {}

    def chunks(self, l):
        return self.FIRST_CHUNKS if l == 0 else 1

    def w_in(self, l):
        return [self.gathered[("in", l, c)] for c in range(self.chunks(l))]

    def _shard(self, key):
        mine = self.local[key[0]][key[1]]
        if key[0] == "in":
            cols = mine.shape[-1] // self.chunks(key[1])
            mine = mine[:, key[2] * cols:(key[2] + 1) * cols]
        return mine

    def _slab(self, key):
        mine = _halves(self._shard(key))
        return lax.dynamic_update_slice(lax.empty((N_CHIPS,) + mine.shape, mine.dtype), mine[None], (self.chip, 0, 0, 0, 0))

    def _gather(self, keys):
        return ("gather", keys), _gather_job([self._slab(key) for key in keys])

    def _gather_window(self, key, n):
        slab = self._slab(key) if n == 0 else self.partial[key]
        return ("gather" if n == len(self.WINDOWS) - 1 else "gather_part", [key]), _gather_job([slab], self.WINDOWS[n])

    def _to_sibling(self, keys):
        return ("to_sibling", keys), _to_sibling_job([_halves(self.grads[key], 1) for key in keys])

    def _scatter(self, keys):
        return ("scatter", keys), _scatter_job([self.chip_sums[key] for key in keys])

    def job(self, stage, l, c=0):
        parts = []
        first = self.chunks(l) > 1
        if stage == "proj_fwd":
            if c + 1 < self.chunks(l):
                parts = [self._gather([("in", l, c + 1)])]
            elif not first:
                parts = [self._gather([("br", l), ("out", l)])]
        elif stage == "attn_fwd":
            parts = ([self._gather([("br", l), ("out", l)])] if first else []) + (
                [self._gather_window(("in", l + 1, 0), 0)] if l + 1 < DEPTH else [])
        elif stage == "hgrn_fwd" and l + 1 < DEPTH:
            parts = [self._gather_window(("in", l + 1, 0), 1)]
        elif stage == "merge_fwd" and l + 1 < DEPTH:
            parts = [self._gather_window(("in", l + 1, 0), 2)]
        elif stage == "attn_bwd":
            parts = [self._to_sibling([("out", l), ("br", l)])] + ([self._scatter([("in", l + 1)])] if l + 1 < DEPTH else [])
        elif stage == "hgrn_bwd":
            parts = [self._scatter([("out", l), ("br", l)])]
        elif stage == "proj_bwd":
            parts = [self._to_sibling([("in", l)])] if l else [self._scatter([("in", 0)])]
        self.pending[(stage, l, c)] = [(tag, len(job.outs)) for tag, job in parts]
        return _join_jobs([job for _, job in parts])

    def done(self, stage, l, outs, c=0):
        if outs is None:
            return
        at = 0
        for (what, keys), n_outs in self.pending[(stage, l, c)]:
            mine, at = outs[at:at + n_outs], at + n_outs
            for n, key in enumerate(keys):
                if what == "gather":
                    self.gathered[key] = mine[n].reshape((N_CHIPS,) + self._shard(key).shape)
                elif what == "gather_part":
                    self.partial[key] = mine[n]
                elif what == "to_sibling":
                    self.chip_sums[key] = _sum_cores(_halves(self.grads[key], 1), mine[n], self.where)
                else:
                    self.scattered[key] = mine[n]

    def first(self, rows):
        tag, job = self._gather([("in", 0, 0)])
        self.pending[("first", 0, 0)] = [(tag, len(job.outs))]
        outs = _run_job(_join_jobs([job, _gather8_job(rows)]), "gather_first")
        self.done("first", 0, outs[:-1])
        return outs[-1]

    def took(self, key, grad):
        self.grads[key] = grad
        if key == ("in", 0):
            tag, job = self._to_sibling([key])
            self.pending[("took", 0, 0)] = [(tag, len(job.outs))]
            self.done("took", 0, _run_job(job, "to_sibling_last"))

    def finish(self, rows):
        keys = [(kind, l) for kind in ("in", "br", "out") for l in range(DEPTH)]
        halves = [_sum_chips(self.chip_sums[key], self.scattered[key], self.where) for key in keys]
        outs = _run_job(_join_jobs([_place_job(halves), _gather8_job(rows)]), "place_halves")
        return {key: w.reshape(self.grads[key].shape[1:]) for key, w in zip(keys, outs[:-1])}, outs[-1]


def _local_step(x0, target, mods, lbs, conv_all, hgrn_norm_w, ln_g, ln_b, plan):
    D = D_MODEL
    after, before = _attn_consts()
    hg_consts = _hgrn_consts()

    saved = []
    xl = x0
    for l in range(DEPTH):
        n = plan.chunks(l)
        (proj, h), got = _proj_fwd(xl, mods[l], plan.gathered[("in", l, 0)], plan.job("proj_fwd", l, 0), (0, n))
        plan.done("proj_fwd", l, got, 0)
        for c in range(1, n):
            (proj,), got = _proj_cols(h, plan.gathered[("in", l, c)], (c, n), proj, plan.job("proj_fwd", l, c))
            plan.done("proj_fwd", l, got, c)
        (o_a, tot), got = _attn_fwd(proj, after, plan.job("attn_fwd", l))
        plan.done("attn_fwd", l, got)
        (o_b, states), got = _hgrn_fwd(proj, lbs[l], hg_consts, plan.job("hgrn_fwd", l))
        plan.done("hgrn_fwd", l, got)
        ys = _branch_fwd(proj, o_a, o_b, hgrn_norm_w, conv_all, l)
        wb = jnp.concatenate(list(plan.gathered[("br", l)]), axis=-1)
        (x_next, merged, y), got = _merge_fwd(xl, mods[l], proj, ys, wb, plan.gathered[("out", l)], ln_g, ln_b, l, plan.job("merge_fwd", l))
        plan.done("merge_fwd", l, got)
        saved.append((xl, proj, h, o_a, tot, o_b, states, ys, merged, y, wb))
        xl = x_next
    dx, loss_blk = _loss_head(xl, target)

    small = [None] * DEPTH
    for l in reversed(range(DEPTH)):
        xin, proj, h, o_a, tot, o_b, states, ys, merged, y, wb = saved[l]
        dx_res, dproj, dy_a, dy_b, dy_c, dln_g, dln_b, dgate, g_br, g_out = _merge_bwd(
            dx, xin, y, merged, mods[l], proj, ys, wb, plan.gathered[("out", l)], ln_g, l)
        plan.took(("out", l), g_out)
        plan.took(("br", l), g_br)
        d_oa, d_ob, dnorm_w, dconv_w, dproj = _branch_bwd(proj, o_a, o_b, hgrn_norm_w, conv_all, dy_a, dy_b, dy_c, dproj, l)
        (dproj,), got = _attn_bwd(proj, d_oa, tot, after, before, dproj, plan.job("attn_bwd", l))
        plan.done("attn_bwd", l, got)
        (dlb, dproj), got = _hgrn_bwd(proj, lbs[l], states, d_ob, hg_consts, dproj, plan.job("hgrn_bwd", l))
        plan.done("hgrn_bwd", l, got)
        plan.took(("in", l), _grad_w_in(h, dproj))
        (dx, dshift, dscale), got = _proj_bwd(dproj, plan.w_in(l), xin, mods[l], dx_res, plan.job("proj_bwd", l))
        plan.done("proj_bwd", l, got)
        small[l] = (dshift, dscale, dgate, dnorm_w, dlb, dln_g, dln_b, dconv_w)
    return loss_blk, dx, small
```

```python
import functools
import math

import numpy as np
import jax
import jax.numpy as jnp
from jax import lax
from jax.experimental import pallas as pl
from jax.experimental.pallas import tpu as pltpu

F32 = jnp.float32
BF = jnp.bfloat16
MESH = pl.DeviceIdType.MESH

DEPTH = 2
D_MODEL = 1024
WIDTH = 512
IN_COLS = 12 * WIDTH + 3 * D_MODEL
N_CHIPS = 4
N_DEV = 8
SB_BLOCK = 128
SB_HEAD_DIM = 64
HG_CHUNK = 128
HG_DIM = 128
LN_EPS = 1e-5
RMS_EPS = 1e-6
ALPHA = (2.0 * DEPTH) ** 0.25
ADAM_LR, ADAM_B1, ADAM_B2, ADAM_EPS, ADAM_WD, ADAM_STEP = 0.001, 0.9, 0.999, 1e-08, 0.01, 10
VMEM_LIMIT = 56 << 20


def _params(**kw):
    return pltpu.CompilerParams(vmem_limit_bytes=VMEM_LIMIT, **kw)


def _dot(a, b):
    return jnp.dot(a, b, preferred_element_type=F32)


def _dot_nt(a, b):
    return lax.dot_general(a, b, (((1,), (1,)), ((), ())), preferred_element_type=F32)


def _dot_tn(a, b):
    return lax.dot_general(a, b, (((0,), (0,)), ((), ())), preferred_element_type=F32)


def _sigmoid(x):
    return 1.0 / (1.0 + jnp.exp(-x))


def _silu(x):
    return x * _sigmoid(x)


def _standardize(x):
    mu = jnp.mean(x, axis=-1, keepdims=True)
    xc = x - mu
    var = jnp.mean(xc * xc, axis=-1, keepdims=True)
    rstd = lax.rsqrt(var + LN_EPS)
    return xc * rstd, rstd


def _standardize_bwd(dxs, xs, rstd):
    return rstd * (dxs - jnp.mean(dxs, axis=-1, keepdims=True) - xs * jnp.mean(dxs * xs, axis=-1, keepdims=True))


class _Job:
    def __init__(self, ins, outs, sems, make, alias=None):
        self.ins, self.outs, self.sems, self.make = list(ins), list(outs), list(sems), make
        self.alias = dict(alias or {})


def _join_jobs(jobs):
    jobs = [j for j in jobs if j is not None]
    if len(jobs) <= 1:
        return jobs[0] if jobs else None

    def make(ins, outs, sems):
        phases, i, o, s = [], 0, 0, 0
        for j in jobs:
            got = j.make(ins[i:i + len(j.ins)], outs[o:o + len(j.outs)], sems[s:s + len(j.sems)])
            i, o, s = i + len(j.ins), o + len(j.outs), s + len(j.sems)
            for n, phase in enumerate(got):
                if n == len(phases):
                    phases.append([])
                phases[n] += phase
        return phases

    alias, i, o = {}, 0, 0
    for j in jobs:
        alias.update({i + a: o + b for a, b in j.alias.items()})
        i, o = i + len(j.ins), o + len(j.outs)
    return _Job(sum([j.ins for j in jobs], []), sum([j.outs for j in jobs], []), sum([j.sems for j in jobs], []), make, alias)


def _flip(v, bit):
    return 1 - v if bit else v


def _halves(a, front=0):
    shape = a.shape
    lead = math.prod(shape[front:-2])
    return a.reshape(shape[:front] + (lead, 2, shape[-2] // 2, shape[-1]))


def _dma_sems(*shapes):
    return [pltpu.SemaphoreType.DMA(s) for s in shapes]


def _same(arrays):
    return [jax.ShapeDtypeStruct(a.shape, a.dtype) for a in arrays]


def _gather_job(slabs, window=None):
    n = len(slabs)
    cols = slice(None) if window is None else pl.ds(*window)

    def make(ins, outs, sems):
        send1, recv1, send2, recv2, send3, recv3 = sems
        mx, my, mc = lax.axis_index("x"), lax.axis_index("y"), lax.axis_index("c")
        across_x, across_y, diagonal = (1 - mx, my), (mx, 1 - my), (1 - mx, 1 - my)
        src = (mx + mc * (1 - 2 * mx), 1 - my + mc * (2 * my - 1))
        dst = (1 - mx + mc * (2 * mx - 1), my + mc * (1 - 2 * my))
        half = lambda a, chip: outs[a].at[2 * chip[0] + chip[1], :, mc, :, cols]
        own, relay, pass_on = [], [], []
        for a in range(n):
            for k, peer in enumerate((across_x, across_y)):
                own.append(pltpu.make_async_remote_copy(
                    src_ref=half(a, (mx, my)), dst_ref=half(a, (mx, my)), send_sem=send1.at[a, k], recv_sem=recv1.at[a, k],
                    device_id=(*peer, mc), device_id_type=MESH))
            relay.append(pltpu.make_async_remote_copy(
                src_ref=half(a, src), dst_ref=half(a, src), send_sem=send2.at[a], recv_sem=recv2.at[a],
                device_id=(*dst, mc), device_id_type=MESH))
            for k, chip in enumerate((across_x, across_y, diagonal)):
                pass_on.append(pltpu.make_async_remote_copy(
                    src_ref=half(a, chip), dst_ref=half(a, chip), send_sem=send3.at[a, k], recv_sem=recv3.at[a, k],
                    device_id=(mx, my, 1 - mc), device_id_type=MESH))
        return [own, relay, pass_on]

    return _Job(slabs, _same(slabs), _dma_sems((n, 2), (n, 2), (n,), (n,), (n, 3), (n, 3)), make, {a: a for a in range(n)})


def _to_sibling_job(grads):
    n = len(grads)

    def make(ins, outs, sems):
        send_sems, recv_sems = sems
        mx, my, mc = lax.axis_index("x"), lax.axis_index("y"), lax.axis_index("c")
        return [[pltpu.make_async_remote_copy(
            src_ref=ins[a].at[:, :, 1 - mc], dst_ref=outs[a], send_sem=send_sems.at[a], recv_sem=recv_sems.at[a],
            device_id=(mx, my, 1 - mc), device_id_type=MESH) for a in range(n)]]

    outs = [jax.ShapeDtypeStruct(g.shape[:2] + g.shape[3:], g.dtype) for g in grads]
    return _Job(grads, outs, _dma_sems((n,), (n,)), make)


def _scatter_job(sums):
    n = len(sums)

    def make(ins, outs, sems):
        send_sems, recv_sems = sems
        mx, my, mc = lax.axis_index("x"), lax.axis_index("y"), lax.axis_index("c")
        copies = []
        for a in range(n):
            for k in range(1, N_CHIPS):
                px, py = _flip(mx, k & 2), _flip(my, k & 1)
                copies.append(pltpu.make_async_remote_copy(
                    src_ref=ins[a].at[2 * px + py], dst_ref=outs[a].at[k - 1], send_sem=send_sems.at[a, k - 1],
                    recv_sem=recv_sems.at[a, k - 1], device_id=(px, py, mc), device_id_type=MESH))
        return [copies]

    pairs = (n, N_CHIPS - 1)
    return _Job(sums, [jax.ShapeDtypeStruct((N_CHIPS - 1,) + s.shape[1:], s.dtype) for s in sums], _dma_sems(pairs, pairs), make)


def _place_job(wholes):
    n = len(wholes)

    def make(ins, outs, sems):
        send_sems, recv_sems = sems
        mx, my, mc = lax.axis_index("x"), lax.axis_index("y"), lax.axis_index("c")
        copies = []
        for a in range(n):
            here = outs[a].at[:, mc]
            copies.append(pltpu.make_async_remote_copy(src_ref=here, dst_ref=here, send_sem=send_sems.at[a], recv_sem=recv_sems.at[a],
                                                       device_id=(mx, my, 1 - mc), device_id_type=MESH))
        return [copies]

    return _Job(wholes, _same(wholes), _dma_sems((n,), (n,)), make, {a: a for a in range(n)})


def _gather8_job(x):
    def make(ins, outs, sems):
        local_sem, send_sems, recv_sems = sems
        mx, my, mc = lax.axis_index("x"), lax.axis_index("y"), lax.axis_index("c")
        here = outs[0].at[4 * mx + 2 * my + mc]
        copies = [pltpu.make_async_copy(ins[0], here, local_sem.at[0])]
        for k in range(1, N_DEV):
            peer = (_flip(mx, k & 4), _flip(my, k & 2), _flip(mc, k & 1))
            copies.append(pltpu.make_async_remote_copy(src_ref=ins[0], dst_ref=here, send_sem=send_sems.at[k - 1],
                                                       recv_sem=recv_sems.at[k - 1], device_id=peer, device_id_type=MESH))
        return [copies]

    return _Job([x], [jax.ShapeDtypeStruct((N_DEV,) + x.shape, x.dtype)], _dma_sems((1,), (N_DEV - 1,), (N_DEV - 1,)), make)


def _run_phases(phases, first=0):
    for n, phase in enumerate(phases):
        if n >= first:
            for cp in phase:
                cp.start()
        for cp in phase:
            cp.wait()


def _run_job(job, name):
    k_in, k_out = len(job.ins), len(job.outs)

    def body(*refs):
        _run_phases(job.make(refs[:k_in], refs[k_in:k_in + k_out], refs[k_in + k_out:]))

    hbm = pl.BlockSpec(memory_space=pl.ANY)
    return pl.pallas_call(body, name=name, in_specs=[hbm] * k_in, out_specs=[hbm] * k_out, out_shape=job.outs,
                          scratch_shapes=job.sems, input_output_aliases=job.alias, compiler_params=_params())(*job.ins)


def _hosted(body, job, args, *, name, grid, in_specs, out_specs, out_shape, scratch_shapes=(), semantics, aliases=None):
    in_specs, out_specs, out_shape, scratch = list(in_specs), list(out_specs), list(out_shape), list(scratch_shapes)
    aliases = dict(aliases or {})
    if job is None:
        outs = pl.pallas_call(body, name=name, grid=grid, in_specs=in_specs, out_specs=out_specs, out_shape=out_shape,
                              scratch_shapes=scratch, input_output_aliases=aliases,
                              compiler_params=_params(dimension_semantics=semantics))(*args)
        return list(outs), None
    n_in, n_out, n_scr, k_in, k_out = len(in_specs), len(out_specs), len(scratch), len(job.ins), len(job.outs)

    def wrapped(*refs):
        ins, rest = refs[:n_in], refs[n_in:]
        job_ins, rest = rest[:k_in], rest[k_in:]
        outs, rest = rest[:n_out], rest[n_out:]
        job_outs, rest = rest[:k_out], rest[k_out:]
        scr, sems = rest[:n_scr], rest[n_scr:]
        ids = [pl.program_id(a) for a in range(len(grid))]
        first = functools.reduce(jnp.logical_and, [i == 0 for i in ids])
        last = functools.reduce(jnp.logical_and, [i == g - 1 for i, g in zip(ids, grid)])

        @pl.when(first)
        def _():
            for cp in job.make(job_ins, job_outs, sems)[0]:
                cp.start()

        body(*ins, *outs, *scr)

        @pl.when(last)
        def _():
            _run_phases(job.make(job_ins, job_outs, sems), first=1)

    hbm = pl.BlockSpec(memory_space=pl.ANY)
    outs = pl.pallas_call(
        wrapped, name=name, grid=grid, in_specs=in_specs + [hbm] * k_in, out_specs=out_specs + [hbm] * k_out,
        out_shape=out_shape + job.outs, scratch_shapes=scratch + job.sems,
        input_output_aliases={**aliases, **{n_in + i: n_out + o for i, o in job.alias.items()}},
        compiler_params=_params(dimension_semantics=("arbitrary",) * len(grid)))(*args, *job.ins)
    return list(outs[:n_out]), list(outs[n_out:])


def _proj_fwd(x, mod, wg, job=None, chunk=(0, 1), tm=512):
    S, D = x.shape
    tm = min(tm, S)
    tn = wg.shape[-1]
    c, n = chunk

    rows = S // tm

    def body(x_ref, mod_ref, w_ref, proj_ref, h_ref, hs):
        mine = pl.ds(pl.multiple_of(pl.program_id(1) * tm, tm), tm)

        @pl.when(pl.program_id(0) == 0)
        def _():
            xs, _ = _standardize(x_ref[...])
            h = xs * (1.0 + mod_ref[:, D:2 * D]) + mod_ref[:, 0:D]
            hb = h.astype(BF)
            hs[mine, :] = hb
            h_ref[...] = hb

        proj_ref[...] = _dot(hs[mine, :], w_ref[...])

    once = lambda j, i: jnp.where(j == 0, i, rows - 1)
    return _hosted(
        body, job, (x, mod, wg), name="proj_fwd",
        grid=(N_CHIPS, rows),
        in_specs=[pl.BlockSpec((tm, D), lambda j, i: (once(j, i), 0)),
                  pl.BlockSpec((1, 3 * D), lambda j, i: (0, 0)),
                  pl.BlockSpec((None, D, tn), lambda j, i: (j, 0, 0))],
        out_specs=[pl.BlockSpec((tm, tn), lambda j, i: (i, j * n + c)),
                   pl.BlockSpec((tm, D), lambda j, i: (once(j, i), 0))],
        out_shape=[jax.ShapeDtypeStruct((S, IN_COLS), F32), jax.ShapeDtypeStruct((S, D), BF)],
        scratch_shapes=[pltpu.VMEM((S, D), BF)],
        semantics=("arbitrary", "arbitrary"))


def _proj_cols(h, wg, chunk, proj, job=None, tm=512):
    S, D = h.shape
    tm = min(tm, S)
    tn = wg.shape[-1]
    c, n = chunk

    def body(h_ref, w_ref, prev_ref, proj_ref):
        mine = pl.ds(pl.multiple_of(pl.program_id(1) * tm, tm), tm)
        proj_ref[...] = _dot(h_ref[mine, :], w_ref[...])

    in_specs = [pl.BlockSpec((S, D), lambda j, i: (0, 0)), pl.BlockSpec((None, D, tn), lambda j, i: (j, 0, 0)),
                pl.BlockSpec(memory_space=pl.ANY)]
    return _hosted(body, job, (h, wg, proj), name="proj_cols", grid=(N_CHIPS, S // tm), in_specs=in_specs,
                   out_specs=[pl.BlockSpec((tm, tn), lambda j, i: (i, j * n + c))],
                   out_shape=[jax.ShapeDtypeStruct(proj.shape, proj.dtype)],
                   semantics=("arbitrary", "arbitrary"), aliases={2: 0})


SB_ROWS = 256
SB_KEYS = 256


def _attn_consts():
    j = np.arange(SB_KEYS)[:, None]
    s = np.arange(SB_KEYS)[None, :]
    from_here = np.concatenate([(j >= s), (j >= s)], axis=0).astype(np.float32)
    return jnp.asarray(from_here, BF), jnp.asarray((j <= s).astype(np.float32), BF)


def _hi_lo(x):
    hi = lax.bitcast_convert_type(lax.bitcast_convert_type(x, jnp.uint32) & jnp.uint32(0xFFFF0000), F32)
    return hi.astype(BF), (x - hi).astype(BF)


def _sums_r(x, t2):
    hi, lo = _hi_lo(x)
    return _dot(jnp.concatenate([hi, lo], axis=1), t2)


def _all_lanes(col, lanes):
    return jnp.broadcast_to(col, (col.shape[0], lanes))


def _attn_rows(ref, r0, rows, lanes, head0, scale=None):
    v = ref[pl.ds(r0, rows), lanes]
    if scale is not None:
        v = v * scale
    return jnp.concatenate([jnp.where(head0, v, 0.0), jnp.where(head0, 0.0, v)], axis=0).astype(BF)


SB_PAIRS_FWD = 4
SB_PAIRS_BWD = 2


def _attn_specs(S, n_pairs):
    return lambda base: pl.BlockSpec((S, n_pairs * SB_BLOCK), lambda p, base=base: (0, base // n_pairs + p))


def _attn_scores(q2n, k_ref, lanes, kj, t2, from_here_ref, masked):
    c0 = pl.multiple_of(kj * SB_KEYS, SB_KEYS)
    kb = k_ref[pl.ds(c0, SB_KEYS), lanes].astype(BF)
    zn = _dot_nt(q2n, kb)
    lsb = jnp.minimum(zn, 0.0) - jnp.log(1.0 + jnp.exp(-jnp.abs(zn)))
    valid = None
    if masked:
        valid = (lax.broadcasted_iota(jnp.int32, zn.shape, 1) + kj * SB_KEYS) < t2
        lsb = jnp.where(valid, lsb, 0.0)
    return c0, kb, zn, valid, lsb, _sums_r(lsb, from_here_ref[...])


def _attn_fwd(proj, from_here, job=None):
    S = proj.shape[0]
    TQ = SB_ROWS
    assert S % TQ == 0 and SB_KEYS == TQ
    scale = SB_HEAD_DIM ** -0.5
    n_pairs = SB_PAIRS_FWD
    pairs = range(n_pairs)
    lanes = [pl.ds(p * SB_BLOCK, SB_BLOCK) for p in pairs]

    def body(q_ref, k_ref, v_ref, from_here_ref, o_ref, tot_ref, run, acc):
        head0 = lax.broadcasted_iota(jnp.int32, (1, 2 * SB_HEAD_DIM), 1) < SB_HEAD_DIM

        def qloop(qi, _):
            r0 = pl.multiple_of(qi * TQ, TQ)
            q2n = [_attn_rows(q_ref, r0, TQ, lanes[p], head0, -scale) for p in pairs]
            trow = lax.broadcasted_iota(jnp.int32, (TQ, SB_KEYS), 0) + qi * TQ
            t2 = jnp.concatenate([trow, trow], axis=0)
            run[...] = jnp.zeros_like(run)
            acc[...] = jnp.zeros_like(acc)

            def step(kj, masked):
                got = [_attn_scores(q2n[p], k_ref, lanes[p], kj, t2, from_here_ref, masked) for p in pairs]
                for p in pairs:
                    c0, _, zn, valid, _, sums = got[p]
                    r = run[p]
                    e = sums - zn + jnp.concatenate([r, r], axis=1)
                    if masked:
                        e = jnp.where(valid, e, -jnp.inf)
                    acc[p] += _dot(jnp.exp(e).astype(BF), v_ref[pl.ds(c0, SB_KEYS), lanes[p]].astype(BF))
                    run[p] = r + _all_lanes(sums[:, 0:1], SB_BLOCK)

            step(qi, True)

            def below(n, _):
                step(qi - 1 - n, False)
                return 0

            lax.fori_loop(0, qi, below, 0)
            for p in pairs:
                o_ref[pl.ds(r0, TQ), lanes[p]] = jnp.where(head0, acc[p, 0:TQ, :], acc[p, TQ:2 * TQ, :])
                tot_ref[p, 0, pl.ds(r0, TQ), :] = run[p, 0:TQ, :]
                tot_ref[p, 1, pl.ds(r0, TQ), :] = run[p, TQ:2 * TQ, :]
            return 0

        lax.fori_loop(0, S // TQ, qloop, 0)

    col = _attn_specs(S, n_pairs)
    state = pltpu.VMEM((n_pairs, 2 * TQ, SB_BLOCK), F32)
    return _hosted(
        body, job, (proj, proj, proj, from_here), name="attn_fwd",
        grid=(WIDTH // (n_pairs * SB_BLOCK),),
        in_specs=[col(0), col(4), col(8), pl.BlockSpec(from_here.shape, lambda p: (0, 0))],
        out_specs=[col(0), pl.BlockSpec((n_pairs, 2, S, 128), lambda p: (p, 0, 0, 0))],
        out_shape=[jax.ShapeDtypeStruct((S, WIDTH), F32), jax.ShapeDtypeStruct((WIDTH // 128, 2, S, 128), F32)],
        scratch_shapes=[state, state],
        semantics=("parallel",))


def _attn_bwd(proj, d_o, tot, from_here, up_to, dproj, job=None):
    S = proj.shape[0]
    TQ = SB_ROWS
    assert S % TQ == 0 and SB_KEYS == TQ
    scale = SB_HEAD_DIM ** -0.5
    n_pairs = SB_PAIRS_BWD
    pairs = range(n_pairs)
    lanes = [pl.ds(p * SB_BLOCK, SB_BLOCK) for p in pairs]

    def body(q_ref, k_ref, v_ref, do_ref, tot_ref, from_here_ref, up_to_ref, _, dproj_ref, pre, cum, dq_acc, dk_acc, dv_acc, stage, sems):
        dq_ref, dk_ref, dv_ref = stage.at[0], stage.at[1], stage.at[2]
        head0 = lax.broadcasted_iota(jnp.int32, (1, 2 * SB_HEAD_DIM), 1) < SB_HEAD_DIM
        dk_acc[...] = jnp.zeros_like(dk_acc)
        dv_acc[...] = jnp.zeros_like(dv_acc)

        def qloop(qi, _):
            r0 = pl.multiple_of(qi * TQ, TQ)
            q2n = [_attn_rows(q_ref, r0, TQ, lanes[p], head0, -scale) for p in pairs]
            do2 = [_attn_rows(do_ref, r0, TQ, lanes[p], head0) for p in pairs]
            trow = lax.broadcasted_iota(jnp.int32, (TQ, SB_KEYS), 0) + qi * TQ
            t2 = jnp.concatenate([trow, trow], axis=0)
            for p in pairs:
                pre[p, 0:TQ, :] = tot_ref[p, 0, pl.ds(r0, TQ), :]
                pre[p, TQ:2 * TQ, :] = tot_ref[p, 1, pl.ds(r0, TQ), :]
            cum[...] = jnp.zeros_like(cum)
            dq_acc[...] = jnp.zeros_like(dq_acc)

            def step(kj, masked):
                got = [_attn_scores(q2n[p], k_ref, lanes[p], kj, t2, from_here_ref, masked) for p in pairs]
                das = [_dot_nt(do2[p], v_ref[pl.ds(got[p][0], SB_KEYS), lanes[p]].astype(BF)) for p in pairs]
                for p in pairs:
                    c0, kb, zn, valid, lsb, sums = got[p]
                    later = pre[p] - _all_lanes(sums[:, 0:1], SB_BLOCK)
                    pre[p] = later
                    e = sums - zn + jnp.concatenate([later, later], axis=1)
                    sig = jnp.exp(lsb - zn)
                    if masked:
                        e = jnp.where(valid, e, -jnp.inf)
                        sig = jnp.where(valid, sig, 0.0)
                    a = jnp.exp(e)
                    w = das[p] * a
                    upto = _dot(w.astype(BF), up_to_ref[...])
                    c = cum[p]
                    dz = w - sig * (upto + jnp.concatenate([c, c], axis=1))
                    cum[p] = c + _all_lanes(upto[:, SB_KEYS - 1:SB_KEYS], SB_BLOCK)
                    dzb = dz.astype(BF)
                    dq_acc[p] += _dot(dzb, kb)
                    dk_acc[pl.ds(c0, SB_KEYS), lanes[p]] += _dot_tn(dzb, q2n[p])
                    dv_acc[pl.ds(c0, SB_KEYS), lanes[p]] += _dot_tn(a.astype(BF), do2[p])

            def below(kj, _):
                step(kj, False)
                return 0

            lax.fori_loop(0, qi, below, 0)
            step(qi, True)
            for p in pairs:
                dq_ref[pl.ds(r0, TQ), lanes[p]] = (jnp.where(head0, dq_acc[p, 0:TQ, :], dq_acc[p, TQ:2 * TQ, :]) * scale).astype(BF)
            return 0

        lax.fori_loop(0, S // TQ, qloop, 0)
        dk_ref[...] = (-dk_acc[...]).astype(BF)
        dv_ref[...] = dv_acc[...].astype(BF)
        lane0 = pl.program_id(0) * (n_pairs * SB_BLOCK)
        _put_columns([dq_ref, dk_ref, dv_ref], dproj_ref, [WIDTH * n + lane0 for n in range(3)], sems)

    col = _attn_specs(S, n_pairs)
    whole = lambda a: pl.BlockSpec(a.shape, lambda p: (0, 0))
    hbm = pl.BlockSpec(memory_space=pl.ANY)
    state = pltpu.VMEM((n_pairs, 2 * TQ, SB_BLOCK), F32)
    grads = pltpu.VMEM((S, n_pairs * SB_BLOCK), F32)
    return _hosted(
        body, job, (proj, proj, proj, d_o, tot, from_here, up_to, dproj), name="attn_bwd",
        grid=(WIDTH // (n_pairs * SB_BLOCK),),
        in_specs=[col(0), col(4), col(8), col(0), pl.BlockSpec((n_pairs, 2, S, 128), lambda p: (p, 0, 0, 0)), whole(from_here),
                  whole(up_to), hbm],
        out_specs=[hbm],
        out_shape=[jax.ShapeDtypeStruct(dproj.shape, dproj.dtype)],
        scratch_shapes=[state, state, state, grads, grads, pltpu.VMEM((3, S, n_pairs * SB_BLOCK), BF), pltpu.SemaphoreType.DMA((3,))],
        semantics=("arbitrary",), aliases={7: 0})


HG_LEVELS = tuple(HG_CHUNK >> n for n in range(1, HG_CHUNK.bit_length()))


def _hgrn_consts():
    C = HG_CHUNK
    t = np.arange(C)[:, None]
    s = np.arange(C)[None, :]
    rows = [(s <= t), (s > t)]
    masks = [(t == s)]
    for m in HG_LEVELS:
        two = 2 * m
        mid = (t // two) * two + m
        right = (t % two) >= m
        rows.append((right & (s >= mid) & (s <= t)) | ((~right) & (s > t) & (s <= mid - 1)))
        masks.append(((t // two) == (s // two)) & right & ((s % two) < m))
    tri = np.concatenate(rows, axis=0).astype(np.float32)
    twice = lambda a: jnp.asarray(np.concatenate([a, a], axis=1), BF)
    return (twice(tri), twice(tri.T), jnp.asarray(np.stack(masks).astype(np.float32), F32))


HG_SUM_BLOCKS = 2 + len(HG_LEVELS)


def _split_rows(g):
    hi = g.astype(BF)
    return jnp.concatenate([hi, (g - hi.astype(F32)).astype(BF)], axis=0)


def _hgrn_sum_blocks(e):
    C = HG_CHUNK
    blocks = tuple(e[n * C:(n + 1) * C] for n in range(HG_SUM_BLOCKS))
    return blocks + (jnp.broadcast_to(e[C - 1:C], (HG_DIM, e.shape[1])),)


@jax.custom_vjp
def _hgrn_sums(tri, tri_t, g):
    return _hgrn_sum_blocks(_dot(tri, _split_rows(g)))


def _hgrn_sums_fwd(tri, tri_t, g):
    return _hgrn_sums(tri, tri_t, g), (tri, tri_t)


def _hgrn_sums_bwd(res, ds):
    tri, tri_t = res
    C = HG_CHUNK
    last = lax.broadcasted_iota(jnp.int32, (C, 1), 0) == C - 1
    prefix = ds[0] + jnp.where(last, jnp.sum(ds[-1], axis=0, keepdims=True), 0.0)
    d = jnp.concatenate((prefix,) + tuple(ds[1:-1]), axis=0)
    return jnp.zeros_like(tri), jnp.zeros_like(tri_t), _dot(tri_t[:, :d.shape[0]], d.astype(BF))


_hgrn_sums.defvjp(_hgrn_sums_fwd, _hgrn_sums_bwd)


def _bf_dot(a, b):
    return _dot(a.astype(BF), b.astype(BF))


def _bf_dot_nt(a, b):
    return _dot_nt(a.astype(BF), b.astype(BF))


def _bf_dot_tn(a, b):
    return _dot_tn(a.astype(BF), b.astype(BF))


@jax.custom_vjp
def _mm(a, b):
    return _bf_dot(a, b)


_mm.defvjp(lambda a, b: (_bf_dot(a, b), (a, b)), lambda r, ct: (_bf_dot_nt(ct, r[1]), _bf_dot_tn(r[0], ct)))


@jax.custom_vjp
def _mm_nt(a, b):
    return _bf_dot_nt(a, b)


_mm_nt.defvjp(lambda a, b: (_bf_dot_nt(a, b), (a, b)), lambda r, ct: (_bf_dot(ct, r[1]), _bf_dot_tn(ct, r[0])))


@jax.custom_vjp
def _mm_tn(a, b):
    return _bf_dot_tn(a, b)


_mm_tn.defvjp(lambda a, b: (_bf_dot_tn(a, b), (a, b)), lambda r, ct: (_bf_dot_nt(r[1], ct), _bf_dot(r[0], ct)))


def _hgrn_gates(tri, tri_t, qraw, fpre, lb):
    q = _silu(qraw)
    f = lb + (1.0 - lb) * _sigmoid(fpre)
    return q, 1.0 - f, _hgrn_sums(tri, tri_t, jnp.log(f))


def _hgrn_mix(masks, q, k, e, v, st):
    prefix, suffix, whole = e[0], e[1], e[-1]
    scores = masks[0] * _mm_nt(q, k)
    for n in range(len(HG_LEVELS)):
        decay = jnp.exp(e[2 + n])
        scores = scores + masks[n + 1] * _mm_nt(q * decay, k * decay)
    o = _mm_nt(q * jnp.exp(prefix), st) + _mm(scores, v)
    st_new = st * jnp.exp(whole) + _mm_tn(v, k * jnp.exp(suffix))
    return o, st_new


def _hgrn_chunk(tri, tri_t, masks, qraw, fpre, v, st, lb):
    q, k, e = _hgrn_gates(tri, tri_t, qraw, fpre, lb)
    return _hgrn_mix(masks, q, k, e, v, st)


HG_HEADS_PER_STEP = 4
HG_LANES = HG_HEADS_PER_STEP * HG_DIM


def _hgrn_specs(S, consts):
    col = lambda base: pl.BlockSpec((S, HG_LANES), lambda p, base=base: (0, base // HG_HEADS_PER_STEP + p))
    whole = [pl.BlockSpec(a.shape, lambda p, n=a.ndim: (0,) * n) for a in consts]
    return col, whole


def _hgrn_fwd(proj, lbs, consts, job=None):
    S = proj.shape[0]
    nc = S // HG_CHUNK
    heads = range(HG_HEADS_PER_STEP)

    def body(q_ref, f_ref, i_ref, lb_ref, tri_ref, trit_ref, mask_ref, o_ref, st_ref):
        tri, tri_t = tri_ref[...], trit_ref[...]
        masks = [mask_ref[n] for n in range(len(HG_LEVELS) + 1)]

        def chunk(ci, sts):
            r0 = pl.multiple_of(ci * HG_CHUNK, HG_CHUNK)
            rows = pl.ds(r0, HG_CHUNK)
            new = []
            lane = [pl.ds(hd * HG_DIM, HG_DIM) for hd in heads]
            gates = [_hgrn_gates(tri, tri_t, q_ref[rows, lane[hd]], f_ref[rows, lane[hd]], lb_ref[hd]) for hd in heads]
            for hd in heads:
                st_ref[hd, ci] = sts[hd]
                o, st_new = _hgrn_mix(masks, *gates[hd], i_ref[rows, lane[hd]], sts[hd])
                o_ref[rows, lane[hd]] = o
                new.append(st_new)
            return tuple(new)

        lax.fori_loop(0, nc, chunk, tuple(jnp.zeros((HG_DIM, HG_DIM), F32) for _ in heads))

    col, whole = _hgrn_specs(S, consts)
    return _hosted(
        body, job, (proj, proj, proj, lbs, *consts), name="hgrn_fwd",
        grid=(WIDTH // HG_LANES,),
        in_specs=[col(16), col(20), col(24), pl.BlockSpec((HG_HEADS_PER_STEP, 1, 128), lambda p: (p, 0, 0))] + whole,
        out_specs=[col(0), pl.BlockSpec((HG_HEADS_PER_STEP, nc, HG_DIM, HG_DIM), lambda p: (p, 0, 0, 0))],
        out_shape=[jax.ShapeDtypeStruct((S, WIDTH), F32), jax.ShapeDtypeStruct((WIDTH // 128, nc, HG_DIM, HG_DIM), F32)],
        semantics=("parallel",))


def _hgrn_bwd(proj, lbs, states, d_o, consts, dproj, job=None):
    S = proj.shape[0]
    nc = S // HG_CHUNK

    def body(q_ref, f_ref, i_ref, lb_ref, st_ref, do_ref, tri_ref, trit_ref, mask_ref, _, dlb_ref, dproj_ref, stage, sems):
        dq_ref, df_ref, di_ref = stage.at[0], stage.at[1], stage.at[2]
        masks = [mask_ref[n] for n in range(len(HG_LEVELS) + 1)]
        fn = functools.partial(_hgrn_chunk, tri_ref[...], trit_ref[...], masks)
        heads = range(HG_HEADS_PER_STEP)

        def chunk(n, carry):
            ci = nc - 1 - n
            r0 = pl.multiple_of(ci * HG_CHUNK, HG_CHUNK)
            rows = pl.ds(r0, HG_CHUNK)
            new = []
            lane = [pl.ds(hd * HG_DIM, HG_DIM) for hd in heads]
            pulls = [jax.vjp(fn, q_ref[rows, lane[hd]], f_ref[rows, lane[hd]], i_ref[rows, lane[hd]], st_ref[hd, ci], lb_ref[hd])[1]
                     for hd in heads]
            for hd in heads:
                d_st, dlb = carry[hd]
                lanes = lane[hd]
                dq, df, di, d_prev, dl = pulls[hd]((do_ref[rows, lanes], d_st))
                dq_ref[rows, lanes] = dq.astype(BF)
                df_ref[rows, lanes] = df.astype(BF)
                di_ref[rows, lanes] = di.astype(BF)
                new.append((d_prev, dlb + dl))
            return tuple(new)

        zero = (jnp.zeros((HG_DIM, HG_DIM), F32), jnp.zeros((1, HG_DIM), F32))
        done = lax.fori_loop(0, nc, chunk, tuple(zero for _ in heads))
        for hd in heads:
            dlb_ref[hd] = done[hd][1]
        lane0 = pl.program_id(0) * HG_LANES
        _put_columns([dq_ref, df_ref, di_ref], dproj_ref, [WIDTH * n + lane0 for n in (4, 5, 6)], sems)

    col, whole = _hgrn_specs(S, consts)
    head = pl.BlockSpec((HG_HEADS_PER_STEP, 1, 128), lambda p: (p, 0, 0))
    hbm = pl.BlockSpec(memory_space=pl.ANY)
    n_in = 6 + len(consts)
    return _hosted(
        body, job, (proj, proj, proj, lbs, states, d_o, *consts, dproj), name="hgrn_bwd",
        grid=(WIDTH // HG_LANES,),
        in_specs=[col(16), col(20), col(24), head, pl.BlockSpec((HG_HEADS_PER_STEP, nc, HG_DIM, HG_DIM), lambda p: (p, 0, 0, 0)), col(0)]
        + whole + [hbm],
        out_specs=[head, hbm],
        out_shape=[jax.ShapeDtypeStruct((WIDTH // 128, 1, 128), F32), jax.ShapeDtypeStruct(dproj.shape, dproj.dtype)],
        scratch_shapes=[pltpu.VMEM((3, S, HG_LANES), BF), pltpu.SemaphoreType.DMA((3,))],
        semantics=("arbitrary",), aliases={n_in: 1})


def _shift_down(x, n):
    rows = lax.broadcasted_iota(jnp.int32, x.shape, 0)
    return jnp.where(rows >= n, pltpu.roll(x, n, 0), 0.0)


def _shift_up(x, n):
    S = x.shape[0]
    rows = lax.broadcasted_iota(jnp.int32, x.shape, 0)
    return jnp.where(rows < S - n, pltpu.roll(x, S - n, 0), 0.0)


def _branch_fwd(proj, o_a, o_b, norm_w, conv_w, layer):
    S = proj.shape[0]

    def body(oa_ref, za_ref, ob_ref, zb_ref, nw_ref, pre_ref, post_ref, u_ref, zc_ref, cw_ref, ya_ref, yb_ref, yc_ref):
        ya_ref[...] = (oa_ref[...] * _silu(za_ref[...])).astype(BF)
        ob = ob_ref[...]
        rn = lax.rsqrt(jnp.mean(ob * ob, axis=-1, keepdims=True) + RMS_EPS)
        yb_ref[...] = (ob * rn * nw_ref[layer:layer + 1, :] * _silu(zb_ref[...])).astype(BF)
        pu = pre_ref[...] * u_ref[...]
        conv = cw_ref[2:3, :] * pu + cw_ref[1:2, :] * _shift_down(pu, 1) + cw_ref[0:1, :] * _shift_down(pu, 2)
        yc_ref[...] = (post_ref[...] * conv * _silu(zc_ref[...])).astype(BF)

    col = lambda base: pl.BlockSpec((S, 128), lambda p, base=base: (0, base + p))
    out = jax.ShapeDtypeStruct((S, WIDTH), BF)
    return pl.pallas_call(
        body, name="branch_fwd",
        grid=(WIDTH // 128,),
        in_specs=[col(0), col(12), col(0), col(28), pl.BlockSpec(norm_w.shape, lambda p: (0, 0)),
                  col(32), col(36), col(40), col(44), pl.BlockSpec((None, None, 3, 128), lambda p: (p, layer, 0, 0))],
        out_specs=[col(0), col(0), col(0)],
        out_shape=[out, out, out],
        compiler_params=_params(dimension_semantics=("parallel",)),
    )(o_a, proj, o_b, proj, norm_w, proj, proj, proj, proj, conv_w)


def _put_columns(tiles, dproj_ref, firsts, sems):
    copies = [pltpu.make_async_copy(t, dproj_ref.at[:, pl.ds(pl.multiple_of(c, 128), t.shape[1])], sems.at[n])
              for n, (t, c) in enumerate(zip(tiles, firsts))]
    for cp in copies:
        cp.start()
    for cp in copies:
        cp.wait()


def _branch_bwd(proj, o_a, o_b, norm_w, conv_w, dy_a, dy_b, dy_c, dproj, layer):
    S = proj.shape[0]
    firsts = [WIDTH * n for n in (3, 7, 8, 9, 10, 11)]

    def dsilu(z):
        s = _sigmoid(z)
        return s * z, s * (1.0 + z * (1.0 - s))

    def body(oa_ref, za_ref, ob_ref, zb_ref, nw_ref, pre_ref, post_ref, u_ref, zc_ref, cw_ref, dya_ref, dyb_ref, dyc_ref, _,
             doa_ref, dob_ref, dnw_ref, dcw_ref, dproj_ref, stage, sems):
        dza_ref, dzb_ref, dpre_ref, dpost_ref, du_ref, dzc_ref = [stage.at[n] for n in range(6)]
        dya = dya_ref[...]
        sa, dsa = dsilu(za_ref[...])
        doa_ref[...] = dya * sa
        dza_ref[...] = (dya * oa_ref[...] * dsa).astype(BF)

        dyb = dyb_ref[...]
        ob = ob_ref[...]
        nw = nw_ref[layer:layer + 1, :]
        sb, dsb = dsilu(zb_ref[...])
        rn = lax.rsqrt(jnp.mean(ob * ob, axis=-1, keepdims=True) + RMS_EPS)
        on = ob * rn
        dzb_ref[...] = (dyb * on * nw * dsb).astype(BF)
        don_w = dyb * sb
        dnw_ref[...] = jnp.sum(don_w * on, axis=0, keepdims=True)
        don = don_w * nw
        dob_ref[...] = rn * (don - on * jnp.mean(don * on, axis=-1, keepdims=True))

        dyc = dyc_ref[...]
        pre, post, u = pre_ref[...], post_ref[...], u_ref[...]
        sc, dsc = dsilu(zc_ref[...])
        pu = pre * u
        pu1, pu2 = _shift_down(pu, 1), _shift_down(pu, 2)
        conv = cw_ref[2:3, :] * pu + cw_ref[1:2, :] * pu1 + cw_ref[0:1, :] * pu2
        dzc_ref[...] = (dyc * post * conv * dsc).astype(BF)
        dpost_ref[...] = (dyc * conv * sc).astype(BF)
        dconv = dyc * post * sc
        dcw_ref[0:1, :] = jnp.sum(dconv * pu2, axis=0, keepdims=True)
        dcw_ref[1:2, :] = jnp.sum(dconv * pu1, axis=0, keepdims=True)
        dcw_ref[2:3, :] = jnp.sum(dconv * pu, axis=0, keepdims=True)
        dpu = cw_ref[2:3, :] * dconv + cw_ref[1:2, :] * _shift_up(dconv, 1) + cw_ref[0:1, :] * _shift_up(dconv, 2)
        dpre_ref[...] = (dpu * u).astype(BF)
        du_ref[...] = (dpu * pre).astype(BF)
        lane0 = pl.program_id(0) * 128
        _put_columns([stage.at[n] for n in range(6)], dproj_ref, [c + lane0 for c in firsts], sems)

    col = lambda base: pl.BlockSpec((S, 128), lambda p, base=base: (0, base + p))
    f32 = jax.ShapeDtypeStruct((S, WIDTH), F32)
    hbm = pl.BlockSpec(memory_space=pl.ANY)
    return pl.pallas_call(
        body, name="branch_bwd",
        grid=(WIDTH // 128,),
        in_specs=[col(0), col(12), col(0), col(28), pl.BlockSpec(norm_w.shape, lambda p: (0, 0)),
                  col(32), col(36), col(40), col(44), pl.BlockSpec((None, None, 3, 128), lambda p: (p, layer, 0, 0)),
                  col(0), col(0), col(0), hbm],
        out_specs=[col(0), col(0), pl.BlockSpec((None, 1, 128), lambda p: (p, 0, 0)), pl.BlockSpec((None, 3, 128), lambda p: (p, 0, 0)), hbm],
        out_shape=[f32, f32, jax.ShapeDtypeStruct((WIDTH // 128, 1, 128), F32), jax.ShapeDtypeStruct((WIDTH // 128, 3, 128), F32),
                   jax.ShapeDtypeStruct(dproj.shape, dproj.dtype)],
        scratch_shapes=[pltpu.VMEM((6, S, 128), BF), pltpu.SemaphoreType.DMA((6,))],
        input_output_aliases={13: 4},
        compiler_params=_params(dimension_semantics=("arbitrary",)),
    )(o_a, proj, o_b, proj, norm_w, proj, proj, proj, proj, conv_w, dy_a, dy_b, dy_c, dproj)


def _branch_proj(y_refs, wb_ref):
    return [_dot(y_refs[i][...], wb_ref[i]) for i in range(3)]


def _merge_fwd(x, mod, proj, ys, wb, wo, ln_g, ln_b, layer, job=None, tm=512):
    S, D = x.shape
    tm = min(tm, S)

    def body(x_ref, mod_ref, ga_ref, gb_ref, gc_ref, ya_ref, yb_ref, yc_ref, wb_ref, wo_ref, g_ref, b_ref, xo_ref, mg_ref, y_ref):
        ps = _branch_proj((ya_ref, yb_ref, yc_ref), wb_ref)
        merged = _sigmoid(ga_ref[...]) * ps[0] + _sigmoid(gb_ref[...]) * ps[1] + _sigmoid(gc_ref[...]) * ps[2]
        mb = merged.astype(BF)
        mg_ref[...] = mb
        y = _dot(mb, wo_ref[...].reshape(D, D))
        y_ref[...] = y
        r = ALPHA * x_ref[...] + (1.0 + mod_ref[:, 2 * D:3 * D]) * y
        xn, _ = _standardize(r)
        xo_ref[...] = xn * g_ref[layer:layer + 1, :] + b_ref[layer:layer + 1, :]

    row = lambda w, c=0: pl.BlockSpec((tm, w), lambda i, c=c: (i, c))
    whole = lambda a: pl.BlockSpec(a.shape, lambda i, n=a.ndim: (0,) * n)
    return _hosted(
        body, job, (x, mod, proj, proj, proj, *ys, wb, wo, ln_g, ln_b), name="merge_fwd",
        grid=(S // tm,),
        in_specs=[row(D), whole(mod), row(D, 6), row(D, 7), row(D, 8), row(WIDTH), row(WIDTH), row(WIDTH), whole(wb),
                  whole(wo), whole(ln_g), whole(ln_b)],
        out_specs=[row(D), row(D), row(D)],
        out_shape=[jax.ShapeDtypeStruct((S, D), F32), jax.ShapeDtypeStruct((S, D), BF), jax.ShapeDtypeStruct((S, D), F32)],
        semantics=("parallel",))


def _merge_bwd(dxo, x, y, merged, mod, proj, ys, wb, wo, ln_g, layer, tm=256):
    S, D = x.shape
    tm = min(tm, S)
    steps = S // tm
    quarter = D // N_CHIPS

    def body(dxo_ref, x_ref, y_ref, mg_ref, mod_ref, ga_ref, gb_ref, gc_ref, ya_ref, yb_ref, yc_ref, wb_ref, wo_ref, g_ref,
             dxr_ref, dg_ref, dya_ref, dyb_ref, dyc_ref, dlg_ref, dlb_ref, dgt_ref, gbr_ref, gout_ref, acc_br, acc_out):
        @pl.when(pl.program_id(0) == 0)
        def _():
            dlg_ref[...] = jnp.zeros_like(dlg_ref)
            dlb_ref[...] = jnp.zeros_like(dlb_ref)
            dgt_ref[...] = jnp.zeros_like(dgt_ref)
            acc_br[...] = jnp.zeros_like(acc_br)
            acc_out[...] = jnp.zeros_like(acc_out)

        gate1 = 1.0 + mod_ref[:, 2 * D:3 * D]
        yv = y_ref[...]
        xn, rstd = _standardize(ALPHA * x_ref[...] + gate1 * yv)
        dxo = dxo_ref[...]
        dlg_ref[...] += jnp.sum(dxo * xn, axis=0, keepdims=True)
        dlb_ref[...] += jnp.sum(dxo, axis=0, keepdims=True)
        dr = _standardize_bwd(dxo * g_ref[layer:layer + 1, :], xn, rstd)
        dxr_ref[...] = ALPHA * dr
        dgt_ref[...] += jnp.sum(dr * yv, axis=0, keepdims=True)
        dyb = (gate1 * dr).astype(BF)
        acc_out[...] += _dot_tn(mg_ref[...], dyb)
        dmerged = _dot_nt(dyb, wo_ref[...].reshape(D, D))
        y_refs = (ya_ref, yb_ref, yc_ref)
        ps = _branch_proj(y_refs, wb_ref)
        for i, (gate_ref, out_ref) in enumerate(((ga_ref, dya_ref), (gb_ref, dyb_ref), (gc_ref, dyc_ref))):
            sg = _sigmoid(gate_ref[...])
            dg_ref[:, i * D:(i + 1) * D] = (dmerged * ps[i] * sg * (1.0 - sg)).astype(BF)
            dp = (dmerged * sg).astype(BF)
            acc_br[i] += _dot_tn(y_refs[i][...], dp)
            out_ref[...] = _dot_nt(dp, wb_ref[i])

        @pl.when(pl.program_id(0) == steps - 1)
        def _():
            for j in range(N_CHIPS):
                gout_ref[j] = acc_out[j * quarter:(j + 1) * quarter, :].astype(BF)
                for i in range(3):
                    gbr_ref[j, i] = acc_br[i, :, j * quarter:(j + 1) * quarter].astype(BF)

    row = lambda w, c=0: pl.BlockSpec((tm, w), lambda i, c=c: (i, c))
    whole = lambda a: pl.BlockSpec(a.shape, lambda i, n=len(a.shape): (0,) * n)
    vec = pl.BlockSpec((1, D), lambda i: (0, 0))
    sd = jax.ShapeDtypeStruct
    g_br, g_out = sd((N_CHIPS, 3, WIDTH, quarter), BF), sd((N_CHIPS, quarter, D), BF)
    return pl.pallas_call(
        body, name="merge_bwd",
        grid=(steps,),
        in_specs=[row(D), row(D), row(D), row(D), whole(mod), row(D, 6), row(D, 7), row(D, 8), row(WIDTH), row(WIDTH), row(WIDTH),
                  whole(wb), whole(wo), whole(ln_g)],
        out_specs=[row(D), row(3 * D, IN_COLS // (3 * D) - 1), row(WIDTH), row(WIDTH), row(WIDTH), vec, vec, vec, whole(g_br), whole(g_out)],
        out_shape=[sd((S, D), F32), sd((S, IN_COLS), BF), sd((S, WIDTH), F32), sd((S, WIDTH), F32), sd((S, WIDTH), F32),
                   sd((1, D), F32), sd((1, D), F32), sd((1, D), F32), g_br, g_out],
        scratch_shapes=[pltpu.VMEM((3, WIDTH, D), F32), pltpu.VMEM((D, D), F32)],
        compiler_params=_params(dimension_semantics=("arbitrary",)),
    )(dxo, x, y, merged, mod, proj, proj, proj, *ys, wb, wo, ln_g)


def _loss_head(x, target, tm=512):
    S, D = x.shape
    tm = min(tm, S)

    def body(x_ref, t_ref, dx_ref, loss_ref):
        @pl.when(pl.program_id(0) == 0)
        def _():
            loss_ref[...] = jnp.zeros_like(loss_ref)

        err = x_ref[...] - t_ref[...]
        dx_ref[...] = err * (1.0 / D)
        loss_ref[...] += 0.5 * jnp.sum(jnp.mean(err * err, axis=-1, keepdims=True))

    row = pl.BlockSpec((tm, D), lambda i: (i, 0))
    return pl.pallas_call(
        body, name="loss_head",
        grid=(S // tm,),
        in_specs=[row, row],
        out_specs=[row, pl.BlockSpec((8, 128), lambda i: (0, 0))],
        out_shape=[jax.ShapeDtypeStruct((S, D), F32), jax.ShapeDtypeStruct((8, 128), F32)],
        compiler_params=_params(dimension_semantics=("arbitrary",)),
    )(x, target)


def _proj_bwd(dproj, wgs, x, mod, dx_res, job=None, tm=512, tk=768):
    S, D = x.shape
    tm = min(tm, S)
    n = len(wgs)
    w_args = list(wgs)
    if n == 1:
        per = wgs[0].shape[-1] // tk
        w_specs = [pl.BlockSpec((None, D, tk), lambda k, i: (k // per, 0, k % per))]
    else:
        assert wgs[0].shape[-1] == tk
        w_specs = [pl.BlockSpec((None, D, tk), lambda k, i, c=c: (jnp.minimum((k + n - 1 - c) // n, N_CHIPS - 1), 0, 0))
                   for c in range(n)]
    nk = IN_COLS // tk

    def body(dp_ref, *rest):
        w_refs, (x_ref, mod_ref, dxr_ref, dx_ref, dsh_ref, dsc_ref, acc) = rest[:n], rest[n:]
        k, i = pl.program_id(0), pl.program_id(1)
        mine = pl.ds(pl.multiple_of(i * tm, tm), tm)

        @pl.when((i == 0) & (k == 0))
        def _():
            dsh_ref[...] = jnp.zeros_like(dsh_ref)
            dsc_ref[...] = jnp.zeros_like(dsc_ref)

        @pl.when(k == 0)
        def _():
            acc[mine, :] = jnp.zeros((tm, D), F32)

        for c in range(n):
            @pl.when(k % n == c)
            def _(c=c):
                acc[mine, :] += _dot_nt(dp_ref[...], w_refs[c][...])

        @pl.when(k == nk - 1)
        def _():
            dh = acc[mine, :]
            xs, rstd = _standardize(x_ref[...])
            dsh_ref[...] += jnp.sum(dh, axis=0, keepdims=True)
            dsc_ref[...] += jnp.sum(dh * xs, axis=0, keepdims=True)
            dx_ref[...] = _standardize_bwd(dh * (1.0 + mod_ref[:, D:2 * D]), xs, rstd) + dxr_ref[...]

    row = pl.BlockSpec((tm, D), lambda k, i: (jnp.where(k == nk - 1, i, 0), 0))
    vec = pl.BlockSpec((1, D), lambda k, i: (0, 0))
    return _hosted(
        body, job, (dproj, *w_args, x, mod, dx_res), name="proj_bwd",
        grid=(nk, S // tm),
        in_specs=[pl.BlockSpec((tm, tk), lambda k, i: (i, k))] + w_specs + [row, pl.BlockSpec((1, 3 * D), lambda k, i: (0, 0)), row],
        out_specs=[row, vec, vec],
        out_shape=[jax.ShapeDtypeStruct((S, D), F32), jax.ShapeDtypeStruct((1, D), F32), jax.ShapeDtypeStruct((1, D), F32)],
        scratch_shapes=[pltpu.VMEM((S, D), F32)],
        semantics=("arbitrary", "arbitrary"))


def _grad_w_in(h, dproj):
    S, D = h.shape
    shard = IN_COLS // N_CHIPS

    def body(h_ref, d_ref, o_ref):
        o_ref[...] = _dot_tn(h_ref[...], d_ref[...]).astype(BF)

    return pl.pallas_call(
        body, name="grad_w_in",
        grid=(N_CHIPS,),
        in_specs=[pl.BlockSpec((S, D), lambda n: (0, 0)), pl.BlockSpec((S, shard), lambda n: (0, n))],
        out_specs=pl.BlockSpec((None, D, shard), lambda n: (n, 0, 0)),
        out_shape=jax.ShapeDtypeStruct((N_CHIPS, D, shard), BF),
        compiler_params=_params(dimension_semantics=("parallel",)),
    )(h, dproj)


def _all_gather8(x, name):
    R, N = x.shape

    def body(x_ref, out_ref, send_sems, recv_sems):
        mx, my, mc = lax.axis_index("x"), lax.axis_index("y"), lax.axis_index("c")
        me = 4 * mx + 2 * my + mc
        out_ref[me] = x_ref[...]
        copies = []
        for k in range(1, N_DEV):
            peer = (_flip(mx, k & 4), _flip(my, k & 2), _flip(mc, k & 1))
            cp = pltpu.make_async_remote_copy(src_ref=x_ref, dst_ref=out_ref.at[me], send_sem=send_sems.at[k - 1],
                                              recv_sem=recv_sems.at[k - 1], device_id=peer, device_id_type=MESH)
            cp.start()
            copies.append(cp)
        for cp in copies:
            cp.wait()

    return pl.pallas_call(
        body, name=name,
        in_specs=[pl.BlockSpec(memory_space=pltpu.VMEM)],
        out_specs=pl.BlockSpec(memory_space=pltpu.VMEM),
        out_shape=jax.ShapeDtypeStruct((N_DEV, R, N), F32),
        scratch_shapes=[pltpu.SemaphoreType.DMA((N_DEV - 1,)), pltpu.SemaphoreType.DMA((N_DEV - 1,))],
        compiler_params=_params(),
    )(x)


def _rows2d(a):
    return a.reshape(-1, a.shape[-1])


def _tile_rows(rows, cols, n_arrays):
    budget = (24 << 20) // (n_arrays * 2 * 4 * cols)
    if rows <= budget:
        return rows
    tm = 8
    for cand in range(8, budget + 1, 8):
        if rows % cand == 0:
            tm = cand
    return tm


SUM_ROWS = 256


def _sum_cores(g, sent, where):
    chips, lead, _, r, cols = g.shape
    tr = min(r, SUM_ROWS)

    def body(where_ref, g_ref, s_ref, out_ref):
        out_ref[...] = (g_ref[...].astype(F32) + s_ref[...].astype(F32)).astype(BF)

    spec = pl.BlockSpec((None, tr, cols), lambda i, j, where_ref: (i, j, 0))
    out = pl.pallas_call(
        body, name="sum_cores",
        grid_spec=pltpu.PrefetchScalarGridSpec(
            num_scalar_prefetch=1, grid=(chips * lead, r // tr),
            in_specs=[pl.BlockSpec((None, None, tr, cols), lambda i, j, where_ref: (i, where_ref[1], j, 0)), spec],
            out_specs=spec),
        out_shape=jax.ShapeDtypeStruct((chips * lead, r, cols), BF),
        compiler_params=_params(dimension_semantics=("parallel", "parallel")),
    )(where, g.reshape(chips * lead, 2, r, cols), sent.reshape(chips * lead, r, cols))
    return out.reshape(chips, lead, r, cols)


def _sum_chips(sums, got, where):
    _, lead, r, cols = sums.shape
    tr = min(r, SUM_ROWS)

    def body(where_ref, s_ref, g_ref, out_ref):
        out_ref[...] = ((s_ref[...].astype(F32) + g_ref[0].astype(F32)) + g_ref[1].astype(F32)) + g_ref[2].astype(F32)

    return pl.pallas_call(
        body, name="sum_chips",
        grid_spec=pltpu.PrefetchScalarGridSpec(
            num_scalar_prefetch=1, grid=(lead, r // tr),
            in_specs=[pl.BlockSpec((None, None, tr, cols), lambda i, j, where_ref: (where_ref[0], i, j, 0)),
                      pl.BlockSpec((N_CHIPS - 1, None, tr, cols), lambda i, j, where_ref: (0, i, j, 0))],
            out_specs=pl.BlockSpec((None, None, tr, cols), lambda i, j, where_ref: (i, where_ref[1], j, 0))),
        out_shape=jax.ShapeDtypeStruct((lead, 2, r, cols), F32),
        compiler_params=_params(dimension_semantics=("parallel", "parallel")),
    )(where, sums, got)


def _adamw(w, m, v, groups, name):
    shape = w.shape
    w2, m2, v2 = _rows2d(w), _rows2d(m), _rows2d(v)
    rows, cols = w2.shape
    ng = len(groups)
    n = len(groups[0])
    slab = rows // ng
    gs = [_rows2d(g) for grp in groups for g in grp]
    tm = _tile_rows(slab, cols, 7 + n)
    tiles = slab // tm
    c1 = 1.0 / (1.0 - ADAM_B1 ** ADAM_STEP)
    c2 = 1.0 / (1.0 - ADAM_B2 ** ADAM_STEP)

    def body(*refs):
        w_ref, m_ref, v_ref = refs[:3]
        g_refs = refs[3:3 + ng * n]
        go_ref, d_ref, mo_ref, vo_ref = refs[3 + ng * n:]
        which = pl.program_id(0)
        for s in range(ng):
            @pl.when(which == s)
            def _(s=s):
                g = g_refs[s * n][...]
                for r in g_refs[s * n + 1:(s + 1) * n]:
                    g = g + r[...]
                mn = ADAM_B1 * m_ref[...] + (1.0 - ADAM_B1) * g
                vn = ADAM_B2 * v_ref[...] + (1.0 - ADAM_B2) * (g * g)
                go_ref[...] = g
                mo_ref[...] = mn
                vo_ref[...] = vn
                d_ref[...] = -ADAM_LR * ((mn * c1) / (jnp.sqrt(vn * c2) + ADAM_EPS) + ADAM_WD * w_ref[...])

    spec = pl.BlockSpec((tm, cols), lambda s, i: (s * tiles + i, 0))
    g_specs = [pl.BlockSpec((tm, cols), lambda s, i, k=k: (jnp.where(s == k, i, jnp.where(s < k, 0, tiles - 1)), 0))
               for k in range(ng) for _ in range(n)]
    outs = pl.pallas_call(
        body, name=name,
        grid=(ng, tiles),
        in_specs=[spec] * 3 + g_specs,
        out_specs=[spec] * 4,
        out_shape=[jax.ShapeDtypeStruct((rows, cols), F32)] * 4,
        compiler_params=_params(dimension_semantics=("arbitrary", "arbitrary")),
    )(w2, m2, v2, *gs)
    return [o.reshape(shape) for o in outs]


def _lower_bounds(r0, r1):
    top = jnp.maximum(r0, r1)
    e0, e1 = jnp.exp(r0 - top), jnp.exp(r1 - top)
    p0, p1 = e0 / (e0 + e1), e1 / (e0 + e1)
    return p0 - p0, (p0 + p1) - p0


def _lbs_fwd(lb_raw):
    def body(lb_ref, out_ref):
        l0, l1 = _lower_bounds(lb_ref[0:1, :], lb_ref[1:2, :])
        out_ref[0:1, :] = l0
        out_ref[1:2, :] = l1

    return pl.pallas_call(body, name="lower_bounds", out_shape=jax.ShapeDtypeStruct(lb_raw.shape, F32), compiler_params=_params())(lb_raw)


def _mod_rows(c_all, w_mod, tn=768):
    _, D, cols = w_mod.shape

    def body(c_ref, w_ref, out_ref):
        out_ref[...] = _dot(c_ref[...].astype(BF), w_ref[...].astype(BF))

    return pl.pallas_call(
        body, name="mod_rows",
        grid=(DEPTH,),
        in_specs=[pl.BlockSpec((N_DEV, D), lambda l: (0, 0)), pl.BlockSpec((None, D, cols), lambda l: (l, 0, 0))],
        out_specs=pl.BlockSpec((N_DEV, cols), lambda l: (0, l)),
        out_shape=jax.ShapeDtypeStruct((N_DEV, DEPTH * cols), F32),
        compiler_params=_params(dimension_semantics=("parallel",)),
    )(c_all, w_mod)


def _grad_w_mod(c_all, dmod_cols):
    D = c_all.shape[1]
    cols = dmod_cols.shape[-1]

    def body(c_ref, d_ref, out_ref):
        out_ref[...] = _dot_tn(c_ref[...].astype(BF), d_ref[...].astype(BF))

    return pl.pallas_call(
        body, name="grad_w_mod",
        grid=(DEPTH,),
        in_specs=[pl.BlockSpec((N_DEV, D), lambda l: (0, 0)), pl.BlockSpec((None, N_DEV, cols), lambda l: (l, 0, 0))],
        out_specs=pl.BlockSpec((None, D, cols), lambda l: (l, 0, 0)),
        out_shape=jax.ShapeDtypeStruct((DEPTH, D, cols), F32),
        compiler_params=_params(dimension_semantics=("parallel",)),
    )(c_all, dmod_cols)


def _sum_devices(parts):
    _, R, N = parts.shape

    def body(p_ref, out_ref):
        acc = p_ref[0]
        for d in range(1, N_DEV):
            acc = acc + p_ref[d]
        out_ref[...] = acc

    return pl.pallas_call(body, name="sum_devices", out_shape=jax.ShapeDtypeStruct((R, N), F32), compiler_params=_params())(parts)


def _lbs_bwd(lb_raw, dl):
    def body(lb_ref, dl_ref, out_ref):
        _, pull = jax.vjp(_lower_bounds, lb_ref[0:1, :], lb_ref[1:2, :])
        d0, d1 = pull((dl_ref[0:1, :], dl_ref[1:2, :]))
        out_ref[0:1, :] = d0
        out_ref[1:2, :] = d1

    return pl.pallas_call(body, name="lower_bounds_bwd", out_shape=jax.ShapeDtypeStruct(lb_raw.shape, F32), compiler_params=_params())(lb_raw, dl)


def kernel(x, c, w_mod, b_mod, w_in, conv_w, hgrn_norm_w, lower_bounds, w_branch, w_out, ln_g, ln_b, loss_target, m_w_mod, m_b_mod, m_w_in, m_conv_w, m_hgrn_norm_w, m_lower_bounds, m_w_branch, m_w_out, m_ln_g, m_ln_b, v_w_mod, v_b_mod, v_w_in, v_conv_w, v_hgrn_norm_w, v_lower_bounds, v_w_branch, v_w_out, v_ln_g, v_ln_b):
    D = D_MODEL
    x0 = x[0]
    target = loss_target[0]
    S = x0.shape[0]
    mx, my, mc = lax.axis_index("x"), lax.axis_index("y"), lax.axis_index("c")
    chip = 2 * mx + my
    me = 2 * chip + mc
    mod_cols = 3 * D // N_CHIPS

    plan = _Plan(w_in.astype(BF), w_branch.astype(BF), w_out.astype(BF), chip, mc)
    n_conv = DEPTH * 3 * (WIDTH // N_CHIPS)
    first = jnp.concatenate([c, conv_w.reshape(1, n_conv), jnp.zeros((1, 2 * D - D - n_conv), F32)], axis=1)
    first = plan.first(first.reshape(8, 2 * D // 8)).reshape(N_DEV, 2 * D)

    c_all = first[:, :D]
    conv_all = first[:, D:D + n_conv].reshape(N_DEV, DEPTH, 3, WIDTH // N_CHIPS)[0::2]
    mod_part = _all_gather8(_mod_rows(c_all, w_mod), "gather_mod")[0::2]
    mod_part = lax.dynamic_index_in_dim(mod_part, me, axis=1, keepdims=False).reshape(N_CHIPS, DEPTH, mod_cols)
    mods = [(mod_part[:, l].reshape(1, 3 * D) + b_mod[l][None, :]) for l in range(DEPTH)]
    lbs = _lbs_fwd(lower_bounds).reshape(DEPTH, WIDTH // 128, 1, 128)
    loss_blk, dx, small = _local_step(x0, target, mods, lbs, conv_all, hgrn_norm_w, ln_g, ln_b, plan)

    n_mod, n_nw, n_lb, n_ln, n_cw = DEPTH * 3 * D, DEPTH * 128, DEPTH * WIDTH, DEPTH * D, DEPTH * 3 * WIDTH
    row = jnp.concatenate(
        [jnp.concatenate([small[l][0], small[l][1], small[l][2]], axis=1) for l in range(DEPTH)]
        + [jnp.sum(small[l][3], axis=0) for l in range(DEPTH)]
        + [small[l][4].reshape(1, WIDTH) for l in range(DEPTH)]
        + [small[l][5] for l in range(DEPTH)] + [small[l][6] for l in range(DEPTH)]
        + [jnp.transpose(small[l][7], (1, 0, 2)).reshape(1, 3 * WIDTH) for l in range(DEPTH)]
        + [loss_blk[0:1, :]], axis=1)
    n_row = row.shape[1]
    fold = -(-n_row // (8 * 128)) * 128
    rows = jnp.concatenate([row, jnp.zeros((1, 8 * fold - n_row), F32)], axis=1).reshape(8, fold)

    whole, gathered = plan.finish(rows)
    grads = {kind: [[whole[(kind, l)]] for l in range(DEPTH)] for kind in ("in", "br", "out")}

    off_nw = n_mod
    off_lb = off_nw + n_nw
    off_lng = off_lb + n_lb
    off_lnb = off_lng + n_ln
    off_cw = off_lnb + n_ln
    off_loss = off_cw + n_cw
    total = _sum_devices(gathered).reshape(1, 8 * fold)
    gathered = gathered.reshape(N_DEV, 1, 8 * fold)
    d_lower = _lbs_bwd(lower_bounds, total[0, off_lb:off_lng].reshape(DEPTH, WIDTH))
    loss = total[0, off_loss]
    d_b_mod = total[0, :n_mod].reshape(DEPTH, 3 * D)
    d_norm_w = total[0, off_nw:off_lb].reshape(DEPTH, 128)
    d_ln_g = total[0, off_lng:off_lnb].reshape(DEPTH, D)
    d_ln_b = total[0, off_lnb:off_cw].reshape(DEPTH, D)
    d_conv = total[0, off_cw:off_loss].reshape(DEPTH, 3, N_CHIPS, WIDTH // N_CHIPS)
    d_conv = lax.dynamic_index_in_dim(d_conv, chip, axis=2, keepdims=False)
    dmod_all = gathered[:, 0, :n_mod].reshape(N_DEV, DEPTH, N_CHIPS, mod_cols)
    dmod_cols = jnp.transpose(lax.dynamic_index_in_dim(dmod_all, chip, axis=2, keepdims=False), (1, 0, 2))
    d_w_mod = _grad_w_mod(c_all, dmod_cols)

    res = {}
    res["w_mod"] = _adamw(w_mod, m_w_mod, v_w_mod, [[d_w_mod]], "adamw_w_mod")
    res["b_mod"] = _adamw(b_mod, m_b_mod, v_b_mod, [[d_b_mod]], "adamw_b_mod")
    res["w_in"] = _adamw(w_in, m_w_in, v_w_in, grads["in"], "adamw_w_in")
    res["conv_w"] = _adamw(conv_w, m_conv_w, v_conv_w, [[d_conv]], "adamw_conv_w")
    res["hgrn_norm_w"] = _adamw(hgrn_norm_w, m_hgrn_norm_w, v_hgrn_norm_w, [[d_norm_w]], "adamw_norm_w")
    res["lower_bounds"] = _adamw(lower_bounds, m_lower_bounds, v_lower_bounds, [[d_lower]], "adamw_lower_bounds")
    res["w_branch"] = _adamw(w_branch, m_w_branch, v_w_branch, grads["br"], "adamw_w_branch")
    res["w_out"] = _adamw(w_out, m_w_out, v_w_out, grads["out"], "adamw_w_out")
    res["ln_g"] = _adamw(ln_g, m_ln_g, v_ln_g, [[d_ln_g]], "adamw_ln_g")
    res["ln_b"] = _adamw(ln_b, m_ln_b, v_ln_b, [[d_ln_b]], "adamw_ln_b")
    names = ["w_mod", "b_mod", "w_in", "conv_w", "hgrn_norm_w", "lower_bounds", "w_branch", "w_out", "ln_g", "ln_b"]
    return (loss, dx[None], *[res[n][0] for n in names], *[res[n][1] for n in names],
            *[res[n][2] for n in names], *[res[n][3] for n in names])


class _Plan:
    FIRST_CHUNKS = 3
    WINDOWS = ((0, 1152), (1152, 640), (1792, 512))

    def __init__(self, w_in, w_br, w_out, chip, core):
        self.local = {"in": w_in, "br": w_br, "out": w_out}
        self.chip, self.where = chip, jnp.stack([chip, core]).astype(jnp.int32)
        self.gathered, self.partial, self.grads, self.chip_sums, self.scattered, self.pending = {}, {}, {}, {}, {}, {}

    def chunks(self, l):
        return self.FIRST_CHUNKS if l == 0 else 1

    def w_in(self, l):
        return [self.gathered[("in", l, c)] for c in range(self.chunks(l))]

    def _shard(self, key):
        mine = self.local[key[0]][key[1]]
        if key[0] == "in":
            cols = mine.shape[-1] // self.chunks(key[1])
            mine = mine[:, key[2] * cols:(key[2] + 1) * cols]
        return mine

    def _slab(self, key):
        mine = _halves(self._shard(key))
        return lax.dynamic_update_slice(lax.empty((N_CHIPS,) + mine.shape, mine.dtype), mine[None], (self.chip, 0, 0, 0, 0))

    def _gather(self, keys):
        return ("gather", keys), _gather_job([self._slab(key) for key in keys])

    def _gather_window(self, key, n):
        slab = self._slab(key) if n == 0 else self.partial[key]
        return ("gather" if n == len(self.WINDOWS) - 1 else "gather_part", [key]), _gather_job([slab], self.WINDOWS[n])

    def _to_sibling(self, keys):
        return ("to_sibling", keys), _to_sibling_job([_halves(self.grads[key], 1) for key in keys])

    def _scatter(self, keys):
        return ("scatter", keys), _scatter_job([self.chip_sums[key] for key in keys])

    def job(self, stage, l, c=0):
        parts = []
        if stage == "proj_fwd":
            parts = [self._gather([("in", l, c + 1)] if c + 1 < self.chunks(l) else [("br", l), ("out", l)])]
        elif stage == "attn_fwd" and l + 1 < DEPTH:
            parts = [self._gather_window(("in", l + 1, 0), 0)]
        elif stage == "hgrn_fwd" and l + 1 < DEPTH:
            parts = [self._gather_window(("in", l + 1, 0), 1)]
        elif stage == "merge_fwd" and l + 1 < DEPTH:
            parts = [self._gather_window(("in", l + 1, 0), 2)]
        elif stage == "attn_bwd":
            parts = [self._to_sibling([("out", l), ("br", l)])] + ([self._scatter([("in", l + 1)])] if l + 1 < DEPTH else [])
        elif stage == "hgrn_bwd":
            parts = [self._scatter([("out", l), ("br", l)])]
        elif stage == "proj_bwd":
            parts = [self._to_sibling([("in", l)])] if l else [self._scatter([("in", 0)])]
        self.pending[(stage, l, c)] = [(tag, len(job.outs)) for tag, job in parts]
        return _join_jobs([job for _, job in parts])

    def done(self, stage, l, outs, c=0):
        if outs is None:
            return
        at = 0
        for (what, keys), n_outs in self.pending[(stage, l, c)]:
            mine, at = outs[at:at + n_outs], at + n_outs
            for n, key in enumerate(keys):
                if what == "gather":
                    self.gathered[key] = mine[n].reshape((N_CHIPS,) + self._shard(key).shape)
                elif what == "gather_part":
                    self.partial[key] = mine[n]
                elif what == "to_sibling":
                    self.chip_sums[key] = _sum_cores(_halves(self.grads[key], 1), mine[n], self.where)
                else:
                    self.scattered[key] = mine[n]

    def first(self, rows):
        tag, job = self._gather([("in", 0, 0)])
        self.pending[("first", 0, 0)] = [(tag, len(job.outs))]
        outs = _run_job(_join_jobs([job, _gather8_job(rows)]), "gather_first")
        self.done("first", 0, outs[:-1])
        return outs[-1]

    def took(self, key, grad):
        self.grads[key] = grad
        if key == ("in", 0):
            tag, job = self._to_sibling([key])
            self.pending[("took", 0, 0)] = [(tag, len(job.outs))]
            self.done("took", 0, _run_job(job, "to_sibling_last"))

    def finish(self, rows):
        keys = [(kind, l) for kind in ("in", "br", "out") for l in range(DEPTH)]
        halves = [_sum_chips(self.chip_sums[key], self.scattered[key], self.where) for key in keys]
        outs = _run_job(_join_jobs([_place_job(halves), _gather8_job(rows)]), "place_halves")
        return {key: w.reshape(self.grads[key].shape[1:]) for key, w in zip(keys, outs[:-1])}, outs[-1]


def _local_step(x0, target, mods, lbs, conv_all, hgrn_norm_w, ln_g, ln_b, plan):
    D = D_MODEL
    after, before = _attn_consts()
    hg_consts = _hgrn_consts()

    saved = []
    xl = x0
    for l in range(DEPTH):
        n = plan.chunks(l)
        (proj, h), got = _proj_fwd(xl, mods[l], plan.gathered[("in", l, 0)], plan.job("proj_fwd", l, 0), (0, n))
        plan.done("proj_fwd", l, got, 0)
        for c in range(1, n):
            (proj,), got = _proj_cols(h, plan.gathered[("in", l, c)], (c, n), proj, plan.job("proj_fwd", l, c))
            plan.done("proj_fwd", l, got, c)
        (o_a, tot), got = _attn_fwd(proj, after, plan.job("attn_fwd", l))
        plan.done("attn_fwd", l, got)
        (o_b, states), got = _hgrn_fwd(proj, lbs[l], hg_consts, plan.job("hgrn_fwd", l))
        plan.done("hgrn_fwd", l, got)
        ys = _branch_fwd(proj, o_a, o_b, hgrn_norm_w, conv_all, l)
        wb = jnp.concatenate(list(plan.gathered[("br", l)]), axis=-1)
        (x_next, merged, y), got = _merge_fwd(xl, mods[l], proj, ys, wb, plan.gathered[("out", l)], ln_g, ln_b, l, plan.job("merge_fwd", l))
        plan.done("merge_fwd", l, got)
        saved.append((xl, proj, h, o_a, tot, o_b, states, ys, merged, y, wb))
        xl = x_next
    dx, loss_blk = _loss_head(xl, target)

    small = [None] * DEPTH
    for l in reversed(range(DEPTH)):
        xin, proj, h, o_a, tot, o_b, states, ys, merged, y, wb = saved[l]
        dx_res, dproj, dy_a, dy_b, dy_c, dln_g, dln_b, dgate, g_br, g_out = _merge_bwd(
            dx, xin, y, merged, mods[l], proj, ys, wb, plan.gathered[("out", l)], ln_g, l)
        plan.took(("out", l), g_out)
        plan.took(("br", l), g_br)
        d_oa, d_ob, dnorm_w, dconv_w, dproj = _branch_bwd(proj, o_a, o_b, hgrn_norm_w, conv_all, dy_a, dy_b, dy_c, dproj, l)
        (dproj,), got = _attn_bwd(proj, d_oa, tot, after, before, dproj, plan.job("attn_bwd", l))
        plan.done("attn_bwd", l, got)
        (dlb, dproj), got = _hgrn_bwd(proj, lbs[l], states, d_ob, hg_consts, dproj, plan.job("hgrn_bwd", l))
        plan.done("hgrn_bwd", l, got)
        plan.took(("in", l), _grad_w_in(h, dproj))
        (dx, dshift, dscale), got = _proj_bwd(dproj, plan.w_in(l), xin, mods[l], dx_res, plan.job("proj_bwd", l))
        plan.done("proj_bwd", l, got)
        small[l] = (dshift, dscale, dgate, dnorm_w, dlb, dln_g, dln_b, dconv_w)
    return loss_blk, dx, small
```

```python
import functools
import math

import numpy as np
import jax
import jax.numpy as jnp
from jax import lax
from jax.experimental import pallas as pl
from jax.experimental.pallas import tpu as pltpu

F32 = jnp.float32
BF = jnp.bfloat16
MESH = pl.DeviceIdType.MESH

DEPTH = 2
D_MODEL = 1024
WIDTH = 512
IN_COLS = 12 * WIDTH + 3 * D_MODEL
N_CHIPS = 4
N_DEV = 8
SB_BLOCK = 128
SB_HEAD_DIM = 64
HG_CHUNK = 128
HG_DIM = 128
LN_EPS = 1e-5
RMS_EPS = 1e-6
ALPHA = (2.0 * DEPTH) ** 0.25
ADAM_LR, ADAM_B1, ADAM_B2, ADAM_EPS, ADAM_WD, ADAM_STEP = 0.001, 0.9, 0.999, 1e-08, 0.01, 10
VMEM_LIMIT = 56 << 20


def _params(**kw):
    return pltpu.CompilerParams(vmem_limit_bytes=VMEM_LIMIT, **kw)


def _dot(a, b):
    return jnp.dot(a, b, preferred_element_type=F32)


def _dot_nt(a, b):
    return lax.dot_general(a, b, (((1,), (1,)), ((), ())), preferred_element_type=F32)


def _dot_tn(a, b):
    return lax.dot_general(a, b, (((0,), (0,)), ((), ())), preferred_element_type=F32)


def _sigmoid(x):
    return 1.0 / (1.0 + jnp.exp(-x))


def _silu(x):
    return x * _sigmoid(x)


def _standardize(x):
    mu = jnp.mean(x, axis=-1, keepdims=True)
    xc = x - mu
    var = jnp.mean(xc * xc, axis=-1, keepdims=True)
    rstd = lax.rsqrt(var + LN_EPS)
    return xc * rstd, rstd


def _standardize_bwd(dxs, xs, rstd):
    return rstd * (dxs - jnp.mean(dxs, axis=-1, keepdims=True) - xs * jnp.mean(dxs * xs, axis=-1, keepdims=True))


class _Job:
    def __init__(self, ins, outs, sems, make, alias=None):
        self.ins, self.outs, self.sems, self.make = list(ins), list(outs), list(sems), make
        self.alias = dict(alias or {})


def _join_jobs(jobs):
    jobs = [j for j in jobs if j is not None]
    if len(jobs) <= 1:
        return jobs[0] if jobs else None

    def make(ins, outs, sems):
        phases, i, o, s = [], 0, 0, 0
        for j in jobs:
            got = j.make(ins[i:i + len(j.ins)], outs[o:o + len(j.outs)], sems[s:s + len(j.sems)])
            i, o, s = i + len(j.ins), o + len(j.outs), s + len(j.sems)
            for n, phase in enumerate(got):
                if n == len(phases):
                    phases.append([])
                phases[n] += phase
        return phases

    alias, i, o = {}, 0, 0
    for j in jobs:
        alias.update({i + a: o + b for a, b in j.alias.items()})
        i, o = i + len(j.ins), o + len(j.outs)
    return _Job(sum([j.ins for j in jobs], []), sum([j.outs for j in jobs], []), sum([j.sems for j in jobs], []), make, alias)


def _flip(v, bit):
    return 1 - v if bit else v


def _halves(a, front=0):
    shape = a.shape
    lead = math.prod(shape[front:-2])
    return a.reshape(shape[:front] + (lead, 2, shape[-2] // 2, shape[-1]))


def _dma_sems(*shapes):
    return [pltpu.SemaphoreType.DMA(s) for s in shapes]


def _same(arrays):
    return [jax.ShapeDtypeStruct(a.shape, a.dtype) for a in arrays]


def _gather_job(slabs, window=None):
    n = len(slabs)
    cols = slice(None) if window is None else pl.ds(*window)

    def make(ins, outs, sems):
        send1, recv1, send2, recv2 = sems
        mx, my, mc = lax.axis_index("x"), lax.axis_index("y"), lax.axis_index("c")
        fetch, pass_on = [], []
        for a in range(n):
            ours = outs[a].at[2 * mx + my, :, mc, :, cols]
            for k in range(1, N_CHIPS):
                px, py = _flip(mx, k & 2), _flip(my, k & 1)
                fetch.append(pltpu.make_async_remote_copy(
                    src_ref=ours, dst_ref=ours, send_sem=send1.at[a, k - 1], recv_sem=recv1.at[a, k - 1],
                    device_id=(px, py, mc), device_id_type=MESH))
                theirs = outs[a].at[2 * px + py, :, mc, :, cols]
                pass_on.append(pltpu.make_async_remote_copy(
                    src_ref=theirs, dst_ref=theirs, send_sem=send2.at[a, k - 1], recv_sem=recv2.at[a, k - 1],
                    device_id=(mx, my, 1 - mc), device_id_type=MESH))
        return [fetch, pass_on]

    pairs = (n, N_CHIPS - 1)
    return _Job(slabs, _same(slabs), _dma_sems(pairs, pairs, pairs, pairs), make, {a: a for a in range(n)})


def _to_sibling_job(grads):
    n = len(grads)

    def make(ins, outs, sems):
        send_sems, recv_sems = sems
        mx, my, mc = lax.axis_index("x"), lax.axis_index("y"), lax.axis_index("c")
        return [[pltpu.make_async_remote_copy(
            src_ref=ins[a].at[:, :, 1 - mc], dst_ref=outs[a], send_sem=send_sems.at[a], recv_sem=recv_sems.at[a],
            device_id=(mx, my, 1 - mc), device_id_type=MESH) for a in range(n)]]

    outs = [jax.ShapeDtypeStruct(g.shape[:2] + g.shape[3:], g.dtype) for g in grads]
    return _Job(grads, outs, _dma_sems((n,), (n,)), make)


def _scatter_job(sums):
    n = len(sums)

    def make(ins, outs, sems):
        send_sems, recv_sems = sems
        mx, my, mc = lax.axis_index("x"), lax.axis_index("y"), lax.axis_index("c")
        copies = []
        for a in range(n):
            for k in range(1, N_CHIPS):
                px, py = _flip(mx, k & 2), _flip(my, k & 1)
                copies.append(pltpu.make_async_remote_copy(
                    src_ref=ins[a].at[2 * px + py], dst_ref=outs[a].at[k - 1], send_sem=send_sems.at[a, k - 1],
                    recv_sem=recv_sems.at[a, k - 1], device_id=(px, py, mc), device_id_type=MESH))
        return [copies]

    pairs = (n, N_CHIPS - 1)
    return _Job(sums, [jax.ShapeDtypeStruct((N_CHIPS - 1,) + s.shape[1:], s.dtype) for s in sums], _dma_sems(pairs, pairs), make)


def _place_job(wholes):
    n = len(wholes)

    def make(ins, outs, sems):
        send_sems, recv_sems = sems
        mx, my, mc = lax.axis_index("x"), lax.axis_index("y"), lax.axis_index("c")
        copies = []
        for a in range(n):
            here = outs[a].at[:, mc]
            copies.append(pltpu.make_async_remote_copy(src_ref=here, dst_ref=here, send_sem=send_sems.at[a], recv_sem=recv_sems.at[a],
                                                       device_id=(mx, my, 1 - mc), device_id_type=MESH))
        return [copies]

    return _Job(wholes, _same(wholes), _dma_sems((n,), (n,)), make, {a: a for a in range(n)})


def _gather8_job(x):
    def make(ins, outs, sems):
        local_sem, send_sems, recv_sems = sems
        mx, my, mc = lax.axis_index("x"), lax.axis_index("y"), lax.axis_index("c")
        here = outs[0].at[4 * mx + 2 * my + mc]
        copies = [pltpu.make_async_copy(ins[0], here, local_sem.at[0])]
        for k in range(1, N_DEV):
            peer = (_flip(mx, k & 4), _flip(my, k & 2), _flip(mc, k & 1))
            copies.append(pltpu.make_async_remote_copy(src_ref=ins[0], dst_ref=here, send_sem=send_sems.at[k - 1],
                                                       recv_sem=recv_sems.at[k - 1], device_id=peer, device_id_type=MESH))
        return [copies]

    return _Job([x], [jax.ShapeDtypeStruct((N_DEV,) + x.shape, x.dtype)], _dma_sems((1,), (N_DEV - 1,), (N_DEV - 1,)), make)


def _run_phases(phases, first=0):
    for n, phase in enumerate(phases):
        if n >= first:
            for cp in phase:
                cp.start()
        for cp in phase:
            cp.wait()


def _run_job(job, name):
    k_in, k_out = len(job.ins), len(job.outs)

    def body(*refs):
        _run_phases(job.make(refs[:k_in], refs[k_in:k_in + k_out], refs[k_in + k_out:]))

    hbm = pl.BlockSpec(memory_space=pl.ANY)
    return pl.pallas_call(body, name=name, in_specs=[hbm] * k_in, out_specs=[hbm] * k_out, out_shape=job.outs,
                          scratch_shapes=job.sems, input_output_aliases=job.alias, compiler_params=_params())(*job.ins)


def _hosted(body, job, args, *, name, grid, in_specs, out_specs, out_shape, scratch_shapes=(), semantics, aliases=None):
    in_specs, out_specs, out_shape, scratch = list(in_specs), list(out_specs), list(out_shape), list(scratch_shapes)
    aliases = dict(aliases or {})
    if job is None:
        outs = pl.pallas_call(body, name=name, grid=grid, in_specs=in_specs, out_specs=out_specs, out_shape=out_shape,
                              scratch_shapes=scratch, input_output_aliases=aliases,
                              compiler_params=_params(dimension_semantics=semantics))(*args)
        return list(outs), None
    n_in, n_out, n_scr, k_in, k_out = len(in_specs), len(out_specs), len(scratch), len(job.ins), len(job.outs)

    def wrapped(*refs):
        ins, rest = refs[:n_in], refs[n_in:]
        job_ins, rest = rest[:k_in], rest[k_in:]
        outs, rest = rest[:n_out], rest[n_out:]
        job_outs, rest = rest[:k_out], rest[k_out:]
        scr, sems = rest[:n_scr], rest[n_scr:]
        ids = [pl.program_id(a) for a in range(len(grid))]
        first = functools.reduce(jnp.logical_and, [i == 0 for i in ids])
        last = functools.reduce(jnp.logical_and, [i == g - 1 for i, g in zip(ids, grid)])

        @pl.when(first)
        def _():
            for cp in job.make(job_ins, job_outs, sems)[0]:
                cp.start()

        body(*ins, *outs, *scr)

        @pl.when(last)
        def _():
            _run_phases(job.make(job_ins, job_outs, sems), first=1)

    hbm = pl.BlockSpec(memory_space=pl.ANY)
    outs = pl.pallas_call(
        wrapped, name=name, grid=grid, in_specs=in_specs + [hbm] * k_in, out_specs=out_specs + [hbm] * k_out,
        out_shape=out_shape + job.outs, scratch_shapes=scratch + job.sems,
        input_output_aliases={**aliases, **{n_in + i: n_out + o for i, o in job.alias.items()}},
        compiler_params=_params(dimension_semantics=("arbitrary",) * len(grid)))(*args, *job.ins)
    return list(outs[:n_out]), list(outs[n_out:])


def _proj_fwd(x, mod, wg, job=None, chunk=(0, 1), tm=512):
    S, D = x.shape
    tm = min(tm, S)
    tn = wg.shape[-1]
    c, n = chunk

    rows = S // tm

    def body(x_ref, mod_ref, w_ref, proj_ref, h_ref, hs):
        mine = pl.ds(pl.multiple_of(pl.program_id(1) * tm, tm), tm)

        @pl.when(pl.program_id(0) == 0)
        def _():
            xs, _ = _standardize(x_ref[...])
            h = xs * (1.0 + mod_ref[:, D:2 * D]) + mod_ref[:, 0:D]
            hb = h.astype(BF)
            hs[mine, :] = hb
            h_ref[...] = hb

        proj_ref[...] = _dot(hs[mine, :], w_ref[...])

    once = lambda j, i: jnp.where(j == 0, i, rows - 1)
    return _hosted(
        body, job, (x, mod, wg), name="proj_fwd",
        grid=(N_CHIPS, rows),
        in_specs=[pl.BlockSpec((tm, D), lambda j, i: (once(j, i), 0)),
                  pl.BlockSpec((1, 3 * D), lambda j, i: (0, 0)),
                  pl.BlockSpec((None, D, tn), lambda j, i: (j, 0, 0))],
        out_specs=[pl.BlockSpec((tm, tn), lambda j, i: (i, j * n + c)),
                   pl.BlockSpec((tm, D), lambda j, i: (once(j, i), 0))],
        out_shape=[jax.ShapeDtypeStruct((S, IN_COLS), F32), jax.ShapeDtypeStruct((S, D), BF)],
        scratch_shapes=[pltpu.VMEM((S, D), BF)],
        semantics=("arbitrary", "arbitrary"))


def _proj_cols(h, wg, chunk, proj, job=None, tm=512):
    S, D = h.shape
    tm = min(tm, S)
    tn = wg.shape[-1]
    c, n = chunk

    def body(h_ref, w_ref, prev_ref, proj_ref):
        mine = pl.ds(pl.multiple_of(pl.program_id(1) * tm, tm), tm)
        proj_ref[...] = _dot(h_ref[mine, :], w_ref[...])

    in_specs = [pl.BlockSpec((S, D), lambda j, i: (0, 0)), pl.BlockSpec((None, D, tn), lambda j, i: (j, 0, 0)),
                pl.BlockSpec(memory_space=pl.ANY)]
    return _hosted(body, job, (h, wg, proj), name="proj_cols", grid=(N_CHIPS, S // tm), in_specs=in_specs,
                   out_specs=[pl.BlockSpec((tm, tn), lambda j, i: (i, j * n + c))],
                   out_shape=[jax.ShapeDtypeStruct(proj.shape, proj.dtype)],
                   semantics=("arbitrary", "arbitrary"), aliases={2: 0})


SB_ROWS = 256
SB_KEYS = 256


def _attn_consts():
    j = np.arange(SB_KEYS)[:, None]
    s = np.arange(SB_KEYS)[None, :]
    from_here = np.concatenate([(j >= s), (j >= s)], axis=0).astype(np.float32)
    return jnp.asarray(from_here, BF), jnp.asarray((j <= s).astype(np.float32), BF)


def _hi_lo(x):
    hi = lax.bitcast_convert_type(lax.bitcast_convert_type(x, jnp.uint32) & jnp.uint32(0xFFFF0000), F32)
    return hi.astype(BF), (x - hi).astype(BF)


def _sums_r(x, t2):
    hi, lo = _hi_lo(x)
    return _dot(jnp.concatenate([hi, lo], axis=1), t2)


def _all_lanes(col, lanes):
    return jnp.broadcast_to(col, (col.shape[0], lanes))


def _attn_rows(ref, r0, rows, lanes, head0, scale=None):
    v = ref[pl.ds(r0, rows), lanes]
    if scale is not None:
        v = v * scale
    return jnp.concatenate([jnp.where(head0, v, 0.0), jnp.where(head0, 0.0, v)], axis=0).astype(BF)


SB_PAIRS_FWD = 4
SB_PAIRS_BWD = 2


def _attn_specs(S, n_pairs):
    return lambda base: pl.BlockSpec((S, n_pairs * SB_BLOCK), lambda p, base=base: (0, base // n_pairs + p))


def _attn_scores(q2n, k_ref, lanes, kj, t2, from_here_ref, masked):
    c0 = pl.multiple_of(kj * SB_KEYS, SB_KEYS)
    kb = k_ref[pl.ds(c0, SB_KEYS), lanes].astype(BF)
    zn = _dot_nt(q2n, kb)
    lsb = jnp.minimum(zn, 0.0) - jnp.log(1.0 + jnp.exp(-jnp.abs(zn)))
    valid = None
    if masked:
        valid = (lax.broadcasted_iota(jnp.int32, zn.shape, 1) + kj * SB_KEYS) < t2
        lsb = jnp.where(valid, lsb, 0.0)
    return c0, kb, zn, valid, lsb, _sums_r(lsb, from_here_ref[...])


def _attn_fwd(proj, from_here, job=None):
    S = proj.shape[0]
    TQ = SB_ROWS
    assert S % TQ == 0 and SB_KEYS == TQ
    scale = SB_HEAD_DIM ** -0.5
    n_pairs = SB_PAIRS_FWD
    pairs = range(n_pairs)
    lanes = [pl.ds(p * SB_BLOCK, SB_BLOCK) for p in pairs]

    def body(q_ref, k_ref, v_ref, from_here_ref, o_ref, tot_ref, run, acc):
        head0 = lax.broadcasted_iota(jnp.int32, (1, 2 * SB_HEAD_DIM), 1) < SB_HEAD_DIM

        def qloop(qi, _):
            r0 = pl.multiple_of(qi * TQ, TQ)
            q2n = [_attn_rows(q_ref, r0, TQ, lanes[p], head0, -scale) for p in pairs]
            trow = lax.broadcasted_iota(jnp.int32, (TQ, SB_KEYS), 0) + qi * TQ
            t2 = jnp.concatenate([trow, trow], axis=0)
            run[...] = jnp.zeros_like(run)
            acc[...] = jnp.zeros_like(acc)

            def step(kj, masked):
                got = [_attn_scores(q2n[p], k_ref, lanes[p], kj, t2, from_here_ref, masked) for p in pairs]
                for p in pairs:
                    c0, _, zn, valid, _, sums = got[p]
                    r = run[p]
                    e = sums - zn + jnp.concatenate([r, r], axis=1)
                    if masked:
                        e = jnp.where(valid, e, -jnp.inf)
                    acc[p] += _dot(jnp.exp(e).astype(BF), v_ref[pl.ds(c0, SB_KEYS), lanes[p]].astype(BF))
                    run[p] = r + _all_lanes(sums[:, 0:1], SB_BLOCK)

            step(qi, True)

            def below(n, _):
                step(qi - 1 - n, False)
                return 0

            lax.fori_loop(0, qi, below, 0)
            for p in pairs:
                o_ref[pl.ds(r0, TQ), lanes[p]] = jnp.where(head0, acc[p, 0:TQ, :], acc[p, TQ:2 * TQ, :])
                tot_ref[p, 0, pl.ds(r0, TQ), :] = run[p, 0:TQ, :]
                tot_ref[p, 1, pl.ds(r0, TQ), :] = run[p, TQ:2 * TQ, :]
            return 0

        lax.fori_loop(0, S // TQ, qloop, 0)

    col = _attn_specs(S, n_pairs)
    state = pltpu.VMEM((n_pairs, 2 * TQ, SB_BLOCK), F32)
    return _hosted(
        body, job, (proj, proj, proj, from_here), name="attn_fwd",
        grid=(WIDTH // (n_pairs * SB_BLOCK),),
        in_specs=[col(0), col(4), col(8), pl.BlockSpec(from_here.shape, lambda p: (0, 0))],
        out_specs=[col(0), pl.BlockSpec((n_pairs, 2, S, 128), lambda p: (p, 0, 0, 0))],
        out_shape=[jax.ShapeDtypeStruct((S, WIDTH), F32), jax.ShapeDtypeStruct((WIDTH // 128, 2, S, 128), F32)],
        scratch_shapes=[state, state],
        semantics=("parallel",))


def _attn_bwd(proj, d_o, tot, from_here, up_to, dproj, job=None):
    S = proj.shape[0]
    TQ = SB_ROWS
    assert S % TQ == 0 and SB_KEYS == TQ
    scale = SB_HEAD_DIM ** -0.5
    n_pairs = SB_PAIRS_BWD
    pairs = range(n_pairs)
    lanes = [pl.ds(p * SB_BLOCK, SB_BLOCK) for p in pairs]

    def body(q_ref, k_ref, v_ref, do_ref, tot_ref, from_here_ref, up_to_ref, _, dproj_ref, pre, cum, dq_acc, dk_acc, dv_acc, stage, sems):
        dq_ref, dk_ref, dv_ref = stage.at[0], stage.at[1], stage.at[2]
        head0 = lax.broadcasted_iota(jnp.int32, (1, 2 * SB_HEAD_DIM), 1) < SB_HEAD_DIM
        dk_acc[...] = jnp.zeros_like(dk_acc)
        dv_acc[...] = jnp.zeros_like(dv_acc)

        def qloop(qi, _):
            r0 = pl.multiple_of(qi * TQ, TQ)
            q2n = [_attn_rows(q_ref, r0, TQ, lanes[p], head0, -scale) for p in pairs]
            do2 = [_attn_rows(do_ref, r0, TQ, lanes[p], head0) for p in pairs]
            trow = lax.broadcasted_iota(jnp.int32, (TQ, SB_KEYS), 0) + qi * TQ
            t2 = jnp.concatenate([trow, trow], axis=0)
            for p in pairs:
                pre[p, 0:TQ, :] = tot_ref[p, 0, pl.ds(r0, TQ), :]
                pre[p, TQ:2 * TQ, :] = tot_ref[p, 1, pl.ds(r0, TQ), :]
            cum[...] = jnp.zeros_like(cum)
            dq_acc[...] = jnp.zeros_like(dq_acc)

            def step(kj, masked):
                got = [_attn_scores(q2n[p], k_ref, lanes[p], kj, t2, from_here_ref, masked) for p in pairs]
                das = [_dot_nt(do2[p], v_ref[pl.ds(got[p][0], SB_KEYS), lanes[p]].astype(BF)) for p in pairs]
                for p in pairs:
                    c0, kb, zn, valid, lsb, sums = got[p]
                    later = pre[p] - _all_lanes(sums[:, 0:1], SB_BLOCK)
                    pre[p] = later
                    e = sums - zn + jnp.concatenate([later, later], axis=1)
                    sig = jnp.exp(lsb - zn)
                    if masked:
                        e = jnp.where(valid, e, -jnp.inf)
                        sig = jnp.where(valid, sig, 0.0)
                    a = jnp.exp(e)
                    w = das[p] * a
                    upto = _dot(w.astype(BF), up_to_ref[...])
                    c = cum[p]
                    dz = w - sig * (upto + jnp.concatenate([c, c], axis=1))
                    cum[p] = c + _all_lanes(upto[:, SB_KEYS - 1:SB_KEYS], SB_BLOCK)
                    dzb = dz.astype(BF)
                    dq_acc[p] += _dot(dzb, kb)
                    dk_acc[pl.ds(c0, SB_KEYS), lanes[p]] += _dot_tn(dzb, q2n[p])
                    dv_acc[pl.ds(c0, SB_KEYS), lanes[p]] += _dot_tn(a.astype(BF), do2[p])

            def below(kj, _):
                step(kj, False)
                return 0

            lax.fori_loop(0, qi, below, 0)
            step(qi, True)
            for p in pairs:
                dq_ref[pl.ds(r0, TQ), lanes[p]] = (jnp.where(head0, dq_acc[p, 0:TQ, :], dq_acc[p, TQ:2 * TQ, :]) * scale).astype(BF)
            return 0

        lax.fori_loop(0, S // TQ, qloop, 0)
        dk_ref[...] = (-dk_acc[...]).astype(BF)
        dv_ref[...] = dv_acc[...].astype(BF)
        lane0 = pl.program_id(0) * (n_pairs * SB_BLOCK)
        _put_columns([dq_ref, dk_ref, dv_ref], dproj_ref, [WIDTH * n + lane0 for n in range(3)], sems)

    col = _attn_specs(S, n_pairs)
    whole = lambda a: pl.BlockSpec(a.shape, lambda p: (0, 0))
    hbm = pl.BlockSpec(memory_space=pl.ANY)
    state = pltpu.VMEM((n_pairs, 2 * TQ, SB_BLOCK), F32)
    grads = pltpu.VMEM((S, n_pairs * SB_BLOCK), F32)
    return _hosted(
        body, job, (proj, proj, proj, d_o, tot, from_here, up_to, dproj), name="attn_bwd",
        grid=(WIDTH // (n_pairs * SB_BLOCK),),
        in_specs=[col(0), col(4), col(8), col(0), pl.BlockSpec((n_pairs, 2, S, 128), lambda p: (p, 0, 0, 0)), whole(from_here),
                  whole(up_to), hbm],
        out_specs=[hbm],
        out_shape=[jax.ShapeDtypeStruct(dproj.shape, dproj.dtype)],
        scratch_shapes=[state, state, state, grads, grads, pltpu.VMEM((3, S, n_pairs * SB_BLOCK), BF), pltpu.SemaphoreType.DMA((3,))],
        semantics=("arbitrary",), aliases={7: 0})


HG_LEVELS = tuple(HG_CHUNK >> n for n in range(1, HG_CHUNK.bit_length()))


def _hgrn_consts():
    C = HG_CHUNK
    t = np.arange(C)[:, None]
    s = np.arange(C)[None, :]
    rows = [(s <= t), (s > t)]
    masks = [(t == s)]
    for m in HG_LEVELS:
        two = 2 * m
        mid = (t // two) * two + m
        right = (t % two) >= m
        rows.append((right & (s >= mid) & (s <= t)) | ((~right) & (s > t) & (s <= mid - 1)))
        masks.append(((t // two) == (s // two)) & right & ((s % two) < m))
    tri = np.concatenate(rows, axis=0).astype(np.float32)
    twice = lambda a: jnp.asarray(np.concatenate([a, a], axis=1), BF)
    return (twice(tri), twice(tri.T), jnp.asarray(np.stack(masks).astype(np.float32), F32))


HG_SUM_BLOCKS = 2 + len(HG_LEVELS)


def _split_rows(g):
    hi = g.astype(BF)
    return jnp.concatenate([hi, (g - hi.astype(F32)).astype(BF)], axis=0)


def _hgrn_sum_blocks(e):
    C = HG_CHUNK
    blocks = tuple(e[n * C:(n + 1) * C] for n in range(HG_SUM_BLOCKS))
    return blocks + (jnp.broadcast_to(e[C - 1:C], (HG_DIM, e.shape[1])),)


@jax.custom_vjp
def _hgrn_sums(tri, tri_t, g):
    return _hgrn_sum_blocks(_dot(tri, _split_rows(g)))


def _hgrn_sums_fwd(tri, tri_t, g):
    return _hgrn_sums(tri, tri_t, g), (tri, tri_t)


def _hgrn_sums_bwd(res, ds):
    tri, tri_t = res
    C = HG_CHUNK
    last = lax.broadcasted_iota(jnp.int32, (C, 1), 0) == C - 1
    prefix = ds[0] + jnp.where(last, jnp.sum(ds[-1], axis=0, keepdims=True), 0.0)
    d = jnp.concatenate((prefix,) + tuple(ds[1:-1]), axis=0)
    return jnp.zeros_like(tri), jnp.zeros_like(tri_t), _dot(tri_t[:, :d.shape[0]], d.astype(BF))


_hgrn_sums.defvjp(_hgrn_sums_fwd, _hgrn_sums_bwd)


def _bf_dot(a, b):
    return _dot(a.astype(BF), b.astype(BF))


def _bf_dot_nt(a, b):
    return _dot_nt(a.astype(BF), b.astype(BF))


def _bf_dot_tn(a, b):
    return _dot_tn(a.astype(BF), b.astype(BF))


@jax.custom_vjp
def _mm(a, b):
    return _bf_dot(a, b)


_mm.defvjp(lambda a, b: (_bf_dot(a, b), (a, b)), lambda r, ct: (_bf_dot_nt(ct, r[1]), _bf_dot_tn(r[0], ct)))


@jax.custom_vjp
def _mm_nt(a, b):
    return _bf_dot_nt(a, b)


_mm_nt.defvjp(lambda a, b: (_bf_dot_nt(a, b), (a, b)), lambda r, ct: (_bf_dot(ct, r[1]), _bf_dot_tn(ct, r[0])))


@jax.custom_vjp
def _mm_tn(a, b):
    return _bf_dot_tn(a, b)


_mm_tn.defvjp(lambda a, b: (_bf_dot_tn(a, b), (a, b)), lambda r, ct: (_bf_dot_nt(r[1], ct), _bf_dot(r[0], ct)))


def _hgrn_gates(tri, tri_t, qraw, fpre, lb):
    q = _silu(qraw)
    f = lb + (1.0 - lb) * _sigmoid(fpre)
    return q, 1.0 - f, _hgrn_sums(tri, tri_t, jnp.log(f))


def _hgrn_mix(masks, q, k, e, v, st):
    prefix, suffix, whole = e[0], e[1], e[-1]
    scores = masks[0] * _mm_nt(q, k)
    for n in range(len(HG_LEVELS)):
        decay = jnp.exp(e[2 + n])
        scores = scores + masks[n + 1] * _mm_nt(q * decay, k * decay)
    o = _mm_nt(q * jnp.exp(prefix), st) + _mm(scores, v)
    st_new = st * jnp.exp(whole) + _mm_tn(v, k * jnp.exp(suffix))
    return o, st_new


def _hgrn_chunk(tri, tri_t, masks, qraw, fpre, v, st, lb):
    q, k, e = _hgrn_gates(tri, tri_t, qraw, fpre, lb)
    return _hgrn_mix(masks, q, k, e, v, st)


HG_HEADS_PER_STEP = 4
HG_LANES = HG_HEADS_PER_STEP * HG_DIM


def _hgrn_specs(S, consts):
    col = lambda base: pl.BlockSpec((S, HG_LANES), lambda p, base=base: (0, base // HG_HEADS_PER_STEP + p))
    whole = [pl.BlockSpec(a.shape, lambda p, n=a.ndim: (0,) * n) for a in consts]
    return col, whole


def _hgrn_fwd(proj, lbs, consts, job=None):
    S = proj.shape[0]
    nc = S // HG_CHUNK
    heads = range(HG_HEADS_PER_STEP)

    def body(q_ref, f_ref, i_ref, lb_ref, tri_ref, trit_ref, mask_ref, o_ref, st_ref):
        tri, tri_t = tri_ref[...], trit_ref[...]
        masks = [mask_ref[n] for n in range(len(HG_LEVELS) + 1)]

        def chunk(ci, sts):
            r0 = pl.multiple_of(ci * HG_CHUNK, HG_CHUNK)
            rows = pl.ds(r0, HG_CHUNK)
            new = []
            lane = [pl.ds(hd * HG_DIM, HG_DIM) for hd in heads]
            gates = [_hgrn_gates(tri, tri_t, q_ref[rows, lane[hd]], f_ref[rows, lane[hd]], lb_ref[hd]) for hd in heads]
            for hd in heads:
                st_ref[hd, ci] = sts[hd]
                o, st_new = _hgrn_mix(masks, *gates[hd], i_ref[rows, lane[hd]], sts[hd])
                o_ref[rows, lane[hd]] = o
                new.append(st_new)
            return tuple(new)

        lax.fori_loop(0, nc, chunk, tuple(jnp.zeros((HG_DIM, HG_DIM), F32) for _ in heads))

    col, whole = _hgrn_specs(S, consts)
    return _hosted(
        body, job, (proj, proj, proj, lbs, *consts), name="hgrn_fwd",
        grid=(WIDTH // HG_LANES,),
        in_specs=[col(16), col(20), col(24), pl.BlockSpec((HG_HEADS_PER_STEP, 1, 128), lambda p: (p, 0, 0))] + whole,
        out_specs=[col(0), pl.BlockSpec((HG_HEADS_PER_STEP, nc, HG_DIM, HG_DIM), lambda p: (p, 0, 0, 0))],
        out_shape=[jax.ShapeDtypeStruct((S, WIDTH), F32), jax.ShapeDtypeStruct((WIDTH // 128, nc, HG_DIM, HG_DIM), F32)],
        semantics=("parallel",))


def _hgrn_bwd(proj, lbs, states, d_o, consts, dproj, job=None):
    S = proj.shape[0]
    nc = S // HG_CHUNK

    def body(q_ref, f_ref, i_ref, lb_ref, st_ref, do_ref, tri_ref, trit_ref, mask_ref, _, dlb_ref, dproj_ref, stage, sems):
        dq_ref, df_ref, di_ref = stage.at[0], stage.at[1], stage.at[2]
        masks = [mask_ref[n] for n in range(len(HG_LEVELS) + 1)]
        fn = functools.partial(_hgrn_chunk, tri_ref[...], trit_ref[...], masks)
        heads = range(HG_HEADS_PER_STEP)

        def chunk(n, carry):
            ci = nc - 1 - n
            r0 = pl.multiple_of(ci * HG_CHUNK, HG_CHUNK)
            rows = pl.ds(r0, HG_CHUNK)
            new = []
            lane = [pl.ds(hd * HG_DIM, HG_DIM) for hd in heads]
            pulls = [jax.vjp(fn, q_ref[rows, lane[hd]], f_ref[rows, lane[hd]], i_ref[rows, lane[hd]], st_ref[hd, ci], lb_ref[hd])[1]
                     for hd in heads]
            for hd in heads:
                d_st, dlb = carry[hd]
                lanes = lane[hd]
                dq, df, di, d_prev, dl = pulls[hd]((do_ref[rows, lanes], d_st))
                dq_ref[rows, lanes] = dq.astype(BF)
                df_ref[rows, lanes] = df.astype(BF)
                di_ref[rows, lanes] = di.astype(BF)
                new.append((d_prev, dlb + dl))
            return tuple(new)

        zero = (jnp.zeros((HG_DIM, HG_DIM), F32), jnp.zeros((1, HG_DIM), F32))
        done = lax.fori_loop(0, nc, chunk, tuple(zero for _ in heads))
        for hd in heads:
            dlb_ref[hd] = done[hd][1]
        lane0 = pl.program_id(0) * HG_LANES
        _put_columns([dq_ref, df_ref, di_ref], dproj_ref, [WIDTH * n + lane0 for n in (4, 5, 6)], sems)

    col, whole = _hgrn_specs(S, consts)
    head = pl.BlockSpec((HG_HEADS_PER_STEP, 1, 128), lambda p: (p, 0, 0))
    hbm = pl.BlockSpec(memory_space=pl.ANY)
    n_in = 6 + len(consts)
    return _hosted(
        body, job, (proj, proj, proj, lbs, states, d_o, *consts, dproj), name="hgrn_bwd",
        grid=(WIDTH // HG_LANES,),
        in_specs=[col(16), col(20), col(24), head, pl.BlockSpec((HG_HEADS_PER_STEP, nc, HG_DIM, HG_DIM), lambda p: (p, 0, 0, 0)), col(0)]
        + whole + [hbm],
        out_specs=[head, hbm],
        out_shape=[jax.ShapeDtypeStruct((WIDTH // 128, 1, 128), F32), jax.ShapeDtypeStruct(dproj.shape, dproj.dtype)],
        scratch_shapes=[pltpu.VMEM((3, S, HG_LANES), BF), pltpu.SemaphoreType.DMA((3,))],
        semantics=("arbitrary",), aliases={n_in: 1})


def _shift_down(x, n):
    rows = lax.broadcasted_iota(jnp.int32, x.shape, 0)
    return jnp.where(rows >= n, pltpu.roll(x, n, 0), 0.0)


def _shift_up(x, n):
    S = x.shape[0]
    rows = lax.broadcasted_iota(jnp.int32, x.shape, 0)
    return jnp.where(rows < S - n, pltpu.roll(x, S - n, 0), 0.0)


def _branch_fwd(proj, o_a, o_b, norm_w, conv_w, layer):
    S = proj.shape[0]

    def body(oa_ref, za_ref, ob_ref, zb_ref, nw_ref, pre_ref, post_ref, u_ref, zc_ref, cw_ref, ya_ref, yb_ref, yc_ref):
        ya_ref[...] = (oa_ref[...] * _silu(za_ref[...])).astype(BF)
        ob = ob_ref[...]
        rn = lax.rsqrt(jnp.mean(ob * ob, axis=-1, keepdims=True) + RMS_EPS)
        yb_ref[...] = (ob * rn * nw_ref[layer:layer + 1, :] * _silu(zb_ref[...])).astype(BF)
        pu = pre_ref[...] * u_ref[...]
        conv = cw_ref[2:3, :] * pu + cw_ref[1:2, :] * _shift_down(pu, 1) + cw_ref[0:1, :] * _shift_down(pu, 2)
        yc_ref[...] = (post_ref[...] * conv * _silu(zc_ref[...])).astype(BF)

    col = lambda base: pl.BlockSpec((S, 128), lambda p, base=base: (0, base + p))
    out = jax.ShapeDtypeStruct((S, WIDTH), BF)
    return pl.pallas_call(
        body, name="branch_fwd",
        grid=(WIDTH // 128,),
        in_specs=[col(0), col(12), col(0), col(28), pl.BlockSpec(norm_w.shape, lambda p: (0, 0)),
                  col(32), col(36), col(40), col(44), pl.BlockSpec((None, None, 3, 128), lambda p: (p, layer, 0, 0))],
        out_specs=[col(0), col(0), col(0)],
        out_shape=[out, out, out],
        compiler_params=_params(dimension_semantics=("parallel",)),
    )(o_a, proj, o_b, proj, norm_w, proj, proj, proj, proj, conv_w)


def _put_columns(tiles, dproj_ref, firsts, sems):
    copies = [pltpu.make_async_copy(t, dproj_ref.at[:, pl.ds(pl.multiple_of(c, 128), t.shape[1])], sems.at[n])
              for n, (t, c) in enumerate(zip(tiles, firsts))]
    for cp in copies:
        cp.start()
    for cp in copies:
        cp.wait()


def _branch_bwd(proj, o_a, o_b, norm_w, conv_w, dy_a, dy_b, dy_c, dproj, layer):
    S = proj.shape[0]
    firsts = [WIDTH * n for n in (3, 7, 8, 9, 10, 11)]

    def dsilu(z):
        s = _sigmoid(z)
        return s * z, s * (1.0 + z * (1.0 - s))

    def body(oa_ref, za_ref, ob_ref, zb_ref, nw_ref, pre_ref, post_ref, u_ref, zc_ref, cw_ref, dya_ref, dyb_ref, dyc_ref, _,
             doa_ref, dob_ref, dnw_ref, dcw_ref, dproj_ref, stage, sems):
        dza_ref, dzb_ref, dpre_ref, dpost_ref, du_ref, dzc_ref = [stage.at[n] for n in range(6)]
        dya = dya_ref[...]
        sa, dsa = dsilu(za_ref[...])
        doa_ref[...] = dya * sa
        dza_ref[...] = (dya * oa_ref[...] * dsa).astype(BF)

        dyb = dyb_ref[...]
        ob = ob_ref[...]
        nw = nw_ref[layer:layer + 1, :]
        sb, dsb = dsilu(zb_ref[...])
        rn = lax.rsqrt(jnp.mean(ob * ob, axis=-1, keepdims=True) + RMS_EPS)
        on = ob * rn
        dzb_ref[...] = (dyb * on * nw * dsb).astype(BF)
        don_w = dyb * sb
        dnw_ref[...] = jnp.sum(don_w * on, axis=0, keepdims=True)
        don = don_w * nw
        dob_ref[...] = rn * (don - on * jnp.mean(don * on, axis=-1, keepdims=True))

        dyc = dyc_ref[...]
        pre, post, u = pre_ref[...], post_ref[...], u_ref[...]
        sc, dsc = dsilu(zc_ref[...])
        pu = pre * u
        pu1, pu2 = _shift_down(pu, 1), _shift_down(pu, 2)
        conv = cw_ref[2:3, :] * pu + cw_ref[1:2, :] * pu1 + cw_ref[0:1, :] * pu2
        dzc_ref[...] = (dyc * post * conv * dsc).astype(BF)
        dpost_ref[...] = (dyc * conv * sc).astype(BF)
        dconv = dyc * post * sc
        dcw_ref[0:1, :] = jnp.sum(dconv * pu2, axis=0, keepdims=True)
        dcw_ref[1:2, :] = jnp.sum(dconv * pu1, axis=0, keepdims=True)
        dcw_ref[2:3, :] = jnp.sum(dconv * pu, axis=0, keepdims=True)
        dpu = cw_ref[2:3, :] * dconv + cw_ref[1:2, :] * _shift_up(dconv, 1) + cw_ref[0:1, :] * _shift_up(dconv, 2)
        dpre_ref[...] = (dpu * u).astype(BF)
        du_ref[...] = (dpu * pre).astype(BF)
        lane0 = pl.program_id(0) * 128
        _put_columns([stage.at[n] for n in range(6)], dproj_ref, [c + lane0 for c in firsts], sems)

    col = lambda base: pl.BlockSpec((S, 128), lambda p, base=base: (0, base + p))
    f32 = jax.ShapeDtypeStruct((S, WIDTH), F32)
    hbm = pl.BlockSpec(memory_space=pl.ANY)
    return pl.pallas_call(
        body, name="branch_bwd",
        grid=(WIDTH // 128,),
        in_specs=[col(0), col(12), col(0), col(28), pl.BlockSpec(norm_w.shape, lambda p: (0, 0)),
                  col(32), col(36), col(40), col(44), pl.BlockSpec((None, None, 3, 128), lambda p: (p, layer, 0, 0)),
                  col(0), col(0), col(0), hbm],
        out_specs=[col(0), col(0), pl.BlockSpec((None, 1, 128), lambda p: (p, 0, 0)), pl.BlockSpec((None, 3, 128), lambda p: (p, 0, 0)), hbm],
        out_shape=[f32, f32, jax.ShapeDtypeStruct((WIDTH // 128, 1, 128), F32), jax.ShapeDtypeStruct((WIDTH // 128, 3, 128), F32),
                   jax.ShapeDtypeStruct(dproj.shape, dproj.dtype)],
        scratch_shapes=[pltpu.VMEM((6, S, 128), BF), pltpu.SemaphoreType.DMA((6,))],
        input_output_aliases={13: 4},
        compiler_params=_params(dimension_semantics=("arbitrary",)),
    )(o_a, proj, o_b, proj, norm_w, proj, proj, proj, proj, conv_w, dy_a, dy_b, dy_c, dproj)


def _branch_proj(y_refs, wb_ref):
    return [_dot(y_refs[i][...], wb_ref[i]) for i in range(3)]


def _merge_fwd(x, mod, proj, ys, wb, wo, ln_g, ln_b, layer, job=None, tm=512):
    S, D = x.shape
    tm = min(tm, S)

    def body(x_ref, mod_ref, ga_ref, gb_ref, gc_ref, ya_ref, yb_ref, yc_ref, wb_ref, wo_ref, g_ref, b_ref, xo_ref, mg_ref, y_ref):
        ps = _branch_proj((ya_ref, yb_ref, yc_ref), wb_ref)
        merged = _sigmoid(ga_ref[...]) * ps[0] + _sigmoid(gb_ref[...]) * ps[1] + _sigmoid(gc_ref[...]) * ps[2]
        mb = merged.astype(BF)
        mg_ref[...] = mb
        y = _dot(mb, wo_ref[...].reshape(D, D))
        y_ref[...] = y
        r = ALPHA * x_ref[...] + (1.0 + mod_ref[:, 2 * D:3 * D]) * y
        xn, _ = _standardize(r)
        xo_ref[...] = xn * g_ref[layer:layer + 1, :] + b_ref[layer:layer + 1, :]

    row = lambda w, c=0: pl.BlockSpec((tm, w), lambda i, c=c: (i, c))
    whole = lambda a: pl.BlockSpec(a.shape, lambda i, n=a.ndim: (0,) * n)
    return _hosted(
        body, job, (x, mod, proj, proj, proj, *ys, wb, wo, ln_g, ln_b), name="merge_fwd",
        grid=(S // tm,),
        in_specs=[row(D), whole(mod), row(D, 6), row(D, 7), row(D, 8), row(WIDTH), row(WIDTH), row(WIDTH), whole(wb),
                  whole(wo), whole(ln_g), whole(ln_b)],
        out_specs=[row(D), row(D), row(D)],
        out_shape=[jax.ShapeDtypeStruct((S, D), F32), jax.ShapeDtypeStruct((S, D), BF), jax.ShapeDtypeStruct((S, D), F32)],
        semantics=("parallel",))


def _merge_bwd(dxo, x, y, merged, mod, proj, ys, wb, wo, ln_g, layer, tm=256):
    S, D = x.shape
    tm = min(tm, S)
    steps = S // tm
    quarter = D // N_CHIPS

    def body(dxo_ref, x_ref, y_ref, mg_ref, mod_ref, ga_ref, gb_ref, gc_ref, ya_ref, yb_ref, yc_ref, wb_ref, wo_ref, g_ref,
             dxr_ref, dg_ref, dya_ref, dyb_ref, dyc_ref, dlg_ref, dlb_ref, dgt_ref, gbr_ref, gout_ref, acc_br, acc_out):
        @pl.when(pl.program_id(0) == 0)
        def _():
            dlg_ref[...] = jnp.zeros_like(dlg_ref)
            dlb_ref[...] = jnp.zeros_like(dlb_ref)
            dgt_ref[...] = jnp.zeros_like(dgt_ref)
            acc_br[...] = jnp.zeros_like(acc_br)
            acc_out[...] = jnp.zeros_like(acc_out)

        gate1 = 1.0 + mod_ref[:, 2 * D:3 * D]
        yv = y_ref[...]
        xn, rstd = _standardize(ALPHA * x_ref[...] + gate1 * yv)
        dxo = dxo_ref[...]
        dlg_ref[...] += jnp.sum(dxo * xn, axis=0, keepdims=True)
        dlb_ref[...] += jnp.sum(dxo, axis=0, keepdims=True)
        dr = _standardize_bwd(dxo * g_ref[layer:layer + 1, :], xn, rstd)
        dxr_ref[...] = ALPHA * dr
        dgt_ref[...] += jnp.sum(dr * yv, axis=0, keepdims=True)
        dyb = (gate1 * dr).astype(BF)
        acc_out[...] += _dot_tn(mg_ref[...], dyb)
        dmerged = _dot_nt(dyb, wo_ref[...].reshape(D, D))
        y_refs = (ya_ref, yb_ref, yc_ref)
        ps = _branch_proj(y_refs, wb_ref)
        for i, (gate_ref, out_ref) in enumerate(((ga_ref, dya_ref), (gb_ref, dyb_ref), (gc_ref, dyc_ref))):
            sg = _sigmoid(gate_ref[...])
            dg_ref[:, i * D:(i + 1) * D] = (dmerged * ps[i] * sg * (1.0 - sg)).astype(BF)
            dp = (dmerged * sg).astype(BF)
            acc_br[i] += _dot_tn(y_refs[i][...], dp)
            out_ref[...] = _dot_nt(dp, wb_ref[i])

        @pl.when(pl.program_id(0) == steps - 1)
        def _():
            for j in range(N_CHIPS):
                gout_ref[j] = acc_out[j * quarter:(j + 1) * quarter, :].astype(BF)
                for i in range(3):
                    gbr_ref[j, i] = acc_br[i, :, j * quarter:(j + 1) * quarter].astype(BF)

    row = lambda w, c=0: pl.BlockSpec((tm, w), lambda i, c=c: (i, c))
    whole = lambda a: pl.BlockSpec(a.shape, lambda i, n=len(a.shape): (0,) * n)
    vec = pl.BlockSpec((1, D), lambda i: (0, 0))
    sd = jax.ShapeDtypeStruct
    g_br, g_out = sd((N_CHIPS, 3, WIDTH, quarter), BF), sd((N_CHIPS, quarter, D), BF)
    return pl.pallas_call(
        body, name="merge_bwd",
        grid=(steps,),
        in_specs=[row(D), row(D), row(D), row(D), whole(mod), row(D, 6), row(D, 7), row(D, 8), row(WIDTH), row(WIDTH), row(WIDTH),
                  whole(wb), whole(wo), whole(ln_g)],
        out_specs=[row(D), row(3 * D, IN_COLS // (3 * D) - 1), row(WIDTH), row(WIDTH), row(WIDTH), vec, vec, vec, whole(g_br), whole(g_out)],
        out_shape=[sd((S, D), F32), sd((S, IN_COLS), BF), sd((S, WIDTH), F32), sd((S, WIDTH), F32), sd((S, WIDTH), F32),
                   sd((1, D), F32), sd((1, D), F32), sd((1, D), F32), g_br, g_out],
        scratch_shapes=[pltpu.VMEM((3, WIDTH, D), F32), pltpu.VMEM((D, D), F32)],
        compiler_params=_params(dimension_semantics=("arbitrary",)),
    )(dxo, x, y, merged, mod, proj, proj, proj, *ys, wb, wo, ln_g)


def _loss_head(x, target, tm=512):
    S, D = x.shape
    tm = min(tm, S)

    def body(x_ref, t_ref, dx_ref, loss_ref):
        @pl.when(pl.program_id(0) == 0)
        def _():
            loss_ref[...] = jnp.zeros_like(loss_ref)

        err = x_ref[...] - t_ref[...]
        dx_ref[...] = err * (1.0 / D)
        loss_ref[...] += 0.5 * jnp.sum(jnp.mean(err * err, axis=-1, keepdims=True))

    row = pl.BlockSpec((tm, D), lambda i: (i, 0))
    return pl.pallas_call(
        body, name="loss_head",
        grid=(S // tm,),
        in_specs=[row, row],
        out_specs=[row, pl.BlockSpec((8, 128), lambda i: (0, 0))],
        out_shape=[jax.ShapeDtypeStruct((S, D), F32), jax.ShapeDtypeStruct((8, 128), F32)],
        compiler_params=_params(dimension_semantics=("arbitrary",)),
    )(x, target)


def _proj_bwd(dproj, wgs, x, mod, dx_res, job=None, tm=512, tk=768):
    S, D = x.shape
    tm = min(tm, S)
    n = len(wgs)
    w_args = list(wgs)
    if n == 1:
        per = wgs[0].shape[-1] // tk
        w_specs = [pl.BlockSpec((None, D, tk), lambda k, i: (k // per, 0, k % per))]
    else:
        assert wgs[0].shape[-1] == tk
        w_specs = [pl.BlockSpec((None, D, tk), lambda k, i, c=c: (jnp.minimum((k + n - 1 - c) // n, N_CHIPS - 1), 0, 0))
                   for c in range(n)]
    nk = IN_COLS // tk

    def body(dp_ref, *rest):
        w_refs, (x_ref, mod_ref, dxr_ref, dx_ref, dsh_ref, dsc_ref, acc) = rest[:n], rest[n:]
        k, i = pl.program_id(0), pl.program_id(1)
        mine = pl.ds(pl.multiple_of(i * tm, tm), tm)

        @pl.when((i == 0) & (k == 0))
        def _():
            dsh_ref[...] = jnp.zeros_like(dsh_ref)
            dsc_ref[...] = jnp.zeros_like(dsc_ref)

        @pl.when(k == 0)
        def _():
            acc[mine, :] = jnp.zeros((tm, D), F32)

        for c in range(n):
            @pl.when(k % n == c)
            def _(c=c):
                acc[mine, :] += _dot_nt(dp_ref[...], w_refs[c][...])

        @pl.when(k == nk - 1)
        def _():
            dh = acc[mine, :]
            xs, rstd = _standardize(x_ref[...])
            dsh_ref[...] += jnp.sum(dh, axis=0, keepdims=True)
            dsc_ref[...] += jnp.sum(dh * xs, axis=0, keepdims=True)
            dx_ref[...] = _standardize_bwd(dh * (1.0 + mod_ref[:, D:2 * D]), xs, rstd) + dxr_ref[...]

    row = pl.BlockSpec((tm, D), lambda k, i: (jnp.where(k == nk - 1, i, 0), 0))
    vec = pl.BlockSpec((1, D), lambda k, i: (0, 0))
    return _hosted(
        body, job, (dproj, *w_args, x, mod, dx_res), name="proj_bwd",
        grid=(nk, S // tm),
        in_specs=[pl.BlockSpec((tm, tk), lambda k, i: (i, k))] + w_specs + [row, pl.BlockSpec((1, 3 * D), lambda k, i: (0, 0)), row],
        out_specs=[row, vec, vec],
        out_shape=[jax.ShapeDtypeStruct((S, D), F32), jax.ShapeDtypeStruct((1, D), F32), jax.ShapeDtypeStruct((1, D), F32)],
        scratch_shapes=[pltpu.VMEM((S, D), F32)],
        semantics=("arbitrary", "arbitrary"))


def _grad_w_in(h, dproj):
    S, D = h.shape
    shard = IN_COLS // N_CHIPS

    def body(h_ref, d_ref, o_ref):
        o_ref[...] = _dot_tn(h_ref[...], d_ref[...]).astype(BF)

    return pl.pallas_call(
        body, name="grad_w_in",
        grid=(N_CHIPS,),
        in_specs=[pl.BlockSpec((S, D), lambda n: (0, 0)), pl.BlockSpec((S, shard), lambda n: (0, n))],
        out_specs=pl.BlockSpec((None, D, shard), lambda n: (n, 0, 0)),
        out_shape=jax.ShapeDtypeStruct((N_CHIPS, D, shard), BF),
        compiler_params=_params(dimension_semantics=("parallel",)),
    )(h, dproj)


def _all_gather8(x, name):
    R, N = x.shape

    def body(x_ref, out_ref, send_sems, recv_sems):
        mx, my, mc = lax.axis_index("x"), lax.axis_index("y"), lax.axis_index("c")
        me = 4 * mx + 2 * my + mc
        out_ref[me] = x_ref[...]
        copies = []
        for k in range(1, N_DEV):
            peer = (_flip(mx, k & 4), _flip(my, k & 2), _flip(mc, k & 1))
            cp = pltpu.make_async_remote_copy(src_ref=x_ref, dst_ref=out_ref.at[me], send_sem=send_sems.at[k - 1],
                                              recv_sem=recv_sems.at[k - 1], device_id=peer, device_id_type=MESH)
            cp.start()
            copies.append(cp)
        for cp in copies:
            cp.wait()

    return pl.pallas_call(
        body, name=name,
        in_specs=[pl.BlockSpec(memory_space=pltpu.VMEM)],
        out_specs=pl.BlockSpec(memory_space=pltpu.VMEM),
        out_shape=jax.ShapeDtypeStruct((N_DEV, R, N), F32),
        scratch_shapes=[pltpu.SemaphoreType.DMA((N_DEV - 1,)), pltpu.SemaphoreType.DMA((N_DEV - 1,))],
        compiler_params=_params(),
    )(x)


def _rows2d(a):
    return a.reshape(-1, a.shape[-1])


def _tile_rows(rows, cols, n_arrays):
    budget = (24 << 20) // (n_arrays * 2 * 4 * cols)
    if rows <= budget:
        return rows
    tm = 8
    for cand in range(8, budget + 1, 8):
        if rows % cand == 0:
            tm = cand
    return tm


SUM_ROWS = 256


def _sum_cores(g, sent, where):
    chips, lead, _, r, cols = g.shape
    tr = min(r, SUM_ROWS)

    def body(where_ref, g_ref, s_ref, out_ref):
        out_ref[...] = (g_ref[...].astype(F32) + s_ref[...].astype(F32)).astype(BF)

    spec = pl.BlockSpec((None, tr, cols), lambda i, j, where_ref: (i, j, 0))
    out = pl.pallas_call(
        body, name="sum_cores",
        grid_spec=pltpu.PrefetchScalarGridSpec(
            num_scalar_prefetch=1, grid=(chips * lead, r // tr),
            in_specs=[pl.BlockSpec((None, None, tr, cols), lambda i, j, where_ref: (i, where_ref[1], j, 0)), spec],
            out_specs=spec),
        out_shape=jax.ShapeDtypeStruct((chips * lead, r, cols), BF),
        compiler_params=_params(dimension_semantics=("parallel", "parallel")),
    )(where, g.reshape(chips * lead, 2, r, cols), sent.reshape(chips * lead, r, cols))
    return out.reshape(chips, lead, r, cols)


def _sum_chips(sums, got, where):
    _, lead, r, cols = sums.shape
    tr = min(r, SUM_ROWS)

    def body(where_ref, s_ref, g_ref, out_ref):
        out_ref[...] = ((s_ref[...].astype(F32) + g_ref[0].astype(F32)) + g_ref[1].astype(F32)) + g_ref[2].astype(F32)

    return pl.pallas_call(
        body, name="sum_chips",
        grid_spec=pltpu.PrefetchScalarGridSpec(
            num_scalar_prefetch=1, grid=(lead, r // tr),
            in_specs=[pl.BlockSpec((None, None, tr, cols), lambda i, j, where_ref: (where_ref[0], i, j, 0)),
                      pl.BlockSpec((N_CHIPS - 1, None, tr, cols), lambda i, j, where_ref: (0, i, j, 0))],
            out_specs=pl.BlockSpec((None, None, tr, cols), lambda i, j, where_ref: (i, where_ref[1], j, 0))),
        out_shape=jax.ShapeDtypeStruct((lead, 2, r, cols), F32),
        compiler_params=_params(dimension_semantics=("parallel", "parallel")),
    )(where, sums, got)


def _adamw(w, m, v, groups, name):
    shape = w.shape
    w2, m2, v2 = _rows2d(w), _rows2d(m), _rows2d(v)
    rows, cols = w2.shape
    ng = len(groups)
    n = len(groups[0])
    slab = rows // ng
    gs = [_rows2d(g) for grp in groups for g in grp]
    tm = _tile_rows(slab, cols, 7 + n)
    tiles = slab // tm
    c1 = 1.0 / (1.0 - ADAM_B1 ** ADAM_STEP)
    c2 = 1.0 / (1.0 - ADAM_B2 ** ADAM_STEP)

    def body(*refs):
        w_ref, m_ref, v_ref = refs[:3]
        g_refs = refs[3:3 + ng * n]
        go_ref, d_ref, mo_ref, vo_ref = refs[3 + ng * n:]
        which = pl.program_id(0)
        for s in range(ng):
            @pl.when(which == s)
            def _(s=s):
                g = g_refs[s * n][...]
                for r in g_refs[s * n + 1:(s + 1) * n]:
                    g = g + r[...]
                mn = ADAM_B1 * m_ref[...] + (1.0 - ADAM_B1) * g
                vn = ADAM_B2 * v_ref[...] + (1.0 - ADAM_B2) * (g * g)
                go_ref[...] = g
                mo_ref[...] = mn
                vo_ref[...] = vn
                d_ref[...] = -ADAM_LR * ((mn * c1) / (jnp.sqrt(vn * c2) + ADAM_EPS) + ADAM_WD * w_ref[...])

    spec = pl.BlockSpec((tm, cols), lambda s, i: (s * tiles + i, 0))
    g_specs = [pl.BlockSpec((tm, cols), lambda s, i, k=k: (jnp.where(s == k, i, jnp.where(s < k, 0, tiles - 1)), 0))
               for k in range(ng) for _ in range(n)]
    outs = pl.pallas_call(
        body, name=name,
        grid=(ng, tiles),
        in_specs=[spec] * 3 + g_specs,
        out_specs=[spec] * 4,
        out_shape=[jax.ShapeDtypeStruct((rows, cols), F32)] * 4,
        compiler_params=_params(dimension_semantics=("arbitrary", "arbitrary")),
    )(w2, m2, v2, *gs)
    return [o.reshape(shape) for o in outs]


def _lower_bounds(r0, r1):
    top = jnp.maximum(r0, r1)
    e0, e1 = jnp.exp(r0 - top), jnp.exp(r1 - top)
    p0, p1 = e0 / (e0 + e1), e1 / (e0 + e1)
    return p0 - p0, (p0 + p1) - p0


def _lbs_fwd(lb_raw):
    def body(lb_ref, out_ref):
        l0, l1 = _lower_bounds(lb_ref[0:1, :], lb_ref[1:2, :])
        out_ref[0:1, :] = l0
        out_ref[1:2, :] = l1

    return pl.pallas_call(body, name="lower_bounds", out_shape=jax.ShapeDtypeStruct(lb_raw.shape, F32), compiler_params=_params())(lb_raw)


def _mod_rows(c_all, w_mod, tn=768):
    _, D, cols = w_mod.shape

    def body(c_ref, w_ref, out_ref):
        out_ref[...] = _dot(c_ref[...].astype(BF), w_ref[...].astype(BF))

    return pl.pallas_call(
        body, name="mod_rows",
        grid=(DEPTH,),
        in_specs=[pl.BlockSpec((N_DEV, D), lambda l: (0, 0)), pl.BlockSpec((None, D, cols), lambda l: (l, 0, 0))],
        out_specs=pl.BlockSpec((N_DEV, cols), lambda l: (0, l)),
        out_shape=jax.ShapeDtypeStruct((N_DEV, DEPTH * cols), F32),
        compiler_params=_params(dimension_semantics=("parallel",)),
    )(c_all, w_mod)


def _grad_w_mod(c_all, dmod_cols):
    D = c_all.shape[1]
    cols = dmod_cols.shape[-1]

    def body(c_ref, d_ref, out_ref):
        out_ref[...] = _dot_tn(c_ref[...].astype(BF), d_ref[...].astype(BF))

    return pl.pallas_call(
        body, name="grad_w_mod",
        grid=(DEPTH,),
        in_specs=[pl.BlockSpec((N_DEV, D), lambda l: (0, 0)), pl.BlockSpec((None, N_DEV, cols), lambda l: (l, 0, 0))],
        out_specs=pl.BlockSpec((None, D, cols), lambda l: (l, 0, 0)),
        out_shape=jax.ShapeDtypeStruct((DEPTH, D, cols), F32),
        compiler_params=_params(dimension_semantics=("parallel",)),
    )(c_all, dmod_cols)


def _sum_devices(parts):
    _, R, N = parts.shape

    def body(p_ref, out_ref):
        acc = p_ref[0]
        for d in range(1, N_DEV):
            acc = acc + p_ref[d]
        out_ref[...] = acc

    return pl.pallas_call(body, name="sum_devices", out_shape=jax.ShapeDtypeStruct((R, N), F32), compiler_params=_params())(parts)


def _lbs_bwd(lb_raw, dl):
    def body(lb_ref, dl_ref, out_ref):
        _, pull = jax.vjp(_lower_bounds, lb_ref[0:1, :], lb_ref[1:2, :])
        d0, d1 = pull((dl_ref[0:1, :], dl_ref[1:2, :]))
        out_ref[0:1, :] = d0
        out_ref[1:2, :] = d1

    return pl.pallas_call(body, name="lower_bounds_bwd", out_shape=jax.ShapeDtypeStruct(lb_raw.shape, F32), compiler_params=_params())(lb_raw, dl)


def kernel(x, c, w_mod, b_mod, w_in, conv_w, hgrn_norm_w, lower_bounds, w_branch, w_out, ln_g, ln_b, loss_target, m_w_mod, m_b_mod, m_w_in, m_conv_w, m_hgrn_norm_w, m_lower_bounds, m_w_branch, m_w_out, m_ln_g, m_ln_b, v_w_mod, v_b_mod, v_w_in, v_conv_w, v_hgrn_norm_w, v_lower_bounds, v_w_branch, v_w_out, v_ln_g, v_ln_b):
    D = D_MODEL
    x0 = x[0]
    target = loss_target[0]
    S = x0.shape[0]
    mx, my, mc = lax.axis_index("x"), lax.axis_index("y"), lax.axis_index("c")
    chip = 2 * mx + my
    me = 2 * chip + mc
    mod_cols = 3 * D // N_CHIPS

    plan = _Plan(w_in.astype(BF), w_branch.astype(BF), w_out.astype(BF), chip, mc)
    n_conv = DEPTH * 3 * (WIDTH // N_CHIPS)
    first = jnp.concatenate([c, conv_w.reshape(1, n_conv), jnp.zeros((1, 2 * D - D - n_conv), F32)], axis=1)
    first = plan.first(first.reshape(8, 2 * D // 8)).reshape(N_DEV, 2 * D)

    c_all = first[:, :D]
    conv_all = first[:, D:D + n_conv].reshape(N_DEV, DEPTH, 3, WIDTH // N_CHIPS)[0::2]
    mod_part = _all_gather8(_mod_rows(c_all, w_mod), "gather_mod")[0::2]
    mod_part = lax.dynamic_index_in_dim(mod_part, me, axis=1, keepdims=False).reshape(N_CHIPS, DEPTH, mod_cols)
    mods = [(mod_part[:, l].reshape(1, 3 * D) + b_mod[l][None, :]) for l in range(DEPTH)]
    lbs = _lbs_fwd(lower_bounds).reshape(DEPTH, WIDTH // 128, 1, 128)
    loss_blk, dx, small = _local_step(x0, target, mods, lbs, conv_all, hgrn_norm_w, ln_g, ln_b, plan)

    n_mod, n_nw, n_lb, n_ln, n_cw = DEPTH * 3 * D, DEPTH * 128, DEPTH * WIDTH, DEPTH * D, DEPTH * 3 * WIDTH
    row = jnp.concatenate(
        [jnp.concatenate([small[l][0], small[l][1], small[l][2]], axis=1) for l in range(DEPTH)]
        + [jnp.sum(small[l][3], axis=0) for l in range(DEPTH)]
        + [small[l][4].reshape(1, WIDTH) for l in range(DEPTH)]
        + [small[l][5] for l in range(DEPTH)] + [small[l][6] for l in range(DEPTH)]
        + [jnp.transpose(small[l][7], (1, 0, 2)).reshape(1, 3 * WIDTH) for l in range(DEPTH)]
        + [loss_blk[0:1, :]], axis=1)
    n_row = row.shape[1]
    fold = -(-n_row // (8 * 128)) * 128
    rows = jnp.concatenate([row, jnp.zeros((1, 8 * fold - n_row), F32)], axis=1).reshape(8, fold)

    whole, gathered = plan.finish(rows)
    grads = {kind: [[whole[(kind, l)]] for l in range(DEPTH)] for kind in ("in", "br", "out")}

    off_nw = n_mod
    off_lb = off_nw + n_nw
    off_lng = off_lb + n_lb
    off_lnb = off_lng + n_ln
    off_cw = off_lnb + n_ln
    off_loss = off_cw + n_cw
    total = _sum_devices(gathered).reshape(1, 8 * fold)
    gathered = gathered.reshape(N_DEV, 1, 8 * fold)
    d_lower = _lbs_bwd(lower_bounds, total[0, off_lb:off_lng].reshape(DEPTH, WIDTH))
    loss = total[0, off_loss]
    d_b_mod = total[0, :n_mod].reshape(DEPTH, 3 * D)
    d_norm_w = total[0, off_nw:off_lb].reshape(DEPTH, 128)
    d_ln_g = total[0, off_lng:off_lnb].reshape(DEPTH, D)
    d_ln_b = total[0, off_lnb:off_cw].reshape(DEPTH, D)
    d_conv = total[0, off_cw:off_loss].reshape(DEPTH, 3, N_CHIPS, WIDTH // N_CHIPS)
    d_conv = lax.dynamic_index_in_dim(d_conv, chip, axis=2, keepdims=False)
    dmod_all = gathered[:, 0, :n_mod].reshape(N_DEV, DEPTH, N_CHIPS, mod_cols)
    dmod_cols = jnp.transpose(lax.dynamic_index_in_dim(dmod_all, chip, axis=2, keepdims=False), (1, 0, 2))
    d_w_mod = _grad_w_mod(c_all, dmod_cols)

    res = {}
    res["w_mod"] = _adamw(w_mod, m_w_mod, v_w_mod, [[d_w_mod]], "adamw_w_mod")
    res["b_mod"] = _adamw(b_mod, m_b_mod, v_b_mod, [[d_b_mod]], "adamw_b_mod")
    res["w_in"] = _adamw(w_in, m_w_in, v_w_in, grads["in"], "adamw_w_in")
    res["conv_w"] = _adamw(conv_w, m_conv_w, v_conv_w, [[d_conv]], "adamw_conv_w")
    res["hgrn_norm_w"] = _adamw(hgrn_norm_w, m_hgrn_norm_w, v_hgrn_norm_w, [[d_norm_w]], "adamw_norm_w")
    res["lower_bounds"] = _adamw(lower_bounds, m_lower_bounds, v_lower_bounds, [[d_lower]], "adamw_lower_bounds")
    res["w_branch"] = _adamw(w_branch, m_w_branch, v_w_branch, grads["br"], "adamw_w_branch")
    res["w_out"] = _adamw(w_out, m_w_out, v_w_out, grads["out"], "adamw_w_out")
    res["ln_g"] = _adamw(ln_g, m_ln_g, v_ln_g, [[d_ln_g]], "adamw_ln_g")
    res["ln_b"] = _adamw(ln_b, m_ln_b, v_ln_b, [[d_ln_b]], "adamw_ln_b")
    names = ["w_mod", "b_mod", "w_in", "conv_w", "hgrn_norm_w", "lower_bounds", "w_branch", "w_out", "ln_g", "ln_b"]
    return (loss, dx[None], *[res[n][0] for n in names], *[res[n][1] for n in names],
            *[res[n][2] for n in names], *[res[n][3] for n in names])


class _Plan:
    FIRST_CHUNKS = 3
    WINDOWS = ((0, 1152), (1152, 640), (1792, 512))

    def __init__(self, w_in, w_br, w_out, chip, core):
        self.local = {"in": w_in, "br": w_br, "out": w_out}
        self.chip, self.where = chip, jnp.stack([chip, core]).astype(jnp.int32)
        self.gathered, self.partial, self.grads, self.chip_sums, self.scattered, self.pending = {}, {}, {}, {}, {}, {}

    def chunks(self, l):
        return self.FIRST_CHUNKS if l == 0 else 1

    def w_in(self, l):
        return [self.gathered[("in", l, c)] for c in range(self.chunks(l))]

    def _shard(self, key):
        mine = self.local[key[0]][key[1]]
        if key[0] == "in":
            cols = mine.shape[-1] // self.chunks(key[1])
            mine = mine[:, key[2] * cols:(key[2] + 1) * cols]
        return mine

    def _slab(self, key):
        mine = _halves(self._shard(key))
        return lax.dynamic_update_slice(lax.empty((N_CHIPS,) + mine.shape, mine.dtype), mine[None], (self.chip, 0, 0, 0, 0))

    def _gather(self, keys):
        return ("gather", keys), _gather_job([self._slab(key) for key in keys])

    def _gather_window(self, key, n):
        slab = self._slab(key) if n == 0 else self.partial[key]
        return ("gather" if n == len(self.WINDOWS) - 1 else "gather_part", [key]), _gather_job([slab], self.WINDOWS[n])

    def _to_sibling(self, keys):
        return ("to_sibling", keys), _to_sibling_job([_halves(self.grads[key], 1) for key in keys])

    def _scatter(self, keys):
        return ("scatter", keys), _scatter_job([self.chip_sums[key] for key in keys])

    def job(self, stage, l, c=0):
        parts = []
        if stage == "proj_fwd":
            parts = [self._gather([("in", l, c + 1)] if c + 1 < self.chunks(l) else [("br", l), ("out", l)])]
        elif stage == "attn_fwd" and l + 1 < DEPTH:
            parts = [self._gather_window(("in", l + 1, 0), 0)]
        elif stage == "hgrn_fwd" and l + 1 < DEPTH:
            parts = [self._gather_window(("in", l + 1, 0), 1)]
        elif stage == "merge_fwd" and l + 1 < DEPTH:
            parts = [self._gather_window(("in", l + 1, 0), 2)]
        elif stage == "attn_bwd":
            parts = [self._to_sibling([("out", l), ("br", l)])] + ([self._scatter([("in", l + 1)])] if l + 1 < DEPTH else [])
        elif stage == "hgrn_bwd":
            parts = [self._scatter([("out", l), ("br", l)])]
        elif stage == "proj_bwd":
            parts = [self._to_sibling([("in", l)])] if l else [self._scatter([("in", 0)])]
        self.pending[(stage, l, c)] = [(tag, len(job.outs)) for tag, job in parts]
        return _join_jobs([job for _, job in parts])

    def done(self, stage, l, outs, c=0):
        if outs is None:
            return
        at = 0
        for (what, keys), n_outs in self.pending[(stage, l, c)]:
            mine, at = outs[at:at + n_outs], at + n_outs
            for n, key in enumerate(keys):
                if what == "gather":
                    self.gathered[key] = mine[n].reshape((N_CHIPS,) + self._shard(key).shape)
                elif what == "gather_part":
                    self.partial[key] = mine[n]
                elif what == "to_sibling":
                    self.chip_sums[key] = _sum_cores(_halves(self.grads[key], 1), mine[n], self.where)
                else:
                    self.scattered[key] = mine[n]

    def first(self, rows):
        tag, job = self._gather([("in", 0, 0)])
        self.pending[("first", 0, 0)] = [(tag, len(job.outs))]
        outs = _run_job(_join_jobs([job, _gather8_job(rows)]), "gather_first")
        self.done("first", 0, outs[:-1])
        return outs[-1]

    def took(self, key, grad):
        self.grads[key] = grad
        if key == ("in", 0):
            tag, job = self._to_sibling([key])
            self.pending[("took", 0, 0)] = [(tag, len(job.outs))]
            self.done("took", 0, _run_job(job, "to_sibling_last"))

    def finish(self, rows):
        keys = [(kind, l) for kind in ("in", "br", "out") for l in range(DEPTH)]
        halves = [_sum_chips(self.chip_sums[key], self.scattered[key], self.where) for key in keys]
        outs = _run_job(_join_jobs([_place_job(halves), _gather8_job(rows)]), "place_halves")
        return {key: w.reshape(self.grads[key].shape[1:]) for key, w in zip(keys, outs[:-1])}, outs[-1]


def _local_step(x0, target, mods, lbs, conv_all, hgrn_norm_w, ln_g, ln_b, plan):
    D = D_MODEL
    after, before = _attn_consts()
    hg_consts = _hgrn_consts()

    saved = []
    xl = x0
    for l in range(DEPTH):
        n = plan.chunks(l)
        (proj, h), got = _proj_fwd(xl, mods[l], plan.gathered[("in", l, 0)], plan.job("proj_fwd", l, 0), (0, n))
        plan.done("proj_fwd", l, got, 0)
        for c in range(1, n):
            (proj,), got = _proj_cols(h, plan.gathered[("in", l, c)], (c, n), proj, plan.job("proj_fwd", l, c))
            plan.done("proj_fwd", l, got, c)
        (o_a, tot), got = _attn_fwd(proj, after, plan.job("attn_fwd", l))
        plan.done("attn_fwd", l, got)
        (o_b, states), got = _hgrn_fwd(proj, lbs[l], hg_consts, plan.job("hgrn_fwd", l))
        plan.done("hgrn_fwd", l, got)
        ys = _branch_fwd(proj, o_a, o_b, hgrn_norm_w, conv_all, l)
        wb = jnp.concatenate(list(plan.gathered[("br", l)]), axis=-1)
        (x_next, merged, y), got = _merge_fwd(xl, mods[l], proj, ys, wb, plan.gathered[("out", l)], ln_g, ln_b, l, plan.job("merge_fwd", l))
        plan.done("merge_fwd", l, got)
        saved.append((xl, proj, h, o_a, tot, o_b, states, ys, merged, y, wb))
        xl = x_next
    dx, loss_blk = _loss_head(xl, target)

    small = [None] * DEPTH
    for l in reversed(range(DEPTH)):
        xin, proj, h, o_a, tot, o_b, states, ys, merged, y, wb = saved[l]
        dx_res, dproj, dy_a, dy_b, dy_c, dln_g, dln_b, dgate, g_br, g_out = _merge_bwd(
            dx, xin, y, merged, mods[l], proj, ys, wb, plan.gathered[("out", l)], ln_g, l)
        plan.took(("out", l), g_out)
        plan.took(("br", l), g_br)
        d_oa, d_ob, dnorm_w, dconv_w, dproj = _branch_bwd(proj, o_a, o_b, hgrn_norm_w, conv_all, dy_a, dy_b, dy_c, dproj, l)
        (dproj,), got = _attn_bwd(proj, d_oa, tot, after, before, dproj, plan.job("attn_bwd", l))
        plan.done("attn_bwd", l, got)
        (dlb, dproj), got = _hgrn_bwd(proj, lbs[l], states, d_ob, hg_consts, dproj, plan.job("hgrn_bwd", l))
        plan.done("hgrn_bwd", l, got)
        plan.took(("in", l), _grad_w_in(h, dproj))
        (dx, dshift, dscale), got = _proj_bwd(dproj, plan.w_in(l), xin, mods[l], dx_res, plan.job("proj_bwd", l))
        plan.done("proj_bwd", l, got)
        small[l] = (dshift, dscale, dgate, dnorm_w, dlb, dln_g, dln_b, dconv_w)
    return loss_blk, dx, small
```

```python
import functools
import math

import numpy as np
import jax
import jax.numpy as jnp
from jax import lax
from jax.experimental import pallas as pl
from jax.experimental.pallas import tpu as pltpu

F32 = jnp.float32
BF = jnp.bfloat16
MESH = pl.DeviceIdType.MESH

DEPTH = 2
D_MODEL = 1024
WIDTH = 512
IN_COLS = 12 * WIDTH + 3 * D_MODEL
N_CHIPS = 4
N_DEV = 8
SB_BLOCK = 128
SB_HEAD_DIM = 64
HG_CHUNK = 128
HG_DIM = 128
LN_EPS = 1e-5
RMS_EPS = 1e-6
ALPHA = (2.0 * DEPTH) ** 0.25
ADAM_LR, ADAM_B1, ADAM_B2, ADAM_EPS, ADAM_WD, ADAM_STEP = 0.001, 0.9, 0.999, 1e-08, 0.01, 10
VMEM_LIMIT = 56 << 20


def _params(**kw):
    return pltpu.CompilerParams(vmem_limit_bytes=VMEM_LIMIT, **kw)


def _dot(a, b):
    return jnp.dot(a, b, preferred_element_type=F32)


def _dot_nt(a, b):
    return lax.dot_general(a, b, (((1,), (1,)), ((), ())), preferred_element_type=F32)


def _dot_tn(a, b):
    return lax.dot_general(a, b, (((0,), (0,)), ((), ())), preferred_element_type=F32)


def _sigmoid(x):
    return 1.0 / (1.0 + jnp.exp(-x))


def _silu(x):
    return x * _sigmoid(x)


def _standardize(x):
    mu = jnp.mean(x, axis=-1, keepdims=True)
    xc = x - mu
    var = jnp.mean(xc * xc, axis=-1, keepdims=True)
    rstd = lax.rsqrt(var + LN_EPS)
    return xc * rstd, rstd


def _standardize_bwd(dxs, xs, rstd):
    return rstd * (dxs - jnp.mean(dxs, axis=-1, keepdims=True) - xs * jnp.mean(dxs * xs, axis=-1, keepdims=True))


class _Job:
    def __init__(self, ins, outs, sems, make, alias=None):
        self.ins, self.outs, self.sems, self.make = list(ins), list(outs), list(sems), make
        self.alias = dict(alias or {})


def _join_jobs(jobs):
    jobs = [j for j in jobs if j is not None]
    if len(jobs) <= 1:
        return jobs[0] if jobs else None

    def make(ins, outs, sems):
        phases, i, o, s = [], 0, 0, 0
        for j in jobs:
            got = j.make(ins[i:i + len(j.ins)], outs[o:o + len(j.outs)], sems[s:s + len(j.sems)])
            i, o, s = i + len(j.ins), o + len(j.outs), s + len(j.sems)
            for n, phase in enumerate(got):
                if n == len(phases):
                    phases.append([])
                phases[n] += phase
        return phases

    alias, i, o = {}, 0, 0
    for j in jobs:
        alias.update({i + a: o + b for a, b in j.alias.items()})
        i, o = i + len(j.ins), o + len(j.outs)
    return _Job(sum([j.ins for j in jobs], []), sum([j.outs for j in jobs], []), sum([j.sems for j in jobs], []), make, alias)


def _flip(v, bit):
    return 1 - v if bit else v


def _halves(a, front=0):
    shape = a.shape
    lead = math.prod(shape[front:-2])
    return a.reshape(shape[:front] + (lead, 2, shape[-2] // 2, shape[-1]))


def _dma_sems(*shapes):
    return [pltpu.SemaphoreType.DMA(s) for s in shapes]


def _same(arrays):
    return [jax.ShapeDtypeStruct(a.shape, a.dtype) for a in arrays]


def _gather_job(slabs, window=None):
    n = len(slabs)
    cols = slice(None) if window is None else pl.ds(*window)

    def make(ins, outs, sems):
        send1, recv1, send2, recv2 = sems
        mx, my, mc = lax.axis_index("x"), lax.axis_index("y"), lax.axis_index("c")
        fetch, pass_on = [], []
        for a in range(n):
            ours = outs[a].at[2 * mx + my, :, mc, :, cols]
            for k in range(1, N_CHIPS):
                px, py = _flip(mx, k & 2), _flip(my, k & 1)
                fetch.append(pltpu.make_async_remote_copy(
                    src_ref=ours, dst_ref=ours, send_sem=send1.at[a, k - 1], recv_sem=recv1.at[a, k - 1],
                    device_id=(px, py, mc), device_id_type=MESH))
                theirs = outs[a].at[2 * px + py, :, mc, :, cols]
                pass_on.append(pltpu.make_async_remote_copy(
                    src_ref=theirs, dst_ref=theirs, send_sem=send2.at[a, k - 1], recv_sem=recv2.at[a, k - 1],
                    device_id=(mx, my, 1 - mc), device_id_type=MESH))
        return [fetch, pass_on]

    pairs = (n, N_CHIPS - 1)
    return _Job(slabs, _same(slabs), _dma_sems(pairs, pairs, pairs, pairs), make, {a: a for a in range(n)})


def _to_sibling_job(grads):
    n = len(grads)

    def make(ins, outs, sems):
        send_sems, recv_sems = sems
        mx, my, mc = lax.axis_index("x"), lax.axis_index("y"), lax.axis_index("c")
        return [[pltpu.make_async_remote_copy(
            src_ref=ins[a].at[:, :, 1 - mc], dst_ref=outs[a], send_sem=send_sems.at[a], recv_sem=recv_sems.at[a],
            device_id=(mx, my, 1 - mc), device_id_type=MESH) for a in range(n)]]

    outs = [jax.ShapeDtypeStruct(g.shape[:2] + g.shape[3:], g.dtype) for g in grads]
    return _Job(grads, outs, _dma_sems((n,), (n,)), make)


def _scatter_job(sums):
    n = len(sums)

    def make(ins, outs, sems):
        send_sems, recv_sems = sems
        mx, my, mc = lax.axis_index("x"), lax.axis_index("y"), lax.axis_index("c")
        copies = []
        for a in range(n):
            for k in range(1, N_CHIPS):
                px, py = _flip(mx, k & 2), _flip(my, k & 1)
                copies.append(pltpu.make_async_remote_copy(
                    src_ref=ins[a].at[2 * px + py], dst_ref=outs[a].at[k - 1], send_sem=send_sems.at[a, k - 1],
                    recv_sem=recv_sems.at[a, k - 1], device_id=(px, py, mc), device_id_type=MESH))
        return [copies]

    pairs = (n, N_CHIPS - 1)
    return _Job(sums, [jax.ShapeDtypeStruct((N_CHIPS - 1,) + s.shape[1:], s.dtype) for s in sums], _dma_sems(pairs, pairs), make)


def _place_job(wholes):
    n = len(wholes)

    def make(ins, outs, sems):
        send_sems, recv_sems = sems
        mx, my, mc = lax.axis_index("x"), lax.axis_index("y"), lax.axis_index("c")
        copies = []
        for a in range(n):
            here = outs[a].at[:, mc]
            copies.append(pltpu.make_async_remote_copy(src_ref=here, dst_ref=here, send_sem=send_sems.at[a], recv_sem=recv_sems.at[a],
                                                       device_id=(mx, my, 1 - mc), device_id_type=MESH))
        return [copies]

    return _Job(wholes, _same(wholes), _dma_sems((n,), (n,)), make, {a: a for a in range(n)})


def _gather8_job(x):
    def make(ins, outs, sems):
        local_sem, send_sems, recv_sems = sems
        mx, my, mc = lax.axis_index("x"), lax.axis_index("y"), lax.axis_index("c")
        here = outs[0].at[4 * mx + 2 * my + mc]
        copies = [pltpu.make_async_copy(ins[0], here, local_sem.at[0])]
        for k in range(1, N_DEV):
            peer = (_flip(mx, k & 4), _flip(my, k & 2), _flip(mc, k & 1))
            copies.append(pltpu.make_async_remote_copy(src_ref=ins[0], dst_ref=here, send_sem=send_sems.at[k - 1],
                                                       recv_sem=recv_sems.at[k - 1], device_id=peer, device_id_type=MESH))
        return [copies]

    return _Job([x], [jax.ShapeDtypeStruct((N_DEV,) + x.shape, x.dtype)], _dma_sems((1,), (N_DEV - 1,), (N_DEV - 1,)), make)


def _run_phases(phases, first=0):
    for n, phase in enumerate(phases):
        if n >= first:
            for cp in phase:
                cp.start()
        for cp in phase:
            cp.wait()


def _run_job(job, name):
    k_in, k_out = len(job.ins), len(job.outs)

    def body(*refs):
        _run_phases(job.make(refs[:k_in], refs[k_in:k_in + k_out], refs[k_in + k_out:]))

    hbm = pl.BlockSpec(memory_space=pl.ANY)
    return pl.pallas_call(body, name=name, in_specs=[hbm] * k_in, out_specs=[hbm] * k_out, out_shape=job.outs,
                          scratch_shapes=job.sems, input_output_aliases=job.alias, compiler_params=_params())(*job.ins)


def _hosted(body, job, args, *, name, grid, in_specs, out_specs, out_shape, scratch_shapes=(), semantics, aliases=None):
    in_specs, out_specs, out_shape, scratch = list(in_specs), list(out_specs), list(out_shape), list(scratch_shapes)
    aliases = dict(aliases or {})
    if job is None:
        outs = pl.pallas_call(body, name=name, grid=grid, in_specs=in_specs, out_specs=out_specs, out_shape=out_shape,
                              scratch_shapes=scratch, input_output_aliases=aliases,
                              compiler_params=_params(dimension_semantics=semantics))(*args)
        return list(outs), None
    n_in, n_out, n_scr, k_in, k_out = len(in_specs), len(out_specs), len(scratch), len(job.ins), len(job.outs)

    def wrapped(*refs):
        ins, rest = refs[:n_in], refs[n_in:]
        job_ins, rest = rest[:k_in], rest[k_in:]
        outs, rest = rest[:n_out], rest[n_out:]
        job_outs, rest = rest[:k_out], rest[k_out:]
        scr, sems = rest[:n_scr], rest[n_scr:]
        ids = [pl.program_id(a) for a in range(len(grid))]
        first = functools.reduce(jnp.logical_and, [i == 0 for i in ids])
        last = functools.reduce(jnp.logical_and, [i == g - 1 for i, g in zip(ids, grid)])

        @pl.when(first)
        def _():
            for cp in job.make(job_ins, job_outs, sems)[0]:
                cp.start()

        body(*ins, *outs, *scr)

        @pl.when(last)
        def _():
            _run_phases(job.make(job_ins, job_outs, sems), first=1)

    hbm = pl.BlockSpec(memory_space=pl.ANY)
    outs = pl.pallas_call(
        wrapped, name=name, grid=grid, in_specs=in_specs + [hbm] * k_in, out_specs=out_specs + [hbm] * k_out,
        out_shape=out_shape + job.outs, scratch_shapes=scratch + job.sems,
        input_output_aliases={**aliases, **{n_in + i: n_out + o for i, o in job.alias.items()}},
        compiler_params=_params(dimension_semantics=("arbitrary",) * len(grid)))(*args, *job.ins)
    return list(outs[:n_out]), list(outs[n_out:])


def _proj_fwd(x, mod, wg, job=None, chunk=(0, 1), tm=512):
    S, D = x.shape
    tm = min(tm, S)
    tn = wg.shape[-1]
    c, n = chunk

    rows = S // tm

    def body(x_ref, mod_ref, w_ref, proj_ref, h_ref, hs):
        mine = pl.ds(pl.multiple_of(pl.program_id(1) * tm, tm), tm)

        @pl.when(pl.program_id(0) == 0)
        def _():
            xs, _ = _standardize(x_ref[...])
            h = xs * (1.0 + mod_ref[:, D:2 * D]) + mod_ref[:, 0:D]
            hb = h.astype(BF)
            hs[mine, :] = hb
            h_ref[...] = hb

        proj_ref[...] = _dot(hs[mine, :], w_ref[...])

    once = lambda j, i: jnp.where(j == 0, i, rows - 1)
    return _hosted(
        body, job, (x, mod, wg), name="proj_fwd",
        grid=(N_CHIPS, rows),
        in_specs=[pl.BlockSpec((tm, D), lambda j, i: (once(j, i), 0)),
                  pl.BlockSpec((1, 3 * D), lambda j, i: (0, 0)),
                  pl.BlockSpec((None, D, tn), lambda j, i: (j, 0, 0))],
        out_specs=[pl.BlockSpec((tm, tn), lambda j, i: (i, j * n + c)),
                   pl.BlockSpec((tm, D), lambda j, i: (once(j, i), 0))],
        out_shape=[jax.ShapeDtypeStruct((S, IN_COLS), F32), jax.ShapeDtypeStruct((S, D), BF)],
        scratch_shapes=[pltpu.VMEM((S, D), BF)],
        semantics=("arbitrary", "arbitrary"))


def _proj_cols(h, wg, chunk, proj, job=None, tm=512):
    S, D = h.shape
    tm = min(tm, S)
    tn = wg.shape[-1]
    c, n = chunk

    def body(h_ref, w_ref, prev_ref, proj_ref):
        mine = pl.ds(pl.multiple_of(pl.program_id(1) * tm, tm), tm)
        proj_ref[...] = _dot(h_ref[mine, :], w_ref[...])

    in_specs = [pl.BlockSpec((S, D), lambda j, i: (0, 0)), pl.BlockSpec((None, D, tn), lambda j, i: (j, 0, 0)),
                pl.BlockSpec(memory_space=pl.ANY)]
    return _hosted(body, job, (h, wg, proj), name="proj_cols", grid=(N_CHIPS, S // tm), in_specs=in_specs,
                   out_specs=[pl.BlockSpec((tm, tn), lambda j, i: (i, j * n + c))],
                   out_shape=[jax.ShapeDtypeStruct(proj.shape, proj.dtype)],
                   semantics=("arbitrary", "arbitrary"), aliases={2: 0})


SB_ROWS = 256
SB_KEYS = 256


def _attn_consts():
    j = np.arange(SB_KEYS)[:, None]
    s = np.arange(SB_KEYS)[None, :]
    from_here = np.concatenate([(j >= s), (j >= s)], axis=0).astype(np.float32)
    return jnp.asarray(from_here, BF), jnp.asarray((j <= s).astype(np.float32), BF)


def _hi_lo(x):
    hi = lax.bitcast_convert_type(lax.bitcast_convert_type(x, jnp.uint32) & jnp.uint32(0xFFFF0000), F32)
    return hi.astype(BF), (x - hi).astype(BF)


def _sums_r(x, t2):
    hi, lo = _hi_lo(x)
    return _dot(jnp.concatenate([hi, lo], axis=1), t2)


def _all_lanes(col, lanes):
    return jnp.broadcast_to(col, (col.shape[0], lanes))


def _attn_rows(ref, r0, rows, lanes, head0, scale=None):
    v = ref[pl.ds(r0, rows), lanes]
    if scale is not None:
        v = v * scale
    return jnp.concatenate([jnp.where(head0, v, 0.0), jnp.where(head0, 0.0, v)], axis=0).astype(BF)


SB_PAIRS_FWD = 4
SB_PAIRS_BWD = 2


def _attn_specs(S, n_pairs):
    return lambda base: pl.BlockSpec((S, n_pairs * SB_BLOCK), lambda p, base=base: (0, base // n_pairs + p))


def _attn_scores(q2n, k_ref, lanes, kj, t2, from_here_ref, masked):
    c0 = pl.multiple_of(kj * SB_KEYS, SB_KEYS)
    kb = k_ref[pl.ds(c0, SB_KEYS), lanes].astype(BF)
    zn = _dot_nt(q2n, kb)
    lsb = jnp.minimum(zn, 0.0) - jnp.log(1.0 + jnp.exp(-jnp.abs(zn)))
    valid = None
    if masked:
        valid = (lax.broadcasted_iota(jnp.int32, zn.shape, 1) + kj * SB_KEYS) < t2
        lsb = jnp.where(valid, lsb, 0.0)
    return c0, kb, zn, valid, lsb, _sums_r(lsb, from_here_ref[...])


def _attn_fwd(proj, from_here, job=None):
    S = proj.shape[0]
    TQ = SB_ROWS
    assert S % TQ == 0 and SB_KEYS == TQ
    scale = SB_HEAD_DIM ** -0.5
    n_pairs = SB_PAIRS_FWD
    pairs = range(n_pairs)
    lanes = [pl.ds(p * SB_BLOCK, SB_BLOCK) for p in pairs]

    def body(q_ref, k_ref, v_ref, from_here_ref, o_ref, tot_ref, run, acc):
        head0 = lax.broadcasted_iota(jnp.int32, (1, 2 * SB_HEAD_DIM), 1) < SB_HEAD_DIM

        def qloop(qi, _):
            r0 = pl.multiple_of(qi * TQ, TQ)
            q2n = [_attn_rows(q_ref, r0, TQ, lanes[p], head0, -scale) for p in pairs]
            trow = lax.broadcasted_iota(jnp.int32, (TQ, SB_KEYS), 0) + qi * TQ
            t2 = jnp.concatenate([trow, trow], axis=0)
            run[...] = jnp.zeros_like(run)
            acc[...] = jnp.zeros_like(acc)

            def step(kj, masked):
                got = [_attn_scores(q2n[p], k_ref, lanes[p], kj, t2, from_here_ref, masked) for p in pairs]
                for p in pairs:
                    c0, _, zn, valid, _, sums = got[p]
                    r = run[p]
                    e = sums - zn + jnp.concatenate([r, r], axis=1)
                    if masked:
                        e = jnp.where(valid, e, -jnp.inf)
                    acc[p] += _dot(jnp.exp(e).astype(BF), v_ref[pl.ds(c0, SB_KEYS), lanes[p]].astype(BF))
                    run[p] = r + _all_lanes(sums[:, 0:1], SB_BLOCK)

            step(qi, True)

            def below(n, _):
                step(qi - 1 - 2 * n, False)
                step(qi - 2 - 2 * n, False)
                return 0

            lax.fori_loop(0, qi // 2, below, 0)

            @pl.when(qi % 2 == 1)
            def _():
                step(0, False)
            for p in pairs:
                o_ref[pl.ds(r0, TQ), lanes[p]] = jnp.where(head0, acc[p, 0:TQ, :], acc[p, TQ:2 * TQ, :])
                tot_ref[p, 0, pl.ds(r0, TQ), :] = run[p, 0:TQ, :]
                tot_ref[p, 1, pl.ds(r0, TQ), :] = run[p, TQ:2 * TQ, :]
            return 0

        lax.fori_loop(0, S // TQ, qloop, 0)

    col = _attn_specs(S, n_pairs)
    state = pltpu.VMEM((n_pairs, 2 * TQ, SB_BLOCK), F32)
    return _hosted(
        body, job, (proj, proj, proj, from_here), name="attn_fwd",
        grid=(WIDTH // (n_pairs * SB_BLOCK),),
        in_specs=[col(0), col(4), col(8), pl.BlockSpec(from_here.shape, lambda p: (0, 0))],
        out_specs=[col(0), pl.BlockSpec((n_pairs, 2, S, 128), lambda p: (p, 0, 0, 0))],
        out_shape=[jax.ShapeDtypeStruct((S, WIDTH), F32), jax.ShapeDtypeStruct((WIDTH // 128, 2, S, 128), F32)],
        scratch_shapes=[state, state],
        semantics=("parallel",))


def _attn_bwd(proj, d_o, tot, from_here, up_to, dproj, job=None):
    S = proj.shape[0]
    TQ = SB_ROWS
    assert S % TQ == 0 and SB_KEYS == TQ
    scale = SB_HEAD_DIM ** -0.5
    n_pairs = SB_PAIRS_BWD
    pairs = range(n_pairs)
    lanes = [pl.ds(p * SB_BLOCK, SB_BLOCK) for p in pairs]

    def body(q_ref, k_ref, v_ref, do_ref, tot_ref, from_here_ref, up_to_ref, _, dproj_ref, pre, cum, dq_acc, dk_acc, dv_acc, stage, sems):
        dq_ref, dk_ref, dv_ref = stage.at[0], stage.at[1], stage.at[2]
        head0 = lax.broadcasted_iota(jnp.int32, (1, 2 * SB_HEAD_DIM), 1) < SB_HEAD_DIM
        dk_acc[...] = jnp.zeros_like(dk_acc)
        dv_acc[...] = jnp.zeros_like(dv_acc)

        def qloop(qi, _):
            r0 = pl.multiple_of(qi * TQ, TQ)
            q2n = [_attn_rows(q_ref, r0, TQ, lanes[p], head0, -scale) for p in pairs]
            do2 = [_attn_rows(do_ref, r0, TQ, lanes[p], head0) for p in pairs]
            trow = lax.broadcasted_iota(jnp.int32, (TQ, SB_KEYS), 0) + qi * TQ
            t2 = jnp.concatenate([trow, trow], axis=0)
            for p in pairs:
                pre[p, 0:TQ, :] = tot_ref[p, 0, pl.ds(r0, TQ), :]
                pre[p, TQ:2 * TQ, :] = tot_ref[p, 1, pl.ds(r0, TQ), :]
            cum[...] = jnp.zeros_like(cum)
            dq_acc[...] = jnp.zeros_like(dq_acc)

            def step(kj, masked):
                got = [_attn_scores(q2n[p], k_ref, lanes[p], kj, t2, from_here_ref, masked) for p in pairs]
                das = [_dot_nt(do2[p], v_ref[pl.ds(got[p][0], SB_KEYS), lanes[p]].astype(BF)) for p in pairs]
                for p in pairs:
                    c0, kb, zn, valid, lsb, sums = got[p]
                    later = pre[p] - _all_lanes(sums[:, 0:1], SB_BLOCK)
                    pre[p] = later
                    e = sums - zn + jnp.concatenate([later, later], axis=1)
                    sig = jnp.exp(lsb - zn)
                    if masked:
                        e = jnp.where(valid, e, -jnp.inf)
                        sig = jnp.where(valid, sig, 0.0)
                    a = jnp.exp(e)
                    w = das[p] * a
                    upto = _dot(w.astype(BF), up_to_ref[...])
                    c = cum[p]
                    dz = w - sig * (upto + jnp.concatenate([c, c], axis=1))
                    cum[p] = c + _all_lanes(upto[:, SB_KEYS - 1:SB_KEYS], SB_BLOCK)
                    dzb = dz.astype(BF)
                    dq_acc[p] += _dot(dzb, kb)
                    dk_acc[pl.ds(c0, SB_KEYS), lanes[p]] += _dot_tn(dzb, q2n[p])
                    dv_acc[pl.ds(c0, SB_KEYS), lanes[p]] += _dot_tn(a.astype(BF), do2[p])

            def below(n, _):
                step(2 * n, False)
                step(2 * n + 1, False)
                return 0

            lax.fori_loop(0, qi // 2, below, 0)

            @pl.when(qi % 2 == 1)
            def _():
                step(qi - 1, False)

            step(qi, True)
            for p in pairs:
                dq_ref[pl.ds(r0, TQ), lanes[p]] = (jnp.where(head0, dq_acc[p, 0:TQ, :], dq_acc[p, TQ:2 * TQ, :]) * scale).astype(BF)
            return 0

        lax.fori_loop(0, S // TQ, qloop, 0)
        dk_ref[...] = (-dk_acc[...]).astype(BF)
        dv_ref[...] = dv_acc[...].astype(BF)
        lane0 = pl.program_id(0) * (n_pairs * SB_BLOCK)
        _put_columns([dq_ref, dk_ref, dv_ref], dproj_ref, [WIDTH * n + lane0 for n in range(3)], sems)

    col = _attn_specs(S, n_pairs)
    whole = lambda a: pl.BlockSpec(a.shape, lambda p: (0, 0))
    hbm = pl.BlockSpec(memory_space=pl.ANY)
    state = pltpu.VMEM((n_pairs, 2 * TQ, SB_BLOCK), F32)
    grads = pltpu.VMEM((S, n_pairs * SB_BLOCK), F32)
    return _hosted(
        body, job, (proj, proj, proj, d_o, tot, from_here, up_to, dproj), name="attn_bwd",
        grid=(WIDTH // (n_pairs * SB_BLOCK),),
        in_specs=[col(0), col(4), col(8), col(0), pl.BlockSpec((n_pairs, 2, S, 128), lambda p: (p, 0, 0, 0)), whole(from_here),
                  whole(up_to), hbm],
        out_specs=[hbm],
        out_shape=[jax.ShapeDtypeStruct(dproj.shape, dproj.dtype)],
        scratch_shapes=[state, state, state, grads, grads, pltpu.VMEM((3, S, n_pairs * SB_BLOCK), BF), pltpu.SemaphoreType.DMA((3,))],
        semantics=("arbitrary",), aliases={7: 0})


HG_LEVELS = tuple(HG_CHUNK >> n for n in range(1, HG_CHUNK.bit_length()))


def _hgrn_consts():
    C = HG_CHUNK
    t = np.arange(C)[:, None]
    s = np.arange(C)[None, :]
    rows = [(s <= t), (s > t)]
    masks = [(t == s)]
    for m in HG_LEVELS:
        two = 2 * m
        mid = (t // two) * two + m
        right = (t % two) >= m
        rows.append((right & (s >= mid) & (s <= t)) | ((~right) & (s > t) & (s <= mid - 1)))
        masks.append(((t // two) == (s // two)) & right & ((s % two) < m))
    tri = np.concatenate(rows, axis=0).astype(np.float32)
    twice = lambda a: jnp.asarray(np.concatenate([a, a], axis=1), BF)
    return (twice(tri), twice(tri.T), jnp.asarray(np.stack(masks).astype(np.float32), F32))


HG_SUM_BLOCKS = 2 + len(HG_LEVELS)


def _split_rows(g):
    hi = g.astype(BF)
    return jnp.concatenate([hi, (g - hi.astype(F32)).astype(BF)], axis=0)


def _hgrn_sum_blocks(e):
    C = HG_CHUNK
    blocks = tuple(e[n * C:(n + 1) * C] for n in range(HG_SUM_BLOCKS))
    return blocks + (jnp.broadcast_to(e[C - 1:C], (HG_DIM, e.shape[1])),)


@jax.custom_vjp
def _hgrn_sums(tri, tri_t, g):
    return _hgrn_sum_blocks(_dot(tri, _split_rows(g)))


def _hgrn_sums_fwd(tri, tri_t, g):
    return _hgrn_sums(tri, tri_t, g), (tri, tri_t)


def _hgrn_sums_bwd(res, ds):
    tri, tri_t = res
    C = HG_CHUNK
    last = lax.broadcasted_iota(jnp.int32, (C, 1), 0) == C - 1
    prefix = ds[0] + jnp.where(last, jnp.sum(ds[-1], axis=0, keepdims=True), 0.0)
    d = jnp.concatenate((prefix,) + tuple(ds[1:-1]), axis=0)
    return jnp.zeros_like(tri), jnp.zeros_like(tri_t), _dot(tri_t[:, :d.shape[0]], d.astype(BF))


_hgrn_sums.defvjp(_hgrn_sums_fwd, _hgrn_sums_bwd)


def _bf_dot(a, b):
    return _dot(a.astype(BF), b.astype(BF))


def _bf_dot_nt(a, b):
    return _dot_nt(a.astype(BF), b.astype(BF))


def _bf_dot_tn(a, b):
    return _dot_tn(a.astype(BF), b.astype(BF))


@jax.custom_vjp
def _mm(a, b):
    return _bf_dot(a, b)


_mm.defvjp(lambda a, b: (_bf_dot(a, b), (a, b)), lambda r, ct: (_bf_dot_nt(ct, r[1]), _bf_dot_tn(r[0], ct)))


@jax.custom_vjp
def _mm_nt(a, b):
    return _bf_dot_nt(a, b)


_mm_nt.defvjp(lambda a, b: (_bf_dot_nt(a, b), (a, b)), lambda r, ct: (_bf_dot(ct, r[1]), _bf_dot_tn(ct, r[0])))


@jax.custom_vjp
def _mm_tn(a, b):
    return _bf_dot_tn(a, b)


_mm_tn.defvjp(lambda a, b: (_bf_dot_tn(a, b), (a, b)), lambda r, ct: (_bf_dot_nt(r[1], ct), _bf_dot(r[0], ct)))


def _hgrn_gates(tri, tri_t, qraw, fpre, lb):
    q = _silu(qraw)
    f = lb + (1.0 - lb) * _sigmoid(fpre)
    return q, 1.0 - f, _hgrn_sums(tri, tri_t, jnp.log(f))


def _hgrn_mix(masks, q, k, e, v, st):
    prefix, suffix, whole = e[0], e[1], e[-1]
    scores = masks[0] * _mm_nt(q, k)
    for n in range(len(HG_LEVELS)):
        decay = jnp.exp(e[2 + n])
        scores = scores + masks[n + 1] * _mm_nt(q * decay, k * decay)
    o = _mm_nt(q * jnp.exp(prefix), st) + _mm(scores, v)
    st_new = st * jnp.exp(whole) + _mm_tn(v, k * jnp.exp(suffix))
    return o, st_new


def _hgrn_chunk(tri, tri_t, masks, qraw, fpre, v, st, lb):
    q, k, e = _hgrn_gates(tri, tri_t, qraw, fpre, lb)
    return _hgrn_mix(masks, q, k, e, v, st)


HG_HEADS_PER_STEP = 4
HG_LANES = HG_HEADS_PER_STEP * HG_DIM


def _hgrn_specs(S, consts):
    col = lambda base: pl.BlockSpec((S, HG_LANES), lambda p, base=base: (0, base // HG_HEADS_PER_STEP + p))
    whole = [pl.BlockSpec(a.shape, lambda p, n=a.ndim: (0,) * n) for a in consts]
    return col, whole


def _hgrn_fwd(proj, lbs, consts, job=None):
    S = proj.shape[0]
    nc = S // HG_CHUNK
    heads = range(HG_HEADS_PER_STEP)

    def body(q_ref, f_ref, i_ref, lb_ref, tri_ref, trit_ref, mask_ref, o_ref, st_ref):
        tri, tri_t = tri_ref[...], trit_ref[...]
        masks = [mask_ref[n] for n in range(len(HG_LEVELS) + 1)]

        def chunk(ci, sts):
            r0 = pl.multiple_of(ci * HG_CHUNK, HG_CHUNK)
            rows = pl.ds(r0, HG_CHUNK)
            new = []
            lane = [pl.ds(hd * HG_DIM, HG_DIM) for hd in heads]
            gates = [_hgrn_gates(tri, tri_t, q_ref[rows, lane[hd]], f_ref[rows, lane[hd]], lb_ref[hd]) for hd in heads]
            for hd in heads:
                st_ref[hd, ci] = sts[hd]
                o, st_new = _hgrn_mix(masks, *gates[hd], i_ref[rows, lane[hd]], sts[hd])
                o_ref[rows, lane[hd]] = o
                new.append(st_new)
            return tuple(new)

        lax.fori_loop(0, nc, chunk, tuple(jnp.zeros((HG_DIM, HG_DIM), F32) for _ in heads))

    col, whole = _hgrn_specs(S, consts)
    return _hosted(
        body, job, (proj, proj, proj, lbs, *consts), name="hgrn_fwd",
        grid=(WIDTH // HG_LANES,),
        in_specs=[col(16), col(20), col(24), pl.BlockSpec((HG_HEADS_PER_STEP, 1, 128), lambda p: (p, 0, 0))] + whole,
        out_specs=[col(0), pl.BlockSpec((HG_HEADS_PER_STEP, nc, HG_DIM, HG_DIM), lambda p: (p, 0, 0, 0))],
        out_shape=[jax.ShapeDtypeStruct((S, WIDTH), F32), jax.ShapeDtypeStruct((WIDTH // 128, nc, HG_DIM, HG_DIM), F32)],
        semantics=("parallel",))


def _hgrn_bwd(proj, lbs, states, d_o, consts, dproj, job=None):
    S = proj.shape[0]
    nc = S // HG_CHUNK

    def body(q_ref, f_ref, i_ref, lb_ref, st_ref, do_ref, tri_ref, trit_ref, mask_ref, _, dlb_ref, dproj_ref, stage, sems):
        dq_ref, df_ref, di_ref = stage.at[0], stage.at[1], stage.at[2]
        masks = [mask_ref[n] for n in range(len(HG_LEVELS) + 1)]
        fn = functools.partial(_hgrn_chunk, tri_ref[...], trit_ref[...], masks)
        heads = range(HG_HEADS_PER_STEP)

        def chunk(n, carry):
            ci = nc - 1 - n
            r0 = pl.multiple_of(ci * HG_CHUNK, HG_CHUNK)
            rows = pl.ds(r0, HG_CHUNK)
            new = []
            lane = [pl.ds(hd * HG_DIM, HG_DIM) for hd in heads]
            pulls = [jax.vjp(fn, q_ref[rows, lane[hd]], f_ref[rows, lane[hd]], i_ref[rows, lane[hd]], st_ref[hd, ci], lb_ref[hd])[1]
                     for hd in heads]
            for hd in heads:
                d_st, dlb = carry[hd]
                lanes = lane[hd]
                dq, df, di, d_prev, dl = pulls[hd]((do_ref[rows, lanes], d_st))
                dq_ref[rows, lanes] = dq.astype(BF)
                df_ref[rows, lanes] = df.astype(BF)
                di_ref[rows, lanes] = di.astype(BF)
                new.append((d_prev, dlb + dl))
            return tuple(new)

        zero = (jnp.zeros((HG_DIM, HG_DIM), F32), jnp.zeros((1, HG_DIM), F32))
        done = lax.fori_loop(0, nc, chunk, tuple(zero for _ in heads))
        for hd in heads:
            dlb_ref[hd] = done[hd][1]
        lane0 = pl.program_id(0) * HG_LANES
        _put_columns([dq_ref, df_ref, di_ref], dproj_ref, [WIDTH * n + lane0 for n in (4, 5, 6)], sems)

    col, whole = _hgrn_specs(S, consts)
    head = pl.BlockSpec((HG_HEADS_PER_STEP, 1, 128), lambda p: (p, 0, 0))
    hbm = pl.BlockSpec(memory_space=pl.ANY)
    n_in = 6 + len(consts)
    return _hosted(
        body, job, (proj, proj, proj, lbs, states, d_o, *consts, dproj), name="hgrn_bwd",
        grid=(WIDTH // HG_LANES,),
        in_specs=[col(16), col(20), col(24), head, pl.BlockSpec((HG_HEADS_PER_STEP, nc, HG_DIM, HG_DIM), lambda p: (p, 0, 0, 0)), col(0)]
        + whole + [hbm],
        out_specs=[head, hbm],
        out_shape=[jax.ShapeDtypeStruct((WIDTH // 128, 1, 128), F32), jax.ShapeDtypeStruct(dproj.shape, dproj.dtype)],
        scratch_shapes=[pltpu.VMEM((3, S, HG_LANES), BF), pltpu.SemaphoreType.DMA((3,))],
        semantics=("arbitrary",), aliases={n_in: 1})


def _shift_down(x, n):
    rows = lax.broadcasted_iota(jnp.int32, x.shape, 0)
    return jnp.where(rows >= n, pltpu.roll(x, n, 0), 0.0)


def _shift_up(x, n):
    S = x.shape[0]
    rows = lax.broadcasted_iota(jnp.int32, x.shape, 0)
    return jnp.where(rows < S - n, pltpu.roll(x, S - n, 0), 0.0)


def _branch_fwd(proj, o_a, o_b, norm_w, conv_w, layer):
    S = proj.shape[0]

    def body(oa_ref, za_ref, ob_ref, zb_ref, nw_ref, pre_ref, post_ref, u_ref, zc_ref, cw_ref, ya_ref, yb_ref, yc_ref):
        ya_ref[...] = (oa_ref[...] * _silu(za_ref[...])).astype(BF)
        ob = ob_ref[...]
        rn = lax.rsqrt(jnp.mean(ob * ob, axis=-1, keepdims=True) + RMS_EPS)
        yb_ref[...] = (ob * rn * nw_ref[layer:layer + 1, :] * _silu(zb_ref[...])).astype(BF)
        pu = pre_ref[...] * u_ref[...]
        conv = cw_ref[2:3, :] * pu + cw_ref[1:2, :] * _shift_down(pu, 1) + cw_ref[0:1, :] * _shift_down(pu, 2)
        yc_ref[...] = (post_ref[...] * conv * _silu(zc_ref[...])).astype(BF)

    col = lambda base: pl.BlockSpec((S, 128), lambda p, base=base: (0, base + p))
    out = jax.ShapeDtypeStruct((S, WIDTH), BF)
    return pl.pallas_call(
        body, name="branch_fwd",
        grid=(WIDTH // 128,),
        in_specs=[col(0), col(12), col(0), col(28), pl.BlockSpec(norm_w.shape, lambda p: (0, 0)),
                  col(32), col(36), col(40), col(44), pl.BlockSpec((None, None, 3, 128), lambda p: (p, layer, 0, 0))],
        out_specs=[col(0), col(0), col(0)],
        out_shape=[out, out, out],
        compiler_params=_params(dimension_semantics=("parallel",)),
    )(o_a, proj, o_b, proj, norm_w, proj, proj, proj, proj, conv_w)


def _put_columns(tiles, dproj_ref, firsts, sems):
    copies = [pltpu.make_async_copy(t, dproj_ref.at[:, pl.ds(pl.multiple_of(c, 128), t.shape[1])], sems.at[n])
              for n, (t, c) in enumerate(zip(tiles, firsts))]
    for cp in copies:
        cp.start()
    for cp in copies:
        cp.wait()


def _branch_bwd(proj, o_a, o_b, norm_w, conv_w, dy_a, dy_b, dy_c, dproj, layer):
    S = proj.shape[0]
    firsts = [WIDTH * n for n in (3, 7, 8, 9, 10, 11)]

    def dsilu(z):
        s = _sigmoid(z)
        return s * z, s * (1.0 + z * (1.0 - s))

    def body(oa_ref, za_ref, ob_ref, zb_ref, nw_ref, pre_ref, post_ref, u_ref, zc_ref, cw_ref, dya_ref, dyb_ref, dyc_ref, _,
             doa_ref, dob_ref, dnw_ref, dcw_ref, dproj_ref, stage, sems):
        dza_ref, dzb_ref, dpre_ref, dpost_ref, du_ref, dzc_ref = [stage.at[n] for n in range(6)]
        dya = dya_ref[...]
        sa, dsa = dsilu(za_ref[...])
        doa_ref[...] = dya * sa
        dza_ref[...] = (dya * oa_ref[...] * dsa).astype(BF)

        dyb = dyb_ref[...]
        ob = ob_ref[...]
        nw = nw_ref[layer:layer + 1, :]
        sb, dsb = dsilu(zb_ref[...])
        rn = lax.rsqrt(jnp.mean(ob * ob, axis=-1, keepdims=True) + RMS_EPS)
        on = ob * rn
        dzb_ref[...] = (dyb * on * nw * dsb).astype(BF)
        don_w = dyb * sb
        dnw_ref[...] = jnp.sum(don_w * on, axis=0, keepdims=True)
        don = don_w * nw
        dob_ref[...] = rn * (don - on * jnp.mean(don * on, axis=-1, keepdims=True))

        dyc = dyc_ref[...]
        pre, post, u = pre_ref[...], post_ref[...], u_ref[...]
        sc, dsc = dsilu(zc_ref[...])
        pu = pre * u
        pu1, pu2 = _shift_down(pu, 1), _shift_down(pu, 2)
        conv = cw_ref[2:3, :] * pu + cw_ref[1:2, :] * pu1 + cw_ref[0:1, :] * pu2
        dzc_ref[...] = (dyc * post * conv * dsc).astype(BF)
        dpost_ref[...] = (dyc * conv * sc).astype(BF)
        dconv = dyc * post * sc
        dcw_ref[0:1, :] = jnp.sum(dconv * pu2, axis=0, keepdims=True)
        dcw_ref[1:2, :] = jnp.sum(dconv * pu1, axis=0, keepdims=True)
        dcw_ref[2:3, :] = jnp.sum(dconv * pu, axis=0, keepdims=True)
        dpu = cw_ref[2:3, :] * dconv + cw_ref[1:2, :] * _shift_up(dconv, 1) + cw_ref[0:1, :] * _shift_up(dconv, 2)
        dpre_ref[...] = (dpu * u).astype(BF)
        du_ref[...] = (dpu * pre).astype(BF)
        lane0 = pl.program_id(0) * 128
        _put_columns([stage.at[n] for n in range(6)], dproj_ref, [c + lane0 for c in firsts], sems)

    col = lambda base: pl.BlockSpec((S, 128), lambda p, base=base: (0, base + p))
    f32 = jax.ShapeDtypeStruct((S, WIDTH), F32)
    hbm = pl.BlockSpec(memory_space=pl.ANY)
    return pl.pallas_call(
        body, name="branch_bwd",
        grid=(WIDTH // 128,),
        in_specs=[col(0), col(12), col(0), col(28), pl.BlockSpec(norm_w.shape, lambda p: (0, 0)),
                  col(32), col(36), col(40), col(44), pl.BlockSpec((None, None, 3, 128), lambda p: (p, layer, 0, 0)),
                  col(0), col(0), col(0), hbm],
        out_specs=[col(0), col(0), pl.BlockSpec((None, 1, 128), lambda p: (p, 0, 0)), pl.BlockSpec((None, 3, 128), lambda p: (p, 0, 0)), hbm],
        out_shape=[f32, f32, jax.ShapeDtypeStruct((WIDTH // 128, 1, 128), F32), jax.ShapeDtypeStruct((WIDTH // 128, 3, 128), F32),
                   jax.ShapeDtypeStruct(dproj.shape, dproj.dtype)],
        scratch_shapes=[pltpu.VMEM((6, S, 128), BF), pltpu.SemaphoreType.DMA((6,))],
        input_output_aliases={13: 4},
        compiler_params=_params(dimension_semantics=("arbitrary",)),
    )(o_a, proj, o_b, proj, norm_w, proj, proj, proj, proj, conv_w, dy_a, dy_b, dy_c, dproj)


def _branch_proj(y_refs, wb_ref):
    return [_dot(y_refs[i][...], wb_ref[i]) for i in range(3)]


def _merge_fwd(x, mod, proj, ys, wb, wo, ln_g, ln_b, layer, job=None, tm=512):
    S, D = x.shape
    tm = min(tm, S)

    def body(x_ref, mod_ref, ga_ref, gb_ref, gc_ref, ya_ref, yb_ref, yc_ref, wb_ref, wo_ref, g_ref, b_ref, xo_ref, mg_ref, y_ref):
        ps = _branch_proj((ya_ref, yb_ref, yc_ref), wb_ref)
        merged = _sigmoid(ga_ref[...]) * ps[0] + _sigmoid(gb_ref[...]) * ps[1] + _sigmoid(gc_ref[...]) * ps[2]
        mb = merged.astype(BF)
        mg_ref[...] = mb
        y = _dot(mb, wo_ref[...].reshape(D, D))
        y_ref[...] = y
        r = ALPHA * x_ref[...] + (1.0 + mod_ref[:, 2 * D:3 * D]) * y
        xn, _ = _standardize(r)
        xo_ref[...] = xn * g_ref[layer:layer + 1, :] + b_ref[layer:layer + 1, :]

    row = lambda w, c=0: pl.BlockSpec((tm, w), lambda i, c=c: (i, c))
    whole = lambda a: pl.BlockSpec(a.shape, lambda i, n=a.ndim: (0,) * n)
    return _hosted(
        body, job, (x, mod, proj, proj, proj, *ys, wb, wo, ln_g, ln_b), name="merge_fwd",
        grid=(S // tm,),
        in_specs=[row(D), whole(mod), row(D, 6), row(D, 7), row(D, 8), row(WIDTH), row(WIDTH), row(WIDTH), whole(wb),
                  whole(wo), whole(ln_g), whole(ln_b)],
        out_specs=[row(D), row(D), row(D)],
        out_shape=[jax.ShapeDtypeStruct((S, D), F32), jax.ShapeDtypeStruct((S, D), BF), jax.ShapeDtypeStruct((S, D), F32)],
        semantics=("parallel",))


def _merge_bwd(dxo, x, y, merged, mod, proj, ys, wb, wo, ln_g, layer, tm=256):
    S, D = x.shape
    tm = min(tm, S)
    steps = S // tm
    quarter = D // N_CHIPS

    def body(dxo_ref, x_ref, y_ref, mg_ref, mod_ref, ga_ref, gb_ref, gc_ref, ya_ref, yb_ref, yc_ref, wb_ref, wo_ref, g_ref,
             dxr_ref, dg_ref, dya_ref, dyb_ref, dyc_ref, dlg_ref, dlb_ref, dgt_ref, gbr_ref, gout_ref, acc_br, acc_out):
        @pl.when(pl.program_id(0) == 0)
        def _():
            dlg_ref[...] = jnp.zeros_like(dlg_ref)
            dlb_ref[...] = jnp.zeros_like(dlb_ref)
            dgt_ref[...] = jnp.zeros_like(dgt_ref)
            acc_br[...] = jnp.zeros_like(acc_br)
            acc_out[...] = jnp.zeros_like(acc_out)

        gate1 = 1.0 + mod_ref[:, 2 * D:3 * D]
        yv = y_ref[...]
        xn, rstd = _standardize(ALPHA * x_ref[...] + gate1 * yv)
        dxo = dxo_ref[...]
        dlg_ref[...] += jnp.sum(dxo * xn, axis=0, keepdims=True)
        dlb_ref[...] += jnp.sum(dxo, axis=0, keepdims=True)
        dr = _standardize_bwd(dxo * g_ref[layer:layer + 1, :], xn, rstd)
        dxr_ref[...] = ALPHA * dr
        dgt_ref[...] += jnp.sum(dr * yv, axis=0, keepdims=True)
        dyb = (gate1 * dr).astype(BF)
        acc_out[...] += _dot_tn(mg_ref[...], dyb)
        dmerged = _dot_nt(dyb, wo_ref[...].reshape(D, D))
        y_refs = (ya_ref, yb_ref, yc_ref)
        ps = _branch_proj(y_refs, wb_ref)
        for i, (gate_ref, out_ref) in enumerate(((ga_ref, dya_ref), (gb_ref, dyb_ref), (gc_ref, dyc_ref))):
            sg = _sigmoid(gate_ref[...])
            dg_ref[:, i * D:(i + 1) * D] = (dmerged * ps[i] * sg * (1.0 - sg)).astype(BF)
            dp = (dmerged * sg).astype(BF)
            acc_br[i] += _dot_tn(y_refs[i][...], dp)
            out_ref[...] = _dot_nt(dp, wb_ref[i])

        @pl.when(pl.program_id(0) == steps - 1)
        def _():
            for j in range(N_CHIPS):
                gout_ref[j] = acc_out[j * quarter:(j + 1) * quarter, :].astype(BF)
                for i in range(3):
                    gbr_ref[j, i] = acc_br[i, :, j * quarter:(j + 1) * quarter].astype(BF)

    row = lambda w, c=0: pl.BlockSpec((tm, w), lambda i, c=c: (i, c))
    whole = lambda a: pl.BlockSpec(a.shape, lambda i, n=len(a.shape): (0,) * n)
    vec = pl.BlockSpec((1, D), lambda i: (0, 0))
    sd = jax.ShapeDtypeStruct
    g_br, g_out = sd((N_CHIPS, 3, WIDTH, quarter), BF), sd((N_CHIPS, quarter, D), BF)
    return pl.pallas_call(
        body, name="merge_bwd",
        grid=(steps,),
        in_specs=[row(D), row(D), row(D), row(D), whole(mod), row(D, 6), row(D, 7), row(D, 8), row(WIDTH), row(WIDTH), row(WIDTH),
                  whole(wb), whole(wo), whole(ln_g)],
        out_specs=[row(D), row(3 * D, IN_COLS // (3 * D) - 1), row(WIDTH), row(WIDTH), row(WIDTH), vec, vec, vec, whole(g_br), whole(g_out)],
        out_shape=[sd((S, D), F32), sd((S, IN_COLS), BF), sd((S, WIDTH), F32), sd((S, WIDTH), F32), sd((S, WIDTH), F32),
                   sd((1, D), F32), sd((1, D), F32), sd((1, D), F32), g_br, g_out],
        scratch_shapes=[pltpu.VMEM((3, WIDTH, D), F32), pltpu.VMEM((D, D), F32)],
        compiler_params=_params(dimension_semantics=("arbitrary",)),
    )(dxo, x, y, merged, mod, proj, proj, proj, *ys, wb, wo, ln_g)


def _loss_head(x, target, tm=512):
    S, D = x.shape
    tm = min(tm, S)

    def body(x_ref, t_ref, dx_ref, loss_ref):
        @pl.when(pl.program_id(0) == 0)
        def _():
            loss_ref[...] = jnp.zeros_like(loss_ref)

        err = x_ref[...] - t_ref[...]
        dx_ref[...] = err * (1.0 / D)
        loss_ref[...] += 0.5 * jnp.sum(jnp.mean(err * err, axis=-1, keepdims=True))

    row = pl.BlockSpec((tm, D), lambda i: (i, 0))
    return pl.pallas_call(
        body, name="loss_head",
        grid=(S // tm,),
        in_specs=[row, row],
        out_specs=[row, pl.BlockSpec((8, 128), lambda i: (0, 0))],
        out_shape=[jax.ShapeDtypeStruct((S, D), F32), jax.ShapeDtypeStruct((8, 128), F32)],
        compiler_params=_params(dimension_semantics=("arbitrary",)),
    )(x, target)


def _proj_bwd(dproj, wgs, x, mod, dx_res, job=None, tm=512, tk=768):
    S, D = x.shape
    tm = min(tm, S)
    n = len(wgs)
    w_args = list(wgs)
    if n == 1:
        per = wgs[0].shape[-1] // tk
        w_specs = [pl.BlockSpec((None, D, tk), lambda k, i: (k // per, 0, k % per))]
    else:
        assert wgs[0].shape[-1] == tk
        w_specs = [pl.BlockSpec((None, D, tk), lambda k, i, c=c: (jnp.minimum((k + n - 1 - c) // n, N_CHIPS - 1), 0, 0))
                   for c in range(n)]
    nk = IN_COLS // tk

    def body(dp_ref, *rest):
        w_refs, (x_ref, mod_ref, dxr_ref, dx_ref, dsh_ref, dsc_ref, acc) = rest[:n], rest[n:]
        k, i = pl.program_id(0), pl.program_id(1)
        mine = pl.ds(pl.multiple_of(i * tm, tm), tm)

        @pl.when((i == 0) & (k == 0))
        def _():
            dsh_ref[...] = jnp.zeros_like(dsh_ref)
            dsc_ref[...] = jnp.zeros_like(dsc_ref)

        @pl.when(k == 0)
        def _():
            acc[mine, :] = jnp.zeros((tm, D), F32)

        for c in range(n):
            @pl.when(k % n == c)
            def _(c=c):
                acc[mine, :] += _dot_nt(dp_ref[...], w_refs[c][...])

        @pl.when(k == nk - 1)
        def _():
            dh = acc[mine, :]
            xs, rstd = _standardize(x_ref[...])
            dsh_ref[...] += jnp.sum(dh, axis=0, keepdims=True)
            dsc_ref[...] += jnp.sum(dh * xs, axis=0, keepdims=True)
            dx_ref[...] = _standardize_bwd(dh * (1.0 + mod_ref[:, D:2 * D]), xs, rstd) + dxr_ref[...]

    row = pl.BlockSpec((tm, D), lambda k, i: (jnp.where(k == nk - 1, i, 0), 0))
    vec = pl.BlockSpec((1, D), lambda k, i: (0, 0))
    return _hosted(
        body, job, (dproj, *w_args, x, mod, dx_res), name="proj_bwd",
        grid=(nk, S // tm),
        in_specs=[pl.BlockSpec((tm, tk), lambda k, i: (i, k))] + w_specs + [row, pl.BlockSpec((1, 3 * D), lambda k, i: (0, 0)), row],
        out_specs=[row, vec, vec],
        out_shape=[jax.ShapeDtypeStruct((S, D), F32), jax.ShapeDtypeStruct((1, D), F32), jax.ShapeDtypeStruct((1, D), F32)],
        scratch_shapes=[pltpu.VMEM((S, D), F32)],
        semantics=("arbitrary", "arbitrary"))


def _grad_w_in(h, dproj):
    S, D = h.shape
    shard = IN_COLS // N_CHIPS

    def body(h_ref, d_ref, o_ref):
        o_ref[...] = _dot_tn(h_ref[...], d_ref[...]).astype(BF)

    return pl.pallas_call(
        body, name="grad_w_in",
        grid=(N_CHIPS,),
        in_specs=[pl.BlockSpec((S, D), lambda n: (0, 0)), pl.BlockSpec((S, shard), lambda n: (0, n))],
        out_specs=pl.BlockSpec((None, D, shard), lambda n: (n, 0, 0)),
        out_shape=jax.ShapeDtypeStruct((N_CHIPS, D, shard), BF),
        compiler_params=_params(dimension_semantics=("parallel",)),
    )(h, dproj)


def _all_gather8(x, name):
    R, N = x.shape

    def body(x_ref, out_ref, send_sems, recv_sems):
        mx, my, mc = lax.axis_index("x"), lax.axis_index("y"), lax.axis_index("c")
        me = 4 * mx + 2 * my + mc
        out_ref[me] = x_ref[...]
        copies = []
        for k in range(1, N_DEV):
            peer = (_flip(mx, k & 4), _flip(my, k & 2), _flip(mc, k & 1))
            cp = pltpu.make_async_remote_copy(src_ref=x_ref, dst_ref=out_ref.at[me], send_sem=send_sems.at[k - 1],
                                              recv_sem=recv_sems.at[k - 1], device_id=peer, device_id_type=MESH)
            cp.start()
            copies.append(cp)
        for cp in copies:
            cp.wait()

    return pl.pallas_call(
        body, name=name,
        in_specs=[pl.BlockSpec(memory_space=pltpu.VMEM)],
        out_specs=pl.BlockSpec(memory_space=pltpu.VMEM),
        out_shape=jax.ShapeDtypeStruct((N_DEV, R, N), F32),
        scratch_shapes=[pltpu.SemaphoreType.DMA((N_DEV - 1,)), pltpu.SemaphoreType.DMA((N_DEV - 1,))],
        compiler_params=_params(),
    )(x)


def _rows2d(a):
    return a.reshape(-1, a.shape[-1])


def _tile_rows(rows, cols, n_arrays):
    budget = (24 << 20) // (n_arrays * 2 * 4 * cols)
    if rows <= budget:
        return rows
    tm = 8
    for cand in range(8, budget + 1, 8):
        if rows % cand == 0:
            tm = cand
    return tm


SUM_ROWS = 256


def _sum_cores(g, sent, where):
    chips, lead, _, r, cols = g.shape
    tr = min(r, SUM_ROWS)

    def body(where_ref, g_ref, s_ref, out_ref):
        out_ref[...] = (g_ref[...].astype(F32) + s_ref[...].astype(F32)).astype(BF)

    spec = pl.BlockSpec((None, tr, cols), lambda i, j, where_ref: (i, j, 0))
    out = pl.pallas_call(
        body, name="sum_cores",
        grid_spec=pltpu.PrefetchScalarGridSpec(
            num_scalar_prefetch=1, grid=(chips * lead, r // tr),
            in_specs=[pl.BlockSpec((None, None, tr, cols), lambda i, j, where_ref: (i, where_ref[1], j, 0)), spec],
            out_specs=spec),
        out_shape=jax.ShapeDtypeStruct((chips * lead, r, cols), BF),
        compiler_params=_params(dimension_semantics=("parallel", "parallel")),
    )(where, g.reshape(chips * lead, 2, r, cols), sent.reshape(chips * lead, r, cols))
    return out.reshape(chips, lead, r, cols)


def _sum_chips(sums, got, where):
    _, lead, r, cols = sums.shape
    tr = min(r, SUM_ROWS)

    def body(where_ref, s_ref, g_ref, out_ref):
        out_ref[...] = ((s_ref[...].astype(F32) + g_ref[0].astype(F32)) + g_ref[1].astype(F32)) + g_ref[2].astype(F32)

    return pl.pallas_call(
        body, name="sum_chips",
        grid_spec=pltpu.PrefetchScalarGridSpec(
            num_scalar_prefetch=1, grid=(lead, r // tr),
            in_specs=[pl.BlockSpec((None, None, tr, cols), lambda i, j, where_ref: (where_ref[0], i, j, 0)),
                      pl.BlockSpec((N_CHIPS - 1, None, tr, cols), lambda i, j, where_ref: (0, i, j, 0))],
            out_specs=pl.BlockSpec((None, None, tr, cols), lambda i, j, where_ref: (i, where_ref[1], j, 0))),
        out_shape=jax.ShapeDtypeStruct((lead, 2, r, cols), F32),
        compiler_params=_params(dimension_semantics=("parallel", "parallel")),
    )(where, sums, got)


def _adamw(w, m, v, groups, name):
    shape = w.shape
    w2, m2, v2 = _rows2d(w), _rows2d(m), _rows2d(v)
    rows, cols = w2.shape
    ng = len(groups)
    n = len(groups[0])
    slab = rows // ng
    gs = [_rows2d(g) for grp in groups for g in grp]
    tm = _tile_rows(slab, cols, 7 + n)
    tiles = slab // tm
    c1 = 1.0 / (1.0 - ADAM_B1 ** ADAM_STEP)
    c2 = 1.0 / (1.0 - ADAM_B2 ** ADAM_STEP)

    def body(*refs):
        w_ref, m_ref, v_ref = refs[:3]
        g_refs = refs[3:3 + ng * n]
        go_ref, d_ref, mo_ref, vo_ref = refs[3 + ng * n:]
        which = pl.program_id(0)
        for s in range(ng):
            @pl.when(which == s)
            def _(s=s):
                g = g_refs[s * n][...]
                for r in g_refs[s * n + 1:(s + 1) * n]:
                    g = g + r[...]
                mn = ADAM_B1 * m_ref[...] + (1.0 - ADAM_B1) * g
                vn = ADAM_B2 * v_ref[...] + (1.0 - ADAM_B2) * (g * g)
                go_ref[...] = g
                mo_ref[...] = mn
                vo_ref[...] = vn
                d_ref[...] = -ADAM_LR * ((mn * c1) / (jnp.sqrt(vn * c2) + ADAM_EPS) + ADAM_WD * w_ref[...])

    spec = pl.BlockSpec((tm, cols), lambda s, i: (s * tiles + i, 0))
    g_specs = [pl.BlockSpec((tm, cols), lambda s, i, k=k: (jnp.where(s == k, i, jnp.where(s < k, 0, tiles - 1)), 0))
               for k in range(ng) for _ in range(n)]
    outs = pl.pallas_call(
        body, name=name,
        grid=(ng, tiles),
        in_specs=[spec] * 3 + g_specs,
        out_specs=[spec] * 4,
        out_shape=[jax.ShapeDtypeStruct((rows, cols), F32)] * 4,
        compiler_params=_params(dimension_semantics=("arbitrary", "arbitrary")),
    )(w2, m2, v2, *gs)
    return [o.reshape(shape) for o in outs]


def _lower_bounds(r0, r1):
    top = jnp.maximum(r0, r1)
    e0, e1 = jnp.exp(r0 - top), jnp.exp(r1 - top)
    p0, p1 = e0 / (e0 + e1), e1 / (e0 + e1)
    return p0 - p0, (p0 + p1) - p0


def _lbs_fwd(lb_raw):
    def body(lb_ref, out_ref):
        l0, l1 = _lower_bounds(lb_ref[0:1, :], lb_ref[1:2, :])
        out_ref[0:1, :] = l0
        out_ref[1:2, :] = l1

    return pl.pallas_call(body, name="lower_bounds", out_shape=jax.ShapeDtypeStruct(lb_raw.shape, F32), compiler_params=_params())(lb_raw)


def _mod_rows(c_all, w_mod, tn=768):
    _, D, cols = w_mod.shape

    def body(c_ref, w_ref, out_ref):
        out_ref[...] = _dot(c_ref[...].astype(BF), w_ref[...].astype(BF))

    return pl.pallas_call(
        body, name="mod_rows",
        grid=(DEPTH,),
        in_specs=[pl.BlockSpec((N_DEV, D), lambda l: (0, 0)), pl.BlockSpec((None, D, cols), lambda l: (l, 0, 0))],
        out_specs=pl.BlockSpec((N_DEV, cols), lambda l: (0, l)),
        out_shape=jax.ShapeDtypeStruct((N_DEV, DEPTH * cols), F32),
        compiler_params=_params(dimension_semantics=("parallel",)),
    )(c_all, w_mod)


def _grad_w_mod(c_all, dmod_cols):
    D = c_all.shape[1]
    cols = dmod_cols.shape[-1]

    def body(c_ref, d_ref, out_ref):
        out_ref[...] = _dot_tn(c_ref[...].astype(BF), d_ref[...].astype(BF))

    return pl.pallas_call(
        body, name="grad_w_mod",
        grid=(DEPTH,),
        in_specs=[pl.BlockSpec((N_DEV, D), lambda l: (0, 0)), pl.BlockSpec((None, N_DEV, cols), lambda l: (l, 0, 0))],
        out_specs=pl.BlockSpec((None, D, cols), lambda l: (l, 0, 0)),
        out_shape=jax.ShapeDtypeStruct((DEPTH, D, cols), F32),
        compiler_params=_params(dimension_semantics=("parallel",)),
    )(c_all, dmod_cols)


def _sum_devices(parts):
    _, R, N = parts.shape

    def body(p_ref, out_ref):
        acc = p_ref[0]
        for d in range(1, N_DEV):
            acc = acc + p_ref[d]
        out_ref[...] = acc

    return pl.pallas_call(body, name="sum_devices", out_shape=jax.ShapeDtypeStruct((R, N), F32), compiler_params=_params())(parts)


def _lbs_bwd(lb_raw, dl):
    def body(lb_ref, dl_ref, out_ref):
        _, pull = jax.vjp(_lower_bounds, lb_ref[0:1, :], lb_ref[1:2, :])
        d0, d1 = pull((dl_ref[0:1, :], dl_ref[1:2, :]))
        out_ref[0:1, :] = d0
        out_ref[1:2, :] = d1

    return pl.pallas_call(body, name="lower_bounds_bwd", out_shape=jax.ShapeDtypeStruct(lb_raw.shape, F32), compiler_params=_params())(lb_raw, dl)


def kernel(x, c, w_mod, b_mod, w_in, conv_w, hgrn_norm_w, lower_bounds, w_branch, w_out, ln_g, ln_b, loss_target, m_w_mod, m_b_mod, m_w_in, m_conv_w, m_hgrn_norm_w, m_lower_bounds, m_w_branch, m_w_out, m_ln_g, m_ln_b, v_w_mod, v_b_mod, v_w_in, v_conv_w, v_hgrn_norm_w, v_lower_bounds, v_w_branch, v_w_out, v_ln_g, v_ln_b):
    D = D_MODEL
    x0 = x[0]
    target = loss_target[0]
    S = x0.shape[0]
    mx, my, mc = lax.axis_index("x"), lax.axis_index("y"), lax.axis_index("c")
    chip = 2 * mx + my
    me = 2 * chip + mc
    mod_cols = 3 * D // N_CHIPS

    plan = _Plan(w_in.astype(BF), w_branch.astype(BF), w_out.astype(BF), chip, mc)
    n_conv = DEPTH * 3 * (WIDTH // N_CHIPS)
    first = jnp.concatenate([c, conv_w.reshape(1, n_conv), jnp.zeros((1, 2 * D - D - n_conv), F32)], axis=1)
    first = plan.first(first.reshape(8, 2 * D // 8)).reshape(N_DEV, 2 * D)

    c_all = first[:, :D]
    conv_all = first[:, D:D + n_conv].reshape(N_DEV, DEPTH, 3, WIDTH // N_CHIPS)[0::2]
    mod_part = _all_gather8(_mod_rows(c_all, w_mod), "gather_mod")[0::2]
    mod_part = lax.dynamic_index_in_dim(mod_part, me, axis=1, keepdims=False).reshape(N_CHIPS, DEPTH, mod_cols)
    mods = [(mod_part[:, l].reshape(1, 3 * D) + b_mod[l][None, :]) for l in range(DEPTH)]
    lbs = _lbs_fwd(lower_bounds).reshape(DEPTH, WIDTH // 128, 1, 128)
    loss_blk, dx, small = _local_step(x0, target, mods, lbs, conv_all, hgrn_norm_w, ln_g, ln_b, plan)

    n_mod, n_nw, n_lb, n_ln, n_cw = DEPTH * 3 * D, DEPTH * 128, DEPTH * WIDTH, DEPTH * D, DEPTH * 3 * WIDTH
    row = jnp.concatenate(
        [jnp.concatenate([small[l][0], small[l][1], small[l][2]], axis=1) for l in range(DEPTH)]
        + [jnp.sum(small[l][3], axis=0) for l in range(DEPTH)]
        + [small[l][4].reshape(1, WIDTH) for l in range(DEPTH)]
        + [small[l][5] for l in range(DEPTH)] + [small[l][6] for l in range(DEPTH)]
        + [jnp.transpose(small[l][7], (1, 0, 2)).reshape(1, 3 * WIDTH) for l in range(DEPTH)]
        + [loss_blk[0:1, :]], axis=1)
    n_row = row.shape[1]
    fold = -(-n_row // (8 * 128)) * 128
    rows = jnp.concatenate([row, jnp.zeros((1, 8 * fold - n_row), F32)], axis=1).reshape(8, fold)

    whole, gathered = plan.finish(rows)
    grads = {kind: [[whole[(kind, l)]] for l in range(DEPTH)] for kind in ("in", "br", "out")}

    off_nw = n_mod
    off_lb = off_nw + n_nw
    off_lng = off_lb + n_lb
    off_lnb = off_lng + n_ln
    off_cw = off_lnb + n_ln
    off_loss = off_cw + n_cw
    total = _sum_devices(gathered).reshape(1, 8 * fold)
    gathered = gathered.reshape(N_DEV, 1, 8 * fold)
    d_lower = _lbs_bwd(lower_bounds, total[0, off_lb:off_lng].reshape(DEPTH, WIDTH))
    loss = total[0, off_loss]
    d_b_mod = total[0, :n_mod].reshape(DEPTH, 3 * D)
    d_norm_w = total[0, off_nw:off_lb].reshape(DEPTH, 128)
    d_ln_g = total[0, off_lng:off_lnb].reshape(DEPTH, D)
    d_ln_b = total[0, off_lnb:off_cw].reshape(DEPTH, D)
    d_conv = total[0, off_cw:off_loss].reshape(DEPTH, 3, N_CHIPS, WIDTH // N_CHIPS)
    d_conv = lax.dynamic_index_in_dim(d_conv, chip, axis=2, keepdims=False)
    dmod_all = gathered[:, 0, :n_mod].reshape(N_DEV, DEPTH, N_CHIPS, mod_cols)
    dmod_cols = jnp.transpose(lax.dynamic_index_in_dim(dmod_all, chip, axis=2, keepdims=False), (1, 0, 2))
    d_w_mod = _grad_w_mod(c_all, dmod_cols)

    res = {}
    res["w_mod"] = _adamw(w_mod, m_w_mod, v_w_mod, [[d_w_mod]], "adamw_w_mod")
    res["b_mod"] = _adamw(b_mod, m_b_mod, v_b_mod, [[d_b_mod]], "adamw_b_mod")
    res["w_in"] = _adamw(w_in, m_w_in, v_w_in, grads["in"], "adamw_w_in")
    res["conv_w"] = _adamw(conv_w, m_conv_w, v_conv_w, [[d_conv]], "adamw_conv_w")
    res["hgrn_norm_w"] = _adamw(hgrn_norm_w, m_hgrn_norm_w, v_hgrn_norm_w, [[d_norm_w]], "adamw_norm_w")
    res["lower_bounds"] = _adamw(lower_bounds, m_lower_bounds, v_lower_bounds, [[d_lower]], "adamw_lower_bounds")
    res["w_branch"] = _adamw(w_branch, m_w_branch, v_w_branch, grads["br"], "adamw_w_branch")
    res["w_out"] = _adamw(w_out, m_w_out, v_w_out, grads["out"], "adamw_w_out")
    res["ln_g"] = _adamw(ln_g, m_ln_g, v_ln_g, [[d_ln_g]], "adamw_ln_g")
    res["ln_b"] = _adamw(ln_b, m_ln_b, v_ln_b, [[d_ln_b]], "adamw_ln_b")
    names = ["w_mod", "b_mod", "w_in", "conv_w", "hgrn_norm_w", "lower_bounds", "w_branch", "w_out", "ln_g", "ln_b"]
    return (loss, dx[None], *[res[n][0] for n in names], *[res[n][1] for n in names],
            *[res[n][2] for n in names], *[res[n][3] for n in names])


class _Plan:
    FIRST_CHUNKS = 3
    WINDOWS = ((0, 1152), (1152, 640), (1792, 512))

    def __init__(self, w_in, w_br, w_out, chip, core):
        self.local = {"in": w_in, "br": w_br, "out": w_out}
        self.chip, self.where = chip, jnp.stack([chip, core]).astype(jnp.int32)
        self.gathered, self.partial, self.grads, self.chip_sums, self.scattered, self.pending = {}, {}, {}, {}, {}, {}

    def chunks(self, l):
        return self.FIRST_CHUNKS if l == 0 else 1

    def w_in(self, l):
        return [self.gathered[("in", l, c)] for c in range(self.chunks(l))]

    def _shard(self, key):
        mine = self.local[key[0]][key[1]]
        if key[0] == "in":
            cols = mine.shape[-1] // self.chunks(key[1])
            mine = mine[:, key[2] * cols:(key[2] + 1) * cols]
        return mine

    def _slab(self, key):
        mine = _halves(self._shard(key))
        return lax.dynamic_update_slice(lax.empty((N_CHIPS,) + mine.shape, mine.dtype), mine[None], (self.chip, 0, 0, 0, 0))

    def _gather(self, keys):
        return ("gather", keys), _gather_job([self._slab(key) for key in keys])

    def _gather_window(self, key, n):
        slab = self._slab(key) if n == 0 else self.partial[key]
        return ("gather" if n == len(self.WINDOWS) - 1 else "gather_part", [key]), _gather_job([slab], self.WINDOWS[n])

    def _to_sibling(self, keys):
        return ("to_sibling", keys), _to_sibling_job([_halves(self.grads[key], 1) for key in keys])

    def _scatter(self, keys):
        return ("scatter", keys), _scatter_job([self.chip_sums[key] for key in keys])

    def job(self, stage, l, c=0):
        parts = []
        if stage == "proj_fwd":
            parts = [self._gather([("in", l, c + 1)] if c + 1 < self.chunks(l) else [("br", l), ("out", l)])]
        elif stage == "attn_fwd" and l + 1 < DEPTH:
            parts = [self._gather_window(("in", l + 1, 0), 0)]
        elif stage == "hgrn_fwd" and l + 1 < DEPTH:
            parts = [self._gather_window(("in", l + 1, 0), 1)]
        elif stage == "merge_fwd" and l + 1 < DEPTH:
            parts = [self._gather_window(("in", l + 1, 0), 2)]
        elif stage == "attn_bwd":
            parts = [self._to_sibling([("out", l), ("br", l)])] + ([self._scatter([("in", l + 1)])] if l + 1 < DEPTH else [])
        elif stage == "hgrn_bwd":
            parts = [self._scatter([("out", l), ("br", l)])]
        elif stage == "proj_bwd":
            parts = [self._to_sibling([("in", l)])] if l else [self._scatter([("in", 0)])]
        self.pending[(stage, l, c)] = [(tag, len(job.outs)) for tag, job in parts]
        return _join_jobs([job for _, job in parts])

    def done(self, stage, l, outs, c=0):
        if outs is None:
            return
        at = 0
        for (what, keys), n_outs in self.pending[(stage, l, c)]:
            mine, at = outs[at:at + n_outs], at + n_outs
            for n, key in enumerate(keys):
                if what == "gather":
                    self.gathered[key] = mine[n].reshape((N_CHIPS,) + self._shard(key).shape)
                elif what == "gather_part":
                    self.partial[key] = mine[n]
                elif what == "to_sibling":
                    self.chip_sums[key] = _sum_cores(_halves(self.grads[key], 1), mine[n], self.where)
                else:
                    self.scattered[key] = mine[n]

    def first(self, rows):
        tag, job = self._gather([("in", 0, 0)])
        self.pending[("first", 0, 0)] = [(tag, len(job.outs))]
        outs = _run_job(_join_jobs([job, _gather8_job(rows)]), "gather_first")
        self.done("first", 0, outs[:-1])
        return outs[-1]

    def took(self, key, grad):
        self.grads[key] = grad
        if key == ("in", 0):
            tag, job = self._to_sibling([key])
            self.pending[("took", 0, 0)] = [(tag, len(job.outs))]
            self.done("took", 0, _run_job(job, "to_sibling_last"))

    def finish(self, rows):
        keys = [(kind, l) for kind in ("in", "br", "out") for l in range(DEPTH)]
        halves = [_sum_chips(self.chip_sums[key], self.scattered[key], self.where) for key in keys]
        outs = _run_job(_join_jobs([_place_job(halves), _gather8_job(rows)]), "place_halves")
        return {key: w.reshape(self.grads[key].shape[1:]) for key, w in zip(keys, outs[:-1])}, outs[-1]


def _local_step(x0, target, mods, lbs, conv_all, hgrn_norm_w, ln_g, ln_b, plan):
    D = D_MODEL
    after, before = _attn_consts()
    hg_consts = _hgrn_consts()

    saved = []
    xl = x0
    for l in range(DEPTH):
        n = plan.chunks(l)
        (proj, h), got = _proj_fwd(xl, mods[l], plan.gathered[("in", l, 0)], plan.job("proj_fwd", l, 0), (0, n))
        plan.done("proj_fwd", l, got, 0)
        for c in range(1, n):
            (proj,), got = _proj_cols(h, plan.gathered[("in", l, c)], (c, n), proj, plan.job("proj_fwd", l, c))
            plan.done("proj_fwd", l, got, c)
        (o_a, tot), got = _attn_fwd(proj, after, plan.job("attn_fwd", l))
        plan.done("attn_fwd", l, got)
        (o_b, states), got = _hgrn_fwd(proj, lbs[l], hg_consts, plan.job("hgrn_fwd", l))
        plan.done("hgrn_fwd", l, got)
        ys = _branch_fwd(proj, o_a, o_b, hgrn_norm_w, conv_all, l)
        wb = jnp.concatenate(list(plan.gathered[("br", l)]), axis=-1)
        (x_next, merged, y), got = _merge_fwd(xl, mods[l], proj, ys, wb, plan.gathered[("out", l)], ln_g, ln_b, l, plan.job("merge_fwd", l))
        plan.done("merge_fwd", l, got)
        saved.append((xl, proj, h, o_a, tot, o_b, states, ys, merged, y, wb))
        xl = x_next
    dx, loss_blk = _loss_head(xl, target)

    small = [None] * DEPTH
    for l in reversed(range(DEPTH)):
        xin, proj, h, o_a, tot, o_b, states, ys, merged, y, wb = saved[l]
        dx_res, dproj, dy_a, dy_b, dy_c, dln_g, dln_b, dgate, g_br, g_out = _merge_bwd(
            dx, xin, y, merged, mods[l], proj, ys, wb, plan.gathered[("out", l)], ln_g, l)
        plan.took(("out", l), g_out)
        plan.took(("br", l), g_br)
        d_oa, d_ob, dnorm_w, dconv_w, dproj = _branch_bwd(proj, o_a, o_b, hgrn_norm_w, conv_all, dy_a, dy_b, dy_c, dproj, l)
        (dproj,), got = _attn_bwd(proj, d_oa, tot, after, before, dproj, plan.job("attn_bwd", l))
        plan.done("attn_bwd", l, got)
        (dlb, dproj), got = _hgrn_bwd(proj, lbs[l], states, d_ob, hg_consts, dproj, plan.job("hgrn_bwd", l))
        plan.done("hgrn_bwd", l, got)
        plan.took(("in", l), _grad_w_in(h, dproj))
        (dx, dshift, dscale), got = _proj_bwd(dproj, plan.w_in(l), xin, mods[l], dx_res, plan.job("proj_bwd", l))
        plan.done("proj_bwd", l, got)
        small[l] = (dshift, dscale, dgate, dnorm_w, dlb, dln_g, dln_b, dconv_w)
    return loss_blk, dx, small
```

```python
import functools
import math

import numpy as np
import jax
import jax.numpy as jnp
from jax import lax
from jax.experimental import pallas as pl
from jax.experimental.pallas import tpu as pltpu

F32 = jnp.float32
BF = jnp.bfloat16
MESH = pl.DeviceIdType.MESH

DEPTH = 2
D_MODEL = 1024
WIDTH = 512
IN_COLS = 12 * WIDTH + 3 * D_MODEL
N_CHIPS = 4
N_DEV = 8
SB_BLOCK = 128
SB_HEAD_DIM = 64
HG_CHUNK = 128
HG_DIM = 128
LN_EPS = 1e-5
RMS_EPS = 1e-6
ALPHA = (2.0 * DEPTH) ** 0.25
ADAM_LR, ADAM_B1, ADAM_B2, ADAM_EPS, ADAM_WD, ADAM_STEP = 0.001, 0.9, 0.999, 1e-08, 0.01, 10
VMEM_LIMIT = 56 << 20


def _params(**kw):
    return pltpu.CompilerParams(vmem_limit_bytes=VMEM_LIMIT, **kw)


def _dot(a, b):
    return jnp.dot(a, b, preferred_element_type=F32)


def _dot_nt(a, b):
    return lax.dot_general(a, b, (((1,), (1,)), ((), ())), preferred_element_type=F32)


def _dot_tn(a, b):
    return lax.dot_general(a, b, (((0,), (0,)), ((), ())), preferred_element_type=F32)


def _sigmoid(x):
    return 1.0 / (1.0 + jnp.exp(-x))


def _silu(x):
    return x * _sigmoid(x)


def _standardize(x):
    mu = jnp.mean(x, axis=-1, keepdims=True)
    xc = x - mu
    var = jnp.mean(xc * xc, axis=-1, keepdims=True)
    rstd = lax.rsqrt(var + LN_EPS)
    return xc * rstd, rstd


def _standardize_bwd(dxs, xs, rstd):
    return rstd * (dxs - jnp.mean(dxs, axis=-1, keepdims=True) - xs * jnp.mean(dxs * xs, axis=-1, keepdims=True))


class _Job:
    def __init__(self, ins, outs, sems, make, alias=None):
        self.ins, self.outs, self.sems, self.make = list(ins), list(outs), list(sems), make
        self.alias = dict(alias or {})


def _join_jobs(jobs):
    jobs = [j for j in jobs if j is not None]
    if len(jobs) <= 1:
        return jobs[0] if jobs else None

    def make(ins, outs, sems):
        phases, i, o, s = [], 0, 0, 0
        for j in jobs:
            got = j.make(ins[i:i + len(j.ins)], outs[o:o + len(j.outs)], sems[s:s + len(j.sems)])
            i, o, s = i + len(j.ins), o + len(j.outs), s + len(j.sems)
            for n, phase in enumerate(got):
                if n == len(phases):
                    phases.append([])
                phases[n] += phase
        return phases

    alias, i, o = {}, 0, 0
    for j in jobs:
        alias.update({i + a: o + b for a, b in j.alias.items()})
        i, o = i + len(j.ins), o + len(j.outs)
    return _Job(sum([j.ins for j in jobs], []), sum([j.outs for j in jobs], []), sum([j.sems for j in jobs], []), make, alias)


def _flip(v, bit):
    return 1 - v if bit else v


def _halves(a, front=0):
    shape = a.shape
    lead = math.prod(shape[front:-2])
    return a.reshape(shape[:front] + (lead, 2, shape[-2] // 2, shape[-1]))


def _dma_sems(*shapes):
    return [pltpu.SemaphoreType.DMA(s) for s in shapes]


def _same(arrays):
    return [jax.ShapeDtypeStruct(a.shape, a.dtype) for a in arrays]


def _gather_job(slabs, window=None):
    n = len(slabs)
    cols = slice(None) if window is None else pl.ds(*window)

    def make(ins, outs, sems):
        send1, recv1, send2, recv2 = sems
        mx, my, mc = lax.axis_index("x"), lax.axis_index("y"), lax.axis_index("c")
        fetch, pass_on = [], []
        for a in range(n):
            ours = outs[a].at[2 * mx + my, :, mc, :, cols]
            for k in range(1, N_CHIPS):
                px, py = _flip(mx, k & 2), _flip(my, k & 1)
                fetch.append(pltpu.make_async_remote_copy(
                    src_ref=ours, dst_ref=ours, send_sem=send1.at[a, k - 1], recv_sem=recv1.at[a, k - 1],
                    device_id=(px, py, mc), device_id_type=MESH))
                theirs = outs[a].at[2 * px + py, :, mc, :, cols]
                pass_on.append(pltpu.make_async_remote_copy(
                    src_ref=theirs, dst_ref=theirs, send_sem=send2.at[a, k - 1], recv_sem=recv2.at[a, k - 1],
                    device_id=(mx, my, 1 - mc), device_id_type=MESH))
        return [fetch, pass_on]

    pairs = (n, N_CHIPS - 1)
    return _Job(slabs, _same(slabs), _dma_sems(pairs, pairs, pairs, pairs), make, {a: a for a in range(n)})


def _to_sibling_job(grads):
    n = len(grads)

    def make(ins, outs, sems):
        send_sems, recv_sems = sems
        mx, my, mc = lax.axis_index("x"), lax.axis_index("y"), lax.axis_index("c")
        return [[pltpu.make_async_remote_copy(
            src_ref=ins[a].at[:, :, 1 - mc], dst_ref=outs[a], send_sem=send_sems.at[a], recv_sem=recv_sems.at[a],
            device_id=(mx, my, 1 - mc), device_id_type=MESH) for a in range(n)]]

    outs = [jax.ShapeDtypeStruct(g.shape[:2] + g.shape[3:], g.dtype) for g in grads]
    return _Job(grads, outs, _dma_sems((n,), (n,)), make)


def _scatter_job(sums):
    n = len(sums)

    def make(ins, outs, sems):
        send_sems, recv_sems = sems
        mx, my, mc = lax.axis_index("x"), lax.axis_index("y"), lax.axis_index("c")
        copies = []
        for a in range(n):
            for k in range(1, N_CHIPS):
                px, py = _flip(mx, k & 2), _flip(my, k & 1)
                copies.append(pltpu.make_async_remote_copy(
                    src_ref=ins[a].at[2 * px + py], dst_ref=outs[a].at[k - 1], send_sem=send_sems.at[a, k - 1],
                    recv_sem=recv_sems.at[a, k - 1], device_id=(px, py, mc), device_id_type=MESH))
        return [copies]

    pairs = (n, N_CHIPS - 1)
    return _Job(sums, [jax.ShapeDtypeStruct((N_CHIPS - 1,) + s.shape[1:], s.dtype) for s in sums], _dma_sems(pairs, pairs), make)


def _place_job(wholes):
    n = len(wholes)

    def make(ins, outs, sems):
        send_sems, recv_sems = sems
        mx, my, mc = lax.axis_index("x"), lax.axis_index("y"), lax.axis_index("c")
        copies = []
        for a in range(n):
            here = outs[a].at[:, mc]
            copies.append(pltpu.make_async_remote_copy(src_ref=here, dst_ref=here, send_sem=send_sems.at[a], recv_sem=recv_sems.at[a],
                                                       device_id=(mx, my, 1 - mc), device_id_type=MESH))
        return [copies]

    return _Job(wholes, _same(wholes), _dma_sems((n,), (n,)), make, {a: a for a in range(n)})


def _gather8_job(x):
    def make(ins, outs, sems):
        local_sem, send_sems, recv_sems = sems
        mx, my, mc = lax.axis_index("x"), lax.axis_index("y"), lax.axis_index("c")
        here = outs[0].at[4 * mx + 2 * my + mc]
        copies = [pltpu.make_async_copy(ins[0], here, local_sem.at[0])]
        for k in range(1, N_DEV):
            peer = (_flip(mx, k & 4), _flip(my, k & 2), _flip(mc, k & 1))
            copies.append(pltpu.make_async_remote_copy(src_ref=ins[0], dst_ref=here, send_sem=send_sems.at[k - 1],
                                                       recv_sem=recv_sems.at[k - 1], device_id=peer, device_id_type=MESH))
        return [copies]

    return _Job([x], [jax.ShapeDtypeStruct((N_DEV,) + x.shape, x.dtype)], _dma_sems((1,), (N_DEV - 1,), (N_DEV - 1,)), make)


def _run_phases(phases, first=0):
    for n, phase in enumerate(phases):
        if n >= first:
            for cp in phase:
                cp.start()
        for cp in phase:
            cp.wait()


def _run_job(job, name):
    k_in, k_out = len(job.ins), len(job.outs)

    def body(*refs):
        _run_phases(job.make(refs[:k_in], refs[k_in:k_in + k_out], refs[k_in + k_out:]))

    hbm = pl.BlockSpec(memory_space=pl.ANY)
    return pl.pallas_call(body, name=name, in_specs=[hbm] * k_in, out_specs=[hbm] * k_out, out_shape=job.outs,
                          scratch_shapes=job.sems, input_output_aliases=job.alias, compiler_params=_params())(*job.ins)


def _hosted(body, job, args, *, name, grid, in_specs, out_specs, out_shape, scratch_shapes=(), semantics, aliases=None):
    in_specs, out_specs, out_shape, scratch = list(in_specs), list(out_specs), list(out_shape), list(scratch_shapes)
    aliases = dict(aliases or {})
    if job is None:
        outs = pl.pallas_call(body, name=name, grid=grid, in_specs=in_specs, out_specs=out_specs, out_shape=out_shape,
                              scratch_shapes=scratch, input_output_aliases=aliases,
                              compiler_params=_params(dimension_semantics=semantics))(*args)
        return list(outs), None
    n_in, n_out, n_scr, k_in, k_out = len(in_specs), len(out_specs), len(scratch), len(job.ins), len(job.outs)

    def wrapped(*refs):
        ins, rest = refs[:n_in], refs[n_in:]
        job_ins, rest = rest[:k_in], rest[k_in:]
        outs, rest = rest[:n_out], rest[n_out:]
        job_outs, rest = rest[:k_out], rest[k_out:]
        scr, sems = rest[:n_scr], rest[n_scr:]
        ids = [pl.program_id(a) for a in range(len(grid))]
        first = functools.reduce(jnp.logical_and, [i == 0 for i in ids])
        last = functools.reduce(jnp.logical_and, [i == g - 1 for i, g in zip(ids, grid)])

        @pl.when(first)
        def _():
            for cp in job.make(job_ins, job_outs, sems)[0]:
                cp.start()

        body(*ins, *outs, *scr)

        @pl.when(last)
        def _():
            _run_phases(job.make(job_ins, job_outs, sems), first=1)

    hbm = pl.BlockSpec(memory_space=pl.ANY)
    outs = pl.pallas_call(
        wrapped, name=name, grid=grid, in_specs=in_specs + [hbm] * k_in, out_specs=out_specs + [hbm] * k_out,
        out_shape=out_shape + job.outs, scratch_shapes=scratch + job.sems,
        input_output_aliases={**aliases, **{n_in + i: n_out + o for i, o in job.alias.items()}},
        compiler_params=_params(dimension_semantics=("arbitrary",) * len(grid)))(*args, *job.ins)
    return list(outs[:n_out]), list(outs[n_out:])


def _proj_fwd(x, mod, wg, job=None, chunk=(0, 1), tm=512):
    S, D = x.shape
    tm = min(tm, S)
    tn = wg.shape[-1]
    c, n = chunk

    rows = S // tm

    def body(x_ref, mod_ref, w_ref, proj_ref, h_ref, hs):
        mine = pl.ds(pl.multiple_of(pl.program_id(1) * tm, tm), tm)

        @pl.when(pl.program_id(0) == 0)
        def _():
            xs, _ = _standardize(x_ref[...])
            h = xs * (1.0 + mod_ref[:, D:2 * D]) + mod_ref[:, 0:D]
            hb = h.astype(BF)
            hs[mine, :] = hb
            h_ref[...] = hb

        proj_ref[...] = _dot(hs[mine, :], w_ref[...])

    once = lambda j, i: jnp.where(j == 0, i, rows - 1)
    return _hosted(
        body, job, (x, mod, wg), name="proj_fwd",
        grid=(N_CHIPS, rows),
        in_specs=[pl.BlockSpec((tm, D), lambda j, i: (once(j, i), 0)),
                  pl.BlockSpec((1, 3 * D), lambda j, i: (0, 0)),
                  pl.BlockSpec((None, D, tn), lambda j, i: (j, 0, 0))],
        out_specs=[pl.BlockSpec((tm, tn), lambda j, i: (i, j * n + c)),
                   pl.BlockSpec((tm, D), lambda j, i: (once(j, i), 0))],
        out_shape=[jax.ShapeDtypeStruct((S, IN_COLS), F32), jax.ShapeDtypeStruct((S, D), BF)],
        scratch_shapes=[pltpu.VMEM((S, D), BF)],
        semantics=("arbitrary", "arbitrary"))


def _proj_cols(h, wg, chunk, proj, job=None, tm=512):
    S, D = h.shape
    tm = min(tm, S)
    tn = wg.shape[-1]
    c, n = chunk

    def body(h_ref, w_ref, prev_ref, proj_ref):
        mine = pl.ds(pl.multiple_of(pl.program_id(1) * tm, tm), tm)
        proj_ref[...] = _dot(h_ref[mine, :], w_ref[...])

    in_specs = [pl.BlockSpec((S, D), lambda j, i: (0, 0)), pl.BlockSpec((None, D, tn), lambda j, i: (j, 0, 0)),
                pl.BlockSpec(memory_space=pl.ANY)]
    return _hosted(body, job, (h, wg, proj), name="proj_cols", grid=(N_CHIPS, S // tm), in_specs=in_specs,
                   out_specs=[pl.BlockSpec((tm, tn), lambda j, i: (i, j * n + c))],
                   out_shape=[jax.ShapeDtypeStruct(proj.shape, proj.dtype)],
                   semantics=("arbitrary", "arbitrary"), aliases={2: 0})


SB_ROWS = 256
SB_KEYS = 256


def _attn_consts():
    j = np.arange(SB_KEYS)[:, None]
    s = np.arange(SB_KEYS)[None, :]
    from_here = np.concatenate([(j >= s), (j >= s)], axis=0).astype(np.float32)
    return jnp.asarray(from_here, BF), jnp.asarray((j <= s).astype(np.float32), BF)


def _hi_lo(x):
    hi = lax.bitcast_convert_type(lax.bitcast_convert_type(x, jnp.uint32) & jnp.uint32(0xFFFF0000), F32)
    return hi.astype(BF), (x - hi).astype(BF)


def _sums_r(x, t2):
    hi, lo = _hi_lo(x)
    return _dot(jnp.concatenate([hi, lo], axis=1), t2)


def _all_lanes(col, lanes):
    return jnp.broadcast_to(col, (col.shape[0], lanes))


def _attn_rows(ref, r0, rows, lanes, head0, scale=None):
    v = ref[pl.ds(r0, rows), lanes]
    if scale is not None:
        v = v * scale
    return jnp.concatenate([jnp.where(head0, v, 0.0), jnp.where(head0, 0.0, v)], axis=0).astype(BF)


SB_PAIRS_FWD = 4
SB_PAIRS_BWD = 2


def _attn_specs(S, n_pairs):
    return lambda base: pl.BlockSpec((S, n_pairs * SB_BLOCK), lambda p, base=base: (0, base // n_pairs + p))


def _attn_scores(q2n, k_ref, lanes, kj, t2, from_here_ref, masked):
    c0 = pl.multiple_of(kj * SB_KEYS, SB_KEYS)
    kb = k_ref[pl.ds(c0, SB_KEYS), lanes].astype(BF)
    zn = _dot_nt(q2n, kb)
    lsb = jnp.minimum(zn, 0.0) - jnp.log(1.0 + jnp.exp(-jnp.abs(zn)))
    valid = None
    if masked:
        valid = (lax.broadcasted_iota(jnp.int32, zn.shape, 1) + kj * SB_KEYS) < t2
        lsb = jnp.where(valid, lsb, 0.0)
    return c0, kb, zn, valid, lsb, _sums_r(lsb, from_here_ref[...])


def _attn_fwd(proj, from_here, job=None):
    S = proj.shape[0]
    TQ = SB_ROWS
    assert S % TQ == 0 and SB_KEYS == TQ
    scale = SB_HEAD_DIM ** -0.5
    n_pairs = SB_PAIRS_FWD
    pairs = range(n_pairs)
    lanes = [pl.ds(p * SB_BLOCK, SB_BLOCK) for p in pairs]

    def body(q_ref, k_ref, v_ref, from_here_ref, o_ref, tot_ref, run, acc):
        head0 = lax.broadcasted_iota(jnp.int32, (1, 2 * SB_HEAD_DIM), 1) < SB_HEAD_DIM

        def qloop(qi, _):
            r0 = pl.multiple_of(qi * TQ, TQ)
            q2n = [_attn_rows(q_ref, r0, TQ, lanes[p], head0, -scale) for p in pairs]
            trow = lax.broadcasted_iota(jnp.int32, (TQ, SB_KEYS), 0) + qi * TQ
            t2 = jnp.concatenate([trow, trow], axis=0)
            run[...] = jnp.zeros_like(run)
            acc[...] = jnp.zeros_like(acc)

            def step(kj, masked):
                got = [_attn_scores(q2n[p], k_ref, lanes[p], kj, t2, from_here_ref, masked) for p in pairs]
                for p in pairs:
                    c0, _, zn, valid, _, sums = got[p]
                    r = run[p]
                    e = sums - zn + jnp.concatenate([r, r], axis=1)
                    if masked:
                        e = jnp.where(valid, e, -jnp.inf)
                    acc[p] += _dot(jnp.exp(e).astype(BF), v_ref[pl.ds(c0, SB_KEYS), lanes[p]].astype(BF))
                    run[p] = r + _all_lanes(sums[:, 0:1], SB_BLOCK)

            step(qi, True)

            def below(n, _):
                step(qi - 1 - 2 * n, False)
                step(qi - 2 - 2 * n, False)
                return 0

            lax.fori_loop(0, qi // 2, below, 0)

            @pl.when(qi % 2 == 1)
            def _():
                step(0, False)
            for p in pairs:
                o_ref[pl.ds(r0, TQ), lanes[p]] = jnp.where(head0, acc[p, 0:TQ, :], acc[p, TQ:2 * TQ, :])
                tot_ref[p, 0, pl.ds(r0, TQ), :] = run[p, 0:TQ, :]
                tot_ref[p, 1, pl.ds(r0, TQ), :] = run[p, TQ:2 * TQ, :]
            return 0

        lax.fori_loop(0, S // TQ, qloop, 0)

    col = _attn_specs(S, n_pairs)
    state = pltpu.VMEM((n_pairs, 2 * TQ, SB_BLOCK), F32)
    return _hosted(
        body, job, (proj, proj, proj, from_here), name="attn_fwd",
        grid=(WIDTH // (n_pairs * SB_BLOCK),),
        in_specs=[col(0), col(4), col(8), pl.BlockSpec(from_here.shape, lambda p: (0, 0))],
        out_specs=[col(0), pl.BlockSpec((n_pairs, 2, S, 128), lambda p: (p, 0, 0, 0))],
        out_shape=[jax.ShapeDtypeStruct((S, WIDTH), F32), jax.ShapeDtypeStruct((WIDTH // 128, 2, S, 128), F32)],
        scratch_shapes=[state, state],
        semantics=("parallel",))


def _attn_bwd(proj, d_o, tot, from_here, up_to, dproj, job=None):
    S = proj.shape[0]
    TQ = SB_ROWS
    assert S % TQ == 0 and SB_KEYS == TQ
    scale = SB_HEAD_DIM ** -0.5
    n_pairs = SB_PAIRS_BWD
    pairs = range(n_pairs)
    lanes = [pl.ds(p * SB_BLOCK, SB_BLOCK) for p in pairs]

    def body(q_ref, k_ref, v_ref, do_ref, tot_ref, from_here_ref, up_to_ref, _, dproj_ref, pre, cum, dq_acc, dk_acc, dv_acc, stage, sems):
        dq_ref, dk_ref, dv_ref = stage.at[0], stage.at[1], stage.at[2]
        head0 = lax.broadcasted_iota(jnp.int32, (1, 2 * SB_HEAD_DIM), 1) < SB_HEAD_DIM
        dk_acc[...] = jnp.zeros_like(dk_acc)
        dv_acc[...] = jnp.zeros_like(dv_acc)

        def qloop(qi, _):
            r0 = pl.multiple_of(qi * TQ, TQ)
            q2n = [_attn_rows(q_ref, r0, TQ, lanes[p], head0, -scale) for p in pairs]
            do2 = [_attn_rows(do_ref, r0, TQ, lanes[p], head0) for p in pairs]
            trow = lax.broadcasted_iota(jnp.int32, (TQ, SB_KEYS), 0) + qi * TQ
            t2 = jnp.concatenate([trow, trow], axis=0)
            for p in pairs:
                pre[p, 0:TQ, :] = tot_ref[p, 0, pl.ds(r0, TQ), :]
                pre[p, TQ:2 * TQ, :] = tot_ref[p, 1, pl.ds(r0, TQ), :]
            cum[...] = jnp.zeros_like(cum)
            dq_acc[...] = jnp.zeros_like(dq_acc)

            def step(kj, masked):
                got = [_attn_scores(q2n[p], k_ref, lanes[p], kj, t2, from_here_ref, masked) for p in pairs]
                das = [_dot_nt(do2[p], v_ref[pl.ds(got[p][0], SB_KEYS), lanes[p]].astype(BF)) for p in pairs]
                for p in pairs:
                    c0, kb, zn, valid, lsb, sums = got[p]
                    later = pre[p] - _all_lanes(sums[:, 0:1], SB_BLOCK)
                    pre[p] = later
                    e = sums - zn + jnp.concatenate([later, later], axis=1)
                    sig = jnp.exp(lsb - zn)
                    if masked:
                        e = jnp.where(valid, e, -jnp.inf)
                        sig = jnp.where(valid, sig, 0.0)
                    a = jnp.exp(e)
                    w = das[p] * a
                    upto = _dot(w.astype(BF), up_to_ref[...])
                    c = cum[p]
                    dz = w - sig * (upto + jnp.concatenate([c, c], axis=1))
                    cum[p] = c + _all_lanes(upto[:, SB_KEYS - 1:SB_KEYS], SB_BLOCK)
                    dzb = dz.astype(BF)
                    dq_acc[p] += _dot(dzb, kb)
                    dk_acc[pl.ds(c0, SB_KEYS), lanes[p]] += _dot_tn(dzb, q2n[p])
                    dv_acc[pl.ds(c0, SB_KEYS), lanes[p]] += _dot_tn(a.astype(BF), do2[p])

            def below(n, _):
                step(2 * n, False)
                step(2 * n + 1, False)
                return 0

            lax.fori_loop(0, qi // 2, below, 0)

            @pl.when(qi % 2 == 1)
            def _():
                step(qi - 1, False)

            step(qi, True)
            for p in pairs:
                dq_ref[pl.ds(r0, TQ), lanes[p]] = (jnp.where(head0, dq_acc[p, 0:TQ, :], dq_acc[p, TQ:2 * TQ, :]) * scale).astype(BF)
            return 0

        lax.fori_loop(0, S // TQ, qloop, 0)
        dk_ref[...] = (-dk_acc[...]).astype(BF)
        dv_ref[...] = dv_acc[...].astype(BF)
        lane0 = pl.program_id(0) * (n_pairs * SB_BLOCK)
        _put_columns([dq_ref, dk_ref, dv_ref], dproj_ref, [WIDTH * n + lane0 for n in range(3)], sems)

    col = _attn_specs(S, n_pairs)
    whole = lambda a: pl.BlockSpec(a.shape, lambda p: (0, 0))
    hbm = pl.BlockSpec(memory_space=pl.ANY)
    state = pltpu.VMEM((n_pairs, 2 * TQ, SB_BLOCK), F32)
    grads = pltpu.VMEM((S, n_pairs * SB_BLOCK), F32)
    return _hosted(
        body, job, (proj, proj, proj, d_o, tot, from_here, up_to, dproj), name="attn_bwd",
        grid=(WIDTH // (n_pairs * SB_BLOCK),),
        in_specs=[col(0), col(4), col(8), col(0), pl.BlockSpec((n_pairs, 2, S, 128), lambda p: (p, 0, 0, 0)), whole(from_here),
                  whole(up_to), hbm],
        out_specs=[hbm],
        out_shape=[jax.ShapeDtypeStruct(dproj.shape, dproj.dtype)],
        scratch_shapes=[state, state, state, grads, grads, pltpu.VMEM((3, S, n_pairs * SB_BLOCK), BF), pltpu.SemaphoreType.DMA((3,))],
        semantics=("arbitrary",), aliases={7: 0})


HG_LEVELS = tuple(HG_CHUNK >> n for n in range(1, HG_CHUNK.bit_length()))


def _hgrn_consts():
    C = HG_CHUNK
    t = np.arange(C)[:, None]
    s = np.arange(C)[None, :]
    rows = [(s <= t), (s > t)]
    masks = [(t == s)]
    for m in HG_LEVELS:
        two = 2 * m
        mid = (t // two) * two + m
        right = (t % two) >= m
        rows.append((right & (s >= mid) & (s <= t)) | ((~right) & (s > t) & (s <= mid - 1)))
        masks.append(((t // two) == (s // two)) & right & ((s % two) < m))
    tri = np.concatenate(rows, axis=0).astype(np.float32)
    twice = lambda a: jnp.asarray(np.concatenate([a, a], axis=1), BF)
    return (twice(tri), twice(tri.T), jnp.asarray(np.stack(masks).astype(np.float32), F32))


HG_SUM_BLOCKS = 2 + len(HG_LEVELS)


def _split_rows(g):
    hi = g.astype(BF)
    return jnp.concatenate([hi, (g - hi.astype(F32)).astype(BF)], axis=0)


def _hgrn_sum_blocks(e):
    C = HG_CHUNK
    blocks = tuple(e[n * C:(n + 1) * C] for n in range(HG_SUM_BLOCKS))
    return blocks + (jnp.broadcast_to(e[C - 1:C], (HG_DIM, e.shape[1])),)


@jax.custom_vjp
def _hgrn_sums(tri, tri_t, g):
    return _hgrn_sum_blocks(_dot(tri, _split_rows(g)))


def _hgrn_sums_fwd(tri, tri_t, g):
    return _hgrn_sums(tri, tri_t, g), (tri, tri_t)


def _hgrn_sums_bwd(res, ds):
    tri, tri_t = res
    C = HG_CHUNK
    last = lax.broadcasted_iota(jnp.int32, (C, 1), 0) == C - 1
    prefix = ds[0] + jnp.where(last, jnp.sum(ds[-1], axis=0, keepdims=True), 0.0)
    d = jnp.concatenate((prefix,) + tuple(ds[1:-1]), axis=0)
    return jnp.zeros_like(tri), jnp.zeros_like(tri_t), _dot(tri_t[:, :d.shape[0]], d.astype(BF))


_hgrn_sums.defvjp(_hgrn_sums_fwd, _hgrn_sums_bwd)


def _bf_dot(a, b):
    return _dot(a.astype(BF), b.astype(BF))


def _bf_dot_nt(a, b):
    return _dot_nt(a.astype(BF), b.astype(BF))


def _bf_dot_tn(a, b):
    return _dot_tn(a.astype(BF), b.astype(BF))


@jax.custom_vjp
def _mm(a, b):
    return _bf_dot(a, b)


_mm.defvjp(lambda a, b: (_bf_dot(a, b), (a, b)), lambda r, ct: (_bf_dot_nt(ct, r[1]), _bf_dot_tn(r[0], ct)))


@jax.custom_vjp
def _mm_nt(a, b):
    return _bf_dot_nt(a, b)


_mm_nt.defvjp(lambda a, b: (_bf_dot_nt(a, b), (a, b)), lambda r, ct: (_bf_dot(ct, r[1]), _bf_dot_tn(ct, r[0])))


@jax.custom_vjp
def _mm_tn(a, b):
    return _bf_dot_tn(a, b)


_mm_tn.defvjp(lambda a, b: (_bf_dot_tn(a, b), (a, b)), lambda r, ct: (_bf_dot_nt(r[1], ct), _bf_dot(r[0], ct)))


def _hgrn_gates(tri, tri_t, qraw, fpre, lb):
    q = _silu(qraw)
    f = lb + (1.0 - lb) * _sigmoid(fpre)
    return q, 1.0 - f, _hgrn_sums(tri, tri_t, jnp.log(f))


def _hgrn_mix(masks, q, k, e, v, st):
    prefix, suffix, whole = e[0], e[1], e[-1]
    scores = masks[0] * _mm_nt(q, k)
    for n in range(len(HG_LEVELS)):
        decay = jnp.exp(e[2 + n])
        scores = scores + masks[n + 1] * _mm_nt(q * decay, k * decay)
    o = _mm_nt(q * jnp.exp(prefix), st) + _mm(scores, v)
    st_new = st * jnp.exp(whole) + _mm_tn(v, k * jnp.exp(suffix))
    return o, st_new


def _hgrn_chunk(tri, tri_t, masks, qraw, fpre, v, st, lb):
    q, k, e = _hgrn_gates(tri, tri_t, qraw, fpre, lb)
    return _hgrn_mix(masks, q, k, e, v, st)


HG_HEADS_PER_STEP = 4
HG_LANES = HG_HEADS_PER_STEP * HG_DIM


def _hgrn_specs(S, consts):
    col = lambda base: pl.BlockSpec((S, HG_LANES), lambda p, base=base: (0, base // HG_HEADS_PER_STEP + p))
    whole = [pl.BlockSpec(a.shape, lambda p, n=a.ndim: (0,) * n) for a in consts]
    return col, whole


def _hgrn_fwd(proj, lbs, consts, job=None):
    S = proj.shape[0]
    nc = S // HG_CHUNK
    heads = range(HG_HEADS_PER_STEP)

    def body(q_ref, f_ref, i_ref, lb_ref, tri_ref, trit_ref, mask_ref, o_ref, st_ref):
        tri, tri_t = tri_ref[...], trit_ref[...]
        masks = [mask_ref[n] for n in range(len(HG_LEVELS) + 1)]

        def chunk(ci, sts):
            r0 = pl.multiple_of(ci * HG_CHUNK, HG_CHUNK)
            rows = pl.ds(r0, HG_CHUNK)
            new = []
            lane = [pl.ds(hd * HG_DIM, HG_DIM) for hd in heads]
            gates = [_hgrn_gates(tri, tri_t, q_ref[rows, lane[hd]], f_ref[rows, lane[hd]], lb_ref[hd]) for hd in heads]
            for hd in heads:
                st_ref[hd, ci] = sts[hd]
                o, st_new = _hgrn_mix(masks, *gates[hd], i_ref[rows, lane[hd]], sts[hd])
                o_ref[rows, lane[hd]] = o
                new.append(st_new)
            return tuple(new)

        lax.fori_loop(0, nc, chunk, tuple(jnp.zeros((HG_DIM, HG_DIM), F32) for _ in heads))

    col, whole = _hgrn_specs(S, consts)
    return _hosted(
        body, job, (proj, proj, proj, lbs, *consts), name="hgrn_fwd",
        grid=(WIDTH // HG_LANES,),
        in_specs=[col(16), col(20), col(24), pl.BlockSpec((HG_HEADS_PER_STEP, 1, 128), lambda p: (p, 0, 0))] + whole,
        out_specs=[col(0), pl.BlockSpec((HG_HEADS_PER_STEP, nc, HG_DIM, HG_DIM), lambda p: (p, 0, 0, 0))],
        out_shape=[jax.ShapeDtypeStruct((S, WIDTH), F32), jax.ShapeDtypeStruct((WIDTH // 128, nc, HG_DIM, HG_DIM), F32)],
        semantics=("parallel",))


def _hgrn_bwd(proj, lbs, states, d_o, consts, dproj, job=None):
    S = proj.shape[0]
    nc = S // HG_CHUNK

    def body(q_ref, f_ref, i_ref, lb_ref, st_ref, do_ref, tri_ref, trit_ref, mask_ref, _, dlb_ref, dproj_ref, stage, sems):
        dq_ref, df_ref, di_ref = stage.at[0], stage.at[1], stage.at[2]
        masks = [mask_ref[n] for n in range(len(HG_LEVELS) + 1)]
        fn = functools.partial(_hgrn_chunk, tri_ref[...], trit_ref[...], masks)
        heads = range(HG_HEADS_PER_STEP)

        def chunk(n, carry):
            ci = nc - 1 - n
            r0 = pl.multiple_of(ci * HG_CHUNK, HG_CHUNK)
            rows = pl.ds(r0, HG_CHUNK)
            new = []
            lane = [pl.ds(hd * HG_DIM, HG_DIM) for hd in heads]
            pulls = [jax.vjp(fn, q_ref[rows, lane[hd]], f_ref[rows, lane[hd]], i_ref[rows, lane[hd]], st_ref[hd, ci], lb_ref[hd])[1]
                     for hd in heads]
            for hd in heads:
                d_st, dlb = carry[hd]
                lanes = lane[hd]
                dq, df, di, d_prev, dl = pulls[hd]((do_ref[rows, lanes], d_st))
                dq_ref[rows, lanes] = dq.astype(BF)
                df_ref[rows, lanes] = df.astype(BF)
                di_ref[rows, lanes] = di.astype(BF)
                new.append((d_prev, dlb + dl))
            return tuple(new)

        zero = (jnp.zeros((HG_DIM, HG_DIM), F32), jnp.zeros((1, HG_DIM), F32))
        done = lax.fori_loop(0, nc, chunk, tuple(zero for _ in heads))
        for hd in heads:
            dlb_ref[hd] = done[hd][1]
        lane0 = pl.program_id(0) * HG_LANES
        _put_columns([dq_ref, df_ref, di_ref], dproj_ref, [WIDTH * n + lane0 for n in (4, 5, 6)], sems)

    col, whole = _hgrn_specs(S, consts)
    head = pl.BlockSpec((HG_HEADS_PER_STEP, 1, 128), lambda p: (p, 0, 0))
    hbm = pl.BlockSpec(memory_space=pl.ANY)
    n_in = 6 + len(consts)
    return _hosted(
        body, job, (proj, proj, proj, lbs, states, d_o, *consts, dproj), name="hgrn_bwd",
        grid=(WIDTH // HG_LANES,),
        in_specs=[col(16), col(20), col(24), head, pl.BlockSpec((HG_HEADS_PER_STEP, nc, HG_DIM, HG_DIM), lambda p: (p, 0, 0, 0)), col(0)]
        + whole + [hbm],
        out_specs=[head, hbm],
        out_shape=[jax.ShapeDtypeStruct((WIDTH // 128, 1, 128), F32), jax.ShapeDtypeStruct(dproj.shape, dproj.dtype)],
        scratch_shapes=[pltpu.VMEM((3, S, HG_LANES), BF), pltpu.SemaphoreType.DMA((3,))],
        semantics=("arbitrary",), aliases={n_in: 1})


def _shift_down(x, n):
    rows = lax.broadcasted_iota(jnp.int32, x.shape, 0)
    return jnp.where(rows >= n, pltpu.roll(x, n, 0), 0.0)


def _shift_up(x, n):
    S = x.shape[0]
    rows = lax.broadcasted_iota(jnp.int32, x.shape, 0)
    return jnp.where(rows < S - n, pltpu.roll(x, S - n, 0), 0.0)


def _branch_fwd(proj, o_a, o_b, norm_w, conv_w, layer):
    S = proj.shape[0]

    def body(oa_ref, za_ref, ob_ref, zb_ref, nw_ref, pre_ref, post_ref, u_ref, zc_ref, cw_ref, ya_ref, yb_ref, yc_ref):
        ya_ref[...] = (oa_ref[...] * _silu(za_ref[...])).astype(BF)
        ob = ob_ref[...]
        rn = lax.rsqrt(jnp.mean(ob * ob, axis=-1, keepdims=True) + RMS_EPS)
        yb_ref[...] = (ob * rn * nw_ref[layer:layer + 1, :] * _silu(zb_ref[...])).astype(BF)
        pu = pre_ref[...] * u_ref[...]
        conv = cw_ref[2:3, :] * pu + cw_ref[1:2, :] * _shift_down(pu, 1) + cw_ref[0:1, :] * _shift_down(pu, 2)
        yc_ref[...] = (post_ref[...] * conv * _silu(zc_ref[...])).astype(BF)

    col = lambda base: pl.BlockSpec((S, 128), lambda p, base=base: (0, base + p))
    out = jax.ShapeDtypeStruct((S, WIDTH), BF)
    return pl.pallas_call(
        body, name="branch_fwd",
        grid=(WIDTH // 128,),
        in_specs=[col(0), col(12), col(0), col(28), pl.BlockSpec(norm_w.shape, lambda p: (0, 0)),
                  col(32), col(36), col(40), col(44), pl.BlockSpec((None, None, 3, 128), lambda p: (p, layer, 0, 0))],
        out_specs=[col(0), col(0), col(0)],
        out_shape=[out, out, out],
        compiler_params=_params(dimension_semantics=("parallel",)),
    )(o_a, proj, o_b, proj, norm_w, proj, proj, proj, proj, conv_w)


def _put_columns(tiles, dproj_ref, firsts, sems):
    copies = [pltpu.make_async_copy(t, dproj_ref.at[:, pl.ds(pl.multiple_of(c, 128), t.shape[1])], sems.at[n])
              for n, (t, c) in enumerate(zip(tiles, firsts))]
    for cp in copies:
        cp.start()
    for cp in copies:
        cp.wait()


def _branch_bwd(proj, o_a, o_b, norm_w, conv_w, dy_a, dy_b, dy_c, dproj, layer):
    S = proj.shape[0]
    firsts = [WIDTH * n for n in (3, 7, 8, 9, 10, 11)]

    def dsilu(z):
        s = _sigmoid(z)
        return s * z, s * (1.0 + z * (1.0 - s))

    def body(oa_ref, za_ref, ob_ref, zb_ref, nw_ref, pre_ref, post_ref, u_ref, zc_ref, cw_ref, dya_ref, dyb_ref, dyc_ref, _,
             doa_ref, dob_ref, dnw_ref, dcw_ref, dproj_ref, stage, sems):
        dza_ref, dzb_ref, dpre_ref, dpost_ref, du_ref, dzc_ref = [stage.at[n] for n in range(6)]
        dya = dya_ref[...]
        sa, dsa = dsilu(za_ref[...])
        doa_ref[...] = dya * sa
        dza_ref[...] = (dya * oa_ref[...] * dsa).astype(BF)

        dyb = dyb_ref[...]
        ob = ob_ref[...]
        nw = nw_ref[layer:layer + 1, :]
        sb, dsb = dsilu(zb_ref[...])
        rn = lax.rsqrt(jnp.mean(ob * ob, axis=-1, keepdims=True) + RMS_EPS)
        on = ob * rn
        dzb_ref[...] = (dyb * on * nw * dsb).astype(BF)
        don_w = dyb * sb
        dnw_ref[...] = jnp.sum(don_w * on, axis=0, keepdims=True)
        don = don_w * nw
        dob_ref[...] = rn * (don - on * jnp.mean(don * on, axis=-1, keepdims=True))

        dyc = dyc_ref[...]
        pre, post, u = pre_ref[...], post_ref[...], u_ref[...]
        sc, dsc = dsilu(zc_ref[...])
        pu = pre * u
        pu1, pu2 = _shift_down(pu, 1), _shift_down(pu, 2)
        conv = cw_ref[2:3, :] * pu + cw_ref[1:2, :] * pu1 + cw_ref[0:1, :] * pu2
        dzc_ref[...] = (dyc * post * conv * dsc).astype(BF)
        dpost_ref[...] = (dyc * conv * sc).astype(BF)
        dconv = dyc * post * sc
        dcw_ref[0:1, :] = jnp.sum(dconv * pu2, axis=0, keepdims=True)
        dcw_ref[1:2, :] = jnp.sum(dconv * pu1, axis=0, keepdims=True)
        dcw_ref[2:3, :] = jnp.sum(dconv * pu, axis=0, keepdims=True)
        dpu = cw_ref[2:3, :] * dconv + cw_ref[1:2, :] * _shift_up(dconv, 1) + cw_ref[0:1, :] * _shift_up(dconv, 2)
        dpre_ref[...] = (dpu * u).astype(BF)
        du_ref[...] = (dpu * pre).astype(BF)
        lane0 = pl.program_id(0) * 128
        _put_columns([stage.at[n] for n in range(6)], dproj_ref, [c + lane0 for c in firsts], sems)

    col = lambda base: pl.BlockSpec((S, 128), lambda p, base=base: (0, base + p))
    f32 = jax.ShapeDtypeStruct((S, WIDTH), F32)
    hbm = pl.BlockSpec(memory_space=pl.ANY)
    return pl.pallas_call(
        body, name="branch_bwd",
        grid=(WIDTH // 128,),
        in_specs=[col(0), col(12), col(0), col(28), pl.BlockSpec(norm_w.shape, lambda p: (0, 0)),
                  col(32), col(36), col(40), col(44), pl.BlockSpec((None, None, 3, 128), lambda p: (p, layer, 0, 0)),
                  col(0), col(0), col(0), hbm],
        out_specs=[col(0), col(0), pl.BlockSpec((None, 1, 128), lambda p: (p, 0, 0)), pl.BlockSpec((None, 3, 128), lambda p: (p, 0, 0)), hbm],
        out_shape=[f32, f32, jax.ShapeDtypeStruct((WIDTH // 128, 1, 128), F32), jax.ShapeDtypeStruct((WIDTH // 128, 3, 128), F32),
                   jax.ShapeDtypeStruct(dproj.shape, dproj.dtype)],
        scratch_shapes=[pltpu.VMEM((6, S, 128), BF), pltpu.SemaphoreType.DMA((6,))],
        input_output_aliases={13: 4},
        compiler_params=_params(dimension_semantics=("arbitrary",)),
    )(o_a, proj, o_b, proj, norm_w, proj, proj, proj, proj, conv_w, dy_a, dy_b, dy_c, dproj)


def _branch_proj(y_refs, wb_ref):
    return [_dot(y_refs[i][...], wb_ref[i]) for i in range(3)]


def _merge_fwd(x, mod, proj, ys, wb, wo, ln_g, ln_b, layer, job=None, tm=512):
    S, D = x.shape
    tm = min(tm, S)

    def body(x_ref, mod_ref, ga_ref, gb_ref, gc_ref, ya_ref, yb_ref, yc_ref, wb_ref, wo_ref, g_ref, b_ref, xo_ref, mg_ref, y_ref):
        ps = _branch_proj((ya_ref, yb_ref, yc_ref), wb_ref)
        merged = _sigmoid(ga_ref[...]) * ps[0] + _sigmoid(gb_ref[...]) * ps[1] + _sigmoid(gc_ref[...]) * ps[2]
        mb = merged.astype(BF)
        mg_ref[...] = mb
        y = _dot(mb, wo_ref[...].reshape(D, D))
        y_ref[...] = y
        r = ALPHA * x_ref[...] + (1.0 + mod_ref[:, 2 * D:3 * D]) * y
        xn, _ = _standardize(r)
        xo_ref[...] = xn * g_ref[layer:layer + 1, :] + b_ref[layer:layer + 1, :]

    row = lambda w, c=0: pl.BlockSpec((tm, w), lambda i, c=c: (i, c))
    whole = lambda a: pl.BlockSpec(a.shape, lambda i, n=a.ndim: (0,) * n)
    return _hosted(
        body, job, (x, mod, proj, proj, proj, *ys, wb, wo, ln_g, ln_b), name="merge_fwd",
        grid=(S // tm,),
        in_specs=[row(D), whole(mod), row(D, 6), row(D, 7), row(D, 8), row(WIDTH), row(WIDTH), row(WIDTH), whole(wb),
                  whole(wo), whole(ln_g), whole(ln_b)],
        out_specs=[row(D), row(D), row(D)],
        out_shape=[jax.ShapeDtypeStruct((S, D), F32), jax.ShapeDtypeStruct((S, D), BF), jax.ShapeDtypeStruct((S, D), F32)],
        semantics=("parallel",))


def _merge_bwd(dxo, x, y, merged, mod, proj, ys, wb, wo, ln_g, layer, tm=256):
    S, D = x.shape
    tm = min(tm, S)
    steps = S // tm
    quarter = D // N_CHIPS

    def body(dxo_ref, x_ref, y_ref, mg_ref, mod_ref, ga_ref, gb_ref, gc_ref, ya_ref, yb_ref, yc_ref, wb_ref, wo_ref, g_ref,
             dxr_ref, dg_ref, dya_ref, dyb_ref, dyc_ref, dlg_ref, dlb_ref, dgt_ref, gbr_ref, gout_ref, acc_br, acc_out):
        @pl.when(pl.program_id(0) == 0)
        def _():
            dlg_ref[...] = jnp.zeros_like(dlg_ref)
            dlb_ref[...] = jnp.zeros_like(dlb_ref)
            dgt_ref[...] = jnp.zeros_like(dgt_ref)
            acc_br[...] = jnp.zeros_like(acc_br)
            acc_out[...] = jnp.zeros_like(acc_out)

        gate1 = 1.0 + mod_ref[:, 2 * D:3 * D]
        yv = y_ref[...]
        xn, rstd = _standardize(ALPHA * x_ref[...] + gate1 * yv)
        dxo = dxo_ref[...]
        dlg_ref[...] += jnp.sum(dxo * xn, axis=0, keepdims=True)
        dlb_ref[...] += jnp.sum(dxo, axis=0, keepdims=True)
        dr = _standardize_bwd(dxo * g_ref[layer:layer + 1, :], xn, rstd)
        dxr_ref[...] = ALPHA * dr
        dgt_ref[...] += jnp.sum(dr * yv, axis=0, keepdims=True)
        dyb = (gate1 * dr).astype(BF)
        acc_out[...] += _dot_tn(mg_ref[...], dyb)
        dmerged = _dot_nt(dyb, wo_ref[...].reshape(D, D))
        y_refs = (ya_ref, yb_ref, yc_ref)
        ps = _branch_proj(y_refs, wb_ref)
        for i, (gate_ref, out_ref) in enumerate(((ga_ref, dya_ref), (gb_ref, dyb_ref), (gc_ref, dyc_ref))):
            sg = _sigmoid(gate_ref[...])
            dg_ref[:, i * D:(i + 1) * D] = (dmerged * ps[i] * sg * (1.0 - sg)).astype(BF)
            dp = (dmerged * sg).astype(BF)
            acc_br[i] += _dot_tn(y_refs[i][...], dp)
            out_ref[...] = _dot_nt(dp, wb_ref[i])

        @pl.when(pl.program_id(0) == steps - 1)
        def _():
            for j in range(N_CHIPS):
                gout_ref[j] = acc_out[j * quarter:(j + 1) * quarter, :].astype(BF)
                for i in range(3):
                    gbr_ref[j, i] = acc_br[i, :, j * quarter:(j + 1) * quarter].astype(BF)

    row = lambda w, c=0: pl.BlockSpec((tm, w), lambda i, c=c: (i, c))
    whole = lambda a: pl.BlockSpec(a.shape, lambda i, n=len(a.shape): (0,) * n)
    vec = pl.BlockSpec((1, D), lambda i: (0, 0))
    sd = jax.ShapeDtypeStruct
    g_br, g_out = sd((N_CHIPS, 3, WIDTH, quarter), BF), sd((N_CHIPS, quarter, D), BF)
    return pl.pallas_call(
        body, name="merge_bwd",
        grid=(steps,),
        in_specs=[row(D), row(D), row(D), row(D), whole(mod), row(D, 6), row(D, 7), row(D, 8), row(WIDTH), row(WIDTH), row(WIDTH),
                  whole(wb), whole(wo), whole(ln_g)],
        out_specs=[row(D), row(3 * D, IN_COLS // (3 * D) - 1), row(WIDTH), row(WIDTH), row(WIDTH), vec, vec, vec, whole(g_br), whole(g_out)],
        out_shape=[sd((S, D), F32), sd((S, IN_COLS), BF), sd((S, WIDTH), F32), sd((S, WIDTH), F32), sd((S, WIDTH), F32),
                   sd((1, D), F32), sd((1, D), F32), sd((1, D), F32), g_br, g_out],
        scratch_shapes=[pltpu.VMEM((3, WIDTH, D), F32), pltpu.VMEM((D, D), F32)],
        compiler_params=_params(dimension_semantics=("arbitrary",)),
    )(dxo, x, y, merged, mod, proj, proj, proj, *ys, wb, wo, ln_g)


def _loss_head(x, target, tm=512):
    S, D = x.shape
    tm = min(tm, S)

    def body(x_ref, t_ref, dx_ref, loss_ref):
        @pl.when(pl.program_id(0) == 0)
        def _():
            loss_ref[...] = jnp.zeros_like(loss_ref)

        err = x_ref[...] - t_ref[...]
        dx_ref[...] = err * (1.0 / D)
        loss_ref[...] += 0.5 * jnp.sum(jnp.mean(err * err, axis=-1, keepdims=True))

    row = pl.BlockSpec((tm, D), lambda i: (i, 0))
    return pl.pallas_call(
        body, name="loss_head",
        grid=(S // tm,),
        in_specs=[row, row],
        out_specs=[row, pl.BlockSpec((8, 128), lambda i: (0, 0))],
        out_shape=[jax.ShapeDtypeStruct((S, D), F32), jax.ShapeDtypeStruct((8, 128), F32)],
        compiler_params=_params(dimension_semantics=("arbitrary",)),
    )(x, target)


def _proj_bwd(dproj, wgs, x, mod, dx_res, job=None, tm=512, tk=768):
    S, D = x.shape
    tm = min(tm, S)
    n = len(wgs)
    w_args = list(wgs)
    if n == 1:
        per = wgs[0].shape[-1] // tk
        w_specs = [pl.BlockSpec((None, D, tk), lambda k, i: (k // per, 0, k % per))]
    else:
        assert wgs[0].shape[-1] == tk
        w_specs = [pl.BlockSpec((None, D, tk), lambda k, i, c=c: (jnp.minimum((k + n - 1 - c) // n, N_CHIPS - 1), 0, 0))
                   for c in range(n)]
    nk = IN_COLS // tk

    def body(dp_ref, *rest):
        w_refs, (x_ref, mod_ref, dxr_ref, dx_ref, dsh_ref, dsc_ref, acc) = rest[:n], rest[n:]
        k, i = pl.program_id(0), pl.program_id(1)
        mine = pl.ds(pl.multiple_of(i * tm, tm), tm)

        @pl.when((i == 0) & (k == 0))
        def _():
            dsh_ref[...] = jnp.zeros_like(dsh_ref)
            dsc_ref[...] = jnp.zeros_like(dsc_ref)

        @pl.when(k == 0)
        def _():
            acc[mine, :] = jnp.zeros((tm, D), F32)

        for c in range(n):
            @pl.when(k % n == c)
            def _(c=c):
                acc[mine, :] += _dot_nt(dp_ref[...], w_refs[c][...])

        @pl.when(k == nk - 1)
        def _():
            dh = acc[mine, :]
            xs, rstd = _standardize(x_ref[...])
            dsh_ref[...] += jnp.sum(dh, axis=0, keepdims=True)
            dsc_ref[...] += jnp.sum(dh * xs, axis=0, keepdims=True)
            dx_ref[...] = _standardize_bwd(dh * (1.0 + mod_ref[:, D:2 * D]), xs, rstd) + dxr_ref[...]

    row = pl.BlockSpec((tm, D), lambda k, i: (jnp.where(k == nk - 1, i, 0), 0))
    vec = pl.BlockSpec((1, D), lambda k, i: (0, 0))
    return _hosted(
        body, job, (dproj, *w_args, x, mod, dx_res), name="proj_bwd",
        grid=(nk, S // tm),
        in_specs=[pl.BlockSpec((tm, tk), lambda k, i: (i, k))] + w_specs + [row, pl.BlockSpec((1, 3 * D), lambda k, i: (0, 0)), row],
        out_specs=[row, vec, vec],
        out_shape=[jax.ShapeDtypeStruct((S, D), F32), jax.ShapeDtypeStruct((1, D), F32), jax.ShapeDtypeStruct((1, D), F32)],
        scratch_shapes=[pltpu.VMEM((S, D), F32)],
        semantics=("arbitrary", "arbitrary"))


def _grad_w_in(h, dproj):
    S, D = h.shape
    shard = IN_COLS // N_CHIPS

    def body(h_ref, d_ref, o_ref):
        o_ref[...] = _dot_tn(h_ref[...], d_ref[...]).astype(BF)

    return pl.pallas_call(
        body, name="grad_w_in",
        grid=(N_CHIPS,),
        in_specs=[pl.BlockSpec((S, D), lambda n: (0, 0)), pl.BlockSpec((S, shard), lambda n: (0, n))],
        out_specs=pl.BlockSpec((None, D, shard), lambda n: (n, 0, 0)),
        out_shape=jax.ShapeDtypeStruct((N_CHIPS, D, shard), BF),
        compiler_params=_params(dimension_semantics=("parallel",)),
    )(h, dproj)


def _all_gather8(x, name):
    R, N = x.shape

    def body(x_ref, out_ref, send_sems, recv_sems):
        mx, my, mc = lax.axis_index("x"), lax.axis_index("y"), lax.axis_index("c")
        me = 4 * mx + 2 * my + mc
        out_ref[me] = x_ref[...]
        copies = []
        for k in range(1, N_DEV):
            peer = (_flip(mx, k & 4), _flip(my, k & 2), _flip(mc, k & 1))
            cp = pltpu.make_async_remote_copy(src_ref=x_ref, dst_ref=out_ref.at[me], send_sem=send_sems.at[k - 1],
                                              recv_sem=recv_sems.at[k - 1], device_id=peer, device_id_type=MESH)
            cp.start()
            copies.append(cp)
        for cp in copies:
            cp.wait()

    return pl.pallas_call(
        body, name=name,
        in_specs=[pl.BlockSpec(memory_space=pltpu.VMEM)],
        out_specs=pl.BlockSpec(memory_space=pltpu.VMEM),
        out_shape=jax.ShapeDtypeStruct((N_DEV, R, N), F32),
        scratch_shapes=[pltpu.SemaphoreType.DMA((N_DEV - 1,)), pltpu.SemaphoreType.DMA((N_DEV - 1,))],
        compiler_params=_params(),
    )(x)


def _rows2d(a):
    return a.reshape(-1, a.shape[-1])


def _tile_rows(rows, cols, n_arrays):
    budget = (24 << 20) // (n_arrays * 2 * 4 * cols)
    if rows <= budget:
        return rows
    tm = 8
    for cand in range(8, budget + 1, 8):
        if rows % cand == 0:
            tm = cand
    return tm


SUM_ROWS = 256


def _sum_cores(g, sent, where):
    chips, lead, _, r, cols = g.shape
    tr = min(r, SUM_ROWS)

    def body(where_ref, g_ref, s_ref, out_ref):
        out_ref[...] = (g_ref[...].astype(F32) + s_ref[...].astype(F32)).astype(BF)

    spec = pl.BlockSpec((None, tr, cols), lambda i, j, where_ref: (i, j, 0))
    out = pl.pallas_call(
        body, name="sum_cores",
        grid_spec=pltpu.PrefetchScalarGridSpec(
            num_scalar_prefetch=1, grid=(chips * lead, r // tr),
            in_specs=[pl.BlockSpec((None, None, tr, cols), lambda i, j, where_ref: (i, where_ref[1], j, 0)), spec],
            out_specs=spec),
        out_shape=jax.ShapeDtypeStruct((chips * lead, r, cols), BF),
        compiler_params=_params(dimension_semantics=("parallel", "parallel")),
    )(where, g.reshape(chips * lead, 2, r, cols), sent.reshape(chips * lead, r, cols))
    return out.reshape(chips, lead, r, cols)


def _sum_chips(sums, got, where):
    _, lead, r, cols = sums.shape
    tr = min(r, SUM_ROWS)

    def body(where_ref, s_ref, g_ref, out_ref):
        out_ref[...] = ((s_ref[...].astype(F32) + g_ref[0].astype(F32)) + g_ref[1].astype(F32)) + g_ref[2].astype(F32)

    return pl.pallas_call(
        body, name="sum_chips",
        grid_spec=pltpu.PrefetchScalarGridSpec(
            num_scalar_prefetch=1, grid=(lead, r // tr),
            in_specs=[pl.BlockSpec((None, None, tr, cols), lambda i, j, where_ref: (where_ref[0], i, j, 0)),
                      pl.BlockSpec((N_CHIPS - 1, None, tr, cols), lambda i, j, where_ref: (0, i, j, 0))],
            out_specs=pl.BlockSpec((None, None, tr, cols), lambda i, j, where_ref: (i, where_ref[1], j, 0))),
        out_shape=jax.ShapeDtypeStruct((lead, 2, r, cols), F32),
        compiler_params=_params(dimension_semantics=("parallel", "parallel")),
    )(where, sums, got)


def _adamw(w, m, v, groups, name):
    shape = w.shape
    w2, m2, v2 = _rows2d(w), _rows2d(m), _rows2d(v)
    rows, cols = w2.shape
    ng = len(groups)
    n = len(groups[0])
    slab = rows // ng
    gs = [_rows2d(g) for grp in groups for g in grp]
    tm = _tile_rows(slab, cols, 7 + n)
    tiles = slab // tm
    c1 = 1.0 / (1.0 - ADAM_B1 ** ADAM_STEP)
    c2 = 1.0 / (1.0 - ADAM_B2 ** ADAM_STEP)

    def body(*refs):
        w_ref, m_ref, v_ref = refs[:3]
        g_refs = refs[3:3 + ng * n]
        go_ref, d_ref, mo_ref, vo_ref = refs[3 + ng * n:]
        which = pl.program_id(0)
        for s in range(ng):
            @pl.when(which == s)
            def _(s=s):
                g = g_refs[s * n][...]
                for r in g_refs[s * n + 1:(s + 1) * n]:
                    g = g + r[...]
                mn = ADAM_B1 * m_ref[...] + (1.0 - ADAM_B1) * g
                vn = ADAM_B2 * v_ref[...] + (1.0 - ADAM_B2) * (g * g)
                go_ref[...] = g
                mo_ref[...] = mn
                vo_ref[...] = vn
                d_ref[...] = -ADAM_LR * ((mn * c1) / (jnp.sqrt(vn * c2) + ADAM_EPS) + ADAM_WD * w_ref[...])

    spec = pl.BlockSpec((tm, cols), lambda s, i: (s * tiles + i, 0))
    g_specs = [pl.BlockSpec((tm, cols), lambda s, i, k=k: (jnp.where(s == k, i, jnp.where(s < k, 0, tiles - 1)), 0))
               for k in range(ng) for _ in range(n)]
    outs = pl.pallas_call(
        body, name=name,
        grid=(ng, tiles),
        in_specs=[spec] * 3 + g_specs,
        out_specs=[spec] * 4,
        out_shape=[jax.ShapeDtypeStruct((rows, cols), F32)] * 4,
        compiler_params=_params(dimension_semantics=("arbitrary", "arbitrary")),
    )(w2, m2, v2, *gs)
    return [o.reshape(shape) for o in outs]


def _lower_bounds(r0, r1):
    top = jnp.maximum(r0, r1)
    e0, e1 = jnp.exp(r0 - top), jnp.exp(r1 - top)
    p0, p1 = e0 / (e0 + e1), e1 / (e0 + e1)
    return p0 - p0, (p0 + p1) - p0


def _lbs_fwd(lb_raw):
    def body(lb_ref, out_ref):
        l0, l1 = _lower_bounds(lb_ref[0:1, :], lb_ref[1:2, :])
        out_ref[0:1, :] = l0
        out_ref[1:2, :] = l1

    return pl.pallas_call(body, name="lower_bounds", out_shape=jax.ShapeDtypeStruct(lb_raw.shape, F32), compiler_params=_params())(lb_raw)


def _mod_rows(c_all, w_mod, tn=768):
    _, D, cols = w_mod.shape

    def body(c_ref, w_ref, out_ref):
        out_ref[...] = _dot(c_ref[...].astype(BF), w_ref[...].astype(BF))

    return pl.pallas_call(
        body, name="mod_rows",
        grid=(DEPTH,),
        in_specs=[pl.BlockSpec((N_DEV, D), lambda l: (0, 0)), pl.BlockSpec((None, D, cols), lambda l: (l, 0, 0))],
        out_specs=pl.BlockSpec((N_DEV, cols), lambda l: (0, l)),
        out_shape=jax.ShapeDtypeStruct((N_DEV, DEPTH * cols), F32),
        compiler_params=_params(dimension_semantics=("parallel",)),
    )(c_all, w_mod)


def _grad_w_mod(c_all, dmod_cols):
    D = c_all.shape[1]
    cols = dmod_cols.shape[-1]

    def body(c_ref, d_ref, out_ref):
        out_ref[...] = _dot_tn(c_ref[...].astype(BF), d_ref[...].astype(BF))

    return pl.pallas_call(
        body, name="grad_w_mod",
        grid=(DEPTH,),
        in_specs=[pl.BlockSpec((N_DEV, D), lambda l: (0, 0)), pl.BlockSpec((None, N_DEV, cols), lambda l: (l, 0, 0))],
        out_specs=pl.BlockSpec((None, D, cols), lambda l: (l, 0, 0)),
        out_shape=jax.ShapeDtypeStruct((DEPTH, D, cols), F32),
        compiler_params=_params(dimension_semantics=("parallel",)),
    )(c_all, dmod_cols)


def _sum_devices(parts):
    _, R, N = parts.shape

    def body(p_ref, out_ref):
        acc = p_ref[0]
        for d in range(1, N_DEV):
            acc = acc + p_ref[d]
        out_ref[...] = acc

    return pl.pallas_call(body, name="sum_devices", out_shape=jax.ShapeDtypeStruct((R, N), F32), compiler_params=_params())(parts)


def _lbs_bwd(lb_raw, dl):
    def body(lb_ref, dl_ref, out_ref):
        _, pull = jax.vjp(_lower_bounds, lb_ref[0:1, :], lb_ref[1:2, :])
        d0, d1 = pull((dl_ref[0:1, :], dl_ref[1:2, :]))
        out_ref[0:1, :] = d0
        out_ref[1:2, :] = d1

    return pl.pallas_call(body, name="lower_bounds_bwd", out_shape=jax.ShapeDtypeStruct(lb_raw.shape, F32), compiler_params=_params())(lb_raw, dl)


def kernel(x, c, w_mod, b_mod, w_in, conv_w, hgrn_norm_w, lower_bounds, w_branch, w_out, ln_g, ln_b, loss_target, m_w_mod, m_b_mod, m_w_in, m_conv_w, m_hgrn_norm_w, m_lower_bounds, m_w_branch, m_w_out, m_ln_g, m_ln_b, v_w_mod, v_b_mod, v_w_in, v_conv_w, v_hgrn_norm_w, v_lower_bounds, v_w_branch, v_w_out, v_ln_g, v_ln_b):
    D = D_MODEL
    x0 = x[0]
    target = loss_target[0]
    S = x0.shape[0]
    mx, my, mc = lax.axis_index("x"), lax.axis_index("y"), lax.axis_index("c")
    chip = 2 * mx + my
    me = 2 * chip + mc
    mod_cols = 3 * D // N_CHIPS

    plan = _Plan(w_in.astype(BF), w_branch.astype(BF), w_out.astype(BF), chip, mc)
    n_conv = DEPTH * 3 * (WIDTH // N_CHIPS)
    first = jnp.concatenate([c, conv_w.reshape(1, n_conv), jnp.zeros((1, 2 * D - D - n_conv), F32)], axis=1)
    first = plan.first(first.reshape(8, 2 * D // 8)).reshape(N_DEV, 2 * D)

    c_all = first[:, :D]
    conv_all = first[:, D:D + n_conv].reshape(N_DEV, DEPTH, 3, WIDTH // N_CHIPS)[0::2]
    mod_part = _all_gather8(_mod_rows(c_all, w_mod), "gather_mod")[0::2]
    mod_part = lax.dynamic_index_in_dim(mod_part, me, axis=1, keepdims=False).reshape(N_CHIPS, DEPTH, mod_cols)
    mods = [(mod_part[:, l].reshape(1, 3 * D) + b_mod[l][None, :]) for l in range(DEPTH)]
    lbs = _lbs_fwd(lower_bounds).reshape(DEPTH, WIDTH // 128, 1, 128)
    loss_blk, dx, small = _local_step(x0, target, mods, lbs, conv_all, hgrn_norm_w, ln_g, ln_b, plan)

    n_mod, n_nw, n_lb, n_ln, n_cw = DEPTH * 3 * D, DEPTH * 128, DEPTH * WIDTH, DEPTH * D, DEPTH * 3 * WIDTH
    row = jnp.concatenate(
        [jnp.concatenate([small[l][0], small[l][1], small[l][2]], axis=1) for l in range(DEPTH)]
        + [jnp.sum(small[l][3], axis=0) for l in range(DEPTH)]
        + [small[l][4].reshape(1, WIDTH) for l in range(DEPTH)]
        + [small[l][5] for l in range(DEPTH)] + [small[l][6] for l in range(DEPTH)]
        + [jnp.transpose(small[l][7], (1, 0, 2)).reshape(1, 3 * WIDTH) for l in range(DEPTH)]
        + [loss_blk[0:1, :]], axis=1)
    n_row = row.shape[1]
    fold = -(-n_row // (8 * 128)) * 128
    rows = jnp.concatenate([row, jnp.zeros((1, 8 * fold - n_row), F32)], axis=1).reshape(8, fold)

    whole, gathered = plan.finish(rows)
    grads = {kind: [[whole[(kind, l)]] for l in range(DEPTH)] for kind in ("in", "br", "out")}

    off_nw = n_mod
    off_lb = off_nw + n_nw
    off_lng = off_lb + n_lb
    off_lnb = off_lng + n_ln
    off_cw = off_lnb + n_ln
    off_loss = off_cw + n_cw
    total = _sum_devices(gathered).reshape(1, 8 * fold)
    gathered = gathered.reshape(N_DEV, 1, 8 * fold)
    d_lower = _lbs_bwd(lower_bounds, total[0, off_lb:off_lng].reshape(DEPTH, WIDTH))
    loss = total[0, off_loss]
    d_b_mod = total[0, :n_mod].reshape(DEPTH, 3 * D)
    d_norm_w = total[0, off_nw:off_lb].reshape(DEPTH, 128)
    d_ln_g = total[0, off_lng:off_lnb].reshape(DEPTH, D)
    d_ln_b = total[0, off_lnb:off_cw].reshape(DEPTH, D)
    d_conv = total[0, off_cw:off_loss].reshape(DEPTH, 3, N_CHIPS, WIDTH // N_CHIPS)
    d_conv = lax.dynamic_index_in_dim(d_conv, chip, axis=2, keepdims=False)
    dmod_all = gathered[:, 0, :n_mod].reshape(N_DEV, DEPTH, N_CHIPS, mod_cols)
    dmod_cols = jnp.transpose(lax.dynamic_index_in_dim(dmod_all, chip, axis=2, keepdims=False), (1, 0, 2))
    d_w_mod = _grad_w_mod(c_all, dmod_cols)

    res = {}
    res["w_mod"] = _adamw(w_mod, m_w_mod, v_w_mod, [[d_w_mod]], "adamw_w_mod")
    res["b_mod"] = _adamw(b_mod, m_b_mod, v_b_mod, [[d_b_mod]], "adamw_b_mod")
    res["w_in"] = _adamw(w_in, m_w_in, v_w_in, grads["in"], "adamw_w_in")
    res["conv_w"] = _adamw(conv_w, m_conv_w, v_conv_w, [[d_conv]], "adamw_conv_w")
    res["hgrn_norm_w"] = _adamw(hgrn_norm_w, m_hgrn_norm_w, v_hgrn_norm_w, [[d_norm_w]], "adamw_norm_w")
    res["lower_bounds"] = _adamw(lower_bounds, m_lower_bounds, v_lower_bounds, [[d_lower]], "adamw_lower_bounds")
    res["w_branch"] = _adamw(w_branch, m_w_branch, v_w_branch, grads["br"], "adamw_w_branch")
    res["w_out"] = _adamw(w_out, m_w_out, v_w_out, grads["out"], "adamw_w_out")
    res["ln_g"] = _adamw(ln_g, m_ln_g, v_ln_g, [[d_ln_g]], "adamw_ln_g")
    res["ln_b"] = _adamw(ln_b, m_ln_b, v_ln_b, [[d_ln_b]], "adamw_ln_b")
    names = ["w_mod", "b_mod", "w_in", "conv_w", "hgrn_norm_w", "lower_bounds", "w_branch", "w_out", "ln_g", "ln_b"]
    return (loss, dx[None], *[res[n][0] for n in names], *[res[n][1] for n in names],
            *[res[n][2] for n in names], *[res[n][3] for n in names])


class _Plan:
    FIRST_CHUNKS = 2
    WINDOWS = ((0, 1152), (1152, 640), (1792, 512))

    def __init__(self, w_in, w_br, w_out, chip, core):
        self.local = {"in": w_in, "br": w_br, "out": w_out}
        self.chip, self.where = chip, jnp.stack([chip, core]).astype(jnp.int32)
        self.gathered, self.partial, self.grads, self.chip_sums, self.scattered, self.pending = {}, {}, {}, {}, {}, {}

    def chunks(self, l):
        return self.FIRST_CHUNKS if l == 0 else 1

    def w_in(self, l):
        return [self.gathered[("in", l, c)] for c in range(self.chunks(l))]

    def _shard(self, key):
        mine = self.local[key[0]][key[1]]
        if key[0] == "in":
            cols = mine.shape[-1] // self.chunks(key[1])
            mine = mine[:, key[2] * cols:(key[2] + 1) * cols]
        return mine

    def _slab(self, key):
        mine = _halves(self._shard(key))
        return lax.dynamic_update_slice(lax.empty((N_CHIPS,) + mine.shape, mine.dtype), mine[None], (self.chip, 0, 0, 0, 0))

    def _gather(self, keys):
        return ("gather", keys), _gather_job([self._slab(key) for key in keys])

    def _gather_window(self, key, n):
        slab = self._slab(key) if n == 0 else self.partial[key]
        return ("gather" if n == len(self.WINDOWS) - 1 else "gather_part", [key]), _gather_job([slab], self.WINDOWS[n])

    def _to_sibling(self, keys):
        return ("to_sibling", keys), _to_sibling_job([_halves(self.grads[key], 1) for key in keys])

    def _scatter(self, keys):
        return ("scatter", keys), _scatter_job([self.chip_sums[key] for key in keys])

    def job(self, stage, l, c=0):
        parts = []
        if stage == "proj_fwd":
            parts = [self._gather([("in", l, c + 1)] if c + 1 < self.chunks(l) else [("br", l), ("out", l)])]
        elif stage == "attn_fwd" and l + 1 < DEPTH:
            parts = [self._gather_window(("in", l + 1, 0), 0)]
        elif stage == "hgrn_fwd" and l + 1 < DEPTH:
            parts = [self._gather_window(("in", l + 1, 0), 1)]
        elif stage == "merge_fwd" and l + 1 < DEPTH:
            parts = [self._gather_window(("in", l + 1, 0), 2)]
        elif stage == "attn_bwd":
            parts = [self._to_sibling([("out", l), ("br", l)])] + ([self._scatter([("in", l + 1)])] if l + 1 < DEPTH else [])
        elif stage == "hgrn_bwd":
            parts = [self._scatter([("out", l), ("br", l)])]
        elif stage == "proj_bwd":
            parts = [self._to_sibling([("in", l)])] if l else [self._scatter([("in", 0)])]
        self.pending[(stage, l, c)] = [(tag, len(job.outs)) for tag, job in parts]
        return _join_jobs([job for _, job in parts])

    def done(self, stage, l, outs, c=0):
        if outs is None:
            return
        at = 0
        for (what, keys), n_outs in self.pending[(stage, l, c)]:
            mine, at = outs[at:at + n_outs], at + n_outs
            for n, key in enumerate(keys):
                if what == "gather":
                    self.gathered[key] = mine[n].reshape((N_CHIPS,) + self._shard(key).shape)
                elif what == "gather_part":
                    self.partial[key] = mine[n]
                elif what == "to_sibling":
                    self.chip_sums[key] = _sum_cores(_halves(self.grads[key], 1), mine[n], self.where)
                else:
                    self.scattered[key] = mine[n]

    def first(self, rows):
        tag, job = self._gather([("in", 0, 0)])
        self.pending[("first", 0, 0)] = [(tag, len(job.outs))]
        outs = _run_job(_join_jobs([job, _gather8_job(rows)]), "gather_first")
        self.done("first", 0, outs[:-1])
        return outs[-1]

    def took(self, key, grad):
        self.grads[key] = grad
        if key == ("in", 0):
            tag, job = self._to_sibling([key])
            self.pending[("took", 0, 0)] = [(tag, len(job.outs))]
            self.done("took", 0, _run_job(job, "to_sibling_last"))

    def finish(self, rows):
        keys = [(kind, l) for kind in ("in", "br", "out") for l in range(DEPTH)]
        halves = [_sum_chips(self.chip_sums[key], self.scattered[key], self.where) for key in keys]
        outs = _run_job(_join_jobs([_place_job(halves), _gather8_job(rows)]), "place_halves")
        return {key: w.reshape(self.grads[key].shape[1:]) for key, w in zip(keys, outs[:-1])}, outs[-1]


def _local_step(x0, target, mods, lbs, conv_all, hgrn_norm_w, ln_g, ln_b, plan):
    D = D_MODEL
    after, before = _attn_consts()
    hg_consts = _hgrn_consts()

    saved = []
    xl = x0
    for l in range(DEPTH):
        n = plan.chunks(l)
        (proj, h), got = _proj_fwd(xl, mods[l], plan.gathered[("in", l, 0)], plan.job("proj_fwd", l, 0), (0, n))
        plan.done("proj_fwd", l, got, 0)
        for c in range(1, n):
            (proj,), got = _proj_cols(h, plan.gathered[("in", l, c)], (c, n), proj, plan.job("proj_fwd", l, c))
            plan.done("proj_fwd", l, got, c)
        (o_a, tot), got = _attn_fwd(proj, after, plan.job("attn_fwd", l))
        plan.done("attn_fwd", l, got)
        (o_b, states), got = _hgrn_fwd(proj, lbs[l], hg_consts, plan.job("hgrn_fwd", l))
        plan.done("hgrn_fwd", l, got)
        ys = _branch_fwd(proj, o_a, o_b, hgrn_norm_w, conv_all, l)
        wb = jnp.concatenate(list(plan.gathered[("br", l)]), axis=-1)
        (x_next, merged, y), got = _merge_fwd(xl, mods[l], proj, ys, wb, plan.gathered[("out", l)], ln_g, ln_b, l, plan.job("merge_fwd", l))
        plan.done("merge_fwd", l, got)
        saved.append((xl, proj, h, o_a, tot, o_b, states, ys, merged, y, wb))
        xl = x_next
    dx, loss_blk = _loss_head(xl, target)

    small = [None] * DEPTH
    for l in reversed(range(DEPTH)):
        xin, proj, h, o_a, tot, o_b, states, ys, merged, y, wb = saved[l]
        dx_res, dproj, dy_a, dy_b, dy_c, dln_g, dln_b, dgate, g_br, g_out = _merge_bwd(
            dx, xin, y, merged, mods[l], proj, ys, wb, plan.gathered[("out", l)], ln_g, l)
        plan.took(("out", l), g_out)
        plan.took(("br", l), g_br)
        d_oa, d_ob, dnorm_w, dconv_w, dproj = _branch_bwd(proj, o_a, o_b, hgrn_norm_w, conv_all, dy_a, dy_b, dy_c, dproj, l)
        (dproj,), got = _attn_bwd(proj, d_oa, tot, after, before, dproj, plan.job("attn_bwd", l))
        plan.done("attn_bwd", l, got)
        (dlb, dproj), got = _hgrn_bwd(proj, lbs[l], states, d_ob, hg_consts, dproj, plan.job("hgrn_bwd", l))
        plan.done("hgrn_bwd", l, got)
        plan.took(("in", l), _grad_w_in(h, dproj))
        wgs = plan.w_in(l)
        tile = {"tk": wgs[0].shape[-1]} if len(wgs) > 1 else {}
        (dx, dshift, dscale), got = _proj_bwd(dproj, wgs, xin, mods[l], dx_res, plan.job("proj_bwd", l), **tile)
        plan.done("proj_bwd", l, got)
        small[l] = (dshift, dscale, dgate, dnorm_w, dlb, dln_g, dln_b, dconv_w)
    return loss_blk, dx, small
```

```python
import functools
import math

import numpy as np
import jax
import jax.numpy as jnp
from jax import lax
from jax.experimental import pallas as pl
from jax.experimental.pallas import tpu as pltpu

F32 = jnp.float32
BF = jnp.bfloat16
MESH = pl.DeviceIdType.MESH

DEPTH = 2
D_MODEL = 1024
WIDTH = 512
IN_COLS = 12 * WIDTH + 3 * D_MODEL
N_CHIPS = 4
N_DEV = 8
SB_BLOCK = 128
SB_HEAD_DIM = 64
HG_CHUNK = 128
HG_DIM = 128
LN_EPS = 1e-5
RMS_EPS = 1e-6
ALPHA = (2.0 * DEPTH) ** 0.25
ADAM_LR, ADAM_B1, ADAM_B2, ADAM_EPS, ADAM_WD, ADAM_STEP = 0.001, 0.9, 0.999, 1e-08, 0.01, 10
VMEM_LIMIT = 56 << 20


def _params(**kw):
    return pltpu.CompilerParams(vmem_limit_bytes=VMEM_LIMIT, **kw)


def _dot(a, b):
    return jnp.dot(a, b, preferred_element_type=F32)


def _dot_nt(a, b):
    return lax.dot_general(a, b, (((1,), (1,)), ((), ())), preferred_element_type=F32)


def _dot_tn(a, b):
    return lax.dot_general(a, b, (((0,), (0,)), ((), ())), preferred_element_type=F32)


def _sigmoid(x):
    return 1.0 / (1.0 + jnp.exp(-x))


def _silu(x):
    return x * _sigmoid(x)


def _standardize(x):
    mu = jnp.mean(x, axis=-1, keepdims=True)
    xc = x - mu
    var = jnp.mean(xc * xc, axis=-1, keepdims=True)
    rstd = lax.rsqrt(var + LN_EPS)
    return xc * rstd, rstd


def _standardize_bwd(dxs, xs, rstd):
    return rstd * (dxs - jnp.mean(dxs, axis=-1, keepdims=True) - xs * jnp.mean(dxs * xs, axis=-1, keepdims=True))


class _Job:
    def __init__(self, ins, outs, sems, make, alias=None):
        self.ins, self.outs, self.sems, self.make = list(ins), list(outs), list(sems), make
        self.alias = dict(alias or {})


def _join_jobs(jobs):
    jobs = [j for j in jobs if j is not None]
    if len(jobs) <= 1:
        return jobs[0] if jobs else None

    def make(ins, outs, sems):
        phases, i, o, s = [], 0, 0, 0
        for j in jobs:
            got = j.make(ins[i:i + len(j.ins)], outs[o:o + len(j.outs)], sems[s:s + len(j.sems)])
            i, o, s = i + len(j.ins), o + len(j.outs), s + len(j.sems)
            for n, phase in enumerate(got):
                if n == len(phases):
                    phases.append([])
                phases[n] += phase
        return phases

    alias, i, o = {}, 0, 0
    for j in jobs:
        alias.update({i + a: o + b for a, b in j.alias.items()})
        i, o = i + len(j.ins), o + len(j.outs)
    return _Job(sum([j.ins for j in jobs], []), sum([j.outs for j in jobs], []), sum([j.sems for j in jobs], []), make, alias)


def _flip(v, bit):
    return 1 - v if bit else v


def _halves(a, front=0):
    shape = a.shape
    lead = math.prod(shape[front:-2])
    return a.reshape(shape[:front] + (lead, 2, shape[-2] // 2, shape[-1]))


def _dma_sems(*shapes):
    return [pltpu.SemaphoreType.DMA(s) for s in shapes]


def _same(arrays):
    return [jax.ShapeDtypeStruct(a.shape, a.dtype) for a in arrays]


def _gather_job(slabs, window=None):
    n = len(slabs)
    cols = slice(None) if window is None else pl.ds(*window)

    def make(ins, outs, sems):
        send1, recv1, send2, recv2 = sems
        mx, my, mc = lax.axis_index("x"), lax.axis_index("y"), lax.axis_index("c")
        fetch, pass_on = [], []
        for a in range(n):
            ours = outs[a].at[2 * mx + my, :, mc, :, cols]
            for k in range(1, N_CHIPS):
                px, py = _flip(mx, k & 2), _flip(my, k & 1)
                fetch.append(pltpu.make_async_remote_copy(
                    src_ref=ours, dst_ref=ours, send_sem=send1.at[a, k - 1], recv_sem=recv1.at[a, k - 1],
                    device_id=(px, py, mc), device_id_type=MESH))
                theirs = outs[a].at[2 * px + py, :, mc, :, cols]
                pass_on.append(pltpu.make_async_remote_copy(
                    src_ref=theirs, dst_ref=theirs, send_sem=send2.at[a, k - 1], recv_sem=recv2.at[a, k - 1],
                    device_id=(mx, my, 1 - mc), device_id_type=MESH))
        return [fetch, pass_on]

    pairs = (n, N_CHIPS - 1)
    return _Job(slabs, _same(slabs), _dma_sems(pairs, pairs, pairs, pairs), make, {a: a for a in range(n)})


def _to_sibling_job(grads):
    n = len(grads)

    def make(ins, outs, sems):
        send_sems, recv_sems = sems
        mx, my, mc = lax.axis_index("x"), lax.axis_index("y"), lax.axis_index("c")
        return [[pltpu.make_async_remote_copy(
            src_ref=ins[a].at[:, :, 1 - mc], dst_ref=outs[a], send_sem=send_sems.at[a], recv_sem=recv_sems.at[a],
            device_id=(mx, my, 1 - mc), device_id_type=MESH) for a in range(n)]]

    outs = [jax.ShapeDtypeStruct(g.shape[:2] + g.shape[3:], g.dtype) for g in grads]
    return _Job(grads, outs, _dma_sems((n,), (n,)), make)


def _scatter_job(sums):
    n = len(sums)

    def make(ins, outs, sems):
        send_sems, recv_sems = sems
        mx, my, mc = lax.axis_index("x"), lax.axis_index("y"), lax.axis_index("c")
        copies = []
        for a in range(n):
            for k in range(1, N_CHIPS):
                px, py = _flip(mx, k & 2), _flip(my, k & 1)
                copies.append(pltpu.make_async_remote_copy(
                    src_ref=ins[a].at[2 * px + py], dst_ref=outs[a].at[k - 1], send_sem=send_sems.at[a, k - 1],
                    recv_sem=recv_sems.at[a, k - 1], device_id=(px, py, mc), device_id_type=MESH))
        return [copies]

    pairs = (n, N_CHIPS - 1)
    return _Job(sums, [jax.ShapeDtypeStruct((N_CHIPS - 1,) + s.shape[1:], s.dtype) for s in sums], _dma_sems(pairs, pairs), make)


def _place_job(wholes):
    n = len(wholes)

    def make(ins, outs, sems):
        send_sems, recv_sems = sems
        mx, my, mc = lax.axis_index("x"), lax.axis_index("y"), lax.axis_index("c")
        copies = []
        for a in range(n):
            here = outs[a].at[:, mc]
            copies.append(pltpu.make_async_remote_copy(src_ref=here, dst_ref=here, send_sem=send_sems.at[a], recv_sem=recv_sems.at[a],
                                                       device_id=(mx, my, 1 - mc), device_id_type=MESH))
        return [copies]

    return _Job(wholes, _same(wholes), _dma_sems((n,), (n,)), make, {a: a for a in range(n)})


def _gather8_job(x):
    def make(ins, outs, sems):
        local_sem, send_sems, recv_sems = sems
        mx, my, mc = lax.axis_index("x"), lax.axis_index("y"), lax.axis_index("c")
        here = outs[0].at[4 * mx + 2 * my + mc]
        copies = [pltpu.make_async_copy(ins[0], here, local_sem.at[0])]
        for k in range(1, N_DEV):
            peer = (_flip(mx, k & 4), _flip(my, k & 2), _flip(mc, k & 1))
            copies.append(pltpu.make_async_remote_copy(src_ref=ins[0], dst_ref=here, send_sem=send_sems.at[k - 1],
                                                       recv_sem=recv_sems.at[k - 1], device_id=peer, device_id_type=MESH))
        return [copies]

    return _Job([x], [jax.ShapeDtypeStruct((N_DEV,) + x.shape, x.dtype)], _dma_sems((1,), (N_DEV - 1,), (N_DEV - 1,)), make)


def _run_phases(phases, first=0):
    for n, phase in enumerate(phases):
        if n >= first:
            for cp in phase:
                cp.start()
        for cp in phase:
            cp.wait()


def _run_job(job, name):
    k_in, k_out = len(job.ins), len(job.outs)

    def body(*refs):
        _run_phases(job.make(refs[:k_in], refs[k_in:k_in + k_out], refs[k_in + k_out:]))

    hbm = pl.BlockSpec(memory_space=pl.ANY)
    return pl.pallas_call(body, name=name, in_specs=[hbm] * k_in, out_specs=[hbm] * k_out, out_shape=job.outs,
                          scratch_shapes=job.sems, input_output_aliases=job.alias, compiler_params=_params())(*job.ins)


def _hosted(body, job, args, *, name, grid, in_specs, out_specs, out_shape, scratch_shapes=(), semantics, aliases=None):
    in_specs, out_specs, out_shape, scratch = list(in_specs), list(out_specs), list(out_shape), list(scratch_shapes)
    aliases = dict(aliases or {})
    if job is None:
        outs = pl.pallas_call(body, name=name, grid=grid, in_specs=in_specs, out_specs=out_specs, out_shape=out_shape,
                              scratch_shapes=scratch, input_output_aliases=aliases,
                              compiler_params=_params(dimension_semantics=semantics))(*args)
        return list(outs), None
    n_in, n_out, n_scr, k_in, k_out = len(in_specs), len(out_specs), len(scratch), len(job.ins), len(job.outs)

    def wrapped(*refs):
        ins, rest = refs[:n_in], refs[n_in:]
        job_ins, rest = rest[:k_in], rest[k_in:]
        outs, rest = rest[:n_out], rest[n_out:]
        job_outs, rest = rest[:k_out], rest[k_out:]
        scr, sems = rest[:n_scr], rest[n_scr:]
        ids = [pl.program_id(a) for a in range(len(grid))]
        first = functools.reduce(jnp.logical_and, [i == 0 for i in ids])
        last = functools.reduce(jnp.logical_and, [i == g - 1 for i, g in zip(ids, grid)])

        @pl.when(first)
        def _():
            for cp in job.make(job_ins, job_outs, sems)[0]:
                cp.start()

        body(*ins, *outs, *scr)

        @pl.when(last)
        def _():
            _run_phases(job.make(job_ins, job_outs, sems), first=1)

    hbm = pl.BlockSpec(memory_space=pl.ANY)
    outs = pl.pallas_call(
        wrapped, name=name, grid=grid, in_specs=in_specs + [hbm] * k_in, out_specs=out_specs + [hbm] * k_out,
        out_shape=out_shape + job.outs, scratch_shapes=scratch + job.sems,
        input_output_aliases={**aliases, **{n_in + i: n_out + o for i, o in job.alias.items()}},
        compiler_params=_params(dimension_semantics=("arbitrary",) * len(grid)))(*args, *job.ins)
    return list(outs[:n_out]), list(outs[n_out:])


def _proj_fwd(x, mod, wg, job=None, chunk=(0, 1), tm=512):
    S, D = x.shape
    tm = min(tm, S)
    tn = wg.shape[-1]
    c, n = chunk

    rows = S // tm

    def body(x_ref, mod_ref, w_ref, proj_ref, h_ref, hs):
        mine = pl.ds(pl.multiple_of(pl.program_id(1) * tm, tm), tm)

        @pl.when(pl.program_id(0) == 0)
        def _():
            xs, _ = _standardize(x_ref[...])
            h = xs * (1.0 + mod_ref[:, D:2 * D]) + mod_ref[:, 0:D]
            hb = h.astype(BF)
            hs[mine, :] = hb
            h_ref[...] = hb

        proj_ref[...] = _dot(hs[mine, :], w_ref[...])

    once = lambda j, i: jnp.where(j == 0, i, rows - 1)
    return _hosted(
        body, job, (x, mod, wg), name="proj_fwd",
        grid=(N_CHIPS, rows),
        in_specs=[pl.BlockSpec((tm, D), lambda j, i: (once(j, i), 0)),
                  pl.BlockSpec((1, 3 * D), lambda j, i: (0, 0)),
                  pl.BlockSpec((None, D, tn), lambda j, i: (j, 0, 0))],
        out_specs=[pl.BlockSpec((tm, tn), lambda j, i: (i, j * n + c)),
                   pl.BlockSpec((tm, D), lambda j, i: (once(j, i), 0))],
        out_shape=[jax.ShapeDtypeStruct((S, IN_COLS), F32), jax.ShapeDtypeStruct((S, D), BF)],
        scratch_shapes=[pltpu.VMEM((S, D), BF)],
        semantics=("arbitrary", "arbitrary"))


def _proj_cols(h, wg, chunk, proj, job=None, tm=512):
    S, D = h.shape
    tm = min(tm, S)
    tn = wg.shape[-1]
    c, n = chunk

    def body(h_ref, w_ref, prev_ref, proj_ref):
        mine = pl.ds(pl.multiple_of(pl.program_id(1) * tm, tm), tm)
        proj_ref[...] = _dot(h_ref[mine, :], w_ref[...])

    in_specs = [pl.BlockSpec((S, D), lambda j, i: (0, 0)), pl.BlockSpec((None, D, tn), lambda j, i: (j, 0, 0)),
                pl.BlockSpec(memory_space=pl.ANY)]
    return _hosted(body, job, (h, wg, proj), name="proj_cols", grid=(N_CHIPS, S // tm), in_specs=in_specs,
                   out_specs=[pl.BlockSpec((tm, tn), lambda j, i: (i, j * n + c))],
                   out_shape=[jax.ShapeDtypeStruct(proj.shape, proj.dtype)],
                   semantics=("arbitrary", "arbitrary"), aliases={2: 0})


SB_ROWS = 256
SB_KEYS = 256


def _attn_consts():
    j = np.arange(SB_KEYS)[:, None]
    s = np.arange(SB_KEYS)[None, :]
    from_here = np.concatenate([(j >= s), (j >= s)], axis=0).astype(np.float32)
    return jnp.asarray(from_here, BF), jnp.asarray((j <= s).astype(np.float32), BF)


def _hi_lo(x):
    hi = lax.bitcast_convert_type(lax.bitcast_convert_type(x, jnp.uint32) & jnp.uint32(0xFFFF0000), F32)
    return hi.astype(BF), (x - hi).astype(BF)


def _sums_r(x, t2):
    hi, lo = _hi_lo(x)
    return _dot(jnp.concatenate([hi, lo], axis=1), t2)


def _all_lanes(col, lanes):
    return jnp.broadcast_to(col, (col.shape[0], lanes))


def _attn_rows(ref, r0, rows, lanes, head0, scale=None):
    v = ref[pl.ds(r0, rows), lanes]
    if scale is not None:
        v = v * scale
    return jnp.concatenate([jnp.where(head0, v, 0.0), jnp.where(head0, 0.0, v)], axis=0).astype(BF)


SB_PAIRS_FWD = 4
SB_PAIRS_BWD = 2


def _attn_specs(S, n_pairs):
    return lambda base: pl.BlockSpec((S, n_pairs * SB_BLOCK), lambda p, base=base: (0, base // n_pairs + p))


def _attn_scores(q2n, k_ref, lanes, kj, t2, from_here_ref, masked):
    c0 = pl.multiple_of(kj * SB_KEYS, SB_KEYS)
    kb = k_ref[pl.ds(c0, SB_KEYS), lanes].astype(BF)
    zn = _dot_nt(q2n, kb)
    lsb = jnp.minimum(zn, 0.0) - jnp.log(1.0 + jnp.exp(-jnp.abs(zn)))
    valid = None
    if masked:
        valid = (lax.broadcasted_iota(jnp.int32, zn.shape, 1) + kj * SB_KEYS) < t2
        lsb = jnp.where(valid, lsb, 0.0)
    return c0, kb, zn, valid, lsb, _sums_r(lsb, from_here_ref[...])


def _attn_fwd(proj, from_here, job=None):
    S = proj.shape[0]
    TQ = SB_ROWS
    assert S % TQ == 0 and SB_KEYS == TQ
    scale = SB_HEAD_DIM ** -0.5
    n_pairs = SB_PAIRS_FWD
    pairs = range(n_pairs)
    lanes = [pl.ds(p * SB_BLOCK, SB_BLOCK) for p in pairs]

    def body(q_ref, k_ref, v_ref, from_here_ref, o_ref, tot_ref, run, acc):
        head0 = lax.broadcasted_iota(jnp.int32, (1, 2 * SB_HEAD_DIM), 1) < SB_HEAD_DIM

        def qloop(qi, _):
            r0 = pl.multiple_of(qi * TQ, TQ)
            q2n = [_attn_rows(q_ref, r0, TQ, lanes[p], head0, -scale) for p in pairs]
            trow = lax.broadcasted_iota(jnp.int32, (TQ, SB_KEYS), 0) + qi * TQ
            t2 = jnp.concatenate([trow, trow], axis=0)
            run[...] = jnp.zeros_like(run)
            acc[...] = jnp.zeros_like(acc)

            def step(kj, masked):
                got = [_attn_scores(q2n[p], k_ref, lanes[p], kj, t2, from_here_ref, masked) for p in pairs]
                for p in pairs:
                    c0, _, zn, valid, _, sums = got[p]
                    r = run[p]
                    e = sums - zn + jnp.concatenate([r, r], axis=1)
                    if masked:
                        e = jnp.where(valid, e, -jnp.inf)
                    acc[p] += _dot(jnp.exp(e).astype(BF), v_ref[pl.ds(c0, SB_KEYS), lanes[p]].astype(BF))
                    run[p] = r + _all_lanes(sums[:, 0:1], SB_BLOCK)

            @pl.when(qi % 2 == 0)
            def _():
                step(qi, True)

            @pl.when(qi % 2 == 1)
            def _():
                step(qi, True)
                step(qi - 1, False)

            first = qi - 1 - qi % 2

            def below(n, _):
                step(first - 2 * n, False)
                step(first - 1 - 2 * n, False)
                return 0

            lax.fori_loop(0, qi // 2, below, 0)
            for p in pairs:
                o_ref[pl.ds(r0, TQ), lanes[p]] = jnp.where(head0, acc[p, 0:TQ, :], acc[p, TQ:2 * TQ, :])
                tot_ref[p, 0, pl.ds(r0, TQ), :] = run[p, 0:TQ, :]
                tot_ref[p, 1, pl.ds(r0, TQ), :] = run[p, TQ:2 * TQ, :]
            return 0

        lax.fori_loop(0, S // TQ, qloop, 0)

    col = _attn_specs(S, n_pairs)
    state = pltpu.VMEM((n_pairs, 2 * TQ, SB_BLOCK), F32)
    return _hosted(
        body, job, (proj, proj, proj, from_here), name="attn_fwd",
        grid=(WIDTH // (n_pairs * SB_BLOCK),),
        in_specs=[col(0), col(4), col(8), pl.BlockSpec(from_here.shape, lambda p: (0, 0))],
        out_specs=[col(0), pl.BlockSpec((n_pairs, 2, S, 128), lambda p: (p, 0, 0, 0))],
        out_shape=[jax.ShapeDtypeStruct((S, WIDTH), F32), jax.ShapeDtypeStruct((WIDTH // 128, 2, S, 128), F32)],
        scratch_shapes=[state, state],
        semantics=("parallel",))


def _attn_bwd(proj, d_o, tot, from_here, up_to, dproj, job=None):
    S = proj.shape[0]
    TQ = SB_ROWS
    assert S % TQ == 0 and SB_KEYS == TQ
    scale = SB_HEAD_DIM ** -0.5
    n_pairs = SB_PAIRS_BWD
    pairs = range(n_pairs)
    lanes = [pl.ds(p * SB_BLOCK, SB_BLOCK) for p in pairs]

    def body(q_ref, k_ref, v_ref, do_ref, tot_ref, from_here_ref, up_to_ref, _, dproj_ref, pre, cum, dq_acc, dk_acc, dv_acc, stage, sems):
        dq_ref, dk_ref, dv_ref = stage.at[0], stage.at[1], stage.at[2]
        head0 = lax.broadcasted_iota(jnp.int32, (1, 2 * SB_HEAD_DIM), 1) < SB_HEAD_DIM
        dk_acc[...] = jnp.zeros_like(dk_acc)
        dv_acc[...] = jnp.zeros_like(dv_acc)

        def qloop(qi, _):
            r0 = pl.multiple_of(qi * TQ, TQ)
            q2n = [_attn_rows(q_ref, r0, TQ, lanes[p], head0, -scale) for p in pairs]
            do2 = [_attn_rows(do_ref, r0, TQ, lanes[p], head0) for p in pairs]
            trow = lax.broadcasted_iota(jnp.int32, (TQ, SB_KEYS), 0) + qi * TQ
            t2 = jnp.concatenate([trow, trow], axis=0)
            for p in pairs:
                pre[p, 0:TQ, :] = tot_ref[p, 0, pl.ds(r0, TQ), :]
                pre[p, TQ:2 * TQ, :] = tot_ref[p, 1, pl.ds(r0, TQ), :]
            cum[...] = jnp.zeros_like(cum)
            dq_acc[...] = jnp.zeros_like(dq_acc)

            def step(kj, masked):
                got = [_attn_scores(q2n[p], k_ref, lanes[p], kj, t2, from_here_ref, masked) for p in pairs]
                das = [_dot_nt(do2[p], v_ref[pl.ds(got[p][0], SB_KEYS), lanes[p]].astype(BF)) for p in pairs]
                for p in pairs:
                    c0, kb, zn, valid, lsb, sums = got[p]
                    later = pre[p] - _all_lanes(sums[:, 0:1], SB_BLOCK)
                    pre[p] = later
                    e = sums - zn + jnp.concatenate([later, later], axis=1)
                    sig = jnp.exp(lsb - zn)
                    if masked:
                        e = jnp.where(valid, e, -jnp.inf)
                        sig = jnp.where(valid, sig, 0.0)
                    a = jnp.exp(e)
                    w = das[p] * a
                    upto = _dot(w.astype(BF), up_to_ref[...])
                    c = cum[p]
                    dz = w - sig * (upto + jnp.concatenate([c, c], axis=1))
                    cum[p] = c + _all_lanes(upto[:, SB_KEYS - 1:SB_KEYS], SB_BLOCK)
                    dzb = dz.astype(BF)
                    dq_acc[p] += _dot(dzb, kb)
                    dk_acc[pl.ds(c0, SB_KEYS), lanes[p]] += _dot_tn(dzb, q2n[p])
                    dv_acc[pl.ds(c0, SB_KEYS), lanes[p]] += _dot_tn(a.astype(BF), do2[p])

            def below(n, _):
                step(2 * n, False)
                step(2 * n + 1, False)
                return 0

            lax.fori_loop(0, qi // 2, below, 0)

            @pl.when(qi % 2 == 1)
            def _():
                step(qi - 1, False)
                step(qi, True)

            @pl.when(qi % 2 == 0)
            def _():
                step(qi, True)
            for p in pairs:
                dq_ref[pl.ds(r0, TQ), lanes[p]] = (jnp.where(head0, dq_acc[p, 0:TQ, :], dq_acc[p, TQ:2 * TQ, :]) * scale).astype(BF)
            return 0

        lax.fori_loop(0, S // TQ, qloop, 0)
        dk_ref[...] = (-dk_acc[...]).astype(BF)
        dv_ref[...] = dv_acc[...].astype(BF)
        lane0 = pl.program_id(0) * (n_pairs * SB_BLOCK)
        _put_columns([dq_ref, dk_ref, dv_ref], dproj_ref, [WIDTH * n + lane0 for n in range(3)], sems)

    col = _attn_specs(S, n_pairs)
    whole = lambda a: pl.BlockSpec(a.shape, lambda p: (0, 0))
    hbm = pl.BlockSpec(memory_space=pl.ANY)
    state = pltpu.VMEM((n_pairs, 2 * TQ, SB_BLOCK), F32)
    grads = pltpu.VMEM((S, n_pairs * SB_BLOCK), F32)
    return _hosted(
        body, job, (proj, proj, proj, d_o, tot, from_here, up_to, dproj), name="attn_bwd",
        grid=(WIDTH // (n_pairs * SB_BLOCK),),
        in_specs=[col(0), col(4), col(8), col(0), pl.BlockSpec((n_pairs, 2, S, 128), lambda p: (p, 0, 0, 0)), whole(from_here),
                  whole(up_to), hbm],
        out_specs=[hbm],
        out_shape=[jax.ShapeDtypeStruct(dproj.shape, dproj.dtype)],
        scratch_shapes=[state, state, state, grads, grads, pltpu.VMEM((3, S, n_pairs * SB_BLOCK), BF), pltpu.SemaphoreType.DMA((3,))],
        semantics=("arbitrary",), aliases={7: 0})


HG_LEVELS = tuple(HG_CHUNK >> n for n in range(1, HG_CHUNK.bit_length()))


def _hgrn_consts():
    C = HG_CHUNK
    t = np.arange(C)[:, None]
    s = np.arange(C)[None, :]
    rows = [(s <= t), (s > t)]
    masks = [(t == s)]
    for m in HG_LEVELS:
        two = 2 * m
        mid = (t // two) * two + m
        right = (t % two) >= m
        rows.append((right & (s >= mid) & (s <= t)) | ((~right) & (s > t) & (s <= mid - 1)))
        masks.append(((t // two) == (s // two)) & right & ((s % two) < m))
    tri = np.concatenate(rows, axis=0).astype(np.float32)
    twice = lambda a: jnp.asarray(np.concatenate([a, a], axis=1), BF)
    return (twice(tri), twice(tri.T), jnp.asarray(np.stack(masks).astype(np.float32), F32))


HG_SUM_BLOCKS = 2 + len(HG_LEVELS)


def _split_rows(g):
    hi = g.astype(BF)
    return jnp.concatenate([hi, (g - hi.astype(F32)).astype(BF)], axis=0)


def _hgrn_sum_blocks(e):
    C = HG_CHUNK
    blocks = tuple(e[n * C:(n + 1) * C] for n in range(HG_SUM_BLOCKS))
    return blocks + (jnp.broadcast_to(e[C - 1:C], (HG_DIM, e.shape[1])),)


@jax.custom_vjp
def _hgrn_sums(tri, tri_t, g):
    return _hgrn_sum_blocks(_dot(tri, _split_rows(g)))


def _hgrn_sums_fwd(tri, tri_t, g):
    return _hgrn_sums(tri, tri_t, g), (tri, tri_t)


def _hgrn_sums_bwd(res, ds):
    tri, tri_t = res
    C = HG_CHUNK
    last = lax.broadcasted_iota(jnp.int32, (C, 1), 0) == C - 1
    prefix = ds[0] + jnp.where(last, jnp.sum(ds[-1], axis=0, keepdims=True), 0.0)
    d = jnp.concatenate((prefix,) + tuple(ds[1:-1]), axis=0)
    return jnp.zeros_like(tri), jnp.zeros_like(tri_t), _dot(tri_t[:, :d.shape[0]], d.astype(BF))


_hgrn_sums.defvjp(_hgrn_sums_fwd, _hgrn_sums_bwd)


def _bf_dot(a, b):
    return _dot(a.astype(BF), b.astype(BF))


def _bf_dot_nt(a, b):
    return _dot_nt(a.astype(BF), b.astype(BF))


def _bf_dot_tn(a, b):
    return _dot_tn(a.astype(BF), b.astype(BF))


@jax.custom_vjp
def _mm(a, b):
    return _bf_dot(a, b)


_mm.defvjp(lambda a, b: (_bf_dot(a, b), (a, b)), lambda r, ct: (_bf_dot_nt(ct, r[1]), _bf_dot_tn(r[0], ct)))


@jax.custom_vjp
def _mm_nt(a, b):
    return _bf_dot_nt(a, b)


_mm_nt.defvjp(lambda a, b: (_bf_dot_nt(a, b), (a, b)), lambda r, ct: (_bf_dot(ct, r[1]), _bf_dot_tn(ct, r[0])))


@jax.custom_vjp
def _mm_tn(a, b):
    return _bf_dot_tn(a, b)


_mm_tn.defvjp(lambda a, b: (_bf_dot_tn(a, b), (a, b)), lambda r, ct: (_bf_dot_nt(r[1], ct), _bf_dot(r[0], ct)))


def _hgrn_gates(tri, tri_t, qraw, fpre, lb):
    q = _silu(qraw)
    f = lb + (1.0 - lb) * _sigmoid(fpre)
    return q, 1.0 - f, _hgrn_sums(tri, tri_t, jnp.log(f))


def _hgrn_mix(masks, q, k, e, v, st):
    prefix, suffix, whole = e[0], e[1], e[-1]
    scores = masks[0] * _mm_nt(q, k)
    for n in range(len(HG_LEVELS)):
        decay = jnp.exp(e[2 + n])
        scores = scores + masks[n + 1] * _mm_nt(q * decay, k * decay)
    o = _mm_nt(q * jnp.exp(prefix), st) + _mm(scores, v)
    st_new = st * jnp.exp(whole) + _mm_tn(v, k * jnp.exp(suffix))
    return o, st_new


def _hgrn_chunk(tri, tri_t, masks, qraw, fpre, v, st, lb):
    q, k, e = _hgrn_gates(tri, tri_t, qraw, fpre, lb)
    return _hgrn_mix(masks, q, k, e, v, st)


HG_HEADS_PER_STEP = 4
HG_LANES = HG_HEADS_PER_STEP * HG_DIM


def _hgrn_specs(S, consts):
    col = lambda base: pl.BlockSpec((S, HG_LANES), lambda p, base=base: (0, base // HG_HEADS_PER_STEP + p))
    whole = [pl.BlockSpec(a.shape, lambda p, n=a.ndim: (0,) * n) for a in consts]
    return col, whole


def _hgrn_fwd(proj, lbs, consts, job=None):
    S = proj.shape[0]
    nc = S // HG_CHUNK
    heads = range(HG_HEADS_PER_STEP)

    def body(q_ref, f_ref, i_ref, lb_ref, tri_ref, trit_ref, mask_ref, o_ref, st_ref):
        tri, tri_t = tri_ref[...], trit_ref[...]
        masks = [mask_ref[n] for n in range(len(HG_LEVELS) + 1)]

        def chunk(ci, sts):
            r0 = pl.multiple_of(ci * HG_CHUNK, HG_CHUNK)
            rows = pl.ds(r0, HG_CHUNK)
            new = []
            lane = [pl.ds(hd * HG_DIM, HG_DIM) for hd in heads]
            gates = [_hgrn_gates(tri, tri_t, q_ref[rows, lane[hd]], f_ref[rows, lane[hd]], lb_ref[hd]) for hd in heads]
            for hd in heads:
                st_ref[hd, ci] = sts[hd]
                o, st_new = _hgrn_mix(masks, *gates[hd], i_ref[rows, lane[hd]], sts[hd])
                o_ref[rows, lane[hd]] = o
                new.append(st_new)
            return tuple(new)

        lax.fori_loop(0, nc, chunk, tuple(jnp.zeros((HG_DIM, HG_DIM), F32) for _ in heads))

    col, whole = _hgrn_specs(S, consts)
    return _hosted(
        body, job, (proj, proj, proj, lbs, *consts), name="hgrn_fwd",
        grid=(WIDTH // HG_LANES,),
        in_specs=[col(16), col(20), col(24), pl.BlockSpec((HG_HEADS_PER_STEP, 1, 128), lambda p: (p, 0, 0))] + whole,
        out_specs=[col(0), pl.BlockSpec((HG_HEADS_PER_STEP, nc, HG_DIM, HG_DIM), lambda p: (p, 0, 0, 0))],
        out_shape=[jax.ShapeDtypeStruct((S, WIDTH), F32), jax.ShapeDtypeStruct((WIDTH // 128, nc, HG_DIM, HG_DIM), F32)],
        semantics=("parallel",))


def _hgrn_bwd(proj, lbs, states, d_o, consts, dproj, job=None):
    S = proj.shape[0]
    nc = S // HG_CHUNK

    def body(q_ref, f_ref, i_ref, lb_ref, st_ref, do_ref, tri_ref, trit_ref, mask_ref, _, dlb_ref, dproj_ref, stage, sems):
        dq_ref, df_ref, di_ref = stage.at[0], stage.at[1], stage.at[2]
        masks = [mask_ref[n] for n in range(len(HG_LEVELS) + 1)]
        fn = functools.partial(_hgrn_chunk, tri_ref[...], trit_ref[...], masks)
        heads = range(HG_HEADS_PER_STEP)

        def chunk(n, carry):
            ci = nc - 1 - n
            r0 = pl.multiple_of(ci * HG_CHUNK, HG_CHUNK)
            rows = pl.ds(r0, HG_CHUNK)
            new = []
            lane = [pl.ds(hd * HG_DIM, HG_DIM) for hd in heads]
            pulls = [jax.vjp(fn, q_ref[rows, lane[hd]], f_ref[rows, lane[hd]], i_ref[rows, lane[hd]], st_ref[hd, ci], lb_ref[hd])[1]
                     for hd in heads]
            for hd in heads:
                d_st, dlb = carry[hd]
                lanes = lane[hd]
                dq, df, di, d_prev, dl = pulls[hd]((do_ref[rows, lanes], d_st))
                dq_ref[rows, lanes] = dq.astype(BF)
                df_ref[rows, lanes] = df.astype(BF)
                di_ref[rows, lanes] = di.astype(BF)
                new.append((d_prev, dlb + dl))
            return tuple(new)

        zero = (jnp.zeros((HG_DIM, HG_DIM), F32), jnp.zeros((1, HG_DIM), F32))
        done = lax.fori_loop(0, nc, chunk, tuple(zero for _ in heads))
        for hd in heads:
            dlb_ref[hd] = done[hd][1]
        lane0 = pl.program_id(0) * HG_LANES
        _put_columns([dq_ref, df_ref, di_ref], dproj_ref, [WIDTH * n + lane0 for n in (4, 5, 6)], sems)

    col, whole = _hgrn_specs(S, consts)
    head = pl.BlockSpec((HG_HEADS_PER_STEP, 1, 128), lambda p: (p, 0, 0))
    hbm = pl.BlockSpec(memory_space=pl.ANY)
    n_in = 6 + len(consts)
    return _hosted(
        body, job, (proj, proj, proj, lbs, states, d_o, *consts, dproj), name="hgrn_bwd",
        grid=(WIDTH // HG_LANES,),
        in_specs=[col(16), col(20), col(24), head, pl.BlockSpec((HG_HEADS_PER_STEP, nc, HG_DIM, HG_DIM), lambda p: (p, 0, 0, 0)), col(0)]
        + whole + [hbm],
        out_specs=[head, hbm],
        out_shape=[jax.ShapeDtypeStruct((WIDTH // 128, 1, 128), F32), jax.ShapeDtypeStruct(dproj.shape, dproj.dtype)],
        scratch_shapes=[pltpu.VMEM((3, S, HG_LANES), BF), pltpu.SemaphoreType.DMA((3,))],
        semantics=("arbitrary",), aliases={n_in: 1})


def _shift_down(x, n):
    rows = lax.broadcasted_iota(jnp.int32, x.shape, 0)
    return jnp.where(rows >= n, pltpu.roll(x, n, 0), 0.0)


def _shift_up(x, n):
    S = x.shape[0]
    rows = lax.broadcasted_iota(jnp.int32, x.shape, 0)
    return jnp.where(rows < S - n, pltpu.roll(x, S - n, 0), 0.0)


def _branch_fwd(proj, o_a, o_b, norm_w, conv_w, layer):
    S = proj.shape[0]

    def body(oa_ref, za_ref, ob_ref, zb_ref, nw_ref, pre_ref, post_ref, u_ref, zc_ref, cw_ref, ya_ref, yb_ref, yc_ref):
        ya_ref[...] = (oa_ref[...] * _silu(za_ref[...])).astype(BF)
        ob = ob_ref[...]
        rn = lax.rsqrt(jnp.mean(ob * ob, axis=-1, keepdims=True) + RMS_EPS)
        yb_ref[...] = (ob * rn * nw_ref[layer:layer + 1, :] * _silu(zb_ref[...])).astype(BF)
        pu = pre_ref[...] * u_ref[...]
        conv = cw_ref[2:3, :] * pu + cw_ref[1:2, :] * _shift_down(pu, 1) + cw_ref[0:1, :] * _shift_down(pu, 2)
        yc_ref[...] = (post_ref[...] * conv * _silu(zc_ref[...])).astype(BF)

    col = lambda base: pl.BlockSpec((S, 128), lambda p, base=base: (0, base + p))
    out = jax.ShapeDtypeStruct((S, WIDTH), BF)
    return pl.pallas_call(
        body, name="branch_fwd",
        grid=(WIDTH // 128,),
        in_specs=[col(0), col(12), col(0), col(28), pl.BlockSpec(norm_w.shape, lambda p: (0, 0)),
                  col(32), col(36), col(40), col(44), pl.BlockSpec((None, None, 3, 128), lambda p: (p, layer, 0, 0))],
        out_specs=[col(0), col(0), col(0)],
        out_shape=[out, out, out],
        compiler_params=_params(dimension_semantics=("parallel",)),
    )(o_a, proj, o_b, proj, norm_w, proj, proj, proj, proj, conv_w)


def _put_columns(tiles, dproj_ref, firsts, sems):
    copies = [pltpu.make_async_copy(t, dproj_ref.at[:, pl.ds(pl.multiple_of(c, 128), t.shape[1])], sems.at[n])
              for n, (t, c) in enumerate(zip(tiles, firsts))]
    for cp in copies:
        cp.start()
    for cp in copies:
        cp.wait()


def _branch_bwd(proj, o_a, o_b, norm_w, conv_w, dy_a, dy_b, dy_c, dproj, layer):
    S = proj.shape[0]
    firsts = [WIDTH * n for n in (3, 7, 8, 9, 10, 11)]

    def dsilu(z):
        s = _sigmoid(z)
        return s * z, s * (1.0 + z * (1.0 - s))

    def body(oa_ref, za_ref, ob_ref, zb_ref, nw_ref, pre_ref, post_ref, u_ref, zc_ref, cw_ref, dya_ref, dyb_ref, dyc_ref, _,
             doa_ref, dob_ref, dnw_ref, dcw_ref, dproj_ref, stage, sems):
        dza_ref, dzb_ref, dpre_ref, dpost_ref, du_ref, dzc_ref = [stage.at[n] for n in range(6)]
        dya = dya_ref[...]
        sa, dsa = dsilu(za_ref[...])
        doa_ref[...] = dya * sa
        dza_ref[...] = (dya * oa_ref[...] * dsa).astype(BF)

        dyb = dyb_ref[...]
        ob = ob_ref[...]
        nw = nw_ref[layer:layer + 1, :]
        sb, dsb = dsilu(zb_ref[...])
        rn = lax.rsqrt(jnp.mean(ob * ob, axis=-1, keepdims=True) + RMS_EPS)
        on = ob * rn
        dzb_ref[...] = (dyb * on * nw * dsb).astype(BF)
        don_w = dyb * sb
        dnw_ref[...] = jnp.sum(don_w * on, axis=0, keepdims=True)
        don = don_w * nw
        dob_ref[...] = rn * (don - on * jnp.mean(don * on, axis=-1, keepdims=True))

        dyc = dyc_ref[...]
        pre, post, u = pre_ref[...], post_ref[...], u_ref[...]
        sc, dsc = dsilu(zc_ref[...])
        pu = pre * u
        pu1, pu2 = _shift_down(pu, 1), _shift_down(pu, 2)
        conv = cw_ref[2:3, :] * pu + cw_ref[1:2, :] * pu1 + cw_ref[0:1, :] * pu2
        dzc_ref[...] = (dyc * post * conv * dsc).astype(BF)
        dpost_ref[...] = (dyc * conv * sc).astype(BF)
        dconv = dyc * post * sc
        dcw_ref[0:1, :] = jnp.sum(dconv * pu2, axis=0, keepdims=True)
        dcw_ref[1:2, :] = jnp.sum(dconv * pu1, axis=0, keepdims=True)
        dcw_ref[2:3, :] = jnp.sum(dconv * pu, axis=0, keepdims=True)
        dpu = cw_ref[2:3, :] * dconv + cw_ref[1:2, :] * _shift_up(dconv, 1) + cw_ref[0:1, :] * _shift_up(dconv, 2)
        dpre_ref[...] = (dpu * u).astype(BF)
        du_ref[...] = (dpu * pre).astype(BF)
        lane0 = pl.program_id(0) * 128
        _put_columns([stage.at[n] for n in range(6)], dproj_ref, [c + lane0 for c in firsts], sems)

    col = lambda base: pl.BlockSpec((S, 128), lambda p, base=base: (0, base + p))
    f32 = jax.ShapeDtypeStruct((S, WIDTH), F32)
    hbm = pl.BlockSpec(memory_space=pl.ANY)
    return pl.pallas_call(
        body, name="branch_bwd",
        grid=(WIDTH // 128,),
        in_specs=[col(0), col(12), col(0), col(28), pl.BlockSpec(norm_w.shape, lambda p: (0, 0)),
                  col(32), col(36), col(40), col(44), pl.BlockSpec((None, None, 3, 128), lambda p: (p, layer, 0, 0)),
                  col(0), col(0), col(0), hbm],
        out_specs=[col(0), col(0), pl.BlockSpec((None, 1, 128), lambda p: (p, 0, 0)), pl.BlockSpec((None, 3, 128), lambda p: (p, 0, 0)), hbm],
        out_shape=[f32, f32, jax.ShapeDtypeStruct((WIDTH // 128, 1, 128), F32), jax.ShapeDtypeStruct((WIDTH // 128, 3, 128), F32),
                   jax.ShapeDtypeStruct(dproj.shape, dproj.dtype)],
        scratch_shapes=[pltpu.VMEM((6, S, 128), BF), pltpu.SemaphoreType.DMA((6,))],
        input_output_aliases={13: 4},
        compiler_params=_params(dimension_semantics=("arbitrary",)),
    )(o_a, proj, o_b, proj, norm_w, proj, proj, proj, proj, conv_w, dy_a, dy_b, dy_c, dproj)


def _branch_proj(y_refs, wb_ref):
    return [_dot(y_refs[i][...], wb_ref[i]) for i in range(3)]


def _merge_fwd(x, mod, proj, ys, wb, wo, ln_g, ln_b, layer, job=None, tm=512):
    S, D = x.shape
    tm = min(tm, S)

    def body(x_ref, mod_ref, ga_ref, gb_ref, gc_ref, ya_ref, yb_ref, yc_ref, wb_ref, wo_ref, g_ref, b_ref, xo_ref, mg_ref, y_ref):
        ps = _branch_proj((ya_ref, yb_ref, yc_ref), wb_ref)
        merged = _sigmoid(ga_ref[...]) * ps[0] + _sigmoid(gb_ref[...]) * ps[1] + _sigmoid(gc_ref[...]) * ps[2]
        mb = merged.astype(BF)
        mg_ref[...] = mb
        y = _dot(mb, wo_ref[...].reshape(D, D))
        y_ref[...] = y
        r = ALPHA * x_ref[...] + (1.0 + mod_ref[:, 2 * D:3 * D]) * y
        xn, _ = _standardize(r)
        xo_ref[...] = xn * g_ref[layer:layer + 1, :] + b_ref[layer:layer + 1, :]

    row = lambda w, c=0: pl.BlockSpec((tm, w), lambda i, c=c: (i, c))
    whole = lambda a: pl.BlockSpec(a.shape, lambda i, n=a.ndim: (0,) * n)
    return _hosted(
        body, job, (x, mod, proj, proj, proj, *ys, wb, wo, ln_g, ln_b), name="merge_fwd",
        grid=(S // tm,),
        in_specs=[row(D), whole(mod), row(D, 6), row(D, 7), row(D, 8), row(WIDTH), row(WIDTH), row(WIDTH), whole(wb),
                  whole(wo), whole(ln_g), whole(ln_b)],
        out_specs=[row(D), row(D), row(D)],
        out_shape=[jax.ShapeDtypeStruct((S, D), F32), jax.ShapeDtypeStruct((S, D), BF), jax.ShapeDtypeStruct((S, D), F32)],
        semantics=("parallel",))


def _merge_bwd(dxo, x, y, merged, mod, proj, ys, wb, wo, ln_g, layer, tm=256):
    S, D = x.shape
    tm = min(tm, S)
    steps = S // tm
    quarter = D // N_CHIPS

    def body(dxo_ref, x_ref, y_ref, mg_ref, mod_ref, ga_ref, gb_ref, gc_ref, ya_ref, yb_ref, yc_ref, wb_ref, wo_ref, g_ref,
             dxr_ref, dg_ref, dya_ref, dyb_ref, dyc_ref, dlg_ref, dlb_ref, dgt_ref, gbr_ref, gout_ref, acc_br, acc_out):
        @pl.when(pl.program_id(0) == 0)
        def _():
            dlg_ref[...] = jnp.zeros_like(dlg_ref)
            dlb_ref[...] = jnp.zeros_like(dlb_ref)
            dgt_ref[...] = jnp.zeros_like(dgt_ref)
            acc_br[...] = jnp.zeros_like(acc_br)
            acc_out[...] = jnp.zeros_like(acc_out)

        gate1 = 1.0 + mod_ref[:, 2 * D:3 * D]
        yv = y_ref[...]
        xn, rstd = _standardize(ALPHA * x_ref[...] + gate1 * yv)
        dxo = dxo_ref[...]
        dlg_ref[...] += jnp.sum(dxo * xn, axis=0, keepdims=True)
        dlb_ref[...] += jnp.sum(dxo, axis=0, keepdims=True)
        dr = _standardize_bwd(dxo * g_ref[layer:layer + 1, :], xn, rstd)
        dxr_ref[...] = ALPHA * dr
        dgt_ref[...] += jnp.sum(dr * yv, axis=0, keepdims=True)
        dyb = (gate1 * dr).astype(BF)
        acc_out[...] += _dot_tn(mg_ref[...], dyb)
        dmerged = _dot_nt(dyb, wo_ref[...].reshape(D, D))
        y_refs = (ya_ref, yb_ref, yc_ref)
        ps = _branch_proj(y_refs, wb_ref)
        for i, (gate_ref, out_ref) in enumerate(((ga_ref, dya_ref), (gb_ref, dyb_ref), (gc_ref, dyc_ref))):
            sg = _sigmoid(gate_ref[...])
            dg_ref[:, i * D:(i + 1) * D] = (dmerged * ps[i] * sg * (1.0 - sg)).astype(BF)
            dp = (dmerged * sg).astype(BF)
            acc_br[i] += _dot_tn(y_refs[i][...], dp)
            out_ref[...] = _dot_nt(dp, wb_ref[i])

        @pl.when(pl.program_id(0) == steps - 1)
        def _():
            for j in range(N_CHIPS):
                gout_ref[j] = acc_out[j * quarter:(j + 1) * quarter, :].astype(BF)
                for i in range(3):
                    gbr_ref[j, i] = acc_br[i, :, j * quarter:(j + 1) * quarter].astype(BF)

    row = lambda w, c=0: pl.BlockSpec((tm, w), lambda i, c=c: (i, c))
    whole = lambda a: pl.BlockSpec(a.shape, lambda i, n=len(a.shape): (0,) * n)
    vec = pl.BlockSpec((1, D), lambda i: (0, 0))
    sd = jax.ShapeDtypeStruct
    g_br, g_out = sd((N_CHIPS, 3, WIDTH, quarter), BF), sd((N_CHIPS, quarter, D), BF)
    return pl.pallas_call(
        body, name="merge_bwd",
        grid=(steps,),
        in_specs=[row(D), row(D), row(D), row(D), whole(mod), row(D, 6), row(D, 7), row(D, 8), row(WIDTH), row(WIDTH), row(WIDTH),
                  whole(wb), whole(wo), whole(ln_g)],
        out_specs=[row(D), row(3 * D, IN_COLS // (3 * D) - 1), row(WIDTH), row(WIDTH), row(WIDTH), vec, vec, vec, whole(g_br), whole(g_out)],
        out_shape=[sd((S, D), F32), sd((S, IN_COLS), BF), sd((S, WIDTH), F32), sd((S, WIDTH), F32), sd((S, WIDTH), F32),
                   sd((1, D), F32), sd((1, D), F32), sd((1, D), F32), g_br, g_out],
        scratch_shapes=[pltpu.VMEM((3, WIDTH, D), F32), pltpu.VMEM((D, D), F32)],
        compiler_params=_params(dimension_semantics=("arbitrary",)),
    )(dxo, x, y, merged, mod, proj, proj, proj, *ys, wb, wo, ln_g)


def _loss_head(x, target, tm=512):
    S, D = x.shape
    tm = min(tm, S)

    def body(x_ref, t_ref, dx_ref, loss_ref):
        @pl.when(pl.program_id(0) == 0)
        def _():
            loss_ref[...] = jnp.zeros_like(loss_ref)

        err = x_ref[...] - t_ref[...]
        dx_ref[...] = err * (1.0 / D)
        loss_ref[...] += 0.5 * jnp.sum(jnp.mean(err * err, axis=-1, keepdims=True))

    row = pl.BlockSpec((tm, D), lambda i: (i, 0))
    return pl.pallas_call(
        body, name="loss_head",
        grid=(S // tm,),
        in_specs=[row, row],
        out_specs=[row, pl.BlockSpec((8, 128), lambda i: (0, 0))],
        out_shape=[jax.ShapeDtypeStruct((S, D), F32), jax.ShapeDtypeStruct((8, 128), F32)],
        compiler_params=_params(dimension_semantics=("arbitrary",)),
    )(x, target)


def _proj_bwd(dproj, wgs, x, mod, dx_res, job=None, tm=512, tk=768):
    S, D = x.shape
    tm = min(tm, S)
    n = len(wgs)
    w_args = list(wgs)
    if n == 1:
        per = wgs[0].shape[-1] // tk
        w_specs = [pl.BlockSpec((None, D, tk), lambda k, i: (k // per, 0, k % per))]
    else:
        assert wgs[0].shape[-1] == tk
        w_specs = [pl.BlockSpec((None, D, tk), lambda k, i, c=c: (jnp.minimum((k + n - 1 - c) // n, N_CHIPS - 1), 0, 0))
                   for c in range(n)]
    nk = IN_COLS // tk

    def body(dp_ref, *rest):
        w_refs, (x_ref, mod_ref, dxr_ref, dx_ref, dsh_ref, dsc_ref, acc) = rest[:n], rest[n:]
        k, i = pl.program_id(0), pl.program_id(1)
        mine = pl.ds(pl.multiple_of(i * tm, tm), tm)

        @pl.when((i == 0) & (k == 0))
        def _():
            dsh_ref[...] = jnp.zeros_like(dsh_ref)
            dsc_ref[...] = jnp.zeros_like(dsc_ref)

        @pl.when(k == 0)
        def _():
            acc[mine, :] = jnp.zeros((tm, D), F32)

        for c in range(n):
            @pl.when(k % n == c)
            def _(c=c):
                acc[mine, :] += _dot_nt(dp_ref[...], w_refs[c][...])

        @pl.when(k == nk - 1)
        def _():
            dh = acc[mine, :]
            xs, rstd = _standardize(x_ref[...])
            dsh_ref[...] += jnp.sum(dh, axis=0, keepdims=True)
            dsc_ref[...] += jnp.sum(dh * xs, axis=0, keepdims=True)
            dx_ref[...] = _standardize_bwd(dh * (1.0 + mod_ref[:, D:2 * D]), xs, rstd) + dxr_ref[...]

    row = pl.BlockSpec((tm, D), lambda k, i: (jnp.where(k == nk - 1, i, 0), 0))
    vec = pl.BlockSpec((1, D), lambda k, i: (0, 0))
    return _hosted(
        body, job, (dproj, *w_args, x, mod, dx_res), name="proj_bwd",
        grid=(nk, S // tm),
        in_specs=[pl.BlockSpec((tm, tk), lambda k, i: (i, k))] + w_specs + [row, pl.BlockSpec((1, 3 * D), lambda k, i: (0, 0)), row],
        out_specs=[row, vec, vec],
        out_shape=[jax.ShapeDtypeStruct((S, D), F32), jax.ShapeDtypeStruct((1, D), F32), jax.ShapeDtypeStruct((1, D), F32)],
        scratch_shapes=[pltpu.VMEM((S, D), F32)],
        semantics=("arbitrary", "arbitrary"))


def _grad_w_in(h, dproj):
    S, D = h.shape
    shard = IN_COLS // N_CHIPS

    def body(h_ref, d_ref, o_ref):
        o_ref[...] = _dot_tn(h_ref[...], d_ref[...]).astype(BF)

    return pl.pallas_call(
        body, name="grad_w_in",
        grid=(N_CHIPS,),
        in_specs=[pl.BlockSpec((S, D), lambda n: (0, 0)), pl.BlockSpec((S, shard), lambda n: (0, n))],
        out_specs=pl.BlockSpec((None, D, shard), lambda n: (n, 0, 0)),
        out_shape=jax.ShapeDtypeStruct((N_CHIPS, D, shard), BF),
        compiler_params=_params(dimension_semantics=("parallel",)),
    )(h, dproj)


def _all_gather8(x, name):
    R, N = x.shape

    def body(x_ref, out_ref, send_sems, recv_sems):
        mx, my, mc = lax.axis_index("x"), lax.axis_index("y"), lax.axis_index("c")
        me = 4 * mx + 2 * my + mc
        out_ref[me] = x_ref[...]
        copies = []
        for k in range(1, N_DEV):
            peer = (_flip(mx, k & 4), _flip(my, k & 2), _flip(mc, k & 1))
            cp = pltpu.make_async_remote_copy(src_ref=x_ref, dst_ref=out_ref.at[me], send_sem=send_sems.at[k - 1],
                                              recv_sem=recv_sems.at[k - 1], device_id=peer, device_id_type=MESH)
            cp.start()
            copies.append(cp)
        for cp in copies:
            cp.wait()

    return pl.pallas_call(
        body, name=name,
        in_specs=[pl.BlockSpec(memory_space=pltpu.VMEM)],
        out_specs=pl.BlockSpec(memory_space=pltpu.VMEM),
        out_shape=jax.ShapeDtypeStruct((N_DEV, R, N), F32),
        scratch_shapes=[pltpu.SemaphoreType.DMA((N_DEV - 1,)), pltpu.SemaphoreType.DMA((N_DEV - 1,))],
        compiler_params=_params(),
    )(x)


def _rows2d(a):
    return a.reshape(-1, a.shape[-1])


def _tile_rows(rows, cols, n_arrays):
    budget = (24 << 20) // (n_arrays * 2 * 4 * cols)
    if rows <= budget:
        return rows
    tm = 8
    for cand in range(8, budget + 1, 8):
        if rows % cand == 0:
            tm = cand
    return tm


SUM_ROWS = 256


def _sum_cores(g, sent, where):
    chips, lead, _, r, cols = g.shape
    tr = min(r, SUM_ROWS)

    def body(where_ref, g_ref, s_ref, out_ref):
        out_ref[...] = (g_ref[...].astype(F32) + s_ref[...].astype(F32)).astype(BF)

    spec = pl.BlockSpec((None, tr, cols), lambda i, j, where_ref: (i, j, 0))
    out = pl.pallas_call(
        body, name="sum_cores",
        grid_spec=pltpu.PrefetchScalarGridSpec(
            num_scalar_prefetch=1, grid=(chips * lead, r // tr),
            in_specs=[pl.BlockSpec((None, None, tr, cols), lambda i, j, where_ref: (i, where_ref[1], j, 0)), spec],
            out_specs=spec),
        out_shape=jax.ShapeDtypeStruct((chips * lead, r, cols), BF),
        compiler_params=_params(dimension_semantics=("parallel", "parallel")),
    )(where, g.reshape(chips * lead, 2, r, cols), sent.reshape(chips * lead, r, cols))
    return out.reshape(chips, lead, r, cols)


def _sum_chips(sums, got, where):
    _, lead, r, cols = sums.shape
    tr = min(r, SUM_ROWS)

    def body(where_ref, s_ref, g_ref, out_ref):
        out_ref[...] = ((s_ref[...].astype(F32) + g_ref[0].astype(F32)) + g_ref[1].astype(F32)) + g_ref[2].astype(F32)

    return pl.pallas_call(
        body, name="sum_chips",
        grid_spec=pltpu.PrefetchScalarGridSpec(
            num_scalar_prefetch=1, grid=(lead, r // tr),
            in_specs=[pl.BlockSpec((None, None, tr, cols), lambda i, j, where_ref: (where_ref[0], i, j, 0)),
                      pl.BlockSpec((N_CHIPS - 1, None, tr, cols), lambda i, j, where_ref: (0, i, j, 0))],
            out_specs=pl.BlockSpec((None, None, tr, cols), lambda i, j, where_ref: (i, where_ref[1], j, 0))),
        out_shape=jax.ShapeDtypeStruct((lead, 2, r, cols), F32),
        compiler_params=_params(dimension_semantics=("parallel", "parallel")),
    )(where, sums, got)


def _adamw(w, m, v, groups, name):
    shape = w.shape
    w2, m2, v2 = _rows2d(w), _rows2d(m), _rows2d(v)
    rows, cols = w2.shape
    ng = len(groups)
    n = len(groups[0])
    slab = rows // ng
    gs = [_rows2d(g) for grp in groups for g in grp]
    tm = _tile_rows(slab, cols, 7 + n)
    tiles = slab // tm
    c1 = 1.0 / (1.0 - ADAM_B1 ** ADAM_STEP)
    c2 = 1.0 / (1.0 - ADAM_B2 ** ADAM_STEP)

    def body(*refs):
        w_ref, m_ref, v_ref = refs[:3]
        g_refs = refs[3:3 + ng * n]
        go_ref, d_ref, mo_ref, vo_ref = refs[3 + ng * n:]
        which = pl.program_id(0)
        for s in range(ng):
            @pl.when(which == s)
            def _(s=s):
                g = g_refs[s * n][...]
                for r in g_refs[s * n + 1:(s + 1) * n]:
                    g = g + r[...]
                mn = ADAM_B1 * m_ref[...] + (1.0 - ADAM_B1) * g
                vn = ADAM_B2 * v_ref[...] + (1.0 - ADAM_B2) * (g * g)
                go_ref[...] = g
                mo_ref[...] = mn
                vo_ref[...] = vn
                d_ref[...] = -ADAM_LR * ((mn * c1) / (jnp.sqrt(vn * c2) + ADAM_EPS) + ADAM_WD * w_ref[...])

    spec = pl.BlockSpec((tm, cols), lambda s, i: (s * tiles + i, 0))
    g_specs = [pl.BlockSpec((tm, cols), lambda s, i, k=k: (jnp.where(s == k, i, jnp.where(s < k, 0, tiles - 1)), 0))
               for k in range(ng) for _ in range(n)]
    outs = pl.pallas_call(
        body, name=name,
        grid=(ng, tiles),
        in_specs=[spec] * 3 + g_specs,
        out_specs=[spec] * 4,
        out_shape=[jax.ShapeDtypeStruct((rows, cols), F32)] * 4,
        compiler_params=_params(dimension_semantics=("arbitrary", "arbitrary")),
    )(w2, m2, v2, *gs)
    return [o.reshape(shape) for o in outs]


def _lower_bounds(r0, r1):
    top = jnp.maximum(r0, r1)
    e0, e1 = jnp.exp(r0 - top), jnp.exp(r1 - top)
    p0, p1 = e0 / (e0 + e1), e1 / (e0 + e1)
    return p0 - p0, (p0 + p1) - p0


def _lbs_fwd(lb_raw):
    def body(lb_ref, out_ref):
        l0, l1 = _lower_bounds(lb_ref[0:1, :], lb_ref[1:2, :])
        out_ref[0:1, :] = l0
        out_ref[1:2, :] = l1

    return pl.pallas_call(body, name="lower_bounds", out_shape=jax.ShapeDtypeStruct(lb_raw.shape, F32), compiler_params=_params())(lb_raw)


def _mod_rows(c_all, w_mod, tn=768):
    _, D, cols = w_mod.shape

    def body(c_ref, w_ref, out_ref):
        out_ref[...] = _dot(c_ref[...].astype(BF), w_ref[...].astype(BF))

    return pl.pallas_call(
        body, name="mod_rows",
        grid=(DEPTH,),
        in_specs=[pl.BlockSpec((N_DEV, D), lambda l: (0, 0)), pl.BlockSpec((None, D, cols), lambda l: (l, 0, 0))],
        out_specs=pl.BlockSpec((N_DEV, cols), lambda l: (0, l)),
        out_shape=jax.ShapeDtypeStruct((N_DEV, DEPTH * cols), F32),
        compiler_params=_params(dimension_semantics=("parallel",)),
    )(c_all, w_mod)


def _grad_w_mod(c_all, dmod_cols):
    D = c_all.shape[1]
    cols = dmod_cols.shape[-1]

    def body(c_ref, d_ref, out_ref):
        out_ref[...] = _dot_tn(c_ref[...].astype(BF), d_ref[...].astype(BF))

    return pl.pallas_call(
        body, name="grad_w_mod",
        grid=(DEPTH,),
        in_specs=[pl.BlockSpec((N_DEV, D), lambda l: (0, 0)), pl.BlockSpec((None, N_DEV, cols), lambda l: (l, 0, 0))],
        out_specs=pl.BlockSpec((None, D, cols), lambda l: (l, 0, 0)),
        out_shape=jax.ShapeDtypeStruct((DEPTH, D, cols), F32),
        compiler_params=_params(dimension_semantics=("parallel",)),
    )(c_all, dmod_cols)


def _sum_devices(parts):
    _, R, N = parts.shape

    def body(p_ref, out_ref):
        acc = p_ref[0]
        for d in range(1, N_DEV):
            acc = acc + p_ref[d]
        out_ref[...] = acc

    return pl.pallas_call(body, name="sum_devices", out_shape=jax.ShapeDtypeStruct((R, N), F32), compiler_params=_params())(parts)


def _lbs_bwd(lb_raw, dl):
    def body(lb_ref, dl_ref, out_ref):
        _, pull = jax.vjp(_lower_bounds, lb_ref[0:1, :], lb_ref[1:2, :])
        d0, d1 = pull((dl_ref[0:1, :], dl_ref[1:2, :]))
        out_ref[0:1, :] = d0
        out_ref[1:2, :] = d1

    return pl.pallas_call(body, name="lower_bounds_bwd", out_shape=jax.ShapeDtypeStruct(lb_raw.shape, F32), compiler_params=_params())(lb_raw, dl)


def kernel(x, c, w_mod, b_mod, w_in, conv_w, hgrn_norm_w, lower_bounds, w_branch, w_out, ln_g, ln_b, loss_target, m_w_mod, m_b_mod, m_w_in, m_conv_w, m_hgrn_norm_w, m_lower_bounds, m_w_branch, m_w_out, m_ln_g, m_ln_b, v_w_mod, v_b_mod, v_w_in, v_conv_w, v_hgrn_norm_w, v_lower_bounds, v_w_branch, v_w_out, v_ln_g, v_ln_b):
    D = D_MODEL
    x0 = x[0]
    target = loss_target[0]
    S = x0.shape[0]
    mx, my, mc = lax.axis_index("x"), lax.axis_index("y"), lax.axis_index("c")
    chip = 2 * mx + my
    me = 2 * chip + mc
    mod_cols = 3 * D // N_CHIPS

    plan = _Plan(w_in, w_branch, w_out, chip, mc)
    n_conv = DEPTH * 3 * (WIDTH // N_CHIPS)
    first = jnp.concatenate([c, conv_w.reshape(1, n_conv), jnp.zeros((1, 2 * D - D - n_conv), F32)], axis=1)
    first = plan.first(first.reshape(8, 2 * D // 8)).reshape(N_DEV, 2 * D)

    c_all = first[:, :D]
    conv_all = first[:, D:D + n_conv].reshape(N_DEV, DEPTH, 3, WIDTH // N_CHIPS)[0::2]
    mod_part = _all_gather8(_mod_rows(c_all, w_mod), "gather_mod")[0::2]
    mod_part = lax.dynamic_index_in_dim(mod_part, me, axis=1, keepdims=False).reshape(N_CHIPS, DEPTH, mod_cols)
    mods = [(mod_part[:, l].reshape(1, 3 * D) + b_mod[l][None, :]) for l in range(DEPTH)]
    lbs = _lbs_fwd(lower_bounds).reshape(DEPTH, WIDTH // 128, 1, 128)
    loss_blk, dx, small = _local_step(x0, target, mods, lbs, conv_all, hgrn_norm_w, ln_g, ln_b, plan)

    n_mod, n_nw, n_lb, n_ln, n_cw = DEPTH * 3 * D, DEPTH * 128, DEPTH * WIDTH, DEPTH * D, DEPTH * 3 * WIDTH
    row = jnp.concatenate(
        [jnp.concatenate([small[l][0], small[l][1], small[l][2]], axis=1) for l in range(DEPTH)]
        + [jnp.sum(small[l][3], axis=0) for l in range(DEPTH)]
        + [small[l][4].reshape(1, WIDTH) for l in range(DEPTH)]
        + [small[l][5] for l in range(DEPTH)] + [small[l][6] for l in range(DEPTH)]
        + [jnp.transpose(small[l][7], (1, 0, 2)).reshape(1, 3 * WIDTH) for l in range(DEPTH)]
        + [loss_blk[0:1, :]], axis=1)
    n_row = row.shape[1]
    fold = -(-n_row // (8 * 128)) * 128
    rows = jnp.concatenate([row, jnp.zeros((1, 8 * fold - n_row), F32)], axis=1).reshape(8, fold)

    whole, gathered = plan.finish(rows)
    grads = {kind: [[whole[(kind, l)]] for l in range(DEPTH)] for kind in ("in", "br", "out")}

    off_nw = n_mod
    off_lb = off_nw + n_nw
    off_lng = off_lb + n_lb
    off_lnb = off_lng + n_ln
    off_cw = off_lnb + n_ln
    off_loss = off_cw + n_cw
    total = _sum_devices(gathered).reshape(1, 8 * fold)
    gathered = gathered.reshape(N_DEV, 1, 8 * fold)
    d_lower = _lbs_bwd(lower_bounds, total[0, off_lb:off_lng].reshape(DEPTH, WIDTH))
    loss = total[0, off_loss]
    d_b_mod = total[0, :n_mod].reshape(DEPTH, 3 * D)
    d_norm_w = total[0, off_nw:off_lb].reshape(DEPTH, 128)
    d_ln_g = total[0, off_lng:off_lnb].reshape(DEPTH, D)
    d_ln_b = total[0, off_lnb:off_cw].reshape(DEPTH, D)
    d_conv = total[0, off_cw:off_loss].reshape(DEPTH, 3, N_CHIPS, WIDTH // N_CHIPS)
    d_conv = lax.dynamic_index_in_dim(d_conv, chip, axis=2, keepdims=False)
    dmod_all = gathered[:, 0, :n_mod].reshape(N_DEV, DEPTH, N_CHIPS, mod_cols)
    dmod_cols = jnp.transpose(lax.dynamic_index_in_dim(dmod_all, chip, axis=2, keepdims=False), (1, 0, 2))
    d_w_mod = _grad_w_mod(c_all, dmod_cols)

    res = {}
    res["w_mod"] = _adamw(w_mod, m_w_mod, v_w_mod, [[d_w_mod]], "adamw_w_mod")
    res["b_mod"] = _adamw(b_mod, m_b_mod, v_b_mod, [[d_b_mod]], "adamw_b_mod")
    res["w_in"] = _adamw(w_in, m_w_in, v_w_in, grads["in"], "adamw_w_in")
    res["conv_w"] = _adamw(conv_w, m_conv_w, v_conv_w, [[d_conv]], "adamw_conv_w")
    res["hgrn_norm_w"] = _adamw(hgrn_norm_w, m_hgrn_norm_w, v_hgrn_norm_w, [[d_norm_w]], "adamw_norm_w")
    res["lower_bounds"] = _adamw(lower_bounds, m_lower_bounds, v_lower_bounds, [[d_lower]], "adamw_lower_bounds")
    res["w_branch"] = _adamw(w_branch, m_w_branch, v_w_branch, grads["br"], "adamw_w_branch")
    res["w_out"] = _adamw(w_out, m_w_out, v_w_out, grads["out"], "adamw_w_out")
    res["ln_g"] = _adamw(ln_g, m_ln_g, v_ln_g, [[d_ln_g]], "adamw_ln_g")
    res["ln_b"] = _adamw(ln_b, m_ln_b, v_ln_b, [[d_ln_b]], "adamw_ln_b")
    names = ["w_mod", "b_mod", "w_in", "conv_w", "hgrn_norm_w", "lower_bounds", "w_branch", "w_out", "ln_g", "ln_b"]
    return (loss, dx[None], *[res[n][0] for n in names], *[res[n][1] for n in names],
            *[res[n][2] for n in names], *[res[n][3] for n in names])


class _Plan:
    FIRST_CHUNKS = 2
    WINDOWS = ((0, 1152), (1152, 640), (1792, 512))

    def __init__(self, w_in, w_br, w_out, chip, core):
        self.local = {"in": w_in, "br": w_br, "out": w_out}
        self.chip, self.where = chip, jnp.stack([chip, core]).astype(jnp.int32)
        self.gathered, self.partial, self.grads, self.chip_sums, self.scattered, self.pending = {}, {}, {}, {}, {}, {}

    def chunks(self, l):
        return self.FIRST_CHUNKS if l == 0 else 1

    def w_in(self, l):
        return [self.gathered[("in", l, c)] for c in range(self.chunks(l))]

    def _shard(self, key):
        mine = self.local[key[0]][key[1]]
        if key[0] == "in":
            cols = mine.shape[-1] // self.chunks(key[1])
            mine = mine[:, key[2] * cols:(key[2] + 1) * cols]
        return mine.astype(BF)

    def _slab(self, key):
        mine = _halves(self._shard(key))
        return lax.dynamic_update_slice(lax.empty((N_CHIPS,) + mine.shape, mine.dtype), mine[None], (self.chip, 0, 0, 0, 0))

    def _gather(self, keys):
        return ("gather", keys), _gather_job([self._slab(key) for key in keys])

    def _gather_window(self, key, n):
        slab = self._slab(key) if n == 0 else self.partial[key]
        return ("gather" if n == len(self.WINDOWS) - 1 else "gather_part", [key]), _gather_job([slab], self.WINDOWS[n])

    def _to_sibling(self, keys):
        return ("to_sibling", keys), _to_sibling_job([_halves(self.grads[key], 1) for key in keys])

    def _scatter(self, keys):
        return ("scatter", keys), _scatter_job([self.chip_sums[key] for key in keys])

    def job(self, stage, l, c=0):
        parts = []
        if stage == "proj_fwd":
            parts = [self._gather([("in", l, c + 1)] if c + 1 < self.chunks(l) else [("br", l), ("out", l)])]
        elif stage == "attn_fwd" and l + 1 < DEPTH:
            parts = [self._gather_window(("in", l + 1, 0), 0)]
        elif stage == "hgrn_fwd" and l + 1 < DEPTH:
            parts = [self._gather_window(("in", l + 1, 0), 1)]
        elif stage == "merge_fwd" and l + 1 < DEPTH:
            parts = [self._gather_window(("in", l + 1, 0), 2)]
        elif stage == "attn_bwd":
            parts = [self._to_sibling([("out", l), ("br", l)])] + ([self._scatter([("in", l + 1)])] if l + 1 < DEPTH else [])
        elif stage == "hgrn_bwd":
            parts = [self._scatter([("out", l), ("br", l)])]
        elif stage == "proj_bwd":
            parts = [self._to_sibling([("in", l)])] if l else [self._scatter([("in", 0)])]
        self.pending[(stage, l, c)] = [(tag, len(job.outs)) for tag, job in parts]
        return _join_jobs([job for _, job in parts])

    def done(self, stage, l, outs, c=0):
        if outs is None:
            return
        at = 0
        for (what, keys), n_outs in self.pending[(stage, l, c)]:
            mine, at = outs[at:at + n_outs], at + n_outs
            for n, key in enumerate(keys):
                if what == "gather":
                    self.gathered[key] = mine[n].reshape((N_CHIPS,) + self._shard(key).shape)
                elif what == "gather_part":
                    self.partial[key] = mine[n]
                elif what == "to_sibling":
                    self.chip_sums[key] = _sum_cores(_halves(self.grads[key], 1), mine[n], self.where)
                else:
                    self.scattered[key] = mine[n]

    def first(self, rows):
        tag, job = self._gather([("in", 0, 0)])
        self.pending[("first", 0, 0)] = [(tag, len(job.outs))]
        outs = _run_job(_join_jobs([job, _gather8_job(rows)]), "gather_first")
        self.done("first", 0, outs[:-1])
        return outs[-1]

    def took(self, key, grad):
        self.grads[key] = grad
        if key == ("in", 0):
            tag, job = self._to_sibling([key])
            self.pending[("took", 0, 0)] = [(tag, len(job.outs))]
            self.done("took", 0, _run_job(job, "to_sibling_last"))

    def finish(self, rows):
        keys = [(kind, l) for kind in ("in", "br", "out") for l in range(DEPTH)]
        halves = [_sum_chips(self.chip_sums[key], self.scattered[key], self.where) for key in keys]
        outs = _run_job(_join_jobs([_place_job(halves), _gather8_job(rows)]), "place_halves")
        return {key: w.reshape(self.grads[key].shape[1:]) for key, w in zip(keys, outs[:-1])}, outs[-1]


def _local_step(x0, target, mods, lbs, conv_all, hgrn_norm_w, ln_g, ln_b, plan):
    D = D_MODEL
    after, before = _attn_consts()
    hg_consts = _hgrn_consts()

    saved = []
    xl = x0
    for l in range(DEPTH):
        n = plan.chunks(l)
        (proj, h), got = _proj_fwd(xl, mods[l], plan.gathered[("in", l, 0)], plan.job("proj_fwd", l, 0), (0, n))
        plan.done("proj_fwd", l, got, 0)
        for c in range(1, n):
            (proj,), got = _proj_cols(h, plan.gathered[("in", l, c)], (c, n), proj, plan.job("proj_fwd", l, c))
            plan.done("proj_fwd", l, got, c)
        (o_a, tot), got = _attn_fwd(proj, after, plan.job("attn_fwd", l))
        plan.done("attn_fwd", l, got)
        (o_b, states), got = _hgrn_fwd(proj, lbs[l], hg_consts, plan.job("hgrn_fwd", l))
        plan.done("hgrn_fwd", l, got)
        ys = _branch_fwd(proj, o_a, o_b, hgrn_norm_w, conv_all, l)
        wb = jnp.concatenate(list(plan.gathered[("br", l)]), axis=-1)
        (x_next, merged, y), got = _merge_fwd(xl, mods[l], proj, ys, wb, plan.gathered[("out", l)], ln_g, ln_b, l, plan.job("merge_fwd", l))
        plan.done("merge_fwd", l, got)
        saved.append((xl, proj, h, o_a, tot, o_b, states, ys, merged, y, wb))
        xl = x_next
    dx, loss_blk = _loss_head(xl, target)

    small = [None] * DEPTH
    for l in reversed(range(DEPTH)):
        xin, proj, h, o_a, tot, o_b, states, ys, merged, y, wb = saved[l]
        dx_res, dproj, dy_a, dy_b, dy_c, dln_g, dln_b, dgate, g_br, g_out = _merge_bwd(
            dx, xin, y, merged, mods[l], proj, ys, wb, plan.gathered[("out", l)], ln_g, l)
        plan.took(("out", l), g_out)
        plan.took(("br", l), g_br)
        d_oa, d_ob, dnorm_w, dconv_w, dproj = _branch_bwd(proj, o_a, o_b, hgrn_norm_w, conv_all, dy_a, dy_b, dy_c, dproj, l)
        (dproj,), got = _attn_bwd(proj, d_oa, tot, after, before, dproj, plan.job("attn_bwd", l))
        plan.done("attn_bwd", l, got)
        (dlb, dproj), got = _hgrn_bwd(proj, lbs[l], states, d_ob, hg_consts, dproj, plan.job("hgrn_bwd", l))
        plan.done("hgrn_bwd", l, got)
        plan.took(("in", l), _grad_w_in(h, dproj))
        wgs = plan.w_in(l)
        tile = {"tk": wgs[0].shape[-1]} if len(wgs) > 1 else {}
        (dx, dshift, dscale), got = _proj_bwd(dproj, wgs, xin, mods[l], dx_res, plan.job("proj_bwd", l), **tile)
        plan.done("proj_bwd", l, got)
        small[l] = (dshift, dscale, dgate, dnorm_w, dlb, dln_g, dln_b, dconv_w)
    return loss_blk, dx, small
```

```python
import functools
import math

import numpy as np
import jax
import jax.numpy as jnp
from jax import lax
from jax.experimental import pallas as pl
from jax.experimental.pallas import tpu as pltpu

F32 = jnp.float32
BF = jnp.bfloat16
MESH = pl.DeviceIdType.MESH

DEPTH = 2
D_MODEL = 1024
WIDTH = 512
IN_COLS = 12 * WIDTH + 3 * D_MODEL
N_CHIPS = 4
N_DEV = 8
SB_BLOCK = 128
SB_HEAD_DIM = 64
HG_CHUNK = 128
HG_DIM = 128
LN_EPS = 1e-5
RMS_EPS = 1e-6
ALPHA = (2.0 * DEPTH) ** 0.25
ADAM_LR, ADAM_B1, ADAM_B2, ADAM_EPS, ADAM_WD, ADAM_STEP = 0.001, 0.9, 0.999, 1e-08, 0.01, 10
VMEM_LIMIT = 56 << 20


def _params(**kw):
    return pltpu.CompilerParams(vmem_limit_bytes=VMEM_LIMIT, **kw)


def _dot(a, b):
    return jnp.dot(a, b, preferred_element_type=F32)


def _dot_nt(a, b):
    return lax.dot_general(a, b, (((1,), (1,)), ((), ())), preferred_element_type=F32)


def _dot_tn(a, b):
    return lax.dot_general(a, b, (((0,), (0,)), ((), ())), preferred_element_type=F32)


def _sigmoid(x):
    return 1.0 / (1.0 + jnp.exp(-x))


def _silu(x):
    return x * _sigmoid(x)


def _standardize(x):
    mu = jnp.mean(x, axis=-1, keepdims=True)
    xc = x - mu
    var = jnp.mean(xc * xc, axis=-1, keepdims=True)
    rstd = lax.rsqrt(var + LN_EPS)
    return xc * rstd, rstd


def _standardize_bwd(dxs, xs, rstd):
    return rstd * (dxs - jnp.mean(dxs, axis=-1, keepdims=True) - xs * jnp.mean(dxs * xs, axis=-1, keepdims=True))


class _Job:
    def __init__(self, ins, outs, sems, make, alias=None):
        self.ins, self.outs, self.sems, self.make = list(ins), list(outs), list(sems), make
        self.alias = dict(alias or {})


def _join_jobs(jobs):
    jobs = [j for j in jobs if j is not None]
    if len(jobs) <= 1:
        return jobs[0] if jobs else None

    def make(ins, outs, sems):
        phases, i, o, s = [], 0, 0, 0
        for j in jobs:
            got = j.make(ins[i:i + len(j.ins)], outs[o:o + len(j.outs)], sems[s:s + len(j.sems)])
            i, o, s = i + len(j.ins), o + len(j.outs), s + len(j.sems)
            for n, phase in enumerate(got):
                if n == len(phases):
                    phases.append([])
                phases[n] += phase
        return phases

    alias, i, o = {}, 0, 0
    for j in jobs:
        alias.update({i + a: o + b for a, b in j.alias.items()})
        i, o = i + len(j.ins), o + len(j.outs)
    return _Job(sum([j.ins for j in jobs], []), sum([j.outs for j in jobs], []), sum([j.sems for j in jobs], []), make, alias)


def _flip(v, bit):
    return 1 - v if bit else v


def _halves(a, front=0):
    shape = a.shape
    lead = math.prod(shape[front:-2])
    return a.reshape(shape[:front] + (lead, 2, shape[-2] // 2, shape[-1]))


def _dma_sems(*shapes):
    return [pltpu.SemaphoreType.DMA(s) for s in shapes]


def _same(arrays):
    return [jax.ShapeDtypeStruct(a.shape, a.dtype) for a in arrays]


def _gather_job(slabs, window=None):
    n = len(slabs)
    cols = slice(None) if window is None else pl.ds(*window)

    def make(ins, outs, sems):
        send1, recv1, send2, recv2 = sems
        mx, my, mc = lax.axis_index("x"), lax.axis_index("y"), lax.axis_index("c")
        fetch, pass_on = [], []
        for a in range(n):
            ours = outs[a].at[2 * mx + my, :, mc, :, cols]
            for k in range(1, N_CHIPS):
                px, py = _flip(mx, k & 2), _flip(my, k & 1)
                fetch.append(pltpu.make_async_remote_copy(
                    src_ref=ours, dst_ref=ours, send_sem=send1.at[a, k - 1], recv_sem=recv1.at[a, k - 1],
                    device_id=(px, py, mc), device_id_type=MESH))
                theirs = outs[a].at[2 * px + py, :, mc, :, cols]
                pass_on.append(pltpu.make_async_remote_copy(
                    src_ref=theirs, dst_ref=theirs, send_sem=send2.at[a, k - 1], recv_sem=recv2.at[a, k - 1],
                    device_id=(mx, my, 1 - mc), device_id_type=MESH))
        return [fetch, pass_on]

    pairs = (n, N_CHIPS - 1)
    return _Job(slabs, _same(slabs), _dma_sems(pairs, pairs, pairs, pairs), make, {a: a for a in range(n)})


def _to_sibling_job(grads):
    n = len(grads)

    def make(ins, outs, sems):
        send_sems, recv_sems = sems
        mx, my, mc = lax.axis_index("x"), lax.axis_index("y"), lax.axis_index("c")
        return [[pltpu.make_async_remote_copy(
            src_ref=ins[a].at[:, :, 1 - mc], dst_ref=outs[a], send_sem=send_sems.at[a], recv_sem=recv_sems.at[a],
            device_id=(mx, my, 1 - mc), device_id_type=MESH) for a in range(n)]]

    outs = [jax.ShapeDtypeStruct(g.shape[:2] + g.shape[3:], g.dtype) for g in grads]
    return _Job(grads, outs, _dma_sems((n,), (n,)), make)


def _scatter_job(sums):
    n = len(sums)

    def make(ins, outs, sems):
        send_sems, recv_sems = sems
        mx, my, mc = lax.axis_index("x"), lax.axis_index("y"), lax.axis_index("c")
        copies = []
        for a in range(n):
            for k in range(1, N_CHIPS):
                px, py = _flip(mx, k & 2), _flip(my, k & 1)
                copies.append(pltpu.make_async_remote_copy(
                    src_ref=ins[a].at[2 * px + py], dst_ref=outs[a].at[k - 1], send_sem=send_sems.at[a, k - 1],
                    recv_sem=recv_sems.at[a, k - 1], device_id=(px, py, mc), device_id_type=MESH))
        return [copies]

    pairs = (n, N_CHIPS - 1)
    return _Job(sums, [jax.ShapeDtypeStruct((N_CHIPS - 1,) + s.shape[1:], s.dtype) for s in sums], _dma_sems(pairs, pairs), make)


def _place_job(wholes):
    n = len(wholes)

    def make(ins, outs, sems):
        send_sems, recv_sems = sems
        mx, my, mc = lax.axis_index("x"), lax.axis_index("y"), lax.axis_index("c")
        copies = []
        for a in range(n):
            here = outs[a].at[:, mc]
            copies.append(pltpu.make_async_remote_copy(src_ref=here, dst_ref=here, send_sem=send_sems.at[a], recv_sem=recv_sems.at[a],
                                                       device_id=(mx, my, 1 - mc), device_id_type=MESH))
        return [copies]

    return _Job(wholes, _same(wholes), _dma_sems((n,), (n,)), make, {a: a for a in range(n)})


def _gather8_job(x):
    def make(ins, outs, sems):
        local_sem, send_sems, recv_sems = sems
        mx, my, mc = lax.axis_index("x"), lax.axis_index("y"), lax.axis_index("c")
        here = outs[0].at[4 * mx + 2 * my + mc]
        copies = [pltpu.make_async_copy(ins[0], here, local_sem.at[0])]
        for k in range(1, N_DEV):
            peer = (_flip(mx, k & 4), _flip(my, k & 2), _flip(mc, k & 1))
            copies.append(pltpu.make_async_remote_copy(src_ref=ins[0], dst_ref=here, send_sem=send_sems.at[k - 1],
                                                       recv_sem=recv_sems.at[k - 1], device_id=peer, device_id_type=MESH))
        return [copies]

    return _Job([x], [jax.ShapeDtypeStruct((N_DEV,) + x.shape, x.dtype)], _dma_sems((1,), (N_DEV - 1,), (N_DEV - 1,)), make)


def _run_phases(phases, first=0):
    for n, phase in enumerate(phases):
        if n >= first:
            for cp in phase:
                cp.start()
        for cp in phase:
            cp.wait()


def _run_job(job, name):
    k_in, k_out = len(job.ins), len(job.outs)

    def body(*refs):
        _run_phases(job.make(refs[:k_in], refs[k_in:k_in + k_out], refs[k_in + k_out:]))

    hbm = pl.BlockSpec(memory_space=pl.ANY)
    return pl.pallas_call(body, name=name, in_specs=[hbm] * k_in, out_specs=[hbm] * k_out, out_shape=job.outs,
                          scratch_shapes=job.sems, input_output_aliases=job.alias, compiler_params=_params())(*job.ins)


def _hosted(body, job, args, *, name, grid, in_specs, out_specs, out_shape, scratch_shapes=(), semantics, aliases=None):
    in_specs, out_specs, out_shape, scratch = list(in_specs), list(out_specs), list(out_shape), list(scratch_shapes)
    aliases = dict(aliases or {})
    if job is None:
        outs = pl.pallas_call(body, name=name, grid=grid, in_specs=in_specs, out_specs=out_specs, out_shape=out_shape,
                              scratch_shapes=scratch, input_output_aliases=aliases,
                              compiler_params=_params(dimension_semantics=semantics))(*args)
        return list(outs), None
    n_in, n_out, n_scr, k_in, k_out = len(in_specs), len(out_specs), len(scratch), len(job.ins), len(job.outs)

    def wrapped(*refs):
        ins, rest = refs[:n_in], refs[n_in:]
        job_ins, rest = rest[:k_in], rest[k_in:]
        outs, rest = rest[:n_out], rest[n_out:]
        job_outs, rest = rest[:k_out], rest[k_out:]
        scr, sems = rest[:n_scr], rest[n_scr:]
        ids = [pl.program_id(a) for a in range(len(grid))]
        first = functools.reduce(jnp.logical_and, [i == 0 for i in ids])
        last = functools.reduce(jnp.logical_and, [i == g - 1 for i, g in zip(ids, grid)])

        @pl.when(first)
        def _():
            for cp in job.make(job_ins, job_outs, sems)[0]:
                cp.start()

        body(*ins, *outs, *scr)

        @pl.when(last)
        def _():
            _run_phases(job.make(job_ins, job_outs, sems), first=1)

    hbm = pl.BlockSpec(memory_space=pl.ANY)
    outs = pl.pallas_call(
        wrapped, name=name, grid=grid, in_specs=in_specs + [hbm] * k_in, out_specs=out_specs + [hbm] * k_out,
        out_shape=out_shape + job.outs, scratch_shapes=scratch + job.sems,
        input_output_aliases={**aliases, **{n_in + i: n_out + o for i, o in job.alias.items()}},
        compiler_params=_params(dimension_semantics=("arbitrary",) * len(grid)))(*args, *job.ins)
    return list(outs[:n_out]), list(outs[n_out:])


def _cast_under(w, job, name, tm=512):
    L, R, C = w.shape

    def body(w_ref, o_ref):
        o_ref[...] = w_ref[...].astype(BF)

    spec = pl.BlockSpec((None, tm, C), lambda l, i: (l, i, 0))
    (cast,), outs = _hosted(body, job, (w,), name=name, grid=(L, R // tm), in_specs=[spec], out_specs=[spec],
                            out_shape=[jax.ShapeDtypeStruct(w.shape, BF)], semantics=("parallel", "parallel"))
    return cast, outs


def _proj_fwd(x, mod, wg, job=None, chunk=(0, 1), tm=512):
    S, D = x.shape
    tm = min(tm, S)
    tn = wg.shape[-1]
    c, n = chunk

    rows = S // tm

    def body(x_ref, mod_ref, w_ref, proj_ref, h_ref, hs):
        mine = pl.ds(pl.multiple_of(pl.program_id(1) * tm, tm), tm)

        @pl.when(pl.program_id(0) == 0)
        def _():
            xs, _ = _standardize(x_ref[...])
            h = xs * (1.0 + mod_ref[:, D:2 * D]) + mod_ref[:, 0:D]
            hb = h.astype(BF)
            hs[mine, :] = hb
            h_ref[...] = hb

        proj_ref[...] = _dot(hs[mine, :], w_ref[...])

    once = lambda j, i: jnp.where(j == 0, i, rows - 1)
    return _hosted(
        body, job, (x, mod, wg), name="proj_fwd",
        grid=(N_CHIPS, rows),
        in_specs=[pl.BlockSpec((tm, D), lambda j, i: (once(j, i), 0)),
                  pl.BlockSpec((1, 3 * D), lambda j, i: (0, 0)),
                  pl.BlockSpec((None, D, tn), lambda j, i: (j, 0, 0))],
        out_specs=[pl.BlockSpec((tm, tn), lambda j, i: (i, j * n + c)),
                   pl.BlockSpec((tm, D), lambda j, i: (once(j, i), 0))],
        out_shape=[jax.ShapeDtypeStruct((S, IN_COLS), F32), jax.ShapeDtypeStruct((S, D), BF)],
        scratch_shapes=[pltpu.VMEM((S, D), BF)],
        semantics=("arbitrary", "arbitrary"))


def _proj_cols(h, wg, chunk, proj, job=None, tm=512):
    S, D = h.shape
    tm = min(tm, S)
    tn = wg.shape[-1]
    c, n = chunk

    def body(h_ref, w_ref, prev_ref, proj_ref):
        mine = pl.ds(pl.multiple_of(pl.program_id(1) * tm, tm), tm)
        proj_ref[...] = _dot(h_ref[mine, :], w_ref[...])

    in_specs = [pl.BlockSpec((S, D), lambda j, i: (0, 0)), pl.BlockSpec((None, D, tn), lambda j, i: (j, 0, 0)),
                pl.BlockSpec(memory_space=pl.ANY)]
    return _hosted(body, job, (h, wg, proj), name="proj_cols", grid=(N_CHIPS, S // tm), in_specs=in_specs,
                   out_specs=[pl.BlockSpec((tm, tn), lambda j, i: (i, j * n + c))],
                   out_shape=[jax.ShapeDtypeStruct(proj.shape, proj.dtype)],
                   semantics=("arbitrary", "arbitrary"), aliases={2: 0})


SB_ROWS = 256
SB_KEYS = 256


def _attn_consts():
    j = np.arange(SB_KEYS)[:, None]
    s = np.arange(SB_KEYS)[None, :]
    from_here = np.concatenate([(j >= s), (j >= s)], axis=0).astype(np.float32)
    return jnp.asarray(from_here, BF), jnp.asarray((j <= s).astype(np.float32), BF)


def _hi_lo(x):
    hi = lax.bitcast_convert_type(lax.bitcast_convert_type(x, jnp.uint32) & jnp.uint32(0xFFFF0000), F32)
    return hi.astype(BF), (x - hi).astype(BF)


def _sums_r(x, t2):
    hi, lo = _hi_lo(x)
    return _dot(jnp.concatenate([hi, lo], axis=1), t2)


def _all_lanes(col, lanes):
    return jnp.broadcast_to(col, (col.shape[0], lanes))


def _attn_rows(ref, r0, rows, lanes, head0, scale=None):
    v = ref[pl.ds(r0, rows), lanes]
    if scale is not None:
        v = v * scale
    return jnp.concatenate([jnp.where(head0, v, 0.0), jnp.where(head0, 0.0, v)], axis=0).astype(BF)


SB_PAIRS_FWD = 4
SB_PAIRS_BWD = 2


def _attn_specs(S, n_pairs):
    return lambda base: pl.BlockSpec((S, n_pairs * SB_BLOCK), lambda p, base=base: (0, base // n_pairs + p))


def _attn_scores(q2n, k_ref, lanes, kj, t2, from_here_ref, masked):
    c0 = pl.multiple_of(kj * SB_KEYS, SB_KEYS)
    kb = k_ref[pl.ds(c0, SB_KEYS), lanes].astype(BF)
    zn = _dot_nt(q2n, kb)
    lsb = jnp.minimum(zn, 0.0) - jnp.log(1.0 + jnp.exp(-jnp.abs(zn)))
    valid = None
    if masked:
        valid = (lax.broadcasted_iota(jnp.int32, zn.shape, 1) + kj * SB_KEYS) < t2
        lsb = jnp.where(valid, lsb, 0.0)
    return c0, kb, zn, valid, lsb, _sums_r(lsb, from_here_ref[...])


def _attn_fwd(proj, from_here, job=None):
    S = proj.shape[0]
    TQ = SB_ROWS
    assert S % TQ == 0 and SB_KEYS == TQ
    scale = SB_HEAD_DIM ** -0.5
    n_pairs = SB_PAIRS_FWD
    pairs = range(n_pairs)
    lanes = [pl.ds(p * SB_BLOCK, SB_BLOCK) for p in pairs]

    def body(q_ref, k_ref, v_ref, from_here_ref, o_ref, tot_ref, run, acc):
        head0 = lax.broadcasted_iota(jnp.int32, (1, 2 * SB_HEAD_DIM), 1) < SB_HEAD_DIM

        def qloop(qi, _):
            r0 = pl.multiple_of(qi * TQ, TQ)
            q2n = [_attn_rows(q_ref, r0, TQ, lanes[p], head0, -scale) for p in pairs]
            trow = lax.broadcasted_iota(jnp.int32, (TQ, SB_KEYS), 0) + qi * TQ
            t2 = jnp.concatenate([trow, trow], axis=0)
            run[...] = jnp.zeros_like(run)
            acc[...] = jnp.zeros_like(acc)

            def step(kj, masked):
                got = [_attn_scores(q2n[p], k_ref, lanes[p], kj, t2, from_here_ref, masked) for p in pairs]
                for p in pairs:
                    c0, _, zn, valid, _, sums = got[p]
                    r = run[p]
                    e = sums - zn + jnp.concatenate([r, r], axis=1)
                    if masked:
                        e = jnp.where(valid, e, -jnp.inf)
                    acc[p] += _dot(jnp.exp(e).astype(BF), v_ref[pl.ds(c0, SB_KEYS), lanes[p]].astype(BF))
                    run[p] = r + _all_lanes(sums[:, 0:1], SB_BLOCK)

            @pl.when(qi % 2 == 0)
            def _():
                step(qi, True)

            @pl.when(qi % 2 == 1)
            def _():
                step(qi, True)
                step(qi - 1, False)

            first = qi - 1 - qi % 2

            def below(n, _):
                step(first - 2 * n, False)
                step(first - 1 - 2 * n, False)
                return 0

            lax.fori_loop(0, qi // 2, below, 0)
            for p in pairs:
                o_ref[pl.ds(r0, TQ), lanes[p]] = jnp.where(head0, acc[p, 0:TQ, :], acc[p, TQ:2 * TQ, :])
                tot_ref[p, 0, pl.ds(r0, TQ), :] = run[p, 0:TQ, :]
                tot_ref[p, 1, pl.ds(r0, TQ), :] = run[p, TQ:2 * TQ, :]
            return 0

        lax.fori_loop(0, S // TQ, qloop, 0)

    col = _attn_specs(S, n_pairs)
    state = pltpu.VMEM((n_pairs, 2 * TQ, SB_BLOCK), F32)
    return _hosted(
        body, job, (proj, proj, proj, from_here), name="attn_fwd",
        grid=(WIDTH // (n_pairs * SB_BLOCK),),
        in_specs=[col(0), col(4), col(8), pl.BlockSpec(from_here.shape, lambda p: (0, 0))],
        out_specs=[col(0), pl.BlockSpec((n_pairs, 2, S, 128), lambda p: (p, 0, 0, 0))],
        out_shape=[jax.ShapeDtypeStruct((S, WIDTH), F32), jax.ShapeDtypeStruct((WIDTH // 128, 2, S, 128), F32)],
        scratch_shapes=[state, state],
        semantics=("parallel",))


def _attn_bwd(proj, d_o, tot, from_here, up_to, dproj, job=None):
    S = proj.shape[0]
    TQ = SB_ROWS
    assert S % TQ == 0 and SB_KEYS == TQ
    scale = SB_HEAD_DIM ** -0.5
    n_pairs = SB_PAIRS_BWD
    pairs = range(n_pairs)
    lanes = [pl.ds(p * SB_BLOCK, SB_BLOCK) for p in pairs]

    def body(q_ref, k_ref, v_ref, do_ref, tot_ref, from_here_ref, up_to_ref, _, dproj_ref, pre, cum, dq_acc, dk_acc, dv_acc, stage, sems):
        dq_ref, dk_ref, dv_ref = stage.at[0], stage.at[1], stage.at[2]
        head0 = lax.broadcasted_iota(jnp.int32, (1, 2 * SB_HEAD_DIM), 1) < SB_HEAD_DIM
        dk_acc[...] = jnp.zeros_like(dk_acc)
        dv_acc[...] = jnp.zeros_like(dv_acc)

        def qloop(qi, _):
            r0 = pl.multiple_of(qi * TQ, TQ)
            q2n = [_attn_rows(q_ref, r0, TQ, lanes[p], head0, -scale) for p in pairs]
            do2 = [_attn_rows(do_ref, r0, TQ, lanes[p], head0) for p in pairs]
            trow = lax.broadcasted_iota(jnp.int32, (TQ, SB_KEYS), 0) + qi * TQ
            t2 = jnp.concatenate([trow, trow], axis=0)
            for p in pairs:
                pre[p, 0:TQ, :] = tot_ref[p, 0, pl.ds(r0, TQ), :]
                pre[p, TQ:2 * TQ, :] = tot_ref[p, 1, pl.ds(r0, TQ), :]
            cum[...] = jnp.zeros_like(cum)
            dq_acc[...] = jnp.zeros_like(dq_acc)

            def step(kj, masked):
                got = [_attn_scores(q2n[p], k_ref, lanes[p], kj, t2, from_here_ref, masked) for p in pairs]
                das = [_dot_nt(do2[p], v_ref[pl.ds(got[p][0], SB_KEYS), lanes[p]].astype(BF)) for p in pairs]
                for p in pairs:
                    c0, kb, zn, valid, lsb, sums = got[p]
                    later = pre[p] - _all_lanes(sums[:, 0:1], SB_BLOCK)
                    pre[p] = later
                    e = sums - zn + jnp.concatenate([later, later], axis=1)
                    sig = jnp.exp(lsb - zn)
                    if masked:
                        e = jnp.where(valid, e, -jnp.inf)
                        sig = jnp.where(valid, sig, 0.0)
                    a = jnp.exp(e)
                    w = das[p] * a
                    upto = _dot(w.astype(BF), up_to_ref[...])
                    c = cum[p]
                    dz = w - sig * (upto + jnp.concatenate([c, c], axis=1))
                    cum[p] = c + _all_lanes(upto[:, SB_KEYS - 1:SB_KEYS], SB_BLOCK)
                    dzb = dz.astype(BF)
                    dq_acc[p] += _dot(dzb, kb)
                    dk_acc[pl.ds(c0, SB_KEYS), lanes[p]] += _dot_tn(dzb, q2n[p])
                    dv_acc[pl.ds(c0, SB_KEYS), lanes[p]] += _dot_tn(a.astype(BF), do2[p])

            def below(n, _):
                step(2 * n, False)
                step(2 * n + 1, False)
                return 0

            lax.fori_loop(0, qi // 2, below, 0)

            @pl.when(qi % 2 == 1)
            def _():
                step(qi - 1, False)
                step(qi, True)

            @pl.when(qi % 2 == 0)
            def _():
                step(qi, True)
            for p in pairs:
                dq_ref[pl.ds(r0, TQ), lanes[p]] = (jnp.where(head0, dq_acc[p, 0:TQ, :], dq_acc[p, TQ:2 * TQ, :]) * scale).astype(BF)
            return 0

        lax.fori_loop(0, S // TQ, qloop, 0)
        dk_ref[...] = (-dk_acc[...]).astype(BF)
        dv_ref[...] = dv_acc[...].astype(BF)
        lane0 = pl.program_id(0) * (n_pairs * SB_BLOCK)
        _put_columns([dq_ref, dk_ref, dv_ref], dproj_ref, [WIDTH * n + lane0 for n in range(3)], sems)

    col = _attn_specs(S, n_pairs)
    whole = lambda a: pl.BlockSpec(a.shape, lambda p: (0, 0))
    hbm = pl.BlockSpec(memory_space=pl.ANY)
    state = pltpu.VMEM((n_pairs, 2 * TQ, SB_BLOCK), F32)
    grads = pltpu.VMEM((S, n_pairs * SB_BLOCK), F32)
    return _hosted(
        body, job, (proj, proj, proj, d_o, tot, from_here, up_to, dproj), name="attn_bwd",
        grid=(WIDTH // (n_pairs * SB_BLOCK),),
        in_specs=[col(0), col(4), col(8), col(0), pl.BlockSpec((n_pairs, 2, S, 128), lambda p: (p, 0, 0, 0)), whole(from_here),
                  whole(up_to), hbm],
        out_specs=[hbm],
        out_shape=[jax.ShapeDtypeStruct(dproj.shape, dproj.dtype)],
        scratch_shapes=[state, state, state, grads, grads, pltpu.VMEM((3, S, n_pairs * SB_BLOCK), BF), pltpu.SemaphoreType.DMA((3,))],
        semantics=("arbitrary",), aliases={7: 0})


HG_LEVELS = tuple(HG_CHUNK >> n for n in range(1, HG_CHUNK.bit_length()))


def _hgrn_consts():
    C = HG_CHUNK
    t = np.arange(C)[:, None]
    s = np.arange(C)[None, :]
    rows = [(s <= t), (s > t)]
    masks = [(t == s)]
    for m in HG_LEVELS:
        two = 2 * m
        mid = (t // two) * two + m
        right = (t % two) >= m
        rows.append((right & (s >= mid) & (s <= t)) | ((~right) & (s > t) & (s <= mid - 1)))
        masks.append(((t // two) == (s // two)) & right & ((s % two) < m))
    tri = np.concatenate(rows, axis=0).astype(np.float32)
    twice = lambda a: jnp.asarray(np.concatenate([a, a], axis=1), BF)
    return (twice(tri), twice(tri.T), jnp.asarray(np.stack(masks).astype(np.float32), F32))


HG_SUM_BLOCKS = 2 + len(HG_LEVELS)


def _split_rows(g):
    hi = g.astype(BF)
    return jnp.concatenate([hi, (g - hi.astype(F32)).astype(BF)], axis=0)


def _hgrn_sum_blocks(e):
    C = HG_CHUNK
    blocks = tuple(e[n * C:(n + 1) * C] for n in range(HG_SUM_BLOCKS))
    return blocks + (jnp.broadcast_to(e[C - 1:C], (HG_DIM, e.shape[1])),)


@jax.custom_vjp
def _hgrn_sums(tri, tri_t, g):
    return _hgrn_sum_blocks(_dot(tri, _split_rows(g)))


def _hgrn_sums_fwd(tri, tri_t, g):
    return _hgrn_sums(tri, tri_t, g), (tri, tri_t)


def _hgrn_sums_bwd(res, ds):
    tri, tri_t = res
    C = HG_CHUNK
    last = lax.broadcasted_iota(jnp.int32, (C, 1), 0) == C - 1
    prefix = ds[0] + jnp.where(last, jnp.sum(ds[-1], axis=0, keepdims=True), 0.0)
    d = jnp.concatenate((prefix,) + tuple(ds[1:-1]), axis=0)
    return jnp.zeros_like(tri), jnp.zeros_like(tri_t), _dot(tri_t[:, :d.shape[0]], d.astype(BF))


_hgrn_sums.defvjp(_hgrn_sums_fwd, _hgrn_sums_bwd)


def _bf_dot(a, b):
    return _dot(a.astype(BF), b.astype(BF))


def _bf_dot_nt(a, b):
    return _dot_nt(a.astype(BF), b.astype(BF))


def _bf_dot_tn(a, b):
    return _dot_tn(a.astype(BF), b.astype(BF))


@jax.custom_vjp
def _mm(a, b):
    return _bf_dot(a, b)


_mm.defvjp(lambda a, b: (_bf_dot(a, b), (a, b)), lambda r, ct: (_bf_dot_nt(ct, r[1]), _bf_dot_tn(r[0], ct)))


@jax.custom_vjp
def _mm_nt(a, b):
    return _bf_dot_nt(a, b)


_mm_nt.defvjp(lambda a, b: (_bf_dot_nt(a, b), (a, b)), lambda r, ct: (_bf_dot(ct, r[1]), _bf_dot_tn(ct, r[0])))


@jax.custom_vjp
def _mm_tn(a, b):
    return _bf_dot_tn(a, b)


_mm_tn.defvjp(lambda a, b: (_bf_dot_tn(a, b), (a, b)), lambda r, ct: (_bf_dot_nt(r[1], ct), _bf_dot(r[0], ct)))


def _hgrn_gates(tri, tri_t, qraw, fpre, lb):
    q = _silu(qraw)
    f = lb + (1.0 - lb) * _sigmoid(fpre)
    return q, 1.0 - f, _hgrn_sums(tri, tri_t, jnp.log(f))


def _hgrn_mix(masks, q, k, e, v, st):
    prefix, suffix, whole = e[0], e[1], e[-1]
    scores = masks[0] * _mm_nt(q, k)
    for n in range(len(HG_LEVELS)):
        decay = jnp.exp(e[2 + n])
        scores = scores + masks[n + 1] * _mm_nt(q * decay, k * decay)
    o = _mm_nt(q * jnp.exp(prefix), st) + _mm(scores, v)
    st_new = st * jnp.exp(whole) + _mm_tn(v, k * jnp.exp(suffix))
    return o, st_new


def _hgrn_chunk(tri, tri_t, masks, qraw, fpre, v, st, lb):
    q, k, e = _hgrn_gates(tri, tri_t, qraw, fpre, lb)
    return _hgrn_mix(masks, q, k, e, v, st)


HG_HEADS_PER_STEP = 4
HG_LANES = HG_HEADS_PER_STEP * HG_DIM


def _hgrn_specs(S, consts):
    col = lambda base: pl.BlockSpec((S, HG_LANES), lambda p, base=base: (0, base // HG_HEADS_PER_STEP + p))
    whole = [pl.BlockSpec(a.shape, lambda p, n=a.ndim: (0,) * n) for a in consts]
    return col, whole


def _hgrn_fwd(proj, lbs, consts, job=None):
    S = proj.shape[0]
    nc = S // HG_CHUNK
    heads = range(HG_HEADS_PER_STEP)

    def body(q_ref, f_ref, i_ref, lb_ref, tri_ref, trit_ref, mask_ref, o_ref, st_ref):
        tri, tri_t = tri_ref[...], trit_ref[...]
        masks = [mask_ref[n] for n in range(len(HG_LEVELS) + 1)]

        def chunk(ci, sts):
            r0 = pl.multiple_of(ci * HG_CHUNK, HG_CHUNK)
            rows = pl.ds(r0, HG_CHUNK)
            new = []
            lane = [pl.ds(hd * HG_DIM, HG_DIM) for hd in heads]
            gates = [_hgrn_gates(tri, tri_t, q_ref[rows, lane[hd]], f_ref[rows, lane[hd]], lb_ref[hd]) for hd in heads]
            for hd in heads:
                st_ref[hd, ci] = sts[hd]
                o, st_new = _hgrn_mix(masks, *gates[hd], i_ref[rows, lane[hd]], sts[hd])
                o_ref[rows, lane[hd]] = o
                new.append(st_new)
            return tuple(new)

        lax.fori_loop(0, nc, chunk, tuple(jnp.zeros((HG_DIM, HG_DIM), F32) for _ in heads))

    col, whole = _hgrn_specs(S, consts)
    return _hosted(
        body, job, (proj, proj, proj, lbs, *consts), name="hgrn_fwd",
        grid=(WIDTH // HG_LANES,),
        in_specs=[col(16), col(20), col(24), pl.BlockSpec((HG_HEADS_PER_STEP, 1, 128), lambda p: (p, 0, 0))] + whole,
        out_specs=[col(0), pl.BlockSpec((HG_HEADS_PER_STEP, nc, HG_DIM, HG_DIM), lambda p: (p, 0, 0, 0))],
        out_shape=[jax.ShapeDtypeStruct((S, WIDTH), F32), jax.ShapeDtypeStruct((WIDTH // 128, nc, HG_DIM, HG_DIM), F32)],
        semantics=("parallel",))


def _hgrn_bwd(proj, lbs, states, d_o, consts, dproj, job=None):
    S = proj.shape[0]
    nc = S // HG_CHUNK

    def body(q_ref, f_ref, i_ref, lb_ref, st_ref, do_ref, tri_ref, trit_ref, mask_ref, _, dlb_ref, dproj_ref, stage, sems):
        dq_ref, df_ref, di_ref = stage.at[0], stage.at[1], stage.at[2]
        masks = [mask_ref[n] for n in range(len(HG_LEVELS) + 1)]
        fn = functools.partial(_hgrn_chunk, tri_ref[...], trit_ref[...], masks)
        heads = range(HG_HEADS_PER_STEP)

        def chunk(n, carry):
            ci = nc - 1 - n
            r0 = pl.multiple_of(ci * HG_CHUNK, HG_CHUNK)
            rows = pl.ds(r0, HG_CHUNK)
            new = []
            lane = [pl.ds(hd * HG_DIM, HG_DIM) for hd in heads]
            pulls = [jax.vjp(fn, q_ref[rows, lane[hd]], f_ref[rows, lane[hd]], i_ref[rows, lane[hd]], st_ref[hd, ci], lb_ref[hd])[1]
                     for hd in heads]
            for hd in heads:
                d_st, dlb = carry[hd]
                lanes = lane[hd]
                dq, df, di, d_prev, dl = pulls[hd]((do_ref[rows, lanes], d_st))
                dq_ref[rows, lanes] = dq.astype(BF)
                df_ref[rows, lanes] = df.astype(BF)
                di_ref[rows, lanes] = di.astype(BF)
                new.append((d_prev, dlb + dl))
            return tuple(new)

        zero = (jnp.zeros((HG_DIM, HG_DIM), F32), jnp.zeros((1, HG_DIM), F32))
        done = lax.fori_loop(0, nc, chunk, tuple(zero for _ in heads))
        for hd in heads:
            dlb_ref[hd] = done[hd][1]
        lane0 = pl.program_id(0) * HG_LANES
        _put_columns([dq_ref, df_ref, di_ref], dproj_ref, [WIDTH * n + lane0 for n in (4, 5, 6)], sems)

    col, whole = _hgrn_specs(S, consts)
    head = pl.BlockSpec((HG_HEADS_PER_STEP, 1, 128), lambda p: (p, 0, 0))
    hbm = pl.BlockSpec(memory_space=pl.ANY)
    n_in = 6 + len(consts)
    return _hosted(
        body, job, (proj, proj, proj, lbs, states, d_o, *consts, dproj), name="hgrn_bwd",
        grid=(WIDTH // HG_LANES,),
        in_specs=[col(16), col(20), col(24), head, pl.BlockSpec((HG_HEADS_PER_STEP, nc, HG_DIM, HG_DIM), lambda p: (p, 0, 0, 0)), col(0)]
        + whole + [hbm],
        out_specs=[head, hbm],
        out_shape=[jax.ShapeDtypeStruct((WIDTH // 128, 1, 128), F32), jax.ShapeDtypeStruct(dproj.shape, dproj.dtype)],
        scratch_shapes=[pltpu.VMEM((3, S, HG_LANES), BF), pltpu.SemaphoreType.DMA((3,))],
        semantics=("arbitrary",), aliases={n_in: 1})


def _shift_down(x, n):
    rows = lax.broadcasted_iota(jnp.int32, x.shape, 0)
    return jnp.where(rows >= n, pltpu.roll(x, n, 0), 0.0)


def _shift_up(x, n):
    S = x.shape[0]
    rows = lax.broadcasted_iota(jnp.int32, x.shape, 0)
    return jnp.where(rows < S - n, pltpu.roll(x, S - n, 0), 0.0)


def _branch_fwd(proj, o_a, o_b, norm_w, conv_w, layer):
    S = proj.shape[0]

    def body(oa_ref, za_ref, ob_ref, zb_ref, nw_ref, pre_ref, post_ref, u_ref, zc_ref, cw_ref, ya_ref, yb_ref, yc_ref):
        ya_ref[...] = (oa_ref[...] * _silu(za_ref[...])).astype(BF)
        ob = ob_ref[...]
        rn = lax.rsqrt(jnp.mean(ob * ob, axis=-1, keepdims=True) + RMS_EPS)
        yb_ref[...] = (ob * rn * nw_ref[layer:layer + 1, :] * _silu(zb_ref[...])).astype(BF)
        pu = pre_ref[...] * u_ref[...]
        conv = cw_ref[2:3, :] * pu + cw_ref[1:2, :] * _shift_down(pu, 1) + cw_ref[0:1, :] * _shift_down(pu, 2)
        yc_ref[...] = (post_ref[...] * conv * _silu(zc_ref[...])).astype(BF)

    col = lambda base: pl.BlockSpec((S, 128), lambda p, base=base: (0, base + p))
    out = jax.ShapeDtypeStruct((S, WIDTH), BF)
    return pl.pallas_call(
        body, name="branch_fwd",
        grid=(WIDTH // 128,),
        in_specs=[col(0), col(12), col(0), col(28), pl.BlockSpec(norm_w.shape, lambda p: (0, 0)),
                  col(32), col(36), col(40), col(44), pl.BlockSpec((None, None, 3, 128), lambda p: (p, layer, 0, 0))],
        out_specs=[col(0), col(0), col(0)],
        out_shape=[out, out, out],
        compiler_params=_params(dimension_semantics=("parallel",)),
    )(o_a, proj, o_b, proj, norm_w, proj, proj, proj, proj, conv_w)


def _put_columns(tiles, dproj_ref, firsts, sems):
    copies = [pltpu.make_async_copy(t, dproj_ref.at[:, pl.ds(pl.multiple_of(c, 128), t.shape[1])], sems.at[n])
              for n, (t, c) in enumerate(zip(tiles, firsts))]
    for cp in copies:
        cp.start()
    for cp in copies:
        cp.wait()


def _branch_bwd(proj, o_a, o_b, norm_w, conv_w, dy_a, dy_b, dy_c, dproj, layer):
    S = proj.shape[0]
    firsts = [WIDTH * n for n in (3, 7, 8, 9, 10, 11)]

    def dsilu(z):
        s = _sigmoid(z)
        return s * z, s * (1.0 + z * (1.0 - s))

    def body(oa_ref, za_ref, ob_ref, zb_ref, nw_ref, pre_ref, post_ref, u_ref, zc_ref, cw_ref, dya_ref, dyb_ref, dyc_ref, _,
             doa_ref, dob_ref, dnw_ref, dcw_ref, dproj_ref, stage, sems):
        dza_ref, dzb_ref, dpre_ref, dpost_ref, du_ref, dzc_ref = [stage.at[n] for n in range(6)]
        dya = dya_ref[...]
        sa, dsa = dsilu(za_ref[...])
        doa_ref[...] = dya * sa
        dza_ref[...] = (dya * oa_ref[...] * dsa).astype(BF)

        dyb = dyb_ref[...]
        ob = ob_ref[...]
        nw = nw_ref[layer:layer + 1, :]
        sb, dsb = dsilu(zb_ref[...])
        rn = lax.rsqrt(jnp.mean(ob * ob, axis=-1, keepdims=True) + RMS_EPS)
        on = ob * rn
        dzb_ref[...] = (dyb * on * nw * dsb).astype(BF)
        don_w = dyb * sb
        dnw_ref[...] = jnp.sum(don_w * on, axis=0, keepdims=True)
        don = don_w * nw
        dob_ref[...] = rn * (don - on * jnp.mean(don * on, axis=-1, keepdims=True))

        dyc = dyc_ref[...]
        pre, post, u = pre_ref[...], post_ref[...], u_ref[...]
        sc, dsc = dsilu(zc_ref[...])
        pu = pre * u
        pu1, pu2 = _shift_down(pu, 1), _shift_down(pu, 2)
        conv = cw_ref[2:3, :] * pu + cw_ref[1:2, :] * pu1 + cw_ref[0:1, :] * pu2
        dzc_ref[...] = (dyc * post * conv * dsc).astype(BF)
        dpost_ref[...] = (dyc * conv * sc).astype(BF)
        dconv = dyc * post * sc
        dcw_ref[0:1, :] = jnp.sum(dconv * pu2, axis=0, keepdims=True)
        dcw_ref[1:2, :] = jnp.sum(dconv * pu1, axis=0, keepdims=True)
        dcw_ref[2:3, :] = jnp.sum(dconv * pu, axis=0, keepdims=True)
        dpu = cw_ref[2:3, :] * dconv + cw_ref[1:2, :] * _shift_up(dconv, 1) + cw_ref[0:1, :] * _shift_up(dconv, 2)
        dpre_ref[...] = (dpu * u).astype(BF)
        du_ref[...] = (dpu * pre).astype(BF)
        lane0 = pl.program_id(0) * 128
        _put_columns([stage.at[n] for n in range(6)], dproj_ref, [c + lane0 for c in firsts], sems)

    col = lambda base: pl.BlockSpec((S, 128), lambda p, base=base: (0, base + p))
    f32 = jax.ShapeDtypeStruct((S, WIDTH), F32)
    hbm = pl.BlockSpec(memory_space=pl.ANY)
    return pl.pallas_call(
        body, name="branch_bwd",
        grid=(WIDTH // 128,),
        in_specs=[col(0), col(12), col(0), col(28), pl.BlockSpec(norm_w.shape, lambda p: (0, 0)),
                  col(32), col(36), col(40), col(44), pl.BlockSpec((None, None, 3, 128), lambda p: (p, layer, 0, 0)),
                  col(0), col(0), col(0), hbm],
        out_specs=[col(0), col(0), pl.BlockSpec((None, 1, 128), lambda p: (p, 0, 0)), pl.BlockSpec((None, 3, 128), lambda p: (p, 0, 0)), hbm],
        out_shape=[f32, f32, jax.ShapeDtypeStruct((WIDTH // 128, 1, 128), F32), jax.ShapeDtypeStruct((WIDTH // 128, 3, 128), F32),
                   jax.ShapeDtypeStruct(dproj.shape, dproj.dtype)],
        scratch_shapes=[pltpu.VMEM((6, S, 128), BF), pltpu.SemaphoreType.DMA((6,))],
        input_output_aliases={13: 4},
        compiler_params=_params(dimension_semantics=("arbitrary",)),
    )(o_a, proj, o_b, proj, norm_w, proj, proj, proj, proj, conv_w, dy_a, dy_b, dy_c, dproj)


def _branch_proj(y_refs, wb_ref):
    return [_dot(y_refs[i][...], wb_ref[i]) for i in range(3)]


def _merge_fwd(x, mod, proj, ys, wb, wo, ln_g, ln_b, layer, job=None, tm=512):
    S, D = x.shape
    tm = min(tm, S)

    def body(x_ref, mod_ref, ga_ref, gb_ref, gc_ref, ya_ref, yb_ref, yc_ref, wb_ref, wo_ref, g_ref, b_ref, xo_ref, mg_ref, y_ref):
        ps = _branch_proj((ya_ref, yb_ref, yc_ref), wb_ref)
        merged = _sigmoid(ga_ref[...]) * ps[0] + _sigmoid(gb_ref[...]) * ps[1] + _sigmoid(gc_ref[...]) * ps[2]
        mb = merged.astype(BF)
        mg_ref[...] = mb
        y = _dot(mb, wo_ref[...].reshape(D, D))
        y_ref[...] = y
        r = ALPHA * x_ref[...] + (1.0 + mod_ref[:, 2 * D:3 * D]) * y
        xn, _ = _standardize(r)
        xo_ref[...] = xn * g_ref[layer:layer + 1, :] + b_ref[layer:layer + 1, :]

    row = lambda w, c=0: pl.BlockSpec((tm, w), lambda i, c=c: (i, c))
    whole = lambda a: pl.BlockSpec(a.shape, lambda i, n=a.ndim: (0,) * n)
    return _hosted(
        body, job, (x, mod, proj, proj, proj, *ys, wb, wo, ln_g, ln_b), name="merge_fwd",
        grid=(S // tm,),
        in_specs=[row(D), whole(mod), row(D, 6), row(D, 7), row(D, 8), row(WIDTH), row(WIDTH), row(WIDTH), whole(wb),
                  whole(wo), whole(ln_g), whole(ln_b)],
        out_specs=[row(D), row(D), row(D)],
        out_shape=[jax.ShapeDtypeStruct((S, D), F32), jax.ShapeDtypeStruct((S, D), BF), jax.ShapeDtypeStruct((S, D), F32)],
        semantics=("parallel",))


def _merge_bwd(dxo, x, y, merged, mod, proj, ys, wb, wo, ln_g, layer, tm=256):
    S, D = x.shape
    tm = min(tm, S)
    steps = S // tm
    quarter = D // N_CHIPS

    def body(dxo_ref, x_ref, y_ref, mg_ref, mod_ref, ga_ref, gb_ref, gc_ref, ya_ref, yb_ref, yc_ref, wb_ref, wo_ref, g_ref,
             dxr_ref, dg_ref, dya_ref, dyb_ref, dyc_ref, dlg_ref, dlb_ref, dgt_ref, gbr_ref, gout_ref, acc_br, acc_out):
        @pl.when(pl.program_id(0) == 0)
        def _():
            dlg_ref[...] = jnp.zeros_like(dlg_ref)
            dlb_ref[...] = jnp.zeros_like(dlb_ref)
            dgt_ref[...] = jnp.zeros_like(dgt_ref)
            acc_br[...] = jnp.zeros_like(acc_br)
            acc_out[...] = jnp.zeros_like(acc_out)

        gate1 = 1.0 + mod_ref[:, 2 * D:3 * D]
        yv = y_ref[...]
        xn, rstd = _standardize(ALPHA * x_ref[...] + gate1 * yv)
        dxo = dxo_ref[...]
        dlg_ref[...] += jnp.sum(dxo * xn, axis=0, keepdims=True)
        dlb_ref[...] += jnp.sum(dxo, axis=0, keepdims=True)
        dr = _standardize_bwd(dxo * g_ref[layer:layer + 1, :], xn, rstd)
        dxr_ref[...] = ALPHA * dr
        dgt_ref[...] += jnp.sum(dr * yv, axis=0, keepdims=True)
        dyb = (gate1 * dr).astype(BF)
        acc_out[...] += _dot_tn(mg_ref[...], dyb)
        dmerged = _dot_nt(dyb, wo_ref[...].reshape(D, D))
        y_refs = (ya_ref, yb_ref, yc_ref)
        ps = _branch_proj(y_refs, wb_ref)
        for i, (gate_ref, out_ref) in enumerate(((ga_ref, dya_ref), (gb_ref, dyb_ref), (gc_ref, dyc_ref))):
            sg = _sigmoid(gate_ref[...])
            dg_ref[:, i * D:(i + 1) * D] = (dmerged * ps[i] * sg * (1.0 - sg)).astype(BF)
            dp = (dmerged * sg).astype(BF)
            acc_br[i] += _dot_tn(y_refs[i][...], dp)
            out_ref[...] = _dot_nt(dp, wb_ref[i])

        @pl.when(pl.program_id(0) == steps - 1)
        def _():
            for j in range(N_CHIPS):
                gout_ref[j] = acc_out[j * quarter:(j + 1) * quarter, :].astype(BF)
                for i in range(3):
                    gbr_ref[j, i] = acc_br[i, :, j * quarter:(j + 1) * quarter].astype(BF)

    row = lambda w, c=0: pl.BlockSpec((tm, w), lambda i, c=c: (i, c))
    whole = lambda a: pl.BlockSpec(a.shape, lambda i, n=len(a.shape): (0,) * n)
    vec = pl.BlockSpec((1, D), lambda i: (0, 0))
    sd = jax.ShapeDtypeStruct
    g_br, g_out = sd((N_CHIPS, 3, WIDTH, quarter), BF), sd((N_CHIPS, quarter, D), BF)
    return pl.pallas_call(
        body, name="merge_bwd",
        grid=(steps,),
        in_specs=[row(D), row(D), row(D), row(D), whole(mod), row(D, 6), row(D, 7), row(D, 8), row(WIDTH), row(WIDTH), row(WIDTH),
                  whole(wb), whole(wo), whole(ln_g)],
        out_specs=[row(D), row(3 * D, IN_COLS // (3 * D) - 1), row(WIDTH), row(WIDTH), row(WIDTH), vec, vec, vec, whole(g_br), whole(g_out)],
        out_shape=[sd((S, D), F32), sd((S, IN_COLS), BF), sd((S, WIDTH), F32), sd((S, WIDTH), F32), sd((S, WIDTH), F32),
                   sd((1, D), F32), sd((1, D), F32), sd((1, D), F32), g_br, g_out],
        scratch_shapes=[pltpu.VMEM((3, WIDTH, D), F32), pltpu.VMEM((D, D), F32)],
        compiler_params=_params(dimension_semantics=("arbitrary",)),
    )(dxo, x, y, merged, mod, proj, proj, proj, *ys, wb, wo, ln_g)


def _loss_head(x, target, tm=512):
    S, D = x.shape
    tm = min(tm, S)

    def body(x_ref, t_ref, dx_ref, loss_ref):
        @pl.when(pl.program_id(0) == 0)
        def _():
            loss_ref[...] = jnp.zeros_like(loss_ref)

        err = x_ref[...] - t_ref[...]
        dx_ref[...] = err * (1.0 / D)
        loss_ref[...] += 0.5 * jnp.sum(jnp.mean(err * err, axis=-1, keepdims=True))

    row = pl.BlockSpec((tm, D), lambda i: (i, 0))
    return pl.pallas_call(
        body, name="loss_head",
        grid=(S // tm,),
        in_specs=[row, row],
        out_specs=[row, pl.BlockSpec((8, 128), lambda i: (0, 0))],
        out_shape=[jax.ShapeDtypeStruct((S, D), F32), jax.ShapeDtypeStruct((8, 128), F32)],
        compiler_params=_params(dimension_semantics=("arbitrary",)),
    )(x, target)


def _proj_bwd(dproj, wgs, x, mod, dx_res, job=None, tm=512, tk=768):
    S, D = x.shape
    tm = min(tm, S)
    n = len(wgs)
    w_args = list(wgs)
    if n == 1:
        per = wgs[0].shape[-1] // tk
        w_specs = [pl.BlockSpec((None, D, tk), lambda k, i: (k // per, 0, k % per))]
    else:
        assert wgs[0].shape[-1] == tk
        w_specs = [pl.BlockSpec((None, D, tk), lambda k, i, c=c: (jnp.minimum((k + n - 1 - c) // n, N_CHIPS - 1), 0, 0))
                   for c in range(n)]
    nk = IN_COLS // tk

    def body(dp_ref, *rest):
        w_refs, (x_ref, mod_ref, dxr_ref, dx_ref, dsh_ref, dsc_ref, acc) = rest[:n], rest[n:]
        k, i = pl.program_id(0), pl.program_id(1)
        mine = pl.ds(pl.multiple_of(i * tm, tm), tm)

        @pl.when((i == 0) & (k == 0))
        def _():
            dsh_ref[...] = jnp.zeros_like(dsh_ref)
            dsc_ref[...] = jnp.zeros_like(dsc_ref)

        @pl.when(k == 0)
        def _():
            acc[mine, :] = jnp.zeros((tm, D), F32)

        for c in range(n):
            @pl.when(k % n == c)
            def _(c=c):
                acc[mine, :] += _dot_nt(dp_ref[...], w_refs[c][...])

        @pl.when(k == nk - 1)
        def _():
            dh = acc[mine, :]
            xs, rstd = _standardize(x_ref[...])
            dsh_ref[...] += jnp.sum(dh, axis=0, keepdims=True)
            dsc_ref[...] += jnp.sum(dh * xs, axis=0, keepdims=True)
            dx_ref[...] = _standardize_bwd(dh * (1.0 + mod_ref[:, D:2 * D]), xs, rstd) + dxr_ref[...]

    row = pl.BlockSpec((tm, D), lambda k, i: (jnp.where(k == nk - 1, i, 0), 0))
    vec = pl.BlockSpec((1, D), lambda k, i: (0, 0))
    return _hosted(
        body, job, (dproj, *w_args, x, mod, dx_res), name="proj_bwd",
        grid=(nk, S // tm),
        in_specs=[pl.BlockSpec((tm, tk), lambda k, i: (i, k))] + w_specs + [row, pl.BlockSpec((1, 3 * D), lambda k, i: (0, 0)), row],
        out_specs=[row, vec, vec],
        out_shape=[jax.ShapeDtypeStruct((S, D), F32), jax.ShapeDtypeStruct((1, D), F32), jax.ShapeDtypeStruct((1, D), F32)],
        scratch_shapes=[pltpu.VMEM((S, D), F32)],
        semantics=("arbitrary", "arbitrary"))


def _grad_w_in(h, dproj):
    S, D = h.shape
    shard = IN_COLS // N_CHIPS

    def body(h_ref, d_ref, o_ref):
        o_ref[...] = _dot_tn(h_ref[...], d_ref[...]).astype(BF)

    return pl.pallas_call(
        body, name="grad_w_in",
        grid=(N_CHIPS,),
        in_specs=[pl.BlockSpec((S, D), lambda n: (0, 0)), pl.BlockSpec((S, shard), lambda n: (0, n))],
        out_specs=pl.BlockSpec((None, D, shard), lambda n: (n, 0, 0)),
        out_shape=jax.ShapeDtypeStruct((N_CHIPS, D, shard), BF),
        compiler_params=_params(dimension_semantics=("parallel",)),
    )(h, dproj)


def _all_gather8(x, name):
    R, N = x.shape

    def body(x_ref, out_ref, send_sems, recv_sems):
        mx, my, mc = lax.axis_index("x"), lax.axis_index("y"), lax.axis_index("c")
        me = 4 * mx + 2 * my + mc
        out_ref[me] = x_ref[...]
        copies = []
        for k in range(1, N_DEV):
            peer = (_flip(mx, k & 4), _flip(my, k & 2), _flip(mc, k & 1))
            cp = pltpu.make_async_remote_copy(src_ref=x_ref, dst_ref=out_ref.at[me], send_sem=send_sems.at[k - 1],
                                              recv_sem=recv_sems.at[k - 1], device_id=peer, device_id_type=MESH)
            cp.start()
            copies.append(cp)
        for cp in copies:
            cp.wait()

    return pl.pallas_call(
        body, name=name,
        in_specs=[pl.BlockSpec(memory_space=pltpu.VMEM)],
        out_specs=pl.BlockSpec(memory_space=pltpu.VMEM),
        out_shape=jax.ShapeDtypeStruct((N_DEV, R, N), F32),
        scratch_shapes=[pltpu.SemaphoreType.DMA((N_DEV - 1,)), pltpu.SemaphoreType.DMA((N_DEV - 1,))],
        compiler_params=_params(),
    )(x)


def _rows2d(a):
    return a.reshape(-1, a.shape[-1])


def _tile_rows(rows, cols, n_arrays):
    budget = (24 << 20) // (n_arrays * 2 * 4 * cols)
    if rows <= budget:
        return rows
    tm = 8
    for cand in range(8, budget + 1, 8):
        if rows % cand == 0:
            tm = cand
    return tm


SUM_ROWS = 256


def _sum_cores(g, sent, where):
    chips, lead, _, r, cols = g.shape
    tr = min(r, SUM_ROWS)

    def body(where_ref, g_ref, s_ref, out_ref):
        out_ref[...] = (g_ref[...].astype(F32) + s_ref[...].astype(F32)).astype(BF)

    spec = pl.BlockSpec((None, tr, cols), lambda i, j, where_ref: (i, j, 0))
    out = pl.pallas_call(
        body, name="sum_cores",
        grid_spec=pltpu.PrefetchScalarGridSpec(
            num_scalar_prefetch=1, grid=(chips * lead, r // tr),
            in_specs=[pl.BlockSpec((None, None, tr, cols), lambda i, j, where_ref: (i, where_ref[1], j, 0)), spec],
            out_specs=spec),
        out_shape=jax.ShapeDtypeStruct((chips * lead, r, cols), BF),
        compiler_params=_params(dimension_semantics=("parallel", "parallel")),
    )(where, g.reshape(chips * lead, 2, r, cols), sent.reshape(chips * lead, r, cols))
    return out.reshape(chips, lead, r, cols)


def _sum_chips(sums, got, where):
    _, lead, r, cols = sums.shape
    tr = min(r, SUM_ROWS)

    def body(where_ref, s_ref, g_ref, out_ref):
        out_ref[...] = ((s_ref[...].astype(F32) + g_ref[0].astype(F32)) + g_ref[1].astype(F32)) + g_ref[2].astype(F32)

    return pl.pallas_call(
        body, name="sum_chips",
        grid_spec=pltpu.PrefetchScalarGridSpec(
            num_scalar_prefetch=1, grid=(lead, r // tr),
            in_specs=[pl.BlockSpec((None, None, tr, cols), lambda i, j, where_ref: (where_ref[0], i, j, 0)),
                      pl.BlockSpec((N_CHIPS - 1, None, tr, cols), lambda i, j, where_ref: (0, i, j, 0))],
            out_specs=pl.BlockSpec((None, None, tr, cols), lambda i, j, where_ref: (i, where_ref[1], j, 0))),
        out_shape=jax.ShapeDtypeStruct((lead, 2, r, cols), F32),
        compiler_params=_params(dimension_semantics=("parallel", "parallel")),
    )(where, sums, got)


def _adamw(w, m, v, groups, name):
    shape = w.shape
    w2, m2, v2 = _rows2d(w), _rows2d(m), _rows2d(v)
    rows, cols = w2.shape
    ng = len(groups)
    n = len(groups[0])
    slab = rows // ng
    gs = [_rows2d(g) for grp in groups for g in grp]
    tm = _tile_rows(slab, cols, 7 + n)
    tiles = slab // tm
    c1 = 1.0 / (1.0 - ADAM_B1 ** ADAM_STEP)
    c2 = 1.0 / (1.0 - ADAM_B2 ** ADAM_STEP)

    def body(*refs):
        w_ref, m_ref, v_ref = refs[:3]
        g_refs = refs[3:3 + ng * n]
        go_ref, d_ref, mo_ref, vo_ref = refs[3 + ng * n:]
        which = pl.program_id(0)
        for s in range(ng):
            @pl.when(which == s)
            def _(s=s):
                g = g_refs[s * n][...]
                for r in g_refs[s * n + 1:(s + 1) * n]:
                    g = g + r[...]
                mn = ADAM_B1 * m_ref[...] + (1.0 - ADAM_B1) * g
                vn = ADAM_B2 * v_ref[...] + (1.0 - ADAM_B2) * (g * g)
                go_ref[...] = g
                mo_ref[...] = mn
                vo_ref[...] = vn
                d_ref[...] = -ADAM_LR * ((mn * c1) / (jnp.sqrt(vn * c2) + ADAM_EPS) + ADAM_WD * w_ref[...])

    spec = pl.BlockSpec((tm, cols), lambda s, i: (s * tiles + i, 0))
    g_specs = [pl.BlockSpec((tm, cols), lambda s, i, k=k: (jnp.where(s == k, i, jnp.where(s < k, 0, tiles - 1)), 0))
               for k in range(ng) for _ in range(n)]
    outs = pl.pallas_call(
        body, name=name,
        grid=(ng, tiles),
        in_specs=[spec] * 3 + g_specs,
        out_specs=[spec] * 4,
        out_shape=[jax.ShapeDtypeStruct((rows, cols), F32)] * 4,
        compiler_params=_params(dimension_semantics=("arbitrary", "arbitrary")),
    )(w2, m2, v2, *gs)
    return [o.reshape(shape) for o in outs]


def _lower_bounds(r0, r1):
    top = jnp.maximum(r0, r1)
    e0, e1 = jnp.exp(r0 - top), jnp.exp(r1 - top)
    p0, p1 = e0 / (e0 + e1), e1 / (e0 + e1)
    return p0 - p0, (p0 + p1) - p0


def _lbs_fwd(lb_raw):
    def body(lb_ref, out_ref):
        l0, l1 = _lower_bounds(lb_ref[0:1, :], lb_ref[1:2, :])
        out_ref[0:1, :] = l0
        out_ref[1:2, :] = l1

    return pl.pallas_call(body, name="lower_bounds", out_shape=jax.ShapeDtypeStruct(lb_raw.shape, F32), compiler_params=_params())(lb_raw)


def _mod_rows(c_all, w_mod, tn=768):
    _, D, cols = w_mod.shape

    def body(c_ref, w_ref, out_ref):
        out_ref[...] = _dot(c_ref[...].astype(BF), w_ref[...].astype(BF))

    return pl.pallas_call(
        body, name="mod_rows",
        grid=(DEPTH,),
        in_specs=[pl.BlockSpec((N_DEV, D), lambda l: (0, 0)), pl.BlockSpec((None, D, cols), lambda l: (l, 0, 0))],
        out_specs=pl.BlockSpec((N_DEV, cols), lambda l: (0, l)),
        out_shape=jax.ShapeDtypeStruct((N_DEV, DEPTH * cols), F32),
        compiler_params=_params(dimension_semantics=("parallel",)),
    )(c_all, w_mod)


def _grad_w_mod(c_all, dmod_cols):
    D = c_all.shape[1]
    cols = dmod_cols.shape[-1]

    def body(c_ref, d_ref, out_ref):
        out_ref[...] = _dot_tn(c_ref[...].astype(BF), d_ref[...].astype(BF))

    return pl.pallas_call(
        body, name="grad_w_mod",
        grid=(DEPTH,),
        in_specs=[pl.BlockSpec((N_DEV, D), lambda l: (0, 0)), pl.BlockSpec((None, N_DEV, cols), lambda l: (l, 0, 0))],
        out_specs=pl.BlockSpec((None, D, cols), lambda l: (l, 0, 0)),
        out_shape=jax.ShapeDtypeStruct((DEPTH, D, cols), F32),
        compiler_params=_params(dimension_semantics=("parallel",)),
    )(c_all, dmod_cols)


def _sum_devices(parts):
    _, R, N = parts.shape

    def body(p_ref, out_ref):
        acc = p_ref[0]
        for d in range(1, N_DEV):
            acc = acc + p_ref[d]
        out_ref[...] = acc

    return pl.pallas_call(body, name="sum_devices", out_shape=jax.ShapeDtypeStruct((R, N), F32), compiler_params=_params())(parts)


def _lbs_bwd(lb_raw, dl):
    def body(lb_ref, dl_ref, out_ref):
        _, pull = jax.vjp(_lower_bounds, lb_ref[0:1, :], lb_ref[1:2, :])
        d0, d1 = pull((dl_ref[0:1, :], dl_ref[1:2, :]))
        out_ref[0:1, :] = d0
        out_ref[1:2, :] = d1

    return pl.pallas_call(body, name="lower_bounds_bwd", out_shape=jax.ShapeDtypeStruct(lb_raw.shape, F32), compiler_params=_params())(lb_raw, dl)


def kernel(x, c, w_mod, b_mod, w_in, conv_w, hgrn_norm_w, lower_bounds, w_branch, w_out, ln_g, ln_b, loss_target, m_w_mod, m_b_mod, m_w_in, m_conv_w, m_hgrn_norm_w, m_lower_bounds, m_w_branch, m_w_out, m_ln_g, m_ln_b, v_w_mod, v_b_mod, v_w_in, v_conv_w, v_hgrn_norm_w, v_lower_bounds, v_w_branch, v_w_out, v_ln_g, v_ln_b):
    D = D_MODEL
    x0 = x[0]
    target = loss_target[0]
    S = x0.shape[0]
    mx, my, mc = lax.axis_index("x"), lax.axis_index("y"), lax.axis_index("c")
    chip = 2 * mx + my
    me = 2 * chip + mc
    mod_cols = 3 * D // N_CHIPS

    plan = _Plan(w_in, w_branch, w_out, chip, mc)
    n_conv = DEPTH * 3 * (WIDTH // N_CHIPS)
    first = jnp.concatenate([c, conv_w.reshape(1, n_conv), jnp.zeros((1, 2 * D - D - n_conv), F32)], axis=1)
    first = plan.first(first.reshape(8, 2 * D // 8)).reshape(N_DEV, 2 * D)

    c_all = first[:, :D]
    conv_all = first[:, D:D + n_conv].reshape(N_DEV, DEPTH, 3, WIDTH // N_CHIPS)[0::2]
    mod_part = _all_gather8(_mod_rows(c_all, w_mod), "gather_mod")[0::2]
    mod_part = lax.dynamic_index_in_dim(mod_part, me, axis=1, keepdims=False).reshape(N_CHIPS, DEPTH, mod_cols)
    mods = [(mod_part[:, l].reshape(1, 3 * D) + b_mod[l][None, :]) for l in range(DEPTH)]
    lbs = _lbs_fwd(lower_bounds).reshape(DEPTH, WIDTH // 128, 1, 128)
    loss_blk, dx, small = _local_step(x0, target, mods, lbs, conv_all, hgrn_norm_w, ln_g, ln_b, plan)

    n_mod, n_nw, n_lb, n_ln, n_cw = DEPTH * 3 * D, DEPTH * 128, DEPTH * WIDTH, DEPTH * D, DEPTH * 3 * WIDTH
    row = jnp.concatenate(
        [jnp.concatenate([small[l][0], small[l][1], small[l][2]], axis=1) for l in range(DEPTH)]
        + [jnp.sum(small[l][3], axis=0) for l in range(DEPTH)]
        + [small[l][4].reshape(1, WIDTH) for l in range(DEPTH)]
        + [small[l][5] for l in range(DEPTH)] + [small[l][6] for l in range(DEPTH)]
        + [jnp.transpose(small[l][7], (1, 0, 2)).reshape(1, 3 * WIDTH) for l in range(DEPTH)]
        + [loss_blk[0:1, :]], axis=1)
    n_row = row.shape[1]
    fold = -(-n_row // (8 * 128)) * 128
    rows = jnp.concatenate([row, jnp.zeros((1, 8 * fold - n_row), F32)], axis=1).reshape(8, fold)

    whole, gathered = plan.finish(rows)
    grads = {kind: [[whole[(kind, l)]] for l in range(DEPTH)] for kind in ("in", "br", "out")}

    off_nw = n_mod
    off_lb = off_nw + n_nw
    off_lng = off_lb + n_lb
    off_lnb = off_lng + n_ln
    off_cw = off_lnb + n_ln
    off_loss = off_cw + n_cw
    total = _sum_devices(gathered).reshape(1, 8 * fold)
    gathered = gathered.reshape(N_DEV, 1, 8 * fold)
    d_lower = _lbs_bwd(lower_bounds, total[0, off_lb:off_lng].reshape(DEPTH, WIDTH))
    loss = total[0, off_loss]
    d_b_mod = total[0, :n_mod].reshape(DEPTH, 3 * D)
    d_norm_w = total[0, off_nw:off_lb].reshape(DEPTH, 128)
    d_ln_g = total[0, off_lng:off_lnb].reshape(DEPTH, D)
    d_ln_b = total[0, off_lnb:off_cw].reshape(DEPTH, D)
    d_conv = total[0, off_cw:off_loss].reshape(DEPTH, 3, N_CHIPS, WIDTH // N_CHIPS)
    d_conv = lax.dynamic_index_in_dim(d_conv, chip, axis=2, keepdims=False)
    dmod_all = gathered[:, 0, :n_mod].reshape(N_DEV, DEPTH, N_CHIPS, mod_cols)
    dmod_cols = jnp.transpose(lax.dynamic_index_in_dim(dmod_all, chip, axis=2, keepdims=False), (1, 0, 2))
    d_w_mod = _grad_w_mod(c_all, dmod_cols)

    res = {}
    res["w_mod"] = _adamw(w_mod, m_w_mod, v_w_mod, [[d_w_mod]], "adamw_w_mod")
    res["b_mod"] = _adamw(b_mod, m_b_mod, v_b_mod, [[d_b_mod]], "adamw_b_mod")
    res["w_in"] = _adamw(w_in, m_w_in, v_w_in, grads["in"], "adamw_w_in")
    res["conv_w"] = _adamw(conv_w, m_conv_w, v_conv_w, [[d_conv]], "adamw_conv_w")
    res["hgrn_norm_w"] = _adamw(hgrn_norm_w, m_hgrn_norm_w, v_hgrn_norm_w, [[d_norm_w]], "adamw_norm_w")
    res["lower_bounds"] = _adamw(lower_bounds, m_lower_bounds, v_lower_bounds, [[d_lower]], "adamw_lower_bounds")
    res["w_branch"] = _adamw(w_branch, m_w_branch, v_w_branch, grads["br"], "adamw_w_branch")
    res["w_out"] = _adamw(w_out, m_w_out, v_w_out, grads["out"], "adamw_w_out")
    res["ln_g"] = _adamw(ln_g, m_ln_g, v_ln_g, [[d_ln_g]], "adamw_ln_g")
    res["ln_b"] = _adamw(ln_b, m_ln_b, v_ln_b, [[d_ln_b]], "adamw_ln_b")
    names = ["w_mod", "b_mod", "w_in", "conv_w", "hgrn_norm_w", "lower_bounds", "w_branch", "w_out", "ln_g", "ln_b"]
    return (loss, dx[None], *[res[n][0] for n in names], *[res[n][1] for n in names],
            *[res[n][2] for n in names], *[res[n][3] for n in names])


class _Plan:
    FIRST_CHUNKS = 2
    WINDOWS = ((0, 1152), (1152, 640), (1792, 512))

    def __init__(self, w_in, w_br, w_out, chip, core):
        self.local = {"in": w_in, "br": w_br, "out": w_out}
        self.chip, self.where = chip, jnp.stack([chip, core]).astype(jnp.int32)
        self.gathered, self.partial, self.grads, self.chip_sums, self.scattered, self.pending = {}, {}, {}, {}, {}, {}

    def chunks(self, l):
        return self.FIRST_CHUNKS if l == 0 else 1

    def w_in(self, l):
        return [self.gathered[("in", l, c)] for c in range(self.chunks(l))]

    def _shard(self, key):
        mine = self.local[key[0]][key[1]]
        if key[0] == "in":
            cols = mine.shape[-1] // self.chunks(key[1])
            mine = mine[:, key[2] * cols:(key[2] + 1) * cols]
        return mine.astype(BF)

    def _slab(self, key):
        mine = _halves(self._shard(key))
        return lax.dynamic_update_slice(lax.empty((N_CHIPS,) + mine.shape, mine.dtype), mine[None], (self.chip, 0, 0, 0, 0))

    def _gather(self, keys):
        return ("gather", keys), _gather_job([self._slab(key) for key in keys])

    def _gather_window(self, key, n):
        slab = self._slab(key) if n == 0 else self.partial[key]
        return ("gather" if n == len(self.WINDOWS) - 1 else "gather_part", [key]), _gather_job([slab], self.WINDOWS[n])

    def _to_sibling(self, keys):
        return ("to_sibling", keys), _to_sibling_job([_halves(self.grads[key], 1) for key in keys])

    def _scatter(self, keys):
        return ("scatter", keys), _scatter_job([self.chip_sums[key] for key in keys])

    def job(self, stage, l, c=0):
        parts = []
        if stage == "proj_fwd":
            parts = [self._gather([("in", l, c + 1)] if c + 1 < self.chunks(l) else [("br", l), ("out", l)])]
        elif stage == "attn_fwd" and l + 1 < DEPTH:
            parts = [self._gather_window(("in", l + 1, 0), 0)]
        elif stage == "hgrn_fwd" and l + 1 < DEPTH:
            parts = [self._gather_window(("in", l + 1, 0), 1)]
        elif stage == "merge_fwd" and l + 1 < DEPTH:
            parts = [self._gather_window(("in", l + 1, 0), 2)]
        elif stage == "attn_bwd":
            parts = [self._to_sibling([("out", l), ("br", l)])] + ([self._scatter([("in", l + 1)])] if l + 1 < DEPTH else [])
        elif stage == "hgrn_bwd":
            parts = [self._scatter([("out", l), ("br", l)])]
        elif stage == "proj_bwd":
            parts = [self._to_sibling([("in", l)])] if l else [self._scatter([("in", 0)])]
        self.pending[(stage, l, c)] = [(tag, len(job.outs)) for tag, job in parts]
        return _join_jobs([job for _, job in parts])

    def done(self, stage, l, outs, c=0):
        if outs is None:
            return
        at = 0
        for (what, keys), n_outs in self.pending[(stage, l, c)]:
            mine, at = outs[at:at + n_outs], at + n_outs
            for n, key in enumerate(keys):
                if what == "gather":
                    self.gathered[key] = mine[n].reshape((N_CHIPS,) + self._shard(key).shape)
                elif what == "gather_part":
                    self.partial[key] = mine[n]
                elif what == "to_sibling":
                    self.chip_sums[key] = _sum_cores(_halves(self.grads[key], 1), mine[n], self.where)
                else:
                    self.scattered[key] = mine[n]

    def first(self, rows):
        tag, job = self._gather([("in", 0, 0)])
        self.pending[("first", 0, 0)] = [(tag, len(job.outs))]
        cast, outs = _cast_under(self.local["in"], _join_jobs([job, _gather8_job(rows)]), "gather_first")
        self.local["in"] = cast
        self.done("first", 0, outs[:-1])
        return outs[-1]

    def took(self, key, grad):
        self.grads[key] = grad
        if key == ("in", 0):
            tag, job = self._to_sibling([key])
            self.pending[("took", 0, 0)] = [(tag, len(job.outs))]
            self.done("took", 0, _run_job(job, "to_sibling_last"))

    def finish(self, rows):
        keys = [(kind, l) for kind in ("in", "br", "out") for l in range(DEPTH)]
        halves = [_sum_chips(self.chip_sums[key], self.scattered[key], self.where) for key in keys]
        outs = _run_job(_join_jobs([_place_job(halves), _gather8_job(rows)]), "place_halves")
        return {key: w.reshape(self.grads[key].shape[1:]) for key, w in zip(keys, outs[:-1])}, outs[-1]


def _local_step(x0, target, mods, lbs, conv_all, hgrn_norm_w, ln_g, ln_b, plan):
    D = D_MODEL
    after, before = _attn_consts()
    hg_consts = _hgrn_consts()

    saved = []
    xl = x0
    for l in range(DEPTH):
        n = plan.chunks(l)
        (proj, h), got = _proj_fwd(xl, mods[l], plan.gathered[("in", l, 0)], plan.job("proj_fwd", l, 0), (0, n))
        plan.done("proj_fwd", l, got, 0)
        for c in range(1, n):
            (proj,), got = _proj_cols(h, plan.gathered[("in", l, c)], (c, n), proj, plan.job("proj_fwd", l, c))
            plan.done("proj_fwd", l, got, c)
        (o_a, tot), got = _attn_fwd(proj, after, plan.job("attn_fwd", l))
        plan.done("attn_fwd", l, got)
        (o_b, states), got = _hgrn_fwd(proj, lbs[l], hg_consts, plan.job("hgrn_fwd", l))
        plan.done("hgrn_fwd", l, got)
        ys = _branch_fwd(proj, o_a, o_b, hgrn_norm_w, conv_all, l)
        wb = jnp.concatenate(list(plan.gathered[("br", l)]), axis=-1)
        (x_next, merged, y), got = _merge_fwd(xl, mods[l], proj, ys, wb, plan.gathered[("out", l)], ln_g, ln_b, l, plan.job("merge_fwd", l))
        plan.done("merge_fwd", l, got)
        saved.append((xl, proj, h, o_a, tot, o_b, states, ys, merged, y, wb))
        xl = x_next
    dx, loss_blk = _loss_head(xl, target)

    small = [None] * DEPTH
    for l in reversed(range(DEPTH)):
        xin, proj, h, o_a, tot, o_b, states, ys, merged, y, wb = saved[l]
        dx_res, dproj, dy_a, dy_b, dy_c, dln_g, dln_b, dgate, g_br, g_out = _merge_bwd(
            dx, xin, y, merged, mods[l], proj, ys, wb, plan.gathered[("out", l)], ln_g, l)
        plan.took(("out", l), g_out)
        plan.took(("br", l), g_br)
        d_oa, d_ob, dnorm_w, dconv_w, dproj = _branch_bwd(proj, o_a, o_b, hgrn_norm_w, conv_all, dy_a, dy_b, dy_c, dproj, l)
        (dproj,), got = _attn_bwd(proj, d_oa, tot, after, before, dproj, plan.job("attn_bwd", l))
        plan.done("attn_bwd", l, got)
        (dlb, dproj), got = _hgrn_bwd(proj, lbs[l], states, d_ob, hg_consts, dproj, plan.job("hgrn_bwd", l))
        plan.done("hgrn_bwd", l, got)
        plan.took(("in", l), _grad_w_in(h, dproj))
        wgs = plan.w_in(l)
        tile = {"tk": wgs[0].shape[-1]} if len(wgs) > 1 else {}
        (dx, dshift, dscale), got = _proj_bwd(dproj, wgs, xin, mods[l], dx_res, plan.job("proj_bwd", l), **tile)
        plan.done("proj_bwd", l, got)
        small[l] = (dshift, dscale, dgate, dnorm_w, dlb, dln_g, dln_b, dconv_w)
    return loss_blk, dx, small
```

```python
import functools
import math

import numpy as np
import jax
import jax.numpy as jnp
from jax import lax
from jax.experimental import pallas as pl
from jax.experimental.pallas import tpu as pltpu

F32 = jnp.float32
BF = jnp.bfloat16
MESH = pl.DeviceIdType.MESH

DEPTH = 2
D_MODEL = 1024
WIDTH = 512
IN_COLS = 12 * WIDTH + 3 * D_MODEL
N_CHIPS = 4
N_DEV = 8
SB_BLOCK = 128
SB_HEAD_DIM = 64
HG_CHUNK = 128
HG_DIM = 128
LN_EPS = 1e-5
RMS_EPS = 1e-6
ALPHA = (2.0 * DEPTH) ** 0.25
ADAM_LR, ADAM_B1, ADAM_B2, ADAM_EPS, ADAM_WD, ADAM_STEP = 0.001, 0.9, 0.999, 1e-08, 0.01, 10
VMEM_LIMIT = 56 << 20


def _params(**kw):
    return pltpu.CompilerParams(vmem_limit_bytes=VMEM_LIMIT, **kw)


def _dot(a, b):
    return jnp.dot(a, b, preferred_element_type=F32)


def _dot_nt(a, b):
    return lax.dot_general(a, b, (((1,), (1,)), ((), ())), preferred_element_type=F32)


def _dot_tn(a, b):
    return lax.dot_general(a, b, (((0,), (0,)), ((), ())), preferred_element_type=F32)


def _sigmoid(x):
    return 1.0 / (1.0 + jnp.exp(-x))


def _silu(x):
    return x * _sigmoid(x)


def _standardize(x):
    mu = jnp.mean(x, axis=-1, keepdims=True)
    xc = x - mu
    var = jnp.mean(xc * xc, axis=-1, keepdims=True)
    rstd = lax.rsqrt(var + LN_EPS)
    return xc * rstd, rstd


def _standardize_bwd(dxs, xs, rstd):
    return rstd * (dxs - jnp.mean(dxs, axis=-1, keepdims=True) - xs * jnp.mean(dxs * xs, axis=-1, keepdims=True))


class _Job:
    def __init__(self, ins, outs, sems, make, alias=None):
        self.ins, self.outs, self.sems, self.make = list(ins), list(outs), list(sems), make
        self.alias = dict(alias or {})


def _join_jobs(jobs):
    jobs = [j for j in jobs if j is not None]
    if len(jobs) <= 1:
        return jobs[0] if jobs else None

    def make(ins, outs, sems):
        phases, i, o, s = [], 0, 0, 0
        for j in jobs:
            got = j.make(ins[i:i + len(j.ins)], outs[o:o + len(j.outs)], sems[s:s + len(j.sems)])
            i, o, s = i + len(j.ins), o + len(j.outs), s + len(j.sems)
            for n, phase in enumerate(got):
                if n == len(phases):
                    phases.append([])
                phases[n] += phase
        return phases

    alias, i, o = {}, 0, 0
    for j in jobs:
        alias.update({i + a: o + b for a, b in j.alias.items()})
        i, o = i + len(j.ins), o + len(j.outs)
    return _Job(sum([j.ins for j in jobs], []), sum([j.outs for j in jobs], []), sum([j.sems for j in jobs], []), make, alias)


def _flip(v, bit):
    return 1 - v if bit else v


def _halves(a, front=0):
    shape = a.shape
    lead = math.prod(shape[front:-2])
    return a.reshape(shape[:front] + (lead, 2, shape[-2] // 2, shape[-1]))


def _dma_sems(*shapes):
    return [pltpu.SemaphoreType.DMA(s) for s in shapes]


def _same(arrays):
    return [jax.ShapeDtypeStruct(a.shape, a.dtype) for a in arrays]


def _gather_job(slabs, window=None, handed=(None,)):
    n = len(slabs)
    cols = lambda w: slice(None) if w is None else pl.ds(*w)

    def make(ins, outs, sems):
        send1, recv1, send2, recv2 = sems
        mx, my, mc = lax.axis_index("x"), lax.axis_index("y"), lax.axis_index("c")
        fetch, pass_on = [], []
        for a in range(n):
            ours = outs[a].at[2 * mx + my, :, mc, :, cols(window)]
            for k in range(1, N_CHIPS):
                px, py = _flip(mx, k & 2), _flip(my, k & 1)
                fetch.append(pltpu.make_async_remote_copy(
                    src_ref=ours, dst_ref=ours, send_sem=send1.at[a, k - 1], recv_sem=recv1.at[a, k - 1],
                    device_id=(px, py, mc), device_id_type=MESH))
                for h, w in enumerate(handed):
                    theirs = outs[a].at[2 * px + py, :, mc, :, cols(w)]
                    (pass_on if w == window else fetch).append(pltpu.make_async_remote_copy(
                        src_ref=theirs, dst_ref=theirs, send_sem=send2.at[a, k - 1, h], recv_sem=recv2.at[a, k - 1, h],
                        device_id=(mx, my, 1 - mc), device_id_type=MESH))
        return [fetch, pass_on] if pass_on else [fetch]

    pairs = (n, N_CHIPS - 1)
    each = pairs + (max(len(handed), 1),)
    return _Job(slabs, _same(slabs), _dma_sems(pairs, pairs, each, each), make, {a: a for a in range(n)})


def _to_sibling_job(grads):
    n = len(grads)

    def make(ins, outs, sems):
        send_sems, recv_sems = sems
        mx, my, mc = lax.axis_index("x"), lax.axis_index("y"), lax.axis_index("c")
        return [[pltpu.make_async_remote_copy(
            src_ref=ins[a].at[:, :, 1 - mc], dst_ref=outs[a], send_sem=send_sems.at[a], recv_sem=recv_sems.at[a],
            device_id=(mx, my, 1 - mc), device_id_type=MESH) for a in range(n)]]

    outs = [jax.ShapeDtypeStruct(g.shape[:2] + g.shape[3:], g.dtype) for g in grads]
    return _Job(grads, outs, _dma_sems((n,), (n,)), make)


def _scatter_job(sums):
    n = len(sums)

    def make(ins, outs, sems):
        send_sems, recv_sems = sems
        mx, my, mc = lax.axis_index("x"), lax.axis_index("y"), lax.axis_index("c")
        copies = []
        for a in range(n):
            for k in range(1, N_CHIPS):
                px, py = _flip(mx, k & 2), _flip(my, k & 1)
                copies.append(pltpu.make_async_remote_copy(
                    src_ref=ins[a].at[2 * px + py], dst_ref=outs[a].at[k - 1], send_sem=send_sems.at[a, k - 1],
                    recv_sem=recv_sems.at[a, k - 1], device_id=(px, py, mc), device_id_type=MESH))
        return [copies]

    pairs = (n, N_CHIPS - 1)
    return _Job(sums, [jax.ShapeDtypeStruct((N_CHIPS - 1,) + s.shape[1:], s.dtype) for s in sums], _dma_sems(pairs, pairs), make)


def _place_job(wholes):
    n = len(wholes)

    def make(ins, outs, sems):
        send_sems, recv_sems = sems
        mx, my, mc = lax.axis_index("x"), lax.axis_index("y"), lax.axis_index("c")
        copies = []
        for a in range(n):
            here = outs[a].at[:, mc]
            copies.append(pltpu.make_async_remote_copy(src_ref=here, dst_ref=here, send_sem=send_sems.at[a], recv_sem=recv_sems.at[a],
                                                       device_id=(mx, my, 1 - mc), device_id_type=MESH))
        return [copies]

    return _Job(wholes, _same(wholes), _dma_sems((n,), (n,)), make, {a: a for a in range(n)})


def _gather8_job(x):
    def make(ins, outs, sems):
        local_sem, send_sems, recv_sems = sems
        mx, my, mc = lax.axis_index("x"), lax.axis_index("y"), lax.axis_index("c")
        here = outs[0].at[4 * mx + 2 * my + mc]
        copies = [pltpu.make_async_copy(ins[0], here, local_sem.at[0])]
        for k in range(1, N_DEV):
            peer = (_flip(mx, k & 4), _flip(my, k & 2), _flip(mc, k & 1))
            copies.append(pltpu.make_async_remote_copy(src_ref=ins[0], dst_ref=here, send_sem=send_sems.at[k - 1],
                                                       recv_sem=recv_sems.at[k - 1], device_id=peer, device_id_type=MESH))
        return [copies]

    return _Job([x], [jax.ShapeDtypeStruct((N_DEV,) + x.shape, x.dtype)], _dma_sems((1,), (N_DEV - 1,), (N_DEV - 1,)), make)


def _run_phases(phases, first=0):
    for n, phase in enumerate(phases):
        if n >= first:
            for cp in phase:
                cp.start()
        for cp in phase:
            cp.wait()


def _run_job(job, name):
    k_in, k_out = len(job.ins), len(job.outs)

    def body(*refs):
        _run_phases(job.make(refs[:k_in], refs[k_in:k_in + k_out], refs[k_in + k_out:]))

    hbm = pl.BlockSpec(memory_space=pl.ANY)
    return pl.pallas_call(body, name=name, in_specs=[hbm] * k_in, out_specs=[hbm] * k_out, out_shape=job.outs,
                          scratch_shapes=job.sems, input_output_aliases=job.alias, compiler_params=_params())(*job.ins)


def _hosted(body, job, args, *, name, grid, in_specs, out_specs, out_shape, scratch_shapes=(), semantics, aliases=None):
    in_specs, out_specs, out_shape, scratch = list(in_specs), list(out_specs), list(out_shape), list(scratch_shapes)
    aliases = dict(aliases or {})
    if job is None:
        outs = pl.pallas_call(body, name=name, grid=grid, in_specs=in_specs, out_specs=out_specs, out_shape=out_shape,
                              scratch_shapes=scratch, input_output_aliases=aliases,
                              compiler_params=_params(dimension_semantics=semantics))(*args)
        return list(outs), None
    n_in, n_out, n_scr, k_in, k_out = len(in_specs), len(out_specs), len(scratch), len(job.ins), len(job.outs)

    def wrapped(*refs):
        ins, rest = refs[:n_in], refs[n_in:]
        job_ins, rest = rest[:k_in], rest[k_in:]
        outs, rest = rest[:n_out], rest[n_out:]
        job_outs, rest = rest[:k_out], rest[k_out:]
        scr, sems = rest[:n_scr], rest[n_scr:]
        ids = [pl.program_id(a) for a in range(len(grid))]
        first = functools.reduce(jnp.logical_and, [i == 0 for i in ids])
        last = functools.reduce(jnp.logical_and, [i == g - 1 for i, g in zip(ids, grid)])

        @pl.when(first)
        def _():
            for cp in job.make(job_ins, job_outs, sems)[0]:
                cp.start()

        body(*ins, *outs, *scr)

        @pl.when(last)
        def _():
            _run_phases(job.make(job_ins, job_outs, sems), first=1)

    hbm = pl.BlockSpec(memory_space=pl.ANY)
    outs = pl.pallas_call(
        wrapped, name=name, grid=grid, in_specs=in_specs + [hbm] * k_in, out_specs=out_specs + [hbm] * k_out,
        out_shape=out_shape + job.outs, scratch_shapes=scratch + job.sems,
        input_output_aliases={**aliases, **{n_in + i: n_out + o for i, o in job.alias.items()}},
        compiler_params=_params(dimension_semantics=("arbitrary",) * len(grid)))(*args, *job.ins)
    return list(outs[:n_out]), list(outs[n_out:])


def _proj_fwd(x, mod, wg, job=None, chunk=(0, 1), tm=512):
    S, D = x.shape
    tm = min(tm, S)
    tn = wg.shape[-1]
    c, n = chunk

    rows = S // tm

    def body(x_ref, mod_ref, w_ref, proj_ref, h_ref, hs):
        mine = pl.ds(pl.multiple_of(pl.program_id(1) * tm, tm), tm)

        @pl.when(pl.program_id(0) == 0)
        def _():
            xs, _ = _standardize(x_ref[...])
            h = xs * (1.0 + mod_ref[:, D:2 * D]) + mod_ref[:, 0:D]
            hb = h.astype(BF)
            hs[mine, :] = hb
            h_ref[...] = hb

        proj_ref[...] = _dot(hs[mine, :], w_ref[...])

    once = lambda j, i: jnp.where(j == 0, i, rows - 1)
    return _hosted(
        body, job, (x, mod, wg), name="proj_fwd",
        grid=(N_CHIPS, rows),
        in_specs=[pl.BlockSpec((tm, D), lambda j, i: (once(j, i), 0)),
                  pl.BlockSpec((1, 3 * D), lambda j, i: (0, 0)),
                  pl.BlockSpec((None, D, tn), lambda j, i: (j, 0, 0))],
        out_specs=[pl.BlockSpec((tm, tn), lambda j, i: (i, j * n + c)),
                   pl.BlockSpec((tm, D), lambda j, i: (once(j, i), 0))],
        out_shape=[jax.ShapeDtypeStruct((S, IN_COLS), F32), jax.ShapeDtypeStruct((S, D), BF)],
        scratch_shapes=[pltpu.VMEM((S, D), BF)],
        semantics=("arbitrary", "arbitrary"))


def _proj_cols(h, wg, chunk, proj, job=None, tm=512):
    S, D = h.shape
    tm = min(tm, S)
    tn = wg.shape[-1]
    c, n = chunk

    def body(h_ref, w_ref, prev_ref, proj_ref):
        mine = pl.ds(pl.multiple_of(pl.program_id(1) * tm, tm), tm)
        proj_ref[...] = _dot(h_ref[mine, :], w_ref[...])

    in_specs = [pl.BlockSpec((S, D), lambda j, i: (0, 0)), pl.BlockSpec((None, D, tn), lambda j, i: (j, 0, 0)),
                pl.BlockSpec(memory_space=pl.ANY)]
    return _hosted(body, job, (h, wg, proj), name="proj_cols", grid=(N_CHIPS, S // tm), in_specs=in_specs,
                   out_specs=[pl.BlockSpec((tm, tn), lambda j, i: (i, j * n + c))],
                   out_shape=[jax.ShapeDtypeStruct(proj.shape, proj.dtype)],
                   semantics=("arbitrary", "arbitrary"), aliases={2: 0})


SB_ROWS = 256
SB_KEYS = 256


def _attn_consts():
    j = np.arange(SB_KEYS)[:, None]
    s = np.arange(SB_KEYS)[None, :]
    from_here = np.concatenate([(j >= s), (j >= s)], axis=0).astype(np.float32)
    return jnp.asarray(from_here, BF), jnp.asarray((j <= s).astype(np.float32), BF)


def _hi_lo(x):
    hi = lax.bitcast_convert_type(lax.bitcast_convert_type(x, jnp.uint32) & jnp.uint32(0xFFFF0000), F32)
    return hi.astype(BF), (x - hi).astype(BF)


def _sums_r(x, t2):
    hi, lo = _hi_lo(x)
    return _dot(jnp.concatenate([hi, lo], axis=1), t2)


def _all_lanes(col, lanes):
    return jnp.broadcast_to(col, (col.shape[0], lanes))


def _attn_rows(ref, r0, rows, lanes, head0, scale=None):
    v = ref[pl.ds(r0, rows), lanes]
    if scale is not None:
        v = v * scale
    return jnp.concatenate([jnp.where(head0, v, 0.0), jnp.where(head0, 0.0, v)], axis=0).astype(BF)


SB_PAIRS_FWD = 4
SB_PAIRS_BWD = 2


def _attn_specs(S, n_pairs):
    return lambda base: pl.BlockSpec((S, n_pairs * SB_BLOCK), lambda p, base=base: (0, base // n_pairs + p))


def _attn_scores(q2n, k_ref, lanes, kj, t2, from_here_ref, masked):
    c0 = pl.multiple_of(kj * SB_KEYS, SB_KEYS)
    kb = k_ref[pl.ds(c0, SB_KEYS), lanes].astype(BF)
    zn = _dot_nt(q2n, kb)
    lsb = jnp.minimum(zn, 0.0) - jnp.log(1.0 + jnp.exp(-jnp.abs(zn)))
    valid = None
    if masked:
        valid = (lax.broadcasted_iota(jnp.int32, zn.shape, 1) + kj * SB_KEYS) < t2
        lsb = jnp.where(valid, lsb, 0.0)
    return c0, kb, zn, valid, lsb, _sums_r(lsb, from_here_ref[...])


def _attn_fwd(proj, from_here, job=None):
    S = proj.shape[0]
    TQ = SB_ROWS
    assert S % TQ == 0 and SB_KEYS == TQ
    scale = SB_HEAD_DIM ** -0.5
    n_pairs = SB_PAIRS_FWD
    pairs = range(n_pairs)
    lanes = [pl.ds(p * SB_BLOCK, SB_BLOCK) for p in pairs]

    def body(q_ref, k_ref, v_ref, from_here_ref, o_ref, tot_ref, run, acc):
        head0 = lax.broadcasted_iota(jnp.int32, (1, 2 * SB_HEAD_DIM), 1) < SB_HEAD_DIM

        def qloop(qi, _):
            r0 = pl.multiple_of(qi * TQ, TQ)
            q2n = [_attn_rows(q_ref, r0, TQ, lanes[p], head0, -scale) for p in pairs]
            trow = lax.broadcasted_iota(jnp.int32, (TQ, SB_KEYS), 0) + qi * TQ
            t2 = jnp.concatenate([trow, trow], axis=0)
            run[...] = jnp.zeros_like(run)
            acc[...] = jnp.zeros_like(acc)

            def step(kj, masked):
                got = [_attn_scores(q2n[p], k_ref, lanes[p], kj, t2, from_here_ref, masked) for p in pairs]
                for p in pairs:
                    c0, _, zn, valid, _, sums = got[p]
                    r = run[p]
                    e = sums - zn + jnp.concatenate([r, r], axis=1)
                    if masked:
                        e = jnp.where(valid, e, -jnp.inf)
                    acc[p] += _dot(jnp.exp(e).astype(BF), v_ref[pl.ds(c0, SB_KEYS), lanes[p]].astype(BF))
                    run[p] = r + _all_lanes(sums[:, 0:1], SB_BLOCK)

            @pl.when(qi % 2 == 0)
            def _():
                step(qi, True)

            @pl.when(qi % 2 == 1)
            def _():
                step(qi, True)
                step(qi - 1, False)

            first = qi - 1 - qi % 2

            def below(n, _):
                step(first - 2 * n, False)
                step(first - 1 - 2 * n, False)
                return 0

            lax.fori_loop(0, qi // 2, below, 0)
            for p in pairs:
                o_ref[pl.ds(r0, TQ), lanes[p]] = jnp.where(head0, acc[p, 0:TQ, :], acc[p, TQ:2 * TQ, :])
                tot_ref[p, 0, pl.ds(r0, TQ), :] = run[p, 0:TQ, :]
                tot_ref[p, 1, pl.ds(r0, TQ), :] = run[p, TQ:2 * TQ, :]
            return 0

        lax.fori_loop(0, S // TQ, qloop, 0)

    col = _attn_specs(S, n_pairs)
    state = pltpu.VMEM((n_pairs, 2 * TQ, SB_BLOCK), F32)
    return _hosted(
        body, job, (proj, proj, proj, from_here), name="attn_fwd",
        grid=(WIDTH // (n_pairs * SB_BLOCK),),
        in_specs=[col(0), col(4), col(8), pl.BlockSpec(from_here.shape, lambda p: (0, 0))],
        out_specs=[col(0), pl.BlockSpec((n_pairs, 2, S, 128), lambda p: (p, 0, 0, 0))],
        out_shape=[jax.ShapeDtypeStruct((S, WIDTH), F32), jax.ShapeDtypeStruct((WIDTH // 128, 2, S, 128), F32)],
        scratch_shapes=[state, state],
        semantics=("parallel",))


def _attn_bwd(proj, d_o, tot, from_here, up_to, dproj, job=None):
    S = proj.shape[0]
    TQ = SB_ROWS
    assert S % TQ == 0 and SB_KEYS == TQ
    scale = SB_HEAD_DIM ** -0.5
    n_pairs = SB_PAIRS_BWD
    pairs = range(n_pairs)
    lanes = [pl.ds(p * SB_BLOCK, SB_BLOCK) for p in pairs]

    def body(q_ref, k_ref, v_ref, do_ref, tot_ref, from_here_ref, up_to_ref, _, dproj_ref, pre, cum, dq_acc, dk_acc, dv_acc, stage, sems):
        dq_ref, dk_ref, dv_ref = stage.at[0], stage.at[1], stage.at[2]
        head0 = lax.broadcasted_iota(jnp.int32, (1, 2 * SB_HEAD_DIM), 1) < SB_HEAD_DIM
        dk_acc[...] = jnp.zeros_like(dk_acc)
        dv_acc[...] = jnp.zeros_like(dv_acc)

        def qloop(qi, _):
            r0 = pl.multiple_of(qi * TQ, TQ)
            q2n = [_attn_rows(q_ref, r0, TQ, lanes[p], head0, -scale) for p in pairs]
            do2 = [_attn_rows(do_ref, r0, TQ, lanes[p], head0) for p in pairs]
            trow = lax.broadcasted_iota(jnp.int32, (TQ, SB_KEYS), 0) + qi * TQ
            t2 = jnp.concatenate([trow, trow], axis=0)
            for p in pairs:
                pre[p, 0:TQ, :] = tot_ref[p, 0, pl.ds(r0, TQ), :]
                pre[p, TQ:2 * TQ, :] = tot_ref[p, 1, pl.ds(r0, TQ), :]
            cum[...] = jnp.zeros_like(cum)
            dq_acc[...] = jnp.zeros_like(dq_acc)

            def step(kj, masked):
                got = [_attn_scores(q2n[p], k_ref, lanes[p], kj, t2, from_here_ref, masked) for p in pairs]
                das = [_dot_nt(do2[p], v_ref[pl.ds(got[p][0], SB_KEYS), lanes[p]].astype(BF)) for p in pairs]
                for p in pairs:
                    c0, kb, zn, valid, lsb, sums = got[p]
                    later = pre[p] - _all_lanes(sums[:, 0:1], SB_BLOCK)
                    pre[p] = later
                    e = sums - zn + jnp.concatenate([later, later], axis=1)
                    sig = jnp.exp(lsb - zn)
                    if masked:
                        e = jnp.where(valid, e, -jnp.inf)
                        sig = jnp.where(valid, sig, 0.0)
                    a = jnp.exp(e)
                    w = das[p] * a
                    upto = _dot(w.astype(BF), up_to_ref[...])
                    c = cum[p]
                    dz = w - sig * (upto + jnp.concatenate([c, c], axis=1))
                    cum[p] = c + _all_lanes(upto[:, SB_KEYS - 1:SB_KEYS], SB_BLOCK)
                    dzb = dz.astype(BF)
                    dq_acc[p] += _dot(dzb, kb)
                    dk_acc[pl.ds(c0, SB_KEYS), lanes[p]] += _dot_tn(dzb, q2n[p])
                    dv_acc[pl.ds(c0, SB_KEYS), lanes[p]] += _dot_tn(a.astype(BF), do2[p])

            def below(n, _):
                step(2 * n, False)
                step(2 * n + 1, False)
                return 0

            lax.fori_loop(0, qi // 2, below, 0)

            @pl.when(qi % 2 == 1)
            def _():
                step(qi - 1, False)
                step(qi, True)

            @pl.when(qi % 2 == 0)
            def _():
                step(qi, True)
            for p in pairs:
                dq_ref[pl.ds(r0, TQ), lanes[p]] = (jnp.where(head0, dq_acc[p, 0:TQ, :], dq_acc[p, TQ:2 * TQ, :]) * scale).astype(BF)
            return 0

        lax.fori_loop(0, S // TQ, qloop, 0)
        dk_ref[...] = (-dk_acc[...]).astype(BF)
        dv_ref[...] = dv_acc[...].astype(BF)
        lane0 = pl.program_id(0) * (n_pairs * SB_BLOCK)
        _put_columns([dq_ref, dk_ref, dv_ref], dproj_ref, [WIDTH * n + lane0 for n in range(3)], sems)

    col = _attn_specs(S, n_pairs)
    whole = lambda a: pl.BlockSpec(a.shape, lambda p: (0, 0))
    hbm = pl.BlockSpec(memory_space=pl.ANY)
    state = pltpu.VMEM((n_pairs, 2 * TQ, SB_BLOCK), F32)
    grads = pltpu.VMEM((S, n_pairs * SB_BLOCK), F32)
    return _hosted(
        body, job, (proj, proj, proj, d_o, tot, from_here, up_to, dproj), name="attn_bwd",
        grid=(WIDTH // (n_pairs * SB_BLOCK),),
        in_specs=[col(0), col(4), col(8), col(0), pl.BlockSpec((n_pairs, 2, S, 128), lambda p: (p, 0, 0, 0)), whole(from_here),
                  whole(up_to), hbm],
        out_specs=[hbm],
        out_shape=[jax.ShapeDtypeStruct(dproj.shape, dproj.dtype)],
        scratch_shapes=[state, state, state, grads, grads, pltpu.VMEM((3, S, n_pairs * SB_BLOCK), BF), pltpu.SemaphoreType.DMA((3,))],
        semantics=("arbitrary",), aliases={7: 0})


HG_LEVELS = tuple(HG_CHUNK >> n for n in range(1, HG_CHUNK.bit_length()))


def _hgrn_consts():
    C = HG_CHUNK
    t = np.arange(C)[:, None]
    s = np.arange(C)[None, :]
    rows = [(s <= t), (s > t)]
    masks = [(t == s)]
    for m in HG_LEVELS:
        two = 2 * m
        mid = (t // two) * two + m
        right = (t % two) >= m
        rows.append((right & (s >= mid) & (s <= t)) | ((~right) & (s > t) & (s <= mid - 1)))
        masks.append(((t // two) == (s // two)) & right & ((s % two) < m))
    tri = np.concatenate(rows, axis=0).astype(np.float32)
    twice = lambda a: jnp.asarray(np.concatenate([a, a], axis=1), BF)
    return (twice(tri), twice(tri.T), jnp.asarray(np.stack(masks).astype(np.float32), F32))


HG_SUM_BLOCKS = 2 + len(HG_LEVELS)


def _split_rows(g):
    hi = g.astype(BF)
    return jnp.concatenate([hi, (g - hi.astype(F32)).astype(BF)], axis=0)


def _hgrn_sum_blocks(e):
    C = HG_CHUNK
    blocks = tuple(e[n * C:(n + 1) * C] for n in range(HG_SUM_BLOCKS))
    return blocks + (jnp.broadcast_to(e[C - 1:C], (HG_DIM, e.shape[1])),)


@jax.custom_vjp
def _hgrn_sums(tri, tri_t, g):
    return _hgrn_sum_blocks(_dot(tri, _split_rows(g)))


def _hgrn_sums_fwd(tri, tri_t, g):
    return _hgrn_sums(tri, tri_t, g), (tri, tri_t)


def _hgrn_sums_bwd(res, ds):
    tri, tri_t = res
    C = HG_CHUNK
    last = lax.broadcasted_iota(jnp.int32, (C, 1), 0) == C - 1
    prefix = ds[0] + jnp.where(last, jnp.sum(ds[-1], axis=0, keepdims=True), 0.0)
    d = jnp.concatenate((prefix,) + tuple(ds[1:-1]), axis=0)
    return jnp.zeros_like(tri), jnp.zeros_like(tri_t), _dot(tri_t[:, :d.shape[0]], d.astype(BF))


_hgrn_sums.defvjp(_hgrn_sums_fwd, _hgrn_sums_bwd)


def _bf_dot(a, b):
    return _dot(a.astype(BF), b.astype(BF))


def _bf_dot_nt(a, b):
    return _dot_nt(a.astype(BF), b.astype(BF))


def _bf_dot_tn(a, b):
    return _dot_tn(a.astype(BF), b.astype(BF))


@jax.custom_vjp
def _mm(a, b):
    return _bf_dot(a, b)


_mm.defvjp(lambda a, b: (_bf_dot(a, b), (a, b)), lambda r, ct: (_bf_dot_nt(ct, r[1]), _bf_dot_tn(r[0], ct)))


@jax.custom_vjp
def _mm_nt(a, b):
    return _bf_dot_nt(a, b)


_mm_nt.defvjp(lambda a, b: (_bf_dot_nt(a, b), (a, b)), lambda r, ct: (_bf_dot(ct, r[1]), _bf_dot_tn(ct, r[0])))


@jax.custom_vjp
def _mm_tn(a, b):
    return _bf_dot_tn(a, b)


_mm_tn.defvjp(lambda a, b: (_bf_dot_tn(a, b), (a, b)), lambda r, ct: (_bf_dot_nt(r[1], ct), _bf_dot(r[0], ct)))


def _hgrn_gates(tri, tri_t, qraw, fpre, lb):
    q = _silu(qraw)
    f = lb + (1.0 - lb) * _sigmoid(fpre)
    return q, 1.0 - f, _hgrn_sums(tri, tri_t, jnp.log(f))


def _hgrn_mix(masks, q, k, e, v, st):
    prefix, suffix, whole = e[0], e[1], e[-1]
    scores = masks[0] * _mm_nt(q, k)
    for n in range(len(HG_LEVELS)):
        decay = jnp.exp(e[2 + n])
        scores = scores + masks[n + 1] * _mm_nt(q * decay, k * decay)
    o = _mm_nt(q * jnp.exp(prefix), st) + _mm(scores, v)
    st_new = st * jnp.exp(whole) + _mm_tn(v, k * jnp.exp(suffix))
    return o, st_new


def _hgrn_chunk(tri, tri_t, masks, qraw, fpre, v, st, lb):
    q, k, e = _hgrn_gates(tri, tri_t, qraw, fpre, lb)
    return _hgrn_mix(masks, q, k, e, v, st)


HG_HEADS_PER_STEP = 4
HG_LANES = HG_HEADS_PER_STEP * HG_DIM


def _hgrn_specs(S, consts):
    col = lambda base: pl.BlockSpec((S, HG_LANES), lambda p, base=base: (0, base // HG_HEADS_PER_STEP + p))
    whole = [pl.BlockSpec(a.shape, lambda p, n=a.ndim: (0,) * n) for a in consts]
    return col, whole


def _hgrn_fwd(proj, lbs, consts, job=None):
    S = proj.shape[0]
    nc = S // HG_CHUNK
    heads = range(HG_HEADS_PER_STEP)

    def body(q_ref, f_ref, i_ref, lb_ref, tri_ref, trit_ref, mask_ref, o_ref, st_ref):
        tri, tri_t = tri_ref[...], trit_ref[...]
        masks = [mask_ref[n] for n in range(len(HG_LEVELS) + 1)]

        def chunk(ci, sts):
            r0 = pl.multiple_of(ci * HG_CHUNK, HG_CHUNK)
            rows = pl.ds(r0, HG_CHUNK)
            new = []
            lane = [pl.ds(hd * HG_DIM, HG_DIM) for hd in heads]
            gates = [_hgrn_gates(tri, tri_t, q_ref[rows, lane[hd]], f_ref[rows, lane[hd]], lb_ref[hd]) for hd in heads]
            for hd in heads:
                st_ref[hd, ci] = sts[hd]
                o, st_new = _hgrn_mix(masks, *gates[hd], i_ref[rows, lane[hd]], sts[hd])
                o_ref[rows, lane[hd]] = o
                new.append(st_new)
            return tuple(new)

        lax.fori_loop(0, nc, chunk, tuple(jnp.zeros((HG_DIM, HG_DIM), F32) for _ in heads))

    col, whole = _hgrn_specs(S, consts)
    return _hosted(
        body, job, (proj, proj, proj, lbs, *consts), name="hgrn_fwd",
        grid=(WIDTH // HG_LANES,),
        in_specs=[col(16), col(20), col(24), pl.BlockSpec((HG_HEADS_PER_STEP, 1, 128), lambda p: (p, 0, 0))] + whole,
        out_specs=[col(0), pl.BlockSpec((HG_HEADS_PER_STEP, nc, HG_DIM, HG_DIM), lambda p: (p, 0, 0, 0))],
        out_shape=[jax.ShapeDtypeStruct((S, WIDTH), F32), jax.ShapeDtypeStruct((WIDTH // 128, nc, HG_DIM, HG_DIM), F32)],
        semantics=("parallel",))


def _hgrn_bwd(proj, lbs, states, d_o, consts, dproj, job=None):
    S = proj.shape[0]
    nc = S // HG_CHUNK

    def body(q_ref, f_ref, i_ref, lb_ref, st_ref, do_ref, tri_ref, trit_ref, mask_ref, _, dlb_ref, dproj_ref, stage, sems):
        dq_ref, df_ref, di_ref = stage.at[0], stage.at[1], stage.at[2]
        masks = [mask_ref[n] for n in range(len(HG_LEVELS) + 1)]
        fn = functools.partial(_hgrn_chunk, tri_ref[...], trit_ref[...], masks)
        heads = range(HG_HEADS_PER_STEP)

        def chunk(n, carry):
            ci = nc - 1 - n
            r0 = pl.multiple_of(ci * HG_CHUNK, HG_CHUNK)
            rows = pl.ds(r0, HG_CHUNK)
            new = []
            lane = [pl.ds(hd * HG_DIM, HG_DIM) for hd in heads]
            pulls = [jax.vjp(fn, q_ref[rows, lane[hd]], f_ref[rows, lane[hd]], i_ref[rows, lane[hd]], st_ref[hd, ci], lb_ref[hd])[1]
                     for hd in heads]
            for hd in heads:
                d_st, dlb = carry[hd]
                lanes = lane[hd]
                dq, df, di, d_prev, dl = pulls[hd]((do_ref[rows, lanes], d_st))
                dq_ref[rows, lanes] = dq.astype(BF)
                df_ref[rows, lanes] = df.astype(BF)
                di_ref[rows, lanes] = di.astype(BF)
                new.append((d_prev, dlb + dl))
            return tuple(new)

        zero = (jnp.zeros((HG_DIM, HG_DIM), F32), jnp.zeros((1, HG_DIM), F32))
        done = lax.fori_loop(0, nc, chunk, tuple(zero for _ in heads))
        for hd in heads:
            dlb_ref[hd] = done[hd][1]
        lane0 = pl.program_id(0) * HG_LANES
        _put_columns([dq_ref, df_ref, di_ref], dproj_ref, [WIDTH * n + lane0 for n in (4, 5, 6)], sems)

    col, whole = _hgrn_specs(S, consts)
    head = pl.BlockSpec((HG_HEADS_PER_STEP, 1, 128), lambda p: (p, 0, 0))
    hbm = pl.BlockSpec(memory_space=pl.ANY)
    n_in = 6 + len(consts)
    return _hosted(
        body, job, (proj, proj, proj, lbs, states, d_o, *consts, dproj), name="hgrn_bwd",
        grid=(WIDTH // HG_LANES,),
        in_specs=[col(16), col(20), col(24), head, pl.BlockSpec((HG_HEADS_PER_STEP, nc, HG_DIM, HG_DIM), lambda p: (p, 0, 0, 0)), col(0)]
        + whole + [hbm],
        out_specs=[head, hbm],
        out_shape=[jax.ShapeDtypeStruct((WIDTH // 128, 1, 128), F32), jax.ShapeDtypeStruct(dproj.shape, dproj.dtype)],
        scratch_shapes=[pltpu.VMEM((3, S, HG_LANES), BF), pltpu.SemaphoreType.DMA((3,))],
        semantics=("arbitrary",), aliases={n_in: 1})


def _shift_down(x, n):
    rows = lax.broadcasted_iota(jnp.int32, x.shape, 0)
    return jnp.where(rows >= n, pltpu.roll(x, n, 0), 0.0)


def _shift_up(x, n):
    S = x.shape[0]
    rows = lax.broadcasted_iota(jnp.int32, x.shape, 0)
    return jnp.where(rows < S - n, pltpu.roll(x, S - n, 0), 0.0)


def _branch_fwd(proj, o_a, o_b, norm_w, conv_w, layer):
    S = proj.shape[0]

    def body(oa_ref, za_ref, ob_ref, zb_ref, nw_ref, pre_ref, post_ref, u_ref, zc_ref, cw_ref, ya_ref, yb_ref, yc_ref):
        ya_ref[...] = (oa_ref[...] * _silu(za_ref[...])).astype(BF)
        ob = ob_ref[...]
        rn = lax.rsqrt(jnp.mean(ob * ob, axis=-1, keepdims=True) + RMS_EPS)
        yb_ref[...] = (ob * rn * nw_ref[layer:layer + 1, :] * _silu(zb_ref[...])).astype(BF)
        pu = pre_ref[...] * u_ref[...]
        conv = cw_ref[2:3, :] * pu + cw_ref[1:2, :] * _shift_down(pu, 1) + cw_ref[0:1, :] * _shift_down(pu, 2)
        yc_ref[...] = (post_ref[...] * conv * _silu(zc_ref[...])).astype(BF)

    col = lambda base: pl.BlockSpec((S, 128), lambda p, base=base: (0, base + p))
    out = jax.ShapeDtypeStruct((S, WIDTH), BF)
    return pl.pallas_call(
        body, name="branch_fwd",
        grid=(WIDTH // 128,),
        in_specs=[col(0), col(12), col(0), col(28), pl.BlockSpec(norm_w.shape, lambda p: (0, 0)),
                  col(32), col(36), col(40), col(44), pl.BlockSpec((None, None, 3, 128), lambda p: (p, layer, 0, 0))],
        out_specs=[col(0), col(0), col(0)],
        out_shape=[out, out, out],
        compiler_params=_params(dimension_semantics=("parallel",)),
    )(o_a, proj, o_b, proj, norm_w, proj, proj, proj, proj, conv_w)


def _put_columns(tiles, dproj_ref, firsts, sems):
    copies = [pltpu.make_async_copy(t, dproj_ref.at[:, pl.ds(pl.multiple_of(c, 128), t.shape[1])], sems.at[n])
              for n, (t, c) in enumerate(zip(tiles, firsts))]
    for cp in copies:
        cp.start()
    for cp in copies:
        cp.wait()


def _branch_bwd(proj, o_a, o_b, norm_w, conv_w, dy_a, dy_b, dy_c, dproj, layer):
    S = proj.shape[0]
    firsts = [WIDTH * n for n in (3, 7, 8, 9, 10, 11)]

    def dsilu(z):
        s = _sigmoid(z)
        return s * z, s * (1.0 + z * (1.0 - s))

    def body(oa_ref, za_ref, ob_ref, zb_ref, nw_ref, pre_ref, post_ref, u_ref, zc_ref, cw_ref, dya_ref, dyb_ref, dyc_ref, _,
             doa_ref, dob_ref, dnw_ref, dcw_ref, dproj_ref, stage, sems):
        dza_ref, dzb_ref, dpre_ref, dpost_ref, du_ref, dzc_ref = [stage.at[n] for n in range(6)]
        dya = dya_ref[...]
        sa, dsa = dsilu(za_ref[...])
        doa_ref[...] = dya * sa
        dza_ref[...] = (dya * oa_ref[...] * dsa).astype(BF)

        dyb = dyb_ref[...]
        ob = ob_ref[...]
        nw = nw_ref[layer:layer + 1, :]
        sb, dsb = dsilu(zb_ref[...])
        rn = lax.rsqrt(jnp.mean(ob * ob, axis=-1, keepdims=True) + RMS_EPS)
        on = ob * rn
        dzb_ref[...] = (dyb * on * nw * dsb).astype(BF)
        don_w = dyb * sb
        dnw_ref[...] = jnp.sum(don_w * on, axis=0, keepdims=True)
        don = don_w * nw
        dob_ref[...] = rn * (don - on * jnp.mean(don * on, axis=-1, keepdims=True))

        dyc = dyc_ref[...]
        pre, post, u = pre_ref[...], post_ref[...], u_ref[...]
        sc, dsc = dsilu(zc_ref[...])
        pu = pre * u
        pu1, pu2 = _shift_down(pu, 1), _shift_down(pu, 2)
        conv = cw_ref[2:3, :] * pu + cw_ref[1:2, :] * pu1 + cw_ref[0:1, :] * pu2
        dzc_ref[...] = (dyc * post * conv * dsc).astype(BF)
        dpost_ref[...] = (dyc * conv * sc).astype(BF)
        dconv = dyc * post * sc
        dcw_ref[0:1, :] = jnp.sum(dconv * pu2, axis=0, keepdims=True)
        dcw_ref[1:2, :] = jnp.sum(dconv * pu1, axis=0, keepdims=True)
        dcw_ref[2:3, :] = jnp.sum(dconv * pu, axis=0, keepdims=True)
        dpu = cw_ref[2:3, :] * dconv + cw_ref[1:2, :] * _shift_up(dconv, 1) + cw_ref[0:1, :] * _shift_up(dconv, 2)
        dpre_ref[...] = (dpu * u).astype(BF)
        du_ref[...] = (dpu * pre).astype(BF)
        lane0 = pl.program_id(0) * 128
        _put_columns([stage.at[n] for n in range(6)], dproj_ref, [c + lane0 for c in firsts], sems)

    col = lambda base: pl.BlockSpec((S, 128), lambda p, base=base: (0, base + p))
    f32 = jax.ShapeDtypeStruct((S, WIDTH), F32)
    hbm = pl.BlockSpec(memory_space=pl.ANY)
    return pl.pallas_call(
        body, name="branch_bwd",
        grid=(WIDTH // 128,),
        in_specs=[col(0), col(12), col(0), col(28), pl.BlockSpec(norm_w.shape, lambda p: (0, 0)),
                  col(32), col(36), col(40), col(44), pl.BlockSpec((None, None, 3, 128), lambda p: (p, layer, 0, 0)),
                  col(0), col(0), col(0), hbm],
        out_specs=[col(0), col(0), pl.BlockSpec((None, 1, 128), lambda p: (p, 0, 0)), pl.BlockSpec((None, 3, 128), lambda p: (p, 0, 0)), hbm],
        out_shape=[f32, f32, jax.ShapeDtypeStruct((WIDTH // 128, 1, 128), F32), jax.ShapeDtypeStruct((WIDTH // 128, 3, 128), F32),
                   jax.ShapeDtypeStruct(dproj.shape, dproj.dtype)],
        scratch_shapes=[pltpu.VMEM((6, S, 128), BF), pltpu.SemaphoreType.DMA((6,))],
        input_output_aliases={13: 4},
        compiler_params=_params(dimension_semantics=("arbitrary",)),
    )(o_a, proj, o_b, proj, norm_w, proj, proj, proj, proj, conv_w, dy_a, dy_b, dy_c, dproj)


def _branch_proj(y_refs, wb_ref):
    return [_dot(y_refs[i][...], wb_ref[i]) for i in range(3)]


def _merge_fwd(x, mod, proj, ys, wb, wo, ln_g, ln_b, layer, job=None, tm=512):
    S, D = x.shape
    tm = min(tm, S)

    def body(x_ref, mod_ref, ga_ref, gb_ref, gc_ref, ya_ref, yb_ref, yc_ref, wb_ref, wo_ref, g_ref, b_ref, xo_ref, mg_ref, y_ref):
        ps = _branch_proj((ya_ref, yb_ref, yc_ref), wb_ref)
        merged = _sigmoid(ga_ref[...]) * ps[0] + _sigmoid(gb_ref[...]) * ps[1] + _sigmoid(gc_ref[...]) * ps[2]
        mb = merged.astype(BF)
        mg_ref[...] = mb
        y = _dot(mb, wo_ref[...].reshape(D, D))
        y_ref[...] = y
        r = ALPHA * x_ref[...] + (1.0 + mod_ref[:, 2 * D:3 * D]) * y
        xn, _ = _standardize(r)
        xo_ref[...] = xn * g_ref[layer:layer + 1, :] + b_ref[layer:layer + 1, :]

    row = lambda w, c=0: pl.BlockSpec((tm, w), lambda i, c=c: (i, c))
    whole = lambda a: pl.BlockSpec(a.shape, lambda i, n=a.ndim: (0,) * n)
    return _hosted(
        body, job, (x, mod, proj, proj, proj, *ys, wb, wo, ln_g, ln_b), name="merge_fwd",
        grid=(S // tm,),
        in_specs=[row(D), whole(mod), row(D, 6), row(D, 7), row(D, 8), row(WIDTH), row(WIDTH), row(WIDTH), whole(wb),
                  whole(wo), whole(ln_g), whole(ln_b)],
        out_specs=[row(D), row(D), row(D)],
        out_shape=[jax.ShapeDtypeStruct((S, D), F32), jax.ShapeDtypeStruct((S, D), BF), jax.ShapeDtypeStruct((S, D), F32)],
        semantics=("parallel",))


def _merge_bwd(dxo, x, y, merged, mod, proj, ys, wb, wo, ln_g, layer, tm=256):
    S, D = x.shape
    tm = min(tm, S)
    steps = S // tm
    quarter = D // N_CHIPS

    def body(dxo_ref, x_ref, y_ref, mg_ref, mod_ref, ga_ref, gb_ref, gc_ref, ya_ref, yb_ref, yc_ref, wb_ref, wo_ref, g_ref,
             dxr_ref, dg_ref, dya_ref, dyb_ref, dyc_ref, dlg_ref, dlb_ref, dgt_ref, gbr_ref, gout_ref, acc_br, acc_out):
        @pl.when(pl.program_id(0) == 0)
        def _():
            dlg_ref[...] = jnp.zeros_like(dlg_ref)
            dlb_ref[...] = jnp.zeros_like(dlb_ref)
            dgt_ref[...] = jnp.zeros_like(dgt_ref)
            acc_br[...] = jnp.zeros_like(acc_br)
            acc_out[...] = jnp.zeros_like(acc_out)

        gate1 = 1.0 + mod_ref[:, 2 * D:3 * D]
        yv = y_ref[...]
        xn, rstd = _standardize(ALPHA * x_ref[...] + gate1 * yv)
        dxo = dxo_ref[...]
        dlg_ref[...] += jnp.sum(dxo * xn, axis=0, keepdims=True)
        dlb_ref[...] += jnp.sum(dxo, axis=0, keepdims=True)
        dr = _standardize_bwd(dxo * g_ref[layer:layer + 1, :], xn, rstd)
        dxr_ref[...] = ALPHA * dr
        dgt_ref[...] += jnp.sum(dr * yv, axis=0, keepdims=True)
        dyb = (gate1 * dr).astype(BF)
        acc_out[...] += _dot_tn(mg_ref[...], dyb)
        dmerged = _dot_nt(dyb, wo_ref[...].reshape(D, D))
        y_refs = (ya_ref, yb_ref, yc_ref)
        ps = _branch_proj(y_refs, wb_ref)
        for i, (gate_ref, out_ref) in enumerate(((ga_ref, dya_ref), (gb_ref, dyb_ref), (gc_ref, dyc_ref))):
            sg = _sigmoid(gate_ref[...])
            dg_ref[:, i * D:(i + 1) * D] = (dmerged * ps[i] * sg * (1.0 - sg)).astype(BF)
            dp = (dmerged * sg).astype(BF)
            acc_br[i] += _dot_tn(y_refs[i][...], dp)
            out_ref[...] = _dot_nt(dp, wb_ref[i])

        @pl.when(pl.program_id(0) == steps - 1)
        def _():
            for j in range(N_CHIPS):
                gout_ref[j] = acc_out[j * quarter:(j + 1) * quarter, :].astype(BF)
                for i in range(3):
                    gbr_ref[j, i] = acc_br[i, :, j * quarter:(j + 1) * quarter].astype(BF)

    row = lambda w, c=0: pl.BlockSpec((tm, w), lambda i, c=c: (i, c))
    whole = lambda a: pl.BlockSpec(a.shape, lambda i, n=len(a.shape): (0,) * n)
    vec = pl.BlockSpec((1, D), lambda i: (0, 0))
    sd = jax.ShapeDtypeStruct
    g_br, g_out = sd((N_CHIPS, 3, WIDTH, quarter), BF), sd((N_CHIPS, quarter, D), BF)
    return pl.pallas_call(
        body, name="merge_bwd",
        grid=(steps,),
        in_specs=[row(D), row(D), row(D), row(D), whole(mod), row(D, 6), row(D, 7), row(D, 8), row(WIDTH), row(WIDTH), row(WIDTH),
                  whole(wb), whole(wo), whole(ln_g)],
        out_specs=[row(D), row(3 * D, IN_COLS // (3 * D) - 1), row(WIDTH), row(WIDTH), row(WIDTH), vec, vec, vec, whole(g_br), whole(g_out)],
        out_shape=[sd((S, D), F32), sd((S, IN_COLS), BF), sd((S, WIDTH), F32), sd((S, WIDTH), F32), sd((S, WIDTH), F32),
                   sd((1, D), F32), sd((1, D), F32), sd((1, D), F32), g_br, g_out],
        scratch_shapes=[pltpu.VMEM((3, WIDTH, D), F32), pltpu.VMEM((D, D), F32)],
        compiler_params=_params(dimension_semantics=("arbitrary",)),
    )(dxo, x, y, merged, mod, proj, proj, proj, *ys, wb, wo, ln_g)


def _loss_head(x, target, tm=512):
    S, D = x.shape
    tm = min(tm, S)

    def body(x_ref, t_ref, dx_ref, loss_ref):
        @pl.when(pl.program_id(0) == 0)
        def _():
            loss_ref[...] = jnp.zeros_like(loss_ref)

        err = x_ref[...] - t_ref[...]
        dx_ref[...] = err * (1.0 / D)
        loss_ref[...] += 0.5 * jnp.sum(jnp.mean(err * err, axis=-1, keepdims=True))

    row = pl.BlockSpec((tm, D), lambda i: (i, 0))
    return pl.pallas_call(
        body, name="loss_head",
        grid=(S // tm,),
        in_specs=[row, row],
        out_specs=[row, pl.BlockSpec((8, 128), lambda i: (0, 0))],
        out_shape=[jax.ShapeDtypeStruct((S, D), F32), jax.ShapeDtypeStruct((8, 128), F32)],
        compiler_params=_params(dimension_semantics=("arbitrary",)),
    )(x, target)


def _proj_bwd(dproj, wgs, x, mod, dx_res, job=None, tm=512, tk=768):
    S, D = x.shape
    tm = min(tm, S)
    n = len(wgs)
    w_args = list(wgs)
    if n == 1:
        per = wgs[0].shape[-1] // tk
        w_specs = [pl.BlockSpec((None, D, tk), lambda k, i: (k // per, 0, k % per))]
    else:
        assert wgs[0].shape[-1] == tk
        w_specs = [pl.BlockSpec((None, D, tk), lambda k, i, c=c: (jnp.minimum((k + n - 1 - c) // n, N_CHIPS - 1), 0, 0))
                   for c in range(n)]
    nk = IN_COLS // tk

    def body(dp_ref, *rest):
        w_refs, (x_ref, mod_ref, dxr_ref, dx_ref, dsh_ref, dsc_ref, acc) = rest[:n], rest[n:]
        k, i = pl.program_id(0), pl.program_id(1)
        mine = pl.ds(pl.multiple_of(i * tm, tm), tm)

        @pl.when((i == 0) & (k == 0))
        def _():
            dsh_ref[...] = jnp.zeros_like(dsh_ref)
            dsc_ref[...] = jnp.zeros_like(dsc_ref)

        @pl.when(k == 0)
        def _():
            acc[mine, :] = jnp.zeros((tm, D), F32)

        for c in range(n):
            @pl.when(k % n == c)
            def _(c=c):
                acc[mine, :] += _dot_nt(dp_ref[...], w_refs[c][...])

        @pl.when(k == nk - 1)
        def _():
            dh = acc[mine, :]
            xs, rstd = _standardize(x_ref[...])
            dsh_ref[...] += jnp.sum(dh, axis=0, keepdims=True)
            dsc_ref[...] += jnp.sum(dh * xs, axis=0, keepdims=True)
            dx_ref[...] = _standardize_bwd(dh * (1.0 + mod_ref[:, D:2 * D]), xs, rstd) + dxr_ref[...]

    row = pl.BlockSpec((tm, D), lambda k, i: (jnp.where(k == nk - 1, i, 0), 0))
    vec = pl.BlockSpec((1, D), lambda k, i: (0, 0))
    return _hosted(
        body, job, (dproj, *w_args, x, mod, dx_res), name="proj_bwd",
        grid=(nk, S // tm),
        in_specs=[pl.BlockSpec((tm, tk), lambda k, i: (i, k))] + w_specs + [row, pl.BlockSpec((1, 3 * D), lambda k, i: (0, 0)), row],
        out_specs=[row, vec, vec],
        out_shape=[jax.ShapeDtypeStruct((S, D), F32), jax.ShapeDtypeStruct((1, D), F32), jax.ShapeDtypeStruct((1, D), F32)],
        scratch_shapes=[pltpu.VMEM((S, D), F32)],
        semantics=("arbitrary", "arbitrary"))


def _grad_w_in(h, dproj):
    S, D = h.shape
    shard = IN_COLS // N_CHIPS

    def body(h_ref, d_ref, o_ref):
        o_ref[...] = _dot_tn(h_ref[...], d_ref[...]).astype(BF)

    return pl.pallas_call(
        body, name="grad_w_in",
        grid=(N_CHIPS,),
        in_specs=[pl.BlockSpec((S, D), lambda n: (0, 0)), pl.BlockSpec((S, shard), lambda n: (0, n))],
        out_specs=pl.BlockSpec((None, D, shard), lambda n: (n, 0, 0)),
        out_shape=jax.ShapeDtypeStruct((N_CHIPS, D, shard), BF),
        compiler_params=_params(dimension_semantics=("parallel",)),
    )(h, dproj)


def _all_gather8(x, name):
    R, N = x.shape

    def body(x_ref, out_ref, send_sems, recv_sems):
        mx, my, mc = lax.axis_index("x"), lax.axis_index("y"), lax.axis_index("c")
        me = 4 * mx + 2 * my + mc
        out_ref[me] = x_ref[...]
        copies = []
        for k in range(1, N_DEV):
            peer = (_flip(mx, k & 4), _flip(my, k & 2), _flip(mc, k & 1))
            cp = pltpu.make_async_remote_copy(src_ref=x_ref, dst_ref=out_ref.at[me], send_sem=send_sems.at[k - 1],
                                              recv_sem=recv_sems.at[k - 1], device_id=peer, device_id_type=MESH)
            cp.start()
            copies.append(cp)
        for cp in copies:
            cp.wait()

    return pl.pallas_call(
        body, name=name,
        in_specs=[pl.BlockSpec(memory_space=pltpu.VMEM)],
        out_specs=pl.BlockSpec(memory_space=pltpu.VMEM),
        out_shape=jax.ShapeDtypeStruct((N_DEV, R, N), F32),
        scratch_shapes=[pltpu.SemaphoreType.DMA((N_DEV - 1,)), pltpu.SemaphoreType.DMA((N_DEV - 1,))],
        compiler_params=_params(),
    )(x)


def _rows2d(a):
    return a.reshape(-1, a.shape[-1])


def _tile_rows(rows, cols, n_arrays):
    budget = (24 << 20) // (n_arrays * 2 * 4 * cols)
    if rows <= budget:
        return rows
    tm = 8
    for cand in range(8, budget + 1, 8):
        if rows % cand == 0:
            tm = cand
    return tm


SUM_ROWS = 256


def _sum_cores(g, sent, where):
    chips, lead, _, r, cols = g.shape
    tr = min(r, SUM_ROWS)

    def body(where_ref, g_ref, s_ref, out_ref):
        out_ref[...] = (g_ref[...].astype(F32) + s_ref[...].astype(F32)).astype(BF)

    spec = pl.BlockSpec((None, tr, cols), lambda i, j, where_ref: (i, j, 0))
    out = pl.pallas_call(
        body, name="sum_cores",
        grid_spec=pltpu.PrefetchScalarGridSpec(
            num_scalar_prefetch=1, grid=(chips * lead, r // tr),
            in_specs=[pl.BlockSpec((None, None, tr, cols), lambda i, j, where_ref: (i, where_ref[1], j, 0)), spec],
            out_specs=spec),
        out_shape=jax.ShapeDtypeStruct((chips * lead, r, cols), BF),
        compiler_params=_params(dimension_semantics=("parallel", "parallel")),
    )(where, g.reshape(chips * lead, 2, r, cols), sent.reshape(chips * lead, r, cols))
    return out.reshape(chips, lead, r, cols)


def _sum_chips(sums, got, where):
    _, lead, r, cols = sums.shape
    tr = min(r, SUM_ROWS)

    def body(where_ref, s_ref, g_ref, out_ref):
        out_ref[...] = ((s_ref[...].astype(F32) + g_ref[0].astype(F32)) + g_ref[1].astype(F32)) + g_ref[2].astype(F32)

    return pl.pallas_call(
        body, name="sum_chips",
        grid_spec=pltpu.PrefetchScalarGridSpec(
            num_scalar_prefetch=1, grid=(lead, r // tr),
            in_specs=[pl.BlockSpec((None, None, tr, cols), lambda i, j, where_ref: (where_ref[0], i, j, 0)),
                      pl.BlockSpec((N_CHIPS - 1, None, tr, cols), lambda i, j, where_ref: (0, i, j, 0))],
            out_specs=pl.BlockSpec((None, None, tr, cols), lambda i, j, where_ref: (i, where_ref[1], j, 0))),
        out_shape=jax.ShapeDtypeStruct((lead, 2, r, cols), F32),
        compiler_params=_params(dimension_semantics=("parallel", "parallel")),
    )(where, sums, got)


def _adamw(w, m, v, groups, name):
    shape = w.shape
    w2, m2, v2 = _rows2d(w), _rows2d(m), _rows2d(v)
    rows, cols = w2.shape
    ng = len(groups)
    n = len(groups[0])
    slab = rows // ng
    gs = [_rows2d(g) for grp in groups for g in grp]
    tm = _tile_rows(slab, cols, 7 + n)
    tiles = slab // tm
    c1 = 1.0 / (1.0 - ADAM_B1 ** ADAM_STEP)
    c2 = 1.0 / (1.0 - ADAM_B2 ** ADAM_STEP)

    def body(*refs):
        w_ref, m_ref, v_ref = refs[:3]
        g_refs = refs[3:3 + ng * n]
        go_ref, d_ref, mo_ref, vo_ref = refs[3 + ng * n:]
        which = pl.program_id(0)
        for s in range(ng):
            @pl.when(which == s)
            def _(s=s):
                g = g_refs[s * n][...]
                for r in g_refs[s * n + 1:(s + 1) * n]:
                    g = g + r[...]
                mn = ADAM_B1 * m_ref[...] + (1.0 - ADAM_B1) * g
                vn = ADAM_B2 * v_ref[...] + (1.0 - ADAM_B2) * (g * g)
                go_ref[...] = g
                mo_ref[...] = mn
                vo_ref[...] = vn
                d_ref[...] = -ADAM_LR * ((mn * c1) / (jnp.sqrt(vn * c2) + ADAM_EPS) + ADAM_WD * w_ref[...])

    spec = pl.BlockSpec((tm, cols), lambda s, i: (s * tiles + i, 0))
    g_specs = [pl.BlockSpec((tm, cols), lambda s, i, k=k: (jnp.where(s == k, i, jnp.where(s < k, 0, tiles - 1)), 0))
               for k in range(ng) for _ in range(n)]
    outs = pl.pallas_call(
        body, name=name,
        grid=(ng, tiles),
        in_specs=[spec] * 3 + g_specs,
        out_specs=[spec] * 4,
        out_shape=[jax.ShapeDtypeStruct((rows, cols), F32)] * 4,
        compiler_params=_params(dimension_semantics=("arbitrary", "arbitrary")),
    )(w2, m2, v2, *gs)
    return [o.reshape(shape) for o in outs]


def _lower_bounds(r0, r1):
    top = jnp.maximum(r0, r1)
    e0, e1 = jnp.exp(r0 - top), jnp.exp(r1 - top)
    p0, p1 = e0 / (e0 + e1), e1 / (e0 + e1)
    return p0 - p0, (p0 + p1) - p0


def _lbs_fwd(lb_raw):
    def body(lb_ref, out_ref):
        l0, l1 = _lower_bounds(lb_ref[0:1, :], lb_ref[1:2, :])
        out_ref[0:1, :] = l0
        out_ref[1:2, :] = l1

    return pl.pallas_call(body, name="lower_bounds", out_shape=jax.ShapeDtypeStruct(lb_raw.shape, F32), compiler_params=_params())(lb_raw)


def _mod_rows(c_all, w_mod, tn=768):
    _, D, cols = w_mod.shape

    def body(c_ref, w_ref, out_ref):
        out_ref[...] = _dot(c_ref[...].astype(BF), w_ref[...].astype(BF))

    return pl.pallas_call(
        body, name="mod_rows",
        grid=(DEPTH,),
        in_specs=[pl.BlockSpec((N_DEV, D), lambda l: (0, 0)), pl.BlockSpec((None, D, cols), lambda l: (l, 0, 0))],
        out_specs=pl.BlockSpec((N_DEV, cols), lambda l: (0, l)),
        out_shape=jax.ShapeDtypeStruct((N_DEV, DEPTH * cols), F32),
        compiler_params=_params(dimension_semantics=("parallel",)),
    )(c_all, w_mod)


def _grad_w_mod(c_all, dmod_cols):
    D = c_all.shape[1]
    cols = dmod_cols.shape[-1]

    def body(c_ref, d_ref, out_ref):
        out_ref[...] = _dot_tn(c_ref[...].astype(BF), d_ref[...].astype(BF))

    return pl.pallas_call(
        body, name="grad_w_mod",
        grid=(DEPTH,),
        in_specs=[pl.BlockSpec((N_DEV, D), lambda l: (0, 0)), pl.BlockSpec((None, N_DEV, cols), lambda l: (l, 0, 0))],
        out_specs=pl.BlockSpec((None, D, cols), lambda l: (l, 0, 0)),
        out_shape=jax.ShapeDtypeStruct((DEPTH, D, cols), F32),
        compiler_params=_params(dimension_semantics=("parallel",)),
    )(c_all, dmod_cols)


def _sum_devices(parts):
    _, R, N = parts.shape

    def body(p_ref, out_ref):
        acc = p_ref[0]
        for d in range(1, N_DEV):
            acc = acc + p_ref[d]
        out_ref[...] = acc

    return pl.pallas_call(body, name="sum_devices", out_shape=jax.ShapeDtypeStruct((R, N), F32), compiler_params=_params())(parts)


def _lbs_bwd(lb_raw, dl):
    def body(lb_ref, dl_ref, out_ref):
        _, pull = jax.vjp(_lower_bounds, lb_ref[0:1, :], lb_ref[1:2, :])
        d0, d1 = pull((dl_ref[0:1, :], dl_ref[1:2, :]))
        out_ref[0:1, :] = d0
        out_ref[1:2, :] = d1

    return pl.pallas_call(body, name="lower_bounds_bwd", out_shape=jax.ShapeDtypeStruct(lb_raw.shape, F32), compiler_params=_params())(lb_raw, dl)


def kernel(x, c, w_mod, b_mod, w_in, conv_w, hgrn_norm_w, lower_bounds, w_branch, w_out, ln_g, ln_b, loss_target, m_w_mod, m_b_mod, m_w_in, m_conv_w, m_hgrn_norm_w, m_lower_bounds, m_w_branch, m_w_out, m_ln_g, m_ln_b, v_w_mod, v_b_mod, v_w_in, v_conv_w, v_hgrn_norm_w, v_lower_bounds, v_w_branch, v_w_out, v_ln_g, v_ln_b):
    D = D_MODEL
    x0 = x[0]
    target = loss_target[0]
    S = x0.shape[0]
    mx, my, mc = lax.axis_index("x"), lax.axis_index("y"), lax.axis_index("c")
    chip = 2 * mx + my
    me = 2 * chip + mc
    mod_cols = 3 * D // N_CHIPS

    plan = _Plan(w_in, w_branch, w_out, chip, mc)
    n_conv = DEPTH * 3 * (WIDTH // N_CHIPS)
    first = jnp.concatenate([c, conv_w.reshape(1, n_conv), jnp.zeros((1, 2 * D - D - n_conv), F32)], axis=1)
    first = plan.first(first.reshape(8, 2 * D // 8)).reshape(N_DEV, 2 * D)

    c_all = first[:, :D]
    conv_all = first[:, D:D + n_conv].reshape(N_DEV, DEPTH, 3, WIDTH // N_CHIPS)[0::2]
    mod_part = _all_gather8(_mod_rows(c_all, w_mod), "gather_mod")[0::2]
    mod_part = lax.dynamic_index_in_dim(mod_part, me, axis=1, keepdims=False).reshape(N_CHIPS, DEPTH, mod_cols)
    mods = [(mod_part[:, l].reshape(1, 3 * D) + b_mod[l][None, :]) for l in range(DEPTH)]
    lbs = _lbs_fwd(lower_bounds).reshape(DEPTH, WIDTH // 128, 1, 128)
    loss_blk, dx, small = _local_step(x0, target, mods, lbs, conv_all, hgrn_norm_w, ln_g, ln_b, plan)

    n_mod, n_nw, n_lb, n_ln, n_cw = DEPTH * 3 * D, DEPTH * 128, DEPTH * WIDTH, DEPTH * D, DEPTH * 3 * WIDTH
    row = jnp.concatenate(
        [jnp.concatenate([small[l][0], small[l][1], small[l][2]], axis=1) for l in range(DEPTH)]
        + [jnp.sum(small[l][3], axis=0) for l in range(DEPTH)]
        + [small[l][4].reshape(1, WIDTH) for l in range(DEPTH)]
        + [small[l][5] for l in range(DEPTH)] + [small[l][6] for l in range(DEPTH)]
        + [jnp.transpose(small[l][7], (1, 0, 2)).reshape(1, 3 * WIDTH) for l in range(DEPTH)]
        + [loss_blk[0:1, :]], axis=1)
    n_row = row.shape[1]
    fold = -(-n_row // (8 * 128)) * 128
    rows = jnp.concatenate([row, jnp.zeros((1, 8 * fold - n_row), F32)], axis=1).reshape(8, fold)

    whole, gathered = plan.finish(rows)
    grads = {kind: [[whole[(kind, l)]] for l in range(DEPTH)] for kind in ("in", "br", "out")}

    off_nw = n_mod
    off_lb = off_nw + n_nw
    off_lng = off_lb + n_lb
    off_lnb = off_lng + n_ln
    off_cw = off_lnb + n_ln
    off_loss = off_cw + n_cw
    total = _sum_devices(gathered).reshape(1, 8 * fold)
    gathered = gathered.reshape(N_DEV, 1, 8 * fold)
    d_lower = _lbs_bwd(lower_bounds, total[0, off_lb:off_lng].reshape(DEPTH, WIDTH))
    loss = total[0, off_loss]
    d_b_mod = total[0, :n_mod].reshape(DEPTH, 3 * D)
    d_norm_w = total[0, off_nw:off_lb].reshape(DEPTH, 128)
    d_ln_g = total[0, off_lng:off_lnb].reshape(DEPTH, D)
    d_ln_b = total[0, off_lnb:off_cw].reshape(DEPTH, D)
    d_conv = total[0, off_cw:off_loss].reshape(DEPTH, 3, N_CHIPS, WIDTH // N_CHIPS)
    d_conv = lax.dynamic_index_in_dim(d_conv, chip, axis=2, keepdims=False)
    dmod_all = gathered[:, 0, :n_mod].reshape(N_DEV, DEPTH, N_CHIPS, mod_cols)
    dmod_cols = jnp.transpose(lax.dynamic_index_in_dim(dmod_all, chip, axis=2, keepdims=False), (1, 0, 2))
    d_w_mod = _grad_w_mod(c_all, dmod_cols)

    res = {}
    res["w_mod"] = _adamw(w_mod, m_w_mod, v_w_mod, [[d_w_mod]], "adamw_w_mod")
    res["b_mod"] = _adamw(b_mod, m_b_mod, v_b_mod, [[d_b_mod]], "adamw_b_mod")
    res["w_in"] = _adamw(w_in, m_w_in, v_w_in, grads["in"], "adamw_w_in")
    res["conv_w"] = _adamw(conv_w, m_conv_w, v_conv_w, [[d_conv]], "adamw_conv_w")
    res["hgrn_norm_w"] = _adamw(hgrn_norm_w, m_hgrn_norm_w, v_hgrn_norm_w, [[d_norm_w]], "adamw_norm_w")
    res["lower_bounds"] = _adamw(lower_bounds, m_lower_bounds, v_lower_bounds, [[d_lower]], "adamw_lower_bounds")
    res["w_branch"] = _adamw(w_branch, m_w_branch, v_w_branch, grads["br"], "adamw_w_branch")
    res["w_out"] = _adamw(w_out, m_w_out, v_w_out, grads["out"], "adamw_w_out")
    res["ln_g"] = _adamw(ln_g, m_ln_g, v_ln_g, [[d_ln_g]], "adamw_ln_g")
    res["ln_b"] = _adamw(ln_b, m_ln_b, v_ln_b, [[d_ln_b]], "adamw_ln_b")
    names = ["w_mod", "b_mod", "w_in", "conv_w", "hgrn_norm_w", "lower_bounds", "w_branch", "w_out", "ln_g", "ln_b"]
    return (loss, dx[None], *[res[n][0] for n in names], *[res[n][1] for n in names],
            *[res[n][2] for n in names], *[res[n][3] for n in names])


class _Plan:
    FIRST_CHUNKS = 2
    WINDOWS = ((0, 1280), (1280, 640), (1920, 384))

    def __init__(self, w_in, w_br, w_out, chip, core):
        self.local = {"in": w_in, "br": w_br, "out": w_out}
        self.chip, self.where = chip, jnp.stack([chip, core]).astype(jnp.int32)
        self.gathered, self.partial, self.grads, self.chip_sums, self.scattered, self.pending = {}, {}, {}, {}, {}, {}

    def chunks(self, l):
        return self.FIRST_CHUNKS if l == 0 else 1

    def w_in(self, l):
        return [self.gathered[("in", l, c)] for c in range(self.chunks(l))]

    def _shard(self, key):
        mine = self.local[key[0]][key[1]]
        if key[0] == "in":
            cols = mine.shape[-1] // self.chunks(key[1])
            mine = mine[:, key[2] * cols:(key[2] + 1) * cols]
        return mine.astype(BF)

    def _slab(self, key):
        mine = _halves(self._shard(key))
        return lax.dynamic_update_slice(lax.empty((N_CHIPS,) + mine.shape, mine.dtype), mine[None], (self.chip, 0, 0, 0, 0))

    def _gather(self, keys):
        return ("gather", keys), _gather_job([self._slab(key) for key in keys])

    def _gather_window(self, key, n):
        slab = self._slab(key) if n == 0 else self.partial[key]
        last = n == len(self.WINDOWS) - 1
        handed = self.WINDOWS[max(n - 1, 0):n] + (self.WINDOWS[n:] if last else ())
        return ("gather" if last else "gather_part", [key]), _gather_job([slab], self.WINDOWS[n], handed)

    def _to_sibling(self, keys):
        return ("to_sibling", keys), _to_sibling_job([_halves(self.grads[key], 1) for key in keys])

    def _scatter(self, keys):
        return ("scatter", keys), _scatter_job([self.chip_sums[key] for key in keys])

    def job(self, stage, l, c=0):
        parts = []
        if stage == "proj_fwd":
            parts = [self._gather([("in", l, c + 1)] if c + 1 < self.chunks(l) else [("br", l), ("out", l)])]
        elif stage == "attn_fwd" and l + 1 < DEPTH:
            parts = [self._gather_window(("in", l + 1, 0), 0)]
        elif stage == "hgrn_fwd" and l + 1 < DEPTH:
            parts = [self._gather_window(("in", l + 1, 0), 1)]
        elif stage == "merge_fwd" and l + 1 < DEPTH:
            parts = [self._gather_window(("in", l + 1, 0), 2)]
        elif stage == "attn_bwd":
            parts = [self._to_sibling([("out", l), ("br", l)])] + ([self._scatter([("in", l + 1)])] if l + 1 < DEPTH else [])
        elif stage == "hgrn_bwd":
            parts = [self._scatter([("out", l), ("br", l)])]
        elif stage == "proj_bwd":
            parts = [self._to_sibling([("in", l)])] if l else [self._scatter([("in", 0)])]
        self.pending[(stage, l, c)] = [(tag, len(job.outs)) for tag, job in parts]
        return _join_jobs([job for _, job in parts])

    def done(self, stage, l, outs, c=0):
        if outs is None:
            return
        at = 0
        for (what, keys), n_outs in self.pending[(stage, l, c)]:
            mine, at = outs[at:at + n_outs], at + n_outs
            for n, key in enumerate(keys):
                if what == "gather":
                    self.gathered[key] = mine[n].reshape((N_CHIPS,) + self._shard(key).shape)
                elif what == "gather_part":
                    self.partial[key] = mine[n]
                elif what == "to_sibling":
                    self.chip_sums[key] = _sum_cores(_halves(self.grads[key], 1), mine[n], self.where)
                else:
                    self.scattered[key] = mine[n]

    def first(self, rows):
        tag, job = self._gather([("in", 0, 0)])
        self.pending[("first", 0, 0)] = [(tag, len(job.outs))]
        outs = _run_job(_join_jobs([job, _gather8_job(rows)]), "gather_first")
        self.done("first", 0, outs[:-1])
        return outs[-1]

    def took(self, key, grad):
        self.grads[key] = grad
        if key == ("in", 0):
            tag, job = self._to_sibling([key])
            self.pending[("took", 0, 0)] = [(tag, len(job.outs))]
            self.done("took", 0, _run_job(job, "to_sibling_last"))

    def finish(self, rows):
        keys = [(kind, l) for kind in ("in", "br", "out") for l in range(DEPTH)]
        halves = [_sum_chips(self.chip_sums[key], self.scattered[key], self.where) for key in keys]
        outs = _run_job(_join_jobs([_place_job(halves), _gather8_job(rows)]), "place_halves")
        return {key: w.reshape(self.grads[key].shape[1:]) for key, w in zip(keys, outs[:-1])}, outs[-1]


def _local_step(x0, target, mods, lbs, conv_all, hgrn_norm_w, ln_g, ln_b, plan):
    D = D_MODEL
    after, before = _attn_consts()
    hg_consts = _hgrn_consts()

    saved = []
    xl = x0
    for l in range(DEPTH):
        n = plan.chunks(l)
        (proj, h), got = _proj_fwd(xl, mods[l], plan.gathered[("in", l, 0)], plan.job("proj_fwd", l, 0), (0, n))
        plan.done("proj_fwd", l, got, 0)
        for c in range(1, n):
            (proj,), got = _proj_cols(h, plan.gathered[("in", l, c)], (c, n), proj, plan.job("proj_fwd", l, c))
            plan.done("proj_fwd", l, got, c)
        (o_a, tot), got = _attn_fwd(proj, after, plan.job("attn_fwd", l))
        plan.done("attn_fwd", l, got)
        (o_b, states), got = _hgrn_fwd(proj, lbs[l], hg_consts, plan.job("hgrn_fwd", l))
        plan.done("hgrn_fwd", l, got)
        ys = _branch_fwd(proj, o_a, o_b, hgrn_norm_w, conv_all, l)
        wb = jnp.concatenate(list(plan.gathered[("br", l)]), axis=-1)
        (x_next, merged, y), got = _merge_fwd(xl, mods[l], proj, ys, wb, plan.gathered[("out", l)], ln_g, ln_b, l, plan.job("merge_fwd", l))
        plan.done("merge_fwd", l, got)
        saved.append((xl, proj, h, o_a, tot, o_b, states, ys, merged, y, wb))
        xl = x_next
    dx, loss_blk = _loss_head(xl, target)

    small = [None] * DEPTH
    for l in reversed(range(DEPTH)):
        xin, proj, h, o_a, tot, o_b, states, ys, merged, y, wb = saved[l]
        dx_res, dproj, dy_a, dy_b, dy_c, dln_g, dln_b, dgate, g_br, g_out = _merge_bwd(
            dx, xin, y, merged, mods[l], proj, ys, wb, plan.gathered[("out", l)], ln_g, l)
        plan.took(("out", l), g_out)
        plan.took(("br", l), g_br)
        d_oa, d_ob, dnorm_w, dconv_w, dproj = _branch_bwd(proj, o_a, o_b, hgrn_norm_w, conv_all, dy_a, dy_b, dy_c, dproj, l)
        (dproj,), got = _attn_bwd(proj, d_oa, tot, after, before, dproj, plan.job("attn_bwd", l))
        plan.done("attn_bwd", l, got)
        (dlb, dproj), got = _hgrn_bwd(proj, lbs[l], states, d_ob, hg_consts, dproj, plan.job("hgrn_bwd", l))
        plan.done("hgrn_bwd", l, got)
        plan.took(("in", l), _grad_w_in(h, dproj))
        wgs = plan.w_in(l)
        tile = {"tk": wgs[0].shape[-1]} if len(wgs) > 1 else {}
        (dx, dshift, dscale), got = _proj_bwd(dproj, wgs, xin, mods[l], dx_res, plan.job("proj_bwd", l), **tile)
        plan.done("proj_bwd", l, got)
        small[l] = (dshift, dscale, dgate, dnorm_w, dlb, dln_g, dln_b, dconv_w)
    return loss_blk, dx, small
```

```python
import functools
import math

import numpy as np
import jax
import jax.numpy as jnp
from jax import lax
from jax.experimental import pallas as pl
from jax.experimental.pallas import tpu as pltpu

F32 = jnp.float32
BF = jnp.bfloat16
MESH = pl.DeviceIdType.MESH

DEPTH = 2
D_MODEL = 1024
WIDTH = 512
IN_COLS = 12 * WIDTH + 3 * D_MODEL
N_CHIPS = 4
N_DEV = 8
SB_BLOCK = 128
SB_HEAD_DIM = 64
HG_CHUNK = 128
HG_DIM = 128
LN_EPS = 1e-5
RMS_EPS = 1e-6
ALPHA = (2.0 * DEPTH) ** 0.25
ADAM_LR, ADAM_B1, ADAM_B2, ADAM_EPS, ADAM_WD, ADAM_STEP = 0.001, 0.9, 0.999, 1e-08, 0.01, 10
VMEM_LIMIT = 56 << 20


def _params(**kw):
    return pltpu.CompilerParams(vmem_limit_bytes=VMEM_LIMIT, **kw)


def _dot(a, b):
    return jnp.dot(a, b, preferred_element_type=F32)


def _dot_nt(a, b):
    return lax.dot_general(a, b, (((1,), (1,)), ((), ())), preferred_element_type=F32)


def _dot_tn(a, b):
    return lax.dot_general(a, b, (((0,), (0,)), ((), ())), preferred_element_type=F32)


def _sigmoid(x):
    return 1.0 / (1.0 + jnp.exp(-x))


def _silu(x):
    return x * _sigmoid(x)


def _standardize(x):
    mu = jnp.mean(x, axis=-1, keepdims=True)
    xc = x - mu
    var = jnp.mean(xc * xc, axis=-1, keepdims=True)
    rstd = lax.rsqrt(var + LN_EPS)
    return xc * rstd, rstd


def _standardize_bwd(dxs, xs, rstd):
    return rstd * (dxs - jnp.mean(dxs, axis=-1, keepdims=True) - xs * jnp.mean(dxs * xs, axis=-1, keepdims=True))


class _Job:
    def __init__(self, ins, outs, sems, make, alias=None):
        self.ins, self.outs, self.sems, self.make = list(ins), list(outs), list(sems), make
        self.alias = dict(alias or {})


def _join_jobs(jobs):
    jobs = [j for j in jobs if j is not None]
    if len(jobs) <= 1:
        return jobs[0] if jobs else None

    def make(ins, outs, sems):
        phases, i, o, s = [], 0, 0, 0
        for j in jobs:
            got = j.make(ins[i:i + len(j.ins)], outs[o:o + len(j.outs)], sems[s:s + len(j.sems)])
            i, o, s = i + len(j.ins), o + len(j.outs), s + len(j.sems)
            for n, phase in enumerate(got):
                if n == len(phases):
                    phases.append([])
                phases[n] += phase
        return phases

    alias, i, o = {}, 0, 0
    for j in jobs:
        alias.update({i + a: o + b for a, b in j.alias.items()})
        i, o = i + len(j.ins), o + len(j.outs)
    return _Job(sum([j.ins for j in jobs], []), sum([j.outs for j in jobs], []), sum([j.sems for j in jobs], []), make, alias)


def _flip(v, bit):
    return 1 - v if bit else v


def _halves(a, front=0):
    shape = a.shape
    lead = math.prod(shape[front:-2])
    return a.reshape(shape[:front] + (lead, 2, shape[-2] // 2, shape[-1]))


def _dma_sems(*shapes):
    return [pltpu.SemaphoreType.DMA(s) for s in shapes]


def _same(arrays):
    return [jax.ShapeDtypeStruct(a.shape, a.dtype) for a in arrays]


def _gather_job(slabs, window=None, handed=(None,)):
    n = len(slabs)
    cols = lambda w: slice(None) if w is None else pl.ds(*w)

    def make(ins, outs, sems):
        send1, recv1, send2, recv2 = sems
        mx, my, mc = lax.axis_index("x"), lax.axis_index("y"), lax.axis_index("c")
        fetch, pass_on = [], []
        for a in range(n):
            for k in range(1, N_CHIPS):
                px, py = _flip(mx, k & 2), _flip(my, k & 1)
                if window != ():
                    ours = outs[a].at[2 * mx + my, :, mc, :, cols(window)]
                    fetch.append(pltpu.make_async_remote_copy(
                        src_ref=ours, dst_ref=ours, send_sem=send1.at[a, k - 1], recv_sem=recv1.at[a, k - 1],
                        device_id=(px, py, mc), device_id_type=MESH))
                for h, w in enumerate(handed):
                    theirs = outs[a].at[2 * px + py, :, mc, :, cols(w)]
                    (pass_on if w == window else fetch).append(pltpu.make_async_remote_copy(
                        src_ref=theirs, dst_ref=theirs, send_sem=send2.at[a, k - 1, h], recv_sem=recv2.at[a, k - 1, h],
                        device_id=(mx, my, 1 - mc), device_id_type=MESH))
        return [fetch, pass_on] if pass_on else [fetch]

    pairs = (n, N_CHIPS - 1)
    each = pairs + (max(len(handed), 1),)
    return _Job(slabs, _same(slabs), _dma_sems(pairs, pairs, each, each), make, {a: a for a in range(n)})


def _to_sibling_job(grads):
    n = len(grads)

    def make(ins, outs, sems):
        send_sems, recv_sems = sems
        mx, my, mc = lax.axis_index("x"), lax.axis_index("y"), lax.axis_index("c")
        return [[pltpu.make_async_remote_copy(
            src_ref=ins[a].at[:, :, 1 - mc], dst_ref=outs[a], send_sem=send_sems.at[a], recv_sem=recv_sems.at[a],
            device_id=(mx, my, 1 - mc), device_id_type=MESH) for a in range(n)]]

    outs = [jax.ShapeDtypeStruct(g.shape[:2] + g.shape[3:], g.dtype) for g in grads]
    return _Job(grads, outs, _dma_sems((n,), (n,)), make)


def _scatter_job(sums):
    n = len(sums)

    def make(ins, outs, sems):
        send_sems, recv_sems = sems
        mx, my, mc = lax.axis_index("x"), lax.axis_index("y"), lax.axis_index("c")
        copies = []
        for a in range(n):
            for k in range(1, N_CHIPS):
                px, py = _flip(mx, k & 2), _flip(my, k & 1)
                copies.append(pltpu.make_async_remote_copy(
                    src_ref=ins[a].at[2 * px + py], dst_ref=outs[a].at[k - 1], send_sem=send_sems.at[a, k - 1],
                    recv_sem=recv_sems.at[a, k - 1], device_id=(px, py, mc), device_id_type=MESH))
        return [copies]

    pairs = (n, N_CHIPS - 1)
    return _Job(sums, [jax.ShapeDtypeStruct((N_CHIPS - 1,) + s.shape[1:], s.dtype) for s in sums], _dma_sems(pairs, pairs), make)


def _place_job(wholes):
    n = len(wholes)

    def make(ins, outs, sems):
        send_sems, recv_sems = sems
        mx, my, mc = lax.axis_index("x"), lax.axis_index("y"), lax.axis_index("c")
        copies = []
        for a in range(n):
            here = outs[a].at[:, mc]
            copies.append(pltpu.make_async_remote_copy(src_ref=here, dst_ref=here, send_sem=send_sems.at[a], recv_sem=recv_sems.at[a],
                                                       device_id=(mx, my, 1 - mc), device_id_type=MESH))
        return [copies]

    return _Job(wholes, _same(wholes), _dma_sems((n,), (n,)), make, {a: a for a in range(n)})


def _gather8_job(x):
    def make(ins, outs, sems):
        local_sem, send_sems, recv_sems = sems
        mx, my, mc = lax.axis_index("x"), lax.axis_index("y"), lax.axis_index("c")
        here = outs[0].at[4 * mx + 2 * my + mc]
        copies = [pltpu.make_async_copy(ins[0], here, local_sem.at[0])]
        for k in range(1, N_DEV):
            peer = (_flip(mx, k & 4), _flip(my, k & 2), _flip(mc, k & 1))
            copies.append(pltpu.make_async_remote_copy(src_ref=ins[0], dst_ref=here, send_sem=send_sems.at[k - 1],
                                                       recv_sem=recv_sems.at[k - 1], device_id=peer, device_id_type=MESH))
        return [copies]

    return _Job([x], [jax.ShapeDtypeStruct((N_DEV,) + x.shape, x.dtype)], _dma_sems((1,), (N_DEV - 1,), (N_DEV - 1,)), make)


def _run_phases(phases, first=0):
    for n, phase in enumerate(phases):
        if n >= first:
            for cp in phase:
                cp.start()
        for cp in phase:
            cp.wait()


def _run_job(job, name):
    k_in, k_out = len(job.ins), len(job.outs)

    def body(*refs):
        _run_phases(job.make(refs[:k_in], refs[k_in:k_in + k_out], refs[k_in + k_out:]))

    hbm = pl.BlockSpec(memory_space=pl.ANY)
    return pl.pallas_call(body, name=name, in_specs=[hbm] * k_in, out_specs=[hbm] * k_out, out_shape=job.outs,
                          scratch_shapes=job.sems, input_output_aliases=job.alias, compiler_params=_params())(*job.ins)


def _hosted(body, job, args, *, name, grid, in_specs, out_specs, out_shape, scratch_shapes=(), semantics, aliases=None):
    in_specs, out_specs, out_shape, scratch = list(in_specs), list(out_specs), list(out_shape), list(scratch_shapes)
    aliases = dict(aliases or {})
    if job is None:
        outs = pl.pallas_call(body, name=name, grid=grid, in_specs=in_specs, out_specs=out_specs, out_shape=out_shape,
                              scratch_shapes=scratch, input_output_aliases=aliases,
                              compiler_params=_params(dimension_semantics=semantics))(*args)
        return list(outs), None
    n_in, n_out, n_scr, k_in, k_out = len(in_specs), len(out_specs), len(scratch), len(job.ins), len(job.outs)

    def wrapped(*refs):
        ins, rest = refs[:n_in], refs[n_in:]
        job_ins, rest = rest[:k_in], rest[k_in:]
        outs, rest = rest[:n_out], rest[n_out:]
        job_outs, rest = rest[:k_out], rest[k_out:]
        scr, sems = rest[:n_scr], rest[n_scr:]
        ids = [pl.program_id(a) for a in range(len(grid))]
        first = functools.reduce(jnp.logical_and, [i == 0 for i in ids])
        last = functools.reduce(jnp.logical_and, [i == g - 1 for i, g in zip(ids, grid)])

        @pl.when(first)
        def _():
            for cp in job.make(job_ins, job_outs, sems)[0]:
                cp.start()

        body(*ins, *outs, *scr)

        @pl.when(last)
        def _():
            _run_phases(job.make(job_ins, job_outs, sems), first=1)

    hbm = pl.BlockSpec(memory_space=pl.ANY)
    outs = pl.pallas_call(
        wrapped, name=name, grid=grid, in_specs=in_specs + [hbm] * k_in, out_specs=out_specs + [hbm] * k_out,
        out_shape=out_shape + job.outs, scratch_shapes=scratch + job.sems,
        input_output_aliases={**aliases, **{n_in + i: n_out + o for i, o in job.alias.items()}},
        compiler_params=_params(dimension_semantics=("arbitrary",) * len(grid)))(*args, *job.ins)
    return list(outs[:n_out]), list(outs[n_out:])


def _proj_fwd(x, mod, wg, job=None, chunk=(0, 1), tm=512):
    S, D = x.shape
    tm = min(tm, S)
    tn = wg.shape[-1]
    c, n = chunk

    rows = S // tm

    def body(x_ref, mod_ref, w_ref, proj_ref, h_ref, hs):
        mine = pl.ds(pl.multiple_of(pl.program_id(1) * tm, tm), tm)

        @pl.when(pl.program_id(0) == 0)
        def _():
            xs, _ = _standardize(x_ref[...])
            h = xs * (1.0 + mod_ref[:, D:2 * D]) + mod_ref[:, 0:D]
            hb = h.astype(BF)
            hs[mine, :] = hb
            h_ref[...] = hb

        proj_ref[...] = _dot(hs[mine, :], w_ref[...])

    once = lambda j, i: jnp.where(j == 0, i, rows - 1)
    return _hosted(
        body, job, (x, mod, wg), name="proj_fwd",
        grid=(N_CHIPS, rows),
        in_specs=[pl.BlockSpec((tm, D), lambda j, i: (once(j, i), 0)),
                  pl.BlockSpec((1, 3 * D), lambda j, i: (0, 0)),
                  pl.BlockSpec((None, D, tn), lambda j, i: (j, 0, 0))],
        out_specs=[pl.BlockSpec((tm, tn), lambda j, i: (i, j * n + c)),
                   pl.BlockSpec((tm, D), lambda j, i: (once(j, i), 0))],
        out_shape=[jax.ShapeDtypeStruct((S, IN_COLS), F32), jax.ShapeDtypeStruct((S, D), BF)],
        scratch_shapes=[pltpu.VMEM((S, D), BF)],
        semantics=("arbitrary", "arbitrary"))


def _proj_cols(h, wg, chunk, proj, job=None, tm=512):
    S, D = h.shape
    tm = min(tm, S)
    tn = wg.shape[-1]
    c, n = chunk

    def body(h_ref, w_ref, prev_ref, proj_ref):
        mine = pl.ds(pl.multiple_of(pl.program_id(1) * tm, tm), tm)
        proj_ref[...] = _dot(h_ref[mine, :], w_ref[...])

    in_specs = [pl.BlockSpec((S, D), lambda j, i: (0, 0)), pl.BlockSpec((None, D, tn), lambda j, i: (j, 0, 0)),
                pl.BlockSpec(memory_space=pl.ANY)]
    return _hosted(body, job, (h, wg, proj), name="proj_cols", grid=(N_CHIPS, S // tm), in_specs=in_specs,
                   out_specs=[pl.BlockSpec((tm, tn), lambda j, i: (i, j * n + c))],
                   out_shape=[jax.ShapeDtypeStruct(proj.shape, proj.dtype)],
                   semantics=("arbitrary", "arbitrary"), aliases={2: 0})


SB_ROWS = 256
SB_KEYS = 256


def _attn_consts():
    j = np.arange(SB_KEYS)[:, None]
    s = np.arange(SB_KEYS)[None, :]
    from_here = np.concatenate([(j >= s), (j >= s)], axis=0).astype(np.float32)
    return jnp.asarray(from_here, BF), jnp.asarray((j <= s).astype(np.float32), BF)


def _hi_lo(x):
    hi = lax.bitcast_convert_type(lax.bitcast_convert_type(x, jnp.uint32) & jnp.uint32(0xFFFF0000), F32)
    return hi.astype(BF), (x - hi).astype(BF)


def _sums_r(x, t2):
    hi, lo = _hi_lo(x)
    return _dot(jnp.concatenate([hi, lo], axis=1), t2)


def _all_lanes(col, lanes):
    return jnp.broadcast_to(col, (col.shape[0], lanes))


def _attn_rows(ref, r0, rows, lanes, head0, scale=None):
    v = ref[pl.ds(r0, rows), lanes]
    if scale is not None:
        v = v * scale
    return jnp.concatenate([jnp.where(head0, v, 0.0), jnp.where(head0, 0.0, v)], axis=0).astype(BF)


SB_PAIRS_FWD = 4
SB_PAIRS_BWD = 2


def _attn_specs(S, n_pairs):
    return lambda base: pl.BlockSpec((S, n_pairs * SB_BLOCK), lambda p, base=base: (0, base // n_pairs + p))


def _attn_scores(q2n, k_ref, lanes, kj, t2, from_here_ref, masked):
    c0 = pl.multiple_of(kj * SB_KEYS, SB_KEYS)
    kb = k_ref[pl.ds(c0, SB_KEYS), lanes].astype(BF)
    zn = _dot_nt(q2n, kb)
    lsb = jnp.minimum(zn, 0.0) - jnp.log(1.0 + jnp.exp(-jnp.abs(zn)))
    valid = None
    if masked:
        valid = (lax.broadcasted_iota(jnp.int32, zn.shape, 1) + kj * SB_KEYS) < t2
        lsb = jnp.where(valid, lsb, 0.0)
    return c0, kb, zn, valid, lsb, _sums_r(lsb, from_here_ref[...])


def _attn_fwd(proj, from_here, job=None):
    S = proj.shape[0]
    TQ = SB_ROWS
    assert S % TQ == 0 and SB_KEYS == TQ
    scale = SB_HEAD_DIM ** -0.5
    n_pairs = SB_PAIRS_FWD
    pairs = range(n_pairs)
    lanes = [pl.ds(p * SB_BLOCK, SB_BLOCK) for p in pairs]

    def body(q_ref, k_ref, v_ref, from_here_ref, o_ref, tot_ref, run, acc):
        head0 = lax.broadcasted_iota(jnp.int32, (1, 2 * SB_HEAD_DIM), 1) < SB_HEAD_DIM

        def qloop(qi, _):
            r0 = pl.multiple_of(qi * TQ, TQ)
            q2n = [_attn_rows(q_ref, r0, TQ, lanes[p], head0, -scale) for p in pairs]
            trow = lax.broadcasted_iota(jnp.int32, (TQ, SB_KEYS), 0) + qi * TQ
            t2 = jnp.concatenate([trow, trow], axis=0)
            run[...] = jnp.zeros_like(run)
            acc[...] = jnp.zeros_like(acc)

            def step(kj, masked):
                got = [_attn_scores(q2n[p], k_ref, lanes[p], kj, t2, from_here_ref, masked) for p in pairs]
                for p in pairs:
                    c0, _, zn, valid, _, sums = got[p]
                    r = run[p]
                    e = sums - zn + jnp.concatenate([r, r], axis=1)
                    if masked:
                        e = jnp.where(valid, e, -jnp.inf)
                    acc[p] += _dot(jnp.exp(e).astype(BF), v_ref[pl.ds(c0, SB_KEYS), lanes[p]].astype(BF))
                    run[p] = r + _all_lanes(sums[:, 0:1], SB_BLOCK)

            @pl.when(qi % 2 == 0)
            def _():
                step(qi, True)

            @pl.when(qi % 2 == 1)
            def _():
                step(qi, True)
                step(qi - 1, False)

            first = qi - 1 - qi % 2

            def below(n, _):
                step(first - 2 * n, False)
                step(first - 1 - 2 * n, False)
                return 0

            lax.fori_loop(0, qi // 2, below, 0)
            for p in pairs:
                o_ref[pl.ds(r0, TQ), lanes[p]] = jnp.where(head0, acc[p, 0:TQ, :], acc[p, TQ:2 * TQ, :])
                tot_ref[p, 0, pl.ds(r0, TQ), :] = run[p, 0:TQ, :]
                tot_ref[p, 1, pl.ds(r0, TQ), :] = run[p, TQ:2 * TQ, :]
            return 0

        lax.fori_loop(0, S // TQ, qloop, 0)

    col = _attn_specs(S, n_pairs)
    state = pltpu.VMEM((n_pairs, 2 * TQ, SB_BLOCK), F32)
    return _hosted(
        body, job, (proj, proj, proj, from_here), name="attn_fwd",
        grid=(WIDTH // (n_pairs * SB_BLOCK),),
        in_specs=[col(0), col(4), col(8), pl.BlockSpec(from_here.shape, lambda p: (0, 0))],
        out_specs=[col(0), pl.BlockSpec((n_pairs, 2, S, 128), lambda p: (p, 0, 0, 0))],
        out_shape=[jax.ShapeDtypeStruct((S, WIDTH), F32), jax.ShapeDtypeStruct((WIDTH // 128, 2, S, 128), F32)],
        scratch_shapes=[state, state],
        semantics=("parallel",))


def _attn_bwd(proj, d_o, tot, from_here, up_to, dproj, job=None):
    S = proj.shape[0]
    TQ = SB_ROWS
    assert S % TQ == 0 and SB_KEYS == TQ
    scale = SB_HEAD_DIM ** -0.5
    n_pairs = SB_PAIRS_BWD
    pairs = range(n_pairs)
    lanes = [pl.ds(p * SB_BLOCK, SB_BLOCK) for p in pairs]

    def body(q_ref, k_ref, v_ref, do_ref, tot_ref, from_here_ref, up_to_ref, _, dproj_ref, pre, cum, dq_acc, dk_acc, dv_acc, stage, sems):
        dq_ref, dk_ref, dv_ref = stage.at[0], stage.at[1], stage.at[2]
        head0 = lax.broadcasted_iota(jnp.int32, (1, 2 * SB_HEAD_DIM), 1) < SB_HEAD_DIM
        dk_acc[...] = jnp.zeros_like(dk_acc)
        dv_acc[...] = jnp.zeros_like(dv_acc)

        def qloop(qi, _):
            r0 = pl.multiple_of(qi * TQ, TQ)
            q2n = [_attn_rows(q_ref, r0, TQ, lanes[p], head0, -scale) for p in pairs]
            do2 = [_attn_rows(do_ref, r0, TQ, lanes[p], head0) for p in pairs]
            trow = lax.broadcasted_iota(jnp.int32, (TQ, SB_KEYS), 0) + qi * TQ
            t2 = jnp.concatenate([trow, trow], axis=0)
            for p in pairs:
                pre[p, 0:TQ, :] = tot_ref[p, 0, pl.ds(r0, TQ), :]
                pre[p, TQ:2 * TQ, :] = tot_ref[p, 1, pl.ds(r0, TQ), :]
            cum[...] = jnp.zeros_like(cum)
            dq_acc[...] = jnp.zeros_like(dq_acc)

            def step(kj, masked):
                got = [_attn_scores(q2n[p], k_ref, lanes[p], kj, t2, from_here_ref, masked) for p in pairs]
                das = [_dot_nt(do2[p], v_ref[pl.ds(got[p][0], SB_KEYS), lanes[p]].astype(BF)) for p in pairs]
                for p in pairs:
                    c0, kb, zn, valid, lsb, sums = got[p]
                    later = pre[p] - _all_lanes(sums[:, 0:1], SB_BLOCK)
                    pre[p] = later
                    e = sums - zn + jnp.concatenate([later, later], axis=1)
                    sig = jnp.exp(lsb - zn)
                    if masked:
                        e = jnp.where(valid, e, -jnp.inf)
                        sig = jnp.where(valid, sig, 0.0)
                    a = jnp.exp(e)
                    w = das[p] * a
                    upto = _dot(w.astype(BF), up_to_ref[...])
                    c = cum[p]
                    dz = w - sig * (upto + jnp.concatenate([c, c], axis=1))
                    cum[p] = c + _all_lanes(upto[:, SB_KEYS - 1:SB_KEYS], SB_BLOCK)
                    dzb = dz.astype(BF)
                    dq_acc[p] += _dot(dzb, kb)
                    dk_acc[pl.ds(c0, SB_KEYS), lanes[p]] += _dot_tn(dzb, q2n[p])
                    dv_acc[pl.ds(c0, SB_KEYS), lanes[p]] += _dot_tn(a.astype(BF), do2[p])

            def below(n, _):
                step(2 * n, False)
                step(2 * n + 1, False)
                return 0

            lax.fori_loop(0, qi // 2, below, 0)

            @pl.when(qi % 2 == 1)
            def _():
                step(qi - 1, False)
                step(qi, True)

            @pl.when(qi % 2 == 0)
            def _():
                step(qi, True)
            for p in pairs:
                dq_ref[pl.ds(r0, TQ), lanes[p]] = (jnp.where(head0, dq_acc[p, 0:TQ, :], dq_acc[p, TQ:2 * TQ, :]) * scale).astype(BF)
            return 0

        lax.fori_loop(0, S // TQ, qloop, 0)
        dk_ref[...] = (-dk_acc[...]).astype(BF)
        dv_ref[...] = dv_acc[...].astype(BF)
        lane0 = pl.program_id(0) * (n_pairs * SB_BLOCK)
        _put_columns([dq_ref, dk_ref, dv_ref], dproj_ref, [WIDTH * n + lane0 for n in range(3)], sems)

    col = _attn_specs(S, n_pairs)
    whole = lambda a: pl.BlockSpec(a.shape, lambda p: (0, 0))
    hbm = pl.BlockSpec(memory_space=pl.ANY)
    state = pltpu.VMEM((n_pairs, 2 * TQ, SB_BLOCK), F32)
    grads = pltpu.VMEM((S, n_pairs * SB_BLOCK), F32)
    return _hosted(
        body, job, (proj, proj, proj, d_o, tot, from_here, up_to, dproj), name="attn_bwd",
        grid=(WIDTH // (n_pairs * SB_BLOCK),),
        in_specs=[col(0), col(4), col(8), col(0), pl.BlockSpec((n_pairs, 2, S, 128), lambda p: (p, 0, 0, 0)), whole(from_here),
                  whole(up_to), hbm],
        out_specs=[hbm],
        out_shape=[jax.ShapeDtypeStruct(dproj.shape, dproj.dtype)],
        scratch_shapes=[state, state, state, grads, grads, pltpu.VMEM((3, S, n_pairs * SB_BLOCK), BF), pltpu.SemaphoreType.DMA((3,))],
        semantics=("arbitrary",), aliases={7: 0})


HG_LEVELS = tuple(HG_CHUNK >> n for n in range(1, HG_CHUNK.bit_length()))


def _hgrn_consts():
    C = HG_CHUNK
    t = np.arange(C)[:, None]
    s = np.arange(C)[None, :]
    rows = [(s <= t), (s > t)]
    masks = [(t == s)]
    for m in HG_LEVELS:
        two = 2 * m
        mid = (t // two) * two + m
        right = (t % two) >= m
        rows.append((right & (s >= mid) & (s <= t)) | ((~right) & (s > t) & (s <= mid - 1)))
        masks.append(((t // two) == (s // two)) & right & ((s % two) < m))
    tri = np.concatenate(rows, axis=0).astype(np.float32)
    twice = lambda a: jnp.asarray(np.concatenate([a, a], axis=1), BF)
    return (twice(tri), twice(tri.T), jnp.asarray(np.stack(masks).astype(np.float32), F32))


HG_SUM_BLOCKS = 2 + len(HG_LEVELS)


def _split_rows(g):
    hi = g.astype(BF)
    return jnp.concatenate([hi, (g - hi.astype(F32)).astype(BF)], axis=0)


def _hgrn_sum_blocks(e):
    C = HG_CHUNK
    blocks = tuple(e[n * C:(n + 1) * C] for n in range(HG_SUM_BLOCKS))
    return blocks + (jnp.broadcast_to(e[C - 1:C], (HG_DIM, e.shape[1])),)


@jax.custom_vjp
def _hgrn_sums(tri, tri_t, g):
    return _hgrn_sum_blocks(_dot(tri, _split_rows(g)))


def _hgrn_sums_fwd(tri, tri_t, g):
    return _hgrn_sums(tri, tri_t, g), (tri, tri_t)


def _hgrn_sums_bwd(res, ds):
    tri, tri_t = res
    C = HG_CHUNK
    last = lax.broadcasted_iota(jnp.int32, (C, 1), 0) == C - 1
    prefix = ds[0] + jnp.where(last, jnp.sum(ds[-1], axis=0, keepdims=True), 0.0)
    d = jnp.concatenate((prefix,) + tuple(ds[1:-1]), axis=0)
    return jnp.zeros_like(tri), jnp.zeros_like(tri_t), _dot(tri_t[:, :d.shape[0]], d.astype(BF))


_hgrn_sums.defvjp(_hgrn_sums_fwd, _hgrn_sums_bwd)


def _bf_dot(a, b):
    return _dot(a.astype(BF), b.astype(BF))


def _bf_dot_nt(a, b):
    return _dot_nt(a.astype(BF), b.astype(BF))


def _bf_dot_tn(a, b):
    return _dot_tn(a.astype(BF), b.astype(BF))


@jax.custom_vjp
def _mm(a, b):
    return _bf_dot(a, b)


_mm.defvjp(lambda a, b: (_bf_dot(a, b), (a, b)), lambda r, ct: (_bf_dot_nt(ct, r[1]), _bf_dot_tn(r[0], ct)))


@jax.custom_vjp
def _mm_nt(a, b):
    return _bf_dot_nt(a, b)


_mm_nt.defvjp(lambda a, b: (_bf_dot_nt(a, b), (a, b)), lambda r, ct: (_bf_dot(ct, r[1]), _bf_dot_tn(ct, r[0])))


@jax.custom_vjp
def _mm_tn(a, b):
    return _bf_dot_tn(a, b)


_mm_tn.defvjp(lambda a, b: (_bf_dot_tn(a, b), (a, b)), lambda r, ct: (_bf_dot_nt(r[1], ct), _bf_dot(r[0], ct)))


def _hgrn_gates(tri, tri_t, qraw, fpre, lb):
    q = _silu(qraw)
    f = lb + (1.0 - lb) * _sigmoid(fpre)
    return q, 1.0 - f, _hgrn_sums(tri, tri_t, jnp.log(f))


def _hgrn_mix(masks, q, k, e, v, st):
    prefix, suffix, whole = e[0], e[1], e[-1]
    scores = masks[0] * _mm_nt(q, k)
    for n in range(len(HG_LEVELS)):
        decay = jnp.exp(e[2 + n])
        scores = scores + masks[n + 1] * _mm_nt(q * decay, k * decay)
    o = _mm_nt(q * jnp.exp(prefix), st) + _mm(scores, v)
    st_new = st * jnp.exp(whole) + _mm_tn(v, k * jnp.exp(suffix))
    return o, st_new


def _hgrn_chunk(tri, tri_t, masks, qraw, fpre, v, st, lb):
    q, k, e = _hgrn_gates(tri, tri_t, qraw, fpre, lb)
    return _hgrn_mix(masks, q, k, e, v, st)


HG_HEADS_PER_STEP = 4
HG_LANES = HG_HEADS_PER_STEP * HG_DIM


def _hgrn_specs(S, consts):
    col = lambda base: pl.BlockSpec((S, HG_LANES), lambda p, base=base: (0, base // HG_HEADS_PER_STEP + p))
    whole = [pl.BlockSpec(a.shape, lambda p, n=a.ndim: (0,) * n) for a in consts]
    return col, whole


def _hgrn_fwd(proj, lbs, consts, job=None):
    S = proj.shape[0]
    nc = S // HG_CHUNK
    heads = range(HG_HEADS_PER_STEP)

    def body(q_ref, f_ref, i_ref, lb_ref, tri_ref, trit_ref, mask_ref, o_ref, st_ref):
        tri, tri_t = tri_ref[...], trit_ref[...]
        masks = [mask_ref[n] for n in range(len(HG_LEVELS) + 1)]

        def chunk(ci, sts):
            r0 = pl.multiple_of(ci * HG_CHUNK, HG_CHUNK)
            rows = pl.ds(r0, HG_CHUNK)
            new = []
            lane = [pl.ds(hd * HG_DIM, HG_DIM) for hd in heads]
            gates = [_hgrn_gates(tri, tri_t, q_ref[rows, lane[hd]], f_ref[rows, lane[hd]], lb_ref[hd]) for hd in heads]
            for hd in heads:
                st_ref[hd, ci] = sts[hd]
                o, st_new = _hgrn_mix(masks, *gates[hd], i_ref[rows, lane[hd]], sts[hd])
                o_ref[rows, lane[hd]] = o
                new.append(st_new)
            return tuple(new)

        lax.fori_loop(0, nc, chunk, tuple(jnp.zeros((HG_DIM, HG_DIM), F32) for _ in heads))

    col, whole = _hgrn_specs(S, consts)
    return _hosted(
        body, job, (proj, proj, proj, lbs, *consts), name="hgrn_fwd",
        grid=(WIDTH // HG_LANES,),
        in_specs=[col(16), col(20), col(24), pl.BlockSpec((HG_HEADS_PER_STEP, 1, 128), lambda p: (p, 0, 0))] + whole,
        out_specs=[col(0), pl.BlockSpec((HG_HEADS_PER_STEP, nc, HG_DIM, HG_DIM), lambda p: (p, 0, 0, 0))],
        out_shape=[jax.ShapeDtypeStruct((S, WIDTH), F32), jax.ShapeDtypeStruct((WIDTH // 128, nc, HG_DIM, HG_DIM), F32)],
        semantics=("parallel",))


def _hgrn_bwd(proj, lbs, states, d_o, consts, dproj, job=None):
    S = proj.shape[0]
    nc = S // HG_CHUNK

    def body(q_ref, f_ref, i_ref, lb_ref, st_ref, do_ref, tri_ref, trit_ref, mask_ref, _, dlb_ref, dproj_ref, stage, sems):
        dq_ref, df_ref, di_ref = stage.at[0], stage.at[1], stage.at[2]
        masks = [mask_ref[n] for n in range(len(HG_LEVELS) + 1)]
        fn = functools.partial(_hgrn_chunk, tri_ref[...], trit_ref[...], masks)
        heads = range(HG_HEADS_PER_STEP)

        def chunk(n, carry):
            ci = nc - 1 - n
            r0 = pl.multiple_of(ci * HG_CHUNK, HG_CHUNK)
            rows = pl.ds(r0, HG_CHUNK)
            new = []
            lane = [pl.ds(hd * HG_DIM, HG_DIM) for hd in heads]
            pulls = [jax.vjp(fn, q_ref[rows, lane[hd]], f_ref[rows, lane[hd]], i_ref[rows, lane[hd]], st_ref[hd, ci], lb_ref[hd])[1]
                     for hd in heads]
            for hd in heads:
                d_st, dlb = carry[hd]
                lanes = lane[hd]
                dq, df, di, d_prev, dl = pulls[hd]((do_ref[rows, lanes], d_st))
                dq_ref[rows, lanes] = dq.astype(BF)
                df_ref[rows, lanes] = df.astype(BF)
                di_ref[rows, lanes] = di.astype(BF)
                new.append((d_prev, dlb + dl))
            return tuple(new)

        zero = (jnp.zeros((HG_DIM, HG_DIM), F32), jnp.zeros((1, HG_DIM), F32))
        done = lax.fori_loop(0, nc, chunk, tuple(zero for _ in heads))
        for hd in heads:
            dlb_ref[hd] = done[hd][1]
        lane0 = pl.program_id(0) * HG_LANES
        _put_columns([dq_ref, df_ref, di_ref], dproj_ref, [WIDTH * n + lane0 for n in (4, 5, 6)], sems)

    col, whole = _hgrn_specs(S, consts)
    head = pl.BlockSpec((HG_HEADS_PER_STEP, 1, 128), lambda p: (p, 0, 0))
    hbm = pl.BlockSpec(memory_space=pl.ANY)
    n_in = 6 + len(consts)
    return _hosted(
        body, job, (proj, proj, proj, lbs, states, d_o, *consts, dproj), name="hgrn_bwd",
        grid=(WIDTH // HG_LANES,),
        in_specs=[col(16), col(20), col(24), head, pl.BlockSpec((HG_HEADS_PER_STEP, nc, HG_DIM, HG_DIM), lambda p: (p, 0, 0, 0)), col(0)]
        + whole + [hbm],
        out_specs=[head, hbm],
        out_shape=[jax.ShapeDtypeStruct((WIDTH // 128, 1, 128), F32), jax.ShapeDtypeStruct(dproj.shape, dproj.dtype)],
        scratch_shapes=[pltpu.VMEM((3, S, HG_LANES), BF), pltpu.SemaphoreType.DMA((3,))],
        semantics=("arbitrary",), aliases={n_in: 1})


def _shift_down(x, n):
    rows = lax.broadcasted_iota(jnp.int32, x.shape, 0)
    return jnp.where(rows >= n, pltpu.roll(x, n, 0), 0.0)


def _shift_up(x, n):
    S = x.shape[0]
    rows = lax.broadcasted_iota(jnp.int32, x.shape, 0)
    return jnp.where(rows < S - n, pltpu.roll(x, S - n, 0), 0.0)


def _branch_fwd(proj, o_a, o_b, norm_w, conv_w, layer):
    S = proj.shape[0]

    def body(oa_ref, za_ref, ob_ref, zb_ref, nw_ref, pre_ref, post_ref, u_ref, zc_ref, cw_ref, ya_ref, yb_ref, yc_ref):
        ya_ref[...] = (oa_ref[...] * _silu(za_ref[...])).astype(BF)
        ob = ob_ref[...]
        rn = lax.rsqrt(jnp.mean(ob * ob, axis=-1, keepdims=True) + RMS_EPS)
        yb_ref[...] = (ob * rn * nw_ref[layer:layer + 1, :] * _silu(zb_ref[...])).astype(BF)
        pu = pre_ref[...] * u_ref[...]
        conv = cw_ref[2:3, :] * pu + cw_ref[1:2, :] * _shift_down(pu, 1) + cw_ref[0:1, :] * _shift_down(pu, 2)
        yc_ref[...] = (post_ref[...] * conv * _silu(zc_ref[...])).astype(BF)

    col = lambda base: pl.BlockSpec((S, 128), lambda p, base=base: (0, base + p))
    out = jax.ShapeDtypeStruct((S, WIDTH), BF)
    return pl.pallas_call(
        body, name="branch_fwd",
        grid=(WIDTH // 128,),
        in_specs=[col(0), col(12), col(0), col(28), pl.BlockSpec(norm_w.shape, lambda p: (0, 0)),
                  col(32), col(36), col(40), col(44), pl.BlockSpec((None, None, 3, 128), lambda p: (p, layer, 0, 0))],
        out_specs=[col(0), col(0), col(0)],
        out_shape=[out, out, out],
        compiler_params=_params(dimension_semantics=("parallel",)),
    )(o_a, proj, o_b, proj, norm_w, proj, proj, proj, proj, conv_w)


def _put_columns(tiles, dproj_ref, firsts, sems):
    copies = [pltpu.make_async_copy(t, dproj_ref.at[:, pl.ds(pl.multiple_of(c, 128), t.shape[1])], sems.at[n])
              for n, (t, c) in enumerate(zip(tiles, firsts))]
    for cp in copies:
        cp.start()
    for cp in copies:
        cp.wait()


def _branch_bwd(proj, o_a, o_b, norm_w, conv_w, dy_a, dy_b, dy_c, dproj, layer):
    S = proj.shape[0]
    firsts = [WIDTH * n for n in (3, 7, 8, 9, 10, 11)]

    def dsilu(z):
        s = _sigmoid(z)
        return s * z, s * (1.0 + z * (1.0 - s))

    def body(oa_ref, za_ref, ob_ref, zb_ref, nw_ref, pre_ref, post_ref, u_ref, zc_ref, cw_ref, dya_ref, dyb_ref, dyc_ref, _,
             doa_ref, dob_ref, dnw_ref, dcw_ref, dproj_ref, stage, sems):
        dza_ref, dzb_ref, dpre_ref, dpost_ref, du_ref, dzc_ref = [stage.at[n] for n in range(6)]
        dya = dya_ref[...]
        sa, dsa = dsilu(za_ref[...])
        doa_ref[...] = dya * sa
        dza_ref[...] = (dya * oa_ref[...] * dsa).astype(BF)

        dyb = dyb_ref[...]
        ob = ob_ref[...]
        nw = nw_ref[layer:layer + 1, :]
        sb, dsb = dsilu(zb_ref[...])
        rn = lax.rsqrt(jnp.mean(ob * ob, axis=-1, keepdims=True) + RMS_EPS)
        on = ob * rn
        dzb_ref[...] = (dyb * on * nw * dsb).astype(BF)
        don_w = dyb * sb
        dnw_ref[...] = jnp.sum(don_w * on, axis=0, keepdims=True)
        don = don_w * nw
        dob_ref[...] = rn * (don - on * jnp.mean(don * on, axis=-1, keepdims=True))

        dyc = dyc_ref[...]
        pre, post, u = pre_ref[...], post_ref[...], u_ref[...]
        sc, dsc = dsilu(zc_ref[...])
        pu = pre * u
        pu1, pu2 = _shift_down(pu, 1), _shift_down(pu, 2)
        conv = cw_ref[2:3, :] * pu + cw_ref[1:2, :] * pu1 + cw_ref[0:1, :] * pu2
        dzc_ref[...] = (dyc * post * conv * dsc).astype(BF)
        dpost_ref[...] = (dyc * conv * sc).astype(BF)
        dconv = dyc * post * sc
        dcw_ref[0:1, :] = jnp.sum(dconv * pu2, axis=0, keepdims=True)
        dcw_ref[1:2, :] = jnp.sum(dconv * pu1, axis=0, keepdims=True)
        dcw_ref[2:3, :] = jnp.sum(dconv * pu, axis=0, keepdims=True)
        dpu = cw_ref[2:3, :] * dconv + cw_ref[1:2, :] * _shift_up(dconv, 1) + cw_ref[0:1, :] * _shift_up(dconv, 2)
        dpre_ref[...] = (dpu * u).astype(BF)
        du_ref[...] = (dpu * pre).astype(BF)
        lane0 = pl.program_id(0) * 128
        _put_columns([stage.at[n] for n in range(6)], dproj_ref, [c + lane0 for c in firsts], sems)

    col = lambda base: pl.BlockSpec((S, 128), lambda p, base=base: (0, base + p))
    f32 = jax.ShapeDtypeStruct((S, WIDTH), F32)
    hbm = pl.BlockSpec(memory_space=pl.ANY)
    return pl.pallas_call(
        body, name="branch_bwd",
        grid=(WIDTH // 128,),
        in_specs=[col(0), col(12), col(0), col(28), pl.BlockSpec(norm_w.shape, lambda p: (0, 0)),
                  col(32), col(36), col(40), col(44), pl.BlockSpec((None, None, 3, 128), lambda p: (p, layer, 0, 0)),
                  col(0), col(0), col(0), hbm],
        out_specs=[col(0), col(0), pl.BlockSpec((None, 1, 128), lambda p: (p, 0, 0)), pl.BlockSpec((None, 3, 128), lambda p: (p, 0, 0)), hbm],
        out_shape=[f32, f32, jax.ShapeDtypeStruct((WIDTH // 128, 1, 128), F32), jax.ShapeDtypeStruct((WIDTH // 128, 3, 128), F32),
                   jax.ShapeDtypeStruct(dproj.shape, dproj.dtype)],
        scratch_shapes=[pltpu.VMEM((6, S, 128), BF), pltpu.SemaphoreType.DMA((6,))],
        input_output_aliases={13: 4},
        compiler_params=_params(dimension_semantics=("arbitrary",)),
    )(o_a, proj, o_b, proj, norm_w, proj, proj, proj, proj, conv_w, dy_a, dy_b, dy_c, dproj)


def _branch_proj(y_refs, wb_ref):
    return [_dot(y_refs[i][...], wb_ref[i]) for i in range(3)]


def _merge_fwd(x, mod, proj, ys, wb, wo, ln_g, ln_b, layer, job=None, tm=512):
    S, D = x.shape
    tm = min(tm, S)

    def body(x_ref, mod_ref, ga_ref, gb_ref, gc_ref, ya_ref, yb_ref, yc_ref, wb_ref, wo_ref, g_ref, b_ref, xo_ref, mg_ref, y_ref):
        ps = _branch_proj((ya_ref, yb_ref, yc_ref), wb_ref)
        merged = _sigmoid(ga_ref[...]) * ps[0] + _sigmoid(gb_ref[...]) * ps[1] + _sigmoid(gc_ref[...]) * ps[2]
        mb = merged.astype(BF)
        mg_ref[...] = mb
        y = _dot(mb, wo_ref[...].reshape(D, D))
        y_ref[...] = y
        r = ALPHA * x_ref[...] + (1.0 + mod_ref[:, 2 * D:3 * D]) * y
        xn, _ = _standardize(r)
        xo_ref[...] = xn * g_ref[layer:layer + 1, :] + b_ref[layer:layer + 1, :]

    row = lambda w, c=0: pl.BlockSpec((tm, w), lambda i, c=c: (i, c))
    whole = lambda a: pl.BlockSpec(a.shape, lambda i, n=a.ndim: (0,) * n)
    return _hosted(
        body, job, (x, mod, proj, proj, proj, *ys, wb, wo, ln_g, ln_b), name="merge_fwd",
        grid=(S // tm,),
        in_specs=[row(D), whole(mod), row(D, 6), row(D, 7), row(D, 8), row(WIDTH), row(WIDTH), row(WIDTH), whole(wb),
                  whole(wo), whole(ln_g), whole(ln_b)],
        out_specs=[row(D), row(D), row(D)],
        out_shape=[jax.ShapeDtypeStruct((S, D), F32), jax.ShapeDtypeStruct((S, D), BF), jax.ShapeDtypeStruct((S, D), F32)],
        semantics=("parallel",))


def _merge_bwd(dxo, x, y, merged, mod, proj, ys, wb, wo, ln_g, layer, tm=256):
    S, D = x.shape
    tm = min(tm, S)
    steps = S // tm
    quarter = D // N_CHIPS

    def body(dxo_ref, x_ref, y_ref, mg_ref, mod_ref, ga_ref, gb_ref, gc_ref, ya_ref, yb_ref, yc_ref, wb_ref, wo_ref, g_ref,
             dxr_ref, dg_ref, dya_ref, dyb_ref, dyc_ref, dlg_ref, dlb_ref, dgt_ref, gbr_ref, gout_ref, acc_br, acc_out):
        @pl.when(pl.program_id(0) == 0)
        def _():
            dlg_ref[...] = jnp.zeros_like(dlg_ref)
            dlb_ref[...] = jnp.zeros_like(dlb_ref)
            dgt_ref[...] = jnp.zeros_like(dgt_ref)
            acc_br[...] = jnp.zeros_like(acc_br)
            acc_out[...] = jnp.zeros_like(acc_out)

        gate1 = 1.0 + mod_ref[:, 2 * D:3 * D]
        yv = y_ref[...]
        xn, rstd = _standardize(ALPHA * x_ref[...] + gate1 * yv)
        dxo = dxo_ref[...]
        dlg_ref[...] += jnp.sum(dxo * xn, axis=0, keepdims=True)
        dlb_ref[...] += jnp.sum(dxo, axis=0, keepdims=True)
        dr = _standardize_bwd(dxo * g_ref[layer:layer + 1, :], xn, rstd)
        dxr_ref[...] = ALPHA * dr
        dgt_ref[...] += jnp.sum(dr * yv, axis=0, keepdims=True)
        dyb = (gate1 * dr).astype(BF)
        acc_out[...] += _dot_tn(mg_ref[...], dyb)
        dmerged = _dot_nt(dyb, wo_ref[...].reshape(D, D))
        y_refs = (ya_ref, yb_ref, yc_ref)
        ps = _branch_proj(y_refs, wb_ref)
        for i, (gate_ref, out_ref) in enumerate(((ga_ref, dya_ref), (gb_ref, dyb_ref), (gc_ref, dyc_ref))):
            sg = _sigmoid(gate_ref[...])
            dg_ref[:, i * D:(i + 1) * D] = (dmerged * ps[i] * sg * (1.0 - sg)).astype(BF)
            dp = (dmerged * sg).astype(BF)
            acc_br[i] += _dot_tn(y_refs[i][...], dp)
            out_ref[...] = _dot_nt(dp, wb_ref[i])

        @pl.when(pl.program_id(0) == steps - 1)
        def _():
            for j in range(N_CHIPS):
                gout_ref[j] = acc_out[j * quarter:(j + 1) * quarter, :].astype(BF)
                for i in range(3):
                    gbr_ref[j, i] = acc_br[i, :, j * quarter:(j + 1) * quarter].astype(BF)

    row = lambda w, c=0: pl.BlockSpec((tm, w), lambda i, c=c: (i, c))
    whole = lambda a: pl.BlockSpec(a.shape, lambda i, n=len(a.shape): (0,) * n)
    vec = pl.BlockSpec((1, D), lambda i: (0, 0))
    sd = jax.ShapeDtypeStruct
    g_br, g_out = sd((N_CHIPS, 3, WIDTH, quarter), BF), sd((N_CHIPS, quarter, D), BF)
    return pl.pallas_call(
        body, name="merge_bwd",
        grid=(steps,),
        in_specs=[row(D), row(D), row(D), row(D), whole(mod), row(D, 6), row(D, 7), row(D, 8), row(WIDTH), row(WIDTH), row(WIDTH),
                  whole(wb), whole(wo), whole(ln_g)],
        out_specs=[row(D), row(3 * D, IN_COLS // (3 * D) - 1), row(WIDTH), row(WIDTH), row(WIDTH), vec, vec, vec, whole(g_br), whole(g_out)],
        out_shape=[sd((S, D), F32), sd((S, IN_COLS), BF), sd((S, WIDTH), F32), sd((S, WIDTH), F32), sd((S, WIDTH), F32),
                   sd((1, D), F32), sd((1, D), F32), sd((1, D), F32), g_br, g_out],
        scratch_shapes=[pltpu.VMEM((3, WIDTH, D), F32), pltpu.VMEM((D, D), F32)],
        compiler_params=_params(dimension_semantics=("arbitrary",)),
    )(dxo, x, y, merged, mod, proj, proj, proj, *ys, wb, wo, ln_g)


def _loss_head(x, target, tm=512):
    S, D = x.shape
    tm = min(tm, S)

    def body(x_ref, t_ref, dx_ref, loss_ref):
        @pl.when(pl.program_id(0) == 0)
        def _():
            loss_ref[...] = jnp.zeros_like(loss_ref)

        err = x_ref[...] - t_ref[...]
        dx_ref[...] = err * (1.0 / D)
        loss_ref[...] += 0.5 * jnp.sum(jnp.mean(err * err, axis=-1, keepdims=True))

    row = pl.BlockSpec((tm, D), lambda i: (i, 0))
    return pl.pallas_call(
        body, name="loss_head",
        grid=(S // tm,),
        in_specs=[row, row],
        out_specs=[row, pl.BlockSpec((8, 128), lambda i: (0, 0))],
        out_shape=[jax.ShapeDtypeStruct((S, D), F32), jax.ShapeDtypeStruct((8, 128), F32)],
        compiler_params=_params(dimension_semantics=("arbitrary",)),
    )(x, target)


def _proj_bwd(dproj, wgs, x, mod, dx_res, job=None, tm=512, tk=768):
    S, D = x.shape
    tm = min(tm, S)
    n = len(wgs)
    w_args = list(wgs)
    if n == 1:
        per = wgs[0].shape[-1] // tk
        w_specs = [pl.BlockSpec((None, D, tk), lambda k, i: (k // per, 0, k % per))]
    else:
        assert wgs[0].shape[-1] == tk
        w_specs = [pl.BlockSpec((None, D, tk), lambda k, i, c=c: (jnp.minimum((k + n - 1 - c) // n, N_CHIPS - 1), 0, 0))
                   for c in range(n)]
    nk = IN_COLS // tk

    def body(dp_ref, *rest):
        w_refs, (x_ref, mod_ref, dxr_ref, dx_ref, dsh_ref, dsc_ref, acc) = rest[:n], rest[n:]
        k, i = pl.program_id(0), pl.program_id(1)
        mine = pl.ds(pl.multiple_of(i * tm, tm), tm)

        @pl.when((i == 0) & (k == 0))
        def _():
            dsh_ref[...] = jnp.zeros_like(dsh_ref)
            dsc_ref[...] = jnp.zeros_like(dsc_ref)

        @pl.when(k == 0)
        def _():
            acc[mine, :] = jnp.zeros((tm, D), F32)

        for c in range(n):
            @pl.when(k % n == c)
            def _(c=c):
                acc[mine, :] += _dot_nt(dp_ref[...], w_refs[c][...])

        @pl.when(k == nk - 1)
        def _():
            dh = acc[mine, :]
            xs, rstd = _standardize(x_ref[...])
            dsh_ref[...] += jnp.sum(dh, axis=0, keepdims=True)
            dsc_ref[...] += jnp.sum(dh * xs, axis=0, keepdims=True)
            dx_ref[...] = _standardize_bwd(dh * (1.0 + mod_ref[:, D:2 * D]), xs, rstd) + dxr_ref[...]

    row = pl.BlockSpec((tm, D), lambda k, i: (jnp.where(k == nk - 1, i, 0), 0))
    vec = pl.BlockSpec((1, D), lambda k, i: (0, 0))
    return _hosted(
        body, job, (dproj, *w_args, x, mod, dx_res), name="proj_bwd",
        grid=(nk, S // tm),
        in_specs=[pl.BlockSpec((tm, tk), lambda k, i: (i, k))] + w_specs + [row, pl.BlockSpec((1, 3 * D), lambda k, i: (0, 0)), row],
        out_specs=[row, vec, vec],
        out_shape=[jax.ShapeDtypeStruct((S, D), F32), jax.ShapeDtypeStruct((1, D), F32), jax.ShapeDtypeStruct((1, D), F32)],
        scratch_shapes=[pltpu.VMEM((S, D), F32)],
        semantics=("arbitrary", "arbitrary"))


def _grad_w_in(h, dproj):
    S, D = h.shape
    shard = IN_COLS // N_CHIPS

    def body(h_ref, d_ref, o_ref):
        o_ref[...] = _dot_tn(h_ref[...], d_ref[...]).astype(BF)

    return pl.pallas_call(
        body, name="grad_w_in",
        grid=(N_CHIPS,),
        in_specs=[pl.BlockSpec((S, D), lambda n: (0, 0)), pl.BlockSpec((S, shard), lambda n: (0, n))],
        out_specs=pl.BlockSpec((None, D, shard), lambda n: (n, 0, 0)),
        out_shape=jax.ShapeDtypeStruct((N_CHIPS, D, shard), BF),
        compiler_params=_params(dimension_semantics=("parallel",)),
    )(h, dproj)


def _all_gather8(x, name):
    R, N = x.shape

    def body(x_ref, out_ref, send_sems, recv_sems):
        mx, my, mc = lax.axis_index("x"), lax.axis_index("y"), lax.axis_index("c")
        me = 4 * mx + 2 * my + mc
        out_ref[me] = x_ref[...]
        copies = []
        for k in range(1, N_DEV):
            peer = (_flip(mx, k & 4), _flip(my, k & 2), _flip(mc, k & 1))
            cp = pltpu.make_async_remote_copy(src_ref=x_ref, dst_ref=out_ref.at[me], send_sem=send_sems.at[k - 1],
                                              recv_sem=recv_sems.at[k - 1], device_id=peer, device_id_type=MESH)
            cp.start()
            copies.append(cp)
        for cp in copies:
            cp.wait()

    return pl.pallas_call(
        body, name=name,
        in_specs=[pl.BlockSpec(memory_space=pltpu.VMEM)],
        out_specs=pl.BlockSpec(memory_space=pltpu.VMEM),
        out_shape=jax.ShapeDtypeStruct((N_DEV, R, N), F32),
        scratch_shapes=[pltpu.SemaphoreType.DMA((N_DEV - 1,)), pltpu.SemaphoreType.DMA((N_DEV - 1,))],
        compiler_params=_params(),
    )(x)


def _rows2d(a):
    return a.reshape(-1, a.shape[-1])


def _tile_rows(rows, cols, n_arrays):
    budget = (24 << 20) // (n_arrays * 2 * 4 * cols)
    if rows <= budget:
        return rows
    tm = 8
    for cand in range(8, budget + 1, 8):
        if rows % cand == 0:
            tm = cand
    return tm


SUM_ROWS = 256


def _sum_cores(g, sent, where):
    chips, lead, _, r, cols = g.shape
    tr = min(r, SUM_ROWS)

    def body(where_ref, g_ref, s_ref, out_ref):
        out_ref[...] = (g_ref[...].astype(F32) + s_ref[...].astype(F32)).astype(BF)

    spec = pl.BlockSpec((None, tr, cols), lambda i, j, where_ref: (i, j, 0))
    out = pl.pallas_call(
        body, name="sum_cores",
        grid_spec=pltpu.PrefetchScalarGridSpec(
            num_scalar_prefetch=1, grid=(chips * lead, r // tr),
            in_specs=[pl.BlockSpec((None, None, tr, cols), lambda i, j, where_ref: (i, where_ref[1], j, 0)), spec],
            out_specs=spec),
        out_shape=jax.ShapeDtypeStruct((chips * lead, r, cols), BF),
        compiler_params=_params(dimension_semantics=("parallel", "parallel")),
    )(where, g.reshape(chips * lead, 2, r, cols), sent.reshape(chips * lead, r, cols))
    return out.reshape(chips, lead, r, cols)


def _sum_chips(sums, got, where):
    _, lead, r, cols = sums.shape
    tr = min(r, SUM_ROWS)

    def body(where_ref, s_ref, g_ref, out_ref):
        out_ref[...] = ((s_ref[...].astype(F32) + g_ref[0].astype(F32)) + g_ref[1].astype(F32)) + g_ref[2].astype(F32)

    return pl.pallas_call(
        body, name="sum_chips",
        grid_spec=pltpu.PrefetchScalarGridSpec(
            num_scalar_prefetch=1, grid=(lead, r // tr),
            in_specs=[pl.BlockSpec((None, None, tr, cols), lambda i, j, where_ref: (where_ref[0], i, j, 0)),
                      pl.BlockSpec((N_CHIPS - 1, None, tr, cols), lambda i, j, where_ref: (0, i, j, 0))],
            out_specs=pl.BlockSpec((None, None, tr, cols), lambda i, j, where_ref: (i, where_ref[1], j, 0))),
        out_shape=jax.ShapeDtypeStruct((lead, 2, r, cols), F32),
        compiler_params=_params(dimension_semantics=("parallel", "parallel")),
    )(where, sums, got)


def _adamw(w, m, v, groups, name):
    shape = w.shape
    w2, m2, v2 = _rows2d(w), _rows2d(m), _rows2d(v)
    rows, cols = w2.shape
    ng = len(groups)
    n = len(groups[0])
    slab = rows // ng
    gs = [_rows2d(g) for grp in groups for g in grp]
    tm = _tile_rows(slab, cols, 7 + n)
    tiles = slab // tm
    c1 = 1.0 / (1.0 - ADAM_B1 ** ADAM_STEP)
    c2 = 1.0 / (1.0 - ADAM_B2 ** ADAM_STEP)

    def body(*refs):
        w_ref, m_ref, v_ref = refs[:3]
        g_refs = refs[3:3 + ng * n]
        go_ref, d_ref, mo_ref, vo_ref = refs[3 + ng * n:]
        which = pl.program_id(0)
        for s in range(ng):
            @pl.when(which == s)
            def _(s=s):
                g = g_refs[s * n][...]
                for r in g_refs[s * n + 1:(s + 1) * n]:
                    g = g + r[...]
                mn = ADAM_B1 * m_ref[...] + (1.0 - ADAM_B1) * g
                vn = ADAM_B2 * v_ref[...] + (1.0 - ADAM_B2) * (g * g)
                go_ref[...] = g
                mo_ref[...] = mn
                vo_ref[...] = vn
                d_ref[...] = -ADAM_LR * ((mn * c1) / (jnp.sqrt(vn * c2) + ADAM_EPS) + ADAM_WD * w_ref[...])

    spec = pl.BlockSpec((tm, cols), lambda s, i: (s * tiles + i, 0))
    g_specs = [pl.BlockSpec((tm, cols), lambda s, i, k=k: (jnp.where(s == k, i, jnp.where(s < k, 0, tiles - 1)), 0))
               for k in range(ng) for _ in range(n)]
    outs = pl.pallas_call(
        body, name=name,
        grid=(ng, tiles),
        in_specs=[spec] * 3 + g_specs,
        out_specs=[spec] * 4,
        out_shape=[jax.ShapeDtypeStruct((rows, cols), F32)] * 4,
        compiler_params=_params(dimension_semantics=("arbitrary", "arbitrary")),
    )(w2, m2, v2, *gs)
    return [o.reshape(shape) for o in outs]


def _lower_bounds(r0, r1):
    top = jnp.maximum(r0, r1)
    e0, e1 = jnp.exp(r0 - top), jnp.exp(r1 - top)
    p0, p1 = e0 / (e0 + e1), e1 / (e0 + e1)
    return p0 - p0, (p0 + p1) - p0


def _lbs_fwd(lb_raw):
    def body(lb_ref, out_ref):
        l0, l1 = _lower_bounds(lb_ref[0:1, :], lb_ref[1:2, :])
        out_ref[0:1, :] = l0
        out_ref[1:2, :] = l1

    return pl.pallas_call(body, name="lower_bounds", out_shape=jax.ShapeDtypeStruct(lb_raw.shape, F32), compiler_params=_params())(lb_raw)


def _mod_rows(c_all, w_mod, tn=768):
    _, D, cols = w_mod.shape

    def body(c_ref, w_ref, out_ref):
        out_ref[...] = _dot(c_ref[...].astype(BF), w_ref[...].astype(BF))

    return pl.pallas_call(
        body, name="mod_rows",
        grid=(DEPTH,),
        in_specs=[pl.BlockSpec((N_DEV, D), lambda l: (0, 0)), pl.BlockSpec((None, D, cols), lambda l: (l, 0, 0))],
        out_specs=pl.BlockSpec((N_DEV, cols), lambda l: (0, l)),
        out_shape=jax.ShapeDtypeStruct((N_DEV, DEPTH * cols), F32),
        compiler_params=_params(dimension_semantics=("parallel",)),
    )(c_all, w_mod)


def _grad_w_mod(c_all, dmod_cols):
    D = c_all.shape[1]
    cols = dmod_cols.shape[-1]

    def body(c_ref, d_ref, out_ref):
        out_ref[...] = _dot_tn(c_ref[...].astype(BF), d_ref[...].astype(BF))

    return pl.pallas_call(
        body, name="grad_w_mod",
        grid=(DEPTH,),
        in_specs=[pl.BlockSpec((N_DEV, D), lambda l: (0, 0)), pl.BlockSpec((None, N_DEV, cols), lambda l: (l, 0, 0))],
        out_specs=pl.BlockSpec((None, D, cols), lambda l: (l, 0, 0)),
        out_shape=jax.ShapeDtypeStruct((DEPTH, D, cols), F32),
        compiler_params=_params(dimension_semantics=("parallel",)),
    )(c_all, dmod_cols)


def _sum_devices(parts):
    _, R, N = parts.shape

    def body(p_ref, out_ref):
        acc = p_ref[0]
        for d in range(1, N_DEV):
            acc = acc + p_ref[d]
        out_ref[...] = acc

    return pl.pallas_call(body, name="sum_devices", out_shape=jax.ShapeDtypeStruct((R, N), F32), compiler_params=_params())(parts)


def _lbs_bwd(lb_raw, dl):
    def body(lb_ref, dl_ref, out_ref):
        _, pull = jax.vjp(_lower_bounds, lb_ref[0:1, :], lb_ref[1:2, :])
        d0, d1 = pull((dl_ref[0:1, :], dl_ref[1:2, :]))
        out_ref[0:1, :] = d0
        out_ref[1:2, :] = d1

    return pl.pallas_call(body, name="lower_bounds_bwd", out_shape=jax.ShapeDtypeStruct(lb_raw.shape, F32), compiler_params=_params())(lb_raw, dl)


def kernel(x, c, w_mod, b_mod, w_in, conv_w, hgrn_norm_w, lower_bounds, w_branch, w_out, ln_g, ln_b, loss_target, m_w_mod, m_b_mod, m_w_in, m_conv_w, m_hgrn_norm_w, m_lower_bounds, m_w_branch, m_w_out, m_ln_g, m_ln_b, v_w_mod, v_b_mod, v_w_in, v_conv_w, v_hgrn_norm_w, v_lower_bounds, v_w_branch, v_w_out, v_ln_g, v_ln_b):
    D = D_MODEL
    x0 = x[0]
    target = loss_target[0]
    S = x0.shape[0]
    mx, my, mc = lax.axis_index("x"), lax.axis_index("y"), lax.axis_index("c")
    chip = 2 * mx + my
    me = 2 * chip + mc
    mod_cols = 3 * D // N_CHIPS

    plan = _Plan(w_in, w_branch, w_out, chip, mc)
    n_conv = DEPTH * 3 * (WIDTH // N_CHIPS)
    first = jnp.concatenate([c, conv_w.reshape(1, n_conv), jnp.zeros((1, 2 * D - D - n_conv), F32)], axis=1)
    first = plan.first(first.reshape(8, 2 * D // 8)).reshape(N_DEV, 2 * D)

    c_all = first[:, :D]
    conv_all = first[:, D:D + n_conv].reshape(N_DEV, DEPTH, 3, WIDTH // N_CHIPS)[0::2]
    mod_part = _all_gather8(_mod_rows(c_all, w_mod), "gather_mod")[0::2]
    mod_part = lax.dynamic_index_in_dim(mod_part, me, axis=1, keepdims=False).reshape(N_CHIPS, DEPTH, mod_cols)
    mods = [(mod_part[:, l].reshape(1, 3 * D) + b_mod[l][None, :]) for l in range(DEPTH)]
    lbs = _lbs_fwd(lower_bounds).reshape(DEPTH, WIDTH // 128, 1, 128)
    loss_blk, dx, small = _local_step(x0, target, mods, lbs, conv_all, hgrn_norm_w, ln_g, ln_b, plan)

    n_mod, n_nw, n_lb, n_ln, n_cw = DEPTH * 3 * D, DEPTH * 128, DEPTH * WIDTH, DEPTH * D, DEPTH * 3 * WIDTH
    row = jnp.concatenate(
        [jnp.concatenate([small[l][0], small[l][1], small[l][2]], axis=1) for l in range(DEPTH)]
        + [jnp.sum(small[l][3], axis=0) for l in range(DEPTH)]
        + [small[l][4].reshape(1, WIDTH) for l in range(DEPTH)]
        + [small[l][5] for l in range(DEPTH)] + [small[l][6] for l in range(DEPTH)]
        + [jnp.transpose(small[l][7], (1, 0, 2)).reshape(1, 3 * WIDTH) for l in range(DEPTH)]
        + [loss_blk[0:1, :]], axis=1)
    n_row = row.shape[1]
    fold = -(-n_row // (8 * 128)) * 128
    rows = jnp.concatenate([row, jnp.zeros((1, 8 * fold - n_row), F32)], axis=1).reshape(8, fold)

    whole, gathered = plan.finish(rows)
    grads = {kind: [[whole[(kind, l)]] for l in range(DEPTH)] for kind in ("in", "br", "out")}

    off_nw = n_mod
    off_lb = off_nw + n_nw
    off_lng = off_lb + n_lb
    off_lnb = off_lng + n_ln
    off_cw = off_lnb + n_ln
    off_loss = off_cw + n_cw
    total = _sum_devices(gathered).reshape(1, 8 * fold)
    gathered = gathered.reshape(N_DEV, 1, 8 * fold)
    d_lower = _lbs_bwd(lower_bounds, total[0, off_lb:off_lng].reshape(DEPTH, WIDTH))
    loss = total[0, off_loss]
    d_b_mod = total[0, :n_mod].reshape(DEPTH, 3 * D)
    d_norm_w = total[0, off_nw:off_lb].reshape(DEPTH, 128)
    d_ln_g = total[0, off_lng:off_lnb].reshape(DEPTH, D)
    d_ln_b = total[0, off_lnb:off_cw].reshape(DEPTH, D)
    d_conv = total[0, off_cw:off_loss].reshape(DEPTH, 3, N_CHIPS, WIDTH // N_CHIPS)
    d_conv = lax.dynamic_index_in_dim(d_conv, chip, axis=2, keepdims=False)
    dmod_all = gathered[:, 0, :n_mod].reshape(N_DEV, DEPTH, N_CHIPS, mod_cols)
    dmod_cols = jnp.transpose(lax.dynamic_index_in_dim(dmod_all, chip, axis=2, keepdims=False), (1, 0, 2))
    d_w_mod = _grad_w_mod(c_all, dmod_cols)

    res = {}
    res["w_mod"] = _adamw(w_mod, m_w_mod, v_w_mod, [[d_w_mod]], "adamw_w_mod")
    res["b_mod"] = _adamw(b_mod, m_b_mod, v_b_mod, [[d_b_mod]], "adamw_b_mod")
    res["w_in"] = _adamw(w_in, m_w_in, v_w_in, grads["in"], "adamw_w_in")
    res["conv_w"] = _adamw(conv_w, m_conv_w, v_conv_w, [[d_conv]], "adamw_conv_w")
    res["hgrn_norm_w"] = _adamw(hgrn_norm_w, m_hgrn_norm_w, v_hgrn_norm_w, [[d_norm_w]], "adamw_norm_w")
    res["lower_bounds"] = _adamw(lower_bounds, m_lower_bounds, v_lower_bounds, [[d_lower]], "adamw_lower_bounds")
    res["w_branch"] = _adamw(w_branch, m_w_branch, v_w_branch, grads["br"], "adamw_w_branch")
    res["w_out"] = _adamw(w_out, m_w_out, v_w_out, grads["out"], "adamw_w_out")
    res["ln_g"] = _adamw(ln_g, m_ln_g, v_ln_g, [[d_ln_g]], "adamw_ln_g")
    res["ln_b"] = _adamw(ln_b, m_ln_b, v_ln_b, [[d_ln_b]], "adamw_ln_b")
    names = ["w_mod", "b_mod", "w_in", "conv_w", "hgrn_norm_w", "lower_bounds", "w_branch", "w_out", "ln_g", "ln_b"]
    return (loss, dx[None], *[res[n][0] for n in names], *[res[n][1] for n in names],
            *[res[n][2] for n in names], *[res[n][3] for n in names])


class _Plan:
    FIRST_CHUNKS = 2
    WINDOWS = ((0, 1280), (1280, 640), (1920, 384))

    def __init__(self, w_in, w_br, w_out, chip, core):
        self.local = {"in": w_in, "br": w_br, "out": w_out}
        self.chip, self.where = chip, jnp.stack([chip, core]).astype(jnp.int32)
        self.gathered, self.partial, self.grads, self.chip_sums, self.scattered, self.pending = {}, {}, {}, {}, {}, {}

    def chunks(self, l):
        return self.FIRST_CHUNKS if l == 0 else 1

    def w_in(self, l):
        return [self.gathered[("in", l, c)] for c in range(self.chunks(l))]

    def _shard(self, key):
        mine = self.local[key[0]][key[1]]
        if key[0] == "in":
            cols = mine.shape[-1] // self.chunks(key[1])
            mine = mine[:, key[2] * cols:(key[2] + 1) * cols]
        return mine.astype(BF)

    def _slab(self, key):
        mine = _halves(self._shard(key))
        return lax.dynamic_update_slice(lax.empty((N_CHIPS,) + mine.shape, mine.dtype), mine[None], (self.chip, 0, 0, 0, 0))

    def _gather(self, keys):
        return ("gather", keys), _gather_job([self._slab(key) for key in keys])

    def _fetch(self, keys):
        return ("gather_part", keys), _gather_job([self._slab(key) for key in keys], None, ())

    def _hand_over(self, keys):
        return ("gather", keys), _gather_job([self.partial[key] for key in keys], (), (None,))

    def _gather_window(self, key, n):
        slab = self._slab(key) if n == 0 else self.partial[key]
        last = n == len(self.WINDOWS) - 1
        handed = self.WINDOWS[max(n - 1, 0):n] + (self.WINDOWS[n:] if last else ())
        return ("gather" if last else "gather_part", [key]), _gather_job([slab], self.WINDOWS[n], handed)

    def _to_sibling(self, keys):
        return ("to_sibling", keys), _to_sibling_job([_halves(self.grads[key], 1) for key in keys])

    def _scatter(self, keys):
        return ("scatter", keys), _scatter_job([self.chip_sums[key] for key in keys])

    def job(self, stage, l, c=0):
        parts = []
        if stage == "proj_fwd":
            parts = [self._gather([("in", l, c + 1)]) if c + 1 < self.chunks(l) else self._fetch([("br", l), ("out", l)])]
        elif stage == "attn_fwd":
            parts = [self._hand_over([("br", l), ("out", l)])] + ([self._gather_window(("in", l + 1, 0), 0)] if l + 1 < DEPTH else [])
        elif stage == "hgrn_fwd" and l + 1 < DEPTH:
            parts = [self._gather_window(("in", l + 1, 0), 1)]
        elif stage == "merge_fwd" and l + 1 < DEPTH:
            parts = [self._gather_window(("in", l + 1, 0), 2)]
        elif stage == "attn_bwd":
            parts = [self._to_sibling([("out", l), ("br", l)])] + ([self._scatter([("in", l + 1)])] if l + 1 < DEPTH else [])
        elif stage == "hgrn_bwd":
            parts = [self._scatter([("out", l), ("br", l)])]
        elif stage == "proj_bwd":
            parts = [self._to_sibling([("in", l)])] if l else [self._scatter([("in", 0)])]
        self.pending[(stage, l, c)] = [(tag, len(job.outs)) for tag, job in parts]
        return _join_jobs([job for _, job in parts])

    def done(self, stage, l, outs, c=0):
        if outs is None:
            return
        at = 0
        for (what, keys), n_outs in self.pending[(stage, l, c)]:
            mine, at = outs[at:at + n_outs], at + n_outs
            for n, key in enumerate(keys):
                if what == "gather":
                    self.gathered[key] = mine[n].reshape((N_CHIPS,) + self._shard(key).shape)
                elif what == "gather_part":
                    self.partial[key] = mine[n]
                elif what == "to_sibling":
                    self.chip_sums[key] = _sum_cores(_halves(self.grads[key], 1), mine[n], self.where)
                else:
                    self.scattered[key] = mine[n]

    def first(self, rows):
        tag, job = self._gather([("in", 0, 0)])
        self.pending[("first", 0, 0)] = [(tag, len(job.outs))]
        outs = _run_job(_join_jobs([job, _gather8_job(rows)]), "gather_first")
        self.done("first", 0, outs[:-1])
        return outs[-1]

    def took(self, key, grad):
        self.grads[key] = grad
        if key == ("in", 0):
            tag, job = self._to_sibling([key])
            self.pending[("took", 0, 0)] = [(tag, len(job.outs))]
            self.done("took", 0, _run_job(job, "to_sibling_last"))

    def finish(self, rows):
        keys = [(kind, l) for kind in ("in", "br", "out") for l in range(DEPTH)]
        halves = [_sum_chips(self.chip_sums[key], self.scattered[key], self.where) for key in keys]
        outs = _run_job(_join_jobs([_place_job(halves), _gather8_job(rows)]), "place_halves")
        return {key: w.reshape(self.grads[key].shape[1:]) for key, w in zip(keys, outs[:-1])}, outs[-1]


def _local_step(x0, target, mods, lbs, conv_all, hgrn_norm_w, ln_g, ln_b, plan):
    D = D_MODEL
    after, before = _attn_consts()
    hg_consts = _hgrn_consts()

    saved = []
    xl = x0
    for l in range(DEPTH):
        n = plan.chunks(l)
        (proj, h), got = _proj_fwd(xl, mods[l], plan.gathered[("in", l, 0)], plan.job("proj_fwd", l, 0), (0, n))
        plan.done("proj_fwd", l, got, 0)
        for c in range(1, n):
            (proj,), got = _proj_cols(h, plan.gathered[("in", l, c)], (c, n), proj, plan.job("proj_fwd", l, c))
            plan.done("proj_fwd", l, got, c)
        (o_a, tot), got = _attn_fwd(proj, after, plan.job("attn_fwd", l))
        plan.done("attn_fwd", l, got)
        (o_b, states), got = _hgrn_fwd(proj, lbs[l], hg_consts, plan.job("hgrn_fwd", l))
        plan.done("hgrn_fwd", l, got)
        ys = _branch_fwd(proj, o_a, o_b, hgrn_norm_w, conv_all, l)
        wb = jnp.concatenate(list(plan.gathered[("br", l)]), axis=-1)
        (x_next, merged, y), got = _merge_fwd(xl, mods[l], proj, ys, wb, plan.gathered[("out", l)], ln_g, ln_b, l, plan.job("merge_fwd", l))
        plan.done("merge_fwd", l, got)
        saved.append((xl, proj, h, o_a, tot, o_b, states, ys, merged, y, wb))
        xl = x_next
    dx, loss_blk = _loss_head(xl, target)

    small = [None] * DEPTH
    for l in reversed(range(DEPTH)):
        xin, proj, h, o_a, tot, o_b, states, ys, merged, y, wb = saved[l]
        dx_res, dproj, dy_a, dy_b, dy_c, dln_g, dln_b, dgate, g_br, g_out = _merge_bwd(
            dx, xin, y, merged, mods[l], proj, ys, wb, plan.gathered[("out", l)], ln_g, l)
        plan.took(("out", l), g_out)
        plan.took(("br", l), g_br)
        d_oa, d_ob, dnorm_w, dconv_w, dproj = _branch_bwd(proj, o_a, o_b, hgrn_norm_w, conv_all, dy_a, dy_b, dy_c, dproj, l)
        (dproj,), got = _attn_bwd(proj, d_oa, tot, after, before, dproj, plan.job("attn_bwd", l))
        plan.done("attn_bwd", l, got)
        (dlb, dproj), got = _hgrn_bwd(proj, lbs[l], states, d_ob, hg_consts, dproj, plan.job("hgrn_bwd", l))
        plan.done("hgrn_bwd", l, got)
        plan.took(("in", l), _grad_w_in(h, dproj))
        wgs = plan.w_in(l)
        tile = {"tk": wgs[0].shape[-1]} if len(wgs) > 1 else {}
        (dx, dshift, dscale), got = _proj_bwd(dproj, wgs, xin, mods[l], dx_res, plan.job("proj_bwd", l), **tile)
        plan.done("proj_bwd", l, got)
        small[l] = (dshift, dscale, dgate, dnorm_w, dlb, dln_g, dln_b, dconv_w)
    return loss_blk, dx, small
```

```python
import functools
import math

import numpy as np
import jax
import jax.numpy as jnp
from jax import lax
from jax.experimental import pallas as pl
from jax.experimental.pallas import tpu as pltpu

F32 = jnp.float32
BF = jnp.bfloat16
MESH = pl.DeviceIdType.MESH

DEPTH = 2
D_MODEL = 1024
WIDTH = 512
IN_COLS = 12 * WIDTH + 3 * D_MODEL
N_CHIPS = 4
N_DEV = 8
SB_BLOCK = 128
SB_HEAD_DIM = 64
HG_CHUNK = 128
HG_DIM = 128
LN_EPS = 1e-5
RMS_EPS = 1e-6
ALPHA = (2.0 * DEPTH) ** 0.25
ADAM_LR, ADAM_B1, ADAM_B2, ADAM_EPS, ADAM_WD, ADAM_STEP = 0.001, 0.9, 0.999, 1e-08, 0.01, 10
VMEM_LIMIT = 56 << 20


def _params(**kw):
    return pltpu.CompilerParams(vmem_limit_bytes=VMEM_LIMIT, **kw)


def _dot(a, b):
    return jnp.dot(a, b, preferred_element_type=F32)


def _dot_nt(a, b):
    return lax.dot_general(a, b, (((1,), (1,)), ((), ())), preferred_element_type=F32)


def _dot_tn(a, b):
    return lax.dot_general(a, b, (((0,), (0,)), ((), ())), preferred_element_type=F32)


def _sigmoid(x):
    return 1.0 / (1.0 + jnp.exp(-x))


def _silu(x):
    return x * _sigmoid(x)


def _standardize(x):
    mu = jnp.mean(x, axis=-1, keepdims=True)
    xc = x - mu
    var = jnp.mean(xc * xc, axis=-1, keepdims=True)
    rstd = lax.rsqrt(var + LN_EPS)
    return xc * rstd, rstd


def _standardize_bwd(dxs, xs, rstd):
    return rstd * (dxs - jnp.mean(dxs, axis=-1, keepdims=True) - xs * jnp.mean(dxs * xs, axis=-1, keepdims=True))


class _Job:
    def __init__(self, ins, outs, sems, make, alias=None):
        self.ins, self.outs, self.sems, self.make = list(ins), list(outs), list(sems), make
        self.alias = dict(alias or {})


def _join_jobs(jobs):
    jobs = [j for j in jobs if j is not None]
    if len(jobs) <= 1:
        return jobs[0] if jobs else None

    def make(ins, outs, sems):
        phases, i, o, s = [], 0, 0, 0
        for j in jobs:
            got = j.make(ins[i:i + len(j.ins)], outs[o:o + len(j.outs)], sems[s:s + len(j.sems)])
            i, o, s = i + len(j.ins), o + len(j.outs), s + len(j.sems)
            for n, phase in enumerate(got):
                if n == len(phases):
                    phases.append([])
                phases[n] += phase
        return phases

    alias, i, o = {}, 0, 0
    for j in jobs:
        alias.update({i + a: o + b for a, b in j.alias.items()})
        i, o = i + len(j.ins), o + len(j.outs)
    return _Job(sum([j.ins for j in jobs], []), sum([j.outs for j in jobs], []), sum([j.sems for j in jobs], []), make, alias)


def _flip(v, bit):
    return 1 - v if bit else v


def _halves(a, front=0):
    shape = a.shape
    lead = math.prod(shape[front:-2])
    return a.reshape(shape[:front] + (lead, 2, shape[-2] // 2, shape[-1]))


def _dma_sems(*shapes):
    return [pltpu.SemaphoreType.DMA(s) for s in shapes]


def _same(arrays):
    return [jax.ShapeDtypeStruct(a.shape, a.dtype) for a in arrays]


def _gather_job(slabs, window=None, handed=(None,)):
    n = len(slabs)
    cols = lambda w: slice(None) if w is None else pl.ds(*w)

    def make(ins, outs, sems):
        send1, recv1, send2, recv2 = sems
        mx, my, mc = lax.axis_index("x"), lax.axis_index("y"), lax.axis_index("c")
        fetch, pass_on = [], []
        for a in range(n):
            for k in range(1, N_CHIPS):
                px, py = _flip(mx, k & 2), _flip(my, k & 1)
                if window != ():
                    ours = outs[a].at[2 * mx + my, :, mc, :, cols(window)]
                    fetch.append(pltpu.make_async_remote_copy(
                        src_ref=ours, dst_ref=ours, send_sem=send1.at[a, k - 1], recv_sem=recv1.at[a, k - 1],
                        device_id=(px, py, mc), device_id_type=MESH))
                for h, w in enumerate(handed):
                    theirs = outs[a].at[2 * px + py, :, mc, :, cols(w)]
                    (pass_on if w == window else fetch).append(pltpu.make_async_remote_copy(
                        src_ref=theirs, dst_ref=theirs, send_sem=send2.at[a, k - 1, h], recv_sem=recv2.at[a, k - 1, h],
                        device_id=(mx, my, 1 - mc), device_id_type=MESH))
        return [fetch, pass_on] if pass_on else [fetch]

    pairs = (n, N_CHIPS - 1)
    each = pairs + (max(len(handed), 1),)
    return _Job(slabs, _same(slabs), _dma_sems(pairs, pairs, each, each), make, {a: a for a in range(n)})


def _to_sibling_job(grads):
    n = len(grads)

    def make(ins, outs, sems):
        send_sems, recv_sems = sems
        mx, my, mc = lax.axis_index("x"), lax.axis_index("y"), lax.axis_index("c")
        return [[pltpu.make_async_remote_copy(
            src_ref=ins[a].at[:, :, 1 - mc], dst_ref=outs[a], send_sem=send_sems.at[a], recv_sem=recv_sems.at[a],
            device_id=(mx, my, 1 - mc), device_id_type=MESH) for a in range(n)]]

    outs = [jax.ShapeDtypeStruct(g.shape[:2] + g.shape[3:], g.dtype) for g in grads]
    return _Job(grads, outs, _dma_sems((n,), (n,)), make)


def _scatter_job(sums):
    n = len(sums)

    def make(ins, outs, sems):
        send_sems, recv_sems = sems
        mx, my, mc = lax.axis_index("x"), lax.axis_index("y"), lax.axis_index("c")
        copies = []
        for a in range(n):
            for k in range(1, N_CHIPS):
                px, py = _flip(mx, k & 2), _flip(my, k & 1)
                copies.append(pltpu.make_async_remote_copy(
                    src_ref=ins[a].at[2 * px + py], dst_ref=outs[a].at[k - 1], send_sem=send_sems.at[a, k - 1],
                    recv_sem=recv_sems.at[a, k - 1], device_id=(px, py, mc), device_id_type=MESH))
        return [copies]

    pairs = (n, N_CHIPS - 1)
    return _Job(sums, [jax.ShapeDtypeStruct((N_CHIPS - 1,) + s.shape[1:], s.dtype) for s in sums], _dma_sems(pairs, pairs), make)


def _place_job(wholes):
    n = len(wholes)

    def make(ins, outs, sems):
        send_sems, recv_sems = sems
        mx, my, mc = lax.axis_index("x"), lax.axis_index("y"), lax.axis_index("c")
        copies = []
        for a in range(n):
            here = outs[a].at[:, mc]
            copies.append(pltpu.make_async_remote_copy(src_ref=here, dst_ref=here, send_sem=send_sems.at[a], recv_sem=recv_sems.at[a],
                                                       device_id=(mx, my, 1 - mc), device_id_type=MESH))
        return [copies]

    return _Job(wholes, _same(wholes), _dma_sems((n,), (n,)), make, {a: a for a in range(n)})


def _gather8_job(x):
    def make(ins, outs, sems):
        local_sem, send_sems, recv_sems = sems
        mx, my, mc = lax.axis_index("x"), lax.axis_index("y"), lax.axis_index("c")
        here = outs[0].at[4 * mx + 2 * my + mc]
        copies = [pltpu.make_async_copy(ins[0], here, local_sem.at[0])]
        for k in range(1, N_DEV):
            peer = (_flip(mx, k & 4), _flip(my, k & 2), _flip(mc, k & 1))
            copies.append(pltpu.make_async_remote_copy(src_ref=ins[0], dst_ref=here, send_sem=send_sems.at[k - 1],
                                                       recv_sem=recv_sems.at[k - 1], device_id=peer, device_id_type=MESH))
        return [copies]

    return _Job([x], [jax.ShapeDtypeStruct((N_DEV,) + x.shape, x.dtype)], _dma_sems((1,), (N_DEV - 1,), (N_DEV - 1,)), make)


def _run_phases(phases, first=0):
    for n, phase in enumerate(phases):
        if n >= first:
            for cp in phase:
                cp.start()
        for cp in phase:
            cp.wait()


def _run_job(job, name):
    k_in, k_out = len(job.ins), len(job.outs)

    def body(*refs):
        _run_phases(job.make(refs[:k_in], refs[k_in:k_in + k_out], refs[k_in + k_out:]))

    hbm = pl.BlockSpec(memory_space=pl.ANY)
    return pl.pallas_call(body, name=name, in_specs=[hbm] * k_in, out_specs=[hbm] * k_out, out_shape=job.outs,
                          scratch_shapes=job.sems, input_output_aliases=job.alias, compiler_params=_params())(*job.ins)


def _hosted(body, job, args, *, name, grid, in_specs, out_specs, out_shape, scratch_shapes=(), semantics, aliases=None):
    in_specs, out_specs, out_shape, scratch = list(in_specs), list(out_specs), list(out_shape), list(scratch_shapes)
    aliases = dict(aliases or {})
    if job is None:
        outs = pl.pallas_call(body, name=name, grid=grid, in_specs=in_specs, out_specs=out_specs, out_shape=out_shape,
                              scratch_shapes=scratch, input_output_aliases=aliases,
                              compiler_params=_params(dimension_semantics=semantics))(*args)
        return list(outs), None
    n_in, n_out, n_scr, k_in, k_out = len(in_specs), len(out_specs), len(scratch), len(job.ins), len(job.outs)

    def wrapped(*refs):
        ins, rest = refs[:n_in], refs[n_in:]
        job_ins, rest = rest[:k_in], rest[k_in:]
        outs, rest = rest[:n_out], rest[n_out:]
        job_outs, rest = rest[:k_out], rest[k_out:]
        scr, sems = rest[:n_scr], rest[n_scr:]
        ids = [pl.program_id(a) for a in range(len(grid))]
        first = functools.reduce(jnp.logical_and, [i == 0 for i in ids])
        last = functools.reduce(jnp.logical_and, [i == g - 1 for i, g in zip(ids, grid)])

        @pl.when(first)
        def _():
            for cp in job.make(job_ins, job_outs, sems)[0]:
                cp.start()

        body(*ins, *outs, *scr)

        @pl.when(last)
        def _():
            _run_phases(job.make(job_ins, job_outs, sems), first=1)

    hbm = pl.BlockSpec(memory_space=pl.ANY)
    outs = pl.pallas_call(
        wrapped, name=name, grid=grid, in_specs=in_specs + [hbm] * k_in, out_specs=out_specs + [hbm] * k_out,
        out_shape=out_shape + job.outs, scratch_shapes=scratch + job.sems,
        input_output_aliases={**aliases, **{n_in + i: n_out + o for i, o in job.alias.items()}},
        compiler_params=_params(dimension_semantics=("arbitrary",) * len(grid)))(*args, *job.ins)
    return list(outs[:n_out]), list(outs[n_out:])


def _proj_fwd(x, mod, wg, job=None, chunk=(0, 1), tm=512):
    S, D = x.shape
    tm = min(tm, S)
    tn = wg.shape[-1]
    c, n = chunk

    rows = S // tm

    def body(x_ref, mod_ref, w_ref, proj_ref, h_ref, hs):
        mine = pl.ds(pl.multiple_of(pl.program_id(1) * tm, tm), tm)

        @pl.when(pl.program_id(0) == 0)
        def _():
            xs, _ = _standardize(x_ref[...])
            h = xs * (1.0 + mod_ref[:, D:2 * D]) + mod_ref[:, 0:D]
            hb = h.astype(BF)
            hs[mine, :] = hb
            h_ref[...] = hb

        proj_ref[...] = _dot(hs[mine, :], w_ref[...])

    once = lambda j, i: jnp.where(j == 0, i, rows - 1)
    return _hosted(
        body, job, (x, mod, wg), name="proj_fwd",
        grid=(N_CHIPS, rows),
        in_specs=[pl.BlockSpec((tm, D), lambda j, i: (once(j, i), 0)),
                  pl.BlockSpec((1, 3 * D), lambda j, i: (0, 0)),
                  pl.BlockSpec((None, D, tn), lambda j, i: (j, 0, 0))],
        out_specs=[pl.BlockSpec((tm, tn), lambda j, i: (i, j * n + c)),
                   pl.BlockSpec((tm, D), lambda j, i: (once(j, i), 0))],
        out_shape=[jax.ShapeDtypeStruct((S, IN_COLS), F32), jax.ShapeDtypeStruct((S, D), BF)],
        scratch_shapes=[pltpu.VMEM((S, D), BF)],
        semantics=("arbitrary", "arbitrary"))


def _proj_cols(h, wg, chunk, proj, job=None, tm=512):
    S, D = h.shape
    tm = min(tm, S)
    tn = wg.shape[-1]
    c, n = chunk

    def body(h_ref, w_ref, prev_ref, proj_ref):
        mine = pl.ds(pl.multiple_of(pl.program_id(1) * tm, tm), tm)
        proj_ref[...] = _dot(h_ref[mine, :], w_ref[...])

    in_specs = [pl.BlockSpec((S, D), lambda j, i: (0, 0)), pl.BlockSpec((None, D, tn), lambda j, i: (j, 0, 0)),
                pl.BlockSpec(memory_space=pl.ANY)]
    return _hosted(body, job, (h, wg, proj), name="proj_cols", grid=(N_CHIPS, S // tm), in_specs=in_specs,
                   out_specs=[pl.BlockSpec((tm, tn), lambda j, i: (i, j * n + c))],
                   out_shape=[jax.ShapeDtypeStruct(proj.shape, proj.dtype)],
                   semantics=("arbitrary", "arbitrary"), aliases={2: 0})


SB_ROWS = 256
SB_KEYS = 256


def _attn_consts():
    j = np.arange(SB_KEYS)[:, None]
    s = np.arange(SB_KEYS)[None, :]
    from_here = np.concatenate([(j >= s), (j >= s)], axis=0).astype(np.float32)
    return jnp.asarray(from_here, BF), jnp.asarray((j <= s).astype(np.float32), BF)


def _hi_lo(x):
    hi = lax.bitcast_convert_type(lax.bitcast_convert_type(x, jnp.uint32) & jnp.uint32(0xFFFF0000), F32)
    return hi.astype(BF), (x - hi).astype(BF)


def _sums_r(x, t2):
    hi, lo = _hi_lo(x)
    return _dot(jnp.concatenate([hi, lo], axis=1), t2)


def _all_lanes(col, lanes):
    return jnp.broadcast_to(col, (col.shape[0], lanes))


def _attn_rows(ref, r0, rows, lanes, head0, scale=None):
    v = ref[pl.ds(r0, rows), lanes]
    if scale is not None:
        v = v * scale
    return jnp.concatenate([jnp.where(head0, v, 0.0), jnp.where(head0, 0.0, v)], axis=0).astype(BF)


SB_PAIRS_FWD = 4
SB_PAIRS_BWD = 2


def _attn_specs(S, n_pairs):
    return lambda base: pl.BlockSpec((S, n_pairs * SB_BLOCK), lambda p, base=base: (0, base // n_pairs + p))


def _attn_scores(q2n, k_ref, lanes, kj, t2, from_here_ref, masked):
    c0 = pl.multiple_of(kj * SB_KEYS, SB_KEYS)
    kb = k_ref[pl.ds(c0, SB_KEYS), lanes].astype(BF)
    zn = _dot_nt(q2n, kb)
    lsb = jnp.minimum(zn, 0.0) - jnp.log(1.0 + jnp.exp(-jnp.abs(zn)))
    valid = None
    if masked:
        valid = (lax.broadcasted_iota(jnp.int32, zn.shape, 1) + kj * SB_KEYS) < t2
        lsb = jnp.where(valid, lsb, 0.0)
    return c0, kb, zn, valid, lsb, _sums_r(lsb, from_here_ref[...])


def _attn_fwd(proj, from_here, job=None):
    S = proj.shape[0]
    TQ = SB_ROWS
    assert S % TQ == 0 and SB_KEYS == TQ
    scale = SB_HEAD_DIM ** -0.5
    n_pairs = SB_PAIRS_FWD
    pairs = range(n_pairs)
    lanes = [pl.ds(p * SB_BLOCK, SB_BLOCK) for p in pairs]

    def body(q_ref, k_ref, v_ref, from_here_ref, o_ref, tot_ref, run, acc):
        head0 = lax.broadcasted_iota(jnp.int32, (1, 2 * SB_HEAD_DIM), 1) < SB_HEAD_DIM

        def qloop(qi, _):
            r0 = pl.multiple_of(qi * TQ, TQ)
            q2n = [_attn_rows(q_ref, r0, TQ, lanes[p], head0, -scale) for p in pairs]
            trow = lax.broadcasted_iota(jnp.int32, (TQ, SB_KEYS), 0) + qi * TQ
            t2 = jnp.concatenate([trow, trow], axis=0)
            run[...] = jnp.zeros_like(run)
            acc[...] = jnp.zeros_like(acc)

            def step(kj, masked):
                got = [_attn_scores(q2n[p], k_ref, lanes[p], kj, t2, from_here_ref, masked) for p in pairs]
                for p in pairs:
                    c0, _, zn, valid, _, sums = got[p]
                    r = run[p]
                    e = sums - zn + jnp.concatenate([r, r], axis=1)
                    if masked:
                        e = jnp.where(valid, e, -jnp.inf)
                    acc[p] += _dot(jnp.exp(e).astype(BF), v_ref[pl.ds(c0, SB_KEYS), lanes[p]].astype(BF))
                    run[p] = r + _all_lanes(sums[:, 0:1], SB_BLOCK)

            @pl.when(qi % 2 == 0)
            def _():
                step(qi, True)

            @pl.when(qi % 2 == 1)
            def _():
                step(qi, True)
                step(qi - 1, False)

            first = qi - 1 - qi % 2

            def below(n, _):
                step(first - 2 * n, False)
                step(first - 1 - 2 * n, False)
                return 0

            lax.fori_loop(0, qi // 2, below, 0)
            for p in pairs:
                o_ref[pl.ds(r0, TQ), lanes[p]] = jnp.where(head0, acc[p, 0:TQ, :], acc[p, TQ:2 * TQ, :])
                tot_ref[p, 0, pl.ds(r0, TQ), :] = run[p, 0:TQ, :]
                tot_ref[p, 1, pl.ds(r0, TQ), :] = run[p, TQ:2 * TQ, :]
            return 0

        lax.fori_loop(0, S // TQ, qloop, 0)

    col = _attn_specs(S, n_pairs)
    state = pltpu.VMEM((n_pairs, 2 * TQ, SB_BLOCK), F32)
    return _hosted(
        body, job, (proj, proj, proj, from_here), name="attn_fwd",
        grid=(WIDTH // (n_pairs * SB_BLOCK),),
        in_specs=[col(0), col(4), col(8), pl.BlockSpec(from_here.shape, lambda p: (0, 0))],
        out_specs=[col(0), pl.BlockSpec((n_pairs, 2, S, 128), lambda p: (p, 0, 0, 0))],
        out_shape=[jax.ShapeDtypeStruct((S, WIDTH), F32), jax.ShapeDtypeStruct((WIDTH // 128, 2, S, 128), F32)],
        scratch_shapes=[state, state],
        semantics=("parallel",))


def _attn_bwd(proj, d_o, tot, from_here, up_to, dproj, job=None):
    S = proj.shape[0]
    TQ = SB_ROWS
    assert S % TQ == 0 and SB_KEYS == TQ
    scale = SB_HEAD_DIM ** -0.5
    n_pairs = SB_PAIRS_BWD
    pairs = range(n_pairs)
    lanes = [pl.ds(p * SB_BLOCK, SB_BLOCK) for p in pairs]

    def body(q_ref, k_ref, v_ref, do_ref, tot_ref, from_here_ref, up_to_ref, _, dproj_ref, pre, cum, dq_acc, dk_acc, dv_acc, stage, sems):
        dq_ref, dk_ref, dv_ref = stage.at[0], stage.at[1], stage.at[2]
        head0 = lax.broadcasted_iota(jnp.int32, (1, 2 * SB_HEAD_DIM), 1) < SB_HEAD_DIM
        dk_acc[...] = jnp.zeros_like(dk_acc)
        dv_acc[...] = jnp.zeros_like(dv_acc)

        def qloop(qi, _):
            r0 = pl.multiple_of(qi * TQ, TQ)
            q2n = [_attn_rows(q_ref, r0, TQ, lanes[p], head0, -scale) for p in pairs]
            do2 = [_attn_rows(do_ref, r0, TQ, lanes[p], head0) for p in pairs]
            trow = lax.broadcasted_iota(jnp.int32, (TQ, SB_KEYS), 0) + qi * TQ
            t2 = jnp.concatenate([trow, trow], axis=0)
            for p in pairs:
                pre[p, 0:TQ, :] = tot_ref[p, 0, pl.ds(r0, TQ), :]
                pre[p, TQ:2 * TQ, :] = tot_ref[p, 1, pl.ds(r0, TQ), :]
            cum[...] = jnp.zeros_like(cum)
            dq_acc[...] = jnp.zeros_like(dq_acc)

            def step(kj, masked):
                got = [_attn_scores(q2n[p], k_ref, lanes[p], kj, t2, from_here_ref, masked) for p in pairs]
                das = [_dot_nt(do2[p], v_ref[pl.ds(got[p][0], SB_KEYS), lanes[p]].astype(BF)) for p in pairs]
                for p in pairs:
                    c0, kb, zn, valid, lsb, sums = got[p]
                    later = pre[p] - _all_lanes(sums[:, 0:1], SB_BLOCK)
                    pre[p] = later
                    e = sums - zn + jnp.concatenate([later, later], axis=1)
                    sig = jnp.exp(lsb - zn)
                    if masked:
                        e = jnp.where(valid, e, -jnp.inf)
                        sig = jnp.where(valid, sig, 0.0)
                    a = jnp.exp(e)
                    w = das[p] * a
                    upto = _dot(w.astype(BF), up_to_ref[...])
                    c = cum[p]
                    dz = w - sig * (upto + jnp.concatenate([c, c], axis=1))
                    cum[p] = c + _all_lanes(upto[:, SB_KEYS - 1:SB_KEYS], SB_BLOCK)
                    dzb = dz.astype(BF)
                    dq_acc[p] += _dot(dzb, kb)
                    dk_acc[pl.ds(c0, SB_KEYS), lanes[p]] += _dot_tn(dzb, q2n[p])
                    dv_acc[pl.ds(c0, SB_KEYS), lanes[p]] += _dot_tn(a.astype(BF), do2[p])

            def below(n, _):
                step(2 * n, False)
                step(2 * n + 1, False)
                return 0

            lax.fori_loop(0, qi // 2, below, 0)

            @pl.when(qi % 2 == 1)
            def _():
                step(qi - 1, False)
                step(qi, True)

            @pl.when(qi % 2 == 0)
            def _():
                step(qi, True)
            for p in pairs:
                dq_ref[pl.ds(r0, TQ), lanes[p]] = (jnp.where(head0, dq_acc[p, 0:TQ, :], dq_acc[p, TQ:2 * TQ, :]) * scale).astype(BF)
            return 0

        lax.fori_loop(0, S // TQ, qloop, 0)
        dk_ref[...] = (-dk_acc[...]).astype(BF)
        dv_ref[...] = dv_acc[...].astype(BF)
        lane0 = pl.program_id(0) * (n_pairs * SB_BLOCK)
        _put_columns([dq_ref, dk_ref, dv_ref], dproj_ref, [WIDTH * n + lane0 for n in range(3)], sems)

    col = _attn_specs(S, n_pairs)
    whole = lambda a: pl.BlockSpec(a.shape, lambda p: (0, 0))
    hbm = pl.BlockSpec(memory_space=pl.ANY)
    state = pltpu.VMEM((n_pairs, 2 * TQ, SB_BLOCK), F32)
    grads = pltpu.VMEM((S, n_pairs * SB_BLOCK), F32)
    return _hosted(
        body, job, (proj, proj, proj, d_o, tot, from_here, up_to, dproj), name="attn_bwd",
        grid=(WIDTH // (n_pairs * SB_BLOCK),),
        in_specs=[col(0), col(4), col(8), col(0), pl.BlockSpec((n_pairs, 2, S, 128), lambda p: (p, 0, 0, 0)), whole(from_here),
                  whole(up_to), hbm],
        out_specs=[hbm],
        out_shape=[jax.ShapeDtypeStruct(dproj.shape, dproj.dtype)],
        scratch_shapes=[state, state, state, grads, grads, pltpu.VMEM((3, S, n_pairs * SB_BLOCK), BF), pltpu.SemaphoreType.DMA((3,))],
        semantics=("arbitrary",), aliases={7: 0})


HG_LEVELS = tuple(HG_CHUNK >> n for n in range(1, HG_CHUNK.bit_length()))


def _hgrn_consts():
    C = HG_CHUNK
    t = np.arange(C)[:, None]
    s = np.arange(C)[None, :]
    rows = [(s <= t), (s > t)]
    masks = [(t == s)]
    for m in HG_LEVELS:
        two = 2 * m
        mid = (t // two) * two + m
        right = (t % two) >= m
        rows.append((right & (s >= mid) & (s <= t)) | ((~right) & (s > t) & (s <= mid - 1)))
        masks.append(((t // two) == (s // two)) & right & ((s % two) < m))
    tri = np.concatenate(rows, axis=0).astype(np.float32)
    twice = lambda a: jnp.asarray(np.concatenate([a, a], axis=1), BF)
    return (twice(tri), twice(tri.T), jnp.asarray(np.stack(masks).astype(np.float32), F32))


HG_SUM_BLOCKS = 2 + len(HG_LEVELS)


def _split_rows(g):
    hi = g.astype(BF)
    return jnp.concatenate([hi, (g - hi.astype(F32)).astype(BF)], axis=0)


def _hgrn_sum_blocks(e):
    C = HG_CHUNK
    blocks = tuple(e[n * C:(n + 1) * C] for n in range(HG_SUM_BLOCKS))
    return blocks + (jnp.broadcast_to(e[C - 1:C], (HG_DIM, e.shape[1])),)


@jax.custom_vjp
def _hgrn_sums(tri, tri_t, g):
    return _hgrn_sum_blocks(_dot(tri, _split_rows(g)))


def _hgrn_sums_fwd(tri, tri_t, g):
    return _hgrn_sums(tri, tri_t, g), (tri, tri_t)


def _hgrn_sums_bwd(res, ds):
    tri, tri_t = res
    C = HG_CHUNK
    last = lax.broadcasted_iota(jnp.int32, (C, 1), 0) == C - 1
    prefix = ds[0] + jnp.where(last, jnp.sum(ds[-1], axis=0, keepdims=True), 0.0)
    d = jnp.concatenate((prefix,) + tuple(ds[1:-1]), axis=0)
    return jnp.zeros_like(tri), jnp.zeros_like(tri_t), _dot(tri_t[:, :d.shape[0]], d.astype(BF))


_hgrn_sums.defvjp(_hgrn_sums_fwd, _hgrn_sums_bwd)


def _bf_dot(a, b):
    return _dot(a.astype(BF), b.astype(BF))


def _bf_dot_nt(a, b):
    return _dot_nt(a.astype(BF), b.astype(BF))


def _bf_dot_tn(a, b):
    return _dot_tn(a.astype(BF), b.astype(BF))


@jax.custom_vjp
def _mm(a, b):
    return _bf_dot(a, b)


_mm.defvjp(lambda a, b: (_bf_dot(a, b), (a, b)), lambda r, ct: (_bf_dot_nt(ct, r[1]), _bf_dot_tn(r[0], ct)))


@jax.custom_vjp
def _mm_nt(a, b):
    return _bf_dot_nt(a, b)


_mm_nt.defvjp(lambda a, b: (_bf_dot_nt(a, b), (a, b)), lambda r, ct: (_bf_dot(ct, r[1]), _bf_dot_tn(ct, r[0])))


@jax.custom_vjp
def _mm_tn(a, b):
    return _bf_dot_tn(a, b)


_mm_tn.defvjp(lambda a, b: (_bf_dot_tn(a, b), (a, b)), lambda r, ct: (_bf_dot_nt(r[1], ct), _bf_dot(r[0], ct)))


def _hgrn_gates(tri, tri_t, qraw, fpre, lb):
    q = _silu(qraw)
    f = lb + (1.0 - lb) * _sigmoid(fpre)
    return q, 1.0 - f, _hgrn_sums(tri, tri_t, jnp.log(f))


def _hgrn_mix(masks, q, k, e, v, st):
    prefix, suffix, whole = e[0], e[1], e[-1]
    scores = masks[0] * _mm_nt(q, k)
    for n in range(len(HG_LEVELS)):
        decay = jnp.exp(e[2 + n])
        scores = scores + masks[n + 1] * _mm_nt(q * decay, k * decay)
    o = _mm_nt(q * jnp.exp(prefix), st) + _mm(scores, v)
    st_new = st * jnp.exp(whole) + _mm_tn(v, k * jnp.exp(suffix))
    return o, st_new


def _hgrn_chunk(tri, tri_t, masks, qraw, fpre, v, st, lb):
    q, k, e = _hgrn_gates(tri, tri_t, qraw, fpre, lb)
    return _hgrn_mix(masks, q, k, e, v, st)


HG_HEADS_PER_STEP = 4
HG_LANES = HG_HEADS_PER_STEP * HG_DIM


def _hgrn_specs(S, consts):
    col = lambda base: pl.BlockSpec((S, HG_LANES), lambda p, base=base: (0, base // HG_HEADS_PER_STEP + p))
    whole = [pl.BlockSpec(a.shape, lambda p, n=a.ndim: (0,) * n) for a in consts]
    return col, whole


def _hgrn_fwd(proj, lbs, consts, job=None):
    S = proj.shape[0]
    nc = S // HG_CHUNK
    heads = range(HG_HEADS_PER_STEP)

    def body(q_ref, f_ref, i_ref, lb_ref, tri_ref, trit_ref, mask_ref, o_ref, st_ref):
        tri, tri_t = tri_ref[...], trit_ref[...]
        masks = [mask_ref[n] for n in range(len(HG_LEVELS) + 1)]

        def chunk(ci, sts):
            r0 = pl.multiple_of(ci * HG_CHUNK, HG_CHUNK)
            rows = pl.ds(r0, HG_CHUNK)
            new = []
            lane = [pl.ds(hd * HG_DIM, HG_DIM) for hd in heads]
            gates = [_hgrn_gates(tri, tri_t, q_ref[rows, lane[hd]], f_ref[rows, lane[hd]], lb_ref[hd]) for hd in heads]
            for hd in heads:
                st_ref[hd, ci] = sts[hd]
                o, st_new = _hgrn_mix(masks, *gates[hd], i_ref[rows, lane[hd]], sts[hd])
                o_ref[rows, lane[hd]] = o
                new.append(st_new)
            return tuple(new)

        lax.fori_loop(0, nc, chunk, tuple(jnp.zeros((HG_DIM, HG_DIM), F32) for _ in heads))

    col, whole = _hgrn_specs(S, consts)
    return _hosted(
        body, job, (proj, proj, proj, lbs, *consts), name="hgrn_fwd",
        grid=(WIDTH // HG_LANES,),
        in_specs=[col(16), col(20), col(24), pl.BlockSpec((HG_HEADS_PER_STEP, 1, 128), lambda p: (p, 0, 0))] + whole,
        out_specs=[col(0), pl.BlockSpec((HG_HEADS_PER_STEP, nc, HG_DIM, HG_DIM), lambda p: (p, 0, 0, 0))],
        out_shape=[jax.ShapeDtypeStruct((S, WIDTH), F32), jax.ShapeDtypeStruct((WIDTH // 128, nc, HG_DIM, HG_DIM), F32)],
        semantics=("parallel",))


def _hgrn_bwd(proj, lbs, states, d_o, consts, dproj, job=None):
    S = proj.shape[0]
    nc = S // HG_CHUNK

    def body(q_ref, f_ref, i_ref, lb_ref, st_ref, do_ref, tri_ref, trit_ref, mask_ref, _, dlb_ref, dproj_ref, stage, sems):
        dq_ref, df_ref, di_ref = stage.at[0], stage.at[1], stage.at[2]
        masks = [mask_ref[n] for n in range(len(HG_LEVELS) + 1)]
        fn = functools.partial(_hgrn_chunk, tri_ref[...], trit_ref[...], masks)
        heads = range(HG_HEADS_PER_STEP)

        def chunk(n, carry):
            ci = nc - 1 - n
            r0 = pl.multiple_of(ci * HG_CHUNK, HG_CHUNK)
            rows = pl.ds(r0, HG_CHUNK)
            new = []
            lane = [pl.ds(hd * HG_DIM, HG_DIM) for hd in heads]
            pulls = [jax.vjp(fn, q_ref[rows, lane[hd]], f_ref[rows, lane[hd]], i_ref[rows, lane[hd]], st_ref[hd, ci], lb_ref[hd])[1]
                     for hd in heads]
            for hd in heads:
                d_st, dlb = carry[hd]
                lanes = lane[hd]
                dq, df, di, d_prev, dl = pulls[hd]((do_ref[rows, lanes], d_st))
                dq_ref[rows, lanes] = dq.astype(BF)
                df_ref[rows, lanes] = df.astype(BF)
                di_ref[rows, lanes] = di.astype(BF)
                new.append((d_prev, dlb + dl))
            return tuple(new)

        zero = (jnp.zeros((HG_DIM, HG_DIM), F32), jnp.zeros((1, HG_DIM), F32))
        done = lax.fori_loop(0, nc, chunk, tuple(zero for _ in heads))
        for hd in heads:
            dlb_ref[hd] = done[hd][1]
        lane0 = pl.program_id(0) * HG_LANES
        _put_columns([dq_ref, df_ref, di_ref], dproj_ref, [WIDTH * n + lane0 for n in (4, 5, 6)], sems)

    col, whole = _hgrn_specs(S, consts)
    head = pl.BlockSpec((HG_HEADS_PER_STEP, 1, 128), lambda p: (p, 0, 0))
    hbm = pl.BlockSpec(memory_space=pl.ANY)
    n_in = 6 + len(consts)
    return _hosted(
        body, job, (proj, proj, proj, lbs, states, d_o, *consts, dproj), name="hgrn_bwd",
        grid=(WIDTH // HG_LANES,),
        in_specs=[col(16), col(20), col(24), head, pl.BlockSpec((HG_HEADS_PER_STEP, nc, HG_DIM, HG_DIM), lambda p: (p, 0, 0, 0)), col(0)]
        + whole + [hbm],
        out_specs=[head, hbm],
        out_shape=[jax.ShapeDtypeStruct((WIDTH // 128, 1, 128), F32), jax.ShapeDtypeStruct(dproj.shape, dproj.dtype)],
        scratch_shapes=[pltpu.VMEM((3, S, HG_LANES), BF), pltpu.SemaphoreType.DMA((3,))],
        semantics=("arbitrary",), aliases={n_in: 1})


def _shift_down(x, n):
    rows = lax.broadcasted_iota(jnp.int32, x.shape, 0)
    return jnp.where(rows >= n, pltpu.roll(x, n, 0), 0.0)


def _shift_up(x, n):
    S = x.shape[0]
    rows = lax.broadcasted_iota(jnp.int32, x.shape, 0)
    return jnp.where(rows < S - n, pltpu.roll(x, S - n, 0), 0.0)


def _branch_fwd(proj, o_a, o_b, norm_w, conv_w, layer):
    S = proj.shape[0]

    def body(oa_ref, za_ref, ob_ref, zb_ref, nw_ref, pre_ref, post_ref, u_ref, zc_ref, cw_ref, ya_ref, yb_ref, yc_ref):
        ya_ref[...] = (oa_ref[...] * _silu(za_ref[...])).astype(BF)
        ob = ob_ref[...]
        rn = lax.rsqrt(jnp.mean(ob * ob, axis=-1, keepdims=True) + RMS_EPS)
        yb_ref[...] = (ob * rn * nw_ref[layer:layer + 1, :] * _silu(zb_ref[...])).astype(BF)
        pu = pre_ref[...] * u_ref[...]
        conv = cw_ref[2:3, :] * pu + cw_ref[1:2, :] * _shift_down(pu, 1) + cw_ref[0:1, :] * _shift_down(pu, 2)
        yc_ref[...] = (post_ref[...] * conv * _silu(zc_ref[...])).astype(BF)

    col = lambda base: pl.BlockSpec((S, 128), lambda p, base=base: (0, base + p))
    out = jax.ShapeDtypeStruct((S, WIDTH), BF)
    return pl.pallas_call(
        body, name="branch_fwd",
        grid=(WIDTH // 128,),
        in_specs=[col(0), col(12), col(0), col(28), pl.BlockSpec(norm_w.shape, lambda p: (0, 0)),
                  col(32), col(36), col(40), col(44), pl.BlockSpec((None, None, 3, 128), lambda p: (p, layer, 0, 0))],
        out_specs=[col(0), col(0), col(0)],
        out_shape=[out, out, out],
        compiler_params=_params(dimension_semantics=("parallel",)),
    )(o_a, proj, o_b, proj, norm_w, proj, proj, proj, proj, conv_w)


def _put_columns(tiles, dproj_ref, firsts, sems):
    copies = [pltpu.make_async_copy(t, dproj_ref.at[:, pl.ds(pl.multiple_of(c, 128), t.shape[1])], sems.at[n])
              for n, (t, c) in enumerate(zip(tiles, firsts))]
    for cp in copies:
        cp.start()
    for cp in copies:
        cp.wait()


def _branch_bwd(proj, o_a, o_b, norm_w, conv_w, dy_a, dy_b, dy_c, dproj, layer):
    S = proj.shape[0]
    firsts = [WIDTH * n for n in (3, 7, 8, 9, 10, 11)]

    def dsilu(z):
        s = _sigmoid(z)
        return s * z, s * (1.0 + z * (1.0 - s))

    def body(oa_ref, za_ref, ob_ref, zb_ref, nw_ref, pre_ref, post_ref, u_ref, zc_ref, cw_ref, dya_ref, dyb_ref, dyc_ref, _,
             doa_ref, dob_ref, dnw_ref, dcw_ref, dproj_ref, stage, sems):
        dza_ref, dzb_ref, dpre_ref, dpost_ref, du_ref, dzc_ref = [stage.at[n] for n in range(6)]
        dya = dya_ref[...]
        sa, dsa = dsilu(za_ref[...])
        doa_ref[...] = dya * sa
        dza_ref[...] = (dya * oa_ref[...] * dsa).astype(BF)

        dyb = dyb_ref[...]
        ob = ob_ref[...]
        nw = nw_ref[layer:layer + 1, :]
        sb, dsb = dsilu(zb_ref[...])
        rn = lax.rsqrt(jnp.mean(ob * ob, axis=-1, keepdims=True) + RMS_EPS)
        on = ob * rn
        dzb_ref[...] = (dyb * on * nw * dsb).astype(BF)
        don_w = dyb * sb
        dnw_ref[...] = jnp.sum(don_w * on, axis=0, keepdims=True)
        don = don_w * nw
        dob_ref[...] = rn * (don - on * jnp.mean(don * on, axis=-1, keepdims=True))

        dyc = dyc_ref[...]
        pre, post, u = pre_ref[...], post_ref[...], u_ref[...]
        sc, dsc = dsilu(zc_ref[...])
        pu = pre * u
        pu1, pu2 = _shift_down(pu, 1), _shift_down(pu, 2)
        conv = cw_ref[2:3, :] * pu + cw_ref[1:2, :] * pu1 + cw_ref[0:1, :] * pu2
        dzc_ref[...] = (dyc * post * conv * dsc).astype(BF)
        dpost_ref[...] = (dyc * conv * sc).astype(BF)
        dconv = dyc * post * sc
        dcw_ref[0:1, :] = jnp.sum(dconv * pu2, axis=0, keepdims=True)
        dcw_ref[1:2, :] = jnp.sum(dconv * pu1, axis=0, keepdims=True)
        dcw_ref[2:3, :] = jnp.sum(dconv * pu, axis=0, keepdims=True)
        dpu = cw_ref[2:3, :] * dconv + cw_ref[1:2, :] * _shift_up(dconv, 1) + cw_ref[0:1, :] * _shift_up(dconv, 2)
        dpre_ref[...] = (dpu * u).astype(BF)
        du_ref[...] = (dpu * pre).astype(BF)
        lane0 = pl.program_id(0) * 128
        _put_columns([stage.at[n] for n in range(6)], dproj_ref, [c + lane0 for c in firsts], sems)

    col = lambda base: pl.BlockSpec((S, 128), lambda p, base=base: (0, base + p))
    f32 = jax.ShapeDtypeStruct((S, WIDTH), F32)
    hbm = pl.BlockSpec(memory_space=pl.ANY)
    return pl.pallas_call(
        body, name="branch_bwd",
        grid=(WIDTH // 128,),
        in_specs=[col(0), col(12), col(0), col(28), pl.BlockSpec(norm_w.shape, lambda p: (0, 0)),
                  col(32), col(36), col(40), col(44), pl.BlockSpec((None, None, 3, 128), lambda p: (p, layer, 0, 0)),
                  col(0), col(0), col(0), hbm],
        out_specs=[col(0), col(0), pl.BlockSpec((None, 1, 128), lambda p: (p, 0, 0)), pl.BlockSpec((None, 3, 128), lambda p: (p, 0, 0)), hbm],
        out_shape=[f32, f32, jax.ShapeDtypeStruct((WIDTH // 128, 1, 128), F32), jax.ShapeDtypeStruct((WIDTH // 128, 3, 128), F32),
                   jax.ShapeDtypeStruct(dproj.shape, dproj.dtype)],
        scratch_shapes=[pltpu.VMEM((6, S, 128), BF), pltpu.SemaphoreType.DMA((6,))],
        input_output_aliases={13: 4},
        compiler_params=_params(dimension_semantics=("arbitrary",)),
    )(o_a, proj, o_b, proj, norm_w, proj, proj, proj, proj, conv_w, dy_a, dy_b, dy_c, dproj)


def _branch_proj(y_refs, wb_ref):
    return [_dot(y_refs[i][...], wb_ref[i]) for i in range(3)]


def _merge_fwd(x, mod, proj, ys, wb, wo, ln_g, ln_b, layer, job=None, tm=512):
    S, D = x.shape
    tm = min(tm, S)

    def body(x_ref, mod_ref, ga_ref, gb_ref, gc_ref, ya_ref, yb_ref, yc_ref, wb_ref, wo_ref, g_ref, b_ref, xo_ref, mg_ref, y_ref):
        ps = _branch_proj((ya_ref, yb_ref, yc_ref), wb_ref)
        merged = _sigmoid(ga_ref[...]) * ps[0] + _sigmoid(gb_ref[...]) * ps[1] + _sigmoid(gc_ref[...]) * ps[2]
        mb = merged.astype(BF)
        mg_ref[...] = mb
        y = _dot(mb, wo_ref[...].reshape(D, D))
        y_ref[...] = y
        r = ALPHA * x_ref[...] + (1.0 + mod_ref[:, 2 * D:3 * D]) * y
        xn, _ = _standardize(r)
        xo_ref[...] = xn * g_ref[layer:layer + 1, :] + b_ref[layer:layer + 1, :]

    row = lambda w, c=0: pl.BlockSpec((tm, w), lambda i, c=c: (i, c))
    whole = lambda a: pl.BlockSpec(a.shape, lambda i, n=a.ndim: (0,) * n)
    return _hosted(
        body, job, (x, mod, proj, proj, proj, *ys, wb, wo, ln_g, ln_b), name="merge_fwd",
        grid=(S // tm,),
        in_specs=[row(D), whole(mod), row(D, 6), row(D, 7), row(D, 8), row(WIDTH), row(WIDTH), row(WIDTH), whole(wb),
                  whole(wo), whole(ln_g), whole(ln_b)],
        out_specs=[row(D), row(D), row(D)],
        out_shape=[jax.ShapeDtypeStruct((S, D), F32), jax.ShapeDtypeStruct((S, D), BF), jax.ShapeDtypeStruct((S, D), F32)],
        semantics=("parallel",))


def _merge_bwd(dxo, x, y, merged, mod, proj, ys, wb, wo, ln_g, layer, tm=256):
    S, D = x.shape
    tm = min(tm, S)
    steps = S // tm
    quarter = D // N_CHIPS

    def body(dxo_ref, x_ref, y_ref, mg_ref, mod_ref, ga_ref, gb_ref, gc_ref, ya_ref, yb_ref, yc_ref, wb_ref, wo_ref, g_ref,
             dxr_ref, dg_ref, dya_ref, dyb_ref, dyc_ref, dlg_ref, dlb_ref, dgt_ref, gbr_ref, gout_ref, acc_br, acc_out):
        @pl.when(pl.program_id(0) == 0)
        def _():
            dlg_ref[...] = jnp.zeros_like(dlg_ref)
            dlb_ref[...] = jnp.zeros_like(dlb_ref)
            dgt_ref[...] = jnp.zeros_like(dgt_ref)
            acc_br[...] = jnp.zeros_like(acc_br)
            acc_out[...] = jnp.zeros_like(acc_out)

        gate1 = 1.0 + mod_ref[:, 2 * D:3 * D]
        yv = y_ref[...]
        xn, rstd = _standardize(ALPHA * x_ref[...] + gate1 * yv)
        dxo = dxo_ref[...]
        dlg_ref[...] += jnp.sum(dxo * xn, axis=0, keepdims=True)
        dlb_ref[...] += jnp.sum(dxo, axis=0, keepdims=True)
        dr = _standardize_bwd(dxo * g_ref[layer:layer + 1, :], xn, rstd)
        dxr_ref[...] = ALPHA * dr
        dgt_ref[...] += jnp.sum(dr * yv, axis=0, keepdims=True)
        dyb = (gate1 * dr).astype(BF)
        acc_out[...] += _dot_tn(mg_ref[...], dyb)
        dmerged = _dot_nt(dyb, wo_ref[...].reshape(D, D))
        y_refs = (ya_ref, yb_ref, yc_ref)
        ps = _branch_proj(y_refs, wb_ref)
        for i, (gate_ref, out_ref) in enumerate(((ga_ref, dya_ref), (gb_ref, dyb_ref), (gc_ref, dyc_ref))):
            sg = _sigmoid(gate_ref[...])
            dg_ref[:, i * D:(i + 1) * D] = (dmerged * ps[i] * sg * (1.0 - sg)).astype(BF)
            dp = (dmerged * sg).astype(BF)
            acc_br[i] += _dot_tn(y_refs[i][...], dp)
            out_ref[...] = _dot_nt(dp, wb_ref[i])

        @pl.when(pl.program_id(0) == steps - 1)
        def _():
            for j in range(N_CHIPS):
                gout_ref[j] = acc_out[j * quarter:(j + 1) * quarter, :].astype(BF)
                for i in range(3):
                    gbr_ref[j, i] = acc_br[i, :, j * quarter:(j + 1) * quarter].astype(BF)

    row = lambda w, c=0: pl.BlockSpec((tm, w), lambda i, c=c: (i, c))
    whole = lambda a: pl.BlockSpec(a.shape, lambda i, n=len(a.shape): (0,) * n)
    vec = pl.BlockSpec((1, D), lambda i: (0, 0))
    sd = jax.ShapeDtypeStruct
    g_br, g_out = sd((N_CHIPS, 3, WIDTH, quarter), BF), sd((N_CHIPS, quarter, D), BF)
    return pl.pallas_call(
        body, name="merge_bwd",
        grid=(steps,),
        in_specs=[row(D), row(D), row(D), row(D), whole(mod), row(D, 6), row(D, 7), row(D, 8), row(WIDTH), row(WIDTH), row(WIDTH),
                  whole(wb), whole(wo), whole(ln_g)],
        out_specs=[row(D), row(3 * D, IN_COLS // (3 * D) - 1), row(WIDTH), row(WIDTH), row(WIDTH), vec, vec, vec, whole(g_br), whole(g_out)],
        out_shape=[sd((S, D), F32), sd((S, IN_COLS), BF), sd((S, WIDTH), F32), sd((S, WIDTH), F32), sd((S, WIDTH), F32),
                   sd((1, D), F32), sd((1, D), F32), sd((1, D), F32), g_br, g_out],
        scratch_shapes=[pltpu.VMEM((3, WIDTH, D), F32), pltpu.VMEM((D, D), F32)],
        compiler_params=_params(dimension_semantics=("arbitrary",)),
    )(dxo, x, y, merged, mod, proj, proj, proj, *ys, wb, wo, ln_g)


def _loss_head(x, target, tm=512):
    S, D = x.shape
    tm = min(tm, S)

    def body(x_ref, t_ref, dx_ref, loss_ref):
        @pl.when(pl.program_id(0) == 0)
        def _():
            loss_ref[...] = jnp.zeros_like(loss_ref)

        err = x_ref[...] - t_ref[...]
        dx_ref[...] = err * (1.0 / D)
        loss_ref[...] += 0.5 * jnp.sum(jnp.mean(err * err, axis=-1, keepdims=True))

    row = pl.BlockSpec((tm, D), lambda i: (i, 0))
    return pl.pallas_call(
        body, name="loss_head",
        grid=(S // tm,),
        in_specs=[row, row],
        out_specs=[row, pl.BlockSpec((8, 128), lambda i: (0, 0))],
        out_shape=[jax.ShapeDtypeStruct((S, D), F32), jax.ShapeDtypeStruct((8, 128), F32)],
        compiler_params=_params(dimension_semantics=("arbitrary",)),
    )(x, target)


def _proj_bwd(dproj, wgs, x, mod, dx_res, job=None, tm=512, tk=768):
    S, D = x.shape
    tm = min(tm, S)
    n = len(wgs)
    w_args = list(wgs)
    if n == 1:
        per = wgs[0].shape[-1] // tk
        w_specs = [pl.BlockSpec((None, D, tk), lambda k, i: (k // per, 0, k % per))]
    else:
        assert wgs[0].shape[-1] == tk
        w_specs = [pl.BlockSpec((None, D, tk), lambda k, i, c=c: (jnp.minimum((k + n - 1 - c) // n, N_CHIPS - 1), 0, 0))
                   for c in range(n)]
    nk = IN_COLS // tk

    def body(dp_ref, *rest):
        w_refs, (x_ref, mod_ref, dxr_ref, dx_ref, dsh_ref, dsc_ref, acc) = rest[:n], rest[n:]
        k, i = pl.program_id(0), pl.program_id(1)
        mine = pl.ds(pl.multiple_of(i * tm, tm), tm)

        @pl.when((i == 0) & (k == 0))
        def _():
            dsh_ref[...] = jnp.zeros_like(dsh_ref)
            dsc_ref[...] = jnp.zeros_like(dsc_ref)

        @pl.when(k == 0)
        def _():
            acc[mine, :] = jnp.zeros((tm, D), F32)

        for c in range(n):
            @pl.when(k % n == c)
            def _(c=c):
                acc[mine, :] += _dot_nt(dp_ref[...], w_refs[c][...])

        @pl.when(k == nk - 1)
        def _():
            dh = acc[mine, :]
            xs, rstd = _standardize(x_ref[...])
            dsh_ref[...] += jnp.sum(dh, axis=0, keepdims=True)
            dsc_ref[...] += jnp.sum(dh * xs, axis=0, keepdims=True)
            dx_ref[...] = _standardize_bwd(dh * (1.0 + mod_ref[:, D:2 * D]), xs, rstd) + dxr_ref[...]

    row = pl.BlockSpec((tm, D), lambda k, i: (jnp.where(k == nk - 1, i, 0), 0))
    vec = pl.BlockSpec((1, D), lambda k, i: (0, 0))
    return _hosted(
        body, job, (dproj, *w_args, x, mod, dx_res), name="proj_bwd",
        grid=(nk, S // tm),
        in_specs=[pl.BlockSpec((tm, tk), lambda k, i: (i, k))] + w_specs + [row, pl.BlockSpec((1, 3 * D), lambda k, i: (0, 0)), row],
        out_specs=[row, vec, vec],
        out_shape=[jax.ShapeDtypeStruct((S, D), F32), jax.ShapeDtypeStruct((1, D), F32), jax.ShapeDtypeStruct((1, D), F32)],
        scratch_shapes=[pltpu.VMEM((S, D), F32)],
        semantics=("arbitrary", "arbitrary"))


def _grad_w_in(h, dproj):
    S, D = h.shape
    shard = IN_COLS // N_CHIPS

    def body(h_ref, d_ref, o_ref):
        o_ref[...] = _dot_tn(h_ref[...], d_ref[...]).astype(BF)

    return pl.pallas_call(
        body, name="grad_w_in",
        grid=(N_CHIPS,),
        in_specs=[pl.BlockSpec((S, D), lambda n: (0, 0)), pl.BlockSpec((S, shard), lambda n: (0, n))],
        out_specs=pl.BlockSpec((None, D, shard), lambda n: (n, 0, 0)),
        out_shape=jax.ShapeDtypeStruct((N_CHIPS, D, shard), BF),
        compiler_params=_params(dimension_semantics=("parallel",)),
    )(h, dproj)


def _all_gather8(x, name):
    R, N = x.shape

    def body(x_ref, out_ref, send_sems, recv_sems):
        mx, my, mc = lax.axis_index("x"), lax.axis_index("y"), lax.axis_index("c")
        me = 4 * mx + 2 * my + mc
        out_ref[me] = x_ref[...]
        copies = []
        for k in range(1, N_DEV):
            peer = (_flip(mx, k & 4), _flip(my, k & 2), _flip(mc, k & 1))
            cp = pltpu.make_async_remote_copy(src_ref=x_ref, dst_ref=out_ref.at[me], send_sem=send_sems.at[k - 1],
                                              recv_sem=recv_sems.at[k - 1], device_id=peer, device_id_type=MESH)
            cp.start()
            copies.append(cp)
        for cp in copies:
            cp.wait()

    return pl.pallas_call(
        body, name=name,
        in_specs=[pl.BlockSpec(memory_space=pltpu.VMEM)],
        out_specs=pl.BlockSpec(memory_space=pltpu.VMEM),
        out_shape=jax.ShapeDtypeStruct((N_DEV, R, N), F32),
        scratch_shapes=[pltpu.SemaphoreType.DMA((N_DEV - 1,)), pltpu.SemaphoreType.DMA((N_DEV - 1,))],
        compiler_params=_params(),
    )(x)


def _rows2d(a):
    return a.reshape(-1, a.shape[-1])


def _tile_rows(rows, cols, n_arrays):
    budget = (24 << 20) // (n_arrays * 2 * 4 * cols)
    if rows <= budget:
        return rows
    tm = 8
    for cand in range(8, budget + 1, 8):
        if rows % cand == 0:
            tm = cand
    return tm


SUM_ROWS = 256


def _sum_cores(g, sent, where):
    chips, lead, _, r, cols = g.shape
    tr = min(r, SUM_ROWS)

    def body(where_ref, g_ref, s_ref, out_ref):
        out_ref[...] = (g_ref[...].astype(F32) + s_ref[...].astype(F32)).astype(BF)

    spec = pl.BlockSpec((None, tr, cols), lambda i, j, where_ref: (i, j, 0))
    out = pl.pallas_call(
        body, name="sum_cores",
        grid_spec=pltpu.PrefetchScalarGridSpec(
            num_scalar_prefetch=1, grid=(chips * lead, r // tr),
            in_specs=[pl.BlockSpec((None, None, tr, cols), lambda i, j, where_ref: (i, where_ref[1], j, 0)), spec],
            out_specs=spec),
        out_shape=jax.ShapeDtypeStruct((chips * lead, r, cols), BF),
        compiler_params=_params(dimension_semantics=("parallel", "parallel")),
    )(where, g.reshape(chips * lead, 2, r, cols), sent.reshape(chips * lead, r, cols))
    return out.reshape(chips, lead, r, cols)


def _sum_chips(sums, got, where):
    _, lead, r, cols = sums.shape
    tr = min(r, SUM_ROWS)

    def body(where_ref, s_ref, g_ref, out_ref):
        out_ref[...] = ((s_ref[...].astype(F32) + g_ref[0].astype(F32)) + g_ref[1].astype(F32)) + g_ref[2].astype(F32)

    return pl.pallas_call(
        body, name="sum_chips",
        grid_spec=pltpu.PrefetchScalarGridSpec(
            num_scalar_prefetch=1, grid=(lead, r // tr),
            in_specs=[pl.BlockSpec((None, None, tr, cols), lambda i, j, where_ref: (where_ref[0], i, j, 0)),
                      pl.BlockSpec((N_CHIPS - 1, None, tr, cols), lambda i, j, where_ref: (0, i, j, 0))],
            out_specs=pl.BlockSpec((None, None, tr, cols), lambda i, j, where_ref: (i, where_ref[1], j, 0))),
        out_shape=jax.ShapeDtypeStruct((lead, 2, r, cols), F32),
        compiler_params=_params(dimension_semantics=("parallel", "parallel")),
    )(where, sums, got)


def _adamw(w, m, v, groups, name):
    shape = w.shape
    w2, m2, v2 = _rows2d(w), _rows2d(m), _rows2d(v)
    rows, cols = w2.shape
    ng = len(groups)
    n = len(groups[0])
    slab = rows // ng
    gs = [_rows2d(g) for grp in groups for g in grp]
    tm = _tile_rows(slab, cols, 7 + n)
    tiles = slab // tm
    c1 = 1.0 / (1.0 - ADAM_B1 ** ADAM_STEP)
    c2 = 1.0 / (1.0 - ADAM_B2 ** ADAM_STEP)

    def body(*refs):
        w_ref, m_ref, v_ref = refs[:3]
        g_refs = refs[3:3 + ng * n]
        go_ref, d_ref, mo_ref, vo_ref = refs[3 + ng * n:]
        which = pl.program_id(0)
        for s in range(ng):
            @pl.when(which == s)
            def _(s=s):
                g = g_refs[s * n][...]
                for r in g_refs[s * n + 1:(s + 1) * n]:
                    g = g + r[...]
                mn = ADAM_B1 * m_ref[...] + (1.0 - ADAM_B1) * g
                vn = ADAM_B2 * v_ref[...] + (1.0 - ADAM_B2) * (g * g)
                go_ref[...] = g
                mo_ref[...] = mn
                vo_ref[...] = vn
                d_ref[...] = -ADAM_LR * ((mn * c1) / (jnp.sqrt(vn * c2) + ADAM_EPS) + ADAM_WD * w_ref[...])

    spec = pl.BlockSpec((tm, cols), lambda s, i: (s * tiles + i, 0))
    g_specs = [pl.BlockSpec((tm, cols), lambda s, i, k=k: (jnp.where(s == k, i, jnp.where(s < k, 0, tiles - 1)), 0))
               for k in range(ng) for _ in range(n)]
    outs = pl.pallas_call(
        body, name=name,
        grid=(ng, tiles),
        in_specs=[spec] * 3 + g_specs,
        out_specs=[spec] * 4,
        out_shape=[jax.ShapeDtypeStruct((rows, cols), F32)] * 4,
        compiler_params=_params(dimension_semantics=("arbitrary", "arbitrary")),
    )(w2, m2, v2, *gs)
    return [o.reshape(shape) for o in outs]


def _lower_bounds(r0, r1):
    top = jnp.maximum(r0, r1)
    e0, e1 = jnp.exp(r0 - top), jnp.exp(r1 - top)
    p0, p1 = e0 / (e0 + e1), e1 / (e0 + e1)
    return p0 - p0, (p0 + p1) - p0


def _lbs_fwd(lb_raw):
    def body(lb_ref, out_ref):
        l0, l1 = _lower_bounds(lb_ref[0:1, :], lb_ref[1:2, :])
        out_ref[0:1, :] = l0
        out_ref[1:2, :] = l1

    return pl.pallas_call(body, name="lower_bounds", out_shape=jax.ShapeDtypeStruct(lb_raw.shape, F32), compiler_params=_params())(lb_raw)


def _mod_rows(c_all, w_mod, tn=768):
    _, D, cols = w_mod.shape

    def body(c_ref, w_ref, out_ref):
        out_ref[...] = _dot(c_ref[...].astype(BF), w_ref[...].astype(BF))

    return pl.pallas_call(
        body, name="mod_rows",
        grid=(DEPTH,),
        in_specs=[pl.BlockSpec((N_DEV, D), lambda l: (0, 0)), pl.BlockSpec((None, D, cols), lambda l: (l, 0, 0))],
        out_specs=pl.BlockSpec((N_DEV, cols), lambda l: (0, l)),
        out_shape=jax.ShapeDtypeStruct((N_DEV, DEPTH * cols), F32),
        compiler_params=_params(dimension_semantics=("parallel",)),
    )(c_all, w_mod)


def _grad_w_mod(c_all, dmod_cols):
    D = c_all.shape[1]
    cols = dmod_cols.shape[-1]

    def body(c_ref, d_ref, out_ref):
        out_ref[...] = _dot_tn(c_ref[...].astype(BF), d_ref[...].astype(BF))

    return pl.pallas_call(
        body, name="grad_w_mod",
        grid=(DEPTH,),
        in_specs=[pl.BlockSpec((N_DEV, D), lambda l: (0, 0)), pl.BlockSpec((None, N_DEV, cols), lambda l: (l, 0, 0))],
        out_specs=pl.BlockSpec((None, D, cols), lambda l: (l, 0, 0)),
        out_shape=jax.ShapeDtypeStruct((DEPTH, D, cols), F32),
        compiler_params=_params(dimension_semantics=("parallel",)),
    )(c_all, dmod_cols)


def _sum_devices(parts):
    _, R, N = parts.shape

    def body(p_ref, out_ref):
        acc = p_ref[0]
        for d in range(1, N_DEV):
            acc = acc + p_ref[d]
        out_ref[...] = acc

    return pl.pallas_call(body, name="sum_devices", out_shape=jax.ShapeDtypeStruct((R, N), F32), compiler_params=_params())(parts)


def _lbs_bwd(lb_raw, dl):
    def body(lb_ref, dl_ref, out_ref):
        _, pull = jax.vjp(_lower_bounds, lb_ref[0:1, :], lb_ref[1:2, :])
        d0, d1 = pull((dl_ref[0:1, :], dl_ref[1:2, :]))
        out_ref[0:1, :] = d0
        out_ref[1:2, :] = d1

    return pl.pallas_call(body, name="lower_bounds_bwd", out_shape=jax.ShapeDtypeStruct(lb_raw.shape, F32), compiler_params=_params())(lb_raw, dl)


def kernel(x, c, w_mod, b_mod, w_in, conv_w, hgrn_norm_w, lower_bounds, w_branch, w_out, ln_g, ln_b, loss_target, m_w_mod, m_b_mod, m_w_in, m_conv_w, m_hgrn_norm_w, m_lower_bounds, m_w_branch, m_w_out, m_ln_g, m_ln_b, v_w_mod, v_b_mod, v_w_in, v_conv_w, v_hgrn_norm_w, v_lower_bounds, v_w_branch, v_w_out, v_ln_g, v_ln_b):
    D = D_MODEL
    x0 = x[0]
    target = loss_target[0]
    S = x0.shape[0]
    mx, my, mc = lax.axis_index("x"), lax.axis_index("y"), lax.axis_index("c")
    chip = 2 * mx + my
    me = 2 * chip + mc
    mod_cols = 3 * D // N_CHIPS

    plan = _Plan(w_in, w_branch, w_out, chip, mc)
    n_conv = DEPTH * 3 * (WIDTH // N_CHIPS)
    first = jnp.concatenate([c, conv_w.reshape(1, n_conv), jnp.zeros((1, 2 * D - D - n_conv), F32)], axis=1)
    first = plan.first(first.reshape(8, 2 * D // 8)).reshape(N_DEV, 2 * D)

    c_all = first[:, :D]
    conv_all = first[:, D:D + n_conv].reshape(N_DEV, DEPTH, 3, WIDTH // N_CHIPS)[0::2]
    mod_part = _all_gather8(_mod_rows(c_all, w_mod), "gather_mod")[0::2]
    mod_part = lax.dynamic_index_in_dim(mod_part, me, axis=1, keepdims=False).reshape(N_CHIPS, DEPTH, mod_cols)
    mods = [(mod_part[:, l].reshape(1, 3 * D) + b_mod[l][None, :]) for l in range(DEPTH)]
    lbs = _lbs_fwd(lower_bounds).reshape(DEPTH, WIDTH // 128, 1, 128)
    loss_blk, dx, small = _local_step(x0, target, mods, lbs, conv_all, hgrn_norm_w, ln_g, ln_b, plan)

    n_mod, n_nw, n_lb, n_ln, n_cw = DEPTH * 3 * D, DEPTH * 128, DEPTH * WIDTH, DEPTH * D, DEPTH * 3 * WIDTH
    row = jnp.concatenate(
        [jnp.concatenate([small[l][0], small[l][1], small[l][2]], axis=1) for l in range(DEPTH)]
        + [jnp.sum(small[l][3], axis=0) for l in range(DEPTH)]
        + [small[l][4].reshape(1, WIDTH) for l in range(DEPTH)]
        + [small[l][5] for l in range(DEPTH)] + [small[l][6] for l in range(DEPTH)]
        + [jnp.transpose(small[l][7], (1, 0, 2)).reshape(1, 3 * WIDTH) for l in range(DEPTH)]
        + [loss_blk[0:1, :]], axis=1)
    n_row = row.shape[1]
    fold = -(-n_row // (8 * 128)) * 128
    rows = jnp.concatenate([row, jnp.zeros((1, 8 * fold - n_row), F32)], axis=1).reshape(8, fold)

    whole, gathered = plan.finish(rows)
    grads = {kind: [[whole[(kind, l)]] for l in range(DEPTH)] for kind in ("in", "br", "out")}

    off_nw = n_mod
    off_lb = off_nw + n_nw
    off_lng = off_lb + n_lb
    off_lnb = off_lng + n_ln
    off_cw = off_lnb + n_ln
    off_loss = off_cw + n_cw
    total = _sum_devices(gathered).reshape(1, 8 * fold)
    gathered = gathered.reshape(N_DEV, 1, 8 * fold)
    d_lower = _lbs_bwd(lower_bounds, total[0, off_lb:off_lng].reshape(DEPTH, WIDTH))
    loss = total[0, off_loss]
    d_b_mod = total[0, :n_mod].reshape(DEPTH, 3 * D)
    d_norm_w = total[0, off_nw:off_lb].reshape(DEPTH, 128)
    d_ln_g = total[0, off_lng:off_lnb].reshape(DEPTH, D)
    d_ln_b = total[0, off_lnb:off_cw].reshape(DEPTH, D)
    d_conv = total[0, off_cw:off_loss].reshape(DEPTH, 3, N_CHIPS, WIDTH // N_CHIPS)
    d_conv = lax.dynamic_index_in_dim(d_conv, chip, axis=2, keepdims=False)
    dmod_all = gathered[:, 0, :n_mod].reshape(N_DEV, DEPTH, N_CHIPS, mod_cols)
    dmod_cols = jnp.transpose(lax.dynamic_index_in_dim(dmod_all, chip, axis=2, keepdims=False), (1, 0, 2))
    d_w_mod = _grad_w_mod(c_all, dmod_cols)

    res = {}
    res["w_mod"] = _adamw(w_mod, m_w_mod, v_w_mod, [[d_w_mod]], "adamw_w_mod")
    res["b_mod"] = _adamw(b_mod, m_b_mod, v_b_mod, [[d_b_mod]], "adamw_b_mod")
    res["w_in"] = _adamw(w_in, m_w_in, v_w_in, grads["in"], "adamw_w_in")
    res["conv_w"] = _adamw(conv_w, m_conv_w, v_conv_w, [[d_conv]], "adamw_conv_w")
    res["hgrn_norm_w"] = _adamw(hgrn_norm_w, m_hgrn_norm_w, v_hgrn_norm_w, [[d_norm_w]], "adamw_norm_w")
    res["lower_bounds"] = _adamw(lower_bounds, m_lower_bounds, v_lower_bounds, [[d_lower]], "adamw_lower_bounds")
    res["w_branch"] = _adamw(w_branch, m_w_branch, v_w_branch, grads["br"], "adamw_w_branch")
    res["w_out"] = _adamw(w_out, m_w_out, v_w_out, grads["out"], "adamw_w_out")
    res["ln_g"] = _adamw(ln_g, m_ln_g, v_ln_g, [[d_ln_g]], "adamw_ln_g")
    res["ln_b"] = _adamw(ln_b, m_ln_b, v_ln_b, [[d_ln_b]], "adamw_ln_b")
    names = ["w_mod", "b_mod", "w_in", "conv_w", "hgrn_norm_w", "lower_bounds", "w_branch", "w_out", "ln_g", "ln_b"]
    return (loss, dx[None], *[res[n][0] for n in names], *[res[n][1] for n in names],
            *[res[n][2] for n in names], *[res[n][3] for n in names])


class _Plan:
    FIRST_CHUNKS = 2
    WINDOWS = ((0, 1280), (1280, 768), (2048, 256))

    def __init__(self, w_in, w_br, w_out, chip, core):
        self.local = {"in": w_in, "br": w_br, "out": w_out}
        self.chip, self.where = chip, jnp.stack([chip, core]).astype(jnp.int32)
        self.gathered, self.partial, self.grads, self.chip_sums, self.scattered, self.pending = {}, {}, {}, {}, {}, {}

    def chunks(self, l):
        return self.FIRST_CHUNKS if l == 0 else 1

    def w_in(self, l):
        return [self.gathered[("in", l, c)] for c in range(self.chunks(l))]

    def _shard(self, key):
        mine = self.local[key[0]][key[1]]
        if key[0] == "in":
            cols = mine.shape[-1] // self.chunks(key[1])
            mine = mine[:, key[2] * cols:(key[2] + 1) * cols]
        return mine.astype(BF)

    def _slab(self, key):
        mine = _halves(self._shard(key))
        return lax.dynamic_update_slice(lax.empty((N_CHIPS,) + mine.shape, mine.dtype), mine[None], (self.chip, 0, 0, 0, 0))

    def _gather(self, keys):
        return ("gather", keys), _gather_job([self._slab(key) for key in keys])

    def _fetch(self, keys):
        return ("gather_part", keys), _gather_job([self._slab(key) for key in keys], None, ())

    def _hand_over(self, keys):
        return ("gather", keys), _gather_job([self.partial[key] for key in keys], (), (None,))

    def _gather_window(self, key, n):
        slab = self._slab(key) if n == 0 else self.partial[key]
        last = n == len(self.WINDOWS) - 1
        handed = self.WINDOWS[max(n - 1, 0):n] + (self.WINDOWS[n:] if last else ())
        return ("gather" if last else "gather_part", [key]), _gather_job([slab], self.WINDOWS[n], handed)

    def _to_sibling(self, keys):
        return ("to_sibling", keys), _to_sibling_job([_halves(self.grads[key], 1) for key in keys])

    def _scatter(self, keys):
        return ("scatter", keys), _scatter_job([self.chip_sums[key] for key in keys])

    def job(self, stage, l, c=0):
        parts = []
        if stage == "proj_fwd":
            parts = [self._gather([("in", l, c + 1)]) if c + 1 < self.chunks(l) else self._fetch([("br", l), ("out", l)])]
        elif stage == "attn_fwd":
            parts = [self._hand_over([("br", l), ("out", l)])] + ([self._gather_window(("in", l + 1, 0), 0)] if l + 1 < DEPTH else [])
        elif stage == "hgrn_fwd" and l + 1 < DEPTH:
            parts = [self._gather_window(("in", l + 1, 0), 1)]
        elif stage == "merge_fwd" and l + 1 < DEPTH:
            parts = [self._gather_window(("in", l + 1, 0), 2)]
        elif stage == "attn_bwd":
            parts = [self._to_sibling([("out", l), ("br", l)])] + ([self._scatter([("in", l + 1)])] if l + 1 < DEPTH else [])
        elif stage == "hgrn_bwd":
            parts = [self._scatter([("out", l), ("br", l)])]
        elif stage == "proj_bwd":
            parts = [self._to_sibling([("in", l)])] if l else [self._scatter([("in", 0)])]
        self.pending[(stage, l, c)] = [(tag, len(job.outs)) for tag, job in parts]
        return _join_jobs([job for _, job in parts])

    def done(self, stage, l, outs, c=0):
        if outs is None:
            return
        at = 0
        for (what, keys), n_outs in self.pending[(stage, l, c)]:
            mine, at = outs[at:at + n_outs], at + n_outs
            for n, key in enumerate(keys):
                if what == "gather":
                    self.gathered[key] = mine[n].reshape((N_CHIPS,) + self._shard(key).shape)
                elif what == "gather_part":
                    self.partial[key] = mine[n]
                elif what == "to_sibling":
                    self.chip_sums[key] = _sum_cores(_halves(self.grads[key], 1), mine[n], self.where)
                else:
                    self.scattered[key] = mine[n]

    def first(self, rows):
        tag, job = self._gather([("in", 0, 0)])
        self.pending[("first", 0, 0)] = [(tag, len(job.outs))]
        outs = _run_job(_join_jobs([job, _gather8_job(rows)]), "gather_first")
        self.done("first", 0, outs[:-1])
        return outs[-1]

    def took(self, key, grad):
        self.grads[key] = grad
        if key == ("in", 0):
            tag, job = self._to_sibling([key])
            self.pending[("took", 0, 0)] = [(tag, len(job.outs))]
            self.done("took", 0, _run_job(job, "to_sibling_last"))

    def finish(self, rows):
        keys = [(kind, l) for kind in ("in", "br", "out") for l in range(DEPTH)]
        halves = [_sum_chips(self.chip_sums[key], self.scattered[key], self.where) for key in keys]
        outs = _run_job(_join_jobs([_place_job(halves), _gather8_job(rows)]), "place_halves")
        return {key: w.reshape(self.grads[key].shape[1:]) for key, w in zip(keys, outs[:-1])}, outs[-1]


def _local_step(x0, target, mods, lbs, conv_all, hgrn_norm_w, ln_g, ln_b, plan):
    D = D_MODEL
    after, before = _attn_consts()
    hg_consts = _hgrn_consts()

    saved = []
    xl = x0
    for l in range(DEPTH):
        n = plan.chunks(l)
        (proj, h), got = _proj_fwd(xl, mods[l], plan.gathered[("in", l, 0)], plan.job("proj_fwd", l, 0), (0, n))
        plan.done("proj_fwd", l, got, 0)
        for c in range(1, n):
            (proj,), got = _proj_cols(h, plan.gathered[("in", l, c)], (c, n), proj, plan.job("proj_fwd", l, c))
            plan.done("proj_fwd", l, got, c)
        (o_a, tot), got = _attn_fwd(proj, after, plan.job("attn_fwd", l))
        plan.done("attn_fwd", l, got)
        (o_b, states), got = _hgrn_fwd(proj, lbs[l], hg_consts, plan.job("hgrn_fwd", l))
        plan.done("hgrn_fwd", l, got)
        ys = _branch_fwd(proj, o_a, o_b, hgrn_norm_w, conv_all, l)
        wb = jnp.concatenate(list(plan.gathered[("br", l)]), axis=-1)
        (x_next, merged, y), got = _merge_fwd(xl, mods[l], proj, ys, wb, plan.gathered[("out", l)], ln_g, ln_b, l, plan.job("merge_fwd", l))
        plan.done("merge_fwd", l, got)
        saved.append((xl, proj, h, o_a, tot, o_b, states, ys, merged, y, wb))
        xl = x_next
    dx, loss_blk = _loss_head(xl, target)

    small = [None] * DEPTH
    for l in reversed(range(DEPTH)):
        xin, proj, h, o_a, tot, o_b, states, ys, merged, y, wb = saved[l]
        dx_res, dproj, dy_a, dy_b, dy_c, dln_g, dln_b, dgate, g_br, g_out = _merge_bwd(
            dx, xin, y, merged, mods[l], proj, ys, wb, plan.gathered[("out", l)], ln_g, l)
        plan.took(("out", l), g_out)
        plan.took(("br", l), g_br)
        d_oa, d_ob, dnorm_w, dconv_w, dproj = _branch_bwd(proj, o_a, o_b, hgrn_norm_w, conv_all, dy_a, dy_b, dy_c, dproj, l)
        (dproj,), got = _attn_bwd(proj, d_oa, tot, after, before, dproj, plan.job("attn_bwd", l))
        plan.done("attn_bwd", l, got)
        (dlb, dproj), got = _hgrn_bwd(proj, lbs[l], states, d_ob, hg_consts, dproj, plan.job("hgrn_bwd", l))
        plan.done("hgrn_bwd", l, got)
        plan.took(("in", l), _grad_w_in(h, dproj))
        wgs = plan.w_in(l)
        tile = {"tk": wgs[0].shape[-1]} if len(wgs) > 1 else {}
        (dx, dshift, dscale), got = _proj_bwd(dproj, wgs, xin, mods[l], dx_res, plan.job("proj_bwd", l), **tile)
        plan.done("proj_bwd", l, got)
        small[l] = (dshift, dscale, dgate, dnorm_w, dlb, dln_g, dln_b, dconv_w)
    return loss_blk, dx, small
```

```python
import functools
import math

import numpy as np
import jax
import jax.numpy as jnp
from jax import lax
from jax.experimental import pallas as pl
from jax.experimental.pallas import tpu as pltpu

F32 = jnp.float32
BF = jnp.bfloat16
MESH = pl.DeviceIdType.MESH

DEPTH = 2
D_MODEL = 1024
WIDTH = 512
IN_COLS = 12 * WIDTH + 3 * D_MODEL
N_CHIPS = 4
N_DEV = 8
SB_BLOCK = 128
SB_HEAD_DIM = 64
HG_CHUNK = 128
HG_DIM = 128
LN_EPS = 1e-5
RMS_EPS = 1e-6
ALPHA = (2.0 * DEPTH) ** 0.25
ADAM_LR, ADAM_B1, ADAM_B2, ADAM_EPS, ADAM_WD, ADAM_STEP = 0.001, 0.9, 0.999, 1e-08, 0.01, 10
VMEM_LIMIT = 56 << 20


def _params(**kw):
    return pltpu.CompilerParams(vmem_limit_bytes=VMEM_LIMIT, **kw)


def _dot(a, b):
    return jnp.dot(a, b, preferred_element_type=F32)


def _dot_nt(a, b):
    return lax.dot_general(a, b, (((1,), (1,)), ((), ())), preferred_element_type=F32)


def _dot_tn(a, b):
    return lax.dot_general(a, b, (((0,), (0,)), ((), ())), preferred_element_type=F32)


def _sigmoid(x):
    return 1.0 / (1.0 + jnp.exp(-x))


def _silu(x):
    return x * _sigmoid(x)


def _standardize(x):
    mu = jnp.mean(x, axis=-1, keepdims=True)
    xc = x - mu
    var = jnp.mean(xc * xc, axis=-1, keepdims=True)
    rstd = lax.rsqrt(var + LN_EPS)
    return xc * rstd, rstd


def _standardize_bwd(dxs, xs, rstd):
    return rstd * (dxs - jnp.mean(dxs, axis=-1, keepdims=True) - xs * jnp.mean(dxs * xs, axis=-1, keepdims=True))


class _Job:
    def __init__(self, ins, outs, sems, make, alias=None):
        self.ins, self.outs, self.sems, self.make = list(ins), list(outs), list(sems), make
        self.alias = dict(alias or {})


def _join_jobs(jobs):
    jobs = [j for j in jobs if j is not None]
    if len(jobs) <= 1:
        return jobs[0] if jobs else None

    def make(ins, outs, sems):
        phases, i, o, s = [], 0, 0, 0
        for j in jobs:
            got = j.make(ins[i:i + len(j.ins)], outs[o:o + len(j.outs)], sems[s:s + len(j.sems)])
            i, o, s = i + len(j.ins), o + len(j.outs), s + len(j.sems)
            for n, phase in enumerate(got):
                if n == len(phases):
                    phases.append([])
                phases[n] += phase
        return phases

    alias, i, o = {}, 0, 0
    for j in jobs:
        alias.update({i + a: o + b for a, b in j.alias.items()})
        i, o = i + len(j.ins), o + len(j.outs)
    return _Job(sum([j.ins for j in jobs], []), sum([j.outs for j in jobs], []), sum([j.sems for j in jobs], []), make, alias)


def _flip(v, bit):
    return 1 - v if bit else v


def _halves(a, front=0):
    shape = a.shape
    lead = math.prod(shape[front:-2])
    return a.reshape(shape[:front] + (lead, 2, shape[-2] // 2, shape[-1]))


def _dma_sems(*shapes):
    return [pltpu.SemaphoreType.DMA(s) for s in shapes]


def _same(arrays):
    return [jax.ShapeDtypeStruct(a.shape, a.dtype) for a in arrays]


def _gather_job(slabs, window=None, handed=(None,)):
    n = len(slabs)
    cols = lambda w: slice(None) if w is None else pl.ds(*w)

    def make(ins, outs, sems):
        send1, recv1, send2, recv2 = sems
        mx, my, mc = lax.axis_index("x"), lax.axis_index("y"), lax.axis_index("c")
        fetch, pass_on = [], []
        for a in range(n):
            for k in range(1, N_CHIPS):
                px, py = _flip(mx, k & 2), _flip(my, k & 1)
                if window != ():
                    ours = outs[a].at[2 * mx + my, :, mc, :, cols(window)]
                    fetch.append(pltpu.make_async_remote_copy(
                        src_ref=ours, dst_ref=ours, send_sem=send1.at[a, k - 1], recv_sem=recv1.at[a, k - 1],
                        device_id=(px, py, mc), device_id_type=MESH))
                for h, w in enumerate(handed):
                    theirs = outs[a].at[2 * px + py, :, mc, :, cols(w)]
                    (pass_on if w == window else fetch).append(pltpu.make_async_remote_copy(
                        src_ref=theirs, dst_ref=theirs, send_sem=send2.at[a, k - 1, h], recv_sem=recv2.at[a, k - 1, h],
                        device_id=(mx, my, 1 - mc), device_id_type=MESH))
        return [fetch, pass_on] if pass_on else [fetch]

    pairs = (n, N_CHIPS - 1)
    each = pairs + (max(len(handed), 1),)
    return _Job(slabs, _same(slabs), _dma_sems(pairs, pairs, each, each), make, {a: a for a in range(n)})


def _to_sibling_job(grads):
    n = len(grads)

    def make(ins, outs, sems):
        send_sems, recv_sems = sems
        mx, my, mc = lax.axis_index("x"), lax.axis_index("y"), lax.axis_index("c")
        return [[pltpu.make_async_remote_copy(
            src_ref=ins[a].at[:, :, 1 - mc], dst_ref=outs[a], send_sem=send_sems.at[a], recv_sem=recv_sems.at[a],
            device_id=(mx, my, 1 - mc), device_id_type=MESH) for a in range(n)]]

    outs = [jax.ShapeDtypeStruct(g.shape[:2] + g.shape[3:], g.dtype) for g in grads]
    return _Job(grads, outs, _dma_sems((n,), (n,)), make)


def _scatter_job(sums):
    n = len(sums)

    def make(ins, outs, sems):
        send_sems, recv_sems = sems
        mx, my, mc = lax.axis_index("x"), lax.axis_index("y"), lax.axis_index("c")
        copies = []
        for a in range(n):
            for k in range(1, N_CHIPS):
                px, py = _flip(mx, k & 2), _flip(my, k & 1)
                copies.append(pltpu.make_async_remote_copy(
                    src_ref=ins[a].at[2 * px + py], dst_ref=outs[a].at[k - 1], send_sem=send_sems.at[a, k - 1],
                    recv_sem=recv_sems.at[a, k - 1], device_id=(px, py, mc), device_id_type=MESH))
        return [copies]

    pairs = (n, N_CHIPS - 1)
    return _Job(sums, [jax.ShapeDtypeStruct((N_CHIPS - 1,) + s.shape[1:], s.dtype) for s in sums], _dma_sems(pairs, pairs), make)


def _place_job(wholes):
    n = len(wholes)

    def make(ins, outs, sems):
        send_sems, recv_sems = sems
        mx, my, mc = lax.axis_index("x"), lax.axis_index("y"), lax.axis_index("c")
        copies = []
        for a in range(n):
            here = outs[a].at[:, mc]
            copies.append(pltpu.make_async_remote_copy(src_ref=here, dst_ref=here, send_sem=send_sems.at[a], recv_sem=recv_sems.at[a],
                                                       device_id=(mx, my, 1 - mc), device_id_type=MESH))
        return [copies]

    return _Job(wholes, _same(wholes), _dma_sems((n,), (n,)), make, {a: a for a in range(n)})


def _gather8_job(x):
    def make(ins, outs, sems):
        local_sem, send_sems, recv_sems = sems
        mx, my, mc = lax.axis_index("x"), lax.axis_index("y"), lax.axis_index("c")
        here = outs[0].at[4 * mx + 2 * my + mc]
        copies = [pltpu.make_async_copy(ins[0], here, local_sem.at[0])]
        for k in range(1, N_DEV):
            peer = (_flip(mx, k & 4), _flip(my, k & 2), _flip(mc, k & 1))
            copies.append(pltpu.make_async_remote_copy(src_ref=ins[0], dst_ref=here, send_sem=send_sems.at[k - 1],
                                                       recv_sem=recv_sems.at[k - 1], device_id=peer, device_id_type=MESH))
        return [copies]

    return _Job([x], [jax.ShapeDtypeStruct((N_DEV,) + x.shape, x.dtype)], _dma_sems((1,), (N_DEV - 1,), (N_DEV - 1,)), make)


def _run_phases(phases, first=0):
    for n, phase in enumerate(phases):
        if n >= first:
            for cp in phase:
                cp.start()
        for cp in phase:
            cp.wait()


def _run_job(job, name):
    k_in, k_out = len(job.ins), len(job.outs)

    def body(*refs):
        _run_phases(job.make(refs[:k_in], refs[k_in:k_in + k_out], refs[k_in + k_out:]))

    hbm = pl.BlockSpec(memory_space=pl.ANY)
    return pl.pallas_call(body, name=name, in_specs=[hbm] * k_in, out_specs=[hbm] * k_out, out_shape=job.outs,
                          scratch_shapes=job.sems, input_output_aliases=job.alias, compiler_params=_params())(*job.ins)


def _hosted(body, job, args, *, name, grid, in_specs, out_specs, out_shape, scratch_shapes=(), semantics, aliases=None):
    in_specs, out_specs, out_shape, scratch = list(in_specs), list(out_specs), list(out_shape), list(scratch_shapes)
    aliases = dict(aliases or {})
    if job is None:
        outs = pl.pallas_call(body, name=name, grid=grid, in_specs=in_specs, out_specs=out_specs, out_shape=out_shape,
                              scratch_shapes=scratch, input_output_aliases=aliases,
                              compiler_params=_params(dimension_semantics=semantics))(*args)
        return list(outs), None
    n_in, n_out, n_scr, k_in, k_out = len(in_specs), len(out_specs), len(scratch), len(job.ins), len(job.outs)

    def wrapped(*refs):
        ins, rest = refs[:n_in], refs[n_in:]
        job_ins, rest = rest[:k_in], rest[k_in:]
        outs, rest = rest[:n_out], rest[n_out:]
        job_outs, rest = rest[:k_out], rest[k_out:]
        scr, sems = rest[:n_scr], rest[n_scr:]
        ids = [pl.program_id(a) for a in range(len(grid))]
        first = functools.reduce(jnp.logical_and, [i == 0 for i in ids])
        last = functools.reduce(jnp.logical_and, [i == g - 1 for i, g in zip(ids, grid)])

        @pl.when(first)
        def _():
            for cp in job.make(job_ins, job_outs, sems)[0]:
                cp.start()

        body(*ins, *outs, *scr)

        @pl.when(last)
        def _():
            _run_phases(job.make(job_ins, job_outs, sems), first=1)

    hbm = pl.BlockSpec(memory_space=pl.ANY)
    outs = pl.pallas_call(
        wrapped, name=name, grid=grid, in_specs=in_specs + [hbm] * k_in, out_specs=out_specs + [hbm] * k_out,
        out_shape=out_shape + job.outs, scratch_shapes=scratch + job.sems,
        input_output_aliases={**aliases, **{n_in + i: n_out + o for i, o in job.alias.items()}},
        compiler_params=_params(dimension_semantics=("arbitrary",) * len(grid)))(*args, *job.ins)
    return list(outs[:n_out]), list(outs[n_out:])


def _proj_fwd(x, mod, wg, job=None, chunk=(0, 1), tm=512):
    S, D = x.shape
    tm = min(tm, S)
    tn = wg.shape[-1]
    c, n = chunk

    rows = S // tm

    def body(x_ref, mod_ref, w_ref, proj_ref, h_ref, hs):
        mine = pl.ds(pl.multiple_of(pl.program_id(1) * tm, tm), tm)

        @pl.when(pl.program_id(0) == 0)
        def _():
            xs, _ = _standardize(x_ref[...])
            h = xs * (1.0 + mod_ref[:, D:2 * D]) + mod_ref[:, 0:D]
            hb = h.astype(BF)
            hs[mine, :] = hb
            h_ref[...] = hb

        proj_ref[...] = _dot(hs[mine, :], w_ref[...])

    once = lambda j, i: jnp.where(j == 0, i, rows - 1)
    return _hosted(
        body, job, (x, mod, wg), name="proj_fwd",
        grid=(N_CHIPS, rows),
        in_specs=[pl.BlockSpec((tm, D), lambda j, i: (once(j, i), 0)),
                  pl.BlockSpec((1, 3 * D), lambda j, i: (0, 0)),
                  pl.BlockSpec((None, D, tn), lambda j, i: (j, 0, 0))],
        out_specs=[pl.BlockSpec((tm, tn), lambda j, i: (i, j * n + c)),
                   pl.BlockSpec((tm, D), lambda j, i: (once(j, i), 0))],
        out_shape=[jax.ShapeDtypeStruct((S, IN_COLS), F32), jax.ShapeDtypeStruct((S, D), BF)],
        scratch_shapes=[pltpu.VMEM((S, D), BF)],
        semantics=("arbitrary", "arbitrary"))


def _proj_cols(h, wg, chunk, proj, job=None, tm=512):
    S, D = h.shape
    tm = min(tm, S)
    tn = wg.shape[-1]
    c, n = chunk

    def body(h_ref, w_ref, prev_ref, proj_ref):
        mine = pl.ds(pl.multiple_of(pl.program_id(1) * tm, tm), tm)
        proj_ref[...] = _dot(h_ref[mine, :], w_ref[...])

    in_specs = [pl.BlockSpec((S, D), lambda j, i: (0, 0)), pl.BlockSpec((None, D, tn), lambda j, i: (j, 0, 0)),
                pl.BlockSpec(memory_space=pl.ANY)]
    return _hosted(body, job, (h, wg, proj), name="proj_cols", grid=(N_CHIPS, S // tm), in_specs=in_specs,
                   out_specs=[pl.BlockSpec((tm, tn), lambda j, i: (i, j * n + c))],
                   out_shape=[jax.ShapeDtypeStruct(proj.shape, proj.dtype)],
                   semantics=("arbitrary", "arbitrary"), aliases={2: 0})


SB_ROWS = 256
SB_KEYS = 256


def _attn_consts():
    j = np.arange(SB_KEYS)[:, None]
    s = np.arange(SB_KEYS)[None, :]
    from_here = np.concatenate([(j >= s), (j >= s)], axis=0).astype(np.float32)
    return jnp.asarray(from_here, BF), jnp.asarray((j <= s).astype(np.float32), BF)


def _hi_lo(x):
    hi = lax.bitcast_convert_type(lax.bitcast_convert_type(x, jnp.uint32) & jnp.uint32(0xFFFF0000), F32)
    return hi.astype(BF), (x - hi).astype(BF)


def _sums_r(x, t2):
    hi, lo = _hi_lo(x)
    return _dot(jnp.concatenate([hi, lo], axis=1), t2)


def _all_lanes(col, lanes):
    return jnp.broadcast_to(col, (col.shape[0], lanes))


def _attn_rows(ref, r0, rows, lanes, head0, scale=None):
    v = ref[pl.ds(r0, rows), lanes]
    if scale is not None:
        v = v * scale
    return jnp.concatenate([jnp.where(head0, v, 0.0), jnp.where(head0, 0.0, v)], axis=0).astype(BF)


SB_PAIRS_FWD = 4
SB_PAIRS_BWD = 2


def _attn_specs(S, n_pairs):
    return lambda base: pl.BlockSpec((S, n_pairs * SB_BLOCK), lambda p, base=base: (0, base // n_pairs + p))


def _attn_scores(q2n, k_ref, lanes, kj, t2, from_here_ref, masked):
    c0 = pl.multiple_of(kj * SB_KEYS, SB_KEYS)
    kb = k_ref[pl.ds(c0, SB_KEYS), lanes].astype(BF)
    zn = _dot_nt(q2n, kb)
    lsb = jnp.minimum(zn, 0.0) - jnp.log(1.0 + jnp.exp(-jnp.abs(zn)))
    valid = None
    if masked:
        valid = (lax.broadcasted_iota(jnp.int32, zn.shape, 1) + kj * SB_KEYS) < t2
        lsb = jnp.where(valid, lsb, 0.0)
    return c0, kb, zn, valid, lsb, _sums_r(lsb, from_here_ref[...])


def _attn_fwd(proj, from_here, job=None):
    S = proj.shape[0]
    TQ = SB_ROWS
    assert S % TQ == 0 and SB_KEYS == TQ
    scale = SB_HEAD_DIM ** -0.5
    n_pairs = SB_PAIRS_FWD
    pairs = range(n_pairs)
    lanes = [pl.ds(p * SB_BLOCK, SB_BLOCK) for p in pairs]

    def body(q_ref, k_ref, v_ref, from_here_ref, o_ref, tot_ref, run, acc):
        head0 = lax.broadcasted_iota(jnp.int32, (1, 2 * SB_HEAD_DIM), 1) < SB_HEAD_DIM

        def qloop(qi, _):
            r0 = pl.multiple_of(qi * TQ, TQ)
            q2n = [_attn_rows(q_ref, r0, TQ, lanes[p], head0, -scale) for p in pairs]
            trow = lax.broadcasted_iota(jnp.int32, (TQ, SB_KEYS), 0) + qi * TQ
            t2 = jnp.concatenate([trow, trow], axis=0)
            run[...] = jnp.zeros_like(run)
            acc[...] = jnp.zeros_like(acc)

            def step(kj, masked):
                got = [_attn_scores(q2n[p], k_ref, lanes[p], kj, t2, from_here_ref, masked) for p in pairs]
                for p in pairs:
                    c0, _, zn, valid, _, sums = got[p]
                    r = run[p]
                    e = sums - zn + jnp.concatenate([r, r], axis=1)
                    if masked:
                        e = jnp.where(valid, e, -jnp.inf)
                    acc[p] += _dot(jnp.exp(e).astype(BF), v_ref[pl.ds(c0, SB_KEYS), lanes[p]].astype(BF))
                    run[p] = r + _all_lanes(sums[:, 0:1], SB_BLOCK)

            @pl.when(qi % 2 == 0)
            def _():
                step(qi, True)

            @pl.when(qi % 2 == 1)
            def _():
                step(qi, True)
                step(qi - 1, False)

            first = qi - 1 - qi % 2

            def below(n, _):
                step(first - 2 * n, False)
                step(first - 1 - 2 * n, False)
                return 0

            lax.fori_loop(0, qi // 2, below, 0)
            for p in pairs:
                o_ref[pl.ds(r0, TQ), lanes[p]] = jnp.where(head0, acc[p, 0:TQ, :], acc[p, TQ:2 * TQ, :])
                tot_ref[p, 0, pl.ds(r0, TQ), :] = run[p, 0:TQ, :]
                tot_ref[p, 1, pl.ds(r0, TQ), :] = run[p, TQ:2 * TQ, :]
            return 0

        lax.fori_loop(0, S // TQ, qloop, 0)

    col = _attn_specs(S, n_pairs)
    state = pltpu.VMEM((n_pairs, 2 * TQ, SB_BLOCK), F32)
    return _hosted(
        body, job, (proj, proj, proj, from_here), name="attn_fwd",
        grid=(WIDTH // (n_pairs * SB_BLOCK),),
        in_specs=[col(0), col(4), col(8), pl.BlockSpec(from_here.shape, lambda p: (0, 0))],
        out_specs=[col(0), pl.BlockSpec((n_pairs, 2, S, 128), lambda p: (p, 0, 0, 0))],
        out_shape=[jax.ShapeDtypeStruct((S, WIDTH), F32), jax.ShapeDtypeStruct((WIDTH // 128, 2, S, 128), F32)],
        scratch_shapes=[state, state],
        semantics=("parallel",))


def _attn_bwd(proj, d_o, tot, from_here, up_to, dproj, job=None):
    S = proj.shape[0]
    TQ = SB_ROWS
    assert S % TQ == 0 and SB_KEYS == TQ
    scale = SB_HEAD_DIM ** -0.5
    n_pairs = SB_PAIRS_BWD
    pairs = range(n_pairs)
    lanes = [pl.ds(p * SB_BLOCK, SB_BLOCK) for p in pairs]

    def body(q_ref, k_ref, v_ref, do_ref, tot_ref, from_here_ref, up_to_ref, _, dproj_ref, pre, cum, dq_acc, dk_acc, dv_acc, stage, sems):
        dq_ref, dk_ref, dv_ref = stage.at[0], stage.at[1], stage.at[2]
        head0 = lax.broadcasted_iota(jnp.int32, (1, 2 * SB_HEAD_DIM), 1) < SB_HEAD_DIM
        dk_acc[...] = jnp.zeros_like(dk_acc)
        dv_acc[...] = jnp.zeros_like(dv_acc)

        def qloop(qi, _):
            r0 = pl.multiple_of(qi * TQ, TQ)
            q2n = [_attn_rows(q_ref, r0, TQ, lanes[p], head0, -scale) for p in pairs]
            do2 = [_attn_rows(do_ref, r0, TQ, lanes[p], head0) for p in pairs]
            trow = lax.broadcasted_iota(jnp.int32, (TQ, SB_KEYS), 0) + qi * TQ
            t2 = jnp.concatenate([trow, trow], axis=0)
            for p in pairs:
                pre[p, 0:TQ, :] = tot_ref[p, 0, pl.ds(r0, TQ), :]
                pre[p, TQ:2 * TQ, :] = tot_ref[p, 1, pl.ds(r0, TQ), :]
            cum[...] = jnp.zeros_like(cum)
            dq_acc[...] = jnp.zeros_like(dq_acc)

            def step(kj, masked):
                got = [_attn_scores(q2n[p], k_ref, lanes[p], kj, t2, from_here_ref, masked) for p in pairs]
                das = [_dot_nt(do2[p], v_ref[pl.ds(got[p][0], SB_KEYS), lanes[p]].astype(BF)) for p in pairs]
                for p in pairs:
                    c0, kb, zn, valid, lsb, sums = got[p]
                    later = pre[p] - _all_lanes(sums[:, 0:1], SB_BLOCK)
                    pre[p] = later
                    e = sums - zn + jnp.concatenate([later, later], axis=1)
                    sig = jnp.exp(lsb - zn)
                    if masked:
                        e = jnp.where(valid, e, -jnp.inf)
                        sig = jnp.where(valid, sig, 0.0)
                    a = jnp.exp(e)
                    w = das[p] * a
                    upto = _dot(w.astype(BF), up_to_ref[...])
                    c = cum[p]
                    dz = w - sig * (upto + jnp.concatenate([c, c], axis=1))
                    cum[p] = c + _all_lanes(upto[:, SB_KEYS - 1:SB_KEYS], SB_BLOCK)
                    dzb = dz.astype(BF)
                    dq_acc[p] += _dot(dzb, kb)
                    dk_acc[pl.ds(c0, SB_KEYS), lanes[p]] += _dot_tn(dzb, q2n[p])
                    dv_acc[pl.ds(c0, SB_KEYS), lanes[p]] += _dot_tn(a.astype(BF), do2[p])

            def below(n, _):
                step(2 * n, False)
                step(2 * n + 1, False)
                return 0

            lax.fori_loop(0, qi // 2, below, 0)

            @pl.when(qi % 2 == 1)
            def _():
                step(qi - 1, False)
                step(qi, True)

            @pl.when(qi % 2 == 0)
            def _():
                step(qi, True)
            for p in pairs:
                dq_ref[pl.ds(r0, TQ), lanes[p]] = (jnp.where(head0, dq_acc[p, 0:TQ, :], dq_acc[p, TQ:2 * TQ, :]) * scale).astype(BF)
            return 0

        lax.fori_loop(0, S // TQ, qloop, 0)
        dk_ref[...] = (-dk_acc[...]).astype(BF)
        dv_ref[...] = dv_acc[...].astype(BF)
        lane0 = pl.program_id(0) * (n_pairs * SB_BLOCK)
        _put_columns([dq_ref, dk_ref, dv_ref], dproj_ref, [WIDTH * n + lane0 for n in range(3)], sems)

    col = _attn_specs(S, n_pairs)
    whole = lambda a: pl.BlockSpec(a.shape, lambda p: (0, 0))
    hbm = pl.BlockSpec(memory_space=pl.ANY)
    state = pltpu.VMEM((n_pairs, 2 * TQ, SB_BLOCK), F32)
    grads = pltpu.VMEM((S, n_pairs * SB_BLOCK), F32)
    return _hosted(
        body, job, (proj, proj, proj, d_o, tot, from_here, up_to, dproj), name="attn_bwd",
        grid=(WIDTH // (n_pairs * SB_BLOCK),),
        in_specs=[col(0), col(4), col(8), col(0), pl.BlockSpec((n_pairs, 2, S, 128), lambda p: (p, 0, 0, 0)), whole(from_here),
                  whole(up_to), hbm],
        out_specs=[hbm],
        out_shape=[jax.ShapeDtypeStruct(dproj.shape, dproj.dtype)],
        scratch_shapes=[state, state, state, grads, grads, pltpu.VMEM((3, S, n_pairs * SB_BLOCK), BF), pltpu.SemaphoreType.DMA((3,))],
        semantics=("arbitrary",), aliases={7: 0})


HG_LEVELS = tuple(HG_CHUNK >> n for n in range(1, HG_CHUNK.bit_length()))


def _hgrn_consts():
    C = HG_CHUNK
    t = np.arange(C)[:, None]
    s = np.arange(C)[None, :]
    rows = [(s <= t), (s > t)]
    masks = [(t == s)]
    for m in HG_LEVELS:
        two = 2 * m
        mid = (t // two) * two + m
        right = (t % two) >= m
        rows.append((right & (s >= mid) & (s <= t)) | ((~right) & (s > t) & (s <= mid - 1)))
        masks.append(((t // two) == (s // two)) & right & ((s % two) < m))
    tri = np.concatenate(rows, axis=0).astype(np.float32)
    twice = lambda a: jnp.asarray(np.concatenate([a, a], axis=1), BF)
    return (twice(tri), twice(tri.T), jnp.asarray(np.stack(masks).astype(np.float32), F32))


HG_SUM_BLOCKS = 2 + len(HG_LEVELS)


def _split_rows(g):
    hi = g.astype(BF)
    return jnp.concatenate([hi, (g - hi.astype(F32)).astype(BF)], axis=0)


def _hgrn_sum_blocks(e):
    C = HG_CHUNK
    blocks = tuple(e[n * C:(n + 1) * C] for n in range(HG_SUM_BLOCKS))
    return blocks + (jnp.broadcast_to(e[C - 1:C], (HG_DIM, e.shape[1])),)


@jax.custom_vjp
def _hgrn_sums(tri, tri_t, g):
    return _hgrn_sum_blocks(_dot(tri, _split_rows(g)))


def _hgrn_sums_fwd(tri, tri_t, g):
    return _hgrn_sums(tri, tri_t, g), (tri, tri_t)


def _hgrn_sums_bwd(res, ds):
    tri, tri_t = res
    C = HG_CHUNK
    last = lax.broadcasted_iota(jnp.int32, (C, 1), 0) == C - 1
    prefix = ds[0] + jnp.where(last, jnp.sum(ds[-1], axis=0, keepdims=True), 0.0)
    d = jnp.concatenate((prefix,) + tuple(ds[1:-1]), axis=0)
    return jnp.zeros_like(tri), jnp.zeros_like(tri_t), _dot(tri_t[:, :d.shape[0]], d.astype(BF))


_hgrn_sums.defvjp(_hgrn_sums_fwd, _hgrn_sums_bwd)


def _bf_dot(a, b):
    return _dot(a.astype(BF), b.astype(BF))


def _bf_dot_nt(a, b):
    return _dot_nt(a.astype(BF), b.astype(BF))


def _bf_dot_tn(a, b):
    return _dot_tn(a.astype(BF), b.astype(BF))


@jax.custom_vjp
def _mm(a, b):
    return _bf_dot(a, b)


_mm.defvjp(lambda a, b: (_bf_dot(a, b), (a, b)), lambda r, ct: (_bf_dot_nt(ct, r[1]), _bf_dot_tn(r[0], ct)))


@jax.custom_vjp
def _mm_nt(a, b):
    return _bf_dot_nt(a, b)


_mm_nt.defvjp(lambda a, b: (_bf_dot_nt(a, b), (a, b)), lambda r, ct: (_bf_dot(ct, r[1]), _bf_dot_tn(ct, r[0])))


@jax.custom_vjp
def _mm_tn(a, b):
    return _bf_dot_tn(a, b)


_mm_tn.defvjp(lambda a, b: (_bf_dot_tn(a, b), (a, b)), lambda r, ct: (_bf_dot_nt(r[1], ct), _bf_dot(r[0], ct)))


def _hgrn_gates(tri, tri_t, qraw, fpre, lb):
    q = _silu(qraw)
    f = lb + (1.0 - lb) * _sigmoid(fpre)
    return q, 1.0 - f, _hgrn_sums(tri, tri_t, jnp.log(f))


def _hgrn_mix(masks, q, k, e, v, st):
    prefix, suffix, whole = e[0], e[1], e[-1]
    scores = masks[0] * _mm_nt(q, k)
    for n in range(len(HG_LEVELS)):
        decay = jnp.exp(e[2 + n])
        scores = scores + masks[n + 1] * _mm_nt(q * decay, k * decay)
    o = _mm_nt(q * jnp.exp(prefix), st) + _mm(scores, v)
    st_new = st * jnp.exp(whole) + _mm_tn(v, k * jnp.exp(suffix))
    return o, st_new


def _hgrn_chunk(tri, tri_t, masks, qraw, fpre, v, st, lb):
    q, k, e = _hgrn_gates(tri, tri_t, qraw, fpre, lb)
    return _hgrn_mix(masks, q, k, e, v, st)


HG_HEADS_PER_STEP = 4
HG_LANES = HG_HEADS_PER_STEP * HG_DIM


def _hgrn_specs(S, consts):
    col = lambda base: pl.BlockSpec((S, HG_LANES), lambda p, base=base: (0, base // HG_HEADS_PER_STEP + p))
    whole = [pl.BlockSpec(a.shape, lambda p, n=a.ndim: (0,) * n) for a in consts]
    return col, whole


def _hgrn_fwd(proj, lbs, consts, job=None):
    S = proj.shape[0]
    nc = S // HG_CHUNK
    heads = range(HG_HEADS_PER_STEP)

    def body(q_ref, f_ref, i_ref, lb_ref, tri_ref, trit_ref, mask_ref, o_ref, st_ref):
        tri, tri_t = tri_ref[...], trit_ref[...]
        masks = [mask_ref[n] for n in range(len(HG_LEVELS) + 1)]

        def chunk(ci, sts):
            r0 = pl.multiple_of(ci * HG_CHUNK, HG_CHUNK)
            rows = pl.ds(r0, HG_CHUNK)
            new = []
            lane = [pl.ds(hd * HG_DIM, HG_DIM) for hd in heads]
            gates = [_hgrn_gates(tri, tri_t, q_ref[rows, lane[hd]], f_ref[rows, lane[hd]], lb_ref[hd]) for hd in heads]
            for hd in heads:
                st_ref[hd, ci] = sts[hd]
                o, st_new = _hgrn_mix(masks, *gates[hd], i_ref[rows, lane[hd]], sts[hd])
                o_ref[rows, lane[hd]] = o
                new.append(st_new)
            return tuple(new)

        lax.fori_loop(0, nc, chunk, tuple(jnp.zeros((HG_DIM, HG_DIM), F32) for _ in heads))

    col, whole = _hgrn_specs(S, consts)
    return _hosted(
        body, job, (proj, proj, proj, lbs, *consts), name="hgrn_fwd",
        grid=(WIDTH // HG_LANES,),
        in_specs=[col(16), col(20), col(24), pl.BlockSpec((HG_HEADS_PER_STEP, 1, 128), lambda p: (p, 0, 0))] + whole,
        out_specs=[col(0), pl.BlockSpec((HG_HEADS_PER_STEP, nc, HG_DIM, HG_DIM), lambda p: (p, 0, 0, 0))],
        out_shape=[jax.ShapeDtypeStruct((S, WIDTH), F32), jax.ShapeDtypeStruct((WIDTH // 128, nc, HG_DIM, HG_DIM), F32)],
        semantics=("parallel",))


def _hgrn_bwd(proj, lbs, states, d_o, consts, dproj, job=None):
    S = proj.shape[0]
    nc = S // HG_CHUNK

    def body(q_ref, f_ref, i_ref, lb_ref, st_ref, do_ref, tri_ref, trit_ref, mask_ref, _, dlb_ref, dproj_ref, stage, sems):
        dq_ref, df_ref, di_ref = stage.at[0], stage.at[1], stage.at[2]
        masks = [mask_ref[n] for n in range(len(HG_LEVELS) + 1)]
        fn = functools.partial(_hgrn_chunk, tri_ref[...], trit_ref[...], masks)
        heads = range(HG_HEADS_PER_STEP)

        def chunk(n, carry):
            ci = nc - 1 - n
            r0 = pl.multiple_of(ci * HG_CHUNK, HG_CHUNK)
            rows = pl.ds(r0, HG_CHUNK)
            new = []
            lane = [pl.ds(hd * HG_DIM, HG_DIM) for hd in heads]
            pulls = [jax.vjp(fn, q_ref[rows, lane[hd]], f_ref[rows, lane[hd]], i_ref[rows, lane[hd]], st_ref[hd, ci], lb_ref[hd])[1]
                     for hd in heads]
            for hd in heads:
                d_st, dlb = carry[hd]
                lanes = lane[hd]
                dq, df, di, d_prev, dl = pulls[hd]((do_ref[rows, lanes], d_st))
                dq_ref[rows, lanes] = dq.astype(BF)
                df_ref[rows, lanes] = df.astype(BF)
                di_ref[rows, lanes] = di.astype(BF)
                new.append((d_prev, dlb + dl))
            return tuple(new)

        zero = (jnp.zeros((HG_DIM, HG_DIM), F32), jnp.zeros((1, HG_DIM), F32))
        done = lax.fori_loop(0, nc, chunk, tuple(zero for _ in heads))
        for hd in heads:
            dlb_ref[hd] = done[hd][1]
        lane0 = pl.program_id(0) * HG_LANES
        _put_columns([dq_ref, df_ref, di_ref], dproj_ref, [WIDTH * n + lane0 for n in (4, 5, 6)], sems)

    col, whole = _hgrn_specs(S, consts)
    head = pl.BlockSpec((HG_HEADS_PER_STEP, 1, 128), lambda p: (p, 0, 0))
    hbm = pl.BlockSpec(memory_space=pl.ANY)
    n_in = 6 + len(consts)
    return _hosted(
        body, job, (proj, proj, proj, lbs, states, d_o, *consts, dproj), name="hgrn_bwd",
        grid=(WIDTH // HG_LANES,),
        in_specs=[col(16), col(20), col(24), head, pl.BlockSpec((HG_HEADS_PER_STEP, nc, HG_DIM, HG_DIM), lambda p: (p, 0, 0, 0)), col(0)]
        + whole + [hbm],
        out_specs=[head, hbm],
        out_shape=[jax.ShapeDtypeStruct((WIDTH // 128, 1, 128), F32), jax.ShapeDtypeStruct(dproj.shape, dproj.dtype)],
        scratch_shapes=[pltpu.VMEM((3, S, HG_LANES), BF), pltpu.SemaphoreType.DMA((3,))],
        semantics=("arbitrary",), aliases={n_in: 1})


def _shift_down(x, n):
    rows = lax.broadcasted_iota(jnp.int32, x.shape, 0)
    return jnp.where(rows >= n, pltpu.roll(x, n, 0), 0.0)


def _shift_up(x, n):
    S = x.shape[0]
    rows = lax.broadcasted_iota(jnp.int32, x.shape, 0)
    return jnp.where(rows < S - n, pltpu.roll(x, S - n, 0), 0.0)


def _branch_fwd(proj, o_a, o_b, norm_w, conv_w, layer):
    S = proj.shape[0]

    def body(oa_ref, za_ref, ob_ref, zb_ref, nw_ref, pre_ref, post_ref, u_ref, zc_ref, cw_ref, ya_ref, yb_ref, yc_ref):
        ya_ref[...] = (oa_ref[...] * _silu(za_ref[...])).astype(BF)
        ob = ob_ref[...]
        rn = lax.rsqrt(jnp.mean(ob * ob, axis=-1, keepdims=True) + RMS_EPS)
        yb_ref[...] = (ob * rn * nw_ref[layer:layer + 1, :] * _silu(zb_ref[...])).astype(BF)
        pu = pre_ref[...] * u_ref[...]
        conv = cw_ref[2:3, :] * pu + cw_ref[1:2, :] * _shift_down(pu, 1) + cw_ref[0:1, :] * _shift_down(pu, 2)
        yc_ref[...] = (post_ref[...] * conv * _silu(zc_ref[...])).astype(BF)

    col = lambda base: pl.BlockSpec((S, 128), lambda p, base=base: (0, base + p))
    out = jax.ShapeDtypeStruct((S, WIDTH), BF)
    return pl.pallas_call(
        body, name="branch_fwd",
        grid=(WIDTH // 128,),
        in_specs=[col(0), col(12), col(0), col(28), pl.BlockSpec(norm_w.shape, lambda p: (0, 0)),
                  col(32), col(36), col(40), col(44), pl.BlockSpec((None, None, 3, 128), lambda p: (p, layer, 0, 0))],
        out_specs=[col(0), col(0), col(0)],
        out_shape=[out, out, out],
        compiler_params=_params(dimension_semantics=("parallel",)),
    )(o_a, proj, o_b, proj, norm_w, proj, proj, proj, proj, conv_w)


def _put_columns(tiles, dproj_ref, firsts, sems):
    copies = [pltpu.make_async_copy(t, dproj_ref.at[:, pl.ds(pl.multiple_of(c, 128), t.shape[1])], sems.at[n])
              for n, (t, c) in enumerate(zip(tiles, firsts))]
    for cp in copies:
        cp.start()
    for cp in copies:
        cp.wait()


def _branch_bwd(proj, o_a, o_b, norm_w, conv_w, dy_a, dy_b, dy_c, dproj, layer):
    S = proj.shape[0]
    firsts = [WIDTH * n for n in (3, 7, 8, 9, 10, 11)]

    def dsilu(z):
        s = _sigmoid(z)
        return s * z, s * (1.0 + z * (1.0 - s))

    def body(oa_ref, za_ref, ob_ref, zb_ref, nw_ref, pre_ref, post_ref, u_ref, zc_ref, cw_ref, dya_ref, dyb_ref, dyc_ref, _,
             doa_ref, dob_ref, dnw_ref, dcw_ref, dproj_ref, stage, sems):
        dza_ref, dzb_ref, dpre_ref, dpost_ref, du_ref, dzc_ref = [stage.at[n] for n in range(6)]
        dya = dya_ref[...]
        sa, dsa = dsilu(za_ref[...])
        doa_ref[...] = dya * sa
        dza_ref[...] = (dya * oa_ref[...] * dsa).astype(BF)

        dyb = dyb_ref[...]
        ob = ob_ref[...]
        nw = nw_ref[layer:layer + 1, :]
        sb, dsb = dsilu(zb_ref[...])
        rn = lax.rsqrt(jnp.mean(ob * ob, axis=-1, keepdims=True) + RMS_EPS)
        on = ob * rn
        dzb_ref[...] = (dyb * on * nw * dsb).astype(BF)
        don_w = dyb * sb
        dnw_ref[...] = jnp.sum(don_w * on, axis=0, keepdims=True)
        don = don_w * nw
        dob_ref[...] = rn * (don - on * jnp.mean(don * on, axis=-1, keepdims=True))

        dyc = dyc_ref[...]
        pre, post, u = pre_ref[...], post_ref[...], u_ref[...]
        sc, dsc = dsilu(zc_ref[...])
        pu = pre * u
        pu1, pu2 = _shift_down(pu, 1), _shift_down(pu, 2)
        conv = cw_ref[2:3, :] * pu + cw_ref[1:2, :] * pu1 + cw_ref[0:1, :] * pu2
        dzc_ref[...] = (dyc * post * conv * dsc).astype(BF)
        dpost_ref[...] = (dyc * conv * sc).astype(BF)
        dconv = dyc * post * sc
        dcw_ref[0:1, :] = jnp.sum(dconv * pu2, axis=0, keepdims=True)
        dcw_ref[1:2, :] = jnp.sum(dconv * pu1, axis=0, keepdims=True)
        dcw_ref[2:3, :] = jnp.sum(dconv * pu, axis=0, keepdims=True)
        dpu = cw_ref[2:3, :] * dconv + cw_ref[1:2, :] * _shift_up(dconv, 1) + cw_ref[0:1, :] * _shift_up(dconv, 2)
        dpre_ref[...] = (dpu * u).astype(BF)
        du_ref[...] = (dpu * pre).astype(BF)
        lane0 = pl.program_id(0) * 128
        _put_columns([stage.at[n] for n in range(6)], dproj_ref, [c + lane0 for c in firsts], sems)

    col = lambda base: pl.BlockSpec((S, 128), lambda p, base=base: (0, base + p))
    f32 = jax.ShapeDtypeStruct((S, WIDTH), F32)
    hbm = pl.BlockSpec(memory_space=pl.ANY)
    return pl.pallas_call(
        body, name="branch_bwd",
        grid=(WIDTH // 128,),
        in_specs=[col(0), col(12), col(0), col(28), pl.BlockSpec(norm_w.shape, lambda p: (0, 0)),
                  col(32), col(36), col(40), col(44), pl.BlockSpec((None, None, 3, 128), lambda p: (p, layer, 0, 0)),
                  col(0), col(0), col(0), hbm],
        out_specs=[col(0), col(0), pl.BlockSpec((None, 1, 128), lambda p: (p, 0, 0)), pl.BlockSpec((None, 3, 128), lambda p: (p, 0, 0)), hbm],
        out_shape=[f32, f32, jax.ShapeDtypeStruct((WIDTH // 128, 1, 128), F32), jax.ShapeDtypeStruct((WIDTH // 128, 3, 128), F32),
                   jax.ShapeDtypeStruct(dproj.shape, dproj.dtype)],
        scratch_shapes=[pltpu.VMEM((6, S, 128), BF), pltpu.SemaphoreType.DMA((6,))],
        input_output_aliases={13: 4},
        compiler_params=_params(dimension_semantics=("arbitrary",)),
    )(o_a, proj, o_b, proj, norm_w, proj, proj, proj, proj, conv_w, dy_a, dy_b, dy_c, dproj)


def _branch_proj(y_refs, wb_ref):
    return [_dot(y_refs[i][...], wb_ref[i]) for i in range(3)]


def _merge_fwd(x, mod, proj, ys, wb, wo, ln_g, ln_b, layer, job=None, tm=512):
    S, D = x.shape
    tm = min(tm, S)

    def body(x_ref, mod_ref, ga_ref, gb_ref, gc_ref, ya_ref, yb_ref, yc_ref, wb_ref, wo_ref, g_ref, b_ref, xo_ref, mg_ref, y_ref):
        ps = _branch_proj((ya_ref, yb_ref, yc_ref), wb_ref)
        merged = _sigmoid(ga_ref[...]) * ps[0] + _sigmoid(gb_ref[...]) * ps[1] + _sigmoid(gc_ref[...]) * ps[2]
        mb = merged.astype(BF)
        mg_ref[...] = mb
        y = _dot(mb, wo_ref[...].reshape(D, D))
        y_ref[...] = y
        r = ALPHA * x_ref[...] + (1.0 + mod_ref[:, 2 * D:3 * D]) * y
        xn, _ = _standardize(r)
        xo_ref[...] = xn * g_ref[layer:layer + 1, :] + b_ref[layer:layer + 1, :]

    row = lambda w, c=0: pl.BlockSpec((tm, w), lambda i, c=c: (i, c))
    whole = lambda a: pl.BlockSpec(a.shape, lambda i, n=a.ndim: (0,) * n)
    return _hosted(
        body, job, (x, mod, proj, proj, proj, *ys, wb, wo, ln_g, ln_b), name="merge_fwd",
        grid=(S // tm,),
        in_specs=[row(D), whole(mod), row(D, 6), row(D, 7), row(D, 8), row(WIDTH), row(WIDTH), row(WIDTH), whole(wb),
                  whole(wo), whole(ln_g), whole(ln_b)],
        out_specs=[row(D), row(D), row(D)],
        out_shape=[jax.ShapeDtypeStruct((S, D), F32), jax.ShapeDtypeStruct((S, D), BF), jax.ShapeDtypeStruct((S, D), F32)],
        semantics=("parallel",))


def _merge_bwd(dxo, x, y, merged, mod, proj, ys, wb, wo, ln_g, layer, tm=256):
    S, D = x.shape
    tm = min(tm, S)
    steps = S // tm
    quarter = D // N_CHIPS

    def body(dxo_ref, x_ref, y_ref, mg_ref, mod_ref, ga_ref, gb_ref, gc_ref, ya_ref, yb_ref, yc_ref, wb_ref, wo_ref, g_ref,
             dxr_ref, dg_ref, dya_ref, dyb_ref, dyc_ref, dlg_ref, dlb_ref, dgt_ref, gbr_ref, gout_ref, acc_br, acc_out):
        @pl.when(pl.program_id(0) == 0)
        def _():
            dlg_ref[...] = jnp.zeros_like(dlg_ref)
            dlb_ref[...] = jnp.zeros_like(dlb_ref)
            dgt_ref[...] = jnp.zeros_like(dgt_ref)
            acc_br[...] = jnp.zeros_like(acc_br)
            acc_out[...] = jnp.zeros_like(acc_out)

        gate1 = 1.0 + mod_ref[:, 2 * D:3 * D]
        yv = y_ref[...]
        xn, rstd = _standardize(ALPHA * x_ref[...] + gate1 * yv)
        dxo = dxo_ref[...]
        dlg_ref[...] += jnp.sum(dxo * xn, axis=0, keepdims=True)
        dlb_ref[...] += jnp.sum(dxo, axis=0, keepdims=True)
        dr = _standardize_bwd(dxo * g_ref[layer:layer + 1, :], xn, rstd)
        dxr_ref[...] = ALPHA * dr
        dgt_ref[...] += jnp.sum(dr * yv, axis=0, keepdims=True)
        dyb = (gate1 * dr).astype(BF)
        acc_out[...] += _dot_tn(mg_ref[...], dyb)
        dmerged = _dot_nt(dyb, wo_ref[...].reshape(D, D))
        y_refs = (ya_ref, yb_ref, yc_ref)
        ps = _branch_proj(y_refs, wb_ref)
        for i, (gate_ref, out_ref) in enumerate(((ga_ref, dya_ref), (gb_ref, dyb_ref), (gc_ref, dyc_ref))):
            sg = _sigmoid(gate_ref[...])
            dg_ref[:, i * D:(i + 1) * D] = (dmerged * ps[i] * sg * (1.0 - sg)).astype(BF)
            dp = (dmerged * sg).astype(BF)
            acc_br[i] += _dot_tn(y_refs[i][...], dp)
            out_ref[...] = _dot_nt(dp, wb_ref[i])

        @pl.when(pl.program_id(0) == steps - 1)
        def _():
            for j in range(N_CHIPS):
                gout_ref[j] = acc_out[j * quarter:(j + 1) * quarter, :].astype(BF)
                for i in range(3):
                    gbr_ref[j, i] = acc_br[i, :, j * quarter:(j + 1) * quarter].astype(BF)

    row = lambda w, c=0: pl.BlockSpec((tm, w), lambda i, c=c: (i, c))
    whole = lambda a: pl.BlockSpec(a.shape, lambda i, n=len(a.shape): (0,) * n)
    vec = pl.BlockSpec((1, D), lambda i: (0, 0))
    sd = jax.ShapeDtypeStruct
    g_br, g_out = sd((N_CHIPS, 3, WIDTH, quarter), BF), sd((N_CHIPS, quarter, D), BF)
    return pl.pallas_call(
        body, name="merge_bwd",
        grid=(steps,),
        in_specs=[row(D), row(D), row(D), row(D), whole(mod), row(D, 6), row(D, 7), row(D, 8), row(WIDTH), row(WIDTH), row(WIDTH),
                  whole(wb), whole(wo), whole(ln_g)],
        out_specs=[row(D), row(3 * D, IN_COLS // (3 * D) - 1), row(WIDTH), row(WIDTH), row(WIDTH), vec, vec, vec, whole(g_br), whole(g_out)],
        out_shape=[sd((S, D), F32), sd((S, IN_COLS), BF), sd((S, WIDTH), F32), sd((S, WIDTH), F32), sd((S, WIDTH), F32),
                   sd((1, D), F32), sd((1, D), F32), sd((1, D), F32), g_br, g_out],
        scratch_shapes=[pltpu.VMEM((3, WIDTH, D), F32), pltpu.VMEM((D, D), F32)],
        compiler_params=_params(dimension_semantics=("arbitrary",)),
    )(dxo, x, y, merged, mod, proj, proj, proj, *ys, wb, wo, ln_g)


def _loss_head(x, target, tm=512):
    S, D = x.shape
    tm = min(tm, S)

    def body(x_ref, t_ref, dx_ref, loss_ref):
        @pl.when(pl.program_id(0) == 0)
        def _():
            loss_ref[...] = jnp.zeros_like(loss_ref)

        err = x_ref[...] - t_ref[...]
        dx_ref[...] = err * (1.0 / D)
        loss_ref[...] += 0.5 * jnp.sum(jnp.mean(err * err, axis=-1, keepdims=True))

    row = pl.BlockSpec((tm, D), lambda i: (i, 0))
    return pl.pallas_call(
        body, name="loss_head",
        grid=(S // tm,),
        in_specs=[row, row],
        out_specs=[row, pl.BlockSpec((8, 128), lambda i: (0, 0))],
        out_shape=[jax.ShapeDtypeStruct((S, D), F32), jax.ShapeDtypeStruct((8, 128), F32)],
        compiler_params=_params(dimension_semantics=("arbitrary",)),
    )(x, target)


def _proj_bwd(dproj, wgs, x, mod, dx_res, job=None, tm=512, tk=768):
    S, D = x.shape
    tm = min(tm, S)
    n = len(wgs)
    w_args = list(wgs)
    if n == 1:
        per = wgs[0].shape[-1] // tk
        w_specs = [pl.BlockSpec((None, D, tk), lambda k, i: (k // per, 0, k % per))]
    else:
        assert wgs[0].shape[-1] == tk
        w_specs = [pl.BlockSpec((None, D, tk), lambda k, i, c=c: (jnp.minimum((k + n - 1 - c) // n, N_CHIPS - 1), 0, 0))
                   for c in range(n)]
    nk = IN_COLS // tk

    def body(dp_ref, *rest):
        w_refs, (x_ref, mod_ref, dxr_ref, dx_ref, dsh_ref, dsc_ref, acc) = rest[:n], rest[n:]
        k, i = pl.program_id(0), pl.program_id(1)
        mine = pl.ds(pl.multiple_of(i * tm, tm), tm)

        @pl.when((i == 0) & (k == 0))
        def _():
            dsh_ref[...] = jnp.zeros_like(dsh_ref)
            dsc_ref[...] = jnp.zeros_like(dsc_ref)

        @pl.when(k == 0)
        def _():
            acc[mine, :] = jnp.zeros((tm, D), F32)

        for c in range(n):
            @pl.when(k % n == c)
            def _(c=c):
                acc[mine, :] += _dot_nt(dp_ref[...], w_refs[c][...])

        @pl.when(k == nk - 1)
        def _():
            dh = acc[mine, :]
            xs, rstd = _standardize(x_ref[...])
            dsh_ref[...] += jnp.sum(dh, axis=0, keepdims=True)
            dsc_ref[...] += jnp.sum(dh * xs, axis=0, keepdims=True)
            dx_ref[...] = _standardize_bwd(dh * (1.0 + mod_ref[:, D:2 * D]), xs, rstd) + dxr_ref[...]

    row = pl.BlockSpec((tm, D), lambda k, i: (jnp.where(k == nk - 1, i, 0), 0))
    vec = pl.BlockSpec((1, D), lambda k, i: (0, 0))
    return _hosted(
        body, job, (dproj, *w_args, x, mod, dx_res), name="proj_bwd",
        grid=(nk, S // tm),
        in_specs=[pl.BlockSpec((tm, tk), lambda k, i: (i, k))] + w_specs + [row, pl.BlockSpec((1, 3 * D), lambda k, i: (0, 0)), row],
        out_specs=[row, vec, vec],
        out_shape=[jax.ShapeDtypeStruct((S, D), F32), jax.ShapeDtypeStruct((1, D), F32), jax.ShapeDtypeStruct((1, D), F32)],
        scratch_shapes=[pltpu.VMEM((S, D), F32)],
        semantics=("arbitrary", "arbitrary"))


def _grad_w_in(h, dproj):
    S, D = h.shape
    shard = IN_COLS // N_CHIPS

    def body(h_ref, d_ref, o_ref):
        o_ref[...] = _dot_tn(h_ref[...], d_ref[...]).astype(BF)

    return pl.pallas_call(
        body, name="grad_w_in",
        grid=(N_CHIPS,),
        in_specs=[pl.BlockSpec((S, D), lambda n: (0, 0)), pl.BlockSpec((S, shard), lambda n: (0, n))],
        out_specs=pl.BlockSpec((None, D, shard), lambda n: (n, 0, 0)),
        out_shape=jax.ShapeDtypeStruct((N_CHIPS, D, shard), BF),
        compiler_params=_params(dimension_semantics=("parallel",)),
    )(h, dproj)


def _all_gather8(x, name):
    R, N = x.shape

    def body(x_ref, out_ref, send_sems, recv_sems):
        mx, my, mc = lax.axis_index("x"), lax.axis_index("y"), lax.axis_index("c")
        me = 4 * mx + 2 * my + mc
        out_ref[me] = x_ref[...]
        copies = []
        for k in range(1, N_DEV):
            peer = (_flip(mx, k & 4), _flip(my, k & 2), _flip(mc, k & 1))
            cp = pltpu.make_async_remote_copy(src_ref=x_ref, dst_ref=out_ref.at[me], send_sem=send_sems.at[k - 1],
                                              recv_sem=recv_sems.at[k - 1], device_id=peer, device_id_type=MESH)
            cp.start()
            copies.append(cp)
        for cp in copies:
            cp.wait()

    return pl.pallas_call(
        body, name=name,
        in_specs=[pl.BlockSpec(memory_space=pltpu.VMEM)],
        out_specs=pl.BlockSpec(memory_space=pltpu.VMEM),
        out_shape=jax.ShapeDtypeStruct((N_DEV, R, N), F32),
        scratch_shapes=[pltpu.SemaphoreType.DMA((N_DEV - 1,)), pltpu.SemaphoreType.DMA((N_DEV - 1,))],
        compiler_params=_params(),
    )(x)


def _rows2d(a):
    return a.reshape(-1, a.shape[-1])


def _tile_rows(rows, cols, n_arrays):
    budget = (24 << 20) // (n_arrays * 2 * 4 * cols)
    if rows <= budget:
        return rows
    tm = 8
    for cand in range(8, budget + 1, 8):
        if rows % cand == 0:
            tm = cand
    return tm


SUM_ROWS = 256


def _sum_cores(g, sent, where):
    chips, lead, _, r, cols = g.shape
    tr = min(r, SUM_ROWS)

    def body(where_ref, g_ref, s_ref, out_ref):
        out_ref[...] = (g_ref[...].astype(F32) + s_ref[...].astype(F32)).astype(BF)

    spec = pl.BlockSpec((None, tr, cols), lambda i, j, where_ref: (i, j, 0))
    out = pl.pallas_call(
        body, name="sum_cores",
        grid_spec=pltpu.PrefetchScalarGridSpec(
            num_scalar_prefetch=1, grid=(chips * lead, r // tr),
            in_specs=[pl.BlockSpec((None, None, tr, cols), lambda i, j, where_ref: (i, where_ref[1], j, 0)), spec],
            out_specs=spec),
        out_shape=jax.ShapeDtypeStruct((chips * lead, r, cols), BF),
        compiler_params=_params(dimension_semantics=("parallel", "parallel")),
    )(where, g.reshape(chips * lead, 2, r, cols), sent.reshape(chips * lead, r, cols))
    return out.reshape(chips, lead, r, cols)


def _sum_chips(sums, got, where):
    _, lead, r, cols = sums.shape
    tr = min(r, SUM_ROWS // 2)

    def body(where_ref, s_ref, g_ref, out_ref):
        out_ref[...] = ((s_ref[...].astype(F32) + g_ref[0].astype(F32)) + g_ref[1].astype(F32)) + g_ref[2].astype(F32)

    return pl.pallas_call(
        body, name="sum_chips",
        grid_spec=pltpu.PrefetchScalarGridSpec(
            num_scalar_prefetch=1, grid=(lead, r // tr),
            in_specs=[pl.BlockSpec((None, None, tr, cols), lambda i, j, where_ref: (where_ref[0], i, j, 0)),
                      pl.BlockSpec((N_CHIPS - 1, None, tr, cols), lambda i, j, where_ref: (0, i, j, 0))],
            out_specs=pl.BlockSpec((None, None, tr, cols), lambda i, j, where_ref: (i, where_ref[1], j, 0))),
        out_shape=jax.ShapeDtypeStruct((lead, 2, r, cols), F32),
        compiler_params=_params(dimension_semantics=("parallel", "parallel")),
    )(where, sums, got)


def _adamw(w, m, v, groups, name):
    shape = w.shape
    w2, m2, v2 = _rows2d(w), _rows2d(m), _rows2d(v)
    rows, cols = w2.shape
    ng = len(groups)
    n = len(groups[0])
    slab = rows // ng
    gs = [_rows2d(g) for grp in groups for g in grp]
    tm = _tile_rows(slab, cols, 7 + n)
    tiles = slab // tm
    c1 = 1.0 / (1.0 - ADAM_B1 ** ADAM_STEP)
    c2 = 1.0 / (1.0 - ADAM_B2 ** ADAM_STEP)

    def body(*refs):
        w_ref, m_ref, v_ref = refs[:3]
        g_refs = refs[3:3 + ng * n]
        go_ref, d_ref, mo_ref, vo_ref = refs[3 + ng * n:]
        which = pl.program_id(0)
        for s in range(ng):
            @pl.when(which == s)
            def _(s=s):
                g = g_refs[s * n][...]
                for r in g_refs[s * n + 1:(s + 1) * n]:
                    g = g + r[...]
                mn = ADAM_B1 * m_ref[...] + (1.0 - ADAM_B1) * g
                vn = ADAM_B2 * v_ref[...] + (1.0 - ADAM_B2) * (g * g)
                go_ref[...] = g
                mo_ref[...] = mn
                vo_ref[...] = vn
                d_ref[...] = -ADAM_LR * ((mn * c1) / (jnp.sqrt(vn * c2) + ADAM_EPS) + ADAM_WD * w_ref[...])

    spec = pl.BlockSpec((tm, cols), lambda s, i: (s * tiles + i, 0))
    g_specs = [pl.BlockSpec((tm, cols), lambda s, i, k=k: (jnp.where(s == k, i, jnp.where(s < k, 0, tiles - 1)), 0))
               for k in range(ng) for _ in range(n)]
    outs = pl.pallas_call(
        body, name=name,
        grid=(ng, tiles),
        in_specs=[spec] * 3 + g_specs,
        out_specs=[spec] * 4,
        out_shape=[jax.ShapeDtypeStruct((rows, cols), F32)] * 4,
        compiler_params=_params(dimension_semantics=("arbitrary", "arbitrary")),
    )(w2, m2, v2, *gs)
    return [o.reshape(shape) for o in outs]


def _lower_bounds(r0, r1):
    top = jnp.maximum(r0, r1)
    e0, e1 = jnp.exp(r0 - top), jnp.exp(r1 - top)
    p0, p1 = e0 / (e0 + e1), e1 / (e0 + e1)
    return p0 - p0, (p0 + p1) - p0


def _lbs_fwd(lb_raw):
    def body(lb_ref, out_ref):
        l0, l1 = _lower_bounds(lb_ref[0:1, :], lb_ref[1:2, :])
        out_ref[0:1, :] = l0
        out_ref[1:2, :] = l1

    return pl.pallas_call(body, name="lower_bounds", out_shape=jax.ShapeDtypeStruct(lb_raw.shape, F32), compiler_params=_params())(lb_raw)


def _mod_rows(c_all, w_mod, tn=768):
    _, D, cols = w_mod.shape

    def body(c_ref, w_ref, out_ref):
        out_ref[...] = _dot(c_ref[...].astype(BF), w_ref[...].astype(BF))

    return pl.pallas_call(
        body, name="mod_rows",
        grid=(DEPTH,),
        in_specs=[pl.BlockSpec((N_DEV, D), lambda l: (0, 0)), pl.BlockSpec((None, D, cols), lambda l: (l, 0, 0))],
        out_specs=pl.BlockSpec((N_DEV, cols), lambda l: (0, l)),
        out_shape=jax.ShapeDtypeStruct((N_DEV, DEPTH * cols), F32),
        compiler_params=_params(dimension_semantics=("parallel",)),
    )(c_all, w_mod)


def _grad_w_mod(c_all, dmod_cols):
    D = c_all.shape[1]
    cols = dmod_cols.shape[-1]

    def body(c_ref, d_ref, out_ref):
        out_ref[...] = _dot_tn(c_ref[...].astype(BF), d_ref[...].astype(BF))

    return pl.pallas_call(
        body, name="grad_w_mod",
        grid=(DEPTH,),
        in_specs=[pl.BlockSpec((N_DEV, D), lambda l: (0, 0)), pl.BlockSpec((None, N_DEV, cols), lambda l: (l, 0, 0))],
        out_specs=pl.BlockSpec((None, D, cols), lambda l: (l, 0, 0)),
        out_shape=jax.ShapeDtypeStruct((DEPTH, D, cols), F32),
        compiler_params=_params(dimension_semantics=("parallel",)),
    )(c_all, dmod_cols)


def _sum_devices(parts):
    _, R, N = parts.shape

    def body(p_ref, out_ref):
        acc = p_ref[0]
        for d in range(1, N_DEV):
            acc = acc + p_ref[d]
        out_ref[...] = acc

    return pl.pallas_call(body, name="sum_devices", out_shape=jax.ShapeDtypeStruct((R, N), F32), compiler_params=_params())(parts)


def _lbs_bwd(lb_raw, dl):
    def body(lb_ref, dl_ref, out_ref):
        _, pull = jax.vjp(_lower_bounds, lb_ref[0:1, :], lb_ref[1:2, :])
        d0, d1 = pull((dl_ref[0:1, :], dl_ref[1:2, :]))
        out_ref[0:1, :] = d0
        out_ref[1:2, :] = d1

    return pl.pallas_call(body, name="lower_bounds_bwd", out_shape=jax.ShapeDtypeStruct(lb_raw.shape, F32), compiler_params=_params())(lb_raw, dl)


def kernel(x, c, w_mod, b_mod, w_in, conv_w, hgrn_norm_w, lower_bounds, w_branch, w_out, ln_g, ln_b, loss_target, m_w_mod, m_b_mod, m_w_in, m_conv_w, m_hgrn_norm_w, m_lower_bounds, m_w_branch, m_w_out, m_ln_g, m_ln_b, v_w_mod, v_b_mod, v_w_in, v_conv_w, v_hgrn_norm_w, v_lower_bounds, v_w_branch, v_w_out, v_ln_g, v_ln_b):
    D = D_MODEL
    x0 = x[0]
    target = loss_target[0]
    S = x0.shape[0]
    mx, my, mc = lax.axis_index("x"), lax.axis_index("y"), lax.axis_index("c")
    chip = 2 * mx + my
    me = 2 * chip + mc
    mod_cols = 3 * D // N_CHIPS

    plan = _Plan(w_in, w_branch, w_out, chip, mc)
    n_conv = DEPTH * 3 * (WIDTH // N_CHIPS)
    first = jnp.concatenate([c, conv_w.reshape(1, n_conv), jnp.zeros((1, 2 * D - D - n_conv), F32)], axis=1)
    first = plan.first(first.reshape(8, 2 * D // 8)).reshape(N_DEV, 2 * D)

    c_all = first[:, :D]
    conv_all = first[:, D:D + n_conv].reshape(N_DEV, DEPTH, 3, WIDTH // N_CHIPS)[0::2]
    mod_part = _all_gather8(_mod_rows(c_all, w_mod), "gather_mod")[0::2]
    mod_part = lax.dynamic_index_in_dim(mod_part, me, axis=1, keepdims=False).reshape(N_CHIPS, DEPTH, mod_cols)
    mods = [(mod_part[:, l].reshape(1, 3 * D) + b_mod[l][None, :]) for l in range(DEPTH)]
    lbs = _lbs_fwd(lower_bounds).reshape(DEPTH, WIDTH // 128, 1, 128)
    loss_blk, dx, small = _local_step(x0, target, mods, lbs, conv_all, hgrn_norm_w, ln_g, ln_b, plan)

    n_mod, n_nw, n_lb, n_ln, n_cw = DEPTH * 3 * D, DEPTH * 128, DEPTH * WIDTH, DEPTH * D, DEPTH * 3 * WIDTH
    row = jnp.concatenate(
        [jnp.concatenate([small[l][0], small[l][1], small[l][2]], axis=1) for l in range(DEPTH)]
        + [jnp.sum(small[l][3], axis=0) for l in range(DEPTH)]
        + [small[l][4].reshape(1, WIDTH) for l in range(DEPTH)]
        + [small[l][5] for l in range(DEPTH)] + [small[l][6] for l in range(DEPTH)]
        + [jnp.transpose(small[l][7], (1, 0, 2)).reshape(1, 3 * WIDTH) for l in range(DEPTH)]
        + [loss_blk[0:1, :]], axis=1)
    n_row = row.shape[1]
    fold = -(-n_row // (8 * 128)) * 128
    rows = jnp.concatenate([row, jnp.zeros((1, 8 * fold - n_row), F32)], axis=1).reshape(8, fold)

    whole, gathered = plan.finish(rows)
    grads = {kind: [[whole[(kind, l)]] for l in range(DEPTH)] for kind in ("in", "br", "out")}

    off_nw = n_mod
    off_lb = off_nw + n_nw
    off_lng = off_lb + n_lb
    off_lnb = off_lng + n_ln
    off_cw = off_lnb + n_ln
    off_loss = off_cw + n_cw
    total = _sum_devices(gathered).reshape(1, 8 * fold)
    gathered = gathered.reshape(N_DEV, 1, 8 * fold)
    d_lower = _lbs_bwd(lower_bounds, total[0, off_lb:off_lng].reshape(DEPTH, WIDTH))
    loss = total[0, off_loss]
    d_b_mod = total[0, :n_mod].reshape(DEPTH, 3 * D)
    d_norm_w = total[0, off_nw:off_lb].reshape(DEPTH, 128)
    d_ln_g = total[0, off_lng:off_lnb].reshape(DEPTH, D)
    d_ln_b = total[0, off_lnb:off_cw].reshape(DEPTH, D)
    d_conv = total[0, off_cw:off_loss].reshape(DEPTH, 3, N_CHIPS, WIDTH // N_CHIPS)
    d_conv = lax.dynamic_index_in_dim(d_conv, chip, axis=2, keepdims=False)
    dmod_all = gathered[:, 0, :n_mod].reshape(N_DEV, DEPTH, N_CHIPS, mod_cols)
    dmod_cols = jnp.transpose(lax.dynamic_index_in_dim(dmod_all, chip, axis=2, keepdims=False), (1, 0, 2))
    d_w_mod = _grad_w_mod(c_all, dmod_cols)

    res = {}
    res["w_mod"] = _adamw(w_mod, m_w_mod, v_w_mod, [[d_w_mod]], "adamw_w_mod")
    res["b_mod"] = _adamw(b_mod, m_b_mod, v_b_mod, [[d_b_mod]], "adamw_b_mod")
    res["w_in"] = _adamw(w_in, m_w_in, v_w_in, grads["in"], "adamw_w_in")
    res["conv_w"] = _adamw(conv_w, m_conv_w, v_conv_w, [[d_conv]], "adamw_conv_w")
    res["hgrn_norm_w"] = _adamw(hgrn_norm_w, m_hgrn_norm_w, v_hgrn_norm_w, [[d_norm_w]], "adamw_norm_w")
    res["lower_bounds"] = _adamw(lower_bounds, m_lower_bounds, v_lower_bounds, [[d_lower]], "adamw_lower_bounds")
    res["w_branch"] = _adamw(w_branch, m_w_branch, v_w_branch, grads["br"], "adamw_w_branch")
    res["w_out"] = _adamw(w_out, m_w_out, v_w_out, grads["out"], "adamw_w_out")
    res["ln_g"] = _adamw(ln_g, m_ln_g, v_ln_g, [[d_ln_g]], "adamw_ln_g")
    res["ln_b"] = _adamw(ln_b, m_ln_b, v_ln_b, [[d_ln_b]], "adamw_ln_b")
    names = ["w_mod", "b_mod", "w_in", "conv_w", "hgrn_norm_w", "lower_bounds", "w_branch", "w_out", "ln_g", "ln_b"]
    return (loss, dx[None], *[res[n][0] for n in names], *[res[n][1] for n in names],
            *[res[n][2] for n in names], *[res[n][3] for n in names])


class _Plan:
    FIRST_CHUNKS = 2
    WINDOWS = ((0, 1280), (1280, 640), (1920, 384))

    def __init__(self, w_in, w_br, w_out, chip, core):
        self.local = {"in": w_in, "br": w_br, "out": w_out}
        self.chip, self.where = chip, jnp.stack([chip, core]).astype(jnp.int32)
        self.gathered, self.partial, self.grads, self.chip_sums, self.scattered, self.pending = {}, {}, {}, {}, {}, {}

    def chunks(self, l):
        return self.FIRST_CHUNKS if l == 0 else 1

    def w_in(self, l):
        return [self.gathered[("in", l, c)] for c in range(self.chunks(l))]

    def _shard(self, key):
        mine = self.local[key[0]][key[1]]
        if key[0] == "in":
            cols = mine.shape[-1] // self.chunks(key[1])
            mine = mine[:, key[2] * cols:(key[2] + 1) * cols]
        return mine.astype(BF)

    def _slab(self, key):
        mine = _halves(self._shard(key))
        return lax.dynamic_update_slice(lax.empty((N_CHIPS,) + mine.shape, mine.dtype), mine[None], (self.chip, 0, 0, 0, 0))

    def _gather(self, keys):
        return ("gather", keys), _gather_job([self._slab(key) for key in keys])

    def _fetch(self, keys):
        return ("gather_part", keys), _gather_job([self._slab(key) for key in keys], None, ())

    def _hand_over(self, keys):
        return ("gather", keys), _gather_job([self.partial[key] for key in keys], (), (None,))

    def _gather_window(self, key, n):
        slab = self._slab(key) if n == 0 else self.partial[key]
        last = n == len(self.WINDOWS) - 1
        handed = self.WINDOWS[max(n - 1, 0):n] + (self.WINDOWS[n:] if last else ())
        return ("gather" if last else "gather_part", [key]), _gather_job([slab], self.WINDOWS[n], handed)

    def _to_sibling(self, keys):
        return ("to_sibling", keys), _to_sibling_job([_halves(self.grads[key], 1) for key in keys])

    def _scatter(self, keys):
        return ("scatter", keys), _scatter_job([self.chip_sums[key] for key in keys])

    def job(self, stage, l, c=0):
        parts = []
        if stage == "proj_fwd":
            parts = [self._gather([("in", l, c + 1)]) if c + 1 < self.chunks(l) else self._fetch([("br", l), ("out", l)])]
        elif stage == "attn_fwd":
            parts = [self._hand_over([("br", l), ("out", l)])] + ([self._gather_window(("in", l + 1, 0), 0)] if l + 1 < DEPTH else [])
        elif stage == "hgrn_fwd" and l + 1 < DEPTH:
            parts = [self._gather_window(("in", l + 1, 0), 1)]
        elif stage == "merge_fwd" and l + 1 < DEPTH:
            parts = [self._gather_window(("in", l + 1, 0), 2)]
        elif stage == "attn_bwd":
            parts = [self._to_sibling([("out", l), ("br", l)])] + ([self._scatter([("in", l + 1)])] if l + 1 < DEPTH else [])
        elif stage == "hgrn_bwd":
            parts = [self._scatter([("out", l), ("br", l)])]
        elif stage == "proj_bwd":
            parts = [self._to_sibling([("in", l)])] if l else [self._scatter([("in", 0)])]
        self.pending[(stage, l, c)] = [(tag, len(job.outs)) for tag, job in parts]
        return _join_jobs([job for _, job in parts])

    def done(self, stage, l, outs, c=0):
        if outs is None:
            return
        at = 0
        for (what, keys), n_outs in self.pending[(stage, l, c)]:
            mine, at = outs[at:at + n_outs], at + n_outs
            for n, key in enumerate(keys):
                if what == "gather":
                    self.gathered[key] = mine[n].reshape((N_CHIPS,) + self._shard(key).shape)
                elif what == "gather_part":
                    self.partial[key] = mine[n]
                elif what == "to_sibling":
                    self.chip_sums[key] = _sum_cores(_halves(self.grads[key], 1), mine[n], self.where)
                else:
                    self.scattered[key] = mine[n]

    def first(self, rows):
        tag, job = self._gather([("in", 0, 0)])
        self.pending[("first", 0, 0)] = [(tag, len(job.outs))]
        outs = _run_job(_join_jobs([job, _gather8_job(rows)]), "gather_first")
        self.done("first", 0, outs[:-1])
        return outs[-1]

    def took(self, key, grad):
        self.grads[key] = grad
        if key == ("in", 0):
            tag, job = self._to_sibling([key])
            self.pending[("took", 0, 0)] = [(tag, len(job.outs))]
            self.done("took", 0, _run_job(job, "to_sibling_last"))

    def finish(self, rows):
        keys = [(kind, l) for kind in ("in", "br", "out") for l in range(DEPTH)]
        halves = [_sum_chips(self.chip_sums[key], self.scattered[key], self.where) for key in keys]
        outs = _run_job(_join_jobs([_place_job(halves), _gather8_job(rows)]), "place_halves")
        return {key: w.reshape(self.grads[key].shape[1:]) for key, w in zip(keys, outs[:-1])}, outs[-1]


def _local_step(x0, target, mods, lbs, conv_all, hgrn_norm_w, ln_g, ln_b, plan):
    D = D_MODEL
    after, before = _attn_consts()
    hg_consts = _hgrn_consts()

    saved = []
    xl = x0
    for l in range(DEPTH):
        n = plan.chunks(l)
        (proj, h), got = _proj_fwd(xl, mods[l], plan.gathered[("in", l, 0)], plan.job("proj_fwd", l, 0), (0, n))
        plan.done("proj_fwd", l, got, 0)
        for c in range(1, n):
            (proj,), got = _proj_cols(h, plan.gathered[("in", l, c)], (c, n), proj, plan.job("proj_fwd", l, c))
            plan.done("proj_fwd", l, got, c)
        (o_a, tot), got = _attn_fwd(proj, after, plan.job("attn_fwd", l))
        plan.done("attn_fwd", l, got)
        (o_b, states), got = _hgrn_fwd(proj, lbs[l], hg_consts, plan.job("hgrn_fwd", l))
        plan.done("hgrn_fwd", l, got)
        ys = _branch_fwd(proj, o_a, o_b, hgrn_norm_w, conv_all, l)
        wb = jnp.concatenate(list(plan.gathered[("br", l)]), axis=-1)
        (x_next, merged, y), got = _merge_fwd(xl, mods[l], proj, ys, wb, plan.gathered[("out", l)], ln_g, ln_b, l, plan.job("merge_fwd", l))
        plan.done("merge_fwd", l, got)
        saved.append((xl, proj, h, o_a, tot, o_b, states, ys, merged, y, wb))
        xl = x_next
    dx, loss_blk = _loss_head(xl, target)

    small = [None] * DEPTH
    for l in reversed(range(DEPTH)):
        xin, proj, h, o_a, tot, o_b, states, ys, merged, y, wb = saved[l]
        dx_res, dproj, dy_a, dy_b, dy_c, dln_g, dln_b, dgate, g_br, g_out = _merge_bwd(
            dx, xin, y, merged, mods[l], proj, ys, wb, plan.gathered[("out", l)], ln_g, l)
        plan.took(("out", l), g_out)
        plan.took(("br", l), g_br)
        d_oa, d_ob, dnorm_w, dconv_w, dproj = _branch_bwd(proj, o_a, o_b, hgrn_norm_w, conv_all, dy_a, dy_b, dy_c, dproj, l)
        (dproj,), got = _attn_bwd(proj, d_oa, tot, after, before, dproj, plan.job("attn_bwd", l))
        plan.done("attn_bwd", l, got)
        (dlb, dproj), got = _hgrn_bwd(proj, lbs[l], states, d_ob, hg_consts, dproj, plan.job("hgrn_bwd", l))
        plan.done("hgrn_bwd", l, got)
        plan.took(("in", l), _grad_w_in(h, dproj))
        wgs = plan.w_in(l)
        tile = {"tk": wgs[0].shape[-1]} if len(wgs) > 1 else {}
        (dx, dshift, dscale), got = _proj_bwd(dproj, wgs, xin, mods[l], dx_res, plan.job("proj_bwd", l), **tile)
        plan.done("proj_bwd", l, got)
        small[l] = (dshift, dscale, dgate, dnorm_w, dlb, dln_g, dln_b, dconv_w)
    return loss_blk, dx, small
```

```python
import functools
import math

import numpy as np
import jax
import jax.numpy as jnp
from jax import lax
from jax.experimental import pallas as pl
from jax.experimental.pallas import tpu as pltpu

F32 = jnp.float32
BF = jnp.bfloat16
MESH = pl.DeviceIdType.MESH

DEPTH = 2
D_MODEL = 1024
WIDTH = 512
IN_COLS = 12 * WIDTH + 3 * D_MODEL
N_CHIPS = 4
N_DEV = 8
SB_BLOCK = 128
SB_HEAD_DIM = 64
HG_CHUNK = 128
HG_DIM = 128
LN_EPS = 1e-5
RMS_EPS = 1e-6
ALPHA = (2.0 * DEPTH) ** 0.25
ADAM_LR, ADAM_B1, ADAM_B2, ADAM_EPS, ADAM_WD, ADAM_STEP = 0.001, 0.9, 0.999, 1e-08, 0.01, 10
VMEM_LIMIT = 56 << 20


def _params(**kw):
    return pltpu.CompilerParams(vmem_limit_bytes=VMEM_LIMIT, **kw)


def _dot(a, b):
    return jnp.dot(a, b, preferred_element_type=F32)


def _dot_nt(a, b):
    return lax.dot_general(a, b, (((1,), (1,)), ((), ())), preferred_element_type=F32)


def _dot_tn(a, b):
    return lax.dot_general(a, b, (((0,), (0,)), ((), ())), preferred_element_type=F32)


def _sigmoid(x):
    return 1.0 / (1.0 + jnp.exp(-x))


def _silu(x):
    return x * _sigmoid(x)


def _standardize(x):
    mu = jnp.mean(x, axis=-1, keepdims=True)
    xc = x - mu
    var = jnp.mean(xc * xc, axis=-1, keepdims=True)
    rstd = lax.rsqrt(var + LN_EPS)
    return xc * rstd, rstd


def _standardize_bwd(dxs, xs, rstd):
    return rstd * (dxs - jnp.mean(dxs, axis=-1, keepdims=True) - xs * jnp.mean(dxs * xs, axis=-1, keepdims=True))


class _Job:
    def __init__(self, ins, outs, sems, make, alias=None):
        self.ins, self.outs, self.sems, self.make = list(ins), list(outs), list(sems), make
        self.alias = dict(alias or {})


def _join_jobs(jobs):
    jobs = [j for j in jobs if j is not None]
    if len(jobs) <= 1:
        return jobs[0] if jobs else None

    def make(ins, outs, sems):
        phases, i, o, s = [], 0, 0, 0
        for j in jobs:
            got = j.make(ins[i:i + len(j.ins)], outs[o:o + len(j.outs)], sems[s:s + len(j.sems)])
            i, o, s = i + len(j.ins), o + len(j.outs), s + len(j.sems)
            for n, phase in enumerate(got):
                if n == len(phases):
                    phases.append([])
                phases[n] += phase
        return phases

    alias, i, o = {}, 0, 0
    for j in jobs:
        alias.update({i + a: o + b for a, b in j.alias.items()})
        i, o = i + len(j.ins), o + len(j.outs)
    return _Job(sum([j.ins for j in jobs], []), sum([j.outs for j in jobs], []), sum([j.sems for j in jobs], []), make, alias)


def _flip(v, bit):
    return 1 - v if bit else v


def _halves(a, front=0):
    shape = a.shape
    lead = math.prod(shape[front:-2])
    return a.reshape(shape[:front] + (lead, 2, shape[-2] // 2, shape[-1]))


def _dma_sems(*shapes):
    return [pltpu.SemaphoreType.DMA(s) for s in shapes]


def _same(arrays):
    return [jax.ShapeDtypeStruct(a.shape, a.dtype) for a in arrays]


def _gather_job(slabs, window=None, handed=(None,)):
    n = len(slabs)
    cols = lambda w: slice(None) if w is None else pl.ds(*w)

    def make(ins, outs, sems):
        send1, recv1, send2, recv2 = sems
        mx, my, mc = lax.axis_index("x"), lax.axis_index("y"), lax.axis_index("c")
        fetch, pass_on = [], []
        for a in range(n):
            for k in range(1, N_CHIPS):
                px, py = _flip(mx, k & 2), _flip(my, k & 1)
                if window != ():
                    ours = outs[a].at[2 * mx + my, :, mc, :, cols(window)]
                    fetch.append(pltpu.make_async_remote_copy(
                        src_ref=ours, dst_ref=ours, send_sem=send1.at[a, k - 1], recv_sem=recv1.at[a, k - 1],
                        device_id=(px, py, mc), device_id_type=MESH))
                for h, w in enumerate(handed):
                    theirs = outs[a].at[2 * px + py, :, mc, :, cols(w)]
                    (pass_on if w == window else fetch).append(pltpu.make_async_remote_copy(
                        src_ref=theirs, dst_ref=theirs, send_sem=send2.at[a, k - 1, h], recv_sem=recv2.at[a, k - 1, h],
                        device_id=(mx, my, 1 - mc), device_id_type=MESH))
        return [fetch, pass_on] if pass_on else [fetch]

    pairs = (n, N_CHIPS - 1)
    each = pairs + (max(len(handed), 1),)
    return _Job(slabs, _same(slabs), _dma_sems(pairs, pairs, each, each), make, {a: a for a in range(n)})


def _to_sibling_job(grads):
    n = len(grads)

    def make(ins, outs, sems):
        send_sems, recv_sems = sems
        mx, my, mc = lax.axis_index("x"), lax.axis_index("y"), lax.axis_index("c")
        return [[pltpu.make_async_remote_copy(
            src_ref=ins[a].at[:, :, 1 - mc], dst_ref=outs[a], send_sem=send_sems.at[a], recv_sem=recv_sems.at[a],
            device_id=(mx, my, 1 - mc), device_id_type=MESH) for a in range(n)]]

    outs = [jax.ShapeDtypeStruct(g.shape[:2] + g.shape[3:], g.dtype) for g in grads]
    return _Job(grads, outs, _dma_sems((n,), (n,)), make)


def _scatter_job(sums):
    n = len(sums)

    def make(ins, outs, sems):
        send_sems, recv_sems = sems
        mx, my, mc = lax.axis_index("x"), lax.axis_index("y"), lax.axis_index("c")
        copies = []
        for a in range(n):
            for k in range(1, N_CHIPS):
                px, py = _flip(mx, k & 2), _flip(my, k & 1)
                copies.append(pltpu.make_async_remote_copy(
                    src_ref=ins[a].at[2 * px + py], dst_ref=outs[a].at[k - 1], send_sem=send_sems.at[a, k - 1],
                    recv_sem=recv_sems.at[a, k - 1], device_id=(px, py, mc), device_id_type=MESH))
        return [copies]

    pairs = (n, N_CHIPS - 1)
    return _Job(sums, [jax.ShapeDtypeStruct((N_CHIPS - 1,) + s.shape[1:], s.dtype) for s in sums], _dma_sems(pairs, pairs), make)


def _place_job(wholes):
    n = len(wholes)

    def make(ins, outs, sems):
        send_sems, recv_sems = sems
        mx, my, mc = lax.axis_index("x"), lax.axis_index("y"), lax.axis_index("c")
        copies = []
        for a in range(n):
            here = outs[a].at[:, mc]
            copies.append(pltpu.make_async_remote_copy(src_ref=here, dst_ref=here, send_sem=send_sems.at[a], recv_sem=recv_sems.at[a],
                                                       device_id=(mx, my, 1 - mc), device_id_type=MESH))
        return [copies]

    return _Job(wholes, _same(wholes), _dma_sems((n,), (n,)), make, {a: a for a in range(n)})


def _gather8_job(x):
    def make(ins, outs, sems):
        local_sem, send_sems, recv_sems = sems
        mx, my, mc = lax.axis_index("x"), lax.axis_index("y"), lax.axis_index("c")
        here = outs[0].at[4 * mx + 2 * my + mc]
        copies = [pltpu.make_async_copy(ins[0], here, local_sem.at[0])]
        for k in range(1, N_DEV):
            peer = (_flip(mx, k & 4), _flip(my, k & 2), _flip(mc, k & 1))
            copies.append(pltpu.make_async_remote_copy(src_ref=ins[0], dst_ref=here, send_sem=send_sems.at[k - 1],
                                                       recv_sem=recv_sems.at[k - 1], device_id=peer, device_id_type=MESH))
        return [copies]

    return _Job([x], [jax.ShapeDtypeStruct((N_DEV,) + x.shape, x.dtype)], _dma_sems((1,), (N_DEV - 1,), (N_DEV - 1,)), make)


def _run_phases(phases, first=0):
    for n, phase in enumerate(phases):
        if n >= first:
            for cp in phase:
                cp.start()
        for cp in phase:
            cp.wait()


def _run_job(job, name):
    k_in, k_out = len(job.ins), len(job.outs)

    def body(*refs):
        _run_phases(job.make(refs[:k_in], refs[k_in:k_in + k_out], refs[k_in + k_out:]))

    hbm = pl.BlockSpec(memory_space=pl.ANY)
    return pl.pallas_call(body, name=name, in_specs=[hbm] * k_in, out_specs=[hbm] * k_out, out_shape=job.outs,
                          scratch_shapes=job.sems, input_output_aliases=job.alias, compiler_params=_params())(*job.ins)


def _hosted(body, job, args, *, name, grid, in_specs, out_specs, out_shape, scratch_shapes=(), semantics, aliases=None):
    in_specs, out_specs, out_shape, scratch = list(in_specs), list(out_specs), list(out_shape), list(scratch_shapes)
    aliases = dict(aliases or {})
    if job is None:
        outs = pl.pallas_call(body, name=name, grid=grid, in_specs=in_specs, out_specs=out_specs, out_shape=out_shape,
                              scratch_shapes=scratch, input_output_aliases=aliases,
                              compiler_params=_params(dimension_semantics=semantics))(*args)
        return list(outs), None
    n_in, n_out, n_scr, k_in, k_out = len(in_specs), len(out_specs), len(scratch), len(job.ins), len(job.outs)

    def wrapped(*refs):
        ins, rest = refs[:n_in], refs[n_in:]
        job_ins, rest = rest[:k_in], rest[k_in:]
        outs, rest = rest[:n_out], rest[n_out:]
        job_outs, rest = rest[:k_out], rest[k_out:]
        scr, sems = rest[:n_scr], rest[n_scr:]
        ids = [pl.program_id(a) for a in range(len(grid))]
        first = functools.reduce(jnp.logical_and, [i == 0 for i in ids])
        last = functools.reduce(jnp.logical_and, [i == g - 1 for i, g in zip(ids, grid)])

        @pl.when(first)
        def _():
            for cp in job.make(job_ins, job_outs, sems)[0]:
                cp.start()

        body(*ins, *outs, *scr)

        @pl.when(last)
        def _():
            _run_phases(job.make(job_ins, job_outs, sems), first=1)

    hbm = pl.BlockSpec(memory_space=pl.ANY)
    outs = pl.pallas_call(
        wrapped, name=name, grid=grid, in_specs=in_specs + [hbm] * k_in, out_specs=out_specs + [hbm] * k_out,
        out_shape=out_shape + job.outs, scratch_shapes=scratch + job.sems,
        input_output_aliases={**aliases, **{n_in + i: n_out + o for i, o in job.alias.items()}},
        compiler_params=_params(dimension_semantics=("arbitrary",) * len(grid)))(*args, *job.ins)
    return list(outs[:n_out]), list(outs[n_out:])


def _proj_fwd(x, mod, wg, job=None, chunk=(0, 1), tm=512):
    S, D = x.shape
    tm = min(tm, S)
    tn = wg.shape[-1]
    c, n = chunk

    rows = S // tm

    def body(x_ref, mod_ref, w_ref, proj_ref, h_ref, hs):
        mine = pl.ds(pl.multiple_of(pl.program_id(1) * tm, tm), tm)

        @pl.when(pl.program_id(0) == 0)
        def _():
            xs, _ = _standardize(x_ref[...])
            h = xs * (1.0 + mod_ref[:, D:2 * D]) + mod_ref[:, 0:D]
            hb = h.astype(BF)
            hs[mine, :] = hb
            h_ref[...] = hb

        proj_ref[...] = _dot(hs[mine, :], w_ref[...])

    once = lambda j, i: jnp.where(j == 0, i, rows - 1)
    return _hosted(
        body, job, (x, mod, wg), name="proj_fwd",
        grid=(N_CHIPS, rows),
        in_specs=[pl.BlockSpec((tm, D), lambda j, i: (once(j, i), 0)),
                  pl.BlockSpec((1, 3 * D), lambda j, i: (0, 0)),
                  pl.BlockSpec((None, D, tn), lambda j, i: (j, 0, 0))],
        out_specs=[pl.BlockSpec((tm, tn), lambda j, i: (i, j * n + c)),
                   pl.BlockSpec((tm, D), lambda j, i: (once(j, i), 0))],
        out_shape=[jax.ShapeDtypeStruct((S, IN_COLS), F32), jax.ShapeDtypeStruct((S, D), BF)],
        scratch_shapes=[pltpu.VMEM((S, D), BF)],
        semantics=("arbitrary", "arbitrary"))


def _proj_cols(h, wg, chunk, proj, job=None, tm=512):
    S, D = h.shape
    tm = min(tm, S)
    tn = wg.shape[-1]
    c, n = chunk

    def body(h_ref, w_ref, prev_ref, proj_ref):
        mine = pl.ds(pl.multiple_of(pl.program_id(1) * tm, tm), tm)
        proj_ref[...] = _dot(h_ref[mine, :], w_ref[...])

    in_specs = [pl.BlockSpec((S, D), lambda j, i: (0, 0)), pl.BlockSpec((None, D, tn), lambda j, i: (j, 0, 0)),
                pl.BlockSpec(memory_space=pl.ANY)]
    return _hosted(body, job, (h, wg, proj), name="proj_cols", grid=(N_CHIPS, S // tm), in_specs=in_specs,
                   out_specs=[pl.BlockSpec((tm, tn), lambda j, i: (i, j * n + c))],
                   out_shape=[jax.ShapeDtypeStruct(proj.shape, proj.dtype)],
                   semantics=("arbitrary", "arbitrary"), aliases={2: 0})


SB_ROWS = 256
SB_KEYS = 256


def _attn_consts():
    j = np.arange(SB_KEYS)[:, None]
    s = np.arange(SB_KEYS)[None, :]
    from_here = np.concatenate([(j >= s), (j >= s)], axis=0).astype(np.float32)
    return jnp.asarray(from_here, BF), jnp.asarray((j <= s).astype(np.float32), BF)


def _hi_lo(x):
    hi = lax.bitcast_convert_type(lax.bitcast_convert_type(x, jnp.uint32) & jnp.uint32(0xFFFF0000), F32)
    return hi.astype(BF), (x - hi).astype(BF)


def _sums_r(x, t2):
    hi, lo = _hi_lo(x)
    return _dot(jnp.concatenate([hi, lo], axis=1), t2)


def _all_lanes(col, lanes):
    return jnp.broadcast_to(col, (col.shape[0], lanes))


def _attn_rows(ref, r0, rows, lanes, head0, scale=None):
    v = ref[pl.ds(r0, rows), lanes]
    if scale is not None:
        v = v * scale
    return jnp.concatenate([jnp.where(head0, v, 0.0), jnp.where(head0, 0.0, v)], axis=0).astype(BF)


SB_PAIRS_FWD = 4
SB_PAIRS_BWD = 2


def _attn_specs(S, n_pairs):
    return lambda base: pl.BlockSpec((S, n_pairs * SB_BLOCK), lambda p, base=base: (0, base // n_pairs + p))


def _attn_scores(q2n, k_ref, lanes, kj, t2, from_here_ref, masked):
    c0 = pl.multiple_of(kj * SB_KEYS, SB_KEYS)
    kb = k_ref[pl.ds(c0, SB_KEYS), lanes].astype(BF)
    zn = _dot_nt(q2n, kb)
    lsb = jnp.minimum(zn, 0.0) - jnp.log(1.0 + jnp.exp(-jnp.abs(zn)))
    valid = None
    if masked:
        valid = (lax.broadcasted_iota(jnp.int32, zn.shape, 1) + kj * SB_KEYS) < t2
        lsb = jnp.where(valid, lsb, 0.0)
    return c0, kb, zn, valid, lsb, _sums_r(lsb, from_here_ref[...])


def _attn_fwd(proj, from_here, job=None):
    S = proj.shape[0]
    TQ = SB_ROWS
    assert S % TQ == 0 and SB_KEYS == TQ
    scale = SB_HEAD_DIM ** -0.5
    n_pairs = SB_PAIRS_FWD
    pairs = range(n_pairs)
    lanes = [pl.ds(p * SB_BLOCK, SB_BLOCK) for p in pairs]

    def body(q_ref, k_ref, v_ref, from_here_ref, o_ref, tot_ref, run, acc):
        head0 = lax.broadcasted_iota(jnp.int32, (1, 2 * SB_HEAD_DIM), 1) < SB_HEAD_DIM

        def qloop(qi, _):
            r0 = pl.multiple_of(qi * TQ, TQ)
            q2n = [_attn_rows(q_ref, r0, TQ, lanes[p], head0, -scale) for p in pairs]
            trow = lax.broadcasted_iota(jnp.int32, (TQ, SB_KEYS), 0) + qi * TQ
            t2 = jnp.concatenate([trow, trow], axis=0)
            run[...] = jnp.zeros_like(run)
            acc[...] = jnp.zeros_like(acc)

            def step(kj, masked):
                got = [_attn_scores(q2n[p], k_ref, lanes[p], kj, t2, from_here_ref, masked) for p in pairs]
                for p in pairs:
                    c0, _, zn, valid, _, sums = got[p]
                    r = run[p]
                    e = sums - zn + jnp.concatenate([r, r], axis=1)
                    if masked:
                        e = jnp.where(valid, e, -jnp.inf)
                    acc[p] += _dot(jnp.exp(e).astype(BF), v_ref[pl.ds(c0, SB_KEYS), lanes[p]].astype(BF))
                    run[p] = r + _all_lanes(sums[:, 0:1], SB_BLOCK)

            @pl.when(qi % 2 == 0)
            def _():
                step(qi, True)

            @pl.when(qi % 2 == 1)
            def _():
                step(qi, True)
                step(qi - 1, False)

            first = qi - 1 - qi % 2

            def below(n, _):
                step(first - 2 * n, False)
                step(first - 1 - 2 * n, False)
                return 0

            lax.fori_loop(0, qi // 2, below, 0)
            for p in pairs:
                o_ref[pl.ds(r0, TQ), lanes[p]] = jnp.where(head0, acc[p, 0:TQ, :], acc[p, TQ:2 * TQ, :])
                tot_ref[p, 0, pl.ds(r0, TQ), :] = run[p, 0:TQ, :]
                tot_ref[p, 1, pl.ds(r0, TQ), :] = run[p, TQ:2 * TQ, :]
            return 0

        lax.fori_loop(0, S // TQ, qloop, 0)

    col = _attn_specs(S, n_pairs)
    state = pltpu.VMEM((n_pairs, 2 * TQ, SB_BLOCK), F32)
    return _hosted(
        body, job, (proj, proj, proj, from_here), name="attn_fwd",
        grid=(WIDTH // (n_pairs * SB_BLOCK),),
        in_specs=[col(0), col(4), col(8), pl.BlockSpec(from_here.shape, lambda p: (0, 0))],
        out_specs=[col(0), pl.BlockSpec((n_pairs, 2, S, 128), lambda p: (p, 0, 0, 0))],
        out_shape=[jax.ShapeDtypeStruct((S, WIDTH), F32), jax.ShapeDtypeStruct((WIDTH // 128, 2, S, 128), F32)],
        scratch_shapes=[state, state],
        semantics=("parallel",))


def _attn_bwd(proj, d_o, tot, from_here, up_to, dproj, job=None):
    S = proj.shape[0]
    TQ = SB_ROWS
    assert S % TQ == 0 and SB_KEYS == TQ
    scale = SB_HEAD_DIM ** -0.5
    n_pairs = SB_PAIRS_BWD
    pairs = range(n_pairs)
    lanes = [pl.ds(p * SB_BLOCK, SB_BLOCK) for p in pairs]

    def body(q_ref, k_ref, v_ref, do_ref, tot_ref, from_here_ref, up_to_ref, _, dproj_ref, pre, cum, dq_acc, dk_acc, dv_acc, stage, sems):
        dq_ref, dk_ref, dv_ref = stage.at[0], stage.at[1], stage.at[2]
        head0 = lax.broadcasted_iota(jnp.int32, (1, 2 * SB_HEAD_DIM), 1) < SB_HEAD_DIM
        dk_acc[...] = jnp.zeros_like(dk_acc)
        dv_acc[...] = jnp.zeros_like(dv_acc)

        def qloop(qi, _):
            r0 = pl.multiple_of(qi * TQ, TQ)
            q2n = [_attn_rows(q_ref, r0, TQ, lanes[p], head0, -scale) for p in pairs]
            do2 = [_attn_rows(do_ref, r0, TQ, lanes[p], head0) for p in pairs]
            trow = lax.broadcasted_iota(jnp.int32, (TQ, SB_KEYS), 0) + qi * TQ
            t2 = jnp.concatenate([trow, trow], axis=0)
            for p in pairs:
                pre[p, 0:TQ, :] = tot_ref[p, 0, pl.ds(r0, TQ), :]
                pre[p, TQ:2 * TQ, :] = tot_ref[p, 1, pl.ds(r0, TQ), :]
            cum[...] = jnp.zeros_like(cum)
            dq_acc[...] = jnp.zeros_like(dq_acc)

            def step(kj, masked):
                got = [_attn_scores(q2n[p], k_ref, lanes[p], kj, t2, from_here_ref, masked) for p in pairs]
                das = [_dot_nt(do2[p], v_ref[pl.ds(got[p][0], SB_KEYS), lanes[p]].astype(BF)) for p in pairs]
                for p in pairs:
                    c0, kb, zn, valid, lsb, sums = got[p]
                    later = pre[p] - _all_lanes(sums[:, 0:1], SB_BLOCK)
                    pre[p] = later
                    e = sums - zn + jnp.concatenate([later, later], axis=1)
                    sig = jnp.exp(lsb - zn)
                    if masked:
                        e = jnp.where(valid, e, -jnp.inf)
                        sig = jnp.where(valid, sig, 0.0)
                    a = jnp.exp(e)
                    w = das[p] * a
                    upto = _dot(w.astype(BF), up_to_ref[...])
                    c = cum[p]
                    dz = w - sig * (upto + jnp.concatenate([c, c], axis=1))
                    cum[p] = c + _all_lanes(upto[:, SB_KEYS - 1:SB_KEYS], SB_BLOCK)
                    dzb = dz.astype(BF)
                    dq_acc[p] += _dot(dzb, kb)
                    dk_acc[pl.ds(c0, SB_KEYS), lanes[p]] += _dot_tn(dzb, q2n[p])
                    dv_acc[pl.ds(c0, SB_KEYS), lanes[p]] += _dot_tn(a.astype(BF), do2[p])

            def below(n, _):
                step(2 * n, False)
                step(2 * n + 1, False)
                return 0

            lax.fori_loop(0, qi // 2, below, 0)

            @pl.when(qi % 2 == 1)
            def _():
                step(qi - 1, False)
                step(qi, True)

            @pl.when(qi % 2 == 0)
            def _():
                step(qi, True)
            for p in pairs:
                dq_ref[pl.ds(r0, TQ), lanes[p]] = (jnp.where(head0, dq_acc[p, 0:TQ, :], dq_acc[p, TQ:2 * TQ, :]) * scale).astype(BF)
            return 0

        lax.fori_loop(0, S // TQ, qloop, 0)
        dk_ref[...] = (-dk_acc[...]).astype(BF)
        dv_ref[...] = dv_acc[...].astype(BF)
        lane0 = pl.program_id(0) * (n_pairs * SB_BLOCK)
        _put_columns([dq_ref, dk_ref, dv_ref], dproj_ref, [WIDTH * n + lane0 for n in range(3)], sems)

    col = _attn_specs(S, n_pairs)
    whole = lambda a: pl.BlockSpec(a.shape, lambda p: (0, 0))
    hbm = pl.BlockSpec(memory_space=pl.ANY)
    state = pltpu.VMEM((n_pairs, 2 * TQ, SB_BLOCK), F32)
    grads = pltpu.VMEM((S, n_pairs * SB_BLOCK), F32)
    return _hosted(
        body, job, (proj, proj, proj, d_o, tot, from_here, up_to, dproj), name="attn_bwd",
        grid=(WIDTH // (n_pairs * SB_BLOCK),),
        in_specs=[col(0), col(4), col(8), col(0), pl.BlockSpec((n_pairs, 2, S, 128), lambda p: (p, 0, 0, 0)), whole(from_here),
                  whole(up_to), hbm],
        out_specs=[hbm],
        out_shape=[jax.ShapeDtypeStruct(dproj.shape, dproj.dtype)],
        scratch_shapes=[state, state, state, grads, grads, pltpu.VMEM((3, S, n_pairs * SB_BLOCK), BF), pltpu.SemaphoreType.DMA((3,))],
        semantics=("arbitrary",), aliases={7: 0})


HG_LEVELS = tuple(HG_CHUNK >> n for n in range(1, HG_CHUNK.bit_length()))


def _hgrn_consts():
    C = HG_CHUNK
    t = np.arange(C)[:, None]
    s = np.arange(C)[None, :]
    rows = [(s <= t), (s > t)]
    masks = [(t == s)]
    for m in HG_LEVELS:
        two = 2 * m
        mid = (t // two) * two + m
        right = (t % two) >= m
        rows.append((right & (s >= mid) & (s <= t)) | ((~right) & (s > t) & (s <= mid - 1)))
        masks.append(((t // two) == (s // two)) & right & ((s % two) < m))
    tri = np.concatenate(rows, axis=0).astype(np.float32)
    twice = lambda a: jnp.asarray(np.concatenate([a, a], axis=1), BF)
    return (twice(tri), twice(tri.T), jnp.asarray(np.stack(masks).astype(np.float32), F32))


HG_SUM_BLOCKS = 2 + len(HG_LEVELS)


def _split_rows(g):
    hi = g.astype(BF)
    return jnp.concatenate([hi, (g - hi.astype(F32)).astype(BF)], axis=0)


def _hgrn_sum_blocks(e):
    C = HG_CHUNK
    blocks = tuple(e[n * C:(n + 1) * C] for n in range(HG_SUM_BLOCKS))
    return blocks + (jnp.broadcast_to(e[C - 1:C], (HG_DIM, e.shape[1])),)


@jax.custom_vjp
def _hgrn_sums(tri, tri_t, g):
    return _hgrn_sum_blocks(_dot(tri, _split_rows(g)))


def _hgrn_sums_fwd(tri, tri_t, g):
    return _hgrn_sums(tri, tri_t, g), (tri, tri_t)


def _hgrn_sums_bwd(res, ds):
    tri, tri_t = res
    C = HG_CHUNK
    last = lax.broadcasted_iota(jnp.int32, (C, 1), 0) == C - 1
    prefix = ds[0] + jnp.where(last, jnp.sum(ds[-1], axis=0, keepdims=True), 0.0)
    d = jnp.concatenate((prefix,) + tuple(ds[1:-1]), axis=0)
    return jnp.zeros_like(tri), jnp.zeros_like(tri_t), _dot(tri_t[:, :d.shape[0]], d.astype(BF))


_hgrn_sums.defvjp(_hgrn_sums_fwd, _hgrn_sums_bwd)


def _bf_dot(a, b):
    return _dot(a.astype(BF), b.astype(BF))


def _bf_dot_nt(a, b):
    return _dot_nt(a.astype(BF), b.astype(BF))


def _bf_dot_tn(a, b):
    return _dot_tn(a.astype(BF), b.astype(BF))


@jax.custom_vjp
def _mm(a, b):
    return _bf_dot(a, b)


_mm.defvjp(lambda a, b: (_bf_dot(a, b), (a, b)), lambda r, ct: (_bf_dot_nt(ct, r[1]), _bf_dot_tn(r[0], ct)))


@jax.custom_vjp
def _mm_nt(a, b):
    return _bf_dot_nt(a, b)


_mm_nt.defvjp(lambda a, b: (_bf_dot_nt(a, b), (a, b)), lambda r, ct: (_bf_dot(ct, r[1]), _bf_dot_tn(ct, r[0])))


@jax.custom_vjp
def _mm_tn(a, b):
    return _bf_dot_tn(a, b)


_mm_tn.defvjp(lambda a, b: (_bf_dot_tn(a, b), (a, b)), lambda r, ct: (_bf_dot_nt(r[1], ct), _bf_dot(r[0], ct)))


def _hgrn_gates(tri, tri_t, qraw, fpre, lb):
    q = _silu(qraw)
    f = lb + (1.0 - lb) * _sigmoid(fpre)
    return q, 1.0 - f, _hgrn_sums(tri, tri_t, jnp.log(f))


def _hgrn_mix(masks, q, k, e, v, st):
    prefix, suffix, whole = e[0], e[1], e[-1]
    scores = masks[0] * _mm_nt(q, k)
    for n in range(len(HG_LEVELS)):
        decay = jnp.exp(e[2 + n])
        scores = scores + masks[n + 1] * _mm_nt(q * decay, k * decay)
    o = _mm_nt(q * jnp.exp(prefix), st) + _mm(scores, v)
    st_new = st * jnp.exp(whole) + _mm_tn(v, k * jnp.exp(suffix))
    return o, st_new


def _hgrn_chunk(tri, tri_t, masks, qraw, fpre, v, st, lb):
    q, k, e = _hgrn_gates(tri, tri_t, qraw, fpre, lb)
    return _hgrn_mix(masks, q, k, e, v, st)


HG_HEADS_PER_STEP = 4
HG_LANES = HG_HEADS_PER_STEP * HG_DIM


def _hgrn_specs(S, consts):
    col = lambda base: pl.BlockSpec((S, HG_LANES), lambda p, base=base: (0, base // HG_HEADS_PER_STEP + p))
    whole = [pl.BlockSpec(a.shape, lambda p, n=a.ndim: (0,) * n) for a in consts]
    return col, whole


def _hgrn_fwd(proj, lbs, consts, job=None):
    S = proj.shape[0]
    nc = S // HG_CHUNK
    heads = range(HG_HEADS_PER_STEP)

    def body(q_ref, f_ref, i_ref, lb_ref, tri_ref, trit_ref, mask_ref, o_ref, st_ref):
        tri, tri_t = tri_ref[...], trit_ref[...]
        masks = [mask_ref[n] for n in range(len(HG_LEVELS) + 1)]

        def chunk(ci, sts):
            r0 = pl.multiple_of(ci * HG_CHUNK, HG_CHUNK)
            rows = pl.ds(r0, HG_CHUNK)
            new = []
            lane = [pl.ds(hd * HG_DIM, HG_DIM) for hd in heads]
            gates = [_hgrn_gates(tri, tri_t, q_ref[rows, lane[hd]], f_ref[rows, lane[hd]], lb_ref[hd]) for hd in heads]
            for hd in heads:
                st_ref[hd, ci] = sts[hd]
                o, st_new = _hgrn_mix(masks, *gates[hd], i_ref[rows, lane[hd]], sts[hd])
                o_ref[rows, lane[hd]] = o
                new.append(st_new)
            return tuple(new)

        lax.fori_loop(0, nc, chunk, tuple(jnp.zeros((HG_DIM, HG_DIM), F32) for _ in heads))

    col, whole = _hgrn_specs(S, consts)
    return _hosted(
        body, job, (proj, proj, proj, lbs, *consts), name="hgrn_fwd",
        grid=(WIDTH // HG_LANES,),
        in_specs=[col(16), col(20), col(24), pl.BlockSpec((HG_HEADS_PER_STEP, 1, 128), lambda p: (p, 0, 0))] + whole,
        out_specs=[col(0), pl.BlockSpec((HG_HEADS_PER_STEP, nc, HG_DIM, HG_DIM), lambda p: (p, 0, 0, 0))],
        out_shape=[jax.ShapeDtypeStruct((S, WIDTH), F32), jax.ShapeDtypeStruct((WIDTH // 128, nc, HG_DIM, HG_DIM), F32)],
        semantics=("parallel",))


def _hgrn_bwd(proj, lbs, states, d_o, consts, dproj, job=None):
    S = proj.shape[0]
    nc = S // HG_CHUNK

    def body(q_ref, f_ref, i_ref, lb_ref, st_ref, do_ref, tri_ref, trit_ref, mask_ref, _, dlb_ref, dproj_ref, stage, sems):
        dq_ref, df_ref, di_ref = stage.at[0], stage.at[1], stage.at[2]
        masks = [mask_ref[n] for n in range(len(HG_LEVELS) + 1)]
        fn = functools.partial(_hgrn_chunk, tri_ref[...], trit_ref[...], masks)
        heads = range(HG_HEADS_PER_STEP)

        def chunk(n, carry):
            ci = nc - 1 - n
            r0 = pl.multiple_of(ci * HG_CHUNK, HG_CHUNK)
            rows = pl.ds(r0, HG_CHUNK)
            new = []
            lane = [pl.ds(hd * HG_DIM, HG_DIM) for hd in heads]
            pulls = [jax.vjp(fn, q_ref[rows, lane[hd]], f_ref[rows, lane[hd]], i_ref[rows, lane[hd]], st_ref[hd, ci], lb_ref[hd])[1]
                     for hd in heads]
            for hd in heads:
                d_st, dlb = carry[hd]
                lanes = lane[hd]
                dq, df, di, d_prev, dl = pulls[hd]((do_ref[rows, lanes], d_st))
                dq_ref[rows, lanes] = dq.astype(BF)
                df_ref[rows, lanes] = df.astype(BF)
                di_ref[rows, lanes] = di.astype(BF)
                new.append((d_prev, dlb + dl))
            return tuple(new)

        zero = (jnp.zeros((HG_DIM, HG_DIM), F32), jnp.zeros((1, HG_DIM), F32))
        done = lax.fori_loop(0, nc, chunk, tuple(zero for _ in heads))
        for hd in heads:
            dlb_ref[hd] = done[hd][1]
        lane0 = pl.program_id(0) * HG_LANES
        _put_columns([dq_ref, df_ref, di_ref], dproj_ref, [WIDTH * n + lane0 for n in (4, 5, 6)], sems)

    col, whole = _hgrn_specs(S, consts)
    head = pl.BlockSpec((HG_HEADS_PER_STEP, 1, 128), lambda p: (p, 0, 0))
    hbm = pl.BlockSpec(memory_space=pl.ANY)
    n_in = 6 + len(consts)
    return _hosted(
        body, job, (proj, proj, proj, lbs, states, d_o, *consts, dproj), name="hgrn_bwd",
        grid=(WIDTH // HG_LANES,),
        in_specs=[col(16), col(20), col(24), head, pl.BlockSpec((HG_HEADS_PER_STEP, nc, HG_DIM, HG_DIM), lambda p: (p, 0, 0, 0)), col(0)]
        + whole + [hbm],
        out_specs=[head, hbm],
        out_shape=[jax.ShapeDtypeStruct((WIDTH // 128, 1, 128), F32), jax.ShapeDtypeStruct(dproj.shape, dproj.dtype)],
        scratch_shapes=[pltpu.VMEM((3, S, HG_LANES), BF), pltpu.SemaphoreType.DMA((3,))],
        semantics=("arbitrary",), aliases={n_in: 1})


def _shift_down(x, n):
    rows = lax.broadcasted_iota(jnp.int32, x.shape, 0)
    return jnp.where(rows >= n, pltpu.roll(x, n, 0), 0.0)


def _shift_up(x, n):
    S = x.shape[0]
    rows = lax.broadcasted_iota(jnp.int32, x.shape, 0)
    return jnp.where(rows < S - n, pltpu.roll(x, S - n, 0), 0.0)


def _branch_fwd(proj, o_a, o_b, norm_w, conv_w, layer):
    S = proj.shape[0]

    def body(oa_ref, za_ref, ob_ref, zb_ref, nw_ref, pre_ref, post_ref, u_ref, zc_ref, cw_ref, ya_ref, yb_ref, yc_ref):
        ya_ref[...] = (oa_ref[...] * _silu(za_ref[...])).astype(BF)
        ob = ob_ref[...]
        rn = lax.rsqrt(jnp.mean(ob * ob, axis=-1, keepdims=True) + RMS_EPS)
        yb_ref[...] = (ob * rn * nw_ref[layer:layer + 1, :] * _silu(zb_ref[...])).astype(BF)
        pu = pre_ref[...] * u_ref[...]
        conv = cw_ref[2:3, :] * pu + cw_ref[1:2, :] * _shift_down(pu, 1) + cw_ref[0:1, :] * _shift_down(pu, 2)
        yc_ref[...] = (post_ref[...] * conv * _silu(zc_ref[...])).astype(BF)

    col = lambda base: pl.BlockSpec((S, 128), lambda p, base=base: (0, base + p))
    out = jax.ShapeDtypeStruct((S, WIDTH), BF)
    return pl.pallas_call(
        body, name="branch_fwd",
        grid=(WIDTH // 128,),
        in_specs=[col(0), col(12), col(0), col(28), pl.BlockSpec(norm_w.shape, lambda p: (0, 0)),
                  col(32), col(36), col(40), col(44), pl.BlockSpec((None, None, 3, 128), lambda p: (p, layer, 0, 0))],
        out_specs=[col(0), col(0), col(0)],
        out_shape=[out, out, out],
        compiler_params=_params(dimension_semantics=("parallel",)),
    )(o_a, proj, o_b, proj, norm_w, proj, proj, proj, proj, conv_w)


def _put_columns(tiles, dproj_ref, firsts, sems):
    copies = [pltpu.make_async_copy(t, dproj_ref.at[:, pl.ds(pl.multiple_of(c, 128), t.shape[1])], sems.at[n])
              for n, (t, c) in enumerate(zip(tiles, firsts))]
    for cp in copies:
        cp.start()
    for cp in copies:
        cp.wait()


def _branch_bwd(proj, o_a, o_b, norm_w, conv_w, dy_a, dy_b, dy_c, dproj, layer):
    S = proj.shape[0]
    firsts = [WIDTH * n for n in (3, 7, 8, 9, 10, 11)]

    def dsilu(z):
        s = _sigmoid(z)
        return s * z, s * (1.0 + z * (1.0 - s))

    def body(oa_ref, za_ref, ob_ref, zb_ref, nw_ref, pre_ref, post_ref, u_ref, zc_ref, cw_ref, dya_ref, dyb_ref, dyc_ref, _,
             doa_ref, dob_ref, dnw_ref, dcw_ref, dproj_ref, stage, sems):
        dza_ref, dzb_ref, dpre_ref, dpost_ref, du_ref, dzc_ref = [stage.at[n] for n in range(6)]
        dya = dya_ref[...]
        sa, dsa = dsilu(za_ref[...])
        doa_ref[...] = dya * sa
        dza_ref[...] = (dya * oa_ref[...] * dsa).astype(BF)

        dyb = dyb_ref[...]
        ob = ob_ref[...]
        nw = nw_ref[layer:layer + 1, :]
        sb, dsb = dsilu(zb_ref[...])
        rn = lax.rsqrt(jnp.mean(ob * ob, axis=-1, keepdims=True) + RMS_EPS)
        on = ob * rn
        dzb_ref[...] = (dyb * on * nw * dsb).astype(BF)
        don_w = dyb * sb
        dnw_ref[...] = jnp.sum(don_w * on, axis=0, keepdims=True)
        don = don_w * nw
        dob_ref[...] = rn * (don - on * jnp.mean(don * on, axis=-1, keepdims=True))

        dyc = dyc_ref[...]
        pre, post, u = pre_ref[...], post_ref[...], u_ref[...]
        sc, dsc = dsilu(zc_ref[...])
        pu = pre * u
        pu1, pu2 = _shift_down(pu, 1), _shift_down(pu, 2)
        conv = cw_ref[2:3, :] * pu + cw_ref[1:2, :] * pu1 + cw_ref[0:1, :] * pu2
        dzc_ref[...] = (dyc * post * conv * dsc).astype(BF)
        dpost_ref[...] = (dyc * conv * sc).astype(BF)
        dconv = dyc * post * sc
        dcw_ref[0:1, :] = jnp.sum(dconv * pu2, axis=0, keepdims=True)
        dcw_ref[1:2, :] = jnp.sum(dconv * pu1, axis=0, keepdims=True)
        dcw_ref[2:3, :] = jnp.sum(dconv * pu, axis=0, keepdims=True)
        dpu = cw_ref[2:3, :] * dconv + cw_ref[1:2, :] * _shift_up(dconv, 1) + cw_ref[0:1, :] * _shift_up(dconv, 2)
        dpre_ref[...] = (dpu * u).astype(BF)
        du_ref[...] = (dpu * pre).astype(BF)
        lane0 = pl.program_id(0) * 128
        _put_columns([stage.at[n] for n in range(6)], dproj_ref, [c + lane0 for c in firsts], sems)

    col = lambda base: pl.BlockSpec((S, 128), lambda p, base=base: (0, base + p))
    f32 = jax.ShapeDtypeStruct((S, WIDTH), F32)
    hbm = pl.BlockSpec(memory_space=pl.ANY)
    return pl.pallas_call(
        body, name="branch_bwd",
        grid=(WIDTH // 128,),
        in_specs=[col(0), col(12), col(0), col(28), pl.BlockSpec(norm_w.shape, lambda p: (0, 0)),
                  col(32), col(36), col(40), col(44), pl.BlockSpec((None, None, 3, 128), lambda p: (p, layer, 0, 0)),
                  col(0), col(0), col(0), hbm],
        out_specs=[col(0), col(0), pl.BlockSpec((None, 1, 128), lambda p: (p, 0, 0)), pl.BlockSpec((None, 3, 128), lambda p: (p, 0, 0)), hbm],
        out_shape=[f32, f32, jax.ShapeDtypeStruct((WIDTH // 128, 1, 128), F32), jax.ShapeDtypeStruct((WIDTH // 128, 3, 128), F32),
                   jax.ShapeDtypeStruct(dproj.shape, dproj.dtype)],
        scratch_shapes=[pltpu.VMEM((6, S, 128), BF), pltpu.SemaphoreType.DMA((6,))],
        input_output_aliases={13: 4},
        compiler_params=_params(dimension_semantics=("arbitrary",)),
    )(o_a, proj, o_b, proj, norm_w, proj, proj, proj, proj, conv_w, dy_a, dy_b, dy_c, dproj)


def _branch_proj(y_refs, wb_ref):
    return [_dot(y_refs[i][...], wb_ref[i]) for i in range(3)]


def _merge_fwd(x, mod, proj, ys, wb, wo, ln_g, ln_b, layer, job=None, tm=512):
    S, D = x.shape
    tm = min(tm, S)

    def body(x_ref, mod_ref, ga_ref, gb_ref, gc_ref, ya_ref, yb_ref, yc_ref, wb_ref, wo_ref, g_ref, b_ref, xo_ref, mg_ref, y_ref):
        ps = _branch_proj((ya_ref, yb_ref, yc_ref), wb_ref)
        merged = _sigmoid(ga_ref[...]) * ps[0] + _sigmoid(gb_ref[...]) * ps[1] + _sigmoid(gc_ref[...]) * ps[2]
        mb = merged.astype(BF)
        mg_ref[...] = mb
        y = _dot(mb, wo_ref[...].reshape(D, D))
        y_ref[...] = y
        r = ALPHA * x_ref[...] + (1.0 + mod_ref[:, 2 * D:3 * D]) * y
        xn, _ = _standardize(r)
        xo_ref[...] = xn * g_ref[layer:layer + 1, :] + b_ref[layer:layer + 1, :]

    row = lambda w, c=0: pl.BlockSpec((tm, w), lambda i, c=c: (i, c))
    whole = lambda a: pl.BlockSpec(a.shape, lambda i, n=a.ndim: (0,) * n)
    return _hosted(
        body, job, (x, mod, proj, proj, proj, *ys, wb, wo, ln_g, ln_b), name="merge_fwd",
        grid=(S // tm,),
        in_specs=[row(D), whole(mod), row(D, 6), row(D, 7), row(D, 8), row(WIDTH), row(WIDTH), row(WIDTH), whole(wb),
                  whole(wo), whole(ln_g), whole(ln_b)],
        out_specs=[row(D), row(D), row(D)],
        out_shape=[jax.ShapeDtypeStruct((S, D), F32), jax.ShapeDtypeStruct((S, D), BF), jax.ShapeDtypeStruct((S, D), F32)],
        semantics=("parallel",))


def _merge_bwd(dxo, x, y, merged, mod, proj, ys, wb, wo, ln_g, layer, tm=256):
    S, D = x.shape
    tm = min(tm, S)
    steps = S // tm
    quarter = D // N_CHIPS

    def body(dxo_ref, x_ref, y_ref, mg_ref, mod_ref, ga_ref, gb_ref, gc_ref, ya_ref, yb_ref, yc_ref, wb_ref, wo_ref, g_ref,
             dxr_ref, dg_ref, dya_ref, dyb_ref, dyc_ref, dlg_ref, dlb_ref, dgt_ref, gbr_ref, gout_ref, acc_br, acc_out):
        @pl.when(pl.program_id(0) == 0)
        def _():
            dlg_ref[...] = jnp.zeros_like(dlg_ref)
            dlb_ref[...] = jnp.zeros_like(dlb_ref)
            dgt_ref[...] = jnp.zeros_like(dgt_ref)
            acc_br[...] = jnp.zeros_like(acc_br)
            acc_out[...] = jnp.zeros_like(acc_out)

        gate1 = 1.0 + mod_ref[:, 2 * D:3 * D]
        yv = y_ref[...]
        xn, rstd = _standardize(ALPHA * x_ref[...] + gate1 * yv)
        dxo = dxo_ref[...]
        dlg_ref[...] += jnp.sum(dxo * xn, axis=0, keepdims=True)
        dlb_ref[...] += jnp.sum(dxo, axis=0, keepdims=True)
        dr = _standardize_bwd(dxo * g_ref[layer:layer + 1, :], xn, rstd)
        dxr_ref[...] = ALPHA * dr
        dgt_ref[...] += jnp.sum(dr * yv, axis=0, keepdims=True)
        dyb = (gate1 * dr).astype(BF)
        acc_out[...] += _dot_tn(mg_ref[...], dyb)
        dmerged = _dot_nt(dyb, wo_ref[...].reshape(D, D))
        y_refs = (ya_ref, yb_ref, yc_ref)
        ps = _branch_proj(y_refs, wb_ref)
        for i, (gate_ref, out_ref) in enumerate(((ga_ref, dya_ref), (gb_ref, dyb_ref), (gc_ref, dyc_ref))):
            sg = _sigmoid(gate_ref[...])
            dg_ref[:, i * D:(i + 1) * D] = (dmerged * ps[i] * sg * (1.0 - sg)).astype(BF)
            dp = (dmerged * sg).astype(BF)
            acc_br[i] += _dot_tn(y_refs[i][...], dp)
            out_ref[...] = _dot_nt(dp, wb_ref[i])

        @pl.when(pl.program_id(0) == steps - 1)
        def _():
            for j in range(N_CHIPS):
                gout_ref[j] = acc_out[j * quarter:(j + 1) * quarter, :].astype(BF)
                for i in range(3):
                    gbr_ref[j, i] = acc_br[i, :, j * quarter:(j + 1) * quarter].astype(BF)

    row = lambda w, c=0: pl.BlockSpec((tm, w), lambda i, c=c: (i, c))
    whole = lambda a: pl.BlockSpec(a.shape, lambda i, n=len(a.shape): (0,) * n)
    vec = pl.BlockSpec((1, D), lambda i: (0, 0))
    sd = jax.ShapeDtypeStruct
    g_br, g_out = sd((N_CHIPS, 3, WIDTH, quarter), BF), sd((N_CHIPS, quarter, D), BF)
    return pl.pallas_call(
        body, name="merge_bwd",
        grid=(steps,),
        in_specs=[row(D), row(D), row(D), row(D), whole(mod), row(D, 6), row(D, 7), row(D, 8), row(WIDTH), row(WIDTH), row(WIDTH),
                  whole(wb), whole(wo), whole(ln_g)],
        out_specs=[row(D), row(3 * D, IN_COLS // (3 * D) - 1), row(WIDTH), row(WIDTH), row(WIDTH), vec, vec, vec, whole(g_br), whole(g_out)],
        out_shape=[sd((S, D), F32), sd((S, IN_COLS), BF), sd((S, WIDTH), F32), sd((S, WIDTH), F32), sd((S, WIDTH), F32),
                   sd((1, D), F32), sd((1, D), F32), sd((1, D), F32), g_br, g_out],
        scratch_shapes=[pltpu.VMEM((3, WIDTH, D), F32), pltpu.VMEM((D, D), F32)],
        compiler_params=_params(dimension_semantics=("arbitrary",)),
    )(dxo, x, y, merged, mod, proj, proj, proj, *ys, wb, wo, ln_g)


def _loss_head(x, target, tm=512):
    S, D = x.shape
    tm = min(tm, S)

    def body(x_ref, t_ref, dx_ref, loss_ref):
        @pl.when(pl.program_id(0) == 0)
        def _():
            loss_ref[...] = jnp.zeros_like(loss_ref)

        err = x_ref[...] - t_ref[...]
        dx_ref[...] = err * (1.0 / D)
        loss_ref[...] += 0.5 * jnp.sum(jnp.mean(err * err, axis=-1, keepdims=True))

    row = pl.BlockSpec((tm, D), lambda i: (i, 0))
    return pl.pallas_call(
        body, name="loss_head",
        grid=(S // tm,),
        in_specs=[row, row],
        out_specs=[row, pl.BlockSpec((8, 128), lambda i: (0, 0))],
        out_shape=[jax.ShapeDtypeStruct((S, D), F32), jax.ShapeDtypeStruct((8, 128), F32)],
        compiler_params=_params(dimension_semantics=("arbitrary",)),
    )(x, target)


def _proj_bwd(dproj, wgs, x, mod, dx_res, job=None, tm=512, tk=768):
    S, D = x.shape
    tm = min(tm, S)
    n = len(wgs)
    w_args = list(wgs)
    if n == 1:
        per = wgs[0].shape[-1] // tk
        w_specs = [pl.BlockSpec((None, D, tk), lambda k, i: (k // per, 0, k % per))]
    else:
        assert wgs[0].shape[-1] == tk
        w_specs = [pl.BlockSpec((None, D, tk), lambda k, i, c=c: (jnp.minimum((k + n - 1 - c) // n, N_CHIPS - 1), 0, 0))
                   for c in range(n)]
    nk = IN_COLS // tk

    def body(dp_ref, *rest):
        w_refs, (x_ref, mod_ref, dxr_ref, dx_ref, dsh_ref, dsc_ref, acc) = rest[:n], rest[n:]
        k, i = pl.program_id(0), pl.program_id(1)
        mine = pl.ds(pl.multiple_of(i * tm, tm), tm)

        @pl.when((i == 0) & (k == 0))
        def _():
            dsh_ref[...] = jnp.zeros_like(dsh_ref)
            dsc_ref[...] = jnp.zeros_like(dsc_ref)

        @pl.when(k == 0)
        def _():
            acc[mine, :] = jnp.zeros((tm, D), F32)

        for c in range(n):
            @pl.when(k % n == c)
            def _(c=c):
                acc[mine, :] += _dot_nt(dp_ref[...], w_refs[c][...])

        @pl.when(k == nk - 1)
        def _():
            dh = acc[mine, :]
            xs, rstd = _standardize(x_ref[...])
            dsh_ref[...] += jnp.sum(dh, axis=0, keepdims=True)
            dsc_ref[...] += jnp.sum(dh * xs, axis=0, keepdims=True)
            dx_ref[...] = _standardize_bwd(dh * (1.0 + mod_ref[:, D:2 * D]), xs, rstd) + dxr_ref[...]

    row = pl.BlockSpec((tm, D), lambda k, i: (jnp.where(k == nk - 1, i, 0), 0))
    vec = pl.BlockSpec((1, D), lambda k, i: (0, 0))
    return _hosted(
        body, job, (dproj, *w_args, x, mod, dx_res), name="proj_bwd",
        grid=(nk, S // tm),
        in_specs=[pl.BlockSpec((tm, tk), lambda k, i: (i, k))] + w_specs + [row, pl.BlockSpec((1, 3 * D), lambda k, i: (0, 0)), row],
        out_specs=[row, vec, vec],
        out_shape=[jax.ShapeDtypeStruct((S, D), F32), jax.ShapeDtypeStruct((1, D), F32), jax.ShapeDtypeStruct((1, D), F32)],
        scratch_shapes=[pltpu.VMEM((S, D), F32)],
        semantics=("arbitrary", "arbitrary"))


def _grad_w_in(h, dproj):
    S, D = h.shape
    shard = IN_COLS // N_CHIPS

    def body(h_ref, d_ref, o_ref):
        o_ref[...] = _dot_tn(h_ref[...], d_ref[...]).astype(BF)

    return pl.pallas_call(
        body, name="grad_w_in",
        grid=(N_CHIPS,),
        in_specs=[pl.BlockSpec((S, D), lambda n: (0, 0)), pl.BlockSpec((S, shard), lambda n: (0, n))],
        out_specs=pl.BlockSpec((None, D, shard), lambda n: (n, 0, 0)),
        out_shape=jax.ShapeDtypeStruct((N_CHIPS, D, shard), BF),
        compiler_params=_params(dimension_semantics=("parallel",)),
    )(h, dproj)


def _all_gather8(x, name):
    R, N = x.shape

    def body(x_ref, out_ref, send_sems, recv_sems):
        mx, my, mc = lax.axis_index("x"), lax.axis_index("y"), lax.axis_index("c")
        me = 4 * mx + 2 * my + mc
        out_ref[me] = x_ref[...]
        copies = []
        for k in range(1, N_DEV):
            peer = (_flip(mx, k & 4), _flip(my, k & 2), _flip(mc, k & 1))
            cp = pltpu.make_async_remote_copy(src_ref=x_ref, dst_ref=out_ref.at[me], send_sem=send_sems.at[k - 1],
                                              recv_sem=recv_sems.at[k - 1], device_id=peer, device_id_type=MESH)
            cp.start()
            copies.append(cp)
        for cp in copies:
            cp.wait()

    return pl.pallas_call(
        body, name=name,
        in_specs=[pl.BlockSpec(memory_space=pltpu.VMEM)],
        out_specs=pl.BlockSpec(memory_space=pltpu.VMEM),
        out_shape=jax.ShapeDtypeStruct((N_DEV, R, N), F32),
        scratch_shapes=[pltpu.SemaphoreType.DMA((N_DEV - 1,)), pltpu.SemaphoreType.DMA((N_DEV - 1,))],
        compiler_params=_params(),
    )(x)


def _rows2d(a):
    return a.reshape(-1, a.shape[-1])


def _tile_rows(rows, cols, n_arrays):
    budget = (24 << 20) // (n_arrays * 2 * 4 * cols)
    if rows <= budget:
        return rows
    tm = 8
    for cand in range(8, budget + 1, 8):
        if rows % cand == 0:
            tm = cand
    return tm


SUM_ROWS = 512


def _sum_cores(g, sent, where):
    chips, lead, _, r, cols = g.shape
    tr = min(r, SUM_ROWS)

    def body(where_ref, g_ref, s_ref, out_ref):
        out_ref[...] = (g_ref[...].astype(F32) + s_ref[...].astype(F32)).astype(BF)

    spec = pl.BlockSpec((None, tr, cols), lambda i, j, where_ref: (i, j, 0))
    out = pl.pallas_call(
        body, name="sum_cores",
        grid_spec=pltpu.PrefetchScalarGridSpec(
            num_scalar_prefetch=1, grid=(chips * lead, r // tr),
            in_specs=[pl.BlockSpec((None, None, tr, cols), lambda i, j, where_ref: (i, where_ref[1], j, 0)), spec],
            out_specs=spec),
        out_shape=jax.ShapeDtypeStruct((chips * lead, r, cols), BF),
        compiler_params=_params(dimension_semantics=("parallel", "parallel")),
    )(where, g.reshape(chips * lead, 2, r, cols), sent.reshape(chips * lead, r, cols))
    return out.reshape(chips, lead, r, cols)


def _sum_chips(sums, got, where):
    _, lead, r, cols = sums.shape
    tr = min(r, SUM_ROWS)

    def body(where_ref, s_ref, g_ref, out_ref):
        out_ref[...] = ((s_ref[...].astype(F32) + g_ref[0].astype(F32)) + g_ref[1].astype(F32)) + g_ref[2].astype(F32)

    return pl.pallas_call(
        body, name="sum_chips",
        grid_spec=pltpu.PrefetchScalarGridSpec(
            num_scalar_prefetch=1, grid=(lead, r // tr),
            in_specs=[pl.BlockSpec((None, None, tr, cols), lambda i, j, where_ref: (where_ref[0], i, j, 0)),
                      pl.BlockSpec((N_CHIPS - 1, None, tr, cols), lambda i, j, where_ref: (0, i, j, 0))],
            out_specs=pl.BlockSpec((None, None, tr, cols), lambda i, j, where_ref: (i, where_ref[1], j, 0))),
        out_shape=jax.ShapeDtypeStruct((lead, 2, r, cols), F32),
        compiler_params=_params(dimension_semantics=("parallel", "parallel")),
    )(where, sums, got)


def _adamw(w, m, v, groups, name):
    shape = w.shape
    w2, m2, v2 = _rows2d(w), _rows2d(m), _rows2d(v)
    rows, cols = w2.shape
    ng = len(groups)
    n = len(groups[0])
    slab = rows // ng
    gs = [_rows2d(g) for grp in groups for g in grp]
    tm = _tile_rows(slab, cols, 7 + n)
    tiles = slab // tm
    c1 = 1.0 / (1.0 - ADAM_B1 ** ADAM_STEP)
    c2 = 1.0 / (1.0 - ADAM_B2 ** ADAM_STEP)

    def body(*refs):
        w_ref, m_ref, v_ref = refs[:3]
        g_refs = refs[3:3 + ng * n]
        go_ref, d_ref, mo_ref, vo_ref = refs[3 + ng * n:]
        which = pl.program_id(0)
        for s in range(ng):
            @pl.when(which == s)
            def _(s=s):
                g = g_refs[s * n][...]
                for r in g_refs[s * n + 1:(s + 1) * n]:
                    g = g + r[...]
                mn = ADAM_B1 * m_ref[...] + (1.0 - ADAM_B1) * g
                vn = ADAM_B2 * v_ref[...] + (1.0 - ADAM_B2) * (g * g)
                go_ref[...] = g
                mo_ref[...] = mn
                vo_ref[...] = vn
                d_ref[...] = -ADAM_LR * ((mn * c1) / (jnp.sqrt(vn * c2) + ADAM_EPS) + ADAM_WD * w_ref[...])

    spec = pl.BlockSpec((tm, cols), lambda s, i: (s * tiles + i, 0))
    g_specs = [pl.BlockSpec((tm, cols), lambda s, i, k=k: (jnp.where(s == k, i, jnp.where(s < k, 0, tiles - 1)), 0))
               for k in range(ng) for _ in range(n)]
    outs = pl.pallas_call(
        body, name=name,
        grid=(ng, tiles),
        in_specs=[spec] * 3 + g_specs,
        out_specs=[spec] * 4,
        out_shape=[jax.ShapeDtypeStruct((rows, cols), F32)] * 4,
        compiler_params=_params(dimension_semantics=("arbitrary", "arbitrary")),
    )(w2, m2, v2, *gs)
    return [o.reshape(shape) for o in outs]


def _lower_bounds(r0, r1):
    top = jnp.maximum(r0, r1)
    e0, e1 = jnp.exp(r0 - top), jnp.exp(r1 - top)
    p0, p1 = e0 / (e0 + e1), e1 / (e0 + e1)
    return p0 - p0, (p0 + p1) - p0


def _lbs_fwd(lb_raw):
    def body(lb_ref, out_ref):
        l0, l1 = _lower_bounds(lb_ref[0:1, :], lb_ref[1:2, :])
        out_ref[0:1, :] = l0
        out_ref[1:2, :] = l1

    return pl.pallas_call(body, name="lower_bounds", out_shape=jax.ShapeDtypeStruct(lb_raw.shape, F32), compiler_params=_params())(lb_raw)


def _mod_rows(c_all, w_mod, tn=768):
    _, D, cols = w_mod.shape

    def body(c_ref, w_ref, out_ref):
        out_ref[...] = _dot(c_ref[...].astype(BF), w_ref[...].astype(BF))

    return pl.pallas_call(
        body, name="mod_rows",
        grid=(DEPTH,),
        in_specs=[pl.BlockSpec((N_DEV, D), lambda l: (0, 0)), pl.BlockSpec((None, D, cols), lambda l: (l, 0, 0))],
        out_specs=pl.BlockSpec((N_DEV, cols), lambda l: (0, l)),
        out_shape=jax.ShapeDtypeStruct((N_DEV, DEPTH * cols), F32),
        compiler_params=_params(dimension_semantics=("parallel",)),
    )(c_all, w_mod)


def _grad_w_mod(c_all, dmod_cols):
    D = c_all.shape[1]
    cols = dmod_cols.shape[-1]

    def body(c_ref, d_ref, out_ref):
        out_ref[...] = _dot_tn(c_ref[...].astype(BF), d_ref[...].astype(BF))

    return pl.pallas_call(
        body, name="grad_w_mod",
        grid=(DEPTH,),
        in_specs=[pl.BlockSpec((N_DEV, D), lambda l: (0, 0)), pl.BlockSpec((None, N_DEV, cols), lambda l: (l, 0, 0))],
        out_specs=pl.BlockSpec((None, D, cols), lambda l: (l, 0, 0)),
        out_shape=jax.ShapeDtypeStruct((DEPTH, D, cols), F32),
        compiler_params=_params(dimension_semantics=("parallel",)),
    )(c_all, dmod_cols)


def _sum_devices(parts):
    _, R, N = parts.shape

    def body(p_ref, out_ref):
        acc = p_ref[0]
        for d in range(1, N_DEV):
            acc = acc + p_ref[d]
        out_ref[...] = acc

    return pl.pallas_call(body, name="sum_devices", out_shape=jax.ShapeDtypeStruct((R, N), F32), compiler_params=_params())(parts)


def _lbs_bwd(lb_raw, dl):
    def body(lb_ref, dl_ref, out_ref):
        _, pull = jax.vjp(_lower_bounds, lb_ref[0:1, :], lb_ref[1:2, :])
        d0, d1 = pull((dl_ref[0:1, :], dl_ref[1:2, :]))
        out_ref[0:1, :] = d0
        out_ref[1:2, :] = d1

    return pl.pallas_call(body, name="lower_bounds_bwd", out_shape=jax.ShapeDtypeStruct(lb_raw.shape, F32), compiler_params=_params())(lb_raw, dl)


def kernel(x, c, w_mod, b_mod, w_in, conv_w, hgrn_norm_w, lower_bounds, w_branch, w_out, ln_g, ln_b, loss_target, m_w_mod, m_b_mod, m_w_in, m_conv_w, m_hgrn_norm_w, m_lower_bounds, m_w_branch, m_w_out, m_ln_g, m_ln_b, v_w_mod, v_b_mod, v_w_in, v_conv_w, v_hgrn_norm_w, v_lower_bounds, v_w_branch, v_w_out, v_ln_g, v_ln_b):
    D = D_MODEL
    x0 = x[0]
    target = loss_target[0]
    S = x0.shape[0]
    mx, my, mc = lax.axis_index("x"), lax.axis_index("y"), lax.axis_index("c")
    chip = 2 * mx + my
    me = 2 * chip + mc
    mod_cols = 3 * D // N_CHIPS

    plan = _Plan(w_in, w_branch, w_out, chip, mc)
    n_conv = DEPTH * 3 * (WIDTH // N_CHIPS)
    first = jnp.concatenate([c, conv_w.reshape(1, n_conv), jnp.zeros((1, 2 * D - D - n_conv), F32)], axis=1)
    first = plan.first(first.reshape(8, 2 * D // 8)).reshape(N_DEV, 2 * D)

    c_all = first[:, :D]
    conv_all = first[:, D:D + n_conv].reshape(N_DEV, DEPTH, 3, WIDTH // N_CHIPS)[0::2]
    mod_part = _all_gather8(_mod_rows(c_all, w_mod), "gather_mod")[0::2]
    mod_part = lax.dynamic_index_in_dim(mod_part, me, axis=1, keepdims=False).reshape(N_CHIPS, DEPTH, mod_cols)
    mods = [(mod_part[:, l].reshape(1, 3 * D) + b_mod[l][None, :]) for l in range(DEPTH)]
    lbs = _lbs_fwd(lower_bounds).reshape(DEPTH, WIDTH // 128, 1, 128)
    loss_blk, dx, small = _local_step(x0, target, mods, lbs, conv_all, hgrn_norm_w, ln_g, ln_b, plan)

    n_mod, n_nw, n_lb, n_ln, n_cw = DEPTH * 3 * D, DEPTH * 128, DEPTH * WIDTH, DEPTH * D, DEPTH * 3 * WIDTH
    row = jnp.concatenate(
        [jnp.concatenate([small[l][0], small[l][1], small[l][2]], axis=1) for l in range(DEPTH)]
        + [jnp.sum(small[l][3], axis=0) for l in range(DEPTH)]
        + [small[l][4].reshape(1, WIDTH) for l in range(DEPTH)]
        + [small[l][5] for l in range(DEPTH)] + [small[l][6] for l in range(DEPTH)]
        + [jnp.transpose(small[l][7], (1, 0, 2)).reshape(1, 3 * WIDTH) for l in range(DEPTH)]
        + [loss_blk[0:1, :]], axis=1)
    n_row = row.shape[1]
    fold = -(-n_row // (8 * 128)) * 128
    rows = jnp.concatenate([row, jnp.zeros((1, 8 * fold - n_row), F32)], axis=1).reshape(8, fold)

    whole, gathered = plan.finish(rows)
    grads = {kind: [[whole[(kind, l)]] for l in range(DEPTH)] for kind in ("in", "br", "out")}

    off_nw = n_mod
    off_lb = off_nw + n_nw
    off_lng = off_lb + n_lb
    off_lnb = off_lng + n_ln
    off_cw = off_lnb + n_ln
    off_loss = off_cw + n_cw
    total = _sum_devices(gathered).reshape(1, 8 * fold)
    gathered = gathered.reshape(N_DEV, 1, 8 * fold)
    d_lower = _lbs_bwd(lower_bounds, total[0, off_lb:off_lng].reshape(DEPTH, WIDTH))
    loss = total[0, off_loss]
    d_b_mod = total[0, :n_mod].reshape(DEPTH, 3 * D)
    d_norm_w = total[0, off_nw:off_lb].reshape(DEPTH, 128)
    d_ln_g = total[0, off_lng:off_lnb].reshape(DEPTH, D)
    d_ln_b = total[0, off_lnb:off_cw].reshape(DEPTH, D)
    d_conv = total[0, off_cw:off_loss].reshape(DEPTH, 3, N_CHIPS, WIDTH // N_CHIPS)
    d_conv = lax.dynamic_index_in_dim(d_conv, chip, axis=2, keepdims=False)
    dmod_all = gathered[:, 0, :n_mod].reshape(N_DEV, DEPTH, N_CHIPS, mod_cols)
    dmod_cols = jnp.transpose(lax.dynamic_index_in_dim(dmod_all, chip, axis=2, keepdims=False), (1, 0, 2))
    d_w_mod = _grad_w_mod(c_all, dmod_cols)

    res = {}
    res["w_mod"] = _adamw(w_mod, m_w_mod, v_w_mod, [[d_w_mod]], "adamw_w_mod")
    res["b_mod"] = _adamw(b_mod, m_b_mod, v_b_mod, [[d_b_mod]], "adamw_b_mod")
    res["w_in"] = _adamw(w_in, m_w_in, v_w_in, grads["in"], "adamw_w_in")
    res["conv_w"] = _adamw(conv_w, m_conv_w, v_conv_w, [[d_conv]], "adamw_conv_w")
    res["hgrn_norm_w"] = _adamw(hgrn_norm_w, m_hgrn_norm_w, v_hgrn_norm_w, [[d_norm_w]], "adamw_norm_w")
    res["lower_bounds"] = _adamw(lower_bounds, m_lower_bounds, v_lower_bounds, [[d_lower]], "adamw_lower_bounds")
    res["w_branch"] = _adamw(w_branch, m_w_branch, v_w_branch, grads["br"], "adamw_w_branch")
    res["w_out"] = _adamw(w_out, m_w_out, v_w_out, grads["out"], "adamw_w_out")
    res["ln_g"] = _adamw(ln_g, m_ln_g, v_ln_g, [[d_ln_g]], "adamw_ln_g")
    res["ln_b"] = _adamw(ln_b, m_ln_b, v_ln_b, [[d_ln_b]], "adamw_ln_b")
    names = ["w_mod", "b_mod", "w_in", "conv_w", "hgrn_norm_w", "lower_bounds", "w_branch", "w_out", "ln_g", "ln_b"]
    return (loss, dx[None], *[res[n][0] for n in names], *[res[n][1] for n in names],
            *[res[n][2] for n in names], *[res[n][3] for n in names])


class _Plan:
    FIRST_CHUNKS = 2
    WINDOWS = ((0, 1280), (1280, 640), (1920, 384))

    def __init__(self, w_in, w_br, w_out, chip, core):
        self.local = {"in": w_in, "br": w_br, "out": w_out}
        self.chip, self.where = chip, jnp.stack([chip, core]).astype(jnp.int32)
        self.gathered, self.partial, self.grads, self.chip_sums, self.scattered, self.pending = {}, {}, {}, {}, {}, {}

    def chunks(self, l):
        return self.FIRST_CHUNKS if l == 0 else 1

    def w_in(self, l):
        return [self.gathered[("in", l, c)] for c in range(self.chunks(l))]

    def _shard(self, key):
        mine = self.local[key[0]][key[1]]
        if key[0] == "in":
            cols = mine.shape[-1] // self.chunks(key[1])
            mine = mine[:, key[2] * cols:(key[2] + 1) * cols]
        return mine.astype(BF)

    def _slab(self, key):
        mine = _halves(self._shard(key))
        return lax.dynamic_update_slice(lax.empty((N_CHIPS,) + mine.shape, mine.dtype), mine[None], (self.chip, 0, 0, 0, 0))

    def _gather(self, keys):
        return ("gather", keys), _gather_job([self._slab(key) for key in keys])

    def _fetch(self, keys):
        return ("gather_part", keys), _gather_job([self._slab(key) for key in keys], None, ())

    def _hand_over(self, keys):
        return ("gather", keys), _gather_job([self.partial[key] for key in keys], (), (None,))

    def _gather_window(self, key, n):
        slab = self._slab(key) if n == 0 else self.partial[key]
        last = n == len(self.WINDOWS) - 1
        handed = self.WINDOWS[max(n - 1, 0):n] + (self.WINDOWS[n:] if last else ())
        return ("gather" if last else "gather_part", [key]), _gather_job([slab], self.WINDOWS[n], handed)

    def _to_sibling(self, keys):
        return ("to_sibling", keys), _to_sibling_job([_halves(self.grads[key], 1) for key in keys])

    def _scatter(self, keys):
        return ("scatter", keys), _scatter_job([self.chip_sums[key] for key in keys])

    def job(self, stage, l, c=0):
        parts = []
        if stage == "proj_fwd":
            parts = [self._gather([("in", l, c + 1)]) if c + 1 < self.chunks(l) else self._fetch([("br", l), ("out", l)])]
        elif stage == "attn_fwd":
            parts = [self._hand_over([("br", l), ("out", l)])] + ([self._gather_window(("in", l + 1, 0), 0)] if l + 1 < DEPTH else [])
        elif stage == "hgrn_fwd" and l + 1 < DEPTH:
            parts = [self._gather_window(("in", l + 1, 0), 1)]
        elif stage == "merge_fwd" and l + 1 < DEPTH:
            parts = [self._gather_window(("in", l + 1, 0), 2)]
        elif stage == "attn_bwd":
            parts = [self._to_sibling([("out", l), ("br", l)])] + ([self._scatter([("in", l + 1)])] if l + 1 < DEPTH else [])
        elif stage == "hgrn_bwd":
            parts = [self._scatter([("out", l), ("br", l)])]
        elif stage == "proj_bwd":
            parts = [self._to_sibling([("in", l)])] if l else [self._scatter([("in", 0)])]
        self.pending[(stage, l, c)] = [(tag, len(job.outs)) for tag, job in parts]
        return _join_jobs([job for _, job in parts])

    def done(self, stage, l, outs, c=0):
        if outs is None:
            return
        at = 0
        for (what, keys), n_outs in self.pending[(stage, l, c)]:
            mine, at = outs[at:at + n_outs], at + n_outs
            for n, key in enumerate(keys):
                if what == "gather":
                    self.gathered[key] = mine[n].reshape((N_CHIPS,) + self._shard(key).shape)
                elif what == "gather_part":
                    self.partial[key] = mine[n]
                elif what == "to_sibling":
                    self.chip_sums[key] = _sum_cores(_halves(self.grads[key], 1), mine[n], self.where)
                else:
                    self.scattered[key] = mine[n]

    def first(self, rows):
        tag, job = self._gather([("in", 0, 0)])
        self.pending[("first", 0, 0)] = [(tag, len(job.outs))]
        outs = _run_job(_join_jobs([job, _gather8_job(rows)]), "gather_first")
        self.done("first", 0, outs[:-1])
        return outs[-1]

    def took(self, key, grad):
        self.grads[key] = grad
        if key == ("in", 0):
            tag, job = self._to_sibling([key])
            self.pending[("took", 0, 0)] = [(tag, len(job.outs))]
            self.done("took", 0, _run_job(job, "to_sibling_last"))

    def finish(self, rows):
        keys = [(kind, l) for kind in ("in", "br", "out") for l in range(DEPTH)]
        halves = [_sum_chips(self.chip_sums[key], self.scattered[key], self.where) for key in keys]
        outs = _run_job(_join_jobs([_place_job(halves), _gather8_job(rows)]), "place_halves")
        return {key: w.reshape(self.grads[key].shape[1:]) for key, w in zip(keys, outs[:-1])}, outs[-1]


def _local_step(x0, target, mods, lbs, conv_all, hgrn_norm_w, ln_g, ln_b, plan):
    D = D_MODEL
    after, before = _attn_consts()
    hg_consts = _hgrn_consts()

    saved = []
    xl = x0
    for l in range(DEPTH):
        n = plan.chunks(l)
        (proj, h), got = _proj_fwd(xl, mods[l], plan.gathered[("in", l, 0)], plan.job("proj_fwd", l, 0), (0, n))
        plan.done("proj_fwd", l, got, 0)
        for c in range(1, n):
            (proj,), got = _proj_cols(h, plan.gathered[("in", l, c)], (c, n), proj, plan.job("proj_fwd", l, c))
            plan.done("proj_fwd", l, got, c)
        (o_a, tot), got = _attn_fwd(proj, after, plan.job("attn_fwd", l))
        plan.done("attn_fwd", l, got)
        (o_b, states), got = _hgrn_fwd(proj, lbs[l], hg_consts, plan.job("hgrn_fwd", l))
        plan.done("hgrn_fwd", l, got)
        ys = _branch_fwd(proj, o_a, o_b, hgrn_norm_w, conv_all, l)
        wb = jnp.concatenate(list(plan.gathered[("br", l)]), axis=-1)
        (x_next, merged, y), got = _merge_fwd(xl, mods[l], proj, ys, wb, plan.gathered[("out", l)], ln_g, ln_b, l, plan.job("merge_fwd", l))
        plan.done("merge_fwd", l, got)
        saved.append((xl, proj, h, o_a, tot, o_b, states, ys, merged, y, wb))
        xl = x_next
    dx, loss_blk = _loss_head(xl, target)

    small = [None] * DEPTH
    for l in reversed(range(DEPTH)):
        xin, proj, h, o_a, tot, o_b, states, ys, merged, y, wb = saved[l]
        dx_res, dproj, dy_a, dy_b, dy_c, dln_g, dln_b, dgate, g_br, g_out = _merge_bwd(
            dx, xin, y, merged, mods[l], proj, ys, wb, plan.gathered[("out", l)], ln_g, l)
        plan.took(("out", l), g_out)
        plan.took(("br", l), g_br)
        d_oa, d_ob, dnorm_w, dconv_w, dproj = _branch_bwd(proj, o_a, o_b, hgrn_norm_w, conv_all, dy_a, dy_b, dy_c, dproj, l)
        (dproj,), got = _attn_bwd(proj, d_oa, tot, after, before, dproj, plan.job("attn_bwd", l))
        plan.done("attn_bwd", l, got)
        (dlb, dproj), got = _hgrn_bwd(proj, lbs[l], states, d_ob, hg_consts, dproj, plan.job("hgrn_bwd", l))
        plan.done("hgrn_bwd", l, got)
        plan.took(("in", l), _grad_w_in(h, dproj))
        wgs = plan.w_in(l)
        tile = {"tk": wgs[0].shape[-1]} if len(wgs) > 1 else {}
        (dx, dshift, dscale), got = _proj_bwd(dproj, wgs, xin, mods[l], dx_res, plan.job("proj_bwd", l), **tile)
        plan.done("proj_bwd", l, got)
        small[l] = (dshift, dscale, dgate, dnorm_w, dlb, dln_g, dln_b, dconv_w)
    return loss_blk, dx, small
```

```python
import functools
import math

import numpy as np
import jax
import jax.numpy as jnp
from jax import lax
from jax.experimental import pallas as pl
from jax.experimental.pallas import tpu as pltpu

F32 = jnp.float32
BF = jnp.bfloat16
MESH = pl.DeviceIdType.MESH

DEPTH = 2
D_MODEL = 1024
WIDTH = 512
IN_COLS = 12 * WIDTH + 3 * D_MODEL
N_CHIPS = 4
N_DEV = 8
SB_BLOCK = 128
SB_HEAD_DIM = 64
HG_CHUNK = 128
HG_DIM = 128
LN_EPS = 1e-5
RMS_EPS = 1e-6
ALPHA = (2.0 * DEPTH) ** 0.25
ADAM_LR, ADAM_B1, ADAM_B2, ADAM_EPS, ADAM_WD, ADAM_STEP = 0.001, 0.9, 0.999, 1e-08, 0.01, 10
VMEM_LIMIT = 56 << 20


def _params(**kw):
    return pltpu.CompilerParams(vmem_limit_bytes=VMEM_LIMIT, **kw)


def _dot(a, b):
    return jnp.dot(a, b, preferred_element_type=F32)


def _dot_nt(a, b):
    return lax.dot_general(a, b, (((1,), (1,)), ((), ())), preferred_element_type=F32)


def _dot_tn(a, b):
    return lax.dot_general(a, b, (((0,), (0,)), ((), ())), preferred_element_type=F32)


def _sigmoid(x):
    return 1.0 / (1.0 + jnp.exp(-x))


def _silu(x):
    return x * _sigmoid(x)


def _standardize(x):
    mu = jnp.mean(x, axis=-1, keepdims=True)
    xc = x - mu
    var = jnp.mean(xc * xc, axis=-1, keepdims=True)
    rstd = lax.rsqrt(var + LN_EPS)
    return xc * rstd, rstd


def _standardize_bwd(dxs, xs, rstd):
    return rstd * (dxs - jnp.mean(dxs, axis=-1, keepdims=True) - xs * jnp.mean(dxs * xs, axis=-1, keepdims=True))


class _Job:
    def __init__(self, ins, outs, sems, make, alias=None):
        self.ins, self.outs, self.sems, self.make = list(ins), list(outs), list(sems), make
        self.alias = dict(alias or {})


def _join_jobs(jobs):
    jobs = [j for j in jobs if j is not None]
    if len(jobs) <= 1:
        return jobs[0] if jobs else None

    def make(ins, outs, sems):
        phases, i, o, s = [], 0, 0, 0
        for j in jobs:
            got = j.make(ins[i:i + len(j.ins)], outs[o:o + len(j.outs)], sems[s:s + len(j.sems)])
            i, o, s = i + len(j.ins), o + len(j.outs), s + len(j.sems)
            for n, phase in enumerate(got):
                if n == len(phases):
                    phases.append([])
                phases[n] += phase
        return phases

    alias, i, o = {}, 0, 0
    for j in jobs:
        alias.update({i + a: o + b for a, b in j.alias.items()})
        i, o = i + len(j.ins), o + len(j.outs)
    return _Job(sum([j.ins for j in jobs], []), sum([j.outs for j in jobs], []), sum([j.sems for j in jobs], []), make, alias)


def _flip(v, bit):
    return 1 - v if bit else v


def _halves(a, front=0):
    shape = a.shape
    lead = math.prod(shape[front:-2])
    return a.reshape(shape[:front] + (lead, 2, shape[-2] // 2, shape[-1]))


def _dma_sems(*shapes):
    return [pltpu.SemaphoreType.DMA(s) for s in shapes]


def _same(arrays):
    return [jax.ShapeDtypeStruct(a.shape, a.dtype) for a in arrays]


def _gather_job(slabs, window=None, handed=(None,)):
    n = len(slabs)
    cols = lambda w: slice(None) if w is None else pl.ds(*w)

    def make(ins, outs, sems):
        send1, recv1, send2, recv2 = sems
        mx, my, mc = lax.axis_index("x"), lax.axis_index("y"), lax.axis_index("c")
        fetch, pass_on = [], []
        for a in range(n):
            for k in range(1, N_CHIPS):
                px, py = _flip(mx, k & 2), _flip(my, k & 1)
                if window != ():
                    ours = outs[a].at[2 * mx + my, :, mc, :, cols(window)]
                    fetch.append(pltpu.make_async_remote_copy(
                        src_ref=ours, dst_ref=ours, send_sem=send1.at[a, k - 1], recv_sem=recv1.at[a, k - 1],
                        device_id=(px, py, mc), device_id_type=MESH))
                for h, w in enumerate(handed):
                    theirs = outs[a].at[2 * px + py, :, mc, :, cols(w)]
                    (pass_on if w == window else fetch).append(pltpu.make_async_remote_copy(
                        src_ref=theirs, dst_ref=theirs, send_sem=send2.at[a, k - 1, h], recv_sem=recv2.at[a, k - 1, h],
                        device_id=(mx, my, 1 - mc), device_id_type=MESH))
        return [fetch, pass_on] if pass_on else [fetch]

    pairs = (n, N_CHIPS - 1)
    each = pairs + (max(len(handed), 1),)
    return _Job(slabs, _same(slabs), _dma_sems(pairs, pairs, each, each), make, {a: a for a in range(n)})


def _to_sibling_job(grads):
    n = len(grads)

    def make(ins, outs, sems):
        send_sems, recv_sems = sems
        mx, my, mc = lax.axis_index("x"), lax.axis_index("y"), lax.axis_index("c")
        return [[pltpu.make_async_remote_copy(
            src_ref=ins[a].at[:, :, 1 - mc], dst_ref=outs[a], send_sem=send_sems.at[a], recv_sem=recv_sems.at[a],
            device_id=(mx, my, 1 - mc), device_id_type=MESH) for a in range(n)]]

    outs = [jax.ShapeDtypeStruct(g.shape[:2] + g.shape[3:], g.dtype) for g in grads]
    return _Job(grads, outs, _dma_sems((n,), (n,)), make)


def _scatter_job(sums):
    n = len(sums)

    def make(ins, outs, sems):
        send_sems, recv_sems = sems
        mx, my, mc = lax.axis_index("x"), lax.axis_index("y"), lax.axis_index("c")
        copies = []
        for a in range(n):
            for k in range(1, N_CHIPS):
                px, py = _flip(mx, k & 2), _flip(my, k & 1)
                copies.append(pltpu.make_async_remote_copy(
                    src_ref=ins[a].at[2 * px + py], dst_ref=outs[a].at[k - 1], send_sem=send_sems.at[a, k - 1],
                    recv_sem=recv_sems.at[a, k - 1], device_id=(px, py, mc), device_id_type=MESH))
        return [copies]

    pairs = (n, N_CHIPS - 1)
    return _Job(sums, [jax.ShapeDtypeStruct((N_CHIPS - 1,) + s.shape[1:], s.dtype) for s in sums], _dma_sems(pairs, pairs), make)


def _place_job(wholes):
    n = len(wholes)

    def make(ins, outs, sems):
        send_sems, recv_sems = sems
        mx, my, mc = lax.axis_index("x"), lax.axis_index("y"), lax.axis_index("c")
        copies = []
        for a in range(n):
            here = outs[a].at[:, mc]
            copies.append(pltpu.make_async_remote_copy(src_ref=here, dst_ref=here, send_sem=send_sems.at[a], recv_sem=recv_sems.at[a],
                                                       device_id=(mx, my, 1 - mc), device_id_type=MESH))
        return [copies]

    return _Job(wholes, _same(wholes), _dma_sems((n,), (n,)), make, {a: a for a in range(n)})


def _gather8_job(x):
    def make(ins, outs, sems):
        local_sem, send_sems, recv_sems = sems
        mx, my, mc = lax.axis_index("x"), lax.axis_index("y"), lax.axis_index("c")
        here = outs[0].at[4 * mx + 2 * my + mc]
        copies = [pltpu.make_async_copy(ins[0], here, local_sem.at[0])]
        for k in range(1, N_DEV):
            peer = (_flip(mx, k & 4), _flip(my, k & 2), _flip(mc, k & 1))
            copies.append(pltpu.make_async_remote_copy(src_ref=ins[0], dst_ref=here, send_sem=send_sems.at[k - 1],
                                                       recv_sem=recv_sems.at[k - 1], device_id=peer, device_id_type=MESH))
        return [copies]

    return _Job([x], [jax.ShapeDtypeStruct((N_DEV,) + x.shape, x.dtype)], _dma_sems((1,), (N_DEV - 1,), (N_DEV - 1,)), make)


def _run_phases(phases, first=0):
    for n, phase in enumerate(phases):
        if n >= first:
            for cp in phase:
                cp.start()
        for cp in phase:
            cp.wait()


def _run_job(job, name):
    k_in, k_out = len(job.ins), len(job.outs)

    def body(*refs):
        _run_phases(job.make(refs[:k_in], refs[k_in:k_in + k_out], refs[k_in + k_out:]))

    hbm = pl.BlockSpec(memory_space=pl.ANY)
    return pl.pallas_call(body, name=name, in_specs=[hbm] * k_in, out_specs=[hbm] * k_out, out_shape=job.outs,
                          scratch_shapes=job.sems, input_output_aliases=job.alias, compiler_params=_params())(*job.ins)


def _hosted(body, job, args, *, name, grid, in_specs, out_specs, out_shape, scratch_shapes=(), semantics, aliases=None):
    in_specs, out_specs, out_shape, scratch = list(in_specs), list(out_specs), list(out_shape), list(scratch_shapes)
    aliases = dict(aliases or {})
    if job is None:
        outs = pl.pallas_call(body, name=name, grid=grid, in_specs=in_specs, out_specs=out_specs, out_shape=out_shape,
                              scratch_shapes=scratch, input_output_aliases=aliases,
                              compiler_params=_params(dimension_semantics=semantics))(*args)
        return list(outs), None
    n_in, n_out, n_scr, k_in, k_out = len(in_specs), len(out_specs), len(scratch), len(job.ins), len(job.outs)

    def wrapped(*refs):
        ins, rest = refs[:n_in], refs[n_in:]
        job_ins, rest = rest[:k_in], rest[k_in:]
        outs, rest = rest[:n_out], rest[n_out:]
        job_outs, rest = rest[:k_out], rest[k_out:]
        scr, sems = rest[:n_scr], rest[n_scr:]
        ids = [pl.program_id(a) for a in range(len(grid))]
        first = functools.reduce(jnp.logical_and, [i == 0 for i in ids])
        last = functools.reduce(jnp.logical_and, [i == g - 1 for i, g in zip(ids, grid)])

        @pl.when(first)
        def _():
            for cp in job.make(job_ins, job_outs, sems)[0]:
                cp.start()

        body(*ins, *outs, *scr)

        @pl.when(last)
        def _():
            _run_phases(job.make(job_ins, job_outs, sems), first=1)

    hbm = pl.BlockSpec(memory_space=pl.ANY)
    outs = pl.pallas_call(
        wrapped, name=name, grid=grid, in_specs=in_specs + [hbm] * k_in, out_specs=out_specs + [hbm] * k_out,
        out_shape=out_shape + job.outs, scratch_shapes=scratch + job.sems,
        input_output_aliases={**aliases, **{n_in + i: n_out + o for i, o in job.alias.items()}},
        compiler_params=_params(dimension_semantics=("arbitrary",) * len(grid)))(*args, *job.ins)
    return list(outs[:n_out]), list(outs[n_out:])


def _proj_fwd(x, mod, wg, job=None, chunk=(0, 1), tm=512):
    S, D = x.shape
    tm = min(tm, S)
    tn = wg.shape[-1]
    c, n = chunk

    rows = S // tm

    def body(x_ref, mod_ref, w_ref, proj_ref, h_ref, hs):
        mine = pl.ds(pl.multiple_of(pl.program_id(1) * tm, tm), tm)

        @pl.when(pl.program_id(0) == 0)
        def _():
            xs, _ = _standardize(x_ref[...])
            h = xs * (1.0 + mod_ref[:, D:2 * D]) + mod_ref[:, 0:D]
            hb = h.astype(BF)
            hs[mine, :] = hb
            h_ref[...] = hb

        proj_ref[...] = _dot(hs[mine, :], w_ref[...])

    once = lambda j, i: jnp.where(j == 0, i, rows - 1)
    return _hosted(
        body, job, (x, mod, wg), name="proj_fwd",
        grid=(N_CHIPS, rows),
        in_specs=[pl.BlockSpec((tm, D), lambda j, i: (once(j, i), 0)),
                  pl.BlockSpec((1, 3 * D), lambda j, i: (0, 0)),
                  pl.BlockSpec((None, D, tn), lambda j, i: (j, 0, 0))],
        out_specs=[pl.BlockSpec((tm, tn), lambda j, i: (i, j * n + c)),
                   pl.BlockSpec((tm, D), lambda j, i: (once(j, i), 0))],
        out_shape=[jax.ShapeDtypeStruct((S, IN_COLS), F32), jax.ShapeDtypeStruct((S, D), BF)],
        scratch_shapes=[pltpu.VMEM((S, D), BF)],
        semantics=("arbitrary", "arbitrary"))


def _proj_cols(h, wg, chunk, proj, job=None, tm=512):
    S, D = h.shape
    tm = min(tm, S)
    tn = wg.shape[-1]
    c, n = chunk

    def body(h_ref, w_ref, prev_ref, proj_ref):
        mine = pl.ds(pl.multiple_of(pl.program_id(1) * tm, tm), tm)
        proj_ref[...] = _dot(h_ref[mine, :], w_ref[...])

    in_specs = [pl.BlockSpec((S, D), lambda j, i: (0, 0)), pl.BlockSpec((None, D, tn), lambda j, i: (j, 0, 0)),
                pl.BlockSpec(memory_space=pl.ANY)]
    return _hosted(body, job, (h, wg, proj), name="proj_cols", grid=(N_CHIPS, S // tm), in_specs=in_specs,
                   out_specs=[pl.BlockSpec((tm, tn), lambda j, i: (i, j * n + c))],
                   out_shape=[jax.ShapeDtypeStruct(proj.shape, proj.dtype)],
                   semantics=("arbitrary", "arbitrary"), aliases={2: 0})


SB_ROWS = 256
SB_KEYS = 256


def _attn_consts():
    j = np.arange(SB_KEYS)[:, None]
    s = np.arange(SB_KEYS)[None, :]
    from_here = np.concatenate([(j >= s), (j >= s)], axis=0).astype(np.float32)
    return jnp.asarray(from_here, BF), jnp.asarray((j <= s).astype(np.float32), BF)


def _hi_lo(x):
    hi = lax.bitcast_convert_type(lax.bitcast_convert_type(x, jnp.uint32) & jnp.uint32(0xFFFF0000), F32)
    return hi.astype(BF), (x - hi).astype(BF)


def _sums_r(x, t2):
    hi, lo = _hi_lo(x)
    return _dot(jnp.concatenate([hi, lo], axis=1), t2)


def _all_lanes(col, lanes):
    return jnp.broadcast_to(col, (col.shape[0], lanes))


def _attn_rows(ref, r0, rows, lanes, head0, scale=None):
    v = ref[pl.ds(r0, rows), lanes]
    if scale is not None:
        v = v * scale
    return jnp.concatenate([jnp.where(head0, v, 0.0), jnp.where(head0, 0.0, v)], axis=0).astype(BF)


SB_PAIRS_FWD = 4
SB_PAIRS_BWD = 2


def _attn_specs(S, n_pairs):
    return lambda base: pl.BlockSpec((S, n_pairs * SB_BLOCK), lambda p, base=base: (0, base // n_pairs + p))


def _attn_scores(q2n, k_ref, lanes, kj, t2, from_here_ref, masked):
    c0 = pl.multiple_of(kj * SB_KEYS, SB_KEYS)
    kb = k_ref[pl.ds(c0, SB_KEYS), lanes].astype(BF)
    zn = _dot_nt(q2n, kb)
    lsb = jnp.minimum(zn, 0.0) - jnp.log(1.0 + jnp.exp(-jnp.abs(zn)))
    valid = None
    if masked:
        valid = (lax.broadcasted_iota(jnp.int32, zn.shape, 1) + kj * SB_KEYS) < t2
        lsb = jnp.where(valid, lsb, 0.0)
    return c0, kb, zn, valid, lsb, _sums_r(lsb, from_here_ref[...])


def _attn_fwd(proj, from_here, job=None):
    S = proj.shape[0]
    TQ = SB_ROWS
    assert S % TQ == 0 and SB_KEYS == TQ
    scale = SB_HEAD_DIM ** -0.5
    n_pairs = SB_PAIRS_FWD
    pairs = range(n_pairs)
    lanes = [pl.ds(p * SB_BLOCK, SB_BLOCK) for p in pairs]

    def body(q_ref, k_ref, v_ref, from_here_ref, o_ref, tot_ref, run, acc):
        head0 = lax.broadcasted_iota(jnp.int32, (1, 2 * SB_HEAD_DIM), 1) < SB_HEAD_DIM

        def qloop(qi, _):
            r0 = pl.multiple_of(qi * TQ, TQ)
            q2n = [_attn_rows(q_ref, r0, TQ, lanes[p], head0, -scale) for p in pairs]
            trow = lax.broadcasted_iota(jnp.int32, (TQ, SB_KEYS), 0) + qi * TQ
            t2 = jnp.concatenate([trow, trow], axis=0)
            run[...] = jnp.zeros_like(run)
            acc[...] = jnp.zeros_like(acc)

            def step(kj, masked):
                got = [_attn_scores(q2n[p], k_ref, lanes[p], kj, t2, from_here_ref, masked) for p in pairs]
                for p in pairs:
                    c0, _, zn, valid, _, sums = got[p]
                    r = run[p]
                    e = sums - zn + jnp.concatenate([r, r], axis=1)
                    if masked:
                        e = jnp.where(valid, e, -jnp.inf)
                    acc[p] += _dot(jnp.exp(e).astype(BF), v_ref[pl.ds(c0, SB_KEYS), lanes[p]].astype(BF))
                    run[p] = r + _all_lanes(sums[:, 0:1], SB_BLOCK)

            @pl.when(qi % 2 == 0)
            def _():
                step(qi, True)

            @pl.when(qi % 2 == 1)
            def _():
                step(qi, True)
                step(qi - 1, False)

            first = qi - 1 - qi % 2

            def below(n, _):
                step(first - 2 * n, False)
                step(first - 1 - 2 * n, False)
                return 0

            lax.fori_loop(0, qi // 2, below, 0)
            for p in pairs:
                o_ref[pl.ds(r0, TQ), lanes[p]] = jnp.where(head0, acc[p, 0:TQ, :], acc[p, TQ:2 * TQ, :])
                tot_ref[p, 0, pl.ds(r0, TQ), :] = run[p, 0:TQ, :]
                tot_ref[p, 1, pl.ds(r0, TQ), :] = run[p, TQ:2 * TQ, :]
            return 0

        lax.fori_loop(0, S // TQ, qloop, 0)

    col = _attn_specs(S, n_pairs)
    state = pltpu.VMEM((n_pairs, 2 * TQ, SB_BLOCK), F32)
    return _hosted(
        body, job, (proj, proj, proj, from_here), name="attn_fwd",
        grid=(WIDTH // (n_pairs * SB_BLOCK),),
        in_specs=[col(0), col(4), col(8), pl.BlockSpec(from_here.shape, lambda p: (0, 0))],
        out_specs=[col(0), pl.BlockSpec((n_pairs, 2, S, 128), lambda p: (p, 0, 0, 0))],
        out_shape=[jax.ShapeDtypeStruct((S, WIDTH), F32), jax.ShapeDtypeStruct((WIDTH // 128, 2, S, 128), F32)],
        scratch_shapes=[state, state],
        semantics=("parallel",))


def _attn_bwd(proj, d_o, tot, from_here, up_to, dproj, job=None):
    S = proj.shape[0]
    TQ = SB_ROWS
    assert S % TQ == 0 and SB_KEYS == TQ
    scale = SB_HEAD_DIM ** -0.5
    n_pairs = SB_PAIRS_BWD
    pairs = range(n_pairs)
    lanes = [pl.ds(p * SB_BLOCK, SB_BLOCK) for p in pairs]

    def body(q_ref, k_ref, v_ref, do_ref, tot_ref, from_here_ref, up_to_ref, _, dproj_ref, pre, cum, dq_acc, dk_acc, dv_acc, stage, sems):
        dq_ref, dk_ref, dv_ref = stage.at[0], stage.at[1], stage.at[2]
        head0 = lax.broadcasted_iota(jnp.int32, (1, 2 * SB_HEAD_DIM), 1) < SB_HEAD_DIM
        dk_acc[...] = jnp.zeros_like(dk_acc)
        dv_acc[...] = jnp.zeros_like(dv_acc)

        def qloop(qi, _):
            r0 = pl.multiple_of(qi * TQ, TQ)
            q2n = [_attn_rows(q_ref, r0, TQ, lanes[p], head0, -scale) for p in pairs]
            do2 = [_attn_rows(do_ref, r0, TQ, lanes[p], head0) for p in pairs]
            trow = lax.broadcasted_iota(jnp.int32, (TQ, SB_KEYS), 0) + qi * TQ
            t2 = jnp.concatenate([trow, trow], axis=0)
            for p in pairs:
                pre[p, 0:TQ, :] = tot_ref[p, 0, pl.ds(r0, TQ), :]
                pre[p, TQ:2 * TQ, :] = tot_ref[p, 1, pl.ds(r0, TQ), :]
            cum[...] = jnp.zeros_like(cum)
            dq_acc[...] = jnp.zeros_like(dq_acc)

            def step(kj, masked):
                got = [_attn_scores(q2n[p], k_ref, lanes[p], kj, t2, from_here_ref, masked) for p in pairs]
                das = [_dot_nt(do2[p], v_ref[pl.ds(got[p][0], SB_KEYS), lanes[p]].astype(BF)) for p in pairs]
                for p in pairs:
                    c0, kb, zn, valid, lsb, sums = got[p]
                    later = pre[p] - _all_lanes(sums[:, 0:1], SB_BLOCK)
                    pre[p] = later
                    e = sums - zn + jnp.concatenate([later, later], axis=1)
                    sig = jnp.exp(lsb - zn)
                    if masked:
                        e = jnp.where(valid, e, -jnp.inf)
                        sig = jnp.where(valid, sig, 0.0)
                    a = jnp.exp(e)
                    w = das[p] * a
                    upto = _dot(w.astype(BF), up_to_ref[...])
                    c = cum[p]
                    dz = w - sig * (upto + jnp.concatenate([c, c], axis=1))
                    cum[p] = c + _all_lanes(upto[:, SB_KEYS - 1:SB_KEYS], SB_BLOCK)
                    dzb = dz.astype(BF)
                    dq_acc[p] += _dot(dzb, kb)
                    dk_acc[pl.ds(c0, SB_KEYS), lanes[p]] += _dot_tn(dzb, q2n[p])
                    dv_acc[pl.ds(c0, SB_KEYS), lanes[p]] += _dot_tn(a.astype(BF), do2[p])

            def below(n, _):
                step(2 * n, False)
                step(2 * n + 1, False)
                return 0

            lax.fori_loop(0, qi // 2, below, 0)

            @pl.when(qi % 2 == 1)
            def _():
                step(qi - 1, False)
                step(qi, True)

            @pl.when(qi % 2 == 0)
            def _():
                step(qi, True)
            for p in pairs:
                dq_ref[pl.ds(r0, TQ), lanes[p]] = (jnp.where(head0, dq_acc[p, 0:TQ, :], dq_acc[p, TQ:2 * TQ, :]) * scale).astype(BF)
            return 0

        lax.fori_loop(0, S // TQ, qloop, 0)
        dk_ref[...] = (-dk_acc[...]).astype(BF)
        dv_ref[...] = dv_acc[...].astype(BF)
        lane0 = pl.program_id(0) * (n_pairs * SB_BLOCK)
        _put_columns([dq_ref, dk_ref, dv_ref], dproj_ref, [WIDTH * n + lane0 for n in range(3)], sems)

    col = _attn_specs(S, n_pairs)
    whole = lambda a: pl.BlockSpec(a.shape, lambda p: (0, 0))
    hbm = pl.BlockSpec(memory_space=pl.ANY)
    state = pltpu.VMEM((n_pairs, 2 * TQ, SB_BLOCK), F32)
    grads = pltpu.VMEM((S, n_pairs * SB_BLOCK), F32)
    return _hosted(
        body, job, (proj, proj, proj, d_o, tot, from_here, up_to, dproj), name="attn_bwd",
        grid=(WIDTH // (n_pairs * SB_BLOCK),),
        in_specs=[col(0), col(4), col(8), col(0), pl.BlockSpec((n_pairs, 2, S, 128), lambda p: (p, 0, 0, 0)), whole(from_here),
                  whole(up_to), hbm],
        out_specs=[hbm],
        out_shape=[jax.ShapeDtypeStruct(dproj.shape, dproj.dtype)],
        scratch_shapes=[state, state, state, grads, grads, pltpu.VMEM((3, S, n_pairs * SB_BLOCK), BF), pltpu.SemaphoreType.DMA((3,))],
        semantics=("arbitrary",), aliases={7: 0})


HG_LEVELS = tuple(HG_CHUNK >> n for n in range(1, HG_CHUNK.bit_length()))


def _hgrn_consts():
    C = HG_CHUNK
    t = np.arange(C)[:, None]
    s = np.arange(C)[None, :]
    rows = [(s <= t), (s > t)]
    masks = [(t == s)]
    for m in HG_LEVELS:
        two = 2 * m
        mid = (t // two) * two + m
        right = (t % two) >= m
        rows.append((right & (s >= mid) & (s <= t)) | ((~right) & (s > t) & (s <= mid - 1)))
        masks.append(((t // two) == (s // two)) & right & ((s % two) < m))
    tri = np.concatenate(rows, axis=0).astype(np.float32)
    twice = lambda a: jnp.asarray(np.concatenate([a, a], axis=1), BF)
    return (twice(tri), twice(tri.T), jnp.asarray(np.stack(masks).astype(np.float32), F32))


HG_SUM_BLOCKS = 2 + len(HG_LEVELS)


def _split_rows(g):
    hi = g.astype(BF)
    return jnp.concatenate([hi, (g - hi.astype(F32)).astype(BF)], axis=0)


def _hgrn_sum_blocks(e):
    C = HG_CHUNK
    blocks = tuple(e[n * C:(n + 1) * C] for n in range(HG_SUM_BLOCKS))
    return blocks + (jnp.broadcast_to(e[C - 1:C], (HG_DIM, e.shape[1])),)


@jax.custom_vjp
def _hgrn_sums(tri, tri_t, g):
    return _hgrn_sum_blocks(_dot(tri, _split_rows(g)))


def _hgrn_sums_fwd(tri, tri_t, g):
    return _hgrn_sums(tri, tri_t, g), (tri, tri_t)


def _hgrn_sums_bwd(res, ds):
    tri, tri_t = res
    C = HG_CHUNK
    last = lax.broadcasted_iota(jnp.int32, (C, 1), 0) == C - 1
    prefix = ds[0] + jnp.where(last, jnp.sum(ds[-1], axis=0, keepdims=True), 0.0)
    d = jnp.concatenate((prefix,) + tuple(ds[1:-1]), axis=0)
    return jnp.zeros_like(tri), jnp.zeros_like(tri_t), _dot(tri_t[:, :d.shape[0]], d.astype(BF))


_hgrn_sums.defvjp(_hgrn_sums_fwd, _hgrn_sums_bwd)


def _bf_dot(a, b):
    return _dot(a.astype(BF), b.astype(BF))


def _bf_dot_nt(a, b):
    return _dot_nt(a.astype(BF), b.astype(BF))


def _bf_dot_tn(a, b):
    return _dot_tn(a.astype(BF), b.astype(BF))


@jax.custom_vjp
def _mm(a, b):
    return _bf_dot(a, b)


_mm.defvjp(lambda a, b: (_bf_dot(a, b), (a, b)), lambda r, ct: (_bf_dot_nt(ct, r[1]), _bf_dot_tn(r[0], ct)))


@jax.custom_vjp
def _mm_nt(a, b):
    return _bf_dot_nt(a, b)


_mm_nt.defvjp(lambda a, b: (_bf_dot_nt(a, b), (a, b)), lambda r, ct: (_bf_dot(ct, r[1]), _bf_dot_tn(ct, r[0])))


@jax.custom_vjp
def _mm_tn(a, b):
    return _bf_dot_tn(a, b)


_mm_tn.defvjp(lambda a, b: (_bf_dot_tn(a, b), (a, b)), lambda r, ct: (_bf_dot_nt(r[1], ct), _bf_dot(r[0], ct)))


def _hgrn_gates(tri, tri_t, qraw, fpre, lb):
    q = _silu(qraw)
    f = lb + (1.0 - lb) * _sigmoid(fpre)
    return q, 1.0 - f, _hgrn_sums(tri, tri_t, jnp.log(f))


def _hgrn_mix(masks, q, k, e, v, st):
    prefix, suffix, whole = e[0], e[1], e[-1]
    scores = masks[0] * _mm_nt(q, k)
    for n in range(len(HG_LEVELS)):
        decay = jnp.exp(e[2 + n])
        scores = scores + masks[n + 1] * _mm_nt(q * decay, k * decay)
    o = _mm_nt(q * jnp.exp(prefix), st) + _mm(scores, v)
    st_new = st * jnp.exp(whole) + _mm_tn(v, k * jnp.exp(suffix))
    return o, st_new


def _hgrn_chunk(tri, tri_t, masks, qraw, fpre, v, st, lb):
    q, k, e = _hgrn_gates(tri, tri_t, qraw, fpre, lb)
    return _hgrn_mix(masks, q, k, e, v, st)


HG_HEADS_PER_STEP = 4
HG_LANES = HG_HEADS_PER_STEP * HG_DIM


def _hgrn_specs(S, consts):
    col = lambda base: pl.BlockSpec((S, HG_LANES), lambda p, base=base: (0, base // HG_HEADS_PER_STEP + p))
    whole = [pl.BlockSpec(a.shape, lambda p, n=a.ndim: (0,) * n) for a in consts]
    return col, whole


def _hgrn_fwd(proj, lbs, consts, job=None):
    S = proj.shape[0]
    nc = S // HG_CHUNK
    heads = range(HG_HEADS_PER_STEP)

    def body(q_ref, f_ref, i_ref, lb_ref, tri_ref, trit_ref, mask_ref, o_ref, st_ref):
        tri, tri_t = tri_ref[...], trit_ref[...]
        masks = [mask_ref[n] for n in range(len(HG_LEVELS) + 1)]

        def chunk(ci, sts):
            r0 = pl.multiple_of(ci * HG_CHUNK, HG_CHUNK)
            rows = pl.ds(r0, HG_CHUNK)
            new = []
            lane = [pl.ds(hd * HG_DIM, HG_DIM) for hd in heads]
            gates = [_hgrn_gates(tri, tri_t, q_ref[rows, lane[hd]], f_ref[rows, lane[hd]], lb_ref[hd]) for hd in heads]
            for hd in heads:
                st_ref[hd, ci] = sts[hd]
                o, st_new = _hgrn_mix(masks, *gates[hd], i_ref[rows, lane[hd]], sts[hd])
                o_ref[rows, lane[hd]] = o
                new.append(st_new)
            return tuple(new)

        lax.fori_loop(0, nc, chunk, tuple(jnp.zeros((HG_DIM, HG_DIM), F32) for _ in heads))

    col, whole = _hgrn_specs(S, consts)
    return _hosted(
        body, job, (proj, proj, proj, lbs, *consts), name="hgrn_fwd",
        grid=(WIDTH // HG_LANES,),
        in_specs=[col(16), col(20), col(24), pl.BlockSpec((HG_HEADS_PER_STEP, 1, 128), lambda p: (p, 0, 0))] + whole,
        out_specs=[col(0), pl.BlockSpec((HG_HEADS_PER_STEP, nc, HG_DIM, HG_DIM), lambda p: (p, 0, 0, 0))],
        out_shape=[jax.ShapeDtypeStruct((S, WIDTH), F32), jax.ShapeDtypeStruct((WIDTH // 128, nc, HG_DIM, HG_DIM), F32)],
        semantics=("parallel",))


def _hgrn_bwd(proj, lbs, states, d_o, consts, dproj, job=None):
    S = proj.shape[0]
    nc = S // HG_CHUNK

    def body(q_ref, f_ref, i_ref, lb_ref, st_ref, do_ref, tri_ref, trit_ref, mask_ref, _, dlb_ref, dproj_ref, stage, sems):
        dq_ref, df_ref, di_ref = stage.at[0], stage.at[1], stage.at[2]
        masks = [mask_ref[n] for n in range(len(HG_LEVELS) + 1)]
        fn = functools.partial(_hgrn_chunk, tri_ref[...], trit_ref[...], masks)
        heads = range(HG_HEADS_PER_STEP)

        def chunk(n, carry):
            ci = nc - 1 - n
            r0 = pl.multiple_of(ci * HG_CHUNK, HG_CHUNK)
            rows = pl.ds(r0, HG_CHUNK)
            new = []
            lane = [pl.ds(hd * HG_DIM, HG_DIM) for hd in heads]
            pulls = [jax.vjp(fn, q_ref[rows, lane[hd]], f_ref[rows, lane[hd]], i_ref[rows, lane[hd]], st_ref[hd, ci], lb_ref[hd])[1]
                     for hd in heads]
            for hd in heads:
                d_st, dlb = carry[hd]
                lanes = lane[hd]
                dq, df, di, d_prev, dl = pulls[hd]((do_ref[rows, lanes], d_st))
                dq_ref[rows, lanes] = dq.astype(BF)
                df_ref[rows, lanes] = df.astype(BF)
                di_ref[rows, lanes] = di.astype(BF)
                new.append((d_prev, dlb + dl))
            return tuple(new)

        zero = (jnp.zeros((HG_DIM, HG_DIM), F32), jnp.zeros((1, HG_DIM), F32))
        done = lax.fori_loop(0, nc, chunk, tuple(zero for _ in heads))
        for hd in heads:
            dlb_ref[hd] = done[hd][1]
        lane0 = pl.program_id(0) * HG_LANES
        _put_columns([dq_ref, df_ref, di_ref], dproj_ref, [WIDTH * n + lane0 for n in (4, 5, 6)], sems)

    col, whole = _hgrn_specs(S, consts)
    head = pl.BlockSpec((HG_HEADS_PER_STEP, 1, 128), lambda p: (p, 0, 0))
    hbm = pl.BlockSpec(memory_space=pl.ANY)
    n_in = 6 + len(consts)
    return _hosted(
        body, job, (proj, proj, proj, lbs, states, d_o, *consts, dproj), name="hgrn_bwd",
        grid=(WIDTH // HG_LANES,),
        in_specs=[col(16), col(20), col(24), head, pl.BlockSpec((HG_HEADS_PER_STEP, nc, HG_DIM, HG_DIM), lambda p: (p, 0, 0, 0)), col(0)]
        + whole + [hbm],
        out_specs=[head, hbm],
        out_shape=[jax.ShapeDtypeStruct((WIDTH // 128, 1, 128), F32), jax.ShapeDtypeStruct(dproj.shape, dproj.dtype)],
        scratch_shapes=[pltpu.VMEM((3, S, HG_LANES), BF), pltpu.SemaphoreType.DMA((3,))],
        semantics=("arbitrary",), aliases={n_in: 1})


def _shift_down(x, n):
    rows = lax.broadcasted_iota(jnp.int32, x.shape, 0)
    return jnp.where(rows >= n, pltpu.roll(x, n, 0), 0.0)


def _shift_up(x, n):
    S = x.shape[0]
    rows = lax.broadcasted_iota(jnp.int32, x.shape, 0)
    return jnp.where(rows < S - n, pltpu.roll(x, S - n, 0), 0.0)


def _branch_fwd(proj, o_a, o_b, norm_w, conv_w, layer):
    S = proj.shape[0]

    def body(oa_ref, za_ref, ob_ref, zb_ref, nw_ref, pre_ref, post_ref, u_ref, zc_ref, cw_ref, ya_ref, yb_ref, yc_ref):
        ya_ref[...] = (oa_ref[...] * _silu(za_ref[...])).astype(BF)
        ob = ob_ref[...]
        rn = lax.rsqrt(jnp.mean(ob * ob, axis=-1, keepdims=True) + RMS_EPS)
        yb_ref[...] = (ob * rn * nw_ref[layer:layer + 1, :] * _silu(zb_ref[...])).astype(BF)
        pu = pre_ref[...] * u_ref[...]
        conv = cw_ref[2:3, :] * pu + cw_ref[1:2, :] * _shift_down(pu, 1) + cw_ref[0:1, :] * _shift_down(pu, 2)
        yc_ref[...] = (post_ref[...] * conv * _silu(zc_ref[...])).astype(BF)

    col = lambda base: pl.BlockSpec((S, 128), lambda p, base=base: (0, base + p))
    out = jax.ShapeDtypeStruct((S, WIDTH), BF)
    return pl.pallas_call(
        body, name="branch_fwd",
        grid=(WIDTH // 128,),
        in_specs=[col(0), col(12), col(0), col(28), pl.BlockSpec(norm_w.shape, lambda p: (0, 0)),
                  col(32), col(36), col(40), col(44), pl.BlockSpec((None, None, 3, 128), lambda p: (p, layer, 0, 0))],
        out_specs=[col(0), col(0), col(0)],
        out_shape=[out, out, out],
        compiler_params=_params(dimension_semantics=("parallel",)),
    )(o_a, proj, o_b, proj, norm_w, proj, proj, proj, proj, conv_w)


def _put_columns(tiles, dproj_ref, firsts, sems):
    copies = [pltpu.make_async_copy(t, dproj_ref.at[:, pl.ds(pl.multiple_of(c, 128), t.shape[1])], sems.at[n])
              for n, (t, c) in enumerate(zip(tiles, firsts))]
    for cp in copies:
        cp.start()
    for cp in copies:
        cp.wait()


def _branch_bwd(proj, o_a, o_b, norm_w, conv_w, dy_a, dy_b, dy_c, dproj, layer):
    S = proj.shape[0]
    firsts = [WIDTH * n for n in (3, 7, 8, 9, 10, 11)]

    def dsilu(z):
        s = _sigmoid(z)
        return s * z, s * (1.0 + z * (1.0 - s))

    def body(oa_ref, za_ref, ob_ref, zb_ref, nw_ref, pre_ref, post_ref, u_ref, zc_ref, cw_ref, dya_ref, dyb_ref, dyc_ref, _,
             doa_ref, dob_ref, dnw_ref, dcw_ref, dproj_ref, stage, sems):
        dza_ref, dzb_ref, dpre_ref, dpost_ref, du_ref, dzc_ref = [stage.at[n] for n in range(6)]
        dya = dya_ref[...]
        sa, dsa = dsilu(za_ref[...])
        doa_ref[...] = dya * sa
        dza_ref[...] = (dya * oa_ref[...] * dsa).astype(BF)

        dyb = dyb_ref[...]
        ob = ob_ref[...]
        nw = nw_ref[layer:layer + 1, :]
        sb, dsb = dsilu(zb_ref[...])
        rn = lax.rsqrt(jnp.mean(ob * ob, axis=-1, keepdims=True) + RMS_EPS)
        on = ob * rn
        dzb_ref[...] = (dyb * on * nw * dsb).astype(BF)
        don_w = dyb * sb
        dnw_ref[...] = jnp.sum(don_w * on, axis=0, keepdims=True)
        don = don_w * nw
        dob_ref[...] = rn * (don - on * jnp.mean(don * on, axis=-1, keepdims=True))

        dyc = dyc_ref[...]
        pre, post, u = pre_ref[...], post_ref[...], u_ref[...]
        sc, dsc = dsilu(zc_ref[...])
        pu = pre * u
        pu1, pu2 = _shift_down(pu, 1), _shift_down(pu, 2)
        conv = cw_ref[2:3, :] * pu + cw_ref[1:2, :] * pu1 + cw_ref[0:1, :] * pu2
        dzc_ref[...] = (dyc * post * conv * dsc).astype(BF)
        dpost_ref[...] = (dyc * conv * sc).astype(BF)
        dconv = dyc * post * sc
        dcw_ref[0:1, :] = jnp.sum(dconv * pu2, axis=0, keepdims=True)
        dcw_ref[1:2, :] = jnp.sum(dconv * pu1, axis=0, keepdims=True)
        dcw_ref[2:3, :] = jnp.sum(dconv * pu, axis=0, keepdims=True)
        dpu = cw_ref[2:3, :] * dconv + cw_ref[1:2, :] * _shift_up(dconv, 1) + cw_ref[0:1, :] * _shift_up(dconv, 2)
        dpre_ref[...] = (dpu * u).astype(BF)
        du_ref[...] = (dpu * pre).astype(BF)
        lane0 = pl.program_id(0) * 128
        _put_columns([stage.at[n] for n in range(6)], dproj_ref, [c + lane0 for c in firsts], sems)

    col = lambda base: pl.BlockSpec((S, 128), lambda p, base=base: (0, base + p))
    f32 = jax.ShapeDtypeStruct((S, WIDTH), F32)
    hbm = pl.BlockSpec(memory_space=pl.ANY)
    return pl.pallas_call(
        body, name="branch_bwd",
        grid=(WIDTH // 128,),
        in_specs=[col(0), col(12), col(0), col(28), pl.BlockSpec(norm_w.shape, lambda p: (0, 0)),
                  col(32), col(36), col(40), col(44), pl.BlockSpec((None, None, 3, 128), lambda p: (p, layer, 0, 0)),
                  col(0), col(0), col(0), hbm],
        out_specs=[col(0), col(0), pl.BlockSpec((None, 1, 128), lambda p: (p, 0, 0)), pl.BlockSpec((None, 3, 128), lambda p: (p, 0, 0)), hbm],
        out_shape=[f32, f32, jax.ShapeDtypeStruct((WIDTH // 128, 1, 128), F32), jax.ShapeDtypeStruct((WIDTH // 128, 3, 128), F32),
                   jax.ShapeDtypeStruct(dproj.shape, dproj.dtype)],
        scratch_shapes=[pltpu.VMEM((6, S, 128), BF), pltpu.SemaphoreType.DMA((6,))],
        input_output_aliases={13: 4},
        compiler_params=_params(dimension_semantics=("arbitrary",)),
    )(o_a, proj, o_b, proj, norm_w, proj, proj, proj, proj, conv_w, dy_a, dy_b, dy_c, dproj)


def _branch_proj(y_refs, wb_ref):
    return [_dot(y_refs[i][...], wb_ref[i]) for i in range(3)]


def _merge_fwd(x, mod, proj, ys, wb, wo, ln_g, ln_b, layer, job=None, tm=512):
    S, D = x.shape
    tm = min(tm, S)

    def body(x_ref, mod_ref, ga_ref, gb_ref, gc_ref, ya_ref, yb_ref, yc_ref, wb_ref, wo_ref, g_ref, b_ref, xo_ref, mg_ref, y_ref):
        ps = _branch_proj((ya_ref, yb_ref, yc_ref), wb_ref)
        merged = _sigmoid(ga_ref[...]) * ps[0] + _sigmoid(gb_ref[...]) * ps[1] + _sigmoid(gc_ref[...]) * ps[2]
        mb = merged.astype(BF)
        mg_ref[...] = mb
        y = _dot(mb, wo_ref[...].reshape(D, D))
        y_ref[...] = y
        r = ALPHA * x_ref[...] + (1.0 + mod_ref[:, 2 * D:3 * D]) * y
        xn, _ = _standardize(r)
        xo_ref[...] = xn * g_ref[layer:layer + 1, :] + b_ref[layer:layer + 1, :]

    row = lambda w, c=0: pl.BlockSpec((tm, w), lambda i, c=c: (i, c))
    whole = lambda a: pl.BlockSpec(a.shape, lambda i, n=a.ndim: (0,) * n)
    return _hosted(
        body, job, (x, mod, proj, proj, proj, *ys, wb, wo, ln_g, ln_b), name="merge_fwd",
        grid=(S // tm,),
        in_specs=[row(D), whole(mod), row(D, 6), row(D, 7), row(D, 8), row(WIDTH), row(WIDTH), row(WIDTH), whole(wb),
                  whole(wo), whole(ln_g), whole(ln_b)],
        out_specs=[row(D), row(D), row(D)],
        out_shape=[jax.ShapeDtypeStruct((S, D), F32), jax.ShapeDtypeStruct((S, D), BF), jax.ShapeDtypeStruct((S, D), F32)],
        semantics=("parallel",))


def _merge_bwd(dxo, x, y, merged, mod, proj, ys, wb, wo, ln_g, layer, tm=256):
    S, D = x.shape
    tm = min(tm, S)
    steps = S // tm
    quarter = D // N_CHIPS

    def body(dxo_ref, x_ref, y_ref, mg_ref, mod_ref, ga_ref, gb_ref, gc_ref, ya_ref, yb_ref, yc_ref, wb_ref, wo_ref, g_ref,
             dxr_ref, dg_ref, dya_ref, dyb_ref, dyc_ref, dlg_ref, dlb_ref, dgt_ref, gbr_ref, gout_ref, acc_br, acc_out):
        @pl.when(pl.program_id(0) == 0)
        def _():
            dlg_ref[...] = jnp.zeros_like(dlg_ref)
            dlb_ref[...] = jnp.zeros_like(dlb_ref)
            dgt_ref[...] = jnp.zeros_like(dgt_ref)
            acc_br[...] = jnp.zeros_like(acc_br)
            acc_out[...] = jnp.zeros_like(acc_out)

        gate1 = 1.0 + mod_ref[:, 2 * D:3 * D]
        yv = y_ref[...]
        xn, rstd = _standardize(ALPHA * x_ref[...] + gate1 * yv)
        dxo = dxo_ref[...]
        dlg_ref[...] += jnp.sum(dxo * xn, axis=0, keepdims=True)
        dlb_ref[...] += jnp.sum(dxo, axis=0, keepdims=True)
        dr = _standardize_bwd(dxo * g_ref[layer:layer + 1, :], xn, rstd)
        dxr_ref[...] = ALPHA * dr
        dgt_ref[...] += jnp.sum(dr * yv, axis=0, keepdims=True)
        dyb = (gate1 * dr).astype(BF)
        acc_out[...] += _dot_tn(mg_ref[...], dyb)
        dmerged = _dot_nt(dyb, wo_ref[...].reshape(D, D))
        y_refs = (ya_ref, yb_ref, yc_ref)
        ps = _branch_proj(y_refs, wb_ref)
        for i, (gate_ref, out_ref) in enumerate(((ga_ref, dya_ref), (gb_ref, dyb_ref), (gc_ref, dyc_ref))):
            sg = _sigmoid(gate_ref[...])
            dg_ref[:, i * D:(i + 1) * D] = (dmerged * ps[i] * sg * (1.0 - sg)).astype(BF)
            dp = (dmerged * sg).astype(BF)
            acc_br[i] += _dot_tn(y_refs[i][...], dp)
            out_ref[...] = _dot_nt(dp, wb_ref[i])

        @pl.when(pl.program_id(0) == steps - 1)
        def _():
            for j in range(N_CHIPS):
                gout_ref[j] = acc_out[j * quarter:(j + 1) * quarter, :].astype(BF)
                for i in range(3):
                    gbr_ref[j, i] = acc_br[i, :, j * quarter:(j + 1) * quarter].astype(BF)

    row = lambda w, c=0: pl.BlockSpec((tm, w), lambda i, c=c: (i, c))
    whole = lambda a: pl.BlockSpec(a.shape, lambda i, n=len(a.shape): (0,) * n)
    vec = pl.BlockSpec((1, D), lambda i: (0, 0))
    sd = jax.ShapeDtypeStruct
    g_br, g_out = sd((N_CHIPS, 3, WIDTH, quarter), BF), sd((N_CHIPS, quarter, D), BF)
    return pl.pallas_call(
        body, name="merge_bwd",
        grid=(steps,),
        in_specs=[row(D), row(D), row(D), row(D), whole(mod), row(D, 6), row(D, 7), row(D, 8), row(WIDTH), row(WIDTH), row(WIDTH),
                  whole(wb), whole(wo), whole(ln_g)],
        out_specs=[row(D), row(3 * D, IN_COLS // (3 * D) - 1), row(WIDTH), row(WIDTH), row(WIDTH), vec, vec, vec, whole(g_br), whole(g_out)],
        out_shape=[sd((S, D), F32), sd((S, IN_COLS), BF), sd((S, WIDTH), F32), sd((S, WIDTH), F32), sd((S, WIDTH), F32),
                   sd((1, D), F32), sd((1, D), F32), sd((1, D), F32), g_br, g_out],
        scratch_shapes=[pltpu.VMEM((3, WIDTH, D), F32), pltpu.VMEM((D, D), F32)],
        compiler_params=_params(dimension_semantics=("arbitrary",)),
    )(dxo, x, y, merged, mod, proj, proj, proj, *ys, wb, wo, ln_g)


def _loss_head(x, target, tm=512):
    S, D = x.shape
    tm = min(tm, S)

    def body(x_ref, t_ref, dx_ref, loss_ref):
        @pl.when(pl.program_id(0) == 0)
        def _():
            loss_ref[...] = jnp.zeros_like(loss_ref)

        err = x_ref[...] - t_ref[...]
        dx_ref[...] = err * (1.0 / D)
        loss_ref[...] += 0.5 * jnp.sum(jnp.mean(err * err, axis=-1, keepdims=True))

    row = pl.BlockSpec((tm, D), lambda i: (i, 0))
    return pl.pallas_call(
        body, name="loss_head",
        grid=(S // tm,),
        in_specs=[row, row],
        out_specs=[row, pl.BlockSpec((8, 128), lambda i: (0, 0))],
        out_shape=[jax.ShapeDtypeStruct((S, D), F32), jax.ShapeDtypeStruct((8, 128), F32)],
        compiler_params=_params(dimension_semantics=("arbitrary",)),
    )(x, target)


def _proj_bwd(dproj, wgs, x, mod, dx_res, job=None, tm=512, tk=768):
    S, D = x.shape
    tm = min(tm, S)
    n = len(wgs)
    w_args = list(wgs)
    if n == 1:
        per = wgs[0].shape[-1] // tk
        w_specs = [pl.BlockSpec((None, D, tk), lambda k, i: (k // per, 0, k % per))]
    else:
        assert wgs[0].shape[-1] == tk
        w_specs = [pl.BlockSpec((None, D, tk), lambda k, i, c=c: (jnp.minimum((k + n - 1 - c) // n, N_CHIPS - 1), 0, 0))
                   for c in range(n)]
    nk = IN_COLS // tk

    def body(dp_ref, *rest):
        w_refs, (x_ref, mod_ref, dxr_ref, dx_ref, dsh_ref, dsc_ref, acc) = rest[:n], rest[n:]
        k, i = pl.program_id(0), pl.program_id(1)
        mine = pl.ds(pl.multiple_of(i * tm, tm), tm)

        @pl.when((i == 0) & (k == 0))
        def _():
            dsh_ref[...] = jnp.zeros_like(dsh_ref)
            dsc_ref[...] = jnp.zeros_like(dsc_ref)

        @pl.when(k == 0)
        def _():
            acc[mine, :] = jnp.zeros((tm, D), F32)

        for c in range(n):
            @pl.when(k % n == c)
            def _(c=c):
                acc[mine, :] += _dot_nt(dp_ref[...], w_refs[c][...])

        @pl.when(k == nk - 1)
        def _():
            dh = acc[mine, :]
            xs, rstd = _standardize(x_ref[...])
            dsh_ref[...] += jnp.sum(dh, axis=0, keepdims=True)
            dsc_ref[...] += jnp.sum(dh * xs, axis=0, keepdims=True)
            dx_ref[...] = _standardize_bwd(dh * (1.0 + mod_ref[:, D:2 * D]), xs, rstd) + dxr_ref[...]

    row = pl.BlockSpec((tm, D), lambda k, i: (jnp.where(k == nk - 1, i, 0), 0))
    vec = pl.BlockSpec((1, D), lambda k, i: (0, 0))
    return _hosted(
        body, job, (dproj, *w_args, x, mod, dx_res), name="proj_bwd",
        grid=(nk, S // tm),
        in_specs=[pl.BlockSpec((tm, tk), lambda k, i: (i, k))] + w_specs + [row, pl.BlockSpec((1, 3 * D), lambda k, i: (0, 0)), row],
        out_specs=[row, vec, vec],
        out_shape=[jax.ShapeDtypeStruct((S, D), F32), jax.ShapeDtypeStruct((1, D), F32), jax.ShapeDtypeStruct((1, D), F32)],
        scratch_shapes=[pltpu.VMEM((S, D), F32)],
        semantics=("arbitrary", "arbitrary"))


def _grad_w_in(h, dproj):
    S, D = h.shape
    shard = IN_COLS // N_CHIPS

    def body(h_ref, d_ref, o_ref):
        o_ref[...] = _dot_tn(h_ref[...], d_ref[...]).astype(BF)

    return pl.pallas_call(
        body, name="grad_w_in",
        grid=(N_CHIPS,),
        in_specs=[pl.BlockSpec((S, D), lambda n: (0, 0)), pl.BlockSpec((S, shard), lambda n: (0, n))],
        out_specs=pl.BlockSpec((None, D, shard), lambda n: (n, 0, 0)),
        out_shape=jax.ShapeDtypeStruct((N_CHIPS, D, shard), BF),
        compiler_params=_params(dimension_semantics=("parallel",)),
    )(h, dproj)


def _all_gather8(x, name):
    R, N = x.shape

    def body(x_ref, out_ref, send_sems, recv_sems):
        mx, my, mc = lax.axis_index("x"), lax.axis_index("y"), lax.axis_index("c")
        me = 4 * mx + 2 * my + mc
        out_ref[me] = x_ref[...]
        copies = []
        for k in range(1, N_DEV):
            peer = (_flip(mx, k & 4), _flip(my, k & 2), _flip(mc, k & 1))
            cp = pltpu.make_async_remote_copy(src_ref=x_ref, dst_ref=out_ref.at[me], send_sem=send_sems.at[k - 1],
                                              recv_sem=recv_sems.at[k - 1], device_id=peer, device_id_type=MESH)
            cp.start()
            copies.append(cp)
        for cp in copies:
            cp.wait()

    return pl.pallas_call(
        body, name=name,
        in_specs=[pl.BlockSpec(memory_space=pltpu.VMEM)],
        out_specs=pl.BlockSpec(memory_space=pltpu.VMEM),
        out_shape=jax.ShapeDtypeStruct((N_DEV, R, N), F32),
        scratch_shapes=[pltpu.SemaphoreType.DMA((N_DEV - 1,)), pltpu.SemaphoreType.DMA((N_DEV - 1,))],
        compiler_params=_params(),
    )(x)


def _rows2d(a):
    return a.reshape(-1, a.shape[-1])


def _tile_rows(rows, cols, n_arrays):
    budget = (24 << 20) // (n_arrays * 2 * 4 * cols)
    if rows <= budget:
        return rows
    tm = 8
    for cand in range(8, budget + 1, 8):
        if rows % cand == 0:
            tm = cand
    return tm


SUM_ROWS = 256


def _sum_cores(g, sent, where):
    chips, lead, _, r, cols = g.shape
    tr = min(r, 2 * SUM_ROWS)

    def body(where_ref, g_ref, s_ref, out_ref):
        out_ref[...] = (g_ref[...].astype(F32) + s_ref[...].astype(F32)).astype(BF)

    spec = pl.BlockSpec((None, tr, cols), lambda i, j, where_ref: (i, j, 0))
    out = pl.pallas_call(
        body, name="sum_cores",
        grid_spec=pltpu.PrefetchScalarGridSpec(
            num_scalar_prefetch=1, grid=(chips * lead, r // tr),
            in_specs=[pl.BlockSpec((None, None, tr, cols), lambda i, j, where_ref: (i, where_ref[1], j, 0)), spec],
            out_specs=spec),
        out_shape=jax.ShapeDtypeStruct((chips * lead, r, cols), BF),
        compiler_params=_params(dimension_semantics=("parallel", "parallel")),
    )(where, g.reshape(chips * lead, 2, r, cols), sent.reshape(chips * lead, r, cols))
    return out.reshape(chips, lead, r, cols)


def _sum_chips(sums, got, where):
    _, lead, r, cols = sums.shape
    tr = min(r, SUM_ROWS)

    def body(where_ref, s_ref, g_ref, out_ref):
        out_ref[...] = ((s_ref[...].astype(F32) + g_ref[0].astype(F32)) + g_ref[1].astype(F32)) + g_ref[2].astype(F32)

    return pl.pallas_call(
        body, name="sum_chips",
        grid_spec=pltpu.PrefetchScalarGridSpec(
            num_scalar_prefetch=1, grid=(lead, r // tr),
            in_specs=[pl.BlockSpec((None, None, tr, cols), lambda i, j, where_ref: (where_ref[0], i, j, 0)),
                      pl.BlockSpec((N_CHIPS - 1, None, tr, cols), lambda i, j, where_ref: (0, i, j, 0))],
            out_specs=pl.BlockSpec((None, None, tr, cols), lambda i, j, where_ref: (i, where_ref[1], j, 0))),
        out_shape=jax.ShapeDtypeStruct((lead, 2, r, cols), F32),
        compiler_params=_params(dimension_semantics=("parallel", "parallel")),
    )(where, sums, got)


def _adamw(w, m, v, groups, name):
    shape = w.shape
    w2, m2, v2 = _rows2d(w), _rows2d(m), _rows2d(v)
    rows, cols = w2.shape
    ng = len(groups)
    n = len(groups[0])
    slab = rows // ng
    gs = [_rows2d(g) for grp in groups for g in grp]
    tm = _tile_rows(slab, cols, 7 + n)
    tiles = slab // tm
    c1 = 1.0 / (1.0 - ADAM_B1 ** ADAM_STEP)
    c2 = 1.0 / (1.0 - ADAM_B2 ** ADAM_STEP)

    def body(*refs):
        w_ref, m_ref, v_ref = refs[:3]
        g_refs = refs[3:3 + ng * n]
        go_ref, d_ref, mo_ref, vo_ref = refs[3 + ng * n:]
        which = pl.program_id(0)
        for s in range(ng):
            @pl.when(which == s)
            def _(s=s):
                g = g_refs[s * n][...]
                for r in g_refs[s * n + 1:(s + 1) * n]:
                    g = g + r[...]
                mn = ADAM_B1 * m_ref[...] + (1.0 - ADAM_B1) * g
                vn = ADAM_B2 * v_ref[...] + (1.0 - ADAM_B2) * (g * g)
                go_ref[...] = g
                mo_ref[...] = mn
                vo_ref[...] = vn
                d_ref[...] = -ADAM_LR * ((mn * c1) / (jnp.sqrt(vn * c2) + ADAM_EPS) + ADAM_WD * w_ref[...])

    spec = pl.BlockSpec((tm, cols), lambda s, i: (s * tiles + i, 0))
    g_specs = [pl.BlockSpec((tm, cols), lambda s, i, k=k: (jnp.where(s == k, i, jnp.where(s < k, 0, tiles - 1)), 0))
               for k in range(ng) for _ in range(n)]
    outs = pl.pallas_call(
        body, name=name,
        grid=(ng, tiles),
        in_specs=[spec] * 3 + g_specs,
        out_specs=[spec] * 4,
        out_shape=[jax.ShapeDtypeStruct((rows, cols), F32)] * 4,
        compiler_params=_params(dimension_semantics=("arbitrary", "arbitrary")),
    )(w2, m2, v2, *gs)
    return [o.reshape(shape) for o in outs]


def _lower_bounds(r0, r1):
    top = jnp.maximum(r0, r1)
    e0, e1 = jnp.exp(r0 - top), jnp.exp(r1 - top)
    p0, p1 = e0 / (e0 + e1), e1 / (e0 + e1)
    return p0 - p0, (p0 + p1) - p0


def _lbs_fwd(lb_raw):
    def body(lb_ref, out_ref):
        l0, l1 = _lower_bounds(lb_ref[0:1, :], lb_ref[1:2, :])
        out_ref[0:1, :] = l0
        out_ref[1:2, :] = l1

    return pl.pallas_call(body, name="lower_bounds", out_shape=jax.ShapeDtypeStruct(lb_raw.shape, F32), compiler_params=_params())(lb_raw)


def _mod_rows(c_all, w_mod, tn=768):
    _, D, cols = w_mod.shape

    def body(c_ref, w_ref, out_ref):
        out_ref[...] = _dot(c_ref[...].astype(BF), w_ref[...].astype(BF))

    return pl.pallas_call(
        body, name="mod_rows",
        grid=(DEPTH,),
        in_specs=[pl.BlockSpec((N_DEV, D), lambda l: (0, 0)), pl.BlockSpec((None, D, cols), lambda l: (l, 0, 0))],
        out_specs=pl.BlockSpec((N_DEV, cols), lambda l: (0, l)),
        out_shape=jax.ShapeDtypeStruct((N_DEV, DEPTH * cols), F32),
        compiler_params=_params(dimension_semantics=("parallel",)),
    )(c_all, w_mod)


def _grad_w_mod(c_all, dmod_cols):
    D = c_all.shape[1]
    cols = dmod_cols.shape[-1]

    def body(c_ref, d_ref, out_ref):
        out_ref[...] = _dot_tn(c_ref[...].astype(BF), d_ref[...].astype(BF))

    return pl.pallas_call(
        body, name="grad_w_mod",
        grid=(DEPTH,),
        in_specs=[pl.BlockSpec((N_DEV, D), lambda l: (0, 0)), pl.BlockSpec((None, N_DEV, cols), lambda l: (l, 0, 0))],
        out_specs=pl.BlockSpec((None, D, cols), lambda l: (l, 0, 0)),
        out_shape=jax.ShapeDtypeStruct((DEPTH, D, cols), F32),
        compiler_params=_params(dimension_semantics=("parallel",)),
    )(c_all, dmod_cols)


def _sum_devices(parts):
    _, R, N = parts.shape

    def body(p_ref, out_ref):
        acc = p_ref[0]
        for d in range(1, N_DEV):
            acc = acc + p_ref[d]
        out_ref[...] = acc

    return pl.pallas_call(body, name="sum_devices", out_shape=jax.ShapeDtypeStruct((R, N), F32), compiler_params=_params())(parts)


def _lbs_bwd(lb_raw, dl):
    def body(lb_ref, dl_ref, out_ref):
        _, pull = jax.vjp(_lower_bounds, lb_ref[0:1, :], lb_ref[1:2, :])
        d0, d1 = pull((dl_ref[0:1, :], dl_ref[1:2, :]))
        out_ref[0:1, :] = d0
        out_ref[1:2, :] = d1

    return pl.pallas_call(body, name="lower_bounds_bwd", out_shape=jax.ShapeDtypeStruct(lb_raw.shape, F32), compiler_params=_params())(lb_raw, dl)


def kernel(x, c, w_mod, b_mod, w_in, conv_w, hgrn_norm_w, lower_bounds, w_branch, w_out, ln_g, ln_b, loss_target, m_w_mod, m_b_mod, m_w_in, m_conv_w, m_hgrn_norm_w, m_lower_bounds, m_w_branch, m_w_out, m_ln_g, m_ln_b, v_w_mod, v_b_mod, v_w_in, v_conv_w, v_hgrn_norm_w, v_lower_bounds, v_w_branch, v_w_out, v_ln_g, v_ln_b):
    D = D_MODEL
    x0 = x[0]
    target = loss_target[0]
    S = x0.shape[0]
    mx, my, mc = lax.axis_index("x"), lax.axis_index("y"), lax.axis_index("c")
    chip = 2 * mx + my
    me = 2 * chip + mc
    mod_cols = 3 * D // N_CHIPS

    plan = _Plan(w_in, w_branch, w_out, chip, mc)
    n_conv = DEPTH * 3 * (WIDTH // N_CHIPS)
    first = jnp.concatenate([c, conv_w.reshape(1, n_conv), jnp.zeros((1, 2 * D - D - n_conv), F32)], axis=1)
    first = plan.first(first.reshape(8, 2 * D // 8)).reshape(N_DEV, 2 * D)

    c_all = first[:, :D]
    conv_all = first[:, D:D + n_conv].reshape(N_DEV, DEPTH, 3, WIDTH // N_CHIPS)[0::2]
    mod_part = _all_gather8(_mod_rows(c_all, w_mod), "gather_mod")[0::2]
    mod_part = lax.dynamic_index_in_dim(mod_part, me, axis=1, keepdims=False).reshape(N_CHIPS, DEPTH, mod_cols)
    mods = [(mod_part[:, l].reshape(1, 3 * D) + b_mod[l][None, :]) for l in range(DEPTH)]
    lbs = _lbs_fwd(lower_bounds).reshape(DEPTH, WIDTH // 128, 1, 128)
    loss_blk, dx, small = _local_step(x0, target, mods, lbs, conv_all, hgrn_norm_w, ln_g, ln_b, plan)

    n_mod, n_nw, n_lb, n_ln, n_cw = DEPTH * 3 * D, DEPTH * 128, DEPTH * WIDTH, DEPTH * D, DEPTH * 3 * WIDTH
    row = jnp.concatenate(
        [jnp.concatenate([small[l][0], small[l][1], small[l][2]], axis=1) for l in range(DEPTH)]
        + [jnp.sum(small[l][3], axis=0) for l in range(DEPTH)]
        + [small[l][4].reshape(1, WIDTH) for l in range(DEPTH)]
        + [small[l][5] for l in range(DEPTH)] + [small[l][6] for l in range(DEPTH)]
        + [jnp.transpose(small[l][7], (1, 0, 2)).reshape(1, 3 * WIDTH) for l in range(DEPTH)]
        + [loss_blk[0:1, :]], axis=1)
    n_row = row.shape[1]
    fold = -(-n_row // (8 * 128)) * 128
    rows = jnp.concatenate([row, jnp.zeros((1, 8 * fold - n_row), F32)], axis=1).reshape(8, fold)

    whole, gathered = plan.finish(rows)
    grads = {kind: [[whole[(kind, l)]] for l in range(DEPTH)] for kind in ("in", "br", "out")}

    off_nw = n_mod
    off_lb = off_nw + n_nw
    off_lng = off_lb + n_lb
    off_lnb = off_lng + n_ln
    off_cw = off_lnb + n_ln
    off_loss = off_cw + n_cw
    total = _sum_devices(gathered).reshape(1, 8 * fold)
    gathered = gathered.reshape(N_DEV, 1, 8 * fold)
    d_lower = _lbs_bwd(lower_bounds, total[0, off_lb:off_lng].reshape(DEPTH, WIDTH))
    loss = total[0, off_loss]
    d_b_mod = total[0, :n_mod].reshape(DEPTH, 3 * D)
    d_norm_w = total[0, off_nw:off_lb].reshape(DEPTH, 128)
    d_ln_g = total[0, off_lng:off_lnb].reshape(DEPTH, D)
    d_ln_b = total[0, off_lnb:off_cw].reshape(DEPTH, D)
    d_conv = total[0, off_cw:off_loss].reshape(DEPTH, 3, N_CHIPS, WIDTH // N_CHIPS)
    d_conv = lax.dynamic_index_in_dim(d_conv, chip, axis=2, keepdims=False)
    dmod_all = gathered[:, 0, :n_mod].reshape(N_DEV, DEPTH, N_CHIPS, mod_cols)
    dmod_cols = jnp.transpose(lax.dynamic_index_in_dim(dmod_all, chip, axis=2, keepdims=False), (1, 0, 2))
    d_w_mod = _grad_w_mod(c_all, dmod_cols)

    res = {}
    res["w_mod"] = _adamw(w_mod, m_w_mod, v_w_mod, [[d_w_mod]], "adamw_w_mod")
    res["b_mod"] = _adamw(b_mod, m_b_mod, v_b_mod, [[d_b_mod]], "adamw_b_mod")
    res["w_in"] = _adamw(w_in, m_w_in, v_w_in, grads["in"], "adamw_w_in")
    res["conv_w"] = _adamw(conv_w, m_conv_w, v_conv_w, [[d_conv]], "adamw_conv_w")
    res["hgrn_norm_w"] = _adamw(hgrn_norm_w, m_hgrn_norm_w, v_hgrn_norm_w, [[d_norm_w]], "adamw_norm_w")
    res["lower_bounds"] = _adamw(lower_bounds, m_lower_bounds, v_lower_bounds, [[d_lower]], "adamw_lower_bounds")
    res["w_branch"] = _adamw(w_branch, m_w_branch, v_w_branch, grads["br"], "adamw_w_branch")
    res["w_out"] = _adamw(w_out, m_w_out, v_w_out, grads["out"], "adamw_w_out")
    res["ln_g"] = _adamw(ln_g, m_ln_g, v_ln_g, [[d_ln_g]], "adamw_ln_g")
    res["ln_b"] = _adamw(ln_b, m_ln_b, v_ln_b, [[d_ln_b]], "adamw_ln_b")
    names = ["w_mod", "b_mod", "w_in", "conv_w", "hgrn_norm_w", "lower_bounds", "w_branch", "w_out", "ln_g", "ln_b"]
    return (loss, dx[None], *[res[n][0] for n in names], *[res[n][1] for n in names],
            *[res[n][2] for n in names], *[res[n][3] for n in names])


class _Plan:
    FIRST_CHUNKS = 2
    WINDOWS = ((0, 1280), (1280, 640), (1920, 384))

    def __init__(self, w_in, w_br, w_out, chip, core):
        self.local = {"in": w_in, "br": w_br, "out": w_out}
        self.chip, self.where = chip, jnp.stack([chip, core]).astype(jnp.int32)
        self.gathered, self.partial, self.grads, self.chip_sums, self.scattered, self.pending = {}, {}, {}, {}, {}, {}

    def chunks(self, l):
        return self.FIRST_CHUNKS if l == 0 else 1

    def w_in(self, l):
        return [self.gathered[("in", l, c)] for c in range(self.chunks(l))]

    def _shard(self, key):
        mine = self.local[key[0]][key[1]]
        if key[0] == "in":
            cols = mine.shape[-1] // self.chunks(key[1])
            mine = mine[:, key[2] * cols:(key[2] + 1) * cols]
        return mine.astype(BF)

    def _slab(self, key):
        mine = _halves(self._shard(key))
        return lax.dynamic_update_slice(lax.empty((N_CHIPS,) + mine.shape, mine.dtype), mine[None], (self.chip, 0, 0, 0, 0))

    def _gather(self, keys):
        return ("gather", keys), _gather_job([self._slab(key) for key in keys])

    def _fetch(self, keys):
        return ("gather_part", keys), _gather_job([self._slab(key) for key in keys], None, ())

    def _hand_over(self, keys):
        return ("gather", keys), _gather_job([self.partial[key] for key in keys], (), (None,))

    def _gather_window(self, key, n):
        slab = self._slab(key) if n == 0 else self.partial[key]
        last = n == len(self.WINDOWS) - 1
        handed = self.WINDOWS[max(n - 1, 0):n] + (self.WINDOWS[n:] if last else ())
        return ("gather" if last else "gather_part", [key]), _gather_job([slab], self.WINDOWS[n], handed)

    def _to_sibling(self, keys):
        return ("to_sibling", keys), _to_sibling_job([_halves(self.grads[key], 1) for key in keys])

    def _scatter(self, keys):
        return ("scatter", keys), _scatter_job([self.chip_sums[key] for key in keys])

    def job(self, stage, l, c=0):
        parts = []
        if stage == "proj_fwd":
            parts = [self._gather([("in", l, c + 1)]) if c + 1 < self.chunks(l) else self._fetch([("br", l), ("out", l)])]
        elif stage == "attn_fwd":
            parts = [self._hand_over([("br", l), ("out", l)])] + ([self._gather_window(("in", l + 1, 0), 0)] if l + 1 < DEPTH else [])
        elif stage == "hgrn_fwd" and l + 1 < DEPTH:
            parts = [self._gather_window(("in", l + 1, 0), 1)]
        elif stage == "merge_fwd" and l + 1 < DEPTH:
            parts = [self._gather_window(("in", l + 1, 0), 2)]
        elif stage == "attn_bwd":
            parts = [self._to_sibling([("out", l), ("br", l)])] + ([self._scatter([("in", l + 1)])] if l + 1 < DEPTH else [])
        elif stage == "hgrn_bwd":
            parts = [self._scatter([("out", l), ("br", l)])]
        elif stage == "proj_bwd":
            parts = [self._to_sibling([("in", l)])] if l else [self._scatter([("in", 0)])]
        self.pending[(stage, l, c)] = [(tag, len(job.outs)) for tag, job in parts]
        return _join_jobs([job for _, job in parts])

    def done(self, stage, l, outs, c=0):
        if outs is None:
            return
        at = 0
        for (what, keys), n_outs in self.pending[(stage, l, c)]:
            mine, at = outs[at:at + n_outs], at + n_outs
            for n, key in enumerate(keys):
                if what == "gather":
                    self.gathered[key] = mine[n].reshape((N_CHIPS,) + self._shard(key).shape)
                elif what == "gather_part":
                    self.partial[key] = mine[n]
                elif what == "to_sibling":
                    self.chip_sums[key] = _sum_cores(_halves(self.grads[key], 1), mine[n], self.where)
                else:
                    self.scattered[key] = mine[n]

    def first(self, rows):
        tag, job = self._gather([("in", 0, 0)])
        self.pending[("first", 0, 0)] = [(tag, len(job.outs))]
        outs = _run_job(_join_jobs([job, _gather8_job(rows)]), "gather_first")
        self.done("first", 0, outs[:-1])
        return outs[-1]

    def took(self, key, grad):
        self.grads[key] = grad
        if key == ("in", 0):
            tag, job = self._to_sibling([key])
            self.pending[("took", 0, 0)] = [(tag, len(job.outs))]
            self.done("took", 0, _run_job(job, "to_sibling_last"))

    def finish(self, rows):
        keys = [(kind, l) for kind in ("in", "br", "out") for l in range(DEPTH)]
        halves = [_sum_chips(self.chip_sums[key], self.scattered[key], self.where) for key in keys]
        outs = _run_job(_join_jobs([_place_job(halves), _gather8_job(rows)]), "place_halves")
        return {key: w.reshape(self.grads[key].shape[1:]) for key, w in zip(keys, outs[:-1])}, outs[-1]


def _local_step(x0, target, mods, lbs, conv_all, hgrn_norm_w, ln_g, ln_b, plan):
    D = D_MODEL
    after, before = _attn_consts()
    hg_consts = _hgrn_consts()

    saved = []
    xl = x0
    for l in range(DEPTH):
        n = plan.chunks(l)
        (proj, h), got = _proj_fwd(xl, mods[l], plan.gathered[("in", l, 0)], plan.job("proj_fwd", l, 0), (0, n))
        plan.done("proj_fwd", l, got, 0)
        for c in range(1, n):
            (proj,), got = _proj_cols(h, plan.gathered[("in", l, c)], (c, n), proj, plan.job("proj_fwd", l, c))
            plan.done("proj_fwd", l, got, c)
        (o_a, tot), got = _attn_fwd(proj, after, plan.job("attn_fwd", l))
        plan.done("attn_fwd", l, got)
        (o_b, states), got = _hgrn_fwd(proj, lbs[l], hg_consts, plan.job("hgrn_fwd", l))
        plan.done("hgrn_fwd", l, got)
        ys = _branch_fwd(proj, o_a, o_b, hgrn_norm_w, conv_all, l)
        wb = jnp.concatenate(list(plan.gathered[("br", l)]), axis=-1)
        (x_next, merged, y), got = _merge_fwd(xl, mods[l], proj, ys, wb, plan.gathered[("out", l)], ln_g, ln_b, l, plan.job("merge_fwd", l))
        plan.done("merge_fwd", l, got)
        saved.append((xl, proj, h, o_a, tot, o_b, states, ys, merged, y, wb))
        xl = x_next
    dx, loss_blk = _loss_head(xl, target)

    small = [None] * DEPTH
    for l in reversed(range(DEPTH)):
        xin, proj, h, o_a, tot, o_b, states, ys, merged, y, wb = saved[l]
        dx_res, dproj, dy_a, dy_b, dy_c, dln_g, dln_b, dgate, g_br, g_out = _merge_bwd(
            dx, xin, y, merged, mods[l], proj, ys, wb, plan.gathered[("out", l)], ln_g, l)
        plan.took(("out", l), g_out)
        plan.took(("br", l), g_br)
        d_oa, d_ob, dnorm_w, dconv_w, dproj = _branch_bwd(proj, o_a, o_b, hgrn_norm_w, conv_all, dy_a, dy_b, dy_c, dproj, l)
        (dproj,), got = _attn_bwd(proj, d_oa, tot, after, before, dproj, plan.job("attn_bwd", l))
        plan.done("attn_bwd", l, got)
        (dlb, dproj), got = _hgrn_bwd(proj, lbs[l], states, d_ob, hg_consts, dproj, plan.job("hgrn_bwd", l))
        plan.done("hgrn_bwd", l, got)
        plan.took(("in", l), _grad_w_in(h, dproj))
        wgs = plan.w_in(l)
        tile = {"tk": wgs[0].shape[-1]} if len(wgs) > 1 else {}
        (dx, dshift, dscale), got = _proj_bwd(dproj, wgs, xin, mods[l], dx_res, plan.job("proj_bwd", l), **tile)
        plan.done("proj_bwd", l, got)
        small[l] = (dshift, dscale, dgate, dnorm_w, dlb, dln_g, dln_b, dconv_w)
    return loss_blk, dx, small
```
